```python
import jax, jax.numpy as jnp
from jax import lax
import numpy as np

D_MODEL = 1024
BATCH = 8
SEQ = 4096
DEPTH = 1

N_META = 16
D_RNN = 1024
N_RNN_HEADS = 4
RNN_HEAD_DIM = D_RNN // N_RNN_HEADS
RNN_CONV_WIDTH = 4
RG_LRU_C = 8.0
D_CONV = 1024
CONV_WIDTH = 31
D_FF = 2816
FFN_RESIDUAL_WEIGHT = 0.5
EPS = 1e-6
IN_SIZES = (D_RNN, D_RNN, D_CONV, D_CONV, D_MODEL, D_MODEL)
IN_TOTAL = sum(IN_SIZES)
IN_SPLITS = tuple(int(v) for v in np.cumsum(IN_SIZES)[:-1])

kernel_name = "hybrid_rglru_conformer_macaron"


def rmsnorm(x, g):
    xf = x.astype(jnp.float32)
    y = xf * lax.rsqrt(jnp.mean(xf * xf, axis=-1, keepdims=True) + EPS)
    return (y * g.astype(jnp.float32)).astype(x.dtype)


def layernorm(x, g, b):
    xf = x.astype(jnp.float32)
    mu = jnp.mean(xf, axis=-1, keepdims=True)
    var = jnp.mean(jnp.square(xf - mu), axis=-1, keepdims=True)
    y = (xf - mu) * lax.rsqrt(var + EPS)
    return (y * g.astype(jnp.float32) + b.astype(jnp.float32)).astype(x.dtype)


def swiglu_ffn(h, w_gu, w_down):
    gate, up = jnp.split(h @ w_gu, 2, axis=-1)
    return (jax.nn.silu(gate) * up) @ w_down


def causal_depthwise_conv(x, w, b):
    k_width, channels = w.shape
    out = lax.conv_general_dilated(
        x, w[:, None, :].astype(x.dtype), window_strides=(1,), padding=((k_width - 1, 0),),
        dimension_numbers=("NWC", "WIO", "NWC"), feature_group_count=channels)
    return out + b


def rg_lru(x, w_a, b_a, w_x, b_x, lam):
    bsz, t_len, _ = x.shape
    xb = x.reshape(bsz, t_len, N_RNN_HEADS, RNN_HEAD_DIM)
    r = jax.nn.sigmoid((jnp.einsum("bthi,hij->bthj", xb, w_a).reshape(bsz, t_len, D_RNN) + b_a).astype(jnp.float32))
    i = jax.nn.sigmoid((jnp.einsum("bthi,hij->bthj", xb, w_x).reshape(bsz, t_len, D_RNN) + b_x).astype(jnp.float32))
    log_a = -RG_LRU_C * r * jax.nn.softplus(-lam.astype(jnp.float32))
    a = jnp.exp(log_a)
    u = jnp.sqrt(-jnp.expm1(2.0 * log_a)) * (i * x.astype(jnp.float32))

    def combine(left, right):
        a_l, h_l = left
        a_r, h_r = right
        return a_l * a_r, a_r * h_l + h_r

    _, h = lax.associative_scan(combine, (a, u), axis=1)
    return h.astype(x.dtype)


def token_mixer(h, w_in, b_in, rnn_conv_w, rnn_conv_b, rg_w_a, rg_b_a, rg_w_x, rg_b_x, rg_lambda,
                rnn_w_proj, conv_dw_w, conv_dw_b, conv_ln_g, conv_ln_b, conv_w_proj, conv_b_proj, w_out):
    proj = h @ w_in + b_in
    x_rnn, y_rnn, glu_v, glu_g, gate_a, gate_b = jnp.split(proj, IN_SPLITS, axis=-1)
    xr = causal_depthwise_conv(x_rnn, rnn_conv_w, rnn_conv_b)
    xr = rg_lru(xr, rg_w_a, rg_b_a, rg_w_x, rg_b_x, rg_lambda)
    y_a = (xr * jax.nn.gelu(y_rnn)) @ rnn_w_proj
    v = glu_v * jax.nn.sigmoid(glu_g)
    v = causal_depthwise_conv(v, conv_dw_w, conv_dw_b)
    v = jax.nn.silu(layernorm(v, conv_ln_g, conv_ln_b))
    y_b = v @ conv_w_proj + conv_b_proj
    merged = jax.nn.sigmoid(gate_a) * y_a + jax.nn.sigmoid(gate_b) * y_b
    return merged @ w_out


def _fwd_setup_inputs(seed: int = 0) -> dict:
    key = jax.random.key(seed)
    ks = iter(jax.random.split(key, 40))
    L = DEPTH
    f32 = jnp.float32

    def nrm(shape, fan_in):
        return jax.random.normal(next(ks), shape, f32) * (fan_in ** -0.5)

    def gain(shape):
        return 1.0 + 0.02 * jax.random.normal(next(ks), shape, f32)

    def bias(shape):
        return 0.01 * jax.random.normal(next(ks), shape, f32)

    x = jax.random.normal(next(ks), (BATCH, SEQ, D_MODEL), f32)
    meta_tokens = jax.random.normal(next(ks), (N_META, D_MODEL), f32)
    u = jax.random.uniform(next(ks), (L, D_RNN), f32, minval=0.9, maxval=0.999)
    s = u ** (1.0 / RG_LRU_C)
    rg_lambda = jnp.log(s) - jnp.log1p(-s)
    return {
        "x": x,
        "meta_tokens": meta_tokens,
        "ffn1_norm": gain((L, D_MODEL)),
        "ffn1_w_gu": nrm((L, D_MODEL, 2 * D_FF), D_MODEL),
        "ffn1_w_down": nrm((L, D_FF, D_MODEL), D_FF),
        "mix_norm": gain((L, D_MODEL)),
        "w_in": nrm((L, D_MODEL, IN_TOTAL), D_MODEL),
        "b_in": bias((L, IN_TOTAL)),
        "rnn_conv_w": nrm((L, RNN_CONV_WIDTH, D_RNN), RNN_CONV_WIDTH),
        "rnn_conv_b": bias((L, D_RNN)),
        "rg_w_a": nrm((L, N_RNN_HEADS, RNN_HEAD_DIM, RNN_HEAD_DIM), RNN_HEAD_DIM),
        "rg_b_a": bias((L, D_RNN)),
        "rg_w_x": nrm((L, N_RNN_HEADS, RNN_HEAD_DIM, RNN_HEAD_DIM), RNN_HEAD_DIM),
        "rg_b_x": bias((L, D_RNN)),
        "rg_lambda": rg_lambda,
        "rnn_w_proj": nrm((L, D_RNN, D_MODEL), D_RNN),
        "conv_dw_w": nrm((L, CONV_WIDTH, D_CONV), CONV_WIDTH),
        "conv_dw_b": bias((L, D_CONV)),
        "conv_ln_g": gain((L, D_CONV)),
        "conv_ln_b": bias((L, D_CONV)),
        "conv_w_proj": nrm((L, D_CONV, D_MODEL), D_CONV),
        "conv_b_proj": bias((L, D_MODEL)),
        "w_out": nrm((L, D_MODEL, D_MODEL), D_MODEL),
        "ffn2_norm": gain((L, D_MODEL)),
        "ffn2_w_gu": nrm((L, D_MODEL, 2 * D_FF), D_MODEL),
        "ffn2_w_down": nrm((L, D_FF, D_MODEL), D_FF),
        "final_norm": gain((D_MODEL,)),
    }


def _fwd_reference(x, meta_tokens, ffn1_norm, ffn1_w_gu, ffn1_w_down, mix_norm, w_in, b_in,
              rnn_conv_w, rnn_conv_b, rg_w_a, rg_b_a, rg_w_x, rg_b_x, rg_lambda, rnn_w_proj,
              conv_dw_w, conv_dw_b, conv_ln_g, conv_ln_b, conv_w_proj, conv_b_proj, w_out,
              ffn2_norm, ffn2_w_gu, ffn2_w_down, final_norm):
    bsz = x.shape[0]
    meta = jnp.broadcast_to(meta_tokens.astype(x.dtype)[None], (bsz, N_META, x.shape[-1]))
    h = jnp.concatenate([meta, x], axis=1)
    for l in range(DEPTH):
        h = h + FFN_RESIDUAL_WEIGHT * swiglu_ffn(rmsnorm(h, ffn1_norm[l]), ffn1_w_gu[l], ffn1_w_down[l])
        h = h + token_mixer(rmsnorm(h, mix_norm[l]), w_in[l], b_in[l], rnn_conv_w[l], rnn_conv_b[l],
                            rg_w_a[l], rg_b_a[l], rg_w_x[l], rg_b_x[l], rg_lambda[l], rnn_w_proj[l],
                            conv_dw_w[l], conv_dw_b[l], conv_ln_g[l], conv_ln_b[l], conv_w_proj[l],
                            conv_b_proj[l], w_out[l])
        h = h + FFN_RESIDUAL_WEIGHT * swiglu_ffn(rmsnorm(h, ffn2_norm[l]), ffn2_w_gu[l], ffn2_w_down[l])
    return rmsnorm(h, final_norm)[:, N_META:, :]


import jax as _jax
import jax.numpy as _jnp

TWIN_FORMAT = 'train_step'
FWD_PARAMS = ['x', 'meta_tokens', 'ffn1_norm', 'ffn1_w_gu', 'ffn1_w_down', 'mix_norm', 'w_in', 'b_in', 'rnn_conv_w', 'rnn_conv_b', 'rg_w_a', 'rg_b_a', 'rg_w_x', 'rg_b_x', 'rg_lambda', 'rnn_w_proj', 'conv_dw_w', 'conv_dw_b', 'conv_ln_g', 'conv_ln_b', 'conv_w_proj', 'conv_b_proj', 'w_out', 'ffn2_norm', 'ffn2_w_gu', 'ffn2_w_down', 'final_norm']
TWIN_WEIGHTS = ['meta_tokens', 'ffn1_norm', 'ffn1_w_gu', 'ffn1_w_down', 'mix_norm', 'w_in', 'b_in', 'rnn_conv_w', 'rnn_conv_b', 'rg_w_a', 'rg_b_a', 'rg_w_x', 'rg_b_x', 'rg_lambda', 'rnn_w_proj', 'conv_dw_w', 'conv_dw_b', 'conv_ln_g', 'conv_ln_b', 'conv_w_proj', 'conv_b_proj', 'w_out', 'ffn2_norm', 'ffn2_w_gu', 'ffn2_w_down', 'final_norm']
TWIN_DIFF_INPUT = 'x'
TWIN_INPUTS = ['x', 'meta_tokens', 'ffn1_norm', 'ffn1_w_gu', 'ffn1_w_down', 'mix_norm', 'w_in', 'b_in', 'rnn_conv_w', 'rnn_conv_b', 'rg_w_a', 'rg_b_a', 'rg_w_x', 'rg_b_x', 'rg_lambda', 'rnn_w_proj', 'conv_dw_w', 'conv_dw_b', 'conv_ln_g', 'conv_ln_b', 'conv_w_proj', 'conv_b_proj', 'w_out', 'ffn2_norm', 'ffn2_w_gu', 'ffn2_w_down', 'final_norm', 'loss_target', 'm_meta_tokens', 'm_ffn1_norm', 'm_ffn1_w_gu', 'm_ffn1_w_down', 'm_mix_norm', 'm_w_in', 'm_b_in', 'm_rnn_conv_w', 'm_rnn_conv_b', 'm_rg_w_a', 'm_rg_b_a', 'm_rg_w_x', 'm_rg_b_x', 'm_rg_lambda', 'm_rnn_w_proj', 'm_conv_dw_w', 'm_conv_dw_b', 'm_conv_ln_g', 'm_conv_ln_b', 'm_conv_w_proj', 'm_conv_b_proj', 'm_w_out', 'm_ffn2_norm', 'm_ffn2_w_gu', 'm_ffn2_w_down', 'm_final_norm', 'v_meta_tokens', 'v_ffn1_norm', 'v_ffn1_w_gu', 'v_ffn1_w_down', 'v_mix_norm', 'v_w_in', 'v_b_in', 'v_rnn_conv_w', 'v_rnn_conv_b', 'v_rg_w_a', 'v_rg_b_a', 'v_rg_w_x', 'v_rg_b_x', 'v_rg_lambda', 'v_rnn_w_proj', 'v_conv_dw_w', 'v_conv_dw_b', 'v_conv_ln_g', 'v_conv_ln_b', 'v_conv_w_proj', 'v_conv_b_proj', 'v_w_out', 'v_ffn2_norm', 'v_ffn2_w_gu', 'v_ffn2_w_down', 'v_final_norm']
TWIN_OUTPUTS = ['loss', 'grad_x', 'grad_meta_tokens', 'grad_ffn1_norm', 'grad_ffn1_w_gu', 'grad_ffn1_w_down', 'grad_mix_norm', 'grad_w_in', 'grad_b_in', 'grad_rnn_conv_w', 'grad_rnn_conv_b', 'grad_rg_w_a', 'grad_rg_b_a', 'grad_rg_w_x', 'grad_rg_b_x', 'grad_rg_lambda', 'grad_rnn_w_proj', 'grad_conv_dw_w', 'grad_conv_dw_b', 'grad_conv_ln_g', 'grad_conv_ln_b', 'grad_conv_w_proj', 'grad_conv_b_proj', 'grad_w_out', 'grad_ffn2_norm', 'grad_ffn2_w_gu', 'grad_ffn2_w_down', 'grad_final_norm', 'delta_meta_tokens', 'delta_ffn1_norm', 'delta_ffn1_w_gu', 'delta_ffn1_w_down', 'delta_mix_norm', 'delta_w_in', 'delta_b_in', 'delta_rnn_conv_w', 'delta_rnn_conv_b', 'delta_rg_w_a', 'delta_rg_b_a', 'delta_rg_w_x', 'delta_rg_b_x', 'delta_rg_lambda', 'delta_rnn_w_proj', 'delta_conv_dw_w', 'delta_conv_dw_b', 'delta_conv_ln_g', 'delta_conv_ln_b', 'delta_conv_w_proj', 'delta_conv_b_proj', 'delta_w_out', 'delta_ffn2_norm', 'delta_ffn2_w_gu', 'delta_ffn2_w_down', 'delta_final_norm', 'new_m_meta_tokens', 'new_m_ffn1_norm', 'new_m_ffn1_w_gu', 'new_m_ffn1_w_down', 'new_m_mix_norm', 'new_m_w_in', 'new_m_b_in', 'new_m_rnn_conv_w', 'new_m_rnn_conv_b', 'new_m_rg_w_a', 'new_m_rg_b_a', 'new_m_rg_w_x', 'new_m_rg_b_x', 'new_m_rg_lambda', 'new_m_rnn_w_proj', 'new_m_conv_dw_w', 'new_m_conv_dw_b', 'new_m_conv_ln_g', 'new_m_conv_ln_b', 'new_m_conv_w_proj', 'new_m_conv_b_proj', 'new_m_w_out', 'new_m_ffn2_norm', 'new_m_ffn2_w_gu', 'new_m_ffn2_w_down', 'new_m_final_norm', 'new_v_meta_tokens', 'new_v_ffn1_norm', 'new_v_ffn1_w_gu', 'new_v_ffn1_w_down', 'new_v_mix_norm', 'new_v_w_in', 'new_v_b_in', 'new_v_rnn_conv_w', 'new_v_rnn_conv_b', 'new_v_rg_w_a', 'new_v_rg_b_a', 'new_v_rg_w_x', 'new_v_rg_b_x', 'new_v_rg_lambda', 'new_v_rnn_w_proj', 'new_v_conv_dw_w', 'new_v_conv_dw_b', 'new_v_conv_ln_g', 'new_v_conv_ln_b', 'new_v_conv_w_proj', 'new_v_conv_b_proj', 'new_v_w_out', 'new_v_ffn2_norm', 'new_v_ffn2_w_gu', 'new_v_ffn2_w_down', 'new_v_final_norm']
TWIN_LEAF_KINDS = {'loss': 'loss', 'grad_x': 'grad_x', 'grad_meta_tokens': 'grad_w', 'grad_ffn1_norm': 'grad_w', 'grad_ffn1_w_gu': 'grad_w', 'grad_ffn1_w_down': 'grad_w', 'grad_mix_norm': 'grad_w', 'grad_w_in': 'grad_w', 'grad_b_in': 'grad_w', 'grad_rnn_conv_w': 'grad_w', 'grad_rnn_conv_b': 'grad_w', 'grad_rg_w_a': 'grad_w', 'grad_rg_b_a': 'grad_w', 'grad_rg_w_x': 'grad_w', 'grad_rg_b_x': 'grad_w', 'grad_rg_lambda': 'grad_w', 'grad_rnn_w_proj': 'grad_w', 'grad_conv_dw_w': 'grad_w', 'grad_conv_dw_b': 'grad_w', 'grad_conv_ln_g': 'grad_w', 'grad_conv_ln_b': 'grad_w', 'grad_conv_w_proj': 'grad_w', 'grad_conv_b_proj': 'grad_w', 'grad_w_out': 'grad_w', 'grad_ffn2_norm': 'grad_w', 'grad_ffn2_w_gu': 'grad_w', 'grad_ffn2_w_down': 'grad_w', 'grad_final_norm': 'grad_w', 'delta_meta_tokens': 'delta_w', 'delta_ffn1_norm': 'delta_w', 'delta_ffn1_w_gu': 'delta_w', 'delta_ffn1_w_down': 'delta_w', 'delta_mix_norm': 'delta_w', 'delta_w_in': 'delta_w', 'delta_b_in': 'delta_w', 'delta_rnn_conv_w': 'delta_w', 'delta_rnn_conv_b': 'delta_w', 'delta_rg_w_a': 'delta_w', 'delta_rg_b_a': 'delta_w', 'delta_rg_w_x': 'delta_w', 'delta_rg_b_x': 'delta_w', 'delta_rg_lambda': 'delta_w', 'delta_rnn_w_proj': 'delta_w', 'delta_conv_dw_w': 'delta_w', 'delta_conv_dw_b': 'delta_w', 'delta_conv_ln_g': 'delta_w', 'delta_conv_ln_b': 'delta_w', 'delta_conv_w_proj': 'delta_w', 'delta_conv_b_proj': 'delta_w', 'delta_w_out': 'delta_w', 'delta_ffn2_norm': 'delta_w', 'delta_ffn2_w_gu': 'delta_w', 'delta_ffn2_w_down': 'delta_w', 'delta_final_norm': 'delta_w', 'new_m_meta_tokens': 'new_m', 'new_m_ffn1_norm': 'new_m', 'new_m_ffn1_w_gu': 'new_m', 'new_m_ffn1_w_down': 'new_m', 'new_m_mix_norm': 'new_m', 'new_m_w_in': 'new_m', 'new_m_b_in': 'new_m', 'new_m_rnn_conv_w': 'new_m', 'new_m_rnn_conv_b': 'new_m', 'new_m_rg_w_a': 'new_m', 'new_m_rg_b_a': 'new_m', 'new_m_rg_w_x': 'new_m', 'new_m_rg_b_x': 'new_m', 'new_m_rg_lambda': 'new_m', 'new_m_rnn_w_proj': 'new_m', 'new_m_conv_dw_w': 'new_m', 'new_m_conv_dw_b': 'new_m', 'new_m_conv_ln_g': 'new_m', 'new_m_conv_ln_b': 'new_m', 'new_m_conv_w_proj': 'new_m', 'new_m_conv_b_proj': 'new_m', 'new_m_w_out': 'new_m', 'new_m_ffn2_norm': 'new_m', 'new_m_ffn2_w_gu': 'new_m', 'new_m_ffn2_w_down': 'new_m', 'new_m_final_norm': 'new_m', 'new_v_meta_tokens': 'new_v', 'new_v_ffn1_norm': 'new_v', 'new_v_ffn1_w_gu': 'new_v', 'new_v_ffn1_w_down': 'new_v', 'new_v_mix_norm': 'new_v', 'new_v_w_in': 'new_v', 'new_v_b_in': 'new_v', 'new_v_rnn_conv_w': 'new_v', 'new_v_rnn_conv_b': 'new_v', 'new_v_rg_w_a': 'new_v', 'new_v_rg_b_a': 'new_v', 'new_v_rg_w_x': 'new_v', 'new_v_rg_b_x': 'new_v', 'new_v_rg_lambda': 'new_v', 'new_v_rnn_w_proj': 'new_v', 'new_v_conv_dw_w': 'new_v', 'new_v_conv_dw_b': 'new_v', 'new_v_conv_ln_g': 'new_v', 'new_v_conv_ln_b': 'new_v', 'new_v_conv_w_proj': 'new_v', 'new_v_conv_b_proj': 'new_v', 'new_v_w_out': 'new_v', 'new_v_ffn2_norm': 'new_v', 'new_v_ffn2_w_gu': 'new_v', 'new_v_ffn2_w_down': 'new_v', 'new_v_final_norm': 'new_v'}


def _forward(args):
    return _fwd_reference(*[args[k] for k in FWD_PARAMS])


def _output_shape():
    def fwd():
        inp = _fwd_setup_inputs(0)
        return _fwd_reference(*[inp[k] for k in FWD_PARAMS])
    out = _jax.eval_shape(fwd)
    return out.shape, out.dtype

N_MICROBATCH = 1
ADAM_LR = 0.001
ADAM_B1 = 0.9
ADAM_B2 = 0.999
ADAM_EPS = 1e-08
ADAM_WD = 0.01
ADAM_STEP = 10
PER_EXAMPLE_BATCH_AXIS = {'x': 0, 'loss_target': 0}
SHARED_INPUTS = []
_WEIGHT_DTYPES = {'meta_tokens': _jnp.float32, 'ffn1_norm': _jnp.float32, 'ffn1_w_gu': _jnp.float32, 'ffn1_w_down': _jnp.float32, 'mix_norm': _jnp.float32, 'w_in': _jnp.float32, 'b_in': _jnp.float32, 'rnn_conv_w': _jnp.float32, 'rnn_conv_b': _jnp.float32, 'rg_w_a': _jnp.float32, 'rg_b_a': _jnp.float32, 'rg_w_x': _jnp.float32, 'rg_b_x': _jnp.float32, 'rg_lambda': _jnp.float32, 'rnn_w_proj': _jnp.float32, 'conv_dw_w': _jnp.float32, 'conv_dw_b': _jnp.float32, 'conv_ln_g': _jnp.float32, 'conv_ln_b': _jnp.float32, 'conv_w_proj': _jnp.float32, 'conv_b_proj': _jnp.float32, 'w_out': _jnp.float32, 'ffn2_norm': _jnp.float32, 'ffn2_w_gu': _jnp.float32, 'ffn2_w_down': _jnp.float32, 'final_norm': _jnp.float32}
MOMENT_SCALE = {'meta_tokens': 4.244933e-03, 'ffn1_norm': 8.218523e-02, 'ffn1_w_gu': 3.341580e-02, 'ffn1_w_down': 5.454053e-02, 'mix_norm': 8.570999e-02, 'w_in': 3.355456e-02, 'b_in': 1.811543e-01, 'rnn_conv_w': 3.735296e-02, 'rnn_conv_b': 3.787497e-01, 'rg_w_a': 1.072149e-02, 'rg_b_a': 8.127993e-03, 'rg_w_x': 1.878890e-02, 'rg_b_x': 1.419597e-02, 'rg_lambda': 1.714702e-02, 'rnn_w_proj': 3.573344e-02, 'conv_dw_w': 5.570479e-02, 'conv_dw_b': 1.052751e-01, 'conv_ln_g': 6.567040e-02, 'conv_ln_b': 5.726210e-02, 'conv_w_proj': 5.453467e-02, 'conv_b_proj': 1.036455e-01, 'w_out': 6.491592e-02, 'ffn2_norm': 6.463988e-02, 'ffn2_w_gu': 2.818630e-02, 'ffn2_w_down': 4.595606e-02, 'final_norm': 3.198118e+01}


def _to_microbatches(a, axis):
    t = _jnp.moveaxis(a, axis, 0)
    t = t.reshape((N_MICROBATCH, t.shape[0] // N_MICROBATCH) + t.shape[1:])
    return _jnp.moveaxis(t, 1, axis + 1)


def setup_inputs(seed: int = 0) -> dict:
    inp = _fwd_setup_inputs(seed)
    key = _jax.random.fold_in(_jax.random.key(seed), 7919)
    shape, _ = _output_shape()
    out = dict(inp)
    out["loss_target"] = _jax.random.normal(_jax.random.fold_in(key, 0), shape, _jnp.float32)
    for i, name in enumerate(TWIN_WEIGHTS):
        w = inp[name].astype(_jnp.float32)
        if MOMENT_SCALE is None:
            s = _jnp.sqrt(_jnp.mean(_jnp.square(w)) + 1e-30)
        else:
            s = MOMENT_SCALE[name]
        km, kv = _jax.random.split(_jax.random.fold_in(key, i + 1))
        out[name] = w
        out["m_" + name] = s * _jax.random.normal(km, w.shape, _jnp.float32)
        out["v_" + name] = (s * s) * _jax.random.uniform(kv, w.shape, _jnp.float32, 0.5, 1.5)
    if N_MICROBATCH > 1:
        for name, axis in PER_EXAMPLE_BATCH_AXIS.items():
            out[name] = _to_microbatches(out[name], axis)
    return {'x': out['x'], 'meta_tokens': out['meta_tokens'], 'ffn1_norm': out['ffn1_norm'], 'ffn1_w_gu': out['ffn1_w_gu'], 'ffn1_w_down': out['ffn1_w_down'], 'mix_norm': out['mix_norm'], 'w_in': out['w_in'], 'b_in': out['b_in'], 'rnn_conv_w': out['rnn_conv_w'], 'rnn_conv_b': out['rnn_conv_b'], 'rg_w_a': out['rg_w_a'], 'rg_b_a': out['rg_b_a'], 'rg_w_x': out['rg_w_x'], 'rg_b_x': out['rg_b_x'], 'rg_lambda': out['rg_lambda'], 'rnn_w_proj': out['rnn_w_proj'], 'conv_dw_w': out['conv_dw_w'], 'conv_dw_b': out['conv_dw_b'], 'conv_ln_g': out['conv_ln_g'], 'conv_ln_b': out['conv_ln_b'], 'conv_w_proj': out['conv_w_proj'], 'conv_b_proj': out['conv_b_proj'], 'w_out': out['w_out'], 'ffn2_norm': out['ffn2_norm'], 'ffn2_w_gu': out['ffn2_w_gu'], 'ffn2_w_down': out['ffn2_w_down'], 'final_norm': out['final_norm'], 'loss_target': out['loss_target'], 'm_meta_tokens': out['m_meta_tokens'], 'm_ffn1_norm': out['m_ffn1_norm'], 'm_ffn1_w_gu': out['m_ffn1_w_gu'], 'm_ffn1_w_down': out['m_ffn1_w_down'], 'm_mix_norm': out['m_mix_norm'], 'm_w_in': out['m_w_in'], 'm_b_in': out['m_b_in'], 'm_rnn_conv_w': out['m_rnn_conv_w'], 'm_rnn_conv_b': out['m_rnn_conv_b'], 'm_rg_w_a': out['m_rg_w_a'], 'm_rg_b_a': out['m_rg_b_a'], 'm_rg_w_x': out['m_rg_w_x'], 'm_rg_b_x': out['m_rg_b_x'], 'm_rg_lambda': out['m_rg_lambda'], 'm_rnn_w_proj': out['m_rnn_w_proj'], 'm_conv_dw_w': out['m_conv_dw_w'], 'm_conv_dw_b': out['m_conv_dw_b'], 'm_conv_ln_g': out['m_conv_ln_g'], 'm_conv_ln_b': out['m_conv_ln_b'], 'm_conv_w_proj': out['m_conv_w_proj'], 'm_conv_b_proj': out['m_conv_b_proj'], 'm_w_out': out['m_w_out'], 'm_ffn2_norm': out['m_ffn2_norm'], 'm_ffn2_w_gu': out['m_ffn2_w_gu'], 'm_ffn2_w_down': out['m_ffn2_w_down'], 'm_final_norm': out['m_final_norm'], 'v_meta_tokens': out['v_meta_tokens'], 'v_ffn1_norm': out['v_ffn1_norm'], 'v_ffn1_w_gu': out['v_ffn1_w_gu'], 'v_ffn1_w_down': out['v_ffn1_w_down'], 'v_mix_norm': out['v_mix_norm'], 'v_w_in': out['v_w_in'], 'v_b_in': out['v_b_in'], 'v_rnn_conv_w': out['v_rnn_conv_w'], 'v_rnn_conv_b': out['v_rnn_conv_b'], 'v_rg_w_a': out['v_rg_w_a'], 'v_rg_b_a': out['v_rg_b_a'], 'v_rg_w_x': out['v_rg_w_x'], 'v_rg_b_x': out['v_rg_b_x'], 'v_rg_lambda': out['v_rg_lambda'], 'v_rnn_w_proj': out['v_rnn_w_proj'], 'v_conv_dw_w': out['v_conv_dw_w'], 'v_conv_dw_b': out['v_conv_dw_b'], 'v_conv_ln_g': out['v_conv_ln_g'], 'v_conv_ln_b': out['v_conv_ln_b'], 'v_conv_w_proj': out['v_conv_w_proj'], 'v_conv_b_proj': out['v_conv_b_proj'], 'v_w_out': out['v_w_out'], 'v_ffn2_norm': out['v_ffn2_norm'], 'v_ffn2_w_gu': out['v_ffn2_w_gu'], 'v_ffn2_w_down': out['v_ffn2_w_down'], 'v_final_norm': out['v_final_norm']}


def _loss(weights, diff, rest, loss_target):
    with _jax.named_scope("forward"):
        args = {**rest, TWIN_DIFF_INPUT: diff, **{k: w.astype(_WEIGHT_DTYPES[k]) for k, w in weights.items()}}
        y = _forward(args)
    with _jax.named_scope("loss_head"):
        err = _jnp.square(y.astype(_jnp.float32) - loss_target)
        return 0.5 * _jnp.sum(_jnp.mean(err, axis=-1)) if err.ndim else 0.5 * err


def _adamw(w, g, m, v):
    m = ADAM_B1 * m + (1.0 - ADAM_B1) * g
    v = ADAM_B2 * v + (1.0 - ADAM_B2) * _jnp.square(g)
    m_hat = m / (1.0 - ADAM_B1 ** ADAM_STEP)
    v_hat = v / (1.0 - ADAM_B2 ** ADAM_STEP)
    delta = -ADAM_LR * (m_hat / (_jnp.sqrt(v_hat) + ADAM_EPS) + ADAM_WD * w)
    return delta, m, v


def reference(x, meta_tokens, ffn1_norm, ffn1_w_gu, ffn1_w_down, mix_norm, w_in, b_in, rnn_conv_w, rnn_conv_b, rg_w_a, rg_b_a, rg_w_x, rg_b_x, rg_lambda, rnn_w_proj, conv_dw_w, conv_dw_b, conv_ln_g, conv_ln_b, conv_w_proj, conv_b_proj, w_out, ffn2_norm, ffn2_w_gu, ffn2_w_down, final_norm, loss_target, m_meta_tokens, m_ffn1_norm, m_ffn1_w_gu, m_ffn1_w_down, m_mix_norm, m_w_in, m_b_in, m_rnn_conv_w, m_rnn_conv_b, m_rg_w_a, m_rg_b_a, m_rg_w_x, m_rg_b_x, m_rg_lambda, m_rnn_w_proj, m_conv_dw_w, m_conv_dw_b, m_conv_ln_g, m_conv_ln_b, m_conv_w_proj, m_conv_b_proj, m_w_out, m_ffn2_norm, m_ffn2_w_gu, m_ffn2_w_down, m_final_norm, v_meta_tokens, v_ffn1_norm, v_ffn1_w_gu, v_ffn1_w_down, v_mix_norm, v_w_in, v_b_in, v_rnn_conv_w, v_rnn_conv_b, v_rg_w_a, v_rg_b_a, v_rg_w_x, v_rg_b_x, v_rg_lambda, v_rnn_w_proj, v_conv_dw_w, v_conv_dw_b, v_conv_ln_g, v_conv_ln_b, v_conv_w_proj, v_conv_b_proj, v_w_out, v_ffn2_norm, v_ffn2_w_gu, v_ffn2_w_down, v_final_norm):
    given = dict(x=x, meta_tokens=meta_tokens, ffn1_norm=ffn1_norm, ffn1_w_gu=ffn1_w_gu, ffn1_w_down=ffn1_w_down, mix_norm=mix_norm, w_in=w_in, b_in=b_in, rnn_conv_w=rnn_conv_w, rnn_conv_b=rnn_conv_b, rg_w_a=rg_w_a, rg_b_a=rg_b_a, rg_w_x=rg_w_x, rg_b_x=rg_b_x, rg_lambda=rg_lambda, rnn_w_proj=rnn_w_proj, conv_dw_w=conv_dw_w, conv_dw_b=conv_dw_b, conv_ln_g=conv_ln_g, conv_ln_b=conv_ln_b, conv_w_proj=conv_w_proj, conv_b_proj=conv_b_proj, w_out=w_out, ffn2_norm=ffn2_norm, ffn2_w_gu=ffn2_w_gu, ffn2_w_down=ffn2_w_down, final_norm=final_norm, loss_target=loss_target, m_meta_tokens=m_meta_tokens, m_ffn1_norm=m_ffn1_norm, m_ffn1_w_gu=m_ffn1_w_gu, m_ffn1_w_down=m_ffn1_w_down, m_mix_norm=m_mix_norm, m_w_in=m_w_in, m_b_in=m_b_in, m_rnn_conv_w=m_rnn_conv_w, m_rnn_conv_b=m_rnn_conv_b, m_rg_w_a=m_rg_w_a, m_rg_b_a=m_rg_b_a, m_rg_w_x=m_rg_w_x, m_rg_b_x=m_rg_b_x, m_rg_lambda=m_rg_lambda, m_rnn_w_proj=m_rnn_w_proj, m_conv_dw_w=m_conv_dw_w, m_conv_dw_b=m_conv_dw_b, m_conv_ln_g=m_conv_ln_g, m_conv_ln_b=m_conv_ln_b, m_conv_w_proj=m_conv_w_proj, m_conv_b_proj=m_conv_b_proj, m_w_out=m_w_out, m_ffn2_norm=m_ffn2_norm, m_ffn2_w_gu=m_ffn2_w_gu, m_ffn2_w_down=m_ffn2_w_down, m_final_norm=m_final_norm, v_meta_tokens=v_meta_tokens, v_ffn1_norm=v_ffn1_norm, v_ffn1_w_gu=v_ffn1_w_gu, v_ffn1_w_down=v_ffn1_w_down, v_mix_norm=v_mix_norm, v_w_in=v_w_in, v_b_in=v_b_in, v_rnn_conv_w=v_rnn_conv_w, v_rnn_conv_b=v_rnn_conv_b, v_rg_w_a=v_rg_w_a, v_rg_b_a=v_rg_b_a, v_rg_w_x=v_rg_w_x, v_rg_b_x=v_rg_b_x, v_rg_lambda=v_rg_lambda, v_rnn_w_proj=v_rnn_w_proj, v_conv_dw_w=v_conv_dw_w, v_conv_dw_b=v_conv_dw_b, v_conv_ln_g=v_conv_ln_g, v_conv_ln_b=v_conv_ln_b, v_conv_w_proj=v_conv_w_proj, v_conv_b_proj=v_conv_b_proj, v_w_out=v_w_out, v_ffn2_norm=v_ffn2_norm, v_ffn2_w_gu=v_ffn2_w_gu, v_ffn2_w_down=v_ffn2_w_down, v_final_norm=v_final_norm)
    weights = {n: given[n] for n in TWIN_WEIGHTS}
    shared = {n: given[n] for n in SHARED_INPUTS}
    per_example = {n: given[n] for n in ['x']}
    grad_fn = _jax.value_and_grad(_loss, argnums=(0, 1))

    def one_microbatch(ex, loss_target):
        ex = dict(ex)
        diff = ex.pop(TWIN_DIFF_INPUT)
        return grad_fn(weights, diff, {**shared, **ex}, loss_target)

    if N_MICROBATCH == 1:
        loss, (grad_w, grad_x) = one_microbatch(per_example, given["loss_target"])
    else:
        def body(carry, xs):
            loss_sum, grad_sum = carry
            l_k, (gw_k, gx_k) = one_microbatch(xs[0], xs[1])
            with _jax.named_scope("update"):
                return (loss_sum + l_k, _jax.tree.map(_jnp.add, grad_sum, gw_k)), gx_k

        init = (_jnp.zeros((), _jnp.float32), _jax.tree.map(_jnp.zeros_like, weights))
        (loss, grad_w), grad_x = _jax.lax.scan(body, init, (per_example, given["loss_target"]))
    with _jax.named_scope("update"):
        delta_w, new_m, new_v = {}, {}, {}
        for n in TWIN_WEIGHTS:
            delta_w[n], new_m[n], new_v[n] = _adamw(weights[n], grad_w[n], given["m_" + n], given["v_" + n])
    return (loss, grad_x, *[grad_w[n] for n in TWIN_WEIGHTS], *[delta_w[n] for n in TWIN_WEIGHTS],
            *[new_m[n] for n in TWIN_WEIGHTS], *[new_v[n] for n in TWIN_WEIGHTS])
```

```python
import functools

import jax
import jax.numpy as jnp
from jax import lax
from jax.experimental import pallas as pl
from jax.experimental.pallas import tpu as pltpu

f32 = jnp.float32
bf16 = jnp.bfloat16

D = 1024
F = 2816
FS = F // 2
NIN = 6 * D
NMETA = 16
NHEAD = 4
HD = D // NHEAD
KC4 = 4
KC31 = 31
HALO = 32
EPS = 1e-6
TM = 384
NCHIP = 4
MESH = pl.DeviceIdType.MESH

ADAM_LR = 0.001
ADAM_B1 = 0.9
ADAM_B2 = 0.999
ADAM_EPS = 1e-08
ADAM_WD = 0.01
ADAM_STEP = 10

VMEM_LIMIT = 56 * 1024 * 1024


def _cp(n_axes, **kw):
    return pltpu.CompilerParams(dimension_semantics=("arbitrary",) * n_axes,
                                vmem_limit_bytes=VMEM_LIMIT, **kw)


def _nt_dot(a, b):
    return lax.dot_general(a, b, (((1,), (1,)), ((), ())), preferred_element_type=f32)


def _tn_dot(a, b):
    return lax.dot_general(a, b, (((0,), (0,)), ((), ())), preferred_element_type=f32)


def _sigmoid(x):
    return 1.0 / (1.0 + jnp.exp(-x))


def _log1p(y):
    u = 1.0 + y
    d = u - 1.0
    return jnp.where(d == 0.0, y, jnp.log(u) * (y / jnp.where(d == 0.0, 1.0, d)))


def _softplus(x):
    return jnp.maximum(x, 0.0) + _log1p(jnp.exp(-jnp.abs(x)))


def _expm1(x):
    series = x * (1.0 + x * (0.5 + x * (1.0 / 6.0 + x * (1.0 / 24.0 + x * (1.0 / 120.0)))))
    return jnp.where(jnp.abs(x) < 0.1, series, jnp.exp(x) - 1.0)


_GELU_C = 0.7978845608028654
_GELU_K = 0.044715


def _gelu_and_grad(y):
    y2 = y * y
    th = jnp.tanh(_GELU_C * (y + _GELU_K * y * y2))
    gel = 0.5 * y * (1.0 + th)
    dgel = 0.5 * (1.0 + th) + 0.5 * y * (1.0 - th * th) * _GELU_C * (1.0 + 3.0 * _GELU_K * y2)
    return gel, dgel


def _rms_stats(h):
    return lax.rsqrt(jnp.mean(h * h, axis=-1, keepdims=True) + EPS)


def _rms_bwd(dn, h, g):
    r = _rms_stats(h)
    nhat = h * r
    dnh = dn * g
    dh = r * (dnh - nhat * jnp.mean(dnh * nhat, axis=-1, keepdims=True))
    dg = jnp.sum(dn * nhat, axis=0, keepdims=True)
    return dh, dg


def _row_ids(shape):
    return lax.broadcasted_iota(jnp.int32, shape, 0)


def _ffn_fwd(h, g, wgu, wd, name):
    t = h.shape[0]
    nj = 2

    def body(h_ref, g_ref, wg_ref, wu_ref, wd_ref, ho_ref, gate_ref, up_ref, n_ref, nb_sc, acc_sc):
        j = pl.program_id(1)

        @pl.when(j == 0)
        def _():
            hh = h_ref[...]
            nb = (hh * _rms_stats(hh) * g_ref[...]).astype(bf16)
            nb_sc[...] = nb
            n_ref[...] = nb
            acc_sc[...] = jnp.zeros_like(acc_sc)

        nb = nb_sc[...]
        gt = jnp.dot(nb, wg_ref[...], preferred_element_type=f32)
        up = jnp.dot(nb, wu_ref[...], preferred_element_type=f32)
        gate_ref[...] = gt.astype(bf16)
        up_ref[...] = up.astype(bf16)
        a = (gt * _sigmoid(gt) * up).astype(bf16)
        acc_sc[...] += jnp.dot(a, wd_ref[...], preferred_element_type=f32)

        @pl.when(j == nj - 1)
        def _():
            ho_ref[...] = h_ref[...] + 0.5 * acc_sc[...]

    return pl.pallas_call(
        body, name=name, grid=(t // TM, nj),
        in_specs=[
            pl.BlockSpec((TM, D), lambda i, j: (i, 0)),
            pl.BlockSpec((1, D), lambda i, j: (0, 0)),
            pl.BlockSpec((None, D, FS), lambda i, j: (j, 0, 0)),
            pl.BlockSpec((None, D, FS), lambda i, j: (2 + j, 0, 0)),
            pl.BlockSpec((FS, D), lambda i, j: (j, 0)),
        ],
        out_specs=[
            pl.BlockSpec((TM, D), lambda i, j: (i, 0)),
            pl.BlockSpec((TM, FS), lambda i, j: (i, j)),
            pl.BlockSpec((TM, FS), lambda i, j: (i, j)),
            pl.BlockSpec((TM, D), lambda i, j: (i, 0)),
        ],
        out_shape=[
            jax.ShapeDtypeStruct((t, D), f32),
            jax.ShapeDtypeStruct((t, F), bf16),
            jax.ShapeDtypeStruct((t, F), bf16),
            jax.ShapeDtypeStruct((t, D), bf16),
        ],
        scratch_shapes=[pltpu.VMEM((TM, D), bf16), pltpu.VMEM((TM, D), f32)],
        compiler_params=_cp(2),
    )(h, g, wgu, wgu, wd)


def _inproj_fwd(h, g, win, b_in):
    t = h.shape[0]
    tn = 512
    nj = NIN // tn
    per = (NIN // NCHIP) // tn

    def body(h_ref, g_ref, w_ref, b_ref, proj_ref, n_ref, nb_sc):
        j = pl.program_id(1)

        @pl.when(j == 0)
        def _():
            hh = h_ref[...]
            nb = (hh * _rms_stats(hh) * g_ref[...]).astype(bf16)
            nb_sc[...] = nb
            n_ref[...] = nb

        proj_ref[...] = jnp.dot(nb_sc[...], w_ref[...], preferred_element_type=f32) + b_ref[...]

    return pl.pallas_call(
        body, name="inproj_fwd", grid=(t // TM, nj),
        in_specs=[
            pl.BlockSpec((TM, D), lambda i, j: (i, 0)),
            pl.BlockSpec((1, D), lambda i, j: (0, 0)),
            pl.BlockSpec((None, D, tn), lambda i, j: (j // per, 0, j % per)),
            pl.BlockSpec((1, tn), lambda i, j: (0, j)),
        ],
        out_specs=[
            pl.BlockSpec((TM, tn), lambda i, j: (i, j)),
            pl.BlockSpec((TM, D), lambda i, j: (i, 0)),
        ],
        out_shape=[jax.ShapeDtypeStruct((t, NIN), f32), jax.ShapeDtypeStruct((t, D), bf16)],
        scratch_shapes=[pltpu.VMEM((TM, D), bf16)],
        compiler_params=_cp(2),
    )(h, g, win, b_in)


def _block_gates(xr, wa_ref, ba, wx_ref, bx, lam):
    xrb = xr.astype(bf16)
    pa = jnp.concatenate([jnp.dot(xrb[:, hh * HD:(hh + 1) * HD], wa_ref[hh], preferred_element_type=f32)
                          for hh in range(NHEAD)], axis=1)
    px = jnp.concatenate([jnp.dot(xrb[:, hh * HD:(hh + 1) * HD], wx_ref[hh], preferred_element_type=f32)
                          for hh in range(NHEAD)], axis=1)
    ra = _sigmoid(pa + ba)
    ii = _sigmoid(px + bx)
    sp = _softplus(-lam)
    log_a = -8.0 * ra * sp
    a = jnp.exp(log_a)
    sq = jnp.sqrt(-_expm1(2.0 * log_a))
    return ra, ii, a, sq, sp


def _rnn_fwd(proj, cw, cb, wa, ba, wx, bx, lam):
    t = proj.shape[0]
    ng = TM // 8

    def body(x_ref, y_ref, cw_ref, cb_ref, wa_ref, ba_ref, wx_ref, bx_ref, lam_ref,
             xr_ref, hr_ref, z_ref, xext_sc, carry_sc, a_sc, h_sc):
        i = pl.program_id(0)

        @pl.when(i == 0)
        def _():
            xext_sc[0:8, :] = jnp.zeros((8, D), f32)
            carry_sc[...] = jnp.zeros_like(carry_sc)

        x = x_ref[...]
        xext_sc[8:8 + TM, :] = x
        xe = xext_sc[...]
        xr = cb_ref[...] + cw_ref[KC4 - 1:KC4, :] * x
        for k in range(KC4 - 1):
            xr = xr + cw_ref[k:k + 1, :] * pltpu.roll(xe, KC4 - 1 - k, 0)[8:8 + TM]
        xext_sc[0:8, :] = x[TM - 8:TM]

        _, ii, a, sq, _ = _block_gates(xr, wa_ref, ba_ref[...], wx_ref, bx_ref[...], lam_ref[...])
        a_sc[...] = a
        h_sc[...] = sq * ii * xr
        row = _row_ids((8, D))

        def group(r, carry):
            off = pl.multiple_of(r * 8, 8)
            aa = a_sc[pl.ds(off, 8), :]
            hh = h_sc[pl.ds(off, 8), :]
            for s in (1, 2, 4):
                a_sh = jnp.where(row >= s, pltpu.roll(aa, s, 0), 1.0)
                h_sh = jnp.where(row >= s, pltpu.roll(hh, s, 0), 0.0)
                hh = aa * h_sh + hh
                aa = aa * a_sh
            hh = hh + aa * carry
            h_sc[pl.ds(off, 8), :] = hh
            return hh[7:8, :]

        carry_sc[...] = lax.fori_loop(0, ng, group, carry_sc[...])
        hr = h_sc[...]
        gel, _ = _gelu_and_grad(y_ref[...])
        xr_ref[...] = xr
        hr_ref[...] = hr
        z_ref[...] = (hr * gel).astype(bf16)

    vec = pl.BlockSpec((1, D), lambda i: (0, 0))
    return pl.pallas_call(
        body, name="rnn_fwd", grid=(t // TM,),
        in_specs=[
            pl.BlockSpec((TM, D), lambda i: (i, 0)),
            pl.BlockSpec((TM, D), lambda i: (i, 1)),
            pl.BlockSpec((KC4, D), lambda i: (0, 0)),
            vec,
            pl.BlockSpec((NHEAD, HD, HD), lambda i: (0, 0, 0)),
            vec,
            pl.BlockSpec((NHEAD, HD, HD), lambda i: (0, 0, 0)),
            vec, vec,
        ],
        out_specs=[pl.BlockSpec((TM, D), lambda i: (i, 0))] * 3,
        out_shape=[jax.ShapeDtypeStruct((t, D), f32), jax.ShapeDtypeStruct((t, D), f32),
                   jax.ShapeDtypeStruct((t, D), bf16)],
        scratch_shapes=[pltpu.VMEM((TM + 8, D), f32), pltpu.VMEM((1, D), f32),
                        pltpu.VMEM((TM, D), f32), pltpu.VMEM((TM, D), f32)],
        compiler_params=_cp(1),
    )(proj, proj, cw, cb, wa, ba, wx, bx, lam)


def _ln_stats(vc):
    mu = jnp.mean(vc, axis=-1, keepdims=True)
    xc = vc - mu
    rstd = lax.rsqrt(jnp.mean(xc * xc, axis=-1, keepdims=True) + EPS)
    return xc * rstd, rstd


def _conv_fwd(proj, w31, b31, ln_g, ln_b):
    t = proj.shape[0]

    def body(gv_ref, gg_ref, w_ref, b_ref, lg_ref, lb_ref, vc_ref, s_ref, vext_sc):
        i = pl.program_id(0)

        @pl.when(i == 0)
        def _():
            vext_sc[0:HALO, :] = jnp.zeros((HALO, D), f32)

        v = gv_ref[...] * _sigmoid(gg_ref[...])
        vext_sc[HALO:HALO + TM, :] = v
        ve = vext_sc[...]
        acc = jnp.zeros((TM, D), f32) + b_ref[...]
        for s in range(8):
            vs = ve if s == 0 else pltpu.roll(ve, s, 0)
            for m in range(HALO // 8):
                k = KC31 - 1 - (8 * m + s)
                if 0 <= k < KC31:
                    acc = acc + w_ref[k:k + 1, :] * vs[HALO - 8 * m:HALO - 8 * m + TM]
        vext_sc[0:HALO, :] = v[TM - HALO:TM]
        xhat, _ = _ln_stats(acc)
        ln = xhat * lg_ref[...] + lb_ref[...]
        vc_ref[...] = acc
        s_ref[...] = (ln * _sigmoid(ln)).astype(bf16)

    vec = pl.BlockSpec((1, D), lambda i: (0, 0))
    return pl.pallas_call(
        body, name="conv_fwd", grid=(t // TM,),
        in_specs=[
            pl.BlockSpec((TM, D), lambda i: (i, 2)),
            pl.BlockSpec((TM, D), lambda i: (i, 3)),
            pl.BlockSpec((KC31, D), lambda i: (0, 0)),
            vec, vec, vec,
        ],
        out_specs=[pl.BlockSpec((TM, D), lambda i: (i, 0))] * 2,
        out_shape=[jax.ShapeDtypeStruct((t, D), f32), jax.ShapeDtypeStruct((t, D), bf16)],
        scratch_shapes=[pltpu.VMEM((TM + HALO, D), f32)],
        compiler_params=_cp(1),
    )(proj, proj, w31, b31, ln_g, ln_b)


def _merge_fwd(h, z, s, proj, wrp, wcp, bcp, wout):
    t = h.shape[0]

    def body(h_ref, z_ref, s_ref, ga_ref, gb_ref, wrp_ref, wcp_ref, bcp_ref, wout_ref, ho_ref):
        ya = jnp.dot(z_ref[...], wrp_ref[...], preferred_element_type=f32)
        yb = jnp.dot(s_ref[...], wcp_ref[...], preferred_element_type=f32) + bcp_ref[...]
        merged = _sigmoid(ga_ref[...]) * ya + _sigmoid(gb_ref[...]) * yb
        ho_ref[...] = h_ref[...] + jnp.dot(merged.astype(bf16), wout_ref[...], preferred_element_type=f32)

    row = pl.BlockSpec((TM, D), lambda i: (i, 0))
    wsq = pl.BlockSpec((D, D), lambda i: (0, 0))
    return pl.pallas_call(
        body, name="merge_fwd", grid=(t // TM,),
        in_specs=[row, row, row,
                  pl.BlockSpec((TM, D), lambda i: (i, 4)),
                  pl.BlockSpec((TM, D), lambda i: (i, 5)),
                  wsq, wsq, pl.BlockSpec((1, D), lambda i: (0, 0)), wsq],
        out_specs=row,
        out_shape=jax.ShapeDtypeStruct((t, D), f32),
        compiler_params=_cp(1),
    )(h, z, s, proj, proj, wrp, wcp, bcp, wout)


def _final_loss(h, g, tgt, n_valid):
    t = h.shape[0]

    def body(h_ref, g_ref, t_ref, dh_ref, loss_ref, dg_ref):
        i = pl.program_id(0)

        @pl.when(i == 0)
        def _():
            loss_ref[...] = jnp.zeros_like(loss_ref)
            dg_ref[...] = jnp.zeros_like(dg_ref)

        hh = h_ref[...]
        gg = g_ref[...]
        row = i * TM + _row_ids((TM, 1))
        valid = jnp.logical_and(row >= NMETA, row < n_valid)
        out = hh * _rms_stats(hh) * gg
        err = jnp.where(valid, out - t_ref[...], 0.0)
        loss_ref[...] += 0.5 * jnp.sum(err * err) * (1.0 / D)
        dh, dg = _rms_bwd(err * (1.0 / D), hh, gg)
        dh_ref[...] = dh
        dg_ref[...] += dg

    row_spec = pl.BlockSpec((TM, D), lambda i: (i, 0))
    return pl.pallas_call(
        body, name="final_loss", grid=(t // TM,),
        in_specs=[row_spec, pl.BlockSpec((1, D), lambda i: (0, 0)), row_spec],
        out_specs=[row_spec, pl.BlockSpec((8, 128), lambda i: (0, 0)), pl.BlockSpec((1, D), lambda i: (0, 0))],
        out_shape=[jax.ShapeDtypeStruct((t, D), f32), jax.ShapeDtypeStruct((8, 128), f32),
                   jax.ShapeDtypeStruct((1, D), f32)],
        compiler_params=_cp(1),
    )(h, g, tgt)


def _ffn_bwd(dh, h, g, gate, up, wgu, wd, name):
    t = h.shape[0]
    nj = 2

    def body(dh_ref, h_ref, g_ref, gate_ref, up_ref, wg_ref, wu_ref, wd_ref,
             dhi_ref, dgate_ref, dup_ref, a_ref, df_ref, dg_ref, dfb_sc, dn_sc):
        i = pl.program_id(0)
        j = pl.program_id(1)

        @pl.when(jnp.logical_and(i == 0, j == 0))
        def _():
            dg_ref[...] = jnp.zeros_like(dg_ref)

        @pl.when(j == 0)
        def _():
            dfb = (0.5 * dh_ref[...]).astype(bf16)
            dfb_sc[...] = dfb
            df_ref[...] = dfb
            dn_sc[...] = jnp.zeros_like(dn_sc)

        da = _nt_dot(dfb_sc[...], wd_ref[...])
        gt = gate_ref[...].astype(f32)
        uu = up_ref[...].astype(f32)
        sg = _sigmoid(gt)
        silu = gt * sg
        a_ref[...] = (silu * uu).astype(bf16)
        dgt = (da * uu * (sg * (1.0 + gt * (1.0 - sg)))).astype(bf16)
        dup = (da * silu).astype(bf16)
        dgate_ref[...] = dgt
        dup_ref[...] = dup
        dn_sc[...] += _nt_dot(dgt, wg_ref[...]) + _nt_dot(dup, wu_ref[...])

        @pl.when(j == nj - 1)
        def _():
            dhin, dg = _rms_bwd(dn_sc[...], h_ref[...], g_ref[...])
            dhi_ref[...] = dh_ref[...] + dhin
            dg_ref[...] += dg

    rowd = pl.BlockSpec((TM, D), lambda i, j: (i, 0))
    rowf = pl.BlockSpec((TM, FS), lambda i, j: (i, j))
    vec = pl.BlockSpec((1, D), lambda i, j: (0, 0))
    return pl.pallas_call(
        body, name=name, grid=(t // TM, nj),
        in_specs=[rowd, rowd, vec, rowf, rowf,
                  pl.BlockSpec((None, D, FS), lambda i, j: (j, 0, 0)),
                  pl.BlockSpec((None, D, FS), lambda i, j: (2 + j, 0, 0)),
                  pl.BlockSpec((FS, D), lambda i, j: (j, 0))],
        out_specs=[rowd, rowf, rowf, rowf, rowd, vec],
        out_shape=[jax.ShapeDtypeStruct((t, D), f32), jax.ShapeDtypeStruct((t, F), bf16),
                   jax.ShapeDtypeStruct((t, F), bf16), jax.ShapeDtypeStruct((t, F), bf16),
                   jax.ShapeDtypeStruct((t, D), bf16), jax.ShapeDtypeStruct((1, D), f32)],
        scratch_shapes=[pltpu.VMEM((TM, D), bf16), pltpu.VMEM((TM, D), f32)],
        compiler_params=_cp(2),
    )(dh, h, g, gate, up, wgu, wgu, wd)


def _big_tile(t):
    for cand in (2112, 1408, 768, 384):
        if t % cand == 0:
            return cand
    raise ValueError(t)


def _tn_matmul(a, b, tk, tn, out_shape, out_block, out_map, name):
    t, kk = a.shape
    _, nn = b.shape
    tmm = _big_tile(t)
    nm = t // tmm

    def body(a_ref, b_ref, o_ref, acc_sc):
        m = pl.program_id(2)

        @pl.when(m == 0)
        def _():
            acc_sc[...] = jnp.zeros_like(acc_sc)

        acc_sc[...] += _tn_dot(a_ref[...], b_ref[...])

        @pl.when(m == nm - 1)
        def _():
            o_ref[...] = acc_sc[...].astype(o_ref.dtype)

    return pl.pallas_call(
        body, name=name, grid=(kk // tk, nn // tn, nm),
        in_specs=[pl.BlockSpec((tmm, tk), lambda k, n, m: (m, k)),
                  pl.BlockSpec((tmm, tn), lambda k, n, m: (m, n))],
        out_specs=pl.BlockSpec(out_block, out_map),
        out_shape=jax.ShapeDtypeStruct(out_shape, bf16),
        scratch_shapes=[pltpu.VMEM((tk, tn), f32)],
        compiler_params=_cp(3),
    )(a, b)


def _merge_bwd(dh, z, s, proj, wrp, wcp, bcp, wout):
    t = dh.shape[0]

    def body(dh_ref, z_ref, s_ref, ga_ref, gb_ref, wrp_ref, wcp_ref, bcp_ref, wout_ref,
             dz_ref, ds_ref, dgab_ref, dhb_ref, mg_ref, dya_ref, dyb_ref, dbcp_ref):
        i = pl.program_id(0)

        @pl.when(i == 0)
        def _():
            dbcp_ref[...] = jnp.zeros_like(dbcp_ref)

        dhb = dh_ref[...].astype(bf16)
        dhb_ref[...] = dhb
        dmg = _nt_dot(dhb, wout_ref[...])
        ya = jnp.dot(z_ref[...], wrp_ref[...], preferred_element_type=f32)
        yb = jnp.dot(s_ref[...], wcp_ref[...], preferred_element_type=f32) + bcp_ref[...]
        sa = _sigmoid(ga_ref[...])
        sb = _sigmoid(gb_ref[...])
        mg_ref[...] = (sa * ya + sb * yb).astype(bf16)
        dgab_ref[:, 0:D] = (dmg * ya * sa * (1.0 - sa)).astype(bf16)
        dgab_ref[:, D:2 * D] = (dmg * yb * sb * (1.0 - sb)).astype(bf16)
        dya = dmg * sa
        dyb = dmg * sb
        dbcp_ref[...] += jnp.sum(dyb, axis=0, keepdims=True)
        dyab = dya.astype(bf16)
        dybb = dyb.astype(bf16)
        dya_ref[...] = dyab
        dyb_ref[...] = dybb
        dz_ref[...] = _nt_dot(dyab, wrp_ref[...])
        ds_ref[...] = _nt_dot(dybb, wcp_ref[...])

    row = pl.BlockSpec((TM, D), lambda i: (i, 0))
    wsq = pl.BlockSpec((D, D), lambda i: (0, 0))
    vec = pl.BlockSpec((1, D), lambda i: (0, 0))
    rowb = jax.ShapeDtypeStruct((t, D), bf16)
    return pl.pallas_call(
        body, name="merge_bwd", grid=(t // TM,),
        in_specs=[row, row, row,
                  pl.BlockSpec((TM, D), lambda i: (i, 4)),
                  pl.BlockSpec((TM, D), lambda i: (i, 5)),
                  wsq, wsq, vec, wsq],
        out_specs=[row, row,
                   pl.BlockSpec((TM, 2 * D), lambda i: (i, 2)),
                   row, row, row, row, vec],
        out_shape=[jax.ShapeDtypeStruct((t, D), f32), jax.ShapeDtypeStruct((t, D), f32),
                   jax.ShapeDtypeStruct((t, NIN), bf16),
                   rowb, rowb, rowb, rowb, jax.ShapeDtypeStruct((1, D), f32)],
        compiler_params=_cp(1),
    )(dh, z, s, proj, proj, wrp, wcp, bcp, wout)


def _conv_bwd(ds, vc, proj, dproj, w31, ln_g, ln_b):
    t = ds.shape[0]
    nt = t // TM
    hb = TM // HALO

    def body(ds_ref, vc_ref, gv_ref, gg_ref, gvp_ref, ggp_ref, dpin_ref, w_ref, lg_ref, lb_ref,
             dgvg_ref, dw_ref, db_ref, dlg_ref, dlb_ref, dext_sc, vext_sc, dwacc_sc):
        del dpin_ref
        i = pl.program_id(0)
        tile = nt - 1 - i

        @pl.when(i == 0)
        def _():
            dext_sc[TM:TM + HALO, :] = jnp.zeros((HALO, D), f32)
            dwacc_sc[...] = jnp.zeros_like(dwacc_sc)
            db_ref[...] = jnp.zeros_like(db_ref)
            dlg_ref[...] = jnp.zeros_like(dlg_ref)
            dlb_ref[...] = jnp.zeros_like(dlb_ref)

        vc = vc_ref[...]
        xhat, rstd = _ln_stats(vc)
        lg = lg_ref[...]
        ln = xhat * lg + lb_ref[...]
        sg = _sigmoid(ln)
        dln = ds_ref[...] * (sg * (1.0 + ln * (1.0 - sg)))
        dlg_ref[...] += jnp.sum(dln * xhat, axis=0, keepdims=True)
        dlb_ref[...] += jnp.sum(dln, axis=0, keepdims=True)
        dxh = dln * lg
        dvc = rstd * (dxh - jnp.mean(dxh, axis=-1, keepdims=True)
                      - xhat * jnp.mean(dxh * xhat, axis=-1, keepdims=True))
        db_ref[...] += jnp.sum(dvc, axis=0, keepdims=True)

        dext_sc[0:TM, :] = dvc
        de = dext_sc[...]
        dv = jnp.zeros((TM, D), f32)
        for s in range(8):
            dsh = de if s == 0 else pltpu.roll(de, TM + HALO - s, 0)
            for m in range(HALO // 8):
                k = KC31 - 1 - (8 * m + s)
                if 0 <= k < KC31:
                    dv = dv + w_ref[k:k + 1, :] * dsh[8 * m:8 * m + TM]
        dext_sc[TM:TM + HALO, :] = dvc[0:HALO]

        gv = gv_ref[...]
        sgg = _sigmoid(gg_ref[...])
        dgvg_ref[:, 0:D] = (dv * sgg).astype(bf16)
        dgvg_ref[:, D:2 * D] = (dv * gv * sgg * (1.0 - sgg)).astype(bf16)

        vprev = gvp_ref[...] * _sigmoid(ggp_ref[...])
        vext_sc[0:HALO, :] = jnp.where(tile > 0, vprev, 0.0)
        vext_sc[HALO:HALO + TM, :] = gv * sgg
        ve = vext_sc[...]
        for s in range(8):
            vs = ve if s == 0 else pltpu.roll(ve, s, 0)
            for m in range(HALO // 8):
                k = KC31 - 1 - (8 * m + s)
                if 0 <= k < KC31:
                    prod = dvc * vs[HALO - 8 * m:HALO - 8 * m + TM]
                    dwacc_sc[k] += jnp.sum(prod.reshape(TM // 8, 8, D), axis=0)

        @pl.when(i == nt - 1)
        def _():
            for k in range(KC31):
                dw_ref[k:k + 1, :] = jnp.sum(dwacc_sc[k], axis=0, keepdims=True)

    rev = lambda i: (nt - 1 - i, 0)
    vec = pl.BlockSpec((1, D), lambda i: (0, 0))
    halo_row = lambda i: jnp.maximum((nt - 1 - i) * hb - 1, 0)
    return pl.pallas_call(
        body, name="conv_bwd", grid=(nt,),
        in_specs=[
            pl.BlockSpec((TM, D), rev),
            pl.BlockSpec((TM, D), rev),
            pl.BlockSpec((TM, D), lambda i: (nt - 1 - i, 2)),
            pl.BlockSpec((TM, D), lambda i: (nt - 1 - i, 3)),
            pl.BlockSpec((HALO, D), lambda i: (halo_row(i), 2)),
            pl.BlockSpec((HALO, D), lambda i: (halo_row(i), 3)),
            pl.BlockSpec(memory_space=pl.ANY),
            pl.BlockSpec((KC31, D), lambda i: (0, 0)),
            vec, vec,
        ],
        out_specs=[
            pl.BlockSpec((TM, 2 * D), lambda i: (nt - 1 - i, 1)),
            pl.BlockSpec((KC31, D), lambda i: (0, 0)),
            vec, vec, vec,
        ],
        out_shape=[jax.ShapeDtypeStruct((t, NIN), bf16),
                   jax.ShapeDtypeStruct((KC31, D), f32),
                   jax.ShapeDtypeStruct((1, D), f32), jax.ShapeDtypeStruct((1, D), f32),
                   jax.ShapeDtypeStruct((1, D), f32)],
        scratch_shapes=[pltpu.VMEM((TM + HALO, D), f32), pltpu.VMEM((TM + HALO, D), f32),
                        pltpu.VMEM((KC31, 8, D), f32)],
        input_output_aliases={6: 0},
        compiler_params=_cp(1),
    )(ds, vc, proj, proj, proj, proj, dproj, w31, ln_g, ln_b)


def _rnn_bwd(dz, xr, hr, proj, dproj, cw, wa, ba, wx, bx, lam):
    t = dz.shape[0]
    nt = t // TM
    ng = TM // 8
    hq = HD // NCHIP

    def body(dz_ref, xr_ref, hr_ref, hrp_ref, x_ref, xp_ref, y_ref, dpin_ref,
             cw_ref, wa_ref, ba_ref, wx_ref, bx_ref, lam_ref,
             dxy_ref, dwa_ref, dwx_ref, dcw_ref, dcb_ref, dba_ref, dbx_ref, dlam_ref,
             anext_sc, gcarry_sc, dext_sc, xext_sc, m_sc, g_sc, dwa_sc, dwx_sc, dsp_sc):
        del dpin_ref
        i = pl.program_id(0)
        tile = nt - 1 - i

        @pl.when(i == 0)
        def _():
            anext_sc[...] = jnp.zeros_like(anext_sc)
            gcarry_sc[...] = jnp.zeros_like(gcarry_sc)
            dext_sc[TM:TM + 8, :] = jnp.zeros((8, D), f32)
            dwa_sc[...] = jnp.zeros_like(dwa_sc)
            dwx_sc[...] = jnp.zeros_like(dwx_sc)
            dsp_sc[...] = jnp.zeros_like(dsp_sc)
            dcw_ref[...] = jnp.zeros_like(dcw_ref)
            dcb_ref[...] = jnp.zeros_like(dcb_ref)
            dba_ref[...] = jnp.zeros_like(dba_ref)
            dbx_ref[...] = jnp.zeros_like(dbx_ref)

        xr = xr_ref[...]
        hr = hr_ref[...]
        dz = dz_ref[...]
        gel, dgel = _gelu_and_grad(y_ref[...])
        dxy_ref[:, D:2 * D] = (dz * hr * dgel).astype(bf16)
        ra, ii, a, sq, sp = _block_gates(xr, wa_ref, ba_ref[...], wx_ref, bx_ref[...], lam_ref[...])

        row = _row_ids((TM, D))
        m_sc[...] = jnp.where(row == TM - 1, anext_sc[...], pltpu.roll(a, TM - 1, 0))
        anext_sc[...] = a[0:1, :]
        g_sc[...] = dz * gel
        row8 = _row_ids((8, D))

        def group(qq, carry):
            off = pl.multiple_of((ng - 1 - qq) * 8, 8)
            mm = m_sc[pl.ds(off, 8), :]
            dd = g_sc[pl.ds(off, 8), :]
            for s in (1, 2, 4):
                m_sh = jnp.where(row8 < 8 - s, pltpu.roll(mm, 8 - s, 0), 1.0)
                d_sh = jnp.where(row8 < 8 - s, pltpu.roll(dd, 8 - s, 0), 0.0)
                dd = dd + mm * d_sh
                mm = mm * m_sh
            dd = dd + mm * carry
            g_sc[pl.ds(off, 8), :] = dd
            return dd[0:1, :]

        gcarry_sc[...] = lax.fori_loop(0, ng, group, gcarry_sc[...])
        gg = g_sc[...]

        hlast = jnp.where(tile > 0, hrp_ref[7:8, :], 0.0)
        hprev = jnp.where(row == 0, hlast, pltpu.roll(hr, 1, 0))
        d_a = gg * hprev
        dsq = gg * ii * xr
        dii = gg * sq * xr
        dxr = gg * sq * ii
        dlog = d_a * a - dsq * (a * a / sq)
        dsp_sc[...] += jnp.sum(dlog * (-8.0 * ra), axis=0, keepdims=True)
        dpa = dlog * (-8.0 * sp) * ra * (1.0 - ra)
        dpx = dii * ii * (1.0 - ii)
        dba_ref[...] += jnp.sum(dpa, axis=0, keepdims=True)
        dbx_ref[...] += jnp.sum(dpx, axis=0, keepdims=True)
        dpab = dpa.astype(bf16)
        dpxb = dpx.astype(bf16)
        xrb = xr.astype(bf16)
        back = []
        for hh in range(NHEAD):
            cols = slice(hh * HD, (hh + 1) * HD)
            back.append(_nt_dot(dpab[:, cols], wa_ref[hh]) + _nt_dot(dpxb[:, cols], wx_ref[hh]))
            dwa_sc[hh] += _tn_dot(xrb[:, cols], dpab[:, cols])
            dwx_sc[hh] += _tn_dot(xrb[:, cols], dpxb[:, cols])
        dxr = dxr + jnp.concatenate(back, axis=1)

        dext_sc[0:TM, :] = dxr
        de = dext_sc[...]
        dx = cw_ref[KC4 - 1:KC4, :] * dxr
        for k in range(KC4 - 1):
            dx = dx + cw_ref[k:k + 1, :] * pltpu.roll(de, TM + 8 - (KC4 - 1 - k), 0)[0:TM]
        dext_sc[TM:TM + 8, :] = dxr[0:8]
        dxy_ref[:, 0:D] = dx.astype(bf16)

        x = x_ref[...]
        xext_sc[0:8, :] = jnp.where(tile > 0, xp_ref[...], 0.0)
        xext_sc[8:8 + TM, :] = x
        xe = xext_sc[...]
        dcw_ref[KC4 - 1:KC4, :] += jnp.sum(dxr * x, axis=0, keepdims=True)
        for k in range(KC4 - 1):
            xs = pltpu.roll(xe, KC4 - 1 - k, 0)[8:8 + TM]
            dcw_ref[k:k + 1, :] += jnp.sum(dxr * xs, axis=0, keepdims=True)
        dcb_ref[...] += jnp.sum(dxr, axis=0, keepdims=True)

        @pl.when(i == nt - 1)
        def _():
            for hh in range(NHEAD):
                for qc in range(NCHIP):
                    dwa_ref[qc, hh] = dwa_sc[hh, qc * hq:(qc + 1) * hq, :].astype(bf16)
                    dwx_ref[qc, hh] = dwx_sc[hh, qc * hq:(qc + 1) * hq, :].astype(bf16)
            dlam_ref[...] = -dsp_sc[...] * _sigmoid(-lam_ref[...])

    rev = lambda i: (nt - 1 - i, 0)
    vec = pl.BlockSpec((1, D), lambda i: (0, 0))
    prev8 = lambda i: jnp.maximum((nt - 1 - i) * ng - 1, 0)
    wblk = pl.BlockSpec((NHEAD, HD, HD), lambda i: (0, 0, 0))
    gblk = pl.BlockSpec((NCHIP, NHEAD, hq, HD), lambda i: (0, 0, 0, 0))
    return pl.pallas_call(
        body, name="rnn_bwd", grid=(nt,),
        in_specs=[
            pl.BlockSpec((TM, D), rev),
            pl.BlockSpec((TM, D), rev),
            pl.BlockSpec((TM, D), rev),
            pl.BlockSpec((8, D), lambda i: (prev8(i), 0)),
            pl.BlockSpec((TM, D), lambda i: (nt - 1 - i, 0)),
            pl.BlockSpec((8, D), lambda i: (prev8(i), 0)),
            pl.BlockSpec((TM, D), lambda i: (nt - 1 - i, 1)),
            pl.BlockSpec(memory_space=pl.ANY),
            pl.BlockSpec((KC4, D), lambda i: (0, 0)),
            wblk, vec, wblk, vec, vec,
        ],
        out_specs=[
            pl.BlockSpec((TM, 2 * D), lambda i: (nt - 1 - i, 0)),
            gblk, gblk,
            pl.BlockSpec((KC4, D), lambda i: (0, 0)),
            vec, vec, vec, vec,
        ],
        out_shape=[jax.ShapeDtypeStruct((t, NIN), bf16),
                   jax.ShapeDtypeStruct((NCHIP, NHEAD, hq, HD), bf16),
                   jax.ShapeDtypeStruct((NCHIP, NHEAD, hq, HD), bf16),
                   jax.ShapeDtypeStruct((KC4, D), f32),
                   jax.ShapeDtypeStruct((1, D), f32), jax.ShapeDtypeStruct((1, D), f32),
                   jax.ShapeDtypeStruct((1, D), f32), jax.ShapeDtypeStruct((1, D), f32)],
        scratch_shapes=[pltpu.VMEM((1, D), f32), pltpu.VMEM((1, D), f32),
                        pltpu.VMEM((TM + 8, D), f32), pltpu.VMEM((TM + 8, D), f32),
                        pltpu.VMEM((TM, D), f32), pltpu.VMEM((TM, D), f32),
                        pltpu.VMEM((NHEAD, HD, HD), f32), pltpu.VMEM((NHEAD, HD, HD), f32),
                        pltpu.VMEM((1, D), f32)],
        input_output_aliases={7: 0},
        compiler_params=_cp(1),
    )(dz, xr, hr, hr, proj, proj, proj, dproj, cw, wa, ba, wx, bx, lam)


def _inproj_bwd(dproj, dh, h, g, win):
    t = h.shape[0]
    tn = 512
    nj = NIN // tn
    per = (NIN // NCHIP) // tn

    def body(dp_ref, dh_ref, h_ref, g_ref, w_ref, dhi_ref, dg_ref, db_ref, dn_sc):
        i = pl.program_id(0)
        j = pl.program_id(1)

        @pl.when(jnp.logical_and(i == 0, j == 0))
        def _():
            dg_ref[...] = jnp.zeros_like(dg_ref)
            db_ref[...] = jnp.zeros_like(db_ref)

        @pl.when(j == 0)
        def _():
            dn_sc[...] = jnp.zeros_like(dn_sc)

        dp = dp_ref[...]
        dn_sc[...] += _nt_dot(dp, w_ref[...])
        db_ref[j] += jnp.sum(dp.astype(f32), axis=0, keepdims=True)

        @pl.when(j == nj - 1)
        def _():
            dhin, dg = _rms_bwd(dn_sc[...], h_ref[...], g_ref[...])
            dhi_ref[...] = dh_ref[...] + dhin
            dg_ref[...] += dg

    rowd = pl.BlockSpec((TM, D), lambda i, j: (i, 0))
    vec = pl.BlockSpec((1, D), lambda i, j: (0, 0))
    return pl.pallas_call(
        body, name="inproj_bwd", grid=(t // TM, nj),
        in_specs=[pl.BlockSpec((TM, tn), lambda i, j: (i, j)), rowd, rowd, vec,
                  pl.BlockSpec((None, D, tn), lambda i, j: (j // per, 0, j % per))],
        out_specs=[rowd, vec, pl.BlockSpec((nj, 1, tn), lambda i, j: (0, 0, 0))],
        out_shape=[jax.ShapeDtypeStruct((t, D), f32), jax.ShapeDtypeStruct((1, D), f32),
                   jax.ShapeDtypeStruct((nj, 1, tn), f32)],
        scratch_shapes=[pltpu.VMEM((TM, D), f32)],
        compiler_params=_cp(2),
    )(dproj, dh, h, g, win)


def _weight_grad_ffn(n, dgate, dup, a, df, tag):
    halves = [
        _tn_matmul(n, part, D, FS, (2, D, FS), (None, D, FS), lambda k, nn, m: (nn, 0, 0), tag + which)
        for part, which in ((dgate, "_dwg"), (dup, "_dwu"))
    ]
    dwd = _tn_matmul(a, df, FS, D, (F, D), (FS, D), lambda k, nn, m: (k, 0), tag + "_dwd")
    return jnp.concatenate(halves, axis=0), dwd


def _square_grad(a, b, name):
    return _tn_matmul(a, b, D, D, (D, D), (D, D), lambda k, nn, m: (0, 0), name)


def _local_step(h0, tgt, n_valid, p):
    h1, gate1, up1, n1 = _ffn_fwd(h0, p["ffn1_norm"], p["ffn1_w_gu"], p["ffn1_w_down"], "ffn1_fwd")
    proj, n2 = _inproj_fwd(h1, p["mix_norm"], p["w_in"], p["b_in"])
    xr, hr, z = _rnn_fwd(proj, p["rnn_conv_w"], p["rnn_conv_b"], p["rg_w_a"], p["rg_b_a"],
                         p["rg_w_x"], p["rg_b_x"], p["rg_lambda"])
    vc, s = _conv_fwd(proj, p["conv_dw_w"], p["conv_dw_b"], p["conv_ln_g"], p["conv_ln_b"])
    h2 = _merge_fwd(h1, z, s, proj, p["rnn_w_proj"], p["conv_w_proj"], p["conv_b_proj"], p["w_out"])
    h3, gate2, up2, n3 = _ffn_fwd(h2, p["ffn2_norm"], p["ffn2_w_gu"], p["ffn2_w_down"], "ffn2_fwd")
    dh3, loss, d_final = _final_loss(h3, p["final_norm"], tgt, n_valid)

    g = {"final_norm": d_final}
    dh2, dgate2, dup2, a2, df2, g["ffn2_norm"] = _ffn_bwd(
        dh3, h2, p["ffn2_norm"], gate2, up2, p["ffn2_w_gu"], p["ffn2_w_down"], "ffn2_bwd")
    g["ffn2_w_gu"], g["ffn2_w_down"] = _weight_grad_ffn(n3, dgate2, dup2, a2, df2, "ffn2")

    dz, ds, dproj, dh2b, merged, dya, dyb, g["conv_b_proj"] = _merge_bwd(
        dh2, z, s, proj, p["rnn_w_proj"], p["conv_w_proj"], p["conv_b_proj"], p["w_out"])
    g["w_out"] = _square_grad(merged, dh2b, "dw_out")
    g["rnn_w_proj"] = _square_grad(z, dya, "dw_rnn_proj")
    g["conv_w_proj"] = _square_grad(s, dyb, "dw_conv_proj")
    dproj, g["conv_dw_w"], g["conv_dw_b"], g["conv_ln_g"], g["conv_ln_b"] = _conv_bwd(
        ds, vc, proj, dproj, p["conv_dw_w"], p["conv_ln_g"], p["conv_ln_b"])
    (dproj, g["rg_w_a"], g["rg_w_x"], g["rnn_conv_w"], g["rnn_conv_b"], g["rg_b_a"], g["rg_b_x"],
     g["rg_lambda"]) = _rnn_bwd(dz, xr, hr, proj, dproj, p["rnn_conv_w"], p["rg_w_a"], p["rg_b_a"],
                                p["rg_w_x"], p["rg_b_x"], p["rg_lambda"])
    dh1, g["mix_norm"], db_in = _inproj_bwd(dproj, dh2, h1, p["mix_norm"], p["w_in"])
    g["b_in"] = db_in.reshape(1, NIN)
    g["w_in"] = _tn_matmul(n2, dproj, D, NIN // NCHIP, (NCHIP, D, NIN // NCHIP),
                           (None, D, NIN // NCHIP), lambda k, nn, m: (nn, 0, 0), "dw_in")

    dh0, dgate1, dup1, a1, df1, g["ffn1_norm"] = _ffn_bwd(
        dh1, h0, p["ffn1_norm"], gate1, up1, p["ffn1_w_gu"], p["ffn1_w_down"], "ffn1_bwd")
    g["ffn1_w_gu"], g["ffn1_w_down"] = _weight_grad_ffn(n1, dgate1, dup1, a1, df1, "ffn1")
    return loss, dh0, g


ANY = pl.BlockSpec(memory_space=pl.ANY)


def _place():
    x, y, c = lax.axis_index("x"), lax.axis_index("y"), lax.axis_index("c")
    chips = [(1 - x, y), (x, 1 - y), (1 - x, 1 - y)]
    return x, y, c, chips


def _chip_id(chip):
    return 2 * chip[0] + chip[1]


def _gather_shards(shards):
    n = len(shards)

    def body(*refs):
        ins, outs = refs[:n], refs[n:2 * n]
        send_sems, recv_sems, local_sems = refs[2 * n:]
        x, y, c, chips = _place()
        q = 2 * x + y
        sibling = (x, y, 1 - c)

        def half(a, which):
            hr = shards[a].shape[0] // 2
            return pl.ds(which * hr, hr)

        def remote(a, k, src, dst, to):
            return pltpu.make_async_remote_copy(src_ref=src, dst_ref=dst, send_sem=send_sems.at[a, k],
                                                recv_sem=recv_sems.at[a, k], device_id=to, device_id_type=MESH)

        local, sent = [], []
        for a in range(n):
            lc = pltpu.make_async_copy(ins[a], outs[a].at[q], local_sems.at[a])
            lc.start()
            local.append(lc)
            for j, chip in enumerate(chips):
                cp = remote(a, j, ins[a].at[half(a, c)], outs[a].at[q, half(a, c)], (chip[0], chip[1], c))
                cp.start()
                sent.append(cp)
        for a in range(n):
            for j, chip in enumerate(chips):
                got = outs[a].at[_chip_id(chip), half(a, c)]
                remote(a, j, got, got, (chip[0], chip[1], c)).wait_recv()
                cp = remote(a, 3 + j, got, got, sibling)
                cp.start()
                sent.append(cp)
        for a in range(n):
            for j, chip in enumerate(chips):
                got = outs[a].at[_chip_id(chip), half(a, 1 - c)]
                remote(a, 3 + j, got, got, sibling).wait_recv()
        for cp in sent:
            cp.wait_send()
        for lc in local:
            lc.wait()

    return pl.pallas_call(
        body, name="gather_shards",
        in_specs=[ANY] * n, out_specs=[ANY] * n,
        out_shape=[jax.ShapeDtypeStruct((NCHIP,) + s.shape, s.dtype) for s in shards],
        scratch_shapes=[pltpu.SemaphoreType.DMA((n, 6)), pltpu.SemaphoreType.DMA((n, 6)),
                        pltpu.SemaphoreType.DMA((n,))],
    )(*shards)


def _pair_exchange(parts):
    n = len(parts)

    def body(*refs):
        ins, outs = refs[:n], refs[n:2 * n]
        send_sems, recv_sems = refs[2 * n:]
        x, y, c, _ = _place()
        copies = []
        for a in range(n):
            hr = parts[a].shape[1] // 2
            cp = pltpu.make_async_remote_copy(
                src_ref=ins[a].at[:, pl.ds((1 - c) * hr, hr)], dst_ref=outs[a],
                send_sem=send_sems.at[a], recv_sem=recv_sems.at[a],
                device_id=(x, y, 1 - c), device_id_type=MESH)
            cp.start()
            copies.append(cp)
        for cp in copies:
            cp.wait()

    return pl.pallas_call(
        body, name="pair_exchange",
        in_specs=[ANY] * n, out_specs=[ANY] * n,
        out_shape=[jax.ShapeDtypeStruct((NCHIP, s.shape[1] // 2, s.shape[2]), s.dtype) for s in parts],
        scratch_shapes=[pltpu.SemaphoreType.DMA((n,)), pltpu.SemaphoreType.DMA((n,))],
    )(*parts)


def _pair_add(part, got, cidx, name):
    _, r, cc = part.shape
    hr = r // 2

    def body(c_ref, p_ref, g_ref, o_ref):
        del c_ref
        o_ref[...] = (p_ref[...].astype(f32) + g_ref[...].astype(f32)).astype(bf16)

    return pl.pallas_call(
        body, name=name,
        grid_spec=pltpu.PrefetchScalarGridSpec(
            num_scalar_prefetch=1, grid=(NCHIP,),
            in_specs=[pl.BlockSpec((None, hr, cc), lambda s, c_ref: (s, c_ref[0], 0)),
                      pl.BlockSpec((None, hr, cc), lambda s, c_ref: (s, 0, 0))],
            out_specs=pl.BlockSpec((None, hr, cc), lambda s, c_ref: (s, 0, 0))),
        out_shape=jax.ShapeDtypeStruct((NCHIP, hr, cc), bf16),
        compiler_params=_cp(1),
    )(cidx, part, got)


def _chip_exchange(sums):
    n = len(sums)

    def body(*refs):
        ins, outs = refs[:n], refs[n:2 * n]
        send_sems, recv_sems, local_sems = refs[2 * n:]
        x, y, c, chips = _place()
        q = 2 * x + y
        local, sent = [], []
        for a in range(n):
            lc = pltpu.make_async_copy(ins[a].at[q], outs[a].at[q], local_sems.at[a])
            lc.start()
            local.append(lc)
            for j, chip in enumerate(chips):
                cp = pltpu.make_async_remote_copy(
                    src_ref=ins[a].at[_chip_id(chip)], dst_ref=outs[a].at[q],
                    send_sem=send_sems.at[a, j], recv_sem=recv_sems.at[a, j],
                    device_id=(chip[0], chip[1], c), device_id_type=MESH)
                cp.start()
                sent.append(cp)
        for a in range(n):
            for j, chip in enumerate(chips):
                got = outs[a].at[_chip_id(chip)]
                pltpu.make_async_remote_copy(
                    src_ref=got, dst_ref=got, send_sem=send_sems.at[a, j], recv_sem=recv_sems.at[a, j],
                    device_id=(chip[0], chip[1], c), device_id_type=MESH).wait_recv()
        for cp in sent:
            cp.wait_send()
        for lc in local:
            lc.wait()

    return pl.pallas_call(
        body, name="chip_exchange",
        in_specs=[ANY] * n, out_specs=[ANY] * n,
        out_shape=[jax.ShapeDtypeStruct(s.shape, s.dtype) for s in sums],
        scratch_shapes=[pltpu.SemaphoreType.DMA((n, 3)), pltpu.SemaphoreType.DMA((n, 3)),
                        pltpu.SemaphoreType.DMA((n,))],
    )(*sums)


def _sum_chips(got, name):
    _, hr, cc = got.shape

    def body(g_ref, o_ref):
        acc = g_ref[0].astype(f32)
        for s in range(1, NCHIP):
            acc = acc + g_ref[s].astype(f32)
        o_ref[...] = acc

    return pl.pallas_call(
        body, name=name, grid=(1,),
        in_specs=[pl.BlockSpec((NCHIP, hr, cc), lambda i: (0, 0, 0))],
        out_specs=pl.BlockSpec((hr, cc), lambda i: (0, 0)),
        out_shape=jax.ShapeDtypeStruct((hr, cc), f32),
        compiler_params=_cp(1),
    )(got)


def _pair_share(halves):
    n = len(halves)

    def body(*refs):
        ins, outs = refs[:n], refs[n:2 * n]
        send_sems, recv_sems, local_sems = refs[2 * n:]
        x, y, c, _ = _place()
        local, sent = [], []
        for a in range(n):
            hr = halves[a].shape[0]
            mine = outs[a].at[pl.ds(c * hr, hr)]
            lc = pltpu.make_async_copy(ins[a], mine, local_sems.at[a])
            lc.start()
            local.append(lc)
            cp = pltpu.make_async_remote_copy(
                src_ref=ins[a], dst_ref=mine, send_sem=send_sems.at[a], recv_sem=recv_sems.at[a],
                device_id=(x, y, 1 - c), device_id_type=MESH)
            cp.start()
            sent.append(cp)
        for a in range(n):
            hr = halves[a].shape[0]
            theirs = outs[a].at[pl.ds((1 - c) * hr, hr)]
            pltpu.make_async_remote_copy(
                src_ref=ins[a], dst_ref=theirs, send_sem=send_sems.at[a], recv_sem=recv_sems.at[a],
                device_id=(x, y, 1 - c), device_id_type=MESH).wait_recv()
        for cp in sent:
            cp.wait_send()
        for lc in local:
            lc.wait()

    return pl.pallas_call(
        body, name="pair_share",
        in_specs=[ANY] * n, out_specs=[ANY] * n,
        out_shape=[jax.ShapeDtypeStruct((2 * s.shape[0], s.shape[1]), s.dtype) for s in halves],
        scratch_shapes=[pltpu.SemaphoreType.DMA((n,)), pltpu.SemaphoreType.DMA((n,)),
                        pltpu.SemaphoreType.DMA((n,))],
    )(*halves)


def _gather_all(pack):
    def body(in_ref, out_ref, send_sems, recv_sems, local_sem):
        x, y, c, _ = _place()
        me = 4 * x + 2 * y + c
        lc = pltpu.make_async_copy(in_ref, out_ref.at[me], local_sem)
        lc.start()
        sent = []

        def peer(k):
            return (1 - x if k & 4 else x, 1 - y if k & 2 else y, 1 - c if k & 1 else c)

        for k in range(1, 8):
            cp = pltpu.make_async_remote_copy(
                src_ref=in_ref, dst_ref=out_ref.at[me], send_sem=send_sems.at[k - 1],
                recv_sem=recv_sems.at[k - 1], device_id=peer(k), device_id_type=MESH)
            cp.start()
            sent.append(cp)
        for k in range(1, 8):
            px, py, pc = peer(k)
            got = out_ref.at[4 * px + 2 * py + pc]
            pltpu.make_async_remote_copy(
                src_ref=in_ref, dst_ref=got, send_sem=send_sems.at[k - 1], recv_sem=recv_sems.at[k - 1],
                device_id=(px, py, pc), device_id_type=MESH).wait_recv()
        for cp in sent:
            cp.wait_send()
        lc.wait()

    return pl.pallas_call(
        body, name="gather_all",
        in_specs=[ANY], out_specs=ANY,
        out_shape=jax.ShapeDtypeStruct((8,) + pack.shape, pack.dtype),
        scratch_shapes=[pltpu.SemaphoreType.DMA((7,)), pltpu.SemaphoreType.DMA((7,)),
                        pltpu.SemaphoreType.DMA(())],
    )(pack)


def _adamw_math(w, g, m, v):
    m = ADAM_B1 * m + (1.0 - ADAM_B1) * g
    v = ADAM_B2 * v + (1.0 - ADAM_B2) * (g * g)
    m_hat = m / (1.0 - ADAM_B1 ** ADAM_STEP)
    v_hat = v / (1.0 - ADAM_B2 ** ADAM_STEP)
    delta = -ADAM_LR * (m_hat / (jnp.sqrt(v_hat) + ADAM_EPS) + ADAM_WD * w)
    return delta, m, v


def _adamw(w, g, m, v, name):
    r, cc = w.shape
    tr = next(r // k for k in range(1, r + 1)
              if r % k == 0 and (r // k) % 8 == 0 and (r // k) * cc * 4 <= (1 << 20))

    def body(w_ref, g_ref, m_ref, v_ref, d_ref, mo_ref, vo_ref):
        d_ref[...], mo_ref[...], vo_ref[...] = _adamw_math(w_ref[...], g_ref[...], m_ref[...], v_ref[...])

    blk = pl.BlockSpec((tr, cc), lambda i: (i, 0))
    return pl.pallas_call(
        body, name=name, grid=(r // tr,),
        in_specs=[blk] * 4, out_specs=[blk] * 3,
        out_shape=[jax.ShapeDtypeStruct((r, cc), f32)] * 3,
        compiler_params=_cp(1),
    )(w, g, m, v)


REPL = [("ffn1_norm", 1), ("mix_norm", 1), ("b_in", 6), ("rnn_conv_b", 1), ("rg_b_a", 1), ("rg_b_x", 1),
        ("rg_lambda", 1), ("conv_dw_b", 1), ("conv_ln_g", 1), ("conv_ln_b", 1), ("conv_b_proj", 1),
        ("ffn2_norm", 1), ("final_norm", 1)]
COLSH = [("meta_tokens", NMETA), ("rnn_conv_w", KC4), ("conv_dw_w", KC31)]
SMALL = REPL + COLSH
CS = D // NCHIP


def _pack_rows():
    starts, row = {}, 0
    for k, rows in REPL:
        starts[k] = row
        row += rows
    for k, rows in COLSH:
        row = -(-row // 8) * 8
        starts[k] = row
        row += rows
    return starts, -(-row // 8) * 8


PACK_START, SMALL_ROWS = _pack_rows()


def _small_pack(g):
    pieces, row = [], 0
    for k, rows in SMALL:
        if PACK_START[k] > row:
            pieces.append(jnp.zeros((PACK_START[k] - row, D), f32))
        pieces.append(g[k].reshape(rows, D))
        row = PACK_START[k] + rows
    pieces.append(jnp.zeros((SMALL_ROWS - row, D), f32))
    return jnp.concatenate(pieces, axis=0)


def _adamw_small(packs, ws, ms, vs):
    ns = len(SMALL)

    def body(*refs):
        pack_ref = refs[0]
        w_refs, m_refs, v_refs = refs[1:1 + ns], refs[1 + ns:1 + 2 * ns], refs[1 + 2 * ns:1 + 3 * ns]
        outs = refs[1 + 3 * ns:1 + 7 * ns]
        g_refs, d_refs, mo_refs, vo_refs = outs[:ns], outs[ns:2 * ns], outs[2 * ns:3 * ns], outs[3 * ns:]
        gsum_sc = refs[1 + 7 * ns]
        q = 2 * lax.axis_index("x") + lax.axis_index("y")
        acc = pack_ref[0]
        for dev in range(1, 8):
            acc = acc + pack_ref[dev]
        gsum_sc[...] = acc
        for idx, (name, rows) in enumerate(SMALL):
            row = PACK_START[name]
            if idx < len(REPL):
                for k in range(rows):
                    cols = slice(k * D, (k + 1) * D)
                    g = gsum_sc[row + k:row + k + 1, :]
                    d, mm, vv = _adamw_math(w_refs[idx][:, cols], g, m_refs[idx][:, cols], v_refs[idx][:, cols])
                    g_refs[idx][:, cols] = g
                    d_refs[idx][:, cols] = d
                    mo_refs[idx][:, cols] = mm
                    vo_refs[idx][:, cols] = vv
            else:
                g = gsum_sc[row:row + rows, pl.ds(pl.multiple_of(q * CS, CS), CS)]
                d, mm, vv = _adamw_math(w_refs[idx][...], g, m_refs[idx][...], v_refs[idx][...])
                g_refs[idx][...] = g
                d_refs[idx][...] = d
                mo_refs[idx][...] = mm
                vo_refs[idx][...] = vv

    shapes = [jax.ShapeDtypeStruct(w.shape, f32) for w in ws]
    return pl.pallas_call(
        body, name="adamw_small",
        out_shape=shapes * 4,
        scratch_shapes=[pltpu.VMEM((SMALL_ROWS, D), f32)],
        compiler_params=pltpu.CompilerParams(vmem_limit_bytes=VMEM_LIMIT),
    )(packs, *ws, *ms, *vs)


BIG = ["ffn1_w_gu", "ffn1_w_down", "w_in", "rg_w_a", "rg_w_x", "rnn_w_proj", "conv_w_proj", "w_out",
       "ffn2_w_gu", "ffn2_w_down"]
WEIGHTS = ['meta_tokens', 'ffn1_norm', 'ffn1_w_gu', 'ffn1_w_down', 'mix_norm', 'w_in', 'b_in', 'rnn_conv_w',
           'rnn_conv_b', 'rg_w_a', 'rg_b_a', 'rg_w_x', 'rg_b_x', 'rg_lambda', 'rnn_w_proj', 'conv_dw_w',
           'conv_dw_b', 'conv_ln_g', 'conv_ln_b', 'conv_w_proj', 'conv_b_proj', 'w_out', 'ffn2_norm',
           'ffn2_w_gu', 'ffn2_w_down', 'final_norm']


def _as2d(a):
    return a.reshape(-1, a.shape[-1])


def _step(x, loss_target, w, m, v):
    seq = x.shape[1]
    n_valid = NMETA + seq
    t = -(-n_valid // TM) * TM

    small_rows = sum(r for _, r in COLSH)
    small = jnp.concatenate([_as2d(w[k]) for k, _ in COLSH] + [jnp.zeros((64 - small_rows, CS), f32)], axis=0)
    gathered = _gather_shards([_as2d(w[k]).astype(bf16) for k in BIG] + [small])
    p = dict(zip(BIG, gathered[:-1]))
    p["ffn1_w_down"] = p["ffn1_w_down"].reshape(F, D)
    p["ffn2_w_down"] = p["ffn2_w_down"].reshape(F, D)
    for k in ("rnn_w_proj", "conv_w_proj", "w_out"):
        p[k] = p[k].reshape(D, D)
    for k in ("rg_w_a", "rg_w_x"):
        p[k] = p[k].reshape(NCHIP, NHEAD, HD // NCHIP, HD).transpose(1, 0, 2, 3).reshape(NHEAD, HD, HD)
    small_full = gathered[-1].transpose(1, 0, 2).reshape(64, D)
    row = 0
    for k, rows in COLSH:
        p[k] = small_full[row:row + rows]
        row += rows
    for k, rows in REPL:
        p[k] = w[k].reshape(1, rows * D)

    h0 = jnp.concatenate([p["meta_tokens"], x[0], jnp.zeros((t - n_valid, D), f32)], axis=0)
    tgt = jnp.concatenate([jnp.zeros((NMETA, D), f32), loss_target[0], jnp.zeros((t - n_valid, D), f32)], axis=0)
    loss_blk, dh0, g = _local_step(h0, tgt, n_valid, p)
    g["meta_tokens"] = dh0[0:NMETA]
    grad_x = dh0[NMETA:n_valid][None]
    loss = lax.psum(loss_blk[0, 0], ("x", "y", "c"))

    cidx = lax.axis_index("c").astype(jnp.int32).reshape(1)
    parts = [g[k].reshape((NCHIP, -1, g[k].shape[-1])) for k in BIG]
    from_sibling = _pair_exchange(parts)
    chip_sums = [_pair_add(pp, gg, cidx, "pair_add_" + k) for pp, gg, k in zip(parts, from_sibling, BIG)]
    from_chips = _chip_exchange(chip_sums)
    halves = [_sum_chips(gg, "sum_chips_" + k) for gg, k in zip(from_chips, BIG)]
    full = dict(zip(BIG, _pair_share(halves)))

    packs = _gather_all(_small_pack(g))

    grads, deltas, new_m, new_v = {}, {}, {}, {}
    for k in BIG:
        d, mm, vv = _adamw(_as2d(w[k]), full[k], _as2d(m[k]), _as2d(v[k]), "adamw_" + k)
        grads[k], deltas[k], new_m[k], new_v[k] = (a.reshape(w[k].shape) for a in (full[k], d, mm, vv))
    names = [k for k, _ in SMALL]
    shape2 = {k: ((1, rows * D) if (k, rows) in REPL else (rows, CS)) for k, rows in SMALL}
    outs = _adamw_small(packs, *[[a[k].reshape(shape2[k]) for k in names] for a in (w, m, v)])
    ns = len(names)
    for i, k in enumerate(names):
        grads[k], deltas[k], new_m[k], new_v[k] = (outs[j * ns + i].reshape(w[k].shape) for j in range(4))

    return (loss, grad_x, *[grads[k] for k in WEIGHTS], *[deltas[k] for k in WEIGHTS],
            *[new_m[k] for k in WEIGHTS], *[new_v[k] for k in WEIGHTS])


def kernel(x, meta_tokens, ffn1_norm, ffn1_w_gu, ffn1_w_down, mix_norm, w_in, b_in, rnn_conv_w, rnn_conv_b, rg_w_a, rg_b_a, rg_w_x, rg_b_x, rg_lambda, rnn_w_proj, conv_dw_w, conv_dw_b, conv_ln_g, conv_ln_b, conv_w_proj, conv_b_proj, w_out, ffn2_norm, ffn2_w_gu, ffn2_w_down, final_norm, loss_target, m_meta_tokens, m_ffn1_norm, m_ffn1_w_gu, m_ffn1_w_down, m_mix_norm, m_w_in, m_b_in, m_rnn_conv_w, m_rnn_conv_b, m_rg_w_a, m_rg_b_a, m_rg_w_x, m_rg_b_x, m_rg_lambda, m_rnn_w_proj, m_conv_dw_w, m_conv_dw_b, m_conv_ln_g, m_conv_ln_b, m_conv_w_proj, m_conv_b_proj, m_w_out, m_ffn2_norm, m_ffn2_w_gu, m_ffn2_w_down, m_final_norm, v_meta_tokens, v_ffn1_norm, v_ffn1_w_gu, v_ffn1_w_down, v_mix_norm, v_w_in, v_b_in, v_rnn_conv_w, v_rnn_conv_b, v_rg_w_a, v_rg_b_a, v_rg_w_x, v_rg_b_x, v_rg_lambda, v_rnn_w_proj, v_conv_dw_w, v_conv_dw_b, v_conv_ln_g, v_conv_ln_b, v_conv_w_proj, v_conv_b_proj, v_w_out, v_ffn2_norm, v_ffn2_w_gu, v_ffn2_w_down, v_final_norm):
    args = locals()
    w = {k: args[k] for k in WEIGHTS}
    m = {k: args["m_" + k] for k in WEIGHTS}
    v = {k: args["v_" + k] for k in WEIGHTS}
    return _step(x, loss_target, w, m, v)
```

```python
import functools

import jax
import jax.numpy as jnp
from jax import lax
from jax.experimental import pallas as pl
from jax.experimental.pallas import tpu as pltpu

f32 = jnp.float32
bf16 = jnp.bfloat16

D = 1024
F = 2816
FS = F // 2
NIN = 6 * D
NMETA = 16
NHEAD = 4
HD = D // NHEAD
KC4 = 4
KC31 = 31
HALO = 32
EPS = 1e-6
TM = 384
NCHIP = 4
MESH = pl.DeviceIdType.MESH

ADAM_LR = 0.001
ADAM_B1 = 0.9
ADAM_B2 = 0.999
ADAM_EPS = 1e-08
ADAM_WD = 0.01
ADAM_STEP = 10

VMEM_LIMIT = 56 * 1024 * 1024


def _cp(n_axes, **kw):
    return pltpu.CompilerParams(dimension_semantics=("arbitrary",) * n_axes,
                                vmem_limit_bytes=VMEM_LIMIT, **kw)


def _nt_dot(a, b):
    return lax.dot_general(a, b, (((1,), (1,)), ((), ())), preferred_element_type=f32)


def _tn_dot(a, b):
    return lax.dot_general(a, b, (((0,), (0,)), ((), ())), preferred_element_type=f32)


def _sigmoid(x):
    return 1.0 / (1.0 + jnp.exp(-x))


def _log1p(y):
    u = 1.0 + y
    d = u - 1.0
    return jnp.where(d == 0.0, y, jnp.log(u) * (y / jnp.where(d == 0.0, 1.0, d)))


def _softplus(x):
    return jnp.maximum(x, 0.0) + _log1p(jnp.exp(-jnp.abs(x)))


def _expm1(x):
    series = x * (1.0 + x * (0.5 + x * (1.0 / 6.0 + x * (1.0 / 24.0 + x * (1.0 / 120.0)))))
    return jnp.where(jnp.abs(x) < 0.1, series, jnp.exp(x) - 1.0)


_GELU_C = 0.7978845608028654
_GELU_K = 0.044715


def _gelu_and_grad(y):
    y2 = y * y
    th = jnp.tanh(_GELU_C * (y + _GELU_K * y * y2))
    gel = 0.5 * y * (1.0 + th)
    dgel = 0.5 * (1.0 + th) + 0.5 * y * (1.0 - th * th) * _GELU_C * (1.0 + 3.0 * _GELU_K * y2)
    return gel, dgel


def _rms_stats(h):
    return lax.rsqrt(jnp.mean(h * h, axis=-1, keepdims=True) + EPS)


def _rms_bwd(dn, h, g):
    r = _rms_stats(h)
    nhat = h * r
    dnh = dn * g
    dh = r * (dnh - nhat * jnp.mean(dnh * nhat, axis=-1, keepdims=True))
    dg = jnp.sum(dn * nhat, axis=0, keepdims=True)
    return dh, dg


def _row_ids(shape):
    return lax.broadcasted_iota(jnp.int32, shape, 0)


def _ffn_fwd(h, g, wgu, wd, name):
    t = h.shape[0]
    nj = 2

    def body(h_ref, g_ref, wg_ref, wu_ref, wd_ref, ho_ref, gate_ref, up_ref, n_ref, nb_sc, acc_sc):
        j = pl.program_id(1)

        @pl.when(j == 0)
        def _():
            hh = h_ref[...]
            nb = (hh * _rms_stats(hh) * g_ref[...]).astype(bf16)
            nb_sc[...] = nb
            n_ref[...] = nb
            acc_sc[...] = jnp.zeros_like(acc_sc)

        nb = nb_sc[...]
        gt = jnp.dot(nb, wg_ref[...], preferred_element_type=f32)
        up = jnp.dot(nb, wu_ref[...], preferred_element_type=f32)
        gate_ref[...] = gt.astype(bf16)
        up_ref[...] = up.astype(bf16)
        a = (gt * _sigmoid(gt) * up).astype(bf16)
        acc_sc[...] += jnp.dot(a, wd_ref[...], preferred_element_type=f32)

        @pl.when(j == nj - 1)
        def _():
            ho_ref[...] = h_ref[...] + 0.5 * acc_sc[...]

    return pl.pallas_call(
        body, name=name, grid=(t // TM, nj),
        in_specs=[
            pl.BlockSpec((TM, D), lambda i, j: (i, 0)),
            pl.BlockSpec((1, D), lambda i, j: (0, 0)),
            pl.BlockSpec((None, D, FS), lambda i, j: (j, 0, 0)),
            pl.BlockSpec((None, D, FS), lambda i, j: (2 + j, 0, 0)),
            pl.BlockSpec((FS, D), lambda i, j: (j, 0)),
        ],
        out_specs=[
            pl.BlockSpec((TM, D), lambda i, j: (i, 0)),
            pl.BlockSpec((TM, FS), lambda i, j: (i, j)),
            pl.BlockSpec((TM, FS), lambda i, j: (i, j)),
            pl.BlockSpec((TM, D), lambda i, j: (i, 0)),
        ],
        out_shape=[
            jax.ShapeDtypeStruct((t, D), f32),
            jax.ShapeDtypeStruct((t, F), bf16),
            jax.ShapeDtypeStruct((t, F), bf16),
            jax.ShapeDtypeStruct((t, D), bf16),
        ],
        scratch_shapes=[pltpu.VMEM((TM, D), bf16), pltpu.VMEM((TM, D), f32)],
        compiler_params=_cp(2),
    )(h, g, wgu, wgu, wd)


def _inproj_fwd(h, g, win, b_in):
    t = h.shape[0]
    tn = NIN // NCHIP
    nj = NIN // tn
    per = (NIN // NCHIP) // tn

    def body(h_ref, g_ref, w_ref, b_ref, proj_ref, n_ref, nb_sc):
        j = pl.program_id(1)

        @pl.when(j == 0)
        def _():
            hh = h_ref[...]
            nb = (hh * _rms_stats(hh) * g_ref[...]).astype(bf16)
            nb_sc[...] = nb
            n_ref[...] = nb

        proj_ref[...] = jnp.dot(nb_sc[...], w_ref[...], preferred_element_type=f32) + b_ref[...]

    return pl.pallas_call(
        body, name="inproj_fwd", grid=(t // TM, nj),
        in_specs=[
            pl.BlockSpec((TM, D), lambda i, j: (i, 0)),
            pl.BlockSpec((1, D), lambda i, j: (0, 0)),
            pl.BlockSpec((None, D, tn), lambda i, j: (j // per, 0, j % per)),
            pl.BlockSpec((1, tn), lambda i, j: (0, j)),
        ],
        out_specs=[
            pl.BlockSpec((TM, tn), lambda i, j: (i, j)),
            pl.BlockSpec((TM, D), lambda i, j: (i, 0)),
        ],
        out_shape=[jax.ShapeDtypeStruct((t, NIN), f32), jax.ShapeDtypeStruct((t, D), bf16)],
        scratch_shapes=[pltpu.VMEM((TM, D), bf16)],
        compiler_params=_cp(2),
    )(h, g, win, b_in)


def _block_gates(xr, wa_ref, ba, wx_ref, bx, lam):
    xrb = xr.astype(bf16)
    pa = jnp.concatenate([jnp.dot(xrb[:, hh * HD:(hh + 1) * HD], wa_ref[hh], preferred_element_type=f32)
                          for hh in range(NHEAD)], axis=1)
    px = jnp.concatenate([jnp.dot(xrb[:, hh * HD:(hh + 1) * HD], wx_ref[hh], preferred_element_type=f32)
                          for hh in range(NHEAD)], axis=1)
    ra = _sigmoid(pa + ba)
    ii = _sigmoid(px + bx)
    sp = _softplus(-lam)
    log_a = -8.0 * ra * sp
    a = jnp.exp(log_a)
    sq = jnp.sqrt(-_expm1(2.0 * log_a))
    return ra, ii, a, sq, sp


def _rnn_fwd(proj, cw, cb, wa, ba, wx, bx, lam):
    t = proj.shape[0]
    ng = TM // 8

    def body(x_ref, y_ref, cw_ref, cb_ref, wa_ref, ba_ref, wx_ref, bx_ref, lam_ref,
             xr_ref, hr_ref, z_ref, xext_sc, carry_sc, a_sc, h_sc):
        i = pl.program_id(0)

        @pl.when(i == 0)
        def _():
            xext_sc[0:8, :] = jnp.zeros((8, D), f32)
            carry_sc[...] = jnp.zeros_like(carry_sc)

        x = x_ref[...]
        xext_sc[8:8 + TM, :] = x
        xe = xext_sc[...]
        xr = cb_ref[...] + cw_ref[KC4 - 1:KC4, :] * x
        for k in range(KC4 - 1):
            xr = xr + cw_ref[k:k + 1, :] * pltpu.roll(xe, KC4 - 1 - k, 0)[8:8 + TM]
        xext_sc[0:8, :] = x[TM - 8:TM]

        _, ii, a, sq, _ = _block_gates(xr, wa_ref, ba_ref[...], wx_ref, bx_ref[...], lam_ref[...])
        a_sc[...] = a
        h_sc[...] = sq * ii * xr
        row = _row_ids((8, D))

        def group(r, carry):
            off = pl.multiple_of(r * 8, 8)
            aa = a_sc[pl.ds(off, 8), :]
            hh = h_sc[pl.ds(off, 8), :]
            for s in (1, 2, 4):
                a_sh = jnp.where(row >= s, pltpu.roll(aa, s, 0), 1.0)
                h_sh = jnp.where(row >= s, pltpu.roll(hh, s, 0), 0.0)
                hh = aa * h_sh + hh
                aa = aa * a_sh
            hh = hh + aa * carry
            h_sc[pl.ds(off, 8), :] = hh
            return hh[7:8, :]

        carry_sc[...] = lax.fori_loop(0, ng, group, carry_sc[...])
        hr = h_sc[...]
        gel, _ = _gelu_and_grad(y_ref[...])
        xr_ref[...] = xr
        hr_ref[...] = hr
        z_ref[...] = (hr * gel).astype(bf16)

    vec = pl.BlockSpec((1, D), lambda i: (0, 0))
    return pl.pallas_call(
        body, name="rnn_fwd", grid=(t // TM,),
        in_specs=[
            pl.BlockSpec((TM, D), lambda i: (i, 0)),
            pl.BlockSpec((TM, D), lambda i: (i, 1)),
            pl.BlockSpec((KC4, D), lambda i: (0, 0)),
            vec,
            pl.BlockSpec((NHEAD, HD, HD), lambda i: (0, 0, 0)),
            vec,
            pl.BlockSpec((NHEAD, HD, HD), lambda i: (0, 0, 0)),
            vec, vec,
        ],
        out_specs=[pl.BlockSpec((TM, D), lambda i: (i, 0))] * 3,
        out_shape=[jax.ShapeDtypeStruct((t, D), f32), jax.ShapeDtypeStruct((t, D), f32),
                   jax.ShapeDtypeStruct((t, D), bf16)],
        scratch_shapes=[pltpu.VMEM((TM + 8, D), f32), pltpu.VMEM((1, D), f32),
                        pltpu.VMEM((TM, D), f32), pltpu.VMEM((TM, D), f32)],
        compiler_params=_cp(1),
    )(proj, proj, cw, cb, wa, ba, wx, bx, lam)


def _ln_stats(vc):
    mu = jnp.mean(vc, axis=-1, keepdims=True)
    xc = vc - mu
    rstd = lax.rsqrt(jnp.mean(xc * xc, axis=-1, keepdims=True) + EPS)
    return xc * rstd, rstd


def _conv_fwd(proj, w31, b31, ln_g, ln_b):
    t = proj.shape[0]

    def body(gv_ref, gg_ref, w_ref, b_ref, lg_ref, lb_ref, vc_ref, s_ref, vext_sc):
        i = pl.program_id(0)

        @pl.when(i == 0)
        def _():
            vext_sc[0:HALO, :] = jnp.zeros((HALO, D), f32)

        v = gv_ref[...] * _sigmoid(gg_ref[...])
        vext_sc[HALO:HALO + TM, :] = v
        ve = vext_sc[...]
        acc = jnp.zeros((TM, D), f32) + b_ref[...]
        for s in range(8):
            vs = ve if s == 0 else pltpu.roll(ve, s, 0)
            for m in range(HALO // 8):
                k = KC31 - 1 - (8 * m + s)
                if 0 <= k < KC31:
                    acc = acc + w_ref[k:k + 1, :] * vs[HALO - 8 * m:HALO - 8 * m + TM]
        vext_sc[0:HALO, :] = v[TM - HALO:TM]
        xhat, _ = _ln_stats(acc)
        ln = xhat * lg_ref[...] + lb_ref[...]
        vc_ref[...] = acc
        s_ref[...] = (ln * _sigmoid(ln)).astype(bf16)

    vec = pl.BlockSpec((1, D), lambda i: (0, 0))
    return pl.pallas_call(
        body, name="conv_fwd", grid=(t // TM,),
        in_specs=[
            pl.BlockSpec((TM, D), lambda i: (i, 2)),
            pl.BlockSpec((TM, D), lambda i: (i, 3)),
            pl.BlockSpec((KC31, D), lambda i: (0, 0)),
            vec, vec, vec,
        ],
        out_specs=[pl.BlockSpec((TM, D), lambda i: (i, 0))] * 2,
        out_shape=[jax.ShapeDtypeStruct((t, D), f32), jax.ShapeDtypeStruct((t, D), bf16)],
        scratch_shapes=[pltpu.VMEM((TM + HALO, D), f32)],
        compiler_params=_cp(1),
    )(proj, proj, w31, b31, ln_g, ln_b)


def _merge_fwd(h, z, s, proj, wrp, wcp, bcp, wout):
    t = h.shape[0]

    def body(h_ref, z_ref, s_ref, ga_ref, gb_ref, wrp_ref, wcp_ref, bcp_ref, wout_ref, ho_ref):
        ya = jnp.dot(z_ref[...], wrp_ref[...], preferred_element_type=f32)
        yb = jnp.dot(s_ref[...], wcp_ref[...], preferred_element_type=f32) + bcp_ref[...]
        merged = _sigmoid(ga_ref[...]) * ya + _sigmoid(gb_ref[...]) * yb
        ho_ref[...] = h_ref[...] + jnp.dot(merged.astype(bf16), wout_ref[...], preferred_element_type=f32)

    row = pl.BlockSpec((TM, D), lambda i: (i, 0))
    wsq = pl.BlockSpec((D, D), lambda i: (0, 0))
    return pl.pallas_call(
        body, name="merge_fwd", grid=(t // TM,),
        in_specs=[row, row, row,
                  pl.BlockSpec((TM, D), lambda i: (i, 4)),
                  pl.BlockSpec((TM, D), lambda i: (i, 5)),
                  wsq, wsq, pl.BlockSpec((1, D), lambda i: (0, 0)), wsq],
        out_specs=row,
        out_shape=jax.ShapeDtypeStruct((t, D), f32),
        compiler_params=_cp(1),
    )(h, z, s, proj, proj, wrp, wcp, bcp, wout)


def _final_loss(h, g, tgt, n_valid):
    t = h.shape[0]

    def body(h_ref, g_ref, t_ref, dh_ref, loss_ref, dg_ref):
        i = pl.program_id(0)

        @pl.when(i == 0)
        def _():
            loss_ref[...] = jnp.zeros_like(loss_ref)
            dg_ref[...] = jnp.zeros_like(dg_ref)

        hh = h_ref[...]
        gg = g_ref[...]
        row = i * TM + _row_ids((TM, 1))
        valid = jnp.logical_and(row >= NMETA, row < n_valid)
        out = hh * _rms_stats(hh) * gg
        err = jnp.where(valid, out - t_ref[...], 0.0)
        loss_ref[...] += 0.5 * jnp.sum(err * err) * (1.0 / D)
        dh, dg = _rms_bwd(err * (1.0 / D), hh, gg)
        dh_ref[...] = dh
        dg_ref[...] += dg

    row_spec = pl.BlockSpec((TM, D), lambda i: (i, 0))
    return pl.pallas_call(
        body, name="final_loss", grid=(t // TM,),
        in_specs=[row_spec, pl.BlockSpec((1, D), lambda i: (0, 0)), row_spec],
        out_specs=[row_spec, pl.BlockSpec((8, 128), lambda i: (0, 0)), pl.BlockSpec((1, D), lambda i: (0, 0))],
        out_shape=[jax.ShapeDtypeStruct((t, D), f32), jax.ShapeDtypeStruct((8, 128), f32),
                   jax.ShapeDtypeStruct((1, D), f32)],
        compiler_params=_cp(1),
    )(h, g, tgt)


def _ffn_bwd(dh, h, g, gate, up, wgu, wd, name):
    t = h.shape[0]
    nj = 2

    def body(dh_ref, h_ref, g_ref, gate_ref, up_ref, wg_ref, wu_ref, wd_ref,
             dhi_ref, dgate_ref, dup_ref, a_ref, df_ref, dg_ref, dfb_sc, dn_sc):
        i = pl.program_id(0)
        j = pl.program_id(1)

        @pl.when(jnp.logical_and(i == 0, j == 0))
        def _():
            dg_ref[...] = jnp.zeros_like(dg_ref)

        @pl.when(j == 0)
        def _():
            dfb = (0.5 * dh_ref[...]).astype(bf16)
            dfb_sc[...] = dfb
            df_ref[...] = dfb
            dn_sc[...] = jnp.zeros_like(dn_sc)

        da = _nt_dot(dfb_sc[...], wd_ref[...])
        gt = gate_ref[...].astype(f32)
        uu = up_ref[...].astype(f32)
        sg = _sigmoid(gt)
        silu = gt * sg
        a_ref[...] = (silu * uu).astype(bf16)
        dgt = (da * uu * (sg * (1.0 + gt * (1.0 - sg)))).astype(bf16)
        dup = (da * silu).astype(bf16)
        dgate_ref[...] = dgt
        dup_ref[...] = dup
        dn_sc[...] += _nt_dot(dgt, wg_ref[...]) + _nt_dot(dup, wu_ref[...])

        @pl.when(j == nj - 1)
        def _():
            dhin, dg = _rms_bwd(dn_sc[...], h_ref[...], g_ref[...])
            dhi_ref[...] = dh_ref[...] + dhin
            dg_ref[...] += dg

    rowd = pl.BlockSpec((TM, D), lambda i, j: (i, 0))
    rowf = pl.BlockSpec((TM, FS), lambda i, j: (i, j))
    vec = pl.BlockSpec((1, D), lambda i, j: (0, 0))
    return pl.pallas_call(
        body, name=name, grid=(t // TM, nj),
        in_specs=[rowd, rowd, vec, rowf, rowf,
                  pl.BlockSpec((None, D, FS), lambda i, j: (j, 0, 0)),
                  pl.BlockSpec((None, D, FS), lambda i, j: (2 + j, 0, 0)),
                  pl.BlockSpec((FS, D), lambda i, j: (j, 0))],
        out_specs=[rowd, rowf, rowf, rowf, rowd, vec],
        out_shape=[jax.ShapeDtypeStruct((t, D), f32), jax.ShapeDtypeStruct((t, F), bf16),
                   jax.ShapeDtypeStruct((t, F), bf16), jax.ShapeDtypeStruct((t, F), bf16),
                   jax.ShapeDtypeStruct((t, D), bf16), jax.ShapeDtypeStruct((1, D), f32)],
        scratch_shapes=[pltpu.VMEM((TM, D), bf16), pltpu.VMEM((TM, D), f32)],
        compiler_params=_cp(2),
    )(dh, h, g, gate, up, wgu, wgu, wd)


def _big_tile(t):
    for cand in (2112, 1408, 768, 384):
        if t % cand == 0:
            return cand
    raise ValueError(t)


def _tn_matmul(a, b, tk, tn, out_shape, out_block, out_map, name):
    t, kk = a.shape
    _, nn = b.shape
    tmm = _big_tile(t)
    nm = t // tmm

    def body(a_ref, b_ref, o_ref, acc_sc):
        m = pl.program_id(2)

        @pl.when(m == 0)
        def _():
            acc_sc[...] = jnp.zeros_like(acc_sc)

        acc_sc[...] += _tn_dot(a_ref[...], b_ref[...])

        @pl.when(m == nm - 1)
        def _():
            o_ref[...] = acc_sc[...].astype(o_ref.dtype)

    return pl.pallas_call(
        body, name=name, grid=(kk // tk, nn // tn, nm),
        in_specs=[pl.BlockSpec((tmm, tk), lambda k, n, m: (m, k)),
                  pl.BlockSpec((tmm, tn), lambda k, n, m: (m, n))],
        out_specs=pl.BlockSpec(out_block, out_map),
        out_shape=jax.ShapeDtypeStruct(out_shape, bf16),
        scratch_shapes=[pltpu.VMEM((tk, tn), f32)],
        compiler_params=_cp(3),
    )(a, b)


def _merge_bwd(dh, z, s, proj, wrp, wcp, bcp, wout):
    t = dh.shape[0]

    def body(dh_ref, z_ref, s_ref, ga_ref, gb_ref, wrp_ref, wcp_ref, bcp_ref, wout_ref,
             dz_ref, ds_ref, dgab_ref, dhb_ref, mg_ref, dya_ref, dyb_ref, dbcp_ref):
        i = pl.program_id(0)

        @pl.when(i == 0)
        def _():
            dbcp_ref[...] = jnp.zeros_like(dbcp_ref)

        dhb = dh_ref[...].astype(bf16)
        dhb_ref[...] = dhb
        dmg = _nt_dot(dhb, wout_ref[...])
        ya = jnp.dot(z_ref[...], wrp_ref[...], preferred_element_type=f32)
        yb = jnp.dot(s_ref[...], wcp_ref[...], preferred_element_type=f32) + bcp_ref[...]
        sa = _sigmoid(ga_ref[...])
        sb = _sigmoid(gb_ref[...])
        mg_ref[...] = (sa * ya + sb * yb).astype(bf16)
        dgab_ref[:, 0:D] = (dmg * ya * sa * (1.0 - sa)).astype(bf16)
        dgab_ref[:, D:2 * D] = (dmg * yb * sb * (1.0 - sb)).astype(bf16)
        dya = dmg * sa
        dyb = dmg * sb
        dbcp_ref[...] += jnp.sum(dyb, axis=0, keepdims=True)
        dyab = dya.astype(bf16)
        dybb = dyb.astype(bf16)
        dya_ref[...] = dyab
        dyb_ref[...] = dybb
        dz_ref[...] = _nt_dot(dyab, wrp_ref[...])
        ds_ref[...] = _nt_dot(dybb, wcp_ref[...])

    row = pl.BlockSpec((TM, D), lambda i: (i, 0))
    wsq = pl.BlockSpec((D, D), lambda i: (0, 0))
    vec = pl.BlockSpec((1, D), lambda i: (0, 0))
    rowb = jax.ShapeDtypeStruct((t, D), bf16)
    return pl.pallas_call(
        body, name="merge_bwd", grid=(t // TM,),
        in_specs=[row, row, row,
                  pl.BlockSpec((TM, D), lambda i: (i, 4)),
                  pl.BlockSpec((TM, D), lambda i: (i, 5)),
                  wsq, wsq, vec, wsq],
        out_specs=[row, row,
                   pl.BlockSpec((TM, 2 * D), lambda i: (i, 2)),
                   row, row, row, row, vec],
        out_shape=[jax.ShapeDtypeStruct((t, D), f32), jax.ShapeDtypeStruct((t, D), f32),
                   jax.ShapeDtypeStruct((t, NIN), bf16),
                   rowb, rowb, rowb, rowb, jax.ShapeDtypeStruct((1, D), f32)],
        compiler_params=_cp(1),
    )(dh, z, s, proj, proj, wrp, wcp, bcp, wout)


def _conv_bwd(ds, vc, proj, dproj, w31, ln_g, ln_b):
    t = ds.shape[0]
    nt = t // TM
    hb = TM // HALO

    def body(ds_ref, vc_ref, gv_ref, gg_ref, gvp_ref, ggp_ref, dpin_ref, w_ref, lg_ref, lb_ref,
             dgvg_ref, dw_ref, db_ref, dlg_ref, dlb_ref, dext_sc, vext_sc, dwacc_sc):
        del dpin_ref
        i = pl.program_id(0)
        tile = nt - 1 - i

        @pl.when(i == 0)
        def _():
            dext_sc[TM:TM + HALO, :] = jnp.zeros((HALO, D), f32)
            dwacc_sc[...] = jnp.zeros_like(dwacc_sc)
            db_ref[...] = jnp.zeros_like(db_ref)
            dlg_ref[...] = jnp.zeros_like(dlg_ref)
            dlb_ref[...] = jnp.zeros_like(dlb_ref)

        vc = vc_ref[...]
        xhat, rstd = _ln_stats(vc)
        lg = lg_ref[...]
        ln = xhat * lg + lb_ref[...]
        sg = _sigmoid(ln)
        dln = ds_ref[...] * (sg * (1.0 + ln * (1.0 - sg)))
        dlg_ref[...] += jnp.sum(dln * xhat, axis=0, keepdims=True)
        dlb_ref[...] += jnp.sum(dln, axis=0, keepdims=True)
        dxh = dln * lg
        dvc = rstd * (dxh - jnp.mean(dxh, axis=-1, keepdims=True)
                      - xhat * jnp.mean(dxh * xhat, axis=-1, keepdims=True))
        db_ref[...] += jnp.sum(dvc, axis=0, keepdims=True)

        dext_sc[0:TM, :] = dvc
        de = dext_sc[...]
        dv = jnp.zeros((TM, D), f32)
        for s in range(8):
            dsh = de if s == 0 else pltpu.roll(de, TM + HALO - s, 0)
            for m in range(HALO // 8):
                k = KC31 - 1 - (8 * m + s)
                if 0 <= k < KC31:
                    dv = dv + w_ref[k:k + 1, :] * dsh[8 * m:8 * m + TM]
        dext_sc[TM:TM + HALO, :] = dvc[0:HALO]

        gv = gv_ref[...]
        sgg = _sigmoid(gg_ref[...])
        dgvg_ref[:, 0:D] = (dv * sgg).astype(bf16)
        dgvg_ref[:, D:2 * D] = (dv * gv * sgg * (1.0 - sgg)).astype(bf16)

        vprev = gvp_ref[...] * _sigmoid(ggp_ref[...])
        vext_sc[0:HALO, :] = jnp.where(tile > 0, vprev, 0.0)
        vext_sc[HALO:HALO + TM, :] = gv * sgg
        ve = vext_sc[...]
        for s in range(8):
            vs = ve if s == 0 else pltpu.roll(ve, s, 0)
            for m in range(HALO // 8):
                k = KC31 - 1 - (8 * m + s)
                if 0 <= k < KC31:
                    prod = dvc * vs[HALO - 8 * m:HALO - 8 * m + TM]
                    dwacc_sc[k] += jnp.sum(prod.reshape(TM // 8, 8, D), axis=0)

        @pl.when(i == nt - 1)
        def _():
            for k in range(KC31):
                dw_ref[k:k + 1, :] = jnp.sum(dwacc_sc[k], axis=0, keepdims=True)

    rev = lambda i: (nt - 1 - i, 0)
    vec = pl.BlockSpec((1, D), lambda i: (0, 0))
    halo_row = lambda i: jnp.maximum((nt - 1 - i) * hb - 1, 0)
    return pl.pallas_call(
        body, name="conv_bwd", grid=(nt,),
        in_specs=[
            pl.BlockSpec((TM, D), rev),
            pl.BlockSpec((TM, D), rev),
            pl.BlockSpec((TM, D), lambda i: (nt - 1 - i, 2)),
            pl.BlockSpec((TM, D), lambda i: (nt - 1 - i, 3)),
            pl.BlockSpec((HALO, D), lambda i: (halo_row(i), 2)),
            pl.BlockSpec((HALO, D), lambda i: (halo_row(i), 3)),
            pl.BlockSpec(memory_space=pl.ANY),
            pl.BlockSpec((KC31, D), lambda i: (0, 0)),
            vec, vec,
        ],
        out_specs=[
            pl.BlockSpec((TM, 2 * D), lambda i: (nt - 1 - i, 1)),
            pl.BlockSpec((KC31, D), lambda i: (0, 0)),
            vec, vec, vec,
        ],
        out_shape=[jax.ShapeDtypeStruct((t, NIN), bf16),
                   jax.ShapeDtypeStruct((KC31, D), f32),
                   jax.ShapeDtypeStruct((1, D), f32), jax.ShapeDtypeStruct((1, D), f32),
                   jax.ShapeDtypeStruct((1, D), f32)],
        scratch_shapes=[pltpu.VMEM((TM + HALO, D), f32), pltpu.VMEM((TM + HALO, D), f32),
                        pltpu.VMEM((KC31, 8, D), f32)],
        input_output_aliases={6: 0},
        compiler_params=_cp(1),
    )(ds, vc, proj, proj, proj, proj, dproj, w31, ln_g, ln_b)


def _rnn_bwd(dz, xr, hr, proj, dproj, cw, wa, ba, wx, bx, lam):
    t = dz.shape[0]
    nt = t // TM
    ng = TM // 8
    hq = HD // NCHIP

    def body(dz_ref, xr_ref, hr_ref, hrp_ref, x_ref, xp_ref, y_ref, dpin_ref,
             cw_ref, wa_ref, ba_ref, wx_ref, bx_ref, lam_ref,
             dxy_ref, dwa_ref, dwx_ref, dcw_ref, dcb_ref, dba_ref, dbx_ref, dlam_ref,
             anext_sc, gcarry_sc, dext_sc, xext_sc, m_sc, g_sc, dwa_sc, dwx_sc, dsp_sc):
        del dpin_ref
        i = pl.program_id(0)
        tile = nt - 1 - i

        @pl.when(i == 0)
        def _():
            anext_sc[...] = jnp.zeros_like(anext_sc)
            gcarry_sc[...] = jnp.zeros_like(gcarry_sc)
            dext_sc[TM:TM + 8, :] = jnp.zeros((8, D), f32)
            dwa_sc[...] = jnp.zeros_like(dwa_sc)
            dwx_sc[...] = jnp.zeros_like(dwx_sc)
            dsp_sc[...] = jnp.zeros_like(dsp_sc)
            dcw_ref[...] = jnp.zeros_like(dcw_ref)
            dcb_ref[...] = jnp.zeros_like(dcb_ref)
            dba_ref[...] = jnp.zeros_like(dba_ref)
            dbx_ref[...] = jnp.zeros_like(dbx_ref)

        xr = xr_ref[...]
        hr = hr_ref[...]
        dz = dz_ref[...]
        gel, dgel = _gelu_and_grad(y_ref[...])
        dxy_ref[:, D:2 * D] = (dz * hr * dgel).astype(bf16)
        ra, ii, a, sq, sp = _block_gates(xr, wa_ref, ba_ref[...], wx_ref, bx_ref[...], lam_ref[...])

        row = _row_ids((TM, D))
        m_sc[...] = jnp.where(row == TM - 1, anext_sc[...], pltpu.roll(a, TM - 1, 0))
        anext_sc[...] = a[0:1, :]
        g_sc[...] = dz * gel
        row8 = _row_ids((8, D))

        def group(qq, carry):
            off = pl.multiple_of((ng - 1 - qq) * 8, 8)
            mm = m_sc[pl.ds(off, 8), :]
            dd = g_sc[pl.ds(off, 8), :]
            for s in (1, 2, 4):
                m_sh = jnp.where(row8 < 8 - s, pltpu.roll(mm, 8 - s, 0), 1.0)
                d_sh = jnp.where(row8 < 8 - s, pltpu.roll(dd, 8 - s, 0), 0.0)
                dd = dd + mm * d_sh
                mm = mm * m_sh
            dd = dd + mm * carry
            g_sc[pl.ds(off, 8), :] = dd
            return dd[0:1, :]

        gcarry_sc[...] = lax.fori_loop(0, ng, group, gcarry_sc[...])
        gg = g_sc[...]

        hlast = jnp.where(tile > 0, hrp_ref[7:8, :], 0.0)
        hprev = jnp.where(row == 0, hlast, pltpu.roll(hr, 1, 0))
        d_a = gg * hprev
        dsq = gg * ii * xr
        dii = gg * sq * xr
        dxr = gg * sq * ii
        dlog = d_a * a - dsq * (a * a / sq)
        dsp_sc[...] += jnp.sum(dlog * (-8.0 * ra), axis=0, keepdims=True)
        dpa = dlog * (-8.0 * sp) * ra * (1.0 - ra)
        dpx = dii * ii * (1.0 - ii)
        dba_ref[...] += jnp.sum(dpa, axis=0, keepdims=True)
        dbx_ref[...] += jnp.sum(dpx, axis=0, keepdims=True)
        dpab = dpa.astype(bf16)
        dpxb = dpx.astype(bf16)
        xrb = xr.astype(bf16)
        back = []
        for hh in range(NHEAD):
            cols = slice(hh * HD, (hh + 1) * HD)
            back.append(_nt_dot(dpab[:, cols], wa_ref[hh]) + _nt_dot(dpxb[:, cols], wx_ref[hh]))
            dwa_sc[hh] += _tn_dot(xrb[:, cols], dpab[:, cols])
            dwx_sc[hh] += _tn_dot(xrb[:, cols], dpxb[:, cols])
        dxr = dxr + jnp.concatenate(back, axis=1)

        dext_sc[0:TM, :] = dxr
        de = dext_sc[...]
        dx = cw_ref[KC4 - 1:KC4, :] * dxr
        for k in range(KC4 - 1):
            dx = dx + cw_ref[k:k + 1, :] * pltpu.roll(de, TM + 8 - (KC4 - 1 - k), 0)[0:TM]
        dext_sc[TM:TM + 8, :] = dxr[0:8]
        dxy_ref[:, 0:D] = dx.astype(bf16)

        x = x_ref[...]
        xext_sc[0:8, :] = jnp.where(tile > 0, xp_ref[...], 0.0)
        xext_sc[8:8 + TM, :] = x
        xe = xext_sc[...]
        dcw_ref[KC4 - 1:KC4, :] += jnp.sum(dxr * x, axis=0, keepdims=True)
        for k in range(KC4 - 1):
            xs = pltpu.roll(xe, KC4 - 1 - k, 0)[8:8 + TM]
            dcw_ref[k:k + 1, :] += jnp.sum(dxr * xs, axis=0, keepdims=True)
        dcb_ref[...] += jnp.sum(dxr, axis=0, keepdims=True)

        @pl.when(i == nt - 1)
        def _():
            for hh in range(NHEAD):
                for qc in range(NCHIP):
                    dwa_ref[qc, hh] = dwa_sc[hh, qc * hq:(qc + 1) * hq, :].astype(bf16)
                    dwx_ref[qc, hh] = dwx_sc[hh, qc * hq:(qc + 1) * hq, :].astype(bf16)
            dlam_ref[...] = -dsp_sc[...] * _sigmoid(-lam_ref[...])

    rev = lambda i: (nt - 1 - i, 0)
    vec = pl.BlockSpec((1, D), lambda i: (0, 0))
    prev8 = lambda i: jnp.maximum((nt - 1 - i) * ng - 1, 0)
    wblk = pl.BlockSpec((NHEAD, HD, HD), lambda i: (0, 0, 0))
    gblk = pl.BlockSpec((NCHIP, NHEAD, hq, HD), lambda i: (0, 0, 0, 0))
    return pl.pallas_call(
        body, name="rnn_bwd", grid=(nt,),
        in_specs=[
            pl.BlockSpec((TM, D), rev),
            pl.BlockSpec((TM, D), rev),
            pl.BlockSpec((TM, D), rev),
            pl.BlockSpec((8, D), lambda i: (prev8(i), 0)),
            pl.BlockSpec((TM, D), lambda i: (nt - 1 - i, 0)),
            pl.BlockSpec((8, D), lambda i: (prev8(i), 0)),
            pl.BlockSpec((TM, D), lambda i: (nt - 1 - i, 1)),
            pl.BlockSpec(memory_space=pl.ANY),
            pl.BlockSpec((KC4, D), lambda i: (0, 0)),
            wblk, vec, wblk, vec, vec,
        ],
        out_specs=[
            pl.BlockSpec((TM, 2 * D), lambda i: (nt - 1 - i, 0)),
            gblk, gblk,
            pl.BlockSpec((KC4, D), lambda i: (0, 0)),
            vec, vec, vec, vec,
        ],
        out_shape=[jax.ShapeDtypeStruct((t, NIN), bf16),
                   jax.ShapeDtypeStruct((NCHIP, NHEAD, hq, HD), bf16),
                   jax.ShapeDtypeStruct((NCHIP, NHEAD, hq, HD), bf16),
                   jax.ShapeDtypeStruct((KC4, D), f32),
                   jax.ShapeDtypeStruct((1, D), f32), jax.ShapeDtypeStruct((1, D), f32),
                   jax.ShapeDtypeStruct((1, D), f32), jax.ShapeDtypeStruct((1, D), f32)],
        scratch_shapes=[pltpu.VMEM((1, D), f32), pltpu.VMEM((1, D), f32),
                        pltpu.VMEM((TM + 8, D), f32), pltpu.VMEM((TM + 8, D), f32),
                        pltpu.VMEM((TM, D), f32), pltpu.VMEM((TM, D), f32),
                        pltpu.VMEM((NHEAD, HD, HD), f32), pltpu.VMEM((NHEAD, HD, HD), f32),
                        pltpu.VMEM((1, D), f32)],
        input_output_aliases={7: 0},
        compiler_params=_cp(1),
    )(dz, xr, hr, hr, proj, proj, proj, dproj, cw, wa, ba, wx, bx, lam)


def _inproj_bwd(dproj, dh, h, g, win):
    t = h.shape[0]
    tn = NIN // NCHIP
    nj = NIN // tn
    per = (NIN // NCHIP) // tn

    def body(dp_ref, dh_ref, h_ref, g_ref, w_ref, dhi_ref, dg_ref, db_ref, dn_sc):
        i = pl.program_id(0)
        j = pl.program_id(1)

        @pl.when(jnp.logical_and(i == 0, j == 0))
        def _():
            dg_ref[...] = jnp.zeros_like(dg_ref)
            db_ref[...] = jnp.zeros_like(db_ref)

        @pl.when(j == 0)
        def _():
            dn_sc[...] = jnp.zeros_like(dn_sc)

        dp = dp_ref[...]
        dn_sc[...] += _nt_dot(dp, w_ref[...])
        db_ref[j] += jnp.sum(dp.astype(f32), axis=0, keepdims=True)

        @pl.when(j == nj - 1)
        def _():
            dhin, dg = _rms_bwd(dn_sc[...], h_ref[...], g_ref[...])
            dhi_ref[...] = dh_ref[...] + dhin
            dg_ref[...] += dg

    rowd = pl.BlockSpec((TM, D), lambda i, j: (i, 0))
    vec = pl.BlockSpec((1, D), lambda i, j: (0, 0))
    return pl.pallas_call(
        body, name="inproj_bwd", grid=(t // TM, nj),
        in_specs=[pl.BlockSpec((TM, tn), lambda i, j: (i, j)), rowd, rowd, vec,
                  pl.BlockSpec((None, D, tn), lambda i, j: (j // per, 0, j % per))],
        out_specs=[rowd, vec, pl.BlockSpec((nj, 1, tn), lambda i, j: (0, 0, 0))],
        out_shape=[jax.ShapeDtypeStruct((t, D), f32), jax.ShapeDtypeStruct((1, D), f32),
                   jax.ShapeDtypeStruct((nj, 1, tn), f32)],
        scratch_shapes=[pltpu.VMEM((TM, D), f32)],
        compiler_params=_cp(2),
    )(dproj, dh, h, g, win)


def _weight_grad_ffn(n, dgate, dup, a, df, tag):
    halves = [
        _tn_matmul(n, part, D, FS, (2, D, FS), (None, D, FS), lambda k, nn, m: (nn, 0, 0), tag + which)
        for part, which in ((dgate, "_dwg"), (dup, "_dwu"))
    ]
    dwd = _tn_matmul(a, df, FS, D, (F, D), (FS, D), lambda k, nn, m: (k, 0), tag + "_dwd")
    return jnp.concatenate(halves, axis=0), dwd


def _square_grad(a, b, name):
    return _tn_matmul(a, b, D, D, (D, D), (D, D), lambda k, nn, m: (0, 0), name)


def _local_step(h0, tgt, n_valid, p):
    h1, gate1, up1, n1 = _ffn_fwd(h0, p["ffn1_norm"], p["ffn1_w_gu"], p["ffn1_w_down"], "ffn1_fwd")
    proj, n2 = _inproj_fwd(h1, p["mix_norm"], p["w_in"], p["b_in"])
    xr, hr, z = _rnn_fwd(proj, p["rnn_conv_w"], p["rnn_conv_b"], p["rg_w_a"], p["rg_b_a"],
                         p["rg_w_x"], p["rg_b_x"], p["rg_lambda"])
    vc, s = _conv_fwd(proj, p["conv_dw_w"], p["conv_dw_b"], p["conv_ln_g"], p["conv_ln_b"])
    h2 = _merge_fwd(h1, z, s, proj, p["rnn_w_proj"], p["conv_w_proj"], p["conv_b_proj"], p["w_out"])
    h3, gate2, up2, n3 = _ffn_fwd(h2, p["ffn2_norm"], p["ffn2_w_gu"], p["ffn2_w_down"], "ffn2_fwd")
    dh3, loss, d_final = _final_loss(h3, p["final_norm"], tgt, n_valid)

    g = {"final_norm": d_final}
    dh2, dgate2, dup2, a2, df2, g["ffn2_norm"] = _ffn_bwd(
        dh3, h2, p["ffn2_norm"], gate2, up2, p["ffn2_w_gu"], p["ffn2_w_down"], "ffn2_bwd")
    g["ffn2_w_gu"], g["ffn2_w_down"] = _weight_grad_ffn(n3, dgate2, dup2, a2, df2, "ffn2")

    dz, ds, dproj, dh2b, merged, dya, dyb, g["conv_b_proj"] = _merge_bwd(
        dh2, z, s, proj, p["rnn_w_proj"], p["conv_w_proj"], p["conv_b_proj"], p["w_out"])
    g["w_out"] = _square_grad(merged, dh2b, "dw_out")
    g["rnn_w_proj"] = _square_grad(z, dya, "dw_rnn_proj")
    g["conv_w_proj"] = _square_grad(s, dyb, "dw_conv_proj")
    dproj, g["conv_dw_w"], g["conv_dw_b"], g["conv_ln_g"], g["conv_ln_b"] = _conv_bwd(
        ds, vc, proj, dproj, p["conv_dw_w"], p["conv_ln_g"], p["conv_ln_b"])
    (dproj, g["rg_w_a"], g["rg_w_x"], g["rnn_conv_w"], g["rnn_conv_b"], g["rg_b_a"], g["rg_b_x"],
     g["rg_lambda"]) = _rnn_bwd(dz, xr, hr, proj, dproj, p["rnn_conv_w"], p["rg_w_a"], p["rg_b_a"],
                                p["rg_w_x"], p["rg_b_x"], p["rg_lambda"])
    dh1, g["mix_norm"], db_in = _inproj_bwd(dproj, dh2, h1, p["mix_norm"], p["w_in"])
    g["b_in"] = db_in.reshape(1, NIN)
    g["w_in"] = _tn_matmul(n2, dproj, D, NIN // NCHIP, (NCHIP, D, NIN // NCHIP),
                           (None, D, NIN // NCHIP), lambda k, nn, m: (nn, 0, 0), "dw_in")

    dh0, dgate1, dup1, a1, df1, g["ffn1_norm"] = _ffn_bwd(
        dh1, h0, p["ffn1_norm"], gate1, up1, p["ffn1_w_gu"], p["ffn1_w_down"], "ffn1_bwd")
    g["ffn1_w_gu"], g["ffn1_w_down"] = _weight_grad_ffn(n1, dgate1, dup1, a1, df1, "ffn1")
    return loss, dh0, g


ANY = pl.BlockSpec(memory_space=pl.ANY)


def _place():
    x, y, c = lax.axis_index("x"), lax.axis_index("y"), lax.axis_index("c")
    chips = [(1 - x, y), (x, 1 - y), (1 - x, 1 - y)]
    return x, y, c, chips


def _chip_id(chip):
    return 2 * chip[0] + chip[1]


def _cast_into_slot(w2d, qc, dtype, name):
    r, cc = w2d.shape
    hr = r // 2

    def body(qc_ref, w_ref, o_ref):
        del qc_ref
        o_ref[...] = w_ref[...].astype(dtype)

    return pl.pallas_call(
        body, name=name,
        grid_spec=pltpu.PrefetchScalarGridSpec(
            num_scalar_prefetch=1, grid=(2,),
            in_specs=[pl.BlockSpec((hr, cc), lambda h, qc_ref: (h, 0))],
            out_specs=pl.BlockSpec((None, None, hr, cc), lambda h, qc_ref: (qc_ref[0], h, 0, 0))),
        out_shape=jax.ShapeDtypeStruct((NCHIP, 2, hr, cc), dtype),
        compiler_params=_cp(1),
    )(qc, w2d)


def _gather_shards(bufs):
    n = len(bufs)

    def body(*refs):
        outs = refs[n:2 * n]
        send_sems, recv_sems = refs[2 * n:]
        x, y, c, chips = _place()
        q = 2 * x + y
        sibling = (x, y, 1 - c)

        def remote(a, k, blk, to):
            return pltpu.make_async_remote_copy(src_ref=blk, dst_ref=blk, send_sem=send_sems.at[a, k],
                                                recv_sem=recv_sems.at[a, k], device_id=to, device_id_type=MESH)

        sent = []
        for a in range(n):
            for j, chip in enumerate(chips):
                cp = remote(a, j, outs[a].at[q, c], (chip[0], chip[1], c))
                cp.start()
                sent.append(cp)
        for a in range(n):
            for j, chip in enumerate(chips):
                got = outs[a].at[_chip_id(chip), c]
                remote(a, j, got, (chip[0], chip[1], c)).wait_recv()
                cp = remote(a, 3 + j, got, sibling)
                cp.start()
                sent.append(cp)
        for a in range(n):
            for j, chip in enumerate(chips):
                remote(a, 3 + j, outs[a].at[_chip_id(chip), 1 - c], sibling).wait_recv()
        for cp in sent:
            cp.wait_send()

    return pl.pallas_call(
        body, name="gather_shards",
        in_specs=[ANY] * n, out_specs=[ANY] * n,
        out_shape=[jax.ShapeDtypeStruct(s.shape, s.dtype) for s in bufs],
        scratch_shapes=[pltpu.SemaphoreType.DMA((n, 6)), pltpu.SemaphoreType.DMA((n, 6))],
        input_output_aliases={a: a for a in range(n)},
    )(*bufs)


def _pair_exchange(parts):
    n = len(parts)

    def body(*refs):
        ins, outs = refs[:n], refs[n:2 * n]
        send_sems, recv_sems = refs[2 * n:]
        x, y, c, _ = _place()
        copies = []
        for a in range(n):
            cp = pltpu.make_async_remote_copy(
                src_ref=ins[a].at[:, 1 - c], dst_ref=outs[a],
                send_sem=send_sems.at[a], recv_sem=recv_sems.at[a],
                device_id=(x, y, 1 - c), device_id_type=MESH)
            cp.start()
            copies.append(cp)
        for cp in copies:
            cp.wait()

    return pl.pallas_call(
        body, name="pair_exchange",
        in_specs=[ANY] * n, out_specs=[ANY] * n,
        out_shape=[jax.ShapeDtypeStruct((NCHIP,) + s.shape[2:], s.dtype) for s in parts],
        scratch_shapes=[pltpu.SemaphoreType.DMA((n,)), pltpu.SemaphoreType.DMA((n,))],
    )(*parts)


def _pair_add(part, got, qc, name):
    _, _, hr, cc = part.shape

    def body(qc_ref, p_ref, g_ref, o_ref, land_ref):
        s = pl.program_id(0)
        val = (p_ref[...].astype(f32) + g_ref[...].astype(f32)).astype(bf16)
        o_ref[...] = val

        @pl.when(s == qc_ref[0])
        def _():
            land_ref[...] = val

    return pl.pallas_call(
        body, name=name,
        grid_spec=pltpu.PrefetchScalarGridSpec(
            num_scalar_prefetch=1, grid=(NCHIP,),
            in_specs=[pl.BlockSpec((None, None, hr, cc), lambda s, qc_ref: (s, qc_ref[1], 0, 0)),
                      pl.BlockSpec((None, hr, cc), lambda s, qc_ref: (s, 0, 0))],
            out_specs=[pl.BlockSpec((None, hr, cc), lambda s, qc_ref: (s, 0, 0)),
                       pl.BlockSpec((None, hr, cc), lambda s, qc_ref: (qc_ref[0], 0, 0))]),
        out_shape=[jax.ShapeDtypeStruct((NCHIP, hr, cc), bf16)] * 2,
        compiler_params=_cp(1),
    )(qc, part, got)


def _chip_exchange(sums, lands):
    n = len(sums)

    def body(*refs):
        ins, outs = refs[:n], refs[2 * n:3 * n]
        send_sems, recv_sems = refs[3 * n:]
        x, y, c, chips = _place()
        q = 2 * x + y
        sent = []
        for a in range(n):
            for j, chip in enumerate(chips):
                cp = pltpu.make_async_remote_copy(
                    src_ref=ins[a].at[_chip_id(chip)], dst_ref=outs[a].at[q],
                    send_sem=send_sems.at[a, j], recv_sem=recv_sems.at[a, j],
                    device_id=(chip[0], chip[1], c), device_id_type=MESH)
                cp.start()
                sent.append(cp)
        for a in range(n):
            for j, chip in enumerate(chips):
                got = outs[a].at[_chip_id(chip)]
                pltpu.make_async_remote_copy(
                    src_ref=got, dst_ref=got, send_sem=send_sems.at[a, j], recv_sem=recv_sems.at[a, j],
                    device_id=(chip[0], chip[1], c), device_id_type=MESH).wait_recv()
        for cp in sent:
            cp.wait_send()

    return pl.pallas_call(
        body, name="chip_exchange",
        in_specs=[ANY] * (2 * n), out_specs=[ANY] * n,
        out_shape=[jax.ShapeDtypeStruct(s.shape, s.dtype) for s in lands],
        scratch_shapes=[pltpu.SemaphoreType.DMA((n, 3)), pltpu.SemaphoreType.DMA((n, 3))],
        input_output_aliases={n + a: a for a in range(n)},
    )(*sums, *lands)


def _sum_chips(got, name):
    _, hr, cc = got.shape

    def body(g_ref, o_ref):
        acc = g_ref[0].astype(f32)
        for s in range(1, NCHIP):
            acc = acc + g_ref[s].astype(f32)
        o_ref[...] = acc

    return pl.pallas_call(
        body, name=name, grid=(1,),
        in_specs=[pl.BlockSpec((NCHIP, hr, cc), lambda i: (0, 0, 0))],
        out_specs=pl.BlockSpec((hr, cc), lambda i: (0, 0)),
        out_shape=jax.ShapeDtypeStruct((hr, cc), f32),
        compiler_params=_cp(1),
    )(got)


def _pair_share(halves):
    n = len(halves)

    def body(*refs):
        ins, outs = refs[:n], refs[n:2 * n]
        send_sems, recv_sems = refs[2 * n:]
        x, y, c, _ = _place()
        copies = []
        for a in range(n):
            cp = pltpu.make_async_remote_copy(
                src_ref=ins[a], dst_ref=outs[a], send_sem=send_sems.at[a], recv_sem=recv_sems.at[a],
                device_id=(x, y, 1 - c), device_id_type=MESH)
            cp.start()
            copies.append(cp)
        for cp in copies:
            cp.wait()

    return pl.pallas_call(
        body, name="pair_share",
        in_specs=[ANY] * n, out_specs=[ANY] * n,
        out_shape=[jax.ShapeDtypeStruct(s.shape, s.dtype) for s in halves],
        scratch_shapes=[pltpu.SemaphoreType.DMA((n,)), pltpu.SemaphoreType.DMA((n,))],
    )(*halves)


def _gather_all(pack):
    def body(in_ref, out_ref, send_sems, recv_sems, local_sem):
        x, y, c, _ = _place()
        me = 4 * x + 2 * y + c
        lc = pltpu.make_async_copy(in_ref, out_ref.at[me], local_sem)
        lc.start()
        sent = []

        def peer(k):
            return (1 - x if k & 4 else x, 1 - y if k & 2 else y, 1 - c if k & 1 else c)

        for k in range(1, 8):
            cp = pltpu.make_async_remote_copy(
                src_ref=in_ref, dst_ref=out_ref.at[me], send_sem=send_sems.at[k - 1],
                recv_sem=recv_sems.at[k - 1], device_id=peer(k), device_id_type=MESH)
            cp.start()
            sent.append(cp)
        for k in range(1, 8):
            px, py, pc = peer(k)
            got = out_ref.at[4 * px + 2 * py + pc]
            pltpu.make_async_remote_copy(
                src_ref=in_ref, dst_ref=got, send_sem=send_sems.at[k - 1], recv_sem=recv_sems.at[k - 1],
                device_id=(px, py, pc), device_id_type=MESH).wait_recv()
        for cp in sent:
            cp.wait_send()
        lc.wait()

    return pl.pallas_call(
        body, name="gather_all",
        in_specs=[ANY], out_specs=ANY,
        out_shape=jax.ShapeDtypeStruct((8,) + pack.shape, pack.dtype),
        scratch_shapes=[pltpu.SemaphoreType.DMA((7,)), pltpu.SemaphoreType.DMA((7,)),
                        pltpu.SemaphoreType.DMA(())],
    )(pack)


def _adamw_math(w, g, m, v):
    m = ADAM_B1 * m + (1.0 - ADAM_B1) * g
    v = ADAM_B2 * v + (1.0 - ADAM_B2) * (g * g)
    m_hat = m / (1.0 - ADAM_B1 ** ADAM_STEP)
    v_hat = v / (1.0 - ADAM_B2 ** ADAM_STEP)
    delta = -ADAM_LR * (m_hat / (jnp.sqrt(v_hat) + ADAM_EPS) + ADAM_WD * w)
    return delta, m, v


def _adamw(w, mine, theirs, m, v, qc, name):
    r, cc = w.shape
    hr = r // 2
    tr = next(hr // k for k in range(1, hr + 1)
              if hr % k == 0 and (hr // k) % 8 == 0 and (hr // k) * cc * 4 <= (1 << 20))
    nb = hr // tr

    def body(qc_ref, w_ref, a_ref, b_ref, m_ref, v_ref, g_ref, d_ref, mo_ref, vo_ref):
        g = jnp.where(pl.program_id(0) == qc_ref[1], a_ref[...], b_ref[...])
        g_ref[...] = g
        d_ref[...], mo_ref[...], vo_ref[...] = _adamw_math(w_ref[...], g, m_ref[...], v_ref[...])

    full = pl.BlockSpec((tr, cc), lambda h, i, qc_ref: (h * nb + i, 0))
    half = pl.BlockSpec((tr, cc), lambda h, i, qc_ref: (i, 0))
    return pl.pallas_call(
        body, name=name,
        grid_spec=pltpu.PrefetchScalarGridSpec(
            num_scalar_prefetch=1, grid=(2, nb),
            in_specs=[full, half, half, full, full], out_specs=[full] * 4),
        out_shape=[jax.ShapeDtypeStruct((r, cc), f32)] * 4,
        compiler_params=_cp(2),
    )(qc, w, mine, theirs, m, v)


REPL = [("ffn1_norm", 1), ("mix_norm", 1), ("b_in", 6), ("rnn_conv_b", 1), ("rg_b_a", 1), ("rg_b_x", 1),
        ("rg_lambda", 1), ("conv_dw_b", 1), ("conv_ln_g", 1), ("conv_ln_b", 1), ("conv_b_proj", 1),
        ("ffn2_norm", 1), ("final_norm", 1)]
COLSH = [("meta_tokens", NMETA), ("rnn_conv_w", KC4), ("conv_dw_w", KC31)]
SMALL = REPL + COLSH
CS = D // NCHIP


def _pack_rows():
    starts, row = {}, 0
    for k, rows in REPL:
        starts[k] = row
        row += rows
    for k, rows in COLSH:
        row = -(-row // 8) * 8
        starts[k] = row
        row += rows
    return starts, -(-row // 8) * 8


PACK_START, SMALL_ROWS = _pack_rows()


def _small_pack(g):
    pieces, row = [], 0
    for k, rows in SMALL:
        if PACK_START[k] > row:
            pieces.append(jnp.zeros((PACK_START[k] - row, D), f32))
        pieces.append(g[k].reshape(rows, D))
        row = PACK_START[k] + rows
    pieces.append(jnp.zeros((SMALL_ROWS - row, D), f32))
    return jnp.concatenate(pieces, axis=0)


def _adamw_small(packs, ws, ms, vs):
    ns = len(SMALL)

    def body(*refs):
        pack_ref = refs[0]
        w_refs, m_refs, v_refs = refs[1:1 + ns], refs[1 + ns:1 + 2 * ns], refs[1 + 2 * ns:1 + 3 * ns]
        outs = refs[1 + 3 * ns:1 + 7 * ns]
        g_refs, d_refs, mo_refs, vo_refs = outs[:ns], outs[ns:2 * ns], outs[2 * ns:3 * ns], outs[3 * ns:]
        gsum_sc = refs[1 + 7 * ns]
        q = 2 * lax.axis_index("x") + lax.axis_index("y")
        acc = pack_ref[0]
        for dev in range(1, 8):
            acc = acc + pack_ref[dev]
        gsum_sc[...] = acc
        for idx, (name, rows) in enumerate(SMALL):
            row = PACK_START[name]
            if idx < len(REPL):
                for k in range(rows):
                    cols = slice(k * D, (k + 1) * D)
                    g = gsum_sc[row + k:row + k + 1, :]
                    d, mm, vv = _adamw_math(w_refs[idx][:, cols], g, m_refs[idx][:, cols], v_refs[idx][:, cols])
                    g_refs[idx][:, cols] = g
                    d_refs[idx][:, cols] = d
                    mo_refs[idx][:, cols] = mm
                    vo_refs[idx][:, cols] = vv
            else:
                g = gsum_sc[row:row + rows, pl.ds(pl.multiple_of(q * CS, CS), CS)]
                d, mm, vv = _adamw_math(w_refs[idx][...], g, m_refs[idx][...], v_refs[idx][...])
                g_refs[idx][...] = g
                d_refs[idx][...] = d
                mo_refs[idx][...] = mm
                vo_refs[idx][...] = vv

    shapes = [jax.ShapeDtypeStruct(w.shape, f32) for w in ws]
    return pl.pallas_call(
        body, name="adamw_small",
        out_shape=shapes * 4,
        scratch_shapes=[pltpu.VMEM((SMALL_ROWS, D), f32)],
        compiler_params=pltpu.CompilerParams(vmem_limit_bytes=VMEM_LIMIT),
    )(packs, *ws, *ms, *vs)


BIG = ["ffn1_w_gu", "ffn1_w_down", "w_in", "rg_w_a", "rg_w_x", "rnn_w_proj", "conv_w_proj", "w_out",
       "ffn2_w_gu", "ffn2_w_down"]
WEIGHTS = ['meta_tokens', 'ffn1_norm', 'ffn1_w_gu', 'ffn1_w_down', 'mix_norm', 'w_in', 'b_in', 'rnn_conv_w',
           'rnn_conv_b', 'rg_w_a', 'rg_b_a', 'rg_w_x', 'rg_b_x', 'rg_lambda', 'rnn_w_proj', 'conv_dw_w',
           'conv_dw_b', 'conv_ln_g', 'conv_ln_b', 'conv_w_proj', 'conv_b_proj', 'w_out', 'ffn2_norm',
           'ffn2_w_gu', 'ffn2_w_down', 'final_norm']


def _as2d(a):
    return a.reshape(-1, a.shape[-1])


def _step(x, loss_target, w, m, v):
    seq = x.shape[1]
    n_valid = NMETA + seq
    t = -(-n_valid // TM) * TM

    qc = jnp.stack([2 * lax.axis_index("x") + lax.axis_index("y"), lax.axis_index("c")]).astype(jnp.int32)
    small_rows = sum(r for _, r in COLSH)
    small = jnp.concatenate([_as2d(w[k]) for k, _ in COLSH] + [jnp.zeros((64 - small_rows, CS), f32)], axis=0)
    bufs = [_cast_into_slot(_as2d(w[k]), qc, bf16, "cast_" + k) for k in BIG]
    bufs.append(_cast_into_slot(small, qc, f32, "cast_small"))
    gathered = [b.reshape(NCHIP, 2 * b.shape[2], b.shape[3]) for b in _gather_shards(bufs)]
    p = dict(zip(BIG, gathered[:-1]))
    p["ffn1_w_down"] = p["ffn1_w_down"].reshape(F, D)
    p["ffn2_w_down"] = p["ffn2_w_down"].reshape(F, D)
    for k in ("rnn_w_proj", "conv_w_proj", "w_out"):
        p[k] = p[k].reshape(D, D)
    for k in ("rg_w_a", "rg_w_x"):
        p[k] = p[k].reshape(NCHIP, NHEAD, HD // NCHIP, HD).transpose(1, 0, 2, 3).reshape(NHEAD, HD, HD)
    small_full = gathered[-1].transpose(1, 0, 2).reshape(64, D)
    row = 0
    for k, rows in COLSH:
        p[k] = small_full[row:row + rows]
        row += rows
    for k, rows in REPL:
        p[k] = w[k].reshape(1, rows * D)

    h0 = jnp.concatenate([p["meta_tokens"], x[0], jnp.zeros((t - n_valid, D), f32)], axis=0)
    tgt = jnp.concatenate([jnp.zeros((NMETA, D), f32), loss_target[0], jnp.zeros((t - n_valid, D), f32)], axis=0)
    loss_blk, dh0, g = _local_step(h0, tgt, n_valid, p)
    g["meta_tokens"] = dh0[0:NMETA]
    grad_x = dh0[NMETA:n_valid][None]
    loss = lax.psum(loss_blk[0, 0], ("x", "y", "c"))

    parts = []
    for k in BIG:
        rows = g[k].size // (NCHIP * g[k].shape[-1])
        parts.append(g[k].reshape((NCHIP, 2, rows // 2, g[k].shape[-1])))
    from_sibling = _pair_exchange(parts)
    added = [_pair_add(pp, gg, qc, "pair_add_" + k) for pp, gg, k in zip(parts, from_sibling, BIG)]
    from_chips = _chip_exchange([s for s, _ in added], [land for _, land in added])
    mine = [_sum_chips(gg, "sum_chips_" + k) for gg, k in zip(from_chips, BIG)]
    theirs = _pair_share(mine)

    packs = _gather_all(_small_pack(g))

    grads, deltas, new_m, new_v = {}, {}, {}, {}
    for k, mi, th in zip(BIG, mine, theirs):
        outs = _adamw(_as2d(w[k]), mi, th, _as2d(m[k]), _as2d(v[k]), qc, "adamw_" + k)
        grads[k], deltas[k], new_m[k], new_v[k] = (a.reshape(w[k].shape) for a in outs)
    names = [k for k, _ in SMALL]
    shape2 = {k: ((1, rows * D) if (k, rows) in REPL else (rows, CS)) for k, rows in SMALL}
    outs = _adamw_small(packs, *[[a[k].reshape(shape2[k]) for k in names] for a in (w, m, v)])
    ns = len(names)
    for i, k in enumerate(names):
        grads[k], deltas[k], new_m[k], new_v[k] = (outs[j * ns + i].reshape(w[k].shape) for j in range(4))

    return (loss, grad_x, *[grads[k] for k in WEIGHTS], *[deltas[k] for k in WEIGHTS],
            *[new_m[k] for k in WEIGHTS], *[new_v[k] for k in WEIGHTS])


def kernel(x, meta_tokens, ffn1_norm, ffn1_w_gu, ffn1_w_down, mix_norm, w_in, b_in, rnn_conv_w, rnn_conv_b, rg_w_a, rg_b_a, rg_w_x, rg_b_x, rg_lambda, rnn_w_proj, conv_dw_w, conv_dw_b, conv_ln_g, conv_ln_b, conv_w_proj, conv_b_proj, w_out, ffn2_norm, ffn2_w_gu, ffn2_w_down, final_norm, loss_target, m_meta_tokens, m_ffn1_norm, m_ffn1_w_gu, m_ffn1_w_down, m_mix_norm, m_w_in, m_b_in, m_rnn_conv_w, m_rnn_conv_b, m_rg_w_a, m_rg_b_a, m_rg_w_x, m_rg_b_x, m_rg_lambda, m_rnn_w_proj, m_conv_dw_w, m_conv_dw_b, m_conv_ln_g, m_conv_ln_b, m_conv_w_proj, m_conv_b_proj, m_w_out, m_ffn2_norm, m_ffn2_w_gu, m_ffn2_w_down, m_final_norm, v_meta_tokens, v_ffn1_norm, v_ffn1_w_gu, v_ffn1_w_down, v_mix_norm, v_w_in, v_b_in, v_rnn_conv_w, v_rnn_conv_b, v_rg_w_a, v_rg_b_a, v_rg_w_x, v_rg_b_x, v_rg_lambda, v_rnn_w_proj, v_conv_dw_w, v_conv_dw_b, v_conv_ln_g, v_conv_ln_b, v_conv_w_proj, v_conv_b_proj, v_w_out, v_ffn2_norm, v_ffn2_w_gu, v_ffn2_w_down, v_final_norm):
    args = locals()
    w = {k: args[k] for k in WEIGHTS}
    m = {k: args["m_" + k] for k in WEIGHTS}
    v = {k: args["v_" + k] for k in WEIGHTS}
    return _step(x, loss_target, w, m, v)
```

```python
import functools

import jax
import jax.numpy as jnp
from jax import lax
from jax.experimental import pallas as pl
from jax.experimental.pallas import tpu as pltpu

f32 = jnp.float32
bf16 = jnp.bfloat16

D = 1024
F = 2816
FS = F // 2
NIN = 6 * D
NMETA = 16
NHEAD = 4
HD = D // NHEAD
KC4 = 4
KC31 = 31
HALO = 32
EPS = 1e-6
TM = 384
NCHIP = 4
MESH = pl.DeviceIdType.MESH

ADAM_LR = 0.001
ADAM_B1 = 0.9
ADAM_B2 = 0.999
ADAM_EPS = 1e-08
ADAM_WD = 0.01
ADAM_STEP = 10

VMEM_LIMIT = 56 * 1024 * 1024


def _cp(n_axes, **kw):
    return pltpu.CompilerParams(dimension_semantics=("arbitrary",) * n_axes,
                                vmem_limit_bytes=VMEM_LIMIT, **kw)


def _nt_dot(a, b):
    return lax.dot_general(a, b, (((1,), (1,)), ((), ())), preferred_element_type=f32)


def _tn_dot(a, b):
    return lax.dot_general(a, b, (((0,), (0,)), ((), ())), preferred_element_type=f32)


def _sigmoid(x):
    return 1.0 / (1.0 + jnp.exp(-x))


def _log1p(y):
    u = 1.0 + y
    d = u - 1.0
    return jnp.where(d == 0.0, y, jnp.log(u) * (y / jnp.where(d == 0.0, 1.0, d)))


def _softplus(x):
    return jnp.maximum(x, 0.0) + _log1p(jnp.exp(-jnp.abs(x)))


def _expm1(x):
    series = x * (1.0 + x * (0.5 + x * (1.0 / 6.0 + x * (1.0 / 24.0 + x * (1.0 / 120.0)))))
    return jnp.where(jnp.abs(x) < 0.1, series, jnp.exp(x) - 1.0)


_GELU_C = 0.7978845608028654
_GELU_K = 0.044715


def _gelu_and_grad(y):
    y2 = y * y
    th = jnp.tanh(_GELU_C * (y + _GELU_K * y * y2))
    gel = 0.5 * y * (1.0 + th)
    dgel = 0.5 * (1.0 + th) + 0.5 * y * (1.0 - th * th) * _GELU_C * (1.0 + 3.0 * _GELU_K * y2)
    return gel, dgel


def _rms_stats(h):
    return lax.rsqrt(jnp.mean(h * h, axis=-1, keepdims=True) + EPS)


def _rms_bwd(dn, h, g):
    r = _rms_stats(h)
    nhat = h * r
    dnh = dn * g
    dh = r * (dnh - nhat * jnp.mean(dnh * nhat, axis=-1, keepdims=True))
    dg = jnp.sum(dn * nhat, axis=0, keepdims=True)
    return dh, dg


def _row_ids(shape):
    return lax.broadcasted_iota(jnp.int32, shape, 0)


def _ffn_fwd(h, g, wgu, wd, name):
    t = h.shape[0]
    nj = 2

    def body(h_ref, g_ref, wg_ref, wu_ref, wd_ref, ho_ref, gate_ref, up_ref, n_ref, nb_sc, acc_sc):
        j = pl.program_id(1)

        @pl.when(j == 0)
        def _():
            hh = h_ref[...]
            nb = (hh * _rms_stats(hh) * g_ref[...]).astype(bf16)
            nb_sc[...] = nb
            n_ref[...] = nb
            acc_sc[...] = jnp.zeros_like(acc_sc)

        nb = nb_sc[...]
        gt = jnp.dot(nb, wg_ref[...], preferred_element_type=f32)
        up = jnp.dot(nb, wu_ref[...], preferred_element_type=f32)
        gate_ref[...] = gt.astype(bf16)
        up_ref[...] = up.astype(bf16)
        a = (gt * _sigmoid(gt) * up).astype(bf16)
        acc_sc[...] += jnp.dot(a, wd_ref[...], preferred_element_type=f32)

        @pl.when(j == nj - 1)
        def _():
            ho_ref[...] = h_ref[...] + 0.5 * acc_sc[...]

    return pl.pallas_call(
        body, name=name, grid=(t // TM, nj),
        in_specs=[
            pl.BlockSpec((TM, D), lambda i, j: (i, 0)),
            pl.BlockSpec((1, D), lambda i, j: (0, 0)),
            pl.BlockSpec((None, D, FS), lambda i, j: (j, 0, 0)),
            pl.BlockSpec((None, D, FS), lambda i, j: (2 + j, 0, 0)),
            pl.BlockSpec((FS, D), lambda i, j: (j, 0)),
        ],
        out_specs=[
            pl.BlockSpec((TM, D), lambda i, j: (i, 0)),
            pl.BlockSpec((TM, FS), lambda i, j: (i, j)),
            pl.BlockSpec((TM, FS), lambda i, j: (i, j)),
            pl.BlockSpec((TM, D), lambda i, j: (i, 0)),
        ],
        out_shape=[
            jax.ShapeDtypeStruct((t, D), f32),
            jax.ShapeDtypeStruct((t, F), bf16),
            jax.ShapeDtypeStruct((t, F), bf16),
            jax.ShapeDtypeStruct((t, D), bf16),
        ],
        scratch_shapes=[pltpu.VMEM((TM, D), bf16), pltpu.VMEM((TM, D), f32)],
        compiler_params=_cp(2),
    )(h, g, wgu, wgu, wd)


def _inproj_fwd(h, g, win, b_in):
    t = h.shape[0]
    tn = NIN // NCHIP
    nj = NIN // tn
    per = (NIN // NCHIP) // tn

    def body(h_ref, g_ref, w_ref, b_ref, proj_ref, n_ref, nb_sc):
        j = pl.program_id(1)

        @pl.when(j == 0)
        def _():
            hh = h_ref[...]
            nb = (hh * _rms_stats(hh) * g_ref[...]).astype(bf16)
            nb_sc[...] = nb
            n_ref[...] = nb

        proj_ref[...] = jnp.dot(nb_sc[...], w_ref[...], preferred_element_type=f32) + b_ref[...]

    return pl.pallas_call(
        body, name="inproj_fwd", grid=(t // TM, nj),
        in_specs=[
            pl.BlockSpec((TM, D), lambda i, j: (i, 0)),
            pl.BlockSpec((1, D), lambda i, j: (0, 0)),
            pl.BlockSpec((None, D, tn), lambda i, j: (j // per, 0, j % per)),
            pl.BlockSpec((1, tn), lambda i, j: (0, j)),
        ],
        out_specs=[
            pl.BlockSpec((TM, tn), lambda i, j: (i, j)),
            pl.BlockSpec((TM, D), lambda i, j: (i, 0)),
        ],
        out_shape=[jax.ShapeDtypeStruct((t, NIN), f32), jax.ShapeDtypeStruct((t, D), bf16)],
        scratch_shapes=[pltpu.VMEM((TM, D), bf16)],
        compiler_params=_cp(2),
    )(h, g, win, b_in)


def _block_gates(xr, wa_ref, ba, wx_ref, bx, lam):
    xrb = xr.astype(bf16)
    pa = jnp.concatenate([jnp.dot(xrb[:, hh * HD:(hh + 1) * HD], wa_ref[hh], preferred_element_type=f32)
                          for hh in range(NHEAD)], axis=1)
    px = jnp.concatenate([jnp.dot(xrb[:, hh * HD:(hh + 1) * HD], wx_ref[hh], preferred_element_type=f32)
                          for hh in range(NHEAD)], axis=1)
    ra = _sigmoid(pa + ba)
    ii = _sigmoid(px + bx)
    sp = _softplus(-lam)
    log_a = -8.0 * ra * sp
    a = jnp.exp(log_a)
    sq = jnp.sqrt(-_expm1(2.0 * log_a))
    return ra, ii, a, sq, sp


def _rnn_fwd(proj, cw, cb, wa, ba, wx, bx, lam):
    t = proj.shape[0]
    ng = TM // 8

    def body(x_ref, y_ref, cw_ref, cb_ref, wa_ref, ba_ref, wx_ref, bx_ref, lam_ref,
             xr_ref, hr_ref, z_ref, xext_sc, carry_sc, a_sc, h_sc):
        i = pl.program_id(0)

        @pl.when(i == 0)
        def _():
            xext_sc[0:8, :] = jnp.zeros((8, D), f32)
            carry_sc[...] = jnp.zeros_like(carry_sc)

        x = x_ref[...]
        xext_sc[8:8 + TM, :] = x
        xe = xext_sc[...]
        xr = cb_ref[...] + cw_ref[KC4 - 1:KC4, :] * x
        for k in range(KC4 - 1):
            xr = xr + cw_ref[k:k + 1, :] * pltpu.roll(xe, KC4 - 1 - k, 0)[8:8 + TM]
        xext_sc[0:8, :] = x[TM - 8:TM]

        _, ii, a, sq, _ = _block_gates(xr, wa_ref, ba_ref[...], wx_ref, bx_ref[...], lam_ref[...])
        a_sc[...] = a
        h_sc[...] = sq * ii * xr
        row = _row_ids((8, D))

        def group(r, carry):
            off = pl.multiple_of(r * 8, 8)
            aa = a_sc[pl.ds(off, 8), :]
            hh = h_sc[pl.ds(off, 8), :]
            for s in (1, 2, 4):
                a_sh = jnp.where(row >= s, pltpu.roll(aa, s, 0), 1.0)
                h_sh = jnp.where(row >= s, pltpu.roll(hh, s, 0), 0.0)
                hh = aa * h_sh + hh
                aa = aa * a_sh
            hh = hh + aa * carry
            h_sc[pl.ds(off, 8), :] = hh
            return hh[7:8, :]

        carry_sc[...] = lax.fori_loop(0, ng, group, carry_sc[...])
        hr = h_sc[...]
        gel, _ = _gelu_and_grad(y_ref[...])
        xr_ref[...] = xr
        hr_ref[...] = hr
        z_ref[...] = (hr * gel).astype(bf16)

    vec = pl.BlockSpec((1, D), lambda i: (0, 0))
    return pl.pallas_call(
        body, name="rnn_fwd", grid=(t // TM,),
        in_specs=[
            pl.BlockSpec((TM, D), lambda i: (i, 0)),
            pl.BlockSpec((TM, D), lambda i: (i, 1)),
            pl.BlockSpec((KC4, D), lambda i: (0, 0)),
            vec,
            pl.BlockSpec((NHEAD, HD, HD), lambda i: (0, 0, 0)),
            vec,
            pl.BlockSpec((NHEAD, HD, HD), lambda i: (0, 0, 0)),
            vec, vec,
        ],
        out_specs=[pl.BlockSpec((TM, D), lambda i: (i, 0))] * 3,
        out_shape=[jax.ShapeDtypeStruct((t, D), f32), jax.ShapeDtypeStruct((t, D), f32),
                   jax.ShapeDtypeStruct((t, D), bf16)],
        scratch_shapes=[pltpu.VMEM((TM + 8, D), f32), pltpu.VMEM((1, D), f32),
                        pltpu.VMEM((TM, D), f32), pltpu.VMEM((TM, D), f32)],
        compiler_params=_cp(1),
    )(proj, proj, cw, cb, wa, ba, wx, bx, lam)


def _ln_stats(vc):
    mu = jnp.mean(vc, axis=-1, keepdims=True)
    xc = vc - mu
    rstd = lax.rsqrt(jnp.mean(xc * xc, axis=-1, keepdims=True) + EPS)
    return xc * rstd, rstd


def _conv_fwd(proj, w31, b31, ln_g, ln_b):
    t = proj.shape[0]

    def body(gv_ref, gg_ref, w_ref, b_ref, lg_ref, lb_ref, vc_ref, s_ref, vext_sc):
        i = pl.program_id(0)

        @pl.when(i == 0)
        def _():
            vext_sc[0:HALO, :] = jnp.zeros((HALO, D), f32)

        v = gv_ref[...] * _sigmoid(gg_ref[...])
        vext_sc[HALO:HALO + TM, :] = v
        ve = vext_sc[...]
        acc = jnp.zeros((TM, D), f32) + b_ref[...]
        for s in range(8):
            vs = ve if s == 0 else pltpu.roll(ve, s, 0)
            for m in range(HALO // 8):
                k = KC31 - 1 - (8 * m + s)
                if 0 <= k < KC31:
                    acc = acc + w_ref[k:k + 1, :] * vs[HALO - 8 * m:HALO - 8 * m + TM]
        vext_sc[0:HALO, :] = v[TM - HALO:TM]
        xhat, _ = _ln_stats(acc)
        ln = xhat * lg_ref[...] + lb_ref[...]
        vc_ref[...] = acc
        s_ref[...] = (ln * _sigmoid(ln)).astype(bf16)

    vec = pl.BlockSpec((1, D), lambda i: (0, 0))
    return pl.pallas_call(
        body, name="conv_fwd", grid=(t // TM,),
        in_specs=[
            pl.BlockSpec((TM, D), lambda i: (i, 2)),
            pl.BlockSpec((TM, D), lambda i: (i, 3)),
            pl.BlockSpec((KC31, D), lambda i: (0, 0)),
            vec, vec, vec,
        ],
        out_specs=[pl.BlockSpec((TM, D), lambda i: (i, 0))] * 2,
        out_shape=[jax.ShapeDtypeStruct((t, D), f32), jax.ShapeDtypeStruct((t, D), bf16)],
        scratch_shapes=[pltpu.VMEM((TM + HALO, D), f32)],
        compiler_params=_cp(1),
    )(proj, proj, w31, b31, ln_g, ln_b)


def _merge_fwd(h, z, s, proj, wrp, wcp, bcp, wout):
    t = h.shape[0]

    def body(h_ref, z_ref, s_ref, ga_ref, gb_ref, wrp_ref, wcp_ref, bcp_ref, wout_ref, ho_ref):
        ya = jnp.dot(z_ref[...], wrp_ref[...], preferred_element_type=f32)
        yb = jnp.dot(s_ref[...], wcp_ref[...], preferred_element_type=f32) + bcp_ref[...]
        merged = _sigmoid(ga_ref[...]) * ya + _sigmoid(gb_ref[...]) * yb
        ho_ref[...] = h_ref[...] + jnp.dot(merged.astype(bf16), wout_ref[...], preferred_element_type=f32)

    row = pl.BlockSpec((TM, D), lambda i: (i, 0))
    wsq = pl.BlockSpec((D, D), lambda i: (0, 0))
    return pl.pallas_call(
        body, name="merge_fwd", grid=(t // TM,),
        in_specs=[row, row, row,
                  pl.BlockSpec((TM, D), lambda i: (i, 4)),
                  pl.BlockSpec((TM, D), lambda i: (i, 5)),
                  wsq, wsq, pl.BlockSpec((1, D), lambda i: (0, 0)), wsq],
        out_specs=row,
        out_shape=jax.ShapeDtypeStruct((t, D), f32),
        compiler_params=_cp(1),
    )(h, z, s, proj, proj, wrp, wcp, bcp, wout)


def _final_loss(h, g, tgt, n_valid):
    t = h.shape[0]

    def body(h_ref, g_ref, t_ref, dh_ref, loss_ref, dg_ref):
        i = pl.program_id(0)

        @pl.when(i == 0)
        def _():
            loss_ref[...] = jnp.zeros_like(loss_ref)
            dg_ref[...] = jnp.zeros_like(dg_ref)

        hh = h_ref[...]
        gg = g_ref[...]
        row = i * TM + _row_ids((TM, 1))
        valid = jnp.logical_and(row >= NMETA, row < n_valid)
        out = hh * _rms_stats(hh) * gg
        err = jnp.where(valid, out - t_ref[...], 0.0)
        loss_ref[...] += 0.5 * jnp.sum(err * err) * (1.0 / D)
        dh, dg = _rms_bwd(err * (1.0 / D), hh, gg)
        dh_ref[...] = dh
        dg_ref[...] += dg

    row_spec = pl.BlockSpec((TM, D), lambda i: (i, 0))
    return pl.pallas_call(
        body, name="final_loss", grid=(t // TM,),
        in_specs=[row_spec, pl.BlockSpec((1, D), lambda i: (0, 0)), row_spec],
        out_specs=[row_spec, pl.BlockSpec((8, 128), lambda i: (0, 0)), pl.BlockSpec((1, D), lambda i: (0, 0))],
        out_shape=[jax.ShapeDtypeStruct((t, D), f32), jax.ShapeDtypeStruct((8, 128), f32),
                   jax.ShapeDtypeStruct((1, D), f32)],
        compiler_params=_cp(1),
    )(h, g, tgt)


def _ffn_bwd(dh, h, g, gate, up, wgu, wd, name):
    t = h.shape[0]
    nj = 2

    def body(dh_ref, h_ref, g_ref, gate_ref, up_ref, wg_ref, wu_ref, wd_ref,
             dhi_ref, dgate_ref, dup_ref, a_ref, df_ref, dg_ref, dfb_sc, dn_sc):
        i = pl.program_id(0)
        j = pl.program_id(1)

        @pl.when(jnp.logical_and(i == 0, j == 0))
        def _():
            dg_ref[...] = jnp.zeros_like(dg_ref)

        @pl.when(j == 0)
        def _():
            dfb = (0.5 * dh_ref[...]).astype(bf16)
            dfb_sc[...] = dfb
            df_ref[...] = dfb
            dn_sc[...] = jnp.zeros_like(dn_sc)

        da = _nt_dot(dfb_sc[...], wd_ref[...])
        gt = gate_ref[...].astype(f32)
        uu = up_ref[...].astype(f32)
        sg = _sigmoid(gt)
        silu = gt * sg
        a_ref[...] = (silu * uu).astype(bf16)
        dgt = (da * uu * (sg * (1.0 + gt * (1.0 - sg)))).astype(bf16)
        dup = (da * silu).astype(bf16)
        dgate_ref[...] = dgt
        dup_ref[...] = dup
        dn_sc[...] += _nt_dot(dgt, wg_ref[...]) + _nt_dot(dup, wu_ref[...])

        @pl.when(j == nj - 1)
        def _():
            dhin, dg = _rms_bwd(dn_sc[...], h_ref[...], g_ref[...])
            dhi_ref[...] = dh_ref[...] + dhin
            dg_ref[...] += dg

    rowd = pl.BlockSpec((TM, D), lambda i, j: (i, 0))
    rowf = pl.BlockSpec((TM, FS), lambda i, j: (i, j))
    vec = pl.BlockSpec((1, D), lambda i, j: (0, 0))
    return pl.pallas_call(
        body, name=name, grid=(t // TM, nj),
        in_specs=[rowd, rowd, vec, rowf, rowf,
                  pl.BlockSpec((None, D, FS), lambda i, j: (j, 0, 0)),
                  pl.BlockSpec((None, D, FS), lambda i, j: (2 + j, 0, 0)),
                  pl.BlockSpec((FS, D), lambda i, j: (j, 0))],
        out_specs=[rowd, rowf, rowf, rowf, rowd, vec],
        out_shape=[jax.ShapeDtypeStruct((t, D), f32), jax.ShapeDtypeStruct((t, F), bf16),
                   jax.ShapeDtypeStruct((t, F), bf16), jax.ShapeDtypeStruct((t, F), bf16),
                   jax.ShapeDtypeStruct((t, D), bf16), jax.ShapeDtypeStruct((1, D), f32)],
        scratch_shapes=[pltpu.VMEM((TM, D), bf16), pltpu.VMEM((TM, D), f32)],
        compiler_params=_cp(2),
    )(dh, h, g, gate, up, wgu, wgu, wd)


def _big_tile(t):
    for cand in (2112, 1408, 768, 384):
        if t % cand == 0:
            return cand
    raise ValueError(t)


def _tn_matmul(a, b, tk, tn, out_shape, out_block, out_map, name):
    t, kk = a.shape
    _, nn = b.shape
    tmm = _big_tile(t)
    nm = t // tmm

    def body(a_ref, b_ref, o_ref, acc_sc):
        m = pl.program_id(2)

        @pl.when(m == 0)
        def _():
            acc_sc[...] = jnp.zeros_like(acc_sc)

        acc_sc[...] += _tn_dot(a_ref[...], b_ref[...])

        @pl.when(m == nm - 1)
        def _():
            o_ref[...] = acc_sc[...].astype(o_ref.dtype)

    return pl.pallas_call(
        body, name=name, grid=(kk // tk, nn // tn, nm),
        in_specs=[pl.BlockSpec((tmm, tk), lambda k, n, m: (m, k)),
                  pl.BlockSpec((tmm, tn), lambda k, n, m: (m, n))],
        out_specs=pl.BlockSpec(out_block, out_map),
        out_shape=jax.ShapeDtypeStruct(out_shape, bf16),
        scratch_shapes=[pltpu.VMEM((tk, tn), f32)],
        compiler_params=_cp(3),
    )(a, b)


def _merge_bwd(dh, z, s, proj, wrp, wcp, bcp, wout):
    t = dh.shape[0]

    def body(dh_ref, z_ref, s_ref, ga_ref, gb_ref, wrp_ref, wcp_ref, bcp_ref, wout_ref,
             dz_ref, ds_ref, dgab_ref, dhb_ref, mg_ref, dya_ref, dyb_ref, dbcp_ref):
        i = pl.program_id(0)

        @pl.when(i == 0)
        def _():
            dbcp_ref[...] = jnp.zeros_like(dbcp_ref)

        dhb = dh_ref[...].astype(bf16)
        dhb_ref[...] = dhb
        dmg = _nt_dot(dhb, wout_ref[...])
        ya = jnp.dot(z_ref[...], wrp_ref[...], preferred_element_type=f32)
        yb = jnp.dot(s_ref[...], wcp_ref[...], preferred_element_type=f32) + bcp_ref[...]
        sa = _sigmoid(ga_ref[...])
        sb = _sigmoid(gb_ref[...])
        mg_ref[...] = (sa * ya + sb * yb).astype(bf16)
        dgab_ref[:, 0:D] = (dmg * ya * sa * (1.0 - sa)).astype(bf16)
        dgab_ref[:, D:2 * D] = (dmg * yb * sb * (1.0 - sb)).astype(bf16)
        dya = dmg * sa
        dyb = dmg * sb
        dbcp_ref[...] += jnp.sum(dyb, axis=0, keepdims=True)
        dyab = dya.astype(bf16)
        dybb = dyb.astype(bf16)
        dya_ref[...] = dyab
        dyb_ref[...] = dybb
        dz_ref[...] = _nt_dot(dyab, wrp_ref[...])
        ds_ref[...] = _nt_dot(dybb, wcp_ref[...])

    row = pl.BlockSpec((TM, D), lambda i: (i, 0))
    wsq = pl.BlockSpec((D, D), lambda i: (0, 0))
    vec = pl.BlockSpec((1, D), lambda i: (0, 0))
    rowb = jax.ShapeDtypeStruct((t, D), bf16)
    return pl.pallas_call(
        body, name="merge_bwd", grid=(t // TM,),
        in_specs=[row, row, row,
                  pl.BlockSpec((TM, D), lambda i: (i, 4)),
                  pl.BlockSpec((TM, D), lambda i: (i, 5)),
                  wsq, wsq, vec, wsq],
        out_specs=[row, row,
                   pl.BlockSpec((TM, 2 * D), lambda i: (i, 2)),
                   row, row, row, row, vec],
        out_shape=[jax.ShapeDtypeStruct((t, D), f32), jax.ShapeDtypeStruct((t, D), f32),
                   jax.ShapeDtypeStruct((t, NIN), bf16),
                   rowb, rowb, rowb, rowb, jax.ShapeDtypeStruct((1, D), f32)],
        compiler_params=_cp(1),
    )(dh, z, s, proj, proj, wrp, wcp, bcp, wout)


def _conv_bwd(ds, vc, proj, dproj, w31, ln_g, ln_b):
    t = ds.shape[0]
    nt = t // TM
    hb = TM // HALO

    def body(ds_ref, vc_ref, gv_ref, gg_ref, gvp_ref, ggp_ref, dpin_ref, w_ref, lg_ref, lb_ref,
             dgvg_ref, dw_ref, db_ref, dlg_ref, dlb_ref, dext_sc, vext_sc, dwacc_sc):
        del dpin_ref
        i = pl.program_id(0)
        tile = nt - 1 - i

        @pl.when(i == 0)
        def _():
            dext_sc[TM:TM + HALO, :] = jnp.zeros((HALO, D), f32)
            dwacc_sc[...] = jnp.zeros_like(dwacc_sc)
            db_ref[...] = jnp.zeros_like(db_ref)
            dlg_ref[...] = jnp.zeros_like(dlg_ref)
            dlb_ref[...] = jnp.zeros_like(dlb_ref)

        vc = vc_ref[...]
        xhat, rstd = _ln_stats(vc)
        lg = lg_ref[...]
        ln = xhat * lg + lb_ref[...]
        sg = _sigmoid(ln)
        dln = ds_ref[...] * (sg * (1.0 + ln * (1.0 - sg)))
        dlg_ref[...] += jnp.sum(dln * xhat, axis=0, keepdims=True)
        dlb_ref[...] += jnp.sum(dln, axis=0, keepdims=True)
        dxh = dln * lg
        dvc = rstd * (dxh - jnp.mean(dxh, axis=-1, keepdims=True)
                      - xhat * jnp.mean(dxh * xhat, axis=-1, keepdims=True))
        db_ref[...] += jnp.sum(dvc, axis=0, keepdims=True)

        dext_sc[0:TM, :] = dvc
        de = dext_sc[...]
        dv = jnp.zeros((TM, D), f32)
        for s in range(8):
            dsh = de if s == 0 else pltpu.roll(de, TM + HALO - s, 0)
            for m in range(HALO // 8):
                k = KC31 - 1 - (8 * m + s)
                if 0 <= k < KC31:
                    dv = dv + w_ref[k:k + 1, :] * dsh[8 * m:8 * m + TM]
        dext_sc[TM:TM + HALO, :] = dvc[0:HALO]

        gv = gv_ref[...]
        sgg = _sigmoid(gg_ref[...])
        dgvg_ref[:, 0:D] = (dv * sgg).astype(bf16)
        dgvg_ref[:, D:2 * D] = (dv * gv * sgg * (1.0 - sgg)).astype(bf16)

        vprev = gvp_ref[...] * _sigmoid(ggp_ref[...])
        vext_sc[0:HALO, :] = jnp.where(tile > 0, vprev, 0.0)
        vext_sc[HALO:HALO + TM, :] = gv * sgg
        ve = vext_sc[...]
        for s in range(8):
            vs = ve if s == 0 else pltpu.roll(ve, s, 0)
            for m in range(HALO // 8):
                k = KC31 - 1 - (8 * m + s)
                if 0 <= k < KC31:
                    prod = dvc * vs[HALO - 8 * m:HALO - 8 * m + TM]
                    dwacc_sc[k] += jnp.sum(prod.reshape(TM // 8, 8, D), axis=0)

        @pl.when(i == nt - 1)
        def _():
            for k in range(KC31):
                dw_ref[k:k + 1, :] = jnp.sum(dwacc_sc[k], axis=0, keepdims=True)

    rev = lambda i: (nt - 1 - i, 0)
    vec = pl.BlockSpec((1, D), lambda i: (0, 0))
    halo_row = lambda i: jnp.maximum((nt - 1 - i) * hb - 1, 0)
    return pl.pallas_call(
        body, name="conv_bwd", grid=(nt,),
        in_specs=[
            pl.BlockSpec((TM, D), rev),
            pl.BlockSpec((TM, D), rev),
            pl.BlockSpec((TM, D), lambda i: (nt - 1 - i, 2)),
            pl.BlockSpec((TM, D), lambda i: (nt - 1 - i, 3)),
            pl.BlockSpec((HALO, D), lambda i: (halo_row(i), 2)),
            pl.BlockSpec((HALO, D), lambda i: (halo_row(i), 3)),
            pl.BlockSpec(memory_space=pl.ANY),
            pl.BlockSpec((KC31, D), lambda i: (0, 0)),
            vec, vec,
        ],
        out_specs=[
            pl.BlockSpec((TM, 2 * D), lambda i: (nt - 1 - i, 1)),
            pl.BlockSpec((KC31, D), lambda i: (0, 0)),
            vec, vec, vec,
        ],
        out_shape=[jax.ShapeDtypeStruct((t, NIN), bf16),
                   jax.ShapeDtypeStruct((KC31, D), f32),
                   jax.ShapeDtypeStruct((1, D), f32), jax.ShapeDtypeStruct((1, D), f32),
                   jax.ShapeDtypeStruct((1, D), f32)],
        scratch_shapes=[pltpu.VMEM((TM + HALO, D), f32), pltpu.VMEM((TM + HALO, D), f32),
                        pltpu.VMEM((KC31, 8, D), f32)],
        input_output_aliases={6: 0},
        compiler_params=_cp(1),
    )(ds, vc, proj, proj, proj, proj, dproj, w31, ln_g, ln_b)


def _rnn_bwd(dz, xr, hr, proj, dproj, cw, wa, ba, wx, bx, lam):
    t = dz.shape[0]
    nt = t // TM
    ng = TM // 8
    hq = HD // NCHIP

    def body(dz_ref, xr_ref, hr_ref, hrp_ref, x_ref, xp_ref, y_ref, dpin_ref,
             cw_ref, wa_ref, ba_ref, wx_ref, bx_ref, lam_ref,
             dxy_ref, dwa_ref, dwx_ref, dcw_ref, dcb_ref, dba_ref, dbx_ref, dlam_ref,
             anext_sc, gcarry_sc, dext_sc, xext_sc, m_sc, g_sc, dwa_sc, dwx_sc, dsp_sc):
        del dpin_ref
        i = pl.program_id(0)
        tile = nt - 1 - i

        @pl.when(i == 0)
        def _():
            anext_sc[...] = jnp.zeros_like(anext_sc)
            gcarry_sc[...] = jnp.zeros_like(gcarry_sc)
            dext_sc[TM:TM + 8, :] = jnp.zeros((8, D), f32)
            dwa_sc[...] = jnp.zeros_like(dwa_sc)
            dwx_sc[...] = jnp.zeros_like(dwx_sc)
            dsp_sc[...] = jnp.zeros_like(dsp_sc)
            dcw_ref[...] = jnp.zeros_like(dcw_ref)
            dcb_ref[...] = jnp.zeros_like(dcb_ref)
            dba_ref[...] = jnp.zeros_like(dba_ref)
            dbx_ref[...] = jnp.zeros_like(dbx_ref)

        xr = xr_ref[...]
        hr = hr_ref[...]
        dz = dz_ref[...]
        gel, dgel = _gelu_and_grad(y_ref[...])
        dxy_ref[:, D:2 * D] = (dz * hr * dgel).astype(bf16)
        ra, ii, a, sq, sp = _block_gates(xr, wa_ref, ba_ref[...], wx_ref, bx_ref[...], lam_ref[...])

        row = _row_ids((TM, D))
        m_sc[...] = jnp.where(row == TM - 1, anext_sc[...], pltpu.roll(a, TM - 1, 0))
        anext_sc[...] = a[0:1, :]
        g_sc[...] = dz * gel
        row8 = _row_ids((8, D))

        def group(qq, carry):
            off = pl.multiple_of((ng - 1 - qq) * 8, 8)
            mm = m_sc[pl.ds(off, 8), :]
            dd = g_sc[pl.ds(off, 8), :]
            for s in (1, 2, 4):
                m_sh = jnp.where(row8 < 8 - s, pltpu.roll(mm, 8 - s, 0), 1.0)
                d_sh = jnp.where(row8 < 8 - s, pltpu.roll(dd, 8 - s, 0), 0.0)
                dd = dd + mm * d_sh
                mm = mm * m_sh
            dd = dd + mm * carry
            g_sc[pl.ds(off, 8), :] = dd
            return dd[0:1, :]

        gcarry_sc[...] = lax.fori_loop(0, ng, group, gcarry_sc[...])
        gg = g_sc[...]

        hlast = jnp.where(tile > 0, hrp_ref[7:8, :], 0.0)
        hprev = jnp.where(row == 0, hlast, pltpu.roll(hr, 1, 0))
        d_a = gg * hprev
        dsq = gg * ii * xr
        dii = gg * sq * xr
        dxr = gg * sq * ii
        dlog = d_a * a - dsq * (a * a / sq)
        dsp_sc[...] += jnp.sum(dlog * (-8.0 * ra), axis=0, keepdims=True)
        dpa = dlog * (-8.0 * sp) * ra * (1.0 - ra)
        dpx = dii * ii * (1.0 - ii)
        dba_ref[...] += jnp.sum(dpa, axis=0, keepdims=True)
        dbx_ref[...] += jnp.sum(dpx, axis=0, keepdims=True)
        dpab = dpa.astype(bf16)
        dpxb = dpx.astype(bf16)
        xrb = xr.astype(bf16)
        back = []
        for hh in range(NHEAD):
            cols = slice(hh * HD, (hh + 1) * HD)
            back.append(_nt_dot(dpab[:, cols], wa_ref[hh]) + _nt_dot(dpxb[:, cols], wx_ref[hh]))
            dwa_sc[hh] += _tn_dot(xrb[:, cols], dpab[:, cols])
            dwx_sc[hh] += _tn_dot(xrb[:, cols], dpxb[:, cols])
        dxr = dxr + jnp.concatenate(back, axis=1)

        dext_sc[0:TM, :] = dxr
        de = dext_sc[...]
        dx = cw_ref[KC4 - 1:KC4, :] * dxr
        for k in range(KC4 - 1):
            dx = dx + cw_ref[k:k + 1, :] * pltpu.roll(de, TM + 8 - (KC4 - 1 - k), 0)[0:TM]
        dext_sc[TM:TM + 8, :] = dxr[0:8]
        dxy_ref[:, 0:D] = dx.astype(bf16)

        x = x_ref[...]
        xext_sc[0:8, :] = jnp.where(tile > 0, xp_ref[...], 0.0)
        xext_sc[8:8 + TM, :] = x
        xe = xext_sc[...]
        dcw_ref[KC4 - 1:KC4, :] += jnp.sum(dxr * x, axis=0, keepdims=True)
        for k in range(KC4 - 1):
            xs = pltpu.roll(xe, KC4 - 1 - k, 0)[8:8 + TM]
            dcw_ref[k:k + 1, :] += jnp.sum(dxr * xs, axis=0, keepdims=True)
        dcb_ref[...] += jnp.sum(dxr, axis=0, keepdims=True)

        @pl.when(i == nt - 1)
        def _():
            for hh in range(NHEAD):
                for qc in range(NCHIP):
                    dwa_ref[qc, hh] = dwa_sc[hh, qc * hq:(qc + 1) * hq, :].astype(bf16)
                    dwx_ref[qc, hh] = dwx_sc[hh, qc * hq:(qc + 1) * hq, :].astype(bf16)
            dlam_ref[...] = -dsp_sc[...] * _sigmoid(-lam_ref[...])

    rev = lambda i: (nt - 1 - i, 0)
    vec = pl.BlockSpec((1, D), lambda i: (0, 0))
    prev8 = lambda i: jnp.maximum((nt - 1 - i) * ng - 1, 0)
    wblk = pl.BlockSpec((NHEAD, HD, HD), lambda i: (0, 0, 0))
    gblk = pl.BlockSpec((NCHIP, NHEAD, hq, HD), lambda i: (0, 0, 0, 0))
    return pl.pallas_call(
        body, name="rnn_bwd", grid=(nt,),
        in_specs=[
            pl.BlockSpec((TM, D), rev),
            pl.BlockSpec((TM, D), rev),
            pl.BlockSpec((TM, D), rev),
            pl.BlockSpec((8, D), lambda i: (prev8(i), 0)),
            pl.BlockSpec((TM, D), lambda i: (nt - 1 - i, 0)),
            pl.BlockSpec((8, D), lambda i: (prev8(i), 0)),
            pl.BlockSpec((TM, D), lambda i: (nt - 1 - i, 1)),
            pl.BlockSpec(memory_space=pl.ANY),
            pl.BlockSpec((KC4, D), lambda i: (0, 0)),
            wblk, vec, wblk, vec, vec,
        ],
        out_specs=[
            pl.BlockSpec((TM, 2 * D), lambda i: (nt - 1 - i, 0)),
            gblk, gblk,
            pl.BlockSpec((KC4, D), lambda i: (0, 0)),
            vec, vec, vec, vec,
        ],
        out_shape=[jax.ShapeDtypeStruct((t, NIN), bf16),
                   jax.ShapeDtypeStruct((NCHIP, NHEAD, hq, HD), bf16),
                   jax.ShapeDtypeStruct((NCHIP, NHEAD, hq, HD), bf16),
                   jax.ShapeDtypeStruct((KC4, D), f32),
                   jax.ShapeDtypeStruct((1, D), f32), jax.ShapeDtypeStruct((1, D), f32),
                   jax.ShapeDtypeStruct((1, D), f32), jax.ShapeDtypeStruct((1, D), f32)],
        scratch_shapes=[pltpu.VMEM((1, D), f32), pltpu.VMEM((1, D), f32),
                        pltpu.VMEM((TM + 8, D), f32), pltpu.VMEM((TM + 8, D), f32),
                        pltpu.VMEM((TM, D), f32), pltpu.VMEM((TM, D), f32),
                        pltpu.VMEM((NHEAD, HD, HD), f32), pltpu.VMEM((NHEAD, HD, HD), f32),
                        pltpu.VMEM((1, D), f32)],
        input_output_aliases={7: 0},
        compiler_params=_cp(1),
    )(dz, xr, hr, hr, proj, proj, proj, dproj, cw, wa, ba, wx, bx, lam)


def _inproj_bwd(dproj, dh, h, g, win):
    t = h.shape[0]
    tn = NIN // NCHIP
    nj = NIN // tn
    per = (NIN // NCHIP) // tn

    def body(dp_ref, dh_ref, h_ref, g_ref, w_ref, dhi_ref, dg_ref, db_ref, dn_sc):
        i = pl.program_id(0)
        j = pl.program_id(1)

        @pl.when(jnp.logical_and(i == 0, j == 0))
        def _():
            dg_ref[...] = jnp.zeros_like(dg_ref)
            db_ref[...] = jnp.zeros_like(db_ref)

        @pl.when(j == 0)
        def _():
            dn_sc[...] = jnp.zeros_like(dn_sc)

        dp = dp_ref[...]
        dn_sc[...] += _nt_dot(dp, w_ref[...])
        db_ref[j] += jnp.sum(dp.astype(f32), axis=0, keepdims=True)

        @pl.when(j == nj - 1)
        def _():
            dhin, dg = _rms_bwd(dn_sc[...], h_ref[...], g_ref[...])
            dhi_ref[...] = dh_ref[...] + dhin
            dg_ref[...] += dg

    rowd = pl.BlockSpec((TM, D), lambda i, j: (i, 0))
    vec = pl.BlockSpec((1, D), lambda i, j: (0, 0))
    return pl.pallas_call(
        body, name="inproj_bwd", grid=(t // TM, nj),
        in_specs=[pl.BlockSpec((TM, tn), lambda i, j: (i, j)), rowd, rowd, vec,
                  pl.BlockSpec((None, D, tn), lambda i, j: (j // per, 0, j % per))],
        out_specs=[rowd, vec, pl.BlockSpec((nj, 1, tn), lambda i, j: (0, 0, 0))],
        out_shape=[jax.ShapeDtypeStruct((t, D), f32), jax.ShapeDtypeStruct((1, D), f32),
                   jax.ShapeDtypeStruct((nj, 1, tn), f32)],
        scratch_shapes=[pltpu.VMEM((TM, D), f32)],
        compiler_params=_cp(2),
    )(dproj, dh, h, g, win)


def _ffn_gu_grad(n, dgate, dup, tag):
    halves = [
        _tn_matmul(n, part, D, FS, (2, D, FS), (None, D, FS), lambda k, nn, m: (nn, 0, 0), tag + which)
        for part, which in ((dgate, "_dwg"), (dup, "_dwu"))
    ]
    return jnp.concatenate(halves, axis=0)


def _ffn_down_grad(a, df, tag):
    return _tn_matmul(a, df, FS, D, (F, D), (FS, D), lambda k, nn, m: (k, 0), tag + "_dwd")


def _square_grad(a, b, name):
    return _tn_matmul(a, b, D, D, (D, D), (D, D), lambda k, nn, m: (0, 0), name)


ANY = pl.BlockSpec(memory_space=pl.ANY)


def _place():
    x, y, c = lax.axis_index("x"), lax.axis_index("y"), lax.axis_index("c")
    chips = [(1 - x, y), (x, 1 - y), (1 - x, 1 - y)]
    return x, y, c, chips


def _chip_id(chip):
    return 2 * chip[0] + chip[1]


def _cast_into_slot(w2d, qc, dtype, name):
    r, cc = w2d.shape
    hr = r // 2

    def body(qc_ref, w_ref, o_ref):
        del qc_ref
        o_ref[...] = w_ref[...].astype(dtype)

    return pl.pallas_call(
        body, name=name,
        grid_spec=pltpu.PrefetchScalarGridSpec(
            num_scalar_prefetch=1, grid=(2,),
            in_specs=[pl.BlockSpec((hr, cc), lambda h, qc_ref: (h, 0))],
            out_specs=pl.BlockSpec((None, None, hr, cc), lambda h, qc_ref: (qc_ref[0], h, 0, 0))),
        out_shape=jax.ShapeDtypeStruct((NCHIP, 2, hr, cc), dtype),
        compiler_params=_cp(1),
    )(qc, w2d)


def _gather_shards(bufs):
    n = len(bufs)

    def body(*refs):
        outs = refs[n:2 * n]
        send_sems, recv_sems = refs[2 * n:]
        x, y, c, chips = _place()
        q = 2 * x + y
        sibling = (x, y, 1 - c)

        def remote(a, k, blk, to):
            return pltpu.make_async_remote_copy(src_ref=blk, dst_ref=blk, send_sem=send_sems.at[a, k],
                                                recv_sem=recv_sems.at[a, k], device_id=to, device_id_type=MESH)

        sent = []
        for a in range(n):
            for j, chip in enumerate(chips):
                cp = remote(a, j, outs[a].at[q, c], (chip[0], chip[1], c))
                cp.start()
                sent.append(cp)
        for a in range(n):
            for j, chip in enumerate(chips):
                got = outs[a].at[_chip_id(chip), c]
                remote(a, j, got, (chip[0], chip[1], c)).wait_recv()
                cp = remote(a, 3 + j, got, sibling)
                cp.start()
                sent.append(cp)
        for a in range(n):
            for j, chip in enumerate(chips):
                remote(a, 3 + j, outs[a].at[_chip_id(chip), 1 - c], sibling).wait_recv()
        for cp in sent:
            cp.wait_send()

    return pl.pallas_call(
        body, name="gather_shards",
        in_specs=[ANY] * n, out_specs=[ANY] * n,
        out_shape=[jax.ShapeDtypeStruct(s.shape, s.dtype) for s in bufs],
        scratch_shapes=[pltpu.SemaphoreType.DMA((n, 6)), pltpu.SemaphoreType.DMA((n, 6))],
        input_output_aliases={a: a for a in range(n)},
    )(*bufs)


def _pair_exchange(parts, name):
    n = len(parts)

    def body(*refs):
        ins, outs = refs[:n], refs[n:2 * n]
        send_sems, recv_sems = refs[2 * n:]
        x, y, c, _ = _place()
        copies = []
        for a in range(n):
            cp = pltpu.make_async_remote_copy(
                src_ref=ins[a].at[:, 1 - c], dst_ref=outs[a],
                send_sem=send_sems.at[a], recv_sem=recv_sems.at[a],
                device_id=(x, y, 1 - c), device_id_type=MESH)
            cp.start()
            copies.append(cp)
        for cp in copies:
            cp.wait()

    return pl.pallas_call(
        body, name=name,
        in_specs=[ANY] * n, out_specs=[ANY] * n,
        out_shape=[jax.ShapeDtypeStruct((NCHIP,) + s.shape[2:], s.dtype) for s in parts],
        scratch_shapes=[pltpu.SemaphoreType.DMA((n,)), pltpu.SemaphoreType.DMA((n,))],
    )(*parts)


def _pair_add(part, got, qc, name):
    _, _, hr, cc = part.shape

    def body(qc_ref, p_ref, g_ref, o_ref, land_ref):
        s = pl.program_id(0)
        val = (p_ref[...].astype(f32) + g_ref[...].astype(f32)).astype(bf16)
        o_ref[...] = val

        @pl.when(s == qc_ref[0])
        def _():
            land_ref[...] = val

    return pl.pallas_call(
        body, name=name,
        grid_spec=pltpu.PrefetchScalarGridSpec(
            num_scalar_prefetch=1, grid=(NCHIP,),
            in_specs=[pl.BlockSpec((None, None, hr, cc), lambda s, qc_ref: (s, qc_ref[1], 0, 0)),
                      pl.BlockSpec((None, hr, cc), lambda s, qc_ref: (s, 0, 0))],
            out_specs=[pl.BlockSpec((None, hr, cc), lambda s, qc_ref: (s, 0, 0)),
                       pl.BlockSpec((None, hr, cc), lambda s, qc_ref: (qc_ref[0], 0, 0))]),
        out_shape=[jax.ShapeDtypeStruct((NCHIP, hr, cc), bf16)] * 2,
        compiler_params=_cp(1),
    )(qc, part, got)


def _chip_exchange(sums, lands):
    n = len(sums)

    def body(*refs):
        ins, outs = refs[:n], refs[2 * n:3 * n]
        send_sems, recv_sems = refs[3 * n:]
        x, y, c, chips = _place()
        q = 2 * x + y
        sent = []
        for a in range(n):
            for j, chip in enumerate(chips):
                cp = pltpu.make_async_remote_copy(
                    src_ref=ins[a].at[_chip_id(chip)], dst_ref=outs[a].at[q],
                    send_sem=send_sems.at[a, j], recv_sem=recv_sems.at[a, j],
                    device_id=(chip[0], chip[1], c), device_id_type=MESH)
                cp.start()
                sent.append(cp)
        for a in range(n):
            for j, chip in enumerate(chips):
                got = outs[a].at[_chip_id(chip)]
                pltpu.make_async_remote_copy(
                    src_ref=got, dst_ref=got, send_sem=send_sems.at[a, j], recv_sem=recv_sems.at[a, j],
                    device_id=(chip[0], chip[1], c), device_id_type=MESH).wait_recv()
        for cp in sent:
            cp.wait_send()

    return pl.pallas_call(
        body, name="chip_exchange",
        in_specs=[ANY] * (2 * n), out_specs=[ANY] * n,
        out_shape=[jax.ShapeDtypeStruct(s.shape, s.dtype) for s in lands],
        scratch_shapes=[pltpu.SemaphoreType.DMA((n, 3)), pltpu.SemaphoreType.DMA((n, 3))],
        input_output_aliases={n + a: a for a in range(n)},
    )(*sums, *lands)


def _sum_chips(got, name):
    _, hr, cc = got.shape

    def body(g_ref, o_ref):
        acc = g_ref[0].astype(f32)
        for s in range(1, NCHIP):
            acc = acc + g_ref[s].astype(f32)
        o_ref[...] = acc

    return pl.pallas_call(
        body, name=name, grid=(1,),
        in_specs=[pl.BlockSpec((NCHIP, hr, cc), lambda i: (0, 0, 0))],
        out_specs=pl.BlockSpec((hr, cc), lambda i: (0, 0)),
        out_shape=jax.ShapeDtypeStruct((hr, cc), f32),
        compiler_params=_cp(1),
    )(got)


def _pair_share(halves, name):
    n = len(halves)

    def body(*refs):
        ins, outs = refs[:n], refs[n:2 * n]
        send_sems, recv_sems = refs[2 * n:]
        x, y, c, _ = _place()
        copies = []
        for a in range(n):
            cp = pltpu.make_async_remote_copy(
                src_ref=ins[a], dst_ref=outs[a], send_sem=send_sems.at[a], recv_sem=recv_sems.at[a],
                device_id=(x, y, 1 - c), device_id_type=MESH)
            cp.start()
            copies.append(cp)
        for cp in copies:
            cp.wait()

    return pl.pallas_call(
        body, name=name,
        in_specs=[ANY] * n, out_specs=[ANY] * n,
        out_shape=[jax.ShapeDtypeStruct(s.shape, s.dtype) for s in halves],
        scratch_shapes=[pltpu.SemaphoreType.DMA((n,)), pltpu.SemaphoreType.DMA((n,))],
    )(*halves)


def _gather_all(pack):
    def body(in_ref, out_ref, send_sems, recv_sems, local_sem):
        x, y, c, _ = _place()
        me = 4 * x + 2 * y + c
        lc = pltpu.make_async_copy(in_ref, out_ref.at[me], local_sem)
        lc.start()
        sent = []

        def peer(k):
            return (1 - x if k & 4 else x, 1 - y if k & 2 else y, 1 - c if k & 1 else c)

        for k in range(1, 8):
            cp = pltpu.make_async_remote_copy(
                src_ref=in_ref, dst_ref=out_ref.at[me], send_sem=send_sems.at[k - 1],
                recv_sem=recv_sems.at[k - 1], device_id=peer(k), device_id_type=MESH)
            cp.start()
            sent.append(cp)
        for k in range(1, 8):
            px, py, pc = peer(k)
            got = out_ref.at[4 * px + 2 * py + pc]
            pltpu.make_async_remote_copy(
                src_ref=in_ref, dst_ref=got, send_sem=send_sems.at[k - 1], recv_sem=recv_sems.at[k - 1],
                device_id=(px, py, pc), device_id_type=MESH).wait_recv()
        for cp in sent:
            cp.wait_send()
        lc.wait()

    return pl.pallas_call(
        body, name="gather_all",
        in_specs=[ANY], out_specs=ANY,
        out_shape=jax.ShapeDtypeStruct((8,) + pack.shape, pack.dtype),
        scratch_shapes=[pltpu.SemaphoreType.DMA((7,)), pltpu.SemaphoreType.DMA((7,)),
                        pltpu.SemaphoreType.DMA(())],
    )(pack)


HBM = pl.BlockSpec(memory_space=pltpu.HBM)
SEM = pl.BlockSpec(memory_space=pltpu.SEMAPHORE)
EFFECT = pltpu.SideEffectType.DATAFLOW_SIDE_EFFECTING
N_PEER = 3


def _in_hbm(a):
    return pltpu.with_memory_space_constraint(a, pltpu.HBM)


def _tie(a, token):
    return lax.optimization_barrier((a, token))[0]


def _gather_copy(buf_ref, send_ref, recv_ref, j, chip, q, c, landing_chip):
    return pltpu.make_async_remote_copy(
        src_ref=buf_ref.at[q, c], dst_ref=buf_ref.at[landing_chip, c],
        send_sem=send_ref.at[j], recv_sem=recv_ref.at[j],
        device_id=(chip[0], chip[1], c), device_id_type=MESH)


def _gather_start(bufs, name):
    n = len(bufs)

    def body(*refs):
        ins = refs[:n]
        send, recv = refs[n:2 * n], refs[2 * n:3 * n]
        token = refs[4 * n]
        x, y, c, chips = _place()
        q = 2 * x + y
        for a in range(n):
            for j, chip in enumerate(chips):
                _gather_copy(ins[a], send[a], recv[a], j, chip, q, c, q).start()
        token[...] = jnp.zeros_like(token)

    sems = [pltpu.SemaphoreType.DMA((N_PEER,))] * (2 * n)
    outs = pl.pallas_call(
        body, name=name,
        in_specs=[HBM] * n,
        out_specs=[SEM] * (2 * n) + [HBM] * n + [pl.BlockSpec(memory_space=pltpu.VMEM)],
        out_shape=sems + [pltpu.HBM(b.shape, b.dtype) for b in bufs] + [jax.ShapeDtypeStruct((8, 128), f32)],
        input_output_aliases={a: 2 * n + a for a in range(n)},
        compiler_params=pltpu.CompilerParams(has_side_effects=EFFECT),
    )(*[_in_hbm(b) for b in bufs])
    return list(outs[:n]), list(outs[n:2 * n]), list(outs[2 * n:3 * n]), outs[3 * n]


def _gather_wait(send, recv, bufs, after, name):
    n = len(bufs)

    def body(*refs):
        ins = refs[:n]
        send_r, recv_r = refs[n:2 * n], refs[2 * n:3 * n]
        x, y, c, chips = _place()
        q = 2 * x + y
        for a in range(n):
            for j, chip in enumerate(chips):
                cp = _gather_copy(ins[a], send_r[a], recv_r[a], j, chip, q, c, _chip_id(chip))
                cp.wait_send()
                cp.wait_recv()

    outs = pl.pallas_call(
        body, name=name,
        in_specs=[HBM] * n + [SEM] * (2 * n) + [ANY],
        out_specs=[HBM] * n,
        out_shape=[pltpu.HBM(b.shape, b.dtype) for b in bufs],
        input_output_aliases={a: a for a in range(n)},
        compiler_params=pltpu.CompilerParams(has_side_effects=EFFECT),
    )(*bufs, *send, *recv, after)
    return list(outs)


def _forward_halves(bufs, name):
    n = len(bufs)

    def body(*refs):
        outs = refs[n:2 * n]
        send_sems, recv_sems = refs[2 * n:]
        x, y, c, chips = _place()
        sibling = (x, y, 1 - c)

        def remote(a, j, blk):
            return pltpu.make_async_remote_copy(src_ref=blk, dst_ref=blk, send_sem=send_sems.at[a, j],
                                                recv_sem=recv_sems.at[a, j], device_id=sibling,
                                                device_id_type=MESH)

        sent = []
        for a in range(n):
            for j, chip in enumerate(chips):
                cp = remote(a, j, outs[a].at[_chip_id(chip), c])
                cp.start()
                sent.append(cp)
        for a in range(n):
            for j, chip in enumerate(chips):
                remote(a, j, outs[a].at[_chip_id(chip), 1 - c]).wait_recv()
        for cp in sent:
            cp.wait_send()

    return pl.pallas_call(
        body, name=name,
        in_specs=[ANY] * n, out_specs=[ANY] * n,
        out_shape=[jax.ShapeDtypeStruct(s.shape, s.dtype) for s in bufs],
        scratch_shapes=[pltpu.SemaphoreType.DMA((n, N_PEER)), pltpu.SemaphoreType.DMA((n, N_PEER))],
        input_output_aliases={a: a for a in range(n)},
    )(*bufs)


def _reduce_copy(sum_ref, land_ref, send_ref, recv_ref, j, chip, q, c, landing_chip):
    return pltpu.make_async_remote_copy(
        src_ref=sum_ref.at[_chip_id(chip)], dst_ref=land_ref.at[landing_chip],
        send_sem=send_ref.at[j], recv_sem=recv_ref.at[j],
        device_id=(chip[0], chip[1], c), device_id_type=MESH)


def _reduce_start(sums, lands, name):
    n = len(sums)

    def body(*refs):
        s_in, l_in = refs[:n], refs[n:2 * n]
        send, recv = refs[2 * n:3 * n], refs[3 * n:4 * n]
        token = refs[6 * n]
        x, y, c, chips = _place()
        q = 2 * x + y
        for a in range(n):
            for j, chip in enumerate(chips):
                _reduce_copy(s_in[a], l_in[a], send[a], recv[a], j, chip, q, c, q).start()
        token[...] = jnp.zeros_like(token)

    sems = [pltpu.SemaphoreType.DMA((N_PEER,))] * (2 * n)
    outs = pl.pallas_call(
        body, name=name,
        in_specs=[HBM] * (2 * n),
        out_specs=[SEM] * (2 * n) + [HBM] * (2 * n) + [pl.BlockSpec(memory_space=pltpu.VMEM)],
        out_shape=sems + [pltpu.HBM(b.shape, b.dtype) for b in list(sums) + list(lands)]
        + [jax.ShapeDtypeStruct((8, 128), f32)],
        input_output_aliases={a: 2 * n + a for a in range(2 * n)},
        compiler_params=pltpu.CompilerParams(has_side_effects=EFFECT),
    )(*[_in_hbm(b) for b in list(sums) + list(lands)])
    return (list(outs[:n]), list(outs[n:2 * n]), list(outs[2 * n:3 * n]), list(outs[3 * n:4 * n]),
            outs[4 * n])


def _reduce_wait(send, recv, sums, lands, after, name):
    n = len(sums)

    def body(*refs):
        s_in, l_in = refs[:n], refs[n:2 * n]
        send_r, recv_r = refs[2 * n:3 * n], refs[3 * n:4 * n]
        x, y, c, chips = _place()
        q = 2 * x + y
        for a in range(n):
            for j, chip in enumerate(chips):
                cp = _reduce_copy(s_in[a], l_in[a], send_r[a], recv_r[a], j, chip, q, c, _chip_id(chip))
                cp.wait_send()
                cp.wait_recv()

    outs = pl.pallas_call(
        body, name=name,
        in_specs=[HBM] * (2 * n) + [SEM] * (2 * n) + [ANY],
        out_specs=[HBM] * (2 * n),
        out_shape=[pltpu.HBM(b.shape, b.dtype) for b in list(sums) + list(lands)],
        input_output_aliases={a: a for a in range(2 * n)},
        compiler_params=pltpu.CompilerParams(has_side_effects=EFFECT),
    )(*sums, *lands, *send, *recv, after)
    return list(outs[n:])


def _adamw_math(w, g, m, v):
    m = ADAM_B1 * m + (1.0 - ADAM_B1) * g
    v = ADAM_B2 * v + (1.0 - ADAM_B2) * (g * g)
    m_hat = m / (1.0 - ADAM_B1 ** ADAM_STEP)
    v_hat = v / (1.0 - ADAM_B2 ** ADAM_STEP)
    delta = -ADAM_LR * (m_hat / (jnp.sqrt(v_hat) + ADAM_EPS) + ADAM_WD * w)
    return delta, m, v


def _adamw(w, mine, theirs, m, v, qc, name):
    r, cc = w.shape
    hr = r // 2
    tr = next(hr // k for k in range(1, hr + 1)
              if hr % k == 0 and (hr // k) % 8 == 0 and (hr // k) * cc * 4 <= (1 << 20))
    nb = hr // tr

    def body(qc_ref, w_ref, a_ref, b_ref, m_ref, v_ref, g_ref, d_ref, mo_ref, vo_ref):
        g = jnp.where(pl.program_id(0) == qc_ref[1], a_ref[...], b_ref[...])
        g_ref[...] = g
        d_ref[...], mo_ref[...], vo_ref[...] = _adamw_math(w_ref[...], g, m_ref[...], v_ref[...])

    full = pl.BlockSpec((tr, cc), lambda h, i, qc_ref: (h * nb + i, 0))
    half = pl.BlockSpec((tr, cc), lambda h, i, qc_ref: (i, 0))
    return pl.pallas_call(
        body, name=name,
        grid_spec=pltpu.PrefetchScalarGridSpec(
            num_scalar_prefetch=1, grid=(2, nb),
            in_specs=[full, half, half, full, full], out_specs=[full] * 4),
        out_shape=[jax.ShapeDtypeStruct((r, cc), f32)] * 4,
        compiler_params=_cp(2),
    )(qc, w, mine, theirs, m, v)


REPL = [("ffn1_norm", 1), ("mix_norm", 1), ("b_in", 6), ("rnn_conv_b", 1), ("rg_b_a", 1), ("rg_b_x", 1),
        ("rg_lambda", 1), ("conv_dw_b", 1), ("conv_ln_g", 1), ("conv_ln_b", 1), ("conv_b_proj", 1),
        ("ffn2_norm", 1), ("final_norm", 1)]
COLSH = [("meta_tokens", NMETA), ("rnn_conv_w", KC4), ("conv_dw_w", KC31)]
SMALL = REPL + COLSH
CS = D // NCHIP


def _pack_rows():
    starts, row = {}, 0
    for k, rows in REPL:
        starts[k] = row
        row += rows
    for k, rows in COLSH:
        row = -(-row // 8) * 8
        starts[k] = row
        row += rows
    return starts, -(-row // 8) * 8


PACK_START, SMALL_ROWS = _pack_rows()


def _small_pack(g):
    pieces, row = [], 0
    for k, rows in SMALL:
        if PACK_START[k] > row:
            pieces.append(jnp.zeros((PACK_START[k] - row, D), f32))
        pieces.append(g[k].reshape(rows, D))
        row = PACK_START[k] + rows
    pieces.append(jnp.zeros((SMALL_ROWS - row, D), f32))
    return jnp.concatenate(pieces, axis=0)


def _adamw_small(packs, ws, ms, vs):
    ns = len(SMALL)

    def body(*refs):
        pack_ref = refs[0]
        w_refs, m_refs, v_refs = refs[1:1 + ns], refs[1 + ns:1 + 2 * ns], refs[1 + 2 * ns:1 + 3 * ns]
        outs = refs[1 + 3 * ns:1 + 7 * ns]
        g_refs, d_refs, mo_refs, vo_refs = outs[:ns], outs[ns:2 * ns], outs[2 * ns:3 * ns], outs[3 * ns:]
        gsum_sc = refs[1 + 7 * ns]
        q = 2 * lax.axis_index("x") + lax.axis_index("y")
        acc = pack_ref[0]
        for dev in range(1, 8):
            acc = acc + pack_ref[dev]
        gsum_sc[...] = acc
        for idx, (name, rows) in enumerate(SMALL):
            row = PACK_START[name]
            if idx < len(REPL):
                for k in range(rows):
                    cols = slice(k * D, (k + 1) * D)
                    g = gsum_sc[row + k:row + k + 1, :]
                    d, mm, vv = _adamw_math(w_refs[idx][:, cols], g, m_refs[idx][:, cols], v_refs[idx][:, cols])
                    g_refs[idx][:, cols] = g
                    d_refs[idx][:, cols] = d
                    mo_refs[idx][:, cols] = mm
                    vo_refs[idx][:, cols] = vv
            else:
                g = gsum_sc[row:row + rows, pl.ds(pl.multiple_of(q * CS, CS), CS)]
                d, mm, vv = _adamw_math(w_refs[idx][...], g, m_refs[idx][...], v_refs[idx][...])
                g_refs[idx][...] = g
                d_refs[idx][...] = d
                mo_refs[idx][...] = mm
                vo_refs[idx][...] = vv

    shapes = [jax.ShapeDtypeStruct(w.shape, f32) for w in ws]
    return pl.pallas_call(
        body, name="adamw_small",
        out_shape=shapes * 4,
        scratch_shapes=[pltpu.VMEM((SMALL_ROWS, D), f32)],
        compiler_params=pltpu.CompilerParams(vmem_limit_bytes=VMEM_LIMIT),
    )(packs, *ws, *ms, *vs)


BIG = ["ffn1_w_gu", "ffn1_w_down", "w_in", "rg_w_a", "rg_w_x", "rnn_w_proj", "conv_w_proj", "w_out",
       "ffn2_w_gu", "ffn2_w_down"]
WEIGHTS = ['meta_tokens', 'ffn1_norm', 'ffn1_w_gu', 'ffn1_w_down', 'mix_norm', 'w_in', 'b_in', 'rnn_conv_w',
           'rnn_conv_b', 'rg_w_a', 'rg_b_a', 'rg_w_x', 'rg_b_x', 'rg_lambda', 'rnn_w_proj', 'conv_dw_w',
           'conv_dw_b', 'conv_ln_g', 'conv_ln_b', 'conv_w_proj', 'conv_b_proj', 'w_out', 'ffn2_norm',
           'ffn2_w_gu', 'ffn2_w_down', 'final_norm']


def _as2d(a):
    return a.reshape(-1, a.shape[-1])


def _step(x, loss_target, w, m, v):
    seq = x.shape[1]
    n_valid = NMETA + seq
    t = -(-n_valid // TM) * TM

    qc = jnp.stack([2 * lax.axis_index("x") + lax.axis_index("y"), lax.axis_index("c")]).astype(jnp.int32)
    p = {k: w[k].reshape(1, rows * D) for k, rows in REPL}

    first = ["ffn1_w_gu", "ffn1_w_down", "small"]
    later = [["w_in"], ["rg_w_a", "rg_w_x", "rnn_w_proj", "conv_w_proj", "w_out"], ["ffn2_w_gu", "ffn2_w_down"]]
    small_rows = sum(r for _, r in COLSH)
    small = jnp.concatenate([_as2d(w[k]) for k, _ in COLSH] + [jnp.zeros((64 - small_rows, CS), f32)], axis=0)

    def cast(k, token=None):
        src, dtype = (small, f32) if k == "small" else (_as2d(w[k]), bf16)
        if token is not None:
            src = _tie(src, token)
        return _cast_into_slot(src, qc, dtype, "cast_" + k)

    send1, recv1, bufs1, token1 = _gather_start([cast(k) for k in first], "gather_start_first")
    rest = [k for grp in later for k in grp]
    send2, recv2, bufs2, token2 = _gather_start([cast(k, token1) for k in rest], "gather_start_rest")

    def finish(names, send, recv, bufs, after, tag):
        done = _forward_halves(_gather_wait(send, recv, bufs, after, "gather_wait_" + tag), "gather_forward_" + tag)
        for k, b in zip(names, done):
            full = b.reshape(NCHIP, 2 * b.shape[2], b.shape[3])
            if k in ("ffn1_w_down", "ffn2_w_down"):
                full = full.reshape(F, D)
            elif k in ("rnn_w_proj", "conv_w_proj", "w_out"):
                full = full.reshape(D, D)
            elif k in ("rg_w_a", "rg_w_x"):
                full = full.reshape(NCHIP, NHEAD, HD // NCHIP, HD).transpose(1, 0, 2, 3).reshape(NHEAD, HD, HD)
            p[k] = full

    def group(names):
        idx = [rest.index(k) for k in names]
        return names, [send2[i] for i in idx], [recv2[i] for i in idx], [bufs2[i] for i in idx]

    finish(first, send1, recv1, bufs1, token2, "first")
    small_full = p.pop("small").transpose(1, 0, 2).reshape(64, D)
    row = 0
    for k, rows in COLSH:
        p[k] = small_full[row:row + rows]
        row += rows

    h0 = jnp.concatenate([p["meta_tokens"], x[0], jnp.zeros((t - n_valid, D), f32)], axis=0)
    tgt = jnp.concatenate([jnp.zeros((NMETA, D), f32), loss_target[0], jnp.zeros((t - n_valid, D), f32)], axis=0)
    h1, gate1, up1, n1 = _ffn_fwd(h0, p["ffn1_norm"], p["ffn1_w_gu"], p["ffn1_w_down"], "ffn1_fwd")
    finish(*group(later[0]), h1, "in")
    proj, n2 = _inproj_fwd(h1, p["mix_norm"], p["w_in"], p["b_in"])
    finish(*group(later[1]), proj, "mix")
    xr, hr, z = _rnn_fwd(proj, p["rnn_conv_w"], p["rnn_conv_b"], p["rg_w_a"], p["rg_b_a"],
                         p["rg_w_x"], p["rg_b_x"], p["rg_lambda"])
    vc, s = _conv_fwd(proj, p["conv_dw_w"], p["conv_dw_b"], p["conv_ln_g"], p["conv_ln_b"])
    h2 = _merge_fwd(h1, z, s, proj, p["rnn_w_proj"], p["conv_w_proj"], p["conv_b_proj"], p["w_out"])
    finish(*group(later[2]), h2, "ffn2")
    h3, gate2, up2, n3 = _ffn_fwd(h2, p["ffn2_norm"], p["ffn2_w_gu"], p["ffn2_w_down"], "ffn2_fwd")
    dh3, loss_blk, d_final = _final_loss(h3, p["final_norm"], tgt, n_valid)
    loss = lax.psum(loss_blk[0, 0], ("x", "y", "c"))

    g = {"final_norm": d_final}
    pending = []

    def reduce_start(names, tag):
        parts = []
        for k in names:
            rows = g[k].size // (NCHIP * g[k].shape[-1])
            parts.append(g[k].reshape((NCHIP, 2, rows // 2, g[k].shape[-1])))
        from_sibling = _pair_exchange(parts, "pair_exchange_" + tag)
        added = [_pair_add(pp, gg, qc, "pair_add_" + k) for pp, gg, k in zip(parts, from_sibling, names)]
        send, recv, sums, lands, token = _reduce_start([a for a, _ in added], [b for _, b in added],
                                                       "reduce_start_" + tag)
        pending.append((names, tag, send, recv, sums, lands))
        return token

    dh2, dgate2, dup2, a2, df2, g["ffn2_norm"] = _ffn_bwd(
        dh3, h2, p["ffn2_norm"], gate2, up2, p["ffn2_w_gu"], p["ffn2_w_down"], "ffn2_bwd")
    g["ffn2_w_gu"] = _ffn_gu_grad(n3, dgate2, dup2, "ffn2")
    g["ffn2_w_down"] = _ffn_down_grad(a2, df2, "ffn2")
    token = reduce_start(["ffn2_w_gu", "ffn2_w_down"], "ffn2")

    dz, ds, dproj, dh2b, merged, dya, dyb, g["conv_b_proj"] = _merge_bwd(
        _tie(dh2, token), z, s, proj, p["rnn_w_proj"], p["conv_w_proj"], p["conv_b_proj"], p["w_out"])
    g["w_out"] = _square_grad(merged, dh2b, "dw_out")
    g["rnn_w_proj"] = _square_grad(z, dya, "dw_rnn_proj")
    g["conv_w_proj"] = _square_grad(s, dyb, "dw_conv_proj")
    dproj, g["conv_dw_w"], g["conv_dw_b"], g["conv_ln_g"], g["conv_ln_b"] = _conv_bwd(
        ds, vc, proj, dproj, p["conv_dw_w"], p["conv_ln_g"], p["conv_ln_b"])
    (dproj, g["rg_w_a"], g["rg_w_x"], g["rnn_conv_w"], g["rnn_conv_b"], g["rg_b_a"], g["rg_b_x"],
     g["rg_lambda"]) = _rnn_bwd(dz, xr, hr, proj, dproj, p["rnn_conv_w"], p["rg_w_a"], p["rg_b_a"],
                                p["rg_w_x"], p["rg_b_x"], p["rg_lambda"])
    token = reduce_start(["w_out", "rnn_w_proj", "conv_w_proj", "rg_w_a", "rg_w_x"], "mix")

    dh1, g["mix_norm"], db_in = _inproj_bwd(_tie(dproj, token), dh2, h1, p["mix_norm"], p["w_in"])
    g["b_in"] = db_in.reshape(1, NIN)
    g["w_in"] = _tn_matmul(n2, dproj, D, NIN // NCHIP, (NCHIP, D, NIN // NCHIP),
                           (None, D, NIN // NCHIP), lambda k, nn, mm: (nn, 0, 0), "dw_in")
    token = reduce_start(["w_in"], "in")

    dh0, dgate1, dup1, a1, df1, g["ffn1_norm"] = _ffn_bwd(
        _tie(dh1, token), h0, p["ffn1_norm"], gate1, up1, p["ffn1_w_gu"], p["ffn1_w_down"], "ffn1_bwd")
    g["ffn1_w_gu"] = _ffn_gu_grad(n1, dgate1, dup1, "ffn1")
    token = reduce_start(["ffn1_w_gu"], "ffn1_gu")
    g["ffn1_w_down"] = _ffn_down_grad(_tie(a1, token), df1, "ffn1")
    token = reduce_start(["ffn1_w_down"], "ffn1_down")
    g["meta_tokens"] = dh0[0:NMETA]
    grad_x = dh0[NMETA:n_valid][None]

    packs = _gather_all(_tie(_small_pack(g), token))

    grads, deltas, new_m, new_v = {}, {}, {}, {}

    def reduce_finish(items, after, tag):
        names, mine = [], []
        for grp_names, grp_tag, send, recv, sums, lands in items:
            landed = _reduce_wait(send, recv, sums, lands, after, "reduce_wait_" + grp_tag)
            mine += [_sum_chips(b, "sum_chips_" + k) for b, k in zip(landed, grp_names)]
            names += grp_names
            after = mine[-1]
        theirs = _pair_share(mine, "pair_share_" + tag)
        for k, mi, th in zip(names, mine, theirs):
            outs = _adamw(_as2d(w[k]), mi, th, _as2d(m[k]), _as2d(v[k]), qc, "adamw_" + k)
            grads[k], deltas[k], new_m[k], new_v[k] = (a.reshape(w[k].shape) for a in outs)
        return new_v[names[-1]]

    after = reduce_finish(pending[:3], packs, "early")
    reduce_finish(pending[3:], after, "late")
    names = [k for k, _ in SMALL]
    shape2 = {k: ((1, rows * D) if (k, rows) in REPL else (rows, CS)) for k, rows in SMALL}
    outs = _adamw_small(packs, *[[a[k].reshape(shape2[k]) for k in names] for a in (w, m, v)])
    ns = len(names)
    for i, k in enumerate(names):
        grads[k], deltas[k], new_m[k], new_v[k] = (outs[j * ns + i].reshape(w[k].shape) for j in range(4))

    return (loss, grad_x, *[grads[k] for k in WEIGHTS], *[deltas[k] for k in WEIGHTS],
            *[new_m[k] for k in WEIGHTS], *[new_v[k] for k in WEIGHTS])


def kernel(x, meta_tokens, ffn1_norm, ffn1_w_gu, ffn1_w_down, mix_norm, w_in, b_in, rnn_conv_w, rnn_conv_b, rg_w_a, rg_b_a, rg_w_x, rg_b_x, rg_lambda, rnn_w_proj, conv_dw_w, conv_dw_b, conv_ln_g, conv_ln_b, conv_w_proj, conv_b_proj, w_out, ffn2_norm, ffn2_w_gu, ffn2_w_down, final_norm, loss_target, m_meta_tokens, m_ffn1_norm, m_ffn1_w_gu, m_ffn1_w_down, m_mix_norm, m_w_in, m_b_in, m_rnn_conv_w, m_rnn_conv_b, m_rg_w_a, m_rg_b_a, m_rg_w_x, m_rg_b_x, m_rg_lambda, m_rnn_w_proj, m_conv_dw_w, m_conv_dw_b, m_conv_ln_g, m_conv_ln_b, m_conv_w_proj, m_conv_b_proj, m_w_out, m_ffn2_norm, m_ffn2_w_gu, m_ffn2_w_down, m_final_norm, v_meta_tokens, v_ffn1_norm, v_ffn1_w_gu, v_ffn1_w_down, v_mix_norm, v_w_in, v_b_in, v_rnn_conv_w, v_rnn_conv_b, v_rg_w_a, v_rg_b_a, v_rg_w_x, v_rg_b_x, v_rg_lambda, v_rnn_w_proj, v_conv_dw_w, v_conv_dw_b, v_conv_ln_g, v_conv_ln_b, v_conv_w_proj, v_conv_b_proj, v_w_out, v_ffn2_norm, v_ffn2_w_gu, v_ffn2_w_down, v_final_norm):
    args = locals()
    w = {k: args[k] for k in WEIGHTS}
    m = {k: args["m_" + k] for k in WEIGHTS}
    v = {k: args["v_" + k] for k in WEIGHTS}
    return _step(x, loss_target, w, m, v)
```

```python
import functools

import jax
import jax.numpy as jnp
from jax import lax
from jax.experimental import pallas as pl
from jax.experimental.pallas import tpu as pltpu

f32 = jnp.float32
bf16 = jnp.bfloat16

D = 1024
F = 2816
FS = F // 2
NIN = 6 * D
NMETA = 16
NHEAD = 4
HD = D // NHEAD
KC4 = 4
KC31 = 31
HALO = 32
EPS = 1e-6
TM = 384
NCHIP = 4
MESH = pl.DeviceIdType.MESH

ADAM_LR = 0.001
ADAM_B1 = 0.9
ADAM_B2 = 0.999
ADAM_EPS = 1e-08
ADAM_WD = 0.01
ADAM_STEP = 10

VMEM_LIMIT = 56 * 1024 * 1024


def _cp(n_axes, **kw):
    return pltpu.CompilerParams(dimension_semantics=("arbitrary",) * n_axes,
                                vmem_limit_bytes=VMEM_LIMIT, **kw)


def _ordered(body, in_specs, args, after):
    if after is None:
        return body, in_specs, args
    return (lambda first, *refs: body(*refs),
            [pl.BlockSpec(memory_space=pl.ANY)] + list(in_specs), (after,) + tuple(args))


def _nt_dot(a, b):
    return lax.dot_general(a, b, (((1,), (1,)), ((), ())), preferred_element_type=f32)


def _tn_dot(a, b):
    return lax.dot_general(a, b, (((0,), (0,)), ((), ())), preferred_element_type=f32)


def _sigmoid(x):
    return 1.0 / (1.0 + jnp.exp(-x))


def _log1p(y):
    u = 1.0 + y
    d = u - 1.0
    return jnp.where(d == 0.0, y, jnp.log(u) * (y / jnp.where(d == 0.0, 1.0, d)))


def _softplus(x):
    return jnp.maximum(x, 0.0) + _log1p(jnp.exp(-jnp.abs(x)))


def _expm1(x):
    series = x * (1.0 + x * (0.5 + x * (1.0 / 6.0 + x * (1.0 / 24.0 + x * (1.0 / 120.0)))))
    return jnp.where(jnp.abs(x) < 0.1, series, jnp.exp(x) - 1.0)


_GELU_C = 0.7978845608028654
_GELU_K = 0.044715


def _gelu_and_grad(y):
    y2 = y * y
    th = jnp.tanh(_GELU_C * (y + _GELU_K * y * y2))
    gel = 0.5 * y * (1.0 + th)
    dgel = 0.5 * (1.0 + th) + 0.5 * y * (1.0 - th * th) * _GELU_C * (1.0 + 3.0 * _GELU_K * y2)
    return gel, dgel


def _rms_stats(h):
    return lax.rsqrt(jnp.mean(h * h, axis=-1, keepdims=True) + EPS)


def _rms_bwd(dn, h, g):
    r = _rms_stats(h)
    nhat = h * r
    dnh = dn * g
    dh = r * (dnh - nhat * jnp.mean(dnh * nhat, axis=-1, keepdims=True))
    dg = jnp.sum(dn * nhat, axis=0, keepdims=True)
    return dh, dg


def _row_ids(shape):
    return lax.broadcasted_iota(jnp.int32, shape, 0)


def _ffn_fwd(h, g, wgu, wd, name):
    t = h.shape[0]
    nj = 2

    def body(h_ref, g_ref, wg_ref, wu_ref, wd_ref, ho_ref, gate_ref, up_ref, n_ref, nb_sc, acc_sc):
        j = pl.program_id(1)

        @pl.when(j == 0)
        def _():
            hh = h_ref[...]
            nb = (hh * _rms_stats(hh) * g_ref[...]).astype(bf16)
            nb_sc[...] = nb
            n_ref[...] = nb
            acc_sc[...] = jnp.zeros_like(acc_sc)

        nb = nb_sc[...]
        gt = jnp.dot(nb, wg_ref[...], preferred_element_type=f32)
        up = jnp.dot(nb, wu_ref[...], preferred_element_type=f32)
        gate_ref[...] = gt.astype(bf16)
        up_ref[...] = up.astype(bf16)
        a = (gt * _sigmoid(gt) * up).astype(bf16)
        acc_sc[...] += jnp.dot(a, wd_ref[...], preferred_element_type=f32)

        @pl.when(j == nj - 1)
        def _():
            ho_ref[...] = h_ref[...] + 0.5 * acc_sc[...]

    return pl.pallas_call(
        body, name=name, grid=(t // TM, nj),
        in_specs=[
            pl.BlockSpec((TM, D), lambda i, j: (i, 0)),
            pl.BlockSpec((1, D), lambda i, j: (0, 0)),
            pl.BlockSpec((None, D, FS), lambda i, j: (j, 0, 0)),
            pl.BlockSpec((None, D, FS), lambda i, j: (2 + j, 0, 0)),
            pl.BlockSpec((FS, D), lambda i, j: (j, 0)),
        ],
        out_specs=[
            pl.BlockSpec((TM, D), lambda i, j: (i, 0)),
            pl.BlockSpec((TM, FS), lambda i, j: (i, j)),
            pl.BlockSpec((TM, FS), lambda i, j: (i, j)),
            pl.BlockSpec((TM, D), lambda i, j: (i, 0)),
        ],
        out_shape=[
            jax.ShapeDtypeStruct((t, D), f32),
            jax.ShapeDtypeStruct((t, F), bf16),
            jax.ShapeDtypeStruct((t, F), bf16),
            jax.ShapeDtypeStruct((t, D), bf16),
        ],
        scratch_shapes=[pltpu.VMEM((TM, D), bf16), pltpu.VMEM((TM, D), f32)],
        compiler_params=_cp(2),
    )(h, g, wgu, wgu, wd)


def _inproj_fwd(h, g, win, b_in):
    t = h.shape[0]
    tn = NIN // NCHIP
    nj = NIN // tn
    per = (NIN // NCHIP) // tn

    def body(h_ref, g_ref, w_ref, b_ref, proj_ref, n_ref, nb_sc):
        j = pl.program_id(1)

        @pl.when(j == 0)
        def _():
            hh = h_ref[...]
            nb = (hh * _rms_stats(hh) * g_ref[...]).astype(bf16)
            nb_sc[...] = nb
            n_ref[...] = nb

        proj_ref[...] = jnp.dot(nb_sc[...], w_ref[...], preferred_element_type=f32) + b_ref[...]

    return pl.pallas_call(
        body, name="inproj_fwd", grid=(t // TM, nj),
        in_specs=[
            pl.BlockSpec((TM, D), lambda i, j: (i, 0)),
            pl.BlockSpec((1, D), lambda i, j: (0, 0)),
            pl.BlockSpec((None, D, tn), lambda i, j: (j // per, 0, j % per)),
            pl.BlockSpec((1, tn), lambda i, j: (0, j)),
        ],
        out_specs=[
            pl.BlockSpec((TM, tn), lambda i, j: (i, j)),
            pl.BlockSpec((TM, D), lambda i, j: (i, 0)),
        ],
        out_shape=[jax.ShapeDtypeStruct((t, NIN), f32), jax.ShapeDtypeStruct((t, D), bf16)],
        scratch_shapes=[pltpu.VMEM((TM, D), bf16)],
        compiler_params=_cp(2),
    )(h, g, win, b_in)


def _block_gates(xr, wa_ref, ba, wx_ref, bx, lam):
    xrb = xr.astype(bf16)
    pa = jnp.concatenate([jnp.dot(xrb[:, hh * HD:(hh + 1) * HD], wa_ref[hh], preferred_element_type=f32)
                          for hh in range(NHEAD)], axis=1)
    px = jnp.concatenate([jnp.dot(xrb[:, hh * HD:(hh + 1) * HD], wx_ref[hh], preferred_element_type=f32)
                          for hh in range(NHEAD)], axis=1)
    ra = _sigmoid(pa + ba)
    ii = _sigmoid(px + bx)
    sp = _softplus(-lam)
    log_a = -8.0 * ra * sp
    a = jnp.exp(log_a)
    sq = jnp.sqrt(-_expm1(2.0 * log_a))
    return ra, ii, a, sq, sp


def _rnn_fwd(proj, cw, cb, wa, ba, wx, bx, lam):
    t = proj.shape[0]
    ng = TM // 8

    def body(x_ref, y_ref, cw_ref, cb_ref, wa_ref, ba_ref, wx_ref, bx_ref, lam_ref,
             xr_ref, hr_ref, z_ref, xext_sc, carry_sc, a_sc, h_sc):
        i = pl.program_id(0)

        @pl.when(i == 0)
        def _():
            xext_sc[0:8, :] = jnp.zeros((8, D), f32)
            carry_sc[...] = jnp.zeros_like(carry_sc)

        x = x_ref[...]
        xext_sc[8:8 + TM, :] = x
        xe = xext_sc[...]
        xr = cb_ref[...] + cw_ref[KC4 - 1:KC4, :] * x
        for k in range(KC4 - 1):
            xr = xr + cw_ref[k:k + 1, :] * pltpu.roll(xe, KC4 - 1 - k, 0)[8:8 + TM]
        xext_sc[0:8, :] = x[TM - 8:TM]

        _, ii, a, sq, _ = _block_gates(xr, wa_ref, ba_ref[...], wx_ref, bx_ref[...], lam_ref[...])
        a_sc[...] = a
        h_sc[...] = sq * ii * xr
        row = _row_ids((8, D))

        def group(r, carry):
            off = pl.multiple_of(r * 8, 8)
            aa = a_sc[pl.ds(off, 8), :]
            hh = h_sc[pl.ds(off, 8), :]
            for s in (1, 2, 4):
                a_sh = jnp.where(row >= s, pltpu.roll(aa, s, 0), 1.0)
                h_sh = jnp.where(row >= s, pltpu.roll(hh, s, 0), 0.0)
                hh = aa * h_sh + hh
                aa = aa * a_sh
            hh = hh + aa * carry
            h_sc[pl.ds(off, 8), :] = hh
            return hh[7:8, :]

        carry_sc[...] = lax.fori_loop(0, ng, group, carry_sc[...])
        hr = h_sc[...]
        gel, _ = _gelu_and_grad(y_ref[...])
        xr_ref[...] = xr
        hr_ref[...] = hr
        z_ref[...] = (hr * gel).astype(bf16)

    vec = pl.BlockSpec((1, D), lambda i: (0, 0))
    return pl.pallas_call(
        body, name="rnn_fwd", grid=(t // TM,),
        in_specs=[
            pl.BlockSpec((TM, D), lambda i: (i, 0)),
            pl.BlockSpec((TM, D), lambda i: (i, 1)),
            pl.BlockSpec((KC4, D), lambda i: (0, 0)),
            vec,
            pl.BlockSpec((NHEAD, HD, HD), lambda i: (0, 0, 0)),
            vec,
            pl.BlockSpec((NHEAD, HD, HD), lambda i: (0, 0, 0)),
            vec, vec,
        ],
        out_specs=[pl.BlockSpec((TM, D), lambda i: (i, 0))] * 3,
        out_shape=[jax.ShapeDtypeStruct((t, D), f32), jax.ShapeDtypeStruct((t, D), f32),
                   jax.ShapeDtypeStruct((t, D), bf16)],
        scratch_shapes=[pltpu.VMEM((TM + 8, D), f32), pltpu.VMEM((1, D), f32),
                        pltpu.VMEM((TM, D), f32), pltpu.VMEM((TM, D), f32)],
        compiler_params=_cp(1),
    )(proj, proj, cw, cb, wa, ba, wx, bx, lam)


def _ln_stats(vc):
    mu = jnp.mean(vc, axis=-1, keepdims=True)
    xc = vc - mu
    rstd = lax.rsqrt(jnp.mean(xc * xc, axis=-1, keepdims=True) + EPS)
    return xc * rstd, rstd


def _conv_fwd(proj, w31, b31, ln_g, ln_b):
    t = proj.shape[0]

    def body(gv_ref, gg_ref, w_ref, b_ref, lg_ref, lb_ref, vc_ref, s_ref, vext_sc):
        i = pl.program_id(0)

        @pl.when(i == 0)
        def _():
            vext_sc[0:HALO, :] = jnp.zeros((HALO, D), f32)

        v = gv_ref[...] * _sigmoid(gg_ref[...])
        vext_sc[HALO:HALO + TM, :] = v
        ve = vext_sc[...]
        acc = jnp.zeros((TM, D), f32) + b_ref[...]
        for s in range(8):
            vs = ve if s == 0 else pltpu.roll(ve, s, 0)
            for m in range(HALO // 8):
                k = KC31 - 1 - (8 * m + s)
                if 0 <= k < KC31:
                    acc = acc + w_ref[k:k + 1, :] * vs[HALO - 8 * m:HALO - 8 * m + TM]
        vext_sc[0:HALO, :] = v[TM - HALO:TM]
        xhat, _ = _ln_stats(acc)
        ln = xhat * lg_ref[...] + lb_ref[...]
        vc_ref[...] = acc
        s_ref[...] = (ln * _sigmoid(ln)).astype(bf16)

    vec = pl.BlockSpec((1, D), lambda i: (0, 0))
    return pl.pallas_call(
        body, name="conv_fwd", grid=(t // TM,),
        in_specs=[
            pl.BlockSpec((TM, D), lambda i: (i, 2)),
            pl.BlockSpec((TM, D), lambda i: (i, 3)),
            pl.BlockSpec((KC31, D), lambda i: (0, 0)),
            vec, vec, vec,
        ],
        out_specs=[pl.BlockSpec((TM, D), lambda i: (i, 0))] * 2,
        out_shape=[jax.ShapeDtypeStruct((t, D), f32), jax.ShapeDtypeStruct((t, D), bf16)],
        scratch_shapes=[pltpu.VMEM((TM + HALO, D), f32)],
        compiler_params=_cp(1),
    )(proj, proj, w31, b31, ln_g, ln_b)


def _merge_fwd(h, z, s, proj, wrp, wcp, bcp, wout):
    t = h.shape[0]

    def body(h_ref, z_ref, s_ref, ga_ref, gb_ref, wrp_ref, wcp_ref, bcp_ref, wout_ref, ho_ref):
        ya = jnp.dot(z_ref[...], wrp_ref[...], preferred_element_type=f32)
        yb = jnp.dot(s_ref[...], wcp_ref[...], preferred_element_type=f32) + bcp_ref[...]
        merged = _sigmoid(ga_ref[...]) * ya + _sigmoid(gb_ref[...]) * yb
        ho_ref[...] = h_ref[...] + jnp.dot(merged.astype(bf16), wout_ref[...], preferred_element_type=f32)

    row = pl.BlockSpec((TM, D), lambda i: (i, 0))
    wsq = pl.BlockSpec((D, D), lambda i: (0, 0))
    return pl.pallas_call(
        body, name="merge_fwd", grid=(t // TM,),
        in_specs=[row, row, row,
                  pl.BlockSpec((TM, D), lambda i: (i, 4)),
                  pl.BlockSpec((TM, D), lambda i: (i, 5)),
                  wsq, wsq, pl.BlockSpec((1, D), lambda i: (0, 0)), wsq],
        out_specs=row,
        out_shape=jax.ShapeDtypeStruct((t, D), f32),
        compiler_params=_cp(1),
    )(h, z, s, proj, proj, wrp, wcp, bcp, wout)


def _final_loss(h, g, tgt, n_valid):
    t = h.shape[0]

    def body(h_ref, g_ref, t_ref, dh_ref, loss_ref, dg_ref):
        i = pl.program_id(0)

        @pl.when(i == 0)
        def _():
            loss_ref[...] = jnp.zeros_like(loss_ref)
            dg_ref[...] = jnp.zeros_like(dg_ref)

        hh = h_ref[...]
        gg = g_ref[...]
        row = i * TM + _row_ids((TM, 1))
        valid = jnp.logical_and(row >= NMETA, row < n_valid)
        out = hh * _rms_stats(hh) * gg
        err = jnp.where(valid, out - t_ref[...], 0.0)
        loss_ref[...] += 0.5 * jnp.sum(err * err) * (1.0 / D)
        dh, dg = _rms_bwd(err * (1.0 / D), hh, gg)
        dh_ref[...] = dh
        dg_ref[...] += dg

    row_spec = pl.BlockSpec((TM, D), lambda i: (i, 0))
    return pl.pallas_call(
        body, name="final_loss", grid=(t // TM,),
        in_specs=[row_spec, pl.BlockSpec((1, D), lambda i: (0, 0)), row_spec],
        out_specs=[row_spec, pl.BlockSpec((8, 128), lambda i: (0, 0)), pl.BlockSpec((1, D), lambda i: (0, 0))],
        out_shape=[jax.ShapeDtypeStruct((t, D), f32), jax.ShapeDtypeStruct((8, 128), f32),
                   jax.ShapeDtypeStruct((1, D), f32)],
        compiler_params=_cp(1),
    )(h, g, tgt)


def _ffn_bwd(dh, h, g, gate, up, wgu, wd, name, after=None):
    t = h.shape[0]
    nj = 2

    def body(dh_ref, h_ref, g_ref, gate_ref, up_ref, wg_ref, wu_ref, wd_ref,
             dhi_ref, dgate_ref, dup_ref, a_ref, df_ref, dg_ref, dfb_sc, dn_sc):
        i = pl.program_id(0)
        j = pl.program_id(1)

        @pl.when(jnp.logical_and(i == 0, j == 0))
        def _():
            dg_ref[...] = jnp.zeros_like(dg_ref)

        @pl.when(j == 0)
        def _():
            dfb = (0.5 * dh_ref[...]).astype(bf16)
            dfb_sc[...] = dfb
            df_ref[...] = dfb
            dn_sc[...] = jnp.zeros_like(dn_sc)

        da = _nt_dot(dfb_sc[...], wd_ref[...])
        gt = gate_ref[...].astype(f32)
        uu = up_ref[...].astype(f32)
        sg = _sigmoid(gt)
        silu = gt * sg
        a_ref[...] = (silu * uu).astype(bf16)
        dgt = (da * uu * (sg * (1.0 + gt * (1.0 - sg)))).astype(bf16)
        dup = (da * silu).astype(bf16)
        dgate_ref[...] = dgt
        dup_ref[...] = dup
        dn_sc[...] += _nt_dot(dgt, wg_ref[...]) + _nt_dot(dup, wu_ref[...])

        @pl.when(j == nj - 1)
        def _():
            dhin, dg = _rms_bwd(dn_sc[...], h_ref[...], g_ref[...])
            dhi_ref[...] = dh_ref[...] + dhin
            dg_ref[...] += dg

    rowd = pl.BlockSpec((TM, D), lambda i, j: (i, 0))
    rowf = pl.BlockSpec((TM, FS), lambda i, j: (i, j))
    vec = pl.BlockSpec((1, D), lambda i, j: (0, 0))
    body, in_specs, args = _ordered(
        body,
        [rowd, rowd, vec, rowf, rowf,
         pl.BlockSpec((None, D, FS), lambda i, j: (j, 0, 0)),
         pl.BlockSpec((None, D, FS), lambda i, j: (2 + j, 0, 0)),
         pl.BlockSpec((FS, D), lambda i, j: (j, 0))],
        (dh, h, g, gate, up, wgu, wgu, wd), after)
    return pl.pallas_call(
        body, name=name, grid=(t // TM, nj),
        in_specs=in_specs,
        out_specs=[rowd, rowf, rowf, rowf, rowd, vec],
        out_shape=[jax.ShapeDtypeStruct((t, D), f32), jax.ShapeDtypeStruct((t, F), bf16),
                   jax.ShapeDtypeStruct((t, F), bf16), jax.ShapeDtypeStruct((t, F), bf16),
                   jax.ShapeDtypeStruct((t, D), bf16), jax.ShapeDtypeStruct((1, D), f32)],
        scratch_shapes=[pltpu.VMEM((TM, D), bf16), pltpu.VMEM((TM, D), f32)],
        compiler_params=_cp(2),
    )(*args)


def _big_tile(t):
    for cand in (2112, 1408, 768, 384):
        if t % cand == 0:
            return cand
    raise ValueError(t)


ANY_SPEC = pl.BlockSpec(memory_space=pl.ANY)


def _tn_matmul(a, b, tk, tn, out_shape, out_block, out_map, name, base=None, after=None):
    t, kk = a.shape
    _, nn = b.shape
    tmm = _big_tile(t)
    nm = t // tmm

    def body(a_ref, b_ref, o_ref, acc_sc):
        m = pl.program_id(2)

        @pl.when(m == 0)
        def _():
            acc_sc[...] = jnp.zeros_like(acc_sc)

        acc_sc[...] += _tn_dot(a_ref[...], b_ref[...])

        @pl.when(m == nm - 1)
        def _():
            o_ref[...] = acc_sc[...].astype(o_ref.dtype)

    in_specs = [pl.BlockSpec((tmm, tk), lambda k, n, m: (m, k)),
                pl.BlockSpec((tmm, tn), lambda k, n, m: (m, n))]
    args, aliases = (a, b), {}
    if base is not None:
        body = (lambda inner: lambda a_ref, b_ref, base_ref, o_ref, acc_sc: inner(a_ref, b_ref, o_ref, acc_sc))(body)
        in_specs, args, aliases = in_specs + [ANY_SPEC], (a, b, base), {2: 0}
    if after is not None:
        body, in_specs, args = _ordered(body, in_specs, args, after)
        aliases = {k + 1: v for k, v in aliases.items()}
    return pl.pallas_call(
        body, name=name, grid=(kk // tk, nn // tn, nm),
        in_specs=in_specs,
        out_specs=pl.BlockSpec(out_block, out_map),
        out_shape=jax.ShapeDtypeStruct(out_shape, bf16),
        scratch_shapes=[pltpu.VMEM((tk, tn), f32)],
        input_output_aliases=aliases,
        compiler_params=_cp(3),
    )(*args)


def _merge_bwd(dh, z, s, proj, wrp, wcp, bcp, wout, after=None):
    t = dh.shape[0]

    def body(dh_ref, z_ref, s_ref, ga_ref, gb_ref, wrp_ref, wcp_ref, bcp_ref, wout_ref,
             dz_ref, ds_ref, dgab_ref, dhb_ref, mg_ref, dya_ref, dyb_ref, dbcp_ref):
        i = pl.program_id(0)

        @pl.when(i == 0)
        def _():
            dbcp_ref[...] = jnp.zeros_like(dbcp_ref)

        dhb = dh_ref[...].astype(bf16)
        dhb_ref[...] = dhb
        dmg = _nt_dot(dhb, wout_ref[...])
        ya = jnp.dot(z_ref[...], wrp_ref[...], preferred_element_type=f32)
        yb = jnp.dot(s_ref[...], wcp_ref[...], preferred_element_type=f32) + bcp_ref[...]
        sa = _sigmoid(ga_ref[...])
        sb = _sigmoid(gb_ref[...])
        mg_ref[...] = (sa * ya + sb * yb).astype(bf16)
        dgab_ref[:, 0:D] = (dmg * ya * sa * (1.0 - sa)).astype(bf16)
        dgab_ref[:, D:2 * D] = (dmg * yb * sb * (1.0 - sb)).astype(bf16)
        dya = dmg * sa
        dyb = dmg * sb
        dbcp_ref[...] += jnp.sum(dyb, axis=0, keepdims=True)
        dyab = dya.astype(bf16)
        dybb = dyb.astype(bf16)
        dya_ref[...] = dyab
        dyb_ref[...] = dybb
        dz_ref[...] = _nt_dot(dyab, wrp_ref[...])
        ds_ref[...] = _nt_dot(dybb, wcp_ref[...])

    row = pl.BlockSpec((TM, D), lambda i: (i, 0))
    wsq = pl.BlockSpec((D, D), lambda i: (0, 0))
    vec = pl.BlockSpec((1, D), lambda i: (0, 0))
    rowb = jax.ShapeDtypeStruct((t, D), bf16)
    body, in_specs, args = _ordered(
        body,
        [row, row, row,
         pl.BlockSpec((TM, D), lambda i: (i, 4)),
         pl.BlockSpec((TM, D), lambda i: (i, 5)),
         wsq, wsq, vec, wsq],
        (dh, z, s, proj, proj, wrp, wcp, bcp, wout), after)
    return pl.pallas_call(
        body, name="merge_bwd", grid=(t // TM,),
        in_specs=in_specs,
        out_specs=[row, row,
                   pl.BlockSpec((TM, 2 * D), lambda i: (i, 2)),
                   row, row, row, row, vec],
        out_shape=[jax.ShapeDtypeStruct((t, D), f32), jax.ShapeDtypeStruct((t, D), f32),
                   jax.ShapeDtypeStruct((t, NIN), bf16),
                   rowb, rowb, rowb, rowb, jax.ShapeDtypeStruct((1, D), f32)],
        compiler_params=_cp(1),
    )(*args)


def _conv_bwd(ds, vc, proj, dproj, w31, ln_g, ln_b):
    t = ds.shape[0]
    nt = t // TM
    hb = TM // HALO

    def body(ds_ref, vc_ref, gv_ref, gg_ref, gvp_ref, ggp_ref, dpin_ref, w_ref, lg_ref, lb_ref,
             dgvg_ref, dw_ref, db_ref, dlg_ref, dlb_ref, dext_sc, vext_sc, dwacc_sc):
        del dpin_ref
        i = pl.program_id(0)
        tile = nt - 1 - i

        @pl.when(i == 0)
        def _():
            dext_sc[TM:TM + HALO, :] = jnp.zeros((HALO, D), f32)
            dwacc_sc[...] = jnp.zeros_like(dwacc_sc)
            db_ref[...] = jnp.zeros_like(db_ref)
            dlg_ref[...] = jnp.zeros_like(dlg_ref)
            dlb_ref[...] = jnp.zeros_like(dlb_ref)

        vc = vc_ref[...]
        xhat, rstd = _ln_stats(vc)
        lg = lg_ref[...]
        ln = xhat * lg + lb_ref[...]
        sg = _sigmoid(ln)
        dln = ds_ref[...] * (sg * (1.0 + ln * (1.0 - sg)))
        dlg_ref[...] += jnp.sum(dln * xhat, axis=0, keepdims=True)
        dlb_ref[...] += jnp.sum(dln, axis=0, keepdims=True)
        dxh = dln * lg
        dvc = rstd * (dxh - jnp.mean(dxh, axis=-1, keepdims=True)
                      - xhat * jnp.mean(dxh * xhat, axis=-1, keepdims=True))
        db_ref[...] += jnp.sum(dvc, axis=0, keepdims=True)

        dext_sc[0:TM, :] = dvc
        de = dext_sc[...]
        dv = jnp.zeros((TM, D), f32)
        for s in range(8):
            dsh = de if s == 0 else pltpu.roll(de, TM + HALO - s, 0)
            for m in range(HALO // 8):
                k = KC31 - 1 - (8 * m + s)
                if 0 <= k < KC31:
                    dv = dv + w_ref[k:k + 1, :] * dsh[8 * m:8 * m + TM]
        dext_sc[TM:TM + HALO, :] = dvc[0:HALO]

        gv = gv_ref[...]
        sgg = _sigmoid(gg_ref[...])
        dgvg_ref[:, 0:D] = (dv * sgg).astype(bf16)
        dgvg_ref[:, D:2 * D] = (dv * gv * sgg * (1.0 - sgg)).astype(bf16)

        vprev = gvp_ref[...] * _sigmoid(ggp_ref[...])
        vext_sc[0:HALO, :] = jnp.where(tile > 0, vprev, 0.0)
        vext_sc[HALO:HALO + TM, :] = gv * sgg
        ve = vext_sc[...]
        for s in range(8):
            vs = ve if s == 0 else pltpu.roll(ve, s, 0)
            for m in range(HALO // 8):
                k = KC31 - 1 - (8 * m + s)
                if 0 <= k < KC31:
                    prod = dvc * vs[HALO - 8 * m:HALO - 8 * m + TM]
                    dwacc_sc[k] += jnp.sum(prod.reshape(TM // 8, 8, D), axis=0)

        @pl.when(i == nt - 1)
        def _():
            for k in range(KC31):
                dw_ref[k:k + 1, :] = jnp.sum(dwacc_sc[k], axis=0, keepdims=True)

    rev = lambda i: (nt - 1 - i, 0)
    vec = pl.BlockSpec((1, D), lambda i: (0, 0))
    halo_row = lambda i: jnp.maximum((nt - 1 - i) * hb - 1, 0)
    return pl.pallas_call(
        body, name="conv_bwd", grid=(nt,),
        in_specs=[
            pl.BlockSpec((TM, D), rev),
            pl.BlockSpec((TM, D), rev),
            pl.BlockSpec((TM, D), lambda i: (nt - 1 - i, 2)),
            pl.BlockSpec((TM, D), lambda i: (nt - 1 - i, 3)),
            pl.BlockSpec((HALO, D), lambda i: (halo_row(i), 2)),
            pl.BlockSpec((HALO, D), lambda i: (halo_row(i), 3)),
            pl.BlockSpec(memory_space=pl.ANY),
            pl.BlockSpec((KC31, D), lambda i: (0, 0)),
            vec, vec,
        ],
        out_specs=[
            pl.BlockSpec((TM, 2 * D), lambda i: (nt - 1 - i, 1)),
            pl.BlockSpec((KC31, D), lambda i: (0, 0)),
            vec, vec, vec,
        ],
        out_shape=[jax.ShapeDtypeStruct((t, NIN), bf16),
                   jax.ShapeDtypeStruct((KC31, D), f32),
                   jax.ShapeDtypeStruct((1, D), f32), jax.ShapeDtypeStruct((1, D), f32),
                   jax.ShapeDtypeStruct((1, D), f32)],
        scratch_shapes=[pltpu.VMEM((TM + HALO, D), f32), pltpu.VMEM((TM + HALO, D), f32),
                        pltpu.VMEM((KC31, 8, D), f32)],
        input_output_aliases={6: 0},
        compiler_params=_cp(1),
    )(ds, vc, proj, proj, proj, proj, dproj, w31, ln_g, ln_b)


def _rnn_bwd(dz, xr, hr, proj, dproj, cw, wa, ba, wx, bx, lam):
    t = dz.shape[0]
    nt = t // TM
    ng = TM // 8
    hq = HD // NCHIP

    def body(dz_ref, xr_ref, hr_ref, hrp_ref, x_ref, xp_ref, y_ref, dpin_ref,
             cw_ref, wa_ref, ba_ref, wx_ref, bx_ref, lam_ref,
             dxy_ref, dwa_ref, dwx_ref, dcw_ref, dcb_ref, dba_ref, dbx_ref, dlam_ref,
             anext_sc, gcarry_sc, dext_sc, xext_sc, m_sc, g_sc, dwa_sc, dwx_sc, dsp_sc):
        del dpin_ref
        i = pl.program_id(0)
        tile = nt - 1 - i

        @pl.when(i == 0)
        def _():
            anext_sc[...] = jnp.zeros_like(anext_sc)
            gcarry_sc[...] = jnp.zeros_like(gcarry_sc)
            dext_sc[TM:TM + 8, :] = jnp.zeros((8, D), f32)
            dwa_sc[...] = jnp.zeros_like(dwa_sc)
            dwx_sc[...] = jnp.zeros_like(dwx_sc)
            dsp_sc[...] = jnp.zeros_like(dsp_sc)
            dcw_ref[...] = jnp.zeros_like(dcw_ref)
            dcb_ref[...] = jnp.zeros_like(dcb_ref)
            dba_ref[...] = jnp.zeros_like(dba_ref)
            dbx_ref[...] = jnp.zeros_like(dbx_ref)

        xr = xr_ref[...]
        hr = hr_ref[...]
        dz = dz_ref[...]
        gel, dgel = _gelu_and_grad(y_ref[...])
        dxy_ref[:, D:2 * D] = (dz * hr * dgel).astype(bf16)
        ra, ii, a, sq, sp = _block_gates(xr, wa_ref, ba_ref[...], wx_ref, bx_ref[...], lam_ref[...])

        row = _row_ids((TM, D))
        m_sc[...] = jnp.where(row == TM - 1, anext_sc[...], pltpu.roll(a, TM - 1, 0))
        anext_sc[...] = a[0:1, :]
        g_sc[...] = dz * gel
        row8 = _row_ids((8, D))

        def group(qq, carry):
            off = pl.multiple_of((ng - 1 - qq) * 8, 8)
            mm = m_sc[pl.ds(off, 8), :]
            dd = g_sc[pl.ds(off, 8), :]
            for s in (1, 2, 4):
                m_sh = jnp.where(row8 < 8 - s, pltpu.roll(mm, 8 - s, 0), 1.0)
                d_sh = jnp.where(row8 < 8 - s, pltpu.roll(dd, 8 - s, 0), 0.0)
                dd = dd + mm * d_sh
                mm = mm * m_sh
            dd = dd + mm * carry
            g_sc[pl.ds(off, 8), :] = dd
            return dd[0:1, :]

        gcarry_sc[...] = lax.fori_loop(0, ng, group, gcarry_sc[...])
        gg = g_sc[...]

        hlast = jnp.where(tile > 0, hrp_ref[7:8, :], 0.0)
        hprev = jnp.where(row == 0, hlast, pltpu.roll(hr, 1, 0))
        d_a = gg * hprev
        dsq = gg * ii * xr
        dii = gg * sq * xr
        dxr = gg * sq * ii
        dlog = d_a * a - dsq * (a * a / sq)
        dsp_sc[...] += jnp.sum(dlog * (-8.0 * ra), axis=0, keepdims=True)
        dpa = dlog * (-8.0 * sp) * ra * (1.0 - ra)
        dpx = dii * ii * (1.0 - ii)
        dba_ref[...] += jnp.sum(dpa, axis=0, keepdims=True)
        dbx_ref[...] += jnp.sum(dpx, axis=0, keepdims=True)
        dpab = dpa.astype(bf16)
        dpxb = dpx.astype(bf16)
        xrb = xr.astype(bf16)
        back = []
        for hh in range(NHEAD):
            cols = slice(hh * HD, (hh + 1) * HD)
            back.append(_nt_dot(dpab[:, cols], wa_ref[hh]) + _nt_dot(dpxb[:, cols], wx_ref[hh]))
            dwa_sc[hh] += _tn_dot(xrb[:, cols], dpab[:, cols])
            dwx_sc[hh] += _tn_dot(xrb[:, cols], dpxb[:, cols])
        dxr = dxr + jnp.concatenate(back, axis=1)

        dext_sc[0:TM, :] = dxr
        de = dext_sc[...]
        dx = cw_ref[KC4 - 1:KC4, :] * dxr
        for k in range(KC4 - 1):
            dx = dx + cw_ref[k:k + 1, :] * pltpu.roll(de, TM + 8 - (KC4 - 1 - k), 0)[0:TM]
        dext_sc[TM:TM + 8, :] = dxr[0:8]
        dxy_ref[:, 0:D] = dx.astype(bf16)

        x = x_ref[...]
        xext_sc[0:8, :] = jnp.where(tile > 0, xp_ref[...], 0.0)
        xext_sc[8:8 + TM, :] = x
        xe = xext_sc[...]
        dcw_ref[KC4 - 1:KC4, :] += jnp.sum(dxr * x, axis=0, keepdims=True)
        for k in range(KC4 - 1):
            xs = pltpu.roll(xe, KC4 - 1 - k, 0)[8:8 + TM]
            dcw_ref[k:k + 1, :] += jnp.sum(dxr * xs, axis=0, keepdims=True)
        dcb_ref[...] += jnp.sum(dxr, axis=0, keepdims=True)

        @pl.when(i == nt - 1)
        def _():
            for hh in range(NHEAD):
                for qc in range(NCHIP):
                    dwa_ref[qc, hh] = dwa_sc[hh, qc * hq:(qc + 1) * hq, :].astype(bf16)
                    dwx_ref[qc, hh] = dwx_sc[hh, qc * hq:(qc + 1) * hq, :].astype(bf16)
            dlam_ref[...] = -dsp_sc[...] * _sigmoid(-lam_ref[...])

    rev = lambda i: (nt - 1 - i, 0)
    vec = pl.BlockSpec((1, D), lambda i: (0, 0))
    prev8 = lambda i: jnp.maximum((nt - 1 - i) * ng - 1, 0)
    wblk = pl.BlockSpec((NHEAD, HD, HD), lambda i: (0, 0, 0))
    gblk = pl.BlockSpec((NCHIP, NHEAD, hq, HD), lambda i: (0, 0, 0, 0))
    return pl.pallas_call(
        body, name="rnn_bwd", grid=(nt,),
        in_specs=[
            pl.BlockSpec((TM, D), rev),
            pl.BlockSpec((TM, D), rev),
            pl.BlockSpec((TM, D), rev),
            pl.BlockSpec((8, D), lambda i: (prev8(i), 0)),
            pl.BlockSpec((TM, D), lambda i: (nt - 1 - i, 0)),
            pl.BlockSpec((8, D), lambda i: (prev8(i), 0)),
            pl.BlockSpec((TM, D), lambda i: (nt - 1 - i, 1)),
            pl.BlockSpec(memory_space=pl.ANY),
            pl.BlockSpec((KC4, D), lambda i: (0, 0)),
            wblk, vec, wblk, vec, vec,
        ],
        out_specs=[
            pl.BlockSpec((TM, 2 * D), lambda i: (nt - 1 - i, 0)),
            gblk, gblk,
            pl.BlockSpec((KC4, D), lambda i: (0, 0)),
            vec, vec, vec, vec,
        ],
        out_shape=[jax.ShapeDtypeStruct((t, NIN), bf16),
                   jax.ShapeDtypeStruct((NCHIP, NHEAD, hq, HD), bf16),
                   jax.ShapeDtypeStruct((NCHIP, NHEAD, hq, HD), bf16),
                   jax.ShapeDtypeStruct((KC4, D), f32),
                   jax.ShapeDtypeStruct((1, D), f32), jax.ShapeDtypeStruct((1, D), f32),
                   jax.ShapeDtypeStruct((1, D), f32), jax.ShapeDtypeStruct((1, D), f32)],
        scratch_shapes=[pltpu.VMEM((1, D), f32), pltpu.VMEM((1, D), f32),
                        pltpu.VMEM((TM + 8, D), f32), pltpu.VMEM((TM + 8, D), f32),
                        pltpu.VMEM((TM, D), f32), pltpu.VMEM((TM, D), f32),
                        pltpu.VMEM((NHEAD, HD, HD), f32), pltpu.VMEM((NHEAD, HD, HD), f32),
                        pltpu.VMEM((1, D), f32)],
        input_output_aliases={7: 0},
        compiler_params=_cp(1),
    )(dz, xr, hr, hr, proj, proj, proj, dproj, cw, wa, ba, wx, bx, lam)


def _inproj_bwd(dproj, dh, h, g, win, after=None):
    t = h.shape[0]
    tn = NIN // NCHIP
    nj = NIN // tn
    per = (NIN // NCHIP) // tn

    def body(dp_ref, dh_ref, h_ref, g_ref, w_ref, dhi_ref, dg_ref, db_ref, dn_sc):
        i = pl.program_id(0)
        j = pl.program_id(1)

        @pl.when(jnp.logical_and(i == 0, j == 0))
        def _():
            dg_ref[...] = jnp.zeros_like(dg_ref)
            db_ref[...] = jnp.zeros_like(db_ref)

        @pl.when(j == 0)
        def _():
            dn_sc[...] = jnp.zeros_like(dn_sc)

        dp = dp_ref[...]
        dn_sc[...] += _nt_dot(dp, w_ref[...])
        db_ref[j] += jnp.sum(dp.astype(f32), axis=0, keepdims=True)

        @pl.when(j == nj - 1)
        def _():
            dhin, dg = _rms_bwd(dn_sc[...], h_ref[...], g_ref[...])
            dhi_ref[...] = dh_ref[...] + dhin
            dg_ref[...] += dg

    rowd = pl.BlockSpec((TM, D), lambda i, j: (i, 0))
    vec = pl.BlockSpec((1, D), lambda i, j: (0, 0))
    body, in_specs, args = _ordered(
        body,
        [pl.BlockSpec((TM, tn), lambda i, j: (i, j)), rowd, rowd, vec,
         pl.BlockSpec((None, D, tn), lambda i, j: (j // per, 0, j % per))],
        (dproj, dh, h, g, win), after)
    return pl.pallas_call(
        body, name="inproj_bwd", grid=(t // TM, nj),
        in_specs=in_specs,
        out_specs=[rowd, vec, pl.BlockSpec((nj, 1, tn), lambda i, j: (0, 0, 0))],
        out_shape=[jax.ShapeDtypeStruct((t, D), f32), jax.ShapeDtypeStruct((1, D), f32),
                   jax.ShapeDtypeStruct((nj, 1, tn), f32)],
        scratch_shapes=[pltpu.VMEM((TM, D), f32)],
        compiler_params=_cp(2),
    )(*args)


def _ffn_gu_grad(n, dgate, dup, tag, after=None):
    half = _tn_matmul(n, dgate, D, FS, (NCHIP, D, FS), (None, D, FS), lambda k, nn, m: (nn, 0, 0),
                      tag + "_dwg", after=after)
    return _tn_matmul(n, dup, D, FS, (NCHIP, D, FS), (None, D, FS), lambda k, nn, m: (2 + nn, 0, 0),
                      tag + "_dwu", base=half)


def _ffn_down_grad(a, df, tag, after=None):
    return _tn_matmul(a, df, FS, D, (F, D), (FS, D), lambda k, nn, m: (k, 0), tag + "_dwd", after=after)


def _square_grad(a, b, name):
    return _tn_matmul(a, b, D, D, (D, D), (D, D), lambda k, nn, m: (0, 0), name)


ANY = pl.BlockSpec(memory_space=pl.ANY)


def _place():
    x, y, c = lax.axis_index("x"), lax.axis_index("y"), lax.axis_index("c")
    chips = [(1 - x, y), (x, 1 - y), (1 - x, 1 - y)]
    return x, y, c, chips


def _chip_id(chip):
    return 2 * chip[0] + chip[1]


def _cast_into_slot(w2d, qc, dtype, name, after=None):
    r, cc = w2d.shape
    hr = r // 2

    def body(qc_ref, *refs):
        del qc_ref
        w_ref, o_ref = refs[-2:]
        o_ref[...] = w_ref[...].astype(dtype)

    in_specs, args = [pl.BlockSpec((hr, cc), lambda h, qc_ref: (h, 0))], (w2d,)
    if after is not None:
        in_specs, args = [ANY_SPEC] + in_specs, (after,) + args
    return pl.pallas_call(
        body, name=name,
        grid_spec=pltpu.PrefetchScalarGridSpec(
            num_scalar_prefetch=1, grid=(2,),
            in_specs=in_specs,
            out_specs=pl.BlockSpec((None, None, hr, cc), lambda h, qc_ref: (qc_ref[0], h, 0, 0))),
        out_shape=jax.ShapeDtypeStruct((NCHIP, 2, hr, cc), dtype),
        compiler_params=_cp(1),
    )(qc, *args)


def _place_pack(pack, qc):
    def body(qc_ref, p_ref, o_ref):
        del qc_ref
        o_ref[...] = p_ref[...]

    return pl.pallas_call(
        body, name="place_pack",
        grid_spec=pltpu.PrefetchScalarGridSpec(
            num_scalar_prefetch=1, grid=(1,),
            in_specs=[pl.BlockSpec(pack.shape, lambda i, qc_ref: (0, 0))],
            out_specs=pl.BlockSpec((None,) + pack.shape, lambda i, qc_ref: (2 * qc_ref[0] + qc_ref[1], 0, 0))),
        out_shape=jax.ShapeDtypeStruct((8,) + pack.shape, pack.dtype),
        compiler_params=_cp(1),
    )(qc, pack)


def _gather_shards(bufs):
    n = len(bufs)

    def body(*refs):
        outs = refs[n:2 * n]
        send_sems, recv_sems = refs[2 * n:]
        x, y, c, chips = _place()
        q = 2 * x + y
        sibling = (x, y, 1 - c)

        def remote(a, k, blk, to):
            return pltpu.make_async_remote_copy(src_ref=blk, dst_ref=blk, send_sem=send_sems.at[a, k],
                                                recv_sem=recv_sems.at[a, k], device_id=to, device_id_type=MESH)

        sent = []
        for a in range(n):
            for j, chip in enumerate(chips):
                cp = remote(a, j, outs[a].at[q, c], (chip[0], chip[1], c))
                cp.start()
                sent.append(cp)
        for a in range(n):
            for j, chip in enumerate(chips):
                got = outs[a].at[_chip_id(chip), c]
                remote(a, j, got, (chip[0], chip[1], c)).wait_recv()
                cp = remote(a, 3 + j, got, sibling)
                cp.start()
                sent.append(cp)
        for a in range(n):
            for j, chip in enumerate(chips):
                remote(a, 3 + j, outs[a].at[_chip_id(chip), 1 - c], sibling).wait_recv()
        for cp in sent:
            cp.wait_send()

    return pl.pallas_call(
        body, name="gather_shards",
        in_specs=[ANY] * n, out_specs=[ANY] * n,
        out_shape=[jax.ShapeDtypeStruct(s.shape, s.dtype) for s in bufs],
        scratch_shapes=[pltpu.SemaphoreType.DMA((n, 6)), pltpu.SemaphoreType.DMA((n, 6))],
        input_output_aliases={a: a for a in range(n)},
    )(*bufs)


def _pair_exchange(parts, name):
    n = len(parts)

    def body(*refs):
        ins, outs = refs[:n], refs[n:2 * n]
        send_sems, recv_sems = refs[2 * n:]
        x, y, c, _ = _place()
        copies = []
        for a in range(n):
            cp = pltpu.make_async_remote_copy(
                src_ref=ins[a].at[:, 1 - c], dst_ref=outs[a],
                send_sem=send_sems.at[a], recv_sem=recv_sems.at[a],
                device_id=(x, y, 1 - c), device_id_type=MESH)
            cp.start()
            copies.append(cp)
        for cp in copies:
            cp.wait()

    return pl.pallas_call(
        body, name=name,
        in_specs=[ANY] * n, out_specs=[ANY] * n,
        out_shape=[jax.ShapeDtypeStruct((NCHIP,) + s.shape[2:], s.dtype) for s in parts],
        scratch_shapes=[pltpu.SemaphoreType.DMA((n,)), pltpu.SemaphoreType.DMA((n,))],
    )(*parts)


def _pair_add(part, got, qc, name):
    _, _, hr, cc = part.shape

    def body(qc_ref, p_ref, g_ref, o_ref, land_ref):
        s = pl.program_id(0)
        val = (p_ref[...].astype(f32) + g_ref[...].astype(f32)).astype(bf16)
        o_ref[...] = val

        @pl.when(s == qc_ref[0])
        def _():
            land_ref[...] = val

    return pl.pallas_call(
        body, name=name,
        grid_spec=pltpu.PrefetchScalarGridSpec(
            num_scalar_prefetch=1, grid=(NCHIP,),
            in_specs=[pl.BlockSpec((None, None, hr, cc), lambda s, qc_ref: (s, qc_ref[1], 0, 0)),
                      pl.BlockSpec((None, hr, cc), lambda s, qc_ref: (s, 0, 0))],
            out_specs=[pl.BlockSpec((None, hr, cc), lambda s, qc_ref: (s, 0, 0)),
                       pl.BlockSpec((None, hr, cc), lambda s, qc_ref: (qc_ref[0], 0, 0))]),
        out_shape=[jax.ShapeDtypeStruct((NCHIP, hr, cc), bf16)] * 2,
        compiler_params=_cp(1),
    )(qc, part, got)


def _chip_exchange(sums, lands):
    n = len(sums)

    def body(*refs):
        ins, outs = refs[:n], refs[2 * n:3 * n]
        send_sems, recv_sems = refs[3 * n:]
        x, y, c, chips = _place()
        q = 2 * x + y
        sent = []
        for a in range(n):
            for j, chip in enumerate(chips):
                cp = pltpu.make_async_remote_copy(
                    src_ref=ins[a].at[_chip_id(chip)], dst_ref=outs[a].at[q],
                    send_sem=send_sems.at[a, j], recv_sem=recv_sems.at[a, j],
                    device_id=(chip[0], chip[1], c), device_id_type=MESH)
                cp.start()
                sent.append(cp)
        for a in range(n):
            for j, chip in enumerate(chips):
                got = outs[a].at[_chip_id(chip)]
                pltpu.make_async_remote_copy(
                    src_ref=got, dst_ref=got, send_sem=send_sems.at[a, j], recv_sem=recv_sems.at[a, j],
                    device_id=(chip[0], chip[1], c), device_id_type=MESH).wait_recv()
        for cp in sent:
            cp.wait_send()

    return pl.pallas_call(
        body, name="chip_exchange",
        in_specs=[ANY] * (2 * n), out_specs=[ANY] * n,
        out_shape=[jax.ShapeDtypeStruct(s.shape, s.dtype) for s in lands],
        scratch_shapes=[pltpu.SemaphoreType.DMA((n, 3)), pltpu.SemaphoreType.DMA((n, 3))],
        input_output_aliases={n + a: a for a in range(n)},
    )(*sums, *lands)


def _sum_chips(got, name):
    _, hr, cc = got.shape

    def body(g_ref, o_ref):
        acc = g_ref[0].astype(f32)
        for s in range(1, NCHIP):
            acc = acc + g_ref[s].astype(f32)
        o_ref[...] = acc

    return pl.pallas_call(
        body, name=name, grid=(1,),
        in_specs=[pl.BlockSpec((NCHIP, hr, cc), lambda i: (0, 0, 0))],
        out_specs=pl.BlockSpec((hr, cc), lambda i: (0, 0)),
        out_shape=jax.ShapeDtypeStruct((hr, cc), f32),
        compiler_params=_cp(1),
    )(got)


def _pair_share(halves, name):
    n = len(halves)

    def body(*refs):
        ins, outs = refs[:n], refs[n:2 * n]
        send_sems, recv_sems = refs[2 * n:]
        x, y, c, _ = _place()
        copies = []
        for a in range(n):
            cp = pltpu.make_async_remote_copy(
                src_ref=ins[a], dst_ref=outs[a], send_sem=send_sems.at[a], recv_sem=recv_sems.at[a],
                device_id=(x, y, 1 - c), device_id_type=MESH)
            cp.start()
            copies.append(cp)
        for cp in copies:
            cp.wait()

    return pl.pallas_call(
        body, name=name,
        in_specs=[ANY] * n, out_specs=[ANY] * n,
        out_shape=[jax.ShapeDtypeStruct(s.shape, s.dtype) for s in halves],
        scratch_shapes=[pltpu.SemaphoreType.DMA((n,)), pltpu.SemaphoreType.DMA((n,))],
    )(*halves)


def _all_copy(buf_ref, send_ref, recv_ref, k, x, y, c, landing):
    px, py, pc = (1 - x if k & 4 else x, 1 - y if k & 2 else y, 1 - c if k & 1 else c)
    me = 4 * x + 2 * y + c
    there = 4 * px + 2 * py + pc
    return pltpu.make_async_remote_copy(
        src_ref=buf_ref.at[me], dst_ref=buf_ref.at[there if landing else me],
        send_sem=send_ref.at[k - 1], recv_sem=recv_ref.at[k - 1],
        device_id=(px, py, pc), device_id_type=MESH)


def _gather_all_start(buf, name):
    def body(in_ref, send, recv, thru, token):
        del thru
        x, y, c, _ = _place()
        for k in range(1, 8):
            _all_copy(in_ref, send, recv, k, x, y, c, False).start()
        token[...] = jnp.zeros_like(token)

    return pl.pallas_call(
        body, name=name,
        in_specs=[HBM],
        out_specs=[SEM, SEM, HBM, pl.BlockSpec(memory_space=pltpu.VMEM)],
        out_shape=[pltpu.SemaphoreType.DMA((7,)), pltpu.SemaphoreType.DMA((7,)),
                   pltpu.HBM(buf.shape, buf.dtype), jax.ShapeDtypeStruct((8, 128), f32)],
        input_output_aliases={0: 2},
        compiler_params=pltpu.CompilerParams(has_side_effects=EFFECT),
    )(_in_hbm(buf))


def _gather_all_wait(send, recv, buf, after, name):
    def body(in_ref, send_r, recv_r, after_ref, out_ref):
        del after_ref, out_ref
        x, y, c, _ = _place()
        for k in range(1, 8):
            cp = _all_copy(in_ref, send_r, recv_r, k, x, y, c, True)
            cp.wait_send()
            cp.wait_recv()

    return pl.pallas_call(
        body, name=name,
        in_specs=[HBM, SEM, SEM, ANY],
        out_specs=HBM,
        out_shape=pltpu.HBM(buf.shape, buf.dtype),
        input_output_aliases={0: 0},
        compiler_params=pltpu.CompilerParams(has_side_effects=EFFECT),
    )(buf, send, recv, after)


HBM = pl.BlockSpec(memory_space=pltpu.HBM)
SEM = pl.BlockSpec(memory_space=pltpu.SEMAPHORE)
EFFECT = pltpu.SideEffectType.DATAFLOW_SIDE_EFFECTING
N_PEER = 3


def _in_hbm(a):
    return pltpu.with_memory_space_constraint(a, pltpu.HBM)


def _tie(a, token):
    return lax.optimization_barrier((a, token))[0]


def _gather_copy(buf_ref, send_ref, recv_ref, j, chip, q, c, landing_chip):
    return pltpu.make_async_remote_copy(
        src_ref=buf_ref.at[q, c], dst_ref=buf_ref.at[landing_chip, c],
        send_sem=send_ref.at[j], recv_sem=recv_ref.at[j],
        device_id=(chip[0], chip[1], c), device_id_type=MESH)


def _gather_start(bufs, name):
    n = len(bufs)

    def body(*refs):
        ins = refs[:n]
        send, recv = refs[n:2 * n], refs[2 * n:3 * n]
        token = refs[4 * n]
        x, y, c, chips = _place()
        q = 2 * x + y
        for a in range(n):
            for j, chip in enumerate(chips):
                _gather_copy(ins[a], send[a], recv[a], j, chip, q, c, q).start()
        token[...] = jnp.zeros_like(token)

    sems = [pltpu.SemaphoreType.DMA((N_PEER,))] * (2 * n)
    outs = pl.pallas_call(
        body, name=name,
        in_specs=[HBM] * n,
        out_specs=[SEM] * (2 * n) + [HBM] * n + [pl.BlockSpec(memory_space=pltpu.VMEM)],
        out_shape=sems + [pltpu.HBM(b.shape, b.dtype) for b in bufs] + [jax.ShapeDtypeStruct((8, 128), f32)],
        input_output_aliases={a: 2 * n + a for a in range(n)},
        compiler_params=pltpu.CompilerParams(has_side_effects=EFFECT),
    )(*[_in_hbm(b) for b in bufs])
    return list(outs[:n]), list(outs[n:2 * n]), list(outs[2 * n:3 * n]), outs[3 * n]


def _gather_wait(send, recv, bufs, after, name):
    n = len(bufs)

    def body(*refs):
        ins = refs[:n]
        send_r, recv_r = refs[n:2 * n], refs[2 * n:3 * n]
        x, y, c, chips = _place()
        q = 2 * x + y
        for a in range(n):
            for j, chip in enumerate(chips):
                cp = _gather_copy(ins[a], send_r[a], recv_r[a], j, chip, q, c, _chip_id(chip))
                cp.wait_send()
                cp.wait_recv()

    afters = after if isinstance(after, (tuple, list)) else (after,)
    outs = pl.pallas_call(
        body, name=name,
        in_specs=[HBM] * n + [SEM] * (2 * n) + [ANY] * len(afters),
        out_specs=[HBM] * n,
        out_shape=[pltpu.HBM(b.shape, b.dtype) for b in bufs],
        input_output_aliases={a: a for a in range(n)},
        compiler_params=pltpu.CompilerParams(has_side_effects=EFFECT),
    )(*bufs, *send, *recv, *afters)
    return list(outs)


def _forward_halves(bufs, name):
    n = len(bufs)

    def body(*refs):
        outs = refs[n:2 * n]
        send_sems, recv_sems = refs[2 * n:]
        x, y, c, chips = _place()
        sibling = (x, y, 1 - c)

        def remote(a, j, blk):
            return pltpu.make_async_remote_copy(src_ref=blk, dst_ref=blk, send_sem=send_sems.at[a, j],
                                                recv_sem=recv_sems.at[a, j], device_id=sibling,
                                                device_id_type=MESH)

        sent = []
        for a in range(n):
            for j, chip in enumerate(chips):
                cp = remote(a, j, outs[a].at[_chip_id(chip), c])
                cp.start()
                sent.append(cp)
        for a in range(n):
            for j, chip in enumerate(chips):
                remote(a, j, outs[a].at[_chip_id(chip), 1 - c]).wait_recv()
        for cp in sent:
            cp.wait_send()

    return pl.pallas_call(
        body, name=name,
        in_specs=[ANY] * n, out_specs=[ANY] * n,
        out_shape=[jax.ShapeDtypeStruct(s.shape, s.dtype) for s in bufs],
        scratch_shapes=[pltpu.SemaphoreType.DMA((n, N_PEER)), pltpu.SemaphoreType.DMA((n, N_PEER))],
        input_output_aliases={a: a for a in range(n)},
    )(*bufs)


def _reduce_copy(sum_ref, land_ref, send_ref, recv_ref, j, chip, q, c, landing_chip):
    return pltpu.make_async_remote_copy(
        src_ref=sum_ref.at[_chip_id(chip)], dst_ref=land_ref.at[landing_chip],
        send_sem=send_ref.at[j], recv_sem=recv_ref.at[j],
        device_id=(chip[0], chip[1], c), device_id_type=MESH)


def _reduce_start(sums, lands, name):
    n = len(sums)

    def body(*refs):
        s_in, l_in = refs[:n], refs[n:2 * n]
        send, recv = refs[2 * n:3 * n], refs[3 * n:4 * n]
        token = refs[6 * n]
        x, y, c, chips = _place()
        q = 2 * x + y
        for a in range(n):
            for j, chip in enumerate(chips):
                _reduce_copy(s_in[a], l_in[a], send[a], recv[a], j, chip, q, c, q).start()
        token[...] = jnp.zeros_like(token)

    sems = [pltpu.SemaphoreType.DMA((N_PEER,))] * (2 * n)
    outs = pl.pallas_call(
        body, name=name,
        in_specs=[HBM] * (2 * n),
        out_specs=[SEM] * (2 * n) + [HBM] * (2 * n) + [pl.BlockSpec(memory_space=pltpu.VMEM)],
        out_shape=sems + [pltpu.HBM(b.shape, b.dtype) for b in list(sums) + list(lands)]
        + [jax.ShapeDtypeStruct((8, 128), f32)],
        input_output_aliases={a: 2 * n + a for a in range(2 * n)},
        compiler_params=pltpu.CompilerParams(has_side_effects=EFFECT),
    )(*[_in_hbm(b) for b in list(sums) + list(lands)])
    return (list(outs[:n]), list(outs[n:2 * n]), list(outs[2 * n:3 * n]), list(outs[3 * n:4 * n]),
            outs[4 * n])


def _reduce_wait(send, recv, sums, lands, after, name):
    n = len(sums)

    def body(*refs):
        s_in, l_in = refs[:n], refs[n:2 * n]
        send_r, recv_r = refs[2 * n:3 * n], refs[3 * n:4 * n]
        x, y, c, chips = _place()
        q = 2 * x + y
        for a in range(n):
            for j, chip in enumerate(chips):
                cp = _reduce_copy(s_in[a], l_in[a], send_r[a], recv_r[a], j, chip, q, c, _chip_id(chip))
                cp.wait_send()
                cp.wait_recv()

    outs = pl.pallas_call(
        body, name=name,
        in_specs=[HBM] * (2 * n) + [SEM] * (2 * n) + [ANY],
        out_specs=[HBM] * (2 * n),
        out_shape=[pltpu.HBM(b.shape, b.dtype) for b in list(sums) + list(lands)],
        input_output_aliases={a: a for a in range(2 * n)},
        compiler_params=pltpu.CompilerParams(has_side_effects=EFFECT),
    )(*sums, *lands, *send, *recv, after)
    return list(outs[n:])


def _adamw_math(w, g, m, v):
    m = ADAM_B1 * m + (1.0 - ADAM_B1) * g
    v = ADAM_B2 * v + (1.0 - ADAM_B2) * (g * g)
    m_hat = m / (1.0 - ADAM_B1 ** ADAM_STEP)
    v_hat = v / (1.0 - ADAM_B2 ** ADAM_STEP)
    delta = -ADAM_LR * (m_hat / (jnp.sqrt(v_hat) + ADAM_EPS) + ADAM_WD * w)
    return delta, m, v


def _adamw(w, mine, theirs, m, v, qc, name):
    r, cc = w.shape
    hr = r // 2
    tr = next(hr // k for k in range(1, hr + 1)
              if hr % k == 0 and (hr // k) % 8 == 0 and (hr // k) * cc * 4 <= (1 << 20))
    nb = hr // tr

    def body(qc_ref, w_ref, a_ref, b_ref, m_ref, v_ref, g_ref, d_ref, mo_ref, vo_ref):
        g = jnp.where(pl.program_id(0) == qc_ref[1], a_ref[...], b_ref[...])
        g_ref[...] = g
        d_ref[...], mo_ref[...], vo_ref[...] = _adamw_math(w_ref[...], g, m_ref[...], v_ref[...])

    full = pl.BlockSpec((tr, cc), lambda h, i, qc_ref: (h * nb + i, 0))
    half = pl.BlockSpec((tr, cc), lambda h, i, qc_ref: (i, 0))
    return pl.pallas_call(
        body, name=name,
        grid_spec=pltpu.PrefetchScalarGridSpec(
            num_scalar_prefetch=1, grid=(2, nb),
            in_specs=[full, half, half, full, full], out_specs=[full] * 4),
        out_shape=[jax.ShapeDtypeStruct((r, cc), f32)] * 4,
        compiler_params=_cp(2),
    )(qc, w, mine, theirs, m, v)


REPL = [("ffn1_norm", 1), ("mix_norm", 1), ("b_in", 6), ("rnn_conv_b", 1), ("rg_b_a", 1), ("rg_b_x", 1),
        ("rg_lambda", 1), ("conv_dw_b", 1), ("conv_ln_g", 1), ("conv_ln_b", 1), ("conv_b_proj", 1),
        ("ffn2_norm", 1), ("final_norm", 1)]
COLSH = [("meta_tokens", NMETA), ("rnn_conv_w", KC4), ("conv_dw_w", KC31)]
SMALL = REPL + COLSH
CS = D // NCHIP


def _pack_rows():
    starts, row = {}, 0
    for k, rows in REPL:
        starts[k] = row
        row += rows
    for k, rows in COLSH:
        row = -(-row // 8) * 8
        starts[k] = row
        row += rows
    return starts, -(-row // 8) * 8


PACK_START, SMALL_ROWS = _pack_rows()


def _small_pack(g):
    pieces, row = [], 0
    for k, rows in SMALL:
        if PACK_START[k] > row:
            pieces.append(jnp.zeros((PACK_START[k] - row, D), f32))
        pieces.append(g[k].reshape(rows, D))
        row = PACK_START[k] + rows
    pieces.append(jnp.zeros((SMALL_ROWS - row, D), f32))
    return jnp.concatenate(pieces, axis=0)


def _adamw_small(packs, ws, ms, vs):
    ns = len(SMALL)

    def body(*refs):
        pack_ref = refs[0]
        w_refs, m_refs, v_refs = refs[1:1 + ns], refs[1 + ns:1 + 2 * ns], refs[1 + 2 * ns:1 + 3 * ns]
        outs = refs[1 + 3 * ns:1 + 7 * ns]
        g_refs, d_refs, mo_refs, vo_refs = outs[:ns], outs[ns:2 * ns], outs[2 * ns:3 * ns], outs[3 * ns:]
        gsum_sc = refs[1 + 7 * ns]
        q = 2 * lax.axis_index("x") + lax.axis_index("y")
        acc = pack_ref[0]
        for dev in range(1, 8):
            acc = acc + pack_ref[dev]
        gsum_sc[...] = acc
        for idx, (name, rows) in enumerate(SMALL):
            row = PACK_START[name]
            if idx < len(REPL):
                for k in range(rows):
                    cols = slice(k * D, (k + 1) * D)
                    g = gsum_sc[row + k:row + k + 1, :]
                    d, mm, vv = _adamw_math(w_refs[idx][:, cols], g, m_refs[idx][:, cols], v_refs[idx][:, cols])
                    g_refs[idx][:, cols] = g
                    d_refs[idx][:, cols] = d
                    mo_refs[idx][:, cols] = mm
                    vo_refs[idx][:, cols] = vv
            else:
                g = gsum_sc[row:row + rows, pl.ds(pl.multiple_of(q * CS, CS), CS)]
                d, mm, vv = _adamw_math(w_refs[idx][...], g, m_refs[idx][...], v_refs[idx][...])
                g_refs[idx][...] = g
                d_refs[idx][...] = d
                mo_refs[idx][...] = mm
                vo_refs[idx][...] = vv

    shapes = [jax.ShapeDtypeStruct(w.shape, f32) for w in ws]
    return pl.pallas_call(
        body, name="adamw_small",
        out_shape=shapes * 4,
        scratch_shapes=[pltpu.VMEM((SMALL_ROWS, D), f32)],
        compiler_params=pltpu.CompilerParams(vmem_limit_bytes=VMEM_LIMIT),
    )(packs, *ws, *ms, *vs)


BIG = ["ffn1_w_gu", "ffn1_w_down", "w_in", "rg_w_a", "rg_w_x", "rnn_w_proj", "conv_w_proj", "w_out",
       "ffn2_w_gu", "ffn2_w_down"]
WEIGHTS = ['meta_tokens', 'ffn1_norm', 'ffn1_w_gu', 'ffn1_w_down', 'mix_norm', 'w_in', 'b_in', 'rnn_conv_w',
           'rnn_conv_b', 'rg_w_a', 'rg_b_a', 'rg_w_x', 'rg_b_x', 'rg_lambda', 'rnn_w_proj', 'conv_dw_w',
           'conv_dw_b', 'conv_ln_g', 'conv_ln_b', 'conv_w_proj', 'conv_b_proj', 'w_out', 'ffn2_norm',
           'ffn2_w_gu', 'ffn2_w_down', 'final_norm']


def _as2d(a):
    return a.reshape(-1, a.shape[-1])


def _step(x, loss_target, w, m, v):
    seq = x.shape[1]
    n_valid = NMETA + seq
    t = -(-n_valid // TM) * TM

    qc = jnp.stack([2 * lax.axis_index("x") + lax.axis_index("y"), lax.axis_index("c")]).astype(jnp.int32)
    p = {k: w[k].reshape(1, rows * D) for k, rows in REPL}

    first = ["ffn1_w_gu", "ffn1_w_down", "small"]
    later = [["w_in"], ["rg_w_a", "rg_w_x", "rnn_w_proj", "conv_w_proj", "w_out"], ["ffn2_w_gu", "ffn2_w_down"]]
    small_rows = sum(r for _, r in COLSH)
    small = jnp.concatenate([_as2d(w[k]) for k, _ in COLSH] + [jnp.zeros((64 - small_rows, CS), f32)], axis=0)

    def cast(k, token=None):
        src, dtype = (small, f32) if k == "small" else (_as2d(w[k]), bf16)
        return _cast_into_slot(src, qc, dtype, "cast_" + k, after=token)

    send1, recv1, bufs1, token1 = _gather_start([cast(k) for k in first], "gather_start_first")
    rest = [k for grp in later for k in grp]
    send2, recv2, bufs2, token2 = _gather_start([cast(k, token1) for k in rest], "gather_start_rest")

    def finish(names, send, recv, bufs, after, tag):
        done = _forward_halves(_gather_wait(send, recv, bufs, after, "gather_wait_" + tag), "gather_forward_" + tag)
        for k, b in zip(names, done):
            full = b.reshape(NCHIP, 2 * b.shape[2], b.shape[3])
            if k in ("ffn1_w_down", "ffn2_w_down"):
                full = full.reshape(F, D)
            elif k in ("rnn_w_proj", "conv_w_proj", "w_out"):
                full = full.reshape(D, D)
            elif k in ("rg_w_a", "rg_w_x"):
                full = full.reshape(NCHIP, NHEAD, HD // NCHIP, HD).transpose(1, 0, 2, 3).reshape(NHEAD, HD, HD)
            p[k] = full

    def group(names):
        idx = [rest.index(k) for k in names]
        return names, [send2[i] for i in idx], [recv2[i] for i in idx], [bufs2[i] for i in idx]

    h0 = jnp.pad(x[0], ((NMETA, t - n_valid), (0, 0)))
    tgt = jnp.pad(loss_target[0], ((NMETA, t - n_valid), (0, 0)))
    finish(first, send1, recv1, bufs1, (token2, h0, tgt), "first")
    small_full = p.pop("small").transpose(1, 0, 2).reshape(64, D)
    row = 0
    for k, rows in COLSH:
        p[k] = small_full[row:row + rows]
        row += rows

    h0 = lax.dynamic_update_slice(h0, p["meta_tokens"], (0, 0))
    h1, gate1, up1, n1 = _ffn_fwd(h0, p["ffn1_norm"], p["ffn1_w_gu"], p["ffn1_w_down"], "ffn1_fwd")
    finish(*group(later[0]), h1, "in")
    proj, n2 = _inproj_fwd(h1, p["mix_norm"], p["w_in"], p["b_in"])
    finish(*group(later[1]), proj, "mix")
    xr, hr, z = _rnn_fwd(proj, p["rnn_conv_w"], p["rnn_conv_b"], p["rg_w_a"], p["rg_b_a"],
                         p["rg_w_x"], p["rg_b_x"], p["rg_lambda"])
    vc, s = _conv_fwd(proj, p["conv_dw_w"], p["conv_dw_b"], p["conv_ln_g"], p["conv_ln_b"])
    h2 = _merge_fwd(h1, z, s, proj, p["rnn_w_proj"], p["conv_w_proj"], p["conv_b_proj"], p["w_out"])
    finish(*group(later[2]), h2, "ffn2")
    h3, gate2, up2, n3 = _ffn_fwd(h2, p["ffn2_norm"], p["ffn2_w_gu"], p["ffn2_w_down"], "ffn2_fwd")
    dh3, loss_blk, d_final = _final_loss(h3, p["final_norm"], tgt, n_valid)
    loss = lax.psum(loss_blk[0, 0], ("x", "y", "c"))

    g = {"final_norm": d_final}
    pending = []

    def reduce_start(names, tag):
        parts = []
        for k in names:
            rows = g[k].size // (NCHIP * g[k].shape[-1])
            parts.append(g[k].reshape((NCHIP, 2, rows // 2, g[k].shape[-1])))
        from_sibling = _pair_exchange(parts, "pair_exchange_" + tag)
        added = [_pair_add(pp, gg, qc, "pair_add_" + k) for pp, gg, k in zip(parts, from_sibling, names)]
        send, recv, sums, lands, token = _reduce_start([a for a, _ in added], [b for _, b in added],
                                                       "reduce_start_" + tag)
        pending.append((names, tag, send, recv, sums, lands))
        return token

    dh2, dgate2, dup2, a2, df2, g["ffn2_norm"] = _ffn_bwd(
        dh3, h2, p["ffn2_norm"], gate2, up2, p["ffn2_w_gu"], p["ffn2_w_down"], "ffn2_bwd")
    g["ffn2_w_gu"] = _ffn_gu_grad(n3, dgate2, dup2, "ffn2")
    g["ffn2_w_down"] = _ffn_down_grad(a2, df2, "ffn2")
    token = reduce_start(["ffn2_w_gu", "ffn2_w_down"], "ffn2")

    dz, ds, dproj, dh2b, merged, dya, dyb, g["conv_b_proj"] = _merge_bwd(
        dh2, z, s, proj, p["rnn_w_proj"], p["conv_w_proj"], p["conv_b_proj"], p["w_out"], after=token)
    g["w_out"] = _square_grad(merged, dh2b, "dw_out")
    g["rnn_w_proj"] = _square_grad(z, dya, "dw_rnn_proj")
    g["conv_w_proj"] = _square_grad(s, dyb, "dw_conv_proj")
    dproj, g["conv_dw_w"], g["conv_dw_b"], g["conv_ln_g"], g["conv_ln_b"] = _conv_bwd(
        ds, vc, proj, dproj, p["conv_dw_w"], p["conv_ln_g"], p["conv_ln_b"])
    (dproj, g["rg_w_a"], g["rg_w_x"], g["rnn_conv_w"], g["rnn_conv_b"], g["rg_b_a"], g["rg_b_x"],
     g["rg_lambda"]) = _rnn_bwd(dz, xr, hr, proj, dproj, p["rnn_conv_w"], p["rg_w_a"], p["rg_b_a"],
                                p["rg_w_x"], p["rg_b_x"], p["rg_lambda"])
    token = reduce_start(["w_out", "rnn_w_proj", "conv_w_proj", "rg_w_a", "rg_w_x"], "mix")

    dh1, g["mix_norm"], db_in = _inproj_bwd(dproj, dh2, h1, p["mix_norm"], p["w_in"], after=token)
    g["b_in"] = db_in.reshape(1, NIN)
    g["w_in"] = _tn_matmul(n2, dproj, D, NIN // NCHIP, (NCHIP, D, NIN // NCHIP),
                           (None, D, NIN // NCHIP), lambda k, nn, mm: (nn, 0, 0), "dw_in")
    token = reduce_start(["w_in"], "in")

    dh0, dgate1, dup1, a1, df1, g["ffn1_norm"] = _ffn_bwd(
        dh1, h0, p["ffn1_norm"], gate1, up1, p["ffn1_w_gu"], p["ffn1_w_down"], "ffn1_bwd", after=token)
    g["meta_tokens"] = dh0[0:NMETA]
    grad_x = dh0[NMETA:n_valid][None]

    send_s, recv_s, pack_buf, token = _gather_all_start(_place_pack(_small_pack(g), qc), "gather_all_start")
    g["ffn1_w_down"] = _ffn_down_grad(a1, df1, "ffn1", after=token)
    token = reduce_start(["ffn1_w_down"], "ffn1_down")
    g["ffn1_w_gu"] = _ffn_gu_grad(n1, dgate1, dup1, "ffn1", after=token)
    token = reduce_start(["ffn1_w_gu"], "ffn1_gu")
    packs = _gather_all_wait(send_s, recv_s, pack_buf, token, "gather_all_wait")

    grads, deltas, new_m, new_v = {}, {}, {}, {}

    def reduce_finish(items, after, tag):
        names, mine = [], []
        for grp_names, grp_tag, send, recv, sums, lands in items:
            landed = _reduce_wait(send, recv, sums, lands, after, "reduce_wait_" + grp_tag)
            mine += [_sum_chips(b, "sum_chips_" + k) for b, k in zip(landed, grp_names)]
            names += grp_names
            after = mine[-1]
        theirs = _pair_share(mine, "pair_share_" + tag)
        for k, mi, th in zip(names, mine, theirs):
            outs = _adamw(_as2d(w[k]), mi, th, _as2d(m[k]), _as2d(v[k]), qc, "adamw_" + k)
            grads[k], deltas[k], new_m[k], new_v[k] = (a.reshape(w[k].shape) for a in outs)
        return new_v[names[-1]]

    after = reduce_finish(pending[:3], packs, "early")
    reduce_finish(pending[3:], after, "late")
    names = [k for k, _ in SMALL]
    shape2 = {k: ((1, rows * D) if (k, rows) in REPL else (rows, CS)) for k, rows in SMALL}
    outs = _adamw_small(packs, *[[a[k].reshape(shape2[k]) for k in names] for a in (w, m, v)])
    ns = len(names)
    for i, k in enumerate(names):
        grads[k], deltas[k], new_m[k], new_v[k] = (outs[j * ns + i].reshape(w[k].shape) for j in range(4))

    return (loss, grad_x, *[grads[k] for k in WEIGHTS], *[deltas[k] for k in WEIGHTS],
            *[new_m[k] for k in WEIGHTS], *[new_v[k] for k in WEIGHTS])


def kernel(x, meta_tokens, ffn1_norm, ffn1_w_gu, ffn1_w_down, mix_norm, w_in, b_in, rnn_conv_w, rnn_conv_b, rg_w_a, rg_b_a, rg_w_x, rg_b_x, rg_lambda, rnn_w_proj, conv_dw_w, conv_dw_b, conv_ln_g, conv_ln_b, conv_w_proj, conv_b_proj, w_out, ffn2_norm, ffn2_w_gu, ffn2_w_down, final_norm, loss_target, m_meta_tokens, m_ffn1_norm, m_ffn1_w_gu, m_ffn1_w_down, m_mix_norm, m_w_in, m_b_in, m_rnn_conv_w, m_rnn_conv_b, m_rg_w_a, m_rg_b_a, m_rg_w_x, m_rg_b_x, m_rg_lambda, m_rnn_w_proj, m_conv_dw_w, m_conv_dw_b, m_conv_ln_g, m_conv_ln_b, m_conv_w_proj, m_conv_b_proj, m_w_out, m_ffn2_norm, m_ffn2_w_gu, m_ffn2_w_down, m_final_norm, v_meta_tokens, v_ffn1_norm, v_ffn1_w_gu, v_ffn1_w_down, v_mix_norm, v_w_in, v_b_in, v_rnn_conv_w, v_rnn_conv_b, v_rg_w_a, v_rg_b_a, v_rg_w_x, v_rg_b_x, v_rg_lambda, v_rnn_w_proj, v_conv_dw_w, v_conv_dw_b, v_conv_ln_g, v_conv_ln_b, v_conv_w_proj, v_conv_b_proj, v_w_out, v_ffn2_norm, v_ffn2_w_gu, v_ffn2_w_down, v_final_norm):
    args = locals()
    w = {k: args[k] for k in WEIGHTS}
    m = {k: args["m_" + k] for k in WEIGHTS}
    v = {k: args["v_" + k] for k in WEIGHTS}
    return _step(x, loss_target, w, m, v)
```

```python
import functools

import jax
import jax.numpy as jnp
from jax import lax
from jax.experimental import pallas as pl
from jax.experimental.pallas import tpu as pltpu

f32 = jnp.float32
bf16 = jnp.bfloat16

D = 1024
F = 2816
FS = F // 2
NIN = 6 * D
NMETA = 16
NHEAD = 4
HD = D // NHEAD
KC4 = 4
KC31 = 31
HALO = 32
EPS = 1e-6
TM = 384
NCHIP = 4
MESH = pl.DeviceIdType.MESH

ADAM_LR = 0.001
ADAM_B1 = 0.9
ADAM_B2 = 0.999
ADAM_EPS = 1e-08
ADAM_WD = 0.01
ADAM_STEP = 10

VMEM_LIMIT = 56 * 1024 * 1024


def _cp(n_axes, **kw):
    return pltpu.CompilerParams(dimension_semantics=("arbitrary",) * n_axes,
                                vmem_limit_bytes=VMEM_LIMIT, **kw)


RESIDENT = pl.BlockSpec(memory_space=pltpu.VMEM)


def _ordered(body, in_specs, args, after):
    if after is None:
        return body, in_specs, args
    return (lambda first, *refs: body(*refs),
            [pl.BlockSpec(memory_space=pl.ANY)] + list(in_specs), (after,) + tuple(args))


def _nt_dot(a, b):
    return lax.dot_general(a, b, (((1,), (1,)), ((), ())), preferred_element_type=f32)


def _tn_dot(a, b):
    return lax.dot_general(a, b, (((0,), (0,)), ((), ())), preferred_element_type=f32)


def _sigmoid(x):
    return 1.0 / (1.0 + jnp.exp(-x))


def _log1p(y):
    u = 1.0 + y
    d = u - 1.0
    return jnp.where(d == 0.0, y, jnp.log(u) * (y / jnp.where(d == 0.0, 1.0, d)))


def _softplus(x):
    return jnp.maximum(x, 0.0) + _log1p(jnp.exp(-jnp.abs(x)))


def _expm1(x):
    series = x * (1.0 + x * (0.5 + x * (1.0 / 6.0 + x * (1.0 / 24.0 + x * (1.0 / 120.0)))))
    return jnp.where(jnp.abs(x) < 0.1, series, jnp.exp(x) - 1.0)


_GELU_C = 0.7978845608028654
_GELU_K = 0.044715


def _gelu_and_grad(y):
    y2 = y * y
    th = jnp.tanh(_GELU_C * (y + _GELU_K * y * y2))
    gel = 0.5 * y * (1.0 + th)
    dgel = 0.5 * (1.0 + th) + 0.5 * y * (1.0 - th * th) * _GELU_C * (1.0 + 3.0 * _GELU_K * y2)
    return gel, dgel


def _rms_stats(h):
    return lax.rsqrt(jnp.mean(h * h, axis=-1, keepdims=True) + EPS)


def _rms_bwd(dn, h, g):
    r = _rms_stats(h)
    nhat = h * r
    dnh = dn * g
    dh = r * (dnh - nhat * jnp.mean(dnh * nhat, axis=-1, keepdims=True))
    dg = jnp.sum(dn * nhat, axis=0, keepdims=True)
    return dh, dg


def _row_ids(shape):
    return lax.broadcasted_iota(jnp.int32, shape, 0)


def _ffn_fwd(h, g, wgu, wd, name):
    t = h.shape[0]
    nj = 2

    def body(h_ref, g_ref, wg_ref, wd_ref, ho_ref, gate_ref, up_ref, n_ref, nb_sc, acc_sc):
        j = pl.program_id(1)

        @pl.when(j == 0)
        def _():
            hh = h_ref[...]
            nb = (hh * _rms_stats(hh) * g_ref[...]).astype(bf16)
            nb_sc[...] = nb
            n_ref[...] = nb
            acc_sc[...] = jnp.zeros_like(acc_sc)

        nb = nb_sc[...]
        gt = jnp.dot(nb, wg_ref[j], preferred_element_type=f32)
        up = jnp.dot(nb, wg_ref[2 + j], preferred_element_type=f32)
        gate_ref[...] = gt.astype(bf16)
        up_ref[...] = up.astype(bf16)
        a = (gt * _sigmoid(gt) * up).astype(bf16)
        acc_sc[...] += jnp.dot(a, wd_ref[j], preferred_element_type=f32)

        @pl.when(j == nj - 1)
        def _():
            ho_ref[...] = h_ref[...] + 0.5 * acc_sc[...]

    return pl.pallas_call(
        body, name=name, grid=(t // TM, nj),
        in_specs=[
            pl.BlockSpec((TM, D), lambda i, j: (i, 0)),
            pl.BlockSpec((1, D), lambda i, j: (0, 0)),
            RESIDENT, RESIDENT,
        ],
        out_specs=[
            pl.BlockSpec((TM, D), lambda i, j: (i, 0)),
            pl.BlockSpec((TM, FS), lambda i, j: (i, j)),
            pl.BlockSpec((TM, FS), lambda i, j: (i, j)),
            pl.BlockSpec((TM, D), lambda i, j: (i, 0)),
        ],
        out_shape=[
            jax.ShapeDtypeStruct((t, D), f32),
            jax.ShapeDtypeStruct((t, F), bf16),
            jax.ShapeDtypeStruct((t, F), bf16),
            jax.ShapeDtypeStruct((t, D), bf16),
        ],
        scratch_shapes=[pltpu.VMEM((TM, D), bf16), pltpu.VMEM((TM, D), f32)],
        compiler_params=_cp(2),
    )(h, g, wgu, wd.reshape(nj, FS, D))


def _inproj_fwd(h, g, win, b_in):
    t = h.shape[0]
    tn = NIN // NCHIP
    nj = NIN // tn
    per = (NIN // NCHIP) // tn

    def body(h_ref, g_ref, w_ref, b_ref, proj_ref, n_ref, nb_sc):
        j = pl.program_id(1)

        @pl.when(j == 0)
        def _():
            hh = h_ref[...]
            nb = (hh * _rms_stats(hh) * g_ref[...]).astype(bf16)
            nb_sc[...] = nb
            n_ref[...] = nb

        proj_ref[...] = jnp.dot(nb_sc[...], w_ref[j], preferred_element_type=f32) + b_ref[...]

    return pl.pallas_call(
        body, name="inproj_fwd", grid=(t // TM, nj),
        in_specs=[
            pl.BlockSpec((TM, D), lambda i, j: (i, 0)),
            pl.BlockSpec((1, D), lambda i, j: (0, 0)),
            RESIDENT,
            pl.BlockSpec((1, tn), lambda i, j: (0, j)),
        ],
        out_specs=[
            pl.BlockSpec((TM, tn), lambda i, j: (i, j)),
            pl.BlockSpec((TM, D), lambda i, j: (i, 0)),
        ],
        out_shape=[jax.ShapeDtypeStruct((t, NIN), f32), jax.ShapeDtypeStruct((t, D), bf16)],
        scratch_shapes=[pltpu.VMEM((TM, D), bf16)],
        compiler_params=_cp(2),
    )(h, g, win, b_in)


def _block_gates(xr, wa_ref, ba, wx_ref, bx, lam):
    xrb = xr.astype(bf16)
    pa = jnp.concatenate([jnp.dot(xrb[:, hh * HD:(hh + 1) * HD], wa_ref[hh], preferred_element_type=f32)
                          for hh in range(NHEAD)], axis=1)
    px = jnp.concatenate([jnp.dot(xrb[:, hh * HD:(hh + 1) * HD], wx_ref[hh], preferred_element_type=f32)
                          for hh in range(NHEAD)], axis=1)
    ra = _sigmoid(pa + ba)
    ii = _sigmoid(px + bx)
    sp = _softplus(-lam)
    log_a = -8.0 * ra * sp
    a = jnp.exp(log_a)
    sq = jnp.sqrt(-_expm1(2.0 * log_a))
    return ra, ii, a, sq, sp


def _rnn_fwd(proj, cw, cb, wa, ba, wx, bx, lam):
    t = proj.shape[0]
    ng = TM // 8

    def body(x_ref, y_ref, cw_ref, cb_ref, wa_ref, ba_ref, wx_ref, bx_ref, lam_ref,
             xr_ref, hr_ref, z_ref, xext_sc, carry_sc, a_sc, h_sc):
        i = pl.program_id(0)

        @pl.when(i == 0)
        def _():
            xext_sc[0:8, :] = jnp.zeros((8, D), f32)
            carry_sc[...] = jnp.zeros_like(carry_sc)

        x = x_ref[...]
        xext_sc[8:8 + TM, :] = x
        xe = xext_sc[...]
        xr = cb_ref[...] + cw_ref[KC4 - 1:KC4, :] * x
        for k in range(KC4 - 1):
            xr = xr + cw_ref[k:k + 1, :] * pltpu.roll(xe, KC4 - 1 - k, 0)[8:8 + TM]
        xext_sc[0:8, :] = x[TM - 8:TM]

        _, ii, a, sq, _ = _block_gates(xr, wa_ref, ba_ref[...], wx_ref, bx_ref[...], lam_ref[...])
        a_sc[...] = a
        h_sc[...] = sq * ii * xr
        row = _row_ids((8, D))

        def group(r, carry):
            off = pl.multiple_of(r * 8, 8)
            aa = a_sc[pl.ds(off, 8), :]
            hh = h_sc[pl.ds(off, 8), :]
            for s in (1, 2, 4):
                a_sh = jnp.where(row >= s, pltpu.roll(aa, s, 0), 1.0)
                h_sh = jnp.where(row >= s, pltpu.roll(hh, s, 0), 0.0)
                hh = aa * h_sh + hh
                aa = aa * a_sh
            hh = hh + aa * carry
            h_sc[pl.ds(off, 8), :] = hh
            return hh[7:8, :]

        carry_sc[...] = lax.fori_loop(0, ng, group, carry_sc[...])
        hr = h_sc[...]
        gel, _ = _gelu_and_grad(y_ref[...])
        xr_ref[...] = xr
        hr_ref[...] = hr
        z_ref[...] = (hr * gel).astype(bf16)

    vec = pl.BlockSpec((1, D), lambda i: (0, 0))
    return pl.pallas_call(
        body, name="rnn_fwd", grid=(t // TM,),
        in_specs=[
            pl.BlockSpec((TM, D), lambda i: (i, 0)),
            pl.BlockSpec((TM, D), lambda i: (i, 1)),
            pl.BlockSpec((KC4, D), lambda i: (0, 0)),
            vec,
            pl.BlockSpec((NHEAD, HD, HD), lambda i: (0, 0, 0)),
            vec,
            pl.BlockSpec((NHEAD, HD, HD), lambda i: (0, 0, 0)),
            vec, vec,
        ],
        out_specs=[pl.BlockSpec((TM, D), lambda i: (i, 0))] * 3,
        out_shape=[jax.ShapeDtypeStruct((t, D), f32), jax.ShapeDtypeStruct((t, D), f32),
                   jax.ShapeDtypeStruct((t, D), bf16)],
        scratch_shapes=[pltpu.VMEM((TM + 8, D), f32), pltpu.VMEM((1, D), f32),
                        pltpu.VMEM((TM, D), f32), pltpu.VMEM((TM, D), f32)],
        compiler_params=_cp(1),
    )(proj, proj, cw, cb, wa, ba, wx, bx, lam)


def _ln_stats(vc):
    mu = jnp.mean(vc, axis=-1, keepdims=True)
    xc = vc - mu
    rstd = lax.rsqrt(jnp.mean(xc * xc, axis=-1, keepdims=True) + EPS)
    return xc * rstd, rstd


def _conv_fwd(proj, w31, b31, ln_g, ln_b):
    t = proj.shape[0]

    def body(gv_ref, gg_ref, w_ref, b_ref, lg_ref, lb_ref, vc_ref, s_ref, vext_sc):
        i = pl.program_id(0)

        @pl.when(i == 0)
        def _():
            vext_sc[0:HALO, :] = jnp.zeros((HALO, D), f32)

        v = gv_ref[...] * _sigmoid(gg_ref[...])
        vext_sc[HALO:HALO + TM, :] = v
        ve = vext_sc[...]
        acc = jnp.zeros((TM, D), f32) + b_ref[...]
        for s in range(8):
            vs = ve if s == 0 else pltpu.roll(ve, s, 0)
            for m in range(HALO // 8):
                k = KC31 - 1 - (8 * m + s)
                if 0 <= k < KC31:
                    acc = acc + w_ref[k:k + 1, :] * vs[HALO - 8 * m:HALO - 8 * m + TM]
        vext_sc[0:HALO, :] = v[TM - HALO:TM]
        xhat, _ = _ln_stats(acc)
        ln = xhat * lg_ref[...] + lb_ref[...]
        vc_ref[...] = acc
        s_ref[...] = (ln * _sigmoid(ln)).astype(bf16)

    vec = pl.BlockSpec((1, D), lambda i: (0, 0))
    return pl.pallas_call(
        body, name="conv_fwd", grid=(t // TM,),
        in_specs=[
            pl.BlockSpec((TM, D), lambda i: (i, 2)),
            pl.BlockSpec((TM, D), lambda i: (i, 3)),
            pl.BlockSpec((KC31, D), lambda i: (0, 0)),
            vec, vec, vec,
        ],
        out_specs=[pl.BlockSpec((TM, D), lambda i: (i, 0))] * 2,
        out_shape=[jax.ShapeDtypeStruct((t, D), f32), jax.ShapeDtypeStruct((t, D), bf16)],
        scratch_shapes=[pltpu.VMEM((TM + HALO, D), f32)],
        compiler_params=_cp(1),
    )(proj, proj, w31, b31, ln_g, ln_b)


def _merge_fwd(h, z, s, proj, wrp, wcp, bcp, wout):
    t = h.shape[0]

    def body(h_ref, z_ref, s_ref, ga_ref, gb_ref, wrp_ref, wcp_ref, bcp_ref, wout_ref, ho_ref):
        ya = jnp.dot(z_ref[...], wrp_ref[...], preferred_element_type=f32)
        yb = jnp.dot(s_ref[...], wcp_ref[...], preferred_element_type=f32) + bcp_ref[...]
        merged = _sigmoid(ga_ref[...]) * ya + _sigmoid(gb_ref[...]) * yb
        ho_ref[...] = h_ref[...] + jnp.dot(merged.astype(bf16), wout_ref[...], preferred_element_type=f32)

    row = pl.BlockSpec((TM, D), lambda i: (i, 0))
    wsq = pl.BlockSpec((D, D), lambda i: (0, 0))
    return pl.pallas_call(
        body, name="merge_fwd", grid=(t // TM,),
        in_specs=[row, row, row,
                  pl.BlockSpec((TM, D), lambda i: (i, 4)),
                  pl.BlockSpec((TM, D), lambda i: (i, 5)),
                  wsq, wsq, pl.BlockSpec((1, D), lambda i: (0, 0)), wsq],
        out_specs=row,
        out_shape=jax.ShapeDtypeStruct((t, D), f32),
        compiler_params=_cp(1),
    )(h, z, s, proj, proj, wrp, wcp, bcp, wout)


def _final_loss(h, g, tgt, n_valid):
    t = h.shape[0]

    def body(h_ref, g_ref, t_ref, dh_ref, loss_ref, dg_ref):
        i = pl.program_id(0)

        @pl.when(i == 0)
        def _():
            loss_ref[...] = jnp.zeros_like(loss_ref)
            dg_ref[...] = jnp.zeros_like(dg_ref)

        hh = h_ref[...]
        gg = g_ref[...]
        row = i * TM + _row_ids((TM, 1))
        valid = jnp.logical_and(row >= NMETA, row < n_valid)
        out = hh * _rms_stats(hh) * gg
        err = jnp.where(valid, out - t_ref[...], 0.0)
        loss_ref[...] += 0.5 * jnp.sum(err * err) * (1.0 / D)
        dh, dg = _rms_bwd(err * (1.0 / D), hh, gg)
        dh_ref[...] = dh
        dg_ref[...] += dg

    row_spec = pl.BlockSpec((TM, D), lambda i: (i, 0))
    return pl.pallas_call(
        body, name="final_loss", grid=(t // TM,),
        in_specs=[row_spec, pl.BlockSpec((1, D), lambda i: (0, 0)), row_spec],
        out_specs=[row_spec, pl.BlockSpec((8, 128), lambda i: (0, 0)), pl.BlockSpec((1, D), lambda i: (0, 0))],
        out_shape=[jax.ShapeDtypeStruct((t, D), f32), jax.ShapeDtypeStruct((8, 128), f32),
                   jax.ShapeDtypeStruct((1, D), f32)],
        compiler_params=_cp(1),
    )(h, g, tgt)


def _ffn_bwd(dh, h, g, gate, up, wgu, wd, name, after=None):
    t = h.shape[0]
    nj = 2

    def body(dh_ref, h_ref, g_ref, gate_ref, up_ref, wg_ref, wd_ref,
             dhi_ref, dgate_ref, dup_ref, a_ref, df_ref, dg_ref, dfb_sc, dn_sc):
        i = pl.program_id(0)
        j = pl.program_id(1)

        @pl.when(jnp.logical_and(i == 0, j == 0))
        def _():
            dg_ref[...] = jnp.zeros_like(dg_ref)

        @pl.when(j == 0)
        def _():
            dfb = (0.5 * dh_ref[...]).astype(bf16)
            dfb_sc[...] = dfb
            df_ref[...] = dfb
            dn_sc[...] = jnp.zeros_like(dn_sc)

        da = _nt_dot(dfb_sc[...], wd_ref[j])
        gt = gate_ref[...].astype(f32)
        uu = up_ref[...].astype(f32)
        sg = _sigmoid(gt)
        silu = gt * sg
        a_ref[...] = (silu * uu).astype(bf16)
        dgt = (da * uu * (sg * (1.0 + gt * (1.0 - sg)))).astype(bf16)
        dup = (da * silu).astype(bf16)
        dgate_ref[...] = dgt
        dup_ref[...] = dup
        dn_sc[...] += _nt_dot(dgt, wg_ref[j]) + _nt_dot(dup, wg_ref[2 + j])

        @pl.when(j == nj - 1)
        def _():
            dhin, dg = _rms_bwd(dn_sc[...], h_ref[...], g_ref[...])
            dhi_ref[...] = dh_ref[...] + dhin
            dg_ref[...] += dg

    rowd = pl.BlockSpec((TM, D), lambda i, j: (i, 0))
    rowf = pl.BlockSpec((TM, FS), lambda i, j: (i, j))
    vec = pl.BlockSpec((1, D), lambda i, j: (0, 0))
    body, in_specs, args = _ordered(
        body,
        [rowd, rowd, vec, rowf, rowf,
         RESIDENT, RESIDENT],
        (dh, h, g, gate, up, wgu, wd.reshape(nj, FS, D)), after)
    return pl.pallas_call(
        body, name=name, grid=(t // TM, nj),
        in_specs=in_specs,
        out_specs=[rowd, rowf, rowf, rowf, rowd, vec],
        out_shape=[jax.ShapeDtypeStruct((t, D), f32), jax.ShapeDtypeStruct((t, F), bf16),
                   jax.ShapeDtypeStruct((t, F), bf16), jax.ShapeDtypeStruct((t, F), bf16),
                   jax.ShapeDtypeStruct((t, D), bf16), jax.ShapeDtypeStruct((1, D), f32)],
        scratch_shapes=[pltpu.VMEM((TM, D), bf16), pltpu.VMEM((TM, D), f32)],
        compiler_params=_cp(2),
    )(*args)


def _big_tile(t):
    for cand in (2112, 1408, 768, 384):
        if t % cand == 0:
            return cand
    raise ValueError(t)


ANY_SPEC = pl.BlockSpec(memory_space=pl.ANY)


def _tn_matmul(a, b, tk, tn, out_shape, out_block, out_map, name, base=None, after=None):
    t, kk = a.shape
    _, nn = b.shape
    tmm = _big_tile(t)
    nm = t // tmm

    def body(a_ref, b_ref, o_ref, acc_sc):
        m = pl.program_id(2)

        @pl.when(m == 0)
        def _():
            acc_sc[...] = jnp.zeros_like(acc_sc)

        acc_sc[...] += _tn_dot(a_ref[...], b_ref[...])

        @pl.when(m == nm - 1)
        def _():
            o_ref[...] = acc_sc[...].astype(o_ref.dtype)

    in_specs = [pl.BlockSpec((tmm, tk), lambda k, n, m: (m, k)),
                pl.BlockSpec((tmm, tn), lambda k, n, m: (m, n))]
    args, aliases = (a, b), {}
    if base is not None:
        body = (lambda inner: lambda a_ref, b_ref, base_ref, o_ref, acc_sc: inner(a_ref, b_ref, o_ref, acc_sc))(body)
        in_specs, args, aliases = in_specs + [ANY_SPEC], (a, b, base), {2: 0}
    if after is not None:
        body, in_specs, args = _ordered(body, in_specs, args, after)
        aliases = {k + 1: v for k, v in aliases.items()}
    return pl.pallas_call(
        body, name=name, grid=(kk // tk, nn // tn, nm),
        in_specs=in_specs,
        out_specs=pl.BlockSpec(out_block, out_map),
        out_shape=jax.ShapeDtypeStruct(out_shape, bf16),
        scratch_shapes=[pltpu.VMEM((tk, tn), f32)],
        input_output_aliases=aliases,
        compiler_params=_cp(3),
    )(*args)


def _merge_bwd(dh, z, s, proj, wrp, wcp, bcp, wout, after=None):
    t = dh.shape[0]

    def body(dh_ref, z_ref, s_ref, ga_ref, gb_ref, wrp_ref, wcp_ref, bcp_ref, wout_ref,
             dz_ref, ds_ref, dgab_ref, dhb_ref, mg_ref, dya_ref, dyb_ref, dbcp_ref):
        i = pl.program_id(0)

        @pl.when(i == 0)
        def _():
            dbcp_ref[...] = jnp.zeros_like(dbcp_ref)

        dhb = dh_ref[...].astype(bf16)
        dhb_ref[...] = dhb
        dmg = _nt_dot(dhb, wout_ref[...])
        ya = jnp.dot(z_ref[...], wrp_ref[...], preferred_element_type=f32)
        yb = jnp.dot(s_ref[...], wcp_ref[...], preferred_element_type=f32) + bcp_ref[...]
        sa = _sigmoid(ga_ref[...])
        sb = _sigmoid(gb_ref[...])
        mg_ref[...] = (sa * ya + sb * yb).astype(bf16)
        dgab_ref[:, 0:D] = (dmg * ya * sa * (1.0 - sa)).astype(bf16)
        dgab_ref[:, D:2 * D] = (dmg * yb * sb * (1.0 - sb)).astype(bf16)
        dya = dmg * sa
        dyb = dmg * sb
        dbcp_ref[...] += jnp.sum(dyb, axis=0, keepdims=True)
        dyab = dya.astype(bf16)
        dybb = dyb.astype(bf16)
        dya_ref[...] = dyab
        dyb_ref[...] = dybb
        dz_ref[...] = _nt_dot(dyab, wrp_ref[...])
        ds_ref[...] = _nt_dot(dybb, wcp_ref[...])

    row = pl.BlockSpec((TM, D), lambda i: (i, 0))
    wsq = pl.BlockSpec((D, D), lambda i: (0, 0))
    vec = pl.BlockSpec((1, D), lambda i: (0, 0))
    rowb = jax.ShapeDtypeStruct((t, D), bf16)
    body, in_specs, args = _ordered(
        body,
        [row, row, row,
         pl.BlockSpec((TM, D), lambda i: (i, 4)),
         pl.BlockSpec((TM, D), lambda i: (i, 5)),
         wsq, wsq, vec, wsq],
        (dh, z, s, proj, proj, wrp, wcp, bcp, wout), after)
    return pl.pallas_call(
        body, name="merge_bwd", grid=(t // TM,),
        in_specs=in_specs,
        out_specs=[row, row,
                   pl.BlockSpec((TM, 2 * D), lambda i: (i, 2)),
                   row, row, row, row, vec],
        out_shape=[jax.ShapeDtypeStruct((t, D), f32), jax.ShapeDtypeStruct((t, D), f32),
                   jax.ShapeDtypeStruct((t, NIN), bf16),
                   rowb, rowb, rowb, rowb, jax.ShapeDtypeStruct((1, D), f32)],
        compiler_params=_cp(1),
    )(*args)


def _conv_bwd(ds, vc, proj, dproj, w31, ln_g, ln_b):
    t = ds.shape[0]
    nt = t // TM
    hb = TM // HALO

    def body(ds_ref, vc_ref, gv_ref, gg_ref, gvp_ref, ggp_ref, dpin_ref, w_ref, lg_ref, lb_ref,
             dgvg_ref, dw_ref, db_ref, dlg_ref, dlb_ref, dext_sc, vext_sc, dwacc_sc):
        del dpin_ref
        i = pl.program_id(0)
        tile = nt - 1 - i

        @pl.when(i == 0)
        def _():
            dext_sc[TM:TM + HALO, :] = jnp.zeros((HALO, D), f32)
            dwacc_sc[...] = jnp.zeros_like(dwacc_sc)
            db_ref[...] = jnp.zeros_like(db_ref)
            dlg_ref[...] = jnp.zeros_like(dlg_ref)
            dlb_ref[...] = jnp.zeros_like(dlb_ref)

        vc = vc_ref[...]
        xhat, rstd = _ln_stats(vc)
        lg = lg_ref[...]
        ln = xhat * lg + lb_ref[...]
        sg = _sigmoid(ln)
        dln = ds_ref[...] * (sg * (1.0 + ln * (1.0 - sg)))
        dlg_ref[...] += jnp.sum(dln * xhat, axis=0, keepdims=True)
        dlb_ref[...] += jnp.sum(dln, axis=0, keepdims=True)
        dxh = dln * lg
        dvc = rstd * (dxh - jnp.mean(dxh, axis=-1, keepdims=True)
                      - xhat * jnp.mean(dxh * xhat, axis=-1, keepdims=True))
        db_ref[...] += jnp.sum(dvc, axis=0, keepdims=True)

        dext_sc[0:TM, :] = dvc
        de = dext_sc[...]
        dv = jnp.zeros((TM, D), f32)
        for s in range(8):
            dsh = de if s == 0 else pltpu.roll(de, TM + HALO - s, 0)
            for m in range(HALO // 8):
                k = KC31 - 1 - (8 * m + s)
                if 0 <= k < KC31:
                    dv = dv + w_ref[k:k + 1, :] * dsh[8 * m:8 * m + TM]
        dext_sc[TM:TM + HALO, :] = dvc[0:HALO]

        gv = gv_ref[...]
        sgg = _sigmoid(gg_ref[...])
        dgvg_ref[:, 0:D] = (dv * sgg).astype(bf16)
        dgvg_ref[:, D:2 * D] = (dv * gv * sgg * (1.0 - sgg)).astype(bf16)

        vprev = gvp_ref[...] * _sigmoid(ggp_ref[...])
        vext_sc[0:HALO, :] = jnp.where(tile > 0, vprev, 0.0)
        vext_sc[HALO:HALO + TM, :] = gv * sgg
        ve = vext_sc[...]
        for s in range(8):
            vs = ve if s == 0 else pltpu.roll(ve, s, 0)
            for m in range(HALO // 8):
                k = KC31 - 1 - (8 * m + s)
                if 0 <= k < KC31:
                    prod = dvc * vs[HALO - 8 * m:HALO - 8 * m + TM]
                    dwacc_sc[k] += jnp.sum(prod.reshape(TM // 8, 8, D), axis=0)

        @pl.when(i == nt - 1)
        def _():
            for k in range(KC31):
                dw_ref[k:k + 1, :] = jnp.sum(dwacc_sc[k], axis=0, keepdims=True)

    rev = lambda i: (nt - 1 - i, 0)
    vec = pl.BlockSpec((1, D), lambda i: (0, 0))
    halo_row = lambda i: jnp.maximum((nt - 1 - i) * hb - 1, 0)
    return pl.pallas_call(
        body, name="conv_bwd", grid=(nt,),
        in_specs=[
            pl.BlockSpec((TM, D), rev),
            pl.BlockSpec((TM, D), rev),
            pl.BlockSpec((TM, D), lambda i: (nt - 1 - i, 2)),
            pl.BlockSpec((TM, D), lambda i: (nt - 1 - i, 3)),
            pl.BlockSpec((HALO, D), lambda i: (halo_row(i), 2)),
            pl.BlockSpec((HALO, D), lambda i: (halo_row(i), 3)),
            pl.BlockSpec(memory_space=pl.ANY),
            pl.BlockSpec((KC31, D), lambda i: (0, 0)),
            vec, vec,
        ],
        out_specs=[
            pl.BlockSpec((TM, 2 * D), lambda i: (nt - 1 - i, 1)),
            pl.BlockSpec((KC31, D), lambda i: (0, 0)),
            vec, vec, vec,
        ],
        out_shape=[jax.ShapeDtypeStruct((t, NIN), bf16),
                   jax.ShapeDtypeStruct((KC31, D), f32),
                   jax.ShapeDtypeStruct((1, D), f32), jax.ShapeDtypeStruct((1, D), f32),
                   jax.ShapeDtypeStruct((1, D), f32)],
        scratch_shapes=[pltpu.VMEM((TM + HALO, D), f32), pltpu.VMEM((TM + HALO, D), f32),
                        pltpu.VMEM((KC31, 8, D), f32)],
        input_output_aliases={6: 0},
        compiler_params=_cp(1),
    )(ds, vc, proj, proj, proj, proj, dproj, w31, ln_g, ln_b)


def _rnn_bwd(dz, xr, hr, proj, dproj, cw, wa, ba, wx, bx, lam):
    t = dz.shape[0]
    nt = t // TM
    ng = TM // 8
    hq = HD // NCHIP

    def body(dz_ref, xr_ref, hr_ref, hrp_ref, x_ref, xp_ref, y_ref, dpin_ref,
             cw_ref, wa_ref, ba_ref, wx_ref, bx_ref, lam_ref,
             dxy_ref, dwa_ref, dwx_ref, dcw_ref, dcb_ref, dba_ref, dbx_ref, dlam_ref,
             anext_sc, gcarry_sc, dext_sc, xext_sc, m_sc, g_sc, dwa_sc, dwx_sc, dsp_sc):
        del dpin_ref
        i = pl.program_id(0)
        tile = nt - 1 - i

        @pl.when(i == 0)
        def _():
            anext_sc[...] = jnp.zeros_like(anext_sc)
            gcarry_sc[...] = jnp.zeros_like(gcarry_sc)
            dext_sc[TM:TM + 8, :] = jnp.zeros((8, D), f32)
            dwa_sc[...] = jnp.zeros_like(dwa_sc)
            dwx_sc[...] = jnp.zeros_like(dwx_sc)
            dsp_sc[...] = jnp.zeros_like(dsp_sc)
            dcw_ref[...] = jnp.zeros_like(dcw_ref)
            dcb_ref[...] = jnp.zeros_like(dcb_ref)
            dba_ref[...] = jnp.zeros_like(dba_ref)
            dbx_ref[...] = jnp.zeros_like(dbx_ref)

        xr = xr_ref[...]
        hr = hr_ref[...]
        dz = dz_ref[...]
        gel, dgel = _gelu_and_grad(y_ref[...])
        dxy_ref[:, D:2 * D] = (dz * hr * dgel).astype(bf16)
        ra, ii, a, sq, sp = _block_gates(xr, wa_ref, ba_ref[...], wx_ref, bx_ref[...], lam_ref[...])

        row = _row_ids((TM, D))
        m_sc[...] = jnp.where(row == TM - 1, anext_sc[...], pltpu.roll(a, TM - 1, 0))
        anext_sc[...] = a[0:1, :]
        g_sc[...] = dz * gel
        row8 = _row_ids((8, D))

        def group(qq, carry):
            off = pl.multiple_of((ng - 1 - qq) * 8, 8)
            mm = m_sc[pl.ds(off, 8), :]
            dd = g_sc[pl.ds(off, 8), :]
            for s in (1, 2, 4):
                m_sh = jnp.where(row8 < 8 - s, pltpu.roll(mm, 8 - s, 0), 1.0)
                d_sh = jnp.where(row8 < 8 - s, pltpu.roll(dd, 8 - s, 0), 0.0)
                dd = dd + mm * d_sh
                mm = mm * m_sh
            dd = dd + mm * carry
            g_sc[pl.ds(off, 8), :] = dd
            return dd[0:1, :]

        gcarry_sc[...] = lax.fori_loop(0, ng, group, gcarry_sc[...])
        gg = g_sc[...]

        hlast = jnp.where(tile > 0, hrp_ref[7:8, :], 0.0)
        hprev = jnp.where(row == 0, hlast, pltpu.roll(hr, 1, 0))
        d_a = gg * hprev
        dsq = gg * ii * xr
        dii = gg * sq * xr
        dxr = gg * sq * ii
        dlog = d_a * a - dsq * (a * a / sq)
        dsp_sc[...] += jnp.sum(dlog * (-8.0 * ra), axis=0, keepdims=True)
        dpa = dlog * (-8.0 * sp) * ra * (1.0 - ra)
        dpx = dii * ii * (1.0 - ii)
        dba_ref[...] += jnp.sum(dpa, axis=0, keepdims=True)
        dbx_ref[...] += jnp.sum(dpx, axis=0, keepdims=True)
        dpab = dpa.astype(bf16)
        dpxb = dpx.astype(bf16)
        xrb = xr.astype(bf16)
        back = []
        for hh in range(NHEAD):
            cols = slice(hh * HD, (hh + 1) * HD)
            back.append(_nt_dot(dpab[:, cols], wa_ref[hh]) + _nt_dot(dpxb[:, cols], wx_ref[hh]))
            dwa_sc[hh] += _tn_dot(xrb[:, cols], dpab[:, cols])
            dwx_sc[hh] += _tn_dot(xrb[:, cols], dpxb[:, cols])
        dxr = dxr + jnp.concatenate(back, axis=1)

        dext_sc[0:TM, :] = dxr
        de = dext_sc[...]
        dx = cw_ref[KC4 - 1:KC4, :] * dxr
        for k in range(KC4 - 1):
            dx = dx + cw_ref[k:k + 1, :] * pltpu.roll(de, TM + 8 - (KC4 - 1 - k), 0)[0:TM]
        dext_sc[TM:TM + 8, :] = dxr[0:8]
        dxy_ref[:, 0:D] = dx.astype(bf16)

        x = x_ref[...]
        xext_sc[0:8, :] = jnp.where(tile > 0, xp_ref[...], 0.0)
        xext_sc[8:8 + TM, :] = x
        xe = xext_sc[...]
        dcw_ref[KC4 - 1:KC4, :] += jnp.sum(dxr * x, axis=0, keepdims=True)
        for k in range(KC4 - 1):
            xs = pltpu.roll(xe, KC4 - 1 - k, 0)[8:8 + TM]
            dcw_ref[k:k + 1, :] += jnp.sum(dxr * xs, axis=0, keepdims=True)
        dcb_ref[...] += jnp.sum(dxr, axis=0, keepdims=True)

        @pl.when(i == nt - 1)
        def _():
            for hh in range(NHEAD):
                for qc in range(NCHIP):
                    dwa_ref[qc, hh] = dwa_sc[hh, qc * hq:(qc + 1) * hq, :].astype(bf16)
                    dwx_ref[qc, hh] = dwx_sc[hh, qc * hq:(qc + 1) * hq, :].astype(bf16)
            dlam_ref[...] = -dsp_sc[...] * _sigmoid(-lam_ref[...])

    rev = lambda i: (nt - 1 - i, 0)
    vec = pl.BlockSpec((1, D), lambda i: (0, 0))
    prev8 = lambda i: jnp.maximum((nt - 1 - i) * ng - 1, 0)
    wblk = pl.BlockSpec((NHEAD, HD, HD), lambda i: (0, 0, 0))
    gblk = pl.BlockSpec((NCHIP, NHEAD, hq, HD), lambda i: (0, 0, 0, 0))
    return pl.pallas_call(
        body, name="rnn_bwd", grid=(nt,),
        in_specs=[
            pl.BlockSpec((TM, D), rev),
            pl.BlockSpec((TM, D), rev),
            pl.BlockSpec((TM, D), rev),
            pl.BlockSpec((8, D), lambda i: (prev8(i), 0)),
            pl.BlockSpec((TM, D), lambda i: (nt - 1 - i, 0)),
            pl.BlockSpec((8, D), lambda i: (prev8(i), 0)),
            pl.BlockSpec((TM, D), lambda i: (nt - 1 - i, 1)),
            pl.BlockSpec(memory_space=pl.ANY),
            pl.BlockSpec((KC4, D), lambda i: (0, 0)),
            wblk, vec, wblk, vec, vec,
        ],
        out_specs=[
            pl.BlockSpec((TM, 2 * D), lambda i: (nt - 1 - i, 0)),
            gblk, gblk,
            pl.BlockSpec((KC4, D), lambda i: (0, 0)),
            vec, vec, vec, vec,
        ],
        out_shape=[jax.ShapeDtypeStruct((t, NIN), bf16),
                   jax.ShapeDtypeStruct((NCHIP, NHEAD, hq, HD), bf16),
                   jax.ShapeDtypeStruct((NCHIP, NHEAD, hq, HD), bf16),
                   jax.ShapeDtypeStruct((KC4, D), f32),
                   jax.ShapeDtypeStruct((1, D), f32), jax.ShapeDtypeStruct((1, D), f32),
                   jax.ShapeDtypeStruct((1, D), f32), jax.ShapeDtypeStruct((1, D), f32)],
        scratch_shapes=[pltpu.VMEM((1, D), f32), pltpu.VMEM((1, D), f32),
                        pltpu.VMEM((TM + 8, D), f32), pltpu.VMEM((TM + 8, D), f32),
                        pltpu.VMEM((TM, D), f32), pltpu.VMEM((TM, D), f32),
                        pltpu.VMEM((NHEAD, HD, HD), f32), pltpu.VMEM((NHEAD, HD, HD), f32),
                        pltpu.VMEM((1, D), f32)],
        input_output_aliases={7: 0},
        compiler_params=_cp(1),
    )(dz, xr, hr, hr, proj, proj, proj, dproj, cw, wa, ba, wx, bx, lam)


def _inproj_bwd(dproj, dh, h, g, win, after=None):
    t = h.shape[0]
    tn = NIN // NCHIP
    nj = NIN // tn
    per = (NIN // NCHIP) // tn

    def body(dp_ref, dh_ref, h_ref, g_ref, w_ref, dhi_ref, dg_ref, db_ref, dn_sc):
        i = pl.program_id(0)
        j = pl.program_id(1)

        @pl.when(jnp.logical_and(i == 0, j == 0))
        def _():
            dg_ref[...] = jnp.zeros_like(dg_ref)
            db_ref[...] = jnp.zeros_like(db_ref)

        @pl.when(j == 0)
        def _():
            dn_sc[...] = jnp.zeros_like(dn_sc)

        dp = dp_ref[...]
        dn_sc[...] += _nt_dot(dp, w_ref[j])
        db_ref[j] += jnp.sum(dp.astype(f32), axis=0, keepdims=True)

        @pl.when(j == nj - 1)
        def _():
            dhin, dg = _rms_bwd(dn_sc[...], h_ref[...], g_ref[...])
            dhi_ref[...] = dh_ref[...] + dhin
            dg_ref[...] += dg

    rowd = pl.BlockSpec((TM, D), lambda i, j: (i, 0))
    vec = pl.BlockSpec((1, D), lambda i, j: (0, 0))
    body, in_specs, args = _ordered(
        body,
        [pl.BlockSpec((TM, tn), lambda i, j: (i, j)), rowd, rowd, vec,
         RESIDENT],
        (dproj, dh, h, g, win), after)
    return pl.pallas_call(
        body, name="inproj_bwd", grid=(t // TM, nj),
        in_specs=in_specs,
        out_specs=[rowd, vec, pl.BlockSpec((nj, 1, tn), lambda i, j: (0, 0, 0))],
        out_shape=[jax.ShapeDtypeStruct((t, D), f32), jax.ShapeDtypeStruct((1, D), f32),
                   jax.ShapeDtypeStruct((nj, 1, tn), f32)],
        scratch_shapes=[pltpu.VMEM((TM, D), f32)],
        compiler_params=_cp(2),
    )(*args)


def _ffn_gu_grad(n, dgate, dup, tag, after=None):
    half = _tn_matmul(n, dgate, D, FS, (NCHIP, D, FS), (None, D, FS), lambda k, nn, m: (nn, 0, 0),
                      tag + "_dwg", after=after)
    return _tn_matmul(n, dup, D, FS, (NCHIP, D, FS), (None, D, FS), lambda k, nn, m: (2 + nn, 0, 0),
                      tag + "_dwu", base=half)


def _ffn_down_grad(a, df, tag, after=None):
    return _tn_matmul(a, df, FS, D, (F, D), (FS, D), lambda k, nn, m: (k, 0), tag + "_dwd", after=after)


def _square_grad(a, b, name):
    return _tn_matmul(a, b, D, D, (D, D), (D, D), lambda k, nn, m: (0, 0), name)


ANY = pl.BlockSpec(memory_space=pl.ANY)


def _place():
    x, y, c = lax.axis_index("x"), lax.axis_index("y"), lax.axis_index("c")
    chips = [(1 - x, y), (x, 1 - y), (1 - x, 1 - y)]
    return x, y, c, chips


def _chip_id(chip):
    return 2 * chip[0] + chip[1]


def _cast_into_slot(w2d, qc, dtype, name, after=None):
    r, cc = w2d.shape
    hr = r // 2

    def body(qc_ref, *refs):
        del qc_ref
        w_ref, o_ref = refs[-2:]
        o_ref[...] = w_ref[...].astype(dtype)

    in_specs, args = [pl.BlockSpec((hr, cc), lambda h, qc_ref: (h, 0))], (w2d,)
    if after is not None:
        in_specs, args = [ANY_SPEC] + in_specs, (after,) + args
    return pl.pallas_call(
        body, name=name,
        grid_spec=pltpu.PrefetchScalarGridSpec(
            num_scalar_prefetch=1, grid=(2,),
            in_specs=in_specs,
            out_specs=pl.BlockSpec((None, None, hr, cc), lambda h, qc_ref: (qc_ref[0], h, 0, 0))),
        out_shape=jax.ShapeDtypeStruct((NCHIP, 2, hr, cc), dtype),
        compiler_params=_cp(1),
    )(qc, *args)


def _place_pack(pack, qc):
    def body(qc_ref, p_ref, o_ref):
        del qc_ref
        o_ref[...] = p_ref[...]

    return pl.pallas_call(
        body, name="place_pack",
        grid_spec=pltpu.PrefetchScalarGridSpec(
            num_scalar_prefetch=1, grid=(1,),
            in_specs=[pl.BlockSpec(pack.shape, lambda i, qc_ref: (0, 0))],
            out_specs=pl.BlockSpec((None,) + pack.shape, lambda i, qc_ref: (2 * qc_ref[0] + qc_ref[1], 0, 0))),
        out_shape=jax.ShapeDtypeStruct((8,) + pack.shape, pack.dtype),
        compiler_params=_cp(1),
    )(qc, pack)


def _gather_shards(bufs):
    n = len(bufs)

    def body(*refs):
        outs = refs[n:2 * n]
        send_sems, recv_sems = refs[2 * n:]
        x, y, c, chips = _place()
        q = 2 * x + y
        sibling = (x, y, 1 - c)

        def remote(a, k, blk, to):
            return pltpu.make_async_remote_copy(src_ref=blk, dst_ref=blk, send_sem=send_sems.at[a, k],
                                                recv_sem=recv_sems.at[a, k], device_id=to, device_id_type=MESH)

        sent = []
        for a in range(n):
            for j, chip in enumerate(chips):
                cp = remote(a, j, outs[a].at[q, c], (chip[0], chip[1], c))
                cp.start()
                sent.append(cp)
        for a in range(n):
            for j, chip in enumerate(chips):
                got = outs[a].at[_chip_id(chip), c]
                remote(a, j, got, (chip[0], chip[1], c)).wait_recv()
                cp = remote(a, 3 + j, got, sibling)
                cp.start()
                sent.append(cp)
        for a in range(n):
            for j, chip in enumerate(chips):
                remote(a, 3 + j, outs[a].at[_chip_id(chip), 1 - c], sibling).wait_recv()
        for cp in sent:
            cp.wait_send()

    return pl.pallas_call(
        body, name="gather_shards",
        in_specs=[ANY] * n, out_specs=[ANY] * n,
        out_shape=[jax.ShapeDtypeStruct(s.shape, s.dtype) for s in bufs],
        scratch_shapes=[pltpu.SemaphoreType.DMA((n, 6)), pltpu.SemaphoreType.DMA((n, 6))],
        input_output_aliases={a: a for a in range(n)},
    )(*bufs)


def _pair_exchange(parts, name):
    n = len(parts)

    def body(*refs):
        ins, outs = refs[:n], refs[n:2 * n]
        send_sems, recv_sems = refs[2 * n:]
        x, y, c, _ = _place()
        copies = []
        for a in range(n):
            cp = pltpu.make_async_remote_copy(
                src_ref=ins[a].at[:, 1 - c], dst_ref=outs[a],
                send_sem=send_sems.at[a], recv_sem=recv_sems.at[a],
                device_id=(x, y, 1 - c), device_id_type=MESH)
            cp.start()
            copies.append(cp)
        for cp in copies:
            cp.wait()

    return pl.pallas_call(
        body, name=name,
        in_specs=[ANY] * n, out_specs=[ANY] * n,
        out_shape=[jax.ShapeDtypeStruct((NCHIP,) + s.shape[2:], s.dtype) for s in parts],
        scratch_shapes=[pltpu.SemaphoreType.DMA((n,)), pltpu.SemaphoreType.DMA((n,))],
    )(*parts)


def _pair_add(part, got, qc, name):
    _, _, hr, cc = part.shape

    def body(qc_ref, p_ref, g_ref, o_ref, land_ref):
        s = pl.program_id(0)
        val = (p_ref[...].astype(f32) + g_ref[...].astype(f32)).astype(bf16)
        o_ref[...] = val

        @pl.when(s == qc_ref[0])
        def _():
            land_ref[...] = val

    return pl.pallas_call(
        body, name=name,
        grid_spec=pltpu.PrefetchScalarGridSpec(
            num_scalar_prefetch=1, grid=(NCHIP,),
            in_specs=[pl.BlockSpec((None, None, hr, cc), lambda s, qc_ref: (s, qc_ref[1], 0, 0)),
                      pl.BlockSpec((None, hr, cc), lambda s, qc_ref: (s, 0, 0))],
            out_specs=[pl.BlockSpec((None, hr, cc), lambda s, qc_ref: (s, 0, 0)),
                       pl.BlockSpec((None, hr, cc), lambda s, qc_ref: (qc_ref[0], 0, 0))]),
        out_shape=[jax.ShapeDtypeStruct((NCHIP, hr, cc), bf16)] * 2,
        compiler_params=_cp(1),
    )(qc, part, got)


def _chip_exchange(sums, lands):
    n = len(sums)

    def body(*refs):
        ins, outs = refs[:n], refs[2 * n:3 * n]
        send_sems, recv_sems = refs[3 * n:]
        x, y, c, chips = _place()
        q = 2 * x + y
        sent = []
        for a in range(n):
            for j, chip in enumerate(chips):
                cp = pltpu.make_async_remote_copy(
                    src_ref=ins[a].at[_chip_id(chip)], dst_ref=outs[a].at[q],
                    send_sem=send_sems.at[a, j], recv_sem=recv_sems.at[a, j],
                    device_id=(chip[0], chip[1], c), device_id_type=MESH)
                cp.start()
                sent.append(cp)
        for a in range(n):
            for j, chip in enumerate(chips):
                got = outs[a].at[_chip_id(chip)]
                pltpu.make_async_remote_copy(
                    src_ref=got, dst_ref=got, send_sem=send_sems.at[a, j], recv_sem=recv_sems.at[a, j],
                    device_id=(chip[0], chip[1], c), device_id_type=MESH).wait_recv()
        for cp in sent:
            cp.wait_send()

    return pl.pallas_call(
        body, name="chip_exchange",
        in_specs=[ANY] * (2 * n), out_specs=[ANY] * n,
        out_shape=[jax.ShapeDtypeStruct(s.shape, s.dtype) for s in lands],
        scratch_shapes=[pltpu.SemaphoreType.DMA((n, 3)), pltpu.SemaphoreType.DMA((n, 3))],
        input_output_aliases={n + a: a for a in range(n)},
    )(*sums, *lands)


def _sum_chips(got, name):
    _, hr, cc = got.shape

    def body(g_ref, o_ref):
        acc = g_ref[0].astype(f32)
        for s in range(1, NCHIP):
            acc = acc + g_ref[s].astype(f32)
        o_ref[...] = acc

    return pl.pallas_call(
        body, name=name, grid=(1,),
        in_specs=[pl.BlockSpec((NCHIP, hr, cc), lambda i: (0, 0, 0))],
        out_specs=pl.BlockSpec((hr, cc), lambda i: (0, 0)),
        out_shape=jax.ShapeDtypeStruct((hr, cc), f32),
        compiler_params=_cp(1),
    )(got)


def _pair_share(halves, name):
    n = len(halves)

    def body(*refs):
        ins, outs = refs[:n], refs[n:2 * n]
        send_sems, recv_sems = refs[2 * n:]
        x, y, c, _ = _place()
        copies = []
        for a in range(n):
            cp = pltpu.make_async_remote_copy(
                src_ref=ins[a], dst_ref=outs[a], send_sem=send_sems.at[a], recv_sem=recv_sems.at[a],
                device_id=(x, y, 1 - c), device_id_type=MESH)
            cp.start()
            copies.append(cp)
        for cp in copies:
            cp.wait()

    return pl.pallas_call(
        body, name=name,
        in_specs=[ANY] * n, out_specs=[ANY] * n,
        out_shape=[jax.ShapeDtypeStruct(s.shape, s.dtype) for s in halves],
        scratch_shapes=[pltpu.SemaphoreType.DMA((n,)), pltpu.SemaphoreType.DMA((n,))],
    )(*halves)


def _all_copy(buf_ref, send_ref, recv_ref, k, x, y, c, landing):
    px, py, pc = (1 - x if k & 4 else x, 1 - y if k & 2 else y, 1 - c if k & 1 else c)
    me = 4 * x + 2 * y + c
    there = 4 * px + 2 * py + pc
    return pltpu.make_async_remote_copy(
        src_ref=buf_ref.at[me], dst_ref=buf_ref.at[there if landing else me],
        send_sem=send_ref.at[k - 1], recv_sem=recv_ref.at[k - 1],
        device_id=(px, py, pc), device_id_type=MESH)


def _gather_all_start(buf, name):
    def body(in_ref, send, recv, thru, token):
        del thru
        x, y, c, _ = _place()
        for k in range(1, 8):
            _all_copy(in_ref, send, recv, k, x, y, c, False).start()
        token[...] = jnp.zeros_like(token)

    return pl.pallas_call(
        body, name=name,
        in_specs=[HBM],
        out_specs=[SEM, SEM, HBM, pl.BlockSpec(memory_space=pltpu.VMEM)],
        out_shape=[pltpu.SemaphoreType.DMA((7,)), pltpu.SemaphoreType.DMA((7,)),
                   pltpu.HBM(buf.shape, buf.dtype), jax.ShapeDtypeStruct((8, 128), f32)],
        input_output_aliases={0: 2},
        compiler_params=pltpu.CompilerParams(has_side_effects=EFFECT),
    )(_in_hbm(buf))


def _gather_all_wait(send, recv, buf, after, name):
    def body(in_ref, send_r, recv_r, after_ref, out_ref):
        del after_ref, out_ref
        x, y, c, _ = _place()
        for k in range(1, 8):
            cp = _all_copy(in_ref, send_r, recv_r, k, x, y, c, True)
            cp.wait_send()
            cp.wait_recv()

    return pl.pallas_call(
        body, name=name,
        in_specs=[HBM, SEM, SEM, ANY],
        out_specs=HBM,
        out_shape=pltpu.HBM(buf.shape, buf.dtype),
        input_output_aliases={0: 0},
        compiler_params=pltpu.CompilerParams(has_side_effects=EFFECT),
    )(buf, send, recv, after)


HBM = pl.BlockSpec(memory_space=pltpu.HBM)
SEM = pl.BlockSpec(memory_space=pltpu.SEMAPHORE)
EFFECT = pltpu.SideEffectType.DATAFLOW_SIDE_EFFECTING
N_PEER = 3


def _in_hbm(a):
    return pltpu.with_memory_space_constraint(a, pltpu.HBM)


def _tie(a, token):
    return lax.optimization_barrier((a, token))[0]


def _gather_copy(buf_ref, send_ref, recv_ref, j, chip, q, c, landing_chip):
    return pltpu.make_async_remote_copy(
        src_ref=buf_ref.at[q, c], dst_ref=buf_ref.at[landing_chip, c],
        send_sem=send_ref.at[j], recv_sem=recv_ref.at[j],
        device_id=(chip[0], chip[1], c), device_id_type=MESH)


def _gather_start(bufs, name):
    n = len(bufs)

    def body(*refs):
        ins = refs[:n]
        send, recv = refs[n:2 * n], refs[2 * n:3 * n]
        token = refs[4 * n]
        x, y, c, chips = _place()
        q = 2 * x + y
        for a in range(n):
            for j, chip in enumerate(chips):
                _gather_copy(ins[a], send[a], recv[a], j, chip, q, c, q).start()
        token[...] = jnp.zeros_like(token)

    sems = [pltpu.SemaphoreType.DMA((N_PEER,))] * (2 * n)
    outs = pl.pallas_call(
        body, name=name,
        in_specs=[HBM] * n,
        out_specs=[SEM] * (2 * n) + [HBM] * n + [pl.BlockSpec(memory_space=pltpu.VMEM)],
        out_shape=sems + [pltpu.HBM(b.shape, b.dtype) for b in bufs] + [jax.ShapeDtypeStruct((8, 128), f32)],
        input_output_aliases={a: 2 * n + a for a in range(n)},
        compiler_params=pltpu.CompilerParams(has_side_effects=EFFECT),
    )(*[_in_hbm(b) for b in bufs])
    return list(outs[:n]), list(outs[n:2 * n]), list(outs[2 * n:3 * n]), outs[3 * n]


def _gather_wait(send, recv, bufs, after, name):
    n = len(bufs)

    def body(*refs):
        ins = refs[:n]
        send_r, recv_r = refs[n:2 * n], refs[2 * n:3 * n]
        x, y, c, chips = _place()
        q = 2 * x + y
        for a in range(n):
            for j, chip in enumerate(chips):
                cp = _gather_copy(ins[a], send_r[a], recv_r[a], j, chip, q, c, _chip_id(chip))
                cp.wait_send()
                cp.wait_recv()

    afters = after if isinstance(after, (tuple, list)) else (after,)
    outs = pl.pallas_call(
        body, name=name,
        in_specs=[HBM] * n + [SEM] * (2 * n) + [ANY] * len(afters),
        out_specs=[HBM] * n,
        out_shape=[pltpu.HBM(b.shape, b.dtype) for b in bufs],
        input_output_aliases={a: a for a in range(n)},
        compiler_params=pltpu.CompilerParams(has_side_effects=EFFECT),
    )(*bufs, *send, *recv, *afters)
    return list(outs)


def _forward_halves(bufs, name):
    n = len(bufs)

    def body(*refs):
        outs = refs[n:2 * n]
        send_sems, recv_sems = refs[2 * n:]
        x, y, c, chips = _place()
        sibling = (x, y, 1 - c)

        def remote(a, j, blk):
            return pltpu.make_async_remote_copy(src_ref=blk, dst_ref=blk, send_sem=send_sems.at[a, j],
                                                recv_sem=recv_sems.at[a, j], device_id=sibling,
                                                device_id_type=MESH)

        sent = []
        for a in range(n):
            for j, chip in enumerate(chips):
                cp = remote(a, j, outs[a].at[_chip_id(chip), c])
                cp.start()
                sent.append(cp)
        for a in range(n):
            for j, chip in enumerate(chips):
                remote(a, j, outs[a].at[_chip_id(chip), 1 - c]).wait_recv()
        for cp in sent:
            cp.wait_send()

    return pl.pallas_call(
        body, name=name,
        in_specs=[ANY] * n, out_specs=[ANY] * n,
        out_shape=[jax.ShapeDtypeStruct(s.shape, s.dtype) for s in bufs],
        scratch_shapes=[pltpu.SemaphoreType.DMA((n, N_PEER)), pltpu.SemaphoreType.DMA((n, N_PEER))],
        input_output_aliases={a: a for a in range(n)},
    )(*bufs)


def _reduce_copy(sum_ref, land_ref, send_ref, recv_ref, j, chip, q, c, landing_chip):
    return pltpu.make_async_remote_copy(
        src_ref=sum_ref.at[_chip_id(chip)], dst_ref=land_ref.at[landing_chip],
        send_sem=send_ref.at[j], recv_sem=recv_ref.at[j],
        device_id=(chip[0], chip[1], c), device_id_type=MESH)


def _reduce_start(sums, lands, name):
    n = len(sums)

    def body(*refs):
        s_in, l_in = refs[:n], refs[n:2 * n]
        send, recv = refs[2 * n:3 * n], refs[3 * n:4 * n]
        token = refs[6 * n]
        x, y, c, chips = _place()
        q = 2 * x + y
        for a in range(n):
            for j, chip in enumerate(chips):
                _reduce_copy(s_in[a], l_in[a], send[a], recv[a], j, chip, q, c, q).start()
        token[...] = jnp.zeros_like(token)

    sems = [pltpu.SemaphoreType.DMA((N_PEER,))] * (2 * n)
    outs = pl.pallas_call(
        body, name=name,
        in_specs=[HBM] * (2 * n),
        out_specs=[SEM] * (2 * n) + [HBM] * (2 * n) + [pl.BlockSpec(memory_space=pltpu.VMEM)],
        out_shape=sems + [pltpu.HBM(b.shape, b.dtype) for b in list(sums) + list(lands)]
        + [jax.ShapeDtypeStruct((8, 128), f32)],
        input_output_aliases={a: 2 * n + a for a in range(2 * n)},
        compiler_params=pltpu.CompilerParams(has_side_effects=EFFECT),
    )(*[_in_hbm(b) for b in list(sums) + list(lands)])
    return (list(outs[:n]), list(outs[n:2 * n]), list(outs[2 * n:3 * n]), list(outs[3 * n:4 * n]),
            outs[4 * n])


def _reduce_wait(send, recv, sums, lands, after, name):
    n = len(sums)

    def body(*refs):
        s_in, l_in = refs[:n], refs[n:2 * n]
        send_r, recv_r = refs[2 * n:3 * n], refs[3 * n:4 * n]
        x, y, c, chips = _place()
        q = 2 * x + y
        for a in range(n):
            for j, chip in enumerate(chips):
                cp = _reduce_copy(s_in[a], l_in[a], send_r[a], recv_r[a], j, chip, q, c, _chip_id(chip))
                cp.wait_send()
                cp.wait_recv()

    outs = pl.pallas_call(
        body, name=name,
        in_specs=[HBM] * (2 * n) + [SEM] * (2 * n) + [ANY],
        out_specs=[HBM] * (2 * n),
        out_shape=[pltpu.HBM(b.shape, b.dtype) for b in list(sums) + list(lands)],
        input_output_aliases={a: a for a in range(2 * n)},
        compiler_params=pltpu.CompilerParams(has_side_effects=EFFECT),
    )(*sums, *lands, *send, *recv, after)
    return list(outs[n:])


def _adamw_math(w, g, m, v):
    m = ADAM_B1 * m + (1.0 - ADAM_B1) * g
    v = ADAM_B2 * v + (1.0 - ADAM_B2) * (g * g)
    m_hat = m / (1.0 - ADAM_B1 ** ADAM_STEP)
    v_hat = v / (1.0 - ADAM_B2 ** ADAM_STEP)
    delta = -ADAM_LR * (m_hat / (jnp.sqrt(v_hat) + ADAM_EPS) + ADAM_WD * w)
    return delta, m, v


def _adamw(w, mine, theirs, m, v, qc, name):
    r, cc = w.shape
    hr = r // 2
    tr = next(hr // k for k in range(1, hr + 1)
              if hr % k == 0 and (hr // k) % 8 == 0 and (hr // k) * cc * 4 <= (1 << 20))
    nb = hr // tr

    def body(qc_ref, w_ref, a_ref, b_ref, m_ref, v_ref, g_ref, d_ref, mo_ref, vo_ref):
        g = jnp.where(pl.program_id(0) == qc_ref[1], a_ref[...], b_ref[...])
        g_ref[...] = g
        d_ref[...], mo_ref[...], vo_ref[...] = _adamw_math(w_ref[...], g, m_ref[...], v_ref[...])

    full = pl.BlockSpec((tr, cc), lambda h, i, qc_ref: (h * nb + i, 0))
    half = pl.BlockSpec((tr, cc), lambda h, i, qc_ref: (i, 0))
    return pl.pallas_call(
        body, name=name,
        grid_spec=pltpu.PrefetchScalarGridSpec(
            num_scalar_prefetch=1, grid=(2, nb),
            in_specs=[full, half, half, full, full], out_specs=[full] * 4),
        out_shape=[jax.ShapeDtypeStruct((r, cc), f32)] * 4,
        compiler_params=_cp(2),
    )(qc, w, mine, theirs, m, v)


REPL = [("ffn1_norm", 1), ("mix_norm", 1), ("b_in", 6), ("rnn_conv_b", 1), ("rg_b_a", 1), ("rg_b_x", 1),
        ("rg_lambda", 1), ("conv_dw_b", 1), ("conv_ln_g", 1), ("conv_ln_b", 1), ("conv_b_proj", 1),
        ("ffn2_norm", 1), ("final_norm", 1)]
COLSH = [("meta_tokens", NMETA), ("rnn_conv_w", KC4), ("conv_dw_w", KC31)]
SMALL = REPL + COLSH
CS = D // NCHIP


def _pack_rows():
    starts, row = {}, 0
    for k, rows in REPL:
        starts[k] = row
        row += rows
    for k, rows in COLSH:
        row = -(-row // 8) * 8
        starts[k] = row
        row += rows
    return starts, -(-row // 8) * 8


PACK_START, SMALL_ROWS = _pack_rows()


def _small_pack(g):
    pieces, row = [], 0
    for k, rows in SMALL:
        if PACK_START[k] > row:
            pieces.append(jnp.zeros((PACK_START[k] - row, D), f32))
        pieces.append(g[k].reshape(rows, D))
        row = PACK_START[k] + rows
    pieces.append(jnp.zeros((SMALL_ROWS - row, D), f32))
    return jnp.concatenate(pieces, axis=0)


def _adamw_small(packs, ws, ms, vs):
    ns = len(SMALL)

    def body(*refs):
        pack_ref = refs[0]
        w_refs, m_refs, v_refs = refs[1:1 + ns], refs[1 + ns:1 + 2 * ns], refs[1 + 2 * ns:1 + 3 * ns]
        outs = refs[1 + 3 * ns:1 + 7 * ns]
        g_refs, d_refs, mo_refs, vo_refs = outs[:ns], outs[ns:2 * ns], outs[2 * ns:3 * ns], outs[3 * ns:]
        gsum_sc = refs[1 + 7 * ns]
        q = 2 * lax.axis_index("x") + lax.axis_index("y")
        acc = pack_ref[0]
        for dev in range(1, 8):
            acc = acc + pack_ref[dev]
        gsum_sc[...] = acc
        for idx, (name, rows) in enumerate(SMALL):
            row = PACK_START[name]
            if idx < len(REPL):
                for k in range(rows):
                    cols = slice(k * D, (k + 1) * D)
                    g = gsum_sc[row + k:row + k + 1, :]
                    d, mm, vv = _adamw_math(w_refs[idx][:, cols], g, m_refs[idx][:, cols], v_refs[idx][:, cols])
                    g_refs[idx][:, cols] = g
                    d_refs[idx][:, cols] = d
                    mo_refs[idx][:, cols] = mm
                    vo_refs[idx][:, cols] = vv
            else:
                g = gsum_sc[row:row + rows, pl.ds(pl.multiple_of(q * CS, CS), CS)]
                d, mm, vv = _adamw_math(w_refs[idx][...], g, m_refs[idx][...], v_refs[idx][...])
                g_refs[idx][...] = g
                d_refs[idx][...] = d
                mo_refs[idx][...] = mm
                vo_refs[idx][...] = vv

    shapes = [jax.ShapeDtypeStruct(w.shape, f32) for w in ws]
    return pl.pallas_call(
        body, name="adamw_small",
        out_shape=shapes * 4,
        scratch_shapes=[pltpu.VMEM((SMALL_ROWS, D), f32)],
        compiler_params=pltpu.CompilerParams(vmem_limit_bytes=VMEM_LIMIT),
    )(packs, *ws, *ms, *vs)


BIG = ["ffn1_w_gu", "ffn1_w_down", "w_in", "rg_w_a", "rg_w_x", "rnn_w_proj", "conv_w_proj", "w_out",
       "ffn2_w_gu", "ffn2_w_down"]
WEIGHTS = ['meta_tokens', 'ffn1_norm', 'ffn1_w_gu', 'ffn1_w_down', 'mix_norm', 'w_in', 'b_in', 'rnn_conv_w',
           'rnn_conv_b', 'rg_w_a', 'rg_b_a', 'rg_w_x', 'rg_b_x', 'rg_lambda', 'rnn_w_proj', 'conv_dw_w',
           'conv_dw_b', 'conv_ln_g', 'conv_ln_b', 'conv_w_proj', 'conv_b_proj', 'w_out', 'ffn2_norm',
           'ffn2_w_gu', 'ffn2_w_down', 'final_norm']


def _as2d(a):
    return a.reshape(-1, a.shape[-1])


def _step(x, loss_target, w, m, v):
    seq = x.shape[1]
    n_valid = NMETA + seq
    t = -(-n_valid // TM) * TM

    qc = jnp.stack([2 * lax.axis_index("x") + lax.axis_index("y"), lax.axis_index("c")]).astype(jnp.int32)
    p = {k: w[k].reshape(1, rows * D) for k, rows in REPL}

    first = ["ffn1_w_gu", "ffn1_w_down", "small"]
    later = [["w_in"], ["rg_w_a", "rg_w_x", "rnn_w_proj", "conv_w_proj", "w_out"], ["ffn2_w_gu", "ffn2_w_down"]]
    small_rows = sum(r for _, r in COLSH)
    small = jnp.concatenate([_as2d(w[k]) for k, _ in COLSH] + [jnp.zeros((64 - small_rows, CS), f32)], axis=0)

    def cast(k, token=None):
        src, dtype = (small, f32) if k == "small" else (_as2d(w[k]), bf16)
        return _cast_into_slot(src, qc, dtype, "cast_" + k, after=token)

    send1, recv1, bufs1, token1 = _gather_start([cast(k) for k in first], "gather_start_first")
    rest = [k for grp in later for k in grp]
    send2, recv2, bufs2, token2 = _gather_start([cast(k, token1) for k in rest], "gather_start_rest")

    def finish(names, send, recv, bufs, after, tag):
        done = _forward_halves(_gather_wait(send, recv, bufs, after, "gather_wait_" + tag), "gather_forward_" + tag)
        for k, b in zip(names, done):
            full = b.reshape(NCHIP, 2 * b.shape[2], b.shape[3])
            if k in ("ffn1_w_down", "ffn2_w_down"):
                full = full.reshape(F, D)
            elif k in ("rnn_w_proj", "conv_w_proj", "w_out"):
                full = full.reshape(D, D)
            elif k in ("rg_w_a", "rg_w_x"):
                full = full.reshape(NCHIP, NHEAD, HD // NCHIP, HD).transpose(1, 0, 2, 3).reshape(NHEAD, HD, HD)
            p[k] = full

    def group(names):
        idx = [rest.index(k) for k in names]
        return names, [send2[i] for i in idx], [recv2[i] for i in idx], [bufs2[i] for i in idx]

    h0 = jnp.pad(x[0], ((NMETA, t - n_valid), (0, 0)))
    tgt = jnp.pad(loss_target[0], ((NMETA, t - n_valid), (0, 0)))
    finish(first, send1, recv1, bufs1, (token2, h0, tgt), "first")
    small_full = p.pop("small").transpose(1, 0, 2).reshape(64, D)
    row = 0
    for k, rows in COLSH:
        p[k] = small_full[row:row + rows]
        row += rows

    h0 = lax.dynamic_update_slice(h0, p["meta_tokens"], (0, 0))
    h1, gate1, up1, n1 = _ffn_fwd(h0, p["ffn1_norm"], p["ffn1_w_gu"], p["ffn1_w_down"], "ffn1_fwd")
    finish(*group(later[0]), h1, "in")
    proj, n2 = _inproj_fwd(h1, p["mix_norm"], p["w_in"], p["b_in"])
    finish(*group(later[1]), proj, "mix")
    xr, hr, z = _rnn_fwd(proj, p["rnn_conv_w"], p["rnn_conv_b"], p["rg_w_a"], p["rg_b_a"],
                         p["rg_w_x"], p["rg_b_x"], p["rg_lambda"])
    vc, s = _conv_fwd(proj, p["conv_dw_w"], p["conv_dw_b"], p["conv_ln_g"], p["conv_ln_b"])
    h2 = _merge_fwd(h1, z, s, proj, p["rnn_w_proj"], p["conv_w_proj"], p["conv_b_proj"], p["w_out"])
    finish(*group(later[2]), h2, "ffn2")
    h3, gate2, up2, n3 = _ffn_fwd(h2, p["ffn2_norm"], p["ffn2_w_gu"], p["ffn2_w_down"], "ffn2_fwd")
    dh3, loss_blk, d_final = _final_loss(h3, p["final_norm"], tgt, n_valid)
    loss = lax.psum(loss_blk[0, 0], ("x", "y", "c"))

    g = {"final_norm": d_final}
    pending = []

    def reduce_start(names, tag):
        parts = []
        for k in names:
            rows = g[k].size // (NCHIP * g[k].shape[-1])
            parts.append(g[k].reshape((NCHIP, 2, rows // 2, g[k].shape[-1])))
        from_sibling = _pair_exchange(parts, "pair_exchange_" + tag)
        added = [_pair_add(pp, gg, qc, "pair_add_" + k) for pp, gg, k in zip(parts, from_sibling, names)]
        send, recv, sums, lands, token = _reduce_start([a for a, _ in added], [b for _, b in added],
                                                       "reduce_start_" + tag)
        pending.append((names, tag, send, recv, sums, lands))
        return token

    dh2, dgate2, dup2, a2, df2, g["ffn2_norm"] = _ffn_bwd(
        dh3, h2, p["ffn2_norm"], gate2, up2, p["ffn2_w_gu"], p["ffn2_w_down"], "ffn2_bwd")
    g["ffn2_w_gu"] = _ffn_gu_grad(n3, dgate2, dup2, "ffn2")
    g["ffn2_w_down"] = _ffn_down_grad(a2, df2, "ffn2")
    token = reduce_start(["ffn2_w_gu", "ffn2_w_down"], "ffn2")

    dz, ds, dproj, dh2b, merged, dya, dyb, g["conv_b_proj"] = _merge_bwd(
        dh2, z, s, proj, p["rnn_w_proj"], p["conv_w_proj"], p["conv_b_proj"], p["w_out"], after=token)
    g["w_out"] = _square_grad(merged, dh2b, "dw_out")
    g["rnn_w_proj"] = _square_grad(z, dya, "dw_rnn_proj")
    g["conv_w_proj"] = _square_grad(s, dyb, "dw_conv_proj")
    dproj, g["conv_dw_w"], g["conv_dw_b"], g["conv_ln_g"], g["conv_ln_b"] = _conv_bwd(
        ds, vc, proj, dproj, p["conv_dw_w"], p["conv_ln_g"], p["conv_ln_b"])
    (dproj, g["rg_w_a"], g["rg_w_x"], g["rnn_conv_w"], g["rnn_conv_b"], g["rg_b_a"], g["rg_b_x"],
     g["rg_lambda"]) = _rnn_bwd(dz, xr, hr, proj, dproj, p["rnn_conv_w"], p["rg_w_a"], p["rg_b_a"],
                                p["rg_w_x"], p["rg_b_x"], p["rg_lambda"])
    token = reduce_start(["w_out", "rnn_w_proj", "conv_w_proj", "rg_w_a", "rg_w_x"], "mix")

    dh1, g["mix_norm"], db_in = _inproj_bwd(dproj, dh2, h1, p["mix_norm"], p["w_in"], after=token)
    g["b_in"] = db_in.reshape(1, NIN)
    g["w_in"] = _tn_matmul(n2, dproj, D, NIN // NCHIP, (NCHIP, D, NIN // NCHIP),
                           (None, D, NIN // NCHIP), lambda k, nn, mm: (nn, 0, 0), "dw_in")
    token = reduce_start(["w_in"], "in")

    dh0, dgate1, dup1, a1, df1, g["ffn1_norm"] = _ffn_bwd(
        dh1, h0, p["ffn1_norm"], gate1, up1, p["ffn1_w_gu"], p["ffn1_w_down"], "ffn1_bwd", after=token)
    g["meta_tokens"] = dh0[0:NMETA]
    grad_x = dh0[NMETA:n_valid][None]

    send_s, recv_s, pack_buf, token = _gather_all_start(_place_pack(_small_pack(g), qc), "gather_all_start")
    g["ffn1_w_down"] = _ffn_down_grad(a1, df1, "ffn1", after=token)
    token = reduce_start(["ffn1_w_down"], "ffn1_down")
    g["ffn1_w_gu"] = _ffn_gu_grad(n1, dgate1, dup1, "ffn1", after=token)
    token = reduce_start(["ffn1_w_gu"], "ffn1_gu")
    packs = _gather_all_wait(send_s, recv_s, pack_buf, token, "gather_all_wait")

    grads, deltas, new_m, new_v = {}, {}, {}, {}

    def reduce_finish(items, after, tag):
        names, mine = [], []
        for grp_names, grp_tag, send, recv, sums, lands in items:
            landed = _reduce_wait(send, recv, sums, lands, after, "reduce_wait_" + grp_tag)
            mine += [_sum_chips(b, "sum_chips_" + k) for b, k in zip(landed, grp_names)]
            names += grp_names
            after = mine[-1]
        theirs = _pair_share(mine, "pair_share_" + tag)
        for k, mi, th in zip(names, mine, theirs):
            outs = _adamw(_as2d(w[k]), mi, th, _as2d(m[k]), _as2d(v[k]), qc, "adamw_" + k)
            grads[k], deltas[k], new_m[k], new_v[k] = (a.reshape(w[k].shape) for a in outs)
        return new_v[names[-1]]

    after = reduce_finish(pending[:3], packs, "early")
    reduce_finish(pending[3:], after, "late")
    names = [k for k, _ in SMALL]
    shape2 = {k: ((1, rows * D) if (k, rows) in REPL else (rows, CS)) for k, rows in SMALL}
    outs = _adamw_small(packs, *[[a[k].reshape(shape2[k]) for k in names] for a in (w, m, v)])
    ns = len(names)
    for i, k in enumerate(names):
        grads[k], deltas[k], new_m[k], new_v[k] = (outs[j * ns + i].reshape(w[k].shape) for j in range(4))

    return (loss, grad_x, *[grads[k] for k in WEIGHTS], *[deltas[k] for k in WEIGHTS],
            *[new_m[k] for k in WEIGHTS], *[new_v[k] for k in WEIGHTS])


def kernel(x, meta_tokens, ffn1_norm, ffn1_w_gu, ffn1_w_down, mix_norm, w_in, b_in, rnn_conv_w, rnn_conv_b, rg_w_a, rg_b_a, rg_w_x, rg_b_x, rg_lambda, rnn_w_proj, conv_dw_w, conv_dw_b, conv_ln_g, conv_ln_b, conv_w_proj, conv_b_proj, w_out, ffn2_norm, ffn2_w_gu, ffn2_w_down, final_norm, loss_target, m_meta_tokens, m_ffn1_norm, m_ffn1_w_gu, m_ffn1_w_down, m_mix_norm, m_w_in, m_b_in, m_rnn_conv_w, m_rnn_conv_b, m_rg_w_a, m_rg_b_a, m_rg_w_x, m_rg_b_x, m_rg_lambda, m_rnn_w_proj, m_conv_dw_w, m_conv_dw_b, m_conv_ln_g, m_conv_ln_b, m_conv_w_proj, m_conv_b_proj, m_w_out, m_ffn2_norm, m_ffn2_w_gu, m_ffn2_w_down, m_final_norm, v_meta_tokens, v_ffn1_norm, v_ffn1_w_gu, v_ffn1_w_down, v_mix_norm, v_w_in, v_b_in, v_rnn_conv_w, v_rnn_conv_b, v_rg_w_a, v_rg_b_a, v_rg_w_x, v_rg_b_x, v_rg_lambda, v_rnn_w_proj, v_conv_dw_w, v_conv_dw_b, v_conv_ln_g, v_conv_ln_b, v_conv_w_proj, v_conv_b_proj, v_w_out, v_ffn2_norm, v_ffn2_w_gu, v_ffn2_w_down, v_final_norm):
    args = locals()
    w = {k: args[k] for k in WEIGHTS}
    m = {k: args["m_" + k] for k in WEIGHTS}
    v = {k: args["v_" + k] for k in WEIGHTS}
    return _step(x, loss_target, w, m, v)
```

```python
import functools

import jax
import jax.numpy as jnp
from jax import lax
from jax.experimental import pallas as pl
from jax.experimental.pallas import tpu as pltpu

f32 = jnp.float32
bf16 = jnp.bfloat16

D = 1024
F = 2816
FS = F // 2
NIN = 6 * D
NMETA = 16
NHEAD = 4
HD = D // NHEAD
KC4 = 4
KC31 = 31
HALO = 32
EPS = 1e-6
TM = 384
NCHIP = 4
MESH = pl.DeviceIdType.MESH

ADAM_LR = 0.001
ADAM_B1 = 0.9
ADAM_B2 = 0.999
ADAM_EPS = 1e-08
ADAM_WD = 0.01
ADAM_STEP = 10

VMEM_LIMIT = 56 * 1024 * 1024
FSUB = [(o, min(256, FS - o)) for o in range(0, FS, 256)]


def _cp(n_axes, **kw):
    return pltpu.CompilerParams(dimension_semantics=("arbitrary",) * n_axes,
                                vmem_limit_bytes=VMEM_LIMIT, **kw)


RESIDENT = pl.BlockSpec(memory_space=pltpu.VMEM)


def _ordered(body, in_specs, args, after):
    if after is None:
        return body, in_specs, args
    return (lambda first, *refs: body(*refs),
            [pl.BlockSpec(memory_space=pl.ANY)] + list(in_specs), (after,) + tuple(args))


def _nt_dot(a, b):
    return lax.dot_general(a, b, (((1,), (1,)), ((), ())), preferred_element_type=f32)


def _tn_dot(a, b):
    return lax.dot_general(a, b, (((0,), (0,)), ((), ())), preferred_element_type=f32)


def _sigmoid(x):
    return 1.0 / (1.0 + jnp.exp(-x))


def _log1p(y):
    u = 1.0 + y
    d = u - 1.0
    return jnp.where(d == 0.0, y, jnp.log(u) * (y / jnp.where(d == 0.0, 1.0, d)))


def _softplus(x):
    return jnp.maximum(x, 0.0) + _log1p(jnp.exp(-jnp.abs(x)))


def _one_minus_square(a, log_a):
    x = 2.0 * log_a
    series = x * (1.0 + x * (0.5 + x * (1.0 / 6.0)))
    return jnp.where(jnp.abs(x) < 0.03, -series, 1.0 - a * a)


_GELU_C = 0.7978845608028654
_GELU_K = 0.044715


def _gelu_and_grad(y):
    y2 = y * y
    th = jnp.tanh(_GELU_C * (y + _GELU_K * y * y2))
    gel = 0.5 * y * (1.0 + th)
    dgel = 0.5 * (1.0 + th) + 0.5 * y * (1.0 - th * th) * _GELU_C * (1.0 + 3.0 * _GELU_K * y2)
    return gel, dgel


def _rms_stats(h):
    return lax.rsqrt(jnp.mean(h * h, axis=-1, keepdims=True) + EPS)


def _rms_bwd(dn, h, g):
    r = _rms_stats(h)
    nhat = h * r
    dnh = dn * g
    dh = r * (dnh - nhat * jnp.mean(dnh * nhat, axis=-1, keepdims=True))
    dg = jnp.sum(dn * nhat, axis=0, keepdims=True)
    return dh, dg


def _row_ids(shape):
    return lax.broadcasted_iota(jnp.int32, shape, 0)


def _ffn_fwd(h, g, wgu, wd, name):
    t = h.shape[0]
    nj = 2

    def body(h_ref, g_ref, wg_ref, wd_ref, ho_ref, gate_ref, up_ref, n_ref, nb_sc, acc_sc, a_sc):
        j = pl.program_id(1)

        @pl.when(j == 0)
        def _():
            hh = h_ref[...]
            nb = (hh * _rms_stats(hh) * g_ref[...]).astype(bf16)
            nb_sc[...] = nb
            n_ref[...] = nb
            acc_sc[...] = jnp.zeros_like(acc_sc)

        nb = nb_sc[...]
        for off, width in FSUB:
            cols = slice(off, off + width)
            gt = jnp.dot(nb, wg_ref[j, :, cols], preferred_element_type=f32)
            up = jnp.dot(nb, wg_ref[2 + j, :, cols], preferred_element_type=f32)
            gate_ref[:, cols] = gt.astype(bf16)
            up_ref[:, cols] = up.astype(bf16)
            a_sc[:, cols] = (gt * _sigmoid(gt) * up).astype(bf16)
        acc_sc[...] += jnp.dot(a_sc[...], wd_ref[j], preferred_element_type=f32)

        @pl.when(j == nj - 1)
        def _():
            ho_ref[...] = h_ref[...] + 0.5 * acc_sc[...]

    tm = TM
    return pl.pallas_call(
        body, name=name, grid=(t // tm, nj),
        in_specs=[
            pl.BlockSpec((tm, D), lambda i, j: (i, 0)),
            pl.BlockSpec((1, D), lambda i, j: (0, 0)),
            RESIDENT, RESIDENT,
        ],
        out_specs=[
            pl.BlockSpec((tm, D), lambda i, j: (i, 0)),
            pl.BlockSpec((tm, FS), lambda i, j: (i, j)),
            pl.BlockSpec((tm, FS), lambda i, j: (i, j)),
            pl.BlockSpec((tm, D), lambda i, j: (i, 0)),
        ],
        out_shape=[
            jax.ShapeDtypeStruct((t, D), f32),
            jax.ShapeDtypeStruct((t, F), bf16),
            jax.ShapeDtypeStruct((t, F), bf16),
            jax.ShapeDtypeStruct((t, D), bf16),
        ],
        scratch_shapes=[pltpu.VMEM((tm, D), bf16), pltpu.VMEM((tm, D), f32), pltpu.VMEM((tm, FS), bf16)],
        compiler_params=_cp(2),
    )(h, g, wgu, wd.reshape(nj, FS, D))


def _inproj_fwd(h, g, win, b_in):
    t = h.shape[0]
    tn = NIN // NCHIP
    nj = NIN // tn
    per = (NIN // NCHIP) // tn

    def body(h_ref, g_ref, w_ref, b_ref, proj_ref, n_ref, nb_sc):
        j = pl.program_id(1)

        @pl.when(j == 0)
        def _():
            hh = h_ref[...]
            nb = (hh * _rms_stats(hh) * g_ref[...]).astype(bf16)
            nb_sc[...] = nb
            n_ref[...] = nb

        proj_ref[...] = jnp.dot(nb_sc[...], w_ref[j], preferred_element_type=f32) + b_ref[...]

    return pl.pallas_call(
        body, name="inproj_fwd", grid=(t // TM, nj),
        in_specs=[
            pl.BlockSpec((TM, D), lambda i, j: (i, 0)),
            pl.BlockSpec((1, D), lambda i, j: (0, 0)),
            RESIDENT,
            pl.BlockSpec((1, tn), lambda i, j: (0, j)),
        ],
        out_specs=[
            pl.BlockSpec((TM, tn), lambda i, j: (i, j)),
            pl.BlockSpec((TM, D), lambda i, j: (i, 0)),
        ],
        out_shape=[jax.ShapeDtypeStruct((t, NIN), f32), jax.ShapeDtypeStruct((t, D), bf16)],
        scratch_shapes=[pltpu.VMEM((TM, D), bf16)],
        compiler_params=_cp(2),
    )(h, g, win, b_in)


def _block_gates(xr, wa_ref, ba, wx_ref, bx, lam):
    xrb = xr.astype(bf16)
    pa = jnp.concatenate([jnp.dot(xrb[:, hh * HD:(hh + 1) * HD], wa_ref[hh], preferred_element_type=f32)
                          for hh in range(NHEAD)], axis=1)
    px = jnp.concatenate([jnp.dot(xrb[:, hh * HD:(hh + 1) * HD], wx_ref[hh], preferred_element_type=f32)
                          for hh in range(NHEAD)], axis=1)
    ra = _sigmoid(pa + ba)
    ii = _sigmoid(px + bx)
    sp = _softplus(-lam)
    log_a = -8.0 * ra * sp
    a = jnp.exp(log_a)
    sq = jnp.sqrt(_one_minus_square(a, log_a))
    return ra, ii, a, sq, sp


def _rnn_fwd(proj, cw, cb, wa, ba, wx, bx, lam):
    t = proj.shape[0]
    ng = TM // 8

    def body(x_ref, y_ref, cw_ref, cb_ref, wa_ref, ba_ref, wx_ref, bx_ref, lam_ref,
             xr_ref, hr_ref, z_ref, xext_sc, carry_sc, a_sc, h_sc):
        i = pl.program_id(0)

        @pl.when(i == 0)
        def _():
            xext_sc[0:8, :] = jnp.zeros((8, D), f32)
            carry_sc[...] = jnp.zeros_like(carry_sc)

        x = x_ref[...]
        xext_sc[8:8 + TM, :] = x
        xe = xext_sc[...]
        xr = cb_ref[...] + cw_ref[KC4 - 1:KC4, :] * x
        for k in range(KC4 - 1):
            xr = xr + cw_ref[k:k + 1, :] * pltpu.roll(xe, KC4 - 1 - k, 0)[8:8 + TM]
        xext_sc[0:8, :] = x[TM - 8:TM]

        _, ii, a, sq, _ = _block_gates(xr, wa_ref, ba_ref[...], wx_ref, bx_ref[...], lam_ref[...])
        a_sc[...] = a
        h_sc[...] = sq * ii * xr
        row = _row_ids((8, D))

        def group(r, carry):
            off = pl.multiple_of(r * 8, 8)
            aa = a_sc[pl.ds(off, 8), :]
            hh = h_sc[pl.ds(off, 8), :]
            for s in (1, 2, 4):
                a_sh = jnp.where(row >= s, pltpu.roll(aa, s, 0), 1.0)
                h_sh = jnp.where(row >= s, pltpu.roll(hh, s, 0), 0.0)
                hh = aa * h_sh + hh
                aa = aa * a_sh
            hh = hh + aa * carry
            h_sc[pl.ds(off, 8), :] = hh
            return hh[7:8, :]

        carry_sc[...] = lax.fori_loop(0, ng, group, carry_sc[...])
        hr = h_sc[...]
        gel, _ = _gelu_and_grad(y_ref[...])
        xr_ref[...] = xr
        hr_ref[...] = hr
        z_ref[...] = (hr * gel).astype(bf16)

    vec = pl.BlockSpec((1, D), lambda i: (0, 0))
    return pl.pallas_call(
        body, name="rnn_fwd", grid=(t // TM,),
        in_specs=[
            pl.BlockSpec((TM, D), lambda i: (i, 0)),
            pl.BlockSpec((TM, D), lambda i: (i, 1)),
            pl.BlockSpec((KC4, D), lambda i: (0, 0)),
            vec,
            pl.BlockSpec((NHEAD, HD, HD), lambda i: (0, 0, 0)),
            vec,
            pl.BlockSpec((NHEAD, HD, HD), lambda i: (0, 0, 0)),
            vec, vec,
        ],
        out_specs=[pl.BlockSpec((TM, D), lambda i: (i, 0))] * 3,
        out_shape=[jax.ShapeDtypeStruct((t, D), f32), jax.ShapeDtypeStruct((t, D), f32),
                   jax.ShapeDtypeStruct((t, D), bf16)],
        scratch_shapes=[pltpu.VMEM((TM + 8, D), f32), pltpu.VMEM((1, D), f32),
                        pltpu.VMEM((TM, D), f32), pltpu.VMEM((TM, D), f32)],
        compiler_params=_cp(1),
    )(proj, proj, cw, cb, wa, ba, wx, bx, lam)


def _ln_stats(vc):
    mu = jnp.mean(vc, axis=-1, keepdims=True)
    xc = vc - mu
    rstd = lax.rsqrt(jnp.mean(xc * xc, axis=-1, keepdims=True) + EPS)
    return xc * rstd, rstd


def _conv_fwd(proj, w31, b31, ln_g, ln_b):
    t = proj.shape[0]

    def body(gv_ref, gg_ref, w_ref, b_ref, lg_ref, lb_ref, vc_ref, s_ref, vext_sc):
        i = pl.program_id(0)

        @pl.when(i == 0)
        def _():
            vext_sc[0:HALO, :] = jnp.zeros((HALO, D), f32)

        v = gv_ref[...] * _sigmoid(gg_ref[...])
        vext_sc[HALO:HALO + TM, :] = v
        ve = vext_sc[...]
        acc = jnp.zeros((TM, D), f32) + b_ref[...]
        for s in range(8):
            vs = ve if s == 0 else pltpu.roll(ve, s, 0)
            for m in range(HALO // 8):
                k = KC31 - 1 - (8 * m + s)
                if 0 <= k < KC31:
                    acc = acc + w_ref[k:k + 1, :] * vs[HALO - 8 * m:HALO - 8 * m + TM]
        vext_sc[0:HALO, :] = v[TM - HALO:TM]
        xhat, _ = _ln_stats(acc)
        ln = xhat * lg_ref[...] + lb_ref[...]
        vc_ref[...] = acc
        s_ref[...] = (ln * _sigmoid(ln)).astype(bf16)

    vec = pl.BlockSpec((1, D), lambda i: (0, 0))
    return pl.pallas_call(
        body, name="conv_fwd", grid=(t // TM,),
        in_specs=[
            pl.BlockSpec((TM, D), lambda i: (i, 2)),
            pl.BlockSpec((TM, D), lambda i: (i, 3)),
            pl.BlockSpec((KC31, D), lambda i: (0, 0)),
            vec, vec, vec,
        ],
        out_specs=[pl.BlockSpec((TM, D), lambda i: (i, 0))] * 2,
        out_shape=[jax.ShapeDtypeStruct((t, D), f32), jax.ShapeDtypeStruct((t, D), bf16)],
        scratch_shapes=[pltpu.VMEM((TM + HALO, D), f32)],
        compiler_params=_cp(1),
    )(proj, proj, w31, b31, ln_g, ln_b)


def _merge_fwd(h, z, s, proj, wrp, wcp, bcp, wout):
    t = h.shape[0]

    def body(h_ref, z_ref, s_ref, ga_ref, gb_ref, wrp_ref, wcp_ref, bcp_ref, wout_ref, ho_ref):
        ya = jnp.dot(z_ref[...], wrp_ref[...], preferred_element_type=f32)
        yb = jnp.dot(s_ref[...], wcp_ref[...], preferred_element_type=f32) + bcp_ref[...]
        merged = _sigmoid(ga_ref[...]) * ya + _sigmoid(gb_ref[...]) * yb
        ho_ref[...] = h_ref[...] + jnp.dot(merged.astype(bf16), wout_ref[...], preferred_element_type=f32)

    row = pl.BlockSpec((TM, D), lambda i: (i, 0))
    wsq = pl.BlockSpec((D, D), lambda i: (0, 0))
    return pl.pallas_call(
        body, name="merge_fwd", grid=(t // TM,),
        in_specs=[row, row, row,
                  pl.BlockSpec((TM, D), lambda i: (i, 4)),
                  pl.BlockSpec((TM, D), lambda i: (i, 5)),
                  wsq, wsq, pl.BlockSpec((1, D), lambda i: (0, 0)), wsq],
        out_specs=row,
        out_shape=jax.ShapeDtypeStruct((t, D), f32),
        compiler_params=_cp(1),
    )(h, z, s, proj, proj, wrp, wcp, bcp, wout)


def _final_loss(h, g, tgt, n_valid):
    t = h.shape[0]

    def body(h_ref, g_ref, t_ref, dh_ref, loss_ref, dg_ref):
        i = pl.program_id(0)

        @pl.when(i == 0)
        def _():
            loss_ref[...] = jnp.zeros_like(loss_ref)
            dg_ref[...] = jnp.zeros_like(dg_ref)

        hh = h_ref[...]
        gg = g_ref[...]
        row = i * TM + _row_ids((TM, 1))
        valid = jnp.logical_and(row >= NMETA, row < n_valid)
        out = hh * _rms_stats(hh) * gg
        err = jnp.where(valid, out - t_ref[...], 0.0)
        loss_ref[...] += 0.5 * jnp.sum(err * err) * (1.0 / D)
        dh, dg = _rms_bwd(err * (1.0 / D), hh, gg)
        dh_ref[...] = dh
        dg_ref[...] += dg

    row_spec = pl.BlockSpec((TM, D), lambda i: (i, 0))
    return pl.pallas_call(
        body, name="final_loss", grid=(t // TM,),
        in_specs=[row_spec, pl.BlockSpec((1, D), lambda i: (0, 0)), row_spec],
        out_specs=[row_spec, pl.BlockSpec((8, 128), lambda i: (0, 0)), pl.BlockSpec((1, D), lambda i: (0, 0))],
        out_shape=[jax.ShapeDtypeStruct((t, D), f32), jax.ShapeDtypeStruct((8, 128), f32),
                   jax.ShapeDtypeStruct((1, D), f32)],
        compiler_params=_cp(1),
    )(h, g, tgt)


def _ffn_bwd(dh, h, g, gate, up, wgu, wd, name, after=None):
    t = h.shape[0]
    nj = 2

    def body(dh_ref, h_ref, g_ref, gate_ref, up_ref, wg_ref, wd_ref,
             dhi_ref, dgate_ref, dup_ref, a_ref, df_ref, dg_ref, dfb_sc, dn_sc):
        i = pl.program_id(0)
        j = pl.program_id(1)

        @pl.when(jnp.logical_and(i == 0, j == 0))
        def _():
            dg_ref[...] = jnp.zeros_like(dg_ref)

        @pl.when(j == 0)
        def _():
            dfb = (0.5 * dh_ref[...]).astype(bf16)
            dfb_sc[...] = dfb
            df_ref[...] = dfb
            dn_sc[...] = jnp.zeros_like(dn_sc)

        dfb = dfb_sc[...]
        for off, width in FSUB:
            cols = slice(off, off + width)
            da = _nt_dot(dfb, wd_ref[j, cols, :])
            gt = gate_ref[:, cols].astype(f32)
            uu = up_ref[:, cols].astype(f32)
            sg = _sigmoid(gt)
            silu = gt * sg
            a_ref[:, cols] = (silu * uu).astype(bf16)
            dgate_ref[:, cols] = (da * uu * (sg * (1.0 + gt * (1.0 - sg)))).astype(bf16)
            dup_ref[:, cols] = (da * silu).astype(bf16)
        dn_sc[...] += _nt_dot(dgate_ref[...], wg_ref[j]) + _nt_dot(dup_ref[...], wg_ref[2 + j])

        @pl.when(j == nj - 1)
        def _():
            dhin, dg = _rms_bwd(dn_sc[...], h_ref[...], g_ref[...])
            dhi_ref[...] = dh_ref[...] + dhin
            dg_ref[...] += dg

    rowd = pl.BlockSpec((TM, D), lambda i, j: (i, 0))
    rowf = pl.BlockSpec((TM, FS), lambda i, j: (i, j))
    vec = pl.BlockSpec((1, D), lambda i, j: (0, 0))
    body, in_specs, args = _ordered(
        body,
        [rowd, rowd, vec, rowf, rowf,
         RESIDENT, RESIDENT],
        (dh, h, g, gate, up, wgu, wd.reshape(nj, FS, D)), after)
    return pl.pallas_call(
        body, name=name, grid=(t // TM, nj),
        in_specs=in_specs,
        out_specs=[rowd, rowf, rowf, rowf, rowd, vec],
        out_shape=[jax.ShapeDtypeStruct((t, D), f32), jax.ShapeDtypeStruct((t, F), bf16),
                   jax.ShapeDtypeStruct((t, F), bf16), jax.ShapeDtypeStruct((t, F), bf16),
                   jax.ShapeDtypeStruct((t, D), bf16), jax.ShapeDtypeStruct((1, D), f32)],
        scratch_shapes=[pltpu.VMEM((TM, D), bf16), pltpu.VMEM((TM, D), f32)],
        compiler_params=_cp(2),
    )(*args)


def _big_tile(t):
    for cand in (2112, 1408, 768, 384):
        if t % cand == 0:
            return cand
    raise ValueError(t)


ANY_SPEC = pl.BlockSpec(memory_space=pl.ANY)


def _tn_matmul(a, b, tk, tn, out_shape, out_block, out_map, name, base=None, after=None):
    t, kk = a.shape
    _, nn = b.shape
    tmm = _big_tile(t)
    nm = t // tmm

    def body(a_ref, b_ref, o_ref, acc_sc):
        m = pl.program_id(2)

        @pl.when(m == 0)
        def _():
            acc_sc[...] = jnp.zeros_like(acc_sc)

        acc_sc[...] += _tn_dot(a_ref[...], b_ref[...])

        @pl.when(m == nm - 1)
        def _():
            o_ref[...] = acc_sc[...].astype(o_ref.dtype)

    in_specs = [pl.BlockSpec((tmm, tk), lambda k, n, m: (m, k)),
                pl.BlockSpec((tmm, tn), lambda k, n, m: (m, n))]
    args, aliases = (a, b), {}
    if base is not None:
        body = (lambda inner: lambda a_ref, b_ref, base_ref, o_ref, acc_sc: inner(a_ref, b_ref, o_ref, acc_sc))(body)
        in_specs, args, aliases = in_specs + [ANY_SPEC], (a, b, base), {2: 0}
    if after is not None:
        body, in_specs, args = _ordered(body, in_specs, args, after)
        aliases = {k + 1: v for k, v in aliases.items()}
    return pl.pallas_call(
        body, name=name, grid=(kk // tk, nn // tn, nm),
        in_specs=in_specs,
        out_specs=pl.BlockSpec(out_block, out_map),
        out_shape=jax.ShapeDtypeStruct(out_shape, bf16),
        scratch_shapes=[pltpu.VMEM((tk, tn), f32)],
        input_output_aliases=aliases,
        compiler_params=_cp(3),
    )(*args)


def _merge_bwd(dh, z, s, proj, wrp, wcp, bcp, wout, after=None):
    t = dh.shape[0]

    def body(dh_ref, z_ref, s_ref, ga_ref, gb_ref, wrp_ref, wcp_ref, bcp_ref, wout_ref,
             dz_ref, ds_ref, dgab_ref, dhb_ref, mg_ref, dya_ref, dyb_ref, dbcp_ref):
        i = pl.program_id(0)

        @pl.when(i == 0)
        def _():
            dbcp_ref[...] = jnp.zeros_like(dbcp_ref)

        dhb = dh_ref[...].astype(bf16)
        dhb_ref[...] = dhb
        dmg = _nt_dot(dhb, wout_ref[...])
        ya = jnp.dot(z_ref[...], wrp_ref[...], preferred_element_type=f32)
        yb = jnp.dot(s_ref[...], wcp_ref[...], preferred_element_type=f32) + bcp_ref[...]
        sa = _sigmoid(ga_ref[...])
        sb = _sigmoid(gb_ref[...])
        mg_ref[...] = (sa * ya + sb * yb).astype(bf16)
        dgab_ref[:, 0:D] = (dmg * ya * sa * (1.0 - sa)).astype(bf16)
        dgab_ref[:, D:2 * D] = (dmg * yb * sb * (1.0 - sb)).astype(bf16)
        dya = dmg * sa
        dyb = dmg * sb
        dbcp_ref[...] += jnp.sum(dyb, axis=0, keepdims=True)
        dyab = dya.astype(bf16)
        dybb = dyb.astype(bf16)
        dya_ref[...] = dyab
        dyb_ref[...] = dybb
        dz_ref[...] = _nt_dot(dyab, wrp_ref[...])
        ds_ref[...] = _nt_dot(dybb, wcp_ref[...])

    row = pl.BlockSpec((TM, D), lambda i: (i, 0))
    wsq = pl.BlockSpec((D, D), lambda i: (0, 0))
    vec = pl.BlockSpec((1, D), lambda i: (0, 0))
    rowb = jax.ShapeDtypeStruct((t, D), bf16)
    body, in_specs, args = _ordered(
        body,
        [row, row, row,
         pl.BlockSpec((TM, D), lambda i: (i, 4)),
         pl.BlockSpec((TM, D), lambda i: (i, 5)),
         wsq, wsq, vec, wsq],
        (dh, z, s, proj, proj, wrp, wcp, bcp, wout), after)
    return pl.pallas_call(
        body, name="merge_bwd", grid=(t // TM,),
        in_specs=in_specs,
        out_specs=[row, row,
                   pl.BlockSpec((TM, 2 * D), lambda i: (i, 2)),
                   row, row, row, row, vec],
        out_shape=[jax.ShapeDtypeStruct((t, D), f32), jax.ShapeDtypeStruct((t, D), f32),
                   jax.ShapeDtypeStruct((t, NIN), bf16),
                   rowb, rowb, rowb, rowb, jax.ShapeDtypeStruct((1, D), f32)],
        compiler_params=_cp(1),
    )(*args)


def _conv_bwd(ds, vc, proj, dproj, w31, ln_g, ln_b):
    t = ds.shape[0]
    nt = t // TM
    hb = TM // HALO

    rb = 16
    nb = TM // rb
    taps = [(KC31 - 1 - (8 * m + s), s, m) for s in range(8) for m in range(HALO // 8)
            if 0 <= KC31 - 1 - (8 * m + s) < KC31]

    def groups(a):
        return jnp.sum(a.reshape(rb // 8, 8, D), axis=0)

    def body(ds_ref, vc_ref, gv_ref, gg_ref, gvp_ref, ggp_ref, dpin_ref, w_ref, lg_ref, lb_ref,
             dgvg_ref, dw_ref, db_ref, dlg_ref, dlb_ref, dext_sc, vext_sc, rot_sc, dwacc_sc, small_sc, wb_sc):
        del dpin_ref
        i = pl.program_id(0)
        tile = nt - 1 - i

        @pl.when(i == 0)
        def _():
            dext_sc[TM:TM + HALO, :] = jnp.zeros((HALO, D), f32)
            dwacc_sc[...] = jnp.zeros_like(dwacc_sc)
            small_sc[...] = jnp.zeros_like(small_sc)

        lg = lg_ref[...]
        lb = lb_ref[...]

        xhat, rstd = _ln_stats(vc_ref[...])
        ln = xhat * lg + lb
        sg = _sigmoid(ln)
        dln = ds_ref[...] * (sg * (1.0 + ln * (1.0 - sg)))
        dxh = dln * lg
        dvc = rstd * (dxh - jnp.mean(dxh, axis=-1, keepdims=True)
                      - xhat * jnp.mean(dxh * xhat, axis=-1, keepdims=True))
        small_sc[0] += jnp.sum((dln * xhat).reshape(TM // 8, 8, D), axis=0)
        small_sc[1] += jnp.sum(dln.reshape(TM // 8, 8, D), axis=0)
        small_sc[2] += jnp.sum(dvc.reshape(TM // 8, 8, D), axis=0)
        dext_sc[0:TM, :] = dvc
        vext_sc[HALO:HALO + TM, :] = gv_ref[...] * _sigmoid(gg_ref[...])
        vext_sc[0:HALO, :] = jnp.where(tile > 0, gvp_ref[...] * _sigmoid(ggp_ref[...]), 0.0)

        @pl.when(i == 0)
        def _():
            for k in range(KC31):
                wb_sc[k] = jnp.broadcast_to(w_ref[k:k + 1, :], (8, D))

        for s in range(1, 8):
            rot_sc[s - 1] = pltpu.roll(dext_sc[...], TM + HALO - s, 0)

        def dv_block(b, carry):
            rows = pl.ds(pl.multiple_of(b * rb, rb), rb)
            acc = jnp.zeros((rb, D), f32)
            for k, s, m in taps:
                src = pl.ds(pl.multiple_of(b * rb + 8 * m, 8), rb)
                slab = dext_sc[src, :] if s == 0 else rot_sc[s - 1, src, :]
                acc = acc + (slab.reshape(rb // 8, 8, D) * wb_sc[k]).reshape(rb, D)
            sgg = _sigmoid(gg_ref[rows, :])
            dgvg_ref[rows, 0:D] = (acc * sgg).astype(bf16)
            dgvg_ref[rows, D:2 * D] = (acc * gv_ref[rows, :] * sgg * (1.0 - sgg)).astype(bf16)
            return carry

        lax.fori_loop(0, nb, dv_block, 0)

        for s in range(1, 8):
            rot_sc[s - 1] = pltpu.roll(vext_sc[...], s, 0)
        for first in range(0, len(taps), 3):
            trio = taps[first:first + 3]

            def dw_block(b, accs, trio=trio):
                rows = pl.ds(pl.multiple_of(b * rb, rb), rb)
                dvc_blk = dext_sc[rows, :]
                out = []
                for acc, (k, s, m) in zip(accs, trio):
                    src = pl.ds(pl.multiple_of(b * rb + HALO - 8 * m, 8), rb)
                    slab = vext_sc[src, :] if s == 0 else rot_sc[s - 1, src, :]
                    out.append(acc + groups(dvc_blk * slab))
                return tuple(out)

            sums = lax.fori_loop(0, nb, dw_block, tuple(jnp.zeros((8, D), f32) for _ in trio))
            for acc, (k, s, m) in zip(sums, trio):
                dwacc_sc[k] += acc
        dext_sc[TM:TM + HALO, :] = dext_sc[0:HALO, :]

        @pl.when(i == nt - 1)
        def _():
            for k in range(KC31):
                dw_ref[k:k + 1, :] = jnp.sum(dwacc_sc[k], axis=0, keepdims=True)
            dlg_ref[...] = jnp.sum(small_sc[0], axis=0, keepdims=True)
            dlb_ref[...] = jnp.sum(small_sc[1], axis=0, keepdims=True)
            db_ref[...] = jnp.sum(small_sc[2], axis=0, keepdims=True)

    rev = lambda i: (nt - 1 - i, 0)
    vec = pl.BlockSpec((1, D), lambda i: (0, 0))
    halo_row = lambda i: jnp.maximum((nt - 1 - i) * hb - 1, 0)
    return pl.pallas_call(
        body, name="conv_bwd", grid=(nt,),
        in_specs=[
            pl.BlockSpec((TM, D), rev),
            pl.BlockSpec((TM, D), rev),
            pl.BlockSpec((TM, D), lambda i: (nt - 1 - i, 2)),
            pl.BlockSpec((TM, D), lambda i: (nt - 1 - i, 3)),
            pl.BlockSpec((HALO, D), lambda i: (halo_row(i), 2)),
            pl.BlockSpec((HALO, D), lambda i: (halo_row(i), 3)),
            pl.BlockSpec(memory_space=pl.ANY),
            pl.BlockSpec((KC31, D), lambda i: (0, 0)),
            vec, vec,
        ],
        out_specs=[
            pl.BlockSpec((TM, 2 * D), lambda i: (nt - 1 - i, 1)),
            pl.BlockSpec((KC31, D), lambda i: (0, 0)),
            vec, vec, vec,
        ],
        out_shape=[jax.ShapeDtypeStruct((t, NIN), bf16),
                   jax.ShapeDtypeStruct((KC31, D), f32),
                   jax.ShapeDtypeStruct((1, D), f32), jax.ShapeDtypeStruct((1, D), f32),
                   jax.ShapeDtypeStruct((1, D), f32)],
        scratch_shapes=[pltpu.VMEM((TM + HALO, D), f32), pltpu.VMEM((TM + HALO, D), f32),
                        pltpu.VMEM((7, TM + HALO, D), f32), pltpu.VMEM((KC31, 8, D), f32),
                        pltpu.VMEM((3, 8, D), f32), pltpu.VMEM((KC31, 8, D), f32)],
        input_output_aliases={6: 0},
        compiler_params=_cp(1),
    )(ds, vc, proj, proj, proj, proj, dproj, w31, ln_g, ln_b)


def _rnn_bwd(dz, xr, hr, proj, dproj, cw, wa, ba, wx, bx, lam):
    t = dz.shape[0]
    nt = t // TM
    ng = TM // 8
    hq = HD // NCHIP

    def body(dz_ref, xr_ref, hr_ref, hrp_ref, x_ref, xp_ref, y_ref, dpin_ref,
             cw_ref, wa_ref, ba_ref, wx_ref, bx_ref, lam_ref,
             dxy_ref, dwa_ref, dwx_ref, dcw_ref, dcb_ref, dba_ref, dbx_ref, dlam_ref,
             anext_sc, gcarry_sc, dext_sc, xext_sc, m_sc, g_sc, dwa_sc, dwx_sc, dsp_sc):
        del dpin_ref
        i = pl.program_id(0)
        tile = nt - 1 - i

        @pl.when(i == 0)
        def _():
            anext_sc[...] = jnp.zeros_like(anext_sc)
            gcarry_sc[...] = jnp.zeros_like(gcarry_sc)
            dext_sc[TM:TM + 8, :] = jnp.zeros((8, D), f32)
            dwa_sc[...] = jnp.zeros_like(dwa_sc)
            dwx_sc[...] = jnp.zeros_like(dwx_sc)
            dsp_sc[...] = jnp.zeros_like(dsp_sc)
            dcw_ref[...] = jnp.zeros_like(dcw_ref)
            dcb_ref[...] = jnp.zeros_like(dcb_ref)
            dba_ref[...] = jnp.zeros_like(dba_ref)
            dbx_ref[...] = jnp.zeros_like(dbx_ref)

        xr = xr_ref[...]
        hr = hr_ref[...]
        dz = dz_ref[...]
        gel, dgel = _gelu_and_grad(y_ref[...])
        dxy_ref[:, D:2 * D] = (dz * hr * dgel).astype(bf16)
        ra, ii, a, sq, sp = _block_gates(xr, wa_ref, ba_ref[...], wx_ref, bx_ref[...], lam_ref[...])

        row = _row_ids((TM, D))
        m_sc[...] = jnp.where(row == TM - 1, anext_sc[...], pltpu.roll(a, TM - 1, 0))
        anext_sc[...] = a[0:1, :]
        g_sc[...] = dz * gel
        row8 = _row_ids((8, D))

        def group(qq, carry):
            off = pl.multiple_of((ng - 1 - qq) * 8, 8)
            mm = m_sc[pl.ds(off, 8), :]
            dd = g_sc[pl.ds(off, 8), :]
            for s in (1, 2, 4):
                m_sh = jnp.where(row8 < 8 - s, pltpu.roll(mm, 8 - s, 0), 1.0)
                d_sh = jnp.where(row8 < 8 - s, pltpu.roll(dd, 8 - s, 0), 0.0)
                dd = dd + mm * d_sh
                mm = mm * m_sh
            dd = dd + mm * carry
            g_sc[pl.ds(off, 8), :] = dd
            return dd[0:1, :]

        gcarry_sc[...] = lax.fori_loop(0, ng, group, gcarry_sc[...])
        gg = g_sc[...]

        hlast = jnp.where(tile > 0, hrp_ref[7:8, :], 0.0)
        hprev = jnp.where(row == 0, hlast, pltpu.roll(hr, 1, 0))
        d_a = gg * hprev
        dsq = gg * ii * xr
        dii = gg * sq * xr
        dxr = gg * sq * ii
        dlog = d_a * a - dsq * (a * a / sq)
        dsp_sc[...] += jnp.sum(dlog * (-8.0 * ra), axis=0, keepdims=True)
        dpa = dlog * (-8.0 * sp) * ra * (1.0 - ra)
        dpx = dii * ii * (1.0 - ii)
        dba_ref[...] += jnp.sum(dpa, axis=0, keepdims=True)
        dbx_ref[...] += jnp.sum(dpx, axis=0, keepdims=True)
        dpab = dpa.astype(bf16)
        dpxb = dpx.astype(bf16)
        xrb = xr.astype(bf16)
        back = []
        for hh in range(NHEAD):
            cols = slice(hh * HD, (hh + 1) * HD)
            back.append(_nt_dot(dpab[:, cols], wa_ref[hh]) + _nt_dot(dpxb[:, cols], wx_ref[hh]))
            dwa_sc[hh] += _tn_dot(xrb[:, cols], dpab[:, cols])
            dwx_sc[hh] += _tn_dot(xrb[:, cols], dpxb[:, cols])
        dxr = dxr + jnp.concatenate(back, axis=1)

        dext_sc[0:TM, :] = dxr
        de = dext_sc[...]
        dx = cw_ref[KC4 - 1:KC4, :] * dxr
        for k in range(KC4 - 1):
            dx = dx + cw_ref[k:k + 1, :] * pltpu.roll(de, TM + 8 - (KC4 - 1 - k), 0)[0:TM]
        dext_sc[TM:TM + 8, :] = dxr[0:8]
        dxy_ref[:, 0:D] = dx.astype(bf16)

        x = x_ref[...]
        xext_sc[0:8, :] = jnp.where(tile > 0, xp_ref[...], 0.0)
        xext_sc[8:8 + TM, :] = x
        xe = xext_sc[...]
        dcw_ref[KC4 - 1:KC4, :] += jnp.sum(dxr * x, axis=0, keepdims=True)
        for k in range(KC4 - 1):
            xs = pltpu.roll(xe, KC4 - 1 - k, 0)[8:8 + TM]
            dcw_ref[k:k + 1, :] += jnp.sum(dxr * xs, axis=0, keepdims=True)
        dcb_ref[...] += jnp.sum(dxr, axis=0, keepdims=True)

        @pl.when(i == nt - 1)
        def _():
            for hh in range(NHEAD):
                for qc in range(NCHIP):
                    dwa_ref[qc, hh] = dwa_sc[hh, qc * hq:(qc + 1) * hq, :].astype(bf16)
                    dwx_ref[qc, hh] = dwx_sc[hh, qc * hq:(qc + 1) * hq, :].astype(bf16)
            dlam_ref[...] = -dsp_sc[...] * _sigmoid(-lam_ref[...])

    rev = lambda i: (nt - 1 - i, 0)
    vec = pl.BlockSpec((1, D), lambda i: (0, 0))
    prev8 = lambda i: jnp.maximum((nt - 1 - i) * ng - 1, 0)
    wblk = pl.BlockSpec((NHEAD, HD, HD), lambda i: (0, 0, 0))
    gblk = pl.BlockSpec((NCHIP, NHEAD, hq, HD), lambda i: (0, 0, 0, 0))
    return pl.pallas_call(
        body, name="rnn_bwd", grid=(nt,),
        in_specs=[
            pl.BlockSpec((TM, D), rev),
            pl.BlockSpec((TM, D), rev),
            pl.BlockSpec((TM, D), rev),
            pl.BlockSpec((8, D), lambda i: (prev8(i), 0)),
            pl.BlockSpec((TM, D), lambda i: (nt - 1 - i, 0)),
            pl.BlockSpec((8, D), lambda i: (prev8(i), 0)),
            pl.BlockSpec((TM, D), lambda i: (nt - 1 - i, 1)),
            pl.BlockSpec(memory_space=pl.ANY),
            pl.BlockSpec((KC4, D), lambda i: (0, 0)),
            wblk, vec, wblk, vec, vec,
        ],
        out_specs=[
            pl.BlockSpec((TM, 2 * D), lambda i: (nt - 1 - i, 0)),
            gblk, gblk,
            pl.BlockSpec((KC4, D), lambda i: (0, 0)),
            vec, vec, vec, vec,
        ],
        out_shape=[jax.ShapeDtypeStruct((t, NIN), bf16),
                   jax.ShapeDtypeStruct((NCHIP, NHEAD, hq, HD), bf16),
                   jax.ShapeDtypeStruct((NCHIP, NHEAD, hq, HD), bf16),
                   jax.ShapeDtypeStruct((KC4, D), f32),
                   jax.ShapeDtypeStruct((1, D), f32), jax.ShapeDtypeStruct((1, D), f32),
                   jax.ShapeDtypeStruct((1, D), f32), jax.ShapeDtypeStruct((1, D), f32)],
        scratch_shapes=[pltpu.VMEM((1, D), f32), pltpu.VMEM((1, D), f32),
                        pltpu.VMEM((TM + 8, D), f32), pltpu.VMEM((TM + 8, D), f32),
                        pltpu.VMEM((TM, D), f32), pltpu.VMEM((TM, D), f32),
                        pltpu.VMEM((NHEAD, HD, HD), f32), pltpu.VMEM((NHEAD, HD, HD), f32),
                        pltpu.VMEM((1, D), f32)],
        input_output_aliases={7: 0},
        compiler_params=_cp(1),
    )(dz, xr, hr, hr, proj, proj, proj, dproj, cw, wa, ba, wx, bx, lam)


def _inproj_bwd(dproj, dh, h, g, win, after=None):
    t = h.shape[0]
    tn = NIN // NCHIP
    nj = NIN // tn
    per = (NIN // NCHIP) // tn

    def body(dp_ref, dh_ref, h_ref, g_ref, w_ref, dhi_ref, dg_ref, db_ref, dn_sc):
        i = pl.program_id(0)
        j = pl.program_id(1)

        @pl.when(jnp.logical_and(i == 0, j == 0))
        def _():
            dg_ref[...] = jnp.zeros_like(dg_ref)
            db_ref[...] = jnp.zeros_like(db_ref)

        @pl.when(j == 0)
        def _():
            dn_sc[...] = jnp.zeros_like(dn_sc)

        dp = dp_ref[...]
        dn_sc[...] += _nt_dot(dp, w_ref[j])
        db_ref[j] += jnp.sum(dp.astype(f32), axis=0, keepdims=True)

        @pl.when(j == nj - 1)
        def _():
            dhin, dg = _rms_bwd(dn_sc[...], h_ref[...], g_ref[...])
            dhi_ref[...] = dh_ref[...] + dhin
            dg_ref[...] += dg

    rowd = pl.BlockSpec((TM, D), lambda i, j: (i, 0))
    vec = pl.BlockSpec((1, D), lambda i, j: (0, 0))
    body, in_specs, args = _ordered(
        body,
        [pl.BlockSpec((TM, tn), lambda i, j: (i, j)), rowd, rowd, vec,
         RESIDENT],
        (dproj, dh, h, g, win), after)
    return pl.pallas_call(
        body, name="inproj_bwd", grid=(t // TM, nj),
        in_specs=in_specs,
        out_specs=[rowd, vec, pl.BlockSpec((nj, 1, tn), lambda i, j: (0, 0, 0))],
        out_shape=[jax.ShapeDtypeStruct((t, D), f32), jax.ShapeDtypeStruct((1, D), f32),
                   jax.ShapeDtypeStruct((nj, 1, tn), f32)],
        scratch_shapes=[pltpu.VMEM((TM, D), f32)],
        compiler_params=_cp(2),
    )(*args)


def _ffn_gu_grad(n, dgate, dup, tag, after=None):
    half = _tn_matmul(n, dgate, D, FS, (NCHIP, D, FS), (None, D, FS), lambda k, nn, m: (nn, 0, 0),
                      tag + "_dwg", after=after)
    return _tn_matmul(n, dup, D, FS, (NCHIP, D, FS), (None, D, FS), lambda k, nn, m: (2 + nn, 0, 0),
                      tag + "_dwu", base=half)


def _ffn_down_grad(a, df, tag, after=None):
    return _tn_matmul(a, df, FS, D, (F, D), (FS, D), lambda k, nn, m: (k, 0), tag + "_dwd", after=after)


def _square_grad(a, b, name):
    return _tn_matmul(a, b, D, D, (D, D), (D, D), lambda k, nn, m: (0, 0), name)


ANY = pl.BlockSpec(memory_space=pl.ANY)


def _place():
    x, y, c = lax.axis_index("x"), lax.axis_index("y"), lax.axis_index("c")
    chips = [(1 - x, y), (x, 1 - y), (1 - x, 1 - y)]
    return x, y, c, chips


def _chip_id(chip):
    return 2 * chip[0] + chip[1]


def _cast_into_slot(w2d, qc, dtype, name, after=None):
    r, cc = w2d.shape
    hr = r // 2

    def body(qc_ref, *refs):
        del qc_ref
        w_ref, o_ref = refs[-2:]
        o_ref[...] = w_ref[...].astype(dtype)

    in_specs, args = [pl.BlockSpec((hr, cc), lambda h, qc_ref: (h, 0))], (w2d,)
    if after is not None:
        in_specs, args = [ANY_SPEC] + in_specs, (after,) + args
    return pl.pallas_call(
        body, name=name,
        grid_spec=pltpu.PrefetchScalarGridSpec(
            num_scalar_prefetch=1, grid=(2,),
            in_specs=in_specs,
            out_specs=pl.BlockSpec((None, None, hr, cc), lambda h, qc_ref: (qc_ref[0], h, 0, 0))),
        out_shape=jax.ShapeDtypeStruct((NCHIP, 2, hr, cc), dtype),
        compiler_params=_cp(1),
    )(qc, *args)


def _place_pack(pack, qc):
    def body(qc_ref, p_ref, o_ref):
        del qc_ref
        o_ref[...] = p_ref[...]

    return pl.pallas_call(
        body, name="place_pack",
        grid_spec=pltpu.PrefetchScalarGridSpec(
            num_scalar_prefetch=1, grid=(1,),
            in_specs=[pl.BlockSpec(pack.shape, lambda i, qc_ref: (0, 0))],
            out_specs=pl.BlockSpec((None,) + pack.shape, lambda i, qc_ref: (2 * qc_ref[0] + qc_ref[1], 0, 0))),
        out_shape=jax.ShapeDtypeStruct((8,) + pack.shape, pack.dtype),
        compiler_params=_cp(1),
    )(qc, pack)


def _gather_shards(bufs):
    n = len(bufs)

    def body(*refs):
        outs = refs[n:2 * n]
        send_sems, recv_sems = refs[2 * n:]
        x, y, c, chips = _place()
        q = 2 * x + y
        sibling = (x, y, 1 - c)

        def remote(a, k, blk, to):
            return pltpu.make_async_remote_copy(src_ref=blk, dst_ref=blk, send_sem=send_sems.at[a, k],
                                                recv_sem=recv_sems.at[a, k], device_id=to, device_id_type=MESH)

        sent = []
        for a in range(n):
            for j, chip in enumerate(chips):
                cp = remote(a, j, outs[a].at[q, c], (chip[0], chip[1], c))
                cp.start()
                sent.append(cp)
        for a in range(n):
            for j, chip in enumerate(chips):
                got = outs[a].at[_chip_id(chip), c]
                remote(a, j, got, (chip[0], chip[1], c)).wait_recv()
                cp = remote(a, 3 + j, got, sibling)
                cp.start()
                sent.append(cp)
        for a in range(n):
            for j, chip in enumerate(chips):
                remote(a, 3 + j, outs[a].at[_chip_id(chip), 1 - c], sibling).wait_recv()
        for cp in sent:
            cp.wait_send()

    return pl.pallas_call(
        body, name="gather_shards",
        in_specs=[ANY] * n, out_specs=[ANY] * n,
        out_shape=[jax.ShapeDtypeStruct(s.shape, s.dtype) for s in bufs],
        scratch_shapes=[pltpu.SemaphoreType.DMA((n, 6)), pltpu.SemaphoreType.DMA((n, 6))],
        input_output_aliases={a: a for a in range(n)},
    )(*bufs)


def _pair_exchange(parts, name):
    n = len(parts)

    def body(*refs):
        ins, outs = refs[:n], refs[n:2 * n]
        send_sems, recv_sems = refs[2 * n:]
        x, y, c, _ = _place()
        copies = []
        for a in range(n):
            cp = pltpu.make_async_remote_copy(
                src_ref=ins[a].at[:, 1 - c], dst_ref=outs[a],
                send_sem=send_sems.at[a], recv_sem=recv_sems.at[a],
                device_id=(x, y, 1 - c), device_id_type=MESH)
            cp.start()
            copies.append(cp)
        for cp in copies:
            cp.wait()

    return pl.pallas_call(
        body, name=name,
        in_specs=[ANY] * n, out_specs=[ANY] * n,
        out_shape=[jax.ShapeDtypeStruct((NCHIP,) + s.shape[2:], s.dtype) for s in parts],
        scratch_shapes=[pltpu.SemaphoreType.DMA((n,)), pltpu.SemaphoreType.DMA((n,))],
    )(*parts)


def _pair_add(part, got, qc, name):
    _, _, hr, cc = part.shape

    def body(qc_ref, p_ref, g_ref, o_ref, land_ref):
        s = pl.program_id(0)
        val = (p_ref[...].astype(f32) + g_ref[...].astype(f32)).astype(bf16)
        o_ref[...] = val

        @pl.when(s == qc_ref[0])
        def _():
            land_ref[...] = val

    return pl.pallas_call(
        body, name=name,
        grid_spec=pltpu.PrefetchScalarGridSpec(
            num_scalar_prefetch=1, grid=(NCHIP,),
            in_specs=[pl.BlockSpec((None, None, hr, cc), lambda s, qc_ref: (s, qc_ref[1], 0, 0)),
                      pl.BlockSpec((None, hr, cc), lambda s, qc_ref: (s, 0, 0))],
            out_specs=[pl.BlockSpec((None, hr, cc), lambda s, qc_ref: (s, 0, 0)),
                       pl.BlockSpec((None, hr, cc), lambda s, qc_ref: (qc_ref[0], 0, 0))]),
        out_shape=[jax.ShapeDtypeStruct((NCHIP, hr, cc), bf16)] * 2,
        compiler_params=_cp(1),
    )(qc, part, got)


def _chip_exchange(sums, lands):
    n = len(sums)

    def body(*refs):
        ins, outs = refs[:n], refs[2 * n:3 * n]
        send_sems, recv_sems = refs[3 * n:]
        x, y, c, chips = _place()
        q = 2 * x + y
        sent = []
        for a in range(n):
            for j, chip in enumerate(chips):
                cp = pltpu.make_async_remote_copy(
                    src_ref=ins[a].at[_chip_id(chip)], dst_ref=outs[a].at[q],
                    send_sem=send_sems.at[a, j], recv_sem=recv_sems.at[a, j],
                    device_id=(chip[0], chip[1], c), device_id_type=MESH)
                cp.start()
                sent.append(cp)
        for a in range(n):
            for j, chip in enumerate(chips):
                got = outs[a].at[_chip_id(chip)]
                pltpu.make_async_remote_copy(
                    src_ref=got, dst_ref=got, send_sem=send_sems.at[a, j], recv_sem=recv_sems.at[a, j],
                    device_id=(chip[0], chip[1], c), device_id_type=MESH).wait_recv()
        for cp in sent:
            cp.wait_send()

    return pl.pallas_call(
        body, name="chip_exchange",
        in_specs=[ANY] * (2 * n), out_specs=[ANY] * n,
        out_shape=[jax.ShapeDtypeStruct(s.shape, s.dtype) for s in lands],
        scratch_shapes=[pltpu.SemaphoreType.DMA((n, 3)), pltpu.SemaphoreType.DMA((n, 3))],
        input_output_aliases={n + a: a for a in range(n)},
    )(*sums, *lands)


def _sum_chips(got, name):
    _, hr, cc = got.shape

    def body(g_ref, o_ref):
        acc = g_ref[0].astype(f32)
        for s in range(1, NCHIP):
            acc = acc + g_ref[s].astype(f32)
        o_ref[...] = acc

    return pl.pallas_call(
        body, name=name, grid=(1,),
        in_specs=[pl.BlockSpec((NCHIP, hr, cc), lambda i: (0, 0, 0))],
        out_specs=pl.BlockSpec((hr, cc), lambda i: (0, 0)),
        out_shape=jax.ShapeDtypeStruct((hr, cc), f32),
        compiler_params=_cp(1),
    )(got)


def _pair_share(halves, name):
    n = len(halves)

    def body(*refs):
        ins, outs = refs[:n], refs[n:2 * n]
        send_sems, recv_sems = refs[2 * n:]
        x, y, c, _ = _place()
        copies = []
        for a in range(n):
            cp = pltpu.make_async_remote_copy(
                src_ref=ins[a], dst_ref=outs[a], send_sem=send_sems.at[a], recv_sem=recv_sems.at[a],
                device_id=(x, y, 1 - c), device_id_type=MESH)
            cp.start()
            copies.append(cp)
        for cp in copies:
            cp.wait()

    return pl.pallas_call(
        body, name=name,
        in_specs=[ANY] * n, out_specs=[ANY] * n,
        out_shape=[jax.ShapeDtypeStruct(s.shape, s.dtype) for s in halves],
        scratch_shapes=[pltpu.SemaphoreType.DMA((n,)), pltpu.SemaphoreType.DMA((n,))],
    )(*halves)


def _all_copy(buf_ref, send_ref, recv_ref, k, x, y, c, landing):
    px, py, pc = (1 - x if k & 4 else x, 1 - y if k & 2 else y, 1 - c if k & 1 else c)
    me = 4 * x + 2 * y + c
    there = 4 * px + 2 * py + pc
    return pltpu.make_async_remote_copy(
        src_ref=buf_ref.at[me], dst_ref=buf_ref.at[there if landing else me],
        send_sem=send_ref.at[k - 1], recv_sem=recv_ref.at[k - 1],
        device_id=(px, py, pc), device_id_type=MESH)


def _gather_all_start(buf, name):
    def body(in_ref, send, recv, thru, token):
        del thru
        x, y, c, _ = _place()
        for k in range(1, 8):
            _all_copy(in_ref, send, recv, k, x, y, c, False).start()
        token[...] = jnp.zeros_like(token)

    return pl.pallas_call(
        body, name=name,
        in_specs=[HBM],
        out_specs=[SEM, SEM, HBM, pl.BlockSpec(memory_space=pltpu.VMEM)],
        out_shape=[pltpu.SemaphoreType.DMA((7,)), pltpu.SemaphoreType.DMA((7,)),
                   pltpu.HBM(buf.shape, buf.dtype), jax.ShapeDtypeStruct((8, 128), f32)],
        input_output_aliases={0: 2},
        compiler_params=pltpu.CompilerParams(has_side_effects=EFFECT),
    )(_in_hbm(buf))


def _gather_all_wait(send, recv, buf, after, name):
    def body(in_ref, send_r, recv_r, after_ref, out_ref):
        del after_ref, out_ref
        x, y, c, _ = _place()
        for k in range(1, 8):
            cp = _all_copy(in_ref, send_r, recv_r, k, x, y, c, True)
            cp.wait_send()
            cp.wait_recv()

    return pl.pallas_call(
        body, name=name,
        in_specs=[HBM, SEM, SEM, ANY],
        out_specs=HBM,
        out_shape=pltpu.HBM(buf.shape, buf.dtype),
        input_output_aliases={0: 0},
        compiler_params=pltpu.CompilerParams(has_side_effects=EFFECT),
    )(buf, send, recv, after)


HBM = pl.BlockSpec(memory_space=pltpu.HBM)
SEM = pl.BlockSpec(memory_space=pltpu.SEMAPHORE)
EFFECT = pltpu.SideEffectType.DATAFLOW_SIDE_EFFECTING
N_PEER = 3


def _in_hbm(a):
    return pltpu.with_memory_space_constraint(a, pltpu.HBM)


def _tie(a, token):
    return lax.optimization_barrier((a, token))[0]


def _gather_copy(buf_ref, send_ref, recv_ref, j, chip, q, c, landing_chip):
    return pltpu.make_async_remote_copy(
        src_ref=buf_ref.at[q, c], dst_ref=buf_ref.at[landing_chip, c],
        send_sem=send_ref.at[j], recv_sem=recv_ref.at[j],
        device_id=(chip[0], chip[1], c), device_id_type=MESH)


def _gather_start(bufs, name):
    n = len(bufs)

    def body(*refs):
        ins = refs[:n]
        send, recv = refs[n:2 * n], refs[2 * n:3 * n]
        token = refs[4 * n]
        x, y, c, chips = _place()
        q = 2 * x + y
        for a in range(n):
            for j, chip in enumerate(chips):
                _gather_copy(ins[a], send[a], recv[a], j, chip, q, c, q).start()
        token[...] = jnp.zeros_like(token)

    sems = [pltpu.SemaphoreType.DMA((N_PEER,))] * (2 * n)
    outs = pl.pallas_call(
        body, name=name,
        in_specs=[HBM] * n,
        out_specs=[SEM] * (2 * n) + [HBM] * n + [pl.BlockSpec(memory_space=pltpu.VMEM)],
        out_shape=sems + [pltpu.HBM(b.shape, b.dtype) for b in bufs] + [jax.ShapeDtypeStruct((8, 128), f32)],
        input_output_aliases={a: 2 * n + a for a in range(n)},
        compiler_params=pltpu.CompilerParams(has_side_effects=EFFECT),
    )(*[_in_hbm(b) for b in bufs])
    return list(outs[:n]), list(outs[n:2 * n]), list(outs[2 * n:3 * n]), outs[3 * n]


def _gather_wait(send, recv, bufs, after, name):
    n = len(bufs)

    def body(*refs):
        ins = refs[:n]
        send_r, recv_r = refs[n:2 * n], refs[2 * n:3 * n]
        x, y, c, chips = _place()
        q = 2 * x + y
        for a in range(n):
            for j, chip in enumerate(chips):
                cp = _gather_copy(ins[a], send_r[a], recv_r[a], j, chip, q, c, _chip_id(chip))
                cp.wait_send()
                cp.wait_recv()

    afters = after if isinstance(after, (tuple, list)) else (after,)
    outs = pl.pallas_call(
        body, name=name,
        in_specs=[HBM] * n + [SEM] * (2 * n) + [ANY] * len(afters),
        out_specs=[HBM] * n,
        out_shape=[pltpu.HBM(b.shape, b.dtype) for b in bufs],
        input_output_aliases={a: a for a in range(n)},
        compiler_params=pltpu.CompilerParams(has_side_effects=EFFECT),
    )(*bufs, *send, *recv, *afters)
    return list(outs)


def _forward_halves(bufs, name):
    n = len(bufs)

    def body(*refs):
        outs = refs[n:2 * n]
        send_sems, recv_sems = refs[2 * n:]
        x, y, c, chips = _place()
        sibling = (x, y, 1 - c)

        def remote(a, j, blk):
            return pltpu.make_async_remote_copy(src_ref=blk, dst_ref=blk, send_sem=send_sems.at[a, j],
                                                recv_sem=recv_sems.at[a, j], device_id=sibling,
                                                device_id_type=MESH)

        sent = []
        for a in range(n):
            for j, chip in enumerate(chips):
                cp = remote(a, j, outs[a].at[_chip_id(chip), c])
                cp.start()
                sent.append(cp)
        for a in range(n):
            for j, chip in enumerate(chips):
                remote(a, j, outs[a].at[_chip_id(chip), 1 - c]).wait_recv()
        for cp in sent:
            cp.wait_send()

    return pl.pallas_call(
        body, name=name,
        in_specs=[ANY] * n, out_specs=[ANY] * n,
        out_shape=[jax.ShapeDtypeStruct(s.shape, s.dtype) for s in bufs],
        scratch_shapes=[pltpu.SemaphoreType.DMA((n, N_PEER)), pltpu.SemaphoreType.DMA((n, N_PEER))],
        input_output_aliases={a: a for a in range(n)},
    )(*bufs)


def _reduce_copy(sum_ref, land_ref, send_ref, recv_ref, j, chip, q, c, landing_chip):
    return pltpu.make_async_remote_copy(
        src_ref=sum_ref.at[_chip_id(chip)], dst_ref=land_ref.at[landing_chip],
        send_sem=send_ref.at[j], recv_sem=recv_ref.at[j],
        device_id=(chip[0], chip[1], c), device_id_type=MESH)


def _reduce_start(sums, lands, name):
    n = len(sums)

    def body(*refs):
        s_in, l_in = refs[:n], refs[n:2 * n]
        send, recv = refs[2 * n:3 * n], refs[3 * n:4 * n]
        token = refs[6 * n]
        x, y, c, chips = _place()
        q = 2 * x + y
        for a in range(n):
            for j, chip in enumerate(chips):
                _reduce_copy(s_in[a], l_in[a], send[a], recv[a], j, chip, q, c, q).start()
        token[...] = jnp.zeros_like(token)

    sems = [pltpu.SemaphoreType.DMA((N_PEER,))] * (2 * n)
    outs = pl.pallas_call(
        body, name=name,
        in_specs=[HBM] * (2 * n),
        out_specs=[SEM] * (2 * n) + [HBM] * (2 * n) + [pl.BlockSpec(memory_space=pltpu.VMEM)],
        out_shape=sems + [pltpu.HBM(b.shape, b.dtype) for b in list(sums) + list(lands)]
        + [jax.ShapeDtypeStruct((8, 128), f32)],
        input_output_aliases={a: 2 * n + a for a in range(2 * n)},
        compiler_params=pltpu.CompilerParams(has_side_effects=EFFECT),
    )(*[_in_hbm(b) for b in list(sums) + list(lands)])
    return (list(outs[:n]), list(outs[n:2 * n]), list(outs[2 * n:3 * n]), list(outs[3 * n:4 * n]),
            outs[4 * n])


def _reduce_wait(send, recv, sums, lands, after, name):
    n = len(sums)

    def body(*refs):
        s_in, l_in = refs[:n], refs[n:2 * n]
        send_r, recv_r = refs[2 * n:3 * n], refs[3 * n:4 * n]
        x, y, c, chips = _place()
        q = 2 * x + y
        for a in range(n):
            for j, chip in enumerate(chips):
                cp = _reduce_copy(s_in[a], l_in[a], send_r[a], recv_r[a], j, chip, q, c, _chip_id(chip))
                cp.wait_send()
                cp.wait_recv()

    outs = pl.pallas_call(
        body, name=name,
        in_specs=[HBM] * (2 * n) + [SEM] * (2 * n) + [ANY],
        out_specs=[HBM] * (2 * n),
        out_shape=[pltpu.HBM(b.shape, b.dtype) for b in list(sums) + list(lands)],
        input_output_aliases={a: a for a in range(2 * n)},
        compiler_params=pltpu.CompilerParams(has_side_effects=EFFECT),
    )(*sums, *lands, *send, *recv, after)
    return list(outs[n:])


def _adamw_math(w, g, m, v):
    m = ADAM_B1 * m + (1.0 - ADAM_B1) * g
    v = ADAM_B2 * v + (1.0 - ADAM_B2) * (g * g)
    m_hat = m / (1.0 - ADAM_B1 ** ADAM_STEP)
    v_hat = v / (1.0 - ADAM_B2 ** ADAM_STEP)
    delta = -ADAM_LR * (m_hat / (jnp.sqrt(v_hat) + ADAM_EPS) + ADAM_WD * w)
    return delta, m, v


def _adamw(w, mine, theirs, m, v, qc, name):
    r, cc = w.shape
    hr = r // 2
    tr = next(hr // k for k in range(1, hr + 1)
              if hr % k == 0 and (hr // k) % 8 == 0 and (hr // k) * cc * 4 <= (1 << 20))
    nb = hr // tr

    def body(qc_ref, w_ref, a_ref, b_ref, m_ref, v_ref, g_ref, d_ref, mo_ref, vo_ref):
        g = jnp.where(pl.program_id(0) == qc_ref[1], a_ref[...], b_ref[...])
        g_ref[...] = g
        d_ref[...], mo_ref[...], vo_ref[...] = _adamw_math(w_ref[...], g, m_ref[...], v_ref[...])

    full = pl.BlockSpec((tr, cc), lambda h, i, qc_ref: (h * nb + i, 0))
    half = pl.BlockSpec((tr, cc), lambda h, i, qc_ref: (i, 0))
    return pl.pallas_call(
        body, name=name,
        grid_spec=pltpu.PrefetchScalarGridSpec(
            num_scalar_prefetch=1, grid=(2, nb),
            in_specs=[full, half, half, full, full], out_specs=[full] * 4),
        out_shape=[jax.ShapeDtypeStruct((r, cc), f32)] * 4,
        compiler_params=_cp(2),
    )(qc, w, mine, theirs, m, v)


REPL = [("ffn1_norm", 1), ("mix_norm", 1), ("b_in", 6), ("rnn_conv_b", 1), ("rg_b_a", 1), ("rg_b_x", 1),
        ("rg_lambda", 1), ("conv_dw_b", 1), ("conv_ln_g", 1), ("conv_ln_b", 1), ("conv_b_proj", 1),
        ("ffn2_norm", 1), ("final_norm", 1)]
COLSH = [("meta_tokens", NMETA), ("rnn_conv_w", KC4), ("conv_dw_w", KC31)]
SMALL = REPL + COLSH
CS = D // NCHIP


def _pack_rows():
    starts, row = {}, 0
    for k, rows in REPL:
        starts[k] = row
        row += rows
    for k, rows in COLSH:
        row = -(-row // 8) * 8
        starts[k] = row
        row += rows
    return starts, -(-row // 8) * 8


PACK_START, SMALL_ROWS = _pack_rows()


def _small_pack(g):
    pieces, row = [], 0
    for k, rows in SMALL:
        if PACK_START[k] > row:
            pieces.append(jnp.zeros((PACK_START[k] - row, D), f32))
        pieces.append(g[k].reshape(rows, D))
        row = PACK_START[k] + rows
    pieces.append(jnp.zeros((SMALL_ROWS - row, D), f32))
    return jnp.concatenate(pieces, axis=0)


def _adamw_small(packs, ws, ms, vs):
    ns = len(SMALL)

    def body(*refs):
        pack_ref = refs[0]
        w_refs, m_refs, v_refs = refs[1:1 + ns], refs[1 + ns:1 + 2 * ns], refs[1 + 2 * ns:1 + 3 * ns]
        outs = refs[1 + 3 * ns:1 + 7 * ns]
        g_refs, d_refs, mo_refs, vo_refs = outs[:ns], outs[ns:2 * ns], outs[2 * ns:3 * ns], outs[3 * ns:]
        gsum_sc = refs[1 + 7 * ns]
        q = 2 * lax.axis_index("x") + lax.axis_index("y")
        acc = pack_ref[0]
        for dev in range(1, 8):
            acc = acc + pack_ref[dev]
        gsum_sc[...] = acc
        for idx, (name, rows) in enumerate(SMALL):
            row = PACK_START[name]
            if idx < len(REPL):
                for k in range(rows):
                    cols = slice(k * D, (k + 1) * D)
                    g = gsum_sc[row + k:row + k + 1, :]
                    d, mm, vv = _adamw_math(w_refs[idx][:, cols], g, m_refs[idx][:, cols], v_refs[idx][:, cols])
                    g_refs[idx][:, cols] = g
                    d_refs[idx][:, cols] = d
                    mo_refs[idx][:, cols] = mm
                    vo_refs[idx][:, cols] = vv
            else:
                g = gsum_sc[row:row + rows, pl.ds(pl.multiple_of(q * CS, CS), CS)]
                d, mm, vv = _adamw_math(w_refs[idx][...], g, m_refs[idx][...], v_refs[idx][...])
                g_refs[idx][...] = g
                d_refs[idx][...] = d
                mo_refs[idx][...] = mm
                vo_refs[idx][...] = vv

    shapes = [jax.ShapeDtypeStruct(w.shape, f32) for w in ws]
    return pl.pallas_call(
        body, name="adamw_small",
        out_shape=shapes * 4,
        scratch_shapes=[pltpu.VMEM((SMALL_ROWS, D), f32)],
        compiler_params=pltpu.CompilerParams(vmem_limit_bytes=VMEM_LIMIT),
    )(packs, *ws, *ms, *vs)


BIG = ["ffn1_w_gu", "ffn1_w_down", "w_in", "rg_w_a", "rg_w_x", "rnn_w_proj", "conv_w_proj", "w_out",
       "ffn2_w_gu", "ffn2_w_down"]
WEIGHTS = ['meta_tokens', 'ffn1_norm', 'ffn1_w_gu', 'ffn1_w_down', 'mix_norm', 'w_in', 'b_in', 'rnn_conv_w',
           'rnn_conv_b', 'rg_w_a', 'rg_b_a', 'rg_w_x', 'rg_b_x', 'rg_lambda', 'rnn_w_proj', 'conv_dw_w',
           'conv_dw_b', 'conv_ln_g', 'conv_ln_b', 'conv_w_proj', 'conv_b_proj', 'w_out', 'ffn2_norm',
           'ffn2_w_gu', 'ffn2_w_down', 'final_norm']


def _as2d(a):
    return a.reshape(-1, a.shape[-1])


def _step(x, loss_target, w, m, v):
    seq = x.shape[1]
    n_valid = NMETA + seq
    t = -(-n_valid // TM) * TM

    qc = jnp.stack([2 * lax.axis_index("x") + lax.axis_index("y"), lax.axis_index("c")]).astype(jnp.int32)
    p = {k: w[k].reshape(1, rows * D) for k, rows in REPL}

    first = ["ffn1_w_gu", "ffn1_w_down", "small"]
    later = [["w_in"], ["rg_w_a", "rg_w_x", "rnn_w_proj", "conv_w_proj", "w_out"], ["ffn2_w_gu", "ffn2_w_down"]]
    small_rows = sum(r for _, r in COLSH)
    small = jnp.concatenate([_as2d(w[k]) for k, _ in COLSH] + [jnp.zeros((64 - small_rows, CS), f32)], axis=0)

    def cast(k, token=None):
        src, dtype = (small, f32) if k == "small" else (_as2d(w[k]), bf16)
        return _cast_into_slot(src, qc, dtype, "cast_" + k, after=token)

    send1, recv1, bufs1, token1 = _gather_start([cast(k) for k in first], "gather_start_first")
    rest = [k for grp in later for k in grp]
    send2, recv2, bufs2, token2 = _gather_start([cast(k, token1) for k in rest], "gather_start_rest")

    def finish(names, send, recv, bufs, after, tag):
        done = _forward_halves(_gather_wait(send, recv, bufs, after, "gather_wait_" + tag), "gather_forward_" + tag)
        for k, b in zip(names, done):
            full = b.reshape(NCHIP, 2 * b.shape[2], b.shape[3])
            if k in ("ffn1_w_down", "ffn2_w_down"):
                full = full.reshape(F, D)
            elif k in ("rnn_w_proj", "conv_w_proj", "w_out"):
                full = full.reshape(D, D)
            elif k in ("rg_w_a", "rg_w_x"):
                full = full.reshape(NCHIP, NHEAD, HD // NCHIP, HD).transpose(1, 0, 2, 3).reshape(NHEAD, HD, HD)
            p[k] = full

    def group(names):
        idx = [rest.index(k) for k in names]
        return names, [send2[i] for i in idx], [recv2[i] for i in idx], [bufs2[i] for i in idx]

    h0 = jnp.pad(x[0], ((NMETA, t - n_valid), (0, 0)))
    tgt = jnp.pad(loss_target[0], ((NMETA, t - n_valid), (0, 0)))
    finish(first, send1, recv1, bufs1, (token2, h0, tgt), "first")
    small_full = p.pop("small").transpose(1, 0, 2).reshape(64, D)
    row = 0
    for k, rows in COLSH:
        p[k] = small_full[row:row + rows]
        row += rows

    h0 = lax.dynamic_update_slice(h0, p["meta_tokens"], (0, 0))
    h1, gate1, up1, n1 = _ffn_fwd(h0, p["ffn1_norm"], p["ffn1_w_gu"], p["ffn1_w_down"], "ffn1_fwd")
    finish(*group(later[0]), h1, "in")
    proj, n2 = _inproj_fwd(h1, p["mix_norm"], p["w_in"], p["b_in"])
    finish(*group(later[1]), proj, "mix")
    xr, hr, z = _rnn_fwd(proj, p["rnn_conv_w"], p["rnn_conv_b"], p["rg_w_a"], p["rg_b_a"],
                         p["rg_w_x"], p["rg_b_x"], p["rg_lambda"])
    vc, s = _conv_fwd(proj, p["conv_dw_w"], p["conv_dw_b"], p["conv_ln_g"], p["conv_ln_b"])
    h2 = _merge_fwd(h1, z, s, proj, p["rnn_w_proj"], p["conv_w_proj"], p["conv_b_proj"], p["w_out"])
    finish(*group(later[2]), h2, "ffn2")
    h3, gate2, up2, n3 = _ffn_fwd(h2, p["ffn2_norm"], p["ffn2_w_gu"], p["ffn2_w_down"], "ffn2_fwd")
    dh3, loss_blk, d_final = _final_loss(h3, p["final_norm"], tgt, n_valid)
    loss = lax.psum(loss_blk[0, 0], ("x", "y", "c"))

    g = {"final_norm": d_final}
    pending = []

    def reduce_start(names, tag):
        parts = []
        for k in names:
            rows = g[k].size // (NCHIP * g[k].shape[-1])
            parts.append(g[k].reshape((NCHIP, 2, rows // 2, g[k].shape[-1])))
        from_sibling = _pair_exchange(parts, "pair_exchange_" + tag)
        added = [_pair_add(pp, gg, qc, "pair_add_" + k) for pp, gg, k in zip(parts, from_sibling, names)]
        send, recv, sums, lands, token = _reduce_start([a for a, _ in added], [b for _, b in added],
                                                       "reduce_start_" + tag)
        pending.append((names, tag, send, recv, sums, lands))
        return token

    dh2, dgate2, dup2, a2, df2, g["ffn2_norm"] = _ffn_bwd(
        dh3, h2, p["ffn2_norm"], gate2, up2, p["ffn2_w_gu"], p["ffn2_w_down"], "ffn2_bwd")
    g["ffn2_w_gu"] = _ffn_gu_grad(n3, dgate2, dup2, "ffn2")
    g["ffn2_w_down"] = _ffn_down_grad(a2, df2, "ffn2")
    token = reduce_start(["ffn2_w_gu", "ffn2_w_down"], "ffn2")

    dz, ds, dproj, dh2b, merged, dya, dyb, g["conv_b_proj"] = _merge_bwd(
        dh2, z, s, proj, p["rnn_w_proj"], p["conv_w_proj"], p["conv_b_proj"], p["w_out"], after=token)
    g["w_out"] = _square_grad(merged, dh2b, "dw_out")
    g["rnn_w_proj"] = _square_grad(z, dya, "dw_rnn_proj")
    g["conv_w_proj"] = _square_grad(s, dyb, "dw_conv_proj")
    dproj, g["conv_dw_w"], g["conv_dw_b"], g["conv_ln_g"], g["conv_ln_b"] = _conv_bwd(
        ds, vc, proj, dproj, p["conv_dw_w"], p["conv_ln_g"], p["conv_ln_b"])
    (dproj, g["rg_w_a"], g["rg_w_x"], g["rnn_conv_w"], g["rnn_conv_b"], g["rg_b_a"], g["rg_b_x"],
     g["rg_lambda"]) = _rnn_bwd(dz, xr, hr, proj, dproj, p["rnn_conv_w"], p["rg_w_a"], p["rg_b_a"],
                                p["rg_w_x"], p["rg_b_x"], p["rg_lambda"])
    token = reduce_start(["w_out", "rnn_w_proj", "conv_w_proj", "rg_w_a", "rg_w_x"], "mix")

    dh1, g["mix_norm"], db_in = _inproj_bwd(dproj, dh2, h1, p["mix_norm"], p["w_in"], after=token)
    g["b_in"] = db_in.reshape(1, NIN)
    g["w_in"] = _tn_matmul(n2, dproj, D, NIN // NCHIP, (NCHIP, D, NIN // NCHIP),
                           (None, D, NIN // NCHIP), lambda k, nn, mm: (nn, 0, 0), "dw_in")
    token = reduce_start(["w_in"], "in")

    dh0, dgate1, dup1, a1, df1, g["ffn1_norm"] = _ffn_bwd(
        dh1, h0, p["ffn1_norm"], gate1, up1, p["ffn1_w_gu"], p["ffn1_w_down"], "ffn1_bwd", after=token)
    g["meta_tokens"] = dh0[0:NMETA]
    grad_x = dh0[NMETA:n_valid][None]

    send_s, recv_s, pack_buf, token = _gather_all_start(_place_pack(_small_pack(g), qc), "gather_all_start")
    g["ffn1_w_down"] = _ffn_down_grad(a1, df1, "ffn1", after=token)
    token = reduce_start(["ffn1_w_down"], "ffn1_down")
    g["ffn1_w_gu"] = _ffn_gu_grad(n1, dgate1, dup1, "ffn1", after=token)
    token = reduce_start(["ffn1_w_gu"], "ffn1_gu")
    packs = _gather_all_wait(send_s, recv_s, pack_buf, token, "gather_all_wait")

    grads, deltas, new_m, new_v = {}, {}, {}, {}

    def reduce_finish(items, after, tag):
        names, mine = [], []
        for grp_names, grp_tag, send, recv, sums, lands in items:
            landed = _reduce_wait(send, recv, sums, lands, after, "reduce_wait_" + grp_tag)
            mine += [_sum_chips(b, "sum_chips_" + k) for b, k in zip(landed, grp_names)]
            names += grp_names
            after = mine[-1]
        theirs = _pair_share(mine, "pair_share_" + tag)
        for k, mi, th in zip(names, mine, theirs):
            outs = _adamw(_as2d(w[k]), mi, th, _as2d(m[k]), _as2d(v[k]), qc, "adamw_" + k)
            grads[k], deltas[k], new_m[k], new_v[k] = (a.reshape(w[k].shape) for a in outs)
        return new_v[names[-1]]

    after = reduce_finish(pending[:3], packs, "early")
    reduce_finish(pending[3:], after, "late")
    names = [k for k, _ in SMALL]
    shape2 = {k: ((1, rows * D) if (k, rows) in REPL else (rows, CS)) for k, rows in SMALL}
    outs = _adamw_small(packs, *[[a[k].reshape(shape2[k]) for k in names] for a in (w, m, v)])
    ns = len(names)
    for i, k in enumerate(names):
        grads[k], deltas[k], new_m[k], new_v[k] = (outs[j * ns + i].reshape(w[k].shape) for j in range(4))

    return (loss, grad_x, *[grads[k] for k in WEIGHTS], *[deltas[k] for k in WEIGHTS],
            *[new_m[k] for k in WEIGHTS], *[new_v[k] for k in WEIGHTS])


def kernel(x, meta_tokens, ffn1_norm, ffn1_w_gu, ffn1_w_down, mix_norm, w_in, b_in, rnn_conv_w, rnn_conv_b, rg_w_a, rg_b_a, rg_w_x, rg_b_x, rg_lambda, rnn_w_proj, conv_dw_w, conv_dw_b, conv_ln_g, conv_ln_b, conv_w_proj, conv_b_proj, w_out, ffn2_norm, ffn2_w_gu, ffn2_w_down, final_norm, loss_target, m_meta_tokens, m_ffn1_norm, m_ffn1_w_gu, m_ffn1_w_down, m_mix_norm, m_w_in, m_b_in, m_rnn_conv_w, m_rnn_conv_b, m_rg_w_a, m_rg_b_a, m_rg_w_x, m_rg_b_x, m_rg_lambda, m_rnn_w_proj, m_conv_dw_w, m_conv_dw_b, m_conv_ln_g, m_conv_ln_b, m_conv_w_proj, m_conv_b_proj, m_w_out, m_ffn2_norm, m_ffn2_w_gu, m_ffn2_w_down, m_final_norm, v_meta_tokens, v_ffn1_norm, v_ffn1_w_gu, v_ffn1_w_down, v_mix_norm, v_w_in, v_b_in, v_rnn_conv_w, v_rnn_conv_b, v_rg_w_a, v_rg_b_a, v_rg_w_x, v_rg_b_x, v_rg_lambda, v_rnn_w_proj, v_conv_dw_w, v_conv_dw_b, v_conv_ln_g, v_conv_ln_b, v_conv_w_proj, v_conv_b_proj, v_w_out, v_ffn2_norm, v_ffn2_w_gu, v_ffn2_w_down, v_final_norm):
    args = locals()
    w = {k: args[k] for k in WEIGHTS}
    m = {k: args["m_" + k] for k in WEIGHTS}
    v = {k: args["v_" + k] for k in WEIGHTS}
    return _step(x, loss_target, w, m, v)
```

```python
import functools

import jax
import jax.numpy as jnp
from jax import lax
from jax.experimental import pallas as pl
from jax.experimental.pallas import tpu as pltpu

f32 = jnp.float32
bf16 = jnp.bfloat16

D = 1024
F = 2816
FS = F // 2
NIN = 6 * D
NMETA = 16
NHEAD = 4
HD = D // NHEAD
KC4 = 4
KC31 = 31
HALO = 32
EPS = 1e-6
TM = 384
NCHIP = 4
MESH = pl.DeviceIdType.MESH

ADAM_LR = 0.001
ADAM_B1 = 0.9
ADAM_B2 = 0.999
ADAM_EPS = 1e-08
ADAM_WD = 0.01
ADAM_STEP = 10

VMEM_LIMIT = 56 * 1024 * 1024
FSUB = [(o, min(256, FS - o)) for o in range(0, FS, 256)]


def _cp(n_axes, **kw):
    return pltpu.CompilerParams(dimension_semantics=("arbitrary",) * n_axes,
                                vmem_limit_bytes=VMEM_LIMIT, **kw)


RESIDENT = pl.BlockSpec(memory_space=pltpu.VMEM)


def _n_after(after):
    return 0 if after is None else (len(after) if isinstance(after, (tuple, list)) else 1)


def _ordered(body, in_specs, args, after):
    if after is None:
        return body, in_specs, args
    extra = tuple(after) if isinstance(after, (tuple, list)) else (after,)
    return (lambda *refs: body(*refs[len(extra):]),
            [pl.BlockSpec(memory_space=pl.ANY)] * len(extra) + list(in_specs), extra + tuple(args))


def _nt_dot(a, b):
    return lax.dot_general(a, b, (((1,), (1,)), ((), ())), preferred_element_type=f32)


def _tn_dot(a, b):
    return lax.dot_general(a, b, (((0,), (0,)), ((), ())), preferred_element_type=f32)


def _sigmoid(x):
    return 0.5 * jnp.tanh(0.5 * x) + 0.5


def _log1p(y):
    u = 1.0 + y
    d = u - 1.0
    return jnp.where(d == 0.0, y, jnp.log(u) * (y / jnp.where(d == 0.0, 1.0, d)))


def _softplus(x):
    return jnp.maximum(x, 0.0) + _log1p(jnp.exp(-jnp.abs(x)))


def _one_minus_square(a, log_a):
    x = 2.0 * log_a
    series = x * (1.0 + x * (0.5 + x * (1.0 / 6.0)))
    return jnp.where(jnp.abs(x) < 0.03, -series, 1.0 - a * a)


_GELU_C = 0.7978845608028654
_GELU_K = 0.044715


def _gelu_and_grad(y):
    y2 = y * y
    th = jnp.tanh(_GELU_C * (y + _GELU_K * y * y2))
    gel = 0.5 * y * (1.0 + th)
    dgel = 0.5 * (1.0 + th) + 0.5 * y * (1.0 - th * th) * _GELU_C * (1.0 + 3.0 * _GELU_K * y2)
    return gel, dgel


def _rms_stats(h):
    return lax.rsqrt(jnp.mean(h * h, axis=-1, keepdims=True) + EPS)


def _rms_bwd(dn, h, g):
    r = _rms_stats(h)
    nhat = h * r
    dnh = dn * g
    dh = r * (dnh - nhat * jnp.mean(dnh * nhat, axis=-1, keepdims=True))
    dg = jnp.sum(dn * nhat, axis=0, keepdims=True)
    return dh, dg


def _row_ids(shape):
    return lax.broadcasted_iota(jnp.int32, shape, 0)


def _ffn_fwd(h, g, wgu, wd, name):
    t = h.shape[0]
    nj = 2

    def body(h_ref, g_ref, wg_ref, wd_ref, ho_ref, gate_ref, up_ref, n_ref, nb_sc, acc_sc, a_sc):
        j = pl.program_id(1)

        @pl.when(j == 0)
        def _():
            hh = h_ref[...]
            nb = (hh * _rms_stats(hh) * g_ref[...]).astype(bf16)
            nb_sc[...] = nb
            n_ref[...] = nb
            acc_sc[...] = jnp.zeros_like(acc_sc)

        nb = nb_sc[...]
        for off, width in FSUB:
            cols = slice(off, off + width)
            gt = jnp.dot(nb, wg_ref[j, :, cols], preferred_element_type=f32)
            up = jnp.dot(nb, wg_ref[2 + j, :, cols], preferred_element_type=f32)
            gate_ref[:, cols] = gt.astype(bf16)
            up_ref[:, cols] = up.astype(bf16)
            a_sc[:, cols] = (gt * _sigmoid(gt) * up).astype(bf16)
        acc_sc[...] += jnp.dot(a_sc[...], wd_ref[j], preferred_element_type=f32)

        @pl.when(j == nj - 1)
        def _():
            ho_ref[...] = h_ref[...] + 0.5 * acc_sc[...]

    tm = TM
    return pl.pallas_call(
        body, name=name, grid=(t // tm, nj),
        in_specs=[
            pl.BlockSpec((tm, D), lambda i, j: (i, 0)),
            pl.BlockSpec((1, D), lambda i, j: (0, 0)),
            RESIDENT, RESIDENT,
        ],
        out_specs=[
            pl.BlockSpec((tm, D), lambda i, j: (i, 0)),
            pl.BlockSpec((tm, FS), lambda i, j: (i, j)),
            pl.BlockSpec((tm, FS), lambda i, j: (i, j)),
            pl.BlockSpec((tm, D), lambda i, j: (i, 0)),
        ],
        out_shape=[
            jax.ShapeDtypeStruct((t, D), f32),
            jax.ShapeDtypeStruct((t, F), bf16),
            jax.ShapeDtypeStruct((t, F), bf16),
            jax.ShapeDtypeStruct((t, D), bf16),
        ],
        scratch_shapes=[pltpu.VMEM((tm, D), bf16), pltpu.VMEM((tm, D), f32), pltpu.VMEM((tm, FS), bf16)],
        compiler_params=_cp(2),
    )(h, g, wgu, wd.reshape(nj, FS, D))


def _inproj_fwd(h, g, win, b_in):
    t = h.shape[0]
    tn = NIN // NCHIP
    nj = NIN // tn
    per = (NIN // NCHIP) // tn

    def body(h_ref, g_ref, w_ref, b_ref, proj_ref, n_ref, nb_sc):
        j = pl.program_id(1)

        @pl.when(j == 0)
        def _():
            hh = h_ref[...]
            nb = (hh * _rms_stats(hh) * g_ref[...]).astype(bf16)
            nb_sc[...] = nb
            n_ref[...] = nb

        proj_ref[...] = jnp.dot(nb_sc[...], w_ref[j], preferred_element_type=f32) + b_ref[...]

    return pl.pallas_call(
        body, name="inproj_fwd", grid=(t // TM, nj),
        in_specs=[
            pl.BlockSpec((TM, D), lambda i, j: (i, 0)),
            pl.BlockSpec((1, D), lambda i, j: (0, 0)),
            RESIDENT,
            pl.BlockSpec((1, tn), lambda i, j: (0, j)),
        ],
        out_specs=[
            pl.BlockSpec((TM, tn), lambda i, j: (i, j)),
            pl.BlockSpec((TM, D), lambda i, j: (i, 0)),
        ],
        out_shape=[jax.ShapeDtypeStruct((t, NIN), f32), jax.ShapeDtypeStruct((t, D), bf16)],
        scratch_shapes=[pltpu.VMEM((TM, D), bf16)],
        compiler_params=_cp(2),
    )(h, g, win, b_in)


def _block_gates(xr, wa_ref, ba, wx_ref, bx, lam):
    xrb = xr.astype(bf16)
    pa = jnp.concatenate([jnp.dot(xrb[:, hh * HD:(hh + 1) * HD], wa_ref[hh], preferred_element_type=f32)
                          for hh in range(NHEAD)], axis=1)
    px = jnp.concatenate([jnp.dot(xrb[:, hh * HD:(hh + 1) * HD], wx_ref[hh], preferred_element_type=f32)
                          for hh in range(NHEAD)], axis=1)
    ra = _sigmoid(pa + ba)
    ii = _sigmoid(px + bx)
    sp = _softplus(-lam)
    log_a = -8.0 * ra * sp
    a = jnp.exp(log_a)
    sq = jnp.sqrt(_one_minus_square(a, log_a))
    return ra, ii, a, sq, sp


def _rnn_fwd(proj, cw, cb, wa, ba, wx, bx, lam):
    t = proj.shape[0]
    ng = TM // 8

    def body(x_ref, y_ref, cw_ref, cb_ref, wa_ref, ba_ref, wx_ref, bx_ref, lam_ref,
             xr_ref, hr_ref, z_ref, xext_sc, carry_sc, a_sc, h_sc):
        i = pl.program_id(0)

        @pl.when(i == 0)
        def _():
            xext_sc[0:8, :] = jnp.zeros((8, D), f32)
            carry_sc[...] = jnp.zeros_like(carry_sc)

        x = x_ref[...]
        xext_sc[8:8 + TM, :] = x
        xe = xext_sc[...]
        xr = cb_ref[...] + cw_ref[KC4 - 1:KC4, :] * x
        for k in range(KC4 - 1):
            xr = xr + cw_ref[k:k + 1, :] * pltpu.roll(xe, KC4 - 1 - k, 0)[8:8 + TM]
        xext_sc[0:8, :] = x[TM - 8:TM]

        _, ii, a, sq, _ = _block_gates(xr, wa_ref, ba_ref[...], wx_ref, bx_ref[...], lam_ref[...])
        a_sc[...] = a
        h_sc[...] = sq * ii * xr
        row = _row_ids((8, D))

        def group(r, carry):
            off = pl.multiple_of(r * 8, 8)
            aa = a_sc[pl.ds(off, 8), :]
            hh = h_sc[pl.ds(off, 8), :]
            for s in (1, 2, 4):
                a_sh = jnp.where(row >= s, pltpu.roll(aa, s, 0), 1.0)
                h_sh = jnp.where(row >= s, pltpu.roll(hh, s, 0), 0.0)
                hh = aa * h_sh + hh
                aa = aa * a_sh
            hh = hh + aa * carry
            h_sc[pl.ds(off, 8), :] = hh
            return hh[7:8, :]

        carry_sc[...] = lax.fori_loop(0, ng, group, carry_sc[...])
        hr = h_sc[...]
        gel, _ = _gelu_and_grad(y_ref[...])
        xr_ref[...] = xr
        hr_ref[...] = hr
        z_ref[...] = (hr * gel).astype(bf16)

    vec = pl.BlockSpec((1, D), lambda i: (0, 0))
    return pl.pallas_call(
        body, name="rnn_fwd", grid=(t // TM,),
        in_specs=[
            pl.BlockSpec((TM, D), lambda i: (i, 0)),
            pl.BlockSpec((TM, D), lambda i: (i, 1)),
            pl.BlockSpec((KC4, D), lambda i: (0, 0)),
            vec,
            pl.BlockSpec((NHEAD, HD, HD), lambda i: (0, 0, 0)),
            vec,
            pl.BlockSpec((NHEAD, HD, HD), lambda i: (0, 0, 0)),
            vec, vec,
        ],
        out_specs=[pl.BlockSpec((TM, D), lambda i: (i, 0))] * 3,
        out_shape=[jax.ShapeDtypeStruct((t, D), f32), jax.ShapeDtypeStruct((t, D), f32),
                   jax.ShapeDtypeStruct((t, D), bf16)],
        scratch_shapes=[pltpu.VMEM((TM + 8, D), f32), pltpu.VMEM((1, D), f32),
                        pltpu.VMEM((TM, D), f32), pltpu.VMEM((TM, D), f32)],
        compiler_params=_cp(1),
    )(proj, proj, cw, cb, wa, ba, wx, bx, lam)


def _ln_stats(vc):
    mu = jnp.mean(vc, axis=-1, keepdims=True)
    xc = vc - mu
    rstd = lax.rsqrt(jnp.mean(xc * xc, axis=-1, keepdims=True) + EPS)
    return xc * rstd, rstd


def _conv_fwd(proj, w31, b31, ln_g, ln_b, after=None):
    t = proj.shape[0]

    def body(gv_ref, gg_ref, w_ref, b_ref, lg_ref, lb_ref, vc_ref, s_ref, vext_sc):
        i = pl.program_id(0)

        @pl.when(i == 0)
        def _():
            vext_sc[0:HALO, :] = jnp.zeros((HALO, D), f32)

        v = gv_ref[...] * _sigmoid(gg_ref[...])
        vext_sc[HALO:HALO + TM, :] = v
        ve = vext_sc[...]
        acc = jnp.zeros((TM, D), f32) + b_ref[...]
        for s in range(8):
            vs = ve if s == 0 else pltpu.roll(ve, s, 0)
            for m in range(HALO // 8):
                k = KC31 - 1 - (8 * m + s)
                if 0 <= k < KC31:
                    acc = acc + w_ref[k:k + 1, :] * vs[HALO - 8 * m:HALO - 8 * m + TM]
        vext_sc[0:HALO, :] = v[TM - HALO:TM]
        xhat, _ = _ln_stats(acc)
        ln = xhat * lg_ref[...] + lb_ref[...]
        vc_ref[...] = acc
        s_ref[...] = (ln * _sigmoid(ln)).astype(bf16)

    vec = pl.BlockSpec((1, D), lambda i: (0, 0))
    body, in_specs, args = _ordered(
        body,
        [pl.BlockSpec((TM, D), lambda i: (i, 2)),
         pl.BlockSpec((TM, D), lambda i: (i, 3)),
         pl.BlockSpec((KC31, D), lambda i: (0, 0)),
         vec, vec, vec],
        (proj, proj, w31, b31, ln_g, ln_b), after)
    return pl.pallas_call(
        body, name="conv_fwd", grid=(t // TM,),
        in_specs=in_specs,
        out_specs=[pl.BlockSpec((TM, D), lambda i: (i, 0))] * 2,
        out_shape=[jax.ShapeDtypeStruct((t, D), f32), jax.ShapeDtypeStruct((t, D), bf16)],
        scratch_shapes=[pltpu.VMEM((TM + HALO, D), f32)],
        compiler_params=_cp(1),
    )(*args)


def _merge_fwd(h, z, s, proj, wrp, wcp, bcp, wout):
    t = h.shape[0]

    def body(h_ref, z_ref, s_ref, ga_ref, gb_ref, wrp_ref, wcp_ref, bcp_ref, wout_ref, ho_ref):
        ya = jnp.dot(z_ref[...], wrp_ref[...], preferred_element_type=f32)
        yb = jnp.dot(s_ref[...], wcp_ref[...], preferred_element_type=f32) + bcp_ref[...]
        merged = _sigmoid(ga_ref[...]) * ya + _sigmoid(gb_ref[...]) * yb
        ho_ref[...] = h_ref[...] + jnp.dot(merged.astype(bf16), wout_ref[...], preferred_element_type=f32)

    row = pl.BlockSpec((TM, D), lambda i: (i, 0))
    wsq = pl.BlockSpec((D, D), lambda i: (0, 0))
    return pl.pallas_call(
        body, name="merge_fwd", grid=(t // TM,),
        in_specs=[row, row, row,
                  pl.BlockSpec((TM, D), lambda i: (i, 4)),
                  pl.BlockSpec((TM, D), lambda i: (i, 5)),
                  wsq, wsq, pl.BlockSpec((1, D), lambda i: (0, 0)), wsq],
        out_specs=row,
        out_shape=jax.ShapeDtypeStruct((t, D), f32),
        compiler_params=_cp(1),
    )(h, z, s, proj, proj, wrp, wcp, bcp, wout)


def _final_loss(h, g, tgt, n_valid):
    t = h.shape[0]

    def body(h_ref, g_ref, t_ref, dh_ref, loss_ref, dg_ref):
        i = pl.program_id(0)

        @pl.when(i == 0)
        def _():
            loss_ref[...] = jnp.zeros_like(loss_ref)
            dg_ref[...] = jnp.zeros_like(dg_ref)

        hh = h_ref[...]
        gg = g_ref[...]
        row = i * TM + _row_ids((TM, 1))
        valid = jnp.logical_and(row >= NMETA, row < n_valid)
        out = hh * _rms_stats(hh) * gg
        err = jnp.where(valid, out - t_ref[...], 0.0)
        loss_ref[...] += 0.5 * jnp.sum(err * err) * (1.0 / D)
        dh, dg = _rms_bwd(err * (1.0 / D), hh, gg)
        dh_ref[...] = dh
        dg_ref[...] += dg

    row_spec = pl.BlockSpec((TM, D), lambda i: (i, 0))
    return pl.pallas_call(
        body, name="final_loss", grid=(t // TM,),
        in_specs=[row_spec, pl.BlockSpec((1, D), lambda i: (0, 0)), row_spec],
        out_specs=[row_spec, pl.BlockSpec((8, 128), lambda i: (0, 0)), pl.BlockSpec((1, D), lambda i: (0, 0))],
        out_shape=[jax.ShapeDtypeStruct((t, D), f32), jax.ShapeDtypeStruct((8, 128), f32),
                   jax.ShapeDtypeStruct((1, D), f32)],
        compiler_params=_cp(1),
    )(h, g, tgt)


def _ffn_bwd(dh, h, g, gate, up, wgu, wd, name, after=None):
    t = h.shape[0]
    nj = 2

    def body(dh_ref, h_ref, g_ref, gate_ref, up_ref, wg_ref, wd_ref,
             dhi_ref, dgate_ref, dup_ref, a_ref, df_ref, dg_ref, dfb_sc, dn_sc):
        i = pl.program_id(0)
        j = pl.program_id(1)

        @pl.when(jnp.logical_and(i == 0, j == 0))
        def _():
            dg_ref[...] = jnp.zeros_like(dg_ref)

        @pl.when(j == 0)
        def _():
            dfb = (0.5 * dh_ref[...]).astype(bf16)
            dfb_sc[...] = dfb
            df_ref[...] = dfb
            dn_sc[...] = jnp.zeros_like(dn_sc)

        dfb = dfb_sc[...]
        for off, width in FSUB:
            cols = slice(off, off + width)
            da = _nt_dot(dfb, wd_ref[j, cols, :])
            gt = gate_ref[:, cols].astype(f32)
            uu = up_ref[:, cols].astype(f32)
            sg = _sigmoid(gt)
            silu = gt * sg
            a_ref[:, cols] = (silu * uu).astype(bf16)
            dgate_ref[:, cols] = (da * uu * (sg * (1.0 + gt * (1.0 - sg)))).astype(bf16)
            dup_ref[:, cols] = (da * silu).astype(bf16)
        dn_sc[...] += _nt_dot(dgate_ref[...], wg_ref[j]) + _nt_dot(dup_ref[...], wg_ref[2 + j])

        @pl.when(j == nj - 1)
        def _():
            dhin, dg = _rms_bwd(dn_sc[...], h_ref[...], g_ref[...])
            dhi_ref[...] = dh_ref[...] + dhin
            dg_ref[...] += dg

    rowd = pl.BlockSpec((TM, D), lambda i, j: (i, 0))
    rowf = pl.BlockSpec((TM, FS), lambda i, j: (i, j))
    vec = pl.BlockSpec((1, D), lambda i, j: (0, 0))
    body, in_specs, args = _ordered(
        body,
        [rowd, rowd, vec, rowf, rowf,
         RESIDENT, RESIDENT],
        (dh, h, g, gate, up, wgu, wd.reshape(nj, FS, D)), after)
    return pl.pallas_call(
        body, name=name, grid=(t // TM, nj),
        in_specs=in_specs,
        out_specs=[rowd, rowf, rowf, rowf, rowd, vec],
        out_shape=[jax.ShapeDtypeStruct((t, D), f32), jax.ShapeDtypeStruct((t, F), bf16),
                   jax.ShapeDtypeStruct((t, F), bf16), jax.ShapeDtypeStruct((t, F), bf16),
                   jax.ShapeDtypeStruct((t, D), bf16), jax.ShapeDtypeStruct((1, D), f32)],
        scratch_shapes=[pltpu.VMEM((TM, D), bf16), pltpu.VMEM((TM, D), f32)],
        compiler_params=_cp(2),
    )(*args)


def _big_tile(t):
    for cand in (2112, 1408, 768, 384):
        if t % cand == 0:
            return cand
    raise ValueError(t)


ANY_SPEC = pl.BlockSpec(memory_space=pl.ANY)


def _tn_matmul(a, b, tk, tn, out_shape, out_block, out_map, name, base=None, after=None):
    t, kk = a.shape
    _, nn = b.shape
    tmm = _big_tile(t)
    nm = t // tmm

    def body(a_ref, b_ref, o_ref, acc_sc):
        m = pl.program_id(2)

        @pl.when(m == 0)
        def _():
            acc_sc[...] = jnp.zeros_like(acc_sc)

        acc_sc[...] += _tn_dot(a_ref[...], b_ref[...])

        @pl.when(m == nm - 1)
        def _():
            o_ref[...] = acc_sc[...].astype(o_ref.dtype)

    in_specs = [pl.BlockSpec((tmm, tk), lambda k, n, m: (m, k)),
                pl.BlockSpec((tmm, tn), lambda k, n, m: (m, n))]
    args, aliases = (a, b), {}
    if base is not None:
        body = (lambda inner: lambda a_ref, b_ref, base_ref, o_ref, acc_sc: inner(a_ref, b_ref, o_ref, acc_sc))(body)
        in_specs, args, aliases = in_specs + [ANY_SPEC], (a, b, base), {2: 0}
    if after is not None:
        body, in_specs, args = _ordered(body, in_specs, args, after)
        aliases = {k + _n_after(after): v for k, v in aliases.items()}
    return pl.pallas_call(
        body, name=name, grid=(kk // tk, nn // tn, nm),
        in_specs=in_specs,
        out_specs=pl.BlockSpec(out_block, out_map),
        out_shape=jax.ShapeDtypeStruct(out_shape, bf16),
        scratch_shapes=[pltpu.VMEM((tk, tn), f32)],
        input_output_aliases=aliases,
        compiler_params=_cp(3),
    )(*args)


def _merge_bwd(dh, z, s, proj, wrp, wcp, bcp, wout, after=None):
    t = dh.shape[0]

    def body(dh_ref, z_ref, s_ref, ga_ref, gb_ref, wrp_ref, wcp_ref, bcp_ref, wout_ref,
             dz_ref, ds_ref, dgab_ref, dhb_ref, mg_ref, dya_ref, dyb_ref, dbcp_ref):
        i = pl.program_id(0)

        @pl.when(i == 0)
        def _():
            dbcp_ref[...] = jnp.zeros_like(dbcp_ref)

        dhb = dh_ref[...].astype(bf16)
        dhb_ref[...] = dhb
        dmg = _nt_dot(dhb, wout_ref[...])
        ya = jnp.dot(z_ref[...], wrp_ref[...], preferred_element_type=f32)
        yb = jnp.dot(s_ref[...], wcp_ref[...], preferred_element_type=f32) + bcp_ref[...]
        sa = _sigmoid(ga_ref[...])
        sb = _sigmoid(gb_ref[...])
        mg_ref[...] = (sa * ya + sb * yb).astype(bf16)
        dgab_ref[:, 0:D] = (dmg * ya * sa * (1.0 - sa)).astype(bf16)
        dgab_ref[:, D:2 * D] = (dmg * yb * sb * (1.0 - sb)).astype(bf16)
        dya = dmg * sa
        dyb = dmg * sb
        dbcp_ref[...] += jnp.sum(dyb, axis=0, keepdims=True)
        dyab = dya.astype(bf16)
        dybb = dyb.astype(bf16)
        dya_ref[...] = dyab
        dyb_ref[...] = dybb
        dz_ref[...] = _nt_dot(dyab, wrp_ref[...])
        ds_ref[...] = _nt_dot(dybb, wcp_ref[...])

    row = pl.BlockSpec((TM, D), lambda i: (i, 0))
    wsq = pl.BlockSpec((D, D), lambda i: (0, 0))
    vec = pl.BlockSpec((1, D), lambda i: (0, 0))
    rowb = jax.ShapeDtypeStruct((t, D), bf16)
    body, in_specs, args = _ordered(
        body,
        [row, row, row,
         pl.BlockSpec((TM, D), lambda i: (i, 4)),
         pl.BlockSpec((TM, D), lambda i: (i, 5)),
         wsq, wsq, vec, wsq],
        (dh, z, s, proj, proj, wrp, wcp, bcp, wout), after)
    return pl.pallas_call(
        body, name="merge_bwd", grid=(t // TM,),
        in_specs=in_specs,
        out_specs=[row, row,
                   pl.BlockSpec((TM, 2 * D), lambda i: (i, 2)),
                   row, row, row, row, vec],
        out_shape=[jax.ShapeDtypeStruct((t, D), f32), jax.ShapeDtypeStruct((t, D), f32),
                   jax.ShapeDtypeStruct((t, NIN), bf16),
                   rowb, rowb, rowb, rowb, jax.ShapeDtypeStruct((1, D), f32)],
        compiler_params=_cp(1),
    )(*args)


def _conv_bwd(ds, vc, proj, dproj, w31, ln_g, ln_b, after=None):
    t = ds.shape[0]
    nt = t // TM
    hb = TM // HALO

    rb = 16
    nb = TM // rb
    taps = [(KC31 - 1 - (8 * m + s), s, m) for s in range(8) for m in range(HALO // 8)
            if 0 <= KC31 - 1 - (8 * m + s) < KC31]

    def groups(a):
        return jnp.sum(a.reshape(rb // 8, 8, D), axis=0)

    def body(ds_ref, vc_ref, gv_ref, gg_ref, gvp_ref, ggp_ref, dpin_ref, w_ref, lg_ref, lb_ref,
             dgvg_ref, dw_ref, db_ref, dlg_ref, dlb_ref, dext_sc, vext_sc, rot_sc, dwacc_sc, small_sc, wb_sc):
        del dpin_ref
        i = pl.program_id(0)
        tile = nt - 1 - i

        @pl.when(i == 0)
        def _():
            dext_sc[TM:TM + HALO, :] = jnp.zeros((HALO, D), f32)
            dwacc_sc[...] = jnp.zeros_like(dwacc_sc)
            small_sc[...] = jnp.zeros_like(small_sc)

        lg = lg_ref[...]
        lb = lb_ref[...]

        xhat, rstd = _ln_stats(vc_ref[...])
        ln = xhat * lg + lb
        sg = _sigmoid(ln)
        dln = ds_ref[...] * (sg * (1.0 + ln * (1.0 - sg)))
        dxh = dln * lg
        dvc = rstd * (dxh - jnp.mean(dxh, axis=-1, keepdims=True)
                      - xhat * jnp.mean(dxh * xhat, axis=-1, keepdims=True))
        small_sc[0] += jnp.sum((dln * xhat).reshape(TM // 8, 8, D), axis=0)
        small_sc[1] += jnp.sum(dln.reshape(TM // 8, 8, D), axis=0)
        small_sc[2] += jnp.sum(dvc.reshape(TM // 8, 8, D), axis=0)
        dext_sc[0:TM, :] = dvc
        vext_sc[HALO:HALO + TM, :] = gv_ref[...] * _sigmoid(gg_ref[...])
        vext_sc[0:HALO, :] = jnp.where(tile > 0, gvp_ref[...] * _sigmoid(ggp_ref[...]), 0.0)

        @pl.when(i == 0)
        def _():
            for k in range(KC31):
                wb_sc[k] = jnp.broadcast_to(w_ref[k:k + 1, :], (8, D))

        for s in range(1, 8):
            rot_sc[s - 1] = pltpu.roll(dext_sc[...], TM + HALO - s, 0)

        def dv_block(b, carry):
            rows = pl.ds(pl.multiple_of(b * rb, rb), rb)
            acc = jnp.zeros((rb, D), f32)
            for k, s, m in taps:
                src = pl.ds(pl.multiple_of(b * rb + 8 * m, 8), rb)
                slab = dext_sc[src, :] if s == 0 else rot_sc[s - 1, src, :]
                acc = acc + (slab.reshape(rb // 8, 8, D) * wb_sc[k]).reshape(rb, D)
            sgg = _sigmoid(gg_ref[rows, :])
            dgvg_ref[rows, 0:D] = (acc * sgg).astype(bf16)
            dgvg_ref[rows, D:2 * D] = (acc * gv_ref[rows, :] * sgg * (1.0 - sgg)).astype(bf16)
            return carry

        lax.fori_loop(0, nb, dv_block, 0)

        for s in range(1, 8):
            rot_sc[s - 1] = pltpu.roll(vext_sc[...], s, 0)
        for first in range(0, len(taps), 3):
            trio = taps[first:first + 3]

            def dw_block(b, accs, trio=trio):
                rows = pl.ds(pl.multiple_of(b * rb, rb), rb)
                dvc_blk = dext_sc[rows, :]
                out = []
                for acc, (k, s, m) in zip(accs, trio):
                    src = pl.ds(pl.multiple_of(b * rb + HALO - 8 * m, 8), rb)
                    slab = vext_sc[src, :] if s == 0 else rot_sc[s - 1, src, :]
                    out.append(acc + groups(dvc_blk * slab))
                return tuple(out)

            sums = lax.fori_loop(0, nb, dw_block, tuple(jnp.zeros((8, D), f32) for _ in trio))
            for acc, (k, s, m) in zip(sums, trio):
                dwacc_sc[k] += acc
        dext_sc[TM:TM + HALO, :] = dext_sc[0:HALO, :]

        @pl.when(i == nt - 1)
        def _():
            for k in range(KC31):
                dw_ref[k:k + 1, :] = jnp.sum(dwacc_sc[k], axis=0, keepdims=True)
            dlg_ref[...] = jnp.sum(small_sc[0], axis=0, keepdims=True)
            dlb_ref[...] = jnp.sum(small_sc[1], axis=0, keepdims=True)
            db_ref[...] = jnp.sum(small_sc[2], axis=0, keepdims=True)

    rev = lambda i: (nt - 1 - i, 0)
    vec = pl.BlockSpec((1, D), lambda i: (0, 0))
    halo_row = lambda i: jnp.maximum((nt - 1 - i) * hb - 1, 0)
    body, in_specs, args = _ordered(
        body,
        [pl.BlockSpec((TM, D), rev),
         pl.BlockSpec((TM, D), rev),
         pl.BlockSpec((TM, D), lambda i: (nt - 1 - i, 2)),
         pl.BlockSpec((TM, D), lambda i: (nt - 1 - i, 3)),
         pl.BlockSpec((HALO, D), lambda i: (halo_row(i), 2)),
         pl.BlockSpec((HALO, D), lambda i: (halo_row(i), 3)),
         pl.BlockSpec(memory_space=pl.ANY),
         pl.BlockSpec((KC31, D), lambda i: (0, 0)),
         vec, vec],
        (ds, vc, proj, proj, proj, proj, dproj, w31, ln_g, ln_b), after)
    return pl.pallas_call(
        body, name="conv_bwd", grid=(nt,),
        in_specs=in_specs,
        out_specs=[
            pl.BlockSpec((TM, 2 * D), lambda i: (nt - 1 - i, 1)),
            pl.BlockSpec((KC31, D), lambda i: (0, 0)),
            vec, vec, vec,
        ],
        out_shape=[jax.ShapeDtypeStruct((t, NIN), bf16),
                   jax.ShapeDtypeStruct((KC31, D), f32),
                   jax.ShapeDtypeStruct((1, D), f32), jax.ShapeDtypeStruct((1, D), f32),
                   jax.ShapeDtypeStruct((1, D), f32)],
        scratch_shapes=[pltpu.VMEM((TM + HALO, D), f32), pltpu.VMEM((TM + HALO, D), f32),
                        pltpu.VMEM((7, TM + HALO, D), f32), pltpu.VMEM((KC31, 8, D), f32),
                        pltpu.VMEM((3, 8, D), f32), pltpu.VMEM((KC31, 8, D), f32)],
        input_output_aliases={6 + _n_after(after): 0},
        compiler_params=_cp(1),
    )(*args)


def _rnn_bwd(dz, xr, hr, proj, dproj, cw, wa, ba, wx, bx, lam):
    t = dz.shape[0]
    nt = t // TM
    ng = TM // 8
    hq = HD // NCHIP

    def body(dz_ref, xr_ref, hr_ref, hrp_ref, x_ref, xp_ref, y_ref, dpin_ref,
             cw_ref, wa_ref, ba_ref, wx_ref, bx_ref, lam_ref,
             dxy_ref, dwa_ref, dwx_ref, dcw_ref, dcb_ref, dba_ref, dbx_ref, dlam_ref,
             anext_sc, gcarry_sc, dext_sc, xext_sc, m_sc, g_sc, dwa_sc, dwx_sc, dsp_sc):
        del dpin_ref
        i = pl.program_id(0)
        tile = nt - 1 - i

        @pl.when(i == 0)
        def _():
            anext_sc[...] = jnp.zeros_like(anext_sc)
            gcarry_sc[...] = jnp.zeros_like(gcarry_sc)
            dext_sc[TM:TM + 8, :] = jnp.zeros((8, D), f32)
            dwa_sc[...] = jnp.zeros_like(dwa_sc)
            dwx_sc[...] = jnp.zeros_like(dwx_sc)
            dsp_sc[...] = jnp.zeros_like(dsp_sc)
            dcw_ref[...] = jnp.zeros_like(dcw_ref)
            dcb_ref[...] = jnp.zeros_like(dcb_ref)
            dba_ref[...] = jnp.zeros_like(dba_ref)
            dbx_ref[...] = jnp.zeros_like(dbx_ref)

        xr = xr_ref[...]
        hr = hr_ref[...]
        dz = dz_ref[...]
        gel, dgel = _gelu_and_grad(y_ref[...])
        dxy_ref[:, D:2 * D] = (dz * hr * dgel).astype(bf16)
        ra, ii, a, sq, sp = _block_gates(xr, wa_ref, ba_ref[...], wx_ref, bx_ref[...], lam_ref[...])

        row = _row_ids((TM, D))
        m_sc[...] = jnp.where(row == TM - 1, anext_sc[...], pltpu.roll(a, TM - 1, 0))
        anext_sc[...] = a[0:1, :]
        g_sc[...] = dz * gel
        row8 = _row_ids((8, D))

        def group(qq, carry):
            off = pl.multiple_of((ng - 1 - qq) * 8, 8)
            mm = m_sc[pl.ds(off, 8), :]
            dd = g_sc[pl.ds(off, 8), :]
            for s in (1, 2, 4):
                m_sh = jnp.where(row8 < 8 - s, pltpu.roll(mm, 8 - s, 0), 1.0)
                d_sh = jnp.where(row8 < 8 - s, pltpu.roll(dd, 8 - s, 0), 0.0)
                dd = dd + mm * d_sh
                mm = mm * m_sh
            dd = dd + mm * carry
            g_sc[pl.ds(off, 8), :] = dd
            return dd[0:1, :]

        gcarry_sc[...] = lax.fori_loop(0, ng, group, gcarry_sc[...])
        gg = g_sc[...]

        hlast = jnp.where(tile > 0, hrp_ref[7:8, :], 0.0)
        hprev = jnp.where(row == 0, hlast, pltpu.roll(hr, 1, 0))
        d_a = gg * hprev
        dsq = gg * ii * xr
        dii = gg * sq * xr
        dxr = gg * sq * ii
        dlog = d_a * a - dsq * (a * a / sq)
        dsp_sc[...] += jnp.sum(dlog * (-8.0 * ra), axis=0, keepdims=True)
        dpa = dlog * (-8.0 * sp) * ra * (1.0 - ra)
        dpx = dii * ii * (1.0 - ii)
        dba_ref[...] += jnp.sum(dpa, axis=0, keepdims=True)
        dbx_ref[...] += jnp.sum(dpx, axis=0, keepdims=True)
        dpab = dpa.astype(bf16)
        dpxb = dpx.astype(bf16)
        xrb = xr.astype(bf16)
        back = []
        for hh in range(NHEAD):
            cols = slice(hh * HD, (hh + 1) * HD)
            back.append(_nt_dot(dpab[:, cols], wa_ref[hh]) + _nt_dot(dpxb[:, cols], wx_ref[hh]))
            dwa_sc[hh] += _tn_dot(xrb[:, cols], dpab[:, cols])
            dwx_sc[hh] += _tn_dot(xrb[:, cols], dpxb[:, cols])
        dxr = dxr + jnp.concatenate(back, axis=1)

        dext_sc[0:TM, :] = dxr
        de = dext_sc[...]
        dx = cw_ref[KC4 - 1:KC4, :] * dxr
        for k in range(KC4 - 1):
            dx = dx + cw_ref[k:k + 1, :] * pltpu.roll(de, TM + 8 - (KC4 - 1 - k), 0)[0:TM]
        dext_sc[TM:TM + 8, :] = dxr[0:8]
        dxy_ref[:, 0:D] = dx.astype(bf16)

        x = x_ref[...]
        xext_sc[0:8, :] = jnp.where(tile > 0, xp_ref[...], 0.0)
        xext_sc[8:8 + TM, :] = x
        xe = xext_sc[...]
        dcw_ref[KC4 - 1:KC4, :] += jnp.sum(dxr * x, axis=0, keepdims=True)
        for k in range(KC4 - 1):
            xs = pltpu.roll(xe, KC4 - 1 - k, 0)[8:8 + TM]
            dcw_ref[k:k + 1, :] += jnp.sum(dxr * xs, axis=0, keepdims=True)
        dcb_ref[...] += jnp.sum(dxr, axis=0, keepdims=True)

        @pl.when(i == nt - 1)
        def _():
            for hh in range(NHEAD):
                for qc in range(NCHIP):
                    dwa_ref[qc, hh] = dwa_sc[hh, qc * hq:(qc + 1) * hq, :].astype(bf16)
                    dwx_ref[qc, hh] = dwx_sc[hh, qc * hq:(qc + 1) * hq, :].astype(bf16)
            dlam_ref[...] = -dsp_sc[...] * _sigmoid(-lam_ref[...])

    rev = lambda i: (nt - 1 - i, 0)
    vec = pl.BlockSpec((1, D), lambda i: (0, 0))
    prev8 = lambda i: jnp.maximum((nt - 1 - i) * ng - 1, 0)
    wblk = pl.BlockSpec((NHEAD, HD, HD), lambda i: (0, 0, 0))
    gblk = pl.BlockSpec((NCHIP, NHEAD, hq, HD), lambda i: (0, 0, 0, 0))
    return pl.pallas_call(
        body, name="rnn_bwd", grid=(nt,),
        in_specs=[
            pl.BlockSpec((TM, D), rev),
            pl.BlockSpec((TM, D), rev),
            pl.BlockSpec((TM, D), rev),
            pl.BlockSpec((8, D), lambda i: (prev8(i), 0)),
            pl.BlockSpec((TM, D), lambda i: (nt - 1 - i, 0)),
            pl.BlockSpec((8, D), lambda i: (prev8(i), 0)),
            pl.BlockSpec((TM, D), lambda i: (nt - 1 - i, 1)),
            pl.BlockSpec(memory_space=pl.ANY),
            pl.BlockSpec((KC4, D), lambda i: (0, 0)),
            wblk, vec, wblk, vec, vec,
        ],
        out_specs=[
            pl.BlockSpec((TM, 2 * D), lambda i: (nt - 1 - i, 0)),
            gblk, gblk,
            pl.BlockSpec((KC4, D), lambda i: (0, 0)),
            vec, vec, vec, vec,
        ],
        out_shape=[jax.ShapeDtypeStruct((t, NIN), bf16),
                   jax.ShapeDtypeStruct((NCHIP, NHEAD, hq, HD), bf16),
                   jax.ShapeDtypeStruct((NCHIP, NHEAD, hq, HD), bf16),
                   jax.ShapeDtypeStruct((KC4, D), f32),
                   jax.ShapeDtypeStruct((1, D), f32), jax.ShapeDtypeStruct((1, D), f32),
                   jax.ShapeDtypeStruct((1, D), f32), jax.ShapeDtypeStruct((1, D), f32)],
        scratch_shapes=[pltpu.VMEM((1, D), f32), pltpu.VMEM((1, D), f32),
                        pltpu.VMEM((TM + 8, D), f32), pltpu.VMEM((TM + 8, D), f32),
                        pltpu.VMEM((TM, D), f32), pltpu.VMEM((TM, D), f32),
                        pltpu.VMEM((NHEAD, HD, HD), f32), pltpu.VMEM((NHEAD, HD, HD), f32),
                        pltpu.VMEM((1, D), f32)],
        input_output_aliases={7: 0},
        compiler_params=_cp(1),
    )(dz, xr, hr, hr, proj, proj, proj, dproj, cw, wa, ba, wx, bx, lam)


def _inproj_bwd(dproj, dh, h, g, win, after=None):
    t = h.shape[0]
    tn = NIN // NCHIP
    nj = NIN // tn
    per = (NIN // NCHIP) // tn

    def body(dp_ref, dh_ref, h_ref, g_ref, w_ref, dhi_ref, dg_ref, db_ref, dn_sc):
        i = pl.program_id(0)
        j = pl.program_id(1)

        @pl.when(jnp.logical_and(i == 0, j == 0))
        def _():
            dg_ref[...] = jnp.zeros_like(dg_ref)
            db_ref[...] = jnp.zeros_like(db_ref)

        @pl.when(j == 0)
        def _():
            dn_sc[...] = jnp.zeros_like(dn_sc)

        dp = dp_ref[...]
        dn_sc[...] += _nt_dot(dp, w_ref[j])
        db_ref[j] += jnp.sum(dp.astype(f32), axis=0, keepdims=True)

        @pl.when(j == nj - 1)
        def _():
            dhin, dg = _rms_bwd(dn_sc[...], h_ref[...], g_ref[...])
            dhi_ref[...] = dh_ref[...] + dhin
            dg_ref[...] += dg

    rowd = pl.BlockSpec((TM, D), lambda i, j: (i, 0))
    vec = pl.BlockSpec((1, D), lambda i, j: (0, 0))
    body, in_specs, args = _ordered(
        body,
        [pl.BlockSpec((TM, tn), lambda i, j: (i, j)), rowd, rowd, vec,
         RESIDENT],
        (dproj, dh, h, g, win), after)
    return pl.pallas_call(
        body, name="inproj_bwd", grid=(t // TM, nj),
        in_specs=in_specs,
        out_specs=[rowd, vec, pl.BlockSpec((nj, 1, tn), lambda i, j: (0, 0, 0))],
        out_shape=[jax.ShapeDtypeStruct((t, D), f32), jax.ShapeDtypeStruct((1, D), f32),
                   jax.ShapeDtypeStruct((nj, 1, tn), f32)],
        scratch_shapes=[pltpu.VMEM((TM, D), f32)],
        compiler_params=_cp(2),
    )(*args)


def _ffn_gu_grad(n, dgate, dup, tag, after=None):
    half = _tn_matmul(n, dgate, D, FS, (NCHIP, D, FS), (None, D, FS), lambda k, nn, m: (nn, 0, 0),
                      tag + "_dwg", after=after)
    return _tn_matmul(n, dup, D, FS, (NCHIP, D, FS), (None, D, FS), lambda k, nn, m: (2 + nn, 0, 0),
                      tag + "_dwu", base=half)


def _ffn_down_grad(a, df, tag, after=None):
    return _tn_matmul(a, df, FS, D, (F, D), (FS, D), lambda k, nn, m: (k, 0), tag + "_dwd", after=after)


def _square_grad(a, b, name):
    return _tn_matmul(a, b, D, D, (D, D), (D, D), lambda k, nn, m: (0, 0), name)


ANY = pl.BlockSpec(memory_space=pl.ANY)


def _place():
    x, y, c = lax.axis_index("x"), lax.axis_index("y"), lax.axis_index("c")
    chips = [(1 - x, y), (x, 1 - y), (1 - x, 1 - y)]
    return x, y, c, chips


def _chip_id(chip):
    return 2 * chip[0] + chip[1]


def _cast_into_slot(w2d, qc, dtype, name, after=None):
    r, cc = w2d.shape
    hr = r // 2

    def body(qc_ref, *refs):
        del qc_ref
        w_ref, o_ref = refs[-2:]
        o_ref[...] = w_ref[...].astype(dtype)

    in_specs, args = [pl.BlockSpec((hr, cc), lambda h, qc_ref: (h, 0))], (w2d,)
    if after is not None:
        in_specs, args = [ANY_SPEC] + in_specs, (after,) + args
    return pl.pallas_call(
        body, name=name,
        grid_spec=pltpu.PrefetchScalarGridSpec(
            num_scalar_prefetch=1, grid=(2,),
            in_specs=in_specs,
            out_specs=pl.BlockSpec((None, None, hr, cc), lambda h, qc_ref: (qc_ref[0], h, 0, 0))),
        out_shape=jax.ShapeDtypeStruct((NCHIP, 2, hr, cc), dtype),
        compiler_params=_cp(1),
    )(qc, *args)


def _place_pack(pack, qc):
    def body(qc_ref, p_ref, o_ref):
        del qc_ref
        o_ref[...] = p_ref[...]

    return pl.pallas_call(
        body, name="place_pack",
        grid_spec=pltpu.PrefetchScalarGridSpec(
            num_scalar_prefetch=1, grid=(1,),
            in_specs=[pl.BlockSpec(pack.shape, lambda i, qc_ref: (0, 0))],
            out_specs=pl.BlockSpec((None,) + pack.shape, lambda i, qc_ref: (2 * qc_ref[0] + qc_ref[1], 0, 0))),
        out_shape=jax.ShapeDtypeStruct((8,) + pack.shape, pack.dtype),
        compiler_params=_cp(1),
    )(qc, pack)


def _gather_shards(bufs):
    n = len(bufs)

    def body(*refs):
        outs = refs[n:2 * n]
        send_sems, recv_sems = refs[2 * n:]
        x, y, c, chips = _place()
        q = 2 * x + y
        sibling = (x, y, 1 - c)

        def remote(a, k, blk, to):
            return pltpu.make_async_remote_copy(src_ref=blk, dst_ref=blk, send_sem=send_sems.at[a, k],
                                                recv_sem=recv_sems.at[a, k], device_id=to, device_id_type=MESH)

        sent = []
        for a in range(n):
            for j, chip in enumerate(chips):
                cp = remote(a, j, outs[a].at[q, c], (chip[0], chip[1], c))
                cp.start()
                sent.append(cp)
        for a in range(n):
            for j, chip in enumerate(chips):
                got = outs[a].at[_chip_id(chip), c]
                remote(a, j, got, (chip[0], chip[1], c)).wait_recv()
                cp = remote(a, 3 + j, got, sibling)
                cp.start()
                sent.append(cp)
        for a in range(n):
            for j, chip in enumerate(chips):
                remote(a, 3 + j, outs[a].at[_chip_id(chip), 1 - c], sibling).wait_recv()
        for cp in sent:
            cp.wait_send()

    return pl.pallas_call(
        body, name="gather_shards",
        in_specs=[ANY] * n, out_specs=[ANY] * n,
        out_shape=[jax.ShapeDtypeStruct(s.shape, s.dtype) for s in bufs],
        scratch_shapes=[pltpu.SemaphoreType.DMA((n, 6)), pltpu.SemaphoreType.DMA((n, 6))],
        input_output_aliases={a: a for a in range(n)},
    )(*bufs)


def _pair_exchange(parts, name):
    n = len(parts)

    def body(*refs):
        ins, outs = refs[:n], refs[n:2 * n]
        send_sems, recv_sems = refs[2 * n:]
        x, y, c, _ = _place()
        copies = []
        for a in range(n):
            cp = pltpu.make_async_remote_copy(
                src_ref=ins[a].at[:, 1 - c], dst_ref=outs[a],
                send_sem=send_sems.at[a], recv_sem=recv_sems.at[a],
                device_id=(x, y, 1 - c), device_id_type=MESH)
            cp.start()
            copies.append(cp)
        for cp in copies:
            cp.wait()

    return pl.pallas_call(
        body, name=name,
        in_specs=[ANY] * n, out_specs=[ANY] * n,
        out_shape=[jax.ShapeDtypeStruct((NCHIP,) + s.shape[2:], s.dtype) for s in parts],
        scratch_shapes=[pltpu.SemaphoreType.DMA((n,)), pltpu.SemaphoreType.DMA((n,))],
    )(*parts)


def _pair_add(part, got, qc, name):
    _, _, hr, cc = part.shape

    def body(qc_ref, p_ref, g_ref, o_ref, land_ref):
        s = pl.program_id(0)
        val = (p_ref[...].astype(f32) + g_ref[...].astype(f32)).astype(bf16)
        o_ref[...] = val

        @pl.when(s == qc_ref[0])
        def _():
            land_ref[...] = val

    return pl.pallas_call(
        body, name=name,
        grid_spec=pltpu.PrefetchScalarGridSpec(
            num_scalar_prefetch=1, grid=(NCHIP,),
            in_specs=[pl.BlockSpec((None, None, hr, cc), lambda s, qc_ref: (s, qc_ref[1], 0, 0)),
                      pl.BlockSpec((None, hr, cc), lambda s, qc_ref: (s, 0, 0))],
            out_specs=[pl.BlockSpec((None, hr, cc), lambda s, qc_ref: (s, 0, 0)),
                       pl.BlockSpec((None, hr, cc), lambda s, qc_ref: (qc_ref[0], 0, 0))]),
        out_shape=[jax.ShapeDtypeStruct((NCHIP, hr, cc), bf16)] * 2,
        compiler_params=_cp(1),
    )(qc, part, got)


def _chip_exchange(sums, lands):
    n = len(sums)

    def body(*refs):
        ins, outs = refs[:n], refs[2 * n:3 * n]
        send_sems, recv_sems = refs[3 * n:]
        x, y, c, chips = _place()
        q = 2 * x + y
        sent = []
        for a in range(n):
            for j, chip in enumerate(chips):
                cp = pltpu.make_async_remote_copy(
                    src_ref=ins[a].at[_chip_id(chip)], dst_ref=outs[a].at[q],
                    send_sem=send_sems.at[a, j], recv_sem=recv_sems.at[a, j],
                    device_id=(chip[0], chip[1], c), device_id_type=MESH)
                cp.start()
                sent.append(cp)
        for a in range(n):
            for j, chip in enumerate(chips):
                got = outs[a].at[_chip_id(chip)]
                pltpu.make_async_remote_copy(
                    src_ref=got, dst_ref=got, send_sem=send_sems.at[a, j], recv_sem=recv_sems.at[a, j],
                    device_id=(chip[0], chip[1], c), device_id_type=MESH).wait_recv()
        for cp in sent:
            cp.wait_send()

    return pl.pallas_call(
        body, name="chip_exchange",
        in_specs=[ANY] * (2 * n), out_specs=[ANY] * n,
        out_shape=[jax.ShapeDtypeStruct(s.shape, s.dtype) for s in lands],
        scratch_shapes=[pltpu.SemaphoreType.DMA((n, 3)), pltpu.SemaphoreType.DMA((n, 3))],
        input_output_aliases={n + a: a for a in range(n)},
    )(*sums, *lands)


def _sum_chips(got, name):
    _, hr, cc = got.shape

    def body(g_ref, o_ref):
        acc = g_ref[0].astype(f32)
        for s in range(1, NCHIP):
            acc = acc + g_ref[s].astype(f32)
        o_ref[...] = acc

    return pl.pallas_call(
        body, name=name, grid=(1,),
        in_specs=[pl.BlockSpec((NCHIP, hr, cc), lambda i: (0, 0, 0))],
        out_specs=pl.BlockSpec((hr, cc), lambda i: (0, 0)),
        out_shape=jax.ShapeDtypeStruct((hr, cc), f32),
        compiler_params=_cp(1),
    )(got)


def _pair_share(halves, name, after=None):
    n = len(halves)
    extra = () if after is None else (after,)

    def body(*refs):
        refs = refs[len(extra):]
        ins, outs = refs[:n], refs[n:2 * n]
        send_sems, recv_sems = refs[2 * n:]
        x, y, c, _ = _place()
        copies = []
        for a in range(n):
            cp = pltpu.make_async_remote_copy(
                src_ref=ins[a], dst_ref=outs[a], send_sem=send_sems.at[a], recv_sem=recv_sems.at[a],
                device_id=(x, y, 1 - c), device_id_type=MESH)
            cp.start()
            copies.append(cp)
        for cp in copies:
            cp.wait()

    return pl.pallas_call(
        body, name=name,
        in_specs=[ANY] * (len(extra) + n), out_specs=[ANY] * n,
        out_shape=[jax.ShapeDtypeStruct(s.shape, s.dtype) for s in halves],
        scratch_shapes=[pltpu.SemaphoreType.DMA((n,)), pltpu.SemaphoreType.DMA((n,))],
    )(*extra, *halves)


def _all_copy(buf_ref, send_ref, recv_ref, k, x, y, c, landing):
    px, py, pc = (1 - x if k & 4 else x, 1 - y if k & 2 else y, 1 - c if k & 1 else c)
    me = 4 * x + 2 * y + c
    there = 4 * px + 2 * py + pc
    return pltpu.make_async_remote_copy(
        src_ref=buf_ref.at[me], dst_ref=buf_ref.at[there if landing else me],
        send_sem=send_ref.at[k - 1], recv_sem=recv_ref.at[k - 1],
        device_id=(px, py, pc), device_id_type=MESH)


def _gather_all_start(buf, name):
    def body(in_ref, send, recv, thru, token):
        del thru
        x, y, c, _ = _place()
        for k in range(1, 8):
            _all_copy(in_ref, send, recv, k, x, y, c, False).start()
        token[...] = jnp.zeros_like(token)

    return pl.pallas_call(
        body, name=name,
        in_specs=[HBM],
        out_specs=[SEM, SEM, HBM, pl.BlockSpec(memory_space=pltpu.VMEM)],
        out_shape=[pltpu.SemaphoreType.DMA((7,)), pltpu.SemaphoreType.DMA((7,)),
                   pltpu.HBM(buf.shape, buf.dtype), jax.ShapeDtypeStruct((8, 128), f32)],
        input_output_aliases={0: 2},
        compiler_params=pltpu.CompilerParams(has_side_effects=EFFECT),
    )(_in_hbm(buf))


def _gather_all_wait(send, recv, buf, after, name):
    def body(in_ref, send_r, recv_r, after_ref, out_ref):
        del after_ref, out_ref
        x, y, c, _ = _place()
        for k in range(1, 8):
            cp = _all_copy(in_ref, send_r, recv_r, k, x, y, c, True)
            cp.wait_send()
            cp.wait_recv()

    return pl.pallas_call(
        body, name=name,
        in_specs=[HBM, SEM, SEM, ANY],
        out_specs=HBM,
        out_shape=pltpu.HBM(buf.shape, buf.dtype),
        input_output_aliases={0: 0},
        compiler_params=pltpu.CompilerParams(has_side_effects=EFFECT),
    )(buf, send, recv, after)


HBM = pl.BlockSpec(memory_space=pltpu.HBM)
SEM = pl.BlockSpec(memory_space=pltpu.SEMAPHORE)
EFFECT = pltpu.SideEffectType.DATAFLOW_SIDE_EFFECTING
N_PEER = 3


def _in_hbm(a):
    return pltpu.with_memory_space_constraint(a, pltpu.HBM)


def _tie(a, token):
    return lax.optimization_barrier((a, token))[0]


def _gather_copy(buf_ref, send_ref, recv_ref, j, chip, q, c, landing_chip):
    return pltpu.make_async_remote_copy(
        src_ref=buf_ref.at[q, c], dst_ref=buf_ref.at[landing_chip, c],
        send_sem=send_ref.at[j], recv_sem=recv_ref.at[j],
        device_id=(chip[0], chip[1], c), device_id_type=MESH)


def _gather_start(bufs, name):
    n = len(bufs)

    def body(*refs):
        ins = refs[:n]
        send, recv = refs[n:2 * n], refs[2 * n:3 * n]
        token = refs[4 * n]
        x, y, c, chips = _place()
        q = 2 * x + y
        for a in range(n):
            for j, chip in enumerate(chips):
                _gather_copy(ins[a], send[a], recv[a], j, chip, q, c, q).start()
        token[...] = jnp.zeros_like(token)

    sems = [pltpu.SemaphoreType.DMA((N_PEER,))] * (2 * n)
    outs = pl.pallas_call(
        body, name=name,
        in_specs=[HBM] * n,
        out_specs=[SEM] * (2 * n) + [HBM] * n + [pl.BlockSpec(memory_space=pltpu.VMEM)],
        out_shape=sems + [pltpu.HBM(b.shape, b.dtype) for b in bufs] + [jax.ShapeDtypeStruct((8, 128), f32)],
        input_output_aliases={a: 2 * n + a for a in range(n)},
        compiler_params=pltpu.CompilerParams(has_side_effects=EFFECT),
    )(*[_in_hbm(b) for b in bufs])
    return list(outs[:n]), list(outs[n:2 * n]), list(outs[2 * n:3 * n]), outs[3 * n]


def _gather_wait(send, recv, bufs, after, name):
    n = len(bufs)

    def body(*refs):
        ins = refs[:n]
        send_r, recv_r = refs[n:2 * n], refs[2 * n:3 * n]
        x, y, c, chips = _place()
        q = 2 * x + y
        for a in range(n):
            for j, chip in enumerate(chips):
                cp = _gather_copy(ins[a], send_r[a], recv_r[a], j, chip, q, c, _chip_id(chip))
                cp.wait_send()
                cp.wait_recv()

    afters = after if isinstance(after, (tuple, list)) else (after,)
    outs = pl.pallas_call(
        body, name=name,
        in_specs=[HBM] * n + [SEM] * (2 * n) + [ANY] * len(afters),
        out_specs=[HBM] * n,
        out_shape=[pltpu.HBM(b.shape, b.dtype) for b in bufs],
        input_output_aliases={a: a for a in range(n)},
        compiler_params=pltpu.CompilerParams(has_side_effects=EFFECT),
    )(*bufs, *send, *recv, *afters)
    return list(outs)


def _forward_halves(bufs, name):
    n = len(bufs)

    def body(*refs):
        outs = refs[n:2 * n]
        send_sems, recv_sems = refs[2 * n:]
        x, y, c, chips = _place()
        sibling = (x, y, 1 - c)

        def remote(a, j, blk):
            return pltpu.make_async_remote_copy(src_ref=blk, dst_ref=blk, send_sem=send_sems.at[a, j],
                                                recv_sem=recv_sems.at[a, j], device_id=sibling,
                                                device_id_type=MESH)

        sent = []
        for a in range(n):
            for j, chip in enumerate(chips):
                cp = remote(a, j, outs[a].at[_chip_id(chip), c])
                cp.start()
                sent.append(cp)
        for a in range(n):
            for j, chip in enumerate(chips):
                remote(a, j, outs[a].at[_chip_id(chip), 1 - c]).wait_recv()
        for cp in sent:
            cp.wait_send()

    return pl.pallas_call(
        body, name=name,
        in_specs=[ANY] * n, out_specs=[ANY] * n,
        out_shape=[jax.ShapeDtypeStruct(s.shape, s.dtype) for s in bufs],
        scratch_shapes=[pltpu.SemaphoreType.DMA((n, N_PEER)), pltpu.SemaphoreType.DMA((n, N_PEER))],
        input_output_aliases={a: a for a in range(n)},
    )(*bufs)


def _reduce_copy(sum_ref, land_ref, send_ref, recv_ref, j, chip, q, c, landing_chip):
    return pltpu.make_async_remote_copy(
        src_ref=sum_ref.at[_chip_id(chip)], dst_ref=land_ref.at[landing_chip],
        send_sem=send_ref.at[j], recv_sem=recv_ref.at[j],
        device_id=(chip[0], chip[1], c), device_id_type=MESH)


def _reduce_start(sums, lands, name):
    n = len(sums)

    def body(*refs):
        s_in, l_in = refs[:n], refs[n:2 * n]
        send, recv = refs[2 * n:3 * n], refs[3 * n:4 * n]
        token = refs[6 * n]
        x, y, c, chips = _place()
        q = 2 * x + y
        for a in range(n):
            for j, chip in enumerate(chips):
                _reduce_copy(s_in[a], l_in[a], send[a], recv[a], j, chip, q, c, q).start()
        token[...] = jnp.zeros_like(token)

    sems = [pltpu.SemaphoreType.DMA((N_PEER,))] * (2 * n)
    outs = pl.pallas_call(
        body, name=name,
        in_specs=[HBM] * (2 * n),
        out_specs=[SEM] * (2 * n) + [HBM] * (2 * n) + [pl.BlockSpec(memory_space=pltpu.VMEM)],
        out_shape=sems + [pltpu.HBM(b.shape, b.dtype) for b in list(sums) + list(lands)]
        + [jax.ShapeDtypeStruct((8, 128), f32)],
        input_output_aliases={a: 2 * n + a for a in range(2 * n)},
        compiler_params=pltpu.CompilerParams(has_side_effects=EFFECT),
    )(*[_in_hbm(b) for b in list(sums) + list(lands)])
    return (list(outs[:n]), list(outs[n:2 * n]), list(outs[2 * n:3 * n]), list(outs[3 * n:4 * n]),
            outs[4 * n])


def _reduce_wait(send, recv, sums, lands, after, name):
    n = len(sums)

    def body(*refs):
        s_in, l_in = refs[:n], refs[n:2 * n]
        send_r, recv_r = refs[2 * n:3 * n], refs[3 * n:4 * n]
        x, y, c, chips = _place()
        q = 2 * x + y
        for a in range(n):
            for j, chip in enumerate(chips):
                cp = _reduce_copy(s_in[a], l_in[a], send_r[a], recv_r[a], j, chip, q, c, _chip_id(chip))
                cp.wait_send()
                cp.wait_recv()

    outs = pl.pallas_call(
        body, name=name,
        in_specs=[HBM] * (2 * n) + [SEM] * (2 * n) + [ANY],
        out_specs=[HBM] * (2 * n),
        out_shape=[pltpu.HBM(b.shape, b.dtype) for b in list(sums) + list(lands)],
        input_output_aliases={a: a for a in range(2 * n)},
        compiler_params=pltpu.CompilerParams(has_side_effects=EFFECT),
    )(*sums, *lands, *send, *recv, after)
    return list(outs[n:])


def _sibling_copy(part_ref, land_ref, send_ref, recv_ref, x, y, c):
    return pltpu.make_async_remote_copy(
        src_ref=part_ref.at[:, 1 - c], dst_ref=land_ref, send_sem=send_ref.at[0], recv_sem=recv_ref.at[0],
        device_id=(x, y, 1 - c), device_id_type=MESH)


def _pair_exchange_start(parts, name):
    n = len(parts)
    lands = [lax.empty((NCHIP,) + p.shape[2:], p.dtype) for p in parts]

    def body(*refs):
        p_in, l_in = refs[:n], refs[n:2 * n]
        send, recv = refs[2 * n:3 * n], refs[3 * n:4 * n]
        token = refs[6 * n]
        x, y, c, _ = _place()
        for a in range(n):
            _sibling_copy(p_in[a], l_in[a], send[a], recv[a], x, y, c).start()
        token[...] = jnp.zeros_like(token)

    sems = [pltpu.SemaphoreType.DMA((1,))] * (2 * n)
    outs = pl.pallas_call(
        body, name=name,
        in_specs=[HBM] * (2 * n),
        out_specs=[SEM] * (2 * n) + [HBM] * (2 * n) + [pl.BlockSpec(memory_space=pltpu.VMEM)],
        out_shape=sems + [pltpu.HBM(b.shape, b.dtype) for b in list(parts) + lands]
        + [jax.ShapeDtypeStruct((8, 128), f32)],
        input_output_aliases={a: 2 * n + a for a in range(2 * n)},
        compiler_params=pltpu.CompilerParams(has_side_effects=EFFECT),
    )(*[_in_hbm(b) for b in list(parts) + lands])
    return (list(outs[:n]), list(outs[n:2 * n]), list(outs[2 * n:3 * n]), list(outs[3 * n:4 * n]),
            outs[4 * n])


def _pair_exchange_wait(send, recv, parts, lands, after, name):
    n = len(parts)

    def body(*refs):
        p_in, l_in = refs[:n], refs[n:2 * n]
        send_r, recv_r = refs[2 * n:3 * n], refs[3 * n:4 * n]
        x, y, c, _ = _place()
        for a in range(n):
            cp = _sibling_copy(p_in[a], l_in[a], send_r[a], recv_r[a], x, y, c)
            cp.wait_send()
            cp.wait_recv()

    outs = pl.pallas_call(
        body, name=name,
        in_specs=[HBM] * (2 * n) + [SEM] * (2 * n) + [ANY],
        out_specs=[HBM] * (2 * n),
        out_shape=[pltpu.HBM(b.shape, b.dtype) for b in list(parts) + list(lands)],
        input_output_aliases={a: a for a in range(2 * n)},
        compiler_params=pltpu.CompilerParams(has_side_effects=EFFECT),
    )(*parts, *lands, *send, *recv, after)
    return list(outs[:n]), list(outs[n:])


def _forward_copy(buf_ref, send_ref, recv_ref, j, chip, x, y, c, landing):
    return pltpu.make_async_remote_copy(
        src_ref=buf_ref.at[_chip_id(chip), c], dst_ref=buf_ref.at[_chip_id(chip), 1 - c if landing else c],
        send_sem=send_ref.at[j], recv_sem=recv_ref.at[j], device_id=(x, y, 1 - c), device_id_type=MESH)


def _forward_start(bufs, name):
    n = len(bufs)

    def body(*refs):
        ins = refs[:n]
        send, recv = refs[n:2 * n], refs[2 * n:3 * n]
        token = refs[4 * n]
        x, y, c, chips = _place()
        for a in range(n):
            for j, chip in enumerate(chips):
                _forward_copy(ins[a], send[a], recv[a], j, chip, x, y, c, False).start()
        token[...] = jnp.zeros_like(token)

    sems = [pltpu.SemaphoreType.DMA((N_PEER,))] * (2 * n)
    outs = pl.pallas_call(
        body, name=name,
        in_specs=[HBM] * n,
        out_specs=[SEM] * (2 * n) + [HBM] * n + [pl.BlockSpec(memory_space=pltpu.VMEM)],
        out_shape=sems + [pltpu.HBM(b.shape, b.dtype) for b in bufs] + [jax.ShapeDtypeStruct((8, 128), f32)],
        input_output_aliases={a: 2 * n + a for a in range(n)},
        compiler_params=pltpu.CompilerParams(has_side_effects=EFFECT),
    )(*[_in_hbm(b) for b in bufs])
    return list(outs[:n]), list(outs[n:2 * n]), list(outs[2 * n:3 * n]), outs[3 * n]


def _forward_wait(send, recv, bufs, after, name):
    n = len(bufs)

    def body(*refs):
        ins = refs[:n]
        send_r, recv_r = refs[n:2 * n], refs[2 * n:3 * n]
        x, y, c, chips = _place()
        for a in range(n):
            for j, chip in enumerate(chips):
                cp = _forward_copy(ins[a], send_r[a], recv_r[a], j, chip, x, y, c, True)
                cp.wait_send()
                cp.wait_recv()

    outs = pl.pallas_call(
        body, name=name,
        in_specs=[HBM] * n + [SEM] * (2 * n) + [ANY],
        out_specs=[HBM] * n,
        out_shape=[pltpu.HBM(b.shape, b.dtype) for b in bufs],
        input_output_aliases={a: a for a in range(n)},
        compiler_params=pltpu.CompilerParams(has_side_effects=EFFECT),
    )(*bufs, *send, *recv, after)
    return list(outs)


def _adamw_math(w, g, m, v):
    m = ADAM_B1 * m + (1.0 - ADAM_B1) * g
    v = ADAM_B2 * v + (1.0 - ADAM_B2) * (g * g)
    m_hat = m / (1.0 - ADAM_B1 ** ADAM_STEP)
    v_hat = v / (1.0 - ADAM_B2 ** ADAM_STEP)
    delta = -ADAM_LR * (m_hat / (jnp.sqrt(v_hat) + ADAM_EPS) + ADAM_WD * w)
    return delta, m, v


def _adamw(w, mine, theirs, m, v, qc, name):
    r, cc = w.shape
    hr = r // 2
    tr = next(hr // k for k in range(1, hr + 1)
              if hr % k == 0 and (hr // k) % 8 == 0 and (hr // k) * cc * 4 <= (1 << 20))
    nb = hr // tr

    def body(qc_ref, w_ref, a_ref, b_ref, m_ref, v_ref, g_ref, d_ref, mo_ref, vo_ref):
        g = jnp.where(pl.program_id(0) == qc_ref[1], a_ref[...], b_ref[...])
        g_ref[...] = g
        d_ref[...], mo_ref[...], vo_ref[...] = _adamw_math(w_ref[...], g, m_ref[...], v_ref[...])

    full = pl.BlockSpec((tr, cc), lambda h, i, qc_ref: (h * nb + i, 0))
    half = pl.BlockSpec((tr, cc), lambda h, i, qc_ref: (i, 0))
    return pl.pallas_call(
        body, name=name,
        grid_spec=pltpu.PrefetchScalarGridSpec(
            num_scalar_prefetch=1, grid=(2, nb),
            in_specs=[full, half, half, full, full], out_specs=[full] * 4),
        out_shape=[jax.ShapeDtypeStruct((r, cc), f32)] * 4,
        compiler_params=_cp(2),
    )(qc, w, mine, theirs, m, v)


REPL = [("ffn1_norm", 1), ("mix_norm", 1), ("b_in", 6), ("rnn_conv_b", 1), ("rg_b_a", 1), ("rg_b_x", 1),
        ("rg_lambda", 1), ("conv_dw_b", 1), ("conv_ln_g", 1), ("conv_ln_b", 1), ("conv_b_proj", 1),
        ("ffn2_norm", 1), ("final_norm", 1)]
COLSH = [("meta_tokens", NMETA), ("rnn_conv_w", KC4), ("conv_dw_w", KC31)]
SMALL = REPL + COLSH
CS = D // NCHIP


def _pack_rows():
    starts, row = {}, 0
    for k, rows in REPL:
        starts[k] = row
        row += rows
    for k, rows in COLSH:
        row = -(-row // 8) * 8
        starts[k] = row
        row += rows
    return starts, -(-row // 8) * 8


PACK_START, SMALL_ROWS = _pack_rows()


def _small_pack(g):
    pieces, row = [], 0
    for k, rows in SMALL:
        if PACK_START[k] > row:
            pieces.append(jnp.zeros((PACK_START[k] - row, D), f32))
        pieces.append(g[k].reshape(rows, D))
        row = PACK_START[k] + rows
    pieces.append(jnp.zeros((SMALL_ROWS - row, D), f32))
    return jnp.concatenate(pieces, axis=0)


def _adamw_small(packs, ws, ms, vs):
    ns = len(SMALL)

    def body(*refs):
        pack_ref = refs[0]
        w_refs, m_refs, v_refs = refs[1:1 + ns], refs[1 + ns:1 + 2 * ns], refs[1 + 2 * ns:1 + 3 * ns]
        outs = refs[1 + 3 * ns:1 + 7 * ns]
        g_refs, d_refs, mo_refs, vo_refs = outs[:ns], outs[ns:2 * ns], outs[2 * ns:3 * ns], outs[3 * ns:]
        gsum_sc = refs[1 + 7 * ns]
        q = 2 * lax.axis_index("x") + lax.axis_index("y")
        acc = pack_ref[0]
        for dev in range(1, 8):
            acc = acc + pack_ref[dev]
        gsum_sc[...] = acc
        for idx, (name, rows) in enumerate(SMALL):
            row = PACK_START[name]
            if idx < len(REPL):
                for k in range(rows):
                    cols = slice(k * D, (k + 1) * D)
                    g = gsum_sc[row + k:row + k + 1, :]
                    d, mm, vv = _adamw_math(w_refs[idx][:, cols], g, m_refs[idx][:, cols], v_refs[idx][:, cols])
                    g_refs[idx][:, cols] = g
                    d_refs[idx][:, cols] = d
                    mo_refs[idx][:, cols] = mm
                    vo_refs[idx][:, cols] = vv
            else:
                g = gsum_sc[row:row + rows, pl.ds(pl.multiple_of(q * CS, CS), CS)]
                d, mm, vv = _adamw_math(w_refs[idx][...], g, m_refs[idx][...], v_refs[idx][...])
                g_refs[idx][...] = g
                d_refs[idx][...] = d
                mo_refs[idx][...] = mm
                vo_refs[idx][...] = vv

    shapes = [jax.ShapeDtypeStruct(w.shape, f32) for w in ws]
    return pl.pallas_call(
        body, name="adamw_small",
        out_shape=shapes * 4,
        scratch_shapes=[pltpu.VMEM((SMALL_ROWS, D), f32)],
        compiler_params=pltpu.CompilerParams(vmem_limit_bytes=VMEM_LIMIT),
    )(packs, *ws, *ms, *vs)


BIG = ["ffn1_w_gu", "ffn1_w_down", "w_in", "rg_w_a", "rg_w_x", "rnn_w_proj", "conv_w_proj", "w_out",
       "ffn2_w_gu", "ffn2_w_down"]
WEIGHTS = ['meta_tokens', 'ffn1_norm', 'ffn1_w_gu', 'ffn1_w_down', 'mix_norm', 'w_in', 'b_in', 'rnn_conv_w',
           'rnn_conv_b', 'rg_w_a', 'rg_b_a', 'rg_w_x', 'rg_b_x', 'rg_lambda', 'rnn_w_proj', 'conv_dw_w',
           'conv_dw_b', 'conv_ln_g', 'conv_ln_b', 'conv_w_proj', 'conv_b_proj', 'w_out', 'ffn2_norm',
           'ffn2_w_gu', 'ffn2_w_down', 'final_norm']


def _as2d(a):
    return a.reshape(-1, a.shape[-1])


def _step(x, loss_target, w, m, v):
    seq = x.shape[1]
    n_valid = NMETA + seq
    t = -(-n_valid // TM) * TM

    qc = jnp.stack([2 * lax.axis_index("x") + lax.axis_index("y"), lax.axis_index("c")]).astype(jnp.int32)
    p = {k: w[k].reshape(1, rows * D) for k, rows in REPL}

    first = ["ffn1_w_gu", "ffn1_w_down", "small"]
    later = [["w_in"], ["rg_w_a", "rg_w_x", "rnn_w_proj", "conv_w_proj", "w_out"], ["ffn2_w_gu", "ffn2_w_down"]]
    small_rows = sum(r for _, r in COLSH)
    small = jnp.concatenate([_as2d(w[k]) for k, _ in COLSH] + [jnp.zeros((64 - small_rows, CS), f32)], axis=0)

    def cast(k, token=None):
        src, dtype = (small, f32) if k == "small" else (_as2d(w[k]), bf16)
        return _cast_into_slot(src, qc, dtype, "cast_" + k, after=token)

    send1, recv1, bufs1, token1 = _gather_start([cast(k) for k in first], "gather_start_first")
    rest = [k for grp in later for k in grp]
    send2, recv2, bufs2, token2 = _gather_start([cast(k, token1) for k in rest], "gather_start_rest")

    def install(names, done):
        for k, b in zip(names, done):
            full = b.reshape(NCHIP, 2 * b.shape[2], b.shape[3])
            if k in ("ffn1_w_down", "ffn2_w_down"):
                full = full.reshape(F, D)
            elif k in ("rnn_w_proj", "conv_w_proj", "w_out"):
                full = full.reshape(D, D)
            elif k in ("rg_w_a", "rg_w_x"):
                full = full.reshape(NCHIP, NHEAD, HD // NCHIP, HD).transpose(1, 0, 2, 3).reshape(NHEAD, HD, HD)
            p[k] = full

    def finish(names, send, recv, bufs, after, tag):
        install(names, _forward_halves(_gather_wait(send, recv, bufs, after, "gather_wait_" + tag),
                                       "gather_forward_" + tag))

    def group(names):
        idx = [rest.index(k) for k in names]
        return names, [send2[i] for i in idx], [recv2[i] for i in idx], [bufs2[i] for i in idx]

    h0 = jnp.pad(x[0], ((NMETA, t - n_valid), (0, 0)))
    tgt = jnp.pad(loss_target[0], ((NMETA, t - n_valid), (0, 0)))
    finish(first, send1, recv1, bufs1, (token2, h0, tgt), "first")
    small_full = p.pop("small").transpose(1, 0, 2).reshape(64, D)
    row = 0
    for k, rows in COLSH:
        p[k] = small_full[row:row + rows]
        row += rows

    h0 = lax.dynamic_update_slice(h0, p["meta_tokens"], (0, 0))
    h1, gate1, up1, n1 = _ffn_fwd(h0, p["ffn1_norm"], p["ffn1_w_gu"], p["ffn1_w_down"], "ffn1_fwd")
    finish(*group(later[0]), h1, "in")
    proj, n2 = _inproj_fwd(h1, p["mix_norm"], p["w_in"], p["b_in"])
    names_l = later[1] + later[2]
    _, send_l, recv_l, bufs_l = group(names_l)
    send_f, recv_f, bufs_f, token = _forward_start(
        _gather_wait(send_l, recv_l, bufs_l, proj, "gather_wait_late"), "gather_forward_start")
    vc, s = _conv_fwd(proj, p["conv_dw_w"], p["conv_dw_b"], p["conv_ln_g"], p["conv_ln_b"], after=token)
    install(names_l, _forward_wait(send_f, recv_f, bufs_f, vc, "gather_forward_wait"))
    xr, hr, z = _rnn_fwd(proj, p["rnn_conv_w"], p["rnn_conv_b"], p["rg_w_a"], p["rg_b_a"],
                         p["rg_w_x"], p["rg_b_x"], p["rg_lambda"])
    h2 = _merge_fwd(h1, z, s, proj, p["rnn_w_proj"], p["conv_w_proj"], p["conv_b_proj"], p["w_out"])
    h3, gate2, up2, n3 = _ffn_fwd(h2, p["ffn2_norm"], p["ffn2_w_gu"], p["ffn2_w_down"], "ffn2_fwd")
    dh3, loss_blk, d_final = _final_loss(h3, p["final_norm"], tgt, n_valid)
    loss = lax.psum(loss_blk[0, 0], ("x", "y", "c"))

    g = {"final_norm": d_final}
    pending = []

    def exchange_start(names, tag):
        parts = []
        for k in names:
            rows = g[k].size // (NCHIP * g[k].shape[-1])
            parts.append(g[k].reshape((NCHIP, 2, rows // 2, g[k].shape[-1])))
        send, recv, parts, lands, token = _pair_exchange_start(parts, "pair_exchange_start_" + tag)
        return (names, tag, send, recv, parts, lands), token

    def reduce_start(state, after):
        names, tag, send, recv, parts, lands = state
        parts, from_sibling = _pair_exchange_wait(send, recv, parts, lands, after, "pair_exchange_wait_" + tag)
        added = [_pair_add(pp, gg, qc, "pair_add_" + k) for pp, gg, k in zip(parts, from_sibling, names)]
        send, recv, sums, lands, token = _reduce_start([a for a, _ in added], [b for _, b in added],
                                                       "reduce_start_" + tag)
        pending.append((names, tag, send, recv, sums, lands))
        return token

    dh2, dgate2, dup2, a2, df2, g["ffn2_norm"] = _ffn_bwd(
        dh3, h2, p["ffn2_norm"], gate2, up2, p["ffn2_w_gu"], p["ffn2_w_down"], "ffn2_bwd")
    g["ffn2_w_gu"] = _ffn_gu_grad(n3, dgate2, dup2, "ffn2")
    g["ffn2_w_down"] = _ffn_down_grad(a2, df2, "ffn2")
    state, token = exchange_start(["ffn2_w_gu", "ffn2_w_down"], "ffn2")

    dz, ds, dproj, dh2b, merged, dya, dyb, g["conv_b_proj"] = _merge_bwd(
        dh2, z, s, proj, p["rnn_w_proj"], p["conv_w_proj"], p["conv_b_proj"], p["w_out"], after=token)
    token = reduce_start(state, dz)
    dproj, g["conv_dw_w"], g["conv_dw_b"], g["conv_ln_g"], g["conv_ln_b"] = _conv_bwd(
        ds, vc, proj, dproj, p["conv_dw_w"], p["conv_ln_g"], p["conv_ln_b"], after=token)
    g["w_out"] = _square_grad(merged, dh2b, "dw_out")
    g["rnn_w_proj"] = _square_grad(z, dya, "dw_rnn_proj")
    g["conv_w_proj"] = _square_grad(s, dyb, "dw_conv_proj")
    (dproj, g["rg_w_a"], g["rg_w_x"], g["rnn_conv_w"], g["rnn_conv_b"], g["rg_b_a"], g["rg_b_x"],
     g["rg_lambda"]) = _rnn_bwd(dz, xr, hr, proj, dproj, p["rnn_conv_w"], p["rg_w_a"], p["rg_b_a"],
                                p["rg_w_x"], p["rg_b_x"], p["rg_lambda"])
    state, token = exchange_start(["w_out", "rnn_w_proj", "conv_w_proj", "rg_w_a", "rg_w_x"], "mix")

    dh1, g["mix_norm"], db_in = _inproj_bwd(dproj, dh2, h1, p["mix_norm"], p["w_in"], after=token)
    g["b_in"] = db_in.reshape(1, NIN)
    token = reduce_start(state, dh1)
    g["w_in"] = _tn_matmul(n2, dproj, D, NIN // NCHIP, (NCHIP, D, NIN // NCHIP),
                           (None, D, NIN // NCHIP), lambda k, nn, mm: (nn, 0, 0), "dw_in", after=token)
    state, token = exchange_start(["w_in"], "in")

    dh0, dgate1, dup1, a1, df1, g["ffn1_norm"] = _ffn_bwd(
        dh1, h0, p["ffn1_norm"], gate1, up1, p["ffn1_w_gu"], p["ffn1_w_down"], "ffn1_bwd", after=token)
    g["meta_tokens"] = dh0[0:NMETA]
    grad_x = dh0[NMETA:n_valid][None]
    token = reduce_start(state, dh0)

    send_s, recv_s, pack_buf, token_s = _gather_all_start(_place_pack(_small_pack(g), qc), "gather_all_start")
    g["ffn1_w_down"] = _ffn_down_grad(a1, df1, "ffn1", after=(token, token_s))
    state, token = exchange_start(["ffn1_w_down"], "ffn1_down")
    g["ffn1_w_gu"] = _ffn_gu_grad(n1, dgate1, dup1, "ffn1", after=token)
    token = reduce_start(state, g["ffn1_w_gu"])
    state_gu, token = exchange_start(["ffn1_w_gu"], "ffn1_gu")
    packs = _gather_all_wait(send_s, recv_s, pack_buf, token, "gather_all_wait")

    grads, deltas, new_m, new_v = {}, {}, {}, {}

    def landed_sums(items, after):
        names, mine = [], []
        for grp_names, grp_tag, send, recv, sums, lands in items:
            landed = _reduce_wait(send, recv, sums, lands, after, "reduce_wait_" + grp_tag)
            mine += [_sum_chips(b, "sum_chips_" + k) for b, k in zip(landed, grp_names)]
            names += grp_names
            after = mine[-1]
        return names, mine

    def share_and_update(names, mine, tag, after=None):
        theirs = _pair_share(mine, "pair_share_" + tag, after=after)
        for k, mi, th in zip(names, mine, theirs):
            outs = _adamw(_as2d(w[k]), mi, th, _as2d(m[k]), _as2d(v[k]), qc, "adamw_" + k)
            grads[k], deltas[k], new_m[k], new_v[k] = (a.reshape(w[k].shape) for a in outs)
        return new_v[names[-1]]

    early_names, early_mine = landed_sums(pending[:3], packs)
    token = reduce_start(state_gu, early_mine[-1])
    after = share_and_update(early_names, early_mine, "early", after=token)
    share_and_update(*landed_sums(pending[3:], after), "late")
    names = [k for k, _ in SMALL]
    shape2 = {k: ((1, rows * D) if (k, rows) in REPL else (rows, CS)) for k, rows in SMALL}
    outs = _adamw_small(packs, *[[a[k].reshape(shape2[k]) for k in names] for a in (w, m, v)])
    ns = len(names)
    for i, k in enumerate(names):
        grads[k], deltas[k], new_m[k], new_v[k] = (outs[j * ns + i].reshape(w[k].shape) for j in range(4))

    return (loss, grad_x, *[grads[k] for k in WEIGHTS], *[deltas[k] for k in WEIGHTS],
            *[new_m[k] for k in WEIGHTS], *[new_v[k] for k in WEIGHTS])


def kernel(x, meta_tokens, ffn1_norm, ffn1_w_gu, ffn1_w_down, mix_norm, w_in, b_in, rnn_conv_w, rnn_conv_b, rg_w_a, rg_b_a, rg_w_x, rg_b_x, rg_lambda, rnn_w_proj, conv_dw_w, conv_dw_b, conv_ln_g, conv_ln_b, conv_w_proj, conv_b_proj, w_out, ffn2_norm, ffn2_w_gu, ffn2_w_down, final_norm, loss_target, m_meta_tokens, m_ffn1_norm, m_ffn1_w_gu, m_ffn1_w_down, m_mix_norm, m_w_in, m_b_in, m_rnn_conv_w, m_rnn_conv_b, m_rg_w_a, m_rg_b_a, m_rg_w_x, m_rg_b_x, m_rg_lambda, m_rnn_w_proj, m_conv_dw_w, m_conv_dw_b, m_conv_ln_g, m_conv_ln_b, m_conv_w_proj, m_conv_b_proj, m_w_out, m_ffn2_norm, m_ffn2_w_gu, m_ffn2_w_down, m_final_norm, v_meta_tokens, v_ffn1_norm, v_ffn1_w_gu, v_ffn1_w_down, v_mix_norm, v_w_in, v_b_in, v_rnn_conv_w, v_rnn_conv_b, v_rg_w_a, v_rg_b_a, v_rg_w_x, v_rg_b_x, v_rg_lambda, v_rnn_w_proj, v_conv_dw_w, v_conv_dw_b, v_conv_ln_g, v_conv_ln_b, v_conv_w_proj, v_conv_b_proj, v_w_out, v_ffn2_norm, v_ffn2_w_gu, v_ffn2_w_down, v_final_norm):
    args = locals()
    w = {k: args[k] for k in WEIGHTS}
    m = {k: args["m_" + k] for k in WEIGHTS}
    v = {k: args["v_" + k] for k in WEIGHTS}
    return _step(x, loss_target, w, m, v)
```

```python
import functools

import jax
import jax.numpy as jnp
from jax import lax
from jax.experimental import pallas as pl
from jax.experimental.pallas import tpu as pltpu

f32 = jnp.float32
bf16 = jnp.bfloat16

D = 1024
F = 2816
FS = F // 2
NIN = 6 * D
NMETA = 16
NHEAD = 4
HD = D // NHEAD
KC4 = 4
KC31 = 31
HALO = 32
EPS = 1e-6
TM = 416
NCHIP = 4
MESH = pl.DeviceIdType.MESH

ADAM_LR = 0.001
ADAM_B1 = 0.9
ADAM_B2 = 0.999
ADAM_EPS = 1e-08
ADAM_WD = 0.01
ADAM_STEP = 10

VMEM_LIMIT = 56 * 1024 * 1024
FSUB = [(o, min(256, FS - o)) for o in range(0, FS, 256)]


def _cp(n_axes, **kw):
    return pltpu.CompilerParams(dimension_semantics=("arbitrary",) * n_axes,
                                vmem_limit_bytes=VMEM_LIMIT, **kw)


RESIDENT = pl.BlockSpec(memory_space=pltpu.VMEM)


def _n_after(after):
    return 0 if after is None else (len(after) if isinstance(after, (tuple, list)) else 1)


def _ordered(body, in_specs, args, after):
    if after is None:
        return body, in_specs, args
    extra = tuple(after) if isinstance(after, (tuple, list)) else (after,)
    return (lambda *refs: body(*refs[len(extra):]),
            [pl.BlockSpec(memory_space=pl.ANY)] * len(extra) + list(in_specs), extra + tuple(args))


def _nt_dot(a, b):
    return lax.dot_general(a, b, (((1,), (1,)), ((), ())), preferred_element_type=f32)


def _tn_dot(a, b):
    return lax.dot_general(a, b, (((0,), (0,)), ((), ())), preferred_element_type=f32)


def _sigmoid(x):
    return 0.5 * jnp.tanh(0.5 * x) + 0.5


def _log1p(y):
    u = 1.0 + y
    d = u - 1.0
    return jnp.where(d == 0.0, y, jnp.log(u) * (y / jnp.where(d == 0.0, 1.0, d)))


def _softplus(x):
    return jnp.maximum(x, 0.0) + _log1p(jnp.exp(-jnp.abs(x)))


def _one_minus_square(a, log_a):
    x = 2.0 * log_a
    series = x * (1.0 + x * (0.5 + x * (1.0 / 6.0)))
    return jnp.where(jnp.abs(x) < 0.03, -series, 1.0 - a * a)


_GELU_C = 0.7978845608028654
_GELU_K = 0.044715


def _gelu_and_grad(y):
    y2 = y * y
    th = jnp.tanh(_GELU_C * (y + _GELU_K * y * y2))
    gel = 0.5 * y * (1.0 + th)
    dgel = 0.5 * (1.0 + th) + 0.5 * y * (1.0 - th * th) * _GELU_C * (1.0 + 3.0 * _GELU_K * y2)
    return gel, dgel


def _rms_stats(h):
    return lax.rsqrt(jnp.mean(h * h, axis=-1, keepdims=True) + EPS)


def _rms_bwd(dn, h, g):
    r = _rms_stats(h)
    nhat = h * r
    dnh = dn * g
    dh = r * (dnh - nhat * jnp.mean(dnh * nhat, axis=-1, keepdims=True))
    dg = jnp.sum(dn * nhat, axis=0, keepdims=True)
    return dh, dg


def _row_ids(shape):
    return lax.broadcasted_iota(jnp.int32, shape, 0)


def _ffn_fwd(h, g, wgu, wd, name):
    t = h.shape[0]
    nj = 2

    def body(h_ref, g_ref, wg_ref, wd_ref, ho_ref, gate_ref, up_ref, n_ref, nb_sc, acc_sc, a_sc):
        j = pl.program_id(1)

        @pl.when(j == 0)
        def _():
            hh = h_ref[...]
            nb = (hh * _rms_stats(hh) * g_ref[...]).astype(bf16)
            nb_sc[...] = nb
            n_ref[...] = nb
            acc_sc[...] = jnp.zeros_like(acc_sc)

        nb = nb_sc[...]
        for off, width in FSUB:
            cols = slice(off, off + width)
            gt = jnp.dot(nb, wg_ref[j, :, cols], preferred_element_type=f32)
            up = jnp.dot(nb, wg_ref[2 + j, :, cols], preferred_element_type=f32)
            gate_ref[:, cols] = gt.astype(bf16)
            up_ref[:, cols] = up.astype(bf16)
            a_sc[:, cols] = (gt * _sigmoid(gt) * up).astype(bf16)
        acc_sc[...] += jnp.dot(a_sc[...], wd_ref[j], preferred_element_type=f32)

        @pl.when(j == nj - 1)
        def _():
            ho_ref[...] = h_ref[...] + 0.5 * acc_sc[...]

    tm = TM
    return pl.pallas_call(
        body, name=name, grid=(t // tm, nj),
        in_specs=[
            pl.BlockSpec((tm, D), lambda i, j: (i, 0)),
            pl.BlockSpec((1, D), lambda i, j: (0, 0)),
            RESIDENT, RESIDENT,
        ],
        out_specs=[
            pl.BlockSpec((tm, D), lambda i, j: (i, 0)),
            pl.BlockSpec((tm, FS), lambda i, j: (i, j)),
            pl.BlockSpec((tm, FS), lambda i, j: (i, j)),
            pl.BlockSpec((tm, D), lambda i, j: (i, 0)),
        ],
        out_shape=[
            jax.ShapeDtypeStruct((t, D), f32),
            jax.ShapeDtypeStruct((t, F), bf16),
            jax.ShapeDtypeStruct((t, F), bf16),
            jax.ShapeDtypeStruct((t, D), bf16),
        ],
        scratch_shapes=[pltpu.VMEM((tm, D), bf16), pltpu.VMEM((tm, D), f32), pltpu.VMEM((tm, FS), bf16)],
        compiler_params=_cp(2),
    )(h, g, wgu, wd.reshape(nj, FS, D))


def _inproj_fwd(h, g, win, b_in):
    t = h.shape[0]
    tn = NIN // NCHIP
    nj = NIN // tn
    per = (NIN // NCHIP) // tn

    def body(h_ref, g_ref, w_ref, b_ref, proj_ref, n_ref, nb_sc):
        j = pl.program_id(1)

        @pl.when(j == 0)
        def _():
            hh = h_ref[...]
            nb = (hh * _rms_stats(hh) * g_ref[...]).astype(bf16)
            nb_sc[...] = nb
            n_ref[...] = nb

        proj_ref[...] = jnp.dot(nb_sc[...], w_ref[j], preferred_element_type=f32) + b_ref[...]

    return pl.pallas_call(
        body, name="inproj_fwd", grid=(t // TM, nj),
        in_specs=[
            pl.BlockSpec((TM, D), lambda i, j: (i, 0)),
            pl.BlockSpec((1, D), lambda i, j: (0, 0)),
            RESIDENT,
            pl.BlockSpec((1, tn), lambda i, j: (0, j)),
        ],
        out_specs=[
            pl.BlockSpec((TM, tn), lambda i, j: (i, j)),
            pl.BlockSpec((TM, D), lambda i, j: (i, 0)),
        ],
        out_shape=[jax.ShapeDtypeStruct((t, NIN), f32), jax.ShapeDtypeStruct((t, D), bf16)],
        scratch_shapes=[pltpu.VMEM((TM, D), bf16)],
        compiler_params=_cp(2),
    )(h, g, win, b_in)


def _block_gates(xr, wa_ref, ba, wx_ref, bx, lam):
    xrb = xr.astype(bf16)
    pa = jnp.concatenate([jnp.dot(xrb[:, hh * HD:(hh + 1) * HD], wa_ref[hh], preferred_element_type=f32)
                          for hh in range(NHEAD)], axis=1)
    px = jnp.concatenate([jnp.dot(xrb[:, hh * HD:(hh + 1) * HD], wx_ref[hh], preferred_element_type=f32)
                          for hh in range(NHEAD)], axis=1)
    ra = _sigmoid(pa + ba)
    ii = _sigmoid(px + bx)
    sp = _softplus(-lam)
    log_a = -8.0 * ra * sp
    a = jnp.exp(log_a)
    sq = jnp.sqrt(_one_minus_square(a, log_a))
    return ra, ii, a, sq, sp


def _rnn_fwd(proj, cw, cb, wa, ba, wx, bx, lam):
    t = proj.shape[0]
    ng = TM // 8

    def body(x_ref, y_ref, cw_ref, cb_ref, wa_ref, ba_ref, wx_ref, bx_ref, lam_ref,
             xr_ref, hr_ref, z_ref, xext_sc, carry_sc, a_sc, h_sc):
        i = pl.program_id(0)

        @pl.when(i == 0)
        def _():
            xext_sc[0:8, :] = jnp.zeros((8, D), f32)
            carry_sc[...] = jnp.zeros_like(carry_sc)

        x = x_ref[...]
        xext_sc[8:8 + TM, :] = x
        xe = xext_sc[...]
        xr = cb_ref[...] + cw_ref[KC4 - 1:KC4, :] * x
        for k in range(KC4 - 1):
            xr = xr + cw_ref[k:k + 1, :] * pltpu.roll(xe, KC4 - 1 - k, 0)[8:8 + TM]
        xext_sc[0:8, :] = x[TM - 8:TM]

        _, ii, a, sq, _ = _block_gates(xr, wa_ref, ba_ref[...], wx_ref, bx_ref[...], lam_ref[...])
        a_sc[...] = a
        h_sc[...] = sq * ii * xr
        row = _row_ids((8, D))

        def group(r, carry):
            off = pl.multiple_of(r * 8, 8)
            aa = a_sc[pl.ds(off, 8), :]
            hh = h_sc[pl.ds(off, 8), :]
            for s in (1, 2, 4):
                a_sh = jnp.where(row >= s, pltpu.roll(aa, s, 0), 1.0)
                h_sh = jnp.where(row >= s, pltpu.roll(hh, s, 0), 0.0)
                hh = aa * h_sh + hh
                aa = aa * a_sh
            hh = hh + aa * carry
            h_sc[pl.ds(off, 8), :] = hh
            return hh[7:8, :]

        carry_sc[...] = lax.fori_loop(0, ng, group, carry_sc[...])
        hr = h_sc[...]
        gel, _ = _gelu_and_grad(y_ref[...])
        xr_ref[...] = xr
        hr_ref[...] = hr
        z_ref[...] = (hr * gel).astype(bf16)

    vec = pl.BlockSpec((1, D), lambda i: (0, 0))
    return pl.pallas_call(
        body, name="rnn_fwd", grid=(t // TM,),
        in_specs=[
            pl.BlockSpec((TM, D), lambda i: (i, 0)),
            pl.BlockSpec((TM, D), lambda i: (i, 1)),
            pl.BlockSpec((KC4, D), lambda i: (0, 0)),
            vec,
            pl.BlockSpec((NHEAD, HD, HD), lambda i: (0, 0, 0)),
            vec,
            pl.BlockSpec((NHEAD, HD, HD), lambda i: (0, 0, 0)),
            vec, vec,
        ],
        out_specs=[pl.BlockSpec((TM, D), lambda i: (i, 0))] * 3,
        out_shape=[jax.ShapeDtypeStruct((t, D), f32), jax.ShapeDtypeStruct((t, D), f32),
                   jax.ShapeDtypeStruct((t, D), bf16)],
        scratch_shapes=[pltpu.VMEM((TM + 8, D), f32), pltpu.VMEM((1, D), f32),
                        pltpu.VMEM((TM, D), f32), pltpu.VMEM((TM, D), f32)],
        compiler_params=_cp(1),
    )(proj, proj, cw, cb, wa, ba, wx, bx, lam)


def _ln_stats(vc):
    mu = jnp.mean(vc, axis=-1, keepdims=True)
    xc = vc - mu
    rstd = lax.rsqrt(jnp.mean(xc * xc, axis=-1, keepdims=True) + EPS)
    return xc * rstd, rstd


def _conv_fwd(proj, w31, b31, ln_g, ln_b, after=None):
    t = proj.shape[0]

    def body(gv_ref, gg_ref, w_ref, b_ref, lg_ref, lb_ref, vc_ref, s_ref, vext_sc):
        i = pl.program_id(0)

        @pl.when(i == 0)
        def _():
            vext_sc[0:HALO, :] = jnp.zeros((HALO, D), f32)

        v = gv_ref[...] * _sigmoid(gg_ref[...])
        vext_sc[HALO:HALO + TM, :] = v
        ve = vext_sc[...]
        acc = jnp.zeros((TM, D), f32) + b_ref[...]
        for s in range(8):
            vs = ve if s == 0 else pltpu.roll(ve, s, 0)
            for m in range(HALO // 8):
                k = KC31 - 1 - (8 * m + s)
                if 0 <= k < KC31:
                    acc = acc + w_ref[k:k + 1, :] * vs[HALO - 8 * m:HALO - 8 * m + TM]
        vext_sc[0:HALO, :] = v[TM - HALO:TM]
        xhat, _ = _ln_stats(acc)
        ln = xhat * lg_ref[...] + lb_ref[...]
        vc_ref[...] = acc
        s_ref[...] = (ln * _sigmoid(ln)).astype(bf16)

    vec = pl.BlockSpec((1, D), lambda i: (0, 0))
    body, in_specs, args = _ordered(
        body,
        [pl.BlockSpec((TM, D), lambda i: (i, 2)),
         pl.BlockSpec((TM, D), lambda i: (i, 3)),
         pl.BlockSpec((KC31, D), lambda i: (0, 0)),
         vec, vec, vec],
        (proj, proj, w31, b31, ln_g, ln_b), after)
    return pl.pallas_call(
        body, name="conv_fwd", grid=(t // TM,),
        in_specs=in_specs,
        out_specs=[pl.BlockSpec((TM, D), lambda i: (i, 0))] * 2,
        out_shape=[jax.ShapeDtypeStruct((t, D), f32), jax.ShapeDtypeStruct((t, D), bf16)],
        scratch_shapes=[pltpu.VMEM((TM + HALO, D), f32)],
        compiler_params=_cp(1),
    )(*args)


def _merge_fwd(h, z, s, proj, wrp, wcp, bcp, wout):
    t = h.shape[0]

    def body(h_ref, z_ref, s_ref, ga_ref, gb_ref, wrp_ref, wcp_ref, bcp_ref, wout_ref, ho_ref):
        ya = jnp.dot(z_ref[...], wrp_ref[...], preferred_element_type=f32)
        yb = jnp.dot(s_ref[...], wcp_ref[...], preferred_element_type=f32) + bcp_ref[...]
        merged = _sigmoid(ga_ref[...]) * ya + _sigmoid(gb_ref[...]) * yb
        ho_ref[...] = h_ref[...] + jnp.dot(merged.astype(bf16), wout_ref[...], preferred_element_type=f32)

    row = pl.BlockSpec((TM, D), lambda i: (i, 0))
    wsq = pl.BlockSpec((D, D), lambda i: (0, 0))
    return pl.pallas_call(
        body, name="merge_fwd", grid=(t // TM,),
        in_specs=[row, row, row,
                  pl.BlockSpec((TM, D), lambda i: (i, 4)),
                  pl.BlockSpec((TM, D), lambda i: (i, 5)),
                  wsq, wsq, pl.BlockSpec((1, D), lambda i: (0, 0)), wsq],
        out_specs=row,
        out_shape=jax.ShapeDtypeStruct((t, D), f32),
        compiler_params=_cp(1),
    )(h, z, s, proj, proj, wrp, wcp, bcp, wout)


def _final_loss(h, g, tgt, n_valid):
    t = h.shape[0]

    def body(h_ref, g_ref, t_ref, dh_ref, loss_ref, dg_ref):
        i = pl.program_id(0)

        @pl.when(i == 0)
        def _():
            loss_ref[...] = jnp.zeros_like(loss_ref)
            dg_ref[...] = jnp.zeros_like(dg_ref)

        hh = h_ref[...]
        gg = g_ref[...]
        row = i * TM + _row_ids((TM, 1))
        valid = jnp.logical_and(row >= NMETA, row < n_valid)
        out = hh * _rms_stats(hh) * gg
        err = jnp.where(valid, out - t_ref[...], 0.0)
        loss_ref[...] += 0.5 * jnp.sum(err * err) * (1.0 / D)
        dh, dg = _rms_bwd(err * (1.0 / D), hh, gg)
        dh_ref[...] = dh
        dg_ref[...] += dg

    row_spec = pl.BlockSpec((TM, D), lambda i: (i, 0))
    return pl.pallas_call(
        body, name="final_loss", grid=(t // TM,),
        in_specs=[row_spec, pl.BlockSpec((1, D), lambda i: (0, 0)), row_spec],
        out_specs=[row_spec, pl.BlockSpec((8, 128), lambda i: (0, 0)), pl.BlockSpec((1, D), lambda i: (0, 0))],
        out_shape=[jax.ShapeDtypeStruct((t, D), f32), jax.ShapeDtypeStruct((8, 128), f32),
                   jax.ShapeDtypeStruct((1, D), f32)],
        compiler_params=_cp(1),
    )(h, g, tgt)


def _ffn_bwd(dh, h, g, gate, up, wgu, wd, name, after=None):
    t = h.shape[0]
    nj = 2

    def body(dh_ref, h_ref, g_ref, gate_ref, up_ref, wg_ref, wd_ref,
             dhi_ref, dgate_ref, dup_ref, a_ref, df_ref, dg_ref, dfb_sc, dn_sc):
        i = pl.program_id(0)
        j = pl.program_id(1)

        @pl.when(jnp.logical_and(i == 0, j == 0))
        def _():
            dg_ref[...] = jnp.zeros_like(dg_ref)

        @pl.when(j == 0)
        def _():
            dfb = (0.5 * dh_ref[...]).astype(bf16)
            dfb_sc[...] = dfb
            df_ref[...] = dfb
            dn_sc[...] = jnp.zeros_like(dn_sc)

        dfb = dfb_sc[...]
        for off, width in FSUB:
            cols = slice(off, off + width)
            da = _nt_dot(dfb, wd_ref[j, cols, :])
            gt = gate_ref[:, cols].astype(f32)
            uu = up_ref[:, cols].astype(f32)
            sg = _sigmoid(gt)
            silu = gt * sg
            a_ref[:, cols] = (silu * uu).astype(bf16)
            dgate_ref[:, cols] = (da * uu * (sg * (1.0 + gt * (1.0 - sg)))).astype(bf16)
            dup_ref[:, cols] = (da * silu).astype(bf16)
        dn_sc[...] += _nt_dot(dgate_ref[...], wg_ref[j]) + _nt_dot(dup_ref[...], wg_ref[2 + j])

        @pl.when(j == nj - 1)
        def _():
            dhin, dg = _rms_bwd(dn_sc[...], h_ref[...], g_ref[...])
            dhi_ref[...] = dh_ref[...] + dhin
            dg_ref[...] += dg

    rowd = pl.BlockSpec((TM, D), lambda i, j: (i, 0))
    rowf = pl.BlockSpec((TM, FS), lambda i, j: (i, j))
    vec = pl.BlockSpec((1, D), lambda i, j: (0, 0))
    body, in_specs, args = _ordered(
        body,
        [rowd, rowd, vec, rowf, rowf,
         RESIDENT, RESIDENT],
        (dh, h, g, gate, up, wgu, wd.reshape(nj, FS, D)), after)
    return pl.pallas_call(
        body, name=name, grid=(t // TM, nj),
        in_specs=in_specs,
        out_specs=[rowd, rowf, rowf, rowf, rowd, vec],
        out_shape=[jax.ShapeDtypeStruct((t, D), f32), jax.ShapeDtypeStruct((t, F), bf16),
                   jax.ShapeDtypeStruct((t, F), bf16), jax.ShapeDtypeStruct((t, F), bf16),
                   jax.ShapeDtypeStruct((t, D), bf16), jax.ShapeDtypeStruct((1, D), f32)],
        scratch_shapes=[pltpu.VMEM((TM, D), bf16), pltpu.VMEM((TM, D), f32)],
        compiler_params=_cp(2),
    )(*args)


def _big_tile(t):
    return max(k * TM for k in range(1, 6) if t % (k * TM) == 0)


ANY_SPEC = pl.BlockSpec(memory_space=pl.ANY)


def _tn_matmul(a, b, tk, tn, out_shape, out_block, out_map, name, base=None, after=None):
    t, kk = a.shape
    _, nn = b.shape
    tmm = _big_tile(t)
    nm = t // tmm

    def body(a_ref, b_ref, o_ref, acc_sc):
        m = pl.program_id(2)

        @pl.when(m == 0)
        def _():
            acc_sc[...] = jnp.zeros_like(acc_sc)

        acc_sc[...] += _tn_dot(a_ref[...], b_ref[...])

        @pl.when(m == nm - 1)
        def _():
            o_ref[...] = acc_sc[...].astype(o_ref.dtype)

    in_specs = [pl.BlockSpec((tmm, tk), lambda k, n, m: (m, k)),
                pl.BlockSpec((tmm, tn), lambda k, n, m: (m, n))]
    args, aliases = (a, b), {}
    if base is not None:
        body = (lambda inner: lambda a_ref, b_ref, base_ref, o_ref, acc_sc: inner(a_ref, b_ref, o_ref, acc_sc))(body)
        in_specs, args, aliases = in_specs + [ANY_SPEC], (a, b, base), {2: 0}
    if after is not None:
        body, in_specs, args = _ordered(body, in_specs, args, after)
        aliases = {k + _n_after(after): v for k, v in aliases.items()}
    return pl.pallas_call(
        body, name=name, grid=(kk // tk, nn // tn, nm),
        in_specs=in_specs,
        out_specs=pl.BlockSpec(out_block, out_map),
        out_shape=jax.ShapeDtypeStruct(out_shape, bf16),
        scratch_shapes=[pltpu.VMEM((tk, tn), f32)],
        input_output_aliases=aliases,
        compiler_params=_cp(3),
    )(*args)


def _merge_bwd(dh, z, s, proj, wrp, wcp, bcp, wout, after=None):
    t = dh.shape[0]

    def body(dh_ref, z_ref, s_ref, ga_ref, gb_ref, wrp_ref, wcp_ref, bcp_ref, wout_ref,
             dz_ref, ds_ref, dgab_ref, dhb_ref, mg_ref, dya_ref, dyb_ref, dbcp_ref):
        i = pl.program_id(0)

        @pl.when(i == 0)
        def _():
            dbcp_ref[...] = jnp.zeros_like(dbcp_ref)

        dhb = dh_ref[...].astype(bf16)
        dhb_ref[...] = dhb
        dmg = _nt_dot(dhb, wout_ref[...])
        ya = jnp.dot(z_ref[...], wrp_ref[...], preferred_element_type=f32)
        yb = jnp.dot(s_ref[...], wcp_ref[...], preferred_element_type=f32) + bcp_ref[...]
        sa = _sigmoid(ga_ref[...])
        sb = _sigmoid(gb_ref[...])
        mg_ref[...] = (sa * ya + sb * yb).astype(bf16)
        dgab_ref[:, 0:D] = (dmg * ya * sa * (1.0 - sa)).astype(bf16)
        dgab_ref[:, D:2 * D] = (dmg * yb * sb * (1.0 - sb)).astype(bf16)
        dya = dmg * sa
        dyb = dmg * sb
        dbcp_ref[...] += jnp.sum(dyb, axis=0, keepdims=True)
        dyab = dya.astype(bf16)
        dybb = dyb.astype(bf16)
        dya_ref[...] = dyab
        dyb_ref[...] = dybb
        dz_ref[...] = _nt_dot(dyab, wrp_ref[...])
        ds_ref[...] = _nt_dot(dybb, wcp_ref[...])

    row = pl.BlockSpec((TM, D), lambda i: (i, 0))
    wsq = pl.BlockSpec((D, D), lambda i: (0, 0))
    vec = pl.BlockSpec((1, D), lambda i: (0, 0))
    rowb = jax.ShapeDtypeStruct((t, D), bf16)
    body, in_specs, args = _ordered(
        body,
        [row, row, row,
         pl.BlockSpec((TM, D), lambda i: (i, 4)),
         pl.BlockSpec((TM, D), lambda i: (i, 5)),
         wsq, wsq, vec, wsq],
        (dh, z, s, proj, proj, wrp, wcp, bcp, wout), after)
    return pl.pallas_call(
        body, name="merge_bwd", grid=(t // TM,),
        in_specs=in_specs,
        out_specs=[row, row,
                   pl.BlockSpec((TM, 2 * D), lambda i: (i, 2)),
                   row, row, row, row, vec],
        out_shape=[jax.ShapeDtypeStruct((t, D), f32), jax.ShapeDtypeStruct((t, D), f32),
                   jax.ShapeDtypeStruct((t, NIN), bf16),
                   rowb, rowb, rowb, rowb, jax.ShapeDtypeStruct((1, D), f32)],
        compiler_params=_cp(1),
    )(*args)


def _conv_bwd(ds, vc, proj, dproj, w31, ln_g, ln_b, after=None):
    t = ds.shape[0]
    nt = t // TM
    hb = TM // HALO

    rb = 16
    nb = TM // rb
    taps = [(KC31 - 1 - (8 * m + s), s, m) for s in range(8) for m in range(HALO // 8)
            if 0 <= KC31 - 1 - (8 * m + s) < KC31]

    def groups(a):
        return jnp.sum(a.reshape(rb // 8, 8, D), axis=0)

    def body(ds_ref, vc_ref, gv_ref, gg_ref, gvp_ref, ggp_ref, dpin_ref, w_ref, lg_ref, lb_ref,
             dgvg_ref, dw_ref, db_ref, dlg_ref, dlb_ref, dext_sc, vext_sc, rot_sc, dwacc_sc, small_sc, wb_sc):
        del dpin_ref
        i = pl.program_id(0)
        tile = nt - 1 - i

        @pl.when(i == 0)
        def _():
            dext_sc[TM:TM + HALO, :] = jnp.zeros((HALO, D), f32)
            dwacc_sc[...] = jnp.zeros_like(dwacc_sc)
            small_sc[...] = jnp.zeros_like(small_sc)

        lg = lg_ref[...]
        lb = lb_ref[...]

        xhat, rstd = _ln_stats(vc_ref[...])
        ln = xhat * lg + lb
        sg = _sigmoid(ln)
        dln = ds_ref[...] * (sg * (1.0 + ln * (1.0 - sg)))
        dxh = dln * lg
        dvc = rstd * (dxh - jnp.mean(dxh, axis=-1, keepdims=True)
                      - xhat * jnp.mean(dxh * xhat, axis=-1, keepdims=True))
        small_sc[0] += jnp.sum((dln * xhat).reshape(TM // 8, 8, D), axis=0)
        small_sc[1] += jnp.sum(dln.reshape(TM // 8, 8, D), axis=0)
        small_sc[2] += jnp.sum(dvc.reshape(TM // 8, 8, D), axis=0)
        dext_sc[0:TM, :] = dvc
        vext_sc[HALO:HALO + TM, :] = gv_ref[...] * _sigmoid(gg_ref[...])
        vext_sc[0:HALO, :] = jnp.where(tile > 0, gvp_ref[...] * _sigmoid(ggp_ref[...]), 0.0)

        @pl.when(i == 0)
        def _():
            for k in range(KC31):
                wb_sc[k] = jnp.broadcast_to(w_ref[k:k + 1, :], (8, D))

        for s in range(1, 8):
            rot_sc[s - 1] = pltpu.roll(dext_sc[...], TM + HALO - s, 0)

        def dv_block(b, carry):
            rows = pl.ds(pl.multiple_of(b * rb, rb), rb)
            acc = jnp.zeros((rb, D), f32)
            for k, s, m in taps:
                src = pl.ds(pl.multiple_of(b * rb + 8 * m, 8), rb)
                slab = dext_sc[src, :] if s == 0 else rot_sc[s - 1, src, :]
                acc = acc + (slab.reshape(rb // 8, 8, D) * wb_sc[k]).reshape(rb, D)
            sgg = _sigmoid(gg_ref[rows, :])
            dgvg_ref[rows, 0:D] = (acc * sgg).astype(bf16)
            dgvg_ref[rows, D:2 * D] = (acc * gv_ref[rows, :] * sgg * (1.0 - sgg)).astype(bf16)
            return carry

        lax.fori_loop(0, nb, dv_block, 0)

        for s in range(1, 8):
            rot_sc[s - 1] = pltpu.roll(vext_sc[...], s, 0)
        for first in range(0, len(taps), 3):
            trio = taps[first:first + 3]

            def dw_block(b, accs, trio=trio):
                rows = pl.ds(pl.multiple_of(b * rb, rb), rb)
                dvc_blk = dext_sc[rows, :]
                out = []
                for acc, (k, s, m) in zip(accs, trio):
                    src = pl.ds(pl.multiple_of(b * rb + HALO - 8 * m, 8), rb)
                    slab = vext_sc[src, :] if s == 0 else rot_sc[s - 1, src, :]
                    out.append(acc + groups(dvc_blk * slab))
                return tuple(out)

            sums = lax.fori_loop(0, nb, dw_block, tuple(jnp.zeros((8, D), f32) for _ in trio))
            for acc, (k, s, m) in zip(sums, trio):
                dwacc_sc[k] += acc
        dext_sc[TM:TM + HALO, :] = dext_sc[0:HALO, :]

        @pl.when(i == nt - 1)
        def _():
            for k in range(KC31):
                dw_ref[k:k + 1, :] = jnp.sum(dwacc_sc[k], axis=0, keepdims=True)
            dlg_ref[...] = jnp.sum(small_sc[0], axis=0, keepdims=True)
            dlb_ref[...] = jnp.sum(small_sc[1], axis=0, keepdims=True)
            db_ref[...] = jnp.sum(small_sc[2], axis=0, keepdims=True)

    rev = lambda i: (nt - 1 - i, 0)
    vec = pl.BlockSpec((1, D), lambda i: (0, 0))
    halo_row = lambda i: jnp.maximum((nt - 1 - i) * hb - 1, 0)
    body, in_specs, args = _ordered(
        body,
        [pl.BlockSpec((TM, D), rev),
         pl.BlockSpec((TM, D), rev),
         pl.BlockSpec((TM, D), lambda i: (nt - 1 - i, 2)),
         pl.BlockSpec((TM, D), lambda i: (nt - 1 - i, 3)),
         pl.BlockSpec((HALO, D), lambda i: (halo_row(i), 2)),
         pl.BlockSpec((HALO, D), lambda i: (halo_row(i), 3)),
         pl.BlockSpec(memory_space=pl.ANY),
         pl.BlockSpec((KC31, D), lambda i: (0, 0)),
         vec, vec],
        (ds, vc, proj, proj, proj, proj, dproj, w31, ln_g, ln_b), after)
    return pl.pallas_call(
        body, name="conv_bwd", grid=(nt,),
        in_specs=in_specs,
        out_specs=[
            pl.BlockSpec((TM, 2 * D), lambda i: (nt - 1 - i, 1)),
            pl.BlockSpec((KC31, D), lambda i: (0, 0)),
            vec, vec, vec,
        ],
        out_shape=[jax.ShapeDtypeStruct((t, NIN), bf16),
                   jax.ShapeDtypeStruct((KC31, D), f32),
                   jax.ShapeDtypeStruct((1, D), f32), jax.ShapeDtypeStruct((1, D), f32),
                   jax.ShapeDtypeStruct((1, D), f32)],
        scratch_shapes=[pltpu.VMEM((TM + HALO, D), f32), pltpu.VMEM((TM + HALO, D), f32),
                        pltpu.VMEM((7, TM + HALO, D), f32), pltpu.VMEM((KC31, 8, D), f32),
                        pltpu.VMEM((3, 8, D), f32), pltpu.VMEM((KC31, 8, D), f32)],
        input_output_aliases={6 + _n_after(after): 0},
        compiler_params=_cp(1),
    )(*args)


def _rnn_bwd(dz, xr, hr, proj, dproj, cw, wa, ba, wx, bx, lam):
    t = dz.shape[0]
    nt = t // TM
    ng = TM // 8
    hq = HD // NCHIP

    def body(dz_ref, xr_ref, hr_ref, hrp_ref, x_ref, xp_ref, y_ref, dpin_ref,
             cw_ref, wa_ref, ba_ref, wx_ref, bx_ref, lam_ref,
             dxy_ref, dwa_ref, dwx_ref, dcw_ref, dcb_ref, dba_ref, dbx_ref, dlam_ref,
             anext_sc, gcarry_sc, dext_sc, xext_sc, m_sc, g_sc, dwa_sc, dwx_sc, dsp_sc):
        del dpin_ref
        i = pl.program_id(0)
        tile = nt - 1 - i

        @pl.when(i == 0)
        def _():
            anext_sc[...] = jnp.zeros_like(anext_sc)
            gcarry_sc[...] = jnp.zeros_like(gcarry_sc)
            dext_sc[TM:TM + 8, :] = jnp.zeros((8, D), f32)
            dwa_sc[...] = jnp.zeros_like(dwa_sc)
            dwx_sc[...] = jnp.zeros_like(dwx_sc)
            dsp_sc[...] = jnp.zeros_like(dsp_sc)
            dcw_ref[...] = jnp.zeros_like(dcw_ref)
            dcb_ref[...] = jnp.zeros_like(dcb_ref)
            dba_ref[...] = jnp.zeros_like(dba_ref)
            dbx_ref[...] = jnp.zeros_like(dbx_ref)

        xr = xr_ref[...]
        hr = hr_ref[...]
        dz = dz_ref[...]
        gel, dgel = _gelu_and_grad(y_ref[...])
        dxy_ref[:, D:2 * D] = (dz * hr * dgel).astype(bf16)
        ra, ii, a, sq, sp = _block_gates(xr, wa_ref, ba_ref[...], wx_ref, bx_ref[...], lam_ref[...])

        row = _row_ids((TM, D))
        m_sc[...] = jnp.where(row == TM - 1, anext_sc[...], pltpu.roll(a, TM - 1, 0))
        anext_sc[...] = a[0:1, :]
        g_sc[...] = dz * gel
        row8 = _row_ids((8, D))

        def group(qq, carry):
            off = pl.multiple_of((ng - 1 - qq) * 8, 8)
            mm = m_sc[pl.ds(off, 8), :]
            dd = g_sc[pl.ds(off, 8), :]
            for s in (1, 2, 4):
                m_sh = jnp.where(row8 < 8 - s, pltpu.roll(mm, 8 - s, 0), 1.0)
                d_sh = jnp.where(row8 < 8 - s, pltpu.roll(dd, 8 - s, 0), 0.0)
                dd = dd + mm * d_sh
                mm = mm * m_sh
            dd = dd + mm * carry
            g_sc[pl.ds(off, 8), :] = dd
            return dd[0:1, :]

        gcarry_sc[...] = lax.fori_loop(0, ng, group, gcarry_sc[...])
        gg = g_sc[...]

        hlast = jnp.where(tile > 0, hrp_ref[7:8, :], 0.0)
        hprev = jnp.where(row == 0, hlast, pltpu.roll(hr, 1, 0))
        d_a = gg * hprev
        dsq = gg * ii * xr
        dii = gg * sq * xr
        dxr = gg * sq * ii
        dlog = d_a * a - dsq * (a * a / sq)
        dsp_sc[...] += jnp.sum(dlog * (-8.0 * ra), axis=0, keepdims=True)
        dpa = dlog * (-8.0 * sp) * ra * (1.0 - ra)
        dpx = dii * ii * (1.0 - ii)
        dba_ref[...] += jnp.sum(dpa, axis=0, keepdims=True)
        dbx_ref[...] += jnp.sum(dpx, axis=0, keepdims=True)
        dpab = dpa.astype(bf16)
        dpxb = dpx.astype(bf16)
        xrb = xr.astype(bf16)
        back = []
        for hh in range(NHEAD):
            cols = slice(hh * HD, (hh + 1) * HD)
            back.append(_nt_dot(dpab[:, cols], wa_ref[hh]) + _nt_dot(dpxb[:, cols], wx_ref[hh]))
            dwa_sc[hh] += _tn_dot(xrb[:, cols], dpab[:, cols])
            dwx_sc[hh] += _tn_dot(xrb[:, cols], dpxb[:, cols])
        dxr = dxr + jnp.concatenate(back, axis=1)

        dext_sc[0:TM, :] = dxr
        de = dext_sc[...]
        dx = cw_ref[KC4 - 1:KC4, :] * dxr
        for k in range(KC4 - 1):
            dx = dx + cw_ref[k:k + 1, :] * pltpu.roll(de, TM + 8 - (KC4 - 1 - k), 0)[0:TM]
        dext_sc[TM:TM + 8, :] = dxr[0:8]
        dxy_ref[:, 0:D] = dx.astype(bf16)

        x = x_ref[...]
        xext_sc[0:8, :] = jnp.where(tile > 0, xp_ref[...], 0.0)
        xext_sc[8:8 + TM, :] = x
        xe = xext_sc[...]
        dcw_ref[KC4 - 1:KC4, :] += jnp.sum(dxr * x, axis=0, keepdims=True)
        for k in range(KC4 - 1):
            xs = pltpu.roll(xe, KC4 - 1 - k, 0)[8:8 + TM]
            dcw_ref[k:k + 1, :] += jnp.sum(dxr * xs, axis=0, keepdims=True)
        dcb_ref[...] += jnp.sum(dxr, axis=0, keepdims=True)

        @pl.when(i == nt - 1)
        def _():
            for hh in range(NHEAD):
                for qc in range(NCHIP):
                    dwa_ref[qc, hh] = dwa_sc[hh, qc * hq:(qc + 1) * hq, :].astype(bf16)
                    dwx_ref[qc, hh] = dwx_sc[hh, qc * hq:(qc + 1) * hq, :].astype(bf16)
            dlam_ref[...] = -dsp_sc[...] * _sigmoid(-lam_ref[...])

    rev = lambda i: (nt - 1 - i, 0)
    vec = pl.BlockSpec((1, D), lambda i: (0, 0))
    prev8 = lambda i: jnp.maximum((nt - 1 - i) * ng - 1, 0)
    wblk = pl.BlockSpec((NHEAD, HD, HD), lambda i: (0, 0, 0))
    gblk = pl.BlockSpec((NCHIP, NHEAD, hq, HD), lambda i: (0, 0, 0, 0))
    return pl.pallas_call(
        body, name="rnn_bwd", grid=(nt,),
        in_specs=[
            pl.BlockSpec((TM, D), rev),
            pl.BlockSpec((TM, D), rev),
            pl.BlockSpec((TM, D), rev),
            pl.BlockSpec((8, D), lambda i: (prev8(i), 0)),
            pl.BlockSpec((TM, D), lambda i: (nt - 1 - i, 0)),
            pl.BlockSpec((8, D), lambda i: (prev8(i), 0)),
            pl.BlockSpec((TM, D), lambda i: (nt - 1 - i, 1)),
            pl.BlockSpec(memory_space=pl.ANY),
            pl.BlockSpec((KC4, D), lambda i: (0, 0)),
            wblk, vec, wblk, vec, vec,
        ],
        out_specs=[
            pl.BlockSpec((TM, 2 * D), lambda i: (nt - 1 - i, 0)),
            gblk, gblk,
            pl.BlockSpec((KC4, D), lambda i: (0, 0)),
            vec, vec, vec, vec,
        ],
        out_shape=[jax.ShapeDtypeStruct((t, NIN), bf16),
                   jax.ShapeDtypeStruct((NCHIP, NHEAD, hq, HD), bf16),
                   jax.ShapeDtypeStruct((NCHIP, NHEAD, hq, HD), bf16),
                   jax.ShapeDtypeStruct((KC4, D), f32),
                   jax.ShapeDtypeStruct((1, D), f32), jax.ShapeDtypeStruct((1, D), f32),
                   jax.ShapeDtypeStruct((1, D), f32), jax.ShapeDtypeStruct((1, D), f32)],
        scratch_shapes=[pltpu.VMEM((1, D), f32), pltpu.VMEM((1, D), f32),
                        pltpu.VMEM((TM + 8, D), f32), pltpu.VMEM((TM + 8, D), f32),
                        pltpu.VMEM((TM, D), f32), pltpu.VMEM((TM, D), f32),
                        pltpu.VMEM((NHEAD, HD, HD), f32), pltpu.VMEM((NHEAD, HD, HD), f32),
                        pltpu.VMEM((1, D), f32)],
        input_output_aliases={7: 0},
        compiler_params=_cp(1),
    )(dz, xr, hr, hr, proj, proj, proj, dproj, cw, wa, ba, wx, bx, lam)


def _inproj_bwd(dproj, dh, h, g, win, after=None):
    t = h.shape[0]
    tn = NIN // NCHIP
    nj = NIN // tn
    per = (NIN // NCHIP) // tn

    def body(dp_ref, dh_ref, h_ref, g_ref, w_ref, dhi_ref, dg_ref, db_ref, dn_sc):
        i = pl.program_id(0)
        j = pl.program_id(1)

        @pl.when(jnp.logical_and(i == 0, j == 0))
        def _():
            dg_ref[...] = jnp.zeros_like(dg_ref)
            db_ref[...] = jnp.zeros_like(db_ref)

        @pl.when(j == 0)
        def _():
            dn_sc[...] = jnp.zeros_like(dn_sc)

        dp = dp_ref[...]
        dn_sc[...] += _nt_dot(dp, w_ref[j])
        db_ref[j] += jnp.sum(dp.astype(f32), axis=0, keepdims=True)

        @pl.when(j == nj - 1)
        def _():
            dhin, dg = _rms_bwd(dn_sc[...], h_ref[...], g_ref[...])
            dhi_ref[...] = dh_ref[...] + dhin
            dg_ref[...] += dg

    rowd = pl.BlockSpec((TM, D), lambda i, j: (i, 0))
    vec = pl.BlockSpec((1, D), lambda i, j: (0, 0))
    body, in_specs, args = _ordered(
        body,
        [pl.BlockSpec((TM, tn), lambda i, j: (i, j)), rowd, rowd, vec,
         RESIDENT],
        (dproj, dh, h, g, win), after)
    return pl.pallas_call(
        body, name="inproj_bwd", grid=(t // TM, nj),
        in_specs=in_specs,
        out_specs=[rowd, vec, pl.BlockSpec((nj, 1, tn), lambda i, j: (0, 0, 0))],
        out_shape=[jax.ShapeDtypeStruct((t, D), f32), jax.ShapeDtypeStruct((1, D), f32),
                   jax.ShapeDtypeStruct((nj, 1, tn), f32)],
        scratch_shapes=[pltpu.VMEM((TM, D), f32)],
        compiler_params=_cp(2),
    )(*args)


def _ffn_gu_grad(n, dgate, dup, tag, after=None):
    half = _tn_matmul(n, dgate, D, FS, (NCHIP, D, FS), (None, D, FS), lambda k, nn, m: (nn, 0, 0),
                      tag + "_dwg", after=after)
    return _tn_matmul(n, dup, D, FS, (NCHIP, D, FS), (None, D, FS), lambda k, nn, m: (2 + nn, 0, 0),
                      tag + "_dwu", base=half)


def _ffn_down_grad(a, df, tag, after=None):
    return _tn_matmul(a, df, FS, D, (F, D), (FS, D), lambda k, nn, m: (k, 0), tag + "_dwd", after=after)


def _square_grad(a, b, name):
    return _tn_matmul(a, b, D, D, (D, D), (D, D), lambda k, nn, m: (0, 0), name)


ANY = pl.BlockSpec(memory_space=pl.ANY)


def _place():
    x, y, c = lax.axis_index("x"), lax.axis_index("y"), lax.axis_index("c")
    chips = [(1 - x, y), (x, 1 - y), (1 - x, 1 - y)]
    return x, y, c, chips


def _chip_id(chip):
    return 2 * chip[0] + chip[1]


def _cast_into_slot(w2d, qc, dtype, name, after=None):
    r, cc = w2d.shape
    hr = r // 2

    def body(qc_ref, *refs):
        del qc_ref
        w_ref, o_ref = refs[-2:]
        o_ref[...] = w_ref[...].astype(dtype)

    in_specs, args = [pl.BlockSpec((hr, cc), lambda h, qc_ref: (h, 0))], (w2d,)
    if after is not None:
        in_specs, args = [ANY_SPEC] + in_specs, (after,) + args
    return pl.pallas_call(
        body, name=name,
        grid_spec=pltpu.PrefetchScalarGridSpec(
            num_scalar_prefetch=1, grid=(2,),
            in_specs=in_specs,
            out_specs=pl.BlockSpec((None, None, hr, cc), lambda h, qc_ref: (qc_ref[0], h, 0, 0))),
        out_shape=jax.ShapeDtypeStruct((NCHIP, 2, hr, cc), dtype),
        compiler_params=_cp(1),
    )(qc, *args)


def _place_pack(pack, qc):
    def body(qc_ref, p_ref, o_ref):
        del qc_ref
        o_ref[...] = p_ref[...]

    return pl.pallas_call(
        body, name="place_pack",
        grid_spec=pltpu.PrefetchScalarGridSpec(
            num_scalar_prefetch=1, grid=(1,),
            in_specs=[pl.BlockSpec(pack.shape, lambda i, qc_ref: (0, 0))],
            out_specs=pl.BlockSpec((None,) + pack.shape, lambda i, qc_ref: (2 * qc_ref[0] + qc_ref[1], 0, 0))),
        out_shape=jax.ShapeDtypeStruct((8,) + pack.shape, pack.dtype),
        compiler_params=_cp(1),
    )(qc, pack)


def _pair_add(part, got, qc, name):
    _, _, hr, cc = part.shape

    def body(qc_ref, p_ref, g_ref, o_ref, land_ref):
        s = pl.program_id(0)
        val = (p_ref[...].astype(f32) + g_ref[...].astype(f32)).astype(bf16)
        o_ref[...] = val

        @pl.when(s == qc_ref[0])
        def _():
            land_ref[...] = val

    return pl.pallas_call(
        body, name=name,
        grid_spec=pltpu.PrefetchScalarGridSpec(
            num_scalar_prefetch=1, grid=(NCHIP,),
            in_specs=[pl.BlockSpec((None, None, hr, cc), lambda s, qc_ref: (s, qc_ref[1], 0, 0)),
                      pl.BlockSpec((None, hr, cc), lambda s, qc_ref: (s, 0, 0))],
            out_specs=[pl.BlockSpec((None, hr, cc), lambda s, qc_ref: (s, 0, 0)),
                       pl.BlockSpec((None, hr, cc), lambda s, qc_ref: (qc_ref[0], 0, 0))]),
        out_shape=[jax.ShapeDtypeStruct((NCHIP, hr, cc), bf16)] * 2,
        compiler_params=_cp(1),
    )(qc, part, got)


def _sum_chips(got, name):
    _, hr, cc = got.shape

    def body(g_ref, o_ref):
        acc = g_ref[0].astype(f32)
        for s in range(1, NCHIP):
            acc = acc + g_ref[s].astype(f32)
        o_ref[...] = acc

    return pl.pallas_call(
        body, name=name, grid=(1,),
        in_specs=[pl.BlockSpec((NCHIP, hr, cc), lambda i: (0, 0, 0))],
        out_specs=pl.BlockSpec((hr, cc), lambda i: (0, 0)),
        out_shape=jax.ShapeDtypeStruct((hr, cc), f32),
        compiler_params=_cp(1),
    )(got)


def _pair_share(halves, name, after=None):
    n = len(halves)
    extra = () if after is None else (after,)

    def body(*refs):
        refs = refs[len(extra):]
        ins, outs = refs[:n], refs[n:2 * n]
        send_sems, recv_sems = refs[2 * n:]
        x, y, c, _ = _place()
        copies = []
        for a in range(n):
            cp = pltpu.make_async_remote_copy(
                src_ref=ins[a], dst_ref=outs[a], send_sem=send_sems.at[a], recv_sem=recv_sems.at[a],
                device_id=(x, y, 1 - c), device_id_type=MESH)
            cp.start()
            copies.append(cp)
        for cp in copies:
            cp.wait()

    return pl.pallas_call(
        body, name=name,
        in_specs=[ANY] * (len(extra) + n), out_specs=[ANY] * n,
        out_shape=[jax.ShapeDtypeStruct(s.shape, s.dtype) for s in halves],
        scratch_shapes=[pltpu.SemaphoreType.DMA((n,)), pltpu.SemaphoreType.DMA((n,))],
    )(*extra, *halves)


def _all_copy(buf_ref, send_ref, recv_ref, k, x, y, c, landing):
    px, py, pc = (1 - x if k & 4 else x, 1 - y if k & 2 else y, 1 - c if k & 1 else c)
    me = 4 * x + 2 * y + c
    there = 4 * px + 2 * py + pc
    return pltpu.make_async_remote_copy(
        src_ref=buf_ref.at[me], dst_ref=buf_ref.at[there if landing else me],
        send_sem=send_ref.at[k - 1], recv_sem=recv_ref.at[k - 1],
        device_id=(px, py, pc), device_id_type=MESH)


def _gather_all_start(buf, name):
    def body(in_ref, send, recv, thru, token):
        del thru
        x, y, c, _ = _place()
        for k in range(1, 8):
            _all_copy(in_ref, send, recv, k, x, y, c, False).start()
        token[...] = jnp.zeros_like(token)

    return pl.pallas_call(
        body, name=name,
        in_specs=[HBM],
        out_specs=[SEM, SEM, HBM, pl.BlockSpec(memory_space=pltpu.VMEM)],
        out_shape=[pltpu.SemaphoreType.DMA((7,)), pltpu.SemaphoreType.DMA((7,)),
                   pltpu.HBM(buf.shape, buf.dtype), jax.ShapeDtypeStruct((8, 128), f32)],
        input_output_aliases={0: 2},
        compiler_params=pltpu.CompilerParams(has_side_effects=EFFECT),
    )(_in_hbm(buf))


def _gather_all_wait(send, recv, buf, after, name):
    def body(in_ref, send_r, recv_r, after_ref, out_ref):
        del after_ref, out_ref
        x, y, c, _ = _place()
        for k in range(1, 8):
            cp = _all_copy(in_ref, send_r, recv_r, k, x, y, c, True)
            cp.wait_send()
            cp.wait_recv()

    return pl.pallas_call(
        body, name=name,
        in_specs=[HBM, SEM, SEM, ANY],
        out_specs=HBM,
        out_shape=pltpu.HBM(buf.shape, buf.dtype),
        input_output_aliases={0: 0},
        compiler_params=pltpu.CompilerParams(has_side_effects=EFFECT),
    )(buf, send, recv, after)


HBM = pl.BlockSpec(memory_space=pltpu.HBM)
SEM = pl.BlockSpec(memory_space=pltpu.SEMAPHORE)
EFFECT = pltpu.SideEffectType.DATAFLOW_SIDE_EFFECTING
N_PEER = 3


def _in_hbm(a):
    return pltpu.with_memory_space_constraint(a, pltpu.HBM)


def _gather_copy(buf_ref, send_ref, recv_ref, j, chip, q, c, landing_chip):
    return pltpu.make_async_remote_copy(
        src_ref=buf_ref.at[q, c], dst_ref=buf_ref.at[landing_chip, c],
        send_sem=send_ref.at[j], recv_sem=recv_ref.at[j],
        device_id=(chip[0], chip[1], c), device_id_type=MESH)


def _gather_start(bufs, name):
    n = len(bufs)

    def body(*refs):
        ins = refs[:n]
        send, recv = refs[n:2 * n], refs[2 * n:3 * n]
        token = refs[4 * n]
        x, y, c, chips = _place()
        q = 2 * x + y
        for a in range(n):
            for j, chip in enumerate(chips):
                _gather_copy(ins[a], send[a], recv[a], j, chip, q, c, q).start()
        token[...] = jnp.zeros_like(token)

    sems = [pltpu.SemaphoreType.DMA((N_PEER,))] * (2 * n)
    outs = pl.pallas_call(
        body, name=name,
        in_specs=[HBM] * n,
        out_specs=[SEM] * (2 * n) + [HBM] * n + [pl.BlockSpec(memory_space=pltpu.VMEM)],
        out_shape=sems + [pltpu.HBM(b.shape, b.dtype) for b in bufs] + [jax.ShapeDtypeStruct((8, 128), f32)],
        input_output_aliases={a: 2 * n + a for a in range(n)},
        compiler_params=pltpu.CompilerParams(has_side_effects=EFFECT),
    )(*[_in_hbm(b) for b in bufs])
    return list(outs[:n]), list(outs[n:2 * n]), list(outs[2 * n:3 * n]), outs[3 * n]


def _gather_wait(send, recv, bufs, after, name):
    n = len(bufs)

    def body(*refs):
        ins = refs[:n]
        send_r, recv_r = refs[n:2 * n], refs[2 * n:3 * n]
        x, y, c, chips = _place()
        q = 2 * x + y
        for a in range(n):
            for j, chip in enumerate(chips):
                cp = _gather_copy(ins[a], send_r[a], recv_r[a], j, chip, q, c, _chip_id(chip))
                cp.wait_send()
                cp.wait_recv()

    afters = after if isinstance(after, (tuple, list)) else (after,)
    outs = pl.pallas_call(
        body, name=name,
        in_specs=[HBM] * n + [SEM] * (2 * n) + [ANY] * len(afters),
        out_specs=[HBM] * n,
        out_shape=[pltpu.HBM(b.shape, b.dtype) for b in bufs],
        input_output_aliases={a: a for a in range(n)},
        compiler_params=pltpu.CompilerParams(has_side_effects=EFFECT),
    )(*bufs, *send, *recv, *afters)
    return list(outs)


def _forward_halves(bufs, name):
    n = len(bufs)

    def body(*refs):
        outs = refs[n:2 * n]
        send_sems, recv_sems = refs[2 * n:]
        x, y, c, chips = _place()
        sibling = (x, y, 1 - c)

        def remote(a, j, blk):
            return pltpu.make_async_remote_copy(src_ref=blk, dst_ref=blk, send_sem=send_sems.at[a, j],
                                                recv_sem=recv_sems.at[a, j], device_id=sibling,
                                                device_id_type=MESH)

        sent = []
        for a in range(n):
            for j, chip in enumerate(chips):
                cp = remote(a, j, outs[a].at[_chip_id(chip), c])
                cp.start()
                sent.append(cp)
        for a in range(n):
            for j, chip in enumerate(chips):
                remote(a, j, outs[a].at[_chip_id(chip), 1 - c]).wait_recv()
        for cp in sent:
            cp.wait_send()

    return pl.pallas_call(
        body, name=name,
        in_specs=[ANY] * n, out_specs=[ANY] * n,
        out_shape=[jax.ShapeDtypeStruct(s.shape, s.dtype) for s in bufs],
        scratch_shapes=[pltpu.SemaphoreType.DMA((n, N_PEER)), pltpu.SemaphoreType.DMA((n, N_PEER))],
        input_output_aliases={a: a for a in range(n)},
    )(*bufs)


def _reduce_copy(sum_ref, land_ref, send_ref, recv_ref, j, chip, q, c, landing_chip):
    return pltpu.make_async_remote_copy(
        src_ref=sum_ref.at[_chip_id(chip)], dst_ref=land_ref.at[landing_chip],
        send_sem=send_ref.at[j], recv_sem=recv_ref.at[j],
        device_id=(chip[0], chip[1], c), device_id_type=MESH)


def _reduce_start(sums, lands, name):
    n = len(sums)

    def body(*refs):
        s_in, l_in = refs[:n], refs[n:2 * n]
        send, recv = refs[2 * n:3 * n], refs[3 * n:4 * n]
        token = refs[6 * n]
        x, y, c, chips = _place()
        q = 2 * x + y
        for a in range(n):
            for j, chip in enumerate(chips):
                _reduce_copy(s_in[a], l_in[a], send[a], recv[a], j, chip, q, c, q).start()
        token[...] = jnp.zeros_like(token)

    sems = [pltpu.SemaphoreType.DMA((N_PEER,))] * (2 * n)
    outs = pl.pallas_call(
        body, name=name,
        in_specs=[HBM] * (2 * n),
        out_specs=[SEM] * (2 * n) + [HBM] * (2 * n) + [pl.BlockSpec(memory_space=pltpu.VMEM)],
        out_shape=sems + [pltpu.HBM(b.shape, b.dtype) for b in list(sums) + list(lands)]
        + [jax.ShapeDtypeStruct((8, 128), f32)],
        input_output_aliases={a: 2 * n + a for a in range(2 * n)},
        compiler_params=pltpu.CompilerParams(has_side_effects=EFFECT),
    )(*[_in_hbm(b) for b in list(sums) + list(lands)])
    return (list(outs[:n]), list(outs[n:2 * n]), list(outs[2 * n:3 * n]), list(outs[3 * n:4 * n]),
            outs[4 * n])


def _reduce_wait(send, recv, sums, lands, after, name):
    n = len(sums)

    def body(*refs):
        s_in, l_in = refs[:n], refs[n:2 * n]
        send_r, recv_r = refs[2 * n:3 * n], refs[3 * n:4 * n]
        x, y, c, chips = _place()
        q = 2 * x + y
        for a in range(n):
            for j, chip in enumerate(chips):
                cp = _reduce_copy(s_in[a], l_in[a], send_r[a], recv_r[a], j, chip, q, c, _chip_id(chip))
                cp.wait_send()
                cp.wait_recv()

    outs = pl.pallas_call(
        body, name=name,
        in_specs=[HBM] * (2 * n) + [SEM] * (2 * n) + [ANY],
        out_specs=[HBM] * (2 * n),
        out_shape=[pltpu.HBM(b.shape, b.dtype) for b in list(sums) + list(lands)],
        input_output_aliases={a: a for a in range(2 * n)},
        compiler_params=pltpu.CompilerParams(has_side_effects=EFFECT),
    )(*sums, *lands, *send, *recv, after)
    return list(outs[n:])


def _sibling_copy(part_ref, land_ref, send_ref, recv_ref, x, y, c):
    return pltpu.make_async_remote_copy(
        src_ref=part_ref.at[:, 1 - c], dst_ref=land_ref, send_sem=send_ref.at[0], recv_sem=recv_ref.at[0],
        device_id=(x, y, 1 - c), device_id_type=MESH)


def _pair_exchange_start(parts, name):
    n = len(parts)
    lands = [lax.empty((NCHIP,) + p.shape[2:], p.dtype) for p in parts]

    def body(*refs):
        p_in, l_in = refs[:n], refs[n:2 * n]
        send, recv = refs[2 * n:3 * n], refs[3 * n:4 * n]
        token = refs[6 * n]
        x, y, c, _ = _place()
        for a in range(n):
            _sibling_copy(p_in[a], l_in[a], send[a], recv[a], x, y, c).start()
        token[...] = jnp.zeros_like(token)

    sems = [pltpu.SemaphoreType.DMA((1,))] * (2 * n)
    outs = pl.pallas_call(
        body, name=name,
        in_specs=[HBM] * (2 * n),
        out_specs=[SEM] * (2 * n) + [HBM] * (2 * n) + [pl.BlockSpec(memory_space=pltpu.VMEM)],
        out_shape=sems + [pltpu.HBM(b.shape, b.dtype) for b in list(parts) + lands]
        + [jax.ShapeDtypeStruct((8, 128), f32)],
        input_output_aliases={a: 2 * n + a for a in range(2 * n)},
        compiler_params=pltpu.CompilerParams(has_side_effects=EFFECT),
    )(*[_in_hbm(b) for b in list(parts) + lands])
    return (list(outs[:n]), list(outs[n:2 * n]), list(outs[2 * n:3 * n]), list(outs[3 * n:4 * n]),
            outs[4 * n])


def _pair_exchange_wait(send, recv, parts, lands, after, name):
    n = len(parts)

    def body(*refs):
        p_in, l_in = refs[:n], refs[n:2 * n]
        send_r, recv_r = refs[2 * n:3 * n], refs[3 * n:4 * n]
        x, y, c, _ = _place()
        for a in range(n):
            cp = _sibling_copy(p_in[a], l_in[a], send_r[a], recv_r[a], x, y, c)
            cp.wait_send()
            cp.wait_recv()

    outs = pl.pallas_call(
        body, name=name,
        in_specs=[HBM] * (2 * n) + [SEM] * (2 * n) + [ANY],
        out_specs=[HBM] * (2 * n),
        out_shape=[pltpu.HBM(b.shape, b.dtype) for b in list(parts) + list(lands)],
        input_output_aliases={a: a for a in range(2 * n)},
        compiler_params=pltpu.CompilerParams(has_side_effects=EFFECT),
    )(*parts, *lands, *send, *recv, after)
    return list(outs[:n]), list(outs[n:])


def _forward_copy(buf_ref, send_ref, recv_ref, j, chip, x, y, c, landing):
    return pltpu.make_async_remote_copy(
        src_ref=buf_ref.at[_chip_id(chip), c], dst_ref=buf_ref.at[_chip_id(chip), 1 - c if landing else c],
        send_sem=send_ref.at[j], recv_sem=recv_ref.at[j], device_id=(x, y, 1 - c), device_id_type=MESH)


def _forward_start(bufs, name):
    n = len(bufs)

    def body(*refs):
        ins = refs[:n]
        send, recv = refs[n:2 * n], refs[2 * n:3 * n]
        token = refs[4 * n]
        x, y, c, chips = _place()
        for a in range(n):
            for j, chip in enumerate(chips):
                _forward_copy(ins[a], send[a], recv[a], j, chip, x, y, c, False).start()
        token[...] = jnp.zeros_like(token)

    sems = [pltpu.SemaphoreType.DMA((N_PEER,))] * (2 * n)
    outs = pl.pallas_call(
        body, name=name,
        in_specs=[HBM] * n,
        out_specs=[SEM] * (2 * n) + [HBM] * n + [pl.BlockSpec(memory_space=pltpu.VMEM)],
        out_shape=sems + [pltpu.HBM(b.shape, b.dtype) for b in bufs] + [jax.ShapeDtypeStruct((8, 128), f32)],
        input_output_aliases={a: 2 * n + a for a in range(n)},
        compiler_params=pltpu.CompilerParams(has_side_effects=EFFECT),
    )(*[_in_hbm(b) for b in bufs])
    return list(outs[:n]), list(outs[n:2 * n]), list(outs[2 * n:3 * n]), outs[3 * n]


def _forward_wait(send, recv, bufs, after, name):
    n = len(bufs)

    def body(*refs):
        ins = refs[:n]
        send_r, recv_r = refs[n:2 * n], refs[2 * n:3 * n]
        x, y, c, chips = _place()
        for a in range(n):
            for j, chip in enumerate(chips):
                cp = _forward_copy(ins[a], send_r[a], recv_r[a], j, chip, x, y, c, True)
                cp.wait_send()
                cp.wait_recv()

    outs = pl.pallas_call(
        body, name=name,
        in_specs=[HBM] * n + [SEM] * (2 * n) + [ANY],
        out_specs=[HBM] * n,
        out_shape=[pltpu.HBM(b.shape, b.dtype) for b in bufs],
        input_output_aliases={a: a for a in range(n)},
        compiler_params=pltpu.CompilerParams(has_side_effects=EFFECT),
    )(*bufs, *send, *recv, after)
    return list(outs)


def _adamw_math(w, g, m, v):
    m = ADAM_B1 * m + (1.0 - ADAM_B1) * g
    v = ADAM_B2 * v + (1.0 - ADAM_B2) * (g * g)
    m_hat = m / (1.0 - ADAM_B1 ** ADAM_STEP)
    v_hat = v / (1.0 - ADAM_B2 ** ADAM_STEP)
    delta = -ADAM_LR * (m_hat / (jnp.sqrt(v_hat) + ADAM_EPS) + ADAM_WD * w)
    return delta, m, v


def _adamw(w, mine, theirs, m, v, qc, name):
    r, cc = w.shape
    hr = r // 2
    tr = next(hr // k for k in range(1, hr + 1)
              if hr % k == 0 and (hr // k) % 8 == 0 and (hr // k) * cc * 4 <= (3 << 19))
    nb = hr // tr

    def body(qc_ref, w_ref, a_ref, b_ref, m_ref, v_ref, g_ref, d_ref, mo_ref, vo_ref):
        g = jnp.where(pl.program_id(0) == qc_ref[1], a_ref[...], b_ref[...])
        g_ref[...] = g
        d_ref[...], mo_ref[...], vo_ref[...] = _adamw_math(w_ref[...], g, m_ref[...], v_ref[...])

    full = pl.BlockSpec((tr, cc), lambda h, i, qc_ref: (h * nb + i, 0))
    half = pl.BlockSpec((tr, cc), lambda h, i, qc_ref: (i, 0))
    return pl.pallas_call(
        body, name=name,
        grid_spec=pltpu.PrefetchScalarGridSpec(
            num_scalar_prefetch=1, grid=(2, nb),
            in_specs=[full, half, half, full, full], out_specs=[full] * 4),
        out_shape=[jax.ShapeDtypeStruct((r, cc), f32)] * 4,
        compiler_params=_cp(2),
    )(qc, w, mine, theirs, m, v)


REPL = [("ffn1_norm", 1), ("mix_norm", 1), ("b_in", 6), ("rnn_conv_b", 1), ("rg_b_a", 1), ("rg_b_x", 1),
        ("rg_lambda", 1), ("conv_dw_b", 1), ("conv_ln_g", 1), ("conv_ln_b", 1), ("conv_b_proj", 1),
        ("ffn2_norm", 1), ("final_norm", 1)]
COLSH = [("meta_tokens", NMETA), ("rnn_conv_w", KC4), ("conv_dw_w", KC31)]
SMALL = REPL + COLSH
CS = D // NCHIP


def _pack_rows():
    starts, row = {}, 0
    for k, rows in REPL:
        starts[k] = row
        row += rows
    for k, rows in COLSH:
        row = -(-row // 8) * 8
        starts[k] = row
        row += rows
    return starts, -(-row // 8) * 8


PACK_START, LOSS_ROW = _pack_rows()
SMALL_ROWS = LOSS_ROW + 8


def _small_pack(g, loss_row):
    pieces, row = [], 0
    for k, rows in SMALL:
        if PACK_START[k] > row:
            pieces.append(jnp.zeros((PACK_START[k] - row, D), f32))
        pieces.append(g[k].reshape(rows, D))
        row = PACK_START[k] + rows
    pieces.append(jnp.zeros((LOSS_ROW - row, D), f32))
    pieces.append(loss_row)
    pieces.append(jnp.zeros((SMALL_ROWS - LOSS_ROW - 1, D), f32))
    return jnp.concatenate(pieces, axis=0)


def _adamw_small(packs, ws, ms, vs):
    ns = len(SMALL)

    def body(*refs):
        pack_ref = refs[0]
        w_refs, m_refs, v_refs = refs[1:1 + ns], refs[1 + ns:1 + 2 * ns], refs[1 + 2 * ns:1 + 3 * ns]
        outs = refs[1 + 3 * ns:1 + 7 * ns]
        g_refs, d_refs, mo_refs, vo_refs = outs[:ns], outs[ns:2 * ns], outs[2 * ns:3 * ns], outs[3 * ns:]
        loss_ref = refs[1 + 7 * ns]
        gsum_sc = refs[2 + 7 * ns]
        q = 2 * lax.axis_index("x") + lax.axis_index("y")
        acc = pack_ref[0]
        for dev in range(1, 8):
            acc = acc + pack_ref[dev]
        gsum_sc[...] = acc
        loss_ref[...] = gsum_sc[LOSS_ROW:LOSS_ROW + 1, :]
        for idx, (name, rows) in enumerate(SMALL):
            row = PACK_START[name]
            if idx < len(REPL):
                for k in range(rows):
                    cols = slice(k * D, (k + 1) * D)
                    g = gsum_sc[row + k:row + k + 1, :]
                    d, mm, vv = _adamw_math(w_refs[idx][:, cols], g, m_refs[idx][:, cols], v_refs[idx][:, cols])
                    g_refs[idx][:, cols] = g
                    d_refs[idx][:, cols] = d
                    mo_refs[idx][:, cols] = mm
                    vo_refs[idx][:, cols] = vv
            else:
                g = gsum_sc[row:row + rows, pl.ds(pl.multiple_of(q * CS, CS), CS)]
                d, mm, vv = _adamw_math(w_refs[idx][...], g, m_refs[idx][...], v_refs[idx][...])
                g_refs[idx][...] = g
                d_refs[idx][...] = d
                mo_refs[idx][...] = mm
                vo_refs[idx][...] = vv

    shapes = [jax.ShapeDtypeStruct(w.shape, f32) for w in ws]
    return pl.pallas_call(
        body, name="adamw_small",
        out_shape=shapes * 4 + [jax.ShapeDtypeStruct((1, D), f32)],
        scratch_shapes=[pltpu.VMEM((SMALL_ROWS, D), f32)],
        compiler_params=pltpu.CompilerParams(vmem_limit_bytes=VMEM_LIMIT),
    )(packs, *ws, *ms, *vs)


BIG = ["ffn1_w_gu", "ffn1_w_down", "w_in", "rg_w_a", "rg_w_x", "rnn_w_proj", "conv_w_proj", "w_out",
       "ffn2_w_gu", "ffn2_w_down"]
WEIGHTS = ['meta_tokens', 'ffn1_norm', 'ffn1_w_gu', 'ffn1_w_down', 'mix_norm', 'w_in', 'b_in', 'rnn_conv_w',
           'rnn_conv_b', 'rg_w_a', 'rg_b_a', 'rg_w_x', 'rg_b_x', 'rg_lambda', 'rnn_w_proj', 'conv_dw_w',
           'conv_dw_b', 'conv_ln_g', 'conv_ln_b', 'conv_w_proj', 'conv_b_proj', 'w_out', 'ffn2_norm',
           'ffn2_w_gu', 'ffn2_w_down', 'final_norm']


def _as2d(a):
    return a.reshape(-1, a.shape[-1])


def _step(x, loss_target, w, m, v):
    seq = x.shape[1]
    n_valid = NMETA + seq
    t = -(-n_valid // TM) * TM

    qc = jnp.stack([2 * lax.axis_index("x") + lax.axis_index("y"), lax.axis_index("c")]).astype(jnp.int32)
    p = {k: w[k].reshape(1, rows * D) for k, rows in REPL}

    first = ["ffn1_w_gu", "ffn1_w_down", "small"]
    later = [["w_in"], ["rg_w_a", "rg_w_x", "rnn_w_proj", "conv_w_proj", "w_out"], ["ffn2_w_gu", "ffn2_w_down"]]
    small_rows = sum(r for _, r in COLSH)
    small = jnp.concatenate([_as2d(w[k]) for k, _ in COLSH] + [jnp.zeros((64 - small_rows, CS), f32)], axis=0)

    def cast(k, token=None):
        src, dtype = (small, f32) if k == "small" else (_as2d(w[k]), bf16)
        return _cast_into_slot(src, qc, dtype, "cast_" + k, after=token)

    send1, recv1, bufs1, token1 = _gather_start([cast(k) for k in first], "gather_start_first")
    rest = [k for grp in later for k in grp]
    send2, recv2, bufs2, token2 = _gather_start([cast(k, token1) for k in rest], "gather_start_rest")

    def install(names, done):
        for k, b in zip(names, done):
            full = b.reshape(NCHIP, 2 * b.shape[2], b.shape[3])
            if k in ("ffn1_w_down", "ffn2_w_down"):
                full = full.reshape(F, D)
            elif k in ("rnn_w_proj", "conv_w_proj", "w_out"):
                full = full.reshape(D, D)
            elif k in ("rg_w_a", "rg_w_x"):
                full = full.reshape(NCHIP, NHEAD, HD // NCHIP, HD).transpose(1, 0, 2, 3).reshape(NHEAD, HD, HD)
            p[k] = full

    def finish(names, send, recv, bufs, after, tag):
        install(names, _forward_halves(_gather_wait(send, recv, bufs, after, "gather_wait_" + tag),
                                       "gather_forward_" + tag))

    def group(names):
        idx = [rest.index(k) for k in names]
        return names, [send2[i] for i in idx], [recv2[i] for i in idx], [bufs2[i] for i in idx]

    h0 = jnp.pad(x[0], ((NMETA, t - n_valid), (0, 0)))
    tgt = jnp.pad(loss_target[0], ((NMETA, t - n_valid), (0, 0)))
    finish(first, send1, recv1, bufs1, (token2, h0, tgt), "first")
    small_full = p.pop("small").transpose(1, 0, 2).reshape(64, D)
    row = 0
    for k, rows in COLSH:
        p[k] = small_full[row:row + rows]
        row += rows

    h0 = lax.dynamic_update_slice(h0, p["meta_tokens"], (0, 0))
    h1, gate1, up1, n1 = _ffn_fwd(h0, p["ffn1_norm"], p["ffn1_w_gu"], p["ffn1_w_down"], "ffn1_fwd")
    finish(*group(later[0]), h1, "in")
    proj, n2 = _inproj_fwd(h1, p["mix_norm"], p["w_in"], p["b_in"])
    names_l = later[1] + later[2]
    _, send_l, recv_l, bufs_l = group(names_l)
    send_f, recv_f, bufs_f, token = _forward_start(
        _gather_wait(send_l, recv_l, bufs_l, proj, "gather_wait_late"), "gather_forward_start")
    vc, s = _conv_fwd(proj, p["conv_dw_w"], p["conv_dw_b"], p["conv_ln_g"], p["conv_ln_b"], after=token)
    install(names_l, _forward_wait(send_f, recv_f, bufs_f, vc, "gather_forward_wait"))
    xr, hr, z = _rnn_fwd(proj, p["rnn_conv_w"], p["rnn_conv_b"], p["rg_w_a"], p["rg_b_a"],
                         p["rg_w_x"], p["rg_b_x"], p["rg_lambda"])
    h2 = _merge_fwd(h1, z, s, proj, p["rnn_w_proj"], p["conv_w_proj"], p["conv_b_proj"], p["w_out"])
    h3, gate2, up2, n3 = _ffn_fwd(h2, p["ffn2_norm"], p["ffn2_w_gu"], p["ffn2_w_down"], "ffn2_fwd")
    dh3, loss_blk, d_final = _final_loss(h3, p["final_norm"], tgt, n_valid)

    g = {"final_norm": d_final}
    pending = []

    def exchange_start(names, tag):
        parts = []
        for k in names:
            rows = g[k].size // (NCHIP * g[k].shape[-1])
            parts.append(g[k].reshape((NCHIP, 2, rows // 2, g[k].shape[-1])))
        send, recv, parts, lands, token = _pair_exchange_start(parts, "pair_exchange_start_" + tag)
        return (names, tag, send, recv, parts, lands), token

    def reduce_start(state, after):
        names, tag, send, recv, parts, lands = state
        parts, from_sibling = _pair_exchange_wait(send, recv, parts, lands, after, "pair_exchange_wait_" + tag)
        added = [_pair_add(pp, gg, qc, "pair_add_" + k) for pp, gg, k in zip(parts, from_sibling, names)]
        send, recv, sums, lands, token = _reduce_start([a for a, _ in added], [b for _, b in added],
                                                       "reduce_start_" + tag)
        pending.append((names, tag, send, recv, sums, lands))
        return token

    dh2, dgate2, dup2, a2, df2, g["ffn2_norm"] = _ffn_bwd(
        dh3, h2, p["ffn2_norm"], gate2, up2, p["ffn2_w_gu"], p["ffn2_w_down"], "ffn2_bwd")
    g["ffn2_w_gu"] = _ffn_gu_grad(n3, dgate2, dup2, "ffn2")
    g["ffn2_w_down"] = _ffn_down_grad(a2, df2, "ffn2")
    state, token = exchange_start(["ffn2_w_gu", "ffn2_w_down"], "ffn2")

    dz, ds, dproj, dh2b, merged, dya, dyb, g["conv_b_proj"] = _merge_bwd(
        dh2, z, s, proj, p["rnn_w_proj"], p["conv_w_proj"], p["conv_b_proj"], p["w_out"], after=token)
    token = reduce_start(state, dz)
    dproj, g["conv_dw_w"], g["conv_dw_b"], g["conv_ln_g"], g["conv_ln_b"] = _conv_bwd(
        ds, vc, proj, dproj, p["conv_dw_w"], p["conv_ln_g"], p["conv_ln_b"], after=token)
    g["w_out"] = _square_grad(merged, dh2b, "dw_out")
    g["rnn_w_proj"] = _square_grad(z, dya, "dw_rnn_proj")
    g["conv_w_proj"] = _square_grad(s, dyb, "dw_conv_proj")
    (dproj, g["rg_w_a"], g["rg_w_x"], g["rnn_conv_w"], g["rnn_conv_b"], g["rg_b_a"], g["rg_b_x"],
     g["rg_lambda"]) = _rnn_bwd(dz, xr, hr, proj, dproj, p["rnn_conv_w"], p["rg_w_a"], p["rg_b_a"],
                                p["rg_w_x"], p["rg_b_x"], p["rg_lambda"])
    state, token = exchange_start(["w_out", "rnn_w_proj", "conv_w_proj", "rg_w_a", "rg_w_x"], "mix")

    dh1, g["mix_norm"], db_in = _inproj_bwd(dproj, dh2, h1, p["mix_norm"], p["w_in"], after=token)
    g["b_in"] = db_in.reshape(1, NIN)
    token = reduce_start(state, dh1)
    g["w_in"] = _tn_matmul(n2, dproj, D, NIN // NCHIP, (NCHIP, D, NIN // NCHIP),
                           (None, D, NIN // NCHIP), lambda k, nn, mm: (nn, 0, 0), "dw_in", after=token)
    state, token = exchange_start(["w_in"], "in")

    dh0, dgate1, dup1, a1, df1, g["ffn1_norm"] = _ffn_bwd(
        dh1, h0, p["ffn1_norm"], gate1, up1, p["ffn1_w_gu"], p["ffn1_w_down"], "ffn1_bwd", after=token)
    g["meta_tokens"] = dh0[0:NMETA]
    grad_x = dh0[NMETA:n_valid][None]
    token = reduce_start(state, dh0)

    send_s, recv_s, pack_buf, token_s = _gather_all_start(
        _place_pack(_small_pack(g, loss_blk.reshape(1, D)), qc), "gather_all_start")
    g["ffn1_w_down"] = _ffn_down_grad(a1, df1, "ffn1", after=(token, token_s))
    state, token = exchange_start(["ffn1_w_down"], "ffn1_down")
    g["ffn1_w_gu"] = _ffn_gu_grad(n1, dgate1, dup1, "ffn1", after=token)
    token = reduce_start(state, g["ffn1_w_gu"])
    state_gu, token = exchange_start(["ffn1_w_gu"], "ffn1_gu")
    packs = _gather_all_wait(send_s, recv_s, pack_buf, token, "gather_all_wait")

    grads, deltas, new_m, new_v = {}, {}, {}, {}

    def landed_sums(items, after):
        names, mine = [], []
        for grp_names, grp_tag, send, recv, sums, lands in items:
            landed = _reduce_wait(send, recv, sums, lands, after, "reduce_wait_" + grp_tag)
            mine += [_sum_chips(b, "sum_chips_" + k) for b, k in zip(landed, grp_names)]
            names += grp_names
            after = mine[-1]
        return names, mine

    def share_and_update(names, mine, tag, after=None):
        theirs = _pair_share(mine, "pair_share_" + tag, after=after)
        for k, mi, th in zip(names, mine, theirs):
            outs = _adamw(_as2d(w[k]), mi, th, _as2d(m[k]), _as2d(v[k]), qc, "adamw_" + k)
            grads[k], deltas[k], new_m[k], new_v[k] = (a.reshape(w[k].shape) for a in outs)
        return new_v[names[-1]]

    early_names, early_mine = landed_sums(pending[:3], packs)
    token = reduce_start(state_gu, early_mine[-1])
    after = share_and_update(early_names, early_mine, "early", after=token)
    share_and_update(*landed_sums(pending[3:], after), "late")
    names = [k for k, _ in SMALL]
    shape2 = {k: ((1, rows * D) if (k, rows) in REPL else (rows, CS)) for k, rows in SMALL}
    outs = _adamw_small(packs, *[[a[k].reshape(shape2[k]) for k in names] for a in (w, m, v)])
    ns = len(names)
    for i, k in enumerate(names):
        grads[k], deltas[k], new_m[k], new_v[k] = (outs[j * ns + i].reshape(w[k].shape) for j in range(4))

    loss = outs[4 * ns][0, 0]
    return (loss, grad_x, *[grads[k] for k in WEIGHTS], *[deltas[k] for k in WEIGHTS],
            *[new_m[k] for k in WEIGHTS], *[new_v[k] for k in WEIGHTS])


def kernel(x, meta_tokens, ffn1_norm, ffn1_w_gu, ffn1_w_down, mix_norm, w_in, b_in, rnn_conv_w, rnn_conv_b, rg_w_a, rg_b_a, rg_w_x, rg_b_x, rg_lambda, rnn_w_proj, conv_dw_w, conv_dw_b, conv_ln_g, conv_ln_b, conv_w_proj, conv_b_proj, w_out, ffn2_norm, ffn2_w_gu, ffn2_w_down, final_norm, loss_target, m_meta_tokens, m_ffn1_norm, m_ffn1_w_gu, m_ffn1_w_down, m_mix_norm, m_w_in, m_b_in, m_rnn_conv_w, m_rnn_conv_b, m_rg_w_a, m_rg_b_a, m_rg_w_x, m_rg_b_x, m_rg_lambda, m_rnn_w_proj, m_conv_dw_w, m_conv_dw_b, m_conv_ln_g, m_conv_ln_b, m_conv_w_proj, m_conv_b_proj, m_w_out, m_ffn2_norm, m_ffn2_w_gu, m_ffn2_w_down, m_final_norm, v_meta_tokens, v_ffn1_norm, v_ffn1_w_gu, v_ffn1_w_down, v_mix_norm, v_w_in, v_b_in, v_rnn_conv_w, v_rnn_conv_b, v_rg_w_a, v_rg_b_a, v_rg_w_x, v_rg_b_x, v_rg_lambda, v_rnn_w_proj, v_conv_dw_w, v_conv_dw_b, v_conv_ln_g, v_conv_ln_b, v_conv_w_proj, v_conv_b_proj, v_w_out, v_ffn2_norm, v_ffn2_w_gu, v_ffn2_w_down, v_final_norm):
    args = locals()
    w = {k: args[k] for k in WEIGHTS}
    m = {k: args["m_" + k] for k in WEIGHTS}
    v = {k: args["v_" + k] for k in WEIGHTS}
    return _step(x, loss_target, w, m, v)
```

```python
import functools

import jax
import jax.numpy as jnp
from jax import lax
from jax.experimental import pallas as pl
from jax.experimental.pallas import tpu as pltpu

f32 = jnp.float32
bf16 = jnp.bfloat16

D = 1024
F = 2816
FS = F // 2
NIN = 6 * D
NMETA = 16
NHEAD = 4
HD = D // NHEAD
KC4 = 4
KC31 = 31
HALO = 32
EPS = 1e-6
TM = 416
NCHIP = 4
MESH = pl.DeviceIdType.MESH

ADAM_LR = 0.001
ADAM_B1 = 0.9
ADAM_B2 = 0.999
ADAM_EPS = 1e-08
ADAM_WD = 0.01
ADAM_STEP = 10

VMEM_LIMIT = 56 * 1024 * 1024
FSUB = [(o, min(256, FS - o)) for o in range(0, FS, 256)]


def _cp(n_axes, **kw):
    return pltpu.CompilerParams(dimension_semantics=("arbitrary",) * n_axes,
                                vmem_limit_bytes=VMEM_LIMIT, **kw)


RESIDENT = pl.BlockSpec(memory_space=pltpu.VMEM)


def _n_after(after):
    return 0 if after is None else (len(after) if isinstance(after, (tuple, list)) else 1)


def _ordered(body, in_specs, args, after):
    if after is None:
        return body, in_specs, args
    extra = tuple(after) if isinstance(after, (tuple, list)) else (after,)
    return (lambda *refs: body(*refs[len(extra):]),
            [pl.BlockSpec(memory_space=pl.ANY)] * len(extra) + list(in_specs), extra + tuple(args))


def _nt_dot(a, b):
    return lax.dot_general(a, b, (((1,), (1,)), ((), ())), preferred_element_type=f32)


def _tn_dot(a, b):
    return lax.dot_general(a, b, (((0,), (0,)), ((), ())), preferred_element_type=f32)


def _sigmoid(x):
    return 0.5 * jnp.tanh(0.5 * x) + 0.5


def _log1p(y):
    u = 1.0 + y
    d = u - 1.0
    return jnp.where(d == 0.0, y, jnp.log(u) * (y / jnp.where(d == 0.0, 1.0, d)))


def _softplus(x):
    return jnp.maximum(x, 0.0) + _log1p(jnp.exp(-jnp.abs(x)))


def _one_minus_square(a, log_a):
    x = 2.0 * log_a
    series = x * (1.0 + x * (0.5 + x * (1.0 / 6.0)))
    return jnp.where(jnp.abs(x) < 0.03, -series, 1.0 - a * a)


_GELU_C = 0.7978845608028654
_GELU_K = 0.044715


def _gelu_and_grad(y):
    y2 = y * y
    th = jnp.tanh(_GELU_C * (y + _GELU_K * y * y2))
    gel = 0.5 * y * (1.0 + th)
    dgel = 0.5 * (1.0 + th) + 0.5 * y * (1.0 - th * th) * _GELU_C * (1.0 + 3.0 * _GELU_K * y2)
    return gel, dgel


def _rms_stats(h):
    return lax.rsqrt(jnp.mean(h * h, axis=-1, keepdims=True) + EPS)


def _rms_bwd(dn, h, g):
    r = _rms_stats(h)
    nhat = h * r
    dnh = dn * g
    dh = r * (dnh - nhat * jnp.mean(dnh * nhat, axis=-1, keepdims=True))
    dg = jnp.sum(dn * nhat, axis=0, keepdims=True)
    return dh, dg


def _row_ids(shape):
    return lax.broadcasted_iota(jnp.int32, shape, 0)


def _ffn_fwd(h, g, wgu, wd, name):
    t = h.shape[0]
    nj = 2

    def body(h_ref, g_ref, wg_ref, wd_ref, ho_ref, gate_ref, up_ref, n_ref, nb_sc, acc_sc, a_sc):
        j = pl.program_id(1)

        @pl.when(j == 0)
        def _():
            hh = h_ref[...]
            nb = (hh * _rms_stats(hh) * g_ref[...]).astype(bf16)
            nb_sc[...] = nb
            n_ref[...] = nb
            acc_sc[...] = jnp.zeros_like(acc_sc)

        nb = nb_sc[...]
        for off, width in FSUB:
            cols = slice(off, off + width)
            gt = jnp.dot(nb, wg_ref[j, :, cols], preferred_element_type=f32)
            up = jnp.dot(nb, wg_ref[2 + j, :, cols], preferred_element_type=f32)
            gate_ref[:, cols] = gt.astype(bf16)
            up_ref[:, cols] = up.astype(bf16)
            a_sc[:, cols] = (gt * _sigmoid(gt) * up).astype(bf16)
        acc_sc[...] += jnp.dot(a_sc[...], wd_ref[j], preferred_element_type=f32)

        @pl.when(j == nj - 1)
        def _():
            ho_ref[...] = h_ref[...] + 0.5 * acc_sc[...]

    tm = TM
    return pl.pallas_call(
        body, name=name, grid=(t // tm, nj),
        in_specs=[
            pl.BlockSpec((tm, D), lambda i, j: (i, 0)),
            pl.BlockSpec((1, D), lambda i, j: (0, 0)),
            RESIDENT, RESIDENT,
        ],
        out_specs=[
            pl.BlockSpec((tm, D), lambda i, j: (i, 0)),
            pl.BlockSpec((tm, FS), lambda i, j: (i, j)),
            pl.BlockSpec((tm, FS), lambda i, j: (i, j)),
            pl.BlockSpec((tm, D), lambda i, j: (i, 0)),
        ],
        out_shape=[
            jax.ShapeDtypeStruct((t, D), f32),
            jax.ShapeDtypeStruct((t, F), bf16),
            jax.ShapeDtypeStruct((t, F), bf16),
            jax.ShapeDtypeStruct((t, D), bf16),
        ],
        scratch_shapes=[pltpu.VMEM((tm, D), bf16), pltpu.VMEM((tm, D), f32), pltpu.VMEM((tm, FS), bf16)],
        compiler_params=_cp(2),
    )(h, g, wgu, wd.reshape(nj, FS, D))


def _inproj_fwd(h, g, win, b_in):
    t = h.shape[0]
    tn = NIN // NCHIP
    nj = NIN // tn
    per = (NIN // NCHIP) // tn

    def body(h_ref, g_ref, w_ref, b_ref, proj_ref, n_ref, nb_sc):
        j = pl.program_id(1)

        @pl.when(j == 0)
        def _():
            hh = h_ref[...]
            nb = (hh * _rms_stats(hh) * g_ref[...]).astype(bf16)
            nb_sc[...] = nb
            n_ref[...] = nb

        proj_ref[...] = jnp.dot(nb_sc[...], w_ref[j], preferred_element_type=f32) + b_ref[...]

    return pl.pallas_call(
        body, name="inproj_fwd", grid=(t // TM, nj),
        in_specs=[
            pl.BlockSpec((TM, D), lambda i, j: (i, 0)),
            pl.BlockSpec((1, D), lambda i, j: (0, 0)),
            RESIDENT,
            pl.BlockSpec((1, tn), lambda i, j: (0, j)),
        ],
        out_specs=[
            pl.BlockSpec((TM, tn), lambda i, j: (i, j)),
            pl.BlockSpec((TM, D), lambda i, j: (i, 0)),
        ],
        out_shape=[jax.ShapeDtypeStruct((t, NIN), f32), jax.ShapeDtypeStruct((t, D), bf16)],
        scratch_shapes=[pltpu.VMEM((TM, D), bf16)],
        compiler_params=_cp(2),
    )(h, g, win, b_in)


def _block_gates(xr, wa_ref, ba, wx_ref, bx, lam):
    xrb = xr.astype(bf16)
    pa = jnp.concatenate([jnp.dot(xrb[:, hh * HD:(hh + 1) * HD], wa_ref[hh], preferred_element_type=f32)
                          for hh in range(NHEAD)], axis=1)
    px = jnp.concatenate([jnp.dot(xrb[:, hh * HD:(hh + 1) * HD], wx_ref[hh], preferred_element_type=f32)
                          for hh in range(NHEAD)], axis=1)
    ra = _sigmoid(pa + ba)
    ii = _sigmoid(px + bx)
    sp = _softplus(-lam)
    log_a = -8.0 * ra * sp
    a = jnp.exp(log_a)
    sq = jnp.sqrt(_one_minus_square(a, log_a))
    return ra, ii, a, sq, sp


def _rnn_fwd(proj, cw, cb, wa, ba, wx, bx, lam):
    t = proj.shape[0]
    ng = TM // 8

    def body(x_ref, y_ref, cw_ref, cb_ref, wa_ref, ba_ref, wx_ref, bx_ref, lam_ref,
             xr_ref, hr_ref, z_ref, xext_sc, carry_sc, a_sc, h_sc):
        i = pl.program_id(0)

        @pl.when(i == 0)
        def _():
            xext_sc[0:8, :] = jnp.zeros((8, D), f32)
            carry_sc[...] = jnp.zeros_like(carry_sc)

        x = x_ref[...]
        xext_sc[8:8 + TM, :] = x
        xe = xext_sc[...]
        xr = cb_ref[...] + cw_ref[KC4 - 1:KC4, :] * x
        for k in range(KC4 - 1):
            xr = xr + cw_ref[k:k + 1, :] * pltpu.roll(xe, KC4 - 1 - k, 0)[8:8 + TM]
        xext_sc[0:8, :] = x[TM - 8:TM]

        _, ii, a, sq, _ = _block_gates(xr, wa_ref, ba_ref[...], wx_ref, bx_ref[...], lam_ref[...])
        a_sc[...] = a
        h_sc[...] = sq * ii * xr
        row = _row_ids((8, D))

        def group(r, carry):
            off = pl.multiple_of(r * 8, 8)
            aa = a_sc[pl.ds(off, 8), :]
            hh = h_sc[pl.ds(off, 8), :]
            for s in (1, 2, 4):
                a_sh = jnp.where(row >= s, pltpu.roll(aa, s, 0), 1.0)
                h_sh = jnp.where(row >= s, pltpu.roll(hh, s, 0), 0.0)
                hh = aa * h_sh + hh
                aa = aa * a_sh
            hh = hh + aa * carry
            h_sc[pl.ds(off, 8), :] = hh
            return hh[7:8, :]

        carry_sc[...] = lax.fori_loop(0, ng, group, carry_sc[...])
        hr = h_sc[...]
        gel, _ = _gelu_and_grad(y_ref[...])
        xr_ref[...] = xr
        hr_ref[...] = hr
        z_ref[...] = (hr * gel).astype(bf16)

    vec = pl.BlockSpec((1, D), lambda i: (0, 0))
    return pl.pallas_call(
        body, name="rnn_fwd", grid=(t // TM,),
        in_specs=[
            pl.BlockSpec((TM, D), lambda i: (i, 0)),
            pl.BlockSpec((TM, D), lambda i: (i, 1)),
            pl.BlockSpec((KC4, D), lambda i: (0, 0)),
            vec,
            pl.BlockSpec((NHEAD, HD, HD), lambda i: (0, 0, 0)),
            vec,
            pl.BlockSpec((NHEAD, HD, HD), lambda i: (0, 0, 0)),
            vec, vec,
        ],
        out_specs=[pl.BlockSpec((TM, D), lambda i: (i, 0))] * 3,
        out_shape=[jax.ShapeDtypeStruct((t, D), f32), jax.ShapeDtypeStruct((t, D), f32),
                   jax.ShapeDtypeStruct((t, D), bf16)],
        scratch_shapes=[pltpu.VMEM((TM + 8, D), f32), pltpu.VMEM((1, D), f32),
                        pltpu.VMEM((TM, D), f32), pltpu.VMEM((TM, D), f32)],
        compiler_params=_cp(1),
    )(proj, proj, cw, cb, wa, ba, wx, bx, lam)


def _ln_stats(vc):
    mu = jnp.mean(vc, axis=-1, keepdims=True)
    xc = vc - mu
    rstd = lax.rsqrt(jnp.mean(xc * xc, axis=-1, keepdims=True) + EPS)
    return xc * rstd, rstd


def _conv_fwd(proj, w31, b31, ln_g, ln_b, after=None):
    t = proj.shape[0]

    def body(gv_ref, gg_ref, w_ref, b_ref, lg_ref, lb_ref, vc_ref, s_ref, vext_sc):
        i = pl.program_id(0)

        @pl.when(i == 0)
        def _():
            vext_sc[0:HALO, :] = jnp.zeros((HALO, D), f32)

        v = gv_ref[...] * _sigmoid(gg_ref[...])
        vext_sc[HALO:HALO + TM, :] = v
        ve = vext_sc[...]
        acc = jnp.zeros((TM, D), f32) + b_ref[...]
        for s in range(8):
            vs = ve if s == 0 else pltpu.roll(ve, s, 0)
            for m in range(HALO // 8):
                k = KC31 - 1 - (8 * m + s)
                if 0 <= k < KC31:
                    acc = acc + w_ref[k:k + 1, :] * vs[HALO - 8 * m:HALO - 8 * m + TM]
        vext_sc[0:HALO, :] = v[TM - HALO:TM]
        xhat, _ = _ln_stats(acc)
        ln = xhat * lg_ref[...] + lb_ref[...]
        vc_ref[...] = acc
        s_ref[...] = (ln * _sigmoid(ln)).astype(bf16)

    vec = pl.BlockSpec((1, D), lambda i: (0, 0))
    body, in_specs, args = _ordered(
        body,
        [pl.BlockSpec((TM, D), lambda i: (i, 2)),
         pl.BlockSpec((TM, D), lambda i: (i, 3)),
         pl.BlockSpec((KC31, D), lambda i: (0, 0)),
         vec, vec, vec],
        (proj, proj, w31, b31, ln_g, ln_b), after)
    return pl.pallas_call(
        body, name="conv_fwd", grid=(t // TM,),
        in_specs=in_specs,
        out_specs=[pl.BlockSpec((TM, D), lambda i: (i, 0))] * 2,
        out_shape=[jax.ShapeDtypeStruct((t, D), f32), jax.ShapeDtypeStruct((t, D), bf16)],
        scratch_shapes=[pltpu.VMEM((TM + HALO, D), f32)],
        compiler_params=_cp(1),
    )(*args)


def _merge_fwd(h, z, s, proj, wrp, wcp, bcp, wout):
    t = h.shape[0]

    def body(h_ref, z_ref, s_ref, ga_ref, gb_ref, wrp_ref, wcp_ref, bcp_ref, wout_ref, ho_ref):
        ya = jnp.dot(z_ref[...], wrp_ref[...], preferred_element_type=f32)
        yb = jnp.dot(s_ref[...], wcp_ref[...], preferred_element_type=f32) + bcp_ref[...]
        merged = _sigmoid(ga_ref[...]) * ya + _sigmoid(gb_ref[...]) * yb
        ho_ref[...] = h_ref[...] + jnp.dot(merged.astype(bf16), wout_ref[...], preferred_element_type=f32)

    row = pl.BlockSpec((TM, D), lambda i: (i, 0))
    wsq = pl.BlockSpec((D, D), lambda i: (0, 0))
    return pl.pallas_call(
        body, name="merge_fwd", grid=(t // TM,),
        in_specs=[row, row, row,
                  pl.BlockSpec((TM, D), lambda i: (i, 4)),
                  pl.BlockSpec((TM, D), lambda i: (i, 5)),
                  wsq, wsq, pl.BlockSpec((1, D), lambda i: (0, 0)), wsq],
        out_specs=row,
        out_shape=jax.ShapeDtypeStruct((t, D), f32),
        compiler_params=_cp(1),
    )(h, z, s, proj, proj, wrp, wcp, bcp, wout)


def _final_loss(h, g, tgt, n_valid):
    t = h.shape[0]

    def body(h_ref, g_ref, t_ref, dh_ref, loss_ref, dg_ref):
        i = pl.program_id(0)

        @pl.when(i == 0)
        def _():
            loss_ref[...] = jnp.zeros_like(loss_ref)
            dg_ref[...] = jnp.zeros_like(dg_ref)

        hh = h_ref[...]
        gg = g_ref[...]
        row = i * TM + _row_ids((TM, 1))
        valid = jnp.logical_and(row >= NMETA, row < n_valid)
        out = hh * _rms_stats(hh) * gg
        err = jnp.where(valid, out - t_ref[...], 0.0)
        loss_ref[...] += 0.5 * jnp.sum(err * err) * (1.0 / D)
        dh, dg = _rms_bwd(err * (1.0 / D), hh, gg)
        dh_ref[...] = dh
        dg_ref[...] += dg

    row_spec = pl.BlockSpec((TM, D), lambda i: (i, 0))
    return pl.pallas_call(
        body, name="final_loss", grid=(t // TM,),
        in_specs=[row_spec, pl.BlockSpec((1, D), lambda i: (0, 0)), row_spec],
        out_specs=[row_spec, pl.BlockSpec((8, 128), lambda i: (0, 0)), pl.BlockSpec((1, D), lambda i: (0, 0))],
        out_shape=[jax.ShapeDtypeStruct((t, D), f32), jax.ShapeDtypeStruct((8, 128), f32),
                   jax.ShapeDtypeStruct((1, D), f32)],
        compiler_params=_cp(1),
    )(h, g, tgt)


def _ffn_bwd(dh, h, g, gate, up, wgu, wd, name, after=None):
    t = h.shape[0]
    nj = 2

    def body(dh_ref, h_ref, g_ref, gate_ref, up_ref, wg_ref, wd_ref,
             dhi_ref, dgate_ref, dup_ref, a_ref, df_ref, dg_ref, dfb_sc, dn_sc):
        i = pl.program_id(0)
        j = pl.program_id(1)

        @pl.when(jnp.logical_and(i == 0, j == 0))
        def _():
            dg_ref[...] = jnp.zeros_like(dg_ref)

        @pl.when(j == 0)
        def _():
            dfb = (0.5 * dh_ref[...]).astype(bf16)
            dfb_sc[...] = dfb
            df_ref[...] = dfb
            dn_sc[...] = jnp.zeros_like(dn_sc)

        dfb = dfb_sc[...]
        for off, width in FSUB:
            cols = slice(off, off + width)
            da = _nt_dot(dfb, wd_ref[j, cols, :])
            gt = gate_ref[:, cols].astype(f32)
            uu = up_ref[:, cols].astype(f32)
            sg = _sigmoid(gt)
            silu = gt * sg
            a_ref[:, cols] = (silu * uu).astype(bf16)
            dgate_ref[:, cols] = (da * uu * (sg * (1.0 + gt * (1.0 - sg)))).astype(bf16)
            dup_ref[:, cols] = (da * silu).astype(bf16)
        dn_sc[...] += _nt_dot(dgate_ref[...], wg_ref[j]) + _nt_dot(dup_ref[...], wg_ref[2 + j])

        @pl.when(j == nj - 1)
        def _():
            dhin, dg = _rms_bwd(dn_sc[...], h_ref[...], g_ref[...])
            dhi_ref[...] = dh_ref[...] + dhin
            dg_ref[...] += dg

    rowd = pl.BlockSpec((TM, D), lambda i, j: (i, 0))
    rowf = pl.BlockSpec((TM, FS), lambda i, j: (i, j))
    vec = pl.BlockSpec((1, D), lambda i, j: (0, 0))
    body, in_specs, args = _ordered(
        body,
        [rowd, rowd, vec, rowf, rowf,
         RESIDENT, RESIDENT],
        (dh, h, g, gate, up, wgu, wd.reshape(nj, FS, D)), after)
    return pl.pallas_call(
        body, name=name, grid=(t // TM, nj),
        in_specs=in_specs,
        out_specs=[rowd, rowf, rowf, rowf, rowd, vec],
        out_shape=[jax.ShapeDtypeStruct((t, D), f32), jax.ShapeDtypeStruct((t, F), bf16),
                   jax.ShapeDtypeStruct((t, F), bf16), jax.ShapeDtypeStruct((t, F), bf16),
                   jax.ShapeDtypeStruct((t, D), bf16), jax.ShapeDtypeStruct((1, D), f32)],
        scratch_shapes=[pltpu.VMEM((TM, D), bf16), pltpu.VMEM((TM, D), f32)],
        compiler_params=_cp(2),
    )(*args)


def _big_tile(t):
    return max(k * TM for k in range(1, 6) if t % (k * TM) == 0)


ANY_SPEC = pl.BlockSpec(memory_space=pl.ANY)


def _tn_matmul(a, b, tk, tn, out_shape, out_block, out_map, name, base=None, after=None):
    t, kk = a.shape
    _, nn = b.shape
    tmm = _big_tile(t)
    nm = t // tmm

    def body(a_ref, b_ref, o_ref, acc_sc):
        m = pl.program_id(2)

        @pl.when(m == 0)
        def _():
            acc_sc[...] = jnp.zeros_like(acc_sc)

        acc_sc[...] += _tn_dot(a_ref[...], b_ref[...])

        @pl.when(m == nm - 1)
        def _():
            o_ref[...] = acc_sc[...].astype(o_ref.dtype)

    in_specs = [pl.BlockSpec((tmm, tk), lambda k, n, m: (m, k)),
                pl.BlockSpec((tmm, tn), lambda k, n, m: (m, n))]
    args, aliases = (a, b), {}
    if base is not None:
        body = (lambda inner: lambda a_ref, b_ref, base_ref, o_ref, acc_sc: inner(a_ref, b_ref, o_ref, acc_sc))(body)
        in_specs, args, aliases = in_specs + [ANY_SPEC], (a, b, base), {2: 0}
    if after is not None:
        body, in_specs, args = _ordered(body, in_specs, args, after)
        aliases = {k + _n_after(after): v for k, v in aliases.items()}
    return pl.pallas_call(
        body, name=name, grid=(kk // tk, nn // tn, nm),
        in_specs=in_specs,
        out_specs=pl.BlockSpec(out_block, out_map),
        out_shape=jax.ShapeDtypeStruct(out_shape, bf16),
        scratch_shapes=[pltpu.VMEM((tk, tn), f32)],
        input_output_aliases=aliases,
        compiler_params=_cp(3),
    )(*args)


def _merge_bwd(dh, z, s, proj, wrp, wcp, bcp, wout, after=None):
    t = dh.shape[0]

    def body(dh_ref, z_ref, s_ref, ga_ref, gb_ref, wrp_ref, wcp_ref, bcp_ref, wout_ref,
             dz_ref, ds_ref, dgab_ref, dhb_ref, mg_ref, dya_ref, dyb_ref, dbcp_ref):
        i = pl.program_id(0)

        @pl.when(i == 0)
        def _():
            dbcp_ref[...] = jnp.zeros_like(dbcp_ref)

        dhb = dh_ref[...].astype(bf16)
        dhb_ref[...] = dhb
        dmg = _nt_dot(dhb, wout_ref[...])
        ya = jnp.dot(z_ref[...], wrp_ref[...], preferred_element_type=f32)
        yb = jnp.dot(s_ref[...], wcp_ref[...], preferred_element_type=f32) + bcp_ref[...]
        sa = _sigmoid(ga_ref[...])
        sb = _sigmoid(gb_ref[...])
        mg_ref[...] = (sa * ya + sb * yb).astype(bf16)
        dgab_ref[:, 0:D] = (dmg * ya * sa * (1.0 - sa)).astype(bf16)
        dgab_ref[:, D:2 * D] = (dmg * yb * sb * (1.0 - sb)).astype(bf16)
        dya = dmg * sa
        dyb = dmg * sb
        dbcp_ref[...] += jnp.sum(dyb, axis=0, keepdims=True)
        dyab = dya.astype(bf16)
        dybb = dyb.astype(bf16)
        dya_ref[...] = dyab
        dyb_ref[...] = dybb
        dz_ref[...] = _nt_dot(dyab, wrp_ref[...])
        ds_ref[...] = _nt_dot(dybb, wcp_ref[...])

    row = pl.BlockSpec((TM, D), lambda i: (i, 0))
    wsq = pl.BlockSpec((D, D), lambda i: (0, 0))
    vec = pl.BlockSpec((1, D), lambda i: (0, 0))
    rowb = jax.ShapeDtypeStruct((t, D), bf16)
    body, in_specs, args = _ordered(
        body,
        [row, row, row,
         pl.BlockSpec((TM, D), lambda i: (i, 4)),
         pl.BlockSpec((TM, D), lambda i: (i, 5)),
         wsq, wsq, vec, wsq],
        (dh, z, s, proj, proj, wrp, wcp, bcp, wout), after)
    return pl.pallas_call(
        body, name="merge_bwd", grid=(t // TM,),
        in_specs=in_specs,
        out_specs=[row, row,
                   pl.BlockSpec((TM, 2 * D), lambda i: (i, 2)),
                   row, row, row, row, vec],
        out_shape=[jax.ShapeDtypeStruct((t, D), f32), jax.ShapeDtypeStruct((t, D), f32),
                   jax.ShapeDtypeStruct((t, NIN), bf16),
                   rowb, rowb, rowb, rowb, jax.ShapeDtypeStruct((1, D), f32)],
        compiler_params=_cp(1),
    )(*args)


def _conv_bwd(ds, vc, proj, dproj, w31, ln_g, ln_b, after=None):
    t = ds.shape[0]
    nt = t // TM
    hb = TM // HALO

    rb = 16
    nb = TM // rb
    taps = [(KC31 - 1 - (8 * m + s), s, m) for s in range(8) for m in range(HALO // 8)
            if 0 <= KC31 - 1 - (8 * m + s) < KC31]

    def groups(a):
        return jnp.sum(a.reshape(rb // 8, 8, D), axis=0)

    def body(ds_ref, vc_ref, gv_ref, gg_ref, gvp_ref, ggp_ref, dpin_ref, w_ref, lg_ref, lb_ref,
             dgvg_ref, dw_ref, db_ref, dlg_ref, dlb_ref, dext_sc, vext_sc, rot_sc, dwacc_sc, small_sc, wb_sc):
        del dpin_ref
        i = pl.program_id(0)
        tile = nt - 1 - i

        @pl.when(i == 0)
        def _():
            dext_sc[TM:TM + HALO, :] = jnp.zeros((HALO, D), f32)
            dwacc_sc[...] = jnp.zeros_like(dwacc_sc)
            small_sc[...] = jnp.zeros_like(small_sc)

        lg = lg_ref[...]
        lb = lb_ref[...]

        xhat, rstd = _ln_stats(vc_ref[...])
        ln = xhat * lg + lb
        sg = _sigmoid(ln)
        dln = ds_ref[...] * (sg * (1.0 + ln * (1.0 - sg)))
        dxh = dln * lg
        dvc = rstd * (dxh - jnp.mean(dxh, axis=-1, keepdims=True)
                      - xhat * jnp.mean(dxh * xhat, axis=-1, keepdims=True))
        small_sc[0] += jnp.sum((dln * xhat).reshape(TM // 8, 8, D), axis=0)
        small_sc[1] += jnp.sum(dln.reshape(TM // 8, 8, D), axis=0)
        small_sc[2] += jnp.sum(dvc.reshape(TM // 8, 8, D), axis=0)
        dext_sc[0:TM, :] = dvc
        vext_sc[HALO:HALO + TM, :] = gv_ref[...] * _sigmoid(gg_ref[...])
        vext_sc[0:HALO, :] = jnp.where(tile > 0, gvp_ref[...] * _sigmoid(ggp_ref[...]), 0.0)

        @pl.when(i == 0)
        def _():
            for k in range(KC31):
                wb_sc[k] = jnp.broadcast_to(w_ref[k:k + 1, :], (8, D))

        for s in range(1, 8):
            rot_sc[s - 1] = pltpu.roll(dext_sc[...], TM + HALO - s, 0)

        def dv_block(b, carry):
            rows = pl.ds(pl.multiple_of(b * rb, rb), rb)
            acc = jnp.zeros((rb, D), f32)
            for k, s, m in taps:
                src = pl.ds(pl.multiple_of(b * rb + 8 * m, 8), rb)
                slab = dext_sc[src, :] if s == 0 else rot_sc[s - 1, src, :]
                acc = acc + (slab.reshape(rb // 8, 8, D) * wb_sc[k]).reshape(rb, D)
            sgg = _sigmoid(gg_ref[rows, :])
            dgvg_ref[rows, 0:D] = (acc * sgg).astype(bf16)
            dgvg_ref[rows, D:2 * D] = (acc * gv_ref[rows, :] * sgg * (1.0 - sgg)).astype(bf16)
            return carry

        lax.fori_loop(0, nb, dv_block, 0)

        for s in range(1, 8):
            rot_sc[s - 1] = pltpu.roll(vext_sc[...], s, 0)
        for first in range(0, len(taps), 3):
            trio = taps[first:first + 3]

            def dw_block(b, accs, trio=trio):
                rows = pl.ds(pl.multiple_of(b * rb, rb), rb)
                dvc_blk = dext_sc[rows, :]
                out = []
                for acc, (k, s, m) in zip(accs, trio):
                    src = pl.ds(pl.multiple_of(b * rb + HALO - 8 * m, 8), rb)
                    slab = vext_sc[src, :] if s == 0 else rot_sc[s - 1, src, :]
                    out.append(acc + groups(dvc_blk * slab))
                return tuple(out)

            sums = lax.fori_loop(0, nb, dw_block, tuple(jnp.zeros((8, D), f32) for _ in trio))
            for acc, (k, s, m) in zip(sums, trio):
                dwacc_sc[k] += acc
        dext_sc[TM:TM + HALO, :] = dext_sc[0:HALO, :]

        @pl.when(i == nt - 1)
        def _():
            for k in range(KC31):
                dw_ref[k:k + 1, :] = jnp.sum(dwacc_sc[k], axis=0, keepdims=True)
            dlg_ref[...] = jnp.sum(small_sc[0], axis=0, keepdims=True)
            dlb_ref[...] = jnp.sum(small_sc[1], axis=0, keepdims=True)
            db_ref[...] = jnp.sum(small_sc[2], axis=0, keepdims=True)

    rev = lambda i: (nt - 1 - i, 0)
    vec = pl.BlockSpec((1, D), lambda i: (0, 0))
    halo_row = lambda i: jnp.maximum((nt - 1 - i) * hb - 1, 0)
    body, in_specs, args = _ordered(
        body,
        [pl.BlockSpec((TM, D), rev),
         pl.BlockSpec((TM, D), rev),
         pl.BlockSpec((TM, D), lambda i: (nt - 1 - i, 2)),
         pl.BlockSpec((TM, D), lambda i: (nt - 1 - i, 3)),
         pl.BlockSpec((HALO, D), lambda i: (halo_row(i), 2)),
         pl.BlockSpec((HALO, D), lambda i: (halo_row(i), 3)),
         pl.BlockSpec(memory_space=pl.ANY),
         pl.BlockSpec((KC31, D), lambda i: (0, 0)),
         vec, vec],
        (ds, vc, proj, proj, proj, proj, dproj, w31, ln_g, ln_b), after)
    return pl.pallas_call(
        body, name="conv_bwd", grid=(nt,),
        in_specs=in_specs,
        out_specs=[
            pl.BlockSpec((TM, 2 * D), lambda i: (nt - 1 - i, 1)),
            pl.BlockSpec((KC31, D), lambda i: (0, 0)),
            vec, vec, vec,
        ],
        out_shape=[jax.ShapeDtypeStruct((t, NIN), bf16),
                   jax.ShapeDtypeStruct((KC31, D), f32),
                   jax.ShapeDtypeStruct((1, D), f32), jax.ShapeDtypeStruct((1, D), f32),
                   jax.ShapeDtypeStruct((1, D), f32)],
        scratch_shapes=[pltpu.VMEM((TM + HALO, D), f32), pltpu.VMEM((TM + HALO, D), f32),
                        pltpu.VMEM((7, TM + HALO, D), f32), pltpu.VMEM((KC31, 8, D), f32),
                        pltpu.VMEM((3, 8, D), f32), pltpu.VMEM((KC31, 8, D), f32)],
        input_output_aliases={6 + _n_after(after): 0},
        compiler_params=_cp(1),
    )(*args)


def _rnn_bwd(dz, xr, hr, proj, dproj, cw, wa, ba, wx, bx, lam):
    t = dz.shape[0]
    nt = t // TM
    ng = TM // 8
    hq = HD // NCHIP

    def body(dz_ref, xr_ref, hr_ref, hrp_ref, x_ref, xp_ref, y_ref, dpin_ref,
             cw_ref, wa_ref, ba_ref, wx_ref, bx_ref, lam_ref,
             dxy_ref, dwa_ref, dwx_ref, dcw_ref, dcb_ref, dba_ref, dbx_ref, dlam_ref,
             anext_sc, gcarry_sc, dext_sc, xext_sc, m_sc, g_sc, dwa_sc, dwx_sc, dsp_sc):
        del dpin_ref
        i = pl.program_id(0)
        tile = nt - 1 - i

        @pl.when(i == 0)
        def _():
            anext_sc[...] = jnp.zeros_like(anext_sc)
            gcarry_sc[...] = jnp.zeros_like(gcarry_sc)
            dext_sc[TM:TM + 8, :] = jnp.zeros((8, D), f32)
            dwa_sc[...] = jnp.zeros_like(dwa_sc)
            dwx_sc[...] = jnp.zeros_like(dwx_sc)
            dsp_sc[...] = jnp.zeros_like(dsp_sc)
            dcw_ref[...] = jnp.zeros_like(dcw_ref)
            dcb_ref[...] = jnp.zeros_like(dcb_ref)
            dba_ref[...] = jnp.zeros_like(dba_ref)
            dbx_ref[...] = jnp.zeros_like(dbx_ref)

        xr = xr_ref[...]
        hr = hr_ref[...]
        dz = dz_ref[...]
        gel, dgel = _gelu_and_grad(y_ref[...])
        dxy_ref[:, D:2 * D] = (dz * hr * dgel).astype(bf16)
        ra, ii, a, sq, sp = _block_gates(xr, wa_ref, ba_ref[...], wx_ref, bx_ref[...], lam_ref[...])

        row = _row_ids((TM, D))
        m_sc[...] = jnp.where(row == TM - 1, anext_sc[...], pltpu.roll(a, TM - 1, 0))
        anext_sc[...] = a[0:1, :]
        g_sc[...] = dz * gel
        row8 = _row_ids((8, D))

        def group(qq, carry):
            off = pl.multiple_of((ng - 1 - qq) * 8, 8)
            mm = m_sc[pl.ds(off, 8), :]
            dd = g_sc[pl.ds(off, 8), :]
            for s in (1, 2, 4):
                m_sh = jnp.where(row8 < 8 - s, pltpu.roll(mm, 8 - s, 0), 1.0)
                d_sh = jnp.where(row8 < 8 - s, pltpu.roll(dd, 8 - s, 0), 0.0)
                dd = dd + mm * d_sh
                mm = mm * m_sh
            dd = dd + mm * carry
            g_sc[pl.ds(off, 8), :] = dd
            return dd[0:1, :]

        gcarry_sc[...] = lax.fori_loop(0, ng, group, gcarry_sc[...])
        gg = g_sc[...]

        hlast = jnp.where(tile > 0, hrp_ref[7:8, :], 0.0)
        hprev = jnp.where(row == 0, hlast, pltpu.roll(hr, 1, 0))
        d_a = gg * hprev
        dsq = gg * ii * xr
        dii = gg * sq * xr
        dxr = gg * sq * ii
        dlog = d_a * a - dsq * (a * a / sq)
        dsp_sc[...] += jnp.sum(dlog * (-8.0 * ra), axis=0, keepdims=True)
        dpa = dlog * (-8.0 * sp) * ra * (1.0 - ra)
        dpx = dii * ii * (1.0 - ii)
        dba_ref[...] += jnp.sum(dpa, axis=0, keepdims=True)
        dbx_ref[...] += jnp.sum(dpx, axis=0, keepdims=True)
        dpab = dpa.astype(bf16)
        dpxb = dpx.astype(bf16)
        xrb = xr.astype(bf16)
        back = []
        for hh in range(NHEAD):
            cols = slice(hh * HD, (hh + 1) * HD)
            back.append(_nt_dot(dpab[:, cols], wa_ref[hh]) + _nt_dot(dpxb[:, cols], wx_ref[hh]))
            dwa_sc[hh] += _tn_dot(xrb[:, cols], dpab[:, cols])
            dwx_sc[hh] += _tn_dot(xrb[:, cols], dpxb[:, cols])
        dxr = dxr + jnp.concatenate(back, axis=1)

        dext_sc[0:TM, :] = dxr
        de = dext_sc[...]
        dx = cw_ref[KC4 - 1:KC4, :] * dxr
        for k in range(KC4 - 1):
            dx = dx + cw_ref[k:k + 1, :] * pltpu.roll(de, TM + 8 - (KC4 - 1 - k), 0)[0:TM]
        dext_sc[TM:TM + 8, :] = dxr[0:8]
        dxy_ref[:, 0:D] = dx.astype(bf16)

        x = x_ref[...]
        xext_sc[0:8, :] = jnp.where(tile > 0, xp_ref[...], 0.0)
        xext_sc[8:8 + TM, :] = x
        xe = xext_sc[...]
        dcw_ref[KC4 - 1:KC4, :] += jnp.sum(dxr * x, axis=0, keepdims=True)
        for k in range(KC4 - 1):
            xs = pltpu.roll(xe, KC4 - 1 - k, 0)[8:8 + TM]
            dcw_ref[k:k + 1, :] += jnp.sum(dxr * xs, axis=0, keepdims=True)
        dcb_ref[...] += jnp.sum(dxr, axis=0, keepdims=True)

        @pl.when(i == nt - 1)
        def _():
            for hh in range(NHEAD):
                for qc in range(NCHIP):
                    dwa_ref[qc, hh] = dwa_sc[hh, qc * hq:(qc + 1) * hq, :].astype(bf16)
                    dwx_ref[qc, hh] = dwx_sc[hh, qc * hq:(qc + 1) * hq, :].astype(bf16)
            dlam_ref[...] = -dsp_sc[...] * _sigmoid(-lam_ref[...])

    rev = lambda i: (nt - 1 - i, 0)
    vec = pl.BlockSpec((1, D), lambda i: (0, 0))
    prev8 = lambda i: jnp.maximum((nt - 1 - i) * ng - 1, 0)
    wblk = pl.BlockSpec((NHEAD, HD, HD), lambda i: (0, 0, 0))
    gblk = pl.BlockSpec((NCHIP, NHEAD, hq, HD), lambda i: (0, 0, 0, 0))
    return pl.pallas_call(
        body, name="rnn_bwd", grid=(nt,),
        in_specs=[
            pl.BlockSpec((TM, D), rev),
            pl.BlockSpec((TM, D), rev),
            pl.BlockSpec((TM, D), rev),
            pl.BlockSpec((8, D), lambda i: (prev8(i), 0)),
            pl.BlockSpec((TM, D), lambda i: (nt - 1 - i, 0)),
            pl.BlockSpec((8, D), lambda i: (prev8(i), 0)),
            pl.BlockSpec((TM, D), lambda i: (nt - 1 - i, 1)),
            pl.BlockSpec(memory_space=pl.ANY),
            pl.BlockSpec((KC4, D), lambda i: (0, 0)),
            wblk, vec, wblk, vec, vec,
        ],
        out_specs=[
            pl.BlockSpec((TM, 2 * D), lambda i: (nt - 1 - i, 0)),
            gblk, gblk,
            pl.BlockSpec((KC4, D), lambda i: (0, 0)),
            vec, vec, vec, vec,
        ],
        out_shape=[jax.ShapeDtypeStruct((t, NIN), bf16),
                   jax.ShapeDtypeStruct((NCHIP, NHEAD, hq, HD), bf16),
                   jax.ShapeDtypeStruct((NCHIP, NHEAD, hq, HD), bf16),
                   jax.ShapeDtypeStruct((KC4, D), f32),
                   jax.ShapeDtypeStruct((1, D), f32), jax.ShapeDtypeStruct((1, D), f32),
                   jax.ShapeDtypeStruct((1, D), f32), jax.ShapeDtypeStruct((1, D), f32)],
        scratch_shapes=[pltpu.VMEM((1, D), f32), pltpu.VMEM((1, D), f32),
                        pltpu.VMEM((TM + 8, D), f32), pltpu.VMEM((TM + 8, D), f32),
                        pltpu.VMEM((TM, D), f32), pltpu.VMEM((TM, D), f32),
                        pltpu.VMEM((NHEAD, HD, HD), f32), pltpu.VMEM((NHEAD, HD, HD), f32),
                        pltpu.VMEM((1, D), f32)],
        input_output_aliases={7: 0},
        compiler_params=_cp(1),
    )(dz, xr, hr, hr, proj, proj, proj, dproj, cw, wa, ba, wx, bx, lam)


def _inproj_bwd(dproj, dh, h, g, win, after=None):
    t = h.shape[0]
    tn = NIN // NCHIP
    nj = NIN // tn
    per = (NIN // NCHIP) // tn

    def body(dp_ref, dh_ref, h_ref, g_ref, w_ref, dhi_ref, dg_ref, db_ref, dn_sc):
        i = pl.program_id(0)
        j = pl.program_id(1)

        @pl.when(jnp.logical_and(i == 0, j == 0))
        def _():
            dg_ref[...] = jnp.zeros_like(dg_ref)
            db_ref[...] = jnp.zeros_like(db_ref)

        @pl.when(j == 0)
        def _():
            dn_sc[...] = jnp.zeros_like(dn_sc)

        dp = dp_ref[...]
        dn_sc[...] += _nt_dot(dp, w_ref[j])
        db_ref[j] += jnp.sum(dp.astype(f32), axis=0, keepdims=True)

        @pl.when(j == nj - 1)
        def _():
            dhin, dg = _rms_bwd(dn_sc[...], h_ref[...], g_ref[...])
            dhi_ref[...] = dh_ref[...] + dhin
            dg_ref[...] += dg

    rowd = pl.BlockSpec((TM, D), lambda i, j: (i, 0))
    vec = pl.BlockSpec((1, D), lambda i, j: (0, 0))
    body, in_specs, args = _ordered(
        body,
        [pl.BlockSpec((TM, tn), lambda i, j: (i, j)), rowd, rowd, vec,
         RESIDENT],
        (dproj, dh, h, g, win), after)
    return pl.pallas_call(
        body, name="inproj_bwd", grid=(t // TM, nj),
        in_specs=in_specs,
        out_specs=[rowd, vec, pl.BlockSpec((nj, 1, tn), lambda i, j: (0, 0, 0))],
        out_shape=[jax.ShapeDtypeStruct((t, D), f32), jax.ShapeDtypeStruct((1, D), f32),
                   jax.ShapeDtypeStruct((nj, 1, tn), f32)],
        scratch_shapes=[pltpu.VMEM((TM, D), f32)],
        compiler_params=_cp(2),
    )(*args)


def _ffn_gu_grad(n, dgate, dup, tag, after=None):
    half = _tn_matmul(n, dgate, D, FS, (NCHIP, D, FS), (None, D, FS), lambda k, nn, m: (nn, 0, 0),
                      tag + "_dwg", after=after)
    return _tn_matmul(n, dup, D, FS, (NCHIP, D, FS), (None, D, FS), lambda k, nn, m: (2 + nn, 0, 0),
                      tag + "_dwu", base=half)


def _ffn_down_grad(a, df, tag, after=None):
    return _tn_matmul(a, df, FS, D, (F, D), (FS, D), lambda k, nn, m: (k, 0), tag + "_dwd", after=after)


def _square_grad(a, b, name):
    return _tn_matmul(a, b, D, D, (D, D), (D, D), lambda k, nn, m: (0, 0), name)


ANY = pl.BlockSpec(memory_space=pl.ANY)


def _place():
    x, y, c = lax.axis_index("x"), lax.axis_index("y"), lax.axis_index("c")
    chips = [(1 - x, y), (x, 1 - y), (1 - x, 1 - y)]
    return x, y, c, chips


def _chip_id(chip):
    return 2 * chip[0] + chip[1]


def _cast_into_slot(w2d, qc, dtype, name, after=None):
    r, cc = w2d.shape
    hr = r // 2

    def body(qc_ref, *refs):
        del qc_ref
        w_ref, o_ref = refs[-2:]
        o_ref[...] = w_ref[...].astype(dtype)

    in_specs, args = [pl.BlockSpec((hr, cc), lambda h, qc_ref: (h, 0))], (w2d,)
    if after is not None:
        in_specs, args = [ANY_SPEC] + in_specs, (after,) + args
    return pl.pallas_call(
        body, name=name,
        grid_spec=pltpu.PrefetchScalarGridSpec(
            num_scalar_prefetch=1, grid=(2,),
            in_specs=in_specs,
            out_specs=pl.BlockSpec((None, None, hr, cc), lambda h, qc_ref: (qc_ref[0], h, 0, 0))),
        out_shape=jax.ShapeDtypeStruct((NCHIP, 2, hr, cc), dtype),
        compiler_params=_cp(1),
    )(qc, *args)


def _place_pack(pack, qc):
    def body(qc_ref, p_ref, o_ref):
        del qc_ref
        o_ref[...] = p_ref[...]

    return pl.pallas_call(
        body, name="place_pack",
        grid_spec=pltpu.PrefetchScalarGridSpec(
            num_scalar_prefetch=1, grid=(1,),
            in_specs=[pl.BlockSpec(pack.shape, lambda i, qc_ref: (0, 0))],
            out_specs=pl.BlockSpec((None,) + pack.shape, lambda i, qc_ref: (2 * qc_ref[0] + qc_ref[1], 0, 0))),
        out_shape=jax.ShapeDtypeStruct((8,) + pack.shape, pack.dtype),
        compiler_params=_cp(1),
    )(qc, pack)


def _pair_add(part, got, qc, name):
    _, _, hr, cc = part.shape

    def body(qc_ref, p_ref, g_ref, o_ref, land_ref):
        s = pl.program_id(0)
        val = (p_ref[...].astype(f32) + g_ref[...].astype(f32)).astype(bf16)
        o_ref[...] = val

        @pl.when(s == qc_ref[0])
        def _():
            land_ref[...] = val

    return pl.pallas_call(
        body, name=name,
        grid_spec=pltpu.PrefetchScalarGridSpec(
            num_scalar_prefetch=1, grid=(NCHIP,),
            in_specs=[pl.BlockSpec((None, None, hr, cc), lambda s, qc_ref: (s, qc_ref[1], 0, 0)),
                      pl.BlockSpec((None, hr, cc), lambda s, qc_ref: (s, 0, 0))],
            out_specs=[pl.BlockSpec((None, hr, cc), lambda s, qc_ref: (s, 0, 0)),
                       pl.BlockSpec((None, hr, cc), lambda s, qc_ref: (qc_ref[0], 0, 0))]),
        out_shape=[jax.ShapeDtypeStruct((NCHIP, hr, cc), bf16)] * 2,
        compiler_params=_cp(1),
    )(qc, part, got)


def _sum_chips(got, name):
    _, hr, cc = got.shape

    def body(g_ref, o_ref):
        acc = g_ref[0].astype(f32)
        for s in range(1, NCHIP):
            acc = acc + g_ref[s].astype(f32)
        o_ref[...] = acc

    return pl.pallas_call(
        body, name=name, grid=(1,),
        in_specs=[pl.BlockSpec((NCHIP, hr, cc), lambda i: (0, 0, 0))],
        out_specs=pl.BlockSpec((hr, cc), lambda i: (0, 0)),
        out_shape=jax.ShapeDtypeStruct((hr, cc), f32),
        compiler_params=_cp(1),
    )(got)


def _pair_share(halves, name, after=None):
    n = len(halves)
    extra = () if after is None else (after,)

    def body(*refs):
        refs = refs[len(extra):]
        ins, outs = refs[:n], refs[n:2 * n]
        send_sems, recv_sems = refs[2 * n:]
        x, y, c, _ = _place()
        copies = []
        for a in range(n):
            cp = pltpu.make_async_remote_copy(
                src_ref=ins[a], dst_ref=outs[a], send_sem=send_sems.at[a], recv_sem=recv_sems.at[a],
                device_id=(x, y, 1 - c), device_id_type=MESH)
            cp.start()
            copies.append(cp)
        for cp in copies:
            cp.wait()

    return pl.pallas_call(
        body, name=name,
        in_specs=[ANY] * (len(extra) + n), out_specs=[ANY] * n,
        out_shape=[jax.ShapeDtypeStruct(s.shape, s.dtype) for s in halves],
        scratch_shapes=[pltpu.SemaphoreType.DMA((n,)), pltpu.SemaphoreType.DMA((n,))],
    )(*extra, *halves)


def _all_copy(buf_ref, send_ref, recv_ref, k, x, y, c, landing):
    px, py, pc = (1 - x if k & 4 else x, 1 - y if k & 2 else y, 1 - c if k & 1 else c)
    me = 4 * x + 2 * y + c
    there = 4 * px + 2 * py + pc
    return pltpu.make_async_remote_copy(
        src_ref=buf_ref.at[me], dst_ref=buf_ref.at[there if landing else me],
        send_sem=send_ref.at[k - 1], recv_sem=recv_ref.at[k - 1],
        device_id=(px, py, pc), device_id_type=MESH)


def _gather_all_start(buf, name):
    def body(in_ref, send, recv, thru, token):
        del thru
        x, y, c, _ = _place()
        for k in range(1, 8):
            _all_copy(in_ref, send, recv, k, x, y, c, False).start()
        token[...] = jnp.zeros_like(token)

    return pl.pallas_call(
        body, name=name,
        in_specs=[HBM],
        out_specs=[SEM, SEM, HBM, pl.BlockSpec(memory_space=pltpu.VMEM)],
        out_shape=[pltpu.SemaphoreType.DMA((7,)), pltpu.SemaphoreType.DMA((7,)),
                   pltpu.HBM(buf.shape, buf.dtype), jax.ShapeDtypeStruct((8, 128), f32)],
        input_output_aliases={0: 2},
        compiler_params=pltpu.CompilerParams(has_side_effects=EFFECT),
    )(_in_hbm(buf))


def _gather_all_wait(send, recv, buf, after, name):
    def body(in_ref, send_r, recv_r, after_ref, out_ref):
        del after_ref, out_ref
        x, y, c, _ = _place()
        for k in range(1, 8):
            cp = _all_copy(in_ref, send_r, recv_r, k, x, y, c, True)
            cp.wait_send()
            cp.wait_recv()

    return pl.pallas_call(
        body, name=name,
        in_specs=[HBM, SEM, SEM, ANY],
        out_specs=HBM,
        out_shape=pltpu.HBM(buf.shape, buf.dtype),
        input_output_aliases={0: 0},
        compiler_params=pltpu.CompilerParams(has_side_effects=EFFECT),
    )(buf, send, recv, after)


HBM = pl.BlockSpec(memory_space=pltpu.HBM)
SEM = pl.BlockSpec(memory_space=pltpu.SEMAPHORE)
EFFECT = pltpu.SideEffectType.DATAFLOW_SIDE_EFFECTING
N_PEER = 3


def _in_hbm(a):
    return pltpu.with_memory_space_constraint(a, pltpu.HBM)


def _gather_copy(buf_ref, send_ref, recv_ref, j, chip, q, c, landing_chip):
    return pltpu.make_async_remote_copy(
        src_ref=buf_ref.at[q, c], dst_ref=buf_ref.at[landing_chip, c],
        send_sem=send_ref.at[j], recv_sem=recv_ref.at[j],
        device_id=(chip[0], chip[1], c), device_id_type=MESH)


def _gather_start(bufs, name):
    n = len(bufs)

    def body(*refs):
        ins = refs[:n]
        send, recv = refs[n:2 * n], refs[2 * n:3 * n]
        token = refs[4 * n]
        x, y, c, chips = _place()
        q = 2 * x + y
        for a in range(n):
            for j, chip in enumerate(chips):
                _gather_copy(ins[a], send[a], recv[a], j, chip, q, c, q).start()
        token[...] = jnp.zeros_like(token)

    sems = [pltpu.SemaphoreType.DMA((N_PEER,))] * (2 * n)
    outs = pl.pallas_call(
        body, name=name,
        in_specs=[HBM] * n,
        out_specs=[SEM] * (2 * n) + [HBM] * n + [pl.BlockSpec(memory_space=pltpu.VMEM)],
        out_shape=sems + [pltpu.HBM(b.shape, b.dtype) for b in bufs] + [jax.ShapeDtypeStruct((8, 128), f32)],
        input_output_aliases={a: 2 * n + a for a in range(n)},
        compiler_params=pltpu.CompilerParams(has_side_effects=EFFECT),
    )(*[_in_hbm(b) for b in bufs])
    return list(outs[:n]), list(outs[n:2 * n]), list(outs[2 * n:3 * n]), outs[3 * n]


def _gather_wait(send, recv, bufs, after, name):
    n = len(bufs)

    def body(*refs):
        ins = refs[:n]
        send_r, recv_r = refs[n:2 * n], refs[2 * n:3 * n]
        x, y, c, chips = _place()
        q = 2 * x + y
        for a in range(n):
            for j, chip in enumerate(chips):
                cp = _gather_copy(ins[a], send_r[a], recv_r[a], j, chip, q, c, _chip_id(chip))
                cp.wait_send()
                cp.wait_recv()

    afters = after if isinstance(after, (tuple, list)) else (after,)
    outs = pl.pallas_call(
        body, name=name,
        in_specs=[HBM] * n + [SEM] * (2 * n) + [ANY] * len(afters),
        out_specs=[HBM] * n,
        out_shape=[pltpu.HBM(b.shape, b.dtype) for b in bufs],
        input_output_aliases={a: a for a in range(n)},
        compiler_params=pltpu.CompilerParams(has_side_effects=EFFECT),
    )(*bufs, *send, *recv, *afters)
    return list(outs)


def _forward_halves(bufs, name):
    n = len(bufs)

    def body(*refs):
        outs = refs[n:2 * n]
        send_sems, recv_sems = refs[2 * n:]
        x, y, c, chips = _place()
        sibling = (x, y, 1 - c)

        def remote(a, j, blk):
            return pltpu.make_async_remote_copy(src_ref=blk, dst_ref=blk, send_sem=send_sems.at[a, j],
                                                recv_sem=recv_sems.at[a, j], device_id=sibling,
                                                device_id_type=MESH)

        sent = []
        for a in range(n):
            for j, chip in enumerate(chips):
                cp = remote(a, j, outs[a].at[_chip_id(chip), c])
                cp.start()
                sent.append(cp)
        for a in range(n):
            for j, chip in enumerate(chips):
                remote(a, j, outs[a].at[_chip_id(chip), 1 - c]).wait_recv()
        for cp in sent:
            cp.wait_send()

    return pl.pallas_call(
        body, name=name,
        in_specs=[ANY] * n, out_specs=[ANY] * n,
        out_shape=[jax.ShapeDtypeStruct(s.shape, s.dtype) for s in bufs],
        scratch_shapes=[pltpu.SemaphoreType.DMA((n, N_PEER)), pltpu.SemaphoreType.DMA((n, N_PEER))],
        input_output_aliases={a: a for a in range(n)},
    )(*bufs)


def _reduce_copy(sum_ref, land_ref, send_ref, recv_ref, j, chip, q, c, landing_chip):
    return pltpu.make_async_remote_copy(
        src_ref=sum_ref.at[_chip_id(chip)], dst_ref=land_ref.at[landing_chip],
        send_sem=send_ref.at[j], recv_sem=recv_ref.at[j],
        device_id=(chip[0], chip[1], c), device_id_type=MESH)


def _reduce_start(sums, lands, name):
    n = len(sums)

    def body(*refs):
        s_in, l_in = refs[:n], refs[n:2 * n]
        send, recv = refs[2 * n:3 * n], refs[3 * n:4 * n]
        token = refs[6 * n]
        x, y, c, chips = _place()
        q = 2 * x + y
        for a in range(n):
            for j, chip in enumerate(chips):
                _reduce_copy(s_in[a], l_in[a], send[a], recv[a], j, chip, q, c, q).start()
        token[...] = jnp.zeros_like(token)

    sems = [pltpu.SemaphoreType.DMA((N_PEER,))] * (2 * n)
    outs = pl.pallas_call(
        body, name=name,
        in_specs=[HBM] * (2 * n),
        out_specs=[SEM] * (2 * n) + [HBM] * (2 * n) + [pl.BlockSpec(memory_space=pltpu.VMEM)],
        out_shape=sems + [pltpu.HBM(b.shape, b.dtype) for b in list(sums) + list(lands)]
        + [jax.ShapeDtypeStruct((8, 128), f32)],
        input_output_aliases={a: 2 * n + a for a in range(2 * n)},
        compiler_params=pltpu.CompilerParams(has_side_effects=EFFECT),
    )(*[_in_hbm(b) for b in list(sums) + list(lands)])
    return (list(outs[:n]), list(outs[n:2 * n]), list(outs[2 * n:3 * n]), list(outs[3 * n:4 * n]),
            outs[4 * n])


def _reduce_wait(send, recv, sums, lands, after, name):
    n = len(sums)

    def body(*refs):
        s_in, l_in = refs[:n], refs[n:2 * n]
        send_r, recv_r = refs[2 * n:3 * n], refs[3 * n:4 * n]
        x, y, c, chips = _place()
        q = 2 * x + y
        for a in range(n):
            for j, chip in enumerate(chips):
                cp = _reduce_copy(s_in[a], l_in[a], send_r[a], recv_r[a], j, chip, q, c, _chip_id(chip))
                cp.wait_send()
                cp.wait_recv()

    afters = after if isinstance(after, (tuple, list)) else (after,)
    outs = pl.pallas_call(
        body, name=name,
        in_specs=[HBM] * (2 * n) + [SEM] * (2 * n) + [ANY] * len(afters),
        out_specs=[HBM] * (2 * n),
        out_shape=[pltpu.HBM(b.shape, b.dtype) for b in list(sums) + list(lands)],
        input_output_aliases={a: a for a in range(2 * n)},
        compiler_params=pltpu.CompilerParams(has_side_effects=EFFECT),
    )(*sums, *lands, *send, *recv, *afters)
    return list(outs[n:])


def _sibling_copy(part_ref, land_ref, send_ref, recv_ref, x, y, c):
    return pltpu.make_async_remote_copy(
        src_ref=part_ref.at[:, 1 - c], dst_ref=land_ref, send_sem=send_ref.at[0], recv_sem=recv_ref.at[0],
        device_id=(x, y, 1 - c), device_id_type=MESH)


def _pair_exchange_start(parts, name):
    n = len(parts)
    lands = [lax.empty((NCHIP,) + p.shape[2:], p.dtype) for p in parts]

    def body(*refs):
        p_in, l_in = refs[:n], refs[n:2 * n]
        send, recv = refs[2 * n:3 * n], refs[3 * n:4 * n]
        token = refs[6 * n]
        x, y, c, _ = _place()
        for a in range(n):
            _sibling_copy(p_in[a], l_in[a], send[a], recv[a], x, y, c).start()
        token[...] = jnp.zeros_like(token)

    sems = [pltpu.SemaphoreType.DMA((1,))] * (2 * n)
    outs = pl.pallas_call(
        body, name=name,
        in_specs=[HBM] * (2 * n),
        out_specs=[SEM] * (2 * n) + [HBM] * (2 * n) + [pl.BlockSpec(memory_space=pltpu.VMEM)],
        out_shape=sems + [pltpu.HBM(b.shape, b.dtype) for b in list(parts) + lands]
        + [jax.ShapeDtypeStruct((8, 128), f32)],
        input_output_aliases={a: 2 * n + a for a in range(2 * n)},
        compiler_params=pltpu.CompilerParams(has_side_effects=EFFECT),
    )(*[_in_hbm(b) for b in list(parts) + lands])
    return (list(outs[:n]), list(outs[n:2 * n]), list(outs[2 * n:3 * n]), list(outs[3 * n:4 * n]),
            outs[4 * n])


def _pair_exchange_wait(send, recv, parts, lands, after, name):
    n = len(parts)

    def body(*refs):
        p_in, l_in = refs[:n], refs[n:2 * n]
        send_r, recv_r = refs[2 * n:3 * n], refs[3 * n:4 * n]
        x, y, c, _ = _place()
        for a in range(n):
            cp = _sibling_copy(p_in[a], l_in[a], send_r[a], recv_r[a], x, y, c)
            cp.wait_send()
            cp.wait_recv()

    outs = pl.pallas_call(
        body, name=name,
        in_specs=[HBM] * (2 * n) + [SEM] * (2 * n) + [ANY],
        out_specs=[HBM] * (2 * n),
        out_shape=[pltpu.HBM(b.shape, b.dtype) for b in list(parts) + list(lands)],
        input_output_aliases={a: a for a in range(2 * n)},
        compiler_params=pltpu.CompilerParams(has_side_effects=EFFECT),
    )(*parts, *lands, *send, *recv, after)
    return list(outs[:n]), list(outs[n:])


def _forward_copy(buf_ref, send_ref, recv_ref, j, chip, x, y, c, landing):
    return pltpu.make_async_remote_copy(
        src_ref=buf_ref.at[_chip_id(chip), c], dst_ref=buf_ref.at[_chip_id(chip), 1 - c if landing else c],
        send_sem=send_ref.at[j], recv_sem=recv_ref.at[j], device_id=(x, y, 1 - c), device_id_type=MESH)


def _forward_start(bufs, name):
    n = len(bufs)

    def body(*refs):
        ins = refs[:n]
        send, recv = refs[n:2 * n], refs[2 * n:3 * n]
        token = refs[4 * n]
        x, y, c, chips = _place()
        for a in range(n):
            for j, chip in enumerate(chips):
                _forward_copy(ins[a], send[a], recv[a], j, chip, x, y, c, False).start()
        token[...] = jnp.zeros_like(token)

    sems = [pltpu.SemaphoreType.DMA((N_PEER,))] * (2 * n)
    outs = pl.pallas_call(
        body, name=name,
        in_specs=[HBM] * n,
        out_specs=[SEM] * (2 * n) + [HBM] * n + [pl.BlockSpec(memory_space=pltpu.VMEM)],
        out_shape=sems + [pltpu.HBM(b.shape, b.dtype) for b in bufs] + [jax.ShapeDtypeStruct((8, 128), f32)],
        input_output_aliases={a: 2 * n + a for a in range(n)},
        compiler_params=pltpu.CompilerParams(has_side_effects=EFFECT),
    )(*[_in_hbm(b) for b in bufs])
    return list(outs[:n]), list(outs[n:2 * n]), list(outs[2 * n:3 * n]), outs[3 * n]


def _forward_wait(send, recv, bufs, after, name):
    n = len(bufs)

    def body(*refs):
        ins = refs[:n]
        send_r, recv_r = refs[n:2 * n], refs[2 * n:3 * n]
        x, y, c, chips = _place()
        for a in range(n):
            for j, chip in enumerate(chips):
                cp = _forward_copy(ins[a], send_r[a], recv_r[a], j, chip, x, y, c, True)
                cp.wait_send()
                cp.wait_recv()

    outs = pl.pallas_call(
        body, name=name,
        in_specs=[HBM] * n + [SEM] * (2 * n) + [ANY],
        out_specs=[HBM] * n,
        out_shape=[pltpu.HBM(b.shape, b.dtype) for b in bufs],
        input_output_aliases={a: a for a in range(n)},
        compiler_params=pltpu.CompilerParams(has_side_effects=EFFECT),
    )(*bufs, *send, *recv, after)
    return list(outs)


def _adamw_math(w, g, m, v):
    m = ADAM_B1 * m + (1.0 - ADAM_B1) * g
    v = ADAM_B2 * v + (1.0 - ADAM_B2) * (g * g)
    m_hat = m / (1.0 - ADAM_B1 ** ADAM_STEP)
    v_hat = v / (1.0 - ADAM_B2 ** ADAM_STEP)
    delta = -ADAM_LR * (m_hat / (jnp.sqrt(v_hat) + ADAM_EPS) + ADAM_WD * w)
    return delta, m, v


def _adamw(w, mine, theirs, m, v, qc, name):
    r, cc = w.shape
    hr = r // 2
    tr = next(hr // k for k in range(1, hr + 1)
              if hr % k == 0 and (hr // k) % 8 == 0 and (hr // k) * cc * 4 <= (3 << 19))
    nb = hr // tr

    def body(qc_ref, w_ref, a_ref, b_ref, m_ref, v_ref, g_ref, d_ref, mo_ref, vo_ref):
        g = jnp.where(pl.program_id(0) == qc_ref[1], a_ref[...], b_ref[...])
        g_ref[...] = g
        d_ref[...], mo_ref[...], vo_ref[...] = _adamw_math(w_ref[...], g, m_ref[...], v_ref[...])

    full = pl.BlockSpec((tr, cc), lambda h, i, qc_ref: (h * nb + i, 0))
    half = pl.BlockSpec((tr, cc), lambda h, i, qc_ref: (i, 0))
    return pl.pallas_call(
        body, name=name,
        grid_spec=pltpu.PrefetchScalarGridSpec(
            num_scalar_prefetch=1, grid=(2, nb),
            in_specs=[full, half, half, full, full], out_specs=[full] * 4),
        out_shape=[jax.ShapeDtypeStruct((r, cc), f32)] * 4,
        compiler_params=_cp(2),
    )(qc, w, mine, theirs, m, v)


REPL = [("ffn1_norm", 1), ("mix_norm", 1), ("b_in", 6), ("rnn_conv_b", 1), ("rg_b_a", 1), ("rg_b_x", 1),
        ("rg_lambda", 1), ("conv_dw_b", 1), ("conv_ln_g", 1), ("conv_ln_b", 1), ("conv_b_proj", 1),
        ("ffn2_norm", 1), ("final_norm", 1)]
COLSH = [("meta_tokens", NMETA), ("rnn_conv_w", KC4), ("conv_dw_w", KC31)]
SMALL = REPL + COLSH
CS = D // NCHIP


def _pack_rows():
    starts, row = {}, 0
    for k, rows in REPL:
        starts[k] = row
        row += rows
    for k, rows in COLSH:
        row = -(-row // 8) * 8
        starts[k] = row
        row += rows
    return starts, -(-row // 8) * 8


PACK_START, LOSS_ROW = _pack_rows()
SMALL_ROWS = LOSS_ROW + 8


def _small_pack(g, loss_row):
    pieces, row = [], 0
    for k, rows in SMALL:
        if PACK_START[k] > row:
            pieces.append(jnp.zeros((PACK_START[k] - row, D), f32))
        pieces.append(g[k].reshape(rows, D))
        row = PACK_START[k] + rows
    pieces.append(jnp.zeros((LOSS_ROW - row, D), f32))
    pieces.append(loss_row)
    pieces.append(jnp.zeros((SMALL_ROWS - LOSS_ROW - 1, D), f32))
    return jnp.concatenate(pieces, axis=0)


def _adamw_small(packs, ws, ms, vs):
    ns = len(SMALL)

    def body(*refs):
        pack_ref = refs[0]
        w_refs, m_refs, v_refs = refs[1:1 + ns], refs[1 + ns:1 + 2 * ns], refs[1 + 2 * ns:1 + 3 * ns]
        outs = refs[1 + 3 * ns:1 + 7 * ns]
        g_refs, d_refs, mo_refs, vo_refs = outs[:ns], outs[ns:2 * ns], outs[2 * ns:3 * ns], outs[3 * ns:]
        loss_ref = refs[1 + 7 * ns]
        gsum_sc = refs[2 + 7 * ns]
        q = 2 * lax.axis_index("x") + lax.axis_index("y")
        acc = pack_ref[0]
        for dev in range(1, 8):
            acc = acc + pack_ref[dev]
        gsum_sc[...] = acc
        loss_ref[...] = gsum_sc[LOSS_ROW:LOSS_ROW + 1, :]
        for idx, (name, rows) in enumerate(SMALL):
            row = PACK_START[name]
            if idx < len(REPL):
                for k in range(rows):
                    cols = slice(k * D, (k + 1) * D)
                    g = gsum_sc[row + k:row + k + 1, :]
                    d, mm, vv = _adamw_math(w_refs[idx][:, cols], g, m_refs[idx][:, cols], v_refs[idx][:, cols])
                    g_refs[idx][:, cols] = g
                    d_refs[idx][:, cols] = d
                    mo_refs[idx][:, cols] = mm
                    vo_refs[idx][:, cols] = vv
            else:
                g = gsum_sc[row:row + rows, pl.ds(pl.multiple_of(q * CS, CS), CS)]
                d, mm, vv = _adamw_math(w_refs[idx][...], g, m_refs[idx][...], v_refs[idx][...])
                g_refs[idx][...] = g
                d_refs[idx][...] = d
                mo_refs[idx][...] = mm
                vo_refs[idx][...] = vv

    shapes = [jax.ShapeDtypeStruct(w.shape, f32) for w in ws]
    return pl.pallas_call(
        body, name="adamw_small",
        out_shape=shapes * 4 + [jax.ShapeDtypeStruct((1, D), f32)],
        scratch_shapes=[pltpu.VMEM((SMALL_ROWS, D), f32)],
        compiler_params=pltpu.CompilerParams(vmem_limit_bytes=VMEM_LIMIT),
    )(packs, *ws, *ms, *vs)


BIG = ["ffn1_w_gu", "ffn1_w_down", "w_in", "rg_w_a", "rg_w_x", "rnn_w_proj", "conv_w_proj", "w_out",
       "ffn2_w_gu", "ffn2_w_down"]
WEIGHTS = ['meta_tokens', 'ffn1_norm', 'ffn1_w_gu', 'ffn1_w_down', 'mix_norm', 'w_in', 'b_in', 'rnn_conv_w',
           'rnn_conv_b', 'rg_w_a', 'rg_b_a', 'rg_w_x', 'rg_b_x', 'rg_lambda', 'rnn_w_proj', 'conv_dw_w',
           'conv_dw_b', 'conv_ln_g', 'conv_ln_b', 'conv_w_proj', 'conv_b_proj', 'w_out', 'ffn2_norm',
           'ffn2_w_gu', 'ffn2_w_down', 'final_norm']


def _as2d(a):
    return a.reshape(-1, a.shape[-1])


def _step(x, loss_target, w, m, v):
    seq = x.shape[1]
    n_valid = NMETA + seq
    t = -(-n_valid // TM) * TM

    qc = jnp.stack([2 * lax.axis_index("x") + lax.axis_index("y"), lax.axis_index("c")]).astype(jnp.int32)
    p = {k: w[k].reshape(1, rows * D) for k, rows in REPL}

    first = ["ffn1_w_gu", "ffn1_w_down", "small"]
    later = [["w_in"], ["rg_w_a", "rg_w_x", "rnn_w_proj", "conv_w_proj", "w_out"], ["ffn2_w_gu", "ffn2_w_down"]]
    small_rows = sum(r for _, r in COLSH)
    small = jnp.concatenate([_as2d(w[k]) for k, _ in COLSH] + [jnp.zeros((64 - small_rows, CS), f32)], axis=0)

    def cast(k, token=None):
        src, dtype = (small, f32) if k == "small" else (_as2d(w[k]), bf16)
        return _cast_into_slot(src, qc, dtype, "cast_" + k, after=token)

    send1, recv1, bufs1, token1 = _gather_start([cast(k) for k in first], "gather_start_first")
    rest = [k for grp in later for k in grp]
    send2, recv2, bufs2, token2 = _gather_start([cast(k, token1) for k in rest], "gather_start_rest")

    def install(names, done):
        for k, b in zip(names, done):
            full = b.reshape(NCHIP, 2 * b.shape[2], b.shape[3])
            if k in ("ffn1_w_down", "ffn2_w_down"):
                full = full.reshape(F, D)
            elif k in ("rnn_w_proj", "conv_w_proj", "w_out"):
                full = full.reshape(D, D)
            elif k in ("rg_w_a", "rg_w_x"):
                full = full.reshape(NCHIP, NHEAD, HD // NCHIP, HD).transpose(1, 0, 2, 3).reshape(NHEAD, HD, HD)
            p[k] = full

    def finish(names, send, recv, bufs, after, tag):
        install(names, _forward_halves(_gather_wait(send, recv, bufs, after, "gather_wait_" + tag),
                                       "gather_forward_" + tag))

    def group(names):
        idx = [rest.index(k) for k in names]
        return names, [send2[i] for i in idx], [recv2[i] for i in idx], [bufs2[i] for i in idx]

    h0 = jnp.pad(x[0], ((NMETA, t - n_valid), (0, 0)))
    tgt = jnp.pad(loss_target[0], ((NMETA, t - n_valid), (0, 0)))
    finish(first, send1, recv1, bufs1, (token2, h0, tgt), "first")
    small_full = p.pop("small").transpose(1, 0, 2).reshape(64, D)
    row = 0
    for k, rows in COLSH:
        p[k] = small_full[row:row + rows]
        row += rows

    h0 = lax.dynamic_update_slice(h0, p["meta_tokens"], (0, 0))
    h1, gate1, up1, n1 = _ffn_fwd(h0, p["ffn1_norm"], p["ffn1_w_gu"], p["ffn1_w_down"], "ffn1_fwd")
    finish(*group(later[0]), h1, "in")
    proj, n2 = _inproj_fwd(h1, p["mix_norm"], p["w_in"], p["b_in"])
    names_l = later[1] + later[2]
    _, send_l, recv_l, bufs_l = group(names_l)
    send_f, recv_f, bufs_f, token = _forward_start(
        _gather_wait(send_l, recv_l, bufs_l, proj, "gather_wait_late"), "gather_forward_start")
    vc, s = _conv_fwd(proj, p["conv_dw_w"], p["conv_dw_b"], p["conv_ln_g"], p["conv_ln_b"], after=token)
    install(names_l, _forward_wait(send_f, recv_f, bufs_f, vc, "gather_forward_wait"))
    xr, hr, z = _rnn_fwd(proj, p["rnn_conv_w"], p["rnn_conv_b"], p["rg_w_a"], p["rg_b_a"],
                         p["rg_w_x"], p["rg_b_x"], p["rg_lambda"])
    h2 = _merge_fwd(h1, z, s, proj, p["rnn_w_proj"], p["conv_w_proj"], p["conv_b_proj"], p["w_out"])
    h3, gate2, up2, n3 = _ffn_fwd(h2, p["ffn2_norm"], p["ffn2_w_gu"], p["ffn2_w_down"], "ffn2_fwd")
    dh3, loss_blk, d_final = _final_loss(h3, p["final_norm"], tgt, n_valid)

    g = {"final_norm": d_final}
    pending = []

    def exchange_start(names, tag):
        parts = []
        for k in names:
            rows = g[k].size // (NCHIP * g[k].shape[-1])
            parts.append(g[k].reshape((NCHIP, 2, rows // 2, g[k].shape[-1])))
        send, recv, parts, lands, token = _pair_exchange_start(parts, "pair_exchange_start_" + tag)
        return (names, tag, send, recv, parts, lands), token

    def reduce_start(state, after):
        names, tag, send, recv, parts, lands = state
        parts, from_sibling = _pair_exchange_wait(send, recv, parts, lands, after, "pair_exchange_wait_" + tag)
        added = [_pair_add(pp, gg, qc, "pair_add_" + k) for pp, gg, k in zip(parts, from_sibling, names)]
        send, recv, sums, lands, token = _reduce_start([a for a, _ in added], [b for _, b in added],
                                                       "reduce_start_" + tag)
        pending.append((names, tag, send, recv, sums, lands))
        return token

    dh2, dgate2, dup2, a2, df2, g["ffn2_norm"] = _ffn_bwd(
        dh3, h2, p["ffn2_norm"], gate2, up2, p["ffn2_w_gu"], p["ffn2_w_down"], "ffn2_bwd")
    g["ffn2_w_gu"] = _ffn_gu_grad(n3, dgate2, dup2, "ffn2")
    g["ffn2_w_down"] = _ffn_down_grad(a2, df2, "ffn2")
    state, token = exchange_start(["ffn2_w_gu", "ffn2_w_down"], "ffn2")

    dz, ds, dproj, dh2b, merged, dya, dyb, g["conv_b_proj"] = _merge_bwd(
        dh2, z, s, proj, p["rnn_w_proj"], p["conv_w_proj"], p["conv_b_proj"], p["w_out"], after=token)
    token = reduce_start(state, dz)
    dproj, g["conv_dw_w"], g["conv_dw_b"], g["conv_ln_g"], g["conv_ln_b"] = _conv_bwd(
        ds, vc, proj, dproj, p["conv_dw_w"], p["conv_ln_g"], p["conv_ln_b"], after=token)
    g["w_out"] = _square_grad(merged, dh2b, "dw_out")
    g["rnn_w_proj"] = _square_grad(z, dya, "dw_rnn_proj")
    g["conv_w_proj"] = _square_grad(s, dyb, "dw_conv_proj")
    (dproj, g["rg_w_a"], g["rg_w_x"], g["rnn_conv_w"], g["rnn_conv_b"], g["rg_b_a"], g["rg_b_x"],
     g["rg_lambda"]) = _rnn_bwd(dz, xr, hr, proj, dproj, p["rnn_conv_w"], p["rg_w_a"], p["rg_b_a"],
                                p["rg_w_x"], p["rg_b_x"], p["rg_lambda"])
    state, token = exchange_start(["w_out", "rnn_w_proj", "conv_w_proj", "rg_w_a", "rg_w_x"], "mix")

    dh1, g["mix_norm"], db_in = _inproj_bwd(dproj, dh2, h1, p["mix_norm"], p["w_in"], after=token)
    g["b_in"] = db_in.reshape(1, NIN)
    token = reduce_start(state, dh1)
    g["w_in"] = _tn_matmul(n2, dproj, D, NIN // NCHIP, (NCHIP, D, NIN // NCHIP),
                           (None, D, NIN // NCHIP), lambda k, nn, mm: (nn, 0, 0), "dw_in", after=token)
    state, token = exchange_start(["w_in"], "in")

    dh0, dgate1, dup1, a1, df1, g["ffn1_norm"] = _ffn_bwd(
        dh1, h0, p["ffn1_norm"], gate1, up1, p["ffn1_w_gu"], p["ffn1_w_down"], "ffn1_bwd", after=token)
    g["meta_tokens"] = dh0[0:NMETA]
    grad_x = dh0[NMETA:n_valid][None]
    token = reduce_start(state, dh0)

    send_s, recv_s, pack_buf, token_s = _gather_all_start(
        _place_pack(_small_pack(g, loss_blk.reshape(1, D)), qc), "gather_all_start")
    g["ffn1_w_down"] = _ffn_down_grad(a1, df1, "ffn1", after=(token, token_s))
    state, token = exchange_start(["ffn1_w_down"], "ffn1_down")
    gate_half = _tn_matmul(n1, dgate1, D, FS, (NCHIP, D, FS), (None, D, FS), lambda k, nn, mm: (nn, 0, 0),
                           "ffn1_dwg", after=token)
    token = reduce_start(state, gate_half)
    g["ffn1_w_gu"] = _tn_matmul(n1, dup1, D, FS, (NCHIP, D, FS), (None, D, FS), lambda k, nn, mm: (2 + nn, 0, 0),
                                "ffn1_dwu", base=gate_half, after=token)
    state_gu, token = exchange_start(["ffn1_w_gu"], "ffn1_gu")
    packs = _gather_all_wait(send_s, recv_s, pack_buf, token, "gather_all_wait")

    grads, deltas, new_m, new_v = {}, {}, {}, {}

    def landed_sums(items, after):
        names, mine = [], []
        for grp_names, grp_tag, send, recv, sums, lands in items:
            landed = _reduce_wait(send, recv, sums, lands, after, "reduce_wait_" + grp_tag)
            mine += [_sum_chips(b, "sum_chips_" + k) for b, k in zip(landed, grp_names)]
            names += grp_names
            after = mine[-1]
        return names, mine

    def share_and_update(names, mine, tag, after=None):
        theirs = _pair_share(mine, "pair_share_" + tag, after=after)
        for k, mi, th in zip(names, mine, theirs):
            outs = _adamw(_as2d(w[k]), mi, th, _as2d(m[k]), _as2d(v[k]), qc, "adamw_" + k)
            grads[k], deltas[k], new_m[k], new_v[k] = (a.reshape(w[k].shape) for a in outs)
        return [new_v[k] for k in names]

    early_names, early_mine = landed_sums(pending[:3], packs)
    token = reduce_start(state_gu, early_mine[-1])
    after = share_and_update(early_names, early_mine, "early", after=token)
    share_and_update(*landed_sums(pending[3:], after), "late")
    names = [k for k, _ in SMALL]
    shape2 = {k: ((1, rows * D) if (k, rows) in REPL else (rows, CS)) for k, rows in SMALL}
    outs = _adamw_small(packs, *[[a[k].reshape(shape2[k]) for k in names] for a in (w, m, v)])
    ns = len(names)
    for i, k in enumerate(names):
        grads[k], deltas[k], new_m[k], new_v[k] = (outs[j * ns + i].reshape(w[k].shape) for j in range(4))

    loss = outs[4 * ns][0, 0]
    return (loss, grad_x, *[grads[k] for k in WEIGHTS], *[deltas[k] for k in WEIGHTS],
            *[new_m[k] for k in WEIGHTS], *[new_v[k] for k in WEIGHTS])


def kernel(x, meta_tokens, ffn1_norm, ffn1_w_gu, ffn1_w_down, mix_norm, w_in, b_in, rnn_conv_w, rnn_conv_b, rg_w_a, rg_b_a, rg_w_x, rg_b_x, rg_lambda, rnn_w_proj, conv_dw_w, conv_dw_b, conv_ln_g, conv_ln_b, conv_w_proj, conv_b_proj, w_out, ffn2_norm, ffn2_w_gu, ffn2_w_down, final_norm, loss_target, m_meta_tokens, m_ffn1_norm, m_ffn1_w_gu, m_ffn1_w_down, m_mix_norm, m_w_in, m_b_in, m_rnn_conv_w, m_rnn_conv_b, m_rg_w_a, m_rg_b_a, m_rg_w_x, m_rg_b_x, m_rg_lambda, m_rnn_w_proj, m_conv_dw_w, m_conv_dw_b, m_conv_ln_g, m_conv_ln_b, m_conv_w_proj, m_conv_b_proj, m_w_out, m_ffn2_norm, m_ffn2_w_gu, m_ffn2_w_down, m_final_norm, v_meta_tokens, v_ffn1_norm, v_ffn1_w_gu, v_ffn1_w_down, v_mix_norm, v_w_in, v_b_in, v_rnn_conv_w, v_rnn_conv_b, v_rg_w_a, v_rg_b_a, v_rg_w_x, v_rg_b_x, v_rg_lambda, v_rnn_w_proj, v_conv_dw_w, v_conv_dw_b, v_conv_ln_g, v_conv_ln_b, v_conv_w_proj, v_conv_b_proj, v_w_out, v_ffn2_norm, v_ffn2_w_gu, v_ffn2_w_down, v_final_norm):
    args = locals()
    w = {k: args[k] for k in WEIGHTS}
    m = {k: args["m_" + k] for k in WEIGHTS}
    v = {k: args["v_" + k] for k in WEIGHTS}
    return _step(x, loss_target, w, m, v)
```

```python
import functools

import jax
import jax.numpy as jnp
from jax import lax
from jax.experimental import pallas as pl
from jax.experimental.pallas import tpu as pltpu

f32 = jnp.float32
bf16 = jnp.bfloat16

D = 1024
F = 2816
FS = F // 2
NIN = 6 * D
NMETA = 16
NHEAD = 4
HD = D // NHEAD
KC4 = 4
KC31 = 31
HALO = 32
EPS = 1e-6
TM = 416
NCHIP = 4
MESH = pl.DeviceIdType.MESH

ADAM_LR = 0.001
ADAM_B1 = 0.9
ADAM_B2 = 0.999
ADAM_EPS = 1e-08
ADAM_WD = 0.01
ADAM_STEP = 10

VMEM_LIMIT = 56 * 1024 * 1024
FSUB = [(o, min(256, FS - o)) for o in range(0, FS, 256)]


def _cp(n_axes, **kw):
    return pltpu.CompilerParams(dimension_semantics=("arbitrary",) * n_axes,
                                vmem_limit_bytes=VMEM_LIMIT, **kw)


RESIDENT = pl.BlockSpec(memory_space=pltpu.VMEM)


def _n_after(after):
    return 0 if after is None else (len(after) if isinstance(after, (tuple, list)) else 1)


def _ordered(body, in_specs, args, after):
    if after is None:
        return body, in_specs, args
    extra = tuple(after) if isinstance(after, (tuple, list)) else (after,)
    return (lambda *refs: body(*refs[len(extra):]),
            [pl.BlockSpec(memory_space=pl.ANY)] * len(extra) + list(in_specs), extra + tuple(args))


def _nt_dot(a, b):
    return lax.dot_general(a, b, (((1,), (1,)), ((), ())), preferred_element_type=f32)


def _tn_dot(a, b):
    return lax.dot_general(a, b, (((0,), (0,)), ((), ())), preferred_element_type=f32)


def _sigmoid(x):
    return 0.5 * jnp.tanh(0.5 * x) + 0.5


def _log1p(y):
    u = 1.0 + y
    d = u - 1.0
    return jnp.where(d == 0.0, y, jnp.log(u) * (y / jnp.where(d == 0.0, 1.0, d)))


def _softplus(x):
    return jnp.maximum(x, 0.0) + _log1p(jnp.exp(-jnp.abs(x)))


def _one_minus_square(a, log_a):
    x = 2.0 * log_a
    series = x * (1.0 + x * (0.5 + x * (1.0 / 6.0)))
    return jnp.where(jnp.abs(x) < 0.03, -series, 1.0 - a * a)


_GELU_C = 0.7978845608028654
_GELU_K = 0.044715


def _gelu_and_grad(y):
    y2 = y * y
    th = jnp.tanh(_GELU_C * (y + _GELU_K * y * y2))
    gel = 0.5 * y * (1.0 + th)
    dgel = 0.5 * (1.0 + th) + 0.5 * y * (1.0 - th * th) * _GELU_C * (1.0 + 3.0 * _GELU_K * y2)
    return gel, dgel


def _rms_stats(h):
    return lax.rsqrt(jnp.mean(h * h, axis=-1, keepdims=True) + EPS)


def _rms_bwd(dn, h, g):
    r = _rms_stats(h)
    nhat = h * r
    dnh = dn * g
    dh = r * (dnh - nhat * jnp.mean(dnh * nhat, axis=-1, keepdims=True))
    dg = jnp.sum(dn * nhat, axis=0, keepdims=True)
    return dh, dg


def _row_ids(shape):
    return lax.broadcasted_iota(jnp.int32, shape, 0)


def _ffn_fwd(h, g, wgu, wd, name):
    t = h.shape[0]
    nj = 2

    def body(h_ref, g_ref, wg_ref, wd_ref, ho_ref, gate_ref, up_ref, n_ref, nb_sc, acc_sc, a_sc):
        j = pl.program_id(1)

        @pl.when(j == 0)
        def _():
            hh = h_ref[...]
            nb = (hh * _rms_stats(hh) * g_ref[...]).astype(bf16)
            nb_sc[...] = nb
            n_ref[...] = nb
            acc_sc[...] = jnp.zeros_like(acc_sc)

        nb = nb_sc[...]
        for off, width in FSUB:
            cols = slice(off, off + width)
            gt = jnp.dot(nb, wg_ref[j, :, cols], preferred_element_type=f32)
            up = jnp.dot(nb, wg_ref[2 + j, :, cols], preferred_element_type=f32)
            gate_ref[:, cols] = gt.astype(bf16)
            up_ref[:, cols] = up.astype(bf16)
            a_sc[:, cols] = (gt * _sigmoid(gt) * up).astype(bf16)
        acc_sc[...] += jnp.dot(a_sc[...], wd_ref[j], preferred_element_type=f32)

        @pl.when(j == nj - 1)
        def _():
            ho_ref[...] = h_ref[...] + 0.5 * acc_sc[...]

    tm = TM
    return pl.pallas_call(
        body, name=name, grid=(t // tm, nj),
        in_specs=[
            pl.BlockSpec((tm, D), lambda i, j: (i, 0)),
            pl.BlockSpec((1, D), lambda i, j: (0, 0)),
            RESIDENT, RESIDENT,
        ],
        out_specs=[
            pl.BlockSpec((tm, D), lambda i, j: (i, 0)),
            pl.BlockSpec((tm, FS), lambda i, j: (i, j)),
            pl.BlockSpec((tm, FS), lambda i, j: (i, j)),
            pl.BlockSpec((tm, D), lambda i, j: (i, 0)),
        ],
        out_shape=[
            jax.ShapeDtypeStruct((t, D), f32),
            jax.ShapeDtypeStruct((t, F), bf16),
            jax.ShapeDtypeStruct((t, F), bf16),
            jax.ShapeDtypeStruct((t, D), bf16),
        ],
        scratch_shapes=[pltpu.VMEM((tm, D), bf16), pltpu.VMEM((tm, D), f32), pltpu.VMEM((tm, FS), bf16)],
        compiler_params=_cp(2),
    )(h, g, wgu, wd.reshape(nj, FS, D))


def _inproj_fwd(h, g, win, b_in):
    t = h.shape[0]
    tn = NIN // NCHIP
    nj = NIN // tn
    per = (NIN // NCHIP) // tn

    def body(h_ref, g_ref, w_ref, b_ref, proj_ref, n_ref, nb_sc):
        j = pl.program_id(1)

        @pl.when(j == 0)
        def _():
            hh = h_ref[...]
            nb = (hh * _rms_stats(hh) * g_ref[...]).astype(bf16)
            nb_sc[...] = nb
            n_ref[...] = nb

        proj_ref[...] = jnp.dot(nb_sc[...], w_ref[j], preferred_element_type=f32) + b_ref[...]

    return pl.pallas_call(
        body, name="inproj_fwd", grid=(t // TM, nj),
        in_specs=[
            pl.BlockSpec((TM, D), lambda i, j: (i, 0)),
            pl.BlockSpec((1, D), lambda i, j: (0, 0)),
            RESIDENT,
            pl.BlockSpec((1, tn), lambda i, j: (0, j)),
        ],
        out_specs=[
            pl.BlockSpec((TM, tn), lambda i, j: (i, j)),
            pl.BlockSpec((TM, D), lambda i, j: (i, 0)),
        ],
        out_shape=[jax.ShapeDtypeStruct((t, NIN), f32), jax.ShapeDtypeStruct((t, D), bf16)],
        scratch_shapes=[pltpu.VMEM((TM, D), bf16)],
        compiler_params=_cp(2),
    )(h, g, win, b_in)


def _block_gates(xr, wa_ref, ba, wx_ref, bx, lam):
    xrb = xr.astype(bf16)
    pa = jnp.concatenate([jnp.dot(xrb[:, hh * HD:(hh + 1) * HD], wa_ref[hh], preferred_element_type=f32)
                          for hh in range(NHEAD)], axis=1)
    px = jnp.concatenate([jnp.dot(xrb[:, hh * HD:(hh + 1) * HD], wx_ref[hh], preferred_element_type=f32)
                          for hh in range(NHEAD)], axis=1)
    ra = _sigmoid(pa + ba)
    ii = _sigmoid(px + bx)
    sp = _softplus(-lam)
    log_a = -8.0 * ra * sp
    a = jnp.exp(log_a)
    sq = jnp.sqrt(_one_minus_square(a, log_a))
    return ra, ii, a, sq, sp


def _rnn_fwd(proj, cw, cb, wa, ba, wx, bx, lam):
    t = proj.shape[0]
    ng = TM // 8

    def body(x_ref, y_ref, cw_ref, cb_ref, wa_ref, ba_ref, wx_ref, bx_ref, lam_ref,
             xr_ref, hr_ref, z_ref, gates_ref, xext_sc, carry_sc, a_sc, h_sc):
        i = pl.program_id(0)

        @pl.when(i == 0)
        def _():
            xext_sc[0:8, :] = jnp.zeros((8, D), f32)
            carry_sc[...] = jnp.zeros_like(carry_sc)

        x = x_ref[...]
        xext_sc[8:8 + TM, :] = x
        xe = xext_sc[...]
        xr = cb_ref[...] + cw_ref[KC4 - 1:KC4, :] * x
        for k in range(KC4 - 1):
            xr = xr + cw_ref[k:k + 1, :] * pltpu.roll(xe, KC4 - 1 - k, 0)[8:8 + TM]
        xext_sc[0:8, :] = x[TM - 8:TM]

        ra, ii, a, sq, _ = _block_gates(xr, wa_ref, ba_ref[...], wx_ref, bx_ref[...], lam_ref[...])
        for slot, val in enumerate((ra, ii, a, sq)):
            gates_ref[slot] = val
        a_sc[...] = a
        h_sc[...] = sq * ii * xr
        row = _row_ids((8, D))

        def group(r, carry):
            off = pl.multiple_of(r * 8, 8)
            aa = a_sc[pl.ds(off, 8), :]
            hh = h_sc[pl.ds(off, 8), :]
            for s in (1, 2, 4):
                a_sh = jnp.where(row >= s, pltpu.roll(aa, s, 0), 1.0)
                h_sh = jnp.where(row >= s, pltpu.roll(hh, s, 0), 0.0)
                hh = aa * h_sh + hh
                aa = aa * a_sh
            hh = hh + aa * carry
            h_sc[pl.ds(off, 8), :] = hh
            return hh[7:8, :]

        carry_sc[...] = lax.fori_loop(0, ng, group, carry_sc[...])
        hr = h_sc[...]
        gel, _ = _gelu_and_grad(y_ref[...])
        xr_ref[...] = xr
        hr_ref[...] = hr
        z_ref[...] = (hr * gel).astype(bf16)

    vec = pl.BlockSpec((1, D), lambda i: (0, 0))
    return pl.pallas_call(
        body, name="rnn_fwd", grid=(t // TM,),
        in_specs=[
            pl.BlockSpec((TM, D), lambda i: (i, 0)),
            pl.BlockSpec((TM, D), lambda i: (i, 1)),
            pl.BlockSpec((KC4, D), lambda i: (0, 0)),
            vec,
            pl.BlockSpec((NHEAD, HD, HD), lambda i: (0, 0, 0)),
            vec,
            pl.BlockSpec((NHEAD, HD, HD), lambda i: (0, 0, 0)),
            vec, vec,
        ],
        out_specs=[pl.BlockSpec((TM, D), lambda i: (i, 0))] * 3 + [pl.BlockSpec((4, TM, D), lambda i: (0, i, 0))],
        out_shape=[jax.ShapeDtypeStruct((t, D), f32), jax.ShapeDtypeStruct((t, D), f32),
                   jax.ShapeDtypeStruct((t, D), bf16), jax.ShapeDtypeStruct((4, t, D), f32)],
        scratch_shapes=[pltpu.VMEM((TM + 8, D), f32), pltpu.VMEM((1, D), f32),
                        pltpu.VMEM((TM, D), f32), pltpu.VMEM((TM, D), f32)],
        compiler_params=_cp(1),
    )(proj, proj, cw, cb, wa, ba, wx, bx, lam)


def _ln_stats(vc):
    mu = jnp.mean(vc, axis=-1, keepdims=True)
    xc = vc - mu
    rstd = lax.rsqrt(jnp.mean(xc * xc, axis=-1, keepdims=True) + EPS)
    return xc * rstd, rstd


def _conv_fwd(proj, w31, b31, ln_g, ln_b, after=None):
    t = proj.shape[0]

    def body(gv_ref, gg_ref, w_ref, b_ref, lg_ref, lb_ref, vc_ref, s_ref, vext_sc):
        i = pl.program_id(0)

        @pl.when(i == 0)
        def _():
            vext_sc[0:HALO, :] = jnp.zeros((HALO, D), f32)

        v = gv_ref[...] * _sigmoid(gg_ref[...])
        vext_sc[HALO:HALO + TM, :] = v
        ve = vext_sc[...]
        acc = jnp.zeros((TM, D), f32) + b_ref[...]
        for s in range(8):
            vs = ve if s == 0 else pltpu.roll(ve, s, 0)
            for m in range(HALO // 8):
                k = KC31 - 1 - (8 * m + s)
                if 0 <= k < KC31:
                    acc = acc + w_ref[k:k + 1, :] * vs[HALO - 8 * m:HALO - 8 * m + TM]
        vext_sc[0:HALO, :] = v[TM - HALO:TM]
        xhat, _ = _ln_stats(acc)
        ln = xhat * lg_ref[...] + lb_ref[...]
        vc_ref[...] = acc
        s_ref[...] = (ln * _sigmoid(ln)).astype(bf16)

    vec = pl.BlockSpec((1, D), lambda i: (0, 0))
    body, in_specs, args = _ordered(
        body,
        [pl.BlockSpec((TM, D), lambda i: (i, 2)),
         pl.BlockSpec((TM, D), lambda i: (i, 3)),
         pl.BlockSpec((KC31, D), lambda i: (0, 0)),
         vec, vec, vec],
        (proj, proj, w31, b31, ln_g, ln_b), after)
    return pl.pallas_call(
        body, name="conv_fwd", grid=(t // TM,),
        in_specs=in_specs,
        out_specs=[pl.BlockSpec((TM, D), lambda i: (i, 0))] * 2,
        out_shape=[jax.ShapeDtypeStruct((t, D), f32), jax.ShapeDtypeStruct((t, D), bf16)],
        scratch_shapes=[pltpu.VMEM((TM + HALO, D), f32)],
        compiler_params=_cp(1),
    )(*args)


def _merge_fwd(h, z, s, proj, wrp, wcp, bcp, wout):
    t = h.shape[0]

    def body(h_ref, z_ref, s_ref, ga_ref, gb_ref, wrp_ref, wcp_ref, bcp_ref, wout_ref, ho_ref):
        ya = jnp.dot(z_ref[...], wrp_ref[...], preferred_element_type=f32)
        yb = jnp.dot(s_ref[...], wcp_ref[...], preferred_element_type=f32) + bcp_ref[...]
        merged = _sigmoid(ga_ref[...]) * ya + _sigmoid(gb_ref[...]) * yb
        ho_ref[...] = h_ref[...] + jnp.dot(merged.astype(bf16), wout_ref[...], preferred_element_type=f32)

    row = pl.BlockSpec((TM, D), lambda i: (i, 0))
    wsq = pl.BlockSpec((D, D), lambda i: (0, 0))
    return pl.pallas_call(
        body, name="merge_fwd", grid=(t // TM,),
        in_specs=[row, row, row,
                  pl.BlockSpec((TM, D), lambda i: (i, 4)),
                  pl.BlockSpec((TM, D), lambda i: (i, 5)),
                  wsq, wsq, pl.BlockSpec((1, D), lambda i: (0, 0)), wsq],
        out_specs=row,
        out_shape=jax.ShapeDtypeStruct((t, D), f32),
        compiler_params=_cp(1),
    )(h, z, s, proj, proj, wrp, wcp, bcp, wout)


def _final_loss(h, g, tgt, n_valid):
    t = h.shape[0]

    def body(h_ref, g_ref, t_ref, dh_ref, loss_ref, dg_ref):
        i = pl.program_id(0)

        @pl.when(i == 0)
        def _():
            loss_ref[...] = jnp.zeros_like(loss_ref)
            dg_ref[...] = jnp.zeros_like(dg_ref)

        hh = h_ref[...]
        gg = g_ref[...]
        row = i * TM + _row_ids((TM, 1))
        valid = jnp.logical_and(row >= NMETA, row < n_valid)
        out = hh * _rms_stats(hh) * gg
        err = jnp.where(valid, out - t_ref[...], 0.0)
        loss_ref[...] += 0.5 * jnp.sum(err * err) * (1.0 / D)
        dh, dg = _rms_bwd(err * (1.0 / D), hh, gg)
        dh_ref[...] = dh
        dg_ref[...] += dg

    row_spec = pl.BlockSpec((TM, D), lambda i: (i, 0))
    return pl.pallas_call(
        body, name="final_loss", grid=(t // TM,),
        in_specs=[row_spec, pl.BlockSpec((1, D), lambda i: (0, 0)), row_spec],
        out_specs=[row_spec, pl.BlockSpec((8, 128), lambda i: (0, 0)), pl.BlockSpec((1, D), lambda i: (0, 0))],
        out_shape=[jax.ShapeDtypeStruct((t, D), f32), jax.ShapeDtypeStruct((8, 128), f32),
                   jax.ShapeDtypeStruct((1, D), f32)],
        compiler_params=_cp(1),
    )(h, g, tgt)


def _ffn_bwd(dh, h, g, gate, up, wgu, wd, name, after=None):
    t = h.shape[0]
    nj = 2

    def body(dh_ref, h_ref, g_ref, gate_ref, up_ref, wg_ref, wd_ref,
             dhi_ref, dgate_ref, dup_ref, a_ref, df_ref, dg_ref, dfb_sc, dn_sc):
        i = pl.program_id(0)
        j = pl.program_id(1)

        @pl.when(jnp.logical_and(i == 0, j == 0))
        def _():
            dg_ref[...] = jnp.zeros_like(dg_ref)

        @pl.when(j == 0)
        def _():
            dfb = (0.5 * dh_ref[...]).astype(bf16)
            dfb_sc[...] = dfb
            df_ref[...] = dfb
            dn_sc[...] = jnp.zeros_like(dn_sc)

        dfb = dfb_sc[...]
        for off, width in FSUB:
            cols = slice(off, off + width)
            da = _nt_dot(dfb, wd_ref[j, cols, :])
            gt = gate_ref[:, cols].astype(f32)
            uu = up_ref[:, cols].astype(f32)
            sg = _sigmoid(gt)
            silu = gt * sg
            a_ref[:, cols] = (silu * uu).astype(bf16)
            dgate_ref[:, cols] = (da * uu * (sg * (1.0 + gt * (1.0 - sg)))).astype(bf16)
            dup_ref[:, cols] = (da * silu).astype(bf16)
        dn_sc[...] += _nt_dot(dgate_ref[...], wg_ref[j]) + _nt_dot(dup_ref[...], wg_ref[2 + j])

        @pl.when(j == nj - 1)
        def _():
            dhin, dg = _rms_bwd(dn_sc[...], h_ref[...], g_ref[...])
            dhi_ref[...] = dh_ref[...] + dhin
            dg_ref[...] += dg

    rowd = pl.BlockSpec((TM, D), lambda i, j: (i, 0))
    rowf = pl.BlockSpec((TM, FS), lambda i, j: (i, j))
    vec = pl.BlockSpec((1, D), lambda i, j: (0, 0))
    body, in_specs, args = _ordered(
        body,
        [rowd, rowd, vec, rowf, rowf,
         RESIDENT, RESIDENT],
        (dh, h, g, gate, up, wgu, wd.reshape(nj, FS, D)), after)
    return pl.pallas_call(
        body, name=name, grid=(t // TM, nj),
        in_specs=in_specs,
        out_specs=[rowd, rowf, rowf, rowf, rowd, vec],
        out_shape=[jax.ShapeDtypeStruct((t, D), f32), jax.ShapeDtypeStruct((t, F), bf16),
                   jax.ShapeDtypeStruct((t, F), bf16), jax.ShapeDtypeStruct((t, F), bf16),
                   jax.ShapeDtypeStruct((t, D), bf16), jax.ShapeDtypeStruct((1, D), f32)],
        scratch_shapes=[pltpu.VMEM((TM, D), bf16), pltpu.VMEM((TM, D), f32)],
        compiler_params=_cp(2),
    )(*args)


def _big_tile(t):
    return max(k * TM for k in range(1, 6) if t % (k * TM) == 0)


ANY_SPEC = pl.BlockSpec(memory_space=pl.ANY)


def _tn_matmul(a, b, tk, tn, out_shape, out_block, out_map, name, base=None, after=None):
    t, kk = a.shape
    _, nn = b.shape
    tmm = _big_tile(t)
    nm = t // tmm

    def body(a_ref, b_ref, o_ref, acc_sc):
        m = pl.program_id(2)

        @pl.when(m == 0)
        def _():
            acc_sc[...] = jnp.zeros_like(acc_sc)

        acc_sc[...] += _tn_dot(a_ref[...], b_ref[...])

        @pl.when(m == nm - 1)
        def _():
            o_ref[...] = acc_sc[...].astype(o_ref.dtype)

    in_specs = [pl.BlockSpec((tmm, tk), lambda k, n, m: (m, k)),
                pl.BlockSpec((tmm, tn), lambda k, n, m: (m, n))]
    args, aliases = (a, b), {}
    if base is not None:
        body = (lambda inner: lambda a_ref, b_ref, base_ref, o_ref, acc_sc: inner(a_ref, b_ref, o_ref, acc_sc))(body)
        in_specs, args, aliases = in_specs + [ANY_SPEC], (a, b, base), {2: 0}
    if after is not None:
        body, in_specs, args = _ordered(body, in_specs, args, after)
        aliases = {k + _n_after(after): v for k, v in aliases.items()}
    return pl.pallas_call(
        body, name=name, grid=(kk // tk, nn // tn, nm),
        in_specs=in_specs,
        out_specs=pl.BlockSpec(out_block, out_map),
        out_shape=jax.ShapeDtypeStruct(out_shape, bf16),
        scratch_shapes=[pltpu.VMEM((tk, tn), f32)],
        input_output_aliases=aliases,
        compiler_params=_cp(3),
    )(*args)


def _merge_bwd(dh, z, s, proj, wrp, wcp, bcp, wout, after=None):
    t = dh.shape[0]

    def body(dh_ref, z_ref, s_ref, ga_ref, gb_ref, wrp_ref, wcp_ref, bcp_ref, wout_ref,
             dz_ref, ds_ref, dgab_ref, dhb_ref, mg_ref, dya_ref, dyb_ref, dbcp_ref):
        i = pl.program_id(0)

        @pl.when(i == 0)
        def _():
            dbcp_ref[...] = jnp.zeros_like(dbcp_ref)

        dhb = dh_ref[...].astype(bf16)
        dhb_ref[...] = dhb
        dmg = _nt_dot(dhb, wout_ref[...])
        ya = jnp.dot(z_ref[...], wrp_ref[...], preferred_element_type=f32)
        yb = jnp.dot(s_ref[...], wcp_ref[...], preferred_element_type=f32) + bcp_ref[...]
        sa = _sigmoid(ga_ref[...])
        sb = _sigmoid(gb_ref[...])
        mg_ref[...] = (sa * ya + sb * yb).astype(bf16)
        dgab_ref[:, 0:D] = (dmg * ya * sa * (1.0 - sa)).astype(bf16)
        dgab_ref[:, D:2 * D] = (dmg * yb * sb * (1.0 - sb)).astype(bf16)
        dya = dmg * sa
        dyb = dmg * sb
        dbcp_ref[...] += jnp.sum(dyb, axis=0, keepdims=True)
        dyab = dya.astype(bf16)
        dybb = dyb.astype(bf16)
        dya_ref[...] = dyab
        dyb_ref[...] = dybb
        dz_ref[...] = _nt_dot(dyab, wrp_ref[...])
        ds_ref[...] = _nt_dot(dybb, wcp_ref[...])

    row = pl.BlockSpec((TM, D), lambda i: (i, 0))
    wsq = pl.BlockSpec((D, D), lambda i: (0, 0))
    vec = pl.BlockSpec((1, D), lambda i: (0, 0))
    rowb = jax.ShapeDtypeStruct((t, D), bf16)
    body, in_specs, args = _ordered(
        body,
        [row, row, row,
         pl.BlockSpec((TM, D), lambda i: (i, 4)),
         pl.BlockSpec((TM, D), lambda i: (i, 5)),
         wsq, wsq, vec, wsq],
        (dh, z, s, proj, proj, wrp, wcp, bcp, wout), after)
    return pl.pallas_call(
        body, name="merge_bwd", grid=(t // TM,),
        in_specs=in_specs,
        out_specs=[row, row,
                   pl.BlockSpec((TM, 2 * D), lambda i: (i, 2)),
                   row, row, row, row, vec],
        out_shape=[jax.ShapeDtypeStruct((t, D), f32), jax.ShapeDtypeStruct((t, D), f32),
                   jax.ShapeDtypeStruct((t, NIN), bf16),
                   rowb, rowb, rowb, rowb, jax.ShapeDtypeStruct((1, D), f32)],
        compiler_params=_cp(1),
    )(*args)


def _conv_bwd(ds, vc, proj, dproj, w31, ln_g, ln_b, after=None):
    t = ds.shape[0]
    nt = t // TM
    hb = TM // HALO

    rb = 16
    nb = TM // rb
    taps = [(KC31 - 1 - (8 * m + s), s, m) for s in range(8) for m in range(HALO // 8)
            if 0 <= KC31 - 1 - (8 * m + s) < KC31]

    def groups(a):
        return jnp.sum(a.reshape(rb // 8, 8, D), axis=0)

    def body(ds_ref, vc_ref, gv_ref, gg_ref, gvp_ref, ggp_ref, dpin_ref, w_ref, lg_ref, lb_ref,
             dgvg_ref, dw_ref, db_ref, dlg_ref, dlb_ref, dext_sc, vext_sc, rot_sc, dwacc_sc, small_sc, wb_sc):
        del dpin_ref
        i = pl.program_id(0)
        tile = nt - 1 - i

        @pl.when(i == 0)
        def _():
            dext_sc[TM:TM + HALO, :] = jnp.zeros((HALO, D), f32)
            dwacc_sc[...] = jnp.zeros_like(dwacc_sc)
            small_sc[...] = jnp.zeros_like(small_sc)

        lg = lg_ref[...]
        lb = lb_ref[...]

        xhat, rstd = _ln_stats(vc_ref[...])
        ln = xhat * lg + lb
        sg = _sigmoid(ln)
        dln = ds_ref[...] * (sg * (1.0 + ln * (1.0 - sg)))
        dxh = dln * lg
        dvc = rstd * (dxh - jnp.mean(dxh, axis=-1, keepdims=True)
                      - xhat * jnp.mean(dxh * xhat, axis=-1, keepdims=True))
        small_sc[0] += jnp.sum((dln * xhat).reshape(TM // 8, 8, D), axis=0)
        small_sc[1] += jnp.sum(dln.reshape(TM // 8, 8, D), axis=0)
        small_sc[2] += jnp.sum(dvc.reshape(TM // 8, 8, D), axis=0)
        dext_sc[0:TM, :] = dvc
        vext_sc[HALO:HALO + TM, :] = gv_ref[...] * _sigmoid(gg_ref[...])
        vext_sc[0:HALO, :] = jnp.where(tile > 0, gvp_ref[...] * _sigmoid(ggp_ref[...]), 0.0)

        @pl.when(i == 0)
        def _():
            for k in range(KC31):
                wb_sc[k] = jnp.broadcast_to(w_ref[k:k + 1, :], (8, D))

        for s in range(1, 8):
            rot_sc[s - 1] = pltpu.roll(dext_sc[...], TM + HALO - s, 0)

        def dv_block(b, carry):
            rows = pl.ds(pl.multiple_of(b * rb, rb), rb)
            acc = jnp.zeros((rb, D), f32)
            for k, s, m in taps:
                src = pl.ds(pl.multiple_of(b * rb + 8 * m, 8), rb)
                slab = dext_sc[src, :] if s == 0 else rot_sc[s - 1, src, :]
                acc = acc + (slab.reshape(rb // 8, 8, D) * wb_sc[k]).reshape(rb, D)
            sgg = _sigmoid(gg_ref[rows, :])
            dgvg_ref[rows, 0:D] = (acc * sgg).astype(bf16)
            dgvg_ref[rows, D:2 * D] = (acc * gv_ref[rows, :] * sgg * (1.0 - sgg)).astype(bf16)
            return carry

        lax.fori_loop(0, nb, dv_block, 0)

        for s in range(1, 8):
            rot_sc[s - 1] = pltpu.roll(vext_sc[...], s, 0)
        for first in range(0, len(taps), 3):
            trio = taps[first:first + 3]

            def dw_block(b, accs, trio=trio):
                rows = pl.ds(pl.multiple_of(b * rb, rb), rb)
                dvc_blk = dext_sc[rows, :]
                out = []
                for acc, (k, s, m) in zip(accs, trio):
                    src = pl.ds(pl.multiple_of(b * rb + HALO - 8 * m, 8), rb)
                    slab = vext_sc[src, :] if s == 0 else rot_sc[s - 1, src, :]
                    out.append(acc + groups(dvc_blk * slab))
                return tuple(out)

            sums = lax.fori_loop(0, nb, dw_block, tuple(jnp.zeros((8, D), f32) for _ in trio))
            for acc, (k, s, m) in zip(sums, trio):
                dwacc_sc[k] += acc
        dext_sc[TM:TM + HALO, :] = dext_sc[0:HALO, :]

        @pl.when(i == nt - 1)
        def _():
            for k in range(KC31):
                dw_ref[k:k + 1, :] = jnp.sum(dwacc_sc[k], axis=0, keepdims=True)
            dlg_ref[...] = jnp.sum(small_sc[0], axis=0, keepdims=True)
            dlb_ref[...] = jnp.sum(small_sc[1], axis=0, keepdims=True)
            db_ref[...] = jnp.sum(small_sc[2], axis=0, keepdims=True)

    rev = lambda i: (nt - 1 - i, 0)
    vec = pl.BlockSpec((1, D), lambda i: (0, 0))
    halo_row = lambda i: jnp.maximum((nt - 1 - i) * hb - 1, 0)
    body, in_specs, args = _ordered(
        body,
        [pl.BlockSpec((TM, D), rev),
         pl.BlockSpec((TM, D), rev),
         pl.BlockSpec((TM, D), lambda i: (nt - 1 - i, 2)),
         pl.BlockSpec((TM, D), lambda i: (nt - 1 - i, 3)),
         pl.BlockSpec((HALO, D), lambda i: (halo_row(i), 2)),
         pl.BlockSpec((HALO, D), lambda i: (halo_row(i), 3)),
         pl.BlockSpec(memory_space=pl.ANY),
         pl.BlockSpec((KC31, D), lambda i: (0, 0)),
         vec, vec],
        (ds, vc, proj, proj, proj, proj, dproj, w31, ln_g, ln_b), after)
    return pl.pallas_call(
        body, name="conv_bwd", grid=(nt,),
        in_specs=in_specs,
        out_specs=[
            pl.BlockSpec((TM, 2 * D), lambda i: (nt - 1 - i, 1)),
            pl.BlockSpec((KC31, D), lambda i: (0, 0)),
            vec, vec, vec,
        ],
        out_shape=[jax.ShapeDtypeStruct((t, NIN), bf16),
                   jax.ShapeDtypeStruct((KC31, D), f32),
                   jax.ShapeDtypeStruct((1, D), f32), jax.ShapeDtypeStruct((1, D), f32),
                   jax.ShapeDtypeStruct((1, D), f32)],
        scratch_shapes=[pltpu.VMEM((TM + HALO, D), f32), pltpu.VMEM((TM + HALO, D), f32),
                        pltpu.VMEM((7, TM + HALO, D), f32), pltpu.VMEM((KC31, 8, D), f32),
                        pltpu.VMEM((3, 8, D), f32), pltpu.VMEM((KC31, 8, D), f32)],
        input_output_aliases={6 + _n_after(after): 0},
        compiler_params=_cp(1),
    )(*args)


def _rnn_bwd(dz, xr, hr, gates, proj, dproj, cw, wa, wx, lam):
    t = dz.shape[0]
    nt = t // TM
    ng = TM // 8
    hq = HD // NCHIP

    def body(dz_ref, xr_ref, hr_ref, hrp_ref, x_ref, xp_ref, y_ref, dpin_ref,
             cw_ref, wa_ref, gates_ref, wx_ref, lam_ref,
             dxy_ref, dwa_ref, dwx_ref, dcw_ref, dcb_ref, dba_ref, dbx_ref, dlam_ref,
             anext_sc, gcarry_sc, dext_sc, xext_sc, m_sc, g_sc, dwa_sc, dwx_sc, dsp_sc):
        del dpin_ref
        i = pl.program_id(0)
        tile = nt - 1 - i

        @pl.when(i == 0)
        def _():
            anext_sc[...] = jnp.zeros_like(anext_sc)
            gcarry_sc[...] = jnp.zeros_like(gcarry_sc)
            dext_sc[TM:TM + 8, :] = jnp.zeros((8, D), f32)
            dwa_sc[...] = jnp.zeros_like(dwa_sc)
            dwx_sc[...] = jnp.zeros_like(dwx_sc)
            dsp_sc[...] = jnp.zeros_like(dsp_sc)
            dcw_ref[...] = jnp.zeros_like(dcw_ref)
            dcb_ref[...] = jnp.zeros_like(dcb_ref)
            dba_ref[...] = jnp.zeros_like(dba_ref)
            dbx_ref[...] = jnp.zeros_like(dbx_ref)

        xr = xr_ref[...]
        hr = hr_ref[...]
        dz = dz_ref[...]
        gel, dgel = _gelu_and_grad(y_ref[...])
        dxy_ref[:, D:2 * D] = (dz * hr * dgel).astype(bf16)
        ra, ii, a, sq = gates_ref[0], gates_ref[1], gates_ref[2], gates_ref[3]
        sp = _softplus(-lam_ref[...])

        row = _row_ids((TM, D))
        m_sc[...] = jnp.where(row == TM - 1, anext_sc[...], pltpu.roll(a, TM - 1, 0))
        anext_sc[...] = a[0:1, :]
        g_sc[...] = dz * gel
        row8 = _row_ids((8, D))

        def group(qq, carry):
            off = pl.multiple_of((ng - 1 - qq) * 8, 8)
            mm = m_sc[pl.ds(off, 8), :]
            dd = g_sc[pl.ds(off, 8), :]
            for s in (1, 2, 4):
                m_sh = jnp.where(row8 < 8 - s, pltpu.roll(mm, 8 - s, 0), 1.0)
                d_sh = jnp.where(row8 < 8 - s, pltpu.roll(dd, 8 - s, 0), 0.0)
                dd = dd + mm * d_sh
                mm = mm * m_sh
            dd = dd + mm * carry
            g_sc[pl.ds(off, 8), :] = dd
            return dd[0:1, :]

        gcarry_sc[...] = lax.fori_loop(0, ng, group, gcarry_sc[...])
        gg = g_sc[...]

        hlast = jnp.where(tile > 0, hrp_ref[7:8, :], 0.0)
        hprev = jnp.where(row == 0, hlast, pltpu.roll(hr, 1, 0))
        d_a = gg * hprev
        dsq = gg * ii * xr
        dii = gg * sq * xr
        dxr = gg * sq * ii
        dlog = d_a * a - dsq * (a * a / sq)
        dsp_sc[...] += jnp.sum(dlog * (-8.0 * ra), axis=0, keepdims=True)
        dpa = dlog * (-8.0 * sp) * ra * (1.0 - ra)
        dpx = dii * ii * (1.0 - ii)
        dba_ref[...] += jnp.sum(dpa, axis=0, keepdims=True)
        dbx_ref[...] += jnp.sum(dpx, axis=0, keepdims=True)
        dpab = dpa.astype(bf16)
        dpxb = dpx.astype(bf16)
        xrb = xr.astype(bf16)
        back = []
        for hh in range(NHEAD):
            cols = slice(hh * HD, (hh + 1) * HD)
            back.append(_nt_dot(dpab[:, cols], wa_ref[hh]) + _nt_dot(dpxb[:, cols], wx_ref[hh]))
            dwa_sc[hh] += _tn_dot(xrb[:, cols], dpab[:, cols])
            dwx_sc[hh] += _tn_dot(xrb[:, cols], dpxb[:, cols])
        dxr = dxr + jnp.concatenate(back, axis=1)

        dext_sc[0:TM, :] = dxr
        de = dext_sc[...]
        dx = cw_ref[KC4 - 1:KC4, :] * dxr
        for k in range(KC4 - 1):
            dx = dx + cw_ref[k:k + 1, :] * pltpu.roll(de, TM + 8 - (KC4 - 1 - k), 0)[0:TM]
        dext_sc[TM:TM + 8, :] = dxr[0:8]
        dxy_ref[:, 0:D] = dx.astype(bf16)

        x = x_ref[...]
        xext_sc[0:8, :] = jnp.where(tile > 0, xp_ref[...], 0.0)
        xext_sc[8:8 + TM, :] = x
        xe = xext_sc[...]
        dcw_ref[KC4 - 1:KC4, :] += jnp.sum(dxr * x, axis=0, keepdims=True)
        for k in range(KC4 - 1):
            xs = pltpu.roll(xe, KC4 - 1 - k, 0)[8:8 + TM]
            dcw_ref[k:k + 1, :] += jnp.sum(dxr * xs, axis=0, keepdims=True)
        dcb_ref[...] += jnp.sum(dxr, axis=0, keepdims=True)

        @pl.when(i == nt - 1)
        def _():
            for hh in range(NHEAD):
                for qc in range(NCHIP):
                    dwa_ref[qc, hh] = dwa_sc[hh, qc * hq:(qc + 1) * hq, :].astype(bf16)
                    dwx_ref[qc, hh] = dwx_sc[hh, qc * hq:(qc + 1) * hq, :].astype(bf16)
            dlam_ref[...] = -dsp_sc[...] * _sigmoid(-lam_ref[...])

    rev = lambda i: (nt - 1 - i, 0)
    vec = pl.BlockSpec((1, D), lambda i: (0, 0))
    prev8 = lambda i: jnp.maximum((nt - 1 - i) * ng - 1, 0)
    wblk = pl.BlockSpec((NHEAD, HD, HD), lambda i: (0, 0, 0))
    gblk = pl.BlockSpec((NCHIP, NHEAD, hq, HD), lambda i: (0, 0, 0, 0))
    return pl.pallas_call(
        body, name="rnn_bwd", grid=(nt,),
        in_specs=[
            pl.BlockSpec((TM, D), rev),
            pl.BlockSpec((TM, D), rev),
            pl.BlockSpec((TM, D), rev),
            pl.BlockSpec((8, D), lambda i: (prev8(i), 0)),
            pl.BlockSpec((TM, D), lambda i: (nt - 1 - i, 0)),
            pl.BlockSpec((8, D), lambda i: (prev8(i), 0)),
            pl.BlockSpec((TM, D), lambda i: (nt - 1 - i, 1)),
            pl.BlockSpec(memory_space=pl.ANY),
            pl.BlockSpec((KC4, D), lambda i: (0, 0)),
            wblk, pl.BlockSpec((4, TM, D), lambda i: (0, nt - 1 - i, 0)), wblk, vec,
        ],
        out_specs=[
            pl.BlockSpec((TM, 2 * D), lambda i: (nt - 1 - i, 0)),
            gblk, gblk,
            pl.BlockSpec((KC4, D), lambda i: (0, 0)),
            vec, vec, vec, vec,
        ],
        out_shape=[jax.ShapeDtypeStruct((t, NIN), bf16),
                   jax.ShapeDtypeStruct((NCHIP, NHEAD, hq, HD), bf16),
                   jax.ShapeDtypeStruct((NCHIP, NHEAD, hq, HD), bf16),
                   jax.ShapeDtypeStruct((KC4, D), f32),
                   jax.ShapeDtypeStruct((1, D), f32), jax.ShapeDtypeStruct((1, D), f32),
                   jax.ShapeDtypeStruct((1, D), f32), jax.ShapeDtypeStruct((1, D), f32)],
        scratch_shapes=[pltpu.VMEM((1, D), f32), pltpu.VMEM((1, D), f32),
                        pltpu.VMEM((TM + 8, D), f32), pltpu.VMEM((TM + 8, D), f32),
                        pltpu.VMEM((TM, D), f32), pltpu.VMEM((TM, D), f32),
                        pltpu.VMEM((NHEAD, HD, HD), f32), pltpu.VMEM((NHEAD, HD, HD), f32),
                        pltpu.VMEM((1, D), f32)],
        input_output_aliases={7: 0},
        compiler_params=_cp(1),
    )(dz, xr, hr, hr, proj, proj, proj, dproj, cw, wa, gates, wx, lam)


def _inproj_bwd(dproj, dh, h, g, win, after=None):
    t = h.shape[0]
    tn = NIN // NCHIP
    nj = NIN // tn
    per = (NIN // NCHIP) // tn

    def body(dp_ref, dh_ref, h_ref, g_ref, w_ref, dhi_ref, dg_ref, db_ref, dn_sc):
        i = pl.program_id(0)
        j = pl.program_id(1)

        @pl.when(jnp.logical_and(i == 0, j == 0))
        def _():
            dg_ref[...] = jnp.zeros_like(dg_ref)
            db_ref[...] = jnp.zeros_like(db_ref)

        @pl.when(j == 0)
        def _():
            dn_sc[...] = jnp.zeros_like(dn_sc)

        dp = dp_ref[...]
        dn_sc[...] += _nt_dot(dp, w_ref[j])
        db_ref[j] += jnp.sum(dp.astype(f32), axis=0, keepdims=True)

        @pl.when(j == nj - 1)
        def _():
            dhin, dg = _rms_bwd(dn_sc[...], h_ref[...], g_ref[...])
            dhi_ref[...] = dh_ref[...] + dhin
            dg_ref[...] += dg

    rowd = pl.BlockSpec((TM, D), lambda i, j: (i, 0))
    vec = pl.BlockSpec((1, D), lambda i, j: (0, 0))
    body, in_specs, args = _ordered(
        body,
        [pl.BlockSpec((TM, tn), lambda i, j: (i, j)), rowd, rowd, vec,
         RESIDENT],
        (dproj, dh, h, g, win), after)
    return pl.pallas_call(
        body, name="inproj_bwd", grid=(t // TM, nj),
        in_specs=in_specs,
        out_specs=[rowd, vec, pl.BlockSpec((nj, 1, tn), lambda i, j: (0, 0, 0))],
        out_shape=[jax.ShapeDtypeStruct((t, D), f32), jax.ShapeDtypeStruct((1, D), f32),
                   jax.ShapeDtypeStruct((nj, 1, tn), f32)],
        scratch_shapes=[pltpu.VMEM((TM, D), f32)],
        compiler_params=_cp(2),
    )(*args)


def _ffn_gu_grad(n, dgate, dup, tag, after=None):
    half = _tn_matmul(n, dgate, D, FS, (NCHIP, D, FS), (None, D, FS), lambda k, nn, m: (nn, 0, 0),
                      tag + "_dwg", after=after)
    return _tn_matmul(n, dup, D, FS, (NCHIP, D, FS), (None, D, FS), lambda k, nn, m: (2 + nn, 0, 0),
                      tag + "_dwu", base=half)


def _ffn_down_grad(a, df, tag, after=None):
    return _tn_matmul(a, df, FS, D, (F, D), (FS, D), lambda k, nn, m: (k, 0), tag + "_dwd", after=after)


def _square_grad(a, b, name):
    return _tn_matmul(a, b, D, D, (D, D), (D, D), lambda k, nn, m: (0, 0), name)


ANY = pl.BlockSpec(memory_space=pl.ANY)


def _place():
    x, y, c = lax.axis_index("x"), lax.axis_index("y"), lax.axis_index("c")
    chips = [(1 - x, y), (x, 1 - y), (1 - x, 1 - y)]
    return x, y, c, chips


def _chip_id(chip):
    return 2 * chip[0] + chip[1]


def _cast_into_slot(w2d, qc, dtype, name, after=None):
    r, cc = w2d.shape
    hr = r // 2

    def body(qc_ref, *refs):
        del qc_ref
        w_ref, o_ref = refs[-2:]
        o_ref[...] = w_ref[...].astype(dtype)

    in_specs, args = [pl.BlockSpec((hr, cc), lambda h, qc_ref: (h, 0))], (w2d,)
    if after is not None:
        in_specs, args = [ANY_SPEC] + in_specs, (after,) + args
    return pl.pallas_call(
        body, name=name,
        grid_spec=pltpu.PrefetchScalarGridSpec(
            num_scalar_prefetch=1, grid=(2,),
            in_specs=in_specs,
            out_specs=pl.BlockSpec((None, None, hr, cc), lambda h, qc_ref: (qc_ref[0], h, 0, 0))),
        out_shape=jax.ShapeDtypeStruct((NCHIP, 2, hr, cc), dtype),
        compiler_params=_cp(1),
    )(qc, *args)


def _place_pack(pack, qc):
    def body(qc_ref, p_ref, o_ref):
        del qc_ref
        o_ref[...] = p_ref[...]

    return pl.pallas_call(
        body, name="place_pack",
        grid_spec=pltpu.PrefetchScalarGridSpec(
            num_scalar_prefetch=1, grid=(1,),
            in_specs=[pl.BlockSpec(pack.shape, lambda i, qc_ref: (0, 0))],
            out_specs=pl.BlockSpec((None,) + pack.shape, lambda i, qc_ref: (2 * qc_ref[0] + qc_ref[1], 0, 0))),
        out_shape=jax.ShapeDtypeStruct((8,) + pack.shape, pack.dtype),
        compiler_params=_cp(1),
    )(qc, pack)


def _pair_add(parts, gots, qc, name):
    n = len(parts)

    def body(qc_ref, *refs):
        s = pl.program_id(0)
        for a in range(n):
            val = (refs[a][...].astype(f32) + refs[n + a][...].astype(f32)).astype(bf16)
            refs[2 * n + a][...] = val

            @pl.when(s == qc_ref[0])
            def _(val=val, land_ref=refs[3 * n + a]):
                land_ref[...] = val

    shapes = [p.shape[2:] for p in parts]
    mine = [pl.BlockSpec((None, None) + sh, lambda s, qc_ref: (s, qc_ref[1], 0, 0)) for sh in shapes]
    block = [pl.BlockSpec((None,) + sh, lambda s, qc_ref: (s, 0, 0)) for sh in shapes]
    own = [pl.BlockSpec((None,) + sh, lambda s, qc_ref: (qc_ref[0], 0, 0)) for sh in shapes]
    outs = pl.pallas_call(
        body, name=name,
        grid_spec=pltpu.PrefetchScalarGridSpec(
            num_scalar_prefetch=1, grid=(NCHIP,), in_specs=mine + block, out_specs=block + own),
        out_shape=[jax.ShapeDtypeStruct((NCHIP,) + sh, bf16) for sh in shapes] * 2,
        compiler_params=_cp(1),
    )(qc, *parts, *gots)
    return list(outs[:n]), list(outs[n:])


def _sum_chips(gots, name):
    n = len(gots)

    def body(*refs):
        for a in range(n):
            acc = refs[a][0].astype(f32)
            for s in range(1, NCHIP):
                acc = acc + refs[a][s].astype(f32)
            refs[n + a][...] = acc

    return list(pl.pallas_call(
        body, name=name, grid=(1,),
        in_specs=[pl.BlockSpec(g.shape, lambda i: (0, 0, 0)) for g in gots],
        out_specs=[pl.BlockSpec(g.shape[1:], lambda i: (0, 0)) for g in gots],
        out_shape=[jax.ShapeDtypeStruct(g.shape[1:], f32) for g in gots],
        compiler_params=_cp(1),
    )(*gots))


def _pair_share(halves, name, after=None):
    n = len(halves)
    extra = () if after is None else (after,)

    def body(*refs):
        refs = refs[len(extra):]
        ins, outs = refs[:n], refs[n:2 * n]
        send_sems, recv_sems = refs[2 * n:]
        x, y, c, _ = _place()
        copies = []
        for a in range(n):
            cp = pltpu.make_async_remote_copy(
                src_ref=ins[a], dst_ref=outs[a], send_sem=send_sems.at[a], recv_sem=recv_sems.at[a],
                device_id=(x, y, 1 - c), device_id_type=MESH)
            cp.start()
            copies.append(cp)
        for cp in copies:
            cp.wait()

    return pl.pallas_call(
        body, name=name,
        in_specs=[ANY] * (len(extra) + n), out_specs=[ANY] * n,
        out_shape=[jax.ShapeDtypeStruct(s.shape, s.dtype) for s in halves],
        scratch_shapes=[pltpu.SemaphoreType.DMA((n,)), pltpu.SemaphoreType.DMA((n,))],
    )(*extra, *halves)


def _all_copy(buf_ref, send_ref, recv_ref, k, x, y, c, landing):
    px, py, pc = (1 - x if k & 4 else x, 1 - y if k & 2 else y, 1 - c if k & 1 else c)
    me = 4 * x + 2 * y + c
    there = 4 * px + 2 * py + pc
    return pltpu.make_async_remote_copy(
        src_ref=buf_ref.at[me], dst_ref=buf_ref.at[there if landing else me],
        send_sem=send_ref.at[k - 1], recv_sem=recv_ref.at[k - 1],
        device_id=(px, py, pc), device_id_type=MESH)


def _gather_all_start(buf, name):
    def body(in_ref, send, recv, thru, token):
        del thru
        x, y, c, _ = _place()
        for k in range(1, 8):
            _all_copy(in_ref, send, recv, k, x, y, c, False).start()
        token[...] = jnp.zeros_like(token)

    return pl.pallas_call(
        body, name=name,
        in_specs=[HBM],
        out_specs=[SEM, SEM, HBM, pl.BlockSpec(memory_space=pltpu.VMEM)],
        out_shape=[pltpu.SemaphoreType.DMA((7,)), pltpu.SemaphoreType.DMA((7,)),
                   pltpu.HBM(buf.shape, buf.dtype), jax.ShapeDtypeStruct((8, 128), f32)],
        input_output_aliases={0: 2},
        compiler_params=pltpu.CompilerParams(has_side_effects=EFFECT),
    )(_in_hbm(buf))


def _gather_all_wait(send, recv, buf, after, name):
    def body(in_ref, send_r, recv_r, after_ref, out_ref):
        del after_ref, out_ref
        x, y, c, _ = _place()
        for k in range(1, 8):
            cp = _all_copy(in_ref, send_r, recv_r, k, x, y, c, True)
            cp.wait_send()
            cp.wait_recv()

    return pl.pallas_call(
        body, name=name,
        in_specs=[HBM, SEM, SEM, ANY],
        out_specs=HBM,
        out_shape=pltpu.HBM(buf.shape, buf.dtype),
        input_output_aliases={0: 0},
        compiler_params=pltpu.CompilerParams(has_side_effects=EFFECT),
    )(buf, send, recv, after)


HBM = pl.BlockSpec(memory_space=pltpu.HBM)
SEM = pl.BlockSpec(memory_space=pltpu.SEMAPHORE)
EFFECT = pltpu.SideEffectType.DATAFLOW_SIDE_EFFECTING
N_PEER = 3


def _in_hbm(a):
    return pltpu.with_memory_space_constraint(a, pltpu.HBM)


def _gather_copy(buf_ref, send_ref, recv_ref, j, chip, q, c, landing_chip):
    return pltpu.make_async_remote_copy(
        src_ref=buf_ref.at[q, c], dst_ref=buf_ref.at[landing_chip, c],
        send_sem=send_ref.at[j], recv_sem=recv_ref.at[j],
        device_id=(chip[0], chip[1], c), device_id_type=MESH)


def _gather_start(bufs, name):
    n = len(bufs)

    def body(*refs):
        ins = refs[:n]
        send, recv = refs[n:2 * n], refs[2 * n:3 * n]
        token = refs[4 * n]
        x, y, c, chips = _place()
        q = 2 * x + y
        for a in range(n):
            for j, chip in enumerate(chips):
                _gather_copy(ins[a], send[a], recv[a], j, chip, q, c, q).start()
        token[...] = jnp.zeros_like(token)

    sems = [pltpu.SemaphoreType.DMA((N_PEER,))] * (2 * n)
    outs = pl.pallas_call(
        body, name=name,
        in_specs=[HBM] * n,
        out_specs=[SEM] * (2 * n) + [HBM] * n + [pl.BlockSpec(memory_space=pltpu.VMEM)],
        out_shape=sems + [pltpu.HBM(b.shape, b.dtype) for b in bufs] + [jax.ShapeDtypeStruct((8, 128), f32)],
        input_output_aliases={a: 2 * n + a for a in range(n)},
        compiler_params=pltpu.CompilerParams(has_side_effects=EFFECT),
    )(*[_in_hbm(b) for b in bufs])
    return list(outs[:n]), list(outs[n:2 * n]), list(outs[2 * n:3 * n]), outs[3 * n]


def _gather_wait(send, recv, bufs, after, name):
    n = len(bufs)

    def body(*refs):
        ins = refs[:n]
        send_r, recv_r = refs[n:2 * n], refs[2 * n:3 * n]
        x, y, c, chips = _place()
        q = 2 * x + y
        for a in range(n):
            for j, chip in enumerate(chips):
                cp = _gather_copy(ins[a], send_r[a], recv_r[a], j, chip, q, c, _chip_id(chip))
                cp.wait_send()
                cp.wait_recv()

    afters = after if isinstance(after, (tuple, list)) else (after,)
    outs = pl.pallas_call(
        body, name=name,
        in_specs=[HBM] * n + [SEM] * (2 * n) + [ANY] * len(afters),
        out_specs=[HBM] * n,
        out_shape=[pltpu.HBM(b.shape, b.dtype) for b in bufs],
        input_output_aliases={a: a for a in range(n)},
        compiler_params=pltpu.CompilerParams(has_side_effects=EFFECT),
    )(*bufs, *send, *recv, *afters)
    return list(outs)


def _forward_halves(bufs, name):
    n = len(bufs)

    def body(*refs):
        outs = refs[n:2 * n]
        send_sems, recv_sems = refs[2 * n:]
        x, y, c, chips = _place()
        sibling = (x, y, 1 - c)

        def remote(a, j, blk):
            return pltpu.make_async_remote_copy(src_ref=blk, dst_ref=blk, send_sem=send_sems.at[a, j],
                                                recv_sem=recv_sems.at[a, j], device_id=sibling,
                                                device_id_type=MESH)

        sent = []
        for a in range(n):
            for j, chip in enumerate(chips):
                cp = remote(a, j, outs[a].at[_chip_id(chip), c])
                cp.start()
                sent.append(cp)
        for a in range(n):
            for j, chip in enumerate(chips):
                remote(a, j, outs[a].at[_chip_id(chip), 1 - c]).wait_recv()
        for cp in sent:
            cp.wait_send()

    return pl.pallas_call(
        body, name=name,
        in_specs=[ANY] * n, out_specs=[ANY] * n,
        out_shape=[jax.ShapeDtypeStruct(s.shape, s.dtype) for s in bufs],
        scratch_shapes=[pltpu.SemaphoreType.DMA((n, N_PEER)), pltpu.SemaphoreType.DMA((n, N_PEER))],
        input_output_aliases={a: a for a in range(n)},
    )(*bufs)


def _reduce_copy(sum_ref, land_ref, send_ref, recv_ref, j, chip, q, c, landing_chip):
    return pltpu.make_async_remote_copy(
        src_ref=sum_ref.at[_chip_id(chip)], dst_ref=land_ref.at[landing_chip],
        send_sem=send_ref.at[j], recv_sem=recv_ref.at[j],
        device_id=(chip[0], chip[1], c), device_id_type=MESH)


def _reduce_start(sums, lands, name):
    n = len(sums)

    def body(*refs):
        s_in, l_in = refs[:n], refs[n:2 * n]
        send, recv = refs[2 * n:3 * n], refs[3 * n:4 * n]
        token = refs[6 * n]
        x, y, c, chips = _place()
        q = 2 * x + y
        for a in range(n):
            for j, chip in enumerate(chips):
                _reduce_copy(s_in[a], l_in[a], send[a], recv[a], j, chip, q, c, q).start()
        token[...] = jnp.zeros_like(token)

    sems = [pltpu.SemaphoreType.DMA((N_PEER,))] * (2 * n)
    outs = pl.pallas_call(
        body, name=name,
        in_specs=[HBM] * (2 * n),
        out_specs=[SEM] * (2 * n) + [HBM] * (2 * n) + [pl.BlockSpec(memory_space=pltpu.VMEM)],
        out_shape=sems + [pltpu.HBM(b.shape, b.dtype) for b in list(sums) + list(lands)]
        + [jax.ShapeDtypeStruct((8, 128), f32)],
        input_output_aliases={a: 2 * n + a for a in range(2 * n)},
        compiler_params=pltpu.CompilerParams(has_side_effects=EFFECT),
    )(*[_in_hbm(b) for b in list(sums) + list(lands)])
    return (list(outs[:n]), list(outs[n:2 * n]), list(outs[2 * n:3 * n]), list(outs[3 * n:4 * n]),
            outs[4 * n])


def _reduce_wait(send, recv, sums, lands, after, name):
    n = len(sums)

    def body(*refs):
        s_in, l_in = refs[:n], refs[n:2 * n]
        send_r, recv_r = refs[2 * n:3 * n], refs[3 * n:4 * n]
        x, y, c, chips = _place()
        q = 2 * x + y
        for a in range(n):
            for j, chip in enumerate(chips):
                cp = _reduce_copy(s_in[a], l_in[a], send_r[a], recv_r[a], j, chip, q, c, _chip_id(chip))
                cp.wait_send()
                cp.wait_recv()

    afters = after if isinstance(after, (tuple, list)) else (after,)
    outs = pl.pallas_call(
        body, name=name,
        in_specs=[HBM] * (2 * n) + [SEM] * (2 * n) + [ANY] * len(afters),
        out_specs=[HBM] * (2 * n),
        out_shape=[pltpu.HBM(b.shape, b.dtype) for b in list(sums) + list(lands)],
        input_output_aliases={a: a for a in range(2 * n)},
        compiler_params=pltpu.CompilerParams(has_side_effects=EFFECT),
    )(*sums, *lands, *send, *recv, *afters)
    return list(outs[n:])


def _sibling_copy(part_ref, land_ref, send_ref, recv_ref, x, y, c):
    return pltpu.make_async_remote_copy(
        src_ref=part_ref.at[:, 1 - c], dst_ref=land_ref, send_sem=send_ref.at[0], recv_sem=recv_ref.at[0],
        device_id=(x, y, 1 - c), device_id_type=MESH)


def _pair_exchange_start(parts, name):
    n = len(parts)
    lands = [lax.empty((NCHIP,) + p.shape[2:], p.dtype) for p in parts]

    def body(*refs):
        p_in, l_in = refs[:n], refs[n:2 * n]
        send, recv = refs[2 * n:3 * n], refs[3 * n:4 * n]
        token = refs[6 * n]
        x, y, c, _ = _place()
        for a in range(n):
            _sibling_copy(p_in[a], l_in[a], send[a], recv[a], x, y, c).start()
        token[...] = jnp.zeros_like(token)

    sems = [pltpu.SemaphoreType.DMA((1,))] * (2 * n)
    outs = pl.pallas_call(
        body, name=name,
        in_specs=[HBM] * (2 * n),
        out_specs=[SEM] * (2 * n) + [HBM] * (2 * n) + [pl.BlockSpec(memory_space=pltpu.VMEM)],
        out_shape=sems + [pltpu.HBM(b.shape, b.dtype) for b in list(parts) + lands]
        + [jax.ShapeDtypeStruct((8, 128), f32)],
        input_output_aliases={a: 2 * n + a for a in range(2 * n)},
        compiler_params=pltpu.CompilerParams(has_side_effects=EFFECT),
    )(*[_in_hbm(b) for b in list(parts) + lands])
    return (list(outs[:n]), list(outs[n:2 * n]), list(outs[2 * n:3 * n]), list(outs[3 * n:4 * n]),
            outs[4 * n])


def _pair_exchange_wait(send, recv, parts, lands, after, name):
    n = len(parts)

    def body(*refs):
        p_in, l_in = refs[:n], refs[n:2 * n]
        send_r, recv_r = refs[2 * n:3 * n], refs[3 * n:4 * n]
        x, y, c, _ = _place()
        for a in range(n):
            cp = _sibling_copy(p_in[a], l_in[a], send_r[a], recv_r[a], x, y, c)
            cp.wait_send()
            cp.wait_recv()

    outs = pl.pallas_call(
        body, name=name,
        in_specs=[HBM] * (2 * n) + [SEM] * (2 * n) + [ANY],
        out_specs=[HBM] * (2 * n),
        out_shape=[pltpu.HBM(b.shape, b.dtype) for b in list(parts) + list(lands)],
        input_output_aliases={a: a for a in range(2 * n)},
        compiler_params=pltpu.CompilerParams(has_side_effects=EFFECT),
    )(*parts, *lands, *send, *recv, after)
    return list(outs[:n]), list(outs[n:])


def _forward_copy(buf_ref, send_ref, recv_ref, j, chip, x, y, c, landing):
    return pltpu.make_async_remote_copy(
        src_ref=buf_ref.at[_chip_id(chip), c], dst_ref=buf_ref.at[_chip_id(chip), 1 - c if landing else c],
        send_sem=send_ref.at[j], recv_sem=recv_ref.at[j], device_id=(x, y, 1 - c), device_id_type=MESH)


def _forward_start(bufs, name):
    n = len(bufs)

    def body(*refs):
        ins = refs[:n]
        send, recv = refs[n:2 * n], refs[2 * n:3 * n]
        token = refs[4 * n]
        x, y, c, chips = _place()
        for a in range(n):
            for j, chip in enumerate(chips):
                _forward_copy(ins[a], send[a], recv[a], j, chip, x, y, c, False).start()
        token[...] = jnp.zeros_like(token)

    sems = [pltpu.SemaphoreType.DMA((N_PEER,))] * (2 * n)
    outs = pl.pallas_call(
        body, name=name,
        in_specs=[HBM] * n,
        out_specs=[SEM] * (2 * n) + [HBM] * n + [pl.BlockSpec(memory_space=pltpu.VMEM)],
        out_shape=sems + [pltpu.HBM(b.shape, b.dtype) for b in bufs] + [jax.ShapeDtypeStruct((8, 128), f32)],
        input_output_aliases={a: 2 * n + a for a in range(n)},
        compiler_params=pltpu.CompilerParams(has_side_effects=EFFECT),
    )(*[_in_hbm(b) for b in bufs])
    return list(outs[:n]), list(outs[n:2 * n]), list(outs[2 * n:3 * n]), outs[3 * n]


def _forward_wait(send, recv, bufs, after, name):
    n = len(bufs)

    def body(*refs):
        ins = refs[:n]
        send_r, recv_r = refs[n:2 * n], refs[2 * n:3 * n]
        x, y, c, chips = _place()
        for a in range(n):
            for j, chip in enumerate(chips):
                cp = _forward_copy(ins[a], send_r[a], recv_r[a], j, chip, x, y, c, True)
                cp.wait_send()
                cp.wait_recv()

    outs = pl.pallas_call(
        body, name=name,
        in_specs=[HBM] * n + [SEM] * (2 * n) + [ANY],
        out_specs=[HBM] * n,
        out_shape=[pltpu.HBM(b.shape, b.dtype) for b in bufs],
        input_output_aliases={a: a for a in range(n)},
        compiler_params=pltpu.CompilerParams(has_side_effects=EFFECT),
    )(*bufs, *send, *recv, after)
    return list(outs)


def _adamw_math(w, g, m, v):
    m = ADAM_B1 * m + (1.0 - ADAM_B1) * g
    v = ADAM_B2 * v + (1.0 - ADAM_B2) * (g * g)
    m_hat = m / (1.0 - ADAM_B1 ** ADAM_STEP)
    v_hat = v / (1.0 - ADAM_B2 ** ADAM_STEP)
    delta = -ADAM_LR * (m_hat / (jnp.sqrt(v_hat) + ADAM_EPS) + ADAM_WD * w)
    return delta, m, v


def _adamw(w, mine, theirs, m, v, qc, name):
    r, cc = w.shape
    hr = r // 2
    tr = next(hr // k for k in range(1, hr + 1)
              if hr % k == 0 and (hr // k) % 8 == 0 and (hr // k) * cc * 4 <= (3 << 19))
    nb = hr // tr

    def body(qc_ref, w_ref, a_ref, b_ref, m_ref, v_ref, g_ref, d_ref, mo_ref, vo_ref):
        g = jnp.where(pl.program_id(0) == qc_ref[1], a_ref[...], b_ref[...])
        g_ref[...] = g
        d_ref[...], mo_ref[...], vo_ref[...] = _adamw_math(w_ref[...], g, m_ref[...], v_ref[...])

    full = pl.BlockSpec((tr, cc), lambda h, i, qc_ref: (h * nb + i, 0))
    half = pl.BlockSpec((tr, cc), lambda h, i, qc_ref: (i, 0))
    return pl.pallas_call(
        body, name=name,
        grid_spec=pltpu.PrefetchScalarGridSpec(
            num_scalar_prefetch=1, grid=(2, nb),
            in_specs=[full, half, half, full, full], out_specs=[full] * 4),
        out_shape=[jax.ShapeDtypeStruct((r, cc), f32)] * 4,
        compiler_params=_cp(2),
    )(qc, w, mine, theirs, m, v)


REPL = [("ffn1_norm", 1), ("mix_norm", 1), ("b_in", 6), ("rnn_conv_b", 1), ("rg_b_a", 1), ("rg_b_x", 1),
        ("rg_lambda", 1), ("conv_dw_b", 1), ("conv_ln_g", 1), ("conv_ln_b", 1), ("conv_b_proj", 1),
        ("ffn2_norm", 1), ("final_norm", 1)]
COLSH = [("meta_tokens", NMETA), ("rnn_conv_w", KC4), ("conv_dw_w", KC31)]
SMALL = REPL + COLSH
CS = D // NCHIP


def _pack_rows():
    starts, row = {}, 0
    for k, rows in REPL:
        starts[k] = row
        row += rows
    for k, rows in COLSH:
        row = -(-row // 8) * 8
        starts[k] = row
        row += rows
    return starts, -(-row // 8) * 8


PACK_START, LOSS_ROW = _pack_rows()
SMALL_ROWS = LOSS_ROW + 8


def _small_pack(g, loss_row):
    pieces, row = [], 0
    for k, rows in SMALL:
        if PACK_START[k] > row:
            pieces.append(jnp.zeros((PACK_START[k] - row, D), f32))
        pieces.append(g[k].reshape(rows, D))
        row = PACK_START[k] + rows
    pieces.append(jnp.zeros((LOSS_ROW - row, D), f32))
    pieces.append(loss_row)
    pieces.append(jnp.zeros((SMALL_ROWS - LOSS_ROW - 1, D), f32))
    return jnp.concatenate(pieces, axis=0)


def _adamw_small(packs, ws, ms, vs):
    ns = len(SMALL)

    def body(*refs):
        pack_ref = refs[0]
        w_refs, m_refs, v_refs = refs[1:1 + ns], refs[1 + ns:1 + 2 * ns], refs[1 + 2 * ns:1 + 3 * ns]
        outs = refs[1 + 3 * ns:1 + 7 * ns]
        g_refs, d_refs, mo_refs, vo_refs = outs[:ns], outs[ns:2 * ns], outs[2 * ns:3 * ns], outs[3 * ns:]
        loss_ref = refs[1 + 7 * ns]
        gsum_sc = refs[2 + 7 * ns]
        q = 2 * lax.axis_index("x") + lax.axis_index("y")
        acc = pack_ref[0]
        for dev in range(1, 8):
            acc = acc + pack_ref[dev]
        gsum_sc[...] = acc
        loss_ref[...] = gsum_sc[LOSS_ROW:LOSS_ROW + 1, :]
        for idx, (name, rows) in enumerate(SMALL):
            row = PACK_START[name]
            if idx < len(REPL):
                for k in range(rows):
                    cols = slice(k * D, (k + 1) * D)
                    g = gsum_sc[row + k:row + k + 1, :]
                    d, mm, vv = _adamw_math(w_refs[idx][:, cols], g, m_refs[idx][:, cols], v_refs[idx][:, cols])
                    g_refs[idx][:, cols] = g
                    d_refs[idx][:, cols] = d
                    mo_refs[idx][:, cols] = mm
                    vo_refs[idx][:, cols] = vv
            else:
                g = gsum_sc[row:row + rows, pl.ds(pl.multiple_of(q * CS, CS), CS)]
                d, mm, vv = _adamw_math(w_refs[idx][...], g, m_refs[idx][...], v_refs[idx][...])
                g_refs[idx][...] = g
                d_refs[idx][...] = d
                mo_refs[idx][...] = mm
                vo_refs[idx][...] = vv

    shapes = [jax.ShapeDtypeStruct(w.shape, f32) for w in ws]
    return pl.pallas_call(
        body, name="adamw_small",
        out_shape=shapes * 4 + [jax.ShapeDtypeStruct((1, D), f32)],
        scratch_shapes=[pltpu.VMEM((SMALL_ROWS, D), f32)],
        compiler_params=pltpu.CompilerParams(vmem_limit_bytes=VMEM_LIMIT),
    )(packs, *ws, *ms, *vs)


BIG = ["ffn1_w_gu", "ffn1_w_down", "w_in", "rg_w_a", "rg_w_x", "rnn_w_proj", "conv_w_proj", "w_out",
       "ffn2_w_gu", "ffn2_w_down"]
WEIGHTS = ['meta_tokens', 'ffn1_norm', 'ffn1_w_gu', 'ffn1_w_down', 'mix_norm', 'w_in', 'b_in', 'rnn_conv_w',
           'rnn_conv_b', 'rg_w_a', 'rg_b_a', 'rg_w_x', 'rg_b_x', 'rg_lambda', 'rnn_w_proj', 'conv_dw_w',
           'conv_dw_b', 'conv_ln_g', 'conv_ln_b', 'conv_w_proj', 'conv_b_proj', 'w_out', 'ffn2_norm',
           'ffn2_w_gu', 'ffn2_w_down', 'final_norm']


def _as2d(a):
    return a.reshape(-1, a.shape[-1])


def _step(x, loss_target, w, m, v):
    seq = x.shape[1]
    n_valid = NMETA + seq
    t = -(-n_valid // TM) * TM

    qc = jnp.stack([2 * lax.axis_index("x") + lax.axis_index("y"), lax.axis_index("c")]).astype(jnp.int32)
    p = {k: w[k].reshape(1, rows * D) for k, rows in REPL}

    first = ["ffn1_w_gu", "ffn1_w_down", "small"]
    later = [["w_in"], ["rg_w_a", "rg_w_x", "rnn_w_proj", "conv_w_proj", "w_out"], ["ffn2_w_gu", "ffn2_w_down"]]
    small_rows = sum(r for _, r in COLSH)
    small = jnp.concatenate([_as2d(w[k]) for k, _ in COLSH] + [jnp.zeros((64 - small_rows, CS), f32)], axis=0)

    def cast(k, token=None):
        src, dtype = (small, f32) if k == "small" else (_as2d(w[k]), bf16)
        return _cast_into_slot(src, qc, dtype, "cast_" + k, after=token)

    send1, recv1, bufs1, token1 = _gather_start([cast(k) for k in first], "gather_start_first")
    rest = [k for grp in later for k in grp]
    send2, recv2, bufs2, token2 = _gather_start([cast(k, token1) for k in rest], "gather_start_rest")

    def install(names, done):
        for k, b in zip(names, done):
            full = b.reshape(NCHIP, 2 * b.shape[2], b.shape[3])
            if k in ("ffn1_w_down", "ffn2_w_down"):
                full = full.reshape(F, D)
            elif k in ("rnn_w_proj", "conv_w_proj", "w_out"):
                full = full.reshape(D, D)
            elif k in ("rg_w_a", "rg_w_x"):
                full = full.reshape(NCHIP, NHEAD, HD // NCHIP, HD).transpose(1, 0, 2, 3).reshape(NHEAD, HD, HD)
            p[k] = full

    def finish(names, send, recv, bufs, after, tag):
        install(names, _forward_halves(_gather_wait(send, recv, bufs, after, "gather_wait_" + tag),
                                       "gather_forward_" + tag))

    def group(names):
        idx = [rest.index(k) for k in names]
        return names, [send2[i] for i in idx], [recv2[i] for i in idx], [bufs2[i] for i in idx]

    h0 = jnp.pad(x[0], ((NMETA, t - n_valid), (0, 0)))
    tgt = jnp.pad(loss_target[0], ((NMETA, t - n_valid), (0, 0)))
    finish(first, send1, recv1, bufs1, (token2, h0, tgt), "first")
    small_full = p.pop("small").transpose(1, 0, 2).reshape(64, D)
    row = 0
    for k, rows in COLSH:
        p[k] = small_full[row:row + rows]
        row += rows

    h0 = lax.dynamic_update_slice(h0, p["meta_tokens"], (0, 0))
    h1, gate1, up1, n1 = _ffn_fwd(h0, p["ffn1_norm"], p["ffn1_w_gu"], p["ffn1_w_down"], "ffn1_fwd")
    finish(*group(later[0]), h1, "in")
    proj, n2 = _inproj_fwd(h1, p["mix_norm"], p["w_in"], p["b_in"])
    names_l = later[1] + later[2]
    _, send_l, recv_l, bufs_l = group(names_l)
    send_f, recv_f, bufs_f, token = _forward_start(
        _gather_wait(send_l, recv_l, bufs_l, proj, "gather_wait_late"), "gather_forward_start")
    vc, s = _conv_fwd(proj, p["conv_dw_w"], p["conv_dw_b"], p["conv_ln_g"], p["conv_ln_b"], after=token)
    install(names_l, _forward_wait(send_f, recv_f, bufs_f, vc, "gather_forward_wait"))
    xr, hr, z, gates = _rnn_fwd(proj, p["rnn_conv_w"], p["rnn_conv_b"], p["rg_w_a"], p["rg_b_a"],
                         p["rg_w_x"], p["rg_b_x"], p["rg_lambda"])
    h2 = _merge_fwd(h1, z, s, proj, p["rnn_w_proj"], p["conv_w_proj"], p["conv_b_proj"], p["w_out"])
    h3, gate2, up2, n3 = _ffn_fwd(h2, p["ffn2_norm"], p["ffn2_w_gu"], p["ffn2_w_down"], "ffn2_fwd")
    dh3, loss_blk, d_final = _final_loss(h3, p["final_norm"], tgt, n_valid)

    g = {"final_norm": d_final}
    pending = []

    def exchange_start(names, tag):
        parts = []
        for k in names:
            rows = g[k].size // (NCHIP * g[k].shape[-1])
            parts.append(g[k].reshape((NCHIP, 2, rows // 2, g[k].shape[-1])))
        send, recv, parts, lands, token = _pair_exchange_start(parts, "pair_exchange_start_" + tag)
        return (names, tag, send, recv, parts, lands), token

    def reduce_start(state, after):
        names, tag, send, recv, parts, lands = state
        parts, from_sibling = _pair_exchange_wait(send, recv, parts, lands, after, "pair_exchange_wait_" + tag)
        sums, lands = _pair_add(parts, from_sibling, qc, "pair_add_" + tag)
        send, recv, sums, lands, token = _reduce_start(sums, lands, "reduce_start_" + tag)
        pending.append((names, tag, send, recv, sums, lands))
        return token

    dh2, dgate2, dup2, a2, df2, g["ffn2_norm"] = _ffn_bwd(
        dh3, h2, p["ffn2_norm"], gate2, up2, p["ffn2_w_gu"], p["ffn2_w_down"], "ffn2_bwd")
    g["ffn2_w_gu"] = _ffn_gu_grad(n3, dgate2, dup2, "ffn2")
    g["ffn2_w_down"] = _ffn_down_grad(a2, df2, "ffn2")
    state, token = exchange_start(["ffn2_w_gu", "ffn2_w_down"], "ffn2")

    dz, ds, dproj, dh2b, merged, dya, dyb, g["conv_b_proj"] = _merge_bwd(
        dh2, z, s, proj, p["rnn_w_proj"], p["conv_w_proj"], p["conv_b_proj"], p["w_out"], after=token)
    token = reduce_start(state, dz)
    dproj, g["conv_dw_w"], g["conv_dw_b"], g["conv_ln_g"], g["conv_ln_b"] = _conv_bwd(
        ds, vc, proj, dproj, p["conv_dw_w"], p["conv_ln_g"], p["conv_ln_b"], after=token)
    g["w_out"] = _square_grad(merged, dh2b, "dw_out")
    g["rnn_w_proj"] = _square_grad(z, dya, "dw_rnn_proj")
    g["conv_w_proj"] = _square_grad(s, dyb, "dw_conv_proj")
    (dproj, g["rg_w_a"], g["rg_w_x"], g["rnn_conv_w"], g["rnn_conv_b"], g["rg_b_a"], g["rg_b_x"],
     g["rg_lambda"]) = _rnn_bwd(dz, xr, hr, gates, proj, dproj, p["rnn_conv_w"], p["rg_w_a"],
                                p["rg_w_x"], p["rg_lambda"])
    state, token = exchange_start(["w_out", "rnn_w_proj", "conv_w_proj", "rg_w_a", "rg_w_x"], "mix")

    dh1, g["mix_norm"], db_in = _inproj_bwd(dproj, dh2, h1, p["mix_norm"], p["w_in"], after=token)
    g["b_in"] = db_in.reshape(1, NIN)
    token = reduce_start(state, dh1)
    g["w_in"] = _tn_matmul(n2, dproj, D, NIN // NCHIP, (NCHIP, D, NIN // NCHIP),
                           (None, D, NIN // NCHIP), lambda k, nn, mm: (nn, 0, 0), "dw_in", after=token)
    state, token = exchange_start(["w_in"], "in")

    dh0, dgate1, dup1, a1, df1, g["ffn1_norm"] = _ffn_bwd(
        dh1, h0, p["ffn1_norm"], gate1, up1, p["ffn1_w_gu"], p["ffn1_w_down"], "ffn1_bwd", after=token)
    g["meta_tokens"] = dh0[0:NMETA]
    grad_x = dh0[NMETA:n_valid][None]
    token = reduce_start(state, dh0)

    send_s, recv_s, pack_buf, token_s = _gather_all_start(
        _place_pack(_small_pack(g, loss_blk.reshape(1, D)), qc), "gather_all_start")
    g["ffn1_w_down"] = _ffn_down_grad(a1, df1, "ffn1", after=(token, token_s))
    state, token = exchange_start(["ffn1_w_down"], "ffn1_down")
    gate_half = _tn_matmul(n1, dgate1, D, FS, (NCHIP, D, FS), (None, D, FS), lambda k, nn, mm: (nn, 0, 0),
                           "ffn1_dwg", after=token)
    token = reduce_start(state, gate_half)
    g["ffn1_w_gu"] = _tn_matmul(n1, dup1, D, FS, (NCHIP, D, FS), (None, D, FS), lambda k, nn, mm: (2 + nn, 0, 0),
                                "ffn1_dwu", base=gate_half, after=token)
    state_gu, token = exchange_start(["ffn1_w_gu"], "ffn1_gu")
    packs = _gather_all_wait(send_s, recv_s, pack_buf, token, "gather_all_wait")

    grads, deltas, new_m, new_v = {}, {}, {}, {}

    def landed_sums(items, after):
        names, mine = [], []
        for grp_names, grp_tag, send, recv, sums, lands in items:
            landed = _reduce_wait(send, recv, sums, lands, after, "reduce_wait_" + grp_tag)
            mine += _sum_chips(landed, "sum_chips_" + grp_tag)
            names += grp_names
            after = mine[-1]
        return names, mine

    def share_and_update(names, mine, tag, after=None):
        theirs = _pair_share(mine, "pair_share_" + tag, after=after)
        for k, mi, th in zip(names, mine, theirs):
            outs = _adamw(_as2d(w[k]), mi, th, _as2d(m[k]), _as2d(v[k]), qc, "adamw_" + k)
            grads[k], deltas[k], new_m[k], new_v[k] = (a.reshape(w[k].shape) for a in outs)
        return [new_v[k] for k in names]

    early_names, early_mine = landed_sums(pending[:3], packs)
    token = reduce_start(state_gu, early_mine[-1])
    after = share_and_update(early_names, early_mine, "early", after=token)
    share_and_update(*landed_sums(pending[3:], after), "late")
    names = [k for k, _ in SMALL]
    shape2 = {k: ((1, rows * D) if (k, rows) in REPL else (rows, CS)) for k, rows in SMALL}
    outs = _adamw_small(packs, *[[a[k].reshape(shape2[k]) for k in names] for a in (w, m, v)])
    ns = len(names)
    for i, k in enumerate(names):
        grads[k], deltas[k], new_m[k], new_v[k] = (outs[j * ns + i].reshape(w[k].shape) for j in range(4))

    loss = outs[4 * ns][0, 0]
    return (loss, grad_x, *[grads[k] for k in WEIGHTS], *[deltas[k] for k in WEIGHTS],
            *[new_m[k] for k in WEIGHTS], *[new_v[k] for k in WEIGHTS])


def kernel(x, meta_tokens, ffn1_norm, ffn1_w_gu, ffn1_w_down, mix_norm, w_in, b_in, rnn_conv_w, rnn_conv_b, rg_w_a, rg_b_a, rg_w_x, rg_b_x, rg_lambda, rnn_w_proj, conv_dw_w, conv_dw_b, conv_ln_g, conv_ln_b, conv_w_proj, conv_b_proj, w_out, ffn2_norm, ffn2_w_gu, ffn2_w_down, final_norm, loss_target, m_meta_tokens, m_ffn1_norm, m_ffn1_w_gu, m_ffn1_w_down, m_mix_norm, m_w_in, m_b_in, m_rnn_conv_w, m_rnn_conv_b, m_rg_w_a, m_rg_b_a, m_rg_w_x, m_rg_b_x, m_rg_lambda, m_rnn_w_proj, m_conv_dw_w, m_conv_dw_b, m_conv_ln_g, m_conv_ln_b, m_conv_w_proj, m_conv_b_proj, m_w_out, m_ffn2_norm, m_ffn2_w_gu, m_ffn2_w_down, m_final_norm, v_meta_tokens, v_ffn1_norm, v_ffn1_w_gu, v_ffn1_w_down, v_mix_norm, v_w_in, v_b_in, v_rnn_conv_w, v_rnn_conv_b, v_rg_w_a, v_rg_b_a, v_rg_w_x, v_rg_b_x, v_rg_lambda, v_rnn_w_proj, v_conv_dw_w, v_conv_dw_b, v_conv_ln_g, v_conv_ln_b, v_conv_w_proj, v_conv_b_proj, v_w_out, v_ffn2_norm, v_ffn2_w_gu, v_ffn2_w_down, v_final_norm):
    args = locals()
    w = {k: args[k] for k in WEIGHTS}
    m = {k: args["m_" + k] for k in WEIGHTS}
    v = {k: args["v_" + k] for k in WEIGHTS}
    return _step(x, loss_target, w, m, v)
```

```python
import functools

import jax
import jax.numpy as jnp
from jax import lax
from jax.experimental import pallas as pl
from jax.experimental.pallas import tpu as pltpu

f32 = jnp.float32
bf16 = jnp.bfloat16

D = 1024
F = 2816
FS = F // 2
NIN = 6 * D
NMETA = 16
NHEAD = 4
HD = D // NHEAD
KC4 = 4
KC31 = 31
HALO = 32
EPS = 1e-6
TM = 416
NCHIP = 4
MESH = pl.DeviceIdType.MESH

ADAM_LR = 0.001
ADAM_B1 = 0.9
ADAM_B2 = 0.999
ADAM_EPS = 1e-08
ADAM_WD = 0.01
ADAM_STEP = 10

VMEM_LIMIT = 56 * 1024 * 1024
FSUB = [(o, min(256, FS - o)) for o in range(0, FS, 256)]


def _cp(n_axes, **kw):
    return pltpu.CompilerParams(dimension_semantics=("arbitrary",) * n_axes,
                                vmem_limit_bytes=VMEM_LIMIT, **kw)


RESIDENT = pl.BlockSpec(memory_space=pltpu.VMEM)


def _n_after(after):
    return 0 if after is None else (len(after) if isinstance(after, (tuple, list)) else 1)


def _ordered(body, in_specs, args, after):
    if after is None:
        return body, in_specs, args
    extra = tuple(after) if isinstance(after, (tuple, list)) else (after,)
    return (lambda *refs: body(*refs[len(extra):]),
            [pl.BlockSpec(memory_space=pl.ANY)] * len(extra) + list(in_specs), extra + tuple(args))


def _nt_dot(a, b):
    return lax.dot_general(a, b, (((1,), (1,)), ((), ())), preferred_element_type=f32)


def _tn_dot(a, b):
    return lax.dot_general(a, b, (((0,), (0,)), ((), ())), preferred_element_type=f32)


def _sigmoid(x):
    return 0.5 * jnp.tanh(0.5 * x) + 0.5


def _log1p(y):
    u = 1.0 + y
    d = u - 1.0
    return jnp.where(d == 0.0, y, jnp.log(u) * (y / jnp.where(d == 0.0, 1.0, d)))


def _softplus(x):
    return jnp.maximum(x, 0.0) + _log1p(jnp.exp(-jnp.abs(x)))


def _one_minus_square(a, log_a):
    x = 2.0 * log_a
    series = x * (1.0 + x * (0.5 + x * (1.0 / 6.0)))
    return jnp.where(jnp.abs(x) < 0.03, -series, 1.0 - a * a)


_GELU_C = 0.7978845608028654
_GELU_K = 0.044715


def _gelu_and_grad(y):
    y2 = y * y
    th = jnp.tanh(_GELU_C * (y + _GELU_K * y * y2))
    gel = 0.5 * y * (1.0 + th)
    dgel = 0.5 * (1.0 + th) + 0.5 * y * (1.0 - th * th) * _GELU_C * (1.0 + 3.0 * _GELU_K * y2)
    return gel, dgel


def _rms_stats(h):
    return lax.rsqrt(jnp.mean(h * h, axis=-1, keepdims=True) + EPS)


def _rms_bwd(dn, h, g):
    r = _rms_stats(h)
    nhat = h * r
    dnh = dn * g
    dh = r * (dnh - nhat * jnp.mean(dnh * nhat, axis=-1, keepdims=True))
    dg = jnp.sum(dn * nhat, axis=0, keepdims=True)
    return dh, dg


def _row_ids(shape):
    return lax.broadcasted_iota(jnp.int32, shape, 0)


def _ffn_fwd(h, g, wgu, wd, name, loss_head=None):
    t = h.shape[0]
    nj = 2
    tm = TM
    n_head = 0 if loss_head is None else 2

    def body(*refs):
        h_ref, g_ref, wg_ref, wd_ref = refs[:4]
        outs = refs[4 + n_head:]
        gate_ref, up_ref, n_ref, nb_sc, acc_sc, a_sc = outs[-6:]
        i = pl.program_id(0)
        j = pl.program_id(1)

        @pl.when(j == 0)
        def _():
            hh = h_ref[...]
            nb = (hh * _rms_stats(hh) * g_ref[...]).astype(bf16)
            nb_sc[...] = nb
            n_ref[...] = nb
            acc_sc[...] = jnp.zeros_like(acc_sc)

        nb = nb_sc[...]
        for off, width in FSUB:
            cols = slice(off, off + width)
            gt = jnp.dot(nb, wg_ref[j, :, cols], preferred_element_type=f32)
            up = jnp.dot(nb, wg_ref[2 + j, :, cols], preferred_element_type=f32)
            gate_ref[:, cols] = gt.astype(bf16)
            up_ref[:, cols] = up.astype(bf16)
            a_sc[:, cols] = (gt * _sigmoid(gt) * up).astype(bf16)
        acc_sc[...] += jnp.dot(a_sc[...], wd_ref[j], preferred_element_type=f32)

        if loss_head is None:
            @pl.when(j == nj - 1)
            def _():
                outs[0][...] = h_ref[...] + 0.5 * acc_sc[...]
        else:
            gf_ref, t_ref = refs[4:6]
            dh_ref, loss_ref, dgf_ref = outs[:3]

            @pl.when(jnp.logical_and(i == 0, j == 0))
            def _():
                loss_ref[...] = jnp.zeros_like(loss_ref)
                dgf_ref[...] = jnp.zeros_like(dgf_ref)

            @pl.when(j == nj - 1)
            def _():
                hh = h_ref[...] + 0.5 * acc_sc[...]
                gf = gf_ref[...]
                row = i * tm + _row_ids((tm, 1))
                valid = jnp.logical_and(row >= NMETA, row < loss_head[2])
                err = jnp.where(valid, hh * _rms_stats(hh) * gf - t_ref[...], 0.0)
                loss_ref[...] += 0.5 * jnp.sum(err * err) * (1.0 / D)
                dh, dgf = _rms_bwd(err * (1.0 / D), hh, gf)
                dh_ref[...] = dh
                dgf_ref[...] += dgf

    rowd = pl.BlockSpec((tm, D), lambda i, j: (i, 0))
    vec = pl.BlockSpec((1, D), lambda i, j: (0, 0))
    rowf = pl.BlockSpec((tm, FS), lambda i, j: (i, j))
    in_specs, args = [rowd, vec, RESIDENT, RESIDENT], [h, g, wgu, wd.reshape(nj, FS, D)]
    out_specs, out_shape = [rowd], [jax.ShapeDtypeStruct((t, D), f32)]
    if loss_head is not None:
        in_specs, args = in_specs + [vec, rowd], args + [loss_head[0], loss_head[1]]
        out_specs += [pl.BlockSpec((8, 128), lambda i, j: (0, 0)), vec]
        out_shape += [jax.ShapeDtypeStruct((8, 128), f32), jax.ShapeDtypeStruct((1, D), f32)]
    return pl.pallas_call(
        body, name=name, grid=(t // tm, nj),
        in_specs=in_specs,
        out_specs=out_specs + [rowf, rowf, rowd],
        out_shape=out_shape + [jax.ShapeDtypeStruct((t, F), bf16), jax.ShapeDtypeStruct((t, F), bf16),
                               jax.ShapeDtypeStruct((t, D), bf16)],
        scratch_shapes=[pltpu.VMEM((tm, D), bf16), pltpu.VMEM((tm, D), f32), pltpu.VMEM((tm, FS), bf16)],
        compiler_params=_cp(2),
    )(*args)


def _inproj_fwd(h, g, win, b_in):
    t = h.shape[0]
    tn = NIN // NCHIP
    nj = NIN // tn
    per = (NIN // NCHIP) // tn

    def body(h_ref, g_ref, w_ref, b_ref, proj_ref, n_ref, nb_sc):
        j = pl.program_id(1)

        @pl.when(j == 0)
        def _():
            hh = h_ref[...]
            nb = (hh * _rms_stats(hh) * g_ref[...]).astype(bf16)
            nb_sc[...] = nb
            n_ref[...] = nb

        proj_ref[...] = jnp.dot(nb_sc[...], w_ref[j], preferred_element_type=f32) + b_ref[...]

    return pl.pallas_call(
        body, name="inproj_fwd", grid=(t // TM, nj),
        in_specs=[
            pl.BlockSpec((TM, D), lambda i, j: (i, 0)),
            pl.BlockSpec((1, D), lambda i, j: (0, 0)),
            RESIDENT,
            pl.BlockSpec((1, tn), lambda i, j: (0, j)),
        ],
        out_specs=[
            pl.BlockSpec((TM, tn), lambda i, j: (i, j)),
            pl.BlockSpec((TM, D), lambda i, j: (i, 0)),
        ],
        out_shape=[jax.ShapeDtypeStruct((t, NIN), f32), jax.ShapeDtypeStruct((t, D), bf16)],
        scratch_shapes=[pltpu.VMEM((TM, D), bf16)],
        compiler_params=_cp(2),
    )(h, g, win, b_in)


def _block_gates(xr, wa_ref, ba, wx_ref, bx, lam):
    xrb = xr.astype(bf16)
    pa = jnp.concatenate([jnp.dot(xrb[:, hh * HD:(hh + 1) * HD], wa_ref[hh], preferred_element_type=f32)
                          for hh in range(NHEAD)], axis=1)
    px = jnp.concatenate([jnp.dot(xrb[:, hh * HD:(hh + 1) * HD], wx_ref[hh], preferred_element_type=f32)
                          for hh in range(NHEAD)], axis=1)
    ra = _sigmoid(pa + ba)
    ii = _sigmoid(px + bx)
    sp = _softplus(-lam)
    log_a = -8.0 * ra * sp
    a = jnp.exp(log_a)
    sq = jnp.sqrt(_one_minus_square(a, log_a))
    return ra, ii, a, sq, sp


def _rnn_fwd(proj, cw, cb, wa, ba, wx, bx, lam):
    t = proj.shape[0]
    ng = TM // 8

    def body(x_ref, y_ref, cw_ref, cb_ref, wa_ref, ba_ref, wx_ref, bx_ref, lam_ref,
             xr_ref, hr_ref, z_ref, gates_ref, xext_sc, carry_sc, a_sc, h_sc):
        i = pl.program_id(0)

        @pl.when(i == 0)
        def _():
            xext_sc[0:8, :] = jnp.zeros((8, D), f32)
            carry_sc[...] = jnp.zeros_like(carry_sc)

        x = x_ref[...]
        xext_sc[8:8 + TM, :] = x
        xe = xext_sc[...]
        xr = cb_ref[...] + cw_ref[KC4 - 1:KC4, :] * x
        for k in range(KC4 - 1):
            xr = xr + cw_ref[k:k + 1, :] * pltpu.roll(xe, KC4 - 1 - k, 0)[8:8 + TM]
        xext_sc[0:8, :] = x[TM - 8:TM]

        ra, ii, a, sq, _ = _block_gates(xr, wa_ref, ba_ref[...], wx_ref, bx_ref[...], lam_ref[...])
        for slot, val in enumerate((ra, ii, a, sq)):
            gates_ref[slot] = val
        a_sc[...] = a
        h_sc[...] = sq * ii * xr
        row = _row_ids((8, D))

        def group(r, carry):
            off = pl.multiple_of(r * 8, 8)
            aa = a_sc[pl.ds(off, 8), :]
            hh = h_sc[pl.ds(off, 8), :]
            for s in (1, 2, 4):
                a_sh = jnp.where(row >= s, pltpu.roll(aa, s, 0), 1.0)
                h_sh = jnp.where(row >= s, pltpu.roll(hh, s, 0), 0.0)
                hh = aa * h_sh + hh
                aa = aa * a_sh
            hh = hh + aa * carry
            h_sc[pl.ds(off, 8), :] = hh
            return hh[7:8, :]

        carry_sc[...] = lax.fori_loop(0, ng, group, carry_sc[...])
        hr = h_sc[...]
        gel, _ = _gelu_and_grad(y_ref[...])
        xr_ref[...] = xr
        hr_ref[...] = hr
        z_ref[...] = (hr * gel).astype(bf16)

    vec = pl.BlockSpec((1, D), lambda i: (0, 0))
    return pl.pallas_call(
        body, name="rnn_fwd", grid=(t // TM,),
        in_specs=[
            pl.BlockSpec((TM, D), lambda i: (i, 0)),
            pl.BlockSpec((TM, D), lambda i: (i, 1)),
            pl.BlockSpec((KC4, D), lambda i: (0, 0)),
            vec,
            pl.BlockSpec((NHEAD, HD, HD), lambda i: (0, 0, 0)),
            vec,
            pl.BlockSpec((NHEAD, HD, HD), lambda i: (0, 0, 0)),
            vec, vec,
        ],
        out_specs=[pl.BlockSpec((TM, D), lambda i: (i, 0))] * 3 + [pl.BlockSpec((4, TM, D), lambda i: (0, i, 0))],
        out_shape=[jax.ShapeDtypeStruct((t, D), f32), jax.ShapeDtypeStruct((t, D), f32),
                   jax.ShapeDtypeStruct((t, D), bf16), jax.ShapeDtypeStruct((4, t, D), f32)],
        scratch_shapes=[pltpu.VMEM((TM + 8, D), f32), pltpu.VMEM((1, D), f32),
                        pltpu.VMEM((TM, D), f32), pltpu.VMEM((TM, D), f32)],
        compiler_params=_cp(1),
    )(proj, proj, cw, cb, wa, ba, wx, bx, lam)


def _ln_stats(vc):
    mu = jnp.mean(vc, axis=-1, keepdims=True)
    xc = vc - mu
    rstd = lax.rsqrt(jnp.mean(xc * xc, axis=-1, keepdims=True) + EPS)
    return xc * rstd, rstd


def _conv_fwd(proj, w31, b31, ln_g, ln_b, after=None):
    t = proj.shape[0]

    def body(gv_ref, gg_ref, w_ref, b_ref, lg_ref, lb_ref, vc_ref, s_ref, vext_sc):
        i = pl.program_id(0)

        @pl.when(i == 0)
        def _():
            vext_sc[0:HALO, :] = jnp.zeros((HALO, D), f32)

        v = gv_ref[...] * _sigmoid(gg_ref[...])
        vext_sc[HALO:HALO + TM, :] = v
        ve = vext_sc[...]
        acc = jnp.zeros((TM, D), f32) + b_ref[...]
        for s in range(8):
            vs = ve if s == 0 else pltpu.roll(ve, s, 0)
            for m in range(HALO // 8):
                k = KC31 - 1 - (8 * m + s)
                if 0 <= k < KC31:
                    acc = acc + w_ref[k:k + 1, :] * vs[HALO - 8 * m:HALO - 8 * m + TM]
        vext_sc[0:HALO, :] = v[TM - HALO:TM]
        xhat, _ = _ln_stats(acc)
        ln = xhat * lg_ref[...] + lb_ref[...]
        vc_ref[...] = acc
        s_ref[...] = (ln * _sigmoid(ln)).astype(bf16)

    vec = pl.BlockSpec((1, D), lambda i: (0, 0))
    body, in_specs, args = _ordered(
        body,
        [pl.BlockSpec((TM, D), lambda i: (i, 2)),
         pl.BlockSpec((TM, D), lambda i: (i, 3)),
         pl.BlockSpec((KC31, D), lambda i: (0, 0)),
         vec, vec, vec],
        (proj, proj, w31, b31, ln_g, ln_b), after)
    return pl.pallas_call(
        body, name="conv_fwd", grid=(t // TM,),
        in_specs=in_specs,
        out_specs=[pl.BlockSpec((TM, D), lambda i: (i, 0))] * 2,
        out_shape=[jax.ShapeDtypeStruct((t, D), f32), jax.ShapeDtypeStruct((t, D), bf16)],
        scratch_shapes=[pltpu.VMEM((TM + HALO, D), f32)],
        compiler_params=_cp(1),
    )(*args)


def _merge_fwd(h, z, s, proj, wrp, wcp, bcp, wout):
    t = h.shape[0]

    def body(h_ref, z_ref, s_ref, ga_ref, gb_ref, wrp_ref, wcp_ref, bcp_ref, wout_ref, ho_ref):
        ya = jnp.dot(z_ref[...], wrp_ref[...], preferred_element_type=f32)
        yb = jnp.dot(s_ref[...], wcp_ref[...], preferred_element_type=f32) + bcp_ref[...]
        merged = _sigmoid(ga_ref[...]) * ya + _sigmoid(gb_ref[...]) * yb
        ho_ref[...] = h_ref[...] + jnp.dot(merged.astype(bf16), wout_ref[...], preferred_element_type=f32)

    row = pl.BlockSpec((TM, D), lambda i: (i, 0))
    wsq = pl.BlockSpec((D, D), lambda i: (0, 0))
    return pl.pallas_call(
        body, name="merge_fwd", grid=(t // TM,),
        in_specs=[row, row, row,
                  pl.BlockSpec((TM, D), lambda i: (i, 4)),
                  pl.BlockSpec((TM, D), lambda i: (i, 5)),
                  wsq, wsq, pl.BlockSpec((1, D), lambda i: (0, 0)), wsq],
        out_specs=row,
        out_shape=jax.ShapeDtypeStruct((t, D), f32),
        compiler_params=_cp(1),
    )(h, z, s, proj, proj, wrp, wcp, bcp, wout)


def _ffn_bwd(dh, h, g, gate, up, wgu, wd, name, after=None):
    t = h.shape[0]
    nj = 2

    def body(dh_ref, h_ref, g_ref, gate_ref, up_ref, wg_ref, wd_ref,
             dhi_ref, dgate_ref, dup_ref, a_ref, df_ref, dg_ref, dfb_sc, dn_sc):
        i = pl.program_id(0)
        j = pl.program_id(1)

        @pl.when(jnp.logical_and(i == 0, j == 0))
        def _():
            dg_ref[...] = jnp.zeros_like(dg_ref)

        @pl.when(j == 0)
        def _():
            dfb = (0.5 * dh_ref[...]).astype(bf16)
            dfb_sc[...] = dfb
            df_ref[...] = dfb
            dn_sc[...] = jnp.zeros_like(dn_sc)

        dfb = dfb_sc[...]
        for off, width in FSUB:
            cols = slice(off, off + width)
            da = _nt_dot(dfb, wd_ref[j, cols, :])
            gt = gate_ref[:, cols].astype(f32)
            uu = up_ref[:, cols].astype(f32)
            sg = _sigmoid(gt)
            silu = gt * sg
            a_ref[:, cols] = (silu * uu).astype(bf16)
            dgate_ref[:, cols] = (da * uu * (sg * (1.0 + gt * (1.0 - sg)))).astype(bf16)
            dup_ref[:, cols] = (da * silu).astype(bf16)
        dn_sc[...] += _nt_dot(dgate_ref[...], wg_ref[j]) + _nt_dot(dup_ref[...], wg_ref[2 + j])

        @pl.when(j == nj - 1)
        def _():
            dhin, dg = _rms_bwd(dn_sc[...], h_ref[...], g_ref[...])
            dhi_ref[...] = dh_ref[...] + dhin
            dg_ref[...] += dg

    rowd = pl.BlockSpec((TM, D), lambda i, j: (i, 0))
    rowf = pl.BlockSpec((TM, FS), lambda i, j: (i, j))
    vec = pl.BlockSpec((1, D), lambda i, j: (0, 0))
    body, in_specs, args = _ordered(
        body,
        [rowd, rowd, vec, rowf, rowf,
         RESIDENT, RESIDENT],
        (dh, h, g, gate, up, wgu, wd.reshape(nj, FS, D)), after)
    return pl.pallas_call(
        body, name=name, grid=(t // TM, nj),
        in_specs=in_specs,
        out_specs=[rowd, rowf, rowf, rowf, rowd, vec],
        out_shape=[jax.ShapeDtypeStruct((t, D), f32), jax.ShapeDtypeStruct((t, F), bf16),
                   jax.ShapeDtypeStruct((t, F), bf16), jax.ShapeDtypeStruct((t, F), bf16),
                   jax.ShapeDtypeStruct((t, D), bf16), jax.ShapeDtypeStruct((1, D), f32)],
        scratch_shapes=[pltpu.VMEM((TM, D), bf16), pltpu.VMEM((TM, D), f32)],
        compiler_params=_cp(2),
    )(*args)


def _big_tile(t):
    return max(k * TM for k in range(1, 6) if t % (k * TM) == 0)


ANY_SPEC = pl.BlockSpec(memory_space=pl.ANY)


def _tn_matmul(a, b, tk, tn, out_shape, out_block, out_map, name, base=None, after=None):
    t, kk = a.shape
    _, nn = b.shape
    tmm = _big_tile(t)
    nm = t // tmm

    def body(a_ref, b_ref, o_ref, acc_sc):
        m = pl.program_id(2)

        @pl.when(m == 0)
        def _():
            acc_sc[...] = jnp.zeros_like(acc_sc)

        acc_sc[...] += _tn_dot(a_ref[...], b_ref[...])

        @pl.when(m == nm - 1)
        def _():
            o_ref[...] = acc_sc[...].astype(o_ref.dtype)

    in_specs = [pl.BlockSpec((tmm, tk), lambda k, n, m: (m, k)),
                pl.BlockSpec((tmm, tn), lambda k, n, m: (m, n))]
    args, aliases = (a, b), {}
    if base is not None:
        body = (lambda inner: lambda a_ref, b_ref, base_ref, o_ref, acc_sc: inner(a_ref, b_ref, o_ref, acc_sc))(body)
        in_specs, args, aliases = in_specs + [ANY_SPEC], (a, b, base), {2: 0}
    if after is not None:
        body, in_specs, args = _ordered(body, in_specs, args, after)
        aliases = {k + _n_after(after): v for k, v in aliases.items()}
    return pl.pallas_call(
        body, name=name, grid=(kk // tk, nn // tn, nm),
        in_specs=in_specs,
        out_specs=pl.BlockSpec(out_block, out_map),
        out_shape=jax.ShapeDtypeStruct(out_shape, bf16),
        scratch_shapes=[pltpu.VMEM((tk, tn), f32)],
        input_output_aliases=aliases,
        compiler_params=_cp(3),
    )(*args)


def _merge_bwd(dh, z, s, proj, wrp, wcp, bcp, wout, after=None):
    t = dh.shape[0]

    def body(dh_ref, z_ref, s_ref, ga_ref, gb_ref, wrp_ref, wcp_ref, bcp_ref, wout_ref,
             dz_ref, ds_ref, dgab_ref, dhb_ref, mg_ref, dya_ref, dyb_ref, dbcp_ref):
        i = pl.program_id(0)

        @pl.when(i == 0)
        def _():
            dbcp_ref[...] = jnp.zeros_like(dbcp_ref)

        dhb = dh_ref[...].astype(bf16)
        dhb_ref[...] = dhb
        dmg = _nt_dot(dhb, wout_ref[...])
        ya = jnp.dot(z_ref[...], wrp_ref[...], preferred_element_type=f32)
        yb = jnp.dot(s_ref[...], wcp_ref[...], preferred_element_type=f32) + bcp_ref[...]
        sa = _sigmoid(ga_ref[...])
        sb = _sigmoid(gb_ref[...])
        mg_ref[...] = (sa * ya + sb * yb).astype(bf16)
        dgab_ref[:, 0:D] = (dmg * ya * sa * (1.0 - sa)).astype(bf16)
        dgab_ref[:, D:2 * D] = (dmg * yb * sb * (1.0 - sb)).astype(bf16)
        dya = dmg * sa
        dyb = dmg * sb
        dbcp_ref[...] += jnp.sum(dyb, axis=0, keepdims=True)
        dyab = dya.astype(bf16)
        dybb = dyb.astype(bf16)
        dya_ref[...] = dyab
        dyb_ref[...] = dybb
        dz_ref[...] = _nt_dot(dyab, wrp_ref[...])
        ds_ref[...] = _nt_dot(dybb, wcp_ref[...])

    row = pl.BlockSpec((TM, D), lambda i: (i, 0))
    wsq = pl.BlockSpec((D, D), lambda i: (0, 0))
    vec = pl.BlockSpec((1, D), lambda i: (0, 0))
    rowb = jax.ShapeDtypeStruct((t, D), bf16)
    body, in_specs, args = _ordered(
        body,
        [row, row, row,
         pl.BlockSpec((TM, D), lambda i: (i, 4)),
         pl.BlockSpec((TM, D), lambda i: (i, 5)),
         wsq, wsq, vec, wsq],
        (dh, z, s, proj, proj, wrp, wcp, bcp, wout), after)
    return pl.pallas_call(
        body, name="merge_bwd", grid=(t // TM,),
        in_specs=in_specs,
        out_specs=[row, row,
                   pl.BlockSpec((TM, 2 * D), lambda i: (i, 2)),
                   row, row, row, row, vec],
        out_shape=[jax.ShapeDtypeStruct((t, D), f32), jax.ShapeDtypeStruct((t, D), f32),
                   jax.ShapeDtypeStruct((t, NIN), bf16),
                   rowb, rowb, rowb, rowb, jax.ShapeDtypeStruct((1, D), f32)],
        compiler_params=_cp(1),
    )(*args)


def _conv_bwd(ds, vc, proj, dproj, w31, ln_g, ln_b, after=None):
    t = ds.shape[0]
    nt = t // TM
    hb = TM // HALO

    rb = 16
    nb = TM // rb
    taps = [(KC31 - 1 - (8 * m + s), s, m) for s in range(8) for m in range(HALO // 8)
            if 0 <= KC31 - 1 - (8 * m + s) < KC31]

    def groups(a):
        return jnp.sum(a.reshape(rb // 8, 8, D), axis=0)

    def body(ds_ref, vc_ref, gv_ref, gg_ref, gvp_ref, ggp_ref, dpin_ref, w_ref, lg_ref, lb_ref,
             dgvg_ref, dw_ref, db_ref, dlg_ref, dlb_ref, dext_sc, vext_sc, rot_sc, dwacc_sc, small_sc, wb_sc):
        del dpin_ref
        i = pl.program_id(0)
        tile = nt - 1 - i

        @pl.when(i == 0)
        def _():
            dext_sc[TM:TM + HALO, :] = jnp.zeros((HALO, D), f32)
            dwacc_sc[...] = jnp.zeros_like(dwacc_sc)
            small_sc[...] = jnp.zeros_like(small_sc)

        lg = lg_ref[...]
        lb = lb_ref[...]

        xhat, rstd = _ln_stats(vc_ref[...])
        ln = xhat * lg + lb
        sg = _sigmoid(ln)
        dln = ds_ref[...] * (sg * (1.0 + ln * (1.0 - sg)))
        dxh = dln * lg
        dvc = rstd * (dxh - jnp.mean(dxh, axis=-1, keepdims=True)
                      - xhat * jnp.mean(dxh * xhat, axis=-1, keepdims=True))
        small_sc[0] += jnp.sum((dln * xhat).reshape(TM // 8, 8, D), axis=0)
        small_sc[1] += jnp.sum(dln.reshape(TM // 8, 8, D), axis=0)
        small_sc[2] += jnp.sum(dvc.reshape(TM // 8, 8, D), axis=0)
        dext_sc[0:TM, :] = dvc
        vext_sc[HALO:HALO + TM, :] = gv_ref[...] * _sigmoid(gg_ref[...])
        vext_sc[0:HALO, :] = jnp.where(tile > 0, gvp_ref[...] * _sigmoid(ggp_ref[...]), 0.0)

        @pl.when(i == 0)
        def _():
            for k in range(KC31):
                wb_sc[k] = jnp.broadcast_to(w_ref[k:k + 1, :], (8, D))

        for s in range(1, 8):
            rot_sc[s - 1] = pltpu.roll(dext_sc[...], TM + HALO - s, 0)

        def dv_block(b, carry):
            rows = pl.ds(pl.multiple_of(b * rb, rb), rb)
            acc = jnp.zeros((rb, D), f32)
            for k, s, m in taps:
                src = pl.ds(pl.multiple_of(b * rb + 8 * m, 8), rb)
                slab = dext_sc[src, :] if s == 0 else rot_sc[s - 1, src, :]
                acc = acc + (slab.reshape(rb // 8, 8, D) * wb_sc[k]).reshape(rb, D)
            sgg = _sigmoid(gg_ref[rows, :])
            dgvg_ref[rows, 0:D] = (acc * sgg).astype(bf16)
            dgvg_ref[rows, D:2 * D] = (acc * gv_ref[rows, :] * sgg * (1.0 - sgg)).astype(bf16)
            return carry

        lax.fori_loop(0, nb, dv_block, 0)

        for s in range(1, 8):
            rot_sc[s - 1] = pltpu.roll(vext_sc[...], s, 0)
        for first in range(0, len(taps), 3):
            trio = taps[first:first + 3]

            def dw_block(b, accs, trio=trio):
                rows = pl.ds(pl.multiple_of(b * rb, rb), rb)
                dvc_blk = dext_sc[rows, :]
                out = []
                for acc, (k, s, m) in zip(accs, trio):
                    src = pl.ds(pl.multiple_of(b * rb + HALO - 8 * m, 8), rb)
                    slab = vext_sc[src, :] if s == 0 else rot_sc[s - 1, src, :]
                    out.append(acc + groups(dvc_blk * slab))
                return tuple(out)

            sums = lax.fori_loop(0, nb, dw_block, tuple(jnp.zeros((8, D), f32) for _ in trio))
            for acc, (k, s, m) in zip(sums, trio):
                dwacc_sc[k] += acc
        dext_sc[TM:TM + HALO, :] = dext_sc[0:HALO, :]

        @pl.when(i == nt - 1)
        def _():
            for k in range(KC31):
                dw_ref[k:k + 1, :] = jnp.sum(dwacc_sc[k], axis=0, keepdims=True)
            dlg_ref[...] = jnp.sum(small_sc[0], axis=0, keepdims=True)
            dlb_ref[...] = jnp.sum(small_sc[1], axis=0, keepdims=True)
            db_ref[...] = jnp.sum(small_sc[2], axis=0, keepdims=True)

    rev = lambda i: (nt - 1 - i, 0)
    vec = pl.BlockSpec((1, D), lambda i: (0, 0))
    halo_row = lambda i: jnp.maximum((nt - 1 - i) * hb - 1, 0)
    body, in_specs, args = _ordered(
        body,
        [pl.BlockSpec((TM, D), rev),
         pl.BlockSpec((TM, D), rev),
         pl.BlockSpec((TM, D), lambda i: (nt - 1 - i, 2)),
         pl.BlockSpec((TM, D), lambda i: (nt - 1 - i, 3)),
         pl.BlockSpec((HALO, D), lambda i: (halo_row(i), 2)),
         pl.BlockSpec((HALO, D), lambda i: (halo_row(i), 3)),
         pl.BlockSpec(memory_space=pl.ANY),
         pl.BlockSpec((KC31, D), lambda i: (0, 0)),
         vec, vec],
        (ds, vc, proj, proj, proj, proj, dproj, w31, ln_g, ln_b), after)
    return pl.pallas_call(
        body, name="conv_bwd", grid=(nt,),
        in_specs=in_specs,
        out_specs=[
            pl.BlockSpec((TM, 2 * D), lambda i: (nt - 1 - i, 1)),
            pl.BlockSpec((KC31, D), lambda i: (0, 0)),
            vec, vec, vec,
        ],
        out_shape=[jax.ShapeDtypeStruct((t, NIN), bf16),
                   jax.ShapeDtypeStruct((KC31, D), f32),
                   jax.ShapeDtypeStruct((1, D), f32), jax.ShapeDtypeStruct((1, D), f32),
                   jax.ShapeDtypeStruct((1, D), f32)],
        scratch_shapes=[pltpu.VMEM((TM + HALO, D), f32), pltpu.VMEM((TM + HALO, D), f32),
                        pltpu.VMEM((7, TM + HALO, D), f32), pltpu.VMEM((KC31, 8, D), f32),
                        pltpu.VMEM((3, 8, D), f32), pltpu.VMEM((KC31, 8, D), f32)],
        input_output_aliases={6 + _n_after(after): 0},
        compiler_params=_cp(1),
    )(*args)


def _rnn_bwd(dz, xr, hr, gates, proj, dproj, cw, wa, wx, lam):
    t = dz.shape[0]
    nt = t // TM
    ng = TM // 8
    hq = HD // NCHIP

    def body(dz_ref, xr_ref, hr_ref, hrp_ref, x_ref, xp_ref, y_ref, dpin_ref,
             cw_ref, wa_ref, gates_ref, wx_ref, lam_ref,
             dxy_ref, dwa_ref, dwx_ref, dcw_ref, dcb_ref, dba_ref, dbx_ref, dlam_ref,
             anext_sc, gcarry_sc, dext_sc, xext_sc, m_sc, g_sc, dwa_sc, dwx_sc, dsp_sc):
        del dpin_ref
        i = pl.program_id(0)
        tile = nt - 1 - i

        @pl.when(i == 0)
        def _():
            anext_sc[...] = jnp.zeros_like(anext_sc)
            gcarry_sc[...] = jnp.zeros_like(gcarry_sc)
            dext_sc[TM:TM + 8, :] = jnp.zeros((8, D), f32)
            dwa_sc[...] = jnp.zeros_like(dwa_sc)
            dwx_sc[...] = jnp.zeros_like(dwx_sc)
            dsp_sc[...] = jnp.zeros_like(dsp_sc)
            dcw_ref[...] = jnp.zeros_like(dcw_ref)
            dcb_ref[...] = jnp.zeros_like(dcb_ref)
            dba_ref[...] = jnp.zeros_like(dba_ref)
            dbx_ref[...] = jnp.zeros_like(dbx_ref)

        xr = xr_ref[...]
        hr = hr_ref[...]
        dz = dz_ref[...]
        gel, dgel = _gelu_and_grad(y_ref[...])
        dxy_ref[:, D:2 * D] = (dz * hr * dgel).astype(bf16)
        ra, ii, a, sq = gates_ref[0], gates_ref[1], gates_ref[2], gates_ref[3]
        sp = _softplus(-lam_ref[...])

        row = _row_ids((TM, D))
        m_sc[...] = jnp.where(row == TM - 1, anext_sc[...], pltpu.roll(a, TM - 1, 0))
        anext_sc[...] = a[0:1, :]
        g_sc[...] = dz * gel
        row8 = _row_ids((8, D))

        def group(qq, carry):
            off = pl.multiple_of((ng - 1 - qq) * 8, 8)
            mm = m_sc[pl.ds(off, 8), :]
            dd = g_sc[pl.ds(off, 8), :]
            for s in (1, 2, 4):
                m_sh = jnp.where(row8 < 8 - s, pltpu.roll(mm, 8 - s, 0), 1.0)
                d_sh = jnp.where(row8 < 8 - s, pltpu.roll(dd, 8 - s, 0), 0.0)
                dd = dd + mm * d_sh
                mm = mm * m_sh
            dd = dd + mm * carry
            g_sc[pl.ds(off, 8), :] = dd
            return dd[0:1, :]

        gcarry_sc[...] = lax.fori_loop(0, ng, group, gcarry_sc[...])
        gg = g_sc[...]

        hlast = jnp.where(tile > 0, hrp_ref[7:8, :], 0.0)
        hprev = jnp.where(row == 0, hlast, pltpu.roll(hr, 1, 0))
        d_a = gg * hprev
        dsq = gg * ii * xr
        dii = gg * sq * xr
        dxr = gg * sq * ii
        dlog = d_a * a - dsq * (a * a / sq)
        dsp_sc[...] += jnp.sum(dlog * (-8.0 * ra), axis=0, keepdims=True)
        dpa = dlog * (-8.0 * sp) * ra * (1.0 - ra)
        dpx = dii * ii * (1.0 - ii)
        dba_ref[...] += jnp.sum(dpa, axis=0, keepdims=True)
        dbx_ref[...] += jnp.sum(dpx, axis=0, keepdims=True)
        dpab = dpa.astype(bf16)
        dpxb = dpx.astype(bf16)
        xrb = xr.astype(bf16)
        back = []
        for hh in range(NHEAD):
            cols = slice(hh * HD, (hh + 1) * HD)
            back.append(_nt_dot(dpab[:, cols], wa_ref[hh]) + _nt_dot(dpxb[:, cols], wx_ref[hh]))
            dwa_sc[hh] += _tn_dot(xrb[:, cols], dpab[:, cols])
            dwx_sc[hh] += _tn_dot(xrb[:, cols], dpxb[:, cols])
        dxr = dxr + jnp.concatenate(back, axis=1)

        dext_sc[0:TM, :] = dxr
        de = dext_sc[...]
        dx = cw_ref[KC4 - 1:KC4, :] * dxr
        for k in range(KC4 - 1):
            dx = dx + cw_ref[k:k + 1, :] * pltpu.roll(de, TM + 8 - (KC4 - 1 - k), 0)[0:TM]
        dext_sc[TM:TM + 8, :] = dxr[0:8]
        dxy_ref[:, 0:D] = dx.astype(bf16)

        x = x_ref[...]
        xext_sc[0:8, :] = jnp.where(tile > 0, xp_ref[...], 0.0)
        xext_sc[8:8 + TM, :] = x
        xe = xext_sc[...]
        dcw_ref[KC4 - 1:KC4, :] += jnp.sum(dxr * x, axis=0, keepdims=True)
        for k in range(KC4 - 1):
            xs = pltpu.roll(xe, KC4 - 1 - k, 0)[8:8 + TM]
            dcw_ref[k:k + 1, :] += jnp.sum(dxr * xs, axis=0, keepdims=True)
        dcb_ref[...] += jnp.sum(dxr, axis=0, keepdims=True)

        @pl.when(i == nt - 1)
        def _():
            for hh in range(NHEAD):
                for qc in range(NCHIP):
                    dwa_ref[qc, hh] = dwa_sc[hh, qc * hq:(qc + 1) * hq, :].astype(bf16)
                    dwx_ref[qc, hh] = dwx_sc[hh, qc * hq:(qc + 1) * hq, :].astype(bf16)
            dlam_ref[...] = -dsp_sc[...] * _sigmoid(-lam_ref[...])

    rev = lambda i: (nt - 1 - i, 0)
    vec = pl.BlockSpec((1, D), lambda i: (0, 0))
    prev8 = lambda i: jnp.maximum((nt - 1 - i) * ng - 1, 0)
    wblk = pl.BlockSpec((NHEAD, HD, HD), lambda i: (0, 0, 0))
    gblk = pl.BlockSpec((NCHIP, NHEAD, hq, HD), lambda i: (0, 0, 0, 0))
    return pl.pallas_call(
        body, name="rnn_bwd", grid=(nt,),
        in_specs=[
            pl.BlockSpec((TM, D), rev),
            pl.BlockSpec((TM, D), rev),
            pl.BlockSpec((TM, D), rev),
            pl.BlockSpec((8, D), lambda i: (prev8(i), 0)),
            pl.BlockSpec((TM, D), lambda i: (nt - 1 - i, 0)),
            pl.BlockSpec((8, D), lambda i: (prev8(i), 0)),
            pl.BlockSpec((TM, D), lambda i: (nt - 1 - i, 1)),
            pl.BlockSpec(memory_space=pl.ANY),
            pl.BlockSpec((KC4, D), lambda i: (0, 0)),
            wblk, pl.BlockSpec((4, TM, D), lambda i: (0, nt - 1 - i, 0)), wblk, vec,
        ],
        out_specs=[
            pl.BlockSpec((TM, 2 * D), lambda i: (nt - 1 - i, 0)),
            gblk, gblk,
            pl.BlockSpec((KC4, D), lambda i: (0, 0)),
            vec, vec, vec, vec,
        ],
        out_shape=[jax.ShapeDtypeStruct((t, NIN), bf16),
                   jax.ShapeDtypeStruct((NCHIP, NHEAD, hq, HD), bf16),
                   jax.ShapeDtypeStruct((NCHIP, NHEAD, hq, HD), bf16),
                   jax.ShapeDtypeStruct((KC4, D), f32),
                   jax.ShapeDtypeStruct((1, D), f32), jax.ShapeDtypeStruct((1, D), f32),
                   jax.ShapeDtypeStruct((1, D), f32), jax.ShapeDtypeStruct((1, D), f32)],
        scratch_shapes=[pltpu.VMEM((1, D), f32), pltpu.VMEM((1, D), f32),
                        pltpu.VMEM((TM + 8, D), f32), pltpu.VMEM((TM + 8, D), f32),
                        pltpu.VMEM((TM, D), f32), pltpu.VMEM((TM, D), f32),
                        pltpu.VMEM((NHEAD, HD, HD), f32), pltpu.VMEM((NHEAD, HD, HD), f32),
                        pltpu.VMEM((1, D), f32)],
        input_output_aliases={7: 0},
        compiler_params=_cp(1),
    )(dz, xr, hr, hr, proj, proj, proj, dproj, cw, wa, gates, wx, lam)


def _inproj_bwd(dproj, dh, h, g, win, after=None):
    t = h.shape[0]
    tn = NIN // NCHIP
    nj = NIN // tn
    per = (NIN // NCHIP) // tn

    def body(dp_ref, dh_ref, h_ref, g_ref, w_ref, dhi_ref, dg_ref, db_ref, dn_sc):
        i = pl.program_id(0)
        j = pl.program_id(1)

        @pl.when(jnp.logical_and(i == 0, j == 0))
        def _():
            dg_ref[...] = jnp.zeros_like(dg_ref)
            db_ref[...] = jnp.zeros_like(db_ref)

        @pl.when(j == 0)
        def _():
            dn_sc[...] = jnp.zeros_like(dn_sc)

        dp = dp_ref[...]
        dn_sc[...] += _nt_dot(dp, w_ref[j])
        db_ref[j] += jnp.sum(dp.astype(f32), axis=0, keepdims=True)

        @pl.when(j == nj - 1)
        def _():
            dhin, dg = _rms_bwd(dn_sc[...], h_ref[...], g_ref[...])
            dhi_ref[...] = dh_ref[...] + dhin
            dg_ref[...] += dg

    rowd = pl.BlockSpec((TM, D), lambda i, j: (i, 0))
    vec = pl.BlockSpec((1, D), lambda i, j: (0, 0))
    body, in_specs, args = _ordered(
        body,
        [pl.BlockSpec((TM, tn), lambda i, j: (i, j)), rowd, rowd, vec,
         RESIDENT],
        (dproj, dh, h, g, win), after)
    return pl.pallas_call(
        body, name="inproj_bwd", grid=(t // TM, nj),
        in_specs=in_specs,
        out_specs=[rowd, vec, pl.BlockSpec((nj, 1, tn), lambda i, j: (0, 0, 0))],
        out_shape=[jax.ShapeDtypeStruct((t, D), f32), jax.ShapeDtypeStruct((1, D), f32),
                   jax.ShapeDtypeStruct((nj, 1, tn), f32)],
        scratch_shapes=[pltpu.VMEM((TM, D), f32)],
        compiler_params=_cp(2),
    )(*args)


def _ffn_gu_grad(n, dgate, dup, tag, after=None):
    half = _tn_matmul(n, dgate, D, FS, (NCHIP, D, FS), (None, D, FS), lambda k, nn, m: (nn, 0, 0),
                      tag + "_dwg", after=after)
    return _tn_matmul(n, dup, D, FS, (NCHIP, D, FS), (None, D, FS), lambda k, nn, m: (2 + nn, 0, 0),
                      tag + "_dwu", base=half)


def _ffn_down_grad(a, df, tag, after=None):
    return _tn_matmul(a, df, FS, D, (F, D), (FS, D), lambda k, nn, m: (k, 0), tag + "_dwd", after=after)


def _square_grad(a, b, name):
    return _tn_matmul(a, b, D, D, (D, D), (D, D), lambda k, nn, m: (0, 0), name)


ANY = pl.BlockSpec(memory_space=pl.ANY)


def _place():
    x, y, c = lax.axis_index("x"), lax.axis_index("y"), lax.axis_index("c")
    chips = [(1 - x, y), (x, 1 - y), (1 - x, 1 - y)]
    return x, y, c, chips


def _chip_id(chip):
    return 2 * chip[0] + chip[1]


def _cast_into_slot(w2d, qc, dtype, name, after=None):
    r, cc = w2d.shape
    hr = r // 2

    def body(qc_ref, *refs):
        del qc_ref
        w_ref, o_ref = refs[-2:]
        o_ref[...] = w_ref[...].astype(dtype)

    in_specs, args = [pl.BlockSpec((hr, cc), lambda h, qc_ref: (h, 0))], (w2d,)
    if after is not None:
        in_specs, args = [ANY_SPEC] + in_specs, (after,) + args
    return pl.pallas_call(
        body, name=name,
        grid_spec=pltpu.PrefetchScalarGridSpec(
            num_scalar_prefetch=1, grid=(2,),
            in_specs=in_specs,
            out_specs=pl.BlockSpec((None, None, hr, cc), lambda h, qc_ref: (qc_ref[0], h, 0, 0))),
        out_shape=jax.ShapeDtypeStruct((NCHIP, 2, hr, cc), dtype),
        compiler_params=_cp(1),
    )(qc, *args)


def _place_pack(pack, qc):
    def body(qc_ref, p_ref, o_ref):
        del qc_ref
        o_ref[...] = p_ref[...]

    return pl.pallas_call(
        body, name="place_pack",
        grid_spec=pltpu.PrefetchScalarGridSpec(
            num_scalar_prefetch=1, grid=(1,),
            in_specs=[pl.BlockSpec(pack.shape, lambda i, qc_ref: (0, 0))],
            out_specs=pl.BlockSpec((None,) + pack.shape, lambda i, qc_ref: (2 * qc_ref[0] + qc_ref[1], 0, 0))),
        out_shape=jax.ShapeDtypeStruct((8,) + pack.shape, pack.dtype),
        compiler_params=_cp(1),
    )(qc, pack)


def _pair_add(parts, gots, qc, name):
    n = len(parts)

    def body(qc_ref, *refs):
        s = pl.program_id(0)
        for a in range(n):
            val = (refs[a][...].astype(f32) + refs[n + a][...].astype(f32)).astype(bf16)
            refs[2 * n + a][...] = val

            @pl.when(s == qc_ref[0])
            def _(val=val, land_ref=refs[3 * n + a]):
                land_ref[...] = val

    shapes = [p.shape[2:] for p in parts]
    mine = [pl.BlockSpec((None, None) + sh, lambda s, qc_ref: (s, qc_ref[1], 0, 0)) for sh in shapes]
    block = [pl.BlockSpec((None,) + sh, lambda s, qc_ref: (s, 0, 0)) for sh in shapes]
    own = [pl.BlockSpec((None,) + sh, lambda s, qc_ref: (qc_ref[0], 0, 0)) for sh in shapes]
    outs = pl.pallas_call(
        body, name=name,
        grid_spec=pltpu.PrefetchScalarGridSpec(
            num_scalar_prefetch=1, grid=(NCHIP,), in_specs=mine + block, out_specs=block + own),
        out_shape=[jax.ShapeDtypeStruct((NCHIP,) + sh, bf16) for sh in shapes] * 2,
        compiler_params=_cp(1),
    )(qc, *parts, *gots)
    return list(outs[:n]), list(outs[n:])


def _sum_chips(gots, name):
    n = len(gots)

    def body(*refs):
        for a in range(n):
            acc = refs[a][0].astype(f32)
            for s in range(1, NCHIP):
                acc = acc + refs[a][s].astype(f32)
            refs[n + a][...] = acc

    return list(pl.pallas_call(
        body, name=name, grid=(1,),
        in_specs=[pl.BlockSpec(g.shape, lambda i: (0, 0, 0)) for g in gots],
        out_specs=[pl.BlockSpec(g.shape[1:], lambda i: (0, 0)) for g in gots],
        out_shape=[jax.ShapeDtypeStruct(g.shape[1:], f32) for g in gots],
        compiler_params=_cp(1),
    )(*gots))


def _pair_share(halves, name, after=None):
    n = len(halves)
    extra = () if after is None else (after,)

    def body(*refs):
        refs = refs[len(extra):]
        ins, outs = refs[:n], refs[n:2 * n]
        send_sems, recv_sems = refs[2 * n:]
        x, y, c, _ = _place()
        copies = []
        for a in range(n):
            cp = pltpu.make_async_remote_copy(
                src_ref=ins[a], dst_ref=outs[a], send_sem=send_sems.at[a], recv_sem=recv_sems.at[a],
                device_id=(x, y, 1 - c), device_id_type=MESH)
            cp.start()
            copies.append(cp)
        for cp in copies:
            cp.wait()

    return pl.pallas_call(
        body, name=name,
        in_specs=[ANY] * (len(extra) + n), out_specs=[ANY] * n,
        out_shape=[jax.ShapeDtypeStruct(s.shape, s.dtype) for s in halves],
        scratch_shapes=[pltpu.SemaphoreType.DMA((n,)), pltpu.SemaphoreType.DMA((n,))],
    )(*extra, *halves)


def _all_copy(buf_ref, send_ref, recv_ref, k, x, y, c, landing):
    px, py, pc = (1 - x if k & 4 else x, 1 - y if k & 2 else y, 1 - c if k & 1 else c)
    me = 4 * x + 2 * y + c
    there = 4 * px + 2 * py + pc
    return pltpu.make_async_remote_copy(
        src_ref=buf_ref.at[me], dst_ref=buf_ref.at[there if landing else me],
        send_sem=send_ref.at[k - 1], recv_sem=recv_ref.at[k - 1],
        device_id=(px, py, pc), device_id_type=MESH)


def _gather_all_start(buf, name):
    def body(in_ref, send, recv, thru, token):
        del thru
        x, y, c, _ = _place()
        for k in range(1, 8):
            _all_copy(in_ref, send, recv, k, x, y, c, False).start()
        token[...] = jnp.zeros_like(token)

    return pl.pallas_call(
        body, name=name,
        in_specs=[HBM],
        out_specs=[SEM, SEM, HBM, pl.BlockSpec(memory_space=pltpu.VMEM)],
        out_shape=[pltpu.SemaphoreType.DMA((7,)), pltpu.SemaphoreType.DMA((7,)),
                   pltpu.HBM(buf.shape, buf.dtype), jax.ShapeDtypeStruct((8, 128), f32)],
        input_output_aliases={0: 2},
        compiler_params=pltpu.CompilerParams(has_side_effects=EFFECT),
    )(_in_hbm(buf))


def _gather_all_wait(send, recv, buf, after, name):
    def body(in_ref, send_r, recv_r, after_ref, out_ref):
        del after_ref, out_ref
        x, y, c, _ = _place()
        for k in range(1, 8):
            cp = _all_copy(in_ref, send_r, recv_r, k, x, y, c, True)
            cp.wait_send()
            cp.wait_recv()

    return pl.pallas_call(
        body, name=name,
        in_specs=[HBM, SEM, SEM, ANY],
        out_specs=HBM,
        out_shape=pltpu.HBM(buf.shape, buf.dtype),
        input_output_aliases={0: 0},
        compiler_params=pltpu.CompilerParams(has_side_effects=EFFECT),
    )(buf, send, recv, after)


HBM = pl.BlockSpec(memory_space=pltpu.HBM)
SEM = pl.BlockSpec(memory_space=pltpu.SEMAPHORE)
EFFECT = pltpu.SideEffectType.DATAFLOW_SIDE_EFFECTING
N_PEER = 3


def _in_hbm(a):
    return pltpu.with_memory_space_constraint(a, pltpu.HBM)


def _gather_copy(buf_ref, send_ref, recv_ref, j, chip, q, c, landing_chip):
    return pltpu.make_async_remote_copy(
        src_ref=buf_ref.at[q, c], dst_ref=buf_ref.at[landing_chip, c],
        send_sem=send_ref.at[j], recv_sem=recv_ref.at[j],
        device_id=(chip[0], chip[1], c), device_id_type=MESH)


def _gather_start(bufs, name):
    n = len(bufs)

    def body(*refs):
        ins = refs[:n]
        send, recv = refs[n:2 * n], refs[2 * n:3 * n]
        token = refs[4 * n]
        x, y, c, chips = _place()
        q = 2 * x + y
        for a in range(n):
            for j, chip in enumerate(chips):
                _gather_copy(ins[a], send[a], recv[a], j, chip, q, c, q).start()
        token[...] = jnp.zeros_like(token)

    sems = [pltpu.SemaphoreType.DMA((N_PEER,))] * (2 * n)
    outs = pl.pallas_call(
        body, name=name,
        in_specs=[HBM] * n,
        out_specs=[SEM] * (2 * n) + [HBM] * n + [pl.BlockSpec(memory_space=pltpu.VMEM)],
        out_shape=sems + [pltpu.HBM(b.shape, b.dtype) for b in bufs] + [jax.ShapeDtypeStruct((8, 128), f32)],
        input_output_aliases={a: 2 * n + a for a in range(n)},
        compiler_params=pltpu.CompilerParams(has_side_effects=EFFECT),
    )(*[_in_hbm(b) for b in bufs])
    return list(outs[:n]), list(outs[n:2 * n]), list(outs[2 * n:3 * n]), outs[3 * n]


def _gather_wait(send, recv, bufs, after, name):
    n = len(bufs)

    def body(*refs):
        ins = refs[:n]
        send_r, recv_r = refs[n:2 * n], refs[2 * n:3 * n]
        x, y, c, chips = _place()
        q = 2 * x + y
        for a in range(n):
            for j, chip in enumerate(chips):
                cp = _gather_copy(ins[a], send_r[a], recv_r[a], j, chip, q, c, _chip_id(chip))
                cp.wait_send()
                cp.wait_recv()

    afters = after if isinstance(after, (tuple, list)) else (after,)
    outs = pl.pallas_call(
        body, name=name,
        in_specs=[HBM] * n + [SEM] * (2 * n) + [ANY] * len(afters),
        out_specs=[HBM] * n,
        out_shape=[pltpu.HBM(b.shape, b.dtype) for b in bufs],
        input_output_aliases={a: a for a in range(n)},
        compiler_params=pltpu.CompilerParams(has_side_effects=EFFECT),
    )(*bufs, *send, *recv, *afters)
    return list(outs)


def _forward_halves(bufs, name):
    n = len(bufs)

    def body(*refs):
        outs = refs[n:2 * n]
        send_sems, recv_sems = refs[2 * n:]
        x, y, c, chips = _place()
        sibling = (x, y, 1 - c)

        def remote(a, j, blk):
            return pltpu.make_async_remote_copy(src_ref=blk, dst_ref=blk, send_sem=send_sems.at[a, j],
                                                recv_sem=recv_sems.at[a, j], device_id=sibling,
                                                device_id_type=MESH)

        sent = []
        for a in range(n):
            for j, chip in enumerate(chips):
                cp = remote(a, j, outs[a].at[_chip_id(chip), c])
                cp.start()
                sent.append(cp)
        for a in range(n):
            for j, chip in enumerate(chips):
                remote(a, j, outs[a].at[_chip_id(chip), 1 - c]).wait_recv()
        for cp in sent:
            cp.wait_send()

    return pl.pallas_call(
        body, name=name,
        in_specs=[ANY] * n, out_specs=[ANY] * n,
        out_shape=[jax.ShapeDtypeStruct(s.shape, s.dtype) for s in bufs],
        scratch_shapes=[pltpu.SemaphoreType.DMA((n, N_PEER)), pltpu.SemaphoreType.DMA((n, N_PEER))],
        input_output_aliases={a: a for a in range(n)},
    )(*bufs)


def _reduce_copy(sum_ref, land_ref, send_ref, recv_ref, j, chip, q, c, landing_chip):
    return pltpu.make_async_remote_copy(
        src_ref=sum_ref.at[_chip_id(chip)], dst_ref=land_ref.at[landing_chip],
        send_sem=send_ref.at[j], recv_sem=recv_ref.at[j],
        device_id=(chip[0], chip[1], c), device_id_type=MESH)


def _reduce_start(sums, lands, name):
    n = len(sums)

    def body(*refs):
        s_in, l_in = refs[:n], refs[n:2 * n]
        send, recv = refs[2 * n:3 * n], refs[3 * n:4 * n]
        token = refs[6 * n]
        x, y, c, chips = _place()
        q = 2 * x + y
        for a in range(n):
            for j, chip in enumerate(chips):
                _reduce_copy(s_in[a], l_in[a], send[a], recv[a], j, chip, q, c, q).start()
        token[...] = jnp.zeros_like(token)

    sems = [pltpu.SemaphoreType.DMA((N_PEER,))] * (2 * n)
    outs = pl.pallas_call(
        body, name=name,
        in_specs=[HBM] * (2 * n),
        out_specs=[SEM] * (2 * n) + [HBM] * (2 * n) + [pl.BlockSpec(memory_space=pltpu.VMEM)],
        out_shape=sems + [pltpu.HBM(b.shape, b.dtype) for b in list(sums) + list(lands)]
        + [jax.ShapeDtypeStruct((8, 128), f32)],
        input_output_aliases={a: 2 * n + a for a in range(2 * n)},
        compiler_params=pltpu.CompilerParams(has_side_effects=EFFECT),
    )(*[_in_hbm(b) for b in list(sums) + list(lands)])
    return (list(outs[:n]), list(outs[n:2 * n]), list(outs[2 * n:3 * n]), list(outs[3 * n:4 * n]),
            outs[4 * n])


def _reduce_wait(send, recv, sums, lands, after, name):
    n = len(sums)

    def body(*refs):
        s_in, l_in = refs[:n], refs[n:2 * n]
        send_r, recv_r = refs[2 * n:3 * n], refs[3 * n:4 * n]
        x, y, c, chips = _place()
        q = 2 * x + y
        for a in range(n):
            for j, chip in enumerate(chips):
                cp = _reduce_copy(s_in[a], l_in[a], send_r[a], recv_r[a], j, chip, q, c, _chip_id(chip))
                cp.wait_send()
                cp.wait_recv()

    afters = after if isinstance(after, (tuple, list)) else (after,)
    outs = pl.pallas_call(
        body, name=name,
        in_specs=[HBM] * (2 * n) + [SEM] * (2 * n) + [ANY] * len(afters),
        out_specs=[HBM] * (2 * n),
        out_shape=[pltpu.HBM(b.shape, b.dtype) for b in list(sums) + list(lands)],
        input_output_aliases={a: a for a in range(2 * n)},
        compiler_params=pltpu.CompilerParams(has_side_effects=EFFECT),
    )(*sums, *lands, *send, *recv, *afters)
    return list(outs[n:])


def _sibling_copy(part_ref, land_ref, send_ref, recv_ref, x, y, c):
    return pltpu.make_async_remote_copy(
        src_ref=part_ref.at[:, 1 - c], dst_ref=land_ref, send_sem=send_ref.at[0], recv_sem=recv_ref.at[0],
        device_id=(x, y, 1 - c), device_id_type=MESH)


def _pair_exchange_start(parts, name):
    n = len(parts)
    lands = [lax.empty((NCHIP,) + p.shape[2:], p.dtype) for p in parts]

    def body(*refs):
        p_in, l_in = refs[:n], refs[n:2 * n]
        send, recv = refs[2 * n:3 * n], refs[3 * n:4 * n]
        token = refs[6 * n]
        x, y, c, _ = _place()
        for a in range(n):
            _sibling_copy(p_in[a], l_in[a], send[a], recv[a], x, y, c).start()
        token[...] = jnp.zeros_like(token)

    sems = [pltpu.SemaphoreType.DMA((1,))] * (2 * n)
    outs = pl.pallas_call(
        body, name=name,
        in_specs=[HBM] * (2 * n),
        out_specs=[SEM] * (2 * n) + [HBM] * (2 * n) + [pl.BlockSpec(memory_space=pltpu.VMEM)],
        out_shape=sems + [pltpu.HBM(b.shape, b.dtype) for b in list(parts) + lands]
        + [jax.ShapeDtypeStruct((8, 128), f32)],
        input_output_aliases={a: 2 * n + a for a in range(2 * n)},
        compiler_params=pltpu.CompilerParams(has_side_effects=EFFECT),
    )(*[_in_hbm(b) for b in list(parts) + lands])
    return (list(outs[:n]), list(outs[n:2 * n]), list(outs[2 * n:3 * n]), list(outs[3 * n:4 * n]),
            outs[4 * n])


def _pair_exchange_wait(send, recv, parts, lands, after, name):
    n = len(parts)

    def body(*refs):
        p_in, l_in = refs[:n], refs[n:2 * n]
        send_r, recv_r = refs[2 * n:3 * n], refs[3 * n:4 * n]
        x, y, c, _ = _place()
        for a in range(n):
            cp = _sibling_copy(p_in[a], l_in[a], send_r[a], recv_r[a], x, y, c)
            cp.wait_send()
            cp.wait_recv()

    outs = pl.pallas_call(
        body, name=name,
        in_specs=[HBM] * (2 * n) + [SEM] * (2 * n) + [ANY],
        out_specs=[HBM] * (2 * n),
        out_shape=[pltpu.HBM(b.shape, b.dtype) for b in list(parts) + list(lands)],
        input_output_aliases={a: a for a in range(2 * n)},
        compiler_params=pltpu.CompilerParams(has_side_effects=EFFECT),
    )(*parts, *lands, *send, *recv, after)
    return list(outs[:n]), list(outs[n:])


def _forward_copy(buf_ref, send_ref, recv_ref, j, chip, x, y, c, landing):
    return pltpu.make_async_remote_copy(
        src_ref=buf_ref.at[_chip_id(chip), c], dst_ref=buf_ref.at[_chip_id(chip), 1 - c if landing else c],
        send_sem=send_ref.at[j], recv_sem=recv_ref.at[j], device_id=(x, y, 1 - c), device_id_type=MESH)


def _forward_start(bufs, name):
    n = len(bufs)

    def body(*refs):
        ins = refs[:n]
        send, recv = refs[n:2 * n], refs[2 * n:3 * n]
        token = refs[4 * n]
        x, y, c, chips = _place()
        for a in range(n):
            for j, chip in enumerate(chips):
                _forward_copy(ins[a], send[a], recv[a], j, chip, x, y, c, False).start()
        token[...] = jnp.zeros_like(token)

    sems = [pltpu.SemaphoreType.DMA((N_PEER,))] * (2 * n)
    outs = pl.pallas_call(
        body, name=name,
        in_specs=[HBM] * n,
        out_specs=[SEM] * (2 * n) + [HBM] * n + [pl.BlockSpec(memory_space=pltpu.VMEM)],
        out_shape=sems + [pltpu.HBM(b.shape, b.dtype) for b in bufs] + [jax.ShapeDtypeStruct((8, 128), f32)],
        input_output_aliases={a: 2 * n + a for a in range(n)},
        compiler_params=pltpu.CompilerParams(has_side_effects=EFFECT),
    )(*[_in_hbm(b) for b in bufs])
    return list(outs[:n]), list(outs[n:2 * n]), list(outs[2 * n:3 * n]), outs[3 * n]


def _forward_wait(send, recv, bufs, after, name):
    n = len(bufs)

    def body(*refs):
        ins = refs[:n]
        send_r, recv_r = refs[n:2 * n], refs[2 * n:3 * n]
        x, y, c, chips = _place()
        for a in range(n):
            for j, chip in enumerate(chips):
                cp = _forward_copy(ins[a], send_r[a], recv_r[a], j, chip, x, y, c, True)
                cp.wait_send()
                cp.wait_recv()

    outs = pl.pallas_call(
        body, name=name,
        in_specs=[HBM] * n + [SEM] * (2 * n) + [ANY],
        out_specs=[HBM] * n,
        out_shape=[pltpu.HBM(b.shape, b.dtype) for b in bufs],
        input_output_aliases={a: a for a in range(n)},
        compiler_params=pltpu.CompilerParams(has_side_effects=EFFECT),
    )(*bufs, *send, *recv, after)
    return list(outs)


def _adamw_math(w, g, m, v):
    m = ADAM_B1 * m + (1.0 - ADAM_B1) * g
    v = ADAM_B2 * v + (1.0 - ADAM_B2) * (g * g)
    m_hat = m / (1.0 - ADAM_B1 ** ADAM_STEP)
    v_hat = v / (1.0 - ADAM_B2 ** ADAM_STEP)
    delta = -ADAM_LR * (m_hat / (jnp.sqrt(v_hat) + ADAM_EPS) + ADAM_WD * w)
    return delta, m, v


def _adamw(w, mine, theirs, m, v, qc, name):
    r, cc = w.shape
    hr = r // 2
    tr = next(hr // k for k in range(1, hr + 1)
              if hr % k == 0 and (hr // k) % 8 == 0 and (hr // k) * cc * 4 <= (3 << 19))
    nb = hr // tr

    def body(qc_ref, w_ref, a_ref, b_ref, m_ref, v_ref, g_ref, d_ref, mo_ref, vo_ref):
        g = jnp.where(pl.program_id(0) == qc_ref[1], a_ref[...], b_ref[...])
        g_ref[...] = g
        d_ref[...], mo_ref[...], vo_ref[...] = _adamw_math(w_ref[...], g, m_ref[...], v_ref[...])

    full = pl.BlockSpec((tr, cc), lambda h, i, qc_ref: (h * nb + i, 0))
    half = pl.BlockSpec((tr, cc), lambda h, i, qc_ref: (i, 0))
    return pl.pallas_call(
        body, name=name,
        grid_spec=pltpu.PrefetchScalarGridSpec(
            num_scalar_prefetch=1, grid=(2, nb),
            in_specs=[full, half, half, full, full], out_specs=[full] * 4),
        out_shape=[jax.ShapeDtypeStruct((r, cc), f32)] * 4,
        compiler_params=_cp(2),
    )(qc, w, mine, theirs, m, v)


REPL = [("ffn1_norm", 1), ("mix_norm", 1), ("b_in", 6), ("rnn_conv_b", 1), ("rg_b_a", 1), ("rg_b_x", 1),
        ("rg_lambda", 1), ("conv_dw_b", 1), ("conv_ln_g", 1), ("conv_ln_b", 1), ("conv_b_proj", 1),
        ("ffn2_norm", 1), ("final_norm", 1)]
COLSH = [("meta_tokens", NMETA), ("rnn_conv_w", KC4), ("conv_dw_w", KC31)]
SMALL = REPL + COLSH
CS = D // NCHIP


def _pack_rows():
    starts, row = {}, 0
    for k, rows in REPL:
        starts[k] = row
        row += rows
    for k, rows in COLSH:
        row = -(-row // 8) * 8
        starts[k] = row
        row += rows
    return starts, -(-row // 8) * 8


PACK_START, LOSS_ROW = _pack_rows()
SMALL_ROWS = LOSS_ROW + 8


def _small_pack(g, loss_row):
    pieces, row = [], 0
    for k, rows in SMALL:
        if PACK_START[k] > row:
            pieces.append(jnp.zeros((PACK_START[k] - row, D), f32))
        pieces.append(g[k].reshape(rows, D))
        row = PACK_START[k] + rows
    pieces.append(jnp.zeros((LOSS_ROW - row, D), f32))
    pieces.append(loss_row)
    pieces.append(jnp.zeros((SMALL_ROWS - LOSS_ROW - 1, D), f32))
    return jnp.concatenate(pieces, axis=0)


def _adamw_small(packs, ws, ms, vs):
    ns = len(SMALL)

    def body(*refs):
        pack_ref = refs[0]
        w_refs, m_refs, v_refs = refs[1:1 + ns], refs[1 + ns:1 + 2 * ns], refs[1 + 2 * ns:1 + 3 * ns]
        outs = refs[1 + 3 * ns:1 + 7 * ns]
        g_refs, d_refs, mo_refs, vo_refs = outs[:ns], outs[ns:2 * ns], outs[2 * ns:3 * ns], outs[3 * ns:]
        loss_ref = refs[1 + 7 * ns]
        gsum_sc = refs[2 + 7 * ns]
        q = 2 * lax.axis_index("x") + lax.axis_index("y")
        acc = pack_ref[0]
        for dev in range(1, 8):
            acc = acc + pack_ref[dev]
        gsum_sc[...] = acc
        loss_ref[...] = gsum_sc[LOSS_ROW:LOSS_ROW + 1, :]
        for idx, (name, rows) in enumerate(SMALL):
            row = PACK_START[name]
            if idx < len(REPL):
                for k in range(rows):
                    cols = slice(k * D, (k + 1) * D)
                    g = gsum_sc[row + k:row + k + 1, :]
                    d, mm, vv = _adamw_math(w_refs[idx][:, cols], g, m_refs[idx][:, cols], v_refs[idx][:, cols])
                    g_refs[idx][:, cols] = g
                    d_refs[idx][:, cols] = d
                    mo_refs[idx][:, cols] = mm
                    vo_refs[idx][:, cols] = vv
            else:
                g = gsum_sc[row:row + rows, pl.ds(pl.multiple_of(q * CS, CS), CS)]
                d, mm, vv = _adamw_math(w_refs[idx][...], g, m_refs[idx][...], v_refs[idx][...])
                g_refs[idx][...] = g
                d_refs[idx][...] = d
                mo_refs[idx][...] = mm
                vo_refs[idx][...] = vv

    shapes = [jax.ShapeDtypeStruct(w.shape, f32) for w in ws]
    return pl.pallas_call(
        body, name="adamw_small",
        out_shape=shapes * 4 + [jax.ShapeDtypeStruct((1, D), f32)],
        scratch_shapes=[pltpu.VMEM((SMALL_ROWS, D), f32)],
        compiler_params=pltpu.CompilerParams(vmem_limit_bytes=VMEM_LIMIT),
    )(packs, *ws, *ms, *vs)


BIG = ["ffn1_w_gu", "ffn1_w_down", "w_in", "rg_w_a", "rg_w_x", "rnn_w_proj", "conv_w_proj", "w_out",
       "ffn2_w_gu", "ffn2_w_down"]
WEIGHTS = ['meta_tokens', 'ffn1_norm', 'ffn1_w_gu', 'ffn1_w_down', 'mix_norm', 'w_in', 'b_in', 'rnn_conv_w',
           'rnn_conv_b', 'rg_w_a', 'rg_b_a', 'rg_w_x', 'rg_b_x', 'rg_lambda', 'rnn_w_proj', 'conv_dw_w',
           'conv_dw_b', 'conv_ln_g', 'conv_ln_b', 'conv_w_proj', 'conv_b_proj', 'w_out', 'ffn2_norm',
           'ffn2_w_gu', 'ffn2_w_down', 'final_norm']


def _as2d(a):
    return a.reshape(-1, a.shape[-1])


def _step(x, loss_target, w, m, v):
    seq = x.shape[1]
    n_valid = NMETA + seq
    t = -(-n_valid // TM) * TM

    qc = jnp.stack([2 * lax.axis_index("x") + lax.axis_index("y"), lax.axis_index("c")]).astype(jnp.int32)
    p = {k: w[k].reshape(1, rows * D) for k, rows in REPL}

    first = ["ffn1_w_gu", "ffn1_w_down", "small"]
    later = [["w_in"], ["rg_w_a", "rg_w_x", "rnn_w_proj", "conv_w_proj", "w_out"], ["ffn2_w_gu", "ffn2_w_down"]]
    small_rows = sum(r for _, r in COLSH)
    small = jnp.concatenate([_as2d(w[k]) for k, _ in COLSH] + [jnp.zeros((64 - small_rows, CS), f32)], axis=0)

    def cast(k, token=None):
        src, dtype = (small, f32) if k == "small" else (_as2d(w[k]), bf16)
        return _cast_into_slot(src, qc, dtype, "cast_" + k, after=token)

    send1, recv1, bufs1, token1 = _gather_start([cast(k) for k in first], "gather_start_first")
    rest = [k for grp in later for k in grp]
    send2, recv2, bufs2, token2 = _gather_start([cast(k, token1) for k in rest], "gather_start_rest")

    def install(names, done):
        for k, b in zip(names, done):
            full = b.reshape(NCHIP, 2 * b.shape[2], b.shape[3])
            if k in ("ffn1_w_down", "ffn2_w_down"):
                full = full.reshape(F, D)
            elif k in ("rnn_w_proj", "conv_w_proj", "w_out"):
                full = full.reshape(D, D)
            elif k in ("rg_w_a", "rg_w_x"):
                full = full.reshape(NCHIP, NHEAD, HD // NCHIP, HD).transpose(1, 0, 2, 3).reshape(NHEAD, HD, HD)
            p[k] = full

    def finish(names, send, recv, bufs, after, tag):
        install(names, _forward_halves(_gather_wait(send, recv, bufs, after, "gather_wait_" + tag),
                                       "gather_forward_" + tag))

    def group(names):
        idx = [rest.index(k) for k in names]
        return names, [send2[i] for i in idx], [recv2[i] for i in idx], [bufs2[i] for i in idx]

    h0 = jnp.pad(x[0] + token1[0:1, 0:1], ((NMETA, t - n_valid), (0, 0)))
    tgt = jnp.pad(loss_target[0] + token2[0:1, 0:1], ((NMETA, t - n_valid), (0, 0)))
    finish(first, send1, recv1, bufs1, (token2, h0, tgt), "first")
    small_full = p.pop("small").transpose(1, 0, 2).reshape(64, D)
    row = 0
    for k, rows in COLSH:
        p[k] = small_full[row:row + rows]
        row += rows

    h0 = lax.dynamic_update_slice(h0, p["meta_tokens"], (0, 0))
    h1, gate1, up1, n1 = _ffn_fwd(h0, p["ffn1_norm"], p["ffn1_w_gu"], p["ffn1_w_down"], "ffn1_fwd")
    finish(*group(later[0]), h1, "in")
    proj, n2 = _inproj_fwd(h1, p["mix_norm"], p["w_in"], p["b_in"])
    names_l = later[1] + later[2]
    _, send_l, recv_l, bufs_l = group(names_l)
    send_f, recv_f, bufs_f, token = _forward_start(
        _gather_wait(send_l, recv_l, bufs_l, proj, "gather_wait_late"), "gather_forward_start")
    vc, s = _conv_fwd(proj, p["conv_dw_w"], p["conv_dw_b"], p["conv_ln_g"], p["conv_ln_b"], after=token)
    install(names_l, _forward_wait(send_f, recv_f, bufs_f, vc, "gather_forward_wait"))
    xr, hr, z, gates = _rnn_fwd(proj, p["rnn_conv_w"], p["rnn_conv_b"], p["rg_w_a"], p["rg_b_a"],
                         p["rg_w_x"], p["rg_b_x"], p["rg_lambda"])
    h2 = _merge_fwd(h1, z, s, proj, p["rnn_w_proj"], p["conv_w_proj"], p["conv_b_proj"], p["w_out"])
    dh3, loss_blk, d_final, gate2, up2, n3 = _ffn_fwd(
        h2, p["ffn2_norm"], p["ffn2_w_gu"], p["ffn2_w_down"], "ffn2_fwd",
        loss_head=(p["final_norm"], tgt, n_valid))

    g = {"final_norm": d_final}
    pending = []

    def exchange_start(names, tag):
        parts = []
        for k in names:
            rows = g[k].size // (NCHIP * g[k].shape[-1])
            parts.append(g[k].reshape((NCHIP, 2, rows // 2, g[k].shape[-1])))
        send, recv, parts, lands, token = _pair_exchange_start(parts, "pair_exchange_start_" + tag)
        return (names, tag, send, recv, parts, lands), token

    def reduce_start(state, after):
        names, tag, send, recv, parts, lands = state
        parts, from_sibling = _pair_exchange_wait(send, recv, parts, lands, after, "pair_exchange_wait_" + tag)
        sums, lands = _pair_add(parts, from_sibling, qc, "pair_add_" + tag)
        send, recv, sums, lands, token = _reduce_start(sums, lands, "reduce_start_" + tag)
        pending.append((names, tag, send, recv, sums, lands))
        return token

    dh2, dgate2, dup2, a2, df2, g["ffn2_norm"] = _ffn_bwd(
        dh3, h2, p["ffn2_norm"], gate2, up2, p["ffn2_w_gu"], p["ffn2_w_down"], "ffn2_bwd")
    g["ffn2_w_gu"] = _ffn_gu_grad(n3, dgate2, dup2, "ffn2")
    g["ffn2_w_down"] = _ffn_down_grad(a2, df2, "ffn2")
    state, token = exchange_start(["ffn2_w_gu", "ffn2_w_down"], "ffn2")

    dz, ds, dproj, dh2b, merged, dya, dyb, g["conv_b_proj"] = _merge_bwd(
        dh2, z, s, proj, p["rnn_w_proj"], p["conv_w_proj"], p["conv_b_proj"], p["w_out"], after=token)
    token = reduce_start(state, dz)
    dproj, g["conv_dw_w"], g["conv_dw_b"], g["conv_ln_g"], g["conv_ln_b"] = _conv_bwd(
        ds, vc, proj, dproj, p["conv_dw_w"], p["conv_ln_g"], p["conv_ln_b"], after=token)
    g["w_out"] = _square_grad(merged, dh2b, "dw_out")
    g["rnn_w_proj"] = _square_grad(z, dya, "dw_rnn_proj")
    g["conv_w_proj"] = _square_grad(s, dyb, "dw_conv_proj")
    (dproj, g["rg_w_a"], g["rg_w_x"], g["rnn_conv_w"], g["rnn_conv_b"], g["rg_b_a"], g["rg_b_x"],
     g["rg_lambda"]) = _rnn_bwd(dz, xr, hr, gates, proj, dproj, p["rnn_conv_w"], p["rg_w_a"],
                                p["rg_w_x"], p["rg_lambda"])

    dh1, g["mix_norm"], db_in = _inproj_bwd(dproj, dh2, h1, p["mix_norm"], p["w_in"])
    g["b_in"] = db_in.reshape(1, NIN)
    g["w_in"] = _tn_matmul(n2, dproj, D, NIN // NCHIP, (NCHIP, D, NIN // NCHIP),
                           (None, D, NIN // NCHIP), lambda k, nn, mm: (nn, 0, 0), "dw_in")
    state, token = exchange_start(["w_out", "rnn_w_proj", "conv_w_proj", "rg_w_a", "rg_w_x", "w_in"], "mix")

    dh0, dgate1, dup1, a1, df1, g["ffn1_norm"] = _ffn_bwd(
        dh1, h0, p["ffn1_norm"], gate1, up1, p["ffn1_w_gu"], p["ffn1_w_down"], "ffn1_bwd", after=token)
    g["meta_tokens"] = dh0[0:NMETA]
    grad_x = dh0[NMETA:n_valid][None]
    token = reduce_start(state, dh0)

    send_s, recv_s, pack_buf, token_s = _gather_all_start(
        _place_pack(_small_pack(g, loss_blk.reshape(1, D)), qc), "gather_all_start")
    g["ffn1_w_down"] = _ffn_down_grad(a1, df1, "ffn1", after=(token, token_s))
    state, token = exchange_start(["ffn1_w_down"], "ffn1_down")
    gate_half = _tn_matmul(n1, dgate1, D, FS, (NCHIP, D, FS), (None, D, FS), lambda k, nn, mm: (nn, 0, 0),
                           "ffn1_dwg", after=token)
    token = reduce_start(state, gate_half)
    g["ffn1_w_gu"] = _tn_matmul(n1, dup1, D, FS, (NCHIP, D, FS), (None, D, FS), lambda k, nn, mm: (2 + nn, 0, 0),
                                "ffn1_dwu", base=gate_half, after=token)
    state_gu, token = exchange_start(["ffn1_w_gu"], "ffn1_gu")
    packs = _gather_all_wait(send_s, recv_s, pack_buf, token, "gather_all_wait")

    grads, deltas, new_m, new_v = {}, {}, {}, {}

    def landed_sums(items, after):
        names, mine = [], []
        for grp_names, grp_tag, send, recv, sums, lands in items:
            landed = _reduce_wait(send, recv, sums, lands, after, "reduce_wait_" + grp_tag)
            mine += _sum_chips(landed, "sum_chips_" + grp_tag)
            names += grp_names
            after = mine[-1]
        return names, mine

    def share_and_update(names, mine, tag, after=None):
        theirs = _pair_share(mine, "pair_share_" + tag, after=after)
        for k, mi, th in zip(names, mine, theirs):
            outs = _adamw(_as2d(w[k]), mi, th, _as2d(m[k]), _as2d(v[k]), qc, "adamw_" + k)
            grads[k], deltas[k], new_m[k], new_v[k] = (a.reshape(w[k].shape) for a in outs)
        return [new_v[k] for k in names]

    early_names, early_mine = landed_sums(pending[:2], packs)
    token = reduce_start(state_gu, early_mine[-1])
    after = share_and_update(early_names, early_mine, "early", after=token)
    share_and_update(*landed_sums(pending[2:], after), "late")
    names = [k for k, _ in SMALL]
    shape2 = {k: ((1, rows * D) if (k, rows) in REPL else (rows, CS)) for k, rows in SMALL}
    outs = _adamw_small(packs, *[[a[k].reshape(shape2[k]) for k in names] for a in (w, m, v)])
    ns = len(names)
    for i, k in enumerate(names):
        grads[k], deltas[k], new_m[k], new_v[k] = (outs[j * ns + i].reshape(w[k].shape) for j in range(4))

    loss = outs[4 * ns][0, 0]
    return (loss, grad_x, *[grads[k] for k in WEIGHTS], *[deltas[k] for k in WEIGHTS],
            *[new_m[k] for k in WEIGHTS], *[new_v[k] for k in WEIGHTS])


def kernel(x, meta_tokens, ffn1_norm, ffn1_w_gu, ffn1_w_down, mix_norm, w_in, b_in, rnn_conv_w, rnn_conv_b, rg_w_a, rg_b_a, rg_w_x, rg_b_x, rg_lambda, rnn_w_proj, conv_dw_w, conv_dw_b, conv_ln_g, conv_ln_b, conv_w_proj, conv_b_proj, w_out, ffn2_norm, ffn2_w_gu, ffn2_w_down, final_norm, loss_target, m_meta_tokens, m_ffn1_norm, m_ffn1_w_gu, m_ffn1_w_down, m_mix_norm, m_w_in, m_b_in, m_rnn_conv_w, m_rnn_conv_b, m_rg_w_a, m_rg_b_a, m_rg_w_x, m_rg_b_x, m_rg_lambda, m_rnn_w_proj, m_conv_dw_w, m_conv_dw_b, m_conv_ln_g, m_conv_ln_b, m_conv_w_proj, m_conv_b_proj, m_w_out, m_ffn2_norm, m_ffn2_w_gu, m_ffn2_w_down, m_final_norm, v_meta_tokens, v_ffn1_norm, v_ffn1_w_gu, v_ffn1_w_down, v_mix_norm, v_w_in, v_b_in, v_rnn_conv_w, v_rnn_conv_b, v_rg_w_a, v_rg_b_a, v_rg_w_x, v_rg_b_x, v_rg_lambda, v_rnn_w_proj, v_conv_dw_w, v_conv_dw_b, v_conv_ln_g, v_conv_ln_b, v_conv_w_proj, v_conv_b_proj, v_w_out, v_ffn2_norm, v_ffn2_w_gu, v_ffn2_w_down, v_final_norm):
    args = locals()
    w = {k: args[k] for k in WEIGHTS}
    m = {k: args["m_" + k] for k in WEIGHTS}
    v = {k: args["v_" + k] for k in WEIGHTS}
    return _step(x, loss_target, w, m, v)
```

```python
import jax
import jax.numpy as jnp
from jax import lax
from jax.experimental import pallas as pl
from jax.experimental.pallas import tpu as pltpu

f32 = jnp.float32
bf16 = jnp.bfloat16

D = 1024
F = 2816
FS = F // 2
NIN = 6 * D
NMETA = 16
NHEAD = 4
HD = D // NHEAD
KC4 = 4
KC31 = 31
HALO = 32
EPS = 1e-6
TM = 416
NCHIP = 4
MESH = pl.DeviceIdType.MESH

ADAM_LR = 0.001
ADAM_B1 = 0.9
ADAM_B2 = 0.999
ADAM_EPS = 1e-08
ADAM_WD = 0.01
ADAM_STEP = 10

VMEM_LIMIT = 56 * 1024 * 1024
FSUB = [(o, min(256, FS - o)) for o in range(0, FS, 256)]


def _cp(n_axes, **kw):
    return pltpu.CompilerParams(dimension_semantics=("arbitrary",) * n_axes,
                                vmem_limit_bytes=VMEM_LIMIT, **kw)


RESIDENT = pl.BlockSpec(memory_space=pltpu.VMEM)


def _n_after(after):
    return 0 if after is None else (len(after) if isinstance(after, (tuple, list)) else 1)


def _ordered(body, in_specs, args, after):
    if after is None:
        return body, in_specs, args
    extra = tuple(after) if isinstance(after, (tuple, list)) else (after,)
    return (lambda *refs: body(*refs[len(extra):]),
            [pl.BlockSpec(memory_space=pl.ANY)] * len(extra) + list(in_specs), extra + tuple(args))


def _nt_dot(a, b):
    return lax.dot_general(a, b, (((1,), (1,)), ((), ())), preferred_element_type=f32)


def _tn_dot(a, b):
    return lax.dot_general(a, b, (((0,), (0,)), ((), ())), preferred_element_type=f32)


def _sigmoid(x):
    return 0.5 * jnp.tanh(0.5 * x) + 0.5


def _log1p(y):
    u = 1.0 + y
    d = u - 1.0
    return jnp.where(d == 0.0, y, jnp.log(u) * (y / jnp.where(d == 0.0, 1.0, d)))


def _softplus(x):
    return jnp.maximum(x, 0.0) + _log1p(jnp.exp(-jnp.abs(x)))


def _one_minus_square(a, log_a):
    x = 2.0 * log_a
    series = x * (1.0 + x * (0.5 + x * (1.0 / 6.0)))
    return jnp.where(jnp.abs(x) < 0.03, -series, 1.0 - a * a)


_GELU_C = 0.7978845608028654
_GELU_K = 0.044715


def _gelu_and_grad(y):
    y2 = y * y
    th = jnp.tanh(_GELU_C * (y + _GELU_K * y * y2))
    gel = 0.5 * y * (1.0 + th)
    dgel = 0.5 * (1.0 + th) + 0.5 * y * (1.0 - th * th) * _GELU_C * (1.0 + 3.0 * _GELU_K * y2)
    return gel, dgel


def _rms_stats(h):
    return lax.rsqrt(jnp.mean(h * h, axis=-1, keepdims=True) + EPS)


def _rms_bwd(dn, h, g):
    r = _rms_stats(h)
    nhat = h * r
    dnh = dn * g
    dh = r * (dnh - nhat * jnp.mean(dnh * nhat, axis=-1, keepdims=True))
    dg = jnp.sum(dn * nhat, axis=0, keepdims=True)
    return dh, dg


def _row_ids(shape):
    return lax.broadcasted_iota(jnp.int32, shape, 0)


def _ffn_fwd(h, g, wgu, wd, name, loss_head=None):
    t = h.shape[0]
    nj = 2
    tm = TM
    n_head = 0 if loss_head is None else 2

    def body(*refs):
        h_ref, g_ref, wg_ref, wd_ref = refs[:4]
        outs = refs[4 + n_head:]
        gate_ref, up_ref, n_ref, nb_sc, acc_sc, a_sc = outs[-6:]
        i = pl.program_id(0)
        j = pl.program_id(1)

        @pl.when(j == 0)
        def _():
            hh = h_ref[...]
            nb = (hh * _rms_stats(hh) * g_ref[...]).astype(bf16)
            nb_sc[...] = nb
            n_ref[...] = nb
            acc_sc[...] = jnp.zeros_like(acc_sc)

        nb = nb_sc[...]
        for off, width in FSUB:
            cols = slice(off, off + width)
            gt = jnp.dot(nb, wg_ref[j, :, cols], preferred_element_type=f32)
            up = jnp.dot(nb, wg_ref[2 + j, :, cols], preferred_element_type=f32)
            gate_ref[:, cols] = gt.astype(bf16)
            up_ref[:, cols] = up.astype(bf16)
            a_sc[:, cols] = (gt * _sigmoid(gt) * up).astype(bf16)
        acc_sc[...] += jnp.dot(a_sc[...], wd_ref[j], preferred_element_type=f32)

        if loss_head is None:
            @pl.when(j == nj - 1)
            def _():
                outs[0][...] = h_ref[...] + 0.5 * acc_sc[...]
        else:
            gf_ref, t_ref = refs[4:6]
            dh_ref, loss_ref, dgf_ref = outs[:3]

            @pl.when(jnp.logical_and(i == 0, j == 0))
            def _():
                loss_ref[...] = jnp.zeros_like(loss_ref)
                dgf_ref[...] = jnp.zeros_like(dgf_ref)

            @pl.when(j == nj - 1)
            def _():
                hh = h_ref[...] + 0.5 * acc_sc[...]
                gf = gf_ref[...]
                row = i * tm + _row_ids((tm, 1))
                valid = jnp.logical_and(row >= NMETA, row < loss_head[2])
                err = jnp.where(valid, hh * _rms_stats(hh) * gf - t_ref[...], 0.0)
                loss_ref[...] += 0.5 * jnp.sum(err * err) * (1.0 / D)
                dh, dgf = _rms_bwd(err * (1.0 / D), hh, gf)
                dh_ref[...] = dh
                dgf_ref[...] += dgf

    rowd = pl.BlockSpec((tm, D), lambda i, j: (i, 0))
    vec = pl.BlockSpec((1, D), lambda i, j: (0, 0))
    rowf = pl.BlockSpec((tm, FS), lambda i, j: (i, j))
    in_specs, args = [rowd, vec, RESIDENT, RESIDENT], [h, g, wgu, wd.reshape(nj, FS, D)]
    out_specs, out_shape = [rowd], [jax.ShapeDtypeStruct((t, D), f32)]
    if loss_head is not None:
        in_specs, args = in_specs + [vec, rowd], args + [loss_head[0], loss_head[1]]
        out_specs += [pl.BlockSpec((8, 128), lambda i, j: (0, 0)), vec]
        out_shape += [jax.ShapeDtypeStruct((8, 128), f32), jax.ShapeDtypeStruct((1, D), f32)]
    return pl.pallas_call(
        body, name=name, grid=(t // tm, nj),
        in_specs=in_specs,
        out_specs=out_specs + [rowf, rowf, rowd],
        out_shape=out_shape + [jax.ShapeDtypeStruct((t, F), bf16), jax.ShapeDtypeStruct((t, F), bf16),
                               jax.ShapeDtypeStruct((t, D), bf16)],
        scratch_shapes=[pltpu.VMEM((tm, D), bf16), pltpu.VMEM((tm, D), f32), pltpu.VMEM((tm, FS), bf16)],
        compiler_params=_cp(2),
    )(*args)


def _inproj_fwd(h, g, win, b_in):
    t = h.shape[0]
    tn = NIN // NCHIP
    nj = NIN // tn

    def body(h_ref, g_ref, w_ref, b_ref, proj_ref, n_ref, nb_sc):
        j = pl.program_id(1)

        @pl.when(j == 0)
        def _():
            hh = h_ref[...]
            nb = (hh * _rms_stats(hh) * g_ref[...]).astype(bf16)
            nb_sc[...] = nb
            n_ref[...] = nb

        proj_ref[...] = jnp.dot(nb_sc[...], w_ref[j], preferred_element_type=f32) + b_ref[...]

    return pl.pallas_call(
        body, name="inproj_fwd", grid=(t // TM, nj),
        in_specs=[
            pl.BlockSpec((TM, D), lambda i, j: (i, 0)),
            pl.BlockSpec((1, D), lambda i, j: (0, 0)),
            RESIDENT,
            pl.BlockSpec((1, tn), lambda i, j: (0, j)),
        ],
        out_specs=[
            pl.BlockSpec((TM, tn), lambda i, j: (i, j)),
            pl.BlockSpec((TM, D), lambda i, j: (i, 0)),
        ],
        out_shape=[jax.ShapeDtypeStruct((t, NIN), f32), jax.ShapeDtypeStruct((t, D), bf16)],
        scratch_shapes=[pltpu.VMEM((TM, D), bf16)],
        compiler_params=_cp(2),
    )(h, g, win, b_in)


def _block_gates(xr, wa_ref, ba, wx_ref, bx, lam):
    xrb = xr.astype(bf16)
    pa = jnp.concatenate([jnp.dot(xrb[:, hh * HD:(hh + 1) * HD], wa_ref[hh], preferred_element_type=f32)
                          for hh in range(NHEAD)], axis=1)
    px = jnp.concatenate([jnp.dot(xrb[:, hh * HD:(hh + 1) * HD], wx_ref[hh], preferred_element_type=f32)
                          for hh in range(NHEAD)], axis=1)
    ra = _sigmoid(pa + ba)
    ii = _sigmoid(px + bx)
    sp = _softplus(-lam)
    log_a = -8.0 * ra * sp
    a = jnp.exp(log_a)
    sq = jnp.sqrt(_one_minus_square(a, log_a))
    return ra, ii, a, sq, sp


def _rnn_fwd(proj, cw, cb, wa, ba, wx, bx, lam):
    t = proj.shape[0]
    ng = TM // 8

    def body(x_ref, y_ref, cw_ref, cb_ref, wa_ref, ba_ref, wx_ref, bx_ref, lam_ref,
             xr_ref, hr_ref, z_ref, gates_ref, xext_sc, carry_sc, a_sc, h_sc):
        i = pl.program_id(0)

        @pl.when(i == 0)
        def _():
            xext_sc[0:8, :] = jnp.zeros((8, D), f32)
            carry_sc[...] = jnp.zeros_like(carry_sc)

        x = x_ref[...]
        xext_sc[8:8 + TM, :] = x
        xe = xext_sc[...]
        xr = cb_ref[...] + cw_ref[KC4 - 1:KC4, :] * x
        for k in range(KC4 - 1):
            xr = xr + cw_ref[k:k + 1, :] * pltpu.roll(xe, KC4 - 1 - k, 0)[8:8 + TM]
        xext_sc[0:8, :] = x[TM - 8:TM]

        ra, ii, a, sq, _ = _block_gates(xr, wa_ref, ba_ref[...], wx_ref, bx_ref[...], lam_ref[...])
        for slot, val in enumerate((ra, ii, a, sq)):
            gates_ref[slot] = val
        a_sc[...] = a
        h_sc[...] = sq * ii * xr
        row = _row_ids((8, D))

        def group(r, carry):
            off = pl.multiple_of(r * 8, 8)
            aa = a_sc[pl.ds(off, 8), :]
            hh = h_sc[pl.ds(off, 8), :]
            for s in (1, 2, 4):
                a_sh = jnp.where(row >= s, pltpu.roll(aa, s, 0), 1.0)
                h_sh = jnp.where(row >= s, pltpu.roll(hh, s, 0), 0.0)
                hh = aa * h_sh + hh
                aa = aa * a_sh
            hh = hh + aa * carry
            h_sc[pl.ds(off, 8), :] = hh
            return hh[7:8, :]

        carry_sc[...] = lax.fori_loop(0, ng, group, carry_sc[...])
        hr = h_sc[...]
        gel, _ = _gelu_and_grad(y_ref[...])
        xr_ref[...] = xr
        hr_ref[...] = hr
        z_ref[...] = (hr * gel).astype(bf16)

    vec = pl.BlockSpec((1, D), lambda i: (0, 0))
    return pl.pallas_call(
        body, name="rnn_fwd", grid=(t // TM,),
        in_specs=[
            pl.BlockSpec((TM, D), lambda i: (i, 0)),
            pl.BlockSpec((TM, D), lambda i: (i, 1)),
            pl.BlockSpec((KC4, D), lambda i: (0, 0)),
            vec,
            pl.BlockSpec((NHEAD, HD, HD), lambda i: (0, 0, 0)),
            vec,
            pl.BlockSpec((NHEAD, HD, HD), lambda i: (0, 0, 0)),
            vec, vec,
        ],
        out_specs=[pl.BlockSpec((TM, D), lambda i: (i, 0))] * 3 + [pl.BlockSpec((4, TM, D), lambda i: (0, i, 0))],
        out_shape=[jax.ShapeDtypeStruct((t, D), f32), jax.ShapeDtypeStruct((t, D), f32),
                   jax.ShapeDtypeStruct((t, D), bf16), jax.ShapeDtypeStruct((4, t, D), f32)],
        scratch_shapes=[pltpu.VMEM((TM + 8, D), f32), pltpu.VMEM((1, D), f32),
                        pltpu.VMEM((TM, D), f32), pltpu.VMEM((TM, D), f32)],
        compiler_params=_cp(1),
    )(proj, proj, cw, cb, wa, ba, wx, bx, lam)


def _ln_stats(vc):
    mu = jnp.mean(vc, axis=-1, keepdims=True)
    xc = vc - mu
    rstd = lax.rsqrt(jnp.mean(xc * xc, axis=-1, keepdims=True) + EPS)
    return xc * rstd, rstd


def _conv_fwd(proj, w31, b31, ln_g, ln_b, after=None):
    t = proj.shape[0]

    def body(gv_ref, gg_ref, w_ref, b_ref, lg_ref, lb_ref, vc_ref, s_ref, vext_sc):
        i = pl.program_id(0)

        @pl.when(i == 0)
        def _():
            vext_sc[0:HALO, :] = jnp.zeros((HALO, D), f32)

        v = gv_ref[...] * _sigmoid(gg_ref[...])
        vext_sc[HALO:HALO + TM, :] = v
        ve = vext_sc[...]
        acc = jnp.zeros((TM, D), f32) + b_ref[...]
        for s in range(8):
            vs = ve if s == 0 else pltpu.roll(ve, s, 0)
            for m in range(HALO // 8):
                k = KC31 - 1 - (8 * m + s)
                if 0 <= k < KC31:
                    acc = acc + w_ref[k:k + 1, :] * vs[HALO - 8 * m:HALO - 8 * m + TM]
        vext_sc[0:HALO, :] = v[TM - HALO:TM]
        xhat, _ = _ln_stats(acc)
        ln = xhat * lg_ref[...] + lb_ref[...]
        vc_ref[...] = acc
        s_ref[...] = (ln * _sigmoid(ln)).astype(bf16)

    vec = pl.BlockSpec((1, D), lambda i: (0, 0))
    body, in_specs, args = _ordered(
        body,
        [pl.BlockSpec((TM, D), lambda i: (i, 2)),
         pl.BlockSpec((TM, D), lambda i: (i, 3)),
         pl.BlockSpec((KC31, D), lambda i: (0, 0)),
         vec, vec, vec],
        (proj, proj, w31, b31, ln_g, ln_b), after)
    return pl.pallas_call(
        body, name="conv_fwd", grid=(t // TM,),
        in_specs=in_specs,
        out_specs=[pl.BlockSpec((TM, D), lambda i: (i, 0))] * 2,
        out_shape=[jax.ShapeDtypeStruct((t, D), f32), jax.ShapeDtypeStruct((t, D), bf16)],
        scratch_shapes=[pltpu.VMEM((TM + HALO, D), f32)],
        compiler_params=_cp(1),
    )(*args)


def _merge_fwd(h, z, s, proj, wrp, wcp, bcp, wout):
    t = h.shape[0]

    def body(h_ref, z_ref, s_ref, ga_ref, gb_ref, wrp_ref, wcp_ref, bcp_ref, wout_ref, ho_ref):
        ya = jnp.dot(z_ref[...], wrp_ref[...], preferred_element_type=f32)
        yb = jnp.dot(s_ref[...], wcp_ref[...], preferred_element_type=f32) + bcp_ref[...]
        merged = _sigmoid(ga_ref[...]) * ya + _sigmoid(gb_ref[...]) * yb
        ho_ref[...] = h_ref[...] + jnp.dot(merged.astype(bf16), wout_ref[...], preferred_element_type=f32)

    row = pl.BlockSpec((TM, D), lambda i: (i, 0))
    wsq = pl.BlockSpec((D, D), lambda i: (0, 0))
    return pl.pallas_call(
        body, name="merge_fwd", grid=(t // TM,),
        in_specs=[row, row, row,
                  pl.BlockSpec((TM, D), lambda i: (i, 4)),
                  pl.BlockSpec((TM, D), lambda i: (i, 5)),
                  wsq, wsq, pl.BlockSpec((1, D), lambda i: (0, 0)), wsq],
        out_specs=row,
        out_shape=jax.ShapeDtypeStruct((t, D), f32),
        compiler_params=_cp(1),
    )(h, z, s, proj, proj, wrp, wcp, bcp, wout)


def _ffn_bwd(dh, h, g, gate, up, wgu, wd, name, after=None):
    t = h.shape[0]
    nj = 2

    def body(dh_ref, h_ref, g_ref, gate_ref, up_ref, wg_ref, wd_ref,
             dhi_ref, dgate_ref, dup_ref, a_ref, df_ref, dg_ref, dfb_sc, dn_sc):
        i = pl.program_id(0)
        j = pl.program_id(1)

        @pl.when(jnp.logical_and(i == 0, j == 0))
        def _():
            dg_ref[...] = jnp.zeros_like(dg_ref)

        @pl.when(j == 0)
        def _():
            dfb = (0.5 * dh_ref[...]).astype(bf16)
            dfb_sc[...] = dfb
            df_ref[...] = dfb
            dn_sc[...] = jnp.zeros_like(dn_sc)

        dfb = dfb_sc[...]
        for off, width in FSUB:
            cols = slice(off, off + width)
            da = _nt_dot(dfb, wd_ref[j, cols, :])
            gt = gate_ref[:, cols].astype(f32)
            uu = up_ref[:, cols].astype(f32)
            sg = _sigmoid(gt)
            silu = gt * sg
            a_ref[:, cols] = (silu * uu).astype(bf16)
            dgate_ref[:, cols] = (da * uu * (sg * (1.0 + gt * (1.0 - sg)))).astype(bf16)
            dup_ref[:, cols] = (da * silu).astype(bf16)
        dn_sc[...] += _nt_dot(dgate_ref[...], wg_ref[j]) + _nt_dot(dup_ref[...], wg_ref[2 + j])

        @pl.when(j == nj - 1)
        def _():
            dhin, dg = _rms_bwd(dn_sc[...], h_ref[...], g_ref[...])
            dhi_ref[...] = dh_ref[...] + dhin
            dg_ref[...] += dg

    rowd = pl.BlockSpec((TM, D), lambda i, j: (i, 0))
    rowf = pl.BlockSpec((TM, FS), lambda i, j: (i, j))
    vec = pl.BlockSpec((1, D), lambda i, j: (0, 0))
    body, in_specs, args = _ordered(
        body,
        [rowd, rowd, vec, rowf, rowf,
         RESIDENT, RESIDENT],
        (dh, h, g, gate, up, wgu, wd.reshape(nj, FS, D)), after)
    return pl.pallas_call(
        body, name=name, grid=(t // TM, nj),
        in_specs=in_specs,
        out_specs=[rowd, rowf, rowf, rowf, rowd, vec],
        out_shape=[jax.ShapeDtypeStruct((t, D), f32), jax.ShapeDtypeStruct((t, F), bf16),
                   jax.ShapeDtypeStruct((t, F), bf16), jax.ShapeDtypeStruct((t, F), bf16),
                   jax.ShapeDtypeStruct((t, D), bf16), jax.ShapeDtypeStruct((1, D), f32)],
        scratch_shapes=[pltpu.VMEM((TM, D), bf16), pltpu.VMEM((TM, D), f32)],
        compiler_params=_cp(2),
    )(*args)


def _big_tile(t):
    return max(k * TM for k in range(1, 6) if t % (k * TM) == 0)


ANY_SPEC = pl.BlockSpec(memory_space=pl.ANY)


def _tn_matmul(a, b, tk, tn, out_shape, out_block, out_map, name, base=None, after=None):
    t, kk = a.shape
    _, nn = b.shape
    tmm = _big_tile(t)
    nm = t // tmm

    def body(a_ref, b_ref, o_ref, acc_sc):
        m = pl.program_id(2)

        @pl.when(m == 0)
        def _():
            acc_sc[...] = jnp.zeros_like(acc_sc)

        acc_sc[...] += _tn_dot(a_ref[...], b_ref[...])

        @pl.when(m == nm - 1)
        def _():
            o_ref[...] = acc_sc[...].astype(o_ref.dtype)

    in_specs = [pl.BlockSpec((tmm, tk), lambda k, n, m: (m, k)),
                pl.BlockSpec((tmm, tn), lambda k, n, m: (m, n))]
    args, aliases = (a, b), {}
    if base is not None:
        body = (lambda inner: lambda a_ref, b_ref, base_ref, o_ref, acc_sc: inner(a_ref, b_ref, o_ref, acc_sc))(body)
        in_specs, args, aliases = in_specs + [ANY_SPEC], (a, b, base), {2: 0}
    if after is not None:
        body, in_specs, args = _ordered(body, in_specs, args, after)
        aliases = {k + _n_after(after): v for k, v in aliases.items()}
    return pl.pallas_call(
        body, name=name, grid=(kk // tk, nn // tn, nm),
        in_specs=in_specs,
        out_specs=pl.BlockSpec(out_block, out_map),
        out_shape=jax.ShapeDtypeStruct(out_shape, bf16),
        scratch_shapes=[pltpu.VMEM((tk, tn), f32)],
        input_output_aliases=aliases,
        compiler_params=_cp(3),
    )(*args)


def _merge_bwd(dh, z, s, proj, wrp, wcp, bcp, wout, after=None):
    t = dh.shape[0]

    def body(dh_ref, z_ref, s_ref, ga_ref, gb_ref, wrp_ref, wcp_ref, bcp_ref, wout_ref,
             dz_ref, ds_ref, dgab_ref, dhb_ref, mg_ref, dya_ref, dyb_ref, dbcp_ref):
        i = pl.program_id(0)

        @pl.when(i == 0)
        def _():
            dbcp_ref[...] = jnp.zeros_like(dbcp_ref)

        dhb = dh_ref[...].astype(bf16)
        dhb_ref[...] = dhb
        dmg = _nt_dot(dhb, wout_ref[...])
        ya = jnp.dot(z_ref[...], wrp_ref[...], preferred_element_type=f32)
        yb = jnp.dot(s_ref[...], wcp_ref[...], preferred_element_type=f32) + bcp_ref[...]
        sa = _sigmoid(ga_ref[...])
        sb = _sigmoid(gb_ref[...])
        mg_ref[...] = (sa * ya + sb * yb).astype(bf16)
        dgab_ref[:, 0:D] = (dmg * ya * sa * (1.0 - sa)).astype(bf16)
        dgab_ref[:, D:2 * D] = (dmg * yb * sb * (1.0 - sb)).astype(bf16)
        dya = dmg * sa
        dyb = dmg * sb
        dbcp_ref[...] += jnp.sum(dyb, axis=0, keepdims=True)
        dyab = dya.astype(bf16)
        dybb = dyb.astype(bf16)
        dya_ref[...] = dyab
        dyb_ref[...] = dybb
        dz_ref[...] = _nt_dot(dyab, wrp_ref[...])
        ds_ref[...] = _nt_dot(dybb, wcp_ref[...])

    row = pl.BlockSpec((TM, D), lambda i: (i, 0))
    wsq = pl.BlockSpec((D, D), lambda i: (0, 0))
    vec = pl.BlockSpec((1, D), lambda i: (0, 0))
    rowb = jax.ShapeDtypeStruct((t, D), bf16)
    body, in_specs, args = _ordered(
        body,
        [row, row, row,
         pl.BlockSpec((TM, D), lambda i: (i, 4)),
         pl.BlockSpec((TM, D), lambda i: (i, 5)),
         wsq, wsq, vec, wsq],
        (dh, z, s, proj, proj, wrp, wcp, bcp, wout), after)
    return pl.pallas_call(
        body, name="merge_bwd", grid=(t // TM,),
        in_specs=in_specs,
        out_specs=[row, row,
                   pl.BlockSpec((TM, 2 * D), lambda i: (i, 2)),
                   row, row, row, row, vec],
        out_shape=[jax.ShapeDtypeStruct((t, D), f32), jax.ShapeDtypeStruct((t, D), f32),
                   jax.ShapeDtypeStruct((t, NIN), bf16),
                   rowb, rowb, rowb, rowb, jax.ShapeDtypeStruct((1, D), f32)],
        compiler_params=_cp(1),
    )(*args)


def _conv_bwd(ds, vc, proj, dproj, w31, ln_g, ln_b, after=None):
    t = ds.shape[0]
    nt = t // TM
    hb = TM // HALO

    rb = 16
    nb = TM // rb
    taps = [(KC31 - 1 - (8 * m + s), s, m) for s in range(8) for m in range(HALO // 8)
            if 0 <= KC31 - 1 - (8 * m + s) < KC31]

    def groups(a):
        return jnp.sum(a.reshape(rb // 8, 8, D), axis=0)

    def body(ds_ref, vc_ref, gv_ref, gg_ref, gvp_ref, ggp_ref, dpin_ref, w_ref, lg_ref, lb_ref,
             dgvg_ref, dw_ref, db_ref, dlg_ref, dlb_ref, dext_sc, vext_sc, rot_sc, dwacc_sc, small_sc, wb_sc):
        del dpin_ref
        i = pl.program_id(0)
        tile = nt - 1 - i

        @pl.when(i == 0)
        def _():
            dext_sc[TM:TM + HALO, :] = jnp.zeros((HALO, D), f32)
            dwacc_sc[...] = jnp.zeros_like(dwacc_sc)
            small_sc[...] = jnp.zeros_like(small_sc)

        lg = lg_ref[...]
        lb = lb_ref[...]

        xhat, rstd = _ln_stats(vc_ref[...])
        ln = xhat * lg + lb
        sg = _sigmoid(ln)
        dln = ds_ref[...] * (sg * (1.0 + ln * (1.0 - sg)))
        dxh = dln * lg
        dvc = rstd * (dxh - jnp.mean(dxh, axis=-1, keepdims=True)
                      - xhat * jnp.mean(dxh * xhat, axis=-1, keepdims=True))
        small_sc[0] += jnp.sum((dln * xhat).reshape(TM // 8, 8, D), axis=0)
        small_sc[1] += jnp.sum(dln.reshape(TM // 8, 8, D), axis=0)
        small_sc[2] += jnp.sum(dvc.reshape(TM // 8, 8, D), axis=0)
        dext_sc[0:TM, :] = dvc
        vext_sc[HALO:HALO + TM, :] = gv_ref[...] * _sigmoid(gg_ref[...])
        vext_sc[0:HALO, :] = jnp.where(tile > 0, gvp_ref[...] * _sigmoid(ggp_ref[...]), 0.0)

        @pl.when(i == 0)
        def _():
            for k in range(KC31):
                wb_sc[k] = jnp.broadcast_to(w_ref[k:k + 1, :], (8, D))

        for s in range(1, 8):
            rot_sc[s - 1] = pltpu.roll(dext_sc[...], TM + HALO - s, 0)

        def dv_block(b, carry):
            rows = pl.ds(pl.multiple_of(b * rb, rb), rb)
            acc = jnp.zeros((rb, D), f32)
            for k, s, m in taps:
                src = pl.ds(pl.multiple_of(b * rb + 8 * m, 8), rb)
                slab = dext_sc[src, :] if s == 0 else rot_sc[s - 1, src, :]
                acc = acc + (slab.reshape(rb // 8, 8, D) * wb_sc[k]).reshape(rb, D)
            sgg = _sigmoid(gg_ref[rows, :])
            dgvg_ref[rows, 0:D] = (acc * sgg).astype(bf16)
            dgvg_ref[rows, D:2 * D] = (acc * gv_ref[rows, :] * sgg * (1.0 - sgg)).astype(bf16)
            return carry

        lax.fori_loop(0, nb, dv_block, 0)

        for s in range(1, 8):
            rot_sc[s - 1] = pltpu.roll(vext_sc[...], s, 0)
        for first in range(0, len(taps), 3):
            trio = taps[first:first + 3]

            def dw_block(b, accs, trio=trio):
                rows = pl.ds(pl.multiple_of(b * rb, rb), rb)
                dvc_blk = dext_sc[rows, :]
                out = []
                for acc, (k, s, m) in zip(accs, trio):
                    src = pl.ds(pl.multiple_of(b * rb + HALO - 8 * m, 8), rb)
                    slab = vext_sc[src, :] if s == 0 else rot_sc[s - 1, src, :]
                    out.append(acc + groups(dvc_blk * slab))
                return tuple(out)

            sums = lax.fori_loop(0, nb, dw_block, tuple(jnp.zeros((8, D), f32) for _ in trio))
            for acc, (k, s, m) in zip(sums, trio):
                dwacc_sc[k] += acc
        dext_sc[TM:TM + HALO, :] = dext_sc[0:HALO, :]

        @pl.when(i == nt - 1)
        def _():
            for k in range(KC31):
                dw_ref[k:k + 1, :] = jnp.sum(dwacc_sc[k], axis=0, keepdims=True)
            dlg_ref[...] = jnp.sum(small_sc[0], axis=0, keepdims=True)
            dlb_ref[...] = jnp.sum(small_sc[1], axis=0, keepdims=True)
            db_ref[...] = jnp.sum(small_sc[2], axis=0, keepdims=True)

    rev = lambda i: (nt - 1 - i, 0)
    vec = pl.BlockSpec((1, D), lambda i: (0, 0))
    halo_row = lambda i: jnp.maximum((nt - 1 - i) * hb - 1, 0)
    body, in_specs, args = _ordered(
        body,
        [pl.BlockSpec((TM, D), rev),
         pl.BlockSpec((TM, D), rev),
         pl.BlockSpec((TM, D), lambda i: (nt - 1 - i, 2)),
         pl.BlockSpec((TM, D), lambda i: (nt - 1 - i, 3)),
         pl.BlockSpec((HALO, D), lambda i: (halo_row(i), 2)),
         pl.BlockSpec((HALO, D), lambda i: (halo_row(i), 3)),
         pl.BlockSpec(memory_space=pl.ANY),
         pl.BlockSpec((KC31, D), lambda i: (0, 0)),
         vec, vec],
        (ds, vc, proj, proj, proj, proj, dproj, w31, ln_g, ln_b), after)
    return pl.pallas_call(
        body, name="conv_bwd", grid=(nt,),
        in_specs=in_specs,
        out_specs=[
            pl.BlockSpec((TM, 2 * D), lambda i: (nt - 1 - i, 1)),
            pl.BlockSpec((KC31, D), lambda i: (0, 0)),
            vec, vec, vec,
        ],
        out_shape=[jax.ShapeDtypeStruct((t, NIN), bf16),
                   jax.ShapeDtypeStruct((KC31, D), f32),
                   jax.ShapeDtypeStruct((1, D), f32), jax.ShapeDtypeStruct((1, D), f32),
                   jax.ShapeDtypeStruct((1, D), f32)],
        scratch_shapes=[pltpu.VMEM((TM + HALO, D), f32), pltpu.VMEM((TM + HALO, D), f32),
                        pltpu.VMEM((7, TM + HALO, D), f32), pltpu.VMEM((KC31, 8, D), f32),
                        pltpu.VMEM((3, 8, D), f32), pltpu.VMEM((KC31, 8, D), f32)],
        input_output_aliases={6 + _n_after(after): 0},
        compiler_params=_cp(1),
    )(*args)


def _rnn_bwd(dz, xr, hr, gates, proj, dproj, cw, wa, wx, lam):
    t = dz.shape[0]
    nt = t // TM
    ng = TM // 8
    hq = HD // NCHIP

    def body(dz_ref, xr_ref, hr_ref, hrp_ref, x_ref, xp_ref, y_ref, dpin_ref,
             cw_ref, wa_ref, gates_ref, wx_ref, lam_ref,
             dxy_ref, dwa_ref, dwx_ref, dcw_ref, dcb_ref, dba_ref, dbx_ref, dlam_ref,
             anext_sc, gcarry_sc, dext_sc, xext_sc, m_sc, g_sc, dwa_sc, dwx_sc, dsp_sc):
        del dpin_ref
        i = pl.program_id(0)
        tile = nt - 1 - i

        @pl.when(i == 0)
        def _():
            anext_sc[...] = jnp.zeros_like(anext_sc)
            gcarry_sc[...] = jnp.zeros_like(gcarry_sc)
            dext_sc[TM:TM + 8, :] = jnp.zeros((8, D), f32)
            dwa_sc[...] = jnp.zeros_like(dwa_sc)
            dwx_sc[...] = jnp.zeros_like(dwx_sc)
            dsp_sc[...] = jnp.zeros_like(dsp_sc)
            dcw_ref[...] = jnp.zeros_like(dcw_ref)
            dcb_ref[...] = jnp.zeros_like(dcb_ref)
            dba_ref[...] = jnp.zeros_like(dba_ref)
            dbx_ref[...] = jnp.zeros_like(dbx_ref)

        xr = xr_ref[...]
        hr = hr_ref[...]
        dz = dz_ref[...]
        gel, dgel = _gelu_and_grad(y_ref[...])
        dxy_ref[:, D:2 * D] = (dz * hr * dgel).astype(bf16)
        ra, ii, a, sq = gates_ref[0], gates_ref[1], gates_ref[2], gates_ref[3]
        sp = _softplus(-lam_ref[...])

        row = _row_ids((TM, D))
        m_sc[...] = jnp.where(row == TM - 1, anext_sc[...], pltpu.roll(a, TM - 1, 0))
        anext_sc[...] = a[0:1, :]
        g_sc[...] = dz * gel
        row8 = _row_ids((8, D))

        def group(qq, carry):
            off = pl.multiple_of((ng - 1 - qq) * 8, 8)
            mm = m_sc[pl.ds(off, 8), :]
            dd = g_sc[pl.ds(off, 8), :]
            for s in (1, 2, 4):
                m_sh = jnp.where(row8 < 8 - s, pltpu.roll(mm, 8 - s, 0), 1.0)
                d_sh = jnp.where(row8 < 8 - s, pltpu.roll(dd, 8 - s, 0), 0.0)
                dd = dd + mm * d_sh
                mm = mm * m_sh
            dd = dd + mm * carry
            g_sc[pl.ds(off, 8), :] = dd
            return dd[0:1, :]

        gcarry_sc[...] = lax.fori_loop(0, ng, group, gcarry_sc[...])
        gg = g_sc[...]

        hlast = jnp.where(tile > 0, hrp_ref[7:8, :], 0.0)
        hprev = jnp.where(row == 0, hlast, pltpu.roll(hr, 1, 0))
        d_a = gg * hprev
        dsq = gg * ii * xr
        dii = gg * sq * xr
        dxr = gg * sq * ii
        dlog = d_a * a - dsq * (a * a / sq)
        dsp_sc[...] += jnp.sum(dlog * (-8.0 * ra), axis=0, keepdims=True)
        dpa = dlog * (-8.0 * sp) * ra * (1.0 - ra)
        dpx = dii * ii * (1.0 - ii)
        dba_ref[...] += jnp.sum(dpa, axis=0, keepdims=True)
        dbx_ref[...] += jnp.sum(dpx, axis=0, keepdims=True)
        dpab = dpa.astype(bf16)
        dpxb = dpx.astype(bf16)
        xrb = xr.astype(bf16)
        back = []
        for hh in range(NHEAD):
            cols = slice(hh * HD, (hh + 1) * HD)
            back.append(_nt_dot(dpab[:, cols], wa_ref[hh]) + _nt_dot(dpxb[:, cols], wx_ref[hh]))
            dwa_sc[hh] += _tn_dot(xrb[:, cols], dpab[:, cols])
            dwx_sc[hh] += _tn_dot(xrb[:, cols], dpxb[:, cols])
        dxr = dxr + jnp.concatenate(back, axis=1)

        dext_sc[0:TM, :] = dxr
        de = dext_sc[...]
        dx = cw_ref[KC4 - 1:KC4, :] * dxr
        for k in range(KC4 - 1):
            dx = dx + cw_ref[k:k + 1, :] * pltpu.roll(de, TM + 8 - (KC4 - 1 - k), 0)[0:TM]
        dext_sc[TM:TM + 8, :] = dxr[0:8]
        dxy_ref[:, 0:D] = dx.astype(bf16)

        x = x_ref[...]
        xext_sc[0:8, :] = jnp.where(tile > 0, xp_ref[...], 0.0)
        xext_sc[8:8 + TM, :] = x
        xe = xext_sc[...]
        dcw_ref[KC4 - 1:KC4, :] += jnp.sum(dxr * x, axis=0, keepdims=True)
        for k in range(KC4 - 1):
            xs = pltpu.roll(xe, KC4 - 1 - k, 0)[8:8 + TM]
            dcw_ref[k:k + 1, :] += jnp.sum(dxr * xs, axis=0, keepdims=True)
        dcb_ref[...] += jnp.sum(dxr, axis=0, keepdims=True)

        @pl.when(i == nt - 1)
        def _():
            for hh in range(NHEAD):
                for qc in range(NCHIP):
                    dwa_ref[qc, hh] = dwa_sc[hh, qc * hq:(qc + 1) * hq, :].astype(bf16)
                    dwx_ref[qc, hh] = dwx_sc[hh, qc * hq:(qc + 1) * hq, :].astype(bf16)
            dlam_ref[...] = -dsp_sc[...] * _sigmoid(-lam_ref[...])

    rev = lambda i: (nt - 1 - i, 0)
    vec = pl.BlockSpec((1, D), lambda i: (0, 0))
    prev8 = lambda i: jnp.maximum((nt - 1 - i) * ng - 1, 0)
    wblk = pl.BlockSpec((NHEAD, HD, HD), lambda i: (0, 0, 0))
    gblk = pl.BlockSpec((NCHIP, NHEAD, hq, HD), lambda i: (0, 0, 0, 0))
    return pl.pallas_call(
        body, name="rnn_bwd", grid=(nt,),
        in_specs=[
            pl.BlockSpec((TM, D), rev),
            pl.BlockSpec((TM, D), rev),
            pl.BlockSpec((TM, D), rev),
            pl.BlockSpec((8, D), lambda i: (prev8(i), 0)),
            pl.BlockSpec((TM, D), lambda i: (nt - 1 - i, 0)),
            pl.BlockSpec((8, D), lambda i: (prev8(i), 0)),
            pl.BlockSpec((TM, D), lambda i: (nt - 1 - i, 1)),
            pl.BlockSpec(memory_space=pl.ANY),
            pl.BlockSpec((KC4, D), lambda i: (0, 0)),
            wblk, pl.BlockSpec((4, TM, D), lambda i: (0, nt - 1 - i, 0)), wblk, vec,
        ],
        out_specs=[
            pl.BlockSpec((TM, 2 * D), lambda i: (nt - 1 - i, 0)),
            gblk, gblk,
            pl.BlockSpec((KC4, D), lambda i: (0, 0)),
            vec, vec, vec, vec,
        ],
        out_shape=[jax.ShapeDtypeStruct((t, NIN), bf16),
                   jax.ShapeDtypeStruct((NCHIP, NHEAD, hq, HD), bf16),
                   jax.ShapeDtypeStruct((NCHIP, NHEAD, hq, HD), bf16),
                   jax.ShapeDtypeStruct((KC4, D), f32),
                   jax.ShapeDtypeStruct((1, D), f32), jax.ShapeDtypeStruct((1, D), f32),
                   jax.ShapeDtypeStruct((1, D), f32), jax.ShapeDtypeStruct((1, D), f32)],
        scratch_shapes=[pltpu.VMEM((1, D), f32), pltpu.VMEM((1, D), f32),
                        pltpu.VMEM((TM + 8, D), f32), pltpu.VMEM((TM + 8, D), f32),
                        pltpu.VMEM((TM, D), f32), pltpu.VMEM((TM, D), f32),
                        pltpu.VMEM((NHEAD, HD, HD), f32), pltpu.VMEM((NHEAD, HD, HD), f32),
                        pltpu.VMEM((1, D), f32)],
        input_output_aliases={7: 0},
        compiler_params=_cp(1),
    )(dz, xr, hr, hr, proj, proj, proj, dproj, cw, wa, gates, wx, lam)


def _inproj_bwd(dproj, dh, h, g, win, after=None):
    t = h.shape[0]
    tn = NIN // NCHIP
    nj = NIN // tn

    def body(dp_ref, dh_ref, h_ref, g_ref, w_ref, dhi_ref, dg_ref, db_ref, dn_sc):
        i = pl.program_id(0)
        j = pl.program_id(1)

        @pl.when(jnp.logical_and(i == 0, j == 0))
        def _():
            dg_ref[...] = jnp.zeros_like(dg_ref)
            db_ref[...] = jnp.zeros_like(db_ref)

        @pl.when(j == 0)
        def _():
            dn_sc[...] = jnp.zeros_like(dn_sc)

        dp = dp_ref[...]
        dn_sc[...] += _nt_dot(dp, w_ref[j])
        db_ref[j] += jnp.sum(dp.astype(f32), axis=0, keepdims=True)

        @pl.when(j == nj - 1)
        def _():
            dhin, dg = _rms_bwd(dn_sc[...], h_ref[...], g_ref[...])
            dhi_ref[...] = dh_ref[...] + dhin
            dg_ref[...] += dg

    rowd = pl.BlockSpec((TM, D), lambda i, j: (i, 0))
    vec = pl.BlockSpec((1, D), lambda i, j: (0, 0))
    body, in_specs, args = _ordered(
        body,
        [pl.BlockSpec((TM, tn), lambda i, j: (i, j)), rowd, rowd, vec,
         RESIDENT],
        (dproj, dh, h, g, win), after)
    return pl.pallas_call(
        body, name="inproj_bwd", grid=(t // TM, nj),
        in_specs=in_specs,
        out_specs=[rowd, vec, pl.BlockSpec((nj, 1, tn), lambda i, j: (0, 0, 0))],
        out_shape=[jax.ShapeDtypeStruct((t, D), f32), jax.ShapeDtypeStruct((1, D), f32),
                   jax.ShapeDtypeStruct((nj, 1, tn), f32)],
        scratch_shapes=[pltpu.VMEM((TM, D), f32)],
        compiler_params=_cp(2),
    )(*args)


def _ffn_gu_grad(n, dgate, dup, tag, after=None):
    half = _tn_matmul(n, dgate, D, FS, (NCHIP, D, FS), (None, D, FS), lambda k, nn, m: (nn, 0, 0),
                      tag + "_dwg", after=after)
    return _tn_matmul(n, dup, D, FS, (NCHIP, D, FS), (None, D, FS), lambda k, nn, m: (2 + nn, 0, 0),
                      tag + "_dwu", base=half)


def _ffn_down_grad(a, df, tag, after=None):
    return _tn_matmul(a, df, FS, D, (F, D), (FS, D), lambda k, nn, m: (k, 0), tag + "_dwd", after=after)


def _square_grad(a, b, name):
    return _tn_matmul(a, b, D, D, (D, D), (D, D), lambda k, nn, m: (0, 0), name)


ANY = pl.BlockSpec(memory_space=pl.ANY)


def _place():
    x, y, c = lax.axis_index("x"), lax.axis_index("y"), lax.axis_index("c")
    chips = [(1 - x, y), (x, 1 - y), (1 - x, 1 - y)]
    return x, y, c, chips


def _chip_id(chip):
    return 2 * chip[0] + chip[1]


def _cast_into_slot(w2d, qc, dtype, name, after=None):
    r, cc = w2d.shape
    hr = r // 2

    def body(qc_ref, *refs):
        del qc_ref
        w_ref, o_ref = refs[-2:]
        o_ref[...] = w_ref[...].astype(dtype)

    in_specs, args = [pl.BlockSpec((hr, cc), lambda h, qc_ref: (h, 0))], (w2d,)
    if after is not None:
        in_specs, args = [ANY_SPEC] + in_specs, (after,) + args
    return pl.pallas_call(
        body, name=name,
        grid_spec=pltpu.PrefetchScalarGridSpec(
            num_scalar_prefetch=1, grid=(2,),
            in_specs=in_specs,
            out_specs=pl.BlockSpec((None, None, hr, cc), lambda h, qc_ref: (qc_ref[0], h, 0, 0))),
        out_shape=jax.ShapeDtypeStruct((NCHIP, 2, hr, cc), dtype),
        compiler_params=_cp(1),
    )(qc, *args)


def _place_pack(pack, qc):
    def body(qc_ref, p_ref, o_ref):
        del qc_ref
        o_ref[...] = p_ref[...]

    return pl.pallas_call(
        body, name="place_pack",
        grid_spec=pltpu.PrefetchScalarGridSpec(
            num_scalar_prefetch=1, grid=(1,),
            in_specs=[pl.BlockSpec(pack.shape, lambda i, qc_ref: (0, 0))],
            out_specs=pl.BlockSpec((None,) + pack.shape, lambda i, qc_ref: (2 * qc_ref[0] + qc_ref[1], 0, 0))),
        out_shape=jax.ShapeDtypeStruct((8,) + pack.shape, pack.dtype),
        compiler_params=_cp(1),
    )(qc, pack)


def _pair_add(parts, gots, qc, name):
    n = len(parts)

    def body(qc_ref, *refs):
        s = pl.program_id(0)
        for a in range(n):
            val = (refs[a][...].astype(f32) + refs[n + a][...].astype(f32)).astype(bf16)
            refs[2 * n + a][...] = val

            @pl.when(s == qc_ref[0])
            def _(val=val, land_ref=refs[3 * n + a]):
                land_ref[...] = val

    shapes = [p.shape[2:] for p in parts]
    mine = [pl.BlockSpec((None, None) + sh, lambda s, qc_ref: (s, qc_ref[1], 0, 0)) for sh in shapes]
    block = [pl.BlockSpec((None,) + sh, lambda s, qc_ref: (s, 0, 0)) for sh in shapes]
    own = [pl.BlockSpec((None,) + sh, lambda s, qc_ref: (qc_ref[0], 0, 0)) for sh in shapes]
    outs = pl.pallas_call(
        body, name=name,
        grid_spec=pltpu.PrefetchScalarGridSpec(
            num_scalar_prefetch=1, grid=(NCHIP,), in_specs=mine + block, out_specs=block + own),
        out_shape=[jax.ShapeDtypeStruct((NCHIP,) + sh, bf16) for sh in shapes] * 2,
        compiler_params=_cp(1),
    )(qc, *parts, *gots)
    return list(outs[:n]), list(outs[n:])


def _sum_chips(gots, name):
    n = len(gots)

    def body(*refs):
        for a in range(n):
            acc = refs[a][0].astype(f32)
            for s in range(1, NCHIP):
                acc = acc + refs[a][s].astype(f32)
            refs[n + a][...] = acc

    return list(pl.pallas_call(
        body, name=name, grid=(1,),
        in_specs=[pl.BlockSpec(g.shape, lambda i: (0, 0, 0)) for g in gots],
        out_specs=[pl.BlockSpec(g.shape[1:], lambda i: (0, 0)) for g in gots],
        out_shape=[jax.ShapeDtypeStruct(g.shape[1:], f32) for g in gots],
        compiler_params=_cp(1),
    )(*gots))


def _pair_share(halves, name, after=None):
    n = len(halves)
    extra = () if after is None else (after,)

    def body(*refs):
        refs = refs[len(extra):]
        ins, outs = refs[:n], refs[n:2 * n]
        send_sems, recv_sems = refs[2 * n:]
        x, y, c, _ = _place()
        copies = []
        for a in range(n):
            cp = pltpu.make_async_remote_copy(
                src_ref=ins[a], dst_ref=outs[a], send_sem=send_sems.at[a], recv_sem=recv_sems.at[a],
                device_id=(x, y, 1 - c), device_id_type=MESH)
            cp.start()
            copies.append(cp)
        for cp in copies:
            cp.wait()

    return pl.pallas_call(
        body, name=name,
        in_specs=[ANY] * (len(extra) + n), out_specs=[ANY] * n,
        out_shape=[jax.ShapeDtypeStruct(s.shape, s.dtype) for s in halves],
        scratch_shapes=[pltpu.SemaphoreType.DMA((n,)), pltpu.SemaphoreType.DMA((n,))],
    )(*extra, *halves)


def _all_copy(buf_ref, send_ref, recv_ref, k, x, y, c, landing):
    px, py, pc = (1 - x if k & 4 else x, 1 - y if k & 2 else y, 1 - c if k & 1 else c)
    me = 4 * x + 2 * y + c
    there = 4 * px + 2 * py + pc
    return pltpu.make_async_remote_copy(
        src_ref=buf_ref.at[me], dst_ref=buf_ref.at[there if landing else me],
        send_sem=send_ref.at[k - 1], recv_sem=recv_ref.at[k - 1],
        device_id=(px, py, pc), device_id_type=MESH)


def _gather_all_start(buf, name):
    def body(in_ref, send, recv, thru, token):
        del thru
        x, y, c, _ = _place()
        for k in range(1, 8):
            _all_copy(in_ref, send, recv, k, x, y, c, False).start()
        token[...] = jnp.zeros_like(token)

    return pl.pallas_call(
        body, name=name,
        in_specs=[HBM],
        out_specs=[SEM, SEM, HBM, pl.BlockSpec(memory_space=pltpu.VMEM)],
        out_shape=[pltpu.SemaphoreType.DMA((7,)), pltpu.SemaphoreType.DMA((7,)),
                   pltpu.HBM(buf.shape, buf.dtype), jax.ShapeDtypeStruct((8, 128), f32)],
        input_output_aliases={0: 2},
        compiler_params=pltpu.CompilerParams(has_side_effects=EFFECT),
    )(_in_hbm(buf))


def _gather_all_wait(send, recv, buf, after, name):
    def body(in_ref, send_r, recv_r, after_ref, out_ref):
        del after_ref, out_ref
        x, y, c, _ = _place()
        for k in range(1, 8):
            cp = _all_copy(in_ref, send_r, recv_r, k, x, y, c, True)
            cp.wait_send()
            cp.wait_recv()

    return pl.pallas_call(
        body, name=name,
        in_specs=[HBM, SEM, SEM, ANY],
        out_specs=HBM,
        out_shape=pltpu.HBM(buf.shape, buf.dtype),
        input_output_aliases={0: 0},
        compiler_params=pltpu.CompilerParams(has_side_effects=EFFECT),
    )(buf, send, recv, after)


HBM = pl.BlockSpec(memory_space=pltpu.HBM)
SEM = pl.BlockSpec(memory_space=pltpu.SEMAPHORE)
EFFECT = pltpu.SideEffectType.DATAFLOW_SIDE_EFFECTING
N_PEER = 3


def _in_hbm(a):
    return pltpu.with_memory_space_constraint(a, pltpu.HBM)


def _gather_copy(buf_ref, send_ref, recv_ref, j, chip, q, c, landing_chip):
    return pltpu.make_async_remote_copy(
        src_ref=buf_ref.at[q, c], dst_ref=buf_ref.at[landing_chip, c],
        send_sem=send_ref.at[j], recv_sem=recv_ref.at[j],
        device_id=(chip[0], chip[1], c), device_id_type=MESH)


def _gather_start(bufs, name):
    n = len(bufs)

    def body(*refs):
        ins = refs[:n]
        send, recv = refs[n:2 * n], refs[2 * n:3 * n]
        token = refs[4 * n]
        x, y, c, chips = _place()
        q = 2 * x + y
        for a in range(n):
            for j, chip in enumerate(chips):
                _gather_copy(ins[a], send[a], recv[a], j, chip, q, c, q).start()
        token[...] = jnp.zeros_like(token)

    sems = [pltpu.SemaphoreType.DMA((N_PEER,))] * (2 * n)
    outs = pl.pallas_call(
        body, name=name,
        in_specs=[HBM] * n,
        out_specs=[SEM] * (2 * n) + [HBM] * n + [pl.BlockSpec(memory_space=pltpu.VMEM)],
        out_shape=sems + [pltpu.HBM(b.shape, b.dtype) for b in bufs] + [jax.ShapeDtypeStruct((8, 128), f32)],
        input_output_aliases={a: 2 * n + a for a in range(n)},
        compiler_params=pltpu.CompilerParams(has_side_effects=EFFECT),
    )(*[_in_hbm(b) for b in bufs])
    return list(outs[:n]), list(outs[n:2 * n]), list(outs[2 * n:3 * n]), outs[3 * n]


def _gather_wait(send, recv, bufs, after, name):
    n = len(bufs)

    def body(*refs):
        ins = refs[:n]
        send_r, recv_r = refs[n:2 * n], refs[2 * n:3 * n]
        x, y, c, chips = _place()
        q = 2 * x + y
        for a in range(n):
            for j, chip in enumerate(chips):
                cp = _gather_copy(ins[a], send_r[a], recv_r[a], j, chip, q, c, _chip_id(chip))
                cp.wait_send()
                cp.wait_recv()

    afters = after if isinstance(after, (tuple, list)) else (after,)
    outs = pl.pallas_call(
        body, name=name,
        in_specs=[HBM] * n + [SEM] * (2 * n) + [ANY] * len(afters),
        out_specs=[HBM] * n,
        out_shape=[pltpu.HBM(b.shape, b.dtype) for b in bufs],
        input_output_aliases={a: a for a in range(n)},
        compiler_params=pltpu.CompilerParams(has_side_effects=EFFECT),
    )(*bufs, *send, *recv, *afters)
    return list(outs)


def _forward_halves(bufs, name):
    n = len(bufs)

    def body(*refs):
        outs = refs[n:2 * n]
        send_sems, recv_sems = refs[2 * n:]
        x, y, c, chips = _place()
        sibling = (x, y, 1 - c)

        def remote(a, j, blk):
            return pltpu.make_async_remote_copy(src_ref=blk, dst_ref=blk, send_sem=send_sems.at[a, j],
                                                recv_sem=recv_sems.at[a, j], device_id=sibling,
                                                device_id_type=MESH)

        sent = []
        for a in range(n):
            for j, chip in enumerate(chips):
                cp = remote(a, j, outs[a].at[_chip_id(chip), c])
                cp.start()
                sent.append(cp)
        for a in range(n):
            for j, chip in enumerate(chips):
                remote(a, j, outs[a].at[_chip_id(chip), 1 - c]).wait_recv()
        for cp in sent:
            cp.wait_send()

    return pl.pallas_call(
        body, name=name,
        in_specs=[ANY] * n, out_specs=[ANY] * n,
        out_shape=[jax.ShapeDtypeStruct(s.shape, s.dtype) for s in bufs],
        scratch_shapes=[pltpu.SemaphoreType.DMA((n, N_PEER)), pltpu.SemaphoreType.DMA((n, N_PEER))],
        input_output_aliases={a: a for a in range(n)},
    )(*bufs)


def _reduce_copy(sum_ref, land_ref, send_ref, recv_ref, j, chip, q, c, landing_chip):
    return pltpu.make_async_remote_copy(
        src_ref=sum_ref.at[_chip_id(chip)], dst_ref=land_ref.at[landing_chip],
        send_sem=send_ref.at[j], recv_sem=recv_ref.at[j],
        device_id=(chip[0], chip[1], c), device_id_type=MESH)


def _reduce_start(sums, lands, name):
    n = len(sums)

    def body(*refs):
        s_in, l_in = refs[:n], refs[n:2 * n]
        send, recv = refs[2 * n:3 * n], refs[3 * n:4 * n]
        token = refs[6 * n]
        x, y, c, chips = _place()
        q = 2 * x + y
        for a in range(n):
            for j, chip in enumerate(chips):
                _reduce_copy(s_in[a], l_in[a], send[a], recv[a], j, chip, q, c, q).start()
        token[...] = jnp.zeros_like(token)

    sems = [pltpu.SemaphoreType.DMA((N_PEER,))] * (2 * n)
    outs = pl.pallas_call(
        body, name=name,
        in_specs=[HBM] * (2 * n),
        out_specs=[SEM] * (2 * n) + [HBM] * (2 * n) + [pl.BlockSpec(memory_space=pltpu.VMEM)],
        out_shape=sems + [pltpu.HBM(b.shape, b.dtype) for b in list(sums) + list(lands)]
        + [jax.ShapeDtypeStruct((8, 128), f32)],
        input_output_aliases={a: 2 * n + a for a in range(2 * n)},
        compiler_params=pltpu.CompilerParams(has_side_effects=EFFECT),
    )(*[_in_hbm(b) for b in list(sums) + list(lands)])
    return (list(outs[:n]), list(outs[n:2 * n]), list(outs[2 * n:3 * n]), list(outs[3 * n:4 * n]),
            outs[4 * n])


def _reduce_wait(send, recv, sums, lands, after, name):
    n = len(sums)

    def body(*refs):
        s_in, l_in = refs[:n], refs[n:2 * n]
        send_r, recv_r = refs[2 * n:3 * n], refs[3 * n:4 * n]
        x, y, c, chips = _place()
        q = 2 * x + y
        for a in range(n):
            for j, chip in enumerate(chips):
                cp = _reduce_copy(s_in[a], l_in[a], send_r[a], recv_r[a], j, chip, q, c, _chip_id(chip))
                cp.wait_send()
                cp.wait_recv()

    afters = after if isinstance(after, (tuple, list)) else (after,)
    outs = pl.pallas_call(
        body, name=name,
        in_specs=[HBM] * (2 * n) + [SEM] * (2 * n) + [ANY] * len(afters),
        out_specs=[HBM] * (2 * n),
        out_shape=[pltpu.HBM(b.shape, b.dtype) for b in list(sums) + list(lands)],
        input_output_aliases={a: a for a in range(2 * n)},
        compiler_params=pltpu.CompilerParams(has_side_effects=EFFECT),
    )(*sums, *lands, *send, *recv, *afters)
    return list(outs[n:])


def _sibling_copy(part_ref, land_ref, send_ref, recv_ref, x, y, c):
    return pltpu.make_async_remote_copy(
        src_ref=part_ref.at[:, 1 - c], dst_ref=land_ref, send_sem=send_ref.at[0], recv_sem=recv_ref.at[0],
        device_id=(x, y, 1 - c), device_id_type=MESH)


def _pair_exchange_start(parts, name):
    n = len(parts)
    lands = [lax.empty((NCHIP,) + p.shape[2:], p.dtype) for p in parts]

    def body(*refs):
        p_in, l_in = refs[:n], refs[n:2 * n]
        send, recv = refs[2 * n:3 * n], refs[3 * n:4 * n]
        token = refs[6 * n]
        x, y, c, _ = _place()
        for a in range(n):
            _sibling_copy(p_in[a], l_in[a], send[a], recv[a], x, y, c).start()
        token[...] = jnp.zeros_like(token)

    sems = [pltpu.SemaphoreType.DMA((1,))] * (2 * n)
    outs = pl.pallas_call(
        body, name=name,
        in_specs=[HBM] * (2 * n),
        out_specs=[SEM] * (2 * n) + [HBM] * (2 * n) + [pl.BlockSpec(memory_space=pltpu.VMEM)],
        out_shape=sems + [pltpu.HBM(b.shape, b.dtype) for b in list(parts) + lands]
        + [jax.ShapeDtypeStruct((8, 128), f32)],
        input_output_aliases={a: 2 * n + a for a in range(2 * n)},
        compiler_params=pltpu.CompilerParams(has_side_effects=EFFECT),
    )(*[_in_hbm(b) for b in list(parts) + lands])
    return (list(outs[:n]), list(outs[n:2 * n]), list(outs[2 * n:3 * n]), list(outs[3 * n:4 * n]),
            outs[4 * n])


def _pair_exchange_wait(send, recv, parts, lands, after, name):
    n = len(parts)

    def body(*refs):
        p_in, l_in = refs[:n], refs[n:2 * n]
        send_r, recv_r = refs[2 * n:3 * n], refs[3 * n:4 * n]
        x, y, c, _ = _place()
        for a in range(n):
            cp = _sibling_copy(p_in[a], l_in[a], send_r[a], recv_r[a], x, y, c)
            cp.wait_send()
            cp.wait_recv()

    outs = pl.pallas_call(
        body, name=name,
        in_specs=[HBM] * (2 * n) + [SEM] * (2 * n) + [ANY],
        out_specs=[HBM] * (2 * n),
        out_shape=[pltpu.HBM(b.shape, b.dtype) for b in list(parts) + list(lands)],
        input_output_aliases={a: a for a in range(2 * n)},
        compiler_params=pltpu.CompilerParams(has_side_effects=EFFECT),
    )(*parts, *lands, *send, *recv, after)
    return list(outs[:n]), list(outs[n:])


def _forward_copy(buf_ref, send_ref, recv_ref, j, chip, x, y, c, landing):
    return pltpu.make_async_remote_copy(
        src_ref=buf_ref.at[_chip_id(chip), c], dst_ref=buf_ref.at[_chip_id(chip), 1 - c if landing else c],
        send_sem=send_ref.at[j], recv_sem=recv_ref.at[j], device_id=(x, y, 1 - c), device_id_type=MESH)


def _forward_start(bufs, name):
    n = len(bufs)

    def body(*refs):
        ins = refs[:n]
        send, recv = refs[n:2 * n], refs[2 * n:3 * n]
        token = refs[4 * n]
        x, y, c, chips = _place()
        for a in range(n):
            for j, chip in enumerate(chips):
                _forward_copy(ins[a], send[a], recv[a], j, chip, x, y, c, False).start()
        token[...] = jnp.zeros_like(token)

    sems = [pltpu.SemaphoreType.DMA((N_PEER,))] * (2 * n)
    outs = pl.pallas_call(
        body, name=name,
        in_specs=[HBM] * n,
        out_specs=[SEM] * (2 * n) + [HBM] * n + [pl.BlockSpec(memory_space=pltpu.VMEM)],
        out_shape=sems + [pltpu.HBM(b.shape, b.dtype) for b in bufs] + [jax.ShapeDtypeStruct((8, 128), f32)],
        input_output_aliases={a: 2 * n + a for a in range(n)},
        compiler_params=pltpu.CompilerParams(has_side_effects=EFFECT),
    )(*[_in_hbm(b) for b in bufs])
    return list(outs[:n]), list(outs[n:2 * n]), list(outs[2 * n:3 * n]), outs[3 * n]


def _forward_wait(send, recv, bufs, after, name):
    n = len(bufs)

    def body(*refs):
        ins = refs[:n]
        send_r, recv_r = refs[n:2 * n], refs[2 * n:3 * n]
        x, y, c, chips = _place()
        for a in range(n):
            for j, chip in enumerate(chips):
                cp = _forward_copy(ins[a], send_r[a], recv_r[a], j, chip, x, y, c, True)
                cp.wait_send()
                cp.wait_recv()

    outs = pl.pallas_call(
        body, name=name,
        in_specs=[HBM] * n + [SEM] * (2 * n) + [ANY],
        out_specs=[HBM] * n,
        out_shape=[pltpu.HBM(b.shape, b.dtype) for b in bufs],
        input_output_aliases={a: a for a in range(n)},
        compiler_params=pltpu.CompilerParams(has_side_effects=EFFECT),
    )(*bufs, *send, *recv, after)
    return list(outs)


def _adamw_math(w, g, m, v):
    m = ADAM_B1 * m + (1.0 - ADAM_B1) * g
    v = ADAM_B2 * v + (1.0 - ADAM_B2) * (g * g)
    m_hat = m / (1.0 - ADAM_B1 ** ADAM_STEP)
    v_hat = v / (1.0 - ADAM_B2 ** ADAM_STEP)
    delta = -ADAM_LR * (m_hat / (jnp.sqrt(v_hat) + ADAM_EPS) + ADAM_WD * w)
    return delta, m, v


ADAMW_BLOCK_BYTES = 3 << 19


def _adamw(ws, mines, theirs, ms, vs, qc, name):
    n = len(ws)
    halves = [w.shape[0] // 2 for w in ws]
    nb = next(k for k in range(1, min(halves) + 1)
              if all(hr % k == 0 and (hr // k) % 8 == 0 and (hr // k) * w.shape[1] * 4 <= ADAMW_BLOCK_BYTES
                     for hr, w in zip(halves, ws)))

    def body(qc_ref, *refs):
        mine_here = pl.program_id(0) == qc_ref[1]
        for a in range(n):
            w_ref, a_ref, b_ref, m_ref, v_ref = (refs[k * n + a] for k in range(5))
            g_ref, d_ref, mo_ref, vo_ref = (refs[(5 + k) * n + a] for k in range(4))
            g = jnp.where(mine_here, a_ref[...], b_ref[...])
            g_ref[...] = g
            d_ref[...], mo_ref[...], vo_ref[...] = _adamw_math(w_ref[...], g, m_ref[...], v_ref[...])

    blocks = [(hr // nb, w.shape[1]) for hr, w in zip(halves, ws)]
    full = [pl.BlockSpec(b, lambda h, i, qc_ref: (h * nb + i, 0)) for b in blocks]
    half = [pl.BlockSpec(b, lambda h, i, qc_ref: (i, 0)) for b in blocks]
    outs = pl.pallas_call(
        body, name=name,
        grid_spec=pltpu.PrefetchScalarGridSpec(
            num_scalar_prefetch=1, grid=(2, nb),
            in_specs=full + half + half + full + full, out_specs=full * 4),
        out_shape=[jax.ShapeDtypeStruct(w.shape, f32) for w in ws] * 4,
        compiler_params=_cp(2),
    )(qc, *ws, *mines, *theirs, *ms, *vs)
    return [tuple(outs[k * n + a] for k in range(4)) for a in range(n)]


REPL = [("ffn1_norm", 1), ("mix_norm", 1), ("b_in", 6), ("rnn_conv_b", 1), ("rg_b_a", 1), ("rg_b_x", 1),
        ("rg_lambda", 1), ("conv_dw_b", 1), ("conv_ln_g", 1), ("conv_ln_b", 1), ("conv_b_proj", 1),
        ("ffn2_norm", 1), ("final_norm", 1)]
COLSH = [("meta_tokens", NMETA), ("rnn_conv_w", KC4), ("conv_dw_w", KC31)]
SMALL = REPL + COLSH
CS = D // NCHIP


def _pack_rows():
    starts, row = {}, 0
    for k, rows in REPL:
        starts[k] = row
        row += rows
    for k, rows in COLSH:
        row = -(-row // 8) * 8
        starts[k] = row
        row += rows
    return starts, -(-row // 8) * 8


PACK_START, LOSS_ROW = _pack_rows()
SMALL_ROWS = LOSS_ROW + 8


def _small_pack(g, loss_row):
    pieces, row = [], 0
    for k, rows in SMALL:
        if PACK_START[k] > row:
            pieces.append(jnp.zeros((PACK_START[k] - row, D), f32))
        pieces.append(g[k].reshape(rows, D))
        row = PACK_START[k] + rows
    pieces.append(jnp.zeros((LOSS_ROW - row, D), f32))
    pieces.append(loss_row)
    pieces.append(jnp.zeros((SMALL_ROWS - LOSS_ROW - 1, D), f32))
    return jnp.concatenate(pieces, axis=0)


def _adamw_small(packs, ws, ms, vs):
    ns = len(SMALL)

    def body(*refs):
        pack_ref = refs[0]
        w_refs, m_refs, v_refs = refs[1:1 + ns], refs[1 + ns:1 + 2 * ns], refs[1 + 2 * ns:1 + 3 * ns]
        outs = refs[1 + 3 * ns:1 + 7 * ns]
        g_refs, d_refs, mo_refs, vo_refs = outs[:ns], outs[ns:2 * ns], outs[2 * ns:3 * ns], outs[3 * ns:]
        loss_ref = refs[1 + 7 * ns]
        gsum_sc = refs[2 + 7 * ns]
        q = 2 * lax.axis_index("x") + lax.axis_index("y")
        acc = pack_ref[0]
        for dev in range(1, 8):
            acc = acc + pack_ref[dev]
        gsum_sc[...] = acc
        loss_ref[...] = gsum_sc[LOSS_ROW:LOSS_ROW + 1, :]
        for idx, (name, rows) in enumerate(SMALL):
            row = PACK_START[name]
            if idx < len(REPL):
                for k in range(rows):
                    cols = slice(k * D, (k + 1) * D)
                    g = gsum_sc[row + k:row + k + 1, :]
                    d, mm, vv = _adamw_math(w_refs[idx][:, cols], g, m_refs[idx][:, cols], v_refs[idx][:, cols])
                    g_refs[idx][:, cols] = g
                    d_refs[idx][:, cols] = d
                    mo_refs[idx][:, cols] = mm
                    vo_refs[idx][:, cols] = vv
            else:
                g = gsum_sc[row:row + rows, pl.ds(pl.multiple_of(q * CS, CS), CS)]
                d, mm, vv = _adamw_math(w_refs[idx][...], g, m_refs[idx][...], v_refs[idx][...])
                g_refs[idx][...] = g
                d_refs[idx][...] = d
                mo_refs[idx][...] = mm
                vo_refs[idx][...] = vv

    shapes = [jax.ShapeDtypeStruct(w.shape, f32) for w in ws]
    return pl.pallas_call(
        body, name="adamw_small",
        out_shape=shapes * 4 + [jax.ShapeDtypeStruct((1, D), f32)],
        scratch_shapes=[pltpu.VMEM((SMALL_ROWS, D), f32)],
        compiler_params=pltpu.CompilerParams(vmem_limit_bytes=VMEM_LIMIT),
    )(packs, *ws, *ms, *vs)


WEIGHTS = ['meta_tokens', 'ffn1_norm', 'ffn1_w_gu', 'ffn1_w_down', 'mix_norm', 'w_in', 'b_in', 'rnn_conv_w',
           'rnn_conv_b', 'rg_w_a', 'rg_b_a', 'rg_w_x', 'rg_b_x', 'rg_lambda', 'rnn_w_proj', 'conv_dw_w',
           'conv_dw_b', 'conv_ln_g', 'conv_ln_b', 'conv_w_proj', 'conv_b_proj', 'w_out', 'ffn2_norm',
           'ffn2_w_gu', 'ffn2_w_down', 'final_norm']


def _as2d(a):
    return a.reshape(-1, a.shape[-1])


def _step(x, loss_target, w, m, v):
    seq = x.shape[1]
    n_valid = NMETA + seq
    t = -(-n_valid // TM) * TM

    qc = jnp.stack([2 * lax.axis_index("x") + lax.axis_index("y"), lax.axis_index("c")]).astype(jnp.int32)
    p = {k: w[k].reshape(1, rows * D) for k, rows in REPL}

    first = ["ffn1_w_gu", "ffn1_w_down", "small"]
    later = [["w_in"], ["rg_w_a", "rg_w_x", "rnn_w_proj", "conv_w_proj", "w_out"], ["ffn2_w_gu", "ffn2_w_down"]]
    small_rows = sum(r for _, r in COLSH)
    small = jnp.concatenate([_as2d(w[k]) for k, _ in COLSH] + [jnp.zeros((64 - small_rows, CS), f32)], axis=0)

    def cast(k, token=None):
        src, dtype = (small, f32) if k == "small" else (_as2d(w[k]), bf16)
        return _cast_into_slot(src, qc, dtype, "cast_" + k, after=token)

    send1, recv1, bufs1, token1 = _gather_start([cast(k) for k in first], "gather_start_first")
    rest = [k for grp in later for k in grp]
    send2, recv2, bufs2, token2 = _gather_start([cast(k, token1) for k in rest], "gather_start_rest")

    def install(names, done):
        for k, b in zip(names, done):
            full = b.reshape(NCHIP, 2 * b.shape[2], b.shape[3])
            if k in ("ffn1_w_down", "ffn2_w_down"):
                full = full.reshape(F, D)
            elif k in ("rnn_w_proj", "conv_w_proj", "w_out"):
                full = full.reshape(D, D)
            elif k in ("rg_w_a", "rg_w_x"):
                full = full.reshape(NCHIP, NHEAD, HD // NCHIP, HD).transpose(1, 0, 2, 3).reshape(NHEAD, HD, HD)
            p[k] = full

    def finish(names, send, recv, bufs, after, tag):
        install(names, _forward_halves(_gather_wait(send, recv, bufs, after, "gather_wait_" + tag),
                                       "gather_forward_" + tag))

    def group(names):
        idx = [rest.index(k) for k in names]
        return names, [send2[i] for i in idx], [recv2[i] for i in idx], [bufs2[i] for i in idx]

    h0 = jnp.pad(x[0] + token1[0:1, 0:1], ((NMETA, t - n_valid), (0, 0)))
    tgt = jnp.pad(loss_target[0] + token2[0:1, 0:1], ((NMETA, t - n_valid), (0, 0)))
    finish(first, send1, recv1, bufs1, (token2, h0, tgt), "first")
    small_full = p.pop("small").transpose(1, 0, 2).reshape(64, D)
    row = 0
    for k, rows in COLSH:
        p[k] = small_full[row:row + rows]
        row += rows

    h0 = lax.dynamic_update_slice(h0, p["meta_tokens"], (0, 0))
    h1, gate1, up1, n1 = _ffn_fwd(h0, p["ffn1_norm"], p["ffn1_w_gu"], p["ffn1_w_down"], "ffn1_fwd")
    finish(*group(later[0]), h1, "in")
    proj, n2 = _inproj_fwd(h1, p["mix_norm"], p["w_in"], p["b_in"])
    names_l = later[1] + later[2]
    _, send_l, recv_l, bufs_l = group(names_l)
    send_f, recv_f, bufs_f, token = _forward_start(
        _gather_wait(send_l, recv_l, bufs_l, proj, "gather_wait_late"), "gather_forward_start")
    vc, s = _conv_fwd(proj, p["conv_dw_w"], p["conv_dw_b"], p["conv_ln_g"], p["conv_ln_b"], after=token)
    install(names_l, _forward_wait(send_f, recv_f, bufs_f, vc, "gather_forward_wait"))
    xr, hr, z, gates = _rnn_fwd(proj, p["rnn_conv_w"], p["rnn_conv_b"], p["rg_w_a"], p["rg_b_a"],
                         p["rg_w_x"], p["rg_b_x"], p["rg_lambda"])
    h2 = _merge_fwd(h1, z, s, proj, p["rnn_w_proj"], p["conv_w_proj"], p["conv_b_proj"], p["w_out"])
    dh3, loss_blk, d_final, gate2, up2, n3 = _ffn_fwd(
        h2, p["ffn2_norm"], p["ffn2_w_gu"], p["ffn2_w_down"], "ffn2_fwd",
        loss_head=(p["final_norm"], tgt, n_valid))

    g = {"final_norm": d_final}
    pending = []

    def exchange_start(names, tag):
        parts = []
        for k in names:
            rows = g[k].size // (NCHIP * g[k].shape[-1])
            parts.append(g[k].reshape((NCHIP, 2, rows // 2, g[k].shape[-1])))
        send, recv, parts, lands, token = _pair_exchange_start(parts, "pair_exchange_start_" + tag)
        return (names, tag, send, recv, parts, lands), token

    def reduce_start(state, after):
        names, tag, send, recv, parts, lands = state
        parts, from_sibling = _pair_exchange_wait(send, recv, parts, lands, after, "pair_exchange_wait_" + tag)
        sums, lands = _pair_add(parts, from_sibling, qc, "pair_add_" + tag)
        send, recv, sums, lands, token = _reduce_start(sums, lands, "reduce_start_" + tag)
        pending.append((names, tag, send, recv, sums, lands))
        return token

    dh2, dgate2, dup2, a2, df2, g["ffn2_norm"] = _ffn_bwd(
        dh3, h2, p["ffn2_norm"], gate2, up2, p["ffn2_w_gu"], p["ffn2_w_down"], "ffn2_bwd")
    g["ffn2_w_gu"] = _ffn_gu_grad(n3, dgate2, dup2, "ffn2")
    g["ffn2_w_down"] = _ffn_down_grad(a2, df2, "ffn2")
    state, token = exchange_start(["ffn2_w_gu", "ffn2_w_down"], "ffn2")

    dz, ds, dproj, dh2b, merged, dya, dyb, g["conv_b_proj"] = _merge_bwd(
        dh2, z, s, proj, p["rnn_w_proj"], p["conv_w_proj"], p["conv_b_proj"], p["w_out"], after=token)
    token = reduce_start(state, dz)
    dproj, g["conv_dw_w"], g["conv_dw_b"], g["conv_ln_g"], g["conv_ln_b"] = _conv_bwd(
        ds, vc, proj, dproj, p["conv_dw_w"], p["conv_ln_g"], p["conv_ln_b"], after=token)
    g["w_out"] = _square_grad(merged, dh2b, "dw_out")
    g["rnn_w_proj"] = _square_grad(z, dya, "dw_rnn_proj")
    g["conv_w_proj"] = _square_grad(s, dyb, "dw_conv_proj")
    (dproj, g["rg_w_a"], g["rg_w_x"], g["rnn_conv_w"], g["rnn_conv_b"], g["rg_b_a"], g["rg_b_x"],
     g["rg_lambda"]) = _rnn_bwd(dz, xr, hr, gates, proj, dproj, p["rnn_conv_w"], p["rg_w_a"],
                                p["rg_w_x"], p["rg_lambda"])

    dh1, g["mix_norm"], db_in = _inproj_bwd(dproj, dh2, h1, p["mix_norm"], p["w_in"])
    g["b_in"] = db_in.reshape(1, NIN)
    g["w_in"] = _tn_matmul(n2, dproj, D, NIN // NCHIP, (NCHIP, D, NIN // NCHIP),
                           (None, D, NIN // NCHIP), lambda k, nn, mm: (nn, 0, 0), "dw_in")
    state, token = exchange_start(["w_out", "rnn_w_proj", "conv_w_proj", "rg_w_a", "rg_w_x", "w_in"], "mix")

    dh0, dgate1, dup1, a1, df1, g["ffn1_norm"] = _ffn_bwd(
        dh1, h0, p["ffn1_norm"], gate1, up1, p["ffn1_w_gu"], p["ffn1_w_down"], "ffn1_bwd", after=token)
    g["meta_tokens"] = dh0[0:NMETA]
    grad_x = dh0[NMETA:n_valid][None]
    token = reduce_start(state, dh0)

    send_s, recv_s, pack_buf, token_s = _gather_all_start(
        _place_pack(_small_pack(g, loss_blk.reshape(1, D)), qc), "gather_all_start")
    g["ffn1_w_down"] = _ffn_down_grad(a1, df1, "ffn1", after=(token, token_s))
    state, token = exchange_start(["ffn1_w_down"], "ffn1_down")
    gate_half = _tn_matmul(n1, dgate1, D, FS, (NCHIP, D, FS), (None, D, FS), lambda k, nn, mm: (nn, 0, 0),
                           "ffn1_dwg", after=token)
    token = reduce_start(state, gate_half)
    g["ffn1_w_gu"] = _tn_matmul(n1, dup1, D, FS, (NCHIP, D, FS), (None, D, FS), lambda k, nn, mm: (2 + nn, 0, 0),
                                "ffn1_dwu", base=gate_half, after=token)
    state_gu, token = exchange_start(["ffn1_w_gu"], "ffn1_gu")
    packs = _gather_all_wait(send_s, recv_s, pack_buf, token, "gather_all_wait")

    grads, deltas, new_m, new_v = {}, {}, {}, {}

    def landed_sums(items, after):
        names, mine = [], []
        for grp_names, grp_tag, send, recv, sums, lands in items:
            landed = _reduce_wait(send, recv, sums, lands, after, "reduce_wait_" + grp_tag)
            mine += _sum_chips(landed, "sum_chips_" + grp_tag)
            names += grp_names
            after = mine[-1]
        return names, mine

    def share_and_update(names, mine, tag, after=None):
        theirs = _pair_share(mine, "pair_share_" + tag, after=after)
        got = dict(zip(names, zip(mine, theirs)))
        square = [k for k in names if got[k][0].shape[0] * 2 <= HD]
        for batch in [[k] for k in names if k not in square] + ([square] if square else []):
            outs = _adamw([_as2d(w[k]) for k in batch], [got[k][0] for k in batch], [got[k][1] for k in batch],
                          [_as2d(m[k]) for k in batch], [_as2d(v[k]) for k in batch], qc,
                          "adamw_" + (batch[0] if len(batch) == 1 else "mixer"))
            for k, out in zip(batch, outs):
                grads[k], deltas[k], new_m[k], new_v[k] = (a.reshape(w[k].shape) for a in out)
        return [new_v[k] for k in names]

    early_names, early_mine = landed_sums(pending[:2], packs)
    token = reduce_start(state_gu, early_mine[-1])
    after = share_and_update(early_names, early_mine, "early", after=token)
    share_and_update(*landed_sums(pending[2:], after), "late")
    names = [k for k, _ in SMALL]
    shape2 = {k: ((1, rows * D) if (k, rows) in REPL else (rows, CS)) for k, rows in SMALL}
    outs = _adamw_small(packs, *[[a[k].reshape(shape2[k]) for k in names] for a in (w, m, v)])
    ns = len(names)
    for i, k in enumerate(names):
        grads[k], deltas[k], new_m[k], new_v[k] = (outs[j * ns + i].reshape(w[k].shape) for j in range(4))

    loss = outs[4 * ns][0, 0]
    return (loss, grad_x, *[grads[k] for k in WEIGHTS], *[deltas[k] for k in WEIGHTS],
            *[new_m[k] for k in WEIGHTS], *[new_v[k] for k in WEIGHTS])


def kernel(x, meta_tokens, ffn1_norm, ffn1_w_gu, ffn1_w_down, mix_norm, w_in, b_in, rnn_conv_w, rnn_conv_b, rg_w_a, rg_b_a, rg_w_x, rg_b_x, rg_lambda, rnn_w_proj, conv_dw_w, conv_dw_b, conv_ln_g, conv_ln_b, conv_w_proj, conv_b_proj, w_out, ffn2_norm, ffn2_w_gu, ffn2_w_down, final_norm, loss_target, m_meta_tokens, m_ffn1_norm, m_ffn1_w_gu, m_ffn1_w_down, m_mix_norm, m_w_in, m_b_in, m_rnn_conv_w, m_rnn_conv_b, m_rg_w_a, m_rg_b_a, m_rg_w_x, m_rg_b_x, m_rg_lambda, m_rnn_w_proj, m_conv_dw_w, m_conv_dw_b, m_conv_ln_g, m_conv_ln_b, m_conv_w_proj, m_conv_b_proj, m_w_out, m_ffn2_norm, m_ffn2_w_gu, m_ffn2_w_down, m_final_norm, v_meta_tokens, v_ffn1_norm, v_ffn1_w_gu, v_ffn1_w_down, v_mix_norm, v_w_in, v_b_in, v_rnn_conv_w, v_rnn_conv_b, v_rg_w_a, v_rg_b_a, v_rg_w_x, v_rg_b_x, v_rg_lambda, v_rnn_w_proj, v_conv_dw_w, v_conv_dw_b, v_conv_ln_g, v_conv_ln_b, v_conv_w_proj, v_conv_b_proj, v_w_out, v_ffn2_norm, v_ffn2_w_gu, v_ffn2_w_down, v_final_norm):
    args = locals()
    w = {k: args[k] for k in WEIGHTS}
    m = {k: args["m_" + k] for k in WEIGHTS}
    v = {k: args["v_" + k] for k in WEIGHTS}
    return _step(x, loss_target, w, m, v)
```

```python
import jax
import jax.numpy as jnp
from jax import lax
from jax.experimental import pallas as pl
from jax.experimental.pallas import tpu as pltpu

f32 = jnp.float32
bf16 = jnp.bfloat16

D = 1024
F = 2816
FS = F // 2
NIN = 6 * D
NMETA = 16
NHEAD = 4
HD = D // NHEAD
KC4 = 4
KC31 = 31
HALO = 32
EPS = 1e-6
TM = 416
NCHIP = 4
MESH = pl.DeviceIdType.MESH

ADAM_LR = 0.001
ADAM_B1 = 0.9
ADAM_B2 = 0.999
ADAM_EPS = 1e-08
ADAM_WD = 0.01
ADAM_STEP = 10

VMEM_LIMIT = 56 * 1024 * 1024
FSUB = [(o, min(256, FS - o)) for o in range(0, FS, 256)]


def _cp(n_axes, **kw):
    return pltpu.CompilerParams(dimension_semantics=("arbitrary",) * n_axes,
                                vmem_limit_bytes=VMEM_LIMIT, **kw)


RESIDENT = pl.BlockSpec(memory_space=pltpu.VMEM)


def _n_after(after):
    return 0 if after is None else (len(after) if isinstance(after, (tuple, list)) else 1)


def _ordered(body, in_specs, args, after):
    if after is None:
        return body, in_specs, args
    extra = tuple(after) if isinstance(after, (tuple, list)) else (after,)
    return (lambda *refs: body(*refs[len(extra):]),
            [pl.BlockSpec(memory_space=pl.ANY)] * len(extra) + list(in_specs), extra + tuple(args))


def _nt_dot(a, b):
    return lax.dot_general(a, b, (((1,), (1,)), ((), ())), preferred_element_type=f32)


def _tn_dot(a, b):
    return lax.dot_general(a, b, (((0,), (0,)), ((), ())), preferred_element_type=f32)


def _sigmoid(x):
    return 0.5 * jnp.tanh(0.5 * x) + 0.5


def _log1p(y):
    u = 1.0 + y
    d = u - 1.0
    return jnp.where(d == 0.0, y, jnp.log(u) * (y / jnp.where(d == 0.0, 1.0, d)))


def _softplus(x):
    return jnp.maximum(x, 0.0) + _log1p(jnp.exp(-jnp.abs(x)))


def _one_minus_square(a, log_a):
    x = 2.0 * log_a
    series = x * (1.0 + x * (0.5 + x * (1.0 / 6.0)))
    return jnp.where(jnp.abs(x) < 0.03, -series, 1.0 - a * a)


_GELU_C = 0.7978845608028654
_GELU_K = 0.044715


def _gelu_and_grad(y):
    y2 = y * y
    th = jnp.tanh(_GELU_C * (y + _GELU_K * y * y2))
    gel = 0.5 * y * (1.0 + th)
    dgel = 0.5 * (1.0 + th) + 0.5 * y * (1.0 - th * th) * _GELU_C * (1.0 + 3.0 * _GELU_K * y2)
    return gel, dgel


def _rms_stats(h):
    return lax.rsqrt(jnp.mean(h * h, axis=-1, keepdims=True) + EPS)


def _rms_bwd(dn, h, g):
    r = _rms_stats(h)
    nhat = h * r
    dnh = dn * g
    dh = r * (dnh - nhat * jnp.mean(dnh * nhat, axis=-1, keepdims=True))
    dg = jnp.sum(dn * nhat, axis=0, keepdims=True)
    return dh, dg


def _row_ids(shape):
    return lax.broadcasted_iota(jnp.int32, shape, 0)


def _ffn_fwd(h, g, wgu, wd, name, loss_head=None):
    t = h.shape[0]
    nj = 2
    tm = TM
    n_head = 0 if loss_head is None else 2

    def body(*refs):
        h_ref, g_ref, wg_ref, wd_ref = refs[:4]
        outs = refs[4 + n_head:]
        gate_ref, up_ref, n_ref, nb_sc, acc_sc, a_sc = outs[-6:]
        i = pl.program_id(0)
        j = pl.program_id(1)

        @pl.when(j == 0)
        def _():
            hh = h_ref[...]
            nb = (hh * _rms_stats(hh) * g_ref[...]).astype(bf16)
            nb_sc[...] = nb
            n_ref[...] = nb
            acc_sc[...] = jnp.zeros_like(acc_sc)

        nb = nb_sc[...]
        for off, width in FSUB:
            cols = slice(off, off + width)
            gt = jnp.dot(nb, wg_ref[j, :, cols], preferred_element_type=f32)
            up = jnp.dot(nb, wg_ref[2 + j, :, cols], preferred_element_type=f32)
            gate_ref[:, cols] = gt.astype(bf16)
            up_ref[:, cols] = up.astype(bf16)
            a_sc[:, cols] = (gt * _sigmoid(gt) * up).astype(bf16)
        acc_sc[...] += jnp.dot(a_sc[...], wd_ref[j], preferred_element_type=f32)

        if loss_head is None:
            @pl.when(j == nj - 1)
            def _():
                outs[0][...] = h_ref[...] + 0.5 * acc_sc[...]
        else:
            gf_ref, t_ref = refs[4:6]
            dh_ref, loss_ref, dgf_ref = outs[:3]

            @pl.when(jnp.logical_and(i == 0, j == 0))
            def _():
                loss_ref[...] = jnp.zeros_like(loss_ref)
                dgf_ref[...] = jnp.zeros_like(dgf_ref)

            @pl.when(j == nj - 1)
            def _():
                hh = h_ref[...] + 0.5 * acc_sc[...]
                gf = gf_ref[...]
                row = i * tm + _row_ids((tm, 1))
                valid = jnp.logical_and(row >= NMETA, row < loss_head[2])
                err = jnp.where(valid, hh * _rms_stats(hh) * gf - t_ref[...], 0.0)
                loss_ref[...] += 0.5 * jnp.sum(err * err) * (1.0 / D)
                dh, dgf = _rms_bwd(err * (1.0 / D), hh, gf)
                dh_ref[...] = dh
                dgf_ref[...] += dgf

    rowd = pl.BlockSpec((tm, D), lambda i, j: (i, 0))
    vec = pl.BlockSpec((1, D), lambda i, j: (0, 0))
    rowf = pl.BlockSpec((tm, FS), lambda i, j: (i, j))
    in_specs, args = [rowd, vec, RESIDENT, RESIDENT], [h, g, wgu, wd.reshape(nj, FS, D)]
    out_specs, out_shape = [rowd], [jax.ShapeDtypeStruct((t, D), f32)]
    if loss_head is not None:
        in_specs, args = in_specs + [vec, rowd], args + [loss_head[0], loss_head[1]]
        out_specs += [pl.BlockSpec((8, 128), lambda i, j: (0, 0)), vec]
        out_shape += [jax.ShapeDtypeStruct((8, 128), f32), jax.ShapeDtypeStruct((1, D), f32)]
    return pl.pallas_call(
        body, name=name, grid=(t // tm, nj),
        in_specs=in_specs,
        out_specs=out_specs + [rowf, rowf, rowd],
        out_shape=out_shape + [jax.ShapeDtypeStruct((t, F), bf16), jax.ShapeDtypeStruct((t, F), bf16),
                               jax.ShapeDtypeStruct((t, D), bf16)],
        scratch_shapes=[pltpu.VMEM((tm, D), bf16), pltpu.VMEM((tm, D), f32), pltpu.VMEM((tm, FS), bf16)],
        compiler_params=_cp(2),
    )(*args)


def _inproj_fwd(h, g, win, b_in):
    t = h.shape[0]
    tn = NIN // NCHIP
    nj = NIN // tn

    def body(h_ref, g_ref, w_ref, b_ref, proj_ref, n_ref, nb_sc):
        j = pl.program_id(1)

        @pl.when(j == 0)
        def _():
            hh = h_ref[...]
            nb = (hh * _rms_stats(hh) * g_ref[...]).astype(bf16)
            nb_sc[...] = nb
            n_ref[...] = nb

        proj_ref[...] = jnp.dot(nb_sc[...], w_ref[j], preferred_element_type=f32) + b_ref[...]

    return pl.pallas_call(
        body, name="inproj_fwd", grid=(t // TM, nj),
        in_specs=[
            pl.BlockSpec((TM, D), lambda i, j: (i, 0)),
            pl.BlockSpec((1, D), lambda i, j: (0, 0)),
            RESIDENT,
            pl.BlockSpec((1, tn), lambda i, j: (0, j)),
        ],
        out_specs=[
            pl.BlockSpec((TM, tn), lambda i, j: (i, j)),
            pl.BlockSpec((TM, D), lambda i, j: (i, 0)),
        ],
        out_shape=[jax.ShapeDtypeStruct((t, NIN), f32), jax.ShapeDtypeStruct((t, D), bf16)],
        scratch_shapes=[pltpu.VMEM((TM, D), bf16)],
        compiler_params=_cp(2),
    )(h, g, win, b_in)


def _block_gates(xr, wa_ref, ba, wx_ref, bx, lam):
    xrb = xr.astype(bf16)
    pa = jnp.concatenate([jnp.dot(xrb[:, hh * HD:(hh + 1) * HD], wa_ref[hh], preferred_element_type=f32)
                          for hh in range(NHEAD)], axis=1)
    px = jnp.concatenate([jnp.dot(xrb[:, hh * HD:(hh + 1) * HD], wx_ref[hh], preferred_element_type=f32)
                          for hh in range(NHEAD)], axis=1)
    ra = _sigmoid(pa + ba)
    ii = _sigmoid(px + bx)
    sp = _softplus(-lam)
    log_a = -8.0 * ra * sp
    a = jnp.exp(log_a)
    sq = jnp.sqrt(_one_minus_square(a, log_a))
    return ra, ii, a, sq, sp


LT = D // 128


def _to_lane_tiles(ref, value):
    for lt in range(LT):
        ref[lt] = value[:, lt * 128:(lt + 1) * 128]


def _from_lane_tiles(ref):
    return jnp.concatenate([ref[lt] for lt in range(LT)], axis=1)


def _chain_scan(mult_sc, val_sc, start, reverse):
    ng = TM // 8

    def lanes(lt):
        return slice(lt * 128, (lt + 1) * 128)

    def chain(gi, carry):
        v_prev, p_prev = carry
        rows = pl.ds(ng - 1 - gi if reverse else gi, 8, stride=ng)
        v_new, p_new = [], []
        for lt in range(LT):
            mm = mult_sc.at[lt][rows, :]
            vv = mm * v_prev[:, lanes(lt)] + val_sc.at[lt][rows, :]
            pp = mm * p_prev[:, lanes(lt)]
            val_sc.at[lt][rows, :] = vv
            mult_sc.at[lt][rows, :] = pp
            v_new.append(vv)
            p_new.append(pp)
        return jnp.concatenate(v_new, axis=1), jnp.concatenate(p_new, axis=1)

    v_end, p_end = lax.fori_loop(0, ng, chain, (jnp.zeros((8, D), f32), jnp.ones((8, D), f32)))
    state, entries = start, [None] * 8
    for r in (reversed(range(8)) if reverse else range(8)):
        entries[r] = state
        state = v_end[r:r + 1, :] + p_end[r:r + 1, :] * state
    entry8 = jnp.concatenate(entries, axis=0)

    def add_entry(gi, carry):
        rows = pl.ds(gi, 8, stride=ng)
        for lt in range(LT):
            val_sc.at[lt][rows, :] = val_sc.at[lt][rows, :] + mult_sc.at[lt][rows, :] * entry8[:, lanes(lt)]
        return carry

    lax.fori_loop(0, ng, add_entry, 0)
    return state


def _rnn_fwd(proj, cw, cb, wa, ba, wx, bx, lam):
    t = proj.shape[0]

    def body(x_ref, y_ref, cw_ref, cb_ref, wa_ref, ba_ref, wx_ref, bx_ref, lam_ref,
             xr_ref, hr_ref, z_ref, gates_ref, xext_sc, carry_sc, a_sc, h_sc):
        i = pl.program_id(0)

        @pl.when(i == 0)
        def _():
            xext_sc[0:8, :] = jnp.zeros((8, D), f32)
            carry_sc[...] = jnp.zeros_like(carry_sc)

        x = x_ref[...]
        xext_sc[8:8 + TM, :] = x
        xe = xext_sc[...]
        xr = cb_ref[...] + cw_ref[KC4 - 1:KC4, :] * x
        for k in range(KC4 - 1):
            xr = xr + cw_ref[k:k + 1, :] * pltpu.roll(xe, KC4 - 1 - k, 0)[8:8 + TM]
        xext_sc[0:8, :] = x[TM - 8:TM]

        ra, ii, a, sq, _ = _block_gates(xr, wa_ref, ba_ref[...], wx_ref, bx_ref[...], lam_ref[...])
        for slot, val in enumerate((ra, ii, a, sq)):
            gates_ref[slot] = val
        _to_lane_tiles(a_sc, a)
        _to_lane_tiles(h_sc, sq * ii * xr)
        carry_sc[...] = _chain_scan(a_sc, h_sc, carry_sc[...], reverse=False)
        hr = _from_lane_tiles(h_sc)
        gel, _ = _gelu_and_grad(y_ref[...])
        xr_ref[...] = xr
        hr_ref[...] = hr
        z_ref[...] = (hr * gel).astype(bf16)

    vec = pl.BlockSpec((1, D), lambda i: (0, 0))
    return pl.pallas_call(
        body, name="rnn_fwd", grid=(t // TM,),
        in_specs=[
            pl.BlockSpec((TM, D), lambda i: (i, 0)),
            pl.BlockSpec((TM, D), lambda i: (i, 1)),
            pl.BlockSpec((KC4, D), lambda i: (0, 0)),
            vec,
            pl.BlockSpec((NHEAD, HD, HD), lambda i: (0, 0, 0)),
            vec,
            pl.BlockSpec((NHEAD, HD, HD), lambda i: (0, 0, 0)),
            vec, vec,
        ],
        out_specs=[pl.BlockSpec((TM, D), lambda i: (i, 0))] * 3 + [pl.BlockSpec((4, TM, D), lambda i: (0, i, 0))],
        out_shape=[jax.ShapeDtypeStruct((t, D), f32), jax.ShapeDtypeStruct((t, D), f32),
                   jax.ShapeDtypeStruct((t, D), bf16), jax.ShapeDtypeStruct((4, t, D), f32)],
        scratch_shapes=[pltpu.VMEM((TM + 8, D), f32), pltpu.VMEM((1, D), f32),
                        pltpu.VMEM((LT, TM, 128), f32), pltpu.VMEM((LT, TM, 128), f32)],
        compiler_params=_cp(1),
    )(proj, proj, cw, cb, wa, ba, wx, bx, lam)


def _ln_stats(vc):
    mu = jnp.mean(vc, axis=-1, keepdims=True)
    xc = vc - mu
    rstd = lax.rsqrt(jnp.mean(xc * xc, axis=-1, keepdims=True) + EPS)
    return xc * rstd, rstd


def _conv_fwd(proj, w31, b31, ln_g, ln_b, after=None):
    t = proj.shape[0]

    def body(gv_ref, gg_ref, w_ref, b_ref, lg_ref, lb_ref, vc_ref, s_ref, vext_sc):
        i = pl.program_id(0)

        @pl.when(i == 0)
        def _():
            vext_sc[0:HALO, :] = jnp.zeros((HALO, D), f32)

        v = gv_ref[...] * _sigmoid(gg_ref[...])
        vext_sc[HALO:HALO + TM, :] = v
        ve = vext_sc[...]
        acc = jnp.zeros((TM, D), f32) + b_ref[...]
        for s in range(8):
            vs = ve if s == 0 else pltpu.roll(ve, s, 0)
            for m in range(HALO // 8):
                k = KC31 - 1 - (8 * m + s)
                if 0 <= k < KC31:
                    acc = acc + w_ref[k:k + 1, :] * vs[HALO - 8 * m:HALO - 8 * m + TM]
        vext_sc[0:HALO, :] = v[TM - HALO:TM]
        xhat, _ = _ln_stats(acc)
        ln = xhat * lg_ref[...] + lb_ref[...]
        vc_ref[...] = acc
        s_ref[...] = (ln * _sigmoid(ln)).astype(bf16)

    vec = pl.BlockSpec((1, D), lambda i: (0, 0))
    body, in_specs, args = _ordered(
        body,
        [pl.BlockSpec((TM, D), lambda i: (i, 2)),
         pl.BlockSpec((TM, D), lambda i: (i, 3)),
         pl.BlockSpec((KC31, D), lambda i: (0, 0)),
         vec, vec, vec],
        (proj, proj, w31, b31, ln_g, ln_b), after)
    return pl.pallas_call(
        body, name="conv_fwd", grid=(t // TM,),
        in_specs=in_specs,
        out_specs=[pl.BlockSpec((TM, D), lambda i: (i, 0))] * 2,
        out_shape=[jax.ShapeDtypeStruct((t, D), f32), jax.ShapeDtypeStruct((t, D), bf16)],
        scratch_shapes=[pltpu.VMEM((TM + HALO, D), f32)],
        compiler_params=_cp(1),
    )(*args)


def _merge_fwd(h, z, s, proj, wrp, wcp, bcp, wout):
    t = h.shape[0]

    def body(h_ref, z_ref, s_ref, ga_ref, gb_ref, wrp_ref, wcp_ref, bcp_ref, wout_ref, ho_ref):
        ya = jnp.dot(z_ref[...], wrp_ref[...], preferred_element_type=f32)
        yb = jnp.dot(s_ref[...], wcp_ref[...], preferred_element_type=f32) + bcp_ref[...]
        merged = _sigmoid(ga_ref[...]) * ya + _sigmoid(gb_ref[...]) * yb
        ho_ref[...] = h_ref[...] + jnp.dot(merged.astype(bf16), wout_ref[...], preferred_element_type=f32)

    row = pl.BlockSpec((TM, D), lambda i: (i, 0))
    wsq = pl.BlockSpec((D, D), lambda i: (0, 0))
    return pl.pallas_call(
        body, name="merge_fwd", grid=(t // TM,),
        in_specs=[row, row, row,
                  pl.BlockSpec((TM, D), lambda i: (i, 4)),
                  pl.BlockSpec((TM, D), lambda i: (i, 5)),
                  wsq, wsq, pl.BlockSpec((1, D), lambda i: (0, 0)), wsq],
        out_specs=row,
        out_shape=jax.ShapeDtypeStruct((t, D), f32),
        compiler_params=_cp(1),
    )(h, z, s, proj, proj, wrp, wcp, bcp, wout)


def _ffn_bwd(dh, h, g, gate, up, wgu, wd, name, after=None):
    t = h.shape[0]
    nj = 2

    def body(dh_ref, h_ref, g_ref, gate_ref, up_ref, wg_ref, wd_ref,
             dhi_ref, dgate_ref, dup_ref, a_ref, df_ref, dg_ref, dfb_sc, dn_sc):
        i = pl.program_id(0)
        j = pl.program_id(1)

        @pl.when(jnp.logical_and(i == 0, j == 0))
        def _():
            dg_ref[...] = jnp.zeros_like(dg_ref)

        @pl.when(j == 0)
        def _():
            dfb = (0.5 * dh_ref[...]).astype(bf16)
            dfb_sc[...] = dfb
            df_ref[...] = dfb
            dn_sc[...] = jnp.zeros_like(dn_sc)

        dfb = dfb_sc[...]
        for off, width in FSUB:
            cols = slice(off, off + width)
            da = _nt_dot(dfb, wd_ref[j, cols, :])
            gt = gate_ref[:, cols].astype(f32)
            uu = up_ref[:, cols].astype(f32)
            sg = _sigmoid(gt)
            silu = gt * sg
            a_ref[:, cols] = (silu * uu).astype(bf16)
            dgate_ref[:, cols] = (da * uu * (sg * (1.0 + gt * (1.0 - sg)))).astype(bf16)
            dup_ref[:, cols] = (da * silu).astype(bf16)
        dn_sc[...] += _nt_dot(dgate_ref[...], wg_ref[j]) + _nt_dot(dup_ref[...], wg_ref[2 + j])

        @pl.when(j == nj - 1)
        def _():
            dhin, dg = _rms_bwd(dn_sc[...], h_ref[...], g_ref[...])
            dhi_ref[...] = dh_ref[...] + dhin
            dg_ref[...] += dg

    rowd = pl.BlockSpec((TM, D), lambda i, j: (i, 0))
    rowf = pl.BlockSpec((TM, FS), lambda i, j: (i, j))
    vec = pl.BlockSpec((1, D), lambda i, j: (0, 0))
    body, in_specs, args = _ordered(
        body,
        [rowd, rowd, vec, rowf, rowf,
         RESIDENT, RESIDENT],
        (dh, h, g, gate, up, wgu, wd.reshape(nj, FS, D)), after)
    return pl.pallas_call(
        body, name=name, grid=(t // TM, nj),
        in_specs=in_specs,
        out_specs=[rowd, rowf, rowf, rowf, rowd, vec],
        out_shape=[jax.ShapeDtypeStruct((t, D), f32), jax.ShapeDtypeStruct((t, F), bf16),
                   jax.ShapeDtypeStruct((t, F), bf16), jax.ShapeDtypeStruct((t, F), bf16),
                   jax.ShapeDtypeStruct((t, D), bf16), jax.ShapeDtypeStruct((1, D), f32)],
        scratch_shapes=[pltpu.VMEM((TM, D), bf16), pltpu.VMEM((TM, D), f32)],
        compiler_params=_cp(2),
    )(*args)


def _big_tile(t):
    return max(k * TM for k in range(1, 6) if t % (k * TM) == 0)


ANY_SPEC = pl.BlockSpec(memory_space=pl.ANY)


def _tn_matmul(a, b, tk, tn, out_shape, out_block, out_map, name, base=None, after=None):
    t, kk = a.shape
    _, nn = b.shape
    tmm = _big_tile(t)
    nm = t // tmm

    def body(a_ref, b_ref, o_ref, acc_sc):
        m = pl.program_id(2)

        @pl.when(m == 0)
        def _():
            acc_sc[...] = jnp.zeros_like(acc_sc)

        acc_sc[...] += _tn_dot(a_ref[...], b_ref[...])

        @pl.when(m == nm - 1)
        def _():
            o_ref[...] = acc_sc[...].astype(o_ref.dtype)

    in_specs = [pl.BlockSpec((tmm, tk), lambda k, n, m: (m, k)),
                pl.BlockSpec((tmm, tn), lambda k, n, m: (m, n))]
    args, aliases = (a, b), {}
    if base is not None:
        body = (lambda inner: lambda a_ref, b_ref, base_ref, o_ref, acc_sc: inner(a_ref, b_ref, o_ref, acc_sc))(body)
        in_specs, args, aliases = in_specs + [ANY_SPEC], (a, b, base), {2: 0}
    if after is not None:
        body, in_specs, args = _ordered(body, in_specs, args, after)
        aliases = {k + _n_after(after): v for k, v in aliases.items()}
    return pl.pallas_call(
        body, name=name, grid=(kk // tk, nn // tn, nm),
        in_specs=in_specs,
        out_specs=pl.BlockSpec(out_block, out_map),
        out_shape=jax.ShapeDtypeStruct(out_shape, bf16),
        scratch_shapes=[pltpu.VMEM((tk, tn), f32)],
        input_output_aliases=aliases,
        compiler_params=_cp(3),
    )(*args)


def _merge_bwd(dh, z, s, proj, wrp, wcp, bcp, wout, after=None):
    t = dh.shape[0]

    def body(dh_ref, z_ref, s_ref, ga_ref, gb_ref, wrp_ref, wcp_ref, bcp_ref, wout_ref,
             dz_ref, ds_ref, dgab_ref, dhb_ref, mg_ref, dya_ref, dyb_ref, dbcp_ref):
        i = pl.program_id(0)

        @pl.when(i == 0)
        def _():
            dbcp_ref[...] = jnp.zeros_like(dbcp_ref)

        dhb = dh_ref[...].astype(bf16)
        dhb_ref[...] = dhb
        dmg = _nt_dot(dhb, wout_ref[...])
        ya = jnp.dot(z_ref[...], wrp_ref[...], preferred_element_type=f32)
        yb = jnp.dot(s_ref[...], wcp_ref[...], preferred_element_type=f32) + bcp_ref[...]
        sa = _sigmoid(ga_ref[...])
        sb = _sigmoid(gb_ref[...])
        mg_ref[...] = (sa * ya + sb * yb).astype(bf16)
        dgab_ref[:, 0:D] = (dmg * ya * sa * (1.0 - sa)).astype(bf16)
        dgab_ref[:, D:2 * D] = (dmg * yb * sb * (1.0 - sb)).astype(bf16)
        dya = dmg * sa
        dyb = dmg * sb
        dbcp_ref[...] += jnp.sum(dyb, axis=0, keepdims=True)
        dyab = dya.astype(bf16)
        dybb = dyb.astype(bf16)
        dya_ref[...] = dyab
        dyb_ref[...] = dybb
        dz_ref[...] = _nt_dot(dyab, wrp_ref[...])
        ds_ref[...] = _nt_dot(dybb, wcp_ref[...])

    row = pl.BlockSpec((TM, D), lambda i: (i, 0))
    wsq = pl.BlockSpec((D, D), lambda i: (0, 0))
    vec = pl.BlockSpec((1, D), lambda i: (0, 0))
    rowb = jax.ShapeDtypeStruct((t, D), bf16)
    body, in_specs, args = _ordered(
        body,
        [row, row, row,
         pl.BlockSpec((TM, D), lambda i: (i, 4)),
         pl.BlockSpec((TM, D), lambda i: (i, 5)),
         wsq, wsq, vec, wsq],
        (dh, z, s, proj, proj, wrp, wcp, bcp, wout), after)
    return pl.pallas_call(
        body, name="merge_bwd", grid=(t // TM,),
        in_specs=in_specs,
        out_specs=[row, row,
                   pl.BlockSpec((TM, 2 * D), lambda i: (i, 2)),
                   row, row, row, row, vec],
        out_shape=[jax.ShapeDtypeStruct((t, D), f32), jax.ShapeDtypeStruct((t, D), f32),
                   jax.ShapeDtypeStruct((t, NIN), bf16),
                   rowb, rowb, rowb, rowb, jax.ShapeDtypeStruct((1, D), f32)],
        compiler_params=_cp(1),
    )(*args)


def _conv_bwd(ds, vc, proj, dproj, w31, ln_g, ln_b, after=None):
    t = ds.shape[0]
    nt = t // TM
    hb = TM // HALO

    rb = 16
    nb = TM // rb
    taps = [(KC31 - 1 - (8 * m + s), s, m) for s in range(8) for m in range(HALO // 8)
            if 0 <= KC31 - 1 - (8 * m + s) < KC31]

    def groups(a):
        return jnp.sum(a.reshape(rb // 8, 8, D), axis=0)

    def body(ds_ref, vc_ref, gv_ref, gg_ref, gvp_ref, ggp_ref, dpin_ref, w_ref, lg_ref, lb_ref,
             dgvg_ref, dw_ref, db_ref, dlg_ref, dlb_ref, dext_sc, vext_sc, rot_sc, dwacc_sc, small_sc, wb_sc):
        del dpin_ref
        i = pl.program_id(0)
        tile = nt - 1 - i

        @pl.when(i == 0)
        def _():
            dext_sc[TM:TM + HALO, :] = jnp.zeros((HALO, D), f32)
            dwacc_sc[...] = jnp.zeros_like(dwacc_sc)
            small_sc[...] = jnp.zeros_like(small_sc)

        lg = lg_ref[...]
        lb = lb_ref[...]

        xhat, rstd = _ln_stats(vc_ref[...])
        ln = xhat * lg + lb
        sg = _sigmoid(ln)
        dln = ds_ref[...] * (sg * (1.0 + ln * (1.0 - sg)))
        dxh = dln * lg
        dvc = rstd * (dxh - jnp.mean(dxh, axis=-1, keepdims=True)
                      - xhat * jnp.mean(dxh * xhat, axis=-1, keepdims=True))
        small_sc[0] += jnp.sum((dln * xhat).reshape(TM // 8, 8, D), axis=0)
        small_sc[1] += jnp.sum(dln.reshape(TM // 8, 8, D), axis=0)
        small_sc[2] += jnp.sum(dvc.reshape(TM // 8, 8, D), axis=0)
        dext_sc[0:TM, :] = dvc
        vext_sc[HALO:HALO + TM, :] = gv_ref[...] * _sigmoid(gg_ref[...])
        vext_sc[0:HALO, :] = jnp.where(tile > 0, gvp_ref[...] * _sigmoid(ggp_ref[...]), 0.0)

        @pl.when(i == 0)
        def _():
            for k in range(KC31):
                wb_sc[k] = jnp.broadcast_to(w_ref[k:k + 1, :], (8, D))

        for s in range(1, 8):
            rot_sc[s - 1] = pltpu.roll(dext_sc[...], TM + HALO - s, 0)

        def dv_block(b, carry):
            rows = pl.ds(pl.multiple_of(b * rb, rb), rb)
            acc = jnp.zeros((rb, D), f32)
            for k, s, m in taps:
                src = pl.ds(pl.multiple_of(b * rb + 8 * m, 8), rb)
                slab = dext_sc[src, :] if s == 0 else rot_sc[s - 1, src, :]
                acc = acc + (slab.reshape(rb // 8, 8, D) * wb_sc[k]).reshape(rb, D)
            sgg = _sigmoid(gg_ref[rows, :])
            dgvg_ref[rows, 0:D] = (acc * sgg).astype(bf16)
            dgvg_ref[rows, D:2 * D] = (acc * gv_ref[rows, :] * sgg * (1.0 - sgg)).astype(bf16)
            return carry

        lax.fori_loop(0, nb, dv_block, 0)

        for s in range(1, 8):
            rot_sc[s - 1] = pltpu.roll(vext_sc[...], s, 0)
        for first in range(0, len(taps), 3):
            trio = taps[first:first + 3]

            def dw_block(b, accs, trio=trio):
                rows = pl.ds(pl.multiple_of(b * rb, rb), rb)
                dvc_blk = dext_sc[rows, :]
                out = []
                for acc, (k, s, m) in zip(accs, trio):
                    src = pl.ds(pl.multiple_of(b * rb + HALO - 8 * m, 8), rb)
                    slab = vext_sc[src, :] if s == 0 else rot_sc[s - 1, src, :]
                    out.append(acc + groups(dvc_blk * slab))
                return tuple(out)

            sums = lax.fori_loop(0, nb, dw_block, tuple(jnp.zeros((8, D), f32) for _ in trio))
            for acc, (k, s, m) in zip(sums, trio):
                dwacc_sc[k] += acc
        dext_sc[TM:TM + HALO, :] = dext_sc[0:HALO, :]

        @pl.when(i == nt - 1)
        def _():
            for k in range(KC31):
                dw_ref[k:k + 1, :] = jnp.sum(dwacc_sc[k], axis=0, keepdims=True)
            dlg_ref[...] = jnp.sum(small_sc[0], axis=0, keepdims=True)
            dlb_ref[...] = jnp.sum(small_sc[1], axis=0, keepdims=True)
            db_ref[...] = jnp.sum(small_sc[2], axis=0, keepdims=True)

    rev = lambda i: (nt - 1 - i, 0)
    vec = pl.BlockSpec((1, D), lambda i: (0, 0))
    halo_row = lambda i: jnp.maximum((nt - 1 - i) * hb - 1, 0)
    body, in_specs, args = _ordered(
        body,
        [pl.BlockSpec((TM, D), rev),
         pl.BlockSpec((TM, D), rev),
         pl.BlockSpec((TM, D), lambda i: (nt - 1 - i, 2)),
         pl.BlockSpec((TM, D), lambda i: (nt - 1 - i, 3)),
         pl.BlockSpec((HALO, D), lambda i: (halo_row(i), 2)),
         pl.BlockSpec((HALO, D), lambda i: (halo_row(i), 3)),
         pl.BlockSpec(memory_space=pl.ANY),
         pl.BlockSpec((KC31, D), lambda i: (0, 0)),
         vec, vec],
        (ds, vc, proj, proj, proj, proj, dproj, w31, ln_g, ln_b), after)
    return pl.pallas_call(
        body, name="conv_bwd", grid=(nt,),
        in_specs=in_specs,
        out_specs=[
            pl.BlockSpec((TM, 2 * D), lambda i: (nt - 1 - i, 1)),
            pl.BlockSpec((KC31, D), lambda i: (0, 0)),
            vec, vec, vec,
        ],
        out_shape=[jax.ShapeDtypeStruct((t, NIN), bf16),
                   jax.ShapeDtypeStruct((KC31, D), f32),
                   jax.ShapeDtypeStruct((1, D), f32), jax.ShapeDtypeStruct((1, D), f32),
                   jax.ShapeDtypeStruct((1, D), f32)],
        scratch_shapes=[pltpu.VMEM((TM + HALO, D), f32), pltpu.VMEM((TM + HALO, D), f32),
                        pltpu.VMEM((7, TM + HALO, D), f32), pltpu.VMEM((KC31, 8, D), f32),
                        pltpu.VMEM((3, 8, D), f32), pltpu.VMEM((KC31, 8, D), f32)],
        input_output_aliases={6 + _n_after(after): 0},
        compiler_params=_cp(1),
    )(*args)


def _rnn_bwd(dz, xr, hr, gates, proj, dproj, cw, wa, wx, lam):
    t = dz.shape[0]
    nt = t // TM
    ng = TM // 8
    hq = HD // NCHIP

    def body(dz_ref, xr_ref, hr_ref, hrp_ref, x_ref, xp_ref, y_ref, dpin_ref,
             cw_ref, wa_ref, gates_ref, wx_ref, lam_ref,
             dxy_ref, dwa_ref, dwx_ref, dcw_ref, dcb_ref, dba_ref, dbx_ref, dlam_ref,
             anext_sc, gcarry_sc, dext_sc, xext_sc, m_sc, g_sc, dwa_sc, dwx_sc, dsp_sc):
        del dpin_ref
        i = pl.program_id(0)
        tile = nt - 1 - i

        @pl.when(i == 0)
        def _():
            anext_sc[...] = jnp.zeros_like(anext_sc)
            gcarry_sc[...] = jnp.zeros_like(gcarry_sc)
            dext_sc[TM:TM + 8, :] = jnp.zeros((8, D), f32)
            dwa_sc[...] = jnp.zeros_like(dwa_sc)
            dwx_sc[...] = jnp.zeros_like(dwx_sc)
            dsp_sc[...] = jnp.zeros_like(dsp_sc)
            dcw_ref[...] = jnp.zeros_like(dcw_ref)
            dcb_ref[...] = jnp.zeros_like(dcb_ref)
            dba_ref[...] = jnp.zeros_like(dba_ref)
            dbx_ref[...] = jnp.zeros_like(dbx_ref)

        xr = xr_ref[...]
        hr = hr_ref[...]
        dz = dz_ref[...]
        gel, dgel = _gelu_and_grad(y_ref[...])
        dxy_ref[:, D:2 * D] = (dz * hr * dgel).astype(bf16)
        ra, ii, a, sq = gates_ref[0], gates_ref[1], gates_ref[2], gates_ref[3]
        sp = _softplus(-lam_ref[...])

        row = _row_ids((TM, D))
        _to_lane_tiles(m_sc, jnp.where(row == TM - 1, anext_sc[...], pltpu.roll(a, TM - 1, 0)))
        anext_sc[...] = a[0:1, :]
        _to_lane_tiles(g_sc, dz * gel)
        gcarry_sc[...] = _chain_scan(m_sc, g_sc, gcarry_sc[...], reverse=True)
        gg = _from_lane_tiles(g_sc)

        hlast = jnp.where(tile > 0, hrp_ref[7:8, :], 0.0)
        hprev = jnp.where(row == 0, hlast, pltpu.roll(hr, 1, 0))
        d_a = gg * hprev
        dsq = gg * ii * xr
        dii = gg * sq * xr
        dxr = gg * sq * ii
        dlog = d_a * a - dsq * (a * a / sq)
        dsp_sc[...] += jnp.sum(dlog * (-8.0 * ra), axis=0, keepdims=True)
        dpa = dlog * (-8.0 * sp) * ra * (1.0 - ra)
        dpx = dii * ii * (1.0 - ii)
        dba_ref[...] += jnp.sum(dpa, axis=0, keepdims=True)
        dbx_ref[...] += jnp.sum(dpx, axis=0, keepdims=True)
        dpab = dpa.astype(bf16)
        dpxb = dpx.astype(bf16)
        xrb = xr.astype(bf16)
        back = []
        for hh in range(NHEAD):
            cols = slice(hh * HD, (hh + 1) * HD)
            back.append(_nt_dot(dpab[:, cols], wa_ref[hh]) + _nt_dot(dpxb[:, cols], wx_ref[hh]))
            dwa_sc[hh] += _tn_dot(xrb[:, cols], dpab[:, cols])
            dwx_sc[hh] += _tn_dot(xrb[:, cols], dpxb[:, cols])
        dxr = dxr + jnp.concatenate(back, axis=1)

        dext_sc[0:TM, :] = dxr
        de = dext_sc[...]
        dx = cw_ref[KC4 - 1:KC4, :] * dxr
        for k in range(KC4 - 1):
            dx = dx + cw_ref[k:k + 1, :] * pltpu.roll(de, TM + 8 - (KC4 - 1 - k), 0)[0:TM]
        dext_sc[TM:TM + 8, :] = dxr[0:8]
        dxy_ref[:, 0:D] = dx.astype(bf16)

        x = x_ref[...]
        xext_sc[0:8, :] = jnp.where(tile > 0, xp_ref[...], 0.0)
        xext_sc[8:8 + TM, :] = x
        xe = xext_sc[...]
        dcw_ref[KC4 - 1:KC4, :] += jnp.sum(dxr * x, axis=0, keepdims=True)
        for k in range(KC4 - 1):
            xs = pltpu.roll(xe, KC4 - 1 - k, 0)[8:8 + TM]
            dcw_ref[k:k + 1, :] += jnp.sum(dxr * xs, axis=0, keepdims=True)
        dcb_ref[...] += jnp.sum(dxr, axis=0, keepdims=True)

        @pl.when(i == nt - 1)
        def _():
            for hh in range(NHEAD):
                for qc in range(NCHIP):
                    dwa_ref[qc, hh] = dwa_sc[hh, qc * hq:(qc + 1) * hq, :].astype(bf16)
                    dwx_ref[qc, hh] = dwx_sc[hh, qc * hq:(qc + 1) * hq, :].astype(bf16)
            dlam_ref[...] = -dsp_sc[...] * _sigmoid(-lam_ref[...])

    rev = lambda i: (nt - 1 - i, 0)
    vec = pl.BlockSpec((1, D), lambda i: (0, 0))
    prev8 = lambda i: jnp.maximum((nt - 1 - i) * ng - 1, 0)
    wblk = pl.BlockSpec((NHEAD, HD, HD), lambda i: (0, 0, 0))
    gblk = pl.BlockSpec((NCHIP, NHEAD, hq, HD), lambda i: (0, 0, 0, 0))
    return pl.pallas_call(
        body, name="rnn_bwd", grid=(nt,),
        in_specs=[
            pl.BlockSpec((TM, D), rev),
            pl.BlockSpec((TM, D), rev),
            pl.BlockSpec((TM, D), rev),
            pl.BlockSpec((8, D), lambda i: (prev8(i), 0)),
            pl.BlockSpec((TM, D), lambda i: (nt - 1 - i, 0)),
            pl.BlockSpec((8, D), lambda i: (prev8(i), 0)),
            pl.BlockSpec((TM, D), lambda i: (nt - 1 - i, 1)),
            pl.BlockSpec(memory_space=pl.ANY),
            pl.BlockSpec((KC4, D), lambda i: (0, 0)),
            wblk, pl.BlockSpec((4, TM, D), lambda i: (0, nt - 1 - i, 0)), wblk, vec,
        ],
        out_specs=[
            pl.BlockSpec((TM, 2 * D), lambda i: (nt - 1 - i, 0)),
            gblk, gblk,
            pl.BlockSpec((KC4, D), lambda i: (0, 0)),
            vec, vec, vec, vec,
        ],
        out_shape=[jax.ShapeDtypeStruct((t, NIN), bf16),
                   jax.ShapeDtypeStruct((NCHIP, NHEAD, hq, HD), bf16),
                   jax.ShapeDtypeStruct((NCHIP, NHEAD, hq, HD), bf16),
                   jax.ShapeDtypeStruct((KC4, D), f32),
                   jax.ShapeDtypeStruct((1, D), f32), jax.ShapeDtypeStruct((1, D), f32),
                   jax.ShapeDtypeStruct((1, D), f32), jax.ShapeDtypeStruct((1, D), f32)],
        scratch_shapes=[pltpu.VMEM((1, D), f32), pltpu.VMEM((1, D), f32),
                        pltpu.VMEM((TM + 8, D), f32), pltpu.VMEM((TM + 8, D), f32),
                        pltpu.VMEM((LT, TM, 128), f32), pltpu.VMEM((LT, TM, 128), f32),
                        pltpu.VMEM((NHEAD, HD, HD), f32), pltpu.VMEM((NHEAD, HD, HD), f32),
                        pltpu.VMEM((1, D), f32)],
        input_output_aliases={7: 0},
        compiler_params=_cp(1),
    )(dz, xr, hr, hr, proj, proj, proj, dproj, cw, wa, gates, wx, lam)


def _inproj_bwd(dproj, dh, h, g, win, after=None):
    t = h.shape[0]
    tn = NIN // NCHIP
    nj = NIN // tn

    def body(dp_ref, dh_ref, h_ref, g_ref, w_ref, dhi_ref, dg_ref, db_ref, dn_sc):
        i = pl.program_id(0)
        j = pl.program_id(1)

        @pl.when(jnp.logical_and(i == 0, j == 0))
        def _():
            dg_ref[...] = jnp.zeros_like(dg_ref)
            db_ref[...] = jnp.zeros_like(db_ref)

        @pl.when(j == 0)
        def _():
            dn_sc[...] = jnp.zeros_like(dn_sc)

        dp = dp_ref[...]
        dn_sc[...] += _nt_dot(dp, w_ref[j])
        db_ref[j] += jnp.sum(dp.astype(f32), axis=0, keepdims=True)

        @pl.when(j == nj - 1)
        def _():
            dhin, dg = _rms_bwd(dn_sc[...], h_ref[...], g_ref[...])
            dhi_ref[...] = dh_ref[...] + dhin
            dg_ref[...] += dg

    rowd = pl.BlockSpec((TM, D), lambda i, j: (i, 0))
    vec = pl.BlockSpec((1, D), lambda i, j: (0, 0))
    body, in_specs, args = _ordered(
        body,
        [pl.BlockSpec((TM, tn), lambda i, j: (i, j)), rowd, rowd, vec,
         RESIDENT],
        (dproj, dh, h, g, win), after)
    return pl.pallas_call(
        body, name="inproj_bwd", grid=(t // TM, nj),
        in_specs=in_specs,
        out_specs=[rowd, vec, pl.BlockSpec((nj, 1, tn), lambda i, j: (0, 0, 0))],
        out_shape=[jax.ShapeDtypeStruct((t, D), f32), jax.ShapeDtypeStruct((1, D), f32),
                   jax.ShapeDtypeStruct((nj, 1, tn), f32)],
        scratch_shapes=[pltpu.VMEM((TM, D), f32)],
        compiler_params=_cp(2),
    )(*args)


def _ffn_gu_grad(n, dgate, dup, tag, after=None):
    half = _tn_matmul(n, dgate, D, FS, (NCHIP, D, FS), (None, D, FS), lambda k, nn, m: (nn, 0, 0),
                      tag + "_dwg", after=after)
    return _tn_matmul(n, dup, D, FS, (NCHIP, D, FS), (None, D, FS), lambda k, nn, m: (2 + nn, 0, 0),
                      tag + "_dwu", base=half)


def _ffn_down_grad(a, df, tag, after=None):
    return _tn_matmul(a, df, FS, D, (F, D), (FS, D), lambda k, nn, m: (k, 0), tag + "_dwd", after=after)


def _square_grad(a, b, name):
    return _tn_matmul(a, b, D, D, (D, D), (D, D), lambda k, nn, m: (0, 0), name)


ANY = pl.BlockSpec(memory_space=pl.ANY)


def _place():
    x, y, c = lax.axis_index("x"), lax.axis_index("y"), lax.axis_index("c")
    chips = [(1 - x, y), (x, 1 - y), (1 - x, 1 - y)]
    return x, y, c, chips


def _chip_id(chip):
    return 2 * chip[0] + chip[1]


def _cast_into_slot(w2d, qc, dtype, name, after=None):
    r, cc = w2d.shape
    hr = r // 2

    def body(qc_ref, *refs):
        del qc_ref
        w_ref, o_ref = refs[-2:]
        o_ref[...] = w_ref[...].astype(dtype)

    in_specs, args = [pl.BlockSpec((hr, cc), lambda h, qc_ref: (h, 0))], (w2d,)
    if after is not None:
        in_specs, args = [ANY_SPEC] + in_specs, (after,) + args
    return pl.pallas_call(
        body, name=name,
        grid_spec=pltpu.PrefetchScalarGridSpec(
            num_scalar_prefetch=1, grid=(2,),
            in_specs=in_specs,
            out_specs=pl.BlockSpec((None, None, hr, cc), lambda h, qc_ref: (qc_ref[0], h, 0, 0))),
        out_shape=jax.ShapeDtypeStruct((NCHIP, 2, hr, cc), dtype),
        compiler_params=_cp(1),
    )(qc, *args)


def _place_pack(pack, qc):
    def body(qc_ref, p_ref, o_ref):
        del qc_ref
        o_ref[...] = p_ref[...]

    return pl.pallas_call(
        body, name="place_pack",
        grid_spec=pltpu.PrefetchScalarGridSpec(
            num_scalar_prefetch=1, grid=(1,),
            in_specs=[pl.BlockSpec(pack.shape, lambda i, qc_ref: (0, 0))],
            out_specs=pl.BlockSpec((None,) + pack.shape, lambda i, qc_ref: (2 * qc_ref[0] + qc_ref[1], 0, 0))),
        out_shape=jax.ShapeDtypeStruct((8,) + pack.shape, pack.dtype),
        compiler_params=_cp(1),
    )(qc, pack)


def _pair_add(parts, gots, qc, name):
    n = len(parts)

    def body(qc_ref, *refs):
        s = pl.program_id(0)
        for a in range(n):
            val = (refs[a][...].astype(f32) + refs[n + a][...].astype(f32)).astype(bf16)
            refs[2 * n + a][...] = val

            @pl.when(s == qc_ref[0])
            def _(val=val, land_ref=refs[3 * n + a]):
                land_ref[...] = val

    shapes = [p.shape[2:] for p in parts]
    mine = [pl.BlockSpec((None, None) + sh, lambda s, qc_ref: (s, qc_ref[1], 0, 0)) for sh in shapes]
    block = [pl.BlockSpec((None,) + sh, lambda s, qc_ref: (s, 0, 0)) for sh in shapes]
    own = [pl.BlockSpec((None,) + sh, lambda s, qc_ref: (qc_ref[0], 0, 0)) for sh in shapes]
    outs = pl.pallas_call(
        body, name=name,
        grid_spec=pltpu.PrefetchScalarGridSpec(
            num_scalar_prefetch=1, grid=(NCHIP,), in_specs=mine + block, out_specs=block + own),
        out_shape=[jax.ShapeDtypeStruct((NCHIP,) + sh, bf16) for sh in shapes] * 2,
        compiler_params=_cp(1),
    )(qc, *parts, *gots)
    return list(outs[:n]), list(outs[n:])


def _sum_chips(gots, name):
    n = len(gots)

    def body(*refs):
        for a in range(n):
            acc = refs[a][0].astype(f32)
            for s in range(1, NCHIP):
                acc = acc + refs[a][s].astype(f32)
            refs[n + a][...] = acc

    return list(pl.pallas_call(
        body, name=name, grid=(1,),
        in_specs=[pl.BlockSpec(g.shape, lambda i: (0, 0, 0)) for g in gots],
        out_specs=[pl.BlockSpec(g.shape[1:], lambda i: (0, 0)) for g in gots],
        out_shape=[jax.ShapeDtypeStruct(g.shape[1:], f32) for g in gots],
        compiler_params=_cp(1),
    )(*gots))


def _pair_share(halves, name, after=None):
    n = len(halves)
    extra = () if after is None else (after,)

    def body(*refs):
        refs = refs[len(extra):]
        ins, outs = refs[:n], refs[n:2 * n]
        send_sems, recv_sems = refs[2 * n:]
        x, y, c, _ = _place()
        copies = []
        for a in range(n):
            cp = pltpu.make_async_remote_copy(
                src_ref=ins[a], dst_ref=outs[a], send_sem=send_sems.at[a], recv_sem=recv_sems.at[a],
                device_id=(x, y, 1 - c), device_id_type=MESH)
            cp.start()
            copies.append(cp)
        for cp in copies:
            cp.wait()

    return pl.pallas_call(
        body, name=name,
        in_specs=[ANY] * (len(extra) + n), out_specs=[ANY] * n,
        out_shape=[jax.ShapeDtypeStruct(s.shape, s.dtype) for s in halves],
        scratch_shapes=[pltpu.SemaphoreType.DMA((n,)), pltpu.SemaphoreType.DMA((n,))],
    )(*extra, *halves)


def _all_copy(buf_ref, send_ref, recv_ref, k, x, y, c, landing):
    px, py, pc = (1 - x if k & 4 else x, 1 - y if k & 2 else y, 1 - c if k & 1 else c)
    me = 4 * x + 2 * y + c
    there = 4 * px + 2 * py + pc
    return pltpu.make_async_remote_copy(
        src_ref=buf_ref.at[me], dst_ref=buf_ref.at[there if landing else me],
        send_sem=send_ref.at[k - 1], recv_sem=recv_ref.at[k - 1],
        device_id=(px, py, pc), device_id_type=MESH)


def _gather_all_start(buf, name):
    def body(in_ref, send, recv, thru, token):
        del thru
        x, y, c, _ = _place()
        for k in range(1, 8):
            _all_copy(in_ref, send, recv, k, x, y, c, False).start()
        token[...] = jnp.zeros_like(token)

    return pl.pallas_call(
        body, name=name,
        in_specs=[HBM],
        out_specs=[SEM, SEM, HBM, pl.BlockSpec(memory_space=pltpu.VMEM)],
        out_shape=[pltpu.SemaphoreType.DMA((7,)), pltpu.SemaphoreType.DMA((7,)),
                   pltpu.HBM(buf.shape, buf.dtype), jax.ShapeDtypeStruct((8, 128), f32)],
        input_output_aliases={0: 2},
        compiler_params=pltpu.CompilerParams(has_side_effects=EFFECT),
    )(_in_hbm(buf))


def _gather_all_wait(send, recv, buf, after, name):
    def body(in_ref, send_r, recv_r, after_ref, out_ref):
        del after_ref, out_ref
        x, y, c, _ = _place()
        for k in range(1, 8):
            cp = _all_copy(in_ref, send_r, recv_r, k, x, y, c, True)
            cp.wait_send()
            cp.wait_recv()

    return pl.pallas_call(
        body, name=name,
        in_specs=[HBM, SEM, SEM, ANY],
        out_specs=HBM,
        out_shape=pltpu.HBM(buf.shape, buf.dtype),
        input_output_aliases={0: 0},
        compiler_params=pltpu.CompilerParams(has_side_effects=EFFECT),
    )(buf, send, recv, after)


HBM = pl.BlockSpec(memory_space=pltpu.HBM)
SEM = pl.BlockSpec(memory_space=pltpu.SEMAPHORE)
EFFECT = pltpu.SideEffectType.DATAFLOW_SIDE_EFFECTING
N_PEER = 3


def _in_hbm(a):
    return pltpu.with_memory_space_constraint(a, pltpu.HBM)


def _gather_copy(buf_ref, send_ref, recv_ref, j, chip, q, c, landing_chip):
    return pltpu.make_async_remote_copy(
        src_ref=buf_ref.at[q, c], dst_ref=buf_ref.at[landing_chip, c],
        send_sem=send_ref.at[j], recv_sem=recv_ref.at[j],
        device_id=(chip[0], chip[1], c), device_id_type=MESH)


def _gather_start(bufs, name):
    n = len(bufs)

    def body(*refs):
        ins = refs[:n]
        send, recv = refs[n:2 * n], refs[2 * n:3 * n]
        token = refs[4 * n]
        x, y, c, chips = _place()
        q = 2 * x + y
        for a in range(n):
            for j, chip in enumerate(chips):
                _gather_copy(ins[a], send[a], recv[a], j, chip, q, c, q).start()
        token[...] = jnp.zeros_like(token)

    sems = [pltpu.SemaphoreType.DMA((N_PEER,))] * (2 * n)
    outs = pl.pallas_call(
        body, name=name,
        in_specs=[HBM] * n,
        out_specs=[SEM] * (2 * n) + [HBM] * n + [pl.BlockSpec(memory_space=pltpu.VMEM)],
        out_shape=sems + [pltpu.HBM(b.shape, b.dtype) for b in bufs] + [jax.ShapeDtypeStruct((8, 128), f32)],
        input_output_aliases={a: 2 * n + a for a in range(n)},
        compiler_params=pltpu.CompilerParams(has_side_effects=EFFECT),
    )(*[_in_hbm(b) for b in bufs])
    return list(outs[:n]), list(outs[n:2 * n]), list(outs[2 * n:3 * n]), outs[3 * n]


def _gather_wait(send, recv, bufs, after, name):
    n = len(bufs)

    def body(*refs):
        ins = refs[:n]
        send_r, recv_r = refs[n:2 * n], refs[2 * n:3 * n]
        x, y, c, chips = _place()
        q = 2 * x + y
        for a in range(n):
            for j, chip in enumerate(chips):
                cp = _gather_copy(ins[a], send_r[a], recv_r[a], j, chip, q, c, _chip_id(chip))
                cp.wait_send()
                cp.wait_recv()

    afters = after if isinstance(after, (tuple, list)) else (after,)
    outs = pl.pallas_call(
        body, name=name,
        in_specs=[HBM] * n + [SEM] * (2 * n) + [ANY] * len(afters),
        out_specs=[HBM] * n,
        out_shape=[pltpu.HBM(b.shape, b.dtype) for b in bufs],
        input_output_aliases={a: a for a in range(n)},
        compiler_params=pltpu.CompilerParams(has_side_effects=EFFECT),
    )(*bufs, *send, *recv, *afters)
    return list(outs)


def _forward_halves(bufs, name):
    n = len(bufs)

    def body(*refs):
        outs = refs[n:2 * n]
        send_sems, recv_sems = refs[2 * n:]
        x, y, c, chips = _place()
        sibling = (x, y, 1 - c)

        def remote(a, j, blk):
            return pltpu.make_async_remote_copy(src_ref=blk, dst_ref=blk, send_sem=send_sems.at[a, j],
                                                recv_sem=recv_sems.at[a, j], device_id=sibling,
                                                device_id_type=MESH)

        sent = []
        for a in range(n):
            for j, chip in enumerate(chips):
                cp = remote(a, j, outs[a].at[_chip_id(chip), c])
                cp.start()
                sent.append(cp)
        for a in range(n):
            for j, chip in enumerate(chips):
                remote(a, j, outs[a].at[_chip_id(chip), 1 - c]).wait_recv()
        for cp in sent:
            cp.wait_send()

    return pl.pallas_call(
        body, name=name,
        in_specs=[ANY] * n, out_specs=[ANY] * n,
        out_shape=[jax.ShapeDtypeStruct(s.shape, s.dtype) for s in bufs],
        scratch_shapes=[pltpu.SemaphoreType.DMA((n, N_PEER)), pltpu.SemaphoreType.DMA((n, N_PEER))],
        input_output_aliases={a: a for a in range(n)},
    )(*bufs)


def _reduce_copy(sum_ref, land_ref, send_ref, recv_ref, j, chip, q, c, landing_chip):
    return pltpu.make_async_remote_copy(
        src_ref=sum_ref.at[_chip_id(chip)], dst_ref=land_ref.at[landing_chip],
        send_sem=send_ref.at[j], recv_sem=recv_ref.at[j],
        device_id=(chip[0], chip[1], c), device_id_type=MESH)


def _reduce_start(sums, lands, name):
    n = len(sums)

    def body(*refs):
        s_in, l_in = refs[:n], refs[n:2 * n]
        send, recv = refs[2 * n:3 * n], refs[3 * n:4 * n]
        token = refs[6 * n]
        x, y, c, chips = _place()
        q = 2 * x + y
        for a in range(n):
            for j, chip in enumerate(chips):
                _reduce_copy(s_in[a], l_in[a], send[a], recv[a], j, chip, q, c, q).start()
        token[...] = jnp.zeros_like(token)

    sems = [pltpu.SemaphoreType.DMA((N_PEER,))] * (2 * n)
    outs = pl.pallas_call(
        body, name=name,
        in_specs=[HBM] * (2 * n),
        out_specs=[SEM] * (2 * n) + [HBM] * (2 * n) + [pl.BlockSpec(memory_space=pltpu.VMEM)],
        out_shape=sems + [pltpu.HBM(b.shape, b.dtype) for b in list(sums) + list(lands)]
        + [jax.ShapeDtypeStruct((8, 128), f32)],
        input_output_aliases={a: 2 * n + a for a in range(2 * n)},
        compiler_params=pltpu.CompilerParams(has_side_effects=EFFECT),
    )(*[_in_hbm(b) for b in list(sums) + list(lands)])
    return (list(outs[:n]), list(outs[n:2 * n]), list(outs[2 * n:3 * n]), list(outs[3 * n:4 * n]),
            outs[4 * n])


def _reduce_wait(send, recv, sums, lands, after, name):
    n = len(sums)

    def body(*refs):
        s_in, l_in = refs[:n], refs[n:2 * n]
        send_r, recv_r = refs[2 * n:3 * n], refs[3 * n:4 * n]
        x, y, c, chips = _place()
        q = 2 * x + y
        for a in range(n):
            for j, chip in enumerate(chips):
                cp = _reduce_copy(s_in[a], l_in[a], send_r[a], recv_r[a], j, chip, q, c, _chip_id(chip))
                cp.wait_send()
                cp.wait_recv()

    afters = after if isinstance(after, (tuple, list)) else (after,)
    outs = pl.pallas_call(
        body, name=name,
        in_specs=[HBM] * (2 * n) + [SEM] * (2 * n) + [ANY] * len(afters),
        out_specs=[HBM] * (2 * n),
        out_shape=[pltpu.HBM(b.shape, b.dtype) for b in list(sums) + list(lands)],
        input_output_aliases={a: a for a in range(2 * n)},
        compiler_params=pltpu.CompilerParams(has_side_effects=EFFECT),
    )(*sums, *lands, *send, *recv, *afters)
    return list(outs[n:])


def _sibling_copy(part_ref, land_ref, send_ref, recv_ref, x, y, c):
    return pltpu.make_async_remote_copy(
        src_ref=part_ref.at[:, 1 - c], dst_ref=land_ref, send_sem=send_ref.at[0], recv_sem=recv_ref.at[0],
        device_id=(x, y, 1 - c), device_id_type=MESH)


def _pair_exchange_start(parts, name):
    n = len(parts)
    lands = [lax.empty((NCHIP,) + p.shape[2:], p.dtype) for p in parts]

    def body(*refs):
        p_in, l_in = refs[:n], refs[n:2 * n]
        send, recv = refs[2 * n:3 * n], refs[3 * n:4 * n]
        token = refs[6 * n]
        x, y, c, _ = _place()
        for a in range(n):
            _sibling_copy(p_in[a], l_in[a], send[a], recv[a], x, y, c).start()
        token[...] = jnp.zeros_like(token)

    sems = [pltpu.SemaphoreType.DMA((1,))] * (2 * n)
    outs = pl.pallas_call(
        body, name=name,
        in_specs=[HBM] * (2 * n),
        out_specs=[SEM] * (2 * n) + [HBM] * (2 * n) + [pl.BlockSpec(memory_space=pltpu.VMEM)],
        out_shape=sems + [pltpu.HBM(b.shape, b.dtype) for b in list(parts) + lands]
        + [jax.ShapeDtypeStruct((8, 128), f32)],
        input_output_aliases={a: 2 * n + a for a in range(2 * n)},
        compiler_params=pltpu.CompilerParams(has_side_effects=EFFECT),
    )(*[_in_hbm(b) for b in list(parts) + lands])
    return (list(outs[:n]), list(outs[n:2 * n]), list(outs[2 * n:3 * n]), list(outs[3 * n:4 * n]),
            outs[4 * n])


def _pair_exchange_wait(send, recv, parts, lands, after, name):
    n = len(parts)

    def body(*refs):
        p_in, l_in = refs[:n], refs[n:2 * n]
        send_r, recv_r = refs[2 * n:3 * n], refs[3 * n:4 * n]
        x, y, c, _ = _place()
        for a in range(n):
            cp = _sibling_copy(p_in[a], l_in[a], send_r[a], recv_r[a], x, y, c)
            cp.wait_send()
            cp.wait_recv()

    outs = pl.pallas_call(
        body, name=name,
        in_specs=[HBM] * (2 * n) + [SEM] * (2 * n) + [ANY],
        out_specs=[HBM] * (2 * n),
        out_shape=[pltpu.HBM(b.shape, b.dtype) for b in list(parts) + list(lands)],
        input_output_aliases={a: a for a in range(2 * n)},
        compiler_params=pltpu.CompilerParams(has_side_effects=EFFECT),
    )(*parts, *lands, *send, *recv, after)
    return list(outs[:n]), list(outs[n:])


def _forward_copy(buf_ref, send_ref, recv_ref, j, chip, x, y, c, landing):
    return pltpu.make_async_remote_copy(
        src_ref=buf_ref.at[_chip_id(chip), c], dst_ref=buf_ref.at[_chip_id(chip), 1 - c if landing else c],
        send_sem=send_ref.at[j], recv_sem=recv_ref.at[j], device_id=(x, y, 1 - c), device_id_type=MESH)


def _forward_start(bufs, name):
    n = len(bufs)

    def body(*refs):
        ins = refs[:n]
        send, recv = refs[n:2 * n], refs[2 * n:3 * n]
        token = refs[4 * n]
        x, y, c, chips = _place()
        for a in range(n):
            for j, chip in enumerate(chips):
                _forward_copy(ins[a], send[a], recv[a], j, chip, x, y, c, False).start()
        token[...] = jnp.zeros_like(token)

    sems = [pltpu.SemaphoreType.DMA((N_PEER,))] * (2 * n)
    outs = pl.pallas_call(
        body, name=name,
        in_specs=[HBM] * n,
        out_specs=[SEM] * (2 * n) + [HBM] * n + [pl.BlockSpec(memory_space=pltpu.VMEM)],
        out_shape=sems + [pltpu.HBM(b.shape, b.dtype) for b in bufs] + [jax.ShapeDtypeStruct((8, 128), f32)],
        input_output_aliases={a: 2 * n + a for a in range(n)},
        compiler_params=pltpu.CompilerParams(has_side_effects=EFFECT),
    )(*[_in_hbm(b) for b in bufs])
    return list(outs[:n]), list(outs[n:2 * n]), list(outs[2 * n:3 * n]), outs[3 * n]


def _forward_wait(send, recv, bufs, after, name):
    n = len(bufs)

    def body(*refs):
        ins = refs[:n]
        send_r, recv_r = refs[n:2 * n], refs[2 * n:3 * n]
        x, y, c, chips = _place()
        for a in range(n):
            for j, chip in enumerate(chips):
                cp = _forward_copy(ins[a], send_r[a], recv_r[a], j, chip, x, y, c, True)
                cp.wait_send()
                cp.wait_recv()

    outs = pl.pallas_call(
        body, name=name,
        in_specs=[HBM] * n + [SEM] * (2 * n) + [ANY],
        out_specs=[HBM] * n,
        out_shape=[pltpu.HBM(b.shape, b.dtype) for b in bufs],
        input_output_aliases={a: a for a in range(n)},
        compiler_params=pltpu.CompilerParams(has_side_effects=EFFECT),
    )(*bufs, *send, *recv, after)
    return list(outs)


def _adamw_math(w, g, m, v):
    m = ADAM_B1 * m + (1.0 - ADAM_B1) * g
    v = ADAM_B2 * v + (1.0 - ADAM_B2) * (g * g)
    m_hat = m / (1.0 - ADAM_B1 ** ADAM_STEP)
    v_hat = v / (1.0 - ADAM_B2 ** ADAM_STEP)
    delta = -ADAM_LR * (m_hat / (jnp.sqrt(v_hat) + ADAM_EPS) + ADAM_WD * w)
    return delta, m, v


ADAMW_BLOCK_BYTES = 3 << 19


def _adamw(ws, mines, theirs, ms, vs, qc, name):
    n = len(ws)
    halves = [w.shape[0] // 2 for w in ws]
    nb = next(k for k in range(1, min(halves) + 1)
              if all(hr % k == 0 and (hr // k) % 8 == 0 and (hr // k) * w.shape[1] * 4 <= ADAMW_BLOCK_BYTES
                     for hr, w in zip(halves, ws)))

    def body(qc_ref, *refs):
        mine_here = pl.program_id(0) == qc_ref[1]
        for a in range(n):
            w_ref, a_ref, b_ref, m_ref, v_ref = (refs[k * n + a] for k in range(5))
            g_ref, d_ref, mo_ref, vo_ref = (refs[(5 + k) * n + a] for k in range(4))
            g = jnp.where(mine_here, a_ref[...], b_ref[...])
            g_ref[...] = g
            d_ref[...], mo_ref[...], vo_ref[...] = _adamw_math(w_ref[...], g, m_ref[...], v_ref[...])

    blocks = [(hr // nb, w.shape[1]) for hr, w in zip(halves, ws)]
    full = [pl.BlockSpec(b, lambda h, i, qc_ref: (h * nb + i, 0)) for b in blocks]
    half = [pl.BlockSpec(b, lambda h, i, qc_ref: (i, 0)) for b in blocks]
    outs = pl.pallas_call(
        body, name=name,
        grid_spec=pltpu.PrefetchScalarGridSpec(
            num_scalar_prefetch=1, grid=(2, nb),
            in_specs=full + half + half + full + full, out_specs=full * 4),
        out_shape=[jax.ShapeDtypeStruct(w.shape, f32) for w in ws] * 4,
        compiler_params=_cp(2),
    )(qc, *ws, *mines, *theirs, *ms, *vs)
    return [tuple(outs[k * n + a] for k in range(4)) for a in range(n)]


REPL = [("ffn1_norm", 1), ("mix_norm", 1), ("b_in", 6), ("rnn_conv_b", 1), ("rg_b_a", 1), ("rg_b_x", 1),
        ("rg_lambda", 1), ("conv_dw_b", 1), ("conv_ln_g", 1), ("conv_ln_b", 1), ("conv_b_proj", 1),
        ("ffn2_norm", 1), ("final_norm", 1)]
COLSH = [("meta_tokens", NMETA), ("rnn_conv_w", KC4), ("conv_dw_w", KC31)]
SMALL = REPL + COLSH
CS = D // NCHIP


def _pack_rows():
    starts, row = {}, 0
    for k, rows in REPL:
        starts[k] = row
        row += rows
    for k, rows in COLSH:
        row = -(-row // 8) * 8
        starts[k] = row
        row += rows
    return starts, -(-row // 8) * 8


PACK_START, LOSS_ROW = _pack_rows()
SMALL_ROWS = LOSS_ROW + 8


def _small_pack(g, loss_row):
    pieces, row = [], 0
    for k, rows in SMALL:
        if PACK_START[k] > row:
            pieces.append(jnp.zeros((PACK_START[k] - row, D), f32))
        pieces.append(g[k].reshape(rows, D))
        row = PACK_START[k] + rows
    pieces.append(jnp.zeros((LOSS_ROW - row, D), f32))
    pieces.append(loss_row)
    pieces.append(jnp.zeros((SMALL_ROWS - LOSS_ROW - 1, D), f32))
    return jnp.concatenate(pieces, axis=0)


def _adamw_small(packs, ws, ms, vs):
    ns = len(SMALL)

    def body(*refs):
        pack_ref = refs[0]
        w_refs, m_refs, v_refs = refs[1:1 + ns], refs[1 + ns:1 + 2 * ns], refs[1 + 2 * ns:1 + 3 * ns]
        outs = refs[1 + 3 * ns:1 + 7 * ns]
        g_refs, d_refs, mo_refs, vo_refs = outs[:ns], outs[ns:2 * ns], outs[2 * ns:3 * ns], outs[3 * ns:]
        loss_ref = refs[1 + 7 * ns]
        gsum_sc = refs[2 + 7 * ns]
        q = 2 * lax.axis_index("x") + lax.axis_index("y")
        acc = pack_ref[0]
        for dev in range(1, 8):
            acc = acc + pack_ref[dev]
        gsum_sc[...] = acc
        loss_ref[...] = gsum_sc[LOSS_ROW:LOSS_ROW + 1, :]
        for idx, (name, rows) in enumerate(SMALL):
            row = PACK_START[name]
            if idx < len(REPL):
                for k in range(rows):
                    cols = slice(k * D, (k + 1) * D)
                    g = gsum_sc[row + k:row + k + 1, :]
                    d, mm, vv = _adamw_math(w_refs[idx][:, cols], g, m_refs[idx][:, cols], v_refs[idx][:, cols])
                    g_refs[idx][:, cols] = g
                    d_refs[idx][:, cols] = d
                    mo_refs[idx][:, cols] = mm
                    vo_refs[idx][:, cols] = vv
            else:
                g = gsum_sc[row:row + rows, pl.ds(pl.multiple_of(q * CS, CS), CS)]
                d, mm, vv = _adamw_math(w_refs[idx][...], g, m_refs[idx][...], v_refs[idx][...])
                g_refs[idx][...] = g
                d_refs[idx][...] = d
                mo_refs[idx][...] = mm
                vo_refs[idx][...] = vv

    shapes = [jax.ShapeDtypeStruct(w.shape, f32) for w in ws]
    return pl.pallas_call(
        body, name="adamw_small",
        out_shape=shapes * 4 + [jax.ShapeDtypeStruct((1, D), f32)],
        scratch_shapes=[pltpu.VMEM((SMALL_ROWS, D), f32)],
        compiler_params=pltpu.CompilerParams(vmem_limit_bytes=VMEM_LIMIT),
    )(packs, *ws, *ms, *vs)


WEIGHTS = ['meta_tokens', 'ffn1_norm', 'ffn1_w_gu', 'ffn1_w_down', 'mix_norm', 'w_in', 'b_in', 'rnn_conv_w',
           'rnn_conv_b', 'rg_w_a', 'rg_b_a', 'rg_w_x', 'rg_b_x', 'rg_lambda', 'rnn_w_proj', 'conv_dw_w',
           'conv_dw_b', 'conv_ln_g', 'conv_ln_b', 'conv_w_proj', 'conv_b_proj', 'w_out', 'ffn2_norm',
           'ffn2_w_gu', 'ffn2_w_down', 'final_norm']


def _as2d(a):
    return a.reshape(-1, a.shape[-1])


def _step(x, loss_target, w, m, v):
    seq = x.shape[1]
    n_valid = NMETA + seq
    t = -(-n_valid // TM) * TM

    qc = jnp.stack([2 * lax.axis_index("x") + lax.axis_index("y"), lax.axis_index("c")]).astype(jnp.int32)
    p = {k: w[k].reshape(1, rows * D) for k, rows in REPL}

    first = ["ffn1_w_gu", "ffn1_w_down", "small"]
    later = [["w_in"], ["rg_w_a", "rg_w_x", "rnn_w_proj", "conv_w_proj", "w_out"], ["ffn2_w_gu", "ffn2_w_down"]]
    small_rows = sum(r for _, r in COLSH)
    small = jnp.concatenate([_as2d(w[k]) for k, _ in COLSH] + [jnp.zeros((64 - small_rows, CS), f32)], axis=0)

    def cast(k, token=None):
        src, dtype = (small, f32) if k == "small" else (_as2d(w[k]), bf16)
        return _cast_into_slot(src, qc, dtype, "cast_" + k, after=token)

    send1, recv1, bufs1, token1 = _gather_start([cast(k) for k in first], "gather_start_first")
    rest = [k for grp in later for k in grp]
    send2, recv2, bufs2, token2 = _gather_start([cast(k, token1) for k in rest], "gather_start_rest")

    def install(names, done):
        for k, b in zip(names, done):
            full = b.reshape(NCHIP, 2 * b.shape[2], b.shape[3])
            if k in ("ffn1_w_down", "ffn2_w_down"):
                full = full.reshape(F, D)
            elif k in ("rnn_w_proj", "conv_w_proj", "w_out"):
                full = full.reshape(D, D)
            elif k in ("rg_w_a", "rg_w_x"):
                full = full.reshape(NCHIP, NHEAD, HD // NCHIP, HD).transpose(1, 0, 2, 3).reshape(NHEAD, HD, HD)
            p[k] = full

    def finish(names, send, recv, bufs, after, tag):
        install(names, _forward_halves(_gather_wait(send, recv, bufs, after, "gather_wait_" + tag),
                                       "gather_forward_" + tag))

    def group(names):
        idx = [rest.index(k) for k in names]
        return names, [send2[i] for i in idx], [recv2[i] for i in idx], [bufs2[i] for i in idx]

    h0 = jnp.pad(x[0] + token1[0:1, 0:1], ((NMETA, t - n_valid), (0, 0)))
    tgt = jnp.pad(loss_target[0] + token2[0:1, 0:1], ((NMETA, t - n_valid), (0, 0)))
    finish(first, send1, recv1, bufs1, (token2, h0, tgt), "first")
    small_full = p.pop("small").transpose(1, 0, 2).reshape(64, D)
    row = 0
    for k, rows in COLSH:
        p[k] = small_full[row:row + rows]
        row += rows

    h0 = lax.dynamic_update_slice(h0, p["meta_tokens"], (0, 0))
    h1, gate1, up1, n1 = _ffn_fwd(h0, p["ffn1_norm"], p["ffn1_w_gu"], p["ffn1_w_down"], "ffn1_fwd")
    finish(*group(later[0]), h1, "in")
    proj, n2 = _inproj_fwd(h1, p["mix_norm"], p["w_in"], p["b_in"])
    names_l = later[1] + later[2]
    _, send_l, recv_l, bufs_l = group(names_l)
    send_f, recv_f, bufs_f, token = _forward_start(
        _gather_wait(send_l, recv_l, bufs_l, proj, "gather_wait_late"), "gather_forward_start")
    vc, s = _conv_fwd(proj, p["conv_dw_w"], p["conv_dw_b"], p["conv_ln_g"], p["conv_ln_b"], after=token)
    install(names_l, _forward_wait(send_f, recv_f, bufs_f, vc, "gather_forward_wait"))
    xr, hr, z, gates = _rnn_fwd(proj, p["rnn_conv_w"], p["rnn_conv_b"], p["rg_w_a"], p["rg_b_a"],
                         p["rg_w_x"], p["rg_b_x"], p["rg_lambda"])
    h2 = _merge_fwd(h1, z, s, proj, p["rnn_w_proj"], p["conv_w_proj"], p["conv_b_proj"], p["w_out"])
    dh3, loss_blk, d_final, gate2, up2, n3 = _ffn_fwd(
        h2, p["ffn2_norm"], p["ffn2_w_gu"], p["ffn2_w_down"], "ffn2_fwd",
        loss_head=(p["final_norm"], tgt, n_valid))

    g = {"final_norm": d_final}
    pending = []

    def exchange_start(names, tag):
        parts = []
        for k in names:
            rows = g[k].size // (NCHIP * g[k].shape[-1])
            parts.append(g[k].reshape((NCHIP, 2, rows // 2, g[k].shape[-1])))
        send, recv, parts, lands, token = _pair_exchange_start(parts, "pair_exchange_start_" + tag)
        return (names, tag, send, recv, parts, lands), token

    def reduce_start(state, after):
        names, tag, send, recv, parts, lands = state
        parts, from_sibling = _pair_exchange_wait(send, recv, parts, lands, after, "pair_exchange_wait_" + tag)
        sums, lands = _pair_add(parts, from_sibling, qc, "pair_add_" + tag)
        send, recv, sums, lands, token = _reduce_start(sums, lands, "reduce_start_" + tag)
        pending.append((names, tag, send, recv, sums, lands))
        return token

    dh2, dgate2, dup2, a2, df2, g["ffn2_norm"] = _ffn_bwd(
        dh3, h2, p["ffn2_norm"], gate2, up2, p["ffn2_w_gu"], p["ffn2_w_down"], "ffn2_bwd")
    g["ffn2_w_gu"] = _ffn_gu_grad(n3, dgate2, dup2, "ffn2")
    g["ffn2_w_down"] = _ffn_down_grad(a2, df2, "ffn2")
    state, token = exchange_start(["ffn2_w_gu", "ffn2_w_down"], "ffn2")

    dz, ds, dproj, dh2b, merged, dya, dyb, g["conv_b_proj"] = _merge_bwd(
        dh2, z, s, proj, p["rnn_w_proj"], p["conv_w_proj"], p["conv_b_proj"], p["w_out"], after=token)
    token = reduce_start(state, dz)
    dproj, g["conv_dw_w"], g["conv_dw_b"], g["conv_ln_g"], g["conv_ln_b"] = _conv_bwd(
        ds, vc, proj, dproj, p["conv_dw_w"], p["conv_ln_g"], p["conv_ln_b"], after=token)
    g["w_out"] = _square_grad(merged, dh2b, "dw_out")
    g["rnn_w_proj"] = _square_grad(z, dya, "dw_rnn_proj")
    g["conv_w_proj"] = _square_grad(s, dyb, "dw_conv_proj")
    (dproj, g["rg_w_a"], g["rg_w_x"], g["rnn_conv_w"], g["rnn_conv_b"], g["rg_b_a"], g["rg_b_x"],
     g["rg_lambda"]) = _rnn_bwd(dz, xr, hr, gates, proj, dproj, p["rnn_conv_w"], p["rg_w_a"],
                                p["rg_w_x"], p["rg_lambda"])

    dh1, g["mix_norm"], db_in = _inproj_bwd(dproj, dh2, h1, p["mix_norm"], p["w_in"])
    g["b_in"] = db_in.reshape(1, NIN)
    g["w_in"] = _tn_matmul(n2, dproj, D, NIN // NCHIP, (NCHIP, D, NIN // NCHIP),
                           (None, D, NIN // NCHIP), lambda k, nn, mm: (nn, 0, 0), "dw_in")
    state, token = exchange_start(["w_out", "rnn_w_proj", "conv_w_proj", "rg_w_a", "rg_w_x", "w_in"], "mix")

    dh0, dgate1, dup1, a1, df1, g["ffn1_norm"] = _ffn_bwd(
        dh1, h0, p["ffn1_norm"], gate1, up1, p["ffn1_w_gu"], p["ffn1_w_down"], "ffn1_bwd", after=token)
    g["meta_tokens"] = dh0[0:NMETA]
    grad_x = dh0[NMETA:n_valid][None]
    token = reduce_start(state, dh0)

    send_s, recv_s, pack_buf, token_s = _gather_all_start(
        _place_pack(_small_pack(g, loss_blk.reshape(1, D)), qc), "gather_all_start")
    g["ffn1_w_down"] = _ffn_down_grad(a1, df1, "ffn1", after=(token, token_s))
    state, token = exchange_start(["ffn1_w_down"], "ffn1_down")
    gate_half = _tn_matmul(n1, dgate1, D, FS, (NCHIP, D, FS), (None, D, FS), lambda k, nn, mm: (nn, 0, 0),
                           "ffn1_dwg", after=token)
    token = reduce_start(state, gate_half)
    g["ffn1_w_gu"] = _tn_matmul(n1, dup1, D, FS, (NCHIP, D, FS), (None, D, FS), lambda k, nn, mm: (2 + nn, 0, 0),
                                "ffn1_dwu", base=gate_half, after=token)
    state_gu, token = exchange_start(["ffn1_w_gu"], "ffn1_gu")
    packs = _gather_all_wait(send_s, recv_s, pack_buf, token, "gather_all_wait")

    grads, deltas, new_m, new_v = {}, {}, {}, {}

    def landed_sums(items, after):
        names, mine = [], []
        for grp_names, grp_tag, send, recv, sums, lands in items:
            landed = _reduce_wait(send, recv, sums, lands, after, "reduce_wait_" + grp_tag)
            mine += _sum_chips(landed, "sum_chips_" + grp_tag)
            names += grp_names
            after = mine[-1]
        return names, mine

    def share_and_update(names, mine, tag, after=None):
        theirs = _pair_share(mine, "pair_share_" + tag, after=after)
        got = dict(zip(names, zip(mine, theirs)))
        square = [k for k in names if got[k][0].shape[0] * 2 <= HD]
        for batch in [[k] for k in names if k not in square] + ([square] if square else []):
            outs = _adamw([_as2d(w[k]) for k in batch], [got[k][0] for k in batch], [got[k][1] for k in batch],
                          [_as2d(m[k]) for k in batch], [_as2d(v[k]) for k in batch], qc,
                          "adamw_" + (batch[0] if len(batch) == 1 else "mixer"))
            for k, out in zip(batch, outs):
                grads[k], deltas[k], new_m[k], new_v[k] = (a.reshape(w[k].shape) for a in out)
        return [new_v[k] for k in names]

    early_names, early_mine = landed_sums(pending[:2], packs)
    token = reduce_start(state_gu, early_mine[-1])
    after = share_and_update(early_names, early_mine, "early", after=token)
    share_and_update(*landed_sums(pending[2:], after), "late")
    names = [k for k, _ in SMALL]
    shape2 = {k: ((1, rows * D) if (k, rows) in REPL else (rows, CS)) for k, rows in SMALL}
    outs = _adamw_small(packs, *[[a[k].reshape(shape2[k]) for k in names] for a in (w, m, v)])
    ns = len(names)
    for i, k in enumerate(names):
        grads[k], deltas[k], new_m[k], new_v[k] = (outs[j * ns + i].reshape(w[k].shape) for j in range(4))

    loss = outs[4 * ns][0, 0]
    return (loss, grad_x, *[grads[k] for k in WEIGHTS], *[deltas[k] for k in WEIGHTS],
            *[new_m[k] for k in WEIGHTS], *[new_v[k] for k in WEIGHTS])


def kernel(x, meta_tokens, ffn1_norm, ffn1_w_gu, ffn1_w_down, mix_norm, w_in, b_in, rnn_conv_w, rnn_conv_b, rg_w_a, rg_b_a, rg_w_x, rg_b_x, rg_lambda, rnn_w_proj, conv_dw_w, conv_dw_b, conv_ln_g, conv_ln_b, conv_w_proj, conv_b_proj, w_out, ffn2_norm, ffn2_w_gu, ffn2_w_down, final_norm, loss_target, m_meta_tokens, m_ffn1_norm, m_ffn1_w_gu, m_ffn1_w_down, m_mix_norm, m_w_in, m_b_in, m_rnn_conv_w, m_rnn_conv_b, m_rg_w_a, m_rg_b_a, m_rg_w_x, m_rg_b_x, m_rg_lambda, m_rnn_w_proj, m_conv_dw_w, m_conv_dw_b, m_conv_ln_g, m_conv_ln_b, m_conv_w_proj, m_conv_b_proj, m_w_out, m_ffn2_norm, m_ffn2_w_gu, m_ffn2_w_down, m_final_norm, v_meta_tokens, v_ffn1_norm, v_ffn1_w_gu, v_ffn1_w_down, v_mix_norm, v_w_in, v_b_in, v_rnn_conv_w, v_rnn_conv_b, v_rg_w_a, v_rg_b_a, v_rg_w_x, v_rg_b_x, v_rg_lambda, v_rnn_w_proj, v_conv_dw_w, v_conv_dw_b, v_conv_ln_g, v_conv_ln_b, v_conv_w_proj, v_conv_b_proj, v_w_out, v_ffn2_norm, v_ffn2_w_gu, v_ffn2_w_down, v_final_norm):
    args = locals()
    w = {k: args[k] for k in WEIGHTS}
    m = {k: args["m_" + k] for k in WEIGHTS}
    v = {k: args["v_" + k] for k in WEIGHTS}
    return _step(x, loss_target, w, m, v)
```

```python
import jax
import jax.numpy as jnp
from jax import lax
from jax.experimental import pallas as pl
from jax.experimental.pallas import tpu as pltpu

f32 = jnp.float32
bf16 = jnp.bfloat16

D = 1024
F = 2816
FS = F // 2
NIN = 6 * D
NMETA = 16
NHEAD = 4
HD = D // NHEAD
KC4 = 4
KC31 = 31
HALO = 32
EPS = 1e-6
TM = 416
NCHIP = 4
MESH = pl.DeviceIdType.MESH

ADAM_LR = 0.001
ADAM_B1 = 0.9
ADAM_B2 = 0.999
ADAM_EPS = 1e-08
ADAM_WD = 0.01
ADAM_STEP = 10

VMEM_LIMIT = 56 * 1024 * 1024
FSUB = [(o, min(256, FS - o)) for o in range(0, FS, 256)]


def _cp(n_axes, **kw):
    return pltpu.CompilerParams(dimension_semantics=("arbitrary",) * n_axes,
                                vmem_limit_bytes=VMEM_LIMIT, **kw)


RESIDENT = pl.BlockSpec(memory_space=pltpu.VMEM)


def _n_after(after):
    return 0 if after is None else (len(after) if isinstance(after, (tuple, list)) else 1)


def _ordered(body, in_specs, args, after):
    if after is None:
        return body, in_specs, args
    extra = tuple(after) if isinstance(after, (tuple, list)) else (after,)
    return (lambda *refs: body(*refs[len(extra):]),
            [pl.BlockSpec(memory_space=pl.ANY)] * len(extra) + list(in_specs), extra + tuple(args))


def _nt_dot(a, b):
    return lax.dot_general(a, b, (((1,), (1,)), ((), ())), preferred_element_type=f32)


def _tn_dot(a, b):
    return lax.dot_general(a, b, (((0,), (0,)), ((), ())), preferred_element_type=f32)


def _sigmoid(x):
    return 0.5 * jnp.tanh(0.5 * x) + 0.5


def _log1p(y):
    u = 1.0 + y
    d = u - 1.0
    return jnp.where(d == 0.0, y, jnp.log(u) * (y / jnp.where(d == 0.0, 1.0, d)))


def _softplus(x):
    return jnp.maximum(x, 0.0) + _log1p(jnp.exp(-jnp.abs(x)))


def _one_minus_square(a, log_a):
    x = 2.0 * log_a
    series = x * (1.0 + x * (0.5 + x * (1.0 / 6.0)))
    return jnp.where(jnp.abs(x) < 0.03, -series, 1.0 - a * a)


_GELU_C = 0.7978845608028654
_GELU_K = 0.044715


def _gelu_and_grad(y):
    y2 = y * y
    th = jnp.tanh(_GELU_C * (y + _GELU_K * y * y2))
    gel = 0.5 * y * (1.0 + th)
    dgel = 0.5 * (1.0 + th) + 0.5 * y * (1.0 - th * th) * _GELU_C * (1.0 + 3.0 * _GELU_K * y2)
    return gel, dgel


def _rms_stats(h):
    return lax.rsqrt(jnp.mean(h * h, axis=-1, keepdims=True) + EPS)


def _rms_bwd(dn, h, g):
    r = _rms_stats(h)
    nhat = h * r
    dnh = dn * g
    dh = r * (dnh - nhat * jnp.mean(dnh * nhat, axis=-1, keepdims=True))
    dg = jnp.sum(dn * nhat, axis=0, keepdims=True)
    return dh, dg


def _row_ids(shape):
    return lax.broadcasted_iota(jnp.int32, shape, 0)


def _ffn_fwd(h, g, wgu, wd, name, loss_head=None):
    t = h.shape[0]
    nj = 2
    tm = TM
    n_head = 0 if loss_head is None else 2

    def body(*refs):
        h_ref, g_ref, wg_ref, wd_ref = refs[:4]
        outs = refs[4 + n_head:]
        gate_ref, up_ref, n_ref, nb_sc, acc_sc, a_sc = outs[-6:]
        i = pl.program_id(0)
        j = pl.program_id(1)

        @pl.when(j == 0)
        def _():
            hh = h_ref[...]
            nb = (hh * _rms_stats(hh) * g_ref[...]).astype(bf16)
            nb_sc[...] = nb
            n_ref[...] = nb
            acc_sc[...] = jnp.zeros_like(acc_sc)

        nb = nb_sc[...]
        for off, width in FSUB:
            cols = slice(off, off + width)
            gt = jnp.dot(nb, wg_ref[j, :, cols], preferred_element_type=f32)
            up = jnp.dot(nb, wg_ref[2 + j, :, cols], preferred_element_type=f32)
            gate_ref[:, cols] = gt.astype(bf16)
            up_ref[:, cols] = up.astype(bf16)
            a_sc[:, cols] = (gt * _sigmoid(gt) * up).astype(bf16)
        acc_sc[...] += jnp.dot(a_sc[...], wd_ref[j], preferred_element_type=f32)

        if loss_head is None:
            @pl.when(j == nj - 1)
            def _():
                outs[0][...] = h_ref[...] + 0.5 * acc_sc[...]
        else:
            gf_ref, t_ref = refs[4:6]
            dh_ref, loss_ref, dgf_ref = outs[:3]

            @pl.when(jnp.logical_and(i == 0, j == 0))
            def _():
                loss_ref[...] = jnp.zeros_like(loss_ref)
                dgf_ref[...] = jnp.zeros_like(dgf_ref)

            @pl.when(j == nj - 1)
            def _():
                hh = h_ref[...] + 0.5 * acc_sc[...]
                gf = gf_ref[...]
                row = i * tm + _row_ids((tm, 1))
                valid = jnp.logical_and(row >= NMETA, row < loss_head[2])
                err = jnp.where(valid, hh * _rms_stats(hh) * gf - t_ref[...], 0.0)
                loss_ref[...] += 0.5 * jnp.sum(err * err) * (1.0 / D)
                dh, dgf = _rms_bwd(err * (1.0 / D), hh, gf)
                dh_ref[...] = dh
                dgf_ref[...] += dgf

    rowd = pl.BlockSpec((tm, D), lambda i, j: (i, 0))
    vec = pl.BlockSpec((1, D), lambda i, j: (0, 0))
    rowf = pl.BlockSpec((tm, FS), lambda i, j: (i, j))
    in_specs, args = [rowd, vec, RESIDENT, RESIDENT], [h, g, wgu, wd.reshape(nj, FS, D)]
    out_specs, out_shape = [rowd], [jax.ShapeDtypeStruct((t, D), f32)]
    if loss_head is not None:
        in_specs, args = in_specs + [vec, rowd], args + [loss_head[0], loss_head[1]]
        out_specs += [pl.BlockSpec((8, 128), lambda i, j: (0, 0)), vec]
        out_shape += [jax.ShapeDtypeStruct((8, 128), f32), jax.ShapeDtypeStruct((1, D), f32)]
    return pl.pallas_call(
        body, name=name, grid=(t // tm, nj),
        in_specs=in_specs,
        out_specs=out_specs + [rowf, rowf, rowd],
        out_shape=out_shape + [jax.ShapeDtypeStruct((t, F), bf16), jax.ShapeDtypeStruct((t, F), bf16),
                               jax.ShapeDtypeStruct((t, D), bf16)],
        scratch_shapes=[pltpu.VMEM((tm, D), bf16), pltpu.VMEM((tm, D), f32), pltpu.VMEM((tm, FS), bf16)],
        compiler_params=_cp(2),
    )(*args)


def _inproj_fwd(h, g, win, b_in):
    t = h.shape[0]
    tn = NIN // NCHIP
    nj = NIN // tn

    def body(h_ref, g_ref, w_ref, b_ref, proj_ref, n_ref, nb_sc):
        j = pl.program_id(1)

        @pl.when(j == 0)
        def _():
            hh = h_ref[...]
            nb = (hh * _rms_stats(hh) * g_ref[...]).astype(bf16)
            nb_sc[...] = nb
            n_ref[...] = nb

        proj_ref[...] = jnp.dot(nb_sc[...], w_ref[j], preferred_element_type=f32) + b_ref[...]

    return pl.pallas_call(
        body, name="inproj_fwd", grid=(t // TM, nj),
        in_specs=[
            pl.BlockSpec((TM, D), lambda i, j: (i, 0)),
            pl.BlockSpec((1, D), lambda i, j: (0, 0)),
            RESIDENT,
            pl.BlockSpec((1, tn), lambda i, j: (0, j)),
        ],
        out_specs=[
            pl.BlockSpec((TM, tn), lambda i, j: (i, j)),
            pl.BlockSpec((TM, D), lambda i, j: (i, 0)),
        ],
        out_shape=[jax.ShapeDtypeStruct((t, NIN), f32), jax.ShapeDtypeStruct((t, D), bf16)],
        scratch_shapes=[pltpu.VMEM((TM, D), bf16)],
        compiler_params=_cp(2),
    )(h, g, win, b_in)


def _block_gates(xr, wa_ref, ba, wx_ref, bx, lam):
    xrb = xr.astype(bf16)
    pa = jnp.concatenate([jnp.dot(xrb[:, hh * HD:(hh + 1) * HD], wa_ref[hh], preferred_element_type=f32)
                          for hh in range(NHEAD)], axis=1)
    px = jnp.concatenate([jnp.dot(xrb[:, hh * HD:(hh + 1) * HD], wx_ref[hh], preferred_element_type=f32)
                          for hh in range(NHEAD)], axis=1)
    ra = _sigmoid(pa + ba)
    ii = _sigmoid(px + bx)
    sp = _softplus(-lam)
    log_a = -8.0 * ra * sp
    a = jnp.exp(log_a)
    sq = jnp.sqrt(_one_minus_square(a, log_a))
    return ra, ii, a, sq, sp


LT = D // 128
UNR = 13


def _to_lane_tiles(ref, value):
    for lt in range(LT):
        ref[lt] = value[:, lt * 128:(lt + 1) * 128]


def _from_lane_tiles(ref):
    return jnp.concatenate([ref[lt] for lt in range(LT)], axis=1)


def _chain_scan(mult_sc, val_sc, start, reverse):
    ng = TM // 8

    def lanes(lt):
        return slice(lt * 128, (lt + 1) * 128)

    def chain(gi, carry):
        v_prev, p_prev = carry
        rows = pl.ds(ng - 1 - gi if reverse else gi, 8, stride=ng)
        v_new, p_new = [], []
        for lt in range(LT):
            mm = mult_sc.at[lt][rows, :]
            vv = mm * v_prev[:, lanes(lt)] + val_sc.at[lt][rows, :]
            pp = mm * p_prev[:, lanes(lt)]
            val_sc.at[lt][rows, :] = vv
            mult_sc.at[lt][rows, :] = pp
            v_new.append(vv)
            p_new.append(pp)
        return jnp.concatenate(v_new, axis=1), jnp.concatenate(p_new, axis=1)

    v_end, p_end = lax.fori_loop(0, ng, chain, (jnp.zeros((8, D), f32), jnp.ones((8, D), f32)))
    state, entries = start, [None] * 8
    for r in (reversed(range(8)) if reverse else range(8)):
        entries[r] = state
        state = v_end[r:r + 1, :] + p_end[r:r + 1, :] * state
    entry8 = jnp.concatenate(entries, axis=0)

    def add_entry(gi, carry):
        rows = pl.ds(gi, 8, stride=ng)
        for lt in range(LT):
            val_sc.at[lt][rows, :] = val_sc.at[lt][rows, :] + mult_sc.at[lt][rows, :] * entry8[:, lanes(lt)]
        return carry

    lax.fori_loop(0, ng, add_entry, 0)
    return state


def _rnn_fwd(proj, cw, cb, wa, ba, wx, bx, lam):
    t = proj.shape[0]

    def body(x_ref, y_ref, cw_ref, cb_ref, wa_ref, ba_ref, wx_ref, bx_ref, lam_ref,
             xr_ref, hr_ref, z_ref, gates_ref, xext_sc, carry_sc, a_sc, h_sc):
        i = pl.program_id(0)

        @pl.when(i == 0)
        def _():
            xext_sc[0:8, :] = jnp.zeros((8, D), f32)
            carry_sc[...] = jnp.zeros_like(carry_sc)

        x = x_ref[...]
        xext_sc[8:8 + TM, :] = x
        xe = xext_sc[...]
        xr = cb_ref[...] + cw_ref[KC4 - 1:KC4, :] * x
        for k in range(KC4 - 1):
            xr = xr + cw_ref[k:k + 1, :] * pltpu.roll(xe, KC4 - 1 - k, 0)[8:8 + TM]
        xext_sc[0:8, :] = x[TM - 8:TM]

        ra, ii, a, sq, _ = _block_gates(xr, wa_ref, ba_ref[...], wx_ref, bx_ref[...], lam_ref[...])
        for slot, val in enumerate((ra, ii, a, sq)):
            gates_ref[slot] = val
        _to_lane_tiles(a_sc, a)
        _to_lane_tiles(h_sc, sq * ii * xr)
        carry_sc[...] = _chain_scan(a_sc, h_sc, carry_sc[...], reverse=False)
        hr = _from_lane_tiles(h_sc)
        gel, _ = _gelu_and_grad(y_ref[...])
        xr_ref[...] = xr
        hr_ref[...] = hr
        z_ref[...] = (hr * gel).astype(bf16)

    vec = pl.BlockSpec((1, D), lambda i: (0, 0))
    return pl.pallas_call(
        body, name="rnn_fwd", grid=(t // TM,),
        in_specs=[
            pl.BlockSpec((TM, D), lambda i: (i, 0)),
            pl.BlockSpec((TM, D), lambda i: (i, 1)),
            pl.BlockSpec((KC4, D), lambda i: (0, 0)),
            vec,
            pl.BlockSpec((NHEAD, HD, HD), lambda i: (0, 0, 0)),
            vec,
            pl.BlockSpec((NHEAD, HD, HD), lambda i: (0, 0, 0)),
            vec, vec,
        ],
        out_specs=[pl.BlockSpec((TM, D), lambda i: (i, 0))] * 3 + [pl.BlockSpec((4, TM, D), lambda i: (0, i, 0))],
        out_shape=[jax.ShapeDtypeStruct((t, D), f32), jax.ShapeDtypeStruct((t, D), f32),
                   jax.ShapeDtypeStruct((t, D), bf16), jax.ShapeDtypeStruct((4, t, D), f32)],
        scratch_shapes=[pltpu.VMEM((TM + 8, D), f32), pltpu.VMEM((1, D), f32),
                        pltpu.VMEM((LT, TM, 128), f32), pltpu.VMEM((LT, TM, 128), f32)],
        compiler_params=_cp(1),
    )(proj, proj, cw, cb, wa, ba, wx, bx, lam)


def _ln_stats(vc):
    mu = jnp.mean(vc, axis=-1, keepdims=True)
    xc = vc - mu
    rstd = lax.rsqrt(jnp.mean(xc * xc, axis=-1, keepdims=True) + EPS)
    return xc * rstd, rstd


def _conv_fwd(proj, w31, b31, ln_g, ln_b, after=None):
    t = proj.shape[0]

    ng = TM // 8

    def body(gv_ref, gg_ref, w_ref, b_ref, lg_ref, lb_ref, vc_ref, s_ref, vext_sc, out_sc):
        i = pl.program_id(0)

        @pl.when(i == 0)
        def _():
            vext_sc[:, 0:HALO, :] = jnp.zeros((LT, HALO, 128), f32)

        v = gv_ref[...] * _sigmoid(gg_ref[...])
        for lt in range(LT):
            vext_sc[lt, HALO:HALO + TM, :] = v[:, lt * 128:(lt + 1) * 128]

        for lt in range(LT):
            lanes = slice(lt * 128, (lt + 1) * 128)
            taps = [jnp.broadcast_to(w_ref[k:k + 1, lanes], (8, 128)) for k in range(KC31)]
            bias = jnp.broadcast_to(b_ref[:, lanes], (8, 128))

            def step(gb, carry, lt=lt, taps=taps, bias=bias):
                accs = [bias] * UNR
                for k in range(KC31):
                    for u in range(UNR):
                        rows = pl.ds(HALO + gb * UNR + u - (KC31 - 1 - k), 8, stride=ng)
                        accs[u] = accs[u] + taps[k] * vext_sc.at[lt][rows, :]
                for u in range(UNR):
                    out_sc.at[lt][pl.ds(gb * UNR + u, 8, stride=ng), :] = accs[u]
                return carry

            lax.fori_loop(0, ng // UNR, step, 0)
        for lt in range(LT):
            vext_sc[lt, 0:HALO, :] = v[TM - HALO:TM, lt * 128:(lt + 1) * 128]
        acc = _from_lane_tiles(out_sc)
        xhat, _ = _ln_stats(acc)
        ln = xhat * lg_ref[...] + lb_ref[...]
        vc_ref[...] = acc
        s_ref[...] = (ln * _sigmoid(ln)).astype(bf16)

    vec = pl.BlockSpec((1, D), lambda i: (0, 0))
    body, in_specs, args = _ordered(
        body,
        [pl.BlockSpec((TM, D), lambda i: (i, 2)),
         pl.BlockSpec((TM, D), lambda i: (i, 3)),
         pl.BlockSpec((KC31, D), lambda i: (0, 0)),
         vec, vec, vec],
        (proj, proj, w31, b31, ln_g, ln_b), after)
    return pl.pallas_call(
        body, name="conv_fwd", grid=(t // TM,),
        in_specs=in_specs,
        out_specs=[pl.BlockSpec((TM, D), lambda i: (i, 0))] * 2,
        out_shape=[jax.ShapeDtypeStruct((t, D), f32), jax.ShapeDtypeStruct((t, D), bf16)],
        scratch_shapes=[pltpu.VMEM((LT, TM + HALO, 128), f32), pltpu.VMEM((LT, TM, 128), f32)],
        compiler_params=_cp(1),
    )(*args)


def _merge_fwd(h, z, s, proj, wrp, wcp, bcp, wout):
    t = h.shape[0]

    def body(h_ref, z_ref, s_ref, ga_ref, gb_ref, wrp_ref, wcp_ref, bcp_ref, wout_ref, ho_ref):
        ya = jnp.dot(z_ref[...], wrp_ref[...], preferred_element_type=f32)
        yb = jnp.dot(s_ref[...], wcp_ref[...], preferred_element_type=f32) + bcp_ref[...]
        merged = _sigmoid(ga_ref[...]) * ya + _sigmoid(gb_ref[...]) * yb
        ho_ref[...] = h_ref[...] + jnp.dot(merged.astype(bf16), wout_ref[...], preferred_element_type=f32)

    row = pl.BlockSpec((TM, D), lambda i: (i, 0))
    wsq = pl.BlockSpec((D, D), lambda i: (0, 0))
    return pl.pallas_call(
        body, name="merge_fwd", grid=(t // TM,),
        in_specs=[row, row, row,
                  pl.BlockSpec((TM, D), lambda i: (i, 4)),
                  pl.BlockSpec((TM, D), lambda i: (i, 5)),
                  wsq, wsq, pl.BlockSpec((1, D), lambda i: (0, 0)), wsq],
        out_specs=row,
        out_shape=jax.ShapeDtypeStruct((t, D), f32),
        compiler_params=_cp(1),
    )(h, z, s, proj, proj, wrp, wcp, bcp, wout)


def _ffn_bwd(dh, h, g, gate, up, wgu, wd, name, after=None):
    t = h.shape[0]
    nj = 2

    def body(dh_ref, h_ref, g_ref, gate_ref, up_ref, wg_ref, wd_ref,
             dhi_ref, dgate_ref, dup_ref, a_ref, df_ref, dg_ref, dfb_sc, dn_sc):
        i = pl.program_id(0)
        j = pl.program_id(1)

        @pl.when(jnp.logical_and(i == 0, j == 0))
        def _():
            dg_ref[...] = jnp.zeros_like(dg_ref)

        @pl.when(j == 0)
        def _():
            dfb = (0.5 * dh_ref[...]).astype(bf16)
            dfb_sc[...] = dfb
            df_ref[...] = dfb
            dn_sc[...] = jnp.zeros_like(dn_sc)

        dfb = dfb_sc[...]
        for off, width in FSUB:
            cols = slice(off, off + width)
            da = _nt_dot(dfb, wd_ref[j, cols, :])
            gt = gate_ref[:, cols].astype(f32)
            uu = up_ref[:, cols].astype(f32)
            sg = _sigmoid(gt)
            silu = gt * sg
            a_ref[:, cols] = (silu * uu).astype(bf16)
            dgate_ref[:, cols] = (da * uu * (sg * (1.0 + gt * (1.0 - sg)))).astype(bf16)
            dup_ref[:, cols] = (da * silu).astype(bf16)
        dn_sc[...] += _nt_dot(dgate_ref[...], wg_ref[j]) + _nt_dot(dup_ref[...], wg_ref[2 + j])

        @pl.when(j == nj - 1)
        def _():
            dhin, dg = _rms_bwd(dn_sc[...], h_ref[...], g_ref[...])
            dhi_ref[...] = dh_ref[...] + dhin
            dg_ref[...] += dg

    rowd = pl.BlockSpec((TM, D), lambda i, j: (i, 0))
    rowf = pl.BlockSpec((TM, FS), lambda i, j: (i, j))
    vec = pl.BlockSpec((1, D), lambda i, j: (0, 0))
    body, in_specs, args = _ordered(
        body,
        [rowd, rowd, vec, rowf, rowf,
         RESIDENT, RESIDENT],
        (dh, h, g, gate, up, wgu, wd.reshape(nj, FS, D)), after)
    return pl.pallas_call(
        body, name=name, grid=(t // TM, nj),
        in_specs=in_specs,
        out_specs=[rowd, rowf, rowf, rowf, rowd, vec],
        out_shape=[jax.ShapeDtypeStruct((t, D), f32), jax.ShapeDtypeStruct((t, F), bf16),
                   jax.ShapeDtypeStruct((t, F), bf16), jax.ShapeDtypeStruct((t, F), bf16),
                   jax.ShapeDtypeStruct((t, D), bf16), jax.ShapeDtypeStruct((1, D), f32)],
        scratch_shapes=[pltpu.VMEM((TM, D), bf16), pltpu.VMEM((TM, D), f32)],
        compiler_params=_cp(2),
    )(*args)


def _big_tile(t):
    return max(k * TM for k in range(1, 6) if t % (k * TM) == 0)


ANY_SPEC = pl.BlockSpec(memory_space=pl.ANY)


def _tn_matmul(a, b, tk, tn, out_shape, out_block, out_map, name, base=None, after=None):
    t, kk = a.shape
    _, nn = b.shape
    tmm = _big_tile(t)
    nm = t // tmm

    def body(a_ref, b_ref, o_ref, acc_sc):
        m = pl.program_id(2)

        @pl.when(m == 0)
        def _():
            acc_sc[...] = jnp.zeros_like(acc_sc)

        acc_sc[...] += _tn_dot(a_ref[...], b_ref[...])

        @pl.when(m == nm - 1)
        def _():
            o_ref[...] = acc_sc[...].astype(o_ref.dtype)

    in_specs = [pl.BlockSpec((tmm, tk), lambda k, n, m: (m, k)),
                pl.BlockSpec((tmm, tn), lambda k, n, m: (m, n))]
    args, aliases = (a, b), {}
    if base is not None:
        body = (lambda inner: lambda a_ref, b_ref, base_ref, o_ref, acc_sc: inner(a_ref, b_ref, o_ref, acc_sc))(body)
        in_specs, args, aliases = in_specs + [ANY_SPEC], (a, b, base), {2: 0}
    if after is not None:
        body, in_specs, args = _ordered(body, in_specs, args, after)
        aliases = {k + _n_after(after): v for k, v in aliases.items()}
    return pl.pallas_call(
        body, name=name, grid=(kk // tk, nn // tn, nm),
        in_specs=in_specs,
        out_specs=pl.BlockSpec(out_block, out_map),
        out_shape=jax.ShapeDtypeStruct(out_shape, bf16),
        scratch_shapes=[pltpu.VMEM((tk, tn), f32)],
        input_output_aliases=aliases,
        compiler_params=_cp(3),
    )(*args)


def _merge_bwd(dh, z, s, proj, wrp, wcp, bcp, wout, after=None):
    t = dh.shape[0]

    def body(dh_ref, z_ref, s_ref, ga_ref, gb_ref, wrp_ref, wcp_ref, bcp_ref, wout_ref,
             dz_ref, ds_ref, dgab_ref, dhb_ref, mg_ref, dya_ref, dyb_ref, dbcp_ref):
        i = pl.program_id(0)

        @pl.when(i == 0)
        def _():
            dbcp_ref[...] = jnp.zeros_like(dbcp_ref)

        dhb = dh_ref[...].astype(bf16)
        dhb_ref[...] = dhb
        dmg = _nt_dot(dhb, wout_ref[...])
        ya = jnp.dot(z_ref[...], wrp_ref[...], preferred_element_type=f32)
        yb = jnp.dot(s_ref[...], wcp_ref[...], preferred_element_type=f32) + bcp_ref[...]
        sa = _sigmoid(ga_ref[...])
        sb = _sigmoid(gb_ref[...])
        mg_ref[...] = (sa * ya + sb * yb).astype(bf16)
        dgab_ref[:, 0:D] = (dmg * ya * sa * (1.0 - sa)).astype(bf16)
        dgab_ref[:, D:2 * D] = (dmg * yb * sb * (1.0 - sb)).astype(bf16)
        dya = dmg * sa
        dyb = dmg * sb
        dbcp_ref[...] += jnp.sum(dyb, axis=0, keepdims=True)
        dyab = dya.astype(bf16)
        dybb = dyb.astype(bf16)
        dya_ref[...] = dyab
        dyb_ref[...] = dybb
        dz_ref[...] = _nt_dot(dyab, wrp_ref[...])
        ds_ref[...] = _nt_dot(dybb, wcp_ref[...])

    row = pl.BlockSpec((TM, D), lambda i: (i, 0))
    wsq = pl.BlockSpec((D, D), lambda i: (0, 0))
    vec = pl.BlockSpec((1, D), lambda i: (0, 0))
    rowb = jax.ShapeDtypeStruct((t, D), bf16)
    body, in_specs, args = _ordered(
        body,
        [row, row, row,
         pl.BlockSpec((TM, D), lambda i: (i, 4)),
         pl.BlockSpec((TM, D), lambda i: (i, 5)),
         wsq, wsq, vec, wsq],
        (dh, z, s, proj, proj, wrp, wcp, bcp, wout), after)
    return pl.pallas_call(
        body, name="merge_bwd", grid=(t // TM,),
        in_specs=in_specs,
        out_specs=[row, row,
                   pl.BlockSpec((TM, 2 * D), lambda i: (i, 2)),
                   row, row, row, row, vec],
        out_shape=[jax.ShapeDtypeStruct((t, D), f32), jax.ShapeDtypeStruct((t, D), f32),
                   jax.ShapeDtypeStruct((t, NIN), bf16),
                   rowb, rowb, rowb, rowb, jax.ShapeDtypeStruct((1, D), f32)],
        compiler_params=_cp(1),
    )(*args)


def _conv_bwd(ds, vc, proj, dproj, w31, ln_g, ln_b, after=None):
    t = ds.shape[0]
    nt = t // TM
    hb = TM // HALO

    ng = TM // 8

    def body(ds_ref, vc_ref, gv_ref, gg_ref, gvp_ref, ggp_ref, dpin_ref, w_ref, lg_ref, lb_ref,
             dgvg_ref, dw_ref, db_ref, dlg_ref, dlb_ref, dext_sc, vext_sc, out_sc, dwacc_sc, small_sc):
        del dpin_ref
        i = pl.program_id(0)
        tile = nt - 1 - i

        @pl.when(i == 0)
        def _():
            dext_sc[:, TM:TM + HALO, :] = jnp.zeros((LT, HALO, 128), f32)
            dwacc_sc[...] = jnp.zeros_like(dwacc_sc)
            small_sc[...] = jnp.zeros_like(small_sc)

        lg = lg_ref[...]
        lb = lb_ref[...]

        xhat, rstd = _ln_stats(vc_ref[...])
        ln = xhat * lg + lb
        sg = _sigmoid(ln)
        dln = ds_ref[...] * (sg * (1.0 + ln * (1.0 - sg)))
        dxh = dln * lg
        dvc = rstd * (dxh - jnp.mean(dxh, axis=-1, keepdims=True)
                      - xhat * jnp.mean(dxh * xhat, axis=-1, keepdims=True))
        small_sc[0] += jnp.sum((dln * xhat).reshape(TM // 8, 8, D), axis=0)
        small_sc[1] += jnp.sum(dln.reshape(TM // 8, 8, D), axis=0)
        small_sc[2] += jnp.sum(dvc.reshape(TM // 8, 8, D), axis=0)
        sgg = _sigmoid(gg_ref[...])
        v = gv_ref[...] * sgg
        vprev = jnp.where(tile > 0, gvp_ref[...] * _sigmoid(ggp_ref[...]), 0.0)
        for lt in range(LT):
            lanes = slice(lt * 128, (lt + 1) * 128)
            dext_sc[lt, 0:TM, :] = dvc[:, lanes]
            vext_sc[lt, HALO:HALO + TM, :] = v[:, lanes]
            vext_sc[lt, 0:HALO, :] = vprev[:, lanes]

        for lt in range(LT):
            lanes = slice(lt * 128, (lt + 1) * 128)
            taps = [jnp.broadcast_to(w_ref[k:k + 1, lanes], (8, 128)) for k in range(KC31)]

            def dv_step(gb, carry, lt=lt, taps=taps):
                accs = [jnp.zeros((8, 128), f32)] * UNR
                for k in range(KC31):
                    for u in range(UNR):
                        rows = pl.ds(gb * UNR + u + (KC31 - 1 - k), 8, stride=ng)
                        accs[u] = accs[u] + taps[k] * dext_sc.at[lt][rows, :]
                for u in range(UNR):
                    out_sc.at[lt][pl.ds(gb * UNR + u, 8, stride=ng), :] = accs[u]
                return carry

            lax.fori_loop(0, ng // UNR, dv_step, 0)
        dv = _from_lane_tiles(out_sc)
        dgvg_ref[:, 0:D] = (dv * sgg).astype(bf16)
        dgvg_ref[:, D:2 * D] = (dv * gv_ref[...] * sgg * (1.0 - sgg)).astype(bf16)

        for lt in range(LT):
            def dw_step(gb, accs, lt=lt):
                accs = list(accs)
                for u in range(4):
                    g = gb * 4 + u
                    dvc_g = dext_sc.at[lt][pl.ds(g, 8, stride=ng), :]
                    for k in range(KC31):
                        rows = pl.ds(HALO + g - (KC31 - 1 - k), 8, stride=ng)
                        accs[k] = accs[k] + dvc_g * vext_sc.at[lt][rows, :]
                return tuple(accs)

            sums = lax.fori_loop(0, ng // 4, dw_step, tuple(jnp.zeros((8, 128), f32) for _ in range(KC31)))
            for k in range(KC31):
                dwacc_sc[k, :, lt * 128:(lt + 1) * 128] += sums[k]
        for lt in range(LT):
            dext_sc[lt, TM:TM + HALO, :] = dext_sc[lt, 0:HALO, :]

        @pl.when(i == nt - 1)
        def _():
            for k in range(KC31):
                dw_ref[k:k + 1, :] = jnp.sum(dwacc_sc[k], axis=0, keepdims=True)
            dlg_ref[...] = jnp.sum(small_sc[0], axis=0, keepdims=True)
            dlb_ref[...] = jnp.sum(small_sc[1], axis=0, keepdims=True)
            db_ref[...] = jnp.sum(small_sc[2], axis=0, keepdims=True)

    rev = lambda i: (nt - 1 - i, 0)
    vec = pl.BlockSpec((1, D), lambda i: (0, 0))
    halo_row = lambda i: jnp.maximum((nt - 1 - i) * hb - 1, 0)
    body, in_specs, args = _ordered(
        body,
        [pl.BlockSpec((TM, D), rev),
         pl.BlockSpec((TM, D), rev),
         pl.BlockSpec((TM, D), lambda i: (nt - 1 - i, 2)),
         pl.BlockSpec((TM, D), lambda i: (nt - 1 - i, 3)),
         pl.BlockSpec((HALO, D), lambda i: (halo_row(i), 2)),
         pl.BlockSpec((HALO, D), lambda i: (halo_row(i), 3)),
         pl.BlockSpec(memory_space=pl.ANY),
         pl.BlockSpec((KC31, D), lambda i: (0, 0)),
         vec, vec],
        (ds, vc, proj, proj, proj, proj, dproj, w31, ln_g, ln_b), after)
    return pl.pallas_call(
        body, name="conv_bwd", grid=(nt,),
        in_specs=in_specs,
        out_specs=[
            pl.BlockSpec((TM, 2 * D), lambda i: (nt - 1 - i, 1)),
            pl.BlockSpec((KC31, D), lambda i: (0, 0)),
            vec, vec, vec,
        ],
        out_shape=[jax.ShapeDtypeStruct((t, NIN), bf16),
                   jax.ShapeDtypeStruct((KC31, D), f32),
                   jax.ShapeDtypeStruct((1, D), f32), jax.ShapeDtypeStruct((1, D), f32),
                   jax.ShapeDtypeStruct((1, D), f32)],
        scratch_shapes=[pltpu.VMEM((LT, TM + HALO, 128), f32), pltpu.VMEM((LT, TM + HALO, 128), f32),
                        pltpu.VMEM((LT, TM, 128), f32), pltpu.VMEM((KC31, 8, D), f32),
                        pltpu.VMEM((3, 8, D), f32)],
        input_output_aliases={6 + _n_after(after): 0},
        compiler_params=_cp(1),
    )(*args)


def _rnn_bwd(dz, xr, hr, gates, proj, dproj, cw, wa, wx, lam):
    t = dz.shape[0]
    nt = t // TM
    ng = TM // 8
    hq = HD // NCHIP

    def body(dz_ref, xr_ref, hr_ref, hrp_ref, x_ref, xp_ref, y_ref, dpin_ref,
             cw_ref, wa_ref, gates_ref, wx_ref, lam_ref,
             dxy_ref, dwa_ref, dwx_ref, dcw_ref, dcb_ref, dba_ref, dbx_ref, dlam_ref,
             anext_sc, gcarry_sc, dext_sc, xext_sc, m_sc, g_sc, dwa_sc, dwx_sc, dsp_sc):
        del dpin_ref
        i = pl.program_id(0)
        tile = nt - 1 - i

        @pl.when(i == 0)
        def _():
            anext_sc[...] = jnp.zeros_like(anext_sc)
            gcarry_sc[...] = jnp.zeros_like(gcarry_sc)
            dext_sc[TM:TM + 8, :] = jnp.zeros((8, D), f32)
            dwa_sc[...] = jnp.zeros_like(dwa_sc)
            dwx_sc[...] = jnp.zeros_like(dwx_sc)
            dsp_sc[...] = jnp.zeros_like(dsp_sc)
            dcw_ref[...] = jnp.zeros_like(dcw_ref)
            dcb_ref[...] = jnp.zeros_like(dcb_ref)
            dba_ref[...] = jnp.zeros_like(dba_ref)
            dbx_ref[...] = jnp.zeros_like(dbx_ref)

        xr = xr_ref[...]
        hr = hr_ref[...]
        dz = dz_ref[...]
        gel, dgel = _gelu_and_grad(y_ref[...])
        dxy_ref[:, D:2 * D] = (dz * hr * dgel).astype(bf16)
        ra, ii, a, sq = gates_ref[0], gates_ref[1], gates_ref[2], gates_ref[3]
        sp = _softplus(-lam_ref[...])

        row = _row_ids((TM, D))
        _to_lane_tiles(m_sc, jnp.where(row == TM - 1, anext_sc[...], pltpu.roll(a, TM - 1, 0)))
        anext_sc[...] = a[0:1, :]
        _to_lane_tiles(g_sc, dz * gel)
        gcarry_sc[...] = _chain_scan(m_sc, g_sc, gcarry_sc[...], reverse=True)
        gg = _from_lane_tiles(g_sc)

        hlast = jnp.where(tile > 0, hrp_ref[7:8, :], 0.0)
        hprev = jnp.where(row == 0, hlast, pltpu.roll(hr, 1, 0))
        d_a = gg * hprev
        dsq = gg * ii * xr
        dii = gg * sq * xr
        dxr = gg * sq * ii
        dlog = d_a * a - dsq * (a * a / sq)
        dsp_sc[...] += jnp.sum(dlog * (-8.0 * ra), axis=0, keepdims=True)
        dpa = dlog * (-8.0 * sp) * ra * (1.0 - ra)
        dpx = dii * ii * (1.0 - ii)
        dba_ref[...] += jnp.sum(dpa, axis=0, keepdims=True)
        dbx_ref[...] += jnp.sum(dpx, axis=0, keepdims=True)
        dpab = dpa.astype(bf16)
        dpxb = dpx.astype(bf16)
        xrb = xr.astype(bf16)
        back = []
        for hh in range(NHEAD):
            cols = slice(hh * HD, (hh + 1) * HD)
            back.append(_nt_dot(dpab[:, cols], wa_ref[hh]) + _nt_dot(dpxb[:, cols], wx_ref[hh]))
            dwa_sc[hh] += _tn_dot(xrb[:, cols], dpab[:, cols])
            dwx_sc[hh] += _tn_dot(xrb[:, cols], dpxb[:, cols])
        dxr = dxr + jnp.concatenate(back, axis=1)

        dext_sc[0:TM, :] = dxr
        de = dext_sc[...]
        dx = cw_ref[KC4 - 1:KC4, :] * dxr
        for k in range(KC4 - 1):
            dx = dx + cw_ref[k:k + 1, :] * pltpu.roll(de, TM + 8 - (KC4 - 1 - k), 0)[0:TM]
        dext_sc[TM:TM + 8, :] = dxr[0:8]
        dxy_ref[:, 0:D] = dx.astype(bf16)

        x = x_ref[...]
        xext_sc[0:8, :] = jnp.where(tile > 0, xp_ref[...], 0.0)
        xext_sc[8:8 + TM, :] = x
        xe = xext_sc[...]
        dcw_ref[KC4 - 1:KC4, :] += jnp.sum(dxr * x, axis=0, keepdims=True)
        for k in range(KC4 - 1):
            xs = pltpu.roll(xe, KC4 - 1 - k, 0)[8:8 + TM]
            dcw_ref[k:k + 1, :] += jnp.sum(dxr * xs, axis=0, keepdims=True)
        dcb_ref[...] += jnp.sum(dxr, axis=0, keepdims=True)

        @pl.when(i == nt - 1)
        def _():
            for hh in range(NHEAD):
                for qc in range(NCHIP):
                    dwa_ref[qc, hh] = dwa_sc[hh, qc * hq:(qc + 1) * hq, :].astype(bf16)
                    dwx_ref[qc, hh] = dwx_sc[hh, qc * hq:(qc + 1) * hq, :].astype(bf16)
            dlam_ref[...] = -dsp_sc[...] * _sigmoid(-lam_ref[...])

    rev = lambda i: (nt - 1 - i, 0)
    vec = pl.BlockSpec((1, D), lambda i: (0, 0))
    prev8 = lambda i: jnp.maximum((nt - 1 - i) * ng - 1, 0)
    wblk = pl.BlockSpec((NHEAD, HD, HD), lambda i: (0, 0, 0))
    gblk = pl.BlockSpec((NCHIP, NHEAD, hq, HD), lambda i: (0, 0, 0, 0))
    return pl.pallas_call(
        body, name="rnn_bwd", grid=(nt,),
        in_specs=[
            pl.BlockSpec((TM, D), rev),
            pl.BlockSpec((TM, D), rev),
            pl.BlockSpec((TM, D), rev),
            pl.BlockSpec((8, D), lambda i: (prev8(i), 0)),
            pl.BlockSpec((TM, D), lambda i: (nt - 1 - i, 0)),
            pl.BlockSpec((8, D), lambda i: (prev8(i), 0)),
            pl.BlockSpec((TM, D), lambda i: (nt - 1 - i, 1)),
            pl.BlockSpec(memory_space=pl.ANY),
            pl.BlockSpec((KC4, D), lambda i: (0, 0)),
            wblk, pl.BlockSpec((4, TM, D), lambda i: (0, nt - 1 - i, 0)), wblk, vec,
        ],
        out_specs=[
            pl.BlockSpec((TM, 2 * D), lambda i: (nt - 1 - i, 0)),
            gblk, gblk,
            pl.BlockSpec((KC4, D), lambda i: (0, 0)),
            vec, vec, vec, vec,
        ],
        out_shape=[jax.ShapeDtypeStruct((t, NIN), bf16),
                   jax.ShapeDtypeStruct((NCHIP, NHEAD, hq, HD), bf16),
                   jax.ShapeDtypeStruct((NCHIP, NHEAD, hq, HD), bf16),
                   jax.ShapeDtypeStruct((KC4, D), f32),
                   jax.ShapeDtypeStruct((1, D), f32), jax.ShapeDtypeStruct((1, D), f32),
                   jax.ShapeDtypeStruct((1, D), f32), jax.ShapeDtypeStruct((1, D), f32)],
        scratch_shapes=[pltpu.VMEM((1, D), f32), pltpu.VMEM((1, D), f32),
                        pltpu.VMEM((TM + 8, D), f32), pltpu.VMEM((TM + 8, D), f32),
                        pltpu.VMEM((LT, TM, 128), f32), pltpu.VMEM((LT, TM, 128), f32),
                        pltpu.VMEM((NHEAD, HD, HD), f32), pltpu.VMEM((NHEAD, HD, HD), f32),
                        pltpu.VMEM((1, D), f32)],
        input_output_aliases={7: 0},
        compiler_params=_cp(1),
    )(dz, xr, hr, hr, proj, proj, proj, dproj, cw, wa, gates, wx, lam)


def _inproj_bwd(dproj, dh, h, g, win, after=None):
    t = h.shape[0]
    tn = NIN // NCHIP
    nj = NIN // tn

    def body(dp_ref, dh_ref, h_ref, g_ref, w_ref, dhi_ref, dg_ref, db_ref, dn_sc):
        i = pl.program_id(0)
        j = pl.program_id(1)

        @pl.when(jnp.logical_and(i == 0, j == 0))
        def _():
            dg_ref[...] = jnp.zeros_like(dg_ref)
            db_ref[...] = jnp.zeros_like(db_ref)

        @pl.when(j == 0)
        def _():
            dn_sc[...] = jnp.zeros_like(dn_sc)

        dp = dp_ref[...]
        dn_sc[...] += _nt_dot(dp, w_ref[j])
        db_ref[j] += jnp.sum(dp.astype(f32), axis=0, keepdims=True)

        @pl.when(j == nj - 1)
        def _():
            dhin, dg = _rms_bwd(dn_sc[...], h_ref[...], g_ref[...])
            dhi_ref[...] = dh_ref[...] + dhin
            dg_ref[...] += dg

    rowd = pl.BlockSpec((TM, D), lambda i, j: (i, 0))
    vec = pl.BlockSpec((1, D), lambda i, j: (0, 0))
    body, in_specs, args = _ordered(
        body,
        [pl.BlockSpec((TM, tn), lambda i, j: (i, j)), rowd, rowd, vec,
         RESIDENT],
        (dproj, dh, h, g, win), after)
    return pl.pallas_call(
        body, name="inproj_bwd", grid=(t // TM, nj),
        in_specs=in_specs,
        out_specs=[rowd, vec, pl.BlockSpec((nj, 1, tn), lambda i, j: (0, 0, 0))],
        out_shape=[jax.ShapeDtypeStruct((t, D), f32), jax.ShapeDtypeStruct((1, D), f32),
                   jax.ShapeDtypeStruct((nj, 1, tn), f32)],
        scratch_shapes=[pltpu.VMEM((TM, D), f32)],
        compiler_params=_cp(2),
    )(*args)


def _ffn_gu_grad(n, dgate, dup, tag, after=None):
    half = _tn_matmul(n, dgate, D, FS, (NCHIP, D, FS), (None, D, FS), lambda k, nn, m: (nn, 0, 0),
                      tag + "_dwg", after=after)
    return _tn_matmul(n, dup, D, FS, (NCHIP, D, FS), (None, D, FS), lambda k, nn, m: (2 + nn, 0, 0),
                      tag + "_dwu", base=half)


def _ffn_down_grad(a, df, tag, after=None):
    return _tn_matmul(a, df, FS, D, (F, D), (FS, D), lambda k, nn, m: (k, 0), tag + "_dwd", after=after)


def _square_grad(a, b, name):
    return _tn_matmul(a, b, D, D, (D, D), (D, D), lambda k, nn, m: (0, 0), name)


ANY = pl.BlockSpec(memory_space=pl.ANY)


def _place():
    x, y, c = lax.axis_index("x"), lax.axis_index("y"), lax.axis_index("c")
    chips = [(1 - x, y), (x, 1 - y), (1 - x, 1 - y)]
    return x, y, c, chips


def _chip_id(chip):
    return 2 * chip[0] + chip[1]


def _cast_into_slot(w2d, qc, dtype, name, after=None):
    r, cc = w2d.shape
    hr = r // 2

    def body(qc_ref, *refs):
        del qc_ref
        w_ref, o_ref = refs[-2:]
        o_ref[...] = w_ref[...].astype(dtype)

    in_specs, args = [pl.BlockSpec((hr, cc), lambda h, qc_ref: (h, 0))], (w2d,)
    if after is not None:
        in_specs, args = [ANY_SPEC] + in_specs, (after,) + args
    return pl.pallas_call(
        body, name=name,
        grid_spec=pltpu.PrefetchScalarGridSpec(
            num_scalar_prefetch=1, grid=(2,),
            in_specs=in_specs,
            out_specs=pl.BlockSpec((None, None, hr, cc), lambda h, qc_ref: (qc_ref[0], h, 0, 0))),
        out_shape=jax.ShapeDtypeStruct((NCHIP, 2, hr, cc), dtype),
        compiler_params=_cp(1),
    )(qc, *args)


def _place_pack(pack, qc):
    def body(qc_ref, p_ref, o_ref):
        del qc_ref
        o_ref[...] = p_ref[...]

    return pl.pallas_call(
        body, name="place_pack",
        grid_spec=pltpu.PrefetchScalarGridSpec(
            num_scalar_prefetch=1, grid=(1,),
            in_specs=[pl.BlockSpec(pack.shape, lambda i, qc_ref: (0, 0))],
            out_specs=pl.BlockSpec((None,) + pack.shape, lambda i, qc_ref: (2 * qc_ref[0] + qc_ref[1], 0, 0))),
        out_shape=jax.ShapeDtypeStruct((8,) + pack.shape, pack.dtype),
        compiler_params=_cp(1),
    )(qc, pack)


def _pair_add(parts, gots, qc, name):
    n = len(parts)

    def body(qc_ref, *refs):
        s = pl.program_id(0)
        for a in range(n):
            val = (refs[a][...].astype(f32) + refs[n + a][...].astype(f32)).astype(bf16)
            refs[2 * n + a][...] = val

            @pl.when(s == qc_ref[0])
            def _(val=val, land_ref=refs[3 * n + a]):
                land_ref[...] = val

    shapes = [p.shape[2:] for p in parts]
    mine = [pl.BlockSpec((None, None) + sh, lambda s, qc_ref: (s, qc_ref[1], 0, 0)) for sh in shapes]
    block = [pl.BlockSpec((None,) + sh, lambda s, qc_ref: (s, 0, 0)) for sh in shapes]
    own = [pl.BlockSpec((None,) + sh, lambda s, qc_ref: (qc_ref[0], 0, 0)) for sh in shapes]
    outs = pl.pallas_call(
        body, name=name,
        grid_spec=pltpu.PrefetchScalarGridSpec(
            num_scalar_prefetch=1, grid=(NCHIP,), in_specs=mine + block, out_specs=block + own),
        out_shape=[jax.ShapeDtypeStruct((NCHIP,) + sh, bf16) for sh in shapes] * 2,
        compiler_params=_cp(1),
    )(qc, *parts, *gots)
    return list(outs[:n]), list(outs[n:])


def _sum_chips(gots, name):
    n = len(gots)

    def body(*refs):
        for a in range(n):
            acc = refs[a][0].astype(f32)
            for s in range(1, NCHIP):
                acc = acc + refs[a][s].astype(f32)
            refs[n + a][...] = acc

    return list(pl.pallas_call(
        body, name=name, grid=(1,),
        in_specs=[pl.BlockSpec(g.shape, lambda i: (0, 0, 0)) for g in gots],
        out_specs=[pl.BlockSpec(g.shape[1:], lambda i: (0, 0)) for g in gots],
        out_shape=[jax.ShapeDtypeStruct(g.shape[1:], f32) for g in gots],
        compiler_params=_cp(1),
    )(*gots))


def _pair_share(halves, name, after=None):
    n = len(halves)
    extra = () if after is None else (after,)

    def body(*refs):
        refs = refs[len(extra):]
        ins, outs = refs[:n], refs[n:2 * n]
        send_sems, recv_sems = refs[2 * n:]
        x, y, c, _ = _place()
        copies = []
        for a in range(n):
            cp = pltpu.make_async_remote_copy(
                src_ref=ins[a], dst_ref=outs[a], send_sem=send_sems.at[a], recv_sem=recv_sems.at[a],
                device_id=(x, y, 1 - c), device_id_type=MESH)
            cp.start()
            copies.append(cp)
        for cp in copies:
            cp.wait()

    return pl.pallas_call(
        body, name=name,
        in_specs=[ANY] * (len(extra) + n), out_specs=[ANY] * n,
        out_shape=[jax.ShapeDtypeStruct(s.shape, s.dtype) for s in halves],
        scratch_shapes=[pltpu.SemaphoreType.DMA((n,)), pltpu.SemaphoreType.DMA((n,))],
    )(*extra, *halves)


def _all_copy(buf_ref, send_ref, recv_ref, k, x, y, c, landing):
    px, py, pc = (1 - x if k & 4 else x, 1 - y if k & 2 else y, 1 - c if k & 1 else c)
    me = 4 * x + 2 * y + c
    there = 4 * px + 2 * py + pc
    return pltpu.make_async_remote_copy(
        src_ref=buf_ref.at[me], dst_ref=buf_ref.at[there if landing else me],
        send_sem=send_ref.at[k - 1], recv_sem=recv_ref.at[k - 1],
        device_id=(px, py, pc), device_id_type=MESH)


def _gather_all_start(buf, name):
    def body(in_ref, send, recv, thru, token):
        del thru
        x, y, c, _ = _place()
        for k in range(1, 8):
            _all_copy(in_ref, send, recv, k, x, y, c, False).start()
        token[...] = jnp.zeros_like(token)

    return pl.pallas_call(
        body, name=name,
        in_specs=[HBM],
        out_specs=[SEM, SEM, HBM, pl.BlockSpec(memory_space=pltpu.VMEM)],
        out_shape=[pltpu.SemaphoreType.DMA((7,)), pltpu.SemaphoreType.DMA((7,)),
                   pltpu.HBM(buf.shape, buf.dtype), jax.ShapeDtypeStruct((8, 128), f32)],
        input_output_aliases={0: 2},
        compiler_params=pltpu.CompilerParams(has_side_effects=EFFECT),
    )(_in_hbm(buf))


def _gather_all_wait(send, recv, buf, after, name):
    def body(in_ref, send_r, recv_r, after_ref, out_ref):
        del after_ref, out_ref
        x, y, c, _ = _place()
        for k in range(1, 8):
            cp = _all_copy(in_ref, send_r, recv_r, k, x, y, c, True)
            cp.wait_send()
            cp.wait_recv()

    return pl.pallas_call(
        body, name=name,
        in_specs=[HBM, SEM, SEM, ANY],
        out_specs=HBM,
        out_shape=pltpu.HBM(buf.shape, buf.dtype),
        input_output_aliases={0: 0},
        compiler_params=pltpu.CompilerParams(has_side_effects=EFFECT),
    )(buf, send, recv, after)


HBM = pl.BlockSpec(memory_space=pltpu.HBM)
SEM = pl.BlockSpec(memory_space=pltpu.SEMAPHORE)
EFFECT = pltpu.SideEffectType.DATAFLOW_SIDE_EFFECTING
N_PEER = 3


def _in_hbm(a):
    return pltpu.with_memory_space_constraint(a, pltpu.HBM)


def _gather_copy(buf_ref, send_ref, recv_ref, j, chip, q, c, landing_chip):
    return pltpu.make_async_remote_copy(
        src_ref=buf_ref.at[q, c], dst_ref=buf_ref.at[landing_chip, c],
        send_sem=send_ref.at[j], recv_sem=recv_ref.at[j],
        device_id=(chip[0], chip[1], c), device_id_type=MESH)


def _gather_start(bufs, name):
    n = len(bufs)

    def body(*refs):
        ins = refs[:n]
        send, recv = refs[n:2 * n], refs[2 * n:3 * n]
        token = refs[4 * n]
        x, y, c, chips = _place()
        q = 2 * x + y
        for a in range(n):
            for j, chip in enumerate(chips):
                _gather_copy(ins[a], send[a], recv[a], j, chip, q, c, q).start()
        token[...] = jnp.zeros_like(token)

    sems = [pltpu.SemaphoreType.DMA((N_PEER,))] * (2 * n)
    outs = pl.pallas_call(
        body, name=name,
        in_specs=[HBM] * n,
        out_specs=[SEM] * (2 * n) + [HBM] * n + [pl.BlockSpec(memory_space=pltpu.VMEM)],
        out_shape=sems + [pltpu.HBM(b.shape, b.dtype) for b in bufs] + [jax.ShapeDtypeStruct((8, 128), f32)],
        input_output_aliases={a: 2 * n + a for a in range(n)},
        compiler_params=pltpu.CompilerParams(has_side_effects=EFFECT),
    )(*[_in_hbm(b) for b in bufs])
    return list(outs[:n]), list(outs[n:2 * n]), list(outs[2 * n:3 * n]), outs[3 * n]


def _gather_wait(send, recv, bufs, after, name):
    n = len(bufs)

    def body(*refs):
        ins = refs[:n]
        send_r, recv_r = refs[n:2 * n], refs[2 * n:3 * n]
        x, y, c, chips = _place()
        q = 2 * x + y
        for a in range(n):
            for j, chip in enumerate(chips):
                cp = _gather_copy(ins[a], send_r[a], recv_r[a], j, chip, q, c, _chip_id(chip))
                cp.wait_send()
                cp.wait_recv()

    afters = after if isinstance(after, (tuple, list)) else (after,)
    outs = pl.pallas_call(
        body, name=name,
        in_specs=[HBM] * n + [SEM] * (2 * n) + [ANY] * len(afters),
        out_specs=[HBM] * n,
        out_shape=[pltpu.HBM(b.shape, b.dtype) for b in bufs],
        input_output_aliases={a: a for a in range(n)},
        compiler_params=pltpu.CompilerParams(has_side_effects=EFFECT),
    )(*bufs, *send, *recv, *afters)
    return list(outs)


def _forward_halves(bufs, name):
    n = len(bufs)

    def body(*refs):
        outs = refs[n:2 * n]
        send_sems, recv_sems = refs[2 * n:]
        x, y, c, chips = _place()
        sibling = (x, y, 1 - c)

        def remote(a, j, blk):
            return pltpu.make_async_remote_copy(src_ref=blk, dst_ref=blk, send_sem=send_sems.at[a, j],
                                                recv_sem=recv_sems.at[a, j], device_id=sibling,
                                                device_id_type=MESH)

        sent = []
        for a in range(n):
            for j, chip in enumerate(chips):
                cp = remote(a, j, outs[a].at[_chip_id(chip), c])
                cp.start()
                sent.append(cp)
        for a in range(n):
            for j, chip in enumerate(chips):
                remote(a, j, outs[a].at[_chip_id(chip), 1 - c]).wait_recv()
        for cp in sent:
            cp.wait_send()

    return pl.pallas_call(
        body, name=name,
        in_specs=[ANY] * n, out_specs=[ANY] * n,
        out_shape=[jax.ShapeDtypeStruct(s.shape, s.dtype) for s in bufs],
        scratch_shapes=[pltpu.SemaphoreType.DMA((n, N_PEER)), pltpu.SemaphoreType.DMA((n, N_PEER))],
        input_output_aliases={a: a for a in range(n)},
    )(*bufs)


def _reduce_copy(sum_ref, land_ref, send_ref, recv_ref, j, chip, q, c, landing_chip):
    return pltpu.make_async_remote_copy(
        src_ref=sum_ref.at[_chip_id(chip)], dst_ref=land_ref.at[landing_chip],
        send_sem=send_ref.at[j], recv_sem=recv_ref.at[j],
        device_id=(chip[0], chip[1], c), device_id_type=MESH)


def _reduce_start(sums, lands, name):
    n = len(sums)

    def body(*refs):
        s_in, l_in = refs[:n], refs[n:2 * n]
        send, recv = refs[2 * n:3 * n], refs[3 * n:4 * n]
        token = refs[6 * n]
        x, y, c, chips = _place()
        q = 2 * x + y
        for a in range(n):
            for j, chip in enumerate(chips):
                _reduce_copy(s_in[a], l_in[a], send[a], recv[a], j, chip, q, c, q).start()
        token[...] = jnp.zeros_like(token)

    sems = [pltpu.SemaphoreType.DMA((N_PEER,))] * (2 * n)
    outs = pl.pallas_call(
        body, name=name,
        in_specs=[HBM] * (2 * n),
        out_specs=[SEM] * (2 * n) + [HBM] * (2 * n) + [pl.BlockSpec(memory_space=pltpu.VMEM)],
        out_shape=sems + [pltpu.HBM(b.shape, b.dtype) for b in list(sums) + list(lands)]
        + [jax.ShapeDtypeStruct((8, 128), f32)],
        input_output_aliases={a: 2 * n + a for a in range(2 * n)},
        compiler_params=pltpu.CompilerParams(has_side_effects=EFFECT),
    )(*[_in_hbm(b) for b in list(sums) + list(lands)])
    return (list(outs[:n]), list(outs[n:2 * n]), list(outs[2 * n:3 * n]), list(outs[3 * n:4 * n]),
            outs[4 * n])


def _reduce_wait(send, recv, sums, lands, after, name):
    n = len(sums)

    def body(*refs):
        s_in, l_in = refs[:n], refs[n:2 * n]
        send_r, recv_r = refs[2 * n:3 * n], refs[3 * n:4 * n]
        x, y, c, chips = _place()
        q = 2 * x + y
        for a in range(n):
            for j, chip in enumerate(chips):
                cp = _reduce_copy(s_in[a], l_in[a], send_r[a], recv_r[a], j, chip, q, c, _chip_id(chip))
                cp.wait_send()
                cp.wait_recv()

    afters = after if isinstance(after, (tuple, list)) else (after,)
    outs = pl.pallas_call(
        body, name=name,
        in_specs=[HBM] * (2 * n) + [SEM] * (2 * n) + [ANY] * len(afters),
        out_specs=[HBM] * (2 * n),
        out_shape=[pltpu.HBM(b.shape, b.dtype) for b in list(sums) + list(lands)],
        input_output_aliases={a: a for a in range(2 * n)},
        compiler_params=pltpu.CompilerParams(has_side_effects=EFFECT),
    )(*sums, *lands, *send, *recv, *afters)
    return list(outs[n:])


def _sibling_copy(part_ref, land_ref, send_ref, recv_ref, x, y, c):
    return pltpu.make_async_remote_copy(
        src_ref=part_ref.at[:, 1 - c], dst_ref=land_ref, send_sem=send_ref.at[0], recv_sem=recv_ref.at[0],
        device_id=(x, y, 1 - c), device_id_type=MESH)


def _pair_exchange_start(parts, name):
    n = len(parts)
    lands = [lax.empty((NCHIP,) + p.shape[2:], p.dtype) for p in parts]

    def body(*refs):
        p_in, l_in = refs[:n], refs[n:2 * n]
        send, recv = refs[2 * n:3 * n], refs[3 * n:4 * n]
        token = refs[6 * n]
        x, y, c, _ = _place()
        for a in range(n):
            _sibling_copy(p_in[a], l_in[a], send[a], recv[a], x, y, c).start()
        token[...] = jnp.zeros_like(token)

    sems = [pltpu.SemaphoreType.DMA((1,))] * (2 * n)
    outs = pl.pallas_call(
        body, name=name,
        in_specs=[HBM] * (2 * n),
        out_specs=[SEM] * (2 * n) + [HBM] * (2 * n) + [pl.BlockSpec(memory_space=pltpu.VMEM)],
        out_shape=sems + [pltpu.HBM(b.shape, b.dtype) for b in list(parts) + lands]
        + [jax.ShapeDtypeStruct((8, 128), f32)],
        input_output_aliases={a: 2 * n + a for a in range(2 * n)},
        compiler_params=pltpu.CompilerParams(has_side_effects=EFFECT),
    )(*[_in_hbm(b) for b in list(parts) + lands])
    return (list(outs[:n]), list(outs[n:2 * n]), list(outs[2 * n:3 * n]), list(outs[3 * n:4 * n]),
            outs[4 * n])


def _pair_exchange_wait(send, recv, parts, lands, after, name):
    n = len(parts)

    def body(*refs):
        p_in, l_in = refs[:n], refs[n:2 * n]
        send_r, recv_r = refs[2 * n:3 * n], refs[3 * n:4 * n]
        x, y, c, _ = _place()
        for a in range(n):
            cp = _sibling_copy(p_in[a], l_in[a], send_r[a], recv_r[a], x, y, c)
            cp.wait_send()
            cp.wait_recv()

    outs = pl.pallas_call(
        body, name=name,
        in_specs=[HBM] * (2 * n) + [SEM] * (2 * n) + [ANY],
        out_specs=[HBM] * (2 * n),
        out_shape=[pltpu.HBM(b.shape, b.dtype) for b in list(parts) + list(lands)],
        input_output_aliases={a: a for a in range(2 * n)},
        compiler_params=pltpu.CompilerParams(has_side_effects=EFFECT),
    )(*parts, *lands, *send, *recv, after)
    return list(outs[:n]), list(outs[n:])


def _forward_copy(buf_ref, send_ref, recv_ref, j, chip, x, y, c, landing):
    return pltpu.make_async_remote_copy(
        src_ref=buf_ref.at[_chip_id(chip), c], dst_ref=buf_ref.at[_chip_id(chip), 1 - c if landing else c],
        send_sem=send_ref.at[j], recv_sem=recv_ref.at[j], device_id=(x, y, 1 - c), device_id_type=MESH)


def _forward_start(bufs, name):
    n = len(bufs)

    def body(*refs):
        ins = refs[:n]
        send, recv = refs[n:2 * n], refs[2 * n:3 * n]
        token = refs[4 * n]
        x, y, c, chips = _place()
        for a in range(n):
            for j, chip in enumerate(chips):
                _forward_copy(ins[a], send[a], recv[a], j, chip, x, y, c, False).start()
        token[...] = jnp.zeros_like(token)

    sems = [pltpu.SemaphoreType.DMA((N_PEER,))] * (2 * n)
    outs = pl.pallas_call(
        body, name=name,
        in_specs=[HBM] * n,
        out_specs=[SEM] * (2 * n) + [HBM] * n + [pl.BlockSpec(memory_space=pltpu.VMEM)],
        out_shape=sems + [pltpu.HBM(b.shape, b.dtype) for b in bufs] + [jax.ShapeDtypeStruct((8, 128), f32)],
        input_output_aliases={a: 2 * n + a for a in range(n)},
        compiler_params=pltpu.CompilerParams(has_side_effects=EFFECT),
    )(*[_in_hbm(b) for b in bufs])
    return list(outs[:n]), list(outs[n:2 * n]), list(outs[2 * n:3 * n]), outs[3 * n]


def _forward_wait(send, recv, bufs, after, name):
    n = len(bufs)

    def body(*refs):
        ins = refs[:n]
        send_r, recv_r = refs[n:2 * n], refs[2 * n:3 * n]
        x, y, c, chips = _place()
        for a in range(n):
            for j, chip in enumerate(chips):
                cp = _forward_copy(ins[a], send_r[a], recv_r[a], j, chip, x, y, c, True)
                cp.wait_send()
                cp.wait_recv()

    outs = pl.pallas_call(
        body, name=name,
        in_specs=[HBM] * n + [SEM] * (2 * n) + [ANY],
        out_specs=[HBM] * n,
        out_shape=[pltpu.HBM(b.shape, b.dtype) for b in bufs],
        input_output_aliases={a: a for a in range(n)},
        compiler_params=pltpu.CompilerParams(has_side_effects=EFFECT),
    )(*bufs, *send, *recv, after)
    return list(outs)


def _adamw_math(w, g, m, v):
    m = ADAM_B1 * m + (1.0 - ADAM_B1) * g
    v = ADAM_B2 * v + (1.0 - ADAM_B2) * (g * g)
    m_hat = m / (1.0 - ADAM_B1 ** ADAM_STEP)
    v_hat = v / (1.0 - ADAM_B2 ** ADAM_STEP)
    delta = -ADAM_LR * (m_hat / (jnp.sqrt(v_hat) + ADAM_EPS) + ADAM_WD * w)
    return delta, m, v


ADAMW_BLOCK_BYTES = 3 << 19


def _adamw(ws, mines, theirs, ms, vs, qc, name):
    n = len(ws)
    halves = [w.shape[0] // 2 for w in ws]
    nb = next(k for k in range(1, min(halves) + 1)
              if all(hr % k == 0 and (hr // k) % 8 == 0 and (hr // k) * w.shape[1] * 4 <= ADAMW_BLOCK_BYTES
                     for hr, w in zip(halves, ws)))

    def body(qc_ref, *refs):
        mine_here = pl.program_id(0) == qc_ref[1]
        for a in range(n):
            w_ref, a_ref, b_ref, m_ref, v_ref = (refs[k * n + a] for k in range(5))
            g_ref, d_ref, mo_ref, vo_ref = (refs[(5 + k) * n + a] for k in range(4))
            g = jnp.where(mine_here, a_ref[...], b_ref[...])
            g_ref[...] = g
            d_ref[...], mo_ref[...], vo_ref[...] = _adamw_math(w_ref[...], g, m_ref[...], v_ref[...])

    blocks = [(hr // nb, w.shape[1]) for hr, w in zip(halves, ws)]
    full = [pl.BlockSpec(b, lambda h, i, qc_ref: (h * nb + i, 0)) for b in blocks]
    half = [pl.BlockSpec(b, lambda h, i, qc_ref: (i, 0)) for b in blocks]
    outs = pl.pallas_call(
        body, name=name,
        grid_spec=pltpu.PrefetchScalarGridSpec(
            num_scalar_prefetch=1, grid=(2, nb),
            in_specs=full + half + half + full + full, out_specs=full * 4),
        out_shape=[jax.ShapeDtypeStruct(w.shape, f32) for w in ws] * 4,
        compiler_params=_cp(2),
    )(qc, *ws, *mines, *theirs, *ms, *vs)
    return [tuple(outs[k * n + a] for k in range(4)) for a in range(n)]


REPL = [("ffn1_norm", 1), ("mix_norm", 1), ("b_in", 6), ("rnn_conv_b", 1), ("rg_b_a", 1), ("rg_b_x", 1),
        ("rg_lambda", 1), ("conv_dw_b", 1), ("conv_ln_g", 1), ("conv_ln_b", 1), ("conv_b_proj", 1),
        ("ffn2_norm", 1), ("final_norm", 1)]
COLSH = [("meta_tokens", NMETA), ("rnn_conv_w", KC4), ("conv_dw_w", KC31)]
SMALL = REPL + COLSH
CS = D // NCHIP


def _pack_rows():
    starts, row = {}, 0
    for k, rows in REPL:
        starts[k] = row
        row += rows
    for k, rows in COLSH:
        row = -(-row // 8) * 8
        starts[k] = row
        row += rows
    return starts, -(-row // 8) * 8


PACK_START, LOSS_ROW = _pack_rows()
SMALL_ROWS = LOSS_ROW + 8


def _small_pack(g, loss_row):
    pieces, row = [], 0
    for k, rows in SMALL:
        if PACK_START[k] > row:
            pieces.append(jnp.zeros((PACK_START[k] - row, D), f32))
        pieces.append(g[k].reshape(rows, D))
        row = PACK_START[k] + rows
    pieces.append(jnp.zeros((LOSS_ROW - row, D), f32))
    pieces.append(loss_row)
    pieces.append(jnp.zeros((SMALL_ROWS - LOSS_ROW - 1, D), f32))
    return jnp.concatenate(pieces, axis=0)


def _adamw_small(packs, ws, ms, vs):
    ns = len(SMALL)

    def body(*refs):
        pack_ref = refs[0]
        w_refs, m_refs, v_refs = refs[1:1 + ns], refs[1 + ns:1 + 2 * ns], refs[1 + 2 * ns:1 + 3 * ns]
        outs = refs[1 + 3 * ns:1 + 7 * ns]
        g_refs, d_refs, mo_refs, vo_refs = outs[:ns], outs[ns:2 * ns], outs[2 * ns:3 * ns], outs[3 * ns:]
        loss_ref = refs[1 + 7 * ns]
        gsum_sc = refs[2 + 7 * ns]
        q = 2 * lax.axis_index("x") + lax.axis_index("y")
        acc = pack_ref[0]
        for dev in range(1, 8):
            acc = acc + pack_ref[dev]
        gsum_sc[...] = acc
        loss_ref[...] = gsum_sc[LOSS_ROW:LOSS_ROW + 1, :]
        for idx, (name, rows) in enumerate(SMALL):
            row = PACK_START[name]
            if idx < len(REPL):
                for k in range(rows):
                    cols = slice(k * D, (k + 1) * D)
                    g = gsum_sc[row + k:row + k + 1, :]
                    d, mm, vv = _adamw_math(w_refs[idx][:, cols], g, m_refs[idx][:, cols], v_refs[idx][:, cols])
                    g_refs[idx][:, cols] = g
                    d_refs[idx][:, cols] = d
                    mo_refs[idx][:, cols] = mm
                    vo_refs[idx][:, cols] = vv
            else:
                g = gsum_sc[row:row + rows, pl.ds(pl.multiple_of(q * CS, CS), CS)]
                d, mm, vv = _adamw_math(w_refs[idx][...], g, m_refs[idx][...], v_refs[idx][...])
                g_refs[idx][...] = g
                d_refs[idx][...] = d
                mo_refs[idx][...] = mm
                vo_refs[idx][...] = vv

    shapes = [jax.ShapeDtypeStruct(w.shape, f32) for w in ws]
    return pl.pallas_call(
        body, name="adamw_small",
        out_shape=shapes * 4 + [jax.ShapeDtypeStruct((1, D), f32)],
        scratch_shapes=[pltpu.VMEM((SMALL_ROWS, D), f32)],
        compiler_params=pltpu.CompilerParams(vmem_limit_bytes=VMEM_LIMIT),
    )(packs, *ws, *ms, *vs)


WEIGHTS = ['meta_tokens', 'ffn1_norm', 'ffn1_w_gu', 'ffn1_w_down', 'mix_norm', 'w_in', 'b_in', 'rnn_conv_w',
           'rnn_conv_b', 'rg_w_a', 'rg_b_a', 'rg_w_x', 'rg_b_x', 'rg_lambda', 'rnn_w_proj', 'conv_dw_w',
           'conv_dw_b', 'conv_ln_g', 'conv_ln_b', 'conv_w_proj', 'conv_b_proj', 'w_out', 'ffn2_norm',
           'ffn2_w_gu', 'ffn2_w_down', 'final_norm']


def _as2d(a):
    return a.reshape(-1, a.shape[-1])


def _step(x, loss_target, w, m, v):
    seq = x.shape[1]
    n_valid = NMETA + seq
    t = -(-n_valid // TM) * TM

    qc = jnp.stack([2 * lax.axis_index("x") + lax.axis_index("y"), lax.axis_index("c")]).astype(jnp.int32)
    p = {k: w[k].reshape(1, rows * D) for k, rows in REPL}

    first = ["ffn1_w_gu", "ffn1_w_down", "small"]
    later = [["w_in"], ["rg_w_a", "rg_w_x", "rnn_w_proj", "conv_w_proj", "w_out"], ["ffn2_w_gu", "ffn2_w_down"]]
    small_rows = sum(r for _, r in COLSH)
    small = jnp.concatenate([_as2d(w[k]) for k, _ in COLSH] + [jnp.zeros((64 - small_rows, CS), f32)], axis=0)

    def cast(k, token=None):
        src, dtype = (small, f32) if k == "small" else (_as2d(w[k]), bf16)
        return _cast_into_slot(src, qc, dtype, "cast_" + k, after=token)

    send1, recv1, bufs1, token1 = _gather_start([cast(k) for k in first], "gather_start_first")
    rest = [k for grp in later for k in grp]
    send2, recv2, bufs2, token2 = _gather_start([cast(k, token1) for k in rest], "gather_start_rest")

    def install(names, done):
        for k, b in zip(names, done):
            full = b.reshape(NCHIP, 2 * b.shape[2], b.shape[3])
            if k in ("ffn1_w_down", "ffn2_w_down"):
                full = full.reshape(F, D)
            elif k in ("rnn_w_proj", "conv_w_proj", "w_out"):
                full = full.reshape(D, D)
            elif k in ("rg_w_a", "rg_w_x"):
                full = full.reshape(NCHIP, NHEAD, HD // NCHIP, HD).transpose(1, 0, 2, 3).reshape(NHEAD, HD, HD)
            p[k] = full

    def finish(names, send, recv, bufs, after, tag):
        install(names, _forward_halves(_gather_wait(send, recv, bufs, after, "gather_wait_" + tag),
                                       "gather_forward_" + tag))

    def group(names):
        idx = [rest.index(k) for k in names]
        return names, [send2[i] for i in idx], [recv2[i] for i in idx], [bufs2[i] for i in idx]

    h0 = jnp.pad(x[0] + token1[0:1, 0:1], ((NMETA, t - n_valid), (0, 0)))
    tgt = jnp.pad(loss_target[0] + token2[0:1, 0:1], ((NMETA, t - n_valid), (0, 0)))
    finish(first, send1, recv1, bufs1, (token2, h0, tgt), "first")
    small_full = p.pop("small").transpose(1, 0, 2).reshape(64, D)
    row = 0
    for k, rows in COLSH:
        p[k] = small_full[row:row + rows]
        row += rows

    h0 = lax.dynamic_update_slice(h0, p["meta_tokens"], (0, 0))
    h1, gate1, up1, n1 = _ffn_fwd(h0, p["ffn1_norm"], p["ffn1_w_gu"], p["ffn1_w_down"], "ffn1_fwd")
    finish(*group(later[0]), h1, "in")
    proj, n2 = _inproj_fwd(h1, p["mix_norm"], p["w_in"], p["b_in"])
    names_l = later[1] + later[2]
    _, send_l, recv_l, bufs_l = group(names_l)
    send_f, recv_f, bufs_f, token = _forward_start(
        _gather_wait(send_l, recv_l, bufs_l, proj, "gather_wait_late"), "gather_forward_start")
    vc, s = _conv_fwd(proj, p["conv_dw_w"], p["conv_dw_b"], p["conv_ln_g"], p["conv_ln_b"], after=token)
    install(names_l, _forward_wait(send_f, recv_f, bufs_f, vc, "gather_forward_wait"))
    xr, hr, z, gates = _rnn_fwd(proj, p["rnn_conv_w"], p["rnn_conv_b"], p["rg_w_a"], p["rg_b_a"],
                         p["rg_w_x"], p["rg_b_x"], p["rg_lambda"])
    h2 = _merge_fwd(h1, z, s, proj, p["rnn_w_proj"], p["conv_w_proj"], p["conv_b_proj"], p["w_out"])
    dh3, loss_blk, d_final, gate2, up2, n3 = _ffn_fwd(
        h2, p["ffn2_norm"], p["ffn2_w_gu"], p["ffn2_w_down"], "ffn2_fwd",
        loss_head=(p["final_norm"], tgt, n_valid))

    g = {"final_norm": d_final}
    pending = []

    def exchange_start(names, tag):
        parts = []
        for k in names:
            rows = g[k].size // (NCHIP * g[k].shape[-1])
            parts.append(g[k].reshape((NCHIP, 2, rows // 2, g[k].shape[-1])))
        send, recv, parts, lands, token = _pair_exchange_start(parts, "pair_exchange_start_" + tag)
        return (names, tag, send, recv, parts, lands), token

    def reduce_start(state, after):
        names, tag, send, recv, parts, lands = state
        parts, from_sibling = _pair_exchange_wait(send, recv, parts, lands, after, "pair_exchange_wait_" + tag)
        sums, lands = _pair_add(parts, from_sibling, qc, "pair_add_" + tag)
        send, recv, sums, lands, token = _reduce_start(sums, lands, "reduce_start_" + tag)
        pending.append((names, tag, send, recv, sums, lands))
        return token

    dh2, dgate2, dup2, a2, df2, g["ffn2_norm"] = _ffn_bwd(
        dh3, h2, p["ffn2_norm"], gate2, up2, p["ffn2_w_gu"], p["ffn2_w_down"], "ffn2_bwd")
    g["ffn2_w_gu"] = _ffn_gu_grad(n3, dgate2, dup2, "ffn2")
    g["ffn2_w_down"] = _ffn_down_grad(a2, df2, "ffn2")
    state, token = exchange_start(["ffn2_w_gu", "ffn2_w_down"], "ffn2")

    dz, ds, dproj, dh2b, merged, dya, dyb, g["conv_b_proj"] = _merge_bwd(
        dh2, z, s, proj, p["rnn_w_proj"], p["conv_w_proj"], p["conv_b_proj"], p["w_out"], after=token)
    token = reduce_start(state, dz)
    dproj, g["conv_dw_w"], g["conv_dw_b"], g["conv_ln_g"], g["conv_ln_b"] = _conv_bwd(
        ds, vc, proj, dproj, p["conv_dw_w"], p["conv_ln_g"], p["conv_ln_b"], after=token)
    g["w_out"] = _square_grad(merged, dh2b, "dw_out")
    g["rnn_w_proj"] = _square_grad(z, dya, "dw_rnn_proj")
    g["conv_w_proj"] = _square_grad(s, dyb, "dw_conv_proj")
    (dproj, g["rg_w_a"], g["rg_w_x"], g["rnn_conv_w"], g["rnn_conv_b"], g["rg_b_a"], g["rg_b_x"],
     g["rg_lambda"]) = _rnn_bwd(dz, xr, hr, gates, proj, dproj, p["rnn_conv_w"], p["rg_w_a"],
                                p["rg_w_x"], p["rg_lambda"])

    dh1, g["mix_norm"], db_in = _inproj_bwd(dproj, dh2, h1, p["mix_norm"], p["w_in"])
    g["b_in"] = db_in.reshape(1, NIN)
    g["w_in"] = _tn_matmul(n2, dproj, D, NIN // NCHIP, (NCHIP, D, NIN // NCHIP),
                           (None, D, NIN // NCHIP), lambda k, nn, mm: (nn, 0, 0), "dw_in")
    state, token = exchange_start(["w_out", "rnn_w_proj", "conv_w_proj", "rg_w_a", "rg_w_x", "w_in"], "mix")

    dh0, dgate1, dup1, a1, df1, g["ffn1_norm"] = _ffn_bwd(
        dh1, h0, p["ffn1_norm"], gate1, up1, p["ffn1_w_gu"], p["ffn1_w_down"], "ffn1_bwd", after=token)
    g["meta_tokens"] = dh0[0:NMETA]
    grad_x = dh0[NMETA:n_valid][None]
    token = reduce_start(state, dh0)

    send_s, recv_s, pack_buf, token_s = _gather_all_start(
        _place_pack(_small_pack(g, loss_blk.reshape(1, D)), qc), "gather_all_start")
    g["ffn1_w_down"] = _ffn_down_grad(a1, df1, "ffn1", after=(token, token_s))
    state, token = exchange_start(["ffn1_w_down"], "ffn1_down")
    gate_half = _tn_matmul(n1, dgate1, D, FS, (NCHIP, D, FS), (None, D, FS), lambda k, nn, mm: (nn, 0, 0),
                           "ffn1_dwg", after=token)
    token = reduce_start(state, gate_half)
    g["ffn1_w_gu"] = _tn_matmul(n1, dup1, D, FS, (NCHIP, D, FS), (None, D, FS), lambda k, nn, mm: (2 + nn, 0, 0),
                                "ffn1_dwu", base=gate_half, after=token)
    state_gu, token = exchange_start(["ffn1_w_gu"], "ffn1_gu")
    packs = _gather_all_wait(send_s, recv_s, pack_buf, token, "gather_all_wait")

    grads, deltas, new_m, new_v = {}, {}, {}, {}

    def landed_sums(items, after):
        names, mine = [], []
        for grp_names, grp_tag, send, recv, sums, lands in items:
            landed = _reduce_wait(send, recv, sums, lands, after, "reduce_wait_" + grp_tag)
            mine += _sum_chips(landed, "sum_chips_" + grp_tag)
            names += grp_names
            after = mine[-1]
        return names, mine

    def share_and_update(names, mine, tag, after=None):
        theirs = _pair_share(mine, "pair_share_" + tag, after=after)
        got = dict(zip(names, zip(mine, theirs)))
        square = [k for k in names if got[k][0].shape[0] * 2 <= HD]
        for batch in [[k] for k in names if k not in square] + ([square] if square else []):
            outs = _adamw([_as2d(w[k]) for k in batch], [got[k][0] for k in batch], [got[k][1] for k in batch],
                          [_as2d(m[k]) for k in batch], [_as2d(v[k]) for k in batch], qc,
                          "adamw_" + (batch[0] if len(batch) == 1 else "mixer"))
            for k, out in zip(batch, outs):
                grads[k], deltas[k], new_m[k], new_v[k] = (a.reshape(w[k].shape) for a in out)
        return [new_v[k] for k in names]

    early_names, early_mine = landed_sums(pending[:2], packs)
    token = reduce_start(state_gu, early_mine[-1])
    after = share_and_update(early_names, early_mine, "early", after=token)
    share_and_update(*landed_sums(pending[2:], after), "late")
    names = [k for k, _ in SMALL]
    shape2 = {k: ((1, rows * D) if (k, rows) in REPL else (rows, CS)) for k, rows in SMALL}
    outs = _adamw_small(packs, *[[a[k].reshape(shape2[k]) for k in names] for a in (w, m, v)])
    ns = len(names)
    for i, k in enumerate(names):
        grads[k], deltas[k], new_m[k], new_v[k] = (outs[j * ns + i].reshape(w[k].shape) for j in range(4))

    loss = outs[4 * ns][0, 0]
    return (loss, grad_x, *[grads[k] for k in WEIGHTS], *[deltas[k] for k in WEIGHTS],
            *[new_m[k] for k in WEIGHTS], *[new_v[k] for k in WEIGHTS])


def kernel(x, meta_tokens, ffn1_norm, ffn1_w_gu, ffn1_w_down, mix_norm, w_in, b_in, rnn_conv_w, rnn_conv_b, rg_w_a, rg_b_a, rg_w_x, rg_b_x, rg_lambda, rnn_w_proj, conv_dw_w, conv_dw_b, conv_ln_g, conv_ln_b, conv_w_proj, conv_b_proj, w_out, ffn2_norm, ffn2_w_gu, ffn2_w_down, final_norm, loss_target, m_meta_tokens, m_ffn1_norm, m_ffn1_w_gu, m_ffn1_w_down, m_mix_norm, m_w_in, m_b_in, m_rnn_conv_w, m_rnn_conv_b, m_rg_w_a, m_rg_b_a, m_rg_w_x, m_rg_b_x, m_rg_lambda, m_rnn_w_proj, m_conv_dw_w, m_conv_dw_b, m_conv_ln_g, m_conv_ln_b, m_conv_w_proj, m_conv_b_proj, m_w_out, m_ffn2_norm, m_ffn2_w_gu, m_ffn2_w_down, m_final_norm, v_meta_tokens, v_ffn1_norm, v_ffn1_w_gu, v_ffn1_w_down, v_mix_norm, v_w_in, v_b_in, v_rnn_conv_w, v_rnn_conv_b, v_rg_w_a, v_rg_b_a, v_rg_w_x, v_rg_b_x, v_rg_lambda, v_rnn_w_proj, v_conv_dw_w, v_conv_dw_b, v_conv_ln_g, v_conv_ln_b, v_conv_w_proj, v_conv_b_proj, v_w_out, v_ffn2_norm, v_ffn2_w_gu, v_ffn2_w_down, v_final_norm):
    args = locals()
    w = {k: args[k] for k in WEIGHTS}
    m = {k: args["m_" + k] for k in WEIGHTS}
    v = {k: args["v_" + k] for k in WEIGHTS}
    return _step(x, loss_target, w, m, v)
```

```python
import jax
import jax.numpy as jnp
from jax import lax
from jax.experimental import pallas as pl
from jax.experimental.pallas import tpu as pltpu

f32 = jnp.float32
bf16 = jnp.bfloat16

D = 1024
F = 2816
FS = F // 2
NIN = 6 * D
NMETA = 16
NHEAD = 4
HD = D // NHEAD
KC4 = 4
KC31 = 31
HALO = 32
EPS = 1e-6
TM = 416
NCHIP = 4
MESH = pl.DeviceIdType.MESH

ADAM_LR = 0.001
ADAM_B1 = 0.9
ADAM_B2 = 0.999
ADAM_EPS = 1e-08
ADAM_WD = 0.01
ADAM_STEP = 10

VMEM_LIMIT = 56 * 1024 * 1024
FSUB = [(o, min(256, FS - o)) for o in range(0, FS, 256)]


def _cp(n_axes, **kw):
    return pltpu.CompilerParams(dimension_semantics=("arbitrary",) * n_axes,
                                vmem_limit_bytes=VMEM_LIMIT, **kw)


RESIDENT = pl.BlockSpec(memory_space=pltpu.VMEM)


def _n_after(after):
    return 0 if after is None else (len(after) if isinstance(after, (tuple, list)) else 1)


def _ordered(body, in_specs, args, after):
    if after is None:
        return body, in_specs, args
    extra = tuple(after) if isinstance(after, (tuple, list)) else (after,)
    return (lambda *refs: body(*refs[len(extra):]),
            [pl.BlockSpec(memory_space=pl.ANY)] * len(extra) + list(in_specs), extra + tuple(args))


def _nt_dot(a, b):
    return lax.dot_general(a, b, (((1,), (1,)), ((), ())), preferred_element_type=f32)


def _tn_dot(a, b):
    return lax.dot_general(a, b, (((0,), (0,)), ((), ())), preferred_element_type=f32)


def _sigmoid(x):
    return 0.5 * jnp.tanh(0.5 * x) + 0.5


def _log1p(y):
    u = 1.0 + y
    d = u - 1.0
    return jnp.where(d == 0.0, y, jnp.log(u) * (y / jnp.where(d == 0.0, 1.0, d)))


def _softplus(x):
    return jnp.maximum(x, 0.0) + _log1p(jnp.exp(-jnp.abs(x)))


def _one_minus_square(a, log_a):
    x = 2.0 * log_a
    series = x * (1.0 + x * (0.5 + x * (1.0 / 6.0)))
    return jnp.where(jnp.abs(x) < 0.03, -series, 1.0 - a * a)


_GELU_C = 0.7978845608028654
_GELU_K = 0.044715


def _gelu_and_grad(y):
    y2 = y * y
    th = jnp.tanh(_GELU_C * (y + _GELU_K * y * y2))
    gel = 0.5 * y * (1.0 + th)
    dgel = 0.5 * (1.0 + th) + 0.5 * y * (1.0 - th * th) * _GELU_C * (1.0 + 3.0 * _GELU_K * y2)
    return gel, dgel


def _rms_stats(h):
    return lax.rsqrt(jnp.mean(h * h, axis=-1, keepdims=True) + EPS)


def _rms_bwd(dn, h, g):
    r = _rms_stats(h)
    nhat = h * r
    dnh = dn * g
    dh = r * (dnh - nhat * jnp.mean(dnh * nhat, axis=-1, keepdims=True))
    dg = jnp.sum(dn * nhat, axis=0, keepdims=True)
    return dh, dg


def _row_ids(shape):
    return lax.broadcasted_iota(jnp.int32, shape, 0)


def _ffn_fwd(h, g, wgu, wd, name, loss_head=None):
    t = h.shape[0]
    nj = 2
    tm = TM
    n_head = 0 if loss_head is None else 2

    def body(*refs):
        h_ref, g_ref, wg_ref, wd_ref = refs[:4]
        outs = refs[4 + n_head:]
        gate_ref, up_ref, n_ref, nb_sc, acc_sc, a_sc = outs[-6:]
        i = pl.program_id(0)
        j = pl.program_id(1)

        @pl.when(j == 0)
        def _():
            hh = h_ref[...]
            nb = (hh * _rms_stats(hh) * g_ref[...]).astype(bf16)
            nb_sc[...] = nb
            n_ref[...] = nb
            acc_sc[...] = jnp.zeros_like(acc_sc)

        nb = nb_sc[...]
        for off, width in FSUB:
            cols = slice(off, off + width)
            gt = jnp.dot(nb, wg_ref[j, :, cols], preferred_element_type=f32)
            up = jnp.dot(nb, wg_ref[2 + j, :, cols], preferred_element_type=f32)
            gate_ref[:, cols] = gt.astype(bf16)
            up_ref[:, cols] = up.astype(bf16)
            a_sc[:, cols] = (gt * _sigmoid(gt) * up).astype(bf16)
        acc_sc[...] += jnp.dot(a_sc[...], wd_ref[j], preferred_element_type=f32)

        if loss_head is None:
            @pl.when(j == nj - 1)
            def _():
                outs[0][...] = h_ref[...] + 0.5 * acc_sc[...]
        else:
            gf_ref, t_ref = refs[4:6]
            dh_ref, loss_ref, dgf_ref = outs[:3]

            @pl.when(jnp.logical_and(i == 0, j == 0))
            def _():
                loss_ref[...] = jnp.zeros_like(loss_ref)
                dgf_ref[...] = jnp.zeros_like(dgf_ref)

            @pl.when(j == nj - 1)
            def _():
                hh = h_ref[...] + 0.5 * acc_sc[...]
                gf = gf_ref[...]
                row = i * tm + _row_ids((tm, 1))
                valid = jnp.logical_and(row >= NMETA, row < loss_head[2])
                err = jnp.where(valid, hh * _rms_stats(hh) * gf - t_ref[...], 0.0)
                loss_ref[...] += 0.5 * jnp.sum(err * err) * (1.0 / D)
                dh, dgf = _rms_bwd(err * (1.0 / D), hh, gf)
                dh_ref[...] = dh
                dgf_ref[...] += dgf

    rowd = pl.BlockSpec((tm, D), lambda i, j: (i, 0))
    vec = pl.BlockSpec((1, D), lambda i, j: (0, 0))
    rowf = pl.BlockSpec((tm, FS), lambda i, j: (i, j))
    in_specs, args = [rowd, vec, RESIDENT, RESIDENT], [h, g, wgu, wd.reshape(nj, FS, D)]
    out_specs, out_shape = [rowd], [jax.ShapeDtypeStruct((t, D), f32)]
    if loss_head is not None:
        in_specs, args = in_specs + [vec, rowd], args + [loss_head[0], loss_head[1]]
        out_specs += [pl.BlockSpec((8, 128), lambda i, j: (0, 0)), vec]
        out_shape += [jax.ShapeDtypeStruct((8, 128), f32), jax.ShapeDtypeStruct((1, D), f32)]
    return pl.pallas_call(
        body, name=name, grid=(t // tm, nj),
        in_specs=in_specs,
        out_specs=out_specs + [rowf, rowf, rowd],
        out_shape=out_shape + [jax.ShapeDtypeStruct((t, F), bf16), jax.ShapeDtypeStruct((t, F), bf16),
                               jax.ShapeDtypeStruct((t, D), bf16)],
        scratch_shapes=[pltpu.VMEM((tm, D), bf16), pltpu.VMEM((tm, D), f32), pltpu.VMEM((tm, FS), bf16)],
        compiler_params=_cp(2),
    )(*args)


def _inproj_fwd(h, g, win, b_in):
    t = h.shape[0]
    tn = NIN // NCHIP
    nj = NIN // tn

    def body(h_ref, g_ref, w_ref, b_ref, proj_ref, n_ref, nb_sc):
        j = pl.program_id(1)

        @pl.when(j == 0)
        def _():
            hh = h_ref[...]
            nb = (hh * _rms_stats(hh) * g_ref[...]).astype(bf16)
            nb_sc[...] = nb
            n_ref[...] = nb

        proj_ref[...] = jnp.dot(nb_sc[...], w_ref[j], preferred_element_type=f32) + b_ref[...]

    return pl.pallas_call(
        body, name="inproj_fwd", grid=(t // TM, nj),
        in_specs=[
            pl.BlockSpec((TM, D), lambda i, j: (i, 0)),
            pl.BlockSpec((1, D), lambda i, j: (0, 0)),
            RESIDENT,
            pl.BlockSpec((1, tn), lambda i, j: (0, j)),
        ],
        out_specs=[
            pl.BlockSpec((TM, tn), lambda i, j: (i, j)),
            pl.BlockSpec((TM, D), lambda i, j: (i, 0)),
        ],
        out_shape=[jax.ShapeDtypeStruct((t, NIN), f32), jax.ShapeDtypeStruct((t, D), bf16)],
        scratch_shapes=[pltpu.VMEM((TM, D), bf16)],
        compiler_params=_cp(2),
    )(h, g, win, b_in)


def _block_gates(xr, wa_ref, ba, wx_ref, bx, lam):
    xrb = xr.astype(bf16)
    pa = jnp.concatenate([jnp.dot(xrb[:, hh * HD:(hh + 1) * HD], wa_ref[hh], preferred_element_type=f32)
                          for hh in range(NHEAD)], axis=1)
    px = jnp.concatenate([jnp.dot(xrb[:, hh * HD:(hh + 1) * HD], wx_ref[hh], preferred_element_type=f32)
                          for hh in range(NHEAD)], axis=1)
    ra = _sigmoid(pa + ba)
    ii = _sigmoid(px + bx)
    sp = _softplus(-lam)
    log_a = -8.0 * ra * sp
    a = jnp.exp(log_a)
    sq = jnp.sqrt(_one_minus_square(a, log_a))
    return ra, ii, a, sq, sp


LT = D // 128
UNR = 13


def _to_lane_tiles(ref, value):
    for lt in range(LT):
        ref[lt] = value[:, lt * 128:(lt + 1) * 128]


def _from_lane_tiles(ref):
    return jnp.concatenate([ref[lt] for lt in range(LT)], axis=1)


def _chain_scan(mult_sc, val_sc, start, reverse):
    ng = TM // 8

    def lanes(lt):
        return slice(lt * 128, (lt + 1) * 128)

    def chain(gi, carry):
        v_prev, p_prev = carry
        rows = pl.ds(ng - 1 - gi if reverse else gi, 8, stride=ng)
        v_new, p_new = [], []
        for lt in range(LT):
            mm = mult_sc.at[lt][rows, :]
            vv = mm * v_prev[:, lanes(lt)] + val_sc.at[lt][rows, :]
            pp = mm * p_prev[:, lanes(lt)]
            val_sc.at[lt][rows, :] = vv
            mult_sc.at[lt][rows, :] = pp
            v_new.append(vv)
            p_new.append(pp)
        return jnp.concatenate(v_new, axis=1), jnp.concatenate(p_new, axis=1)

    v_end, p_end = lax.fori_loop(0, ng, chain, (jnp.zeros((8, D), f32), jnp.ones((8, D), f32)))
    state, entries = start, [None] * 8
    for r in (reversed(range(8)) if reverse else range(8)):
        entries[r] = state
        state = v_end[r:r + 1, :] + p_end[r:r + 1, :] * state
    entry8 = jnp.concatenate(entries, axis=0)

    def add_entry(gi, carry):
        rows = pl.ds(gi, 8, stride=ng)
        for lt in range(LT):
            val_sc.at[lt][rows, :] = val_sc.at[lt][rows, :] + mult_sc.at[lt][rows, :] * entry8[:, lanes(lt)]
        return carry

    lax.fori_loop(0, ng, add_entry, 0)
    return state


def _strided_conv(w_ref, src_sc, out_sc, base, shifts, bias_ref=None):
    ng = TM // 8
    for lt in range(LT):
        lanes = slice(lt * 128, (lt + 1) * 128)
        taps = [jnp.broadcast_to(w_ref[k:k + 1, lanes], (8, 128)) for k in range(len(shifts))]
        init = (jnp.zeros((8, 128), f32) if bias_ref is None
                else jnp.broadcast_to(bias_ref[:, lanes], (8, 128)))

        def step(gb, carry, lt=lt, taps=taps, init=init):
            accs = [init] * UNR
            for k, shift in enumerate(shifts):
                for u in range(UNR):
                    rows = pl.ds(base + gb * UNR + u + shift, 8, stride=ng)
                    accs[u] = accs[u] + taps[k] * src_sc.at[lt][rows, :]
            for u in range(UNR):
                out_sc.at[lt][pl.ds(gb * UNR + u, 8, stride=ng), :] = accs[u]
            return carry

        lax.fori_loop(0, ng // UNR, step, 0)


def _strided_corr(a_sc, src_sc, base, shifts):
    ng = TM // 8
    per_tile = []
    for lt in range(LT):
        def step(gb, accs, lt=lt):
            accs = list(accs)
            for u in range(4):
                g = gb * 4 + u
                a_g = a_sc.at[lt][pl.ds(g, 8, stride=ng), :]
                for k, shift in enumerate(shifts):
                    accs[k] = accs[k] + a_g * src_sc.at[lt][pl.ds(base + g + shift, 8, stride=ng), :]
            return tuple(accs)

        per_tile.append(lax.fori_loop(0, ng // 4, step, tuple(jnp.zeros((8, 128), f32) for _ in shifts)))
    return [jnp.concatenate([per_tile[lt][k] for lt in range(LT)], axis=1) for k in range(len(shifts))]


def _rnn_fwd(proj, cw, cb, wa, ba, wx, bx, lam):
    t = proj.shape[0]

    def body(x_ref, y_ref, cw_ref, cb_ref, wa_ref, ba_ref, wx_ref, bx_ref, lam_ref,
             xr_ref, hr_ref, z_ref, gates_ref, xext_sc, carry_sc, a_sc, h_sc):
        i = pl.program_id(0)

        @pl.when(i == 0)
        def _():
            xext_sc[0:8, :] = jnp.zeros((8, D), f32)
            carry_sc[...] = jnp.zeros_like(carry_sc)

        x = x_ref[...]
        xext_sc[8:8 + TM, :] = x
        xe = xext_sc[...]
        xr = cb_ref[...] + cw_ref[KC4 - 1:KC4, :] * x
        for k in range(KC4 - 1):
            xr = xr + cw_ref[k:k + 1, :] * pltpu.roll(xe, KC4 - 1 - k, 0)[8:8 + TM]
        xext_sc[0:8, :] = x[TM - 8:TM]

        ra, ii, a, sq, _ = _block_gates(xr, wa_ref, ba_ref[...], wx_ref, bx_ref[...], lam_ref[...])
        for slot, val in enumerate((ra, ii, a, sq)):
            gates_ref[slot] = val
        _to_lane_tiles(a_sc, a)
        _to_lane_tiles(h_sc, sq * ii * xr)
        carry_sc[...] = _chain_scan(a_sc, h_sc, carry_sc[...], reverse=False)
        hr = _from_lane_tiles(h_sc)
        gel, _ = _gelu_and_grad(y_ref[...])
        xr_ref[...] = xr
        hr_ref[...] = hr
        z_ref[...] = (hr * gel).astype(bf16)

    vec = pl.BlockSpec((1, D), lambda i: (0, 0))
    return pl.pallas_call(
        body, name="rnn_fwd", grid=(t // TM,),
        in_specs=[
            pl.BlockSpec((TM, D), lambda i: (i, 0)),
            pl.BlockSpec((TM, D), lambda i: (i, 1)),
            pl.BlockSpec((KC4, D), lambda i: (0, 0)),
            vec,
            pl.BlockSpec((NHEAD, HD, HD), lambda i: (0, 0, 0)),
            vec,
            pl.BlockSpec((NHEAD, HD, HD), lambda i: (0, 0, 0)),
            vec, vec,
        ],
        out_specs=[pl.BlockSpec((TM, D), lambda i: (i, 0))] * 3 + [pl.BlockSpec((4, TM, D), lambda i: (0, i, 0))],
        out_shape=[jax.ShapeDtypeStruct((t, D), f32), jax.ShapeDtypeStruct((t, D), f32),
                   jax.ShapeDtypeStruct((t, D), bf16), jax.ShapeDtypeStruct((4, t, D), f32)],
        scratch_shapes=[pltpu.VMEM((TM + 8, D), f32), pltpu.VMEM((1, D), f32),
                        pltpu.VMEM((LT, TM, 128), f32), pltpu.VMEM((LT, TM, 128), f32)],
        compiler_params=_cp(1),
    )(proj, proj, cw, cb, wa, ba, wx, bx, lam)


def _ln_stats(vc):
    mu = jnp.mean(vc, axis=-1, keepdims=True)
    xc = vc - mu
    rstd = lax.rsqrt(jnp.mean(xc * xc, axis=-1, keepdims=True) + EPS)
    return xc * rstd, rstd


def _conv_fwd(proj, w31, b31, ln_g, ln_b, after=None):
    t = proj.shape[0]

    def body(gv_ref, gg_ref, w_ref, b_ref, lg_ref, lb_ref, vc_ref, s_ref, vext_sc, out_sc):
        i = pl.program_id(0)

        @pl.when(i == 0)
        def _():
            vext_sc[:, 0:HALO, :] = jnp.zeros((LT, HALO, 128), f32)

        v = gv_ref[...] * _sigmoid(gg_ref[...])
        for lt in range(LT):
            vext_sc[lt, HALO:HALO + TM, :] = v[:, lt * 128:(lt + 1) * 128]

        _strided_conv(w_ref, vext_sc, out_sc, HALO, [k - (KC31 - 1) for k in range(KC31)], b_ref)
        for lt in range(LT):
            vext_sc[lt, 0:HALO, :] = v[TM - HALO:TM, lt * 128:(lt + 1) * 128]
        acc = _from_lane_tiles(out_sc)
        xhat, _ = _ln_stats(acc)
        ln = xhat * lg_ref[...] + lb_ref[...]
        vc_ref[...] = acc
        s_ref[...] = (ln * _sigmoid(ln)).astype(bf16)

    vec = pl.BlockSpec((1, D), lambda i: (0, 0))
    body, in_specs, args = _ordered(
        body,
        [pl.BlockSpec((TM, D), lambda i: (i, 2)),
         pl.BlockSpec((TM, D), lambda i: (i, 3)),
         pl.BlockSpec((KC31, D), lambda i: (0, 0)),
         vec, vec, vec],
        (proj, proj, w31, b31, ln_g, ln_b), after)
    return pl.pallas_call(
        body, name="conv_fwd", grid=(t // TM,),
        in_specs=in_specs,
        out_specs=[pl.BlockSpec((TM, D), lambda i: (i, 0))] * 2,
        out_shape=[jax.ShapeDtypeStruct((t, D), f32), jax.ShapeDtypeStruct((t, D), bf16)],
        scratch_shapes=[pltpu.VMEM((LT, TM + HALO, 128), f32), pltpu.VMEM((LT, TM, 128), f32)],
        compiler_params=_cp(1),
    )(*args)


def _merge_fwd(h, z, s, proj, wrp, wcp, bcp, wout):
    t = h.shape[0]

    def body(h_ref, z_ref, s_ref, ga_ref, gb_ref, wrp_ref, wcp_ref, bcp_ref, wout_ref, ho_ref):
        ya = jnp.dot(z_ref[...], wrp_ref[...], preferred_element_type=f32)
        yb = jnp.dot(s_ref[...], wcp_ref[...], preferred_element_type=f32) + bcp_ref[...]
        merged = _sigmoid(ga_ref[...]) * ya + _sigmoid(gb_ref[...]) * yb
        ho_ref[...] = h_ref[...] + jnp.dot(merged.astype(bf16), wout_ref[...], preferred_element_type=f32)

    row = pl.BlockSpec((TM, D), lambda i: (i, 0))
    wsq = pl.BlockSpec((D, D), lambda i: (0, 0))
    return pl.pallas_call(
        body, name="merge_fwd", grid=(t // TM,),
        in_specs=[row, row, row,
                  pl.BlockSpec((TM, D), lambda i: (i, 4)),
                  pl.BlockSpec((TM, D), lambda i: (i, 5)),
                  wsq, wsq, pl.BlockSpec((1, D), lambda i: (0, 0)), wsq],
        out_specs=row,
        out_shape=jax.ShapeDtypeStruct((t, D), f32),
        compiler_params=_cp(1),
    )(h, z, s, proj, proj, wrp, wcp, bcp, wout)


def _ffn_bwd(dh, h, g, gate, up, wgu, wd, name, after=None):
    t = h.shape[0]
    nj = 2

    def body(dh_ref, h_ref, g_ref, gate_ref, up_ref, wg_ref, wd_ref,
             dhi_ref, dgate_ref, dup_ref, a_ref, df_ref, dg_ref, dfb_sc, dn_sc):
        i = pl.program_id(0)
        j = pl.program_id(1)

        @pl.when(jnp.logical_and(i == 0, j == 0))
        def _():
            dg_ref[...] = jnp.zeros_like(dg_ref)

        @pl.when(j == 0)
        def _():
            dfb = (0.5 * dh_ref[...]).astype(bf16)
            dfb_sc[...] = dfb
            df_ref[...] = dfb
            dn_sc[...] = jnp.zeros_like(dn_sc)

        dfb = dfb_sc[...]
        for off, width in FSUB:
            cols = slice(off, off + width)
            da = _nt_dot(dfb, wd_ref[j, cols, :])
            gt = gate_ref[:, cols].astype(f32)
            uu = up_ref[:, cols].astype(f32)
            sg = _sigmoid(gt)
            silu = gt * sg
            a_ref[:, cols] = (silu * uu).astype(bf16)
            dgate_ref[:, cols] = (da * uu * (sg * (1.0 + gt * (1.0 - sg)))).astype(bf16)
            dup_ref[:, cols] = (da * silu).astype(bf16)
        dn_sc[...] += _nt_dot(dgate_ref[...], wg_ref[j]) + _nt_dot(dup_ref[...], wg_ref[2 + j])

        @pl.when(j == nj - 1)
        def _():
            dhin, dg = _rms_bwd(dn_sc[...], h_ref[...], g_ref[...])
            dhi_ref[...] = dh_ref[...] + dhin
            dg_ref[...] += dg

    rowd = pl.BlockSpec((TM, D), lambda i, j: (i, 0))
    rowf = pl.BlockSpec((TM, FS), lambda i, j: (i, j))
    vec = pl.BlockSpec((1, D), lambda i, j: (0, 0))
    body, in_specs, args = _ordered(
        body,
        [rowd, rowd, vec, rowf, rowf,
         RESIDENT, RESIDENT],
        (dh, h, g, gate, up, wgu, wd.reshape(nj, FS, D)), after)
    return pl.pallas_call(
        body, name=name, grid=(t // TM, nj),
        in_specs=in_specs,
        out_specs=[rowd, rowf, rowf, rowf, rowd, vec],
        out_shape=[jax.ShapeDtypeStruct((t, D), f32), jax.ShapeDtypeStruct((t, F), bf16),
                   jax.ShapeDtypeStruct((t, F), bf16), jax.ShapeDtypeStruct((t, F), bf16),
                   jax.ShapeDtypeStruct((t, D), bf16), jax.ShapeDtypeStruct((1, D), f32)],
        scratch_shapes=[pltpu.VMEM((TM, D), bf16), pltpu.VMEM((TM, D), f32)],
        compiler_params=_cp(2),
    )(*args)


def _big_tile(t):
    return max(k * TM for k in range(1, 6) if t % (k * TM) == 0)


ANY_SPEC = pl.BlockSpec(memory_space=pl.ANY)


def _tn_matmul(a, b, tk, tn, out_shape, out_block, out_map, name, base=None, after=None):
    t, kk = a.shape
    _, nn = b.shape
    tmm = _big_tile(t)
    nm = t // tmm

    def body(a_ref, b_ref, o_ref, acc_sc):
        m = pl.program_id(2)

        @pl.when(m == 0)
        def _():
            acc_sc[...] = jnp.zeros_like(acc_sc)

        acc_sc[...] += _tn_dot(a_ref[...], b_ref[...])

        @pl.when(m == nm - 1)
        def _():
            o_ref[...] = acc_sc[...].astype(o_ref.dtype)

    in_specs = [pl.BlockSpec((tmm, tk), lambda k, n, m: (m, k)),
                pl.BlockSpec((tmm, tn), lambda k, n, m: (m, n))]
    args, aliases = (a, b), {}
    if base is not None:
        body = (lambda inner: lambda a_ref, b_ref, base_ref, o_ref, acc_sc: inner(a_ref, b_ref, o_ref, acc_sc))(body)
        in_specs, args, aliases = in_specs + [ANY_SPEC], (a, b, base), {2: 0}
    if after is not None:
        body, in_specs, args = _ordered(body, in_specs, args, after)
        aliases = {k + _n_after(after): v for k, v in aliases.items()}
    return pl.pallas_call(
        body, name=name, grid=(kk // tk, nn // tn, nm),
        in_specs=in_specs,
        out_specs=pl.BlockSpec(out_block, out_map),
        out_shape=jax.ShapeDtypeStruct(out_shape, bf16),
        scratch_shapes=[pltpu.VMEM((tk, tn), f32)],
        input_output_aliases=aliases,
        compiler_params=_cp(3),
    )(*args)


def _merge_bwd(dh, z, s, proj, wrp, wcp, bcp, wout, after=None):
    t = dh.shape[0]

    def body(dh_ref, z_ref, s_ref, ga_ref, gb_ref, wrp_ref, wcp_ref, bcp_ref, wout_ref,
             dz_ref, ds_ref, dgab_ref, dhb_ref, mg_ref, dya_ref, dyb_ref, dbcp_ref):
        i = pl.program_id(0)

        @pl.when(i == 0)
        def _():
            dbcp_ref[...] = jnp.zeros_like(dbcp_ref)

        dhb = dh_ref[...].astype(bf16)
        dhb_ref[...] = dhb
        dmg = _nt_dot(dhb, wout_ref[...])
        ya = jnp.dot(z_ref[...], wrp_ref[...], preferred_element_type=f32)
        yb = jnp.dot(s_ref[...], wcp_ref[...], preferred_element_type=f32) + bcp_ref[...]
        sa = _sigmoid(ga_ref[...])
        sb = _sigmoid(gb_ref[...])
        mg_ref[...] = (sa * ya + sb * yb).astype(bf16)
        dgab_ref[:, 0:D] = (dmg * ya * sa * (1.0 - sa)).astype(bf16)
        dgab_ref[:, D:2 * D] = (dmg * yb * sb * (1.0 - sb)).astype(bf16)
        dya = dmg * sa
        dyb = dmg * sb
        dbcp_ref[...] += jnp.sum(dyb, axis=0, keepdims=True)
        dyab = dya.astype(bf16)
        dybb = dyb.astype(bf16)
        dya_ref[...] = dyab
        dyb_ref[...] = dybb
        dz_ref[...] = _nt_dot(dyab, wrp_ref[...])
        ds_ref[...] = _nt_dot(dybb, wcp_ref[...])

    row = pl.BlockSpec((TM, D), lambda i: (i, 0))
    wsq = pl.BlockSpec((D, D), lambda i: (0, 0))
    vec = pl.BlockSpec((1, D), lambda i: (0, 0))
    rowb = jax.ShapeDtypeStruct((t, D), bf16)
    body, in_specs, args = _ordered(
        body,
        [row, row, row,
         pl.BlockSpec((TM, D), lambda i: (i, 4)),
         pl.BlockSpec((TM, D), lambda i: (i, 5)),
         wsq, wsq, vec, wsq],
        (dh, z, s, proj, proj, wrp, wcp, bcp, wout), after)
    return pl.pallas_call(
        body, name="merge_bwd", grid=(t // TM,),
        in_specs=in_specs,
        out_specs=[row, row,
                   pl.BlockSpec((TM, 2 * D), lambda i: (i, 2)),
                   row, row, row, row, vec],
        out_shape=[jax.ShapeDtypeStruct((t, D), f32), jax.ShapeDtypeStruct((t, D), f32),
                   jax.ShapeDtypeStruct((t, NIN), bf16),
                   rowb, rowb, rowb, rowb, jax.ShapeDtypeStruct((1, D), f32)],
        compiler_params=_cp(1),
    )(*args)


def _conv_bwd(ds, vc, proj, dproj, w31, ln_g, ln_b, after=None):
    t = ds.shape[0]
    nt = t // TM
    hb = TM // HALO

    def body(ds_ref, vc_ref, gv_ref, gg_ref, gvp_ref, ggp_ref, dpin_ref, w_ref, lg_ref, lb_ref,
             dgvg_ref, dw_ref, db_ref, dlg_ref, dlb_ref, dext_sc, vext_sc, out_sc, dwacc_sc, small_sc):
        del dpin_ref
        i = pl.program_id(0)
        tile = nt - 1 - i

        @pl.when(i == 0)
        def _():
            dext_sc[:, TM:TM + HALO, :] = jnp.zeros((LT, HALO, 128), f32)
            dwacc_sc[...] = jnp.zeros_like(dwacc_sc)
            small_sc[...] = jnp.zeros_like(small_sc)

        lg = lg_ref[...]
        lb = lb_ref[...]

        xhat, rstd = _ln_stats(vc_ref[...])
        ln = xhat * lg + lb
        sg = _sigmoid(ln)
        dln = ds_ref[...] * (sg * (1.0 + ln * (1.0 - sg)))
        dxh = dln * lg
        dvc = rstd * (dxh - jnp.mean(dxh, axis=-1, keepdims=True)
                      - xhat * jnp.mean(dxh * xhat, axis=-1, keepdims=True))
        small_sc[0] += jnp.sum((dln * xhat).reshape(TM // 8, 8, D), axis=0)
        small_sc[1] += jnp.sum(dln.reshape(TM // 8, 8, D), axis=0)
        small_sc[2] += jnp.sum(dvc.reshape(TM // 8, 8, D), axis=0)
        sgg = _sigmoid(gg_ref[...])
        v = gv_ref[...] * sgg
        vprev = jnp.where(tile > 0, gvp_ref[...] * _sigmoid(ggp_ref[...]), 0.0)
        for lt in range(LT):
            lanes = slice(lt * 128, (lt + 1) * 128)
            dext_sc[lt, 0:TM, :] = dvc[:, lanes]
            vext_sc[lt, HALO:HALO + TM, :] = v[:, lanes]
            vext_sc[lt, 0:HALO, :] = vprev[:, lanes]

        _strided_conv(w_ref, dext_sc, out_sc, 0, [KC31 - 1 - k for k in range(KC31)])
        dv = _from_lane_tiles(out_sc)
        dgvg_ref[:, 0:D] = (dv * sgg).astype(bf16)
        dgvg_ref[:, D:2 * D] = (dv * gv_ref[...] * sgg * (1.0 - sgg)).astype(bf16)

        for k, part in enumerate(_strided_corr(dext_sc, vext_sc, HALO, [k - (KC31 - 1) for k in range(KC31)])):
            dwacc_sc[k] += part
        for lt in range(LT):
            dext_sc[lt, TM:TM + HALO, :] = dext_sc[lt, 0:HALO, :]

        @pl.when(i == nt - 1)
        def _():
            for k in range(KC31):
                dw_ref[k:k + 1, :] = jnp.sum(dwacc_sc[k], axis=0, keepdims=True)
            dlg_ref[...] = jnp.sum(small_sc[0], axis=0, keepdims=True)
            dlb_ref[...] = jnp.sum(small_sc[1], axis=0, keepdims=True)
            db_ref[...] = jnp.sum(small_sc[2], axis=0, keepdims=True)

    rev = lambda i: (nt - 1 - i, 0)
    vec = pl.BlockSpec((1, D), lambda i: (0, 0))
    halo_row = lambda i: jnp.maximum((nt - 1 - i) * hb - 1, 0)
    body, in_specs, args = _ordered(
        body,
        [pl.BlockSpec((TM, D), rev),
         pl.BlockSpec((TM, D), rev),
         pl.BlockSpec((TM, D), lambda i: (nt - 1 - i, 2)),
         pl.BlockSpec((TM, D), lambda i: (nt - 1 - i, 3)),
         pl.BlockSpec((HALO, D), lambda i: (halo_row(i), 2)),
         pl.BlockSpec((HALO, D), lambda i: (halo_row(i), 3)),
         pl.BlockSpec(memory_space=pl.ANY),
         pl.BlockSpec((KC31, D), lambda i: (0, 0)),
         vec, vec],
        (ds, vc, proj, proj, proj, proj, dproj, w31, ln_g, ln_b), after)
    return pl.pallas_call(
        body, name="conv_bwd", grid=(nt,),
        in_specs=in_specs,
        out_specs=[
            pl.BlockSpec((TM, 2 * D), lambda i: (nt - 1 - i, 1)),
            pl.BlockSpec((KC31, D), lambda i: (0, 0)),
            vec, vec, vec,
        ],
        out_shape=[jax.ShapeDtypeStruct((t, NIN), bf16),
                   jax.ShapeDtypeStruct((KC31, D), f32),
                   jax.ShapeDtypeStruct((1, D), f32), jax.ShapeDtypeStruct((1, D), f32),
                   jax.ShapeDtypeStruct((1, D), f32)],
        scratch_shapes=[pltpu.VMEM((LT, TM + HALO, 128), f32), pltpu.VMEM((LT, TM + HALO, 128), f32),
                        pltpu.VMEM((LT, TM, 128), f32), pltpu.VMEM((KC31, 8, D), f32),
                        pltpu.VMEM((3, 8, D), f32)],
        input_output_aliases={6 + _n_after(after): 0},
        compiler_params=_cp(1),
    )(*args)


def _rnn_bwd(dz, xr, hr, gates, proj, dproj, cw, wa, wx, lam):
    t = dz.shape[0]
    nt = t // TM
    ng = TM // 8
    hq = HD // NCHIP

    def body(dz_ref, xr_ref, hr_ref, hrp_ref, x_ref, xp_ref, y_ref, dpin_ref,
             cw_ref, wa_ref, gates_ref, wx_ref, lam_ref,
             dxy_ref, dwa_ref, dwx_ref, dcw_ref, dcb_ref, dba_ref, dbx_ref, dlam_ref,
             anext_sc, gcarry_sc, dext_sc, xext_sc, m_sc, g_sc, dwa_sc, dwx_sc, dsp_sc):
        del dpin_ref
        i = pl.program_id(0)
        tile = nt - 1 - i

        @pl.when(i == 0)
        def _():
            anext_sc[...] = jnp.zeros_like(anext_sc)
            gcarry_sc[...] = jnp.zeros_like(gcarry_sc)
            dext_sc[TM:TM + 8, :] = jnp.zeros((8, D), f32)
            dwa_sc[...] = jnp.zeros_like(dwa_sc)
            dwx_sc[...] = jnp.zeros_like(dwx_sc)
            dsp_sc[...] = jnp.zeros_like(dsp_sc)
            dcw_ref[...] = jnp.zeros_like(dcw_ref)
            dcb_ref[...] = jnp.zeros_like(dcb_ref)
            dba_ref[...] = jnp.zeros_like(dba_ref)
            dbx_ref[...] = jnp.zeros_like(dbx_ref)

        xr = xr_ref[...]
        hr = hr_ref[...]
        dz = dz_ref[...]
        gel, dgel = _gelu_and_grad(y_ref[...])
        dxy_ref[:, D:2 * D] = (dz * hr * dgel).astype(bf16)
        ra, ii, a, sq = gates_ref[0], gates_ref[1], gates_ref[2], gates_ref[3]
        sp = _softplus(-lam_ref[...])

        row = _row_ids((TM, D))
        _to_lane_tiles(m_sc, jnp.where(row == TM - 1, anext_sc[...], pltpu.roll(a, TM - 1, 0)))
        anext_sc[...] = a[0:1, :]
        _to_lane_tiles(g_sc, dz * gel)
        gcarry_sc[...] = _chain_scan(m_sc, g_sc, gcarry_sc[...], reverse=True)
        gg = _from_lane_tiles(g_sc)

        hlast = jnp.where(tile > 0, hrp_ref[7:8, :], 0.0)
        hprev = jnp.where(row == 0, hlast, pltpu.roll(hr, 1, 0))
        d_a = gg * hprev
        dsq = gg * ii * xr
        dii = gg * sq * xr
        dxr = gg * sq * ii
        dlog = d_a * a - dsq * (a * a / sq)
        dsp_sc[...] += jnp.sum(dlog * (-8.0 * ra), axis=0, keepdims=True)
        dpa = dlog * (-8.0 * sp) * ra * (1.0 - ra)
        dpx = dii * ii * (1.0 - ii)
        dba_ref[...] += jnp.sum(dpa, axis=0, keepdims=True)
        dbx_ref[...] += jnp.sum(dpx, axis=0, keepdims=True)
        dpab = dpa.astype(bf16)
        dpxb = dpx.astype(bf16)
        xrb = xr.astype(bf16)
        back = []
        for hh in range(NHEAD):
            cols = slice(hh * HD, (hh + 1) * HD)
            back.append(_nt_dot(dpab[:, cols], wa_ref[hh]) + _nt_dot(dpxb[:, cols], wx_ref[hh]))
            dwa_sc[hh] += _tn_dot(xrb[:, cols], dpab[:, cols])
            dwx_sc[hh] += _tn_dot(xrb[:, cols], dpxb[:, cols])
        dxr = dxr + jnp.concatenate(back, axis=1)

        dext_sc[0:TM, :] = dxr
        de = dext_sc[...]
        dx = cw_ref[KC4 - 1:KC4, :] * dxr
        for k in range(KC4 - 1):
            dx = dx + cw_ref[k:k + 1, :] * pltpu.roll(de, TM + 8 - (KC4 - 1 - k), 0)[0:TM]
        dext_sc[TM:TM + 8, :] = dxr[0:8]
        dxy_ref[:, 0:D] = dx.astype(bf16)

        x = x_ref[...]
        xext_sc[0:8, :] = jnp.where(tile > 0, xp_ref[...], 0.0)
        xext_sc[8:8 + TM, :] = x
        xe = xext_sc[...]
        dcw_ref[KC4 - 1:KC4, :] += jnp.sum(dxr * x, axis=0, keepdims=True)
        for k in range(KC4 - 1):
            xs = pltpu.roll(xe, KC4 - 1 - k, 0)[8:8 + TM]
            dcw_ref[k:k + 1, :] += jnp.sum(dxr * xs, axis=0, keepdims=True)
        dcb_ref[...] += jnp.sum(dxr, axis=0, keepdims=True)

        @pl.when(i == nt - 1)
        def _():
            for hh in range(NHEAD):
                for qc in range(NCHIP):
                    dwa_ref[qc, hh] = dwa_sc[hh, qc * hq:(qc + 1) * hq, :].astype(bf16)
                    dwx_ref[qc, hh] = dwx_sc[hh, qc * hq:(qc + 1) * hq, :].astype(bf16)
            dlam_ref[...] = -dsp_sc[...] * _sigmoid(-lam_ref[...])

    rev = lambda i: (nt - 1 - i, 0)
    vec = pl.BlockSpec((1, D), lambda i: (0, 0))
    prev8 = lambda i: jnp.maximum((nt - 1 - i) * ng - 1, 0)
    wblk = pl.BlockSpec((NHEAD, HD, HD), lambda i: (0, 0, 0))
    gblk = pl.BlockSpec((NCHIP, NHEAD, hq, HD), lambda i: (0, 0, 0, 0))
    return pl.pallas_call(
        body, name="rnn_bwd", grid=(nt,),
        in_specs=[
            pl.BlockSpec((TM, D), rev),
            pl.BlockSpec((TM, D), rev),
            pl.BlockSpec((TM, D), rev),
            pl.BlockSpec((8, D), lambda i: (prev8(i), 0)),
            pl.BlockSpec((TM, D), lambda i: (nt - 1 - i, 0)),
            pl.BlockSpec((8, D), lambda i: (prev8(i), 0)),
            pl.BlockSpec((TM, D), lambda i: (nt - 1 - i, 1)),
            pl.BlockSpec(memory_space=pl.ANY),
            pl.BlockSpec((KC4, D), lambda i: (0, 0)),
            wblk, pl.BlockSpec((4, TM, D), lambda i: (0, nt - 1 - i, 0)), wblk, vec,
        ],
        out_specs=[
            pl.BlockSpec((TM, 2 * D), lambda i: (nt - 1 - i, 0)),
            gblk, gblk,
            pl.BlockSpec((KC4, D), lambda i: (0, 0)),
            vec, vec, vec, vec,
        ],
        out_shape=[jax.ShapeDtypeStruct((t, NIN), bf16),
                   jax.ShapeDtypeStruct((NCHIP, NHEAD, hq, HD), bf16),
                   jax.ShapeDtypeStruct((NCHIP, NHEAD, hq, HD), bf16),
                   jax.ShapeDtypeStruct((KC4, D), f32),
                   jax.ShapeDtypeStruct((1, D), f32), jax.ShapeDtypeStruct((1, D), f32),
                   jax.ShapeDtypeStruct((1, D), f32), jax.ShapeDtypeStruct((1, D), f32)],
        scratch_shapes=[pltpu.VMEM((1, D), f32), pltpu.VMEM((1, D), f32),
                        pltpu.VMEM((TM + 8, D), f32), pltpu.VMEM((TM + 8, D), f32),
                        pltpu.VMEM((LT, TM, 128), f32), pltpu.VMEM((LT, TM, 128), f32),
                        pltpu.VMEM((NHEAD, HD, HD), f32), pltpu.VMEM((NHEAD, HD, HD), f32),
                        pltpu.VMEM((1, D), f32)],
        input_output_aliases={7: 0},
        compiler_params=_cp(1),
    )(dz, xr, hr, hr, proj, proj, proj, dproj, cw, wa, gates, wx, lam)


def _inproj_bwd(dproj, dh, h, g, win, after=None):
    t = h.shape[0]
    tn = NIN // NCHIP
    nj = NIN // tn

    def body(dp_ref, dh_ref, h_ref, g_ref, w_ref, dhi_ref, dg_ref, db_ref, dn_sc):
        i = pl.program_id(0)
        j = pl.program_id(1)

        @pl.when(jnp.logical_and(i == 0, j == 0))
        def _():
            dg_ref[...] = jnp.zeros_like(dg_ref)
            db_ref[...] = jnp.zeros_like(db_ref)

        @pl.when(j == 0)
        def _():
            dn_sc[...] = jnp.zeros_like(dn_sc)

        dp = dp_ref[...]
        dn_sc[...] += _nt_dot(dp, w_ref[j])
        db_ref[j] += jnp.sum(dp.astype(f32), axis=0, keepdims=True)

        @pl.when(j == nj - 1)
        def _():
            dhin, dg = _rms_bwd(dn_sc[...], h_ref[...], g_ref[...])
            dhi_ref[...] = dh_ref[...] + dhin
            dg_ref[...] += dg

    rowd = pl.BlockSpec((TM, D), lambda i, j: (i, 0))
    vec = pl.BlockSpec((1, D), lambda i, j: (0, 0))
    body, in_specs, args = _ordered(
        body,
        [pl.BlockSpec((TM, tn), lambda i, j: (i, j)), rowd, rowd, vec,
         RESIDENT],
        (dproj, dh, h, g, win), after)
    return pl.pallas_call(
        body, name="inproj_bwd", grid=(t // TM, nj),
        in_specs=in_specs,
        out_specs=[rowd, vec, pl.BlockSpec((nj, 1, tn), lambda i, j: (0, 0, 0))],
        out_shape=[jax.ShapeDtypeStruct((t, D), f32), jax.ShapeDtypeStruct((1, D), f32),
                   jax.ShapeDtypeStruct((nj, 1, tn), f32)],
        scratch_shapes=[pltpu.VMEM((TM, D), f32)],
        compiler_params=_cp(2),
    )(*args)


def _ffn_gu_grad(n, dgate, dup, tag, after=None):
    half = _tn_matmul(n, dgate, D, FS, (NCHIP, D, FS), (None, D, FS), lambda k, nn, m: (nn, 0, 0),
                      tag + "_dwg", after=after)
    return _tn_matmul(n, dup, D, FS, (NCHIP, D, FS), (None, D, FS), lambda k, nn, m: (2 + nn, 0, 0),
                      tag + "_dwu", base=half)


def _ffn_down_grad(a, df, tag, after=None):
    return _tn_matmul(a, df, FS, D, (F, D), (FS, D), lambda k, nn, m: (k, 0), tag + "_dwd", after=after)


def _square_grad(a, b, name):
    return _tn_matmul(a, b, D, D, (D, D), (D, D), lambda k, nn, m: (0, 0), name)


ANY = pl.BlockSpec(memory_space=pl.ANY)


def _place():
    x, y, c = lax.axis_index("x"), lax.axis_index("y"), lax.axis_index("c")
    chips = [(1 - x, y), (x, 1 - y), (1 - x, 1 - y)]
    return x, y, c, chips


def _chip_id(chip):
    return 2 * chip[0] + chip[1]


def _cast_into_slot(w2d, qc, dtype, name, after=None):
    r, cc = w2d.shape
    hr = r // 2

    def body(qc_ref, *refs):
        del qc_ref
        w_ref, o_ref = refs[-2:]
        o_ref[...] = w_ref[...].astype(dtype)

    in_specs, args = [pl.BlockSpec((hr, cc), lambda h, qc_ref: (h, 0))], (w2d,)
    if after is not None:
        in_specs, args = [ANY_SPEC] + in_specs, (after,) + args
    return pl.pallas_call(
        body, name=name,
        grid_spec=pltpu.PrefetchScalarGridSpec(
            num_scalar_prefetch=1, grid=(2,),
            in_specs=in_specs,
            out_specs=pl.BlockSpec((None, None, hr, cc), lambda h, qc_ref: (qc_ref[0], h, 0, 0))),
        out_shape=jax.ShapeDtypeStruct((NCHIP, 2, hr, cc), dtype),
        compiler_params=_cp(1),
    )(qc, *args)


def _place_pack(pack, qc):
    def body(qc_ref, p_ref, o_ref):
        del qc_ref
        o_ref[...] = p_ref[...]

    return pl.pallas_call(
        body, name="place_pack",
        grid_spec=pltpu.PrefetchScalarGridSpec(
            num_scalar_prefetch=1, grid=(1,),
            in_specs=[pl.BlockSpec(pack.shape, lambda i, qc_ref: (0, 0))],
            out_specs=pl.BlockSpec((None,) + pack.shape, lambda i, qc_ref: (2 * qc_ref[0] + qc_ref[1], 0, 0))),
        out_shape=jax.ShapeDtypeStruct((8,) + pack.shape, pack.dtype),
        compiler_params=_cp(1),
    )(qc, pack)


def _pair_add(parts, gots, qc, name):
    n = len(parts)

    def body(qc_ref, *refs):
        s = pl.program_id(0)
        for a in range(n):
            val = (refs[a][...].astype(f32) + refs[n + a][...].astype(f32)).astype(bf16)
            refs[2 * n + a][...] = val

            @pl.when(s == qc_ref[0])
            def _(val=val, land_ref=refs[3 * n + a]):
                land_ref[...] = val

    shapes = [p.shape[2:] for p in parts]
    mine = [pl.BlockSpec((None, None) + sh, lambda s, qc_ref: (s, qc_ref[1], 0, 0)) for sh in shapes]
    block = [pl.BlockSpec((None,) + sh, lambda s, qc_ref: (s, 0, 0)) for sh in shapes]
    own = [pl.BlockSpec((None,) + sh, lambda s, qc_ref: (qc_ref[0], 0, 0)) for sh in shapes]
    outs = pl.pallas_call(
        body, name=name,
        grid_spec=pltpu.PrefetchScalarGridSpec(
            num_scalar_prefetch=1, grid=(NCHIP,), in_specs=mine + block, out_specs=block + own),
        out_shape=[jax.ShapeDtypeStruct((NCHIP,) + sh, bf16) for sh in shapes] * 2,
        compiler_params=_cp(1),
    )(qc, *parts, *gots)
    return list(outs[:n]), list(outs[n:])


def _sum_chips(gots, name):
    n = len(gots)

    def body(*refs):
        for a in range(n):
            acc = refs[a][0].astype(f32)
            for s in range(1, NCHIP):
                acc = acc + refs[a][s].astype(f32)
            refs[n + a][...] = acc

    return list(pl.pallas_call(
        body, name=name, grid=(1,),
        in_specs=[pl.BlockSpec(g.shape, lambda i: (0, 0, 0)) for g in gots],
        out_specs=[pl.BlockSpec(g.shape[1:], lambda i: (0, 0)) for g in gots],
        out_shape=[jax.ShapeDtypeStruct(g.shape[1:], f32) for g in gots],
        compiler_params=_cp(1),
    )(*gots))


def _pair_share(halves, name, after=None):
    n = len(halves)
    extra = () if after is None else (after,)

    def body(*refs):
        refs = refs[len(extra):]
        ins, outs = refs[:n], refs[n:2 * n]
        send_sems, recv_sems = refs[2 * n:]
        x, y, c, _ = _place()
        copies = []
        for a in range(n):
            cp = pltpu.make_async_remote_copy(
                src_ref=ins[a], dst_ref=outs[a], send_sem=send_sems.at[a], recv_sem=recv_sems.at[a],
                device_id=(x, y, 1 - c), device_id_type=MESH)
            cp.start()
            copies.append(cp)
        for cp in copies:
            cp.wait()

    return pl.pallas_call(
        body, name=name,
        in_specs=[ANY] * (len(extra) + n), out_specs=[ANY] * n,
        out_shape=[jax.ShapeDtypeStruct(s.shape, s.dtype) for s in halves],
        scratch_shapes=[pltpu.SemaphoreType.DMA((n,)), pltpu.SemaphoreType.DMA((n,))],
    )(*extra, *halves)


def _all_copy(buf_ref, send_ref, recv_ref, k, x, y, c, landing):
    px, py, pc = (1 - x if k & 4 else x, 1 - y if k & 2 else y, 1 - c if k & 1 else c)
    me = 4 * x + 2 * y + c
    there = 4 * px + 2 * py + pc
    return pltpu.make_async_remote_copy(
        src_ref=buf_ref.at[me], dst_ref=buf_ref.at[there if landing else me],
        send_sem=send_ref.at[k - 1], recv_sem=recv_ref.at[k - 1],
        device_id=(px, py, pc), device_id_type=MESH)


def _gather_all_start(buf, name):
    def body(in_ref, send, recv, thru, token):
        del thru
        x, y, c, _ = _place()
        for k in range(1, 8):
            _all_copy(in_ref, send, recv, k, x, y, c, False).start()
        token[...] = jnp.zeros_like(token)

    return pl.pallas_call(
        body, name=name,
        in_specs=[HBM],
        out_specs=[SEM, SEM, HBM, pl.BlockSpec(memory_space=pltpu.VMEM)],
        out_shape=[pltpu.SemaphoreType.DMA((7,)), pltpu.SemaphoreType.DMA((7,)),
                   pltpu.HBM(buf.shape, buf.dtype), jax.ShapeDtypeStruct((8, 128), f32)],
        input_output_aliases={0: 2},
        compiler_params=pltpu.CompilerParams(has_side_effects=EFFECT),
    )(_in_hbm(buf))


def _gather_all_wait(send, recv, buf, after, name):
    def body(in_ref, send_r, recv_r, after_ref, out_ref):
        del after_ref, out_ref
        x, y, c, _ = _place()
        for k in range(1, 8):
            cp = _all_copy(in_ref, send_r, recv_r, k, x, y, c, True)
            cp.wait_send()
            cp.wait_recv()

    return pl.pallas_call(
        body, name=name,
        in_specs=[HBM, SEM, SEM, ANY],
        out_specs=HBM,
        out_shape=pltpu.HBM(buf.shape, buf.dtype),
        input_output_aliases={0: 0},
        compiler_params=pltpu.CompilerParams(has_side_effects=EFFECT),
    )(buf, send, recv, after)


HBM = pl.BlockSpec(memory_space=pltpu.HBM)
SEM = pl.BlockSpec(memory_space=pltpu.SEMAPHORE)
EFFECT = pltpu.SideEffectType.DATAFLOW_SIDE_EFFECTING
N_PEER = 3


def _in_hbm(a):
    return pltpu.with_memory_space_constraint(a, pltpu.HBM)


def _gather_copy(buf_ref, send_ref, recv_ref, j, chip, q, c, landing_chip):
    return pltpu.make_async_remote_copy(
        src_ref=buf_ref.at[q, c], dst_ref=buf_ref.at[landing_chip, c],
        send_sem=send_ref.at[j], recv_sem=recv_ref.at[j],
        device_id=(chip[0], chip[1], c), device_id_type=MESH)


def _gather_start(bufs, name):
    n = len(bufs)

    def body(*refs):
        ins = refs[:n]
        send, recv = refs[n:2 * n], refs[2 * n:3 * n]
        token = refs[4 * n]
        x, y, c, chips = _place()
        q = 2 * x + y
        for a in range(n):
            for j, chip in enumerate(chips):
                _gather_copy(ins[a], send[a], recv[a], j, chip, q, c, q).start()
        token[...] = jnp.zeros_like(token)

    sems = [pltpu.SemaphoreType.DMA((N_PEER,))] * (2 * n)
    outs = pl.pallas_call(
        body, name=name,
        in_specs=[HBM] * n,
        out_specs=[SEM] * (2 * n) + [HBM] * n + [pl.BlockSpec(memory_space=pltpu.VMEM)],
        out_shape=sems + [pltpu.HBM(b.shape, b.dtype) for b in bufs] + [jax.ShapeDtypeStruct((8, 128), f32)],
        input_output_aliases={a: 2 * n + a for a in range(n)},
        compiler_params=pltpu.CompilerParams(has_side_effects=EFFECT),
    )(*[_in_hbm(b) for b in bufs])
    return list(outs[:n]), list(outs[n:2 * n]), list(outs[2 * n:3 * n]), outs[3 * n]


def _gather_wait(send, recv, bufs, after, name):
    n = len(bufs)

    def body(*refs):
        ins = refs[:n]
        send_r, recv_r = refs[n:2 * n], refs[2 * n:3 * n]
        x, y, c, chips = _place()
        q = 2 * x + y
        for a in range(n):
            for j, chip in enumerate(chips):
                cp = _gather_copy(ins[a], send_r[a], recv_r[a], j, chip, q, c, _chip_id(chip))
                cp.wait_send()
                cp.wait_recv()

    afters = after if isinstance(after, (tuple, list)) else (after,)
    outs = pl.pallas_call(
        body, name=name,
        in_specs=[HBM] * n + [SEM] * (2 * n) + [ANY] * len(afters),
        out_specs=[HBM] * n,
        out_shape=[pltpu.HBM(b.shape, b.dtype) for b in bufs],
        input_output_aliases={a: a for a in range(n)},
        compiler_params=pltpu.CompilerParams(has_side_effects=EFFECT),
    )(*bufs, *send, *recv, *afters)
    return list(outs)


def _forward_halves(bufs, name):
    n = len(bufs)

    def body(*refs):
        outs = refs[n:2 * n]
        send_sems, recv_sems = refs[2 * n:]
        x, y, c, chips = _place()
        sibling = (x, y, 1 - c)

        def remote(a, j, blk):
            return pltpu.make_async_remote_copy(src_ref=blk, dst_ref=blk, send_sem=send_sems.at[a, j],
                                                recv_sem=recv_sems.at[a, j], device_id=sibling,
                                                device_id_type=MESH)

        sent = []
        for a in range(n):
            for j, chip in enumerate(chips):
                cp = remote(a, j, outs[a].at[_chip_id(chip), c])
                cp.start()
                sent.append(cp)
        for a in range(n):
            for j, chip in enumerate(chips):
                remote(a, j, outs[a].at[_chip_id(chip), 1 - c]).wait_recv()
        for cp in sent:
            cp.wait_send()

    return pl.pallas_call(
        body, name=name,
        in_specs=[ANY] * n, out_specs=[ANY] * n,
        out_shape=[jax.ShapeDtypeStruct(s.shape, s.dtype) for s in bufs],
        scratch_shapes=[pltpu.SemaphoreType.DMA((n, N_PEER)), pltpu.SemaphoreType.DMA((n, N_PEER))],
        input_output_aliases={a: a for a in range(n)},
    )(*bufs)


def _reduce_copy(sum_ref, land_ref, send_ref, recv_ref, j, chip, q, c, landing_chip):
    return pltpu.make_async_remote_copy(
        src_ref=sum_ref.at[_chip_id(chip)], dst_ref=land_ref.at[landing_chip],
        send_sem=send_ref.at[j], recv_sem=recv_ref.at[j],
        device_id=(chip[0], chip[1], c), device_id_type=MESH)


def _reduce_start(sums, lands, name):
    n = len(sums)

    def body(*refs):
        s_in, l_in = refs[:n], refs[n:2 * n]
        send, recv = refs[2 * n:3 * n], refs[3 * n:4 * n]
        token = refs[6 * n]
        x, y, c, chips = _place()
        q = 2 * x + y
        for a in range(n):
            for j, chip in enumerate(chips):
                _reduce_copy(s_in[a], l_in[a], send[a], recv[a], j, chip, q, c, q).start()
        token[...] = jnp.zeros_like(token)

    sems = [pltpu.SemaphoreType.DMA((N_PEER,))] * (2 * n)
    outs = pl.pallas_call(
        body, name=name,
        in_specs=[HBM] * (2 * n),
        out_specs=[SEM] * (2 * n) + [HBM] * (2 * n) + [pl.BlockSpec(memory_space=pltpu.VMEM)],
        out_shape=sems + [pltpu.HBM(b.shape, b.dtype) for b in list(sums) + list(lands)]
        + [jax.ShapeDtypeStruct((8, 128), f32)],
        input_output_aliases={a: 2 * n + a for a in range(2 * n)},
        compiler_params=pltpu.CompilerParams(has_side_effects=EFFECT),
    )(*[_in_hbm(b) for b in list(sums) + list(lands)])
    return (list(outs[:n]), list(outs[n:2 * n]), list(outs[2 * n:3 * n]), list(outs[3 * n:4 * n]),
            outs[4 * n])


def _reduce_wait(send, recv, sums, lands, after, name):
    n = len(sums)

    def body(*refs):
        s_in, l_in = refs[:n], refs[n:2 * n]
        send_r, recv_r = refs[2 * n:3 * n], refs[3 * n:4 * n]
        x, y, c, chips = _place()
        q = 2 * x + y
        for a in range(n):
            for j, chip in enumerate(chips):
                cp = _reduce_copy(s_in[a], l_in[a], send_r[a], recv_r[a], j, chip, q, c, _chip_id(chip))
                cp.wait_send()
                cp.wait_recv()

    afters = after if isinstance(after, (tuple, list)) else (after,)
    outs = pl.pallas_call(
        body, name=name,
        in_specs=[HBM] * (2 * n) + [SEM] * (2 * n) + [ANY] * len(afters),
        out_specs=[HBM] * (2 * n),
        out_shape=[pltpu.HBM(b.shape, b.dtype) for b in list(sums) + list(lands)],
        input_output_aliases={a: a for a in range(2 * n)},
        compiler_params=pltpu.CompilerParams(has_side_effects=EFFECT),
    )(*sums, *lands, *send, *recv, *afters)
    return list(outs[n:])


def _sibling_copy(part_ref, land_ref, send_ref, recv_ref, x, y, c):
    return pltpu.make_async_remote_copy(
        src_ref=part_ref.at[:, 1 - c], dst_ref=land_ref, send_sem=send_ref.at[0], recv_sem=recv_ref.at[0],
        device_id=(x, y, 1 - c), device_id_type=MESH)


def _pair_exchange_start(parts, name):
    n = len(parts)
    lands = [lax.empty((NCHIP,) + p.shape[2:], p.dtype) for p in parts]

    def body(*refs):
        p_in, l_in = refs[:n], refs[n:2 * n]
        send, recv = refs[2 * n:3 * n], refs[3 * n:4 * n]
        token = refs[6 * n]
        x, y, c, _ = _place()
        for a in range(n):
            _sibling_copy(p_in[a], l_in[a], send[a], recv[a], x, y, c).start()
        token[...] = jnp.zeros_like(token)

    sems = [pltpu.SemaphoreType.DMA((1,))] * (2 * n)
    outs = pl.pallas_call(
        body, name=name,
        in_specs=[HBM] * (2 * n),
        out_specs=[SEM] * (2 * n) + [HBM] * (2 * n) + [pl.BlockSpec(memory_space=pltpu.VMEM)],
        out_shape=sems + [pltpu.HBM(b.shape, b.dtype) for b in list(parts) + lands]
        + [jax.ShapeDtypeStruct((8, 128), f32)],
        input_output_aliases={a: 2 * n + a for a in range(2 * n)},
        compiler_params=pltpu.CompilerParams(has_side_effects=EFFECT),
    )(*[_in_hbm(b) for b in list(parts) + lands])
    return (list(outs[:n]), list(outs[n:2 * n]), list(outs[2 * n:3 * n]), list(outs[3 * n:4 * n]),
            outs[4 * n])


def _pair_exchange_wait(send, recv, parts, lands, after, name):
    n = len(parts)

    def body(*refs):
        p_in, l_in = refs[:n], refs[n:2 * n]
        send_r, recv_r = refs[2 * n:3 * n], refs[3 * n:4 * n]
        x, y, c, _ = _place()
        for a in range(n):
            cp = _sibling_copy(p_in[a], l_in[a], send_r[a], recv_r[a], x, y, c)
            cp.wait_send()
            cp.wait_recv()

    outs = pl.pallas_call(
        body, name=name,
        in_specs=[HBM] * (2 * n) + [SEM] * (2 * n) + [ANY],
        out_specs=[HBM] * (2 * n),
        out_shape=[pltpu.HBM(b.shape, b.dtype) for b in list(parts) + list(lands)],
        input_output_aliases={a: a for a in range(2 * n)},
        compiler_params=pltpu.CompilerParams(has_side_effects=EFFECT),
    )(*parts, *lands, *send, *recv, after)
    return list(outs[:n]), list(outs[n:])


def _forward_copy(buf_ref, send_ref, recv_ref, j, chip, x, y, c, landing):
    return pltpu.make_async_remote_copy(
        src_ref=buf_ref.at[_chip_id(chip), c], dst_ref=buf_ref.at[_chip_id(chip), 1 - c if landing else c],
        send_sem=send_ref.at[j], recv_sem=recv_ref.at[j], device_id=(x, y, 1 - c), device_id_type=MESH)


def _forward_start(bufs, name):
    n = len(bufs)

    def body(*refs):
        ins = refs[:n]
        send, recv = refs[n:2 * n], refs[2 * n:3 * n]
        token = refs[4 * n]
        x, y, c, chips = _place()
        for a in range(n):
            for j, chip in enumerate(chips):
                _forward_copy(ins[a], send[a], recv[a], j, chip, x, y, c, False).start()
        token[...] = jnp.zeros_like(token)

    sems = [pltpu.SemaphoreType.DMA((N_PEER,))] * (2 * n)
    outs = pl.pallas_call(
        body, name=name,
        in_specs=[HBM] * n,
        out_specs=[SEM] * (2 * n) + [HBM] * n + [pl.BlockSpec(memory_space=pltpu.VMEM)],
        out_shape=sems + [pltpu.HBM(b.shape, b.dtype) for b in bufs] + [jax.ShapeDtypeStruct((8, 128), f32)],
        input_output_aliases={a: 2 * n + a for a in range(n)},
        compiler_params=pltpu.CompilerParams(has_side_effects=EFFECT),
    )(*[_in_hbm(b) for b in bufs])
    return list(outs[:n]), list(outs[n:2 * n]), list(outs[2 * n:3 * n]), outs[3 * n]


def _forward_wait(send, recv, bufs, after, name):
    n = len(bufs)

    def body(*refs):
        ins = refs[:n]
        send_r, recv_r = refs[n:2 * n], refs[2 * n:3 * n]
        x, y, c, chips = _place()
        for a in range(n):
            for j, chip in enumerate(chips):
                cp = _forward_copy(ins[a], send_r[a], recv_r[a], j, chip, x, y, c, True)
                cp.wait_send()
                cp.wait_recv()

    outs = pl.pallas_call(
        body, name=name,
        in_specs=[HBM] * n + [SEM] * (2 * n) + [ANY],
        out_specs=[HBM] * n,
        out_shape=[pltpu.HBM(b.shape, b.dtype) for b in bufs],
        input_output_aliases={a: a for a in range(n)},
        compiler_params=pltpu.CompilerParams(has_side_effects=EFFECT),
    )(*bufs, *send, *recv, after)
    return list(outs)


def _adamw_math(w, g, m, v):
    m = ADAM_B1 * m + (1.0 - ADAM_B1) * g
    v = ADAM_B2 * v + (1.0 - ADAM_B2) * (g * g)
    m_hat = m / (1.0 - ADAM_B1 ** ADAM_STEP)
    v_hat = v / (1.0 - ADAM_B2 ** ADAM_STEP)
    delta = -ADAM_LR * (m_hat / (jnp.sqrt(v_hat) + ADAM_EPS) + ADAM_WD * w)
    return delta, m, v


ADAMW_BLOCK_BYTES = 3 << 19


def _adamw(ws, mines, theirs, ms, vs, qc, name):
    n = len(ws)
    halves = [w.shape[0] // 2 for w in ws]
    nb = next(k for k in range(1, min(halves) + 1)
              if all(hr % k == 0 and (hr // k) % 8 == 0 and (hr // k) * w.shape[1] * 4 <= ADAMW_BLOCK_BYTES
                     for hr, w in zip(halves, ws)))

    def body(qc_ref, *refs):
        mine_here = pl.program_id(0) == qc_ref[1]
        for a in range(n):
            w_ref, a_ref, b_ref, m_ref, v_ref = (refs[k * n + a] for k in range(5))
            g_ref, d_ref, mo_ref, vo_ref = (refs[(5 + k) * n + a] for k in range(4))
            g = jnp.where(mine_here, a_ref[...], b_ref[...])
            g_ref[...] = g
            d_ref[...], mo_ref[...], vo_ref[...] = _adamw_math(w_ref[...], g, m_ref[...], v_ref[...])

    blocks = [(hr // nb, w.shape[1]) for hr, w in zip(halves, ws)]
    full = [pl.BlockSpec(b, lambda h, i, qc_ref: (h * nb + i, 0)) for b in blocks]
    half = [pl.BlockSpec(b, lambda h, i, qc_ref: (i, 0)) for b in blocks]
    outs = pl.pallas_call(
        body, name=name,
        grid_spec=pltpu.PrefetchScalarGridSpec(
            num_scalar_prefetch=1, grid=(2, nb),
            in_specs=full + half + half + full + full, out_specs=full * 4),
        out_shape=[jax.ShapeDtypeStruct(w.shape, f32) for w in ws] * 4,
        compiler_params=_cp(2),
    )(qc, *ws, *mines, *theirs, *ms, *vs)
    return [tuple(outs[k * n + a] for k in range(4)) for a in range(n)]


REPL = [("ffn1_norm", 1), ("mix_norm", 1), ("b_in", 6), ("rnn_conv_b", 1), ("rg_b_a", 1), ("rg_b_x", 1),
        ("rg_lambda", 1), ("conv_dw_b", 1), ("conv_ln_g", 1), ("conv_ln_b", 1), ("conv_b_proj", 1),
        ("ffn2_norm", 1), ("final_norm", 1)]
COLSH = [("meta_tokens", NMETA), ("rnn_conv_w", KC4), ("conv_dw_w", KC31)]
SMALL = REPL + COLSH
CS = D // NCHIP


def _pack_rows():
    starts, row = {}, 0
    for k, rows in REPL:
        starts[k] = row
        row += rows
    for k, rows in COLSH:
        row = -(-row // 8) * 8
        starts[k] = row
        row += rows
    return starts, -(-row // 8) * 8


PACK_START, LOSS_ROW = _pack_rows()
SMALL_ROWS = LOSS_ROW + 8


def _small_pack(g, loss_row):
    pieces, row = [], 0
    for k, rows in SMALL:
        if PACK_START[k] > row:
            pieces.append(jnp.zeros((PACK_START[k] - row, D), f32))
        pieces.append(g[k].reshape(rows, D))
        row = PACK_START[k] + rows
    pieces.append(jnp.zeros((LOSS_ROW - row, D), f32))
    pieces.append(loss_row)
    pieces.append(jnp.zeros((SMALL_ROWS - LOSS_ROW - 1, D), f32))
    return jnp.concatenate(pieces, axis=0)


def _adamw_small(packs, ws, ms, vs):
    ns = len(SMALL)

    def body(*refs):
        pack_ref = refs[0]
        w_refs, m_refs, v_refs = refs[1:1 + ns], refs[1 + ns:1 + 2 * ns], refs[1 + 2 * ns:1 + 3 * ns]
        outs = refs[1 + 3 * ns:1 + 7 * ns]
        g_refs, d_refs, mo_refs, vo_refs = outs[:ns], outs[ns:2 * ns], outs[2 * ns:3 * ns], outs[3 * ns:]
        loss_ref = refs[1 + 7 * ns]
        gsum_sc = refs[2 + 7 * ns]
        q = 2 * lax.axis_index("x") + lax.axis_index("y")
        acc = pack_ref[0]
        for dev in range(1, 8):
            acc = acc + pack_ref[dev]
        gsum_sc[...] = acc
        loss_ref[...] = gsum_sc[LOSS_ROW:LOSS_ROW + 1, :]
        for idx, (name, rows) in enumerate(SMALL):
            row = PACK_START[name]
            if idx < len(REPL):
                for k in range(rows):
                    cols = slice(k * D, (k + 1) * D)
                    g = gsum_sc[row + k:row + k + 1, :]
                    d, mm, vv = _adamw_math(w_refs[idx][:, cols], g, m_refs[idx][:, cols], v_refs[idx][:, cols])
                    g_refs[idx][:, cols] = g
                    d_refs[idx][:, cols] = d
                    mo_refs[idx][:, cols] = mm
                    vo_refs[idx][:, cols] = vv
            else:
                g = gsum_sc[row:row + rows, pl.ds(pl.multiple_of(q * CS, CS), CS)]
                d, mm, vv = _adamw_math(w_refs[idx][...], g, m_refs[idx][...], v_refs[idx][...])
                g_refs[idx][...] = g
                d_refs[idx][...] = d
                mo_refs[idx][...] = mm
                vo_refs[idx][...] = vv

    shapes = [jax.ShapeDtypeStruct(w.shape, f32) for w in ws]
    return pl.pallas_call(
        body, name="adamw_small",
        out_shape=shapes * 4 + [jax.ShapeDtypeStruct((1, D), f32)],
        scratch_shapes=[pltpu.VMEM((SMALL_ROWS, D), f32)],
        compiler_params=pltpu.CompilerParams(vmem_limit_bytes=VMEM_LIMIT),
    )(packs, *ws, *ms, *vs)


WEIGHTS = ['meta_tokens', 'ffn1_norm', 'ffn1_w_gu', 'ffn1_w_down', 'mix_norm', 'w_in', 'b_in', 'rnn_conv_w',
           'rnn_conv_b', 'rg_w_a', 'rg_b_a', 'rg_w_x', 'rg_b_x', 'rg_lambda', 'rnn_w_proj', 'conv_dw_w',
           'conv_dw_b', 'conv_ln_g', 'conv_ln_b', 'conv_w_proj', 'conv_b_proj', 'w_out', 'ffn2_norm',
           'ffn2_w_gu', 'ffn2_w_down', 'final_norm']


def _as2d(a):
    return a.reshape(-1, a.shape[-1])


def _step(x, loss_target, w, m, v):
    seq = x.shape[1]
    n_valid = NMETA + seq
    t = -(-n_valid // TM) * TM

    qc = jnp.stack([2 * lax.axis_index("x") + lax.axis_index("y"), lax.axis_index("c")]).astype(jnp.int32)
    p = {k: w[k].reshape(1, rows * D) for k, rows in REPL}

    first = ["ffn1_w_gu", "ffn1_w_down", "small"]
    later = [["w_in"], ["rg_w_a", "rg_w_x", "rnn_w_proj", "conv_w_proj", "w_out"], ["ffn2_w_gu", "ffn2_w_down"]]
    small_rows = sum(r for _, r in COLSH)
    small = jnp.concatenate([_as2d(w[k]) for k, _ in COLSH] + [jnp.zeros((64 - small_rows, CS), f32)], axis=0)

    def cast(k, token=None):
        src, dtype = (small, f32) if k == "small" else (_as2d(w[k]), bf16)
        return _cast_into_slot(src, qc, dtype, "cast_" + k, after=token)

    send1, recv1, bufs1, token1 = _gather_start([cast(k) for k in first], "gather_start_first")
    rest = [k for grp in later for k in grp]
    send2, recv2, bufs2, token2 = _gather_start([cast(k, token1) for k in rest], "gather_start_rest")

    def install(names, done):
        for k, b in zip(names, done):
            full = b.reshape(NCHIP, 2 * b.shape[2], b.shape[3])
            if k in ("ffn1_w_down", "ffn2_w_down"):
                full = full.reshape(F, D)
            elif k in ("rnn_w_proj", "conv_w_proj", "w_out"):
                full = full.reshape(D, D)
            elif k in ("rg_w_a", "rg_w_x"):
                full = full.reshape(NCHIP, NHEAD, HD // NCHIP, HD).transpose(1, 0, 2, 3).reshape(NHEAD, HD, HD)
            p[k] = full

    def finish(names, send, recv, bufs, after, tag):
        install(names, _forward_halves(_gather_wait(send, recv, bufs, after, "gather_wait_" + tag),
                                       "gather_forward_" + tag))

    def group(names):
        idx = [rest.index(k) for k in names]
        return names, [send2[i] for i in idx], [recv2[i] for i in idx], [bufs2[i] for i in idx]

    h0 = jnp.pad(x[0] + token1[0:1, 0:1], ((NMETA, t - n_valid), (0, 0)))
    tgt = jnp.pad(loss_target[0] + token2[0:1, 0:1], ((NMETA, t - n_valid), (0, 0)))
    finish(first, send1, recv1, bufs1, (token2, h0, tgt), "first")
    small_full = p.pop("small").transpose(1, 0, 2).reshape(64, D)
    row = 0
    for k, rows in COLSH:
        p[k] = small_full[row:row + rows]
        row += rows

    h0 = lax.dynamic_update_slice(h0, p["meta_tokens"], (0, 0))
    h1, gate1, up1, n1 = _ffn_fwd(h0, p["ffn1_norm"], p["ffn1_w_gu"], p["ffn1_w_down"], "ffn1_fwd")
    finish(*group(later[0]), h1, "in")
    proj, n2 = _inproj_fwd(h1, p["mix_norm"], p["w_in"], p["b_in"])
    names_l = later[1] + later[2]
    _, send_l, recv_l, bufs_l = group(names_l)
    send_f, recv_f, bufs_f, token = _forward_start(
        _gather_wait(send_l, recv_l, bufs_l, proj, "gather_wait_late"), "gather_forward_start")
    vc, s = _conv_fwd(proj, p["conv_dw_w"], p["conv_dw_b"], p["conv_ln_g"], p["conv_ln_b"], after=token)
    install(names_l, _forward_wait(send_f, recv_f, bufs_f, vc, "gather_forward_wait"))
    xr, hr, z, gates = _rnn_fwd(proj, p["rnn_conv_w"], p["rnn_conv_b"], p["rg_w_a"], p["rg_b_a"],
                         p["rg_w_x"], p["rg_b_x"], p["rg_lambda"])
    h2 = _merge_fwd(h1, z, s, proj, p["rnn_w_proj"], p["conv_w_proj"], p["conv_b_proj"], p["w_out"])
    dh3, loss_blk, d_final, gate2, up2, n3 = _ffn_fwd(
        h2, p["ffn2_norm"], p["ffn2_w_gu"], p["ffn2_w_down"], "ffn2_fwd",
        loss_head=(p["final_norm"], tgt, n_valid))

    g = {"final_norm": d_final}
    pending = []

    def exchange_start(names, tag):
        parts = []
        for k in names:
            rows = g[k].size // (NCHIP * g[k].shape[-1])
            parts.append(g[k].reshape((NCHIP, 2, rows // 2, g[k].shape[-1])))
        send, recv, parts, lands, token = _pair_exchange_start(parts, "pair_exchange_start_" + tag)
        return (names, tag, send, recv, parts, lands), token

    def reduce_start(state, after):
        names, tag, send, recv, parts, lands = state
        parts, from_sibling = _pair_exchange_wait(send, recv, parts, lands, after, "pair_exchange_wait_" + tag)
        sums, lands = _pair_add(parts, from_sibling, qc, "pair_add_" + tag)
        send, recv, sums, lands, token = _reduce_start(sums, lands, "reduce_start_" + tag)
        pending.append((names, tag, send, recv, sums, lands))
        return token

    dh2, dgate2, dup2, a2, df2, g["ffn2_norm"] = _ffn_bwd(
        dh3, h2, p["ffn2_norm"], gate2, up2, p["ffn2_w_gu"], p["ffn2_w_down"], "ffn2_bwd")
    g["ffn2_w_gu"] = _ffn_gu_grad(n3, dgate2, dup2, "ffn2")
    g["ffn2_w_down"] = _ffn_down_grad(a2, df2, "ffn2")
    state, token = exchange_start(["ffn2_w_gu", "ffn2_w_down"], "ffn2")

    dz, ds, dproj, dh2b, merged, dya, dyb, g["conv_b_proj"] = _merge_bwd(
        dh2, z, s, proj, p["rnn_w_proj"], p["conv_w_proj"], p["conv_b_proj"], p["w_out"], after=token)
    token = reduce_start(state, dz)
    dproj, g["conv_dw_w"], g["conv_dw_b"], g["conv_ln_g"], g["conv_ln_b"] = _conv_bwd(
        ds, vc, proj, dproj, p["conv_dw_w"], p["conv_ln_g"], p["conv_ln_b"], after=token)
    g["w_out"] = _square_grad(merged, dh2b, "dw_out")
    g["rnn_w_proj"] = _square_grad(z, dya, "dw_rnn_proj")
    g["conv_w_proj"] = _square_grad(s, dyb, "dw_conv_proj")
    (dproj, g["rg_w_a"], g["rg_w_x"], g["rnn_conv_w"], g["rnn_conv_b"], g["rg_b_a"], g["rg_b_x"],
     g["rg_lambda"]) = _rnn_bwd(dz, xr, hr, gates, proj, dproj, p["rnn_conv_w"], p["rg_w_a"],
                                p["rg_w_x"], p["rg_lambda"])

    dh1, g["mix_norm"], db_in = _inproj_bwd(dproj, dh2, h1, p["mix_norm"], p["w_in"])
    g["b_in"] = db_in.reshape(1, NIN)
    g["w_in"] = _tn_matmul(n2, dproj, D, NIN // NCHIP, (NCHIP, D, NIN // NCHIP),
                           (None, D, NIN // NCHIP), lambda k, nn, mm: (nn, 0, 0), "dw_in")
    state, token = exchange_start(["w_out", "rnn_w_proj", "conv_w_proj", "rg_w_a", "rg_w_x", "w_in"], "mix")

    dh0, dgate1, dup1, a1, df1, g["ffn1_norm"] = _ffn_bwd(
        dh1, h0, p["ffn1_norm"], gate1, up1, p["ffn1_w_gu"], p["ffn1_w_down"], "ffn1_bwd", after=token)
    g["meta_tokens"] = dh0[0:NMETA]
    grad_x = dh0[NMETA:n_valid][None]
    token = reduce_start(state, dh0)

    send_s, recv_s, pack_buf, token_s = _gather_all_start(
        _place_pack(_small_pack(g, loss_blk.reshape(1, D)), qc), "gather_all_start")
    g["ffn1_w_down"] = _ffn_down_grad(a1, df1, "ffn1", after=(token, token_s))
    state, token = exchange_start(["ffn1_w_down"], "ffn1_down")
    gate_half = _tn_matmul(n1, dgate1, D, FS, (NCHIP, D, FS), (None, D, FS), lambda k, nn, mm: (nn, 0, 0),
                           "ffn1_dwg", after=token)
    token = reduce_start(state, gate_half)
    g["ffn1_w_gu"] = _tn_matmul(n1, dup1, D, FS, (NCHIP, D, FS), (None, D, FS), lambda k, nn, mm: (2 + nn, 0, 0),
                                "ffn1_dwu", base=gate_half, after=token)
    state_gu, token = exchange_start(["ffn1_w_gu"], "ffn1_gu")
    packs = _gather_all_wait(send_s, recv_s, pack_buf, token, "gather_all_wait")

    grads, deltas, new_m, new_v = {}, {}, {}, {}

    def landed_sums(items, after):
        names, mine = [], []
        for grp_names, grp_tag, send, recv, sums, lands in items:
            landed = _reduce_wait(send, recv, sums, lands, after, "reduce_wait_" + grp_tag)
            mine += _sum_chips(landed, "sum_chips_" + grp_tag)
            names += grp_names
            after = mine[-1]
        return names, mine

    def share_and_update(names, mine, tag, after=None):
        theirs = _pair_share(mine, "pair_share_" + tag, after=after)
        got = dict(zip(names, zip(mine, theirs)))
        square = [k for k in names if got[k][0].shape[0] * 2 <= HD]
        for batch in [[k] for k in names if k not in square] + ([square] if square else []):
            outs = _adamw([_as2d(w[k]) for k in batch], [got[k][0] for k in batch], [got[k][1] for k in batch],
                          [_as2d(m[k]) for k in batch], [_as2d(v[k]) for k in batch], qc,
                          "adamw_" + (batch[0] if len(batch) == 1 else "mixer"))
            for k, out in zip(batch, outs):
                grads[k], deltas[k], new_m[k], new_v[k] = (a.reshape(w[k].shape) for a in out)
        return [new_v[k] for k in names]

    early_names, early_mine = landed_sums(pending[:2], packs)
    token = reduce_start(state_gu, early_mine[-1])
    after = share_and_update(early_names, early_mine, "early", after=token)
    share_and_update(*landed_sums(pending[2:], after), "late")
    names = [k for k, _ in SMALL]
    shape2 = {k: ((1, rows * D) if (k, rows) in REPL else (rows, CS)) for k, rows in SMALL}
    outs = _adamw_small(packs, *[[a[k].reshape(shape2[k]) for k in names] for a in (w, m, v)])
    ns = len(names)
    for i, k in enumerate(names):
        grads[k], deltas[k], new_m[k], new_v[k] = (outs[j * ns + i].reshape(w[k].shape) for j in range(4))

    loss = outs[4 * ns][0, 0]
    return (loss, grad_x, *[grads[k] for k in WEIGHTS], *[deltas[k] for k in WEIGHTS],
            *[new_m[k] for k in WEIGHTS], *[new_v[k] for k in WEIGHTS])


def kernel(x, meta_tokens, ffn1_norm, ffn1_w_gu, ffn1_w_down, mix_norm, w_in, b_in, rnn_conv_w, rnn_conv_b, rg_w_a, rg_b_a, rg_w_x, rg_b_x, rg_lambda, rnn_w_proj, conv_dw_w, conv_dw_b, conv_ln_g, conv_ln_b, conv_w_proj, conv_b_proj, w_out, ffn2_norm, ffn2_w_gu, ffn2_w_down, final_norm, loss_target, m_meta_tokens, m_ffn1_norm, m_ffn1_w_gu, m_ffn1_w_down, m_mix_norm, m_w_in, m_b_in, m_rnn_conv_w, m_rnn_conv_b, m_rg_w_a, m_rg_b_a, m_rg_w_x, m_rg_b_x, m_rg_lambda, m_rnn_w_proj, m_conv_dw_w, m_conv_dw_b, m_conv_ln_g, m_conv_ln_b, m_conv_w_proj, m_conv_b_proj, m_w_out, m_ffn2_norm, m_ffn2_w_gu, m_ffn2_w_down, m_final_norm, v_meta_tokens, v_ffn1_norm, v_ffn1_w_gu, v_ffn1_w_down, v_mix_norm, v_w_in, v_b_in, v_rnn_conv_w, v_rnn_conv_b, v_rg_w_a, v_rg_b_a, v_rg_w_x, v_rg_b_x, v_rg_lambda, v_rnn_w_proj, v_conv_dw_w, v_conv_dw_b, v_conv_ln_g, v_conv_ln_b, v_conv_w_proj, v_conv_b_proj, v_w_out, v_ffn2_norm, v_ffn2_w_gu, v_ffn2_w_down, v_final_norm):
    args = locals()
    w = {k: args[k] for k in WEIGHTS}
    m = {k: args["m_" + k] for k in WEIGHTS}
    v = {k: args["v_" + k] for k in WEIGHTS}
    return _step(x, loss_target, w, m, v)
```

```python
import jax
import jax.numpy as jnp
from jax import lax
from jax.experimental import pallas as pl
from jax.experimental.pallas import tpu as pltpu

f32 = jnp.float32
bf16 = jnp.bfloat16

D = 1024
F = 2816
FS = F // 2
NIN = 6 * D
NMETA = 16
NHEAD = 4
HD = D // NHEAD
KC4 = 4
KC31 = 31
HALO = 32
EPS = 1e-6
TM = 416
NCHIP = 4
MESH = pl.DeviceIdType.MESH

ADAM_LR = 0.001
ADAM_B1 = 0.9
ADAM_B2 = 0.999
ADAM_EPS = 1e-08
ADAM_WD = 0.01
ADAM_STEP = 10

VMEM_LIMIT = 56 * 1024 * 1024
FSUB = [(o, min(256, FS - o)) for o in range(0, FS, 256)]


def _cp(n_axes, **kw):
    return pltpu.CompilerParams(dimension_semantics=("arbitrary",) * n_axes,
                                vmem_limit_bytes=VMEM_LIMIT, **kw)


RESIDENT = pl.BlockSpec(memory_space=pltpu.VMEM)


def _n_after(after):
    return 0 if after is None else (len(after) if isinstance(after, (tuple, list)) else 1)


def _ordered(body, in_specs, args, after):
    if after is None:
        return body, in_specs, args
    extra = tuple(after) if isinstance(after, (tuple, list)) else (after,)
    return (lambda *refs: body(*refs[len(extra):]),
            [pl.BlockSpec(memory_space=pl.ANY)] * len(extra) + list(in_specs), extra + tuple(args))


def _nt_dot(a, b):
    return lax.dot_general(a, b, (((1,), (1,)), ((), ())), preferred_element_type=f32)


def _tn_dot(a, b):
    return lax.dot_general(a, b, (((0,), (0,)), ((), ())), preferred_element_type=f32)


def _sigmoid(x):
    return 0.5 * jnp.tanh(0.5 * x) + 0.5


def _log1p(y):
    u = 1.0 + y
    d = u - 1.0
    return jnp.where(d == 0.0, y, jnp.log(u) * (y / jnp.where(d == 0.0, 1.0, d)))


def _softplus(x):
    return jnp.maximum(x, 0.0) + _log1p(jnp.exp(-jnp.abs(x)))


def _one_minus_square(a, log_a):
    x = 2.0 * log_a
    series = x * (1.0 + x * (0.5 + x * (1.0 / 6.0)))
    return jnp.where(jnp.abs(x) < 0.03, -series, 1.0 - a * a)


_GELU_C = 0.7978845608028654
_GELU_K = 0.044715


def _gelu_and_grad(y):
    y2 = y * y
    th = jnp.tanh(_GELU_C * (y + _GELU_K * y * y2))
    gel = 0.5 * y * (1.0 + th)
    dgel = 0.5 * (1.0 + th) + 0.5 * y * (1.0 - th * th) * _GELU_C * (1.0 + 3.0 * _GELU_K * y2)
    return gel, dgel


def _rms_stats(h):
    return lax.rsqrt(jnp.mean(h * h, axis=-1, keepdims=True) + EPS)


def _rms_bwd(dn, h, g):
    r = _rms_stats(h)
    nhat = h * r
    dnh = dn * g
    dh = r * (dnh - nhat * jnp.mean(dnh * nhat, axis=-1, keepdims=True))
    dg = jnp.sum(dn * nhat, axis=0, keepdims=True)
    return dh, dg


def _row_ids(shape):
    return lax.broadcasted_iota(jnp.int32, shape, 0)


def _ffn_fwd(h, g, wgu, wd, name, loss_head=None):
    t = h.shape[0]
    nj = 2
    tm = TM
    n_head = 0 if loss_head is None else 2

    def body(*refs):
        h_ref, g_ref, wg_ref, wd_ref = refs[:4]
        outs = refs[4 + n_head:]
        gate_ref, up_ref, n_ref, nb_sc, acc_sc, a_sc = outs[-6:]
        i = pl.program_id(0)
        j = pl.program_id(1)

        @pl.when(j == 0)
        def _():
            hh = h_ref[...]
            nb = (hh * _rms_stats(hh) * g_ref[...]).astype(bf16)
            nb_sc[...] = nb
            n_ref[...] = nb
            acc_sc[...] = jnp.zeros_like(acc_sc)

        nb = nb_sc[...]
        for off, width in FSUB:
            cols = slice(off, off + width)
            gt = jnp.dot(nb, wg_ref[j, :, cols], preferred_element_type=f32)
            up = jnp.dot(nb, wg_ref[2 + j, :, cols], preferred_element_type=f32)
            gate_ref[:, cols] = gt.astype(bf16)
            up_ref[:, cols] = up.astype(bf16)
            a_sc[:, cols] = (gt * _sigmoid(gt) * up).astype(bf16)
        acc_sc[...] += jnp.dot(a_sc[...], wd_ref[j], preferred_element_type=f32)

        if loss_head is None:
            @pl.when(j == nj - 1)
            def _():
                outs[0][...] = h_ref[...] + 0.5 * acc_sc[...]
        else:
            gf_ref, t_ref = refs[4:6]
            dh_ref, loss_ref, dgf_ref = outs[:3]

            @pl.when(jnp.logical_and(i == 0, j == 0))
            def _():
                loss_ref[...] = jnp.zeros_like(loss_ref)
                dgf_ref[...] = jnp.zeros_like(dgf_ref)

            @pl.when(j == nj - 1)
            def _():
                hh = h_ref[...] + 0.5 * acc_sc[...]
                gf = gf_ref[...]
                row = i * tm + _row_ids((tm, 1))
                valid = jnp.logical_and(row >= NMETA, row < loss_head[2])
                err = jnp.where(valid, hh * _rms_stats(hh) * gf - t_ref[...], 0.0)
                loss_ref[...] += 0.5 * jnp.sum(err * err) * (1.0 / D)
                dh, dgf = _rms_bwd(err * (1.0 / D), hh, gf)
                dh_ref[...] = dh
                dgf_ref[...] += dgf

    rowd = pl.BlockSpec((tm, D), lambda i, j: (i, 0))
    vec = pl.BlockSpec((1, D), lambda i, j: (0, 0))
    rowf = pl.BlockSpec((tm, FS), lambda i, j: (i, j))
    in_specs, args = [rowd, vec, RESIDENT, RESIDENT], [h, g, wgu, wd.reshape(nj, FS, D)]
    out_specs, out_shape = [rowd], [jax.ShapeDtypeStruct((t, D), f32)]
    if loss_head is not None:
        in_specs, args = in_specs + [vec, rowd], args + [loss_head[0], loss_head[1]]
        out_specs += [pl.BlockSpec((8, 128), lambda i, j: (0, 0)), vec]
        out_shape += [jax.ShapeDtypeStruct((8, 128), f32), jax.ShapeDtypeStruct((1, D), f32)]
    return pl.pallas_call(
        body, name=name, grid=(t // tm, nj),
        in_specs=in_specs,
        out_specs=out_specs + [rowf, rowf, rowd],
        out_shape=out_shape + [jax.ShapeDtypeStruct((t, F), bf16), jax.ShapeDtypeStruct((t, F), bf16),
                               jax.ShapeDtypeStruct((t, D), bf16)],
        scratch_shapes=[pltpu.VMEM((tm, D), bf16), pltpu.VMEM((tm, D), f32), pltpu.VMEM((tm, FS), bf16)],
        compiler_params=_cp(2),
    )(*args)


def _inproj_fwd(h, g, win, b_in):
    t = h.shape[0]
    tn = NIN // NCHIP
    nj = NIN // tn

    def body(h_ref, g_ref, w_ref, b_ref, proj_ref, n_ref, nb_sc):
        j = pl.program_id(1)

        @pl.when(j == 0)
        def _():
            hh = h_ref[...]
            nb = (hh * _rms_stats(hh) * g_ref[...]).astype(bf16)
            nb_sc[...] = nb
            n_ref[...] = nb

        proj_ref[...] = jnp.dot(nb_sc[...], w_ref[j], preferred_element_type=f32) + b_ref[...]

    return pl.pallas_call(
        body, name="inproj_fwd", grid=(t // TM, nj),
        in_specs=[
            pl.BlockSpec((TM, D), lambda i, j: (i, 0)),
            pl.BlockSpec((1, D), lambda i, j: (0, 0)),
            RESIDENT,
            pl.BlockSpec((1, tn), lambda i, j: (0, j)),
        ],
        out_specs=[
            pl.BlockSpec((TM, tn), lambda i, j: (i, j)),
            pl.BlockSpec((TM, D), lambda i, j: (i, 0)),
        ],
        out_shape=[jax.ShapeDtypeStruct((t, NIN), f32), jax.ShapeDtypeStruct((t, D), bf16)],
        scratch_shapes=[pltpu.VMEM((TM, D), bf16)],
        compiler_params=_cp(2),
    )(h, g, win, b_in)


def _block_gates(xr, wa_ref, ba, wx_ref, bx, lam):
    xrb = xr.astype(bf16)
    pa = jnp.concatenate([jnp.dot(xrb[:, hh * HD:(hh + 1) * HD], wa_ref[hh], preferred_element_type=f32)
                          for hh in range(NHEAD)], axis=1)
    px = jnp.concatenate([jnp.dot(xrb[:, hh * HD:(hh + 1) * HD], wx_ref[hh], preferred_element_type=f32)
                          for hh in range(NHEAD)], axis=1)
    ra = _sigmoid(pa + ba)
    ii = _sigmoid(px + bx)
    sp = _softplus(-lam)
    log_a = -8.0 * ra * sp
    a = jnp.exp(log_a)
    sq = jnp.sqrt(_one_minus_square(a, log_a))
    return ra, ii, a, sq, sp


LT = D // 128
UNR = 13


def _to_lane_tiles(ref, value):
    for lt in range(LT):
        ref[lt] = value[:, lt * 128:(lt + 1) * 128]


def _from_lane_tiles(ref):
    return jnp.concatenate([ref[lt] for lt in range(LT)], axis=1)


def _chain_scan(mult_sc, val_sc, start, reverse):
    ng = TM // 8

    def lanes(lt):
        return slice(lt * 128, (lt + 1) * 128)

    def chain(gi, carry):
        v_prev, p_prev = carry
        rows = pl.ds(ng - 1 - gi if reverse else gi, 8, stride=ng)
        v_new, p_new = [], []
        for lt in range(LT):
            mm = mult_sc.at[lt][rows, :]
            vv = mm * v_prev[:, lanes(lt)] + val_sc.at[lt][rows, :]
            pp = mm * p_prev[:, lanes(lt)]
            val_sc.at[lt][rows, :] = vv
            mult_sc.at[lt][rows, :] = pp
            v_new.append(vv)
            p_new.append(pp)
        return jnp.concatenate(v_new, axis=1), jnp.concatenate(p_new, axis=1)

    v_end, p_end = lax.fori_loop(0, ng, chain, (jnp.zeros((8, D), f32), jnp.ones((8, D), f32)))
    state, entries = start, [None] * 8
    for r in (reversed(range(8)) if reverse else range(8)):
        entries[r] = state
        state = v_end[r:r + 1, :] + p_end[r:r + 1, :] * state
    entry8 = jnp.concatenate(entries, axis=0)

    def add_entry(gi, carry):
        rows = pl.ds(gi, 8, stride=ng)
        for lt in range(LT):
            val_sc.at[lt][rows, :] = val_sc.at[lt][rows, :] + mult_sc.at[lt][rows, :] * entry8[:, lanes(lt)]
        return carry

    lax.fori_loop(0, ng, add_entry, 0)
    return state


def _strided_conv(w_ref, src_sc, out_sc, base, shifts, bias_ref=None):
    ng = TM // 8
    for lt in range(LT):
        lanes = slice(lt * 128, (lt + 1) * 128)
        taps = [jnp.broadcast_to(w_ref[k:k + 1, lanes], (8, 128)) for k in range(len(shifts))]
        init = (jnp.zeros((8, 128), f32) if bias_ref is None
                else jnp.broadcast_to(bias_ref[:, lanes], (8, 128)))

        def step(gb, carry, lt=lt, taps=taps, init=init):
            accs = [init] * UNR
            for k, shift in enumerate(shifts):
                for u in range(UNR):
                    rows = pl.ds(base + gb * UNR + u + shift, 8, stride=ng)
                    accs[u] = accs[u] + taps[k] * src_sc.at[lt][rows, :]
            for u in range(UNR):
                out_sc.at[lt][pl.ds(gb * UNR + u, 8, stride=ng), :] = accs[u]
            return carry

        lax.fori_loop(0, ng // UNR, step, 0)


def _strided_corr(a_sc, src_sc, base, shifts):
    ng = TM // 8
    per_tile = []
    for lt in range(LT):
        def step(gb, accs, lt=lt):
            accs = list(accs)
            for u in range(UNR):
                g = gb * UNR + u
                a_g = a_sc.at[lt][pl.ds(g, 8, stride=ng), :]
                for k, shift in enumerate(shifts):
                    accs[k] = accs[k] + a_g * src_sc.at[lt][pl.ds(base + g + shift, 8, stride=ng), :]
            return tuple(accs)

        per_tile.append(lax.fori_loop(0, ng // UNR, step, tuple(jnp.zeros((8, 128), f32) for _ in shifts)))
    return [jnp.concatenate([per_tile[lt][k] for lt in range(LT)], axis=1) for k in range(len(shifts))]


def _rnn_fwd(proj, cw, cb, wa, ba, wx, bx, lam):
    t = proj.shape[0]

    def body(x_ref, y_ref, cw_ref, cb_ref, wa_ref, ba_ref, wx_ref, bx_ref, lam_ref,
             xr_ref, hr_ref, z_ref, gates_ref, xext_sc, carry_sc, a_sc, h_sc):
        i = pl.program_id(0)

        @pl.when(i == 0)
        def _():
            xext_sc[0:8, :] = jnp.zeros((8, D), f32)
            carry_sc[...] = jnp.zeros_like(carry_sc)

        x = x_ref[...]
        xext_sc[8:8 + TM, :] = x
        xe = xext_sc[...]
        xr = cb_ref[...] + cw_ref[KC4 - 1:KC4, :] * x
        for k in range(KC4 - 1):
            xr = xr + cw_ref[k:k + 1, :] * pltpu.roll(xe, KC4 - 1 - k, 0)[8:8 + TM]
        xext_sc[0:8, :] = x[TM - 8:TM]

        ra, ii, a, sq, _ = _block_gates(xr, wa_ref, ba_ref[...], wx_ref, bx_ref[...], lam_ref[...])
        for slot, val in enumerate((ra, ii, a, sq)):
            gates_ref[slot] = val
        _to_lane_tiles(a_sc, a)
        _to_lane_tiles(h_sc, sq * ii * xr)
        carry_sc[...] = _chain_scan(a_sc, h_sc, carry_sc[...], reverse=False)
        hr = _from_lane_tiles(h_sc)
        gel, _ = _gelu_and_grad(y_ref[...])
        xr_ref[...] = xr
        hr_ref[...] = hr
        z_ref[...] = (hr * gel).astype(bf16)

    vec = pl.BlockSpec((1, D), lambda i: (0, 0))
    return pl.pallas_call(
        body, name="rnn_fwd", grid=(t // TM,),
        in_specs=[
            pl.BlockSpec((TM, D), lambda i: (i, 0)),
            pl.BlockSpec((TM, D), lambda i: (i, 1)),
            pl.BlockSpec((KC4, D), lambda i: (0, 0)),
            vec,
            pl.BlockSpec((NHEAD, HD, HD), lambda i: (0, 0, 0)),
            vec,
            pl.BlockSpec((NHEAD, HD, HD), lambda i: (0, 0, 0)),
            vec, vec,
        ],
        out_specs=[pl.BlockSpec((TM, D), lambda i: (i, 0))] * 3 + [pl.BlockSpec((4, TM, D), lambda i: (0, i, 0))],
        out_shape=[jax.ShapeDtypeStruct((t, D), f32), jax.ShapeDtypeStruct((t, D), f32),
                   jax.ShapeDtypeStruct((t, D), bf16), jax.ShapeDtypeStruct((4, t, D), f32)],
        scratch_shapes=[pltpu.VMEM((TM + 8, D), f32), pltpu.VMEM((1, D), f32),
                        pltpu.VMEM((LT, TM, 128), f32), pltpu.VMEM((LT, TM, 128), f32)],
        compiler_params=_cp(1),
    )(proj, proj, cw, cb, wa, ba, wx, bx, lam)


def _ln_stats(vc):
    mu = jnp.mean(vc, axis=-1, keepdims=True)
    xc = vc - mu
    rstd = lax.rsqrt(jnp.mean(xc * xc, axis=-1, keepdims=True) + EPS)
    return xc * rstd, rstd


def _conv_fwd(proj, w31, b31, ln_g, ln_b, after=None):
    t = proj.shape[0]

    def body(gv_ref, gg_ref, w_ref, b_ref, lg_ref, lb_ref, vc_ref, s_ref, vext_sc, out_sc):
        i = pl.program_id(0)

        @pl.when(i == 0)
        def _():
            vext_sc[:, 0:HALO, :] = jnp.zeros((LT, HALO, 128), f32)

        v = gv_ref[...] * _sigmoid(gg_ref[...])
        for lt in range(LT):
            vext_sc[lt, HALO:HALO + TM, :] = v[:, lt * 128:(lt + 1) * 128]

        _strided_conv(w_ref, vext_sc, out_sc, HALO, [k - (KC31 - 1) for k in range(KC31)], b_ref)
        for lt in range(LT):
            vext_sc[lt, 0:HALO, :] = v[TM - HALO:TM, lt * 128:(lt + 1) * 128]
        acc = _from_lane_tiles(out_sc)
        xhat, _ = _ln_stats(acc)
        ln = xhat * lg_ref[...] + lb_ref[...]
        vc_ref[...] = acc
        s_ref[...] = (ln * _sigmoid(ln)).astype(bf16)

    vec = pl.BlockSpec((1, D), lambda i: (0, 0))
    body, in_specs, args = _ordered(
        body,
        [pl.BlockSpec((TM, D), lambda i: (i, 2)),
         pl.BlockSpec((TM, D), lambda i: (i, 3)),
         pl.BlockSpec((KC31, D), lambda i: (0, 0)),
         vec, vec, vec],
        (proj, proj, w31, b31, ln_g, ln_b), after)
    return pl.pallas_call(
        body, name="conv_fwd", grid=(t // TM,),
        in_specs=in_specs,
        out_specs=[pl.BlockSpec((TM, D), lambda i: (i, 0))] * 2,
        out_shape=[jax.ShapeDtypeStruct((t, D), f32), jax.ShapeDtypeStruct((t, D), bf16)],
        scratch_shapes=[pltpu.VMEM((LT, TM + HALO, 128), f32), pltpu.VMEM((LT, TM, 128), f32)],
        compiler_params=_cp(1),
    )(*args)


def _merge_fwd(h, z, s, proj, wrp, wcp, bcp, wout):
    t = h.shape[0]

    def body(h_ref, z_ref, s_ref, ga_ref, gb_ref, wrp_ref, wcp_ref, bcp_ref, wout_ref, ho_ref):
        ya = jnp.dot(z_ref[...], wrp_ref[...], preferred_element_type=f32)
        yb = jnp.dot(s_ref[...], wcp_ref[...], preferred_element_type=f32) + bcp_ref[...]
        merged = _sigmoid(ga_ref[...]) * ya + _sigmoid(gb_ref[...]) * yb
        ho_ref[...] = h_ref[...] + jnp.dot(merged.astype(bf16), wout_ref[...], preferred_element_type=f32)

    row = pl.BlockSpec((TM, D), lambda i: (i, 0))
    wsq = pl.BlockSpec((D, D), lambda i: (0, 0))
    return pl.pallas_call(
        body, name="merge_fwd", grid=(t // TM,),
        in_specs=[row, row, row,
                  pl.BlockSpec((TM, D), lambda i: (i, 4)),
                  pl.BlockSpec((TM, D), lambda i: (i, 5)),
                  wsq, wsq, pl.BlockSpec((1, D), lambda i: (0, 0)), wsq],
        out_specs=row,
        out_shape=jax.ShapeDtypeStruct((t, D), f32),
        compiler_params=_cp(1),
    )(h, z, s, proj, proj, wrp, wcp, bcp, wout)


def _ffn_bwd(dh, h, g, gate, up, wgu, wd, name, after=None):
    t = h.shape[0]
    nj = 2

    def body(dh_ref, h_ref, g_ref, gate_ref, up_ref, wg_ref, wd_ref,
             dhi_ref, dgate_ref, dup_ref, a_ref, df_ref, dg_ref, dfb_sc, dn_sc):
        i = pl.program_id(0)
        j = pl.program_id(1)

        @pl.when(jnp.logical_and(i == 0, j == 0))
        def _():
            dg_ref[...] = jnp.zeros_like(dg_ref)

        @pl.when(j == 0)
        def _():
            dfb = (0.5 * dh_ref[...]).astype(bf16)
            dfb_sc[...] = dfb
            df_ref[...] = dfb
            dn_sc[...] = jnp.zeros_like(dn_sc)

        dfb = dfb_sc[...]
        for off, width in FSUB:
            cols = slice(off, off + width)
            da = _nt_dot(dfb, wd_ref[j, cols, :])
            gt = gate_ref[:, cols].astype(f32)
            uu = up_ref[:, cols].astype(f32)
            sg = _sigmoid(gt)
            silu = gt * sg
            a_ref[:, cols] = (silu * uu).astype(bf16)
            dgate_ref[:, cols] = (da * uu * (sg * (1.0 + gt * (1.0 - sg)))).astype(bf16)
            dup_ref[:, cols] = (da * silu).astype(bf16)
        dn_sc[...] += _nt_dot(dgate_ref[...], wg_ref[j]) + _nt_dot(dup_ref[...], wg_ref[2 + j])

        @pl.when(j == nj - 1)
        def _():
            dhin, dg = _rms_bwd(dn_sc[...], h_ref[...], g_ref[...])
            dhi_ref[...] = dh_ref[...] + dhin
            dg_ref[...] += dg

    rowd = pl.BlockSpec((TM, D), lambda i, j: (i, 0))
    rowf = pl.BlockSpec((TM, FS), lambda i, j: (i, j))
    vec = pl.BlockSpec((1, D), lambda i, j: (0, 0))
    body, in_specs, args = _ordered(
        body,
        [rowd, rowd, vec, rowf, rowf,
         RESIDENT, RESIDENT],
        (dh, h, g, gate, up, wgu, wd.reshape(nj, FS, D)), after)
    return pl.pallas_call(
        body, name=name, grid=(t // TM, nj),
        in_specs=in_specs,
        out_specs=[rowd, rowf, rowf, rowf, rowd, vec],
        out_shape=[jax.ShapeDtypeStruct((t, D), f32), jax.ShapeDtypeStruct((t, F), bf16),
                   jax.ShapeDtypeStruct((t, F), bf16), jax.ShapeDtypeStruct((t, F), bf16),
                   jax.ShapeDtypeStruct((t, D), bf16), jax.ShapeDtypeStruct((1, D), f32)],
        scratch_shapes=[pltpu.VMEM((TM, D), bf16), pltpu.VMEM((TM, D), f32)],
        compiler_params=_cp(2),
    )(*args)


def _big_tile(t):
    return max(k * TM for k in range(1, 6) if t % (k * TM) == 0)


ANY_SPEC = pl.BlockSpec(memory_space=pl.ANY)


def _tn_matmul(a, b, tk, tn, out_shape, out_block, out_map, name, base=None, after=None):
    t, kk = a.shape
    _, nn = b.shape
    tmm = _big_tile(t)
    nm = t // tmm

    def body(a_ref, b_ref, o_ref, acc_sc):
        m = pl.program_id(2)

        @pl.when(m == 0)
        def _():
            acc_sc[...] = jnp.zeros_like(acc_sc)

        acc_sc[...] += _tn_dot(a_ref[...], b_ref[...])

        @pl.when(m == nm - 1)
        def _():
            o_ref[...] = acc_sc[...].astype(o_ref.dtype)

    in_specs = [pl.BlockSpec((tmm, tk), lambda k, n, m: (m, k)),
                pl.BlockSpec((tmm, tn), lambda k, n, m: (m, n))]
    args, aliases = (a, b), {}
    if base is not None:
        body = (lambda inner: lambda a_ref, b_ref, base_ref, o_ref, acc_sc: inner(a_ref, b_ref, o_ref, acc_sc))(body)
        in_specs, args, aliases = in_specs + [ANY_SPEC], (a, b, base), {2: 0}
    if after is not None:
        body, in_specs, args = _ordered(body, in_specs, args, after)
        aliases = {k + _n_after(after): v for k, v in aliases.items()}
    return pl.pallas_call(
        body, name=name, grid=(kk // tk, nn // tn, nm),
        in_specs=in_specs,
        out_specs=pl.BlockSpec(out_block, out_map),
        out_shape=jax.ShapeDtypeStruct(out_shape, bf16),
        scratch_shapes=[pltpu.VMEM((tk, tn), f32)],
        input_output_aliases=aliases,
        compiler_params=_cp(3),
    )(*args)


def _merge_bwd(dh, z, s, proj, wrp, wcp, bcp, wout, after=None):
    t = dh.shape[0]

    def body(dh_ref, z_ref, s_ref, ga_ref, gb_ref, wrp_ref, wcp_ref, bcp_ref, wout_ref,
             dz_ref, ds_ref, dgab_ref, dhb_ref, mg_ref, dya_ref, dyb_ref, dbcp_ref):
        i = pl.program_id(0)

        @pl.when(i == 0)
        def _():
            dbcp_ref[...] = jnp.zeros_like(dbcp_ref)

        dhb = dh_ref[...].astype(bf16)
        dhb_ref[...] = dhb
        dmg = _nt_dot(dhb, wout_ref[...])
        ya = jnp.dot(z_ref[...], wrp_ref[...], preferred_element_type=f32)
        yb = jnp.dot(s_ref[...], wcp_ref[...], preferred_element_type=f32) + bcp_ref[...]
        sa = _sigmoid(ga_ref[...])
        sb = _sigmoid(gb_ref[...])
        mg_ref[...] = (sa * ya + sb * yb).astype(bf16)
        dgab_ref[:, 0:D] = (dmg * ya * sa * (1.0 - sa)).astype(bf16)
        dgab_ref[:, D:2 * D] = (dmg * yb * sb * (1.0 - sb)).astype(bf16)
        dya = dmg * sa
        dyb = dmg * sb
        dbcp_ref[...] += jnp.sum(dyb, axis=0, keepdims=True)
        dyab = dya.astype(bf16)
        dybb = dyb.astype(bf16)
        dya_ref[...] = dyab
        dyb_ref[...] = dybb
        dz_ref[...] = _nt_dot(dyab, wrp_ref[...])
        ds_ref[...] = _nt_dot(dybb, wcp_ref[...])

    row = pl.BlockSpec((TM, D), lambda i: (i, 0))
    wsq = pl.BlockSpec((D, D), lambda i: (0, 0))
    vec = pl.BlockSpec((1, D), lambda i: (0, 0))
    rowb = jax.ShapeDtypeStruct((t, D), bf16)
    body, in_specs, args = _ordered(
        body,
        [row, row, row,
         pl.BlockSpec((TM, D), lambda i: (i, 4)),
         pl.BlockSpec((TM, D), lambda i: (i, 5)),
         wsq, wsq, vec, wsq],
        (dh, z, s, proj, proj, wrp, wcp, bcp, wout), after)
    return pl.pallas_call(
        body, name="merge_bwd", grid=(t // TM,),
        in_specs=in_specs,
        out_specs=[row, row,
                   pl.BlockSpec((TM, 2 * D), lambda i: (i, 2)),
                   row, row, row, row, vec],
        out_shape=[jax.ShapeDtypeStruct((t, D), f32), jax.ShapeDtypeStruct((t, D), f32),
                   jax.ShapeDtypeStruct((t, NIN), bf16),
                   rowb, rowb, rowb, rowb, jax.ShapeDtypeStruct((1, D), f32)],
        compiler_params=_cp(1),
    )(*args)


def _conv_bwd(ds, vc, proj, dproj, w31, ln_g, ln_b, after=None):
    t = ds.shape[0]
    nt = t // TM
    hb = TM // HALO

    def body(ds_ref, vc_ref, gv_ref, gg_ref, gvp_ref, ggp_ref, dpin_ref, w_ref, lg_ref, lb_ref,
             dgvg_ref, dw_ref, db_ref, dlg_ref, dlb_ref, dext_sc, vext_sc, out_sc, dwacc_sc, small_sc):
        del dpin_ref
        i = pl.program_id(0)
        tile = nt - 1 - i

        @pl.when(i == 0)
        def _():
            dext_sc[:, TM:TM + HALO, :] = jnp.zeros((LT, HALO, 128), f32)
            dwacc_sc[...] = jnp.zeros_like(dwacc_sc)
            small_sc[...] = jnp.zeros_like(small_sc)

        lg = lg_ref[...]
        lb = lb_ref[...]

        xhat, rstd = _ln_stats(vc_ref[...])
        ln = xhat * lg + lb
        sg = _sigmoid(ln)
        dln = ds_ref[...] * (sg * (1.0 + ln * (1.0 - sg)))
        dxh = dln * lg
        dvc = rstd * (dxh - jnp.mean(dxh, axis=-1, keepdims=True)
                      - xhat * jnp.mean(dxh * xhat, axis=-1, keepdims=True))
        small_sc[0] += jnp.sum((dln * xhat).reshape(TM // 8, 8, D), axis=0)
        small_sc[1] += jnp.sum(dln.reshape(TM // 8, 8, D), axis=0)
        small_sc[2] += jnp.sum(dvc.reshape(TM // 8, 8, D), axis=0)
        sgg = _sigmoid(gg_ref[...])
        v = gv_ref[...] * sgg
        vprev = jnp.where(tile > 0, gvp_ref[...] * _sigmoid(ggp_ref[...]), 0.0)
        for lt in range(LT):
            lanes = slice(lt * 128, (lt + 1) * 128)
            dext_sc[lt, 0:TM, :] = dvc[:, lanes]
            vext_sc[lt, HALO:HALO + TM, :] = v[:, lanes]
            vext_sc[lt, 0:HALO, :] = vprev[:, lanes]

        _strided_conv(w_ref, dext_sc, out_sc, 0, [KC31 - 1 - k for k in range(KC31)])
        dv = _from_lane_tiles(out_sc)
        dgvg_ref[:, 0:D] = (dv * sgg).astype(bf16)
        dgvg_ref[:, D:2 * D] = (dv * gv_ref[...] * sgg * (1.0 - sgg)).astype(bf16)

        for k, part in enumerate(_strided_corr(dext_sc, vext_sc, HALO, [k - (KC31 - 1) for k in range(KC31)])):
            dwacc_sc[k] += part
        for lt in range(LT):
            dext_sc[lt, TM:TM + HALO, :] = dext_sc[lt, 0:HALO, :]

        @pl.when(i == nt - 1)
        def _():
            for k in range(KC31):
                dw_ref[k:k + 1, :] = jnp.sum(dwacc_sc[k], axis=0, keepdims=True)
            dlg_ref[...] = jnp.sum(small_sc[0], axis=0, keepdims=True)
            dlb_ref[...] = jnp.sum(small_sc[1], axis=0, keepdims=True)
            db_ref[...] = jnp.sum(small_sc[2], axis=0, keepdims=True)

    rev = lambda i: (nt - 1 - i, 0)
    vec = pl.BlockSpec((1, D), lambda i: (0, 0))
    halo_row = lambda i: jnp.maximum((nt - 1 - i) * hb - 1, 0)
    body, in_specs, args = _ordered(
        body,
        [pl.BlockSpec((TM, D), rev),
         pl.BlockSpec((TM, D), rev),
         pl.BlockSpec((TM, D), lambda i: (nt - 1 - i, 2)),
         pl.BlockSpec((TM, D), lambda i: (nt - 1 - i, 3)),
         pl.BlockSpec((HALO, D), lambda i: (halo_row(i), 2)),
         pl.BlockSpec((HALO, D), lambda i: (halo_row(i), 3)),
         pl.BlockSpec(memory_space=pl.ANY),
         pl.BlockSpec((KC31, D), lambda i: (0, 0)),
         vec, vec],
        (ds, vc, proj, proj, proj, proj, dproj, w31, ln_g, ln_b), after)
    return pl.pallas_call(
        body, name="conv_bwd", grid=(nt,),
        in_specs=in_specs,
        out_specs=[
            pl.BlockSpec((TM, 2 * D), lambda i: (nt - 1 - i, 1)),
            pl.BlockSpec((KC31, D), lambda i: (0, 0)),
            vec, vec, vec,
        ],
        out_shape=[jax.ShapeDtypeStruct((t, NIN), bf16),
                   jax.ShapeDtypeStruct((KC31, D), f32),
                   jax.ShapeDtypeStruct((1, D), f32), jax.ShapeDtypeStruct((1, D), f32),
                   jax.ShapeDtypeStruct((1, D), f32)],
        scratch_shapes=[pltpu.VMEM((LT, TM + HALO, 128), f32), pltpu.VMEM((LT, TM + HALO, 128), f32),
                        pltpu.VMEM((LT, TM, 128), f32), pltpu.VMEM((KC31, 8, D), f32),
                        pltpu.VMEM((3, 8, D), f32)],
        input_output_aliases={6 + _n_after(after): 0},
        compiler_params=_cp(1),
    )(*args)


def _rnn_bwd(dz, xr, hr, gates, proj, dproj, cw, wa, wx, lam):
    t = dz.shape[0]
    nt = t // TM
    ng = TM // 8
    hq = HD // NCHIP

    def body(dz_ref, xr_ref, hr_ref, hrp_ref, x_ref, xp_ref, y_ref, dpin_ref,
             cw_ref, wa_ref, gates_ref, wx_ref, lam_ref,
             dxy_ref, dwa_ref, dwx_ref, dcw_ref, dcb_ref, dba_ref, dbx_ref, dlam_ref,
             anext_sc, gcarry_sc, dext_sc, xext_sc, m_sc, g_sc, dwa_sc, dwx_sc, dsp_sc):
        del dpin_ref
        i = pl.program_id(0)
        tile = nt - 1 - i

        @pl.when(i == 0)
        def _():
            anext_sc[...] = jnp.zeros_like(anext_sc)
            gcarry_sc[...] = jnp.zeros_like(gcarry_sc)
            dext_sc[TM:TM + 8, :] = jnp.zeros((8, D), f32)
            dwa_sc[...] = jnp.zeros_like(dwa_sc)
            dwx_sc[...] = jnp.zeros_like(dwx_sc)
            dsp_sc[...] = jnp.zeros_like(dsp_sc)
            dcw_ref[...] = jnp.zeros_like(dcw_ref)
            dcb_ref[...] = jnp.zeros_like(dcb_ref)
            dba_ref[...] = jnp.zeros_like(dba_ref)
            dbx_ref[...] = jnp.zeros_like(dbx_ref)

        xr = xr_ref[...]
        hr = hr_ref[...]
        dz = dz_ref[...]
        gel, dgel = _gelu_and_grad(y_ref[...])
        dxy_ref[:, D:2 * D] = (dz * hr * dgel).astype(bf16)
        ra, ii, a, sq = gates_ref[0], gates_ref[1], gates_ref[2], gates_ref[3]
        sp = _softplus(-lam_ref[...])

        row = _row_ids((TM, D))
        _to_lane_tiles(m_sc, jnp.where(row == TM - 1, anext_sc[...], pltpu.roll(a, TM - 1, 0)))
        anext_sc[...] = a[0:1, :]
        _to_lane_tiles(g_sc, dz * gel)
        gcarry_sc[...] = _chain_scan(m_sc, g_sc, gcarry_sc[...], reverse=True)
        gg = _from_lane_tiles(g_sc)

        hlast = jnp.where(tile > 0, hrp_ref[7:8, :], 0.0)
        hprev = jnp.where(row == 0, hlast, pltpu.roll(hr, 1, 0))
        d_a = gg * hprev
        dsq = gg * ii * xr
        dii = gg * sq * xr
        dxr = gg * sq * ii
        dlog = d_a * a - dsq * (a * a / sq)
        dsp_sc[...] += jnp.sum(dlog * (-8.0 * ra), axis=0, keepdims=True)
        dpa = dlog * (-8.0 * sp) * ra * (1.0 - ra)
        dpx = dii * ii * (1.0 - ii)
        dba_ref[...] += jnp.sum(dpa, axis=0, keepdims=True)
        dbx_ref[...] += jnp.sum(dpx, axis=0, keepdims=True)
        dpab = dpa.astype(bf16)
        dpxb = dpx.astype(bf16)
        xrb = xr.astype(bf16)
        back = []
        for hh in range(NHEAD):
            cols = slice(hh * HD, (hh + 1) * HD)
            back.append(_nt_dot(dpab[:, cols], wa_ref[hh]) + _nt_dot(dpxb[:, cols], wx_ref[hh]))
            dwa_sc[hh] += _tn_dot(xrb[:, cols], dpab[:, cols])
            dwx_sc[hh] += _tn_dot(xrb[:, cols], dpxb[:, cols])
        dxr = dxr + jnp.concatenate(back, axis=1)

        dext_sc[0:TM, :] = dxr
        de = dext_sc[...]
        dx = cw_ref[KC4 - 1:KC4, :] * dxr
        for k in range(KC4 - 1):
            dx = dx + cw_ref[k:k + 1, :] * pltpu.roll(de, TM + 8 - (KC4 - 1 - k), 0)[0:TM]
        dext_sc[TM:TM + 8, :] = dxr[0:8]
        dxy_ref[:, 0:D] = dx.astype(bf16)

        x = x_ref[...]
        xext_sc[0:8, :] = jnp.where(tile > 0, xp_ref[...], 0.0)
        xext_sc[8:8 + TM, :] = x
        xe = xext_sc[...]
        dcw_ref[KC4 - 1:KC4, :] += jnp.sum(dxr * x, axis=0, keepdims=True)
        for k in range(KC4 - 1):
            xs = pltpu.roll(xe, KC4 - 1 - k, 0)[8:8 + TM]
            dcw_ref[k:k + 1, :] += jnp.sum(dxr * xs, axis=0, keepdims=True)
        dcb_ref[...] += jnp.sum(dxr, axis=0, keepdims=True)

        @pl.when(i == nt - 1)
        def _():
            for hh in range(NHEAD):
                for qc in range(NCHIP):
                    dwa_ref[qc, hh] = dwa_sc[hh, qc * hq:(qc + 1) * hq, :].astype(bf16)
                    dwx_ref[qc, hh] = dwx_sc[hh, qc * hq:(qc + 1) * hq, :].astype(bf16)
            dlam_ref[...] = -dsp_sc[...] * _sigmoid(-lam_ref[...])

    rev = lambda i: (nt - 1 - i, 0)
    vec = pl.BlockSpec((1, D), lambda i: (0, 0))
    prev8 = lambda i: jnp.maximum((nt - 1 - i) * ng - 1, 0)
    wblk = pl.BlockSpec((NHEAD, HD, HD), lambda i: (0, 0, 0))
    gblk = pl.BlockSpec((NCHIP, NHEAD, hq, HD), lambda i: (0, 0, 0, 0))
    return pl.pallas_call(
        body, name="rnn_bwd", grid=(nt,),
        in_specs=[
            pl.BlockSpec((TM, D), rev),
            pl.BlockSpec((TM, D), rev),
            pl.BlockSpec((TM, D), rev),
            pl.BlockSpec((8, D), lambda i: (prev8(i), 0)),
            pl.BlockSpec((TM, D), lambda i: (nt - 1 - i, 0)),
            pl.BlockSpec((8, D), lambda i: (prev8(i), 0)),
            pl.BlockSpec((TM, D), lambda i: (nt - 1 - i, 1)),
            pl.BlockSpec(memory_space=pl.ANY),
            pl.BlockSpec((KC4, D), lambda i: (0, 0)),
            wblk, pl.BlockSpec((4, TM, D), lambda i: (0, nt - 1 - i, 0)), wblk, vec,
        ],
        out_specs=[
            pl.BlockSpec((TM, 2 * D), lambda i: (nt - 1 - i, 0)),
            gblk, gblk,
            pl.BlockSpec((KC4, D), lambda i: (0, 0)),
            vec, vec, vec, vec,
        ],
        out_shape=[jax.ShapeDtypeStruct((t, NIN), bf16),
                   jax.ShapeDtypeStruct((NCHIP, NHEAD, hq, HD), bf16),
                   jax.ShapeDtypeStruct((NCHIP, NHEAD, hq, HD), bf16),
                   jax.ShapeDtypeStruct((KC4, D), f32),
                   jax.ShapeDtypeStruct((1, D), f32), jax.ShapeDtypeStruct((1, D), f32),
                   jax.ShapeDtypeStruct((1, D), f32), jax.ShapeDtypeStruct((1, D), f32)],
        scratch_shapes=[pltpu.VMEM((1, D), f32), pltpu.VMEM((1, D), f32),
                        pltpu.VMEM((TM + 8, D), f32), pltpu.VMEM((TM + 8, D), f32),
                        pltpu.VMEM((LT, TM, 128), f32), pltpu.VMEM((LT, TM, 128), f32),
                        pltpu.VMEM((NHEAD, HD, HD), f32), pltpu.VMEM((NHEAD, HD, HD), f32),
                        pltpu.VMEM((1, D), f32)],
        input_output_aliases={7: 0},
        compiler_params=_cp(1),
    )(dz, xr, hr, hr, proj, proj, proj, dproj, cw, wa, gates, wx, lam)


def _inproj_bwd(dproj, dh, h, g, win, after=None):
    t = h.shape[0]
    tn = NIN // NCHIP
    nj = NIN // tn

    def body(dp_ref, dh_ref, h_ref, g_ref, w_ref, dhi_ref, dg_ref, db_ref, dn_sc):
        i = pl.program_id(0)
        j = pl.program_id(1)

        @pl.when(jnp.logical_and(i == 0, j == 0))
        def _():
            dg_ref[...] = jnp.zeros_like(dg_ref)
            db_ref[...] = jnp.zeros_like(db_ref)

        @pl.when(j == 0)
        def _():
            dn_sc[...] = jnp.zeros_like(dn_sc)

        dp = dp_ref[...]
        dn_sc[...] += _nt_dot(dp, w_ref[j])
        db_ref[j] += jnp.sum(dp.astype(f32), axis=0, keepdims=True)

        @pl.when(j == nj - 1)
        def _():
            dhin, dg = _rms_bwd(dn_sc[...], h_ref[...], g_ref[...])
            dhi_ref[...] = dh_ref[...] + dhin
            dg_ref[...] += dg

    rowd = pl.BlockSpec((TM, D), lambda i, j: (i, 0))
    vec = pl.BlockSpec((1, D), lambda i, j: (0, 0))
    body, in_specs, args = _ordered(
        body,
        [pl.BlockSpec((TM, tn), lambda i, j: (i, j)), rowd, rowd, vec,
         RESIDENT],
        (dproj, dh, h, g, win), after)
    return pl.pallas_call(
        body, name="inproj_bwd", grid=(t // TM, nj),
        in_specs=in_specs,
        out_specs=[rowd, vec, pl.BlockSpec((nj, 1, tn), lambda i, j: (0, 0, 0))],
        out_shape=[jax.ShapeDtypeStruct((t, D), f32), jax.ShapeDtypeStruct((1, D), f32),
                   jax.ShapeDtypeStruct((nj, 1, tn), f32)],
        scratch_shapes=[pltpu.VMEM((TM, D), f32)],
        compiler_params=_cp(2),
    )(*args)


def _ffn_gu_grad(n, dgate, dup, tag, after=None):
    half = _tn_matmul(n, dgate, D, FS, (NCHIP, D, FS), (None, D, FS), lambda k, nn, m: (nn, 0, 0),
                      tag + "_dwg", after=after)
    return _tn_matmul(n, dup, D, FS, (NCHIP, D, FS), (None, D, FS), lambda k, nn, m: (2 + nn, 0, 0),
                      tag + "_dwu", base=half)


def _ffn_down_grad(a, df, tag, after=None):
    return _tn_matmul(a, df, FS, D, (F, D), (FS, D), lambda k, nn, m: (k, 0), tag + "_dwd", after=after)


def _square_grad(a, b, name):
    return _tn_matmul(a, b, D, D, (D, D), (D, D), lambda k, nn, m: (0, 0), name)


ANY = pl.BlockSpec(memory_space=pl.ANY)


def _place():
    x, y, c = lax.axis_index("x"), lax.axis_index("y"), lax.axis_index("c")
    chips = [(1 - x, y), (x, 1 - y), (1 - x, 1 - y)]
    return x, y, c, chips


def _chip_id(chip):
    return 2 * chip[0] + chip[1]


def _cast_into_slot(w2d, qc, dtype, name, after=None):
    r, cc = w2d.shape
    hr = r // 2

    def body(qc_ref, *refs):
        del qc_ref
        w_ref, o_ref = refs[-2:]
        o_ref[...] = w_ref[...].astype(dtype)

    in_specs, args = [pl.BlockSpec((hr, cc), lambda h, qc_ref: (h, 0))], (w2d,)
    if after is not None:
        in_specs, args = [ANY_SPEC] + in_specs, (after,) + args
    return pl.pallas_call(
        body, name=name,
        grid_spec=pltpu.PrefetchScalarGridSpec(
            num_scalar_prefetch=1, grid=(2,),
            in_specs=in_specs,
            out_specs=pl.BlockSpec((None, None, hr, cc), lambda h, qc_ref: (qc_ref[0], h, 0, 0))),
        out_shape=jax.ShapeDtypeStruct((NCHIP, 2, hr, cc), dtype),
        compiler_params=_cp(1),
    )(qc, *args)


def _place_pack(pack, qc):
    def body(qc_ref, p_ref, o_ref):
        del qc_ref
        o_ref[...] = p_ref[...]

    return pl.pallas_call(
        body, name="place_pack",
        grid_spec=pltpu.PrefetchScalarGridSpec(
            num_scalar_prefetch=1, grid=(1,),
            in_specs=[pl.BlockSpec(pack.shape, lambda i, qc_ref: (0, 0))],
            out_specs=pl.BlockSpec((None,) + pack.shape, lambda i, qc_ref: (2 * qc_ref[0] + qc_ref[1], 0, 0))),
        out_shape=jax.ShapeDtypeStruct((8,) + pack.shape, pack.dtype),
        compiler_params=_cp(1),
    )(qc, pack)


def _pair_add(parts, gots, qc, name):
    n = len(parts)

    def body(qc_ref, *refs):
        s = pl.program_id(0)
        for a in range(n):
            val = (refs[a][...].astype(f32) + refs[n + a][...].astype(f32)).astype(bf16)
            refs[2 * n + a][...] = val

            @pl.when(s == qc_ref[0])
            def _(val=val, land_ref=refs[3 * n + a]):
                land_ref[...] = val

    shapes = [p.shape[2:] for p in parts]
    mine = [pl.BlockSpec((None, None) + sh, lambda s, qc_ref: (s, qc_ref[1], 0, 0)) for sh in shapes]
    block = [pl.BlockSpec((None,) + sh, lambda s, qc_ref: (s, 0, 0)) for sh in shapes]
    own = [pl.BlockSpec((None,) + sh, lambda s, qc_ref: (qc_ref[0], 0, 0)) for sh in shapes]
    outs = pl.pallas_call(
        body, name=name,
        grid_spec=pltpu.PrefetchScalarGridSpec(
            num_scalar_prefetch=1, grid=(NCHIP,), in_specs=mine + block, out_specs=block + own),
        out_shape=[jax.ShapeDtypeStruct((NCHIP,) + sh, bf16) for sh in shapes] * 2,
        compiler_params=_cp(1),
    )(qc, *parts, *gots)
    return list(outs[:n]), list(outs[n:])


def _sum_chips(gots, name):
    n = len(gots)

    def body(*refs):
        for a in range(n):
            acc = refs[a][0].astype(f32)
            for s in range(1, NCHIP):
                acc = acc + refs[a][s].astype(f32)
            refs[n + a][...] = acc

    return list(pl.pallas_call(
        body, name=name, grid=(1,),
        in_specs=[pl.BlockSpec(g.shape, lambda i: (0, 0, 0)) for g in gots],
        out_specs=[pl.BlockSpec(g.shape[1:], lambda i: (0, 0)) for g in gots],
        out_shape=[jax.ShapeDtypeStruct(g.shape[1:], f32) for g in gots],
        compiler_params=_cp(1),
    )(*gots))


def _pair_share(halves, name, after=None):
    n = len(halves)
    extra = () if after is None else (after,)

    def body(*refs):
        refs = refs[len(extra):]
        ins, outs = refs[:n], refs[n:2 * n]
        send_sems, recv_sems = refs[2 * n:]
        x, y, c, _ = _place()
        copies = []
        for a in range(n):
            cp = pltpu.make_async_remote_copy(
                src_ref=ins[a], dst_ref=outs[a], send_sem=send_sems.at[a], recv_sem=recv_sems.at[a],
                device_id=(x, y, 1 - c), device_id_type=MESH)
            cp.start()
            copies.append(cp)
        for cp in copies:
            cp.wait()

    return pl.pallas_call(
        body, name=name,
        in_specs=[ANY] * (len(extra) + n), out_specs=[ANY] * n,
        out_shape=[jax.ShapeDtypeStruct(s.shape, s.dtype) for s in halves],
        scratch_shapes=[pltpu.SemaphoreType.DMA((n,)), pltpu.SemaphoreType.DMA((n,))],
    )(*extra, *halves)


def _all_copy(buf_ref, send_ref, recv_ref, k, x, y, c, landing):
    px, py, pc = (1 - x if k & 4 else x, 1 - y if k & 2 else y, 1 - c if k & 1 else c)
    me = 4 * x + 2 * y + c
    there = 4 * px + 2 * py + pc
    return pltpu.make_async_remote_copy(
        src_ref=buf_ref.at[me], dst_ref=buf_ref.at[there if landing else me],
        send_sem=send_ref.at[k - 1], recv_sem=recv_ref.at[k - 1],
        device_id=(px, py, pc), device_id_type=MESH)


def _gather_all_start(buf, name):
    def body(in_ref, send, recv, thru, token):
        del thru
        x, y, c, _ = _place()
        for k in range(1, 8):
            _all_copy(in_ref, send, recv, k, x, y, c, False).start()
        token[...] = jnp.zeros_like(token)

    return pl.pallas_call(
        body, name=name,
        in_specs=[HBM],
        out_specs=[SEM, SEM, HBM, pl.BlockSpec(memory_space=pltpu.VMEM)],
        out_shape=[pltpu.SemaphoreType.DMA((7,)), pltpu.SemaphoreType.DMA((7,)),
                   pltpu.HBM(buf.shape, buf.dtype), jax.ShapeDtypeStruct((8, 128), f32)],
        input_output_aliases={0: 2},
        compiler_params=pltpu.CompilerParams(has_side_effects=EFFECT),
    )(_in_hbm(buf))


def _gather_all_wait(send, recv, buf, after, name):
    def body(in_ref, send_r, recv_r, after_ref, out_ref):
        del after_ref, out_ref
        x, y, c, _ = _place()
        for k in range(1, 8):
            cp = _all_copy(in_ref, send_r, recv_r, k, x, y, c, True)
            cp.wait_send()
            cp.wait_recv()

    return pl.pallas_call(
        body, name=name,
        in_specs=[HBM, SEM, SEM, ANY],
        out_specs=HBM,
        out_shape=pltpu.HBM(buf.shape, buf.dtype),
        input_output_aliases={0: 0},
        compiler_params=pltpu.CompilerParams(has_side_effects=EFFECT),
    )(buf, send, recv, after)


HBM = pl.BlockSpec(memory_space=pltpu.HBM)
SEM = pl.BlockSpec(memory_space=pltpu.SEMAPHORE)
EFFECT = pltpu.SideEffectType.DATAFLOW_SIDE_EFFECTING
N_PEER = 3


def _in_hbm(a):
    return pltpu.with_memory_space_constraint(a, pltpu.HBM)


def _gather_copy(buf_ref, send_ref, recv_ref, j, chip, q, c, landing_chip):
    return pltpu.make_async_remote_copy(
        src_ref=buf_ref.at[q, c], dst_ref=buf_ref.at[landing_chip, c],
        send_sem=send_ref.at[j], recv_sem=recv_ref.at[j],
        device_id=(chip[0], chip[1], c), device_id_type=MESH)


def _gather_start(bufs, name):
    n = len(bufs)

    def body(*refs):
        ins = refs[:n]
        send, recv = refs[n:2 * n], refs[2 * n:3 * n]
        token = refs[4 * n]
        x, y, c, chips = _place()
        q = 2 * x + y
        for a in range(n):
            for j, chip in enumerate(chips):
                _gather_copy(ins[a], send[a], recv[a], j, chip, q, c, q).start()
        token[...] = jnp.zeros_like(token)

    sems = [pltpu.SemaphoreType.DMA((N_PEER,))] * (2 * n)
    outs = pl.pallas_call(
        body, name=name,
        in_specs=[HBM] * n,
        out_specs=[SEM] * (2 * n) + [HBM] * n + [pl.BlockSpec(memory_space=pltpu.VMEM)],
        out_shape=sems + [pltpu.HBM(b.shape, b.dtype) for b in bufs] + [jax.ShapeDtypeStruct((8, 128), f32)],
        input_output_aliases={a: 2 * n + a for a in range(n)},
        compiler_params=pltpu.CompilerParams(has_side_effects=EFFECT),
    )(*[_in_hbm(b) for b in bufs])
    return list(outs[:n]), list(outs[n:2 * n]), list(outs[2 * n:3 * n]), outs[3 * n]


def _gather_wait(send, recv, bufs, after, name):
    n = len(bufs)

    def body(*refs):
        ins = refs[:n]
        send_r, recv_r = refs[n:2 * n], refs[2 * n:3 * n]
        x, y, c, chips = _place()
        q = 2 * x + y
        for a in range(n):
            for j, chip in enumerate(chips):
                cp = _gather_copy(ins[a], send_r[a], recv_r[a], j, chip, q, c, _chip_id(chip))
                cp.wait_send()
                cp.wait_recv()

    afters = after if isinstance(after, (tuple, list)) else (after,)
    outs = pl.pallas_call(
        body, name=name,
        in_specs=[HBM] * n + [SEM] * (2 * n) + [ANY] * len(afters),
        out_specs=[HBM] * n,
        out_shape=[pltpu.HBM(b.shape, b.dtype) for b in bufs],
        input_output_aliases={a: a for a in range(n)},
        compiler_params=pltpu.CompilerParams(has_side_effects=EFFECT),
    )(*bufs, *send, *recv, *afters)
    return list(outs)


def _forward_halves(bufs, name):
    n = len(bufs)

    def body(*refs):
        outs = refs[n:2 * n]
        send_sems, recv_sems = refs[2 * n:]
        x, y, c, chips = _place()
        sibling = (x, y, 1 - c)

        def remote(a, j, blk):
            return pltpu.make_async_remote_copy(src_ref=blk, dst_ref=blk, send_sem=send_sems.at[a, j],
                                                recv_sem=recv_sems.at[a, j], device_id=sibling,
                                                device_id_type=MESH)

        sent = []
        for a in range(n):
            for j, chip in enumerate(chips):
                cp = remote(a, j, outs[a].at[_chip_id(chip), c])
                cp.start()
                sent.append(cp)
        for a in range(n):
            for j, chip in enumerate(chips):
                remote(a, j, outs[a].at[_chip_id(chip), 1 - c]).wait_recv()
        for cp in sent:
            cp.wait_send()

    return pl.pallas_call(
        body, name=name,
        in_specs=[ANY] * n, out_specs=[ANY] * n,
        out_shape=[jax.ShapeDtypeStruct(s.shape, s.dtype) for s in bufs],
        scratch_shapes=[pltpu.SemaphoreType.DMA((n, N_PEER)), pltpu.SemaphoreType.DMA((n, N_PEER))],
        input_output_aliases={a: a for a in range(n)},
    )(*bufs)


def _reduce_copy(sum_ref, land_ref, send_ref, recv_ref, j, chip, q, c, landing_chip):
    return pltpu.make_async_remote_copy(
        src_ref=sum_ref.at[_chip_id(chip)], dst_ref=land_ref.at[landing_chip],
        send_sem=send_ref.at[j], recv_sem=recv_ref.at[j],
        device_id=(chip[0], chip[1], c), device_id_type=MESH)


def _reduce_start(sums, lands, name):
    n = len(sums)

    def body(*refs):
        s_in, l_in = refs[:n], refs[n:2 * n]
        send, recv = refs[2 * n:3 * n], refs[3 * n:4 * n]
        token = refs[6 * n]
        x, y, c, chips = _place()
        q = 2 * x + y
        for a in range(n):
            for j, chip in enumerate(chips):
                _reduce_copy(s_in[a], l_in[a], send[a], recv[a], j, chip, q, c, q).start()
        token[...] = jnp.zeros_like(token)

    sems = [pltpu.SemaphoreType.DMA((N_PEER,))] * (2 * n)
    outs = pl.pallas_call(
        body, name=name,
        in_specs=[HBM] * (2 * n),
        out_specs=[SEM] * (2 * n) + [HBM] * (2 * n) + [pl.BlockSpec(memory_space=pltpu.VMEM)],
        out_shape=sems + [pltpu.HBM(b.shape, b.dtype) for b in list(sums) + list(lands)]
        + [jax.ShapeDtypeStruct((8, 128), f32)],
        input_output_aliases={a: 2 * n + a for a in range(2 * n)},
        compiler_params=pltpu.CompilerParams(has_side_effects=EFFECT),
    )(*[_in_hbm(b) for b in list(sums) + list(lands)])
    return (list(outs[:n]), list(outs[n:2 * n]), list(outs[2 * n:3 * n]), list(outs[3 * n:4 * n]),
            outs[4 * n])


def _reduce_wait(send, recv, sums, lands, after, name):
    n = len(sums)

    def body(*refs):
        s_in, l_in = refs[:n], refs[n:2 * n]
        send_r, recv_r = refs[2 * n:3 * n], refs[3 * n:4 * n]
        x, y, c, chips = _place()
        q = 2 * x + y
        for a in range(n):
            for j, chip in enumerate(chips):
                cp = _reduce_copy(s_in[a], l_in[a], send_r[a], recv_r[a], j, chip, q, c, _chip_id(chip))
                cp.wait_send()
                cp.wait_recv()

    afters = after if isinstance(after, (tuple, list)) else (after,)
    outs = pl.pallas_call(
        body, name=name,
        in_specs=[HBM] * (2 * n) + [SEM] * (2 * n) + [ANY] * len(afters),
        out_specs=[HBM] * (2 * n),
        out_shape=[pltpu.HBM(b.shape, b.dtype) for b in list(sums) + list(lands)],
        input_output_aliases={a: a for a in range(2 * n)},
        compiler_params=pltpu.CompilerParams(has_side_effects=EFFECT),
    )(*sums, *lands, *send, *recv, *afters)
    return list(outs[n:])


def _sibling_copy(part_ref, land_ref, send_ref, recv_ref, x, y, c):
    return pltpu.make_async_remote_copy(
        src_ref=part_ref.at[:, 1 - c], dst_ref=land_ref, send_sem=send_ref.at[0], recv_sem=recv_ref.at[0],
        device_id=(x, y, 1 - c), device_id_type=MESH)


def _pair_exchange_start(parts, name):
    n = len(parts)
    lands = [lax.empty((NCHIP,) + p.shape[2:], p.dtype) for p in parts]

    def body(*refs):
        p_in, l_in = refs[:n], refs[n:2 * n]
        send, recv = refs[2 * n:3 * n], refs[3 * n:4 * n]
        token = refs[6 * n]
        x, y, c, _ = _place()
        for a in range(n):
            _sibling_copy(p_in[a], l_in[a], send[a], recv[a], x, y, c).start()
        token[...] = jnp.zeros_like(token)

    sems = [pltpu.SemaphoreType.DMA((1,))] * (2 * n)
    outs = pl.pallas_call(
        body, name=name,
        in_specs=[HBM] * (2 * n),
        out_specs=[SEM] * (2 * n) + [HBM] * (2 * n) + [pl.BlockSpec(memory_space=pltpu.VMEM)],
        out_shape=sems + [pltpu.HBM(b.shape, b.dtype) for b in list(parts) + lands]
        + [jax.ShapeDtypeStruct((8, 128), f32)],
        input_output_aliases={a: 2 * n + a for a in range(2 * n)},
        compiler_params=pltpu.CompilerParams(has_side_effects=EFFECT),
    )(*[_in_hbm(b) for b in list(parts) + lands])
    return (list(outs[:n]), list(outs[n:2 * n]), list(outs[2 * n:3 * n]), list(outs[3 * n:4 * n]),
            outs[4 * n])


def _pair_exchange_wait(send, recv, parts, lands, after, name):
    n = len(parts)

    def body(*refs):
        p_in, l_in = refs[:n], refs[n:2 * n]
        send_r, recv_r = refs[2 * n:3 * n], refs[3 * n:4 * n]
        x, y, c, _ = _place()
        for a in range(n):
            cp = _sibling_copy(p_in[a], l_in[a], send_r[a], recv_r[a], x, y, c)
            cp.wait_send()
            cp.wait_recv()

    outs = pl.pallas_call(
        body, name=name,
        in_specs=[HBM] * (2 * n) + [SEM] * (2 * n) + [ANY],
        out_specs=[HBM] * (2 * n),
        out_shape=[pltpu.HBM(b.shape, b.dtype) for b in list(parts) + list(lands)],
        input_output_aliases={a: a for a in range(2 * n)},
        compiler_params=pltpu.CompilerParams(has_side_effects=EFFECT),
    )(*parts, *lands, *send, *recv, after)
    return list(outs[:n]), list(outs[n:])


def _forward_copy(buf_ref, send_ref, recv_ref, j, chip, x, y, c, landing):
    return pltpu.make_async_remote_copy(
        src_ref=buf_ref.at[_chip_id(chip), c], dst_ref=buf_ref.at[_chip_id(chip), 1 - c if landing else c],
        send_sem=send_ref.at[j], recv_sem=recv_ref.at[j], device_id=(x, y, 1 - c), device_id_type=MESH)


def _forward_start(bufs, name):
    n = len(bufs)

    def body(*refs):
        ins = refs[:n]
        send, recv = refs[n:2 * n], refs[2 * n:3 * n]
        token = refs[4 * n]
        x, y, c, chips = _place()
        for a in range(n):
            for j, chip in enumerate(chips):
                _forward_copy(ins[a], send[a], recv[a], j, chip, x, y, c, False).start()
        token[...] = jnp.zeros_like(token)

    sems = [pltpu.SemaphoreType.DMA((N_PEER,))] * (2 * n)
    outs = pl.pallas_call(
        body, name=name,
        in_specs=[HBM] * n,
        out_specs=[SEM] * (2 * n) + [HBM] * n + [pl.BlockSpec(memory_space=pltpu.VMEM)],
        out_shape=sems + [pltpu.HBM(b.shape, b.dtype) for b in bufs] + [jax.ShapeDtypeStruct((8, 128), f32)],
        input_output_aliases={a: 2 * n + a for a in range(n)},
        compiler_params=pltpu.CompilerParams(has_side_effects=EFFECT),
    )(*[_in_hbm(b) for b in bufs])
    return list(outs[:n]), list(outs[n:2 * n]), list(outs[2 * n:3 * n]), outs[3 * n]


def _forward_wait(send, recv, bufs, after, name):
    n = len(bufs)

    def body(*refs):
        ins = refs[:n]
        send_r, recv_r = refs[n:2 * n], refs[2 * n:3 * n]
        x, y, c, chips = _place()
        for a in range(n):
            for j, chip in enumerate(chips):
                cp = _forward_copy(ins[a], send_r[a], recv_r[a], j, chip, x, y, c, True)
                cp.wait_send()
                cp.wait_recv()

    outs = pl.pallas_call(
        body, name=name,
        in_specs=[HBM] * n + [SEM] * (2 * n) + [ANY],
        out_specs=[HBM] * n,
        out_shape=[pltpu.HBM(b.shape, b.dtype) for b in bufs],
        input_output_aliases={a: a for a in range(n)},
        compiler_params=pltpu.CompilerParams(has_side_effects=EFFECT),
    )(*bufs, *send, *recv, after)
    return list(outs)


def _adamw_math(w, g, m, v):
    m = ADAM_B1 * m + (1.0 - ADAM_B1) * g
    v = ADAM_B2 * v + (1.0 - ADAM_B2) * (g * g)
    m_hat = m / (1.0 - ADAM_B1 ** ADAM_STEP)
    v_hat = v / (1.0 - ADAM_B2 ** ADAM_STEP)
    delta = -ADAM_LR * (m_hat / (jnp.sqrt(v_hat) + ADAM_EPS) + ADAM_WD * w)
    return delta, m, v


ADAMW_BLOCK_BYTES = 3 << 19


def _adamw(ws, mines, theirs, ms, vs, qc, name):
    n = len(ws)
    halves = [w.shape[0] // 2 for w in ws]
    nb = next(k for k in range(1, min(halves) + 1)
              if all(hr % k == 0 and (hr // k) % 8 == 0 and (hr // k) * w.shape[1] * 4 <= ADAMW_BLOCK_BYTES
                     for hr, w in zip(halves, ws)))

    def body(qc_ref, *refs):
        mine_here = pl.program_id(0) == qc_ref[1]
        for a in range(n):
            w_ref, a_ref, b_ref, m_ref, v_ref = (refs[k * n + a] for k in range(5))
            g_ref, d_ref, mo_ref, vo_ref = (refs[(5 + k) * n + a] for k in range(4))
            g = jnp.where(mine_here, a_ref[...], b_ref[...])
            g_ref[...] = g
            d_ref[...], mo_ref[...], vo_ref[...] = _adamw_math(w_ref[...], g, m_ref[...], v_ref[...])

    blocks = [(hr // nb, w.shape[1]) for hr, w in zip(halves, ws)]
    full = [pl.BlockSpec(b, lambda h, i, qc_ref: (h * nb + i, 0)) for b in blocks]
    half = [pl.BlockSpec(b, lambda h, i, qc_ref: (i, 0)) for b in blocks]
    outs = pl.pallas_call(
        body, name=name,
        grid_spec=pltpu.PrefetchScalarGridSpec(
            num_scalar_prefetch=1, grid=(2, nb),
            in_specs=full + half + half + full + full, out_specs=full * 4),
        out_shape=[jax.ShapeDtypeStruct(w.shape, f32) for w in ws] * 4,
        compiler_params=_cp(2),
    )(qc, *ws, *mines, *theirs, *ms, *vs)
    return [tuple(outs[k * n + a] for k in range(4)) for a in range(n)]


REPL = [("ffn1_norm", 1), ("mix_norm", 1), ("b_in", 6), ("rnn_conv_b", 1), ("rg_b_a", 1), ("rg_b_x", 1),
        ("rg_lambda", 1), ("conv_dw_b", 1), ("conv_ln_g", 1), ("conv_ln_b", 1), ("conv_b_proj", 1),
        ("ffn2_norm", 1), ("final_norm", 1)]
COLSH = [("meta_tokens", NMETA), ("rnn_conv_w", KC4), ("conv_dw_w", KC31)]
SMALL = REPL + COLSH
CS = D // NCHIP


def _pack_rows():
    starts, row = {}, 0
    for k, rows in REPL:
        starts[k] = row
        row += rows
    for k, rows in COLSH:
        row = -(-row // 8) * 8
        starts[k] = row
        row += rows
    return starts, -(-row // 8) * 8


PACK_START, LOSS_ROW = _pack_rows()
SMALL_ROWS = LOSS_ROW + 8


def _small_pack(g, loss_row):
    pieces, row = [], 0
    for k, rows in SMALL:
        if PACK_START[k] > row:
            pieces.append(jnp.zeros((PACK_START[k] - row, D), f32))
        pieces.append(g[k].reshape(rows, D))
        row = PACK_START[k] + rows
    pieces.append(jnp.zeros((LOSS_ROW - row, D), f32))
    pieces.append(loss_row)
    pieces.append(jnp.zeros((SMALL_ROWS - LOSS_ROW - 1, D), f32))
    return jnp.concatenate(pieces, axis=0)


def _adamw_small(packs, ws, ms, vs):
    ns = len(SMALL)

    def body(*refs):
        pack_ref = refs[0]
        w_refs, m_refs, v_refs = refs[1:1 + ns], refs[1 + ns:1 + 2 * ns], refs[1 + 2 * ns:1 + 3 * ns]
        outs = refs[1 + 3 * ns:1 + 7 * ns]
        g_refs, d_refs, mo_refs, vo_refs = outs[:ns], outs[ns:2 * ns], outs[2 * ns:3 * ns], outs[3 * ns:]
        loss_ref = refs[1 + 7 * ns]
        gsum_sc = refs[2 + 7 * ns]
        q = 2 * lax.axis_index("x") + lax.axis_index("y")
        acc = pack_ref[0]
        for dev in range(1, 8):
            acc = acc + pack_ref[dev]
        gsum_sc[...] = acc
        loss_ref[...] = gsum_sc[LOSS_ROW:LOSS_ROW + 1, :]
        for idx, (name, rows) in enumerate(SMALL):
            row = PACK_START[name]
            if idx < len(REPL):
                for k in range(rows):
                    cols = slice(k * D, (k + 1) * D)
                    g = gsum_sc[row + k:row + k + 1, :]
                    d, mm, vv = _adamw_math(w_refs[idx][:, cols], g, m_refs[idx][:, cols], v_refs[idx][:, cols])
                    g_refs[idx][:, cols] = g
                    d_refs[idx][:, cols] = d
                    mo_refs[idx][:, cols] = mm
                    vo_refs[idx][:, cols] = vv
            else:
                g = gsum_sc[row:row + rows, pl.ds(pl.multiple_of(q * CS, CS), CS)]
                d, mm, vv = _adamw_math(w_refs[idx][...], g, m_refs[idx][...], v_refs[idx][...])
                g_refs[idx][...] = g
                d_refs[idx][...] = d
                mo_refs[idx][...] = mm
                vo_refs[idx][...] = vv

    shapes = [jax.ShapeDtypeStruct(w.shape, f32) for w in ws]
    return pl.pallas_call(
        body, name="adamw_small",
        out_shape=shapes * 4 + [jax.ShapeDtypeStruct((1, D), f32)],
        scratch_shapes=[pltpu.VMEM((SMALL_ROWS, D), f32)],
        compiler_params=pltpu.CompilerParams(vmem_limit_bytes=VMEM_LIMIT),
    )(packs, *ws, *ms, *vs)


WEIGHTS = ['meta_tokens', 'ffn1_norm', 'ffn1_w_gu', 'ffn1_w_down', 'mix_norm', 'w_in', 'b_in', 'rnn_conv_w',
           'rnn_conv_b', 'rg_w_a', 'rg_b_a', 'rg_w_x', 'rg_b_x', 'rg_lambda', 'rnn_w_proj', 'conv_dw_w',
           'conv_dw_b', 'conv_ln_g', 'conv_ln_b', 'conv_w_proj', 'conv_b_proj', 'w_out', 'ffn2_norm',
           'ffn2_w_gu', 'ffn2_w_down', 'final_norm']


def _as2d(a):
    return a.reshape(-1, a.shape[-1])


def _step(x, loss_target, w, m, v):
    seq = x.shape[1]
    n_valid = NMETA + seq
    t = -(-n_valid // TM) * TM

    qc = jnp.stack([2 * lax.axis_index("x") + lax.axis_index("y"), lax.axis_index("c")]).astype(jnp.int32)
    p = {k: w[k].reshape(1, rows * D) for k, rows in REPL}

    first = ["ffn1_w_gu", "ffn1_w_down", "small"]
    later = [["w_in"], ["rg_w_a", "rg_w_x", "rnn_w_proj", "conv_w_proj", "w_out"], ["ffn2_w_gu", "ffn2_w_down"]]
    small_rows = sum(r for _, r in COLSH)
    small = jnp.concatenate([_as2d(w[k]) for k, _ in COLSH] + [jnp.zeros((64 - small_rows, CS), f32)], axis=0)

    def cast(k, token=None):
        src, dtype = (small, f32) if k == "small" else (_as2d(w[k]), bf16)
        return _cast_into_slot(src, qc, dtype, "cast_" + k, after=token)

    send1, recv1, bufs1, token1 = _gather_start([cast(k) for k in first], "gather_start_first")
    rest = [k for grp in later for k in grp]
    send2, recv2, bufs2, token2 = _gather_start([cast(k, token1) for k in rest], "gather_start_rest")

    def install(names, done):
        for k, b in zip(names, done):
            full = b.reshape(NCHIP, 2 * b.shape[2], b.shape[3])
            if k in ("ffn1_w_down", "ffn2_w_down"):
                full = full.reshape(F, D)
            elif k in ("rnn_w_proj", "conv_w_proj", "w_out"):
                full = full.reshape(D, D)
            elif k in ("rg_w_a", "rg_w_x"):
                full = full.reshape(NCHIP, NHEAD, HD // NCHIP, HD).transpose(1, 0, 2, 3).reshape(NHEAD, HD, HD)
            p[k] = full

    def finish(names, send, recv, bufs, after, tag):
        install(names, _forward_halves(_gather_wait(send, recv, bufs, after, "gather_wait_" + tag),
                                       "gather_forward_" + tag))

    def group(names):
        idx = [rest.index(k) for k in names]
        return names, [send2[i] for i in idx], [recv2[i] for i in idx], [bufs2[i] for i in idx]

    h0 = jnp.pad(x[0] + token1[0:1, 0:1], ((NMETA, t - n_valid), (0, 0)))
    tgt = jnp.pad(loss_target[0] + token2[0:1, 0:1], ((NMETA, t - n_valid), (0, 0)))
    finish(first, send1, recv1, bufs1, (token2, h0, tgt), "first")
    small_full = p.pop("small").transpose(1, 0, 2).reshape(64, D)
    row = 0
    for k, rows in COLSH:
        p[k] = small_full[row:row + rows]
        row += rows

    h0 = lax.dynamic_update_slice(h0, p["meta_tokens"], (0, 0))
    h1, gate1, up1, n1 = _ffn_fwd(h0, p["ffn1_norm"], p["ffn1_w_gu"], p["ffn1_w_down"], "ffn1_fwd")
    finish(*group(later[0]), h1, "in")
    proj, n2 = _inproj_fwd(h1, p["mix_norm"], p["w_in"], p["b_in"])
    names_l = later[1] + later[2]
    _, send_l, recv_l, bufs_l = group(names_l)
    send_f, recv_f, bufs_f, token = _forward_start(
        _gather_wait(send_l, recv_l, bufs_l, proj, "gather_wait_late"), "gather_forward_start")
    vc, s = _conv_fwd(proj, p["conv_dw_w"], p["conv_dw_b"], p["conv_ln_g"], p["conv_ln_b"], after=token)
    install(names_l, _forward_wait(send_f, recv_f, bufs_f, vc, "gather_forward_wait"))
    xr, hr, z, gates = _rnn_fwd(proj, p["rnn_conv_w"], p["rnn_conv_b"], p["rg_w_a"], p["rg_b_a"],
                         p["rg_w_x"], p["rg_b_x"], p["rg_lambda"])
    h2 = _merge_fwd(h1, z, s, proj, p["rnn_w_proj"], p["conv_w_proj"], p["conv_b_proj"], p["w_out"])
    dh3, loss_blk, d_final, gate2, up2, n3 = _ffn_fwd(
        h2, p["ffn2_norm"], p["ffn2_w_gu"], p["ffn2_w_down"], "ffn2_fwd",
        loss_head=(p["final_norm"], tgt, n_valid))

    g = {"final_norm": d_final}
    pending = []

    def exchange_start(names, tag):
        parts = []
        for k in names:
            rows = g[k].size // (NCHIP * g[k].shape[-1])
            parts.append(g[k].reshape((NCHIP, 2, rows // 2, g[k].shape[-1])))
        send, recv, parts, lands, token = _pair_exchange_start(parts, "pair_exchange_start_" + tag)
        return (names, tag, send, recv, parts, lands), token

    def reduce_start(state, after):
        names, tag, send, recv, parts, lands = state
        parts, from_sibling = _pair_exchange_wait(send, recv, parts, lands, after, "pair_exchange_wait_" + tag)
        sums, lands = _pair_add(parts, from_sibling, qc, "pair_add_" + tag)
        send, recv, sums, lands, token = _reduce_start(sums, lands, "reduce_start_" + tag)
        pending.append((names, tag, send, recv, sums, lands))
        return token

    dh2, dgate2, dup2, a2, df2, g["ffn2_norm"] = _ffn_bwd(
        dh3, h2, p["ffn2_norm"], gate2, up2, p["ffn2_w_gu"], p["ffn2_w_down"], "ffn2_bwd")
    g["ffn2_w_gu"] = _ffn_gu_grad(n3, dgate2, dup2, "ffn2")
    g["ffn2_w_down"] = _ffn_down_grad(a2, df2, "ffn2")
    state, token = exchange_start(["ffn2_w_gu", "ffn2_w_down"], "ffn2")

    dz, ds, dproj, dh2b, merged, dya, dyb, g["conv_b_proj"] = _merge_bwd(
        dh2, z, s, proj, p["rnn_w_proj"], p["conv_w_proj"], p["conv_b_proj"], p["w_out"], after=token)
    token = reduce_start(state, dz)
    dproj, g["conv_dw_w"], g["conv_dw_b"], g["conv_ln_g"], g["conv_ln_b"] = _conv_bwd(
        ds, vc, proj, dproj, p["conv_dw_w"], p["conv_ln_g"], p["conv_ln_b"], after=token)
    g["w_out"] = _square_grad(merged, dh2b, "dw_out")
    g["rnn_w_proj"] = _square_grad(z, dya, "dw_rnn_proj")
    g["conv_w_proj"] = _square_grad(s, dyb, "dw_conv_proj")
    (dproj, g["rg_w_a"], g["rg_w_x"], g["rnn_conv_w"], g["rnn_conv_b"], g["rg_b_a"], g["rg_b_x"],
     g["rg_lambda"]) = _rnn_bwd(dz, xr, hr, gates, proj, dproj, p["rnn_conv_w"], p["rg_w_a"],
                                p["rg_w_x"], p["rg_lambda"])

    dh1, g["mix_norm"], db_in = _inproj_bwd(dproj, dh2, h1, p["mix_norm"], p["w_in"])
    g["b_in"] = db_in.reshape(1, NIN)
    g["w_in"] = _tn_matmul(n2, dproj, D, NIN // NCHIP, (NCHIP, D, NIN // NCHIP),
                           (None, D, NIN // NCHIP), lambda k, nn, mm: (nn, 0, 0), "dw_in")
    state, token = exchange_start(["w_out", "rnn_w_proj", "conv_w_proj", "rg_w_a", "rg_w_x", "w_in"], "mix")

    dh0, dgate1, dup1, a1, df1, g["ffn1_norm"] = _ffn_bwd(
        dh1, h0, p["ffn1_norm"], gate1, up1, p["ffn1_w_gu"], p["ffn1_w_down"], "ffn1_bwd", after=token)
    g["meta_tokens"] = dh0[0:NMETA]
    grad_x = dh0[NMETA:n_valid][None]
    token = reduce_start(state, dh0)

    send_s, recv_s, pack_buf, token_s = _gather_all_start(
        _place_pack(_small_pack(g, loss_blk.reshape(1, D)), qc), "gather_all_start")
    g["ffn1_w_down"] = _ffn_down_grad(a1, df1, "ffn1", after=(token, token_s))
    state, token = exchange_start(["ffn1_w_down"], "ffn1_down")
    gate_half = _tn_matmul(n1, dgate1, D, FS, (NCHIP, D, FS), (None, D, FS), lambda k, nn, mm: (nn, 0, 0),
                           "ffn1_dwg", after=token)
    token = reduce_start(state, gate_half)
    g["ffn1_w_gu"] = _tn_matmul(n1, dup1, D, FS, (NCHIP, D, FS), (None, D, FS), lambda k, nn, mm: (2 + nn, 0, 0),
                                "ffn1_dwu", base=gate_half, after=token)
    state_gu, token = exchange_start(["ffn1_w_gu"], "ffn1_gu")
    packs = _gather_all_wait(send_s, recv_s, pack_buf, token, "gather_all_wait")

    grads, deltas, new_m, new_v = {}, {}, {}, {}

    def landed_sums(items, after):
        names, mine = [], []
        for grp_names, grp_tag, send, recv, sums, lands in items:
            landed = _reduce_wait(send, recv, sums, lands, after, "reduce_wait_" + grp_tag)
            mine += _sum_chips(landed, "sum_chips_" + grp_tag)
            names += grp_names
            after = mine[-1]
        return names, mine

    def share_and_update(names, mine, tag, after=None):
        theirs = _pair_share(mine, "pair_share_" + tag, after=after)
        got = dict(zip(names, zip(mine, theirs)))
        square = [k for k in names if got[k][0].shape[0] * 2 <= HD]
        for batch in [[k] for k in names if k not in square] + ([square] if square else []):
            outs = _adamw([_as2d(w[k]) for k in batch], [got[k][0] for k in batch], [got[k][1] for k in batch],
                          [_as2d(m[k]) for k in batch], [_as2d(v[k]) for k in batch], qc,
                          "adamw_" + (batch[0] if len(batch) == 1 else "mixer"))
            for k, out in zip(batch, outs):
                grads[k], deltas[k], new_m[k], new_v[k] = (a.reshape(w[k].shape) for a in out)
        return [new_v[k] for k in names]

    early_names, early_mine = landed_sums(pending[:2], packs)
    token = reduce_start(state_gu, early_mine[-1])
    after = share_and_update(early_names, early_mine, "early", after=token)
    share_and_update(*landed_sums(pending[2:], after), "late")
    names = [k for k, _ in SMALL]
    shape2 = {k: ((1, rows * D) if (k, rows) in REPL else (rows, CS)) for k, rows in SMALL}
    outs = _adamw_small(packs, *[[a[k].reshape(shape2[k]) for k in names] for a in (w, m, v)])
    ns = len(names)
    for i, k in enumerate(names):
        grads[k], deltas[k], new_m[k], new_v[k] = (outs[j * ns + i].reshape(w[k].shape) for j in range(4))

    loss = outs[4 * ns][0, 0]
    return (loss, grad_x, *[grads[k] for k in WEIGHTS], *[deltas[k] for k in WEIGHTS],
            *[new_m[k] for k in WEIGHTS], *[new_v[k] for k in WEIGHTS])


def kernel(x, meta_tokens, ffn1_norm, ffn1_w_gu, ffn1_w_down, mix_norm, w_in, b_in, rnn_conv_w, rnn_conv_b, rg_w_a, rg_b_a, rg_w_x, rg_b_x, rg_lambda, rnn_w_proj, conv_dw_w, conv_dw_b, conv_ln_g, conv_ln_b, conv_w_proj, conv_b_proj, w_out, ffn2_norm, ffn2_w_gu, ffn2_w_down, final_norm, loss_target, m_meta_tokens, m_ffn1_norm, m_ffn1_w_gu, m_ffn1_w_down, m_mix_norm, m_w_in, m_b_in, m_rnn_conv_w, m_rnn_conv_b, m_rg_w_a, m_rg_b_a, m_rg_w_x, m_rg_b_x, m_rg_lambda, m_rnn_w_proj, m_conv_dw_w, m_conv_dw_b, m_conv_ln_g, m_conv_ln_b, m_conv_w_proj, m_conv_b_proj, m_w_out, m_ffn2_norm, m_ffn2_w_gu, m_ffn2_w_down, m_final_norm, v_meta_tokens, v_ffn1_norm, v_ffn1_w_gu, v_ffn1_w_down, v_mix_norm, v_w_in, v_b_in, v_rnn_conv_w, v_rnn_conv_b, v_rg_w_a, v_rg_b_a, v_rg_w_x, v_rg_b_x, v_rg_lambda, v_rnn_w_proj, v_conv_dw_w, v_conv_dw_b, v_conv_ln_g, v_conv_ln_b, v_conv_w_proj, v_conv_b_proj, v_w_out, v_ffn2_norm, v_ffn2_w_gu, v_ffn2_w_down, v_final_norm):
    args = locals()
    w = {k: args[k] for k in WEIGHTS}
    m = {k: args["m_" + k] for k in WEIGHTS}
    v = {k: args["v_" + k] for k in WEIGHTS}
    return _step(x, loss_target, w, m, v)
```

```python
import jax
import jax.numpy as jnp
from jax import lax
from jax.experimental import pallas as pl
from jax.experimental.pallas import tpu as pltpu

f32 = jnp.float32
bf16 = jnp.bfloat16

D = 1024
F = 2816
FS = F // 2
NIN = 6 * D
NMETA = 16
NHEAD = 4
HD = D // NHEAD
KC4 = 4
KC31 = 31
HALO = 32
EPS = 1e-6
TM = 416
NCHIP = 4
MESH = pl.DeviceIdType.MESH

ADAM_LR = 0.001
ADAM_B1 = 0.9
ADAM_B2 = 0.999
ADAM_EPS = 1e-08
ADAM_WD = 0.01
ADAM_STEP = 10

VMEM_LIMIT = 56 * 1024 * 1024
FSUB = [(o, min(256, FS - o)) for o in range(0, FS, 256)]


def _cp(n_axes, **kw):
    return pltpu.CompilerParams(dimension_semantics=("arbitrary",) * n_axes,
                                vmem_limit_bytes=VMEM_LIMIT, **kw)


RESIDENT = pl.BlockSpec(memory_space=pltpu.VMEM)


def _n_after(after):
    return 0 if after is None else (len(after) if isinstance(after, (tuple, list)) else 1)


def _ordered(body, in_specs, args, after):
    if after is None:
        return body, in_specs, args
    extra = tuple(after) if isinstance(after, (tuple, list)) else (after,)
    return (lambda *refs: body(*refs[len(extra):]),
            [pl.BlockSpec(memory_space=pl.ANY)] * len(extra) + list(in_specs), extra + tuple(args))


def _nt_dot(a, b):
    return lax.dot_general(a, b, (((1,), (1,)), ((), ())), preferred_element_type=f32)


def _tn_dot(a, b):
    return lax.dot_general(a, b, (((0,), (0,)), ((), ())), preferred_element_type=f32)


def _sigmoid(x):
    return 0.5 * jnp.tanh(0.5 * x) + 0.5


def _log1p(y):
    u = 1.0 + y
    d = u - 1.0
    return jnp.where(d == 0.0, y, jnp.log(u) * (y / jnp.where(d == 0.0, 1.0, d)))


def _softplus(x):
    return jnp.maximum(x, 0.0) + _log1p(jnp.exp(-jnp.abs(x)))


def _one_minus_square(a, log_a):
    x = 2.0 * log_a
    series = x * (1.0 + x * (0.5 + x * (1.0 / 6.0)))
    return jnp.where(jnp.abs(x) < 0.03, -series, 1.0 - a * a)


_GELU_C = 0.7978845608028654
_GELU_K = 0.044715


def _gelu_and_grad(y):
    y2 = y * y
    th = jnp.tanh(_GELU_C * (y + _GELU_K * y * y2))
    gel = 0.5 * y * (1.0 + th)
    dgel = 0.5 * (1.0 + th) + 0.5 * y * (1.0 - th * th) * _GELU_C * (1.0 + 3.0 * _GELU_K * y2)
    return gel, dgel


def _rms_stats(h):
    return lax.rsqrt(jnp.mean(h * h, axis=-1, keepdims=True) + EPS)


def _rms_bwd(dn, h, g):
    r = _rms_stats(h)
    nhat = h * r
    dnh = dn * g
    dh = r * (dnh - nhat * jnp.mean(dnh * nhat, axis=-1, keepdims=True))
    dg = jnp.sum(dn * nhat, axis=0, keepdims=True)
    return dh, dg


def _row_ids(shape):
    return lax.broadcasted_iota(jnp.int32, shape, 0)


def _ffn_fwd(h, g, wgu, wd, name, loss_head=None):
    t = h.shape[0]
    nj = 2
    tm = TM
    n_head = 0 if loss_head is None else 2

    def body(*refs):
        h_ref, g_ref, wg_ref, wd_ref = refs[:4]
        outs = refs[4 + n_head:]
        gate_ref, up_ref, n_ref, a_sc = outs[-4:]
        i = pl.program_id(0)
        hh = h_ref[...]
        nb = (hh * _rms_stats(hh) * g_ref[...]).astype(bf16)
        n_ref[...] = nb

        acc = None
        for j in range(nj):
            for off, width in FSUB:
                cols = slice(off, off + width)
                out_cols = slice(j * FS + off, j * FS + off + width)
                gt = jnp.dot(nb, wg_ref[j, :, cols], preferred_element_type=f32)
                up = jnp.dot(nb, wg_ref[2 + j, :, cols], preferred_element_type=f32)
                gate_ref[:, out_cols] = gt.astype(bf16)
                up_ref[:, out_cols] = up.astype(bf16)
                a_sc[:, cols] = (gt * _sigmoid(gt) * up).astype(bf16)
            part = jnp.dot(a_sc[...], wd_ref[j], preferred_element_type=f32)
            acc = part if acc is None else acc + part
        hh = hh + 0.5 * acc

        if loss_head is None:
            outs[0][...] = hh
        else:
            gf_ref, t_ref = refs[4:6]
            dh_ref, loss_ref, dgf_ref = outs[:3]

            @pl.when(i == 0)
            def _():
                loss_ref[...] = jnp.zeros_like(loss_ref)
                dgf_ref[...] = jnp.zeros_like(dgf_ref)

            gf = gf_ref[...]
            row = i * tm + _row_ids((tm, 1))
            valid = jnp.logical_and(row >= NMETA, row < loss_head[2])
            err = jnp.where(valid, hh * _rms_stats(hh) * gf - t_ref[...], 0.0)
            loss_ref[...] += 0.5 * jnp.sum(err * err) * (1.0 / D)
            dh, dgf = _rms_bwd(err * (1.0 / D), hh, gf)
            dh_ref[...] = dh
            dgf_ref[...] += dgf

    rowd = pl.BlockSpec((tm, D), lambda i: (i, 0))
    vec = pl.BlockSpec((1, D), lambda i: (0, 0))
    rowf = pl.BlockSpec((tm, F), lambda i: (i, 0))
    in_specs, args = [rowd, vec, RESIDENT, RESIDENT], [h, g, wgu, wd.reshape(nj, FS, D)]
    out_specs, out_shape = [rowd], [jax.ShapeDtypeStruct((t, D), f32)]
    if loss_head is not None:
        in_specs, args = in_specs + [vec, rowd], args + [loss_head[0], loss_head[1]]
        out_specs += [pl.BlockSpec((8, 128), lambda i: (0, 0)), vec]
        out_shape += [jax.ShapeDtypeStruct((8, 128), f32), jax.ShapeDtypeStruct((1, D), f32)]
    return pl.pallas_call(
        body, name=name, grid=(t // tm,),
        in_specs=in_specs,
        out_specs=out_specs + [rowf, rowf, rowd],
        out_shape=out_shape + [jax.ShapeDtypeStruct((t, F), bf16), jax.ShapeDtypeStruct((t, F), bf16),
                               jax.ShapeDtypeStruct((t, D), bf16)],
        scratch_shapes=[pltpu.VMEM((tm, FS), bf16)],
        compiler_params=_cp(1),
    )(*args)


def _inproj_fwd(h, g, win, b_in):
    t = h.shape[0]
    tn = NIN // NCHIP

    def body(h_ref, g_ref, w_ref, b_ref, proj_ref, n_ref):
        hh = h_ref[...]
        nb = (hh * _rms_stats(hh) * g_ref[...]).astype(bf16)
        n_ref[...] = nb
        for s in range(NCHIP):
            cols = slice(s * tn, (s + 1) * tn)
            proj_ref[:, cols] = jnp.dot(nb, w_ref[s], preferred_element_type=f32) + b_ref[:, cols]

    return pl.pallas_call(
        body, name="inproj_fwd", grid=(t // TM,),
        in_specs=[
            pl.BlockSpec((TM, D), lambda i: (i, 0)),
            pl.BlockSpec((1, D), lambda i: (0, 0)),
            RESIDENT,
            pl.BlockSpec((1, NIN), lambda i: (0, 0)),
        ],
        out_specs=[
            pl.BlockSpec((TM, NIN), lambda i: (i, 0)),
            pl.BlockSpec((TM, D), lambda i: (i, 0)),
        ],
        out_shape=[jax.ShapeDtypeStruct((t, NIN), f32), jax.ShapeDtypeStruct((t, D), bf16)],
        compiler_params=_cp(1),
    )(h, g, win, b_in)


def _block_gates(xr, wa_ref, ba, wx_ref, bx, lam):
    xrb = xr.astype(bf16)
    pa = jnp.concatenate([jnp.dot(xrb[:, hh * HD:(hh + 1) * HD], wa_ref[hh], preferred_element_type=f32)
                          for hh in range(NHEAD)], axis=1)
    px = jnp.concatenate([jnp.dot(xrb[:, hh * HD:(hh + 1) * HD], wx_ref[hh], preferred_element_type=f32)
                          for hh in range(NHEAD)], axis=1)
    ra = _sigmoid(pa + ba)
    ii = _sigmoid(px + bx)
    sp = _softplus(-lam)
    log_a = -8.0 * ra * sp
    a = jnp.exp(log_a)
    sq = jnp.sqrt(_one_minus_square(a, log_a))
    return ra, ii, a, sq, sp


LT = D // 128
UNR = 13


def _to_lane_tiles(ref, value):
    for lt in range(LT):
        ref[lt] = value[:, lt * 128:(lt + 1) * 128]


def _from_lane_tiles(ref):
    return jnp.concatenate([ref[lt] for lt in range(LT)], axis=1)


def _chain_scan(mult_sc, val_sc, start, reverse):
    ng = TM // 8

    def lanes(lt):
        return slice(lt * 128, (lt + 1) * 128)

    def chain(gi, carry):
        v_prev, p_prev = carry
        rows = pl.ds(ng - 1 - gi if reverse else gi, 8, stride=ng)
        v_new, p_new = [], []
        for lt in range(LT):
            mm = mult_sc.at[lt][rows, :]
            vv = mm * v_prev[:, lanes(lt)] + val_sc.at[lt][rows, :]
            pp = mm * p_prev[:, lanes(lt)]
            val_sc.at[lt][rows, :] = vv
            mult_sc.at[lt][rows, :] = pp
            v_new.append(vv)
            p_new.append(pp)
        return jnp.concatenate(v_new, axis=1), jnp.concatenate(p_new, axis=1)

    v_end, p_end = lax.fori_loop(0, ng, chain, (jnp.zeros((8, D), f32), jnp.ones((8, D), f32)))
    state, entries = start, [None] * 8
    for r in (reversed(range(8)) if reverse else range(8)):
        entries[r] = state
        state = v_end[r:r + 1, :] + p_end[r:r + 1, :] * state
    entry8 = jnp.concatenate(entries, axis=0)

    def add_entry(gi, carry):
        rows = pl.ds(gi, 8, stride=ng)
        for lt in range(LT):
            val_sc.at[lt][rows, :] = val_sc.at[lt][rows, :] + mult_sc.at[lt][rows, :] * entry8[:, lanes(lt)]
        return carry

    lax.fori_loop(0, ng, add_entry, 0)
    return state


def _strided_conv(w_ref, src_sc, out_sc, base, shifts, bias_ref=None):
    ng = TM // 8
    for lt in range(LT):
        lanes = slice(lt * 128, (lt + 1) * 128)
        taps = [jnp.broadcast_to(w_ref[k:k + 1, lanes], (8, 128)) for k in range(len(shifts))]
        init = (jnp.zeros((8, 128), f32) if bias_ref is None
                else jnp.broadcast_to(bias_ref[:, lanes], (8, 128)))

        def step(gb, carry, lt=lt, taps=taps, init=init):
            accs = [init] * UNR
            for k, shift in enumerate(shifts):
                for u in range(UNR):
                    rows = pl.ds(base + gb * UNR + u + shift, 8, stride=ng)
                    accs[u] = accs[u] + taps[k] * src_sc.at[lt][rows, :]
            for u in range(UNR):
                out_sc.at[lt][pl.ds(gb * UNR + u, 8, stride=ng), :] = accs[u]
            return carry

        lax.fori_loop(0, ng // UNR, step, 0)


def _strided_corr(a_sc, src_sc, base, shifts):
    ng = TM // 8
    per_tile = []
    for lt in range(LT):
        def step(gb, accs, lt=lt):
            accs = list(accs)
            for u in range(UNR):
                g = gb * UNR + u
                a_g = a_sc.at[lt][pl.ds(g, 8, stride=ng), :]
                for k, shift in enumerate(shifts):
                    accs[k] = accs[k] + a_g * src_sc.at[lt][pl.ds(base + g + shift, 8, stride=ng), :]
            return tuple(accs)

        per_tile.append(lax.fori_loop(0, ng // UNR, step, tuple(jnp.zeros((8, 128), f32) for _ in shifts)))
    return [jnp.concatenate([per_tile[lt][k] for lt in range(LT)], axis=1) for k in range(len(shifts))]


def _rnn_fwd(proj, cw, cb, wa, ba, wx, bx, lam):
    t = proj.shape[0]

    def body(x_ref, y_ref, cw_ref, cb_ref, wa_ref, ba_ref, wx_ref, bx_ref, lam_ref,
             xr_ref, hr_ref, z_ref, gates_ref, xext_sc, carry_sc, a_sc, h_sc):
        i = pl.program_id(0)

        @pl.when(i == 0)
        def _():
            xext_sc[0:8, :] = jnp.zeros((8, D), f32)
            carry_sc[...] = jnp.zeros_like(carry_sc)

        x = x_ref[...]
        xext_sc[8:8 + TM, :] = x
        xe = xext_sc[...]
        xr = cb_ref[...] + cw_ref[KC4 - 1:KC4, :] * x
        for k in range(KC4 - 1):
            xr = xr + cw_ref[k:k + 1, :] * pltpu.roll(xe, KC4 - 1 - k, 0)[8:8 + TM]
        xext_sc[0:8, :] = x[TM - 8:TM]

        ra, ii, a, sq, _ = _block_gates(xr, wa_ref, ba_ref[...], wx_ref, bx_ref[...], lam_ref[...])
        for slot, val in enumerate((ra, ii, a, sq)):
            gates_ref[slot] = val
        _to_lane_tiles(a_sc, a)
        _to_lane_tiles(h_sc, sq * ii * xr)
        carry_sc[...] = _chain_scan(a_sc, h_sc, carry_sc[...], reverse=False)
        hr = _from_lane_tiles(h_sc)
        gel, _ = _gelu_and_grad(y_ref[...])
        xr_ref[...] = xr
        hr_ref[...] = hr
        z_ref[...] = (hr * gel).astype(bf16)

    vec = pl.BlockSpec((1, D), lambda i: (0, 0))
    return pl.pallas_call(
        body, name="rnn_fwd", grid=(t // TM,),
        in_specs=[
            pl.BlockSpec((TM, D), lambda i: (i, 0)),
            pl.BlockSpec((TM, D), lambda i: (i, 1)),
            pl.BlockSpec((KC4, D), lambda i: (0, 0)),
            vec,
            pl.BlockSpec((NHEAD, HD, HD), lambda i: (0, 0, 0)),
            vec,
            pl.BlockSpec((NHEAD, HD, HD), lambda i: (0, 0, 0)),
            vec, vec,
        ],
        out_specs=[pl.BlockSpec((TM, D), lambda i: (i, 0))] * 3 + [pl.BlockSpec((4, TM, D), lambda i: (0, i, 0))],
        out_shape=[jax.ShapeDtypeStruct((t, D), f32), jax.ShapeDtypeStruct((t, D), f32),
                   jax.ShapeDtypeStruct((t, D), bf16), jax.ShapeDtypeStruct((4, t, D), f32)],
        scratch_shapes=[pltpu.VMEM((TM + 8, D), f32), pltpu.VMEM((1, D), f32),
                        pltpu.VMEM((LT, TM, 128), f32), pltpu.VMEM((LT, TM, 128), f32)],
        compiler_params=_cp(1),
    )(proj, proj, cw, cb, wa, ba, wx, bx, lam)


def _ln_stats(vc):
    mu = jnp.mean(vc, axis=-1, keepdims=True)
    xc = vc - mu
    rstd = lax.rsqrt(jnp.mean(xc * xc, axis=-1, keepdims=True) + EPS)
    return xc * rstd, rstd


def _conv_fwd(proj, w31, b31, ln_g, ln_b, after=None):
    t = proj.shape[0]

    def body(gv_ref, gg_ref, w_ref, b_ref, lg_ref, lb_ref, vc_ref, s_ref, vext_sc, out_sc):
        i = pl.program_id(0)

        @pl.when(i == 0)
        def _():
            vext_sc[:, 0:HALO, :] = jnp.zeros((LT, HALO, 128), f32)

        v = gv_ref[...] * _sigmoid(gg_ref[...])
        for lt in range(LT):
            vext_sc[lt, HALO:HALO + TM, :] = v[:, lt * 128:(lt + 1) * 128]

        _strided_conv(w_ref, vext_sc, out_sc, HALO, [k - (KC31 - 1) for k in range(KC31)], b_ref)
        for lt in range(LT):
            vext_sc[lt, 0:HALO, :] = v[TM - HALO:TM, lt * 128:(lt + 1) * 128]
        acc = _from_lane_tiles(out_sc)
        xhat, _ = _ln_stats(acc)
        ln = xhat * lg_ref[...] + lb_ref[...]
        vc_ref[...] = acc
        s_ref[...] = (ln * _sigmoid(ln)).astype(bf16)

    vec = pl.BlockSpec((1, D), lambda i: (0, 0))
    body, in_specs, args = _ordered(
        body,
        [pl.BlockSpec((TM, D), lambda i: (i, 2)),
         pl.BlockSpec((TM, D), lambda i: (i, 3)),
         pl.BlockSpec((KC31, D), lambda i: (0, 0)),
         vec, vec, vec],
        (proj, proj, w31, b31, ln_g, ln_b), after)
    return pl.pallas_call(
        body, name="conv_fwd", grid=(t // TM,),
        in_specs=in_specs,
        out_specs=[pl.BlockSpec((TM, D), lambda i: (i, 0))] * 2,
        out_shape=[jax.ShapeDtypeStruct((t, D), f32), jax.ShapeDtypeStruct((t, D), bf16)],
        scratch_shapes=[pltpu.VMEM((LT, TM + HALO, 128), f32), pltpu.VMEM((LT, TM, 128), f32)],
        compiler_params=_cp(1),
    )(*args)


def _merge_fwd(h, z, s, proj, wrp, wcp, bcp, wout):
    t = h.shape[0]

    def body(h_ref, z_ref, s_ref, ga_ref, gb_ref, wrp_ref, wcp_ref, bcp_ref, wout_ref, ho_ref):
        ya = jnp.dot(z_ref[...], wrp_ref[...], preferred_element_type=f32)
        yb = jnp.dot(s_ref[...], wcp_ref[...], preferred_element_type=f32) + bcp_ref[...]
        merged = _sigmoid(ga_ref[...]) * ya + _sigmoid(gb_ref[...]) * yb
        ho_ref[...] = h_ref[...] + jnp.dot(merged.astype(bf16), wout_ref[...], preferred_element_type=f32)

    row = pl.BlockSpec((TM, D), lambda i: (i, 0))
    wsq = pl.BlockSpec((D, D), lambda i: (0, 0))
    return pl.pallas_call(
        body, name="merge_fwd", grid=(t // TM,),
        in_specs=[row, row, row,
                  pl.BlockSpec((TM, D), lambda i: (i, 4)),
                  pl.BlockSpec((TM, D), lambda i: (i, 5)),
                  wsq, wsq, pl.BlockSpec((1, D), lambda i: (0, 0)), wsq],
        out_specs=row,
        out_shape=jax.ShapeDtypeStruct((t, D), f32),
        compiler_params=_cp(1),
    )(h, z, s, proj, proj, wrp, wcp, bcp, wout)


def _ffn_bwd(dh, h, g, gate, up, wgu, wd, name, after=None):
    t = h.shape[0]
    nj = 2

    def body(dh_ref, h_ref, g_ref, gate_ref, up_ref, wg_ref, wd_ref,
             dhi_ref, dgate_ref, dup_ref, a_ref, df_ref, dg_ref, dfb_sc, dn_sc):
        i = pl.program_id(0)
        j = pl.program_id(1)

        @pl.when(jnp.logical_and(i == 0, j == 0))
        def _():
            dg_ref[...] = jnp.zeros_like(dg_ref)

        @pl.when(j == 0)
        def _():
            dfb = (0.5 * dh_ref[...]).astype(bf16)
            dfb_sc[...] = dfb
            df_ref[...] = dfb
            dn_sc[...] = jnp.zeros_like(dn_sc)

        dfb = dfb_sc[...]
        for off, width in FSUB:
            cols = slice(off, off + width)
            da = _nt_dot(dfb, wd_ref[j, cols, :])
            gt = gate_ref[:, cols].astype(f32)
            uu = up_ref[:, cols].astype(f32)
            sg = _sigmoid(gt)
            silu = gt * sg
            a_ref[:, cols] = (silu * uu).astype(bf16)
            dgate_ref[:, cols] = (da * uu * (sg * (1.0 + gt * (1.0 - sg)))).astype(bf16)
            dup_ref[:, cols] = (da * silu).astype(bf16)
        dn_sc[...] += _nt_dot(dgate_ref[...], wg_ref[j]) + _nt_dot(dup_ref[...], wg_ref[2 + j])

        @pl.when(j == nj - 1)
        def _():
            dhin, dg = _rms_bwd(dn_sc[...], h_ref[...], g_ref[...])
            dhi_ref[...] = dh_ref[...] + dhin
            dg_ref[...] += dg

    rowd = pl.BlockSpec((TM, D), lambda i, j: (i, 0))
    rowf = pl.BlockSpec((TM, FS), lambda i, j: (i, j))
    vec = pl.BlockSpec((1, D), lambda i, j: (0, 0))
    body, in_specs, args = _ordered(
        body,
        [rowd, rowd, vec, rowf, rowf,
         RESIDENT, RESIDENT],
        (dh, h, g, gate, up, wgu, wd.reshape(nj, FS, D)), after)
    return pl.pallas_call(
        body, name=name, grid=(t // TM, nj),
        in_specs=in_specs,
        out_specs=[rowd, rowf, rowf, rowf, rowd, vec],
        out_shape=[jax.ShapeDtypeStruct((t, D), f32), jax.ShapeDtypeStruct((t, F), bf16),
                   jax.ShapeDtypeStruct((t, F), bf16), jax.ShapeDtypeStruct((t, F), bf16),
                   jax.ShapeDtypeStruct((t, D), bf16), jax.ShapeDtypeStruct((1, D), f32)],
        scratch_shapes=[pltpu.VMEM((TM, D), bf16), pltpu.VMEM((TM, D), f32)],
        compiler_params=_cp(2),
    )(*args)


def _big_tile(t):
    return max(k * TM for k in range(1, 6) if t % (k * TM) == 0)


ANY_SPEC = pl.BlockSpec(memory_space=pl.ANY)


def _tn_matmul(a, b, tk, tn, out_shape, out_block, out_map, name, base=None, after=None):
    t, kk = a.shape
    _, nn = b.shape
    tmm = _big_tile(t)
    nm = t // tmm

    def body(a_ref, b_ref, o_ref, acc_sc):
        m = pl.program_id(2)

        @pl.when(m == 0)
        def _():
            acc_sc[...] = jnp.zeros_like(acc_sc)

        acc_sc[...] += _tn_dot(a_ref[...], b_ref[...])

        @pl.when(m == nm - 1)
        def _():
            o_ref[...] = acc_sc[...].astype(o_ref.dtype)

    in_specs = [pl.BlockSpec((tmm, tk), lambda k, n, m: (m, k)),
                pl.BlockSpec((tmm, tn), lambda k, n, m: (m, n))]
    args, aliases = (a, b), {}
    if base is not None:
        body = (lambda inner: lambda a_ref, b_ref, base_ref, o_ref, acc_sc: inner(a_ref, b_ref, o_ref, acc_sc))(body)
        in_specs, args, aliases = in_specs + [ANY_SPEC], (a, b, base), {2: 0}
    if after is not None:
        body, in_specs, args = _ordered(body, in_specs, args, after)
        aliases = {k + _n_after(after): v for k, v in aliases.items()}
    return pl.pallas_call(
        body, name=name, grid=(kk // tk, nn // tn, nm),
        in_specs=in_specs,
        out_specs=pl.BlockSpec(out_block, out_map),
        out_shape=jax.ShapeDtypeStruct(out_shape, bf16),
        scratch_shapes=[pltpu.VMEM((tk, tn), f32)],
        input_output_aliases=aliases,
        compiler_params=_cp(3),
    )(*args)


def _merge_bwd(dh, z, s, proj, wrp, wcp, bcp, wout, after=None):
    t = dh.shape[0]

    def body(dh_ref, z_ref, s_ref, ga_ref, gb_ref, wrp_ref, wcp_ref, bcp_ref, wout_ref,
             dz_ref, ds_ref, dgab_ref, dhb_ref, mg_ref, dya_ref, dyb_ref, dbcp_ref):
        i = pl.program_id(0)

        @pl.when(i == 0)
        def _():
            dbcp_ref[...] = jnp.zeros_like(dbcp_ref)

        dhb = dh_ref[...].astype(bf16)
        dhb_ref[...] = dhb
        dmg = _nt_dot(dhb, wout_ref[...])
        ya = jnp.dot(z_ref[...], wrp_ref[...], preferred_element_type=f32)
        yb = jnp.dot(s_ref[...], wcp_ref[...], preferred_element_type=f32) + bcp_ref[...]
        sa = _sigmoid(ga_ref[...])
        sb = _sigmoid(gb_ref[...])
        mg_ref[...] = (sa * ya + sb * yb).astype(bf16)
        dgab_ref[:, 0:D] = (dmg * ya * sa * (1.0 - sa)).astype(bf16)
        dgab_ref[:, D:2 * D] = (dmg * yb * sb * (1.0 - sb)).astype(bf16)
        dya = dmg * sa
        dyb = dmg * sb
        dbcp_ref[...] += jnp.sum(dyb, axis=0, keepdims=True)
        dyab = dya.astype(bf16)
        dybb = dyb.astype(bf16)
        dya_ref[...] = dyab
        dyb_ref[...] = dybb
        dz_ref[...] = _nt_dot(dyab, wrp_ref[...])
        ds_ref[...] = _nt_dot(dybb, wcp_ref[...])

    row = pl.BlockSpec((TM, D), lambda i: (i, 0))
    wsq = pl.BlockSpec((D, D), lambda i: (0, 0))
    vec = pl.BlockSpec((1, D), lambda i: (0, 0))
    rowb = jax.ShapeDtypeStruct((t, D), bf16)
    body, in_specs, args = _ordered(
        body,
        [row, row, row,
         pl.BlockSpec((TM, D), lambda i: (i, 4)),
         pl.BlockSpec((TM, D), lambda i: (i, 5)),
         wsq, wsq, vec, wsq],
        (dh, z, s, proj, proj, wrp, wcp, bcp, wout), after)
    return pl.pallas_call(
        body, name="merge_bwd", grid=(t // TM,),
        in_specs=in_specs,
        out_specs=[row, row,
                   pl.BlockSpec((TM, 2 * D), lambda i: (i, 2)),
                   row, row, row, row, vec],
        out_shape=[jax.ShapeDtypeStruct((t, D), f32), jax.ShapeDtypeStruct((t, D), f32),
                   jax.ShapeDtypeStruct((t, NIN), bf16),
                   rowb, rowb, rowb, rowb, jax.ShapeDtypeStruct((1, D), f32)],
        compiler_params=_cp(1),
    )(*args)


def _conv_bwd(ds, vc, proj, dproj, w31, ln_g, ln_b, after=None):
    t = ds.shape[0]
    nt = t // TM
    hb = TM // HALO

    def body(ds_ref, vc_ref, gv_ref, gg_ref, gvp_ref, ggp_ref, dpin_ref, w_ref, lg_ref, lb_ref,
             dgvg_ref, dw_ref, db_ref, dlg_ref, dlb_ref, dext_sc, vext_sc, out_sc, dwacc_sc, small_sc):
        del dpin_ref
        i = pl.program_id(0)
        tile = nt - 1 - i

        @pl.when(i == 0)
        def _():
            dext_sc[:, TM:TM + HALO, :] = jnp.zeros((LT, HALO, 128), f32)
            dwacc_sc[...] = jnp.zeros_like(dwacc_sc)
            small_sc[...] = jnp.zeros_like(small_sc)

        lg = lg_ref[...]
        lb = lb_ref[...]

        xhat, rstd = _ln_stats(vc_ref[...])
        ln = xhat * lg + lb
        sg = _sigmoid(ln)
        dln = ds_ref[...] * (sg * (1.0 + ln * (1.0 - sg)))
        dxh = dln * lg
        dvc = rstd * (dxh - jnp.mean(dxh, axis=-1, keepdims=True)
                      - xhat * jnp.mean(dxh * xhat, axis=-1, keepdims=True))
        small_sc[0] += jnp.sum((dln * xhat).reshape(TM // 8, 8, D), axis=0)
        small_sc[1] += jnp.sum(dln.reshape(TM // 8, 8, D), axis=0)
        small_sc[2] += jnp.sum(dvc.reshape(TM // 8, 8, D), axis=0)
        sgg = _sigmoid(gg_ref[...])
        v = gv_ref[...] * sgg
        vprev = jnp.where(tile > 0, gvp_ref[...] * _sigmoid(ggp_ref[...]), 0.0)
        for lt in range(LT):
            lanes = slice(lt * 128, (lt + 1) * 128)
            dext_sc[lt, 0:TM, :] = dvc[:, lanes]
            vext_sc[lt, HALO:HALO + TM, :] = v[:, lanes]
            vext_sc[lt, 0:HALO, :] = vprev[:, lanes]

        _strided_conv(w_ref, dext_sc, out_sc, 0, [KC31 - 1 - k for k in range(KC31)])
        dv = _from_lane_tiles(out_sc)
        dgvg_ref[:, 0:D] = (dv * sgg).astype(bf16)
        dgvg_ref[:, D:2 * D] = (dv * gv_ref[...] * sgg * (1.0 - sgg)).astype(bf16)

        for k, part in enumerate(_strided_corr(dext_sc, vext_sc, HALO, [k - (KC31 - 1) for k in range(KC31)])):
            dwacc_sc[k] += part
        for lt in range(LT):
            dext_sc[lt, TM:TM + HALO, :] = dext_sc[lt, 0:HALO, :]

        @pl.when(i == nt - 1)
        def _():
            for k in range(KC31):
                dw_ref[k:k + 1, :] = jnp.sum(dwacc_sc[k], axis=0, keepdims=True)
            dlg_ref[...] = jnp.sum(small_sc[0], axis=0, keepdims=True)
            dlb_ref[...] = jnp.sum(small_sc[1], axis=0, keepdims=True)
            db_ref[...] = jnp.sum(small_sc[2], axis=0, keepdims=True)

    rev = lambda i: (nt - 1 - i, 0)
    vec = pl.BlockSpec((1, D), lambda i: (0, 0))
    halo_row = lambda i: jnp.maximum((nt - 1 - i) * hb - 1, 0)
    body, in_specs, args = _ordered(
        body,
        [pl.BlockSpec((TM, D), rev),
         pl.BlockSpec((TM, D), rev),
         pl.BlockSpec((TM, D), lambda i: (nt - 1 - i, 2)),
         pl.BlockSpec((TM, D), lambda i: (nt - 1 - i, 3)),
         pl.BlockSpec((HALO, D), lambda i: (halo_row(i), 2)),
         pl.BlockSpec((HALO, D), lambda i: (halo_row(i), 3)),
         pl.BlockSpec(memory_space=pl.ANY),
         pl.BlockSpec((KC31, D), lambda i: (0, 0)),
         vec, vec],
        (ds, vc, proj, proj, proj, proj, dproj, w31, ln_g, ln_b), after)
    return pl.pallas_call(
        body, name="conv_bwd", grid=(nt,),
        in_specs=in_specs,
        out_specs=[
            pl.BlockSpec((TM, 2 * D), lambda i: (nt - 1 - i, 1)),
            pl.BlockSpec((KC31, D), lambda i: (0, 0)),
            vec, vec, vec,
        ],
        out_shape=[jax.ShapeDtypeStruct((t, NIN), bf16),
                   jax.ShapeDtypeStruct((KC31, D), f32),
                   jax.ShapeDtypeStruct((1, D), f32), jax.ShapeDtypeStruct((1, D), f32),
                   jax.ShapeDtypeStruct((1, D), f32)],
        scratch_shapes=[pltpu.VMEM((LT, TM + HALO, 128), f32), pltpu.VMEM((LT, TM + HALO, 128), f32),
                        pltpu.VMEM((LT, TM, 128), f32), pltpu.VMEM((KC31, 8, D), f32),
                        pltpu.VMEM((3, 8, D), f32)],
        input_output_aliases={6 + _n_after(after): 0},
        compiler_params=_cp(1),
    )(*args)


def _rnn_bwd(dz, xr, hr, gates, proj, dproj, cw, wa, wx, lam):
    t = dz.shape[0]
    nt = t // TM
    ng = TM // 8
    hq = HD // NCHIP

    def body(dz_ref, xr_ref, hr_ref, hrp_ref, x_ref, xp_ref, y_ref, dpin_ref,
             cw_ref, wa_ref, gates_ref, wx_ref, lam_ref,
             dxy_ref, dwa_ref, dwx_ref, dcw_ref, dcb_ref, dba_ref, dbx_ref, dlam_ref,
             anext_sc, gcarry_sc, dext_sc, xext_sc, m_sc, g_sc, dwa_sc, dwx_sc, dsp_sc):
        del dpin_ref
        i = pl.program_id(0)
        tile = nt - 1 - i

        @pl.when(i == 0)
        def _():
            anext_sc[...] = jnp.zeros_like(anext_sc)
            gcarry_sc[...] = jnp.zeros_like(gcarry_sc)
            dext_sc[TM:TM + 8, :] = jnp.zeros((8, D), f32)
            dwa_sc[...] = jnp.zeros_like(dwa_sc)
            dwx_sc[...] = jnp.zeros_like(dwx_sc)
            dsp_sc[...] = jnp.zeros_like(dsp_sc)
            dcw_ref[...] = jnp.zeros_like(dcw_ref)
            dcb_ref[...] = jnp.zeros_like(dcb_ref)
            dba_ref[...] = jnp.zeros_like(dba_ref)
            dbx_ref[...] = jnp.zeros_like(dbx_ref)

        xr = xr_ref[...]
        hr = hr_ref[...]
        dz = dz_ref[...]
        gel, dgel = _gelu_and_grad(y_ref[...])
        dxy_ref[:, D:2 * D] = (dz * hr * dgel).astype(bf16)
        ra, ii, a, sq = gates_ref[0], gates_ref[1], gates_ref[2], gates_ref[3]
        sp = _softplus(-lam_ref[...])

        row = _row_ids((TM, D))
        _to_lane_tiles(m_sc, jnp.where(row == TM - 1, anext_sc[...], pltpu.roll(a, TM - 1, 0)))
        anext_sc[...] = a[0:1, :]
        _to_lane_tiles(g_sc, dz * gel)
        gcarry_sc[...] = _chain_scan(m_sc, g_sc, gcarry_sc[...], reverse=True)
        gg = _from_lane_tiles(g_sc)

        hlast = jnp.where(tile > 0, hrp_ref[7:8, :], 0.0)
        hprev = jnp.where(row == 0, hlast, pltpu.roll(hr, 1, 0))
        d_a = gg * hprev
        dsq = gg * ii * xr
        dii = gg * sq * xr
        dxr = gg * sq * ii
        dlog = d_a * a - dsq * (a * a / sq)
        dsp_sc[...] += jnp.sum(dlog * (-8.0 * ra), axis=0, keepdims=True)
        dpa = dlog * (-8.0 * sp) * ra * (1.0 - ra)
        dpx = dii * ii * (1.0 - ii)
        dba_ref[...] += jnp.sum(dpa, axis=0, keepdims=True)
        dbx_ref[...] += jnp.sum(dpx, axis=0, keepdims=True)
        dpab = dpa.astype(bf16)
        dpxb = dpx.astype(bf16)
        xrb = xr.astype(bf16)
        back = []
        for hh in range(NHEAD):
            cols = slice(hh * HD, (hh + 1) * HD)
            back.append(_nt_dot(dpab[:, cols], wa_ref[hh]) + _nt_dot(dpxb[:, cols], wx_ref[hh]))
            dwa_sc[hh] += _tn_dot(xrb[:, cols], dpab[:, cols])
            dwx_sc[hh] += _tn_dot(xrb[:, cols], dpxb[:, cols])
        dxr = dxr + jnp.concatenate(back, axis=1)

        dext_sc[0:TM, :] = dxr
        de = dext_sc[...]
        dx = cw_ref[KC4 - 1:KC4, :] * dxr
        for k in range(KC4 - 1):
            dx = dx + cw_ref[k:k + 1, :] * pltpu.roll(de, TM + 8 - (KC4 - 1 - k), 0)[0:TM]
        dext_sc[TM:TM + 8, :] = dxr[0:8]
        dxy_ref[:, 0:D] = dx.astype(bf16)

        x = x_ref[...]
        xext_sc[0:8, :] = jnp.where(tile > 0, xp_ref[...], 0.0)
        xext_sc[8:8 + TM, :] = x
        xe = xext_sc[...]
        dcw_ref[KC4 - 1:KC4, :] += jnp.sum(dxr * x, axis=0, keepdims=True)
        for k in range(KC4 - 1):
            xs = pltpu.roll(xe, KC4 - 1 - k, 0)[8:8 + TM]
            dcw_ref[k:k + 1, :] += jnp.sum(dxr * xs, axis=0, keepdims=True)
        dcb_ref[...] += jnp.sum(dxr, axis=0, keepdims=True)

        @pl.when(i == nt - 1)
        def _():
            for hh in range(NHEAD):
                for qc in range(NCHIP):
                    dwa_ref[qc, hh] = dwa_sc[hh, qc * hq:(qc + 1) * hq, :].astype(bf16)
                    dwx_ref[qc, hh] = dwx_sc[hh, qc * hq:(qc + 1) * hq, :].astype(bf16)
            dlam_ref[...] = -dsp_sc[...] * _sigmoid(-lam_ref[...])

    rev = lambda i: (nt - 1 - i, 0)
    vec = pl.BlockSpec((1, D), lambda i: (0, 0))
    prev8 = lambda i: jnp.maximum((nt - 1 - i) * ng - 1, 0)
    wblk = pl.BlockSpec((NHEAD, HD, HD), lambda i: (0, 0, 0))
    gblk = pl.BlockSpec((NCHIP, NHEAD, hq, HD), lambda i: (0, 0, 0, 0))
    return pl.pallas_call(
        body, name="rnn_bwd", grid=(nt,),
        in_specs=[
            pl.BlockSpec((TM, D), rev),
            pl.BlockSpec((TM, D), rev),
            pl.BlockSpec((TM, D), rev),
            pl.BlockSpec((8, D), lambda i: (prev8(i), 0)),
            pl.BlockSpec((TM, D), lambda i: (nt - 1 - i, 0)),
            pl.BlockSpec((8, D), lambda i: (prev8(i), 0)),
            pl.BlockSpec((TM, D), lambda i: (nt - 1 - i, 1)),
            pl.BlockSpec(memory_space=pl.ANY),
            pl.BlockSpec((KC4, D), lambda i: (0, 0)),
            wblk, pl.BlockSpec((4, TM, D), lambda i: (0, nt - 1 - i, 0)), wblk, vec,
        ],
        out_specs=[
            pl.BlockSpec((TM, 2 * D), lambda i: (nt - 1 - i, 0)),
            gblk, gblk,
            pl.BlockSpec((KC4, D), lambda i: (0, 0)),
            vec, vec, vec, vec,
        ],
        out_shape=[jax.ShapeDtypeStruct((t, NIN), bf16),
                   jax.ShapeDtypeStruct((NCHIP, NHEAD, hq, HD), bf16),
                   jax.ShapeDtypeStruct((NCHIP, NHEAD, hq, HD), bf16),
                   jax.ShapeDtypeStruct((KC4, D), f32),
                   jax.ShapeDtypeStruct((1, D), f32), jax.ShapeDtypeStruct((1, D), f32),
                   jax.ShapeDtypeStruct((1, D), f32), jax.ShapeDtypeStruct((1, D), f32)],
        scratch_shapes=[pltpu.VMEM((1, D), f32), pltpu.VMEM((1, D), f32),
                        pltpu.VMEM((TM + 8, D), f32), pltpu.VMEM((TM + 8, D), f32),
                        pltpu.VMEM((LT, TM, 128), f32), pltpu.VMEM((LT, TM, 128), f32),
                        pltpu.VMEM((NHEAD, HD, HD), f32), pltpu.VMEM((NHEAD, HD, HD), f32),
                        pltpu.VMEM((1, D), f32)],
        input_output_aliases={7: 0},
        compiler_params=_cp(1),
    )(dz, xr, hr, hr, proj, proj, proj, dproj, cw, wa, gates, wx, lam)


def _inproj_bwd(dproj, dh, h, g, win, after=None):
    t = h.shape[0]
    tn = NIN // NCHIP

    def body(dp_ref, dh_ref, h_ref, g_ref, w_ref, dhi_ref, dg_ref, db_ref):
        @pl.when(pl.program_id(0) == 0)
        def _():
            dg_ref[...] = jnp.zeros_like(dg_ref)
            db_ref[...] = jnp.zeros_like(db_ref)

        dn = None
        for s in range(NCHIP):
            dp = dp_ref[:, s * tn:(s + 1) * tn]
            part = _nt_dot(dp, w_ref[s])
            dn = part if dn is None else dn + part
            db_ref[s] += jnp.sum(dp.astype(f32), axis=0, keepdims=True)
        dhin, dg = _rms_bwd(dn, h_ref[...], g_ref[...])
        dhi_ref[...] = dh_ref[...] + dhin
        dg_ref[...] += dg

    rowd = pl.BlockSpec((TM, D), lambda i: (i, 0))
    vec = pl.BlockSpec((1, D), lambda i: (0, 0))
    body, in_specs, args = _ordered(
        body,
        [pl.BlockSpec((TM, NIN), lambda i: (i, 0)), rowd, rowd, vec, RESIDENT],
        (dproj, dh, h, g, win), after)
    return pl.pallas_call(
        body, name="inproj_bwd", grid=(t // TM,),
        in_specs=in_specs,
        out_specs=[rowd, vec, pl.BlockSpec((NCHIP, 1, tn), lambda i: (0, 0, 0))],
        out_shape=[jax.ShapeDtypeStruct((t, D), f32), jax.ShapeDtypeStruct((1, D), f32),
                   jax.ShapeDtypeStruct((NCHIP, 1, tn), f32)],
        compiler_params=_cp(1),
    )(*args)


def _ffn_gu_grad(n, dgate, dup, tag, after=None):
    half = _tn_matmul(n, dgate, D, FS, (NCHIP, D, FS), (None, D, FS), lambda k, nn, m: (nn, 0, 0),
                      tag + "_dwg", after=after)
    return _tn_matmul(n, dup, D, FS, (NCHIP, D, FS), (None, D, FS), lambda k, nn, m: (2 + nn, 0, 0),
                      tag + "_dwu", base=half)


def _ffn_down_grad(a, df, tag, after=None):
    return _tn_matmul(a, df, FS, D, (F, D), (FS, D), lambda k, nn, m: (k, 0), tag + "_dwd", after=after)


def _square_grad(a, b, name):
    return _tn_matmul(a, b, D, D, (D, D), (D, D), lambda k, nn, m: (0, 0), name)


ANY = pl.BlockSpec(memory_space=pl.ANY)


def _place():
    x, y, c = lax.axis_index("x"), lax.axis_index("y"), lax.axis_index("c")
    chips = [(1 - x, y), (x, 1 - y), (1 - x, 1 - y)]
    return x, y, c, chips


def _chip_id(chip):
    return 2 * chip[0] + chip[1]


def _cast_into_slot(w2d, qc, dtype, name, after=None):
    r, cc = w2d.shape
    hr = r // 2

    def body(qc_ref, *refs):
        del qc_ref
        w_ref, o_ref = refs[-2:]
        o_ref[...] = w_ref[...].astype(dtype)

    in_specs, args = [pl.BlockSpec((hr, cc), lambda h, qc_ref: (h, 0))], (w2d,)
    if after is not None:
        in_specs, args = [ANY_SPEC] + in_specs, (after,) + args
    return pl.pallas_call(
        body, name=name,
        grid_spec=pltpu.PrefetchScalarGridSpec(
            num_scalar_prefetch=1, grid=(2,),
            in_specs=in_specs,
            out_specs=pl.BlockSpec((None, None, hr, cc), lambda h, qc_ref: (qc_ref[0], h, 0, 0))),
        out_shape=jax.ShapeDtypeStruct((NCHIP, 2, hr, cc), dtype),
        compiler_params=_cp(1),
    )(qc, *args)


def _place_pack(pack, qc):
    def body(qc_ref, p_ref, o_ref):
        del qc_ref
        o_ref[...] = p_ref[...]

    return pl.pallas_call(
        body, name="place_pack",
        grid_spec=pltpu.PrefetchScalarGridSpec(
            num_scalar_prefetch=1, grid=(1,),
            in_specs=[pl.BlockSpec(pack.shape, lambda i, qc_ref: (0, 0))],
            out_specs=pl.BlockSpec((None,) + pack.shape, lambda i, qc_ref: (2 * qc_ref[0] + qc_ref[1], 0, 0))),
        out_shape=jax.ShapeDtypeStruct((8,) + pack.shape, pack.dtype),
        compiler_params=_cp(1),
    )(qc, pack)


def _pair_add(parts, gots, qc, name):
    n = len(parts)

    def body(qc_ref, *refs):
        s = pl.program_id(0)
        for a in range(n):
            val = (refs[a][...].astype(f32) + refs[n + a][...].astype(f32)).astype(bf16)
            refs[2 * n + a][...] = val

            @pl.when(s == qc_ref[0])
            def _(val=val, land_ref=refs[3 * n + a]):
                land_ref[...] = val

    shapes = [p.shape[2:] for p in parts]
    mine = [pl.BlockSpec((None, None) + sh, lambda s, qc_ref: (s, qc_ref[1], 0, 0)) for sh in shapes]
    block = [pl.BlockSpec((None,) + sh, lambda s, qc_ref: (s, 0, 0)) for sh in shapes]
    own = [pl.BlockSpec((None,) + sh, lambda s, qc_ref: (qc_ref[0], 0, 0)) for sh in shapes]
    outs = pl.pallas_call(
        body, name=name,
        grid_spec=pltpu.PrefetchScalarGridSpec(
            num_scalar_prefetch=1, grid=(NCHIP,), in_specs=mine + block, out_specs=block + own),
        out_shape=[jax.ShapeDtypeStruct((NCHIP,) + sh, bf16) for sh in shapes] * 2,
        compiler_params=_cp(1),
    )(qc, *parts, *gots)
    return list(outs[:n]), list(outs[n:])


def _sum_chips(gots, name):
    n = len(gots)

    def body(*refs):
        for a in range(n):
            acc = refs[a][0].astype(f32)
            for s in range(1, NCHIP):
                acc = acc + refs[a][s].astype(f32)
            refs[n + a][...] = acc

    return list(pl.pallas_call(
        body, name=name, grid=(1,),
        in_specs=[pl.BlockSpec(g.shape, lambda i: (0, 0, 0)) for g in gots],
        out_specs=[pl.BlockSpec(g.shape[1:], lambda i: (0, 0)) for g in gots],
        out_shape=[jax.ShapeDtypeStruct(g.shape[1:], f32) for g in gots],
        compiler_params=_cp(1),
    )(*gots))


def _pair_share(halves, name, after=None):
    n = len(halves)
    extra = () if after is None else (after,)

    def body(*refs):
        refs = refs[len(extra):]
        ins, outs = refs[:n], refs[n:2 * n]
        send_sems, recv_sems = refs[2 * n:]
        x, y, c, _ = _place()
        copies = []
        for a in range(n):
            cp = pltpu.make_async_remote_copy(
                src_ref=ins[a], dst_ref=outs[a], send_sem=send_sems.at[a], recv_sem=recv_sems.at[a],
                device_id=(x, y, 1 - c), device_id_type=MESH)
            cp.start()
            copies.append(cp)
        for cp in copies:
            cp.wait()

    return pl.pallas_call(
        body, name=name,
        in_specs=[ANY] * (len(extra) + n), out_specs=[ANY] * n,
        out_shape=[jax.ShapeDtypeStruct(s.shape, s.dtype) for s in halves],
        scratch_shapes=[pltpu.SemaphoreType.DMA((n,)), pltpu.SemaphoreType.DMA((n,))],
    )(*extra, *halves)


def _all_copy(buf_ref, send_ref, recv_ref, k, x, y, c, landing):
    px, py, pc = (1 - x if k & 4 else x, 1 - y if k & 2 else y, 1 - c if k & 1 else c)
    me = 4 * x + 2 * y + c
    there = 4 * px + 2 * py + pc
    return pltpu.make_async_remote_copy(
        src_ref=buf_ref.at[me], dst_ref=buf_ref.at[there if landing else me],
        send_sem=send_ref.at[k - 1], recv_sem=recv_ref.at[k - 1],
        device_id=(px, py, pc), device_id_type=MESH)


def _gather_all_start(buf, name):
    def body(in_ref, send, recv, thru, token):
        del thru
        x, y, c, _ = _place()
        for k in range(1, 8):
            _all_copy(in_ref, send, recv, k, x, y, c, False).start()
        token[...] = jnp.zeros_like(token)

    return pl.pallas_call(
        body, name=name,
        in_specs=[HBM],
        out_specs=[SEM, SEM, HBM, pl.BlockSpec(memory_space=pltpu.VMEM)],
        out_shape=[pltpu.SemaphoreType.DMA((7,)), pltpu.SemaphoreType.DMA((7,)),
                   pltpu.HBM(buf.shape, buf.dtype), jax.ShapeDtypeStruct((8, 128), f32)],
        input_output_aliases={0: 2},
        compiler_params=pltpu.CompilerParams(has_side_effects=EFFECT),
    )(_in_hbm(buf))


def _gather_all_wait(send, recv, buf, after, name):
    def body(in_ref, send_r, recv_r, after_ref, out_ref):
        del after_ref, out_ref
        x, y, c, _ = _place()
        for k in range(1, 8):
            cp = _all_copy(in_ref, send_r, recv_r, k, x, y, c, True)
            cp.wait_send()
            cp.wait_recv()

    return pl.pallas_call(
        body, name=name,
        in_specs=[HBM, SEM, SEM, ANY],
        out_specs=HBM,
        out_shape=pltpu.HBM(buf.shape, buf.dtype),
        input_output_aliases={0: 0},
        compiler_params=pltpu.CompilerParams(has_side_effects=EFFECT),
    )(buf, send, recv, after)


HBM = pl.BlockSpec(memory_space=pltpu.HBM)
SEM = pl.BlockSpec(memory_space=pltpu.SEMAPHORE)
EFFECT = pltpu.SideEffectType.DATAFLOW_SIDE_EFFECTING
N_PEER = 3


def _in_hbm(a):
    return pltpu.with_memory_space_constraint(a, pltpu.HBM)


def _gather_copy(buf_ref, send_ref, recv_ref, j, chip, q, c, landing_chip):
    return pltpu.make_async_remote_copy(
        src_ref=buf_ref.at[q, c], dst_ref=buf_ref.at[landing_chip, c],
        send_sem=send_ref.at[j], recv_sem=recv_ref.at[j],
        device_id=(chip[0], chip[1], c), device_id_type=MESH)


def _gather_start(bufs, name):
    n = len(bufs)

    def body(*refs):
        ins = refs[:n]
        send, recv = refs[n:2 * n], refs[2 * n:3 * n]
        token = refs[4 * n]
        x, y, c, chips = _place()
        q = 2 * x + y
        for a in range(n):
            for j, chip in enumerate(chips):
                _gather_copy(ins[a], send[a], recv[a], j, chip, q, c, q).start()
        token[...] = jnp.zeros_like(token)

    sems = [pltpu.SemaphoreType.DMA((N_PEER,))] * (2 * n)
    outs = pl.pallas_call(
        body, name=name,
        in_specs=[HBM] * n,
        out_specs=[SEM] * (2 * n) + [HBM] * n + [pl.BlockSpec(memory_space=pltpu.VMEM)],
        out_shape=sems + [pltpu.HBM(b.shape, b.dtype) for b in bufs] + [jax.ShapeDtypeStruct((8, 128), f32)],
        input_output_aliases={a: 2 * n + a for a in range(n)},
        compiler_params=pltpu.CompilerParams(has_side_effects=EFFECT),
    )(*[_in_hbm(b) for b in bufs])
    return list(outs[:n]), list(outs[n:2 * n]), list(outs[2 * n:3 * n]), outs[3 * n]


def _gather_wait(send, recv, bufs, after, name):
    n = len(bufs)

    def body(*refs):
        ins = refs[:n]
        send_r, recv_r = refs[n:2 * n], refs[2 * n:3 * n]
        x, y, c, chips = _place()
        q = 2 * x + y
        for a in range(n):
            for j, chip in enumerate(chips):
                cp = _gather_copy(ins[a], send_r[a], recv_r[a], j, chip, q, c, _chip_id(chip))
                cp.wait_send()
                cp.wait_recv()

    afters = after if isinstance(after, (tuple, list)) else (after,)
    outs = pl.pallas_call(
        body, name=name,
        in_specs=[HBM] * n + [SEM] * (2 * n) + [ANY] * len(afters),
        out_specs=[HBM] * n,
        out_shape=[pltpu.HBM(b.shape, b.dtype) for b in bufs],
        input_output_aliases={a: a for a in range(n)},
        compiler_params=pltpu.CompilerParams(has_side_effects=EFFECT),
    )(*bufs, *send, *recv, *afters)
    return list(outs)


def _forward_halves(bufs, name):
    n = len(bufs)

    def body(*refs):
        outs = refs[n:2 * n]
        send_sems, recv_sems = refs[2 * n:]
        x, y, c, chips = _place()
        sibling = (x, y, 1 - c)

        def remote(a, j, blk):
            return pltpu.make_async_remote_copy(src_ref=blk, dst_ref=blk, send_sem=send_sems.at[a, j],
                                                recv_sem=recv_sems.at[a, j], device_id=sibling,
                                                device_id_type=MESH)

        sent = []
        for a in range(n):
            for j, chip in enumerate(chips):
                cp = remote(a, j, outs[a].at[_chip_id(chip), c])
                cp.start()
                sent.append(cp)
        for a in range(n):
            for j, chip in enumerate(chips):
                remote(a, j, outs[a].at[_chip_id(chip), 1 - c]).wait_recv()
        for cp in sent:
            cp.wait_send()

    return pl.pallas_call(
        body, name=name,
        in_specs=[ANY] * n, out_specs=[ANY] * n,
        out_shape=[jax.ShapeDtypeStruct(s.shape, s.dtype) for s in bufs],
        scratch_shapes=[pltpu.SemaphoreType.DMA((n, N_PEER)), pltpu.SemaphoreType.DMA((n, N_PEER))],
        input_output_aliases={a: a for a in range(n)},
    )(*bufs)


def _reduce_copy(sum_ref, land_ref, send_ref, recv_ref, j, chip, q, c, landing_chip):
    return pltpu.make_async_remote_copy(
        src_ref=sum_ref.at[_chip_id(chip)], dst_ref=land_ref.at[landing_chip],
        send_sem=send_ref.at[j], recv_sem=recv_ref.at[j],
        device_id=(chip[0], chip[1], c), device_id_type=MESH)


def _reduce_start(sums, lands, name):
    n = len(sums)

    def body(*refs):
        s_in, l_in = refs[:n], refs[n:2 * n]
        send, recv = refs[2 * n:3 * n], refs[3 * n:4 * n]
        token = refs[6 * n]
        x, y, c, chips = _place()
        q = 2 * x + y
        for a in range(n):
            for j, chip in enumerate(chips):
                _reduce_copy(s_in[a], l_in[a], send[a], recv[a], j, chip, q, c, q).start()
        token[...] = jnp.zeros_like(token)

    sems = [pltpu.SemaphoreType.DMA((N_PEER,))] * (2 * n)
    outs = pl.pallas_call(
        body, name=name,
        in_specs=[HBM] * (2 * n),
        out_specs=[SEM] * (2 * n) + [HBM] * (2 * n) + [pl.BlockSpec(memory_space=pltpu.VMEM)],
        out_shape=sems + [pltpu.HBM(b.shape, b.dtype) for b in list(sums) + list(lands)]
        + [jax.ShapeDtypeStruct((8, 128), f32)],
        input_output_aliases={a: 2 * n + a for a in range(2 * n)},
        compiler_params=pltpu.CompilerParams(has_side_effects=EFFECT),
    )(*[_in_hbm(b) for b in list(sums) + list(lands)])
    return (list(outs[:n]), list(outs[n:2 * n]), list(outs[2 * n:3 * n]), list(outs[3 * n:4 * n]),
            outs[4 * n])


def _reduce_wait(send, recv, sums, lands, after, name):
    n = len(sums)

    def body(*refs):
        s_in, l_in = refs[:n], refs[n:2 * n]
        send_r, recv_r = refs[2 * n:3 * n], refs[3 * n:4 * n]
        x, y, c, chips = _place()
        q = 2 * x + y
        for a in range(n):
            for j, chip in enumerate(chips):
                cp = _reduce_copy(s_in[a], l_in[a], send_r[a], recv_r[a], j, chip, q, c, _chip_id(chip))
                cp.wait_send()
                cp.wait_recv()

    afters = after if isinstance(after, (tuple, list)) else (after,)
    outs = pl.pallas_call(
        body, name=name,
        in_specs=[HBM] * (2 * n) + [SEM] * (2 * n) + [ANY] * len(afters),
        out_specs=[HBM] * (2 * n),
        out_shape=[pltpu.HBM(b.shape, b.dtype) for b in list(sums) + list(lands)],
        input_output_aliases={a: a for a in range(2 * n)},
        compiler_params=pltpu.CompilerParams(has_side_effects=EFFECT),
    )(*sums, *lands, *send, *recv, *afters)
    return list(outs[n:])


def _sibling_copy(part_ref, land_ref, send_ref, recv_ref, x, y, c):
    return pltpu.make_async_remote_copy(
        src_ref=part_ref.at[:, 1 - c], dst_ref=land_ref, send_sem=send_ref.at[0], recv_sem=recv_ref.at[0],
        device_id=(x, y, 1 - c), device_id_type=MESH)


def _pair_exchange_start(parts, name):
    n = len(parts)
    lands = [lax.empty((NCHIP,) + p.shape[2:], p.dtype) for p in parts]

    def body(*refs):
        p_in, l_in = refs[:n], refs[n:2 * n]
        send, recv = refs[2 * n:3 * n], refs[3 * n:4 * n]
        token = refs[6 * n]
        x, y, c, _ = _place()
        for a in range(n):
            _sibling_copy(p_in[a], l_in[a], send[a], recv[a], x, y, c).start()
        token[...] = jnp.zeros_like(token)

    sems = [pltpu.SemaphoreType.DMA((1,))] * (2 * n)
    outs = pl.pallas_call(
        body, name=name,
        in_specs=[HBM] * (2 * n),
        out_specs=[SEM] * (2 * n) + [HBM] * (2 * n) + [pl.BlockSpec(memory_space=pltpu.VMEM)],
        out_shape=sems + [pltpu.HBM(b.shape, b.dtype) for b in list(parts) + lands]
        + [jax.ShapeDtypeStruct((8, 128), f32)],
        input_output_aliases={a: 2 * n + a for a in range(2 * n)},
        compiler_params=pltpu.CompilerParams(has_side_effects=EFFECT),
    )(*[_in_hbm(b) for b in list(parts) + lands])
    return (list(outs[:n]), list(outs[n:2 * n]), list(outs[2 * n:3 * n]), list(outs[3 * n:4 * n]),
            outs[4 * n])


def _pair_exchange_wait(send, recv, parts, lands, after, name):
    n = len(parts)

    def body(*refs):
        p_in, l_in = refs[:n], refs[n:2 * n]
        send_r, recv_r = refs[2 * n:3 * n], refs[3 * n:4 * n]
        x, y, c, _ = _place()
        for a in range(n):
            cp = _sibling_copy(p_in[a], l_in[a], send_r[a], recv_r[a], x, y, c)
            cp.wait_send()
            cp.wait_recv()

    outs = pl.pallas_call(
        body, name=name,
        in_specs=[HBM] * (2 * n) + [SEM] * (2 * n) + [ANY],
        out_specs=[HBM] * (2 * n),
        out_shape=[pltpu.HBM(b.shape, b.dtype) for b in list(parts) + list(lands)],
        input_output_aliases={a: a for a in range(2 * n)},
        compiler_params=pltpu.CompilerParams(has_side_effects=EFFECT),
    )(*parts, *lands, *send, *recv, after)
    return list(outs[:n]), list(outs[n:])


def _forward_copy(buf_ref, send_ref, recv_ref, j, chip, x, y, c, landing):
    return pltpu.make_async_remote_copy(
        src_ref=buf_ref.at[_chip_id(chip), c], dst_ref=buf_ref.at[_chip_id(chip), 1 - c if landing else c],
        send_sem=send_ref.at[j], recv_sem=recv_ref.at[j], device_id=(x, y, 1 - c), device_id_type=MESH)


def _forward_start(bufs, name):
    n = len(bufs)

    def body(*refs):
        ins = refs[:n]
        send, recv = refs[n:2 * n], refs[2 * n:3 * n]
        token = refs[4 * n]
        x, y, c, chips = _place()
        for a in range(n):
            for j, chip in enumerate(chips):
                _forward_copy(ins[a], send[a], recv[a], j, chip, x, y, c, False).start()
        token[...] = jnp.zeros_like(token)

    sems = [pltpu.SemaphoreType.DMA((N_PEER,))] * (2 * n)
    outs = pl.pallas_call(
        body, name=name,
        in_specs=[HBM] * n,
        out_specs=[SEM] * (2 * n) + [HBM] * n + [pl.BlockSpec(memory_space=pltpu.VMEM)],
        out_shape=sems + [pltpu.HBM(b.shape, b.dtype) for b in bufs] + [jax.ShapeDtypeStruct((8, 128), f32)],
        input_output_aliases={a: 2 * n + a for a in range(n)},
        compiler_params=pltpu.CompilerParams(has_side_effects=EFFECT),
    )(*[_in_hbm(b) for b in bufs])
    return list(outs[:n]), list(outs[n:2 * n]), list(outs[2 * n:3 * n]), outs[3 * n]


def _forward_wait(send, recv, bufs, after, name):
    n = len(bufs)

    def body(*refs):
        ins = refs[:n]
        send_r, recv_r = refs[n:2 * n], refs[2 * n:3 * n]
        x, y, c, chips = _place()
        for a in range(n):
            for j, chip in enumerate(chips):
                cp = _forward_copy(ins[a], send_r[a], recv_r[a], j, chip, x, y, c, True)
                cp.wait_send()
                cp.wait_recv()

    outs = pl.pallas_call(
        body, name=name,
        in_specs=[HBM] * n + [SEM] * (2 * n) + [ANY],
        out_specs=[HBM] * n,
        out_shape=[pltpu.HBM(b.shape, b.dtype) for b in bufs],
        input_output_aliases={a: a for a in range(n)},
        compiler_params=pltpu.CompilerParams(has_side_effects=EFFECT),
    )(*bufs, *send, *recv, after)
    return list(outs)


def _adamw_math(w, g, m, v):
    m = ADAM_B1 * m + (1.0 - ADAM_B1) * g
    v = ADAM_B2 * v + (1.0 - ADAM_B2) * (g * g)
    m_hat = m / (1.0 - ADAM_B1 ** ADAM_STEP)
    v_hat = v / (1.0 - ADAM_B2 ** ADAM_STEP)
    delta = -ADAM_LR * (m_hat / (jnp.sqrt(v_hat) + ADAM_EPS) + ADAM_WD * w)
    return delta, m, v


ADAMW_BLOCK_BYTES = 3 << 19


def _adamw(ws, mines, theirs, ms, vs, qc, name):
    n = len(ws)
    halves = [w.shape[0] // 2 for w in ws]
    nb = next(k for k in range(1, min(halves) + 1)
              if all(hr % k == 0 and (hr // k) % 8 == 0 and (hr // k) * w.shape[1] * 4 <= ADAMW_BLOCK_BYTES
                     for hr, w in zip(halves, ws)))

    def body(qc_ref, *refs):
        mine_here = pl.program_id(0) == qc_ref[1]
        for a in range(n):
            w_ref, a_ref, b_ref, m_ref, v_ref = (refs[k * n + a] for k in range(5))
            g_ref, d_ref, mo_ref, vo_ref = (refs[(5 + k) * n + a] for k in range(4))
            g = jnp.where(mine_here, a_ref[...], b_ref[...])
            g_ref[...] = g
            d_ref[...], mo_ref[...], vo_ref[...] = _adamw_math(w_ref[...], g, m_ref[...], v_ref[...])

    blocks = [(hr // nb, w.shape[1]) for hr, w in zip(halves, ws)]
    full = [pl.BlockSpec(b, lambda h, i, qc_ref: (h * nb + i, 0)) for b in blocks]
    half = [pl.BlockSpec(b, lambda h, i, qc_ref: (i, 0)) for b in blocks]
    outs = pl.pallas_call(
        body, name=name,
        grid_spec=pltpu.PrefetchScalarGridSpec(
            num_scalar_prefetch=1, grid=(2, nb),
            in_specs=full + half + half + full + full, out_specs=full * 4),
        out_shape=[jax.ShapeDtypeStruct(w.shape, f32) for w in ws] * 4,
        compiler_params=_cp(2),
    )(qc, *ws, *mines, *theirs, *ms, *vs)
    return [tuple(outs[k * n + a] for k in range(4)) for a in range(n)]


REPL = [("ffn1_norm", 1), ("mix_norm", 1), ("b_in", 6), ("rnn_conv_b", 1), ("rg_b_a", 1), ("rg_b_x", 1),
        ("rg_lambda", 1), ("conv_dw_b", 1), ("conv_ln_g", 1), ("conv_ln_b", 1), ("conv_b_proj", 1),
        ("ffn2_norm", 1), ("final_norm", 1)]
COLSH = [("meta_tokens", NMETA), ("rnn_conv_w", KC4), ("conv_dw_w", KC31)]
SMALL = REPL + COLSH
CS = D // NCHIP


def _pack_rows():
    starts, row = {}, 0
    for k, rows in REPL:
        starts[k] = row
        row += rows
    for k, rows in COLSH:
        row = -(-row // 8) * 8
        starts[k] = row
        row += rows
    return starts, -(-row // 8) * 8


PACK_START, LOSS_ROW = _pack_rows()
SMALL_ROWS = LOSS_ROW + 8


def _small_pack(g, loss_row):
    pieces, row = [], 0
    for k, rows in SMALL:
        if PACK_START[k] > row:
            pieces.append(jnp.zeros((PACK_START[k] - row, D), f32))
        pieces.append(g[k].reshape(rows, D))
        row = PACK_START[k] + rows
    pieces.append(jnp.zeros((LOSS_ROW - row, D), f32))
    pieces.append(loss_row)
    pieces.append(jnp.zeros((SMALL_ROWS - LOSS_ROW - 1, D), f32))
    return jnp.concatenate(pieces, axis=0)


def _adamw_small(packs, ws, ms, vs):
    ns = len(SMALL)

    def body(*refs):
        pack_ref = refs[0]
        w_refs, m_refs, v_refs = refs[1:1 + ns], refs[1 + ns:1 + 2 * ns], refs[1 + 2 * ns:1 + 3 * ns]
        outs = refs[1 + 3 * ns:1 + 7 * ns]
        g_refs, d_refs, mo_refs, vo_refs = outs[:ns], outs[ns:2 * ns], outs[2 * ns:3 * ns], outs[3 * ns:]
        loss_ref = refs[1 + 7 * ns]
        gsum_sc = refs[2 + 7 * ns]
        q = 2 * lax.axis_index("x") + lax.axis_index("y")
        acc = pack_ref[0]
        for dev in range(1, 8):
            acc = acc + pack_ref[dev]
        gsum_sc[...] = acc
        loss_ref[...] = gsum_sc[LOSS_ROW:LOSS_ROW + 1, :]
        for idx, (name, rows) in enumerate(SMALL):
            row = PACK_START[name]
            if idx < len(REPL):
                for k in range(rows):
                    cols = slice(k * D, (k + 1) * D)
                    g = gsum_sc[row + k:row + k + 1, :]
                    d, mm, vv = _adamw_math(w_refs[idx][:, cols], g, m_refs[idx][:, cols], v_refs[idx][:, cols])
                    g_refs[idx][:, cols] = g
                    d_refs[idx][:, cols] = d
                    mo_refs[idx][:, cols] = mm
                    vo_refs[idx][:, cols] = vv
            else:
                g = gsum_sc[row:row + rows, pl.ds(pl.multiple_of(q * CS, CS), CS)]
                d, mm, vv = _adamw_math(w_refs[idx][...], g, m_refs[idx][...], v_refs[idx][...])
                g_refs[idx][...] = g
                d_refs[idx][...] = d
                mo_refs[idx][...] = mm
                vo_refs[idx][...] = vv

    shapes = [jax.ShapeDtypeStruct(w.shape, f32) for w in ws]
    return pl.pallas_call(
        body, name="adamw_small",
        out_shape=shapes * 4 + [jax.ShapeDtypeStruct((1, D), f32)],
        scratch_shapes=[pltpu.VMEM((SMALL_ROWS, D), f32)],
        compiler_params=pltpu.CompilerParams(vmem_limit_bytes=VMEM_LIMIT),
    )(packs, *ws, *ms, *vs)


WEIGHTS = ['meta_tokens', 'ffn1_norm', 'ffn1_w_gu', 'ffn1_w_down', 'mix_norm', 'w_in', 'b_in', 'rnn_conv_w',
           'rnn_conv_b', 'rg_w_a', 'rg_b_a', 'rg_w_x', 'rg_b_x', 'rg_lambda', 'rnn_w_proj', 'conv_dw_w',
           'conv_dw_b', 'conv_ln_g', 'conv_ln_b', 'conv_w_proj', 'conv_b_proj', 'w_out', 'ffn2_norm',
           'ffn2_w_gu', 'ffn2_w_down', 'final_norm']


def _as2d(a):
    return a.reshape(-1, a.shape[-1])


def _step(x, loss_target, w, m, v):
    seq = x.shape[1]
    n_valid = NMETA + seq
    t = -(-n_valid // TM) * TM

    qc = jnp.stack([2 * lax.axis_index("x") + lax.axis_index("y"), lax.axis_index("c")]).astype(jnp.int32)
    p = {k: w[k].reshape(1, rows * D) for k, rows in REPL}

    first = ["ffn1_w_gu", "ffn1_w_down", "small"]
    later = [["w_in"], ["rg_w_a", "rg_w_x", "rnn_w_proj", "conv_w_proj", "w_out"], ["ffn2_w_gu", "ffn2_w_down"]]
    small_rows = sum(r for _, r in COLSH)
    small = jnp.concatenate([_as2d(w[k]) for k, _ in COLSH] + [jnp.zeros((64 - small_rows, CS), f32)], axis=0)

    def cast(k, token=None):
        src, dtype = (small, f32) if k == "small" else (_as2d(w[k]), bf16)
        return _cast_into_slot(src, qc, dtype, "cast_" + k, after=token)

    send1, recv1, bufs1, token1 = _gather_start([cast(k) for k in first], "gather_start_first")
    rest = [k for grp in later for k in grp]
    send2, recv2, bufs2, token2 = _gather_start([cast(k, token1) for k in rest], "gather_start_rest")

    def install(names, done):
        for k, b in zip(names, done):
            full = b.reshape(NCHIP, 2 * b.shape[2], b.shape[3])
            if k in ("ffn1_w_down", "ffn2_w_down"):
                full = full.reshape(F, D)
            elif k in ("rnn_w_proj", "conv_w_proj", "w_out"):
                full = full.reshape(D, D)
            elif k in ("rg_w_a", "rg_w_x"):
                full = full.reshape(NCHIP, NHEAD, HD // NCHIP, HD).transpose(1, 0, 2, 3).reshape(NHEAD, HD, HD)
            p[k] = full

    def finish(names, send, recv, bufs, after, tag):
        install(names, _forward_halves(_gather_wait(send, recv, bufs, after, "gather_wait_" + tag),
                                       "gather_forward_" + tag))

    def group(names):
        idx = [rest.index(k) for k in names]
        return names, [send2[i] for i in idx], [recv2[i] for i in idx], [bufs2[i] for i in idx]

    h0 = jnp.pad(x[0] + token1[0:1, 0:1], ((NMETA, t - n_valid), (0, 0)))
    tgt = jnp.pad(loss_target[0] + token2[0:1, 0:1], ((NMETA, t - n_valid), (0, 0)))
    finish(first, send1, recv1, bufs1, (token2, h0, tgt), "first")
    small_full = p.pop("small").transpose(1, 0, 2).reshape(64, D)
    row = 0
    for k, rows in COLSH:
        p[k] = small_full[row:row + rows]
        row += rows

    h0 = lax.dynamic_update_slice(h0, p["meta_tokens"], (0, 0))
    h1, gate1, up1, n1 = _ffn_fwd(h0, p["ffn1_norm"], p["ffn1_w_gu"], p["ffn1_w_down"], "ffn1_fwd")
    finish(*group(later[0]), h1, "in")
    proj, n2 = _inproj_fwd(h1, p["mix_norm"], p["w_in"], p["b_in"])
    names_l = later[1] + later[2]
    _, send_l, recv_l, bufs_l = group(names_l)
    send_f, recv_f, bufs_f, token = _forward_start(
        _gather_wait(send_l, recv_l, bufs_l, proj, "gather_wait_late"), "gather_forward_start")
    vc, s = _conv_fwd(proj, p["conv_dw_w"], p["conv_dw_b"], p["conv_ln_g"], p["conv_ln_b"], after=token)
    install(names_l, _forward_wait(send_f, recv_f, bufs_f, vc, "gather_forward_wait"))
    xr, hr, z, gates = _rnn_fwd(proj, p["rnn_conv_w"], p["rnn_conv_b"], p["rg_w_a"], p["rg_b_a"],
                         p["rg_w_x"], p["rg_b_x"], p["rg_lambda"])
    h2 = _merge_fwd(h1, z, s, proj, p["rnn_w_proj"], p["conv_w_proj"], p["conv_b_proj"], p["w_out"])
    dh3, loss_blk, d_final, gate2, up2, n3 = _ffn_fwd(
        h2, p["ffn2_norm"], p["ffn2_w_gu"], p["ffn2_w_down"], "ffn2_fwd",
        loss_head=(p["final_norm"], tgt, n_valid))

    g = {"final_norm": d_final}
    pending = []

    def exchange_start(names, tag):
        parts = []
        for k in names:
            rows = g[k].size // (NCHIP * g[k].shape[-1])
            parts.append(g[k].reshape((NCHIP, 2, rows // 2, g[k].shape[-1])))
        send, recv, parts, lands, token = _pair_exchange_start(parts, "pair_exchange_start_" + tag)
        return (names, tag, send, recv, parts, lands), token

    def reduce_start(state, after):
        names, tag, send, recv, parts, lands = state
        parts, from_sibling = _pair_exchange_wait(send, recv, parts, lands, after, "pair_exchange_wait_" + tag)
        sums, lands = _pair_add(parts, from_sibling, qc, "pair_add_" + tag)
        send, recv, sums, lands, token = _reduce_start(sums, lands, "reduce_start_" + tag)
        pending.append((names, tag, send, recv, sums, lands))
        return token

    dh2, dgate2, dup2, a2, df2, g["ffn2_norm"] = _ffn_bwd(
        dh3, h2, p["ffn2_norm"], gate2, up2, p["ffn2_w_gu"], p["ffn2_w_down"], "ffn2_bwd")
    g["ffn2_w_gu"] = _ffn_gu_grad(n3, dgate2, dup2, "ffn2")
    g["ffn2_w_down"] = _ffn_down_grad(a2, df2, "ffn2")
    state, token = exchange_start(["ffn2_w_gu", "ffn2_w_down"], "ffn2")

    dz, ds, dproj, dh2b, merged, dya, dyb, g["conv_b_proj"] = _merge_bwd(
        dh2, z, s, proj, p["rnn_w_proj"], p["conv_w_proj"], p["conv_b_proj"], p["w_out"], after=token)
    token = reduce_start(state, dz)
    dproj, g["conv_dw_w"], g["conv_dw_b"], g["conv_ln_g"], g["conv_ln_b"] = _conv_bwd(
        ds, vc, proj, dproj, p["conv_dw_w"], p["conv_ln_g"], p["conv_ln_b"], after=token)
    g["w_out"] = _square_grad(merged, dh2b, "dw_out")
    g["rnn_w_proj"] = _square_grad(z, dya, "dw_rnn_proj")
    g["conv_w_proj"] = _square_grad(s, dyb, "dw_conv_proj")
    (dproj, g["rg_w_a"], g["rg_w_x"], g["rnn_conv_w"], g["rnn_conv_b"], g["rg_b_a"], g["rg_b_x"],
     g["rg_lambda"]) = _rnn_bwd(dz, xr, hr, gates, proj, dproj, p["rnn_conv_w"], p["rg_w_a"],
                                p["rg_w_x"], p["rg_lambda"])

    dh1, g["mix_norm"], db_in = _inproj_bwd(dproj, dh2, h1, p["mix_norm"], p["w_in"])
    g["b_in"] = db_in.reshape(1, NIN)
    g["w_in"] = _tn_matmul(n2, dproj, D, NIN // NCHIP, (NCHIP, D, NIN // NCHIP),
                           (None, D, NIN // NCHIP), lambda k, nn, mm: (nn, 0, 0), "dw_in")
    state, token = exchange_start(["w_out", "rnn_w_proj", "conv_w_proj", "rg_w_a", "rg_w_x", "w_in"], "mix")

    dh0, dgate1, dup1, a1, df1, g["ffn1_norm"] = _ffn_bwd(
        dh1, h0, p["ffn1_norm"], gate1, up1, p["ffn1_w_gu"], p["ffn1_w_down"], "ffn1_bwd", after=token)
    g["meta_tokens"] = dh0[0:NMETA]
    grad_x = dh0[NMETA:n_valid][None]
    token = reduce_start(state, dh0)

    send_s, recv_s, pack_buf, token_s = _gather_all_start(
        _place_pack(_small_pack(g, loss_blk.reshape(1, D)), qc), "gather_all_start")
    g["ffn1_w_down"] = _ffn_down_grad(a1, df1, "ffn1", after=(token, token_s))
    state, token = exchange_start(["ffn1_w_down"], "ffn1_down")
    gate_half = _tn_matmul(n1, dgate1, D, FS, (NCHIP, D, FS), (None, D, FS), lambda k, nn, mm: (nn, 0, 0),
                           "ffn1_dwg", after=token)
    token = reduce_start(state, gate_half)
    g["ffn1_w_gu"] = _tn_matmul(n1, dup1, D, FS, (NCHIP, D, FS), (None, D, FS), lambda k, nn, mm: (2 + nn, 0, 0),
                                "ffn1_dwu", base=gate_half, after=token)
    state_gu, token = exchange_start(["ffn1_w_gu"], "ffn1_gu")
    packs = _gather_all_wait(send_s, recv_s, pack_buf, token, "gather_all_wait")

    grads, deltas, new_m, new_v = {}, {}, {}, {}

    def landed_sums(items, after):
        names, mine = [], []
        for grp_names, grp_tag, send, recv, sums, lands in items:
            landed = _reduce_wait(send, recv, sums, lands, after, "reduce_wait_" + grp_tag)
            mine += _sum_chips(landed, "sum_chips_" + grp_tag)
            names += grp_names
            after = mine[-1]
        return names, mine

    def share_and_update(names, mine, tag, after=None):
        theirs = _pair_share(mine, "pair_share_" + tag, after=after)
        got = dict(zip(names, zip(mine, theirs)))
        square = [k for k in names if got[k][0].shape[0] * 2 <= HD]
        for batch in [[k] for k in names if k not in square] + ([square] if square else []):
            outs = _adamw([_as2d(w[k]) for k in batch], [got[k][0] for k in batch], [got[k][1] for k in batch],
                          [_as2d(m[k]) for k in batch], [_as2d(v[k]) for k in batch], qc,
                          "adamw_" + (batch[0] if len(batch) == 1 else "mixer"))
            for k, out in zip(batch, outs):
                grads[k], deltas[k], new_m[k], new_v[k] = (a.reshape(w[k].shape) for a in out)
        return [new_v[k] for k in names]

    early_names, early_mine = landed_sums(pending[:2], packs)
    token = reduce_start(state_gu, early_mine[-1])
    after = share_and_update(early_names, early_mine, "early", after=token)
    share_and_update(*landed_sums(pending[2:], after), "late")
    names = [k for k, _ in SMALL]
    shape2 = {k: ((1, rows * D) if (k, rows) in REPL else (rows, CS)) for k, rows in SMALL}
    outs = _adamw_small(packs, *[[a[k].reshape(shape2[k]) for k in names] for a in (w, m, v)])
    ns = len(names)
    for i, k in enumerate(names):
        grads[k], deltas[k], new_m[k], new_v[k] = (outs[j * ns + i].reshape(w[k].shape) for j in range(4))

    loss = outs[4 * ns][0, 0]
    return (loss, grad_x, *[grads[k] for k in WEIGHTS], *[deltas[k] for k in WEIGHTS],
            *[new_m[k] for k in WEIGHTS], *[new_v[k] for k in WEIGHTS])


def kernel(x, meta_tokens, ffn1_norm, ffn1_w_gu, ffn1_w_down, mix_norm, w_in, b_in, rnn_conv_w, rnn_conv_b, rg_w_a, rg_b_a, rg_w_x, rg_b_x, rg_lambda, rnn_w_proj, conv_dw_w, conv_dw_b, conv_ln_g, conv_ln_b, conv_w_proj, conv_b_proj, w_out, ffn2_norm, ffn2_w_gu, ffn2_w_down, final_norm, loss_target, m_meta_tokens, m_ffn1_norm, m_ffn1_w_gu, m_ffn1_w_down, m_mix_norm, m_w_in, m_b_in, m_rnn_conv_w, m_rnn_conv_b, m_rg_w_a, m_rg_b_a, m_rg_w_x, m_rg_b_x, m_rg_lambda, m_rnn_w_proj, m_conv_dw_w, m_conv_dw_b, m_conv_ln_g, m_conv_ln_b, m_conv_w_proj, m_conv_b_proj, m_w_out, m_ffn2_norm, m_ffn2_w_gu, m_ffn2_w_down, m_final_norm, v_meta_tokens, v_ffn1_norm, v_ffn1_w_gu, v_ffn1_w_down, v_mix_norm, v_w_in, v_b_in, v_rnn_conv_w, v_rnn_conv_b, v_rg_w_a, v_rg_b_a, v_rg_w_x, v_rg_b_x, v_rg_lambda, v_rnn_w_proj, v_conv_dw_w, v_conv_dw_b, v_conv_ln_g, v_conv_ln_b, v_conv_w_proj, v_conv_b_proj, v_w_out, v_ffn2_norm, v_ffn2_w_gu, v_ffn2_w_down, v_final_norm):
    args = locals()
    w = {k: args[k] for k in WEIGHTS}
    m = {k: args["m_" + k] for k in WEIGHTS}
    v = {k: args["v_" + k] for k in WEIGHTS}
    return _step(x, loss_target, w, m, v)
```

```python
import jax
import jax.numpy as jnp
from jax import lax
from jax.experimental import pallas as pl
from jax.experimental.pallas import tpu as pltpu

f32 = jnp.float32
bf16 = jnp.bfloat16

D = 1024
F = 2816
FS = F // 2
NIN = 6 * D
NMETA = 16
NHEAD = 4
HD = D // NHEAD
KC4 = 4
KC31 = 31
HALO = 32
EPS = 1e-6
TM = 416
NCHIP = 4
MESH = pl.DeviceIdType.MESH

ADAM_LR = 0.001
ADAM_B1 = 0.9
ADAM_B2 = 0.999
ADAM_EPS = 1e-08
ADAM_WD = 0.01
ADAM_STEP = 10

VMEM_LIMIT = 56 * 1024 * 1024
FSUB = [(o, min(256, FS - o)) for o in range(0, FS, 256)]


def _cp(n_axes, **kw):
    return pltpu.CompilerParams(dimension_semantics=("arbitrary",) * n_axes,
                                vmem_limit_bytes=VMEM_LIMIT, **kw)


RESIDENT = pl.BlockSpec(memory_space=pltpu.VMEM)


def _n_after(after):
    return 0 if after is None else (len(after) if isinstance(after, (tuple, list)) else 1)


def _ordered(body, in_specs, args, after):
    if after is None:
        return body, in_specs, args
    extra = tuple(after) if isinstance(after, (tuple, list)) else (after,)
    return (lambda *refs: body(*refs[len(extra):]),
            [pl.BlockSpec(memory_space=pl.ANY)] * len(extra) + list(in_specs), extra + tuple(args))


def _nt_dot(a, b):
    return lax.dot_general(a, b, (((1,), (1,)), ((), ())), preferred_element_type=f32)


def _tn_dot(a, b):
    return lax.dot_general(a, b, (((0,), (0,)), ((), ())), preferred_element_type=f32)


def _sigmoid(x):
    return 0.5 * jnp.tanh(0.5 * x) + 0.5


def _log1p(y):
    u = 1.0 + y
    d = u - 1.0
    return jnp.where(d == 0.0, y, jnp.log(u) * (y / jnp.where(d == 0.0, 1.0, d)))


def _softplus(x):
    return jnp.maximum(x, 0.0) + _log1p(jnp.exp(-jnp.abs(x)))


def _one_minus_square(a, log_a):
    x = 2.0 * log_a
    series = x * (1.0 + x * (0.5 + x * (1.0 / 6.0)))
    return jnp.where(jnp.abs(x) < 0.03, -series, 1.0 - a * a)


_GELU_C = 0.7978845608028654
_GELU_K = 0.044715


def _gelu_and_grad(y):
    y2 = y * y
    th = jnp.tanh(_GELU_C * (y + _GELU_K * y * y2))
    gel = 0.5 * y * (1.0 + th)
    dgel = 0.5 * (1.0 + th) + 0.5 * y * (1.0 - th * th) * _GELU_C * (1.0 + 3.0 * _GELU_K * y2)
    return gel, dgel


def _rms_stats(h):
    return lax.rsqrt(jnp.mean(h * h, axis=-1, keepdims=True) + EPS)


def _rms_bwd(dn, h, g):
    r = _rms_stats(h)
    nhat = h * r
    dnh = dn * g
    dh = r * (dnh - nhat * jnp.mean(dnh * nhat, axis=-1, keepdims=True))
    dg = jnp.sum(dn * nhat, axis=0, keepdims=True)
    return dh, dg


def _row_ids(shape):
    return lax.broadcasted_iota(jnp.int32, shape, 0)


def _ffn_fwd(h, g, wgu, wd, name, loss_head=None):
    t = h.shape[0]
    nj = 2
    tm = TM
    n_head = 0 if loss_head is None else 2

    def body(*refs):
        h_ref, g_ref, wg_ref, wd_ref = refs[:4]
        outs = refs[4 + n_head:]
        gate_ref, up_ref, a_ref, n_ref = outs[-4:]
        i = pl.program_id(0)
        hh = h_ref[...]
        nb = (hh * _rms_stats(hh) * g_ref[...]).astype(bf16)
        n_ref[...] = nb

        acc = None
        for j in range(nj):
            for off, width in FSUB:
                cols = slice(off, off + width)
                out_cols = slice(j * FS + off, j * FS + off + width)
                gt = jnp.dot(nb, wg_ref[j, :, cols], preferred_element_type=f32)
                up = jnp.dot(nb, wg_ref[2 + j, :, cols], preferred_element_type=f32)
                gate_ref[:, out_cols] = gt.astype(bf16)
                up_ref[:, out_cols] = up.astype(bf16)
                a_ref[:, out_cols] = (gt * _sigmoid(gt) * up).astype(bf16)
            part = jnp.dot(a_ref[:, j * FS:(j + 1) * FS], wd_ref[j], preferred_element_type=f32)
            acc = part if acc is None else acc + part
        hh = hh + 0.5 * acc

        if loss_head is None:
            outs[0][...] = hh
        else:
            gf_ref, t_ref = refs[4:6]
            dh_ref, loss_ref, dgf_ref = outs[:3]

            @pl.when(i == 0)
            def _():
                loss_ref[...] = jnp.zeros_like(loss_ref)
                dgf_ref[...] = jnp.zeros_like(dgf_ref)

            gf = gf_ref[...]
            row = i * tm + _row_ids((tm, 1))
            valid = jnp.logical_and(row >= NMETA, row < loss_head[2])
            err = jnp.where(valid, hh * _rms_stats(hh) * gf - t_ref[...], 0.0)
            loss_ref[...] += 0.5 * jnp.sum(err * err) * (1.0 / D)
            dh, dgf = _rms_bwd(err * (1.0 / D), hh, gf)
            dh_ref[...] = dh
            dgf_ref[...] += dgf

    rowd = pl.BlockSpec((tm, D), lambda i: (i, 0))
    vec = pl.BlockSpec((1, D), lambda i: (0, 0))
    rowf = pl.BlockSpec((tm, F), lambda i: (i, 0))
    in_specs, args = [rowd, vec, RESIDENT, RESIDENT], [h, g, wgu, wd.reshape(nj, FS, D)]
    out_specs, out_shape = [rowd], [jax.ShapeDtypeStruct((t, D), f32)]
    if loss_head is not None:
        in_specs, args = in_specs + [vec, rowd], args + [loss_head[0], loss_head[1]]
        out_specs += [pl.BlockSpec((8, 128), lambda i: (0, 0)), vec]
        out_shape += [jax.ShapeDtypeStruct((8, 128), f32), jax.ShapeDtypeStruct((1, D), f32)]
    return pl.pallas_call(
        body, name=name, grid=(t // tm,),
        in_specs=in_specs,
        out_specs=out_specs + [rowf, rowf, rowf, rowd],
        out_shape=out_shape + [jax.ShapeDtypeStruct((t, F), bf16)] * 3 + [jax.ShapeDtypeStruct((t, D), bf16)],
        compiler_params=_cp(1),
    )(*args)


def _inproj_fwd(h, g, win, b_in):
    t = h.shape[0]
    tn = NIN // NCHIP

    def body(h_ref, g_ref, w_ref, b_ref, proj_ref, n_ref):
        hh = h_ref[...]
        nb = (hh * _rms_stats(hh) * g_ref[...]).astype(bf16)
        n_ref[...] = nb
        for s in range(NCHIP):
            cols = slice(s * tn, (s + 1) * tn)
            proj_ref[:, cols] = jnp.dot(nb, w_ref[s], preferred_element_type=f32) + b_ref[:, cols]

    return pl.pallas_call(
        body, name="inproj_fwd", grid=(t // TM,),
        in_specs=[
            pl.BlockSpec((TM, D), lambda i: (i, 0)),
            pl.BlockSpec((1, D), lambda i: (0, 0)),
            RESIDENT,
            pl.BlockSpec((1, NIN), lambda i: (0, 0)),
        ],
        out_specs=[
            pl.BlockSpec((TM, NIN), lambda i: (i, 0)),
            pl.BlockSpec((TM, D), lambda i: (i, 0)),
        ],
        out_shape=[jax.ShapeDtypeStruct((t, NIN), f32), jax.ShapeDtypeStruct((t, D), bf16)],
        compiler_params=_cp(1),
    )(h, g, win, b_in)


def _block_gates(xr, wa_ref, ba, wx_ref, bx, lam):
    xrb = xr.astype(bf16)
    pa = jnp.concatenate([jnp.dot(xrb[:, hh * HD:(hh + 1) * HD], wa_ref[hh], preferred_element_type=f32)
                          for hh in range(NHEAD)], axis=1)
    px = jnp.concatenate([jnp.dot(xrb[:, hh * HD:(hh + 1) * HD], wx_ref[hh], preferred_element_type=f32)
                          for hh in range(NHEAD)], axis=1)
    ra = _sigmoid(pa + ba)
    ii = _sigmoid(px + bx)
    sp = _softplus(-lam)
    log_a = -8.0 * ra * sp
    a = jnp.exp(log_a)
    sq = jnp.sqrt(_one_minus_square(a, log_a))
    return ra, ii, a, sq, sp


LT = D // 128
UNR = 13


def _to_lane_tiles(ref, value):
    for lt in range(LT):
        ref[lt] = value[:, lt * 128:(lt + 1) * 128]


def _from_lane_tiles(ref):
    return jnp.concatenate([ref[lt] for lt in range(LT)], axis=1)


def _chain_scan(mult_sc, val_sc, start, reverse):
    ng = TM // 8

    def lanes(lt):
        return slice(lt * 128, (lt + 1) * 128)

    def chain(gi, carry):
        v_prev, p_prev = carry
        rows = pl.ds(ng - 1 - gi if reverse else gi, 8, stride=ng)
        v_new, p_new = [], []
        for lt in range(LT):
            mm = mult_sc.at[lt][rows, :]
            vv = mm * v_prev[:, lanes(lt)] + val_sc.at[lt][rows, :]
            pp = mm * p_prev[:, lanes(lt)]
            val_sc.at[lt][rows, :] = vv
            mult_sc.at[lt][rows, :] = pp
            v_new.append(vv)
            p_new.append(pp)
        return jnp.concatenate(v_new, axis=1), jnp.concatenate(p_new, axis=1)

    v_end, p_end = lax.fori_loop(0, ng, chain, (jnp.zeros((8, D), f32), jnp.ones((8, D), f32)))
    state, entries = start, [None] * 8
    for r in (reversed(range(8)) if reverse else range(8)):
        entries[r] = state
        state = v_end[r:r + 1, :] + p_end[r:r + 1, :] * state
    entry8 = jnp.concatenate(entries, axis=0)

    def add_entry(gi, carry):
        rows = pl.ds(gi, 8, stride=ng)
        for lt in range(LT):
            val_sc.at[lt][rows, :] = val_sc.at[lt][rows, :] + mult_sc.at[lt][rows, :] * entry8[:, lanes(lt)]
        return carry

    lax.fori_loop(0, ng, add_entry, 0)
    return state


def _strided_conv(w_ref, src_sc, out_sc, base, shifts, bias_ref=None):
    ng = TM // 8
    for lt in range(LT):
        lanes = slice(lt * 128, (lt + 1) * 128)
        taps = [jnp.broadcast_to(w_ref[k:k + 1, lanes], (8, 128)) for k in range(len(shifts))]
        init = (jnp.zeros((8, 128), f32) if bias_ref is None
                else jnp.broadcast_to(bias_ref[:, lanes], (8, 128)))

        def step(gb, carry, lt=lt, taps=taps, init=init):
            accs = [init] * UNR
            for k, shift in enumerate(shifts):
                for u in range(UNR):
                    rows = pl.ds(base + gb * UNR + u + shift, 8, stride=ng)
                    accs[u] = accs[u] + taps[k] * src_sc.at[lt][rows, :]
            for u in range(UNR):
                out_sc.at[lt][pl.ds(gb * UNR + u, 8, stride=ng), :] = accs[u]
            return carry

        lax.fori_loop(0, ng // UNR, step, 0)


def _strided_corr(a_sc, src_sc, base, shifts):
    ng = TM // 8
    per_tile = []
    for lt in range(LT):
        def step(gb, accs, lt=lt):
            accs = list(accs)
            for u in range(UNR):
                g = gb * UNR + u
                a_g = a_sc.at[lt][pl.ds(g, 8, stride=ng), :]
                for k, shift in enumerate(shifts):
                    accs[k] = accs[k] + a_g * src_sc.at[lt][pl.ds(base + g + shift, 8, stride=ng), :]
            return tuple(accs)

        per_tile.append(lax.fori_loop(0, ng // UNR, step, tuple(jnp.zeros((8, 128), f32) for _ in shifts)))
    return [jnp.concatenate([per_tile[lt][k] for lt in range(LT)], axis=1) for k in range(len(shifts))]


def _rnn_fwd(proj, cw, cb, wa, ba, wx, bx, lam):
    t = proj.shape[0]

    def body(x_ref, y_ref, cw_ref, cb_ref, wa_ref, ba_ref, wx_ref, bx_ref, lam_ref,
             xr_ref, hr_ref, z_ref, gates_ref, xext_sc, carry_sc, a_sc, h_sc):
        i = pl.program_id(0)

        @pl.when(i == 0)
        def _():
            xext_sc[0:8, :] = jnp.zeros((8, D), f32)
            carry_sc[...] = jnp.zeros_like(carry_sc)

        x = x_ref[...]
        xext_sc[8:8 + TM, :] = x
        xe = xext_sc[...]
        xr = cb_ref[...] + cw_ref[KC4 - 1:KC4, :] * x
        for k in range(KC4 - 1):
            xr = xr + cw_ref[k:k + 1, :] * pltpu.roll(xe, KC4 - 1 - k, 0)[8:8 + TM]
        xext_sc[0:8, :] = x[TM - 8:TM]

        ra, ii, a, sq, _ = _block_gates(xr, wa_ref, ba_ref[...], wx_ref, bx_ref[...], lam_ref[...])
        for slot, val in enumerate((ra, ii, a, sq)):
            gates_ref[slot] = val
        _to_lane_tiles(a_sc, a)
        _to_lane_tiles(h_sc, sq * ii * xr)
        carry_sc[...] = _chain_scan(a_sc, h_sc, carry_sc[...], reverse=False)
        hr = _from_lane_tiles(h_sc)
        gel, _ = _gelu_and_grad(y_ref[...])
        xr_ref[...] = xr
        hr_ref[...] = hr
        z_ref[...] = (hr * gel).astype(bf16)

    vec = pl.BlockSpec((1, D), lambda i: (0, 0))
    return pl.pallas_call(
        body, name="rnn_fwd", grid=(t // TM,),
        in_specs=[
            pl.BlockSpec((TM, D), lambda i: (i, 0)),
            pl.BlockSpec((TM, D), lambda i: (i, 1)),
            pl.BlockSpec((KC4, D), lambda i: (0, 0)),
            vec,
            pl.BlockSpec((NHEAD, HD, HD), lambda i: (0, 0, 0)),
            vec,
            pl.BlockSpec((NHEAD, HD, HD), lambda i: (0, 0, 0)),
            vec, vec,
        ],
        out_specs=[pl.BlockSpec((TM, D), lambda i: (i, 0))] * 3 + [pl.BlockSpec((4, TM, D), lambda i: (0, i, 0))],
        out_shape=[jax.ShapeDtypeStruct((t, D), f32), jax.ShapeDtypeStruct((t, D), f32),
                   jax.ShapeDtypeStruct((t, D), bf16), jax.ShapeDtypeStruct((4, t, D), f32)],
        scratch_shapes=[pltpu.VMEM((TM + 8, D), f32), pltpu.VMEM((1, D), f32),
                        pltpu.VMEM((LT, TM, 128), f32), pltpu.VMEM((LT, TM, 128), f32)],
        compiler_params=_cp(1),
    )(proj, proj, cw, cb, wa, ba, wx, bx, lam)


def _ln_stats(vc):
    mu = jnp.mean(vc, axis=-1, keepdims=True)
    xc = vc - mu
    rstd = lax.rsqrt(jnp.mean(xc * xc, axis=-1, keepdims=True) + EPS)
    return xc * rstd, rstd


def _conv_fwd(proj, w31, b31, ln_g, ln_b, after=None):
    t = proj.shape[0]

    def body(gv_ref, gg_ref, w_ref, b_ref, lg_ref, lb_ref, vc_ref, s_ref, vext_sc, out_sc):
        i = pl.program_id(0)

        @pl.when(i == 0)
        def _():
            vext_sc[:, 0:HALO, :] = jnp.zeros((LT, HALO, 128), f32)

        v = gv_ref[...] * _sigmoid(gg_ref[...])
        for lt in range(LT):
            vext_sc[lt, HALO:HALO + TM, :] = v[:, lt * 128:(lt + 1) * 128]

        _strided_conv(w_ref, vext_sc, out_sc, HALO, [k - (KC31 - 1) for k in range(KC31)], b_ref)
        for lt in range(LT):
            vext_sc[lt, 0:HALO, :] = v[TM - HALO:TM, lt * 128:(lt + 1) * 128]
        acc = _from_lane_tiles(out_sc)
        xhat, _ = _ln_stats(acc)
        ln = xhat * lg_ref[...] + lb_ref[...]
        vc_ref[...] = acc
        s_ref[...] = (ln * _sigmoid(ln)).astype(bf16)

    vec = pl.BlockSpec((1, D), lambda i: (0, 0))
    body, in_specs, args = _ordered(
        body,
        [pl.BlockSpec((TM, D), lambda i: (i, 2)),
         pl.BlockSpec((TM, D), lambda i: (i, 3)),
         pl.BlockSpec((KC31, D), lambda i: (0, 0)),
         vec, vec, vec],
        (proj, proj, w31, b31, ln_g, ln_b), after)
    return pl.pallas_call(
        body, name="conv_fwd", grid=(t // TM,),
        in_specs=in_specs,
        out_specs=[pl.BlockSpec((TM, D), lambda i: (i, 0))] * 2,
        out_shape=[jax.ShapeDtypeStruct((t, D), f32), jax.ShapeDtypeStruct((t, D), bf16)],
        scratch_shapes=[pltpu.VMEM((LT, TM + HALO, 128), f32), pltpu.VMEM((LT, TM, 128), f32)],
        compiler_params=_cp(1),
    )(*args)


def _merge_fwd(h, z, s, proj, wrp, wcp, bcp, wout):
    t = h.shape[0]

    def body(h_ref, z_ref, s_ref, ga_ref, gb_ref, wrp_ref, wcp_ref, bcp_ref, wout_ref, ho_ref):
        ya = jnp.dot(z_ref[...], wrp_ref[...], preferred_element_type=f32)
        yb = jnp.dot(s_ref[...], wcp_ref[...], preferred_element_type=f32) + bcp_ref[...]
        merged = _sigmoid(ga_ref[...]) * ya + _sigmoid(gb_ref[...]) * yb
        ho_ref[...] = h_ref[...] + jnp.dot(merged.astype(bf16), wout_ref[...], preferred_element_type=f32)

    row = pl.BlockSpec((TM, D), lambda i: (i, 0))
    wsq = pl.BlockSpec((D, D), lambda i: (0, 0))
    return pl.pallas_call(
        body, name="merge_fwd", grid=(t // TM,),
        in_specs=[row, row, row,
                  pl.BlockSpec((TM, D), lambda i: (i, 4)),
                  pl.BlockSpec((TM, D), lambda i: (i, 5)),
                  wsq, wsq, pl.BlockSpec((1, D), lambda i: (0, 0)), wsq],
        out_specs=row,
        out_shape=jax.ShapeDtypeStruct((t, D), f32),
        compiler_params=_cp(1),
    )(h, z, s, proj, proj, wrp, wcp, bcp, wout)


def _ffn_bwd(dh, h, g, gate, up, wgu, wd, name, after=None):
    t = h.shape[0]
    nj = 2

    def body(dh_ref, h_ref, g_ref, gate_ref, up_ref, wg_ref, wd_ref,
             dhi_ref, dgate_ref, dup_ref, df_ref, dg_ref):
        @pl.when(pl.program_id(0) == 0)
        def _():
            dg_ref[...] = jnp.zeros_like(dg_ref)

        dh = dh_ref[...]
        dfb = (0.5 * dh).astype(bf16)
        df_ref[...] = dfb

        dn = None
        for j in range(nj):
            half = slice(j * FS, (j + 1) * FS)
            for off, width in FSUB:
                cols = slice(off, off + width)
                out_cols = slice(j * FS + off, j * FS + off + width)
                da = _nt_dot(dfb, wd_ref[j, cols, :])
                gt = gate_ref[:, out_cols].astype(f32)
                uu = up_ref[:, out_cols].astype(f32)
                sg = _sigmoid(gt)
                dgate_ref[:, out_cols] = (da * uu * (sg * (1.0 + gt * (1.0 - sg)))).astype(bf16)
                dup_ref[:, out_cols] = (da * (gt * sg)).astype(bf16)
            part = _nt_dot(dgate_ref[:, half], wg_ref[j]) + _nt_dot(dup_ref[:, half], wg_ref[2 + j])
            dn = part if dn is None else dn + part
        dhin, dg = _rms_bwd(dn, h_ref[...], g_ref[...])
        dhi_ref[...] = dh + dhin
        dg_ref[...] += dg

    rowd = pl.BlockSpec((TM, D), lambda i: (i, 0))
    rowf = pl.BlockSpec((TM, F), lambda i: (i, 0))
    vec = pl.BlockSpec((1, D), lambda i: (0, 0))
    body, in_specs, args = _ordered(
        body,
        [rowd, rowd, vec, rowf, rowf, RESIDENT, RESIDENT],
        (dh, h, g, gate, up, wgu, wd.reshape(nj, FS, D)), after)
    return pl.pallas_call(
        body, name=name, grid=(t // TM,),
        in_specs=in_specs,
        out_specs=[rowd, rowf, rowf, rowd, vec],
        out_shape=[jax.ShapeDtypeStruct((t, D), f32), jax.ShapeDtypeStruct((t, F), bf16),
                   jax.ShapeDtypeStruct((t, F), bf16),
                   jax.ShapeDtypeStruct((t, D), bf16), jax.ShapeDtypeStruct((1, D), f32)],
        compiler_params=_cp(1),
    )(*args)


def _big_tile(t):
    return max(k * TM for k in range(1, 6) if t % (k * TM) == 0)


ANY_SPEC = pl.BlockSpec(memory_space=pl.ANY)


def _tn_matmul(a, b, tk, tn, out_shape, out_block, out_map, name, base=None, after=None):
    t, kk = a.shape
    _, nn = b.shape
    tmm = _big_tile(t)
    nm = t // tmm

    def body(a_ref, b_ref, o_ref, acc_sc):
        m = pl.program_id(2)

        @pl.when(m == 0)
        def _():
            acc_sc[...] = jnp.zeros_like(acc_sc)

        acc_sc[...] += _tn_dot(a_ref[...], b_ref[...])

        @pl.when(m == nm - 1)
        def _():
            o_ref[...] = acc_sc[...].astype(o_ref.dtype)

    in_specs = [pl.BlockSpec((tmm, tk), lambda k, n, m: (m, k)),
                pl.BlockSpec((tmm, tn), lambda k, n, m: (m, n))]
    args, aliases = (a, b), {}
    if base is not None:
        body = (lambda inner: lambda a_ref, b_ref, base_ref, o_ref, acc_sc: inner(a_ref, b_ref, o_ref, acc_sc))(body)
        in_specs, args, aliases = in_specs + [ANY_SPEC], (a, b, base), {2: 0}
    if after is not None:
        body, in_specs, args = _ordered(body, in_specs, args, after)
        aliases = {k + _n_after(after): v for k, v in aliases.items()}
    return pl.pallas_call(
        body, name=name, grid=(kk // tk, nn // tn, nm),
        in_specs=in_specs,
        out_specs=pl.BlockSpec(out_block, out_map),
        out_shape=jax.ShapeDtypeStruct(out_shape, bf16),
        scratch_shapes=[pltpu.VMEM((tk, tn), f32)],
        input_output_aliases=aliases,
        compiler_params=_cp(3),
    )(*args)


def _merge_bwd(dh, z, s, proj, wrp, wcp, bcp, wout, after=None):
    t = dh.shape[0]

    def body(dh_ref, z_ref, s_ref, ga_ref, gb_ref, wrp_ref, wcp_ref, bcp_ref, wout_ref,
             dz_ref, ds_ref, dgab_ref, dhb_ref, mg_ref, dya_ref, dyb_ref, dbcp_ref):
        i = pl.program_id(0)

        @pl.when(i == 0)
        def _():
            dbcp_ref[...] = jnp.zeros_like(dbcp_ref)

        dhb = dh_ref[...].astype(bf16)
        dhb_ref[...] = dhb
        dmg = _nt_dot(dhb, wout_ref[...])
        ya = jnp.dot(z_ref[...], wrp_ref[...], preferred_element_type=f32)
        yb = jnp.dot(s_ref[...], wcp_ref[...], preferred_element_type=f32) + bcp_ref[...]
        sa = _sigmoid(ga_ref[...])
        sb = _sigmoid(gb_ref[...])
        mg_ref[...] = (sa * ya + sb * yb).astype(bf16)
        dgab_ref[:, 0:D] = (dmg * ya * sa * (1.0 - sa)).astype(bf16)
        dgab_ref[:, D:2 * D] = (dmg * yb * sb * (1.0 - sb)).astype(bf16)
        dya = dmg * sa
        dyb = dmg * sb
        dbcp_ref[...] += jnp.sum(dyb, axis=0, keepdims=True)
        dyab = dya.astype(bf16)
        dybb = dyb.astype(bf16)
        dya_ref[...] = dyab
        dyb_ref[...] = dybb
        dz_ref[...] = _nt_dot(dyab, wrp_ref[...])
        ds_ref[...] = _nt_dot(dybb, wcp_ref[...])

    row = pl.BlockSpec((TM, D), lambda i: (i, 0))
    wsq = pl.BlockSpec((D, D), lambda i: (0, 0))
    vec = pl.BlockSpec((1, D), lambda i: (0, 0))
    rowb = jax.ShapeDtypeStruct((t, D), bf16)
    body, in_specs, args = _ordered(
        body,
        [row, row, row,
         pl.BlockSpec((TM, D), lambda i: (i, 4)),
         pl.BlockSpec((TM, D), lambda i: (i, 5)),
         wsq, wsq, vec, wsq],
        (dh, z, s, proj, proj, wrp, wcp, bcp, wout), after)
    return pl.pallas_call(
        body, name="merge_bwd", grid=(t // TM,),
        in_specs=in_specs,
        out_specs=[row, row,
                   pl.BlockSpec((TM, 2 * D), lambda i: (i, 2)),
                   row, row, row, row, vec],
        out_shape=[jax.ShapeDtypeStruct((t, D), f32), jax.ShapeDtypeStruct((t, D), f32),
                   jax.ShapeDtypeStruct((t, NIN), bf16),
                   rowb, rowb, rowb, rowb, jax.ShapeDtypeStruct((1, D), f32)],
        compiler_params=_cp(1),
    )(*args)


def _conv_bwd(ds, vc, proj, dproj, w31, ln_g, ln_b, after=None):
    t = ds.shape[0]
    nt = t // TM
    hb = TM // HALO

    def body(ds_ref, vc_ref, gv_ref, gg_ref, gvp_ref, ggp_ref, dpin_ref, w_ref, lg_ref, lb_ref,
             dgvg_ref, dw_ref, db_ref, dlg_ref, dlb_ref, dext_sc, vext_sc, out_sc, dwacc_sc, small_sc):
        del dpin_ref
        i = pl.program_id(0)
        tile = nt - 1 - i

        @pl.when(i == 0)
        def _():
            dext_sc[:, TM:TM + HALO, :] = jnp.zeros((LT, HALO, 128), f32)
            dwacc_sc[...] = jnp.zeros_like(dwacc_sc)
            small_sc[...] = jnp.zeros_like(small_sc)

        lg = lg_ref[...]
        lb = lb_ref[...]

        xhat, rstd = _ln_stats(vc_ref[...])
        ln = xhat * lg + lb
        sg = _sigmoid(ln)
        dln = ds_ref[...] * (sg * (1.0 + ln * (1.0 - sg)))
        dxh = dln * lg
        dvc = rstd * (dxh - jnp.mean(dxh, axis=-1, keepdims=True)
                      - xhat * jnp.mean(dxh * xhat, axis=-1, keepdims=True))
        small_sc[0] += jnp.sum((dln * xhat).reshape(TM // 8, 8, D), axis=0)
        small_sc[1] += jnp.sum(dln.reshape(TM // 8, 8, D), axis=0)
        small_sc[2] += jnp.sum(dvc.reshape(TM // 8, 8, D), axis=0)
        sgg = _sigmoid(gg_ref[...])
        v = gv_ref[...] * sgg
        vprev = jnp.where(tile > 0, gvp_ref[...] * _sigmoid(ggp_ref[...]), 0.0)
        for lt in range(LT):
            lanes = slice(lt * 128, (lt + 1) * 128)
            dext_sc[lt, 0:TM, :] = dvc[:, lanes]
            vext_sc[lt, HALO:HALO + TM, :] = v[:, lanes]
            vext_sc[lt, 0:HALO, :] = vprev[:, lanes]

        _strided_conv(w_ref, dext_sc, out_sc, 0, [KC31 - 1 - k for k in range(KC31)])
        dv = _from_lane_tiles(out_sc)
        dgvg_ref[:, 0:D] = (dv * sgg).astype(bf16)
        dgvg_ref[:, D:2 * D] = (dv * gv_ref[...] * sgg * (1.0 - sgg)).astype(bf16)

        for k, part in enumerate(_strided_corr(dext_sc, vext_sc, HALO, [k - (KC31 - 1) for k in range(KC31)])):
            dwacc_sc[k] += part
        for lt in range(LT):
            dext_sc[lt, TM:TM + HALO, :] = dext_sc[lt, 0:HALO, :]

        @pl.when(i == nt - 1)
        def _():
            for k in range(KC31):
                dw_ref[k:k + 1, :] = jnp.sum(dwacc_sc[k], axis=0, keepdims=True)
            dlg_ref[...] = jnp.sum(small_sc[0], axis=0, keepdims=True)
            dlb_ref[...] = jnp.sum(small_sc[1], axis=0, keepdims=True)
            db_ref[...] = jnp.sum(small_sc[2], axis=0, keepdims=True)

    rev = lambda i: (nt - 1 - i, 0)
    vec = pl.BlockSpec((1, D), lambda i: (0, 0))
    halo_row = lambda i: jnp.maximum((nt - 1 - i) * hb - 1, 0)
    body, in_specs, args = _ordered(
        body,
        [pl.BlockSpec((TM, D), rev),
         pl.BlockSpec((TM, D), rev),
         pl.BlockSpec((TM, D), lambda i: (nt - 1 - i, 2)),
         pl.BlockSpec((TM, D), lambda i: (nt - 1 - i, 3)),
         pl.BlockSpec((HALO, D), lambda i: (halo_row(i), 2)),
         pl.BlockSpec((HALO, D), lambda i: (halo_row(i), 3)),
         pl.BlockSpec(memory_space=pl.ANY),
         pl.BlockSpec((KC31, D), lambda i: (0, 0)),
         vec, vec],
        (ds, vc, proj, proj, proj, proj, dproj, w31, ln_g, ln_b), after)
    return pl.pallas_call(
        body, name="conv_bwd", grid=(nt,),
        in_specs=in_specs,
        out_specs=[
            pl.BlockSpec((TM, 2 * D), lambda i: (nt - 1 - i, 1)),
            pl.BlockSpec((KC31, D), lambda i: (0, 0)),
            vec, vec, vec,
        ],
        out_shape=[jax.ShapeDtypeStruct((t, NIN), bf16),
                   jax.ShapeDtypeStruct((KC31, D), f32),
                   jax.ShapeDtypeStruct((1, D), f32), jax.ShapeDtypeStruct((1, D), f32),
                   jax.ShapeDtypeStruct((1, D), f32)],
        scratch_shapes=[pltpu.VMEM((LT, TM + HALO, 128), f32), pltpu.VMEM((LT, TM + HALO, 128), f32),
                        pltpu.VMEM((LT, TM, 128), f32), pltpu.VMEM((KC31, 8, D), f32),
                        pltpu.VMEM((3, 8, D), f32)],
        input_output_aliases={6 + _n_after(after): 0},
        compiler_params=_cp(1),
    )(*args)


def _rnn_bwd(dz, xr, hr, gates, proj, dproj, cw, wa, wx, lam):
    t = dz.shape[0]
    nt = t // TM
    ng = TM // 8
    hq = HD // NCHIP

    def body(dz_ref, xr_ref, hr_ref, hrp_ref, x_ref, xp_ref, y_ref, dpin_ref,
             cw_ref, wa_ref, gates_ref, wx_ref, lam_ref,
             dxy_ref, dwa_ref, dwx_ref, dcw_ref, dcb_ref, dba_ref, dbx_ref, dlam_ref,
             anext_sc, gcarry_sc, dext_sc, xext_sc, m_sc, g_sc, dwa_sc, dwx_sc, dsp_sc):
        del dpin_ref
        i = pl.program_id(0)
        tile = nt - 1 - i

        @pl.when(i == 0)
        def _():
            anext_sc[...] = jnp.zeros_like(anext_sc)
            gcarry_sc[...] = jnp.zeros_like(gcarry_sc)
            dext_sc[TM:TM + 8, :] = jnp.zeros((8, D), f32)
            dwa_sc[...] = jnp.zeros_like(dwa_sc)
            dwx_sc[...] = jnp.zeros_like(dwx_sc)
            dsp_sc[...] = jnp.zeros_like(dsp_sc)
            dcw_ref[...] = jnp.zeros_like(dcw_ref)
            dcb_ref[...] = jnp.zeros_like(dcb_ref)
            dba_ref[...] = jnp.zeros_like(dba_ref)
            dbx_ref[...] = jnp.zeros_like(dbx_ref)

        xr = xr_ref[...]
        hr = hr_ref[...]
        dz = dz_ref[...]
        gel, dgel = _gelu_and_grad(y_ref[...])
        dxy_ref[:, D:2 * D] = (dz * hr * dgel).astype(bf16)
        ra, ii, a, sq = gates_ref[0], gates_ref[1], gates_ref[2], gates_ref[3]
        sp = _softplus(-lam_ref[...])

        row = _row_ids((TM, D))
        _to_lane_tiles(m_sc, jnp.where(row == TM - 1, anext_sc[...], pltpu.roll(a, TM - 1, 0)))
        anext_sc[...] = a[0:1, :]
        _to_lane_tiles(g_sc, dz * gel)
        gcarry_sc[...] = _chain_scan(m_sc, g_sc, gcarry_sc[...], reverse=True)
        gg = _from_lane_tiles(g_sc)

        hlast = jnp.where(tile > 0, hrp_ref[7:8, :], 0.0)
        hprev = jnp.where(row == 0, hlast, pltpu.roll(hr, 1, 0))
        d_a = gg * hprev
        dsq = gg * ii * xr
        dii = gg * sq * xr
        dxr = gg * sq * ii
        dlog = d_a * a - dsq * (a * a / sq)
        dsp_sc[...] += jnp.sum(dlog * (-8.0 * ra), axis=0, keepdims=True)
        dpa = dlog * (-8.0 * sp) * ra * (1.0 - ra)
        dpx = dii * ii * (1.0 - ii)
        dba_ref[...] += jnp.sum(dpa, axis=0, keepdims=True)
        dbx_ref[...] += jnp.sum(dpx, axis=0, keepdims=True)
        dpab = dpa.astype(bf16)
        dpxb = dpx.astype(bf16)
        xrb = xr.astype(bf16)
        back = []
        for hh in range(NHEAD):
            cols = slice(hh * HD, (hh + 1) * HD)
            back.append(_nt_dot(dpab[:, cols], wa_ref[hh]) + _nt_dot(dpxb[:, cols], wx_ref[hh]))
            dwa_sc[hh] += _tn_dot(xrb[:, cols], dpab[:, cols])
            dwx_sc[hh] += _tn_dot(xrb[:, cols], dpxb[:, cols])
        dxr = dxr + jnp.concatenate(back, axis=1)

        dext_sc[0:TM, :] = dxr
        de = dext_sc[...]
        dx = cw_ref[KC4 - 1:KC4, :] * dxr
        for k in range(KC4 - 1):
            dx = dx + cw_ref[k:k + 1, :] * pltpu.roll(de, TM + 8 - (KC4 - 1 - k), 0)[0:TM]
        dext_sc[TM:TM + 8, :] = dxr[0:8]
        dxy_ref[:, 0:D] = dx.astype(bf16)

        x = x_ref[...]
        xext_sc[0:8, :] = jnp.where(tile > 0, xp_ref[...], 0.0)
        xext_sc[8:8 + TM, :] = x
        xe = xext_sc[...]
        dcw_ref[KC4 - 1:KC4, :] += jnp.sum(dxr * x, axis=0, keepdims=True)
        for k in range(KC4 - 1):
            xs = pltpu.roll(xe, KC4 - 1 - k, 0)[8:8 + TM]
            dcw_ref[k:k + 1, :] += jnp.sum(dxr * xs, axis=0, keepdims=True)
        dcb_ref[...] += jnp.sum(dxr, axis=0, keepdims=True)

        @pl.when(i == nt - 1)
        def _():
            for hh in range(NHEAD):
                for qc in range(NCHIP):
                    dwa_ref[qc, hh] = dwa_sc[hh, qc * hq:(qc + 1) * hq, :].astype(bf16)
                    dwx_ref[qc, hh] = dwx_sc[hh, qc * hq:(qc + 1) * hq, :].astype(bf16)
            dlam_ref[...] = -dsp_sc[...] * _sigmoid(-lam_ref[...])

    rev = lambda i: (nt - 1 - i, 0)
    vec = pl.BlockSpec((1, D), lambda i: (0, 0))
    prev8 = lambda i: jnp.maximum((nt - 1 - i) * ng - 1, 0)
    wblk = pl.BlockSpec((NHEAD, HD, HD), lambda i: (0, 0, 0))
    gblk = pl.BlockSpec((NCHIP, NHEAD, hq, HD), lambda i: (0, 0, 0, 0))
    return pl.pallas_call(
        body, name="rnn_bwd", grid=(nt,),
        in_specs=[
            pl.BlockSpec((TM, D), rev),
            pl.BlockSpec((TM, D), rev),
            pl.BlockSpec((TM, D), rev),
            pl.BlockSpec((8, D), lambda i: (prev8(i), 0)),
            pl.BlockSpec((TM, D), lambda i: (nt - 1 - i, 0)),
            pl.BlockSpec((8, D), lambda i: (prev8(i), 0)),
            pl.BlockSpec((TM, D), lambda i: (nt - 1 - i, 1)),
            pl.BlockSpec(memory_space=pl.ANY),
            pl.BlockSpec((KC4, D), lambda i: (0, 0)),
            wblk, pl.BlockSpec((4, TM, D), lambda i: (0, nt - 1 - i, 0)), wblk, vec,
        ],
        out_specs=[
            pl.BlockSpec((TM, 2 * D), lambda i: (nt - 1 - i, 0)),
            gblk, gblk,
            pl.BlockSpec((KC4, D), lambda i: (0, 0)),
            vec, vec, vec, vec,
        ],
        out_shape=[jax.ShapeDtypeStruct((t, NIN), bf16),
                   jax.ShapeDtypeStruct((NCHIP, NHEAD, hq, HD), bf16),
                   jax.ShapeDtypeStruct((NCHIP, NHEAD, hq, HD), bf16),
                   jax.ShapeDtypeStruct((KC4, D), f32),
                   jax.ShapeDtypeStruct((1, D), f32), jax.ShapeDtypeStruct((1, D), f32),
                   jax.ShapeDtypeStruct((1, D), f32), jax.ShapeDtypeStruct((1, D), f32)],
        scratch_shapes=[pltpu.VMEM((1, D), f32), pltpu.VMEM((1, D), f32),
                        pltpu.VMEM((TM + 8, D), f32), pltpu.VMEM((TM + 8, D), f32),
                        pltpu.VMEM((LT, TM, 128), f32), pltpu.VMEM((LT, TM, 128), f32),
                        pltpu.VMEM((NHEAD, HD, HD), f32), pltpu.VMEM((NHEAD, HD, HD), f32),
                        pltpu.VMEM((1, D), f32)],
        input_output_aliases={7: 0},
        compiler_params=_cp(1),
    )(dz, xr, hr, hr, proj, proj, proj, dproj, cw, wa, gates, wx, lam)


def _inproj_bwd(dproj, dh, h, g, win, after=None):
    t = h.shape[0]
    tn = NIN // NCHIP

    def body(dp_ref, dh_ref, h_ref, g_ref, w_ref, dhi_ref, dg_ref, db_ref):
        @pl.when(pl.program_id(0) == 0)
        def _():
            dg_ref[...] = jnp.zeros_like(dg_ref)
            db_ref[...] = jnp.zeros_like(db_ref)

        dn = None
        for s in range(NCHIP):
            dp = dp_ref[:, s * tn:(s + 1) * tn]
            part = _nt_dot(dp, w_ref[s])
            dn = part if dn is None else dn + part
            db_ref[s] += jnp.sum(dp.astype(f32), axis=0, keepdims=True)
        dhin, dg = _rms_bwd(dn, h_ref[...], g_ref[...])
        dhi_ref[...] = dh_ref[...] + dhin
        dg_ref[...] += dg

    rowd = pl.BlockSpec((TM, D), lambda i: (i, 0))
    vec = pl.BlockSpec((1, D), lambda i: (0, 0))
    body, in_specs, args = _ordered(
        body,
        [pl.BlockSpec((TM, NIN), lambda i: (i, 0)), rowd, rowd, vec, RESIDENT],
        (dproj, dh, h, g, win), after)
    return pl.pallas_call(
        body, name="inproj_bwd", grid=(t // TM,),
        in_specs=in_specs,
        out_specs=[rowd, vec, pl.BlockSpec((NCHIP, 1, tn), lambda i: (0, 0, 0))],
        out_shape=[jax.ShapeDtypeStruct((t, D), f32), jax.ShapeDtypeStruct((1, D), f32),
                   jax.ShapeDtypeStruct((NCHIP, 1, tn), f32)],
        compiler_params=_cp(1),
    )(*args)


def _ffn_gu_grad(n, dgate, dup, tag, after=None):
    half = _tn_matmul(n, dgate, D, FS, (NCHIP, D, FS), (None, D, FS), lambda k, nn, m: (nn, 0, 0),
                      tag + "_dwg", after=after)
    return _tn_matmul(n, dup, D, FS, (NCHIP, D, FS), (None, D, FS), lambda k, nn, m: (2 + nn, 0, 0),
                      tag + "_dwu", base=half)


def _ffn_down_grad(a, df, tag, after=None):
    return _tn_matmul(a, df, FS, D, (F, D), (FS, D), lambda k, nn, m: (k, 0), tag + "_dwd", after=after)


def _square_grad(a, b, name):
    return _tn_matmul(a, b, D, D, (D, D), (D, D), lambda k, nn, m: (0, 0), name)


ANY = pl.BlockSpec(memory_space=pl.ANY)


def _place():
    x, y, c = lax.axis_index("x"), lax.axis_index("y"), lax.axis_index("c")
    chips = [(1 - x, y), (x, 1 - y), (1 - x, 1 - y)]
    return x, y, c, chips


def _chip_id(chip):
    return 2 * chip[0] + chip[1]


def _cast_into_slot(w2d, qc, dtype, name, after=None):
    r, cc = w2d.shape
    hr = r // 2

    def body(qc_ref, *refs):
        del qc_ref
        w_ref, o_ref = refs[-2:]
        o_ref[...] = w_ref[...].astype(dtype)

    in_specs, args = [pl.BlockSpec((hr, cc), lambda h, qc_ref: (h, 0))], (w2d,)
    if after is not None:
        in_specs, args = [ANY_SPEC] + in_specs, (after,) + args
    return pl.pallas_call(
        body, name=name,
        grid_spec=pltpu.PrefetchScalarGridSpec(
            num_scalar_prefetch=1, grid=(2,),
            in_specs=in_specs,
            out_specs=pl.BlockSpec((None, None, hr, cc), lambda h, qc_ref: (qc_ref[0], h, 0, 0))),
        out_shape=jax.ShapeDtypeStruct((NCHIP, 2, hr, cc), dtype),
        compiler_params=_cp(1),
    )(qc, *args)


def _place_pack(pack, qc):
    def body(qc_ref, p_ref, o_ref):
        del qc_ref
        o_ref[...] = p_ref[...]

    return pl.pallas_call(
        body, name="place_pack",
        grid_spec=pltpu.PrefetchScalarGridSpec(
            num_scalar_prefetch=1, grid=(1,),
            in_specs=[pl.BlockSpec(pack.shape, lambda i, qc_ref: (0, 0))],
            out_specs=pl.BlockSpec((None,) + pack.shape, lambda i, qc_ref: (2 * qc_ref[0] + qc_ref[1], 0, 0))),
        out_shape=jax.ShapeDtypeStruct((8,) + pack.shape, pack.dtype),
        compiler_params=_cp(1),
    )(qc, pack)


def _pair_add(parts, gots, qc, name):
    n = len(parts)

    def body(qc_ref, *refs):
        s = pl.program_id(0)
        for a in range(n):
            val = (refs[a][...].astype(f32) + refs[n + a][...].astype(f32)).astype(bf16)
            refs[2 * n + a][...] = val

            @pl.when(s == qc_ref[0])
            def _(val=val, land_ref=refs[3 * n + a]):
                land_ref[...] = val

    shapes = [p.shape[2:] for p in parts]
    mine = [pl.BlockSpec((None, None) + sh, lambda s, qc_ref: (s, qc_ref[1], 0, 0)) for sh in shapes]
    block = [pl.BlockSpec((None,) + sh, lambda s, qc_ref: (s, 0, 0)) for sh in shapes]
    own = [pl.BlockSpec((None,) + sh, lambda s, qc_ref: (qc_ref[0], 0, 0)) for sh in shapes]
    outs = pl.pallas_call(
        body, name=name,
        grid_spec=pltpu.PrefetchScalarGridSpec(
            num_scalar_prefetch=1, grid=(NCHIP,), in_specs=mine + block, out_specs=block + own),
        out_shape=[jax.ShapeDtypeStruct((NCHIP,) + sh, bf16) for sh in shapes] * 2,
        compiler_params=_cp(1),
    )(qc, *parts, *gots)
    return list(outs[:n]), list(outs[n:])


def _sum_chips(gots, name):
    n = len(gots)

    def body(*refs):
        for a in range(n):
            acc = refs[a][0].astype(f32)
            for s in range(1, NCHIP):
                acc = acc + refs[a][s].astype(f32)
            refs[n + a][...] = acc

    return list(pl.pallas_call(
        body, name=name, grid=(1,),
        in_specs=[pl.BlockSpec(g.shape, lambda i: (0, 0, 0)) for g in gots],
        out_specs=[pl.BlockSpec(g.shape[1:], lambda i: (0, 0)) for g in gots],
        out_shape=[jax.ShapeDtypeStruct(g.shape[1:], f32) for g in gots],
        compiler_params=_cp(1),
    )(*gots))


def _pair_share(halves, name, after=None):
    n = len(halves)
    extra = () if after is None else (after,)

    def body(*refs):
        refs = refs[len(extra):]
        ins, outs = refs[:n], refs[n:2 * n]
        send_sems, recv_sems = refs[2 * n:]
        x, y, c, _ = _place()
        copies = []
        for a in range(n):
            cp = pltpu.make_async_remote_copy(
                src_ref=ins[a], dst_ref=outs[a], send_sem=send_sems.at[a], recv_sem=recv_sems.at[a],
                device_id=(x, y, 1 - c), device_id_type=MESH)
            cp.start()
            copies.append(cp)
        for cp in copies:
            cp.wait()

    return pl.pallas_call(
        body, name=name,
        in_specs=[ANY] * (len(extra) + n), out_specs=[ANY] * n,
        out_shape=[jax.ShapeDtypeStruct(s.shape, s.dtype) for s in halves],
        scratch_shapes=[pltpu.SemaphoreType.DMA((n,)), pltpu.SemaphoreType.DMA((n,))],
    )(*extra, *halves)


def _all_copy(buf_ref, send_ref, recv_ref, k, x, y, c, landing):
    px, py, pc = (1 - x if k & 4 else x, 1 - y if k & 2 else y, 1 - c if k & 1 else c)
    me = 4 * x + 2 * y + c
    there = 4 * px + 2 * py + pc
    return pltpu.make_async_remote_copy(
        src_ref=buf_ref.at[me], dst_ref=buf_ref.at[there if landing else me],
        send_sem=send_ref.at[k - 1], recv_sem=recv_ref.at[k - 1],
        device_id=(px, py, pc), device_id_type=MESH)


def _gather_all_start(buf, name):
    def body(in_ref, send, recv, thru, token):
        del thru
        x, y, c, _ = _place()
        for k in range(1, 8):
            _all_copy(in_ref, send, recv, k, x, y, c, False).start()
        token[...] = jnp.zeros_like(token)

    return pl.pallas_call(
        body, name=name,
        in_specs=[HBM],
        out_specs=[SEM, SEM, HBM, pl.BlockSpec(memory_space=pltpu.VMEM)],
        out_shape=[pltpu.SemaphoreType.DMA((7,)), pltpu.SemaphoreType.DMA((7,)),
                   pltpu.HBM(buf.shape, buf.dtype), jax.ShapeDtypeStruct((8, 128), f32)],
        input_output_aliases={0: 2},
        compiler_params=pltpu.CompilerParams(has_side_effects=EFFECT),
    )(_in_hbm(buf))


def _gather_all_wait(send, recv, buf, after, name):
    def body(in_ref, send_r, recv_r, after_ref, out_ref):
        del after_ref, out_ref
        x, y, c, _ = _place()
        for k in range(1, 8):
            cp = _all_copy(in_ref, send_r, recv_r, k, x, y, c, True)
            cp.wait_send()
            cp.wait_recv()

    return pl.pallas_call(
        body, name=name,
        in_specs=[HBM, SEM, SEM, ANY],
        out_specs=HBM,
        out_shape=pltpu.HBM(buf.shape, buf.dtype),
        input_output_aliases={0: 0},
        compiler_params=pltpu.CompilerParams(has_side_effects=EFFECT),
    )(buf, send, recv, after)


HBM = pl.BlockSpec(memory_space=pltpu.HBM)
SEM = pl.BlockSpec(memory_space=pltpu.SEMAPHORE)
EFFECT = pltpu.SideEffectType.DATAFLOW_SIDE_EFFECTING
N_PEER = 3


def _in_hbm(a):
    return pltpu.with_memory_space_constraint(a, pltpu.HBM)


def _gather_copy(buf_ref, send_ref, recv_ref, j, chip, q, c, landing_chip):
    return pltpu.make_async_remote_copy(
        src_ref=buf_ref.at[q, c], dst_ref=buf_ref.at[landing_chip, c],
        send_sem=send_ref.at[j], recv_sem=recv_ref.at[j],
        device_id=(chip[0], chip[1], c), device_id_type=MESH)


def _gather_start(bufs, name):
    n = len(bufs)

    def body(*refs):
        ins = refs[:n]
        send, recv = refs[n:2 * n], refs[2 * n:3 * n]
        token = refs[4 * n]
        x, y, c, chips = _place()
        q = 2 * x + y
        for a in range(n):
            for j, chip in enumerate(chips):
                _gather_copy(ins[a], send[a], recv[a], j, chip, q, c, q).start()
        token[...] = jnp.zeros_like(token)

    sems = [pltpu.SemaphoreType.DMA((N_PEER,))] * (2 * n)
    outs = pl.pallas_call(
        body, name=name,
        in_specs=[HBM] * n,
        out_specs=[SEM] * (2 * n) + [HBM] * n + [pl.BlockSpec(memory_space=pltpu.VMEM)],
        out_shape=sems + [pltpu.HBM(b.shape, b.dtype) for b in bufs] + [jax.ShapeDtypeStruct((8, 128), f32)],
        input_output_aliases={a: 2 * n + a for a in range(n)},
        compiler_params=pltpu.CompilerParams(has_side_effects=EFFECT),
    )(*[_in_hbm(b) for b in bufs])
    return list(outs[:n]), list(outs[n:2 * n]), list(outs[2 * n:3 * n]), outs[3 * n]


def _gather_wait(send, recv, bufs, after, name):
    n = len(bufs)

    def body(*refs):
        ins = refs[:n]
        send_r, recv_r = refs[n:2 * n], refs[2 * n:3 * n]
        x, y, c, chips = _place()
        q = 2 * x + y
        for a in range(n):
            for j, chip in enumerate(chips):
                cp = _gather_copy(ins[a], send_r[a], recv_r[a], j, chip, q, c, _chip_id(chip))
                cp.wait_send()
                cp.wait_recv()

    afters = after if isinstance(after, (tuple, list)) else (after,)
    outs = pl.pallas_call(
        body, name=name,
        in_specs=[HBM] * n + [SEM] * (2 * n) + [ANY] * len(afters),
        out_specs=[HBM] * n,
        out_shape=[pltpu.HBM(b.shape, b.dtype) for b in bufs],
        input_output_aliases={a: a for a in range(n)},
        compiler_params=pltpu.CompilerParams(has_side_effects=EFFECT),
    )(*bufs, *send, *recv, *afters)
    return list(outs)


def _forward_halves(bufs, name):
    n = len(bufs)

    def body(*refs):
        outs = refs[n:2 * n]
        send_sems, recv_sems = refs[2 * n:]
        x, y, c, chips = _place()
        sibling = (x, y, 1 - c)

        def remote(a, j, blk):
            return pltpu.make_async_remote_copy(src_ref=blk, dst_ref=blk, send_sem=send_sems.at[a, j],
                                                recv_sem=recv_sems.at[a, j], device_id=sibling,
                                                device_id_type=MESH)

        sent = []
        for a in range(n):
            for j, chip in enumerate(chips):
                cp = remote(a, j, outs[a].at[_chip_id(chip), c])
                cp.start()
                sent.append(cp)
        for a in range(n):
            for j, chip in enumerate(chips):
                remote(a, j, outs[a].at[_chip_id(chip), 1 - c]).wait_recv()
        for cp in sent:
            cp.wait_send()

    return pl.pallas_call(
        body, name=name,
        in_specs=[ANY] * n, out_specs=[ANY] * n,
        out_shape=[jax.ShapeDtypeStruct(s.shape, s.dtype) for s in bufs],
        scratch_shapes=[pltpu.SemaphoreType.DMA((n, N_PEER)), pltpu.SemaphoreType.DMA((n, N_PEER))],
        input_output_aliases={a: a for a in range(n)},
    )(*bufs)


def _reduce_copy(sum_ref, land_ref, send_ref, recv_ref, j, chip, q, c, landing_chip):
    return pltpu.make_async_remote_copy(
        src_ref=sum_ref.at[_chip_id(chip)], dst_ref=land_ref.at[landing_chip],
        send_sem=send_ref.at[j], recv_sem=recv_ref.at[j],
        device_id=(chip[0], chip[1], c), device_id_type=MESH)


def _reduce_start(sums, lands, name):
    n = len(sums)

    def body(*refs):
        s_in, l_in = refs[:n], refs[n:2 * n]
        send, recv = refs[2 * n:3 * n], refs[3 * n:4 * n]
        token = refs[6 * n]
        x, y, c, chips = _place()
        q = 2 * x + y
        for a in range(n):
            for j, chip in enumerate(chips):
                _reduce_copy(s_in[a], l_in[a], send[a], recv[a], j, chip, q, c, q).start()
        token[...] = jnp.zeros_like(token)

    sems = [pltpu.SemaphoreType.DMA((N_PEER,))] * (2 * n)
    outs = pl.pallas_call(
        body, name=name,
        in_specs=[HBM] * (2 * n),
        out_specs=[SEM] * (2 * n) + [HBM] * (2 * n) + [pl.BlockSpec(memory_space=pltpu.VMEM)],
        out_shape=sems + [pltpu.HBM(b.shape, b.dtype) for b in list(sums) + list(lands)]
        + [jax.ShapeDtypeStruct((8, 128), f32)],
        input_output_aliases={a: 2 * n + a for a in range(2 * n)},
        compiler_params=pltpu.CompilerParams(has_side_effects=EFFECT),
    )(*[_in_hbm(b) for b in list(sums) + list(lands)])
    return (list(outs[:n]), list(outs[n:2 * n]), list(outs[2 * n:3 * n]), list(outs[3 * n:4 * n]),
            outs[4 * n])


def _reduce_wait(send, recv, sums, lands, after, name):
    n = len(sums)

    def body(*refs):
        s_in, l_in = refs[:n], refs[n:2 * n]
        send_r, recv_r = refs[2 * n:3 * n], refs[3 * n:4 * n]
        x, y, c, chips = _place()
        q = 2 * x + y
        for a in range(n):
            for j, chip in enumerate(chips):
                cp = _reduce_copy(s_in[a], l_in[a], send_r[a], recv_r[a], j, chip, q, c, _chip_id(chip))
                cp.wait_send()
                cp.wait_recv()

    afters = after if isinstance(after, (tuple, list)) else (after,)
    outs = pl.pallas_call(
        body, name=name,
        in_specs=[HBM] * (2 * n) + [SEM] * (2 * n) + [ANY] * len(afters),
        out_specs=[HBM] * (2 * n),
        out_shape=[pltpu.HBM(b.shape, b.dtype) for b in list(sums) + list(lands)],
        input_output_aliases={a: a for a in range(2 * n)},
        compiler_params=pltpu.CompilerParams(has_side_effects=EFFECT),
    )(*sums, *lands, *send, *recv, *afters)
    return list(outs[n:])


def _sibling_copy(part_ref, land_ref, send_ref, recv_ref, x, y, c):
    return pltpu.make_async_remote_copy(
        src_ref=part_ref.at[:, 1 - c], dst_ref=land_ref, send_sem=send_ref.at[0], recv_sem=recv_ref.at[0],
        device_id=(x, y, 1 - c), device_id_type=MESH)


def _pair_exchange_start(parts, name):
    n = len(parts)
    lands = [lax.empty((NCHIP,) + p.shape[2:], p.dtype) for p in parts]

    def body(*refs):
        p_in, l_in = refs[:n], refs[n:2 * n]
        send, recv = refs[2 * n:3 * n], refs[3 * n:4 * n]
        token = refs[6 * n]
        x, y, c, _ = _place()
        for a in range(n):
            _sibling_copy(p_in[a], l_in[a], send[a], recv[a], x, y, c).start()
        token[...] = jnp.zeros_like(token)

    sems = [pltpu.SemaphoreType.DMA((1,))] * (2 * n)
    outs = pl.pallas_call(
        body, name=name,
        in_specs=[HBM] * (2 * n),
        out_specs=[SEM] * (2 * n) + [HBM] * (2 * n) + [pl.BlockSpec(memory_space=pltpu.VMEM)],
        out_shape=sems + [pltpu.HBM(b.shape, b.dtype) for b in list(parts) + lands]
        + [jax.ShapeDtypeStruct((8, 128), f32)],
        input_output_aliases={a: 2 * n + a for a in range(2 * n)},
        compiler_params=pltpu.CompilerParams(has_side_effects=EFFECT),
    )(*[_in_hbm(b) for b in list(parts) + lands])
    return (list(outs[:n]), list(outs[n:2 * n]), list(outs[2 * n:3 * n]), list(outs[3 * n:4 * n]),
            outs[4 * n])


def _pair_exchange_wait(send, recv, parts, lands, after, name):
    n = len(parts)

    def body(*refs):
        p_in, l_in = refs[:n], refs[n:2 * n]
        send_r, recv_r = refs[2 * n:3 * n], refs[3 * n:4 * n]
        x, y, c, _ = _place()
        for a in range(n):
            cp = _sibling_copy(p_in[a], l_in[a], send_r[a], recv_r[a], x, y, c)
            cp.wait_send()
            cp.wait_recv()

    outs = pl.pallas_call(
        body, name=name,
        in_specs=[HBM] * (2 * n) + [SEM] * (2 * n) + [ANY],
        out_specs=[HBM] * (2 * n),
        out_shape=[pltpu.HBM(b.shape, b.dtype) for b in list(parts) + list(lands)],
        input_output_aliases={a: a for a in range(2 * n)},
        compiler_params=pltpu.CompilerParams(has_side_effects=EFFECT),
    )(*parts, *lands, *send, *recv, after)
    return list(outs[:n]), list(outs[n:])


def _forward_copy(buf_ref, send_ref, recv_ref, j, chip, x, y, c, landing):
    return pltpu.make_async_remote_copy(
        src_ref=buf_ref.at[_chip_id(chip), c], dst_ref=buf_ref.at[_chip_id(chip), 1 - c if landing else c],
        send_sem=send_ref.at[j], recv_sem=recv_ref.at[j], device_id=(x, y, 1 - c), device_id_type=MESH)


def _forward_start(bufs, name):
    n = len(bufs)

    def body(*refs):
        ins = refs[:n]
        send, recv = refs[n:2 * n], refs[2 * n:3 * n]
        token = refs[4 * n]
        x, y, c, chips = _place()
        for a in range(n):
            for j, chip in enumerate(chips):
                _forward_copy(ins[a], send[a], recv[a], j, chip, x, y, c, False).start()
        token[...] = jnp.zeros_like(token)

    sems = [pltpu.SemaphoreType.DMA((N_PEER,))] * (2 * n)
    outs = pl.pallas_call(
        body, name=name,
        in_specs=[HBM] * n,
        out_specs=[SEM] * (2 * n) + [HBM] * n + [pl.BlockSpec(memory_space=pltpu.VMEM)],
        out_shape=sems + [pltpu.HBM(b.shape, b.dtype) for b in bufs] + [jax.ShapeDtypeStruct((8, 128), f32)],
        input_output_aliases={a: 2 * n + a for a in range(n)},
        compiler_params=pltpu.CompilerParams(has_side_effects=EFFECT),
    )(*[_in_hbm(b) for b in bufs])
    return list(outs[:n]), list(outs[n:2 * n]), list(outs[2 * n:3 * n]), outs[3 * n]


def _forward_wait(send, recv, bufs, after, name):
    n = len(bufs)

    def body(*refs):
        ins = refs[:n]
        send_r, recv_r = refs[n:2 * n], refs[2 * n:3 * n]
        x, y, c, chips = _place()
        for a in range(n):
            for j, chip in enumerate(chips):
                cp = _forward_copy(ins[a], send_r[a], recv_r[a], j, chip, x, y, c, True)
                cp.wait_send()
                cp.wait_recv()

    outs = pl.pallas_call(
        body, name=name,
        in_specs=[HBM] * n + [SEM] * (2 * n) + [ANY],
        out_specs=[HBM] * n,
        out_shape=[pltpu.HBM(b.shape, b.dtype) for b in bufs],
        input_output_aliases={a: a for a in range(n)},
        compiler_params=pltpu.CompilerParams(has_side_effects=EFFECT),
    )(*bufs, *send, *recv, after)
    return list(outs)


def _adamw_math(w, g, m, v):
    m = ADAM_B1 * m + (1.0 - ADAM_B1) * g
    v = ADAM_B2 * v + (1.0 - ADAM_B2) * (g * g)
    m_hat = m / (1.0 - ADAM_B1 ** ADAM_STEP)
    v_hat = v / (1.0 - ADAM_B2 ** ADAM_STEP)
    delta = -ADAM_LR * (m_hat / (jnp.sqrt(v_hat) + ADAM_EPS) + ADAM_WD * w)
    return delta, m, v


ADAMW_BLOCK_BYTES = 3 << 19


def _adamw(ws, mines, theirs, ms, vs, qc, name):
    n = len(ws)
    halves = [w.shape[0] // 2 for w in ws]
    nb = next(k for k in range(1, min(halves) + 1)
              if all(hr % k == 0 and (hr // k) % 8 == 0 and (hr // k) * w.shape[1] * 4 <= ADAMW_BLOCK_BYTES
                     for hr, w in zip(halves, ws)))

    def body(qc_ref, *refs):
        mine_here = pl.program_id(0) == qc_ref[1]
        for a in range(n):
            w_ref, a_ref, b_ref, m_ref, v_ref = (refs[k * n + a] for k in range(5))
            g_ref, d_ref, mo_ref, vo_ref = (refs[(5 + k) * n + a] for k in range(4))
            g = jnp.where(mine_here, a_ref[...], b_ref[...])
            g_ref[...] = g
            d_ref[...], mo_ref[...], vo_ref[...] = _adamw_math(w_ref[...], g, m_ref[...], v_ref[...])

    blocks = [(hr // nb, w.shape[1]) for hr, w in zip(halves, ws)]
    full = [pl.BlockSpec(b, lambda h, i, qc_ref: (h * nb + i, 0)) for b in blocks]
    half = [pl.BlockSpec(b, lambda h, i, qc_ref: (i, 0)) for b in blocks]
    outs = pl.pallas_call(
        body, name=name,
        grid_spec=pltpu.PrefetchScalarGridSpec(
            num_scalar_prefetch=1, grid=(2, nb),
            in_specs=full + half + half + full + full, out_specs=full * 4),
        out_shape=[jax.ShapeDtypeStruct(w.shape, f32) for w in ws] * 4,
        compiler_params=_cp(2),
    )(qc, *ws, *mines, *theirs, *ms, *vs)
    return [tuple(outs[k * n + a] for k in range(4)) for a in range(n)]


REPL = [("ffn1_norm", 1), ("mix_norm", 1), ("b_in", 6), ("rnn_conv_b", 1), ("rg_b_a", 1), ("rg_b_x", 1),
        ("rg_lambda", 1), ("conv_dw_b", 1), ("conv_ln_g", 1), ("conv_ln_b", 1), ("conv_b_proj", 1),
        ("ffn2_norm", 1), ("final_norm", 1)]
COLSH = [("meta_tokens", NMETA), ("rnn_conv_w", KC4), ("conv_dw_w", KC31)]
SMALL = REPL + COLSH
CS = D // NCHIP


def _pack_rows():
    starts, row = {}, 0
    for k, rows in REPL:
        starts[k] = row
        row += rows
    for k, rows in COLSH:
        row = -(-row // 8) * 8
        starts[k] = row
        row += rows
    return starts, -(-row // 8) * 8


PACK_START, LOSS_ROW = _pack_rows()
SMALL_ROWS = LOSS_ROW + 8


def _small_pack(g, loss_row):
    pieces, row = [], 0
    for k, rows in SMALL:
        if PACK_START[k] > row:
            pieces.append(jnp.zeros((PACK_START[k] - row, D), f32))
        pieces.append(g[k].reshape(rows, D))
        row = PACK_START[k] + rows
    pieces.append(jnp.zeros((LOSS_ROW - row, D), f32))
    pieces.append(loss_row)
    pieces.append(jnp.zeros((SMALL_ROWS - LOSS_ROW - 1, D), f32))
    return jnp.concatenate(pieces, axis=0)


def _adamw_small(packs, ws, ms, vs):
    ns = len(SMALL)

    def body(*refs):
        pack_ref = refs[0]
        w_refs, m_refs, v_refs = refs[1:1 + ns], refs[1 + ns:1 + 2 * ns], refs[1 + 2 * ns:1 + 3 * ns]
        outs = refs[1 + 3 * ns:1 + 7 * ns]
        g_refs, d_refs, mo_refs, vo_refs = outs[:ns], outs[ns:2 * ns], outs[2 * ns:3 * ns], outs[3 * ns:]
        loss_ref = refs[1 + 7 * ns]
        gsum_sc = refs[2 + 7 * ns]
        q = 2 * lax.axis_index("x") + lax.axis_index("y")
        acc = pack_ref[0]
        for dev in range(1, 8):
            acc = acc + pack_ref[dev]
        gsum_sc[...] = acc
        loss_ref[...] = gsum_sc[LOSS_ROW:LOSS_ROW + 1, :]
        for idx, (name, rows) in enumerate(SMALL):
            row = PACK_START[name]
            if idx < len(REPL):
                for k in range(rows):
                    cols = slice(k * D, (k + 1) * D)
                    g = gsum_sc[row + k:row + k + 1, :]
                    d, mm, vv = _adamw_math(w_refs[idx][:, cols], g, m_refs[idx][:, cols], v_refs[idx][:, cols])
                    g_refs[idx][:, cols] = g
                    d_refs[idx][:, cols] = d
                    mo_refs[idx][:, cols] = mm
                    vo_refs[idx][:, cols] = vv
            else:
                g = gsum_sc[row:row + rows, pl.ds(pl.multiple_of(q * CS, CS), CS)]
                d, mm, vv = _adamw_math(w_refs[idx][...], g, m_refs[idx][...], v_refs[idx][...])
                g_refs[idx][...] = g
                d_refs[idx][...] = d
                mo_refs[idx][...] = mm
                vo_refs[idx][...] = vv

    shapes = [jax.ShapeDtypeStruct(w.shape, f32) for w in ws]
    return pl.pallas_call(
        body, name="adamw_small",
        out_shape=shapes * 4 + [jax.ShapeDtypeStruct((1, D), f32)],
        scratch_shapes=[pltpu.VMEM((SMALL_ROWS, D), f32)],
        compiler_params=pltpu.CompilerParams(vmem_limit_bytes=VMEM_LIMIT),
    )(packs, *ws, *ms, *vs)


WEIGHTS = ['meta_tokens', 'ffn1_norm', 'ffn1_w_gu', 'ffn1_w_down', 'mix_norm', 'w_in', 'b_in', 'rnn_conv_w',
           'rnn_conv_b', 'rg_w_a', 'rg_b_a', 'rg_w_x', 'rg_b_x', 'rg_lambda', 'rnn_w_proj', 'conv_dw_w',
           'conv_dw_b', 'conv_ln_g', 'conv_ln_b', 'conv_w_proj', 'conv_b_proj', 'w_out', 'ffn2_norm',
           'ffn2_w_gu', 'ffn2_w_down', 'final_norm']


def _as2d(a):
    return a.reshape(-1, a.shape[-1])


def _step(x, loss_target, w, m, v):
    seq = x.shape[1]
    n_valid = NMETA + seq
    t = -(-n_valid // TM) * TM

    qc = jnp.stack([2 * lax.axis_index("x") + lax.axis_index("y"), lax.axis_index("c")]).astype(jnp.int32)
    p = {k: w[k].reshape(1, rows * D) for k, rows in REPL}

    first = ["ffn1_w_gu", "ffn1_w_down", "small"]
    later = [["w_in"], ["rg_w_a", "rg_w_x", "rnn_w_proj", "conv_w_proj", "w_out"], ["ffn2_w_gu", "ffn2_w_down"]]
    small_rows = sum(r for _, r in COLSH)
    small = jnp.concatenate([_as2d(w[k]) for k, _ in COLSH] + [jnp.zeros((64 - small_rows, CS), f32)], axis=0)

    def cast(k, token=None):
        src, dtype = (small, f32) if k == "small" else (_as2d(w[k]), bf16)
        return _cast_into_slot(src, qc, dtype, "cast_" + k, after=token)

    send1, recv1, bufs1, token1 = _gather_start([cast(k) for k in first], "gather_start_first")
    rest = [k for grp in later for k in grp]
    send2, recv2, bufs2, token2 = _gather_start([cast(k, token1) for k in rest], "gather_start_rest")

    def install(names, done):
        for k, b in zip(names, done):
            full = b.reshape(NCHIP, 2 * b.shape[2], b.shape[3])
            if k in ("ffn1_w_down", "ffn2_w_down"):
                full = full.reshape(F, D)
            elif k in ("rnn_w_proj", "conv_w_proj", "w_out"):
                full = full.reshape(D, D)
            elif k in ("rg_w_a", "rg_w_x"):
                full = full.reshape(NCHIP, NHEAD, HD // NCHIP, HD).transpose(1, 0, 2, 3).reshape(NHEAD, HD, HD)
            p[k] = full

    def finish(names, send, recv, bufs, after, tag):
        install(names, _forward_halves(_gather_wait(send, recv, bufs, after, "gather_wait_" + tag),
                                       "gather_forward_" + tag))

    def group(names):
        idx = [rest.index(k) for k in names]
        return names, [send2[i] for i in idx], [recv2[i] for i in idx], [bufs2[i] for i in idx]

    h0 = jnp.pad(x[0] + token1[0:1, 0:1], ((NMETA, t - n_valid), (0, 0)))
    tgt = jnp.pad(loss_target[0] + token2[0:1, 0:1], ((NMETA, t - n_valid), (0, 0)))
    finish(first, send1, recv1, bufs1, (token2, h0, tgt), "first")
    small_full = p.pop("small").transpose(1, 0, 2).reshape(64, D)
    row = 0
    for k, rows in COLSH:
        p[k] = small_full[row:row + rows]
        row += rows

    h0 = lax.dynamic_update_slice(h0, p["meta_tokens"], (0, 0))
    h1, gate1, up1, a1, n1 = _ffn_fwd(h0, p["ffn1_norm"], p["ffn1_w_gu"], p["ffn1_w_down"], "ffn1_fwd")
    finish(*group(later[0]), h1, "in")
    proj, n2 = _inproj_fwd(h1, p["mix_norm"], p["w_in"], p["b_in"])
    names_l = later[1] + later[2]
    _, send_l, recv_l, bufs_l = group(names_l)
    send_f, recv_f, bufs_f, token = _forward_start(
        _gather_wait(send_l, recv_l, bufs_l, proj, "gather_wait_late"), "gather_forward_start")
    vc, s = _conv_fwd(proj, p["conv_dw_w"], p["conv_dw_b"], p["conv_ln_g"], p["conv_ln_b"], after=token)
    install(names_l, _forward_wait(send_f, recv_f, bufs_f, vc, "gather_forward_wait"))
    xr, hr, z, gates = _rnn_fwd(proj, p["rnn_conv_w"], p["rnn_conv_b"], p["rg_w_a"], p["rg_b_a"],
                         p["rg_w_x"], p["rg_b_x"], p["rg_lambda"])
    h2 = _merge_fwd(h1, z, s, proj, p["rnn_w_proj"], p["conv_w_proj"], p["conv_b_proj"], p["w_out"])
    dh3, loss_blk, d_final, gate2, up2, a2, n3 = _ffn_fwd(
        h2, p["ffn2_norm"], p["ffn2_w_gu"], p["ffn2_w_down"], "ffn2_fwd",
        loss_head=(p["final_norm"], tgt, n_valid))

    g = {"final_norm": d_final}
    pending = []

    def exchange_start(names, tag):
        parts = []
        for k in names:
            rows = g[k].size // (NCHIP * g[k].shape[-1])
            parts.append(g[k].reshape((NCHIP, 2, rows // 2, g[k].shape[-1])))
        send, recv, parts, lands, token = _pair_exchange_start(parts, "pair_exchange_start_" + tag)
        return (names, tag, send, recv, parts, lands), token

    def reduce_start(state, after):
        names, tag, send, recv, parts, lands = state
        parts, from_sibling = _pair_exchange_wait(send, recv, parts, lands, after, "pair_exchange_wait_" + tag)
        sums, lands = _pair_add(parts, from_sibling, qc, "pair_add_" + tag)
        send, recv, sums, lands, token = _reduce_start(sums, lands, "reduce_start_" + tag)
        pending.append((names, tag, send, recv, sums, lands))
        return token

    dh2, dgate2, dup2, df2, g["ffn2_norm"] = _ffn_bwd(
        dh3, h2, p["ffn2_norm"], gate2, up2, p["ffn2_w_gu"], p["ffn2_w_down"], "ffn2_bwd")
    g["ffn2_w_gu"] = _ffn_gu_grad(n3, dgate2, dup2, "ffn2")
    g["ffn2_w_down"] = _ffn_down_grad(a2, df2, "ffn2")
    state, token = exchange_start(["ffn2_w_gu", "ffn2_w_down"], "ffn2")

    dz, ds, dproj, dh2b, merged, dya, dyb, g["conv_b_proj"] = _merge_bwd(
        dh2, z, s, proj, p["rnn_w_proj"], p["conv_w_proj"], p["conv_b_proj"], p["w_out"], after=token)
    token = reduce_start(state, dz)
    dproj, g["conv_dw_w"], g["conv_dw_b"], g["conv_ln_g"], g["conv_ln_b"] = _conv_bwd(
        ds, vc, proj, dproj, p["conv_dw_w"], p["conv_ln_g"], p["conv_ln_b"], after=token)
    g["w_out"] = _square_grad(merged, dh2b, "dw_out")
    g["rnn_w_proj"] = _square_grad(z, dya, "dw_rnn_proj")
    g["conv_w_proj"] = _square_grad(s, dyb, "dw_conv_proj")
    (dproj, g["rg_w_a"], g["rg_w_x"], g["rnn_conv_w"], g["rnn_conv_b"], g["rg_b_a"], g["rg_b_x"],
     g["rg_lambda"]) = _rnn_bwd(dz, xr, hr, gates, proj, dproj, p["rnn_conv_w"], p["rg_w_a"],
                                p["rg_w_x"], p["rg_lambda"])

    dh1, g["mix_norm"], db_in = _inproj_bwd(dproj, dh2, h1, p["mix_norm"], p["w_in"])
    g["b_in"] = db_in.reshape(1, NIN)
    g["w_in"] = _tn_matmul(n2, dproj, D, NIN // NCHIP, (NCHIP, D, NIN // NCHIP),
                           (None, D, NIN // NCHIP), lambda k, nn, mm: (nn, 0, 0), "dw_in")
    state, token = exchange_start(["w_out", "rnn_w_proj", "conv_w_proj", "rg_w_a", "rg_w_x", "w_in"], "mix")

    dh0, dgate1, dup1, df1, g["ffn1_norm"] = _ffn_bwd(
        dh1, h0, p["ffn1_norm"], gate1, up1, p["ffn1_w_gu"], p["ffn1_w_down"], "ffn1_bwd", after=token)
    g["meta_tokens"] = dh0[0:NMETA]
    grad_x = dh0[NMETA:n_valid][None]
    token = reduce_start(state, dh0)

    send_s, recv_s, pack_buf, token_s = _gather_all_start(
        _place_pack(_small_pack(g, loss_blk.reshape(1, D)), qc), "gather_all_start")
    g["ffn1_w_down"] = _ffn_down_grad(a1, df1, "ffn1", after=(token, token_s))
    state, token = exchange_start(["ffn1_w_down"], "ffn1_down")
    gate_half = _tn_matmul(n1, dgate1, D, FS, (NCHIP, D, FS), (None, D, FS), lambda k, nn, mm: (nn, 0, 0),
                           "ffn1_dwg", after=token)
    token = reduce_start(state, gate_half)
    g["ffn1_w_gu"] = _tn_matmul(n1, dup1, D, FS, (NCHIP, D, FS), (None, D, FS), lambda k, nn, mm: (2 + nn, 0, 0),
                                "ffn1_dwu", base=gate_half, after=token)
    state_gu, token = exchange_start(["ffn1_w_gu"], "ffn1_gu")
    packs = _gather_all_wait(send_s, recv_s, pack_buf, token, "gather_all_wait")

    grads, deltas, new_m, new_v = {}, {}, {}, {}

    def landed_sums(items, after):
        names, mine = [], []
        for grp_names, grp_tag, send, recv, sums, lands in items:
            landed = _reduce_wait(send, recv, sums, lands, after, "reduce_wait_" + grp_tag)
            mine += _sum_chips(landed, "sum_chips_" + grp_tag)
            names += grp_names
            after = mine[-1]
        return names, mine

    def share_and_update(names, mine, tag, after=None):
        theirs = _pair_share(mine, "pair_share_" + tag, after=after)
        got = dict(zip(names, zip(mine, theirs)))
        square = [k for k in names if got[k][0].shape[0] * 2 <= HD]
        for batch in [[k] for k in names if k not in square] + ([square] if square else []):
            outs = _adamw([_as2d(w[k]) for k in batch], [got[k][0] for k in batch], [got[k][1] for k in batch],
                          [_as2d(m[k]) for k in batch], [_as2d(v[k]) for k in batch], qc,
                          "adamw_" + (batch[0] if len(batch) == 1 else "mixer"))
            for k, out in zip(batch, outs):
                grads[k], deltas[k], new_m[k], new_v[k] = (a.reshape(w[k].shape) for a in out)
        return [new_v[k] for k in names]

    early_names, early_mine = landed_sums(pending[:2], packs)
    token = reduce_start(state_gu, early_mine[-1])
    after = share_and_update(early_names, early_mine, "early", after=token)
    share_and_update(*landed_sums(pending[2:], after), "late")
    names = [k for k, _ in SMALL]
    shape2 = {k: ((1, rows * D) if (k, rows) in REPL else (rows, CS)) for k, rows in SMALL}
    outs = _adamw_small(packs, *[[a[k].reshape(shape2[k]) for k in names] for a in (w, m, v)])
    ns = len(names)
    for i, k in enumerate(names):
        grads[k], deltas[k], new_m[k], new_v[k] = (outs[j * ns + i].reshape(w[k].shape) for j in range(4))

    loss = outs[4 * ns][0, 0]
    return (loss, grad_x, *[grads[k] for k in WEIGHTS], *[deltas[k] for k in WEIGHTS],
            *[new_m[k] for k in WEIGHTS], *[new_v[k] for k in WEIGHTS])


def kernel(x, meta_tokens, ffn1_norm, ffn1_w_gu, ffn1_w_down, mix_norm, w_in, b_in, rnn_conv_w, rnn_conv_b, rg_w_a, rg_b_a, rg_w_x, rg_b_x, rg_lambda, rnn_w_proj, conv_dw_w, conv_dw_b, conv_ln_g, conv_ln_b, conv_w_proj, conv_b_proj, w_out, ffn2_norm, ffn2_w_gu, ffn2_w_down, final_norm, loss_target, m_meta_tokens, m_ffn1_norm, m_ffn1_w_gu, m_ffn1_w_down, m_mix_norm, m_w_in, m_b_in, m_rnn_conv_w, m_rnn_conv_b, m_rg_w_a, m_rg_b_a, m_rg_w_x, m_rg_b_x, m_rg_lambda, m_rnn_w_proj, m_conv_dw_w, m_conv_dw_b, m_conv_ln_g, m_conv_ln_b, m_conv_w_proj, m_conv_b_proj, m_w_out, m_ffn2_norm, m_ffn2_w_gu, m_ffn2_w_down, m_final_norm, v_meta_tokens, v_ffn1_norm, v_ffn1_w_gu, v_ffn1_w_down, v_mix_norm, v_w_in, v_b_in, v_rnn_conv_w, v_rnn_conv_b, v_rg_w_a, v_rg_b_a, v_rg_w_x, v_rg_b_x, v_rg_lambda, v_rnn_w_proj, v_conv_dw_w, v_conv_dw_b, v_conv_ln_g, v_conv_ln_b, v_conv_w_proj, v_conv_b_proj, v_w_out, v_ffn2_norm, v_ffn2_w_gu, v_ffn2_w_down, v_final_norm):
    args = locals()
    w = {k: args[k] for k in WEIGHTS}
    m = {k: args["m_" + k] for k in WEIGHTS}
    v = {k: args["v_" + k] for k in WEIGHTS}
    return _step(x, loss_target, w, m, v)
```

```python
import jax
import jax.numpy as jnp
from jax import lax
from jax.experimental import pallas as pl
from jax.experimental.pallas import tpu as pltpu

f32 = jnp.float32
bf16 = jnp.bfloat16

D = 1024
F = 2816
FS = F // 2
NIN = 6 * D
NMETA = 16
NHEAD = 4
HD = D // NHEAD
KC4 = 4
KC31 = 31
HALO = 32
EPS = 1e-6
TM = 416
NCHIP = 4
MESH = pl.DeviceIdType.MESH

ADAM_LR = 0.001
ADAM_B1 = 0.9
ADAM_B2 = 0.999
ADAM_EPS = 1e-08
ADAM_WD = 0.01
ADAM_STEP = 10

VMEM_LIMIT = 56 * 1024 * 1024
FSUB = [(o, min(256, FS - o)) for o in range(0, FS, 256)]


def _cp(n_axes, **kw):
    return pltpu.CompilerParams(dimension_semantics=("arbitrary",) * n_axes,
                                vmem_limit_bytes=VMEM_LIMIT, **kw)


RESIDENT = pl.BlockSpec(memory_space=pltpu.VMEM)


def _n_after(after):
    return 0 if after is None else (len(after) if isinstance(after, (tuple, list)) else 1)


def _ordered(body, in_specs, args, after):
    if after is None:
        return body, in_specs, args
    extra = tuple(after) if isinstance(after, (tuple, list)) else (after,)
    return (lambda *refs: body(*refs[len(extra):]),
            [pl.BlockSpec(memory_space=pl.ANY)] * len(extra) + list(in_specs), extra + tuple(args))


def _nt_dot(a, b):
    return lax.dot_general(a, b, (((1,), (1,)), ((), ())), preferred_element_type=f32)


def _tn_dot(a, b):
    return lax.dot_general(a, b, (((0,), (0,)), ((), ())), preferred_element_type=f32)


def _sigmoid(x):
    return 0.5 * jnp.tanh(0.5 * x) + 0.5


def _log1p(y):
    u = 1.0 + y
    d = u - 1.0
    return jnp.where(d == 0.0, y, jnp.log(u) * (y / jnp.where(d == 0.0, 1.0, d)))


def _softplus(x):
    return jnp.maximum(x, 0.0) + _log1p(jnp.exp(-jnp.abs(x)))


def _one_minus_square(a, log_a):
    x = 2.0 * log_a
    series = x * (1.0 + x * (0.5 + x * (1.0 / 6.0)))
    return jnp.where(jnp.abs(x) < 0.03, -series, 1.0 - a * a)


_GELU_C = 0.7978845608028654
_GELU_K = 0.044715


def _gelu_and_grad(y):
    y2 = y * y
    th = jnp.tanh(_GELU_C * (y + _GELU_K * y * y2))
    gel = 0.5 * y * (1.0 + th)
    dgel = 0.5 * (1.0 + th) + 0.5 * y * (1.0 - th * th) * _GELU_C * (1.0 + 3.0 * _GELU_K * y2)
    return gel, dgel


def _rms_stats(h):
    return lax.rsqrt(jnp.mean(h * h, axis=-1, keepdims=True) + EPS)


def _rms_bwd(dn, h, g):
    r = _rms_stats(h)
    nhat = h * r
    dnh = dn * g
    dh = r * (dnh - nhat * jnp.mean(dnh * nhat, axis=-1, keepdims=True))
    dg = jnp.sum(dn * nhat, axis=0, keepdims=True)
    return dh, dg


def _row_ids(shape):
    return lax.broadcasted_iota(jnp.int32, shape, 0)


def _ffn_fwd(h, g, wgu, wd, name, loss_head=None):
    t = h.shape[0]
    nj = 2
    tm = TM
    n_head = 0 if loss_head is None else 2

    def body(*refs):
        h_ref, g_ref, wg_ref, wd_ref = refs[:4]
        outs = refs[4 + n_head:]
        gate_ref, up_ref, a_ref, n_ref = outs[-4:]
        i = pl.program_id(0)
        hh = h_ref[...]
        nb = (hh * _rms_stats(hh) * g_ref[...]).astype(bf16)
        n_ref[...] = nb

        acc = None
        for j in range(nj):
            for off, width in FSUB:
                cols = slice(off, off + width)
                out_cols = slice(j * FS + off, j * FS + off + width)
                gt = jnp.dot(nb, wg_ref[j, :, cols], preferred_element_type=f32)
                up = jnp.dot(nb, wg_ref[2 + j, :, cols], preferred_element_type=f32)
                gate_ref[:, out_cols] = gt.astype(bf16)
                up_ref[:, out_cols] = up.astype(bf16)
                a_ref[:, out_cols] = (gt * _sigmoid(gt) * up).astype(bf16)
            part = jnp.dot(a_ref[:, j * FS:(j + 1) * FS], wd_ref[j], preferred_element_type=f32)
            acc = part if acc is None else acc + part
        hh = hh + 0.5 * acc

        if loss_head is None:
            outs[0][...] = hh
        else:
            gf_ref, t_ref = refs[4:6]
            dh_ref, loss_ref, dgf_ref = outs[:3]

            @pl.when(i == 0)
            def _():
                loss_ref[...] = jnp.zeros_like(loss_ref)
                dgf_ref[...] = jnp.zeros_like(dgf_ref)

            gf = gf_ref[...]
            row = i * tm + _row_ids((tm, 1))
            valid = jnp.logical_and(row >= NMETA, row < loss_head[2])
            err = jnp.where(valid, hh * _rms_stats(hh) * gf - t_ref[...], 0.0)
            loss_ref[...] += 0.5 * jnp.sum(err * err) * (1.0 / D)
            dh, dgf = _rms_bwd(err * (1.0 / D), hh, gf)
            dh_ref[...] = dh
            dgf_ref[...] += dgf

    rowd = pl.BlockSpec((tm, D), lambda i: (i, 0))
    vec = pl.BlockSpec((1, D), lambda i: (0, 0))
    rowf = pl.BlockSpec((tm, F), lambda i: (i, 0))
    in_specs, args = [rowd, vec, RESIDENT, RESIDENT], [h, g, wgu, wd.reshape(nj, FS, D)]
    out_specs, out_shape = [rowd], [jax.ShapeDtypeStruct((t, D), f32)]
    if loss_head is not None:
        in_specs, args = in_specs + [vec, rowd], args + [loss_head[0], loss_head[1]]
        out_specs += [pl.BlockSpec((8, 128), lambda i: (0, 0)), vec]
        out_shape += [jax.ShapeDtypeStruct((8, 128), f32), jax.ShapeDtypeStruct((1, D), f32)]
    return pl.pallas_call(
        body, name=name, grid=(t // tm,),
        in_specs=in_specs,
        out_specs=out_specs + [rowf, rowf, rowf, rowd],
        out_shape=out_shape + [jax.ShapeDtypeStruct((t, F), bf16)] * 3 + [jax.ShapeDtypeStruct((t, D), bf16)],
        compiler_params=_cp(1),
    )(*args)


def _inproj_fwd(h, g, win, b_in):
    t = h.shape[0]
    tn = NIN // NCHIP

    def body(h_ref, g_ref, w_ref, b_ref, proj_ref, n_ref):
        hh = h_ref[...]
        nb = (hh * _rms_stats(hh) * g_ref[...]).astype(bf16)
        n_ref[...] = nb
        for s in range(NCHIP):
            cols = slice(s * tn, (s + 1) * tn)
            proj_ref[:, cols] = jnp.dot(nb, w_ref[s], preferred_element_type=f32) + b_ref[:, cols]

    return pl.pallas_call(
        body, name="inproj_fwd", grid=(t // TM,),
        in_specs=[
            pl.BlockSpec((TM, D), lambda i: (i, 0)),
            pl.BlockSpec((1, D), lambda i: (0, 0)),
            RESIDENT,
            pl.BlockSpec((1, NIN), lambda i: (0, 0)),
        ],
        out_specs=[
            pl.BlockSpec((TM, NIN), lambda i: (i, 0)),
            pl.BlockSpec((TM, D), lambda i: (i, 0)),
        ],
        out_shape=[jax.ShapeDtypeStruct((t, NIN), f32), jax.ShapeDtypeStruct((t, D), bf16)],
        compiler_params=_cp(1),
    )(h, g, win, b_in)


def _block_gates(xr, wa_ref, ba, wx_ref, bx, lam):
    xrb = xr.astype(bf16)
    pa = jnp.concatenate([jnp.dot(xrb[:, hh * HD:(hh + 1) * HD], wa_ref[hh], preferred_element_type=f32)
                          for hh in range(NHEAD)], axis=1)
    px = jnp.concatenate([jnp.dot(xrb[:, hh * HD:(hh + 1) * HD], wx_ref[hh], preferred_element_type=f32)
                          for hh in range(NHEAD)], axis=1)
    ra = _sigmoid(pa + ba)
    ii = _sigmoid(px + bx)
    sp = _softplus(-lam)
    log_a = -8.0 * ra * sp
    a = jnp.exp(log_a)
    sq = jnp.sqrt(_one_minus_square(a, log_a))
    return ra, ii, a, sq, sp


LT = D // 128
UNR = 13


def _to_lane_tiles(ref, value):
    for lt in range(LT):
        ref[lt] = value[:, lt * 128:(lt + 1) * 128]


def _from_lane_tiles(ref):
    return jnp.concatenate([ref[lt] for lt in range(LT)], axis=1)


def _chain_scan(mult_sc, val_sc, start, reverse):
    ng = TM // 8

    def lanes(lt):
        return slice(lt * 128, (lt + 1) * 128)

    def chain(gi, carry):
        v_prev, p_prev = carry
        rows = pl.ds(ng - 1 - gi if reverse else gi, 8, stride=ng)
        v_new, p_new = [], []
        for lt in range(LT):
            mm = mult_sc.at[lt][rows, :]
            vv = mm * v_prev[:, lanes(lt)] + val_sc.at[lt][rows, :]
            pp = mm * p_prev[:, lanes(lt)]
            val_sc.at[lt][rows, :] = vv
            mult_sc.at[lt][rows, :] = pp
            v_new.append(vv)
            p_new.append(pp)
        return jnp.concatenate(v_new, axis=1), jnp.concatenate(p_new, axis=1)

    v_end, p_end = lax.fori_loop(0, ng, chain, (jnp.zeros((8, D), f32), jnp.ones((8, D), f32)))
    state, entries = start, [None] * 8
    for r in (reversed(range(8)) if reverse else range(8)):
        entries[r] = state
        state = v_end[r:r + 1, :] + p_end[r:r + 1, :] * state
    entry8 = jnp.concatenate(entries, axis=0)

    def add_entry(gi, carry):
        rows = pl.ds(gi, 8, stride=ng)
        for lt in range(LT):
            val_sc.at[lt][rows, :] = val_sc.at[lt][rows, :] + mult_sc.at[lt][rows, :] * entry8[:, lanes(lt)]
        return carry

    lax.fori_loop(0, ng, add_entry, 0)
    return state


def _strided_conv(w_ref, src_sc, out_sc, base, shifts, bias_ref=None):
    ng = TM // 8
    for lt in range(LT):
        lanes = slice(lt * 128, (lt + 1) * 128)
        taps = [jnp.broadcast_to(w_ref[k:k + 1, lanes], (8, 128)) for k in range(len(shifts))]
        init = (jnp.zeros((8, 128), f32) if bias_ref is None
                else jnp.broadcast_to(bias_ref[:, lanes], (8, 128)))

        def step(gb, carry, lt=lt, taps=taps, init=init):
            accs = [init] * UNR
            for k, shift in enumerate(shifts):
                for u in range(UNR):
                    rows = pl.ds(base + gb * UNR + u + shift, 8, stride=ng)
                    accs[u] = accs[u] + taps[k] * src_sc.at[lt][rows, :]
            for u in range(UNR):
                out_sc.at[lt][pl.ds(gb * UNR + u, 8, stride=ng), :] = accs[u]
            return carry

        lax.fori_loop(0, ng // UNR, step, 0)


def _strided_corr(a_sc, src_sc, base, shifts):
    ng = TM // 8
    per_tile = []
    for lt in range(LT):
        def step(gb, accs, lt=lt):
            accs = list(accs)
            for u in range(UNR):
                g = gb * UNR + u
                a_g = a_sc.at[lt][pl.ds(g, 8, stride=ng), :]
                for k, shift in enumerate(shifts):
                    accs[k] = accs[k] + a_g * src_sc.at[lt][pl.ds(base + g + shift, 8, stride=ng), :]
            return tuple(accs)

        per_tile.append(lax.fori_loop(0, ng // UNR, step, tuple(jnp.zeros((8, 128), f32) for _ in shifts)))
    return [jnp.concatenate([per_tile[lt][k] for lt in range(LT)], axis=1) for k in range(len(shifts))]


def _rnn_fwd(proj, cw, cb, wa, ba, wx, bx, lam):
    t = proj.shape[0]

    def body(x_ref, y_ref, cw_ref, cb_ref, wa_ref, ba_ref, wx_ref, bx_ref, lam_ref,
             xr_ref, hr_ref, z_ref, gates_ref, xext_sc, carry_sc, a_sc, h_sc):
        i = pl.program_id(0)

        @pl.when(i == 0)
        def _():
            xext_sc[0:8, :] = jnp.zeros((8, D), f32)
            carry_sc[...] = jnp.zeros_like(carry_sc)

        x = x_ref[...]
        xext_sc[8:8 + TM, :] = x
        xe = xext_sc[...]
        xr = cb_ref[...] + cw_ref[KC4 - 1:KC4, :] * x
        for k in range(KC4 - 1):
            xr = xr + cw_ref[k:k + 1, :] * pltpu.roll(xe, KC4 - 1 - k, 0)[8:8 + TM]
        xext_sc[0:8, :] = x[TM - 8:TM]

        ra, ii, a, sq, _ = _block_gates(xr, wa_ref, ba_ref[...], wx_ref, bx_ref[...], lam_ref[...])
        for slot, val in enumerate((ra, ii, a, sq)):
            gates_ref[slot] = val
        _to_lane_tiles(a_sc, a)
        _to_lane_tiles(h_sc, sq * ii * xr)
        carry_sc[...] = _chain_scan(a_sc, h_sc, carry_sc[...], reverse=False)
        hr = _from_lane_tiles(h_sc)
        gel, _ = _gelu_and_grad(y_ref[...])
        xr_ref[...] = xr
        hr_ref[...] = hr
        z_ref[...] = (hr * gel).astype(bf16)

    vec = pl.BlockSpec((1, D), lambda i: (0, 0))
    return pl.pallas_call(
        body, name="rnn_fwd", grid=(t // TM,),
        in_specs=[
            pl.BlockSpec((TM, D), lambda i: (i, 0)),
            pl.BlockSpec((TM, D), lambda i: (i, 1)),
            pl.BlockSpec((KC4, D), lambda i: (0, 0)),
            vec,
            pl.BlockSpec((NHEAD, HD, HD), lambda i: (0, 0, 0)),
            vec,
            pl.BlockSpec((NHEAD, HD, HD), lambda i: (0, 0, 0)),
            vec, vec,
        ],
        out_specs=[pl.BlockSpec((TM, D), lambda i: (i, 0))] * 3 + [pl.BlockSpec((4, TM, D), lambda i: (0, i, 0))],
        out_shape=[jax.ShapeDtypeStruct((t, D), f32), jax.ShapeDtypeStruct((t, D), f32),
                   jax.ShapeDtypeStruct((t, D), bf16), jax.ShapeDtypeStruct((4, t, D), f32)],
        scratch_shapes=[pltpu.VMEM((TM + 8, D), f32), pltpu.VMEM((1, D), f32),
                        pltpu.VMEM((LT, TM, 128), f32), pltpu.VMEM((LT, TM, 128), f32)],
        compiler_params=_cp(1),
    )(proj, proj, cw, cb, wa, ba, wx, bx, lam)


def _ln_stats(vc):
    mu = jnp.mean(vc, axis=-1, keepdims=True)
    xc = vc - mu
    rstd = lax.rsqrt(jnp.mean(xc * xc, axis=-1, keepdims=True) + EPS)
    return xc * rstd, rstd


def _conv_fwd(proj, w31, b31, ln_g, ln_b, after=None):
    t = proj.shape[0]

    def body(gv_ref, gg_ref, w_ref, b_ref, lg_ref, lb_ref, vc_ref, s_ref, vext_sc, out_sc):
        i = pl.program_id(0)

        @pl.when(i == 0)
        def _():
            vext_sc[:, 0:HALO, :] = jnp.zeros((LT, HALO, 128), f32)

        v = gv_ref[...] * _sigmoid(gg_ref[...])
        for lt in range(LT):
            vext_sc[lt, HALO:HALO + TM, :] = v[:, lt * 128:(lt + 1) * 128]

        _strided_conv(w_ref, vext_sc, out_sc, HALO, [k - (KC31 - 1) for k in range(KC31)], b_ref)
        for lt in range(LT):
            vext_sc[lt, 0:HALO, :] = v[TM - HALO:TM, lt * 128:(lt + 1) * 128]
        acc = _from_lane_tiles(out_sc)
        xhat, _ = _ln_stats(acc)
        ln = xhat * lg_ref[...] + lb_ref[...]
        vc_ref[...] = acc
        s_ref[...] = (ln * _sigmoid(ln)).astype(bf16)

    vec = pl.BlockSpec((1, D), lambda i: (0, 0))
    body, in_specs, args = _ordered(
        body,
        [pl.BlockSpec((TM, D), lambda i: (i, 2)),
         pl.BlockSpec((TM, D), lambda i: (i, 3)),
         pl.BlockSpec((KC31, D), lambda i: (0, 0)),
         vec, vec, vec],
        (proj, proj, w31, b31, ln_g, ln_b), after)
    return pl.pallas_call(
        body, name="conv_fwd", grid=(t // TM,),
        in_specs=in_specs,
        out_specs=[pl.BlockSpec((TM, D), lambda i: (i, 0))] * 2,
        out_shape=[jax.ShapeDtypeStruct((t, D), f32), jax.ShapeDtypeStruct((t, D), bf16)],
        scratch_shapes=[pltpu.VMEM((LT, TM + HALO, 128), f32), pltpu.VMEM((LT, TM, 128), f32)],
        compiler_params=_cp(1),
    )(*args)


def _merge_fwd(h, z, s, proj, wrp, wcp, bcp, wout):
    t = h.shape[0]

    def body(h_ref, z_ref, s_ref, ga_ref, gb_ref, wrp_ref, wcp_ref, bcp_ref, wout_ref, ho_ref):
        ya = jnp.dot(z_ref[...], wrp_ref[...], preferred_element_type=f32)
        yb = jnp.dot(s_ref[...], wcp_ref[...], preferred_element_type=f32) + bcp_ref[...]
        merged = _sigmoid(ga_ref[...]) * ya + _sigmoid(gb_ref[...]) * yb
        ho_ref[...] = h_ref[...] + jnp.dot(merged.astype(bf16), wout_ref[...], preferred_element_type=f32)

    row = pl.BlockSpec((TM, D), lambda i: (i, 0))
    wsq = pl.BlockSpec((D, D), lambda i: (0, 0))
    return pl.pallas_call(
        body, name="merge_fwd", grid=(t // TM,),
        in_specs=[row, row, row,
                  pl.BlockSpec((TM, D), lambda i: (i, 4)),
                  pl.BlockSpec((TM, D), lambda i: (i, 5)),
                  wsq, wsq, pl.BlockSpec((1, D), lambda i: (0, 0)), wsq],
        out_specs=row,
        out_shape=jax.ShapeDtypeStruct((t, D), f32),
        compiler_params=_cp(1),
    )(h, z, s, proj, proj, wrp, wcp, bcp, wout)


def _ffn_bwd(dh, h, g, gate, up, wgu, wd, name, after=None, grad_rows=None):
    t = h.shape[0]
    nj = 2
    nt = t // TM
    if grad_rows is not None:
        first, stop = grad_rows
        last_rows = stop - (nt - 1) * TM
        assert nt >= 2 and first % 8 == 0 and last_rows % 8 == 0 and 0 < first < TM and 0 < last_rows <= TM

    def body(dh_ref, h_ref, g_ref, gate_ref, up_ref, wg_ref, wd_ref,
             dhi_ref, dgate_ref, dup_ref, df_ref, dg_ref, *extra):
        @pl.when(pl.program_id(0) == 0)
        def _():
            dg_ref[...] = jnp.zeros_like(dg_ref)

        dh = dh_ref[...]
        dfb = (0.5 * dh).astype(bf16)
        df_ref[...] = dfb

        dn = None
        for j in range(nj):
            half = slice(j * FS, (j + 1) * FS)
            for off, width in FSUB:
                cols = slice(off, off + width)
                out_cols = slice(j * FS + off, j * FS + off + width)
                da = _nt_dot(dfb, wd_ref[j, cols, :])
                gt = gate_ref[:, out_cols].astype(f32)
                uu = up_ref[:, out_cols].astype(f32)
                sg = _sigmoid(gt)
                dgate_ref[:, out_cols] = (da * uu * (sg * (1.0 + gt * (1.0 - sg)))).astype(bf16)
                dup_ref[:, out_cols] = (da * (gt * sg)).astype(bf16)
            part = _nt_dot(dgate_ref[:, half], wg_ref[j]) + _nt_dot(dup_ref[:, half], wg_ref[2 + j])
            dn = part if dn is None else dn + part
        dhin, dg = _rms_bwd(dn, h_ref[...], g_ref[...])
        dg_ref[...] += dg

        if grad_rows is None:
            dhi_ref[...] = dh + dhin
        else:
            gx_ref, stage_sc, sem = extra
            i = pl.program_id(0)

            def copy(tile, src0, rows):
                return pltpu.make_async_copy(
                    stage_sc.at[pl.ds(src0, rows)],
                    gx_ref.at[pl.ds(pl.multiple_of(tile * TM + src0 - first, 8), rows)],
                    sem.at[0])

            @pl.when(i == 1)
            def _():
                copy(0, first, TM - first).wait()

            @pl.when(i > 1)
            def _():
                copy(i - 1, 0, TM).wait()

            dhi = dh + dhin
            dhi_ref[...] = dhi
            stage_sc[...] = dhi

            @pl.when(i == 0)
            def _():
                copy(0, first, TM - first).start()

            @pl.when(jnp.logical_and(i > 0, i < nt - 1))
            def _():
                copy(i, 0, TM).start()

            @pl.when(i == nt - 1)
            def _():
                tail = copy(nt - 1, 0, last_rows)
                tail.start()
                tail.wait()

    rowd = pl.BlockSpec((TM, D), lambda i: (i, 0))
    rowf = pl.BlockSpec((TM, F), lambda i: (i, 0))
    vec = pl.BlockSpec((1, D), lambda i: (0, 0))
    body, in_specs, args = _ordered(
        body,
        [rowd, rowd, vec, rowf, rowf, RESIDENT, RESIDENT],
        (dh, h, g, gate, up, wgu, wd.reshape(nj, FS, D)), after)
    out_specs = [rowd, rowf, rowf, rowd, vec]
    out_shape = [jax.ShapeDtypeStruct((t, D), f32), jax.ShapeDtypeStruct((t, F), bf16),
                 jax.ShapeDtypeStruct((t, F), bf16),
                 jax.ShapeDtypeStruct((t, D), bf16), jax.ShapeDtypeStruct((1, D), f32)]
    scratch = []
    if grad_rows is not None:
        out_specs.append(pl.BlockSpec(memory_space=pl.ANY))
        out_shape.append(jax.ShapeDtypeStruct((stop - first, D), f32))
        scratch += [pltpu.VMEM((TM, D), f32), pltpu.SemaphoreType.DMA((1,))]
    return pl.pallas_call(
        body, name=name, grid=(nt,),
        in_specs=in_specs,
        out_specs=out_specs,
        out_shape=out_shape,
        scratch_shapes=scratch,
        compiler_params=_cp(1),
    )(*args)


def _big_tile(t):
    return max(k * TM for k in range(1, 6) if t % (k * TM) == 0)


ANY_SPEC = pl.BlockSpec(memory_space=pl.ANY)


def _tn_matmul(a, b, tk, tn, out_shape, out_block, out_map, name, base=None, after=None):
    t, kk = a.shape
    _, nn = b.shape
    tmm = _big_tile(t)
    nm = t // tmm

    def body(a_ref, b_ref, o_ref, acc_sc):
        m = pl.program_id(2)

        @pl.when(m == 0)
        def _():
            acc_sc[...] = jnp.zeros_like(acc_sc)

        acc_sc[...] += _tn_dot(a_ref[...], b_ref[...])

        @pl.when(m == nm - 1)
        def _():
            o_ref[...] = acc_sc[...].astype(o_ref.dtype)

    in_specs = [pl.BlockSpec((tmm, tk), lambda k, n, m: (m, k)),
                pl.BlockSpec((tmm, tn), lambda k, n, m: (m, n))]
    args, aliases = (a, b), {}
    if base is not None:
        body = (lambda inner: lambda a_ref, b_ref, base_ref, o_ref, acc_sc: inner(a_ref, b_ref, o_ref, acc_sc))(body)
        in_specs, args, aliases = in_specs + [ANY_SPEC], (a, b, base), {2: 0}
    if after is not None:
        body, in_specs, args = _ordered(body, in_specs, args, after)
        aliases = {k + _n_after(after): v for k, v in aliases.items()}
    return pl.pallas_call(
        body, name=name, grid=(kk // tk, nn // tn, nm),
        in_specs=in_specs,
        out_specs=pl.BlockSpec(out_block, out_map),
        out_shape=jax.ShapeDtypeStruct(out_shape, bf16),
        scratch_shapes=[pltpu.VMEM((tk, tn), f32)],
        input_output_aliases=aliases,
        compiler_params=_cp(3),
    )(*args)


def _merge_bwd(dh, z, s, proj, wrp, wcp, bcp, wout, after=None):
    t = dh.shape[0]

    def body(dh_ref, z_ref, s_ref, ga_ref, gb_ref, wrp_ref, wcp_ref, bcp_ref, wout_ref,
             dz_ref, ds_ref, dgab_ref, dhb_ref, mg_ref, dya_ref, dyb_ref, dbcp_ref):
        i = pl.program_id(0)

        @pl.when(i == 0)
        def _():
            dbcp_ref[...] = jnp.zeros_like(dbcp_ref)

        dhb = dh_ref[...].astype(bf16)
        dhb_ref[...] = dhb
        dmg = _nt_dot(dhb, wout_ref[...])
        ya = jnp.dot(z_ref[...], wrp_ref[...], preferred_element_type=f32)
        yb = jnp.dot(s_ref[...], wcp_ref[...], preferred_element_type=f32) + bcp_ref[...]
        sa = _sigmoid(ga_ref[...])
        sb = _sigmoid(gb_ref[...])
        mg_ref[...] = (sa * ya + sb * yb).astype(bf16)
        dgab_ref[:, 0:D] = (dmg * ya * sa * (1.0 - sa)).astype(bf16)
        dgab_ref[:, D:2 * D] = (dmg * yb * sb * (1.0 - sb)).astype(bf16)
        dya = dmg * sa
        dyb = dmg * sb
        dbcp_ref[...] += jnp.sum(dyb, axis=0, keepdims=True)
        dyab = dya.astype(bf16)
        dybb = dyb.astype(bf16)
        dya_ref[...] = dyab
        dyb_ref[...] = dybb
        dz_ref[...] = _nt_dot(dyab, wrp_ref[...])
        ds_ref[...] = _nt_dot(dybb, wcp_ref[...])

    row = pl.BlockSpec((TM, D), lambda i: (i, 0))
    wsq = pl.BlockSpec((D, D), lambda i: (0, 0))
    vec = pl.BlockSpec((1, D), lambda i: (0, 0))
    rowb = jax.ShapeDtypeStruct((t, D), bf16)
    body, in_specs, args = _ordered(
        body,
        [row, row, row,
         pl.BlockSpec((TM, D), lambda i: (i, 4)),
         pl.BlockSpec((TM, D), lambda i: (i, 5)),
         wsq, wsq, vec, wsq],
        (dh, z, s, proj, proj, wrp, wcp, bcp, wout), after)
    return pl.pallas_call(
        body, name="merge_bwd", grid=(t // TM,),
        in_specs=in_specs,
        out_specs=[row, row,
                   pl.BlockSpec((TM, 2 * D), lambda i: (i, 2)),
                   row, row, row, row, vec],
        out_shape=[jax.ShapeDtypeStruct((t, D), f32), jax.ShapeDtypeStruct((t, D), f32),
                   jax.ShapeDtypeStruct((t, NIN), bf16),
                   rowb, rowb, rowb, rowb, jax.ShapeDtypeStruct((1, D), f32)],
        compiler_params=_cp(1),
    )(*args)


def _conv_bwd(ds, vc, proj, dproj, w31, ln_g, ln_b, after=None):
    t = ds.shape[0]
    nt = t // TM
    hb = TM // HALO

    def body(ds_ref, vc_ref, gv_ref, gg_ref, gvp_ref, ggp_ref, dpin_ref, w_ref, lg_ref, lb_ref,
             dgvg_ref, dw_ref, db_ref, dlg_ref, dlb_ref, dext_sc, vext_sc, out_sc, dwacc_sc, small_sc):
        del dpin_ref
        i = pl.program_id(0)
        tile = nt - 1 - i

        @pl.when(i == 0)
        def _():
            dext_sc[:, TM:TM + HALO, :] = jnp.zeros((LT, HALO, 128), f32)
            dwacc_sc[...] = jnp.zeros_like(dwacc_sc)
            small_sc[...] = jnp.zeros_like(small_sc)

        lg = lg_ref[...]
        lb = lb_ref[...]

        xhat, rstd = _ln_stats(vc_ref[...])
        ln = xhat * lg + lb
        sg = _sigmoid(ln)
        dln = ds_ref[...] * (sg * (1.0 + ln * (1.0 - sg)))
        dxh = dln * lg
        dvc = rstd * (dxh - jnp.mean(dxh, axis=-1, keepdims=True)
                      - xhat * jnp.mean(dxh * xhat, axis=-1, keepdims=True))
        small_sc[0] += jnp.sum((dln * xhat).reshape(TM // 8, 8, D), axis=0)
        small_sc[1] += jnp.sum(dln.reshape(TM // 8, 8, D), axis=0)
        small_sc[2] += jnp.sum(dvc.reshape(TM // 8, 8, D), axis=0)
        sgg = _sigmoid(gg_ref[...])
        v = gv_ref[...] * sgg
        vprev = jnp.where(tile > 0, gvp_ref[...] * _sigmoid(ggp_ref[...]), 0.0)
        for lt in range(LT):
            lanes = slice(lt * 128, (lt + 1) * 128)
            dext_sc[lt, 0:TM, :] = dvc[:, lanes]
            vext_sc[lt, HALO:HALO + TM, :] = v[:, lanes]
            vext_sc[lt, 0:HALO, :] = vprev[:, lanes]

        _strided_conv(w_ref, dext_sc, out_sc, 0, [KC31 - 1 - k for k in range(KC31)])
        dv = _from_lane_tiles(out_sc)
        dgvg_ref[:, 0:D] = (dv * sgg).astype(bf16)
        dgvg_ref[:, D:2 * D] = (dv * gv_ref[...] * sgg * (1.0 - sgg)).astype(bf16)

        for k, part in enumerate(_strided_corr(dext_sc, vext_sc, HALO, [k - (KC31 - 1) for k in range(KC31)])):
            dwacc_sc[k] += part
        for lt in range(LT):
            dext_sc[lt, TM:TM + HALO, :] = dext_sc[lt, 0:HALO, :]

        @pl.when(i == nt - 1)
        def _():
            for k in range(KC31):
                dw_ref[k:k + 1, :] = jnp.sum(dwacc_sc[k], axis=0, keepdims=True)
            dlg_ref[...] = jnp.sum(small_sc[0], axis=0, keepdims=True)
            dlb_ref[...] = jnp.sum(small_sc[1], axis=0, keepdims=True)
            db_ref[...] = jnp.sum(small_sc[2], axis=0, keepdims=True)

    rev = lambda i: (nt - 1 - i, 0)
    vec = pl.BlockSpec((1, D), lambda i: (0, 0))
    halo_row = lambda i: jnp.maximum((nt - 1 - i) * hb - 1, 0)
    body, in_specs, args = _ordered(
        body,
        [pl.BlockSpec((TM, D), rev),
         pl.BlockSpec((TM, D), rev),
         pl.BlockSpec((TM, D), lambda i: (nt - 1 - i, 2)),
         pl.BlockSpec((TM, D), lambda i: (nt - 1 - i, 3)),
         pl.BlockSpec((HALO, D), lambda i: (halo_row(i), 2)),
         pl.BlockSpec((HALO, D), lambda i: (halo_row(i), 3)),
         pl.BlockSpec(memory_space=pl.ANY),
         pl.BlockSpec((KC31, D), lambda i: (0, 0)),
         vec, vec],
        (ds, vc, proj, proj, proj, proj, dproj, w31, ln_g, ln_b), after)
    return pl.pallas_call(
        body, name="conv_bwd", grid=(nt,),
        in_specs=in_specs,
        out_specs=[
            pl.BlockSpec((TM, 2 * D), lambda i: (nt - 1 - i, 1)),
            pl.BlockSpec((KC31, D), lambda i: (0, 0)),
            vec, vec, vec,
        ],
        out_shape=[jax.ShapeDtypeStruct((t, NIN), bf16),
                   jax.ShapeDtypeStruct((KC31, D), f32),
                   jax.ShapeDtypeStruct((1, D), f32), jax.ShapeDtypeStruct((1, D), f32),
                   jax.ShapeDtypeStruct((1, D), f32)],
        scratch_shapes=[pltpu.VMEM((LT, TM + HALO, 128), f32), pltpu.VMEM((LT, TM + HALO, 128), f32),
                        pltpu.VMEM((LT, TM, 128), f32), pltpu.VMEM((KC31, 8, D), f32),
                        pltpu.VMEM((3, 8, D), f32)],
        input_output_aliases={6 + _n_after(after): 0},
        compiler_params=_cp(1),
    )(*args)


def _rnn_bwd(dz, xr, hr, gates, proj, dproj, cw, wa, wx, lam):
    t = dz.shape[0]
    nt = t // TM
    ng = TM // 8
    hq = HD // NCHIP

    def body(dz_ref, xr_ref, hr_ref, hrp_ref, x_ref, xp_ref, y_ref, dpin_ref,
             cw_ref, wa_ref, gates_ref, wx_ref, lam_ref,
             dxy_ref, dwa_ref, dwx_ref, dcw_ref, dcb_ref, dba_ref, dbx_ref, dlam_ref,
             anext_sc, gcarry_sc, dext_sc, xext_sc, m_sc, g_sc, dwa_sc, dwx_sc, dsp_sc):
        del dpin_ref
        i = pl.program_id(0)
        tile = nt - 1 - i

        @pl.when(i == 0)
        def _():
            anext_sc[...] = jnp.zeros_like(anext_sc)
            gcarry_sc[...] = jnp.zeros_like(gcarry_sc)
            dext_sc[TM:TM + 8, :] = jnp.zeros((8, D), f32)
            dwa_sc[...] = jnp.zeros_like(dwa_sc)
            dwx_sc[...] = jnp.zeros_like(dwx_sc)
            dsp_sc[...] = jnp.zeros_like(dsp_sc)
            dcw_ref[...] = jnp.zeros_like(dcw_ref)
            dcb_ref[...] = jnp.zeros_like(dcb_ref)
            dba_ref[...] = jnp.zeros_like(dba_ref)
            dbx_ref[...] = jnp.zeros_like(dbx_ref)

        xr = xr_ref[...]
        hr = hr_ref[...]
        dz = dz_ref[...]
        gel, dgel = _gelu_and_grad(y_ref[...])
        dxy_ref[:, D:2 * D] = (dz * hr * dgel).astype(bf16)
        ra, ii, a, sq = gates_ref[0], gates_ref[1], gates_ref[2], gates_ref[3]
        sp = _softplus(-lam_ref[...])

        row = _row_ids((TM, D))
        _to_lane_tiles(m_sc, jnp.where(row == TM - 1, anext_sc[...], pltpu.roll(a, TM - 1, 0)))
        anext_sc[...] = a[0:1, :]
        _to_lane_tiles(g_sc, dz * gel)
        gcarry_sc[...] = _chain_scan(m_sc, g_sc, gcarry_sc[...], reverse=True)
        gg = _from_lane_tiles(g_sc)

        hlast = jnp.where(tile > 0, hrp_ref[7:8, :], 0.0)
        hprev = jnp.where(row == 0, hlast, pltpu.roll(hr, 1, 0))
        d_a = gg * hprev
        dsq = gg * ii * xr
        dii = gg * sq * xr
        dxr = gg * sq * ii
        dlog = d_a * a - dsq * (a * a / sq)
        dsp_sc[...] += jnp.sum(dlog * (-8.0 * ra), axis=0, keepdims=True)
        dpa = dlog * (-8.0 * sp) * ra * (1.0 - ra)
        dpx = dii * ii * (1.0 - ii)
        dba_ref[...] += jnp.sum(dpa, axis=0, keepdims=True)
        dbx_ref[...] += jnp.sum(dpx, axis=0, keepdims=True)
        dpab = dpa.astype(bf16)
        dpxb = dpx.astype(bf16)
        xrb = xr.astype(bf16)
        back = []
        for hh in range(NHEAD):
            cols = slice(hh * HD, (hh + 1) * HD)
            back.append(_nt_dot(dpab[:, cols], wa_ref[hh]) + _nt_dot(dpxb[:, cols], wx_ref[hh]))
            dwa_sc[hh] += _tn_dot(xrb[:, cols], dpab[:, cols])
            dwx_sc[hh] += _tn_dot(xrb[:, cols], dpxb[:, cols])
        dxr = dxr + jnp.concatenate(back, axis=1)

        dext_sc[0:TM, :] = dxr
        de = dext_sc[...]
        dx = cw_ref[KC4 - 1:KC4, :] * dxr
        for k in range(KC4 - 1):
            dx = dx + cw_ref[k:k + 1, :] * pltpu.roll(de, TM + 8 - (KC4 - 1 - k), 0)[0:TM]
        dext_sc[TM:TM + 8, :] = dxr[0:8]
        dxy_ref[:, 0:D] = dx.astype(bf16)

        x = x_ref[...]
        xext_sc[0:8, :] = jnp.where(tile > 0, xp_ref[...], 0.0)
        xext_sc[8:8 + TM, :] = x
        xe = xext_sc[...]
        dcw_ref[KC4 - 1:KC4, :] += jnp.sum(dxr * x, axis=0, keepdims=True)
        for k in range(KC4 - 1):
            xs = pltpu.roll(xe, KC4 - 1 - k, 0)[8:8 + TM]
            dcw_ref[k:k + 1, :] += jnp.sum(dxr * xs, axis=0, keepdims=True)
        dcb_ref[...] += jnp.sum(dxr, axis=0, keepdims=True)

        @pl.when(i == nt - 1)
        def _():
            for hh in range(NHEAD):
                for qc in range(NCHIP):
                    dwa_ref[qc, hh] = dwa_sc[hh, qc * hq:(qc + 1) * hq, :].astype(bf16)
                    dwx_ref[qc, hh] = dwx_sc[hh, qc * hq:(qc + 1) * hq, :].astype(bf16)
            dlam_ref[...] = -dsp_sc[...] * _sigmoid(-lam_ref[...])

    rev = lambda i: (nt - 1 - i, 0)
    vec = pl.BlockSpec((1, D), lambda i: (0, 0))
    prev8 = lambda i: jnp.maximum((nt - 1 - i) * ng - 1, 0)
    wblk = pl.BlockSpec((NHEAD, HD, HD), lambda i: (0, 0, 0))
    gblk = pl.BlockSpec((NCHIP, NHEAD, hq, HD), lambda i: (0, 0, 0, 0))
    return pl.pallas_call(
        body, name="rnn_bwd", grid=(nt,),
        in_specs=[
            pl.BlockSpec((TM, D), rev),
            pl.BlockSpec((TM, D), rev),
            pl.BlockSpec((TM, D), rev),
            pl.BlockSpec((8, D), lambda i: (prev8(i), 0)),
            pl.BlockSpec((TM, D), lambda i: (nt - 1 - i, 0)),
            pl.BlockSpec((8, D), lambda i: (prev8(i), 0)),
            pl.BlockSpec((TM, D), lambda i: (nt - 1 - i, 1)),
            pl.BlockSpec(memory_space=pl.ANY),
            pl.BlockSpec((KC4, D), lambda i: (0, 0)),
            wblk, pl.BlockSpec((4, TM, D), lambda i: (0, nt - 1 - i, 0)), wblk, vec,
        ],
        out_specs=[
            pl.BlockSpec((TM, 2 * D), lambda i: (nt - 1 - i, 0)),
            gblk, gblk,
            pl.BlockSpec((KC4, D), lambda i: (0, 0)),
            vec, vec, vec, vec,
        ],
        out_shape=[jax.ShapeDtypeStruct((t, NIN), bf16),
                   jax.ShapeDtypeStruct((NCHIP, NHEAD, hq, HD), bf16),
                   jax.ShapeDtypeStruct((NCHIP, NHEAD, hq, HD), bf16),
                   jax.ShapeDtypeStruct((KC4, D), f32),
                   jax.ShapeDtypeStruct((1, D), f32), jax.ShapeDtypeStruct((1, D), f32),
                   jax.ShapeDtypeStruct((1, D), f32), jax.ShapeDtypeStruct((1, D), f32)],
        scratch_shapes=[pltpu.VMEM((1, D), f32), pltpu.VMEM((1, D), f32),
                        pltpu.VMEM((TM + 8, D), f32), pltpu.VMEM((TM + 8, D), f32),
                        pltpu.VMEM((LT, TM, 128), f32), pltpu.VMEM((LT, TM, 128), f32),
                        pltpu.VMEM((NHEAD, HD, HD), f32), pltpu.VMEM((NHEAD, HD, HD), f32),
                        pltpu.VMEM((1, D), f32)],
        input_output_aliases={7: 0},
        compiler_params=_cp(1),
    )(dz, xr, hr, hr, proj, proj, proj, dproj, cw, wa, gates, wx, lam)


def _inproj_bwd(dproj, dh, h, g, win, after=None):
    t = h.shape[0]
    tn = NIN // NCHIP

    def body(dp_ref, dh_ref, h_ref, g_ref, w_ref, dhi_ref, dg_ref, db_ref):
        @pl.when(pl.program_id(0) == 0)
        def _():
            dg_ref[...] = jnp.zeros_like(dg_ref)
            db_ref[...] = jnp.zeros_like(db_ref)

        dn = None
        for s in range(NCHIP):
            dp = dp_ref[:, s * tn:(s + 1) * tn]
            part = _nt_dot(dp, w_ref[s])
            dn = part if dn is None else dn + part
            db_ref[s] += jnp.sum(dp.astype(f32), axis=0, keepdims=True)
        dhin, dg = _rms_bwd(dn, h_ref[...], g_ref[...])
        dhi_ref[...] = dh_ref[...] + dhin
        dg_ref[...] += dg

    rowd = pl.BlockSpec((TM, D), lambda i: (i, 0))
    vec = pl.BlockSpec((1, D), lambda i: (0, 0))
    body, in_specs, args = _ordered(
        body,
        [pl.BlockSpec((TM, NIN), lambda i: (i, 0)), rowd, rowd, vec, RESIDENT],
        (dproj, dh, h, g, win), after)
    return pl.pallas_call(
        body, name="inproj_bwd", grid=(t // TM,),
        in_specs=in_specs,
        out_specs=[rowd, vec, pl.BlockSpec((NCHIP, 1, tn), lambda i: (0, 0, 0))],
        out_shape=[jax.ShapeDtypeStruct((t, D), f32), jax.ShapeDtypeStruct((1, D), f32),
                   jax.ShapeDtypeStruct((NCHIP, 1, tn), f32)],
        compiler_params=_cp(1),
    )(*args)


def _ffn_gu_grad(n, dgate, dup, tag, after=None):
    half = _tn_matmul(n, dgate, D, FS, (NCHIP, D, FS), (None, D, FS), lambda k, nn, m: (nn, 0, 0),
                      tag + "_dwg", after=after)
    return _tn_matmul(n, dup, D, FS, (NCHIP, D, FS), (None, D, FS), lambda k, nn, m: (2 + nn, 0, 0),
                      tag + "_dwu", base=half)


def _ffn_down_grad(a, df, tag, after=None):
    return _tn_matmul(a, df, FS, D, (F, D), (FS, D), lambda k, nn, m: (k, 0), tag + "_dwd", after=after)


def _square_grad(a, b, name):
    return _tn_matmul(a, b, D, D, (D, D), (D, D), lambda k, nn, m: (0, 0), name)


ANY = pl.BlockSpec(memory_space=pl.ANY)


def _place():
    x, y, c = lax.axis_index("x"), lax.axis_index("y"), lax.axis_index("c")
    chips = [(1 - x, y), (x, 1 - y), (1 - x, 1 - y)]
    return x, y, c, chips


def _chip_id(chip):
    return 2 * chip[0] + chip[1]


def _cast_into_slot(w2d, qc, dtype, name, after=None):
    r, cc = w2d.shape
    hr = r // 2

    def body(qc_ref, *refs):
        del qc_ref
        w_ref, o_ref = refs[-2:]
        o_ref[...] = w_ref[...].astype(dtype)

    in_specs, args = [pl.BlockSpec((hr, cc), lambda h, qc_ref: (h, 0))], (w2d,)
    if after is not None:
        in_specs, args = [ANY_SPEC] + in_specs, (after,) + args
    return pl.pallas_call(
        body, name=name,
        grid_spec=pltpu.PrefetchScalarGridSpec(
            num_scalar_prefetch=1, grid=(2,),
            in_specs=in_specs,
            out_specs=pl.BlockSpec((None, None, hr, cc), lambda h, qc_ref: (qc_ref[0], h, 0, 0))),
        out_shape=jax.ShapeDtypeStruct((NCHIP, 2, hr, cc), dtype),
        compiler_params=_cp(1),
    )(qc, *args)


def _place_pack(pack, qc):
    def body(qc_ref, p_ref, o_ref):
        del qc_ref
        o_ref[...] = p_ref[...]

    return pl.pallas_call(
        body, name="place_pack",
        grid_spec=pltpu.PrefetchScalarGridSpec(
            num_scalar_prefetch=1, grid=(1,),
            in_specs=[pl.BlockSpec(pack.shape, lambda i, qc_ref: (0, 0))],
            out_specs=pl.BlockSpec((None,) + pack.shape, lambda i, qc_ref: (2 * qc_ref[0] + qc_ref[1], 0, 0))),
        out_shape=jax.ShapeDtypeStruct((8,) + pack.shape, pack.dtype),
        compiler_params=_cp(1),
    )(qc, pack)


def _pair_add(parts, gots, qc, name):
    n = len(parts)

    def body(qc_ref, *refs):
        s = pl.program_id(0)
        for a in range(n):
            val = (refs[a][...].astype(f32) + refs[n + a][...].astype(f32)).astype(bf16)
            refs[2 * n + a][...] = val

            @pl.when(s == qc_ref[0])
            def _(val=val, land_ref=refs[3 * n + a]):
                land_ref[...] = val

    shapes = [p.shape[2:] for p in parts]
    mine = [pl.BlockSpec((None, None) + sh, lambda s, qc_ref: (s, qc_ref[1], 0, 0)) for sh in shapes]
    block = [pl.BlockSpec((None,) + sh, lambda s, qc_ref: (s, 0, 0)) for sh in shapes]
    own = [pl.BlockSpec((None,) + sh, lambda s, qc_ref: (qc_ref[0], 0, 0)) for sh in shapes]
    outs = pl.pallas_call(
        body, name=name,
        grid_spec=pltpu.PrefetchScalarGridSpec(
            num_scalar_prefetch=1, grid=(NCHIP,), in_specs=mine + block, out_specs=block + own),
        out_shape=[jax.ShapeDtypeStruct((NCHIP,) + sh, bf16) for sh in shapes] * 2,
        compiler_params=_cp(1),
    )(qc, *parts, *gots)
    return list(outs[:n]), list(outs[n:])


def _sum_chips(gots, name):
    n = len(gots)

    def body(*refs):
        for a in range(n):
            acc = refs[a][0].astype(f32)
            for s in range(1, NCHIP):
                acc = acc + refs[a][s].astype(f32)
            refs[n + a][...] = acc

    return list(pl.pallas_call(
        body, name=name, grid=(1,),
        in_specs=[pl.BlockSpec(g.shape, lambda i: (0, 0, 0)) for g in gots],
        out_specs=[pl.BlockSpec(g.shape[1:], lambda i: (0, 0)) for g in gots],
        out_shape=[jax.ShapeDtypeStruct(g.shape[1:], f32) for g in gots],
        compiler_params=_cp(1),
    )(*gots))


def _pair_share(halves, name, after=None):
    n = len(halves)
    extra = () if after is None else (after,)

    def body(*refs):
        refs = refs[len(extra):]
        ins, outs = refs[:n], refs[n:2 * n]
        send_sems, recv_sems = refs[2 * n:]
        x, y, c, _ = _place()
        copies = []
        for a in range(n):
            cp = pltpu.make_async_remote_copy(
                src_ref=ins[a], dst_ref=outs[a], send_sem=send_sems.at[a], recv_sem=recv_sems.at[a],
                device_id=(x, y, 1 - c), device_id_type=MESH)
            cp.start()
            copies.append(cp)
        for cp in copies:
            cp.wait()

    return pl.pallas_call(
        body, name=name,
        in_specs=[ANY] * (len(extra) + n), out_specs=[ANY] * n,
        out_shape=[jax.ShapeDtypeStruct(s.shape, s.dtype) for s in halves],
        scratch_shapes=[pltpu.SemaphoreType.DMA((n,)), pltpu.SemaphoreType.DMA((n,))],
    )(*extra, *halves)


def _all_copy(buf_ref, send_ref, recv_ref, k, x, y, c, landing):
    px, py, pc = (1 - x if k & 4 else x, 1 - y if k & 2 else y, 1 - c if k & 1 else c)
    me = 4 * x + 2 * y + c
    there = 4 * px + 2 * py + pc
    return pltpu.make_async_remote_copy(
        src_ref=buf_ref.at[me], dst_ref=buf_ref.at[there if landing else me],
        send_sem=send_ref.at[k - 1], recv_sem=recv_ref.at[k - 1],
        device_id=(px, py, pc), device_id_type=MESH)


def _gather_all_start(buf, name):
    def body(in_ref, send, recv, thru, token):
        del thru
        x, y, c, _ = _place()
        for k in range(1, 8):
            _all_copy(in_ref, send, recv, k, x, y, c, False).start()
        token[...] = jnp.zeros_like(token)

    return pl.pallas_call(
        body, name=name,
        in_specs=[HBM],
        out_specs=[SEM, SEM, HBM, pl.BlockSpec(memory_space=pltpu.VMEM)],
        out_shape=[pltpu.SemaphoreType.DMA((7,)), pltpu.SemaphoreType.DMA((7,)),
                   pltpu.HBM(buf.shape, buf.dtype), jax.ShapeDtypeStruct((8, 128), f32)],
        input_output_aliases={0: 2},
        compiler_params=pltpu.CompilerParams(has_side_effects=EFFECT),
    )(_in_hbm(buf))


def _gather_all_wait(send, recv, buf, after, name):
    def body(in_ref, send_r, recv_r, after_ref, out_ref):
        del after_ref, out_ref
        x, y, c, _ = _place()
        for k in range(1, 8):
            cp = _all_copy(in_ref, send_r, recv_r, k, x, y, c, True)
            cp.wait_send()
            cp.wait_recv()

    return pl.pallas_call(
        body, name=name,
        in_specs=[HBM, SEM, SEM, ANY],
        out_specs=HBM,
        out_shape=pltpu.HBM(buf.shape, buf.dtype),
        input_output_aliases={0: 0},
        compiler_params=pltpu.CompilerParams(has_side_effects=EFFECT),
    )(buf, send, recv, after)


HBM = pl.BlockSpec(memory_space=pltpu.HBM)
SEM = pl.BlockSpec(memory_space=pltpu.SEMAPHORE)
EFFECT = pltpu.SideEffectType.DATAFLOW_SIDE_EFFECTING
N_PEER = 3


def _in_hbm(a):
    return pltpu.with_memory_space_constraint(a, pltpu.HBM)


def _gather_copy(buf_ref, send_ref, recv_ref, j, chip, q, c, landing_chip):
    return pltpu.make_async_remote_copy(
        src_ref=buf_ref.at[q, c], dst_ref=buf_ref.at[landing_chip, c],
        send_sem=send_ref.at[j], recv_sem=recv_ref.at[j],
        device_id=(chip[0], chip[1], c), device_id_type=MESH)


def _gather_start(bufs, name):
    n = len(bufs)

    def body(*refs):
        ins = refs[:n]
        send, recv = refs[n:2 * n], refs[2 * n:3 * n]
        token = refs[4 * n]
        x, y, c, chips = _place()
        q = 2 * x + y
        for a in range(n):
            for j, chip in enumerate(chips):
                _gather_copy(ins[a], send[a], recv[a], j, chip, q, c, q).start()
        token[...] = jnp.zeros_like(token)

    sems = [pltpu.SemaphoreType.DMA((N_PEER,))] * (2 * n)
    outs = pl.pallas_call(
        body, name=name,
        in_specs=[HBM] * n,
        out_specs=[SEM] * (2 * n) + [HBM] * n + [pl.BlockSpec(memory_space=pltpu.VMEM)],
        out_shape=sems + [pltpu.HBM(b.shape, b.dtype) for b in bufs] + [jax.ShapeDtypeStruct((8, 128), f32)],
        input_output_aliases={a: 2 * n + a for a in range(n)},
        compiler_params=pltpu.CompilerParams(has_side_effects=EFFECT),
    )(*[_in_hbm(b) for b in bufs])
    return list(outs[:n]), list(outs[n:2 * n]), list(outs[2 * n:3 * n]), outs[3 * n]


def _gather_wait(send, recv, bufs, after, name):
    n = len(bufs)

    def body(*refs):
        ins = refs[:n]
        send_r, recv_r = refs[n:2 * n], refs[2 * n:3 * n]
        x, y, c, chips = _place()
        q = 2 * x + y
        for a in range(n):
            for j, chip in enumerate(chips):
                cp = _gather_copy(ins[a], send_r[a], recv_r[a], j, chip, q, c, _chip_id(chip))
                cp.wait_send()
                cp.wait_recv()

    afters = after if isinstance(after, (tuple, list)) else (after,)
    outs = pl.pallas_call(
        body, name=name,
        in_specs=[HBM] * n + [SEM] * (2 * n) + [ANY] * len(afters),
        out_specs=[HBM] * n,
        out_shape=[pltpu.HBM(b.shape, b.dtype) for b in bufs],
        input_output_aliases={a: a for a in range(n)},
        compiler_params=pltpu.CompilerParams(has_side_effects=EFFECT),
    )(*bufs, *send, *recv, *afters)
    return list(outs)


def _forward_halves(bufs, name):
    n = len(bufs)

    def body(*refs):
        outs = refs[n:2 * n]
        send_sems, recv_sems = refs[2 * n:]
        x, y, c, chips = _place()
        sibling = (x, y, 1 - c)

        def remote(a, j, blk):
            return pltpu.make_async_remote_copy(src_ref=blk, dst_ref=blk, send_sem=send_sems.at[a, j],
                                                recv_sem=recv_sems.at[a, j], device_id=sibling,
                                                device_id_type=MESH)

        sent = []
        for a in range(n):
            for j, chip in enumerate(chips):
                cp = remote(a, j, outs[a].at[_chip_id(chip), c])
                cp.start()
                sent.append(cp)
        for a in range(n):
            for j, chip in enumerate(chips):
                remote(a, j, outs[a].at[_chip_id(chip), 1 - c]).wait_recv()
        for cp in sent:
            cp.wait_send()

    return pl.pallas_call(
        body, name=name,
        in_specs=[ANY] * n, out_specs=[ANY] * n,
        out_shape=[jax.ShapeDtypeStruct(s.shape, s.dtype) for s in bufs],
        scratch_shapes=[pltpu.SemaphoreType.DMA((n, N_PEER)), pltpu.SemaphoreType.DMA((n, N_PEER))],
        input_output_aliases={a: a for a in range(n)},
    )(*bufs)


def _reduce_copy(sum_ref, land_ref, send_ref, recv_ref, j, chip, q, c, landing_chip):
    return pltpu.make_async_remote_copy(
        src_ref=sum_ref.at[_chip_id(chip)], dst_ref=land_ref.at[landing_chip],
        send_sem=send_ref.at[j], recv_sem=recv_ref.at[j],
        device_id=(chip[0], chip[1], c), device_id_type=MESH)


def _reduce_start(sums, lands, name):
    n = len(sums)

    def body(*refs):
        s_in, l_in = refs[:n], refs[n:2 * n]
        send, recv = refs[2 * n:3 * n], refs[3 * n:4 * n]
        token = refs[6 * n]
        x, y, c, chips = _place()
        q = 2 * x + y
        for a in range(n):
            for j, chip in enumerate(chips):
                _reduce_copy(s_in[a], l_in[a], send[a], recv[a], j, chip, q, c, q).start()
        token[...] = jnp.zeros_like(token)

    sems = [pltpu.SemaphoreType.DMA((N_PEER,))] * (2 * n)
    outs = pl.pallas_call(
        body, name=name,
        in_specs=[HBM] * (2 * n),
        out_specs=[SEM] * (2 * n) + [HBM] * (2 * n) + [pl.BlockSpec(memory_space=pltpu.VMEM)],
        out_shape=sems + [pltpu.HBM(b.shape, b.dtype) for b in list(sums) + list(lands)]
        + [jax.ShapeDtypeStruct((8, 128), f32)],
        input_output_aliases={a: 2 * n + a for a in range(2 * n)},
        compiler_params=pltpu.CompilerParams(has_side_effects=EFFECT),
    )(*[_in_hbm(b) for b in list(sums) + list(lands)])
    return (list(outs[:n]), list(outs[n:2 * n]), list(outs[2 * n:3 * n]), list(outs[3 * n:4 * n]),
            outs[4 * n])


def _reduce_wait(send, recv, sums, lands, after, name):
    n = len(sums)

    def body(*refs):
        s_in, l_in = refs[:n], refs[n:2 * n]
        send_r, recv_r = refs[2 * n:3 * n], refs[3 * n:4 * n]
        x, y, c, chips = _place()
        q = 2 * x + y
        for a in range(n):
            for j, chip in enumerate(chips):
                cp = _reduce_copy(s_in[a], l_in[a], send_r[a], recv_r[a], j, chip, q, c, _chip_id(chip))
                cp.wait_send()
                cp.wait_recv()

    afters = after if isinstance(after, (tuple, list)) else (after,)
    outs = pl.pallas_call(
        body, name=name,
        in_specs=[HBM] * (2 * n) + [SEM] * (2 * n) + [ANY] * len(afters),
        out_specs=[HBM] * (2 * n),
        out_shape=[pltpu.HBM(b.shape, b.dtype) for b in list(sums) + list(lands)],
        input_output_aliases={a: a for a in range(2 * n)},
        compiler_params=pltpu.CompilerParams(has_side_effects=EFFECT),
    )(*sums, *lands, *send, *recv, *afters)
    return list(outs[n:])


def _sibling_copy(part_ref, land_ref, send_ref, recv_ref, x, y, c):
    return pltpu.make_async_remote_copy(
        src_ref=part_ref.at[:, 1 - c], dst_ref=land_ref, send_sem=send_ref.at[0], recv_sem=recv_ref.at[0],
        device_id=(x, y, 1 - c), device_id_type=MESH)


def _pair_exchange_start(parts, name):
    n = len(parts)
    lands = [lax.empty((NCHIP,) + p.shape[2:], p.dtype) for p in parts]

    def body(*refs):
        p_in, l_in = refs[:n], refs[n:2 * n]
        send, recv = refs[2 * n:3 * n], refs[3 * n:4 * n]
        token = refs[6 * n]
        x, y, c, _ = _place()
        for a in range(n):
            _sibling_copy(p_in[a], l_in[a], send[a], recv[a], x, y, c).start()
        token[...] = jnp.zeros_like(token)

    sems = [pltpu.SemaphoreType.DMA((1,))] * (2 * n)
    outs = pl.pallas_call(
        body, name=name,
        in_specs=[HBM] * (2 * n),
        out_specs=[SEM] * (2 * n) + [HBM] * (2 * n) + [pl.BlockSpec(memory_space=pltpu.VMEM)],
        out_shape=sems + [pltpu.HBM(b.shape, b.dtype) for b in list(parts) + lands]
        + [jax.ShapeDtypeStruct((8, 128), f32)],
        input_output_aliases={a: 2 * n + a for a in range(2 * n)},
        compiler_params=pltpu.CompilerParams(has_side_effects=EFFECT),
    )(*[_in_hbm(b) for b in list(parts) + lands])
    return (list(outs[:n]), list(outs[n:2 * n]), list(outs[2 * n:3 * n]), list(outs[3 * n:4 * n]),
            outs[4 * n])


def _pair_exchange_wait(send, recv, parts, lands, after, name):
    n = len(parts)

    def body(*refs):
        p_in, l_in = refs[:n], refs[n:2 * n]
        send_r, recv_r = refs[2 * n:3 * n], refs[3 * n:4 * n]
        x, y, c, _ = _place()
        for a in range(n):
            cp = _sibling_copy(p_in[a], l_in[a], send_r[a], recv_r[a], x, y, c)
            cp.wait_send()
            cp.wait_recv()

    outs = pl.pallas_call(
        body, name=name,
        in_specs=[HBM] * (2 * n) + [SEM] * (2 * n) + [ANY],
        out_specs=[HBM] * (2 * n),
        out_shape=[pltpu.HBM(b.shape, b.dtype) for b in list(parts) + list(lands)],
        input_output_aliases={a: a for a in range(2 * n)},
        compiler_params=pltpu.CompilerParams(has_side_effects=EFFECT),
    )(*parts, *lands, *send, *recv, after)
    return list(outs[:n]), list(outs[n:])


def _forward_copy(buf_ref, send_ref, recv_ref, j, chip, x, y, c, landing):
    return pltpu.make_async_remote_copy(
        src_ref=buf_ref.at[_chip_id(chip), c], dst_ref=buf_ref.at[_chip_id(chip), 1 - c if landing else c],
        send_sem=send_ref.at[j], recv_sem=recv_ref.at[j], device_id=(x, y, 1 - c), device_id_type=MESH)


def _forward_start(bufs, name):
    n = len(bufs)

    def body(*refs):
        ins = refs[:n]
        send, recv = refs[n:2 * n], refs[2 * n:3 * n]
        token = refs[4 * n]
        x, y, c, chips = _place()
        for a in range(n):
            for j, chip in enumerate(chips):
                _forward_copy(ins[a], send[a], recv[a], j, chip, x, y, c, False).start()
        token[...] = jnp.zeros_like(token)

    sems = [pltpu.SemaphoreType.DMA((N_PEER,))] * (2 * n)
    outs = pl.pallas_call(
        body, name=name,
        in_specs=[HBM] * n,
        out_specs=[SEM] * (2 * n) + [HBM] * n + [pl.BlockSpec(memory_space=pltpu.VMEM)],
        out_shape=sems + [pltpu.HBM(b.shape, b.dtype) for b in bufs] + [jax.ShapeDtypeStruct((8, 128), f32)],
        input_output_aliases={a: 2 * n + a for a in range(n)},
        compiler_params=pltpu.CompilerParams(has_side_effects=EFFECT),
    )(*[_in_hbm(b) for b in bufs])
    return list(outs[:n]), list(outs[n:2 * n]), list(outs[2 * n:3 * n]), outs[3 * n]


def _forward_wait(send, recv, bufs, after, name):
    n = len(bufs)

    def body(*refs):
        ins = refs[:n]
        send_r, recv_r = refs[n:2 * n], refs[2 * n:3 * n]
        x, y, c, chips = _place()
        for a in range(n):
            for j, chip in enumerate(chips):
                cp = _forward_copy(ins[a], send_r[a], recv_r[a], j, chip, x, y, c, True)
                cp.wait_send()
                cp.wait_recv()

    outs = pl.pallas_call(
        body, name=name,
        in_specs=[HBM] * n + [SEM] * (2 * n) + [ANY],
        out_specs=[HBM] * n,
        out_shape=[pltpu.HBM(b.shape, b.dtype) for b in bufs],
        input_output_aliases={a: a for a in range(n)},
        compiler_params=pltpu.CompilerParams(has_side_effects=EFFECT),
    )(*bufs, *send, *recv, after)
    return list(outs)


def _adamw_math(w, g, m, v):
    m = ADAM_B1 * m + (1.0 - ADAM_B1) * g
    v = ADAM_B2 * v + (1.0 - ADAM_B2) * (g * g)
    m_hat = m / (1.0 - ADAM_B1 ** ADAM_STEP)
    v_hat = v / (1.0 - ADAM_B2 ** ADAM_STEP)
    delta = -ADAM_LR * (m_hat / (jnp.sqrt(v_hat) + ADAM_EPS) + ADAM_WD * w)
    return delta, m, v


ADAMW_BLOCK_BYTES = 3 << 19


def _adamw(ws, mines, theirs, ms, vs, qc, name):
    n = len(ws)
    halves = [w.shape[0] // 2 for w in ws]
    nb = next(k for k in range(1, min(halves) + 1)
              if all(hr % k == 0 and (hr // k) % 8 == 0 and (hr // k) * w.shape[1] * 4 <= ADAMW_BLOCK_BYTES
                     for hr, w in zip(halves, ws)))

    def body(qc_ref, *refs):
        mine_here = pl.program_id(0) == qc_ref[1]
        for a in range(n):
            w_ref, a_ref, b_ref, m_ref, v_ref = (refs[k * n + a] for k in range(5))
            g_ref, d_ref, mo_ref, vo_ref = (refs[(5 + k) * n + a] for k in range(4))
            g = jnp.where(mine_here, a_ref[...], b_ref[...])
            g_ref[...] = g
            d_ref[...], mo_ref[...], vo_ref[...] = _adamw_math(w_ref[...], g, m_ref[...], v_ref[...])

    blocks = [(hr // nb, w.shape[1]) for hr, w in zip(halves, ws)]
    full = [pl.BlockSpec(b, lambda h, i, qc_ref: (h * nb + i, 0)) for b in blocks]
    half = [pl.BlockSpec(b, lambda h, i, qc_ref: (i, 0)) for b in blocks]
    outs = pl.pallas_call(
        body, name=name,
        grid_spec=pltpu.PrefetchScalarGridSpec(
            num_scalar_prefetch=1, grid=(2, nb),
            in_specs=full + half + half + full + full, out_specs=full * 4),
        out_shape=[jax.ShapeDtypeStruct(w.shape, f32) for w in ws] * 4,
        compiler_params=_cp(2),
    )(qc, *ws, *mines, *theirs, *ms, *vs)
    return [tuple(outs[k * n + a] for k in range(4)) for a in range(n)]


REPL = [("ffn1_norm", 1), ("mix_norm", 1), ("b_in", 6), ("rnn_conv_b", 1), ("rg_b_a", 1), ("rg_b_x", 1),
        ("rg_lambda", 1), ("conv_dw_b", 1), ("conv_ln_g", 1), ("conv_ln_b", 1), ("conv_b_proj", 1),
        ("ffn2_norm", 1), ("final_norm", 1)]
COLSH = [("meta_tokens", NMETA), ("rnn_conv_w", KC4), ("conv_dw_w", KC31)]
SMALL = REPL + COLSH
CS = D // NCHIP


def _pack_rows():
    starts, row = {}, 0
    for k, rows in REPL:
        starts[k] = row
        row += rows
    for k, rows in COLSH:
        row = -(-row // 8) * 8
        starts[k] = row
        row += rows
    return starts, -(-row // 8) * 8


PACK_START, LOSS_ROW = _pack_rows()
SMALL_ROWS = LOSS_ROW + 8


def _small_pack(g, loss_row):
    pieces, row = [], 0
    for k, rows in SMALL:
        if PACK_START[k] > row:
            pieces.append(jnp.zeros((PACK_START[k] - row, D), f32))
        pieces.append(g[k].reshape(rows, D))
        row = PACK_START[k] + rows
    pieces.append(jnp.zeros((LOSS_ROW - row, D), f32))
    pieces.append(loss_row)
    pieces.append(jnp.zeros((SMALL_ROWS - LOSS_ROW - 1, D), f32))
    return jnp.concatenate(pieces, axis=0)


def _adamw_small(packs, ws, ms, vs):
    ns = len(SMALL)

    def body(*refs):
        pack_ref = refs[0]
        w_refs, m_refs, v_refs = refs[1:1 + ns], refs[1 + ns:1 + 2 * ns], refs[1 + 2 * ns:1 + 3 * ns]
        outs = refs[1 + 3 * ns:1 + 7 * ns]
        g_refs, d_refs, mo_refs, vo_refs = outs[:ns], outs[ns:2 * ns], outs[2 * ns:3 * ns], outs[3 * ns:]
        loss_ref = refs[1 + 7 * ns]
        gsum_sc = refs[2 + 7 * ns]
        q = 2 * lax.axis_index("x") + lax.axis_index("y")
        acc = pack_ref[0]
        for dev in range(1, 8):
            acc = acc + pack_ref[dev]
        gsum_sc[...] = acc
        loss_ref[...] = gsum_sc[LOSS_ROW:LOSS_ROW + 1, :]
        for idx, (name, rows) in enumerate(SMALL):
            row = PACK_START[name]
            if idx < len(REPL):
                for k in range(rows):
                    cols = slice(k * D, (k + 1) * D)
                    g = gsum_sc[row + k:row + k + 1, :]
                    d, mm, vv = _adamw_math(w_refs[idx][:, cols], g, m_refs[idx][:, cols], v_refs[idx][:, cols])
                    g_refs[idx][:, cols] = g
                    d_refs[idx][:, cols] = d
                    mo_refs[idx][:, cols] = mm
                    vo_refs[idx][:, cols] = vv
            else:
                g = gsum_sc[row:row + rows, pl.ds(pl.multiple_of(q * CS, CS), CS)]
                d, mm, vv = _adamw_math(w_refs[idx][...], g, m_refs[idx][...], v_refs[idx][...])
                g_refs[idx][...] = g
                d_refs[idx][...] = d
                mo_refs[idx][...] = mm
                vo_refs[idx][...] = vv

    shapes = [jax.ShapeDtypeStruct(w.shape, f32) for w in ws]
    return pl.pallas_call(
        body, name="adamw_small",
        out_shape=shapes * 4 + [jax.ShapeDtypeStruct((1, D), f32)],
        scratch_shapes=[pltpu.VMEM((SMALL_ROWS, D), f32)],
        compiler_params=pltpu.CompilerParams(vmem_limit_bytes=VMEM_LIMIT),
    )(packs, *ws, *ms, *vs)


WEIGHTS = ['meta_tokens', 'ffn1_norm', 'ffn1_w_gu', 'ffn1_w_down', 'mix_norm', 'w_in', 'b_in', 'rnn_conv_w',
           'rnn_conv_b', 'rg_w_a', 'rg_b_a', 'rg_w_x', 'rg_b_x', 'rg_lambda', 'rnn_w_proj', 'conv_dw_w',
           'conv_dw_b', 'conv_ln_g', 'conv_ln_b', 'conv_w_proj', 'conv_b_proj', 'w_out', 'ffn2_norm',
           'ffn2_w_gu', 'ffn2_w_down', 'final_norm']


def _as2d(a):
    return a.reshape(-1, a.shape[-1])


def _step(x, loss_target, w, m, v):
    seq = x.shape[1]
    n_valid = NMETA + seq
    t = -(-n_valid // TM) * TM

    qc = jnp.stack([2 * lax.axis_index("x") + lax.axis_index("y"), lax.axis_index("c")]).astype(jnp.int32)
    p = {k: w[k].reshape(1, rows * D) for k, rows in REPL}

    first = ["ffn1_w_gu", "ffn1_w_down", "small"]
    later = [["w_in"], ["rg_w_a", "rg_w_x", "rnn_w_proj", "conv_w_proj", "w_out"], ["ffn2_w_gu", "ffn2_w_down"]]
    small_rows = sum(r for _, r in COLSH)
    small = jnp.concatenate([_as2d(w[k]) for k, _ in COLSH] + [jnp.zeros((64 - small_rows, CS), f32)], axis=0)

    def cast(k, token=None):
        src, dtype = (small, f32) if k == "small" else (_as2d(w[k]), bf16)
        return _cast_into_slot(src, qc, dtype, "cast_" + k, after=token)

    send1, recv1, bufs1, token1 = _gather_start([cast(k) for k in first], "gather_start_first")
    rest = [k for grp in later for k in grp]
    send2, recv2, bufs2, token2 = _gather_start([cast(k, token1) for k in rest], "gather_start_rest")

    def install(names, done):
        for k, b in zip(names, done):
            full = b.reshape(NCHIP, 2 * b.shape[2], b.shape[3])
            if k in ("ffn1_w_down", "ffn2_w_down"):
                full = full.reshape(F, D)
            elif k in ("rnn_w_proj", "conv_w_proj", "w_out"):
                full = full.reshape(D, D)
            elif k in ("rg_w_a", "rg_w_x"):
                full = full.reshape(NCHIP, NHEAD, HD // NCHIP, HD).transpose(1, 0, 2, 3).reshape(NHEAD, HD, HD)
            p[k] = full

    def finish(names, send, recv, bufs, after, tag):
        install(names, _forward_halves(_gather_wait(send, recv, bufs, after, "gather_wait_" + tag),
                                       "gather_forward_" + tag))

    def group(names):
        idx = [rest.index(k) for k in names]
        return names, [send2[i] for i in idx], [recv2[i] for i in idx], [bufs2[i] for i in idx]

    h0 = jnp.pad(x[0] + token1[0:1, 0:1], ((NMETA, t - n_valid), (0, 0)))
    tgt = jnp.pad(loss_target[0] + token2[0:1, 0:1], ((NMETA, t - n_valid), (0, 0)))
    finish(first, send1, recv1, bufs1, (token2, h0, tgt), "first")
    small_full = p.pop("small").transpose(1, 0, 2).reshape(64, D)
    row = 0
    for k, rows in COLSH:
        p[k] = small_full[row:row + rows]
        row += rows

    h0 = lax.dynamic_update_slice(h0, p["meta_tokens"], (0, 0))
    h1, gate1, up1, a1, n1 = _ffn_fwd(h0, p["ffn1_norm"], p["ffn1_w_gu"], p["ffn1_w_down"], "ffn1_fwd")
    finish(*group(later[0]), h1, "in")
    proj, n2 = _inproj_fwd(h1, p["mix_norm"], p["w_in"], p["b_in"])
    names_l = later[1] + later[2]
    _, send_l, recv_l, bufs_l = group(names_l)
    send_f, recv_f, bufs_f, token = _forward_start(
        _gather_wait(send_l, recv_l, bufs_l, proj, "gather_wait_late"), "gather_forward_start")
    vc, s = _conv_fwd(proj, p["conv_dw_w"], p["conv_dw_b"], p["conv_ln_g"], p["conv_ln_b"], after=token)
    install(names_l, _forward_wait(send_f, recv_f, bufs_f, vc, "gather_forward_wait"))
    xr, hr, z, gates = _rnn_fwd(proj, p["rnn_conv_w"], p["rnn_conv_b"], p["rg_w_a"], p["rg_b_a"],
                         p["rg_w_x"], p["rg_b_x"], p["rg_lambda"])
    h2 = _merge_fwd(h1, z, s, proj, p["rnn_w_proj"], p["conv_w_proj"], p["conv_b_proj"], p["w_out"])
    dh3, loss_blk, d_final, gate2, up2, a2, n3 = _ffn_fwd(
        h2, p["ffn2_norm"], p["ffn2_w_gu"], p["ffn2_w_down"], "ffn2_fwd",
        loss_head=(p["final_norm"], tgt, n_valid))

    g = {"final_norm": d_final}
    pending = []

    def exchange_start(names, tag):
        parts = []
        for k in names:
            rows = g[k].size // (NCHIP * g[k].shape[-1])
            parts.append(g[k].reshape((NCHIP, 2, rows // 2, g[k].shape[-1])))
        send, recv, parts, lands, token = _pair_exchange_start(parts, "pair_exchange_start_" + tag)
        return (names, tag, send, recv, parts, lands), token

    def reduce_start(state, after):
        names, tag, send, recv, parts, lands = state
        parts, from_sibling = _pair_exchange_wait(send, recv, parts, lands, after, "pair_exchange_wait_" + tag)
        sums, lands = _pair_add(parts, from_sibling, qc, "pair_add_" + tag)
        send, recv, sums, lands, token = _reduce_start(sums, lands, "reduce_start_" + tag)
        pending.append((names, tag, send, recv, sums, lands))
        return token

    dh2, dgate2, dup2, df2, g["ffn2_norm"] = _ffn_bwd(
        dh3, h2, p["ffn2_norm"], gate2, up2, p["ffn2_w_gu"], p["ffn2_w_down"], "ffn2_bwd")
    g["ffn2_w_gu"] = _ffn_gu_grad(n3, dgate2, dup2, "ffn2")
    g["ffn2_w_down"] = _ffn_down_grad(a2, df2, "ffn2")
    state, token = exchange_start(["ffn2_w_gu", "ffn2_w_down"], "ffn2")

    dz, ds, dproj, dh2b, merged, dya, dyb, g["conv_b_proj"] = _merge_bwd(
        dh2, z, s, proj, p["rnn_w_proj"], p["conv_w_proj"], p["conv_b_proj"], p["w_out"], after=token)
    token = reduce_start(state, dz)
    dproj, g["conv_dw_w"], g["conv_dw_b"], g["conv_ln_g"], g["conv_ln_b"] = _conv_bwd(
        ds, vc, proj, dproj, p["conv_dw_w"], p["conv_ln_g"], p["conv_ln_b"], after=token)
    g["w_out"] = _square_grad(merged, dh2b, "dw_out")
    g["rnn_w_proj"] = _square_grad(z, dya, "dw_rnn_proj")
    g["conv_w_proj"] = _square_grad(s, dyb, "dw_conv_proj")
    (dproj, g["rg_w_a"], g["rg_w_x"], g["rnn_conv_w"], g["rnn_conv_b"], g["rg_b_a"], g["rg_b_x"],
     g["rg_lambda"]) = _rnn_bwd(dz, xr, hr, gates, proj, dproj, p["rnn_conv_w"], p["rg_w_a"],
                                p["rg_w_x"], p["rg_lambda"])

    dh1, g["mix_norm"], db_in = _inproj_bwd(dproj, dh2, h1, p["mix_norm"], p["w_in"])
    g["b_in"] = db_in.reshape(1, NIN)
    g["w_in"] = _tn_matmul(n2, dproj, D, NIN // NCHIP, (NCHIP, D, NIN // NCHIP),
                           (None, D, NIN // NCHIP), lambda k, nn, mm: (nn, 0, 0), "dw_in")
    state, token = exchange_start(["w_out", "rnn_w_proj", "conv_w_proj", "rg_w_a", "rg_w_x", "w_in"], "mix")

    dh0, dgate1, dup1, df1, g["ffn1_norm"], grad_x = _ffn_bwd(
        dh1, h0, p["ffn1_norm"], gate1, up1, p["ffn1_w_gu"], p["ffn1_w_down"], "ffn1_bwd", after=token,
        grad_rows=(NMETA, n_valid))
    g["meta_tokens"] = dh0[0:NMETA]
    grad_x = grad_x[None]
    token = reduce_start(state, dh0)

    send_s, recv_s, pack_buf, token_s = _gather_all_start(
        _place_pack(_small_pack(g, loss_blk.reshape(1, D)), qc), "gather_all_start")
    g["ffn1_w_down"] = _ffn_down_grad(a1, df1, "ffn1", after=(token, token_s))
    state, token = exchange_start(["ffn1_w_down"], "ffn1_down")
    gate_half = _tn_matmul(n1, dgate1, D, FS, (NCHIP, D, FS), (None, D, FS), lambda k, nn, mm: (nn, 0, 0),
                           "ffn1_dwg", after=token)
    token = reduce_start(state, gate_half)
    g["ffn1_w_gu"] = _tn_matmul(n1, dup1, D, FS, (NCHIP, D, FS), (None, D, FS), lambda k, nn, mm: (2 + nn, 0, 0),
                                "ffn1_dwu", base=gate_half, after=token)
    state_gu, token = exchange_start(["ffn1_w_gu"], "ffn1_gu")
    packs = _gather_all_wait(send_s, recv_s, pack_buf, token, "gather_all_wait")

    grads, deltas, new_m, new_v = {}, {}, {}, {}

    def landed_sums(items, after):
        names, mine = [], []
        for grp_names, grp_tag, send, recv, sums, lands in items:
            landed = _reduce_wait(send, recv, sums, lands, after, "reduce_wait_" + grp_tag)
            mine += _sum_chips(landed, "sum_chips_" + grp_tag)
            names += grp_names
            after = mine[-1]
        return names, mine

    def share_and_update(names, mine, tag, after=None):
        theirs = _pair_share(mine, "pair_share_" + tag, after=after)
        got = dict(zip(names, zip(mine, theirs)))
        square = [k for k in names if got[k][0].shape[0] * 2 <= HD]
        for batch in [[k] for k in names if k not in square] + ([square] if square else []):
            outs = _adamw([_as2d(w[k]) for k in batch], [got[k][0] for k in batch], [got[k][1] for k in batch],
                          [_as2d(m[k]) for k in batch], [_as2d(v[k]) for k in batch], qc,
                          "adamw_" + (batch[0] if len(batch) == 1 else "mixer"))
            for k, out in zip(batch, outs):
                grads[k], deltas[k], new_m[k], new_v[k] = (a.reshape(w[k].shape) for a in out)
        return [new_v[k] for k in names]

    early_names, early_mine = landed_sums(pending[:2], packs)
    token = reduce_start(state_gu, early_mine[-1])
    after = share_and_update(early_names, early_mine, "early", after=token)
    share_and_update(*landed_sums(pending[2:], after), "late")
    names = [k for k, _ in SMALL]
    shape2 = {k: ((1, rows * D) if (k, rows) in REPL else (rows, CS)) for k, rows in SMALL}
    outs = _adamw_small(packs, *[[a[k].reshape(shape2[k]) for k in names] for a in (w, m, v)])
    ns = len(names)
    for i, k in enumerate(names):
        grads[k], deltas[k], new_m[k], new_v[k] = (outs[j * ns + i].reshape(w[k].shape) for j in range(4))

    loss = outs[4 * ns][0, 0]
    return (loss, grad_x, *[grads[k] for k in WEIGHTS], *[deltas[k] for k in WEIGHTS],
            *[new_m[k] for k in WEIGHTS], *[new_v[k] for k in WEIGHTS])


def kernel(x, meta_tokens, ffn1_norm, ffn1_w_gu, ffn1_w_down, mix_norm, w_in, b_in, rnn_conv_w, rnn_conv_b, rg_w_a, rg_b_a, rg_w_x, rg_b_x, rg_lambda, rnn_w_proj, conv_dw_w, conv_dw_b, conv_ln_g, conv_ln_b, conv_w_proj, conv_b_proj, w_out, ffn2_norm, ffn2_w_gu, ffn2_w_down, final_norm, loss_target, m_meta_tokens, m_ffn1_norm, m_ffn1_w_gu, m_ffn1_w_down, m_mix_norm, m_w_in, m_b_in, m_rnn_conv_w, m_rnn_conv_b, m_rg_w_a, m_rg_b_a, m_rg_w_x, m_rg_b_x, m_rg_lambda, m_rnn_w_proj, m_conv_dw_w, m_conv_dw_b, m_conv_ln_g, m_conv_ln_b, m_conv_w_proj, m_conv_b_proj, m_w_out, m_ffn2_norm, m_ffn2_w_gu, m_ffn2_w_down, m_final_norm, v_meta_tokens, v_ffn1_norm, v_ffn1_w_gu, v_ffn1_w_down, v_mix_norm, v_w_in, v_b_in, v_rnn_conv_w, v_rnn_conv_b, v_rg_w_a, v_rg_b_a, v_rg_w_x, v_rg_b_x, v_rg_lambda, v_rnn_w_proj, v_conv_dw_w, v_conv_dw_b, v_conv_ln_g, v_conv_ln_b, v_conv_w_proj, v_conv_b_proj, v_w_out, v_ffn2_norm, v_ffn2_w_gu, v_ffn2_w_down, v_final_norm):
    args = locals()
    w = {k: args[k] for k in WEIGHTS}
    m = {k: args["m_" + k] for k in WEIGHTS}
    v = {k: args["v_" + k] for k in WEIGHTS}
    return _step(x, loss_target, w, m, v)
```

```python
import jax
import jax.numpy as jnp
from jax import lax
from jax.experimental import pallas as pl
from jax.experimental.pallas import tpu as pltpu

f32 = jnp.float32
bf16 = jnp.bfloat16

D = 1024
F = 2816
FS = F // 2
NIN = 6 * D
NMETA = 16
NHEAD = 4
HD = D // NHEAD
KC4 = 4
KC31 = 31
HALO = 32
EPS = 1e-6
TM = 416
NCHIP = 4
MESH = pl.DeviceIdType.MESH

ADAM_LR = 0.001
ADAM_B1 = 0.9
ADAM_B2 = 0.999
ADAM_EPS = 1e-08
ADAM_WD = 0.01
ADAM_STEP = 10

VMEM_LIMIT = 56 * 1024 * 1024
FSUB = [(o, min(256, FS - o)) for o in range(0, FS, 256)]


def _cp(n_axes, **kw):
    return pltpu.CompilerParams(dimension_semantics=("arbitrary",) * n_axes,
                                vmem_limit_bytes=VMEM_LIMIT, **kw)


RESIDENT = pl.BlockSpec(memory_space=pltpu.VMEM)


def _n_after(after):
    return 0 if after is None else (len(after) if isinstance(after, (tuple, list)) else 1)


def _ordered(body, in_specs, args, after):
    if after is None:
        return body, in_specs, args
    extra = tuple(after) if isinstance(after, (tuple, list)) else (after,)
    return (lambda *refs: body(*refs[len(extra):]),
            [pl.BlockSpec(memory_space=pl.ANY)] * len(extra) + list(in_specs), extra + tuple(args))


def _nt_dot(a, b):
    return lax.dot_general(a, b, (((1,), (1,)), ((), ())), preferred_element_type=f32)


def _tn_dot(a, b):
    return lax.dot_general(a, b, (((0,), (0,)), ((), ())), preferred_element_type=f32)


def _sigmoid(x):
    return 0.5 * jnp.tanh(0.5 * x) + 0.5


def _log1p(y):
    u = 1.0 + y
    d = u - 1.0
    return jnp.where(d == 0.0, y, jnp.log(u) * (y / jnp.where(d == 0.0, 1.0, d)))


def _softplus(x):
    return jnp.maximum(x, 0.0) + _log1p(jnp.exp(-jnp.abs(x)))


def _one_minus_square(a, log_a):
    x = 2.0 * log_a
    series = x * (1.0 + x * (0.5 + x * (1.0 / 6.0)))
    return jnp.where(jnp.abs(x) < 0.03, -series, 1.0 - a * a)


_GELU_C = 0.7978845608028654
_GELU_K = 0.044715


def _gelu_and_grad(y):
    y2 = y * y
    th = jnp.tanh(_GELU_C * (y + _GELU_K * y * y2))
    gel = 0.5 * y * (1.0 + th)
    dgel = 0.5 * (1.0 + th) + 0.5 * y * (1.0 - th * th) * _GELU_C * (1.0 + 3.0 * _GELU_K * y2)
    return gel, dgel


def _rms_stats(h):
    return lax.rsqrt(jnp.mean(h * h, axis=-1, keepdims=True) + EPS)


def _rms_bwd(dn, h, g):
    r = _rms_stats(h)
    nhat = h * r
    dnh = dn * g
    dh = r * (dnh - nhat * jnp.mean(dnh * nhat, axis=-1, keepdims=True))
    dg = jnp.sum(dn * nhat, axis=0, keepdims=True)
    return dh, dg


def _row_ids(shape):
    return lax.broadcasted_iota(jnp.int32, shape, 0)


def _ffn_fwd(h, g, wgu, wd, name, loss_head=None):
    t = h.shape[0]
    nj = 2
    tm = TM
    n_head = 0 if loss_head is None else 2

    def body(*refs):
        h_ref, g_ref, wg_ref, wd_ref = refs[:4]
        outs = refs[4 + n_head:]
        gate_ref, up_ref, a_ref, n_ref = outs[-4:]
        i = pl.program_id(0)
        hh = h_ref[...]
        nb = (hh * _rms_stats(hh) * g_ref[...]).astype(bf16)
        n_ref[...] = nb

        acc = None
        for j in range(nj):
            for off, width in FSUB:
                cols = slice(off, off + width)
                out_cols = slice(j * FS + off, j * FS + off + width)
                gt = jnp.dot(nb, wg_ref[j, :, cols], preferred_element_type=f32)
                up = jnp.dot(nb, wg_ref[2 + j, :, cols], preferred_element_type=f32)
                gate_ref[:, out_cols] = gt.astype(bf16)
                up_ref[:, out_cols] = up.astype(bf16)
                a_ref[:, out_cols] = (gt * _sigmoid(gt) * up).astype(bf16)
            part = jnp.dot(a_ref[:, j * FS:(j + 1) * FS], wd_ref[j], preferred_element_type=f32)
            acc = part if acc is None else acc + part
        hh = hh + 0.5 * acc

        if loss_head is None:
            outs[0][...] = hh
        else:
            gf_ref, t_ref = refs[4:6]
            dh_ref, loss_ref, dgf_ref = outs[:3]

            @pl.when(i == 0)
            def _():
                loss_ref[...] = jnp.zeros_like(loss_ref)
                dgf_ref[...] = jnp.zeros_like(dgf_ref)

            gf = gf_ref[...]
            row = i * tm + _row_ids((tm, 1))
            valid = jnp.logical_and(row >= NMETA, row < loss_head[2])
            err = jnp.where(valid, hh * _rms_stats(hh) * gf - t_ref[...], 0.0)
            loss_ref[...] += 0.5 * jnp.sum(err * err) * (1.0 / D)
            dh, dgf = _rms_bwd(err * (1.0 / D), hh, gf)
            dh_ref[...] = dh
            dgf_ref[...] += dgf

    rowd = pl.BlockSpec((tm, D), lambda i: (i, 0))
    vec = pl.BlockSpec((1, D), lambda i: (0, 0))
    rowf = pl.BlockSpec((tm, F), lambda i: (i, 0))
    in_specs, args = [rowd, vec, RESIDENT, RESIDENT], [h, g, wgu, wd.reshape(nj, FS, D)]
    out_specs, out_shape = [rowd], [jax.ShapeDtypeStruct((t, D), f32)]
    if loss_head is not None:
        in_specs, args = in_specs + [vec, rowd], args + [loss_head[0], loss_head[1]]
        out_specs += [pl.BlockSpec((8, 128), lambda i: (0, 0)), vec]
        out_shape += [jax.ShapeDtypeStruct((8, 128), f32), jax.ShapeDtypeStruct((1, D), f32)]
    return pl.pallas_call(
        body, name=name, grid=(t // tm,),
        in_specs=in_specs,
        out_specs=out_specs + [rowf, rowf, rowf, rowd],
        out_shape=out_shape + [jax.ShapeDtypeStruct((t, F), bf16)] * 3 + [jax.ShapeDtypeStruct((t, D), bf16)],
        compiler_params=_cp(1),
    )(*args)


def _inproj_fwd(h, g, win, b_in):
    t = h.shape[0]
    tn = NIN // NCHIP

    def body(h_ref, g_ref, w_ref, b_ref, proj_ref, n_ref):
        hh = h_ref[...]
        nb = (hh * _rms_stats(hh) * g_ref[...]).astype(bf16)
        n_ref[...] = nb
        for s in range(NCHIP):
            cols = slice(s * tn, (s + 1) * tn)
            proj_ref[:, cols] = jnp.dot(nb, w_ref[s], preferred_element_type=f32) + b_ref[:, cols]

    return pl.pallas_call(
        body, name="inproj_fwd", grid=(t // TM,),
        in_specs=[
            pl.BlockSpec((TM, D), lambda i: (i, 0)),
            pl.BlockSpec((1, D), lambda i: (0, 0)),
            RESIDENT,
            pl.BlockSpec((1, NIN), lambda i: (0, 0)),
        ],
        out_specs=[
            pl.BlockSpec((TM, NIN), lambda i: (i, 0)),
            pl.BlockSpec((TM, D), lambda i: (i, 0)),
        ],
        out_shape=[jax.ShapeDtypeStruct((t, NIN), f32), jax.ShapeDtypeStruct((t, D), bf16)],
        compiler_params=_cp(1),
    )(h, g, win, b_in)


def _block_gates(xr, wa_ref, ba, wx_ref, bx, lam):
    xrb = xr.astype(bf16)
    pa = jnp.concatenate([jnp.dot(xrb[:, hh * HD:(hh + 1) * HD], wa_ref[hh], preferred_element_type=f32)
                          for hh in range(NHEAD)], axis=1)
    px = jnp.concatenate([jnp.dot(xrb[:, hh * HD:(hh + 1) * HD], wx_ref[hh], preferred_element_type=f32)
                          for hh in range(NHEAD)], axis=1)
    ra = _sigmoid(pa + ba)
    ii = _sigmoid(px + bx)
    sp = _softplus(-lam)
    log_a = -8.0 * ra * sp
    a = jnp.exp(log_a)
    sq = jnp.sqrt(_one_minus_square(a, log_a))
    return ra, ii, a, sq, sp


LT = D // 128
UNR = 13


def _to_lane_tiles(ref, value):
    for lt in range(LT):
        ref[lt] = value[:, lt * 128:(lt + 1) * 128]


def _from_lane_tiles(ref):
    return jnp.concatenate([ref[lt] for lt in range(LT)], axis=1)


def _chain_scan(mult_sc, val_sc, start, reverse):
    ng = TM // 8

    def lanes(lt):
        return slice(lt * 128, (lt + 1) * 128)

    def chain(gi, carry):
        v_prev, p_prev = carry
        rows = pl.ds(ng - 1 - gi if reverse else gi, 8, stride=ng)
        v_new, p_new = [], []
        for lt in range(LT):
            mm = mult_sc.at[lt][rows, :]
            vv = mm * v_prev[:, lanes(lt)] + val_sc.at[lt][rows, :]
            pp = mm * p_prev[:, lanes(lt)]
            val_sc.at[lt][rows, :] = vv
            mult_sc.at[lt][rows, :] = pp
            v_new.append(vv)
            p_new.append(pp)
        return jnp.concatenate(v_new, axis=1), jnp.concatenate(p_new, axis=1)

    v_end, p_end = lax.fori_loop(0, ng, chain, (jnp.zeros((8, D), f32), jnp.ones((8, D), f32)))
    state, entries = start, [None] * 8
    for r in (reversed(range(8)) if reverse else range(8)):
        entries[r] = state
        state = v_end[r:r + 1, :] + p_end[r:r + 1, :] * state
    entry8 = jnp.concatenate(entries, axis=0)

    def add_entry(gi, carry):
        rows = pl.ds(gi, 8, stride=ng)
        for lt in range(LT):
            val_sc.at[lt][rows, :] = val_sc.at[lt][rows, :] + mult_sc.at[lt][rows, :] * entry8[:, lanes(lt)]
        return carry

    lax.fori_loop(0, ng, add_entry, 0)
    return state


def _strided_conv(w_ref, src_sc, out_sc, base, shifts, bias_ref=None):
    ng = TM // 8
    for lt in range(LT):
        lanes = slice(lt * 128, (lt + 1) * 128)
        taps = [jnp.broadcast_to(w_ref[k:k + 1, lanes], (8, 128)) for k in range(len(shifts))]
        init = (jnp.zeros((8, 128), f32) if bias_ref is None
                else jnp.broadcast_to(bias_ref[:, lanes], (8, 128)))

        def step(gb, carry, lt=lt, taps=taps, init=init):
            accs = [init] * UNR
            for k, shift in enumerate(shifts):
                for u in range(UNR):
                    rows = pl.ds(base + gb * UNR + u + shift, 8, stride=ng)
                    accs[u] = accs[u] + taps[k] * src_sc.at[lt][rows, :]
            for u in range(UNR):
                out_sc.at[lt][pl.ds(gb * UNR + u, 8, stride=ng), :] = accs[u]
            return carry

        lax.fori_loop(0, ng // UNR, step, 0)


def _strided_corr(a_sc, src_sc, base, shifts):
    ng = TM // 8
    per_tile = []
    for lt in range(LT):
        def step(gb, accs, lt=lt):
            accs = list(accs)
            for u in range(UNR):
                g = gb * UNR + u
                a_g = a_sc.at[lt][pl.ds(g, 8, stride=ng), :]
                for k, shift in enumerate(shifts):
                    accs[k] = accs[k] + a_g * src_sc.at[lt][pl.ds(base + g + shift, 8, stride=ng), :]
            return tuple(accs)

        per_tile.append(lax.fori_loop(0, ng // UNR, step, tuple(jnp.zeros((8, 128), f32) for _ in shifts)))
    return [jnp.concatenate([per_tile[lt][k] for lt in range(LT)], axis=1) for k in range(len(shifts))]


def _rnn_fwd(proj, cw, cb, wa, ba, wx, bx, lam):
    t = proj.shape[0]

    def body(x_ref, y_ref, cw_ref, cb_ref, wa_ref, ba_ref, wx_ref, bx_ref, lam_ref,
             xr_ref, hr_ref, z_ref, gates_ref, xext_sc, carry_sc, a_sc, h_sc):
        i = pl.program_id(0)

        @pl.when(i == 0)
        def _():
            xext_sc[0:8, :] = jnp.zeros((8, D), f32)
            carry_sc[...] = jnp.zeros_like(carry_sc)

        x = x_ref[...]
        xext_sc[8:8 + TM, :] = x
        xe = xext_sc[...]
        xr = cb_ref[...] + cw_ref[KC4 - 1:KC4, :] * x
        for k in range(KC4 - 1):
            xr = xr + cw_ref[k:k + 1, :] * pltpu.roll(xe, KC4 - 1 - k, 0)[8:8 + TM]
        xext_sc[0:8, :] = x[TM - 8:TM]

        ra, ii, a, sq, _ = _block_gates(xr, wa_ref, ba_ref[...], wx_ref, bx_ref[...], lam_ref[...])
        for slot, val in enumerate((ra, ii, a, sq)):
            gates_ref[slot] = val
        _to_lane_tiles(a_sc, a)
        _to_lane_tiles(h_sc, sq * ii * xr)
        carry_sc[...] = _chain_scan(a_sc, h_sc, carry_sc[...], reverse=False)
        hr = _from_lane_tiles(h_sc)
        gel, _ = _gelu_and_grad(y_ref[...])
        xr_ref[...] = xr
        hr_ref[...] = hr
        z_ref[...] = (hr * gel).astype(bf16)

    vec = pl.BlockSpec((1, D), lambda i: (0, 0))
    return pl.pallas_call(
        body, name="rnn_fwd", grid=(t // TM,),
        in_specs=[
            pl.BlockSpec((TM, D), lambda i: (i, 0)),
            pl.BlockSpec((TM, D), lambda i: (i, 1)),
            pl.BlockSpec((KC4, D), lambda i: (0, 0)),
            vec,
            pl.BlockSpec((NHEAD, HD, HD), lambda i: (0, 0, 0)),
            vec,
            pl.BlockSpec((NHEAD, HD, HD), lambda i: (0, 0, 0)),
            vec, vec,
        ],
        out_specs=[pl.BlockSpec((TM, D), lambda i: (i, 0))] * 3 + [pl.BlockSpec((4, TM, D), lambda i: (0, i, 0))],
        out_shape=[jax.ShapeDtypeStruct((t, D), f32), jax.ShapeDtypeStruct((t, D), f32),
                   jax.ShapeDtypeStruct((t, D), bf16), jax.ShapeDtypeStruct((4, t, D), f32)],
        scratch_shapes=[pltpu.VMEM((TM + 8, D), f32), pltpu.VMEM((1, D), f32),
                        pltpu.VMEM((LT, TM, 128), f32), pltpu.VMEM((LT, TM, 128), f32)],
        compiler_params=_cp(1),
    )(proj, proj, cw, cb, wa, ba, wx, bx, lam)


def _ln_stats(vc):
    mu = jnp.mean(vc, axis=-1, keepdims=True)
    xc = vc - mu
    rstd = lax.rsqrt(jnp.mean(xc * xc, axis=-1, keepdims=True) + EPS)
    return xc * rstd, rstd


def _conv_fwd(proj, w31, b31, ln_g, ln_b, after=None):
    t = proj.shape[0]

    def body(gv_ref, gg_ref, w_ref, b_ref, lg_ref, lb_ref, vc_ref, s_ref, vext_sc, out_sc):
        i = pl.program_id(0)

        @pl.when(i == 0)
        def _():
            vext_sc[:, 0:HALO, :] = jnp.zeros((LT, HALO, 128), f32)

        v = gv_ref[...] * _sigmoid(gg_ref[...])
        for lt in range(LT):
            vext_sc[lt, HALO:HALO + TM, :] = v[:, lt * 128:(lt + 1) * 128]

        _strided_conv(w_ref, vext_sc, out_sc, HALO, [k - (KC31 - 1) for k in range(KC31)], b_ref)
        for lt in range(LT):
            vext_sc[lt, 0:HALO, :] = v[TM - HALO:TM, lt * 128:(lt + 1) * 128]
        acc = _from_lane_tiles(out_sc)
        xhat, _ = _ln_stats(acc)
        ln = xhat * lg_ref[...] + lb_ref[...]
        vc_ref[...] = acc
        s_ref[...] = (ln * _sigmoid(ln)).astype(bf16)

    vec = pl.BlockSpec((1, D), lambda i: (0, 0))
    body, in_specs, args = _ordered(
        body,
        [pl.BlockSpec((TM, D), lambda i: (i, 2)),
         pl.BlockSpec((TM, D), lambda i: (i, 3)),
         pl.BlockSpec((KC31, D), lambda i: (0, 0)),
         vec, vec, vec],
        (proj, proj, w31, b31, ln_g, ln_b), after)
    return pl.pallas_call(
        body, name="conv_fwd", grid=(t // TM,),
        in_specs=in_specs,
        out_specs=[pl.BlockSpec((TM, D), lambda i: (i, 0))] * 2,
        out_shape=[jax.ShapeDtypeStruct((t, D), f32), jax.ShapeDtypeStruct((t, D), bf16)],
        scratch_shapes=[pltpu.VMEM((LT, TM + HALO, 128), f32), pltpu.VMEM((LT, TM, 128), f32)],
        compiler_params=_cp(1),
    )(*args)


def _merge_fwd(h, z, s, proj, wrp, wcp, bcp, wout):
    t = h.shape[0]

    def body(h_ref, z_ref, s_ref, ga_ref, gb_ref, wrp_ref, wcp_ref, bcp_ref, wout_ref, ho_ref):
        ya = jnp.dot(z_ref[...], wrp_ref[...], preferred_element_type=f32)
        yb = jnp.dot(s_ref[...], wcp_ref[...], preferred_element_type=f32) + bcp_ref[...]
        merged = _sigmoid(ga_ref[...]) * ya + _sigmoid(gb_ref[...]) * yb
        ho_ref[...] = h_ref[...] + jnp.dot(merged.astype(bf16), wout_ref[...], preferred_element_type=f32)

    row = pl.BlockSpec((TM, D), lambda i: (i, 0))
    wsq = pl.BlockSpec((D, D), lambda i: (0, 0))
    return pl.pallas_call(
        body, name="merge_fwd", grid=(t // TM,),
        in_specs=[row, row, row,
                  pl.BlockSpec((TM, D), lambda i: (i, 4)),
                  pl.BlockSpec((TM, D), lambda i: (i, 5)),
                  wsq, wsq, pl.BlockSpec((1, D), lambda i: (0, 0)), wsq],
        out_specs=row,
        out_shape=jax.ShapeDtypeStruct((t, D), f32),
        compiler_params=_cp(1),
    )(h, z, s, proj, proj, wrp, wcp, bcp, wout)


def _ffn_bwd(dh, h, g, gate, up, wgu, wd, name, after=None, grad_rows=None):
    t = h.shape[0]
    nj = 2
    nt = t // TM
    if grad_rows is not None:
        first, stop = grad_rows
        last_rows = stop - (nt - 1) * TM
        assert nt >= 2 and first % 8 == 0 and last_rows % 8 == 0 and 0 < first < TM and 0 < last_rows <= TM

    def body(dh_ref, h_ref, g_ref, gate_ref, up_ref, wg_ref, wd_ref,
             dhi_ref, dgate_ref, dup_ref, df_ref, dg_ref, *extra):
        @pl.when(pl.program_id(0) == 0)
        def _():
            dg_ref[...] = jnp.zeros_like(dg_ref)

        dh = dh_ref[...]
        dfb = (0.5 * dh).astype(bf16)
        df_ref[...] = dfb

        dn = None
        for j in range(nj):
            half = slice(j * FS, (j + 1) * FS)
            for off, width in FSUB:
                cols = slice(off, off + width)
                out_cols = slice(j * FS + off, j * FS + off + width)
                da = _nt_dot(dfb, wd_ref[j, cols, :])
                gt = gate_ref[:, out_cols].astype(f32)
                uu = up_ref[:, out_cols].astype(f32)
                sg = _sigmoid(gt)
                dgate_ref[:, out_cols] = (da * uu * (sg * (1.0 + gt * (1.0 - sg)))).astype(bf16)
                dup_ref[:, out_cols] = (da * (gt * sg)).astype(bf16)
            part = _nt_dot(dgate_ref[:, half], wg_ref[j]) + _nt_dot(dup_ref[:, half], wg_ref[2 + j])
            dn = part if dn is None else dn + part
        dhin, dg = _rms_bwd(dn, h_ref[...], g_ref[...])
        dg_ref[...] += dg

        if grad_rows is None:
            dhi_ref[...] = dh + dhin
        else:
            gx_ref, stage_sc, sem = extra
            i = pl.program_id(0)

            def copy(tile, src0, rows):
                return pltpu.make_async_copy(
                    stage_sc.at[pl.ds(src0, rows)],
                    gx_ref.at[pl.ds(pl.multiple_of(tile * TM + src0 - first, 8), rows)],
                    sem.at[0])

            @pl.when(i == 1)
            def _():
                copy(0, first, TM - first).wait()

            @pl.when(i > 1)
            def _():
                copy(i - 1, 0, TM).wait()

            dhi = dh + dhin
            dhi_ref[...] = dhi
            stage_sc[...] = dhi

            @pl.when(i == 0)
            def _():
                copy(0, first, TM - first).start()

            @pl.when(jnp.logical_and(i > 0, i < nt - 1))
            def _():
                copy(i, 0, TM).start()

            @pl.when(i == nt - 1)
            def _():
                tail = copy(nt - 1, 0, last_rows)
                tail.start()
                tail.wait()

    rowd = pl.BlockSpec((TM, D), lambda i: (i, 0))
    rowf = pl.BlockSpec((TM, F), lambda i: (i, 0))
    vec = pl.BlockSpec((1, D), lambda i: (0, 0))
    body, in_specs, args = _ordered(
        body,
        [rowd, rowd, vec, rowf, rowf, RESIDENT, RESIDENT],
        (dh, h, g, gate, up, wgu, wd.reshape(nj, FS, D)), after)
    out_specs = [rowd, rowf, rowf, rowd, vec]
    out_shape = [jax.ShapeDtypeStruct((t, D), f32), jax.ShapeDtypeStruct((t, F), bf16),
                 jax.ShapeDtypeStruct((t, F), bf16),
                 jax.ShapeDtypeStruct((t, D), bf16), jax.ShapeDtypeStruct((1, D), f32)]
    scratch = []
    if grad_rows is not None:
        out_specs.append(pl.BlockSpec(memory_space=pl.ANY))
        out_shape.append(jax.ShapeDtypeStruct((stop - first, D), f32))
        scratch += [pltpu.VMEM((TM, D), f32), pltpu.SemaphoreType.DMA((1,))]
    return pl.pallas_call(
        body, name=name, grid=(nt,),
        in_specs=in_specs,
        out_specs=out_specs,
        out_shape=out_shape,
        scratch_shapes=scratch,
        compiler_params=_cp(1),
    )(*args)


def _big_tile(t):
    return max(k * TM for k in range(1, 6) if t % (k * TM) == 0)


ANY_SPEC = pl.BlockSpec(memory_space=pl.ANY)


def _tn_matmul(a, b, tk, tn, out_shape, out_block, out_map, name, base=None, after=None):
    t, kk = a.shape
    _, nn = b.shape
    tmm = _big_tile(t)
    nm = t // tmm

    def body(a_ref, b_ref, o_ref, acc_sc):
        m = pl.program_id(2)

        @pl.when(m == 0)
        def _():
            acc_sc[...] = jnp.zeros_like(acc_sc)

        acc_sc[...] += _tn_dot(a_ref[...], b_ref[...])

        @pl.when(m == nm - 1)
        def _():
            o_ref[...] = acc_sc[...].astype(o_ref.dtype)

    in_specs = [pl.BlockSpec((tmm, tk), lambda k, n, m: (m, k)),
                pl.BlockSpec((tmm, tn), lambda k, n, m: (m, n))]
    args, aliases = (a, b), {}
    if base is not None:
        body = (lambda inner: lambda a_ref, b_ref, base_ref, o_ref, acc_sc: inner(a_ref, b_ref, o_ref, acc_sc))(body)
        in_specs, args, aliases = in_specs + [ANY_SPEC], (a, b, base), {2: 0}
    if after is not None:
        body, in_specs, args = _ordered(body, in_specs, args, after)
        aliases = {k + _n_after(after): v for k, v in aliases.items()}
    return pl.pallas_call(
        body, name=name, grid=(kk // tk, nn // tn, nm),
        in_specs=in_specs,
        out_specs=pl.BlockSpec(out_block, out_map),
        out_shape=jax.ShapeDtypeStruct(out_shape, bf16),
        scratch_shapes=[pltpu.VMEM((tk, tn), f32)],
        input_output_aliases=aliases,
        compiler_params=_cp(3),
    )(*args)


def _merge_bwd(dh, z, s, proj, wrp, wcp, bcp, wout, after=None):
    t = dh.shape[0]

    def body(dh_ref, z_ref, s_ref, ga_ref, gb_ref, wrp_ref, wcp_ref, bcp_ref, wout_ref,
             dz_ref, ds_ref, dgab_ref, dhb_ref, mg_ref, dya_ref, dyb_ref, dbcp_ref):
        i = pl.program_id(0)

        @pl.when(i == 0)
        def _():
            dbcp_ref[...] = jnp.zeros_like(dbcp_ref)

        dhb = dh_ref[...].astype(bf16)
        dhb_ref[...] = dhb
        dmg = _nt_dot(dhb, wout_ref[...])
        ya = jnp.dot(z_ref[...], wrp_ref[...], preferred_element_type=f32)
        yb = jnp.dot(s_ref[...], wcp_ref[...], preferred_element_type=f32) + bcp_ref[...]
        sa = _sigmoid(ga_ref[...])
        sb = _sigmoid(gb_ref[...])
        mg_ref[...] = (sa * ya + sb * yb).astype(bf16)
        dgab_ref[:, 0:D] = (dmg * ya * sa * (1.0 - sa)).astype(bf16)
        dgab_ref[:, D:2 * D] = (dmg * yb * sb * (1.0 - sb)).astype(bf16)
        dya = dmg * sa
        dyb = dmg * sb
        dbcp_ref[...] += jnp.sum(dyb, axis=0, keepdims=True)
        dyab = dya.astype(bf16)
        dybb = dyb.astype(bf16)
        dya_ref[...] = dyab
        dyb_ref[...] = dybb
        dz_ref[...] = _nt_dot(dyab, wrp_ref[...])
        ds_ref[...] = _nt_dot(dybb, wcp_ref[...])

    row = pl.BlockSpec((TM, D), lambda i: (i, 0))
    wsq = pl.BlockSpec((D, D), lambda i: (0, 0))
    vec = pl.BlockSpec((1, D), lambda i: (0, 0))
    rowb = jax.ShapeDtypeStruct((t, D), bf16)
    body, in_specs, args = _ordered(
        body,
        [row, row, row,
         pl.BlockSpec((TM, D), lambda i: (i, 4)),
         pl.BlockSpec((TM, D), lambda i: (i, 5)),
         wsq, wsq, vec, wsq],
        (dh, z, s, proj, proj, wrp, wcp, bcp, wout), after)
    return pl.pallas_call(
        body, name="merge_bwd", grid=(t // TM,),
        in_specs=in_specs,
        out_specs=[row, row,
                   pl.BlockSpec((TM, 2 * D), lambda i: (i, 2)),
                   row, row, row, row, vec],
        out_shape=[jax.ShapeDtypeStruct((t, D), f32), jax.ShapeDtypeStruct((t, D), f32),
                   jax.ShapeDtypeStruct((t, NIN), bf16),
                   rowb, rowb, rowb, rowb, jax.ShapeDtypeStruct((1, D), f32)],
        compiler_params=_cp(1),
    )(*args)


def _conv_bwd(ds, vc, proj, dproj, w31, ln_g, ln_b, after=None):
    t = ds.shape[0]
    nt = t // TM
    hb = TM // HALO

    def body(ds_ref, vc_ref, gv_ref, gg_ref, gvp_ref, ggp_ref, dpin_ref, w_ref, lg_ref, lb_ref,
             dgvg_ref, dw_ref, db_ref, dlg_ref, dlb_ref, dext_sc, vext_sc, out_sc, dwacc_sc, small_sc):
        del dpin_ref
        i = pl.program_id(0)
        tile = nt - 1 - i

        @pl.when(i == 0)
        def _():
            dext_sc[:, TM:TM + HALO, :] = jnp.zeros((LT, HALO, 128), f32)
            dwacc_sc[...] = jnp.zeros_like(dwacc_sc)
            small_sc[...] = jnp.zeros_like(small_sc)

        lg = lg_ref[...]
        lb = lb_ref[...]

        xhat, rstd = _ln_stats(vc_ref[...])
        ln = xhat * lg + lb
        sg = _sigmoid(ln)
        dln = ds_ref[...] * (sg * (1.0 + ln * (1.0 - sg)))
        dxh = dln * lg
        dvc = rstd * (dxh - jnp.mean(dxh, axis=-1, keepdims=True)
                      - xhat * jnp.mean(dxh * xhat, axis=-1, keepdims=True))
        small_sc[0] += jnp.sum((dln * xhat).reshape(TM // 8, 8, D), axis=0)
        small_sc[1] += jnp.sum(dln.reshape(TM // 8, 8, D), axis=0)
        small_sc[2] += jnp.sum(dvc.reshape(TM // 8, 8, D), axis=0)
        sgg = _sigmoid(gg_ref[...])
        v = gv_ref[...] * sgg
        vprev = jnp.where(tile > 0, gvp_ref[...] * _sigmoid(ggp_ref[...]), 0.0)
        for lt in range(LT):
            lanes = slice(lt * 128, (lt + 1) * 128)
            dext_sc[lt, 0:TM, :] = dvc[:, lanes]
            vext_sc[lt, HALO:HALO + TM, :] = v[:, lanes]
            vext_sc[lt, 0:HALO, :] = vprev[:, lanes]

        _strided_conv(w_ref, dext_sc, out_sc, 0, [KC31 - 1 - k for k in range(KC31)])
        dv = _from_lane_tiles(out_sc)
        dgvg_ref[:, 0:D] = (dv * sgg).astype(bf16)
        dgvg_ref[:, D:2 * D] = (dv * gv_ref[...] * sgg * (1.0 - sgg)).astype(bf16)

        for k, part in enumerate(_strided_corr(dext_sc, vext_sc, HALO, [k - (KC31 - 1) for k in range(KC31)])):
            dwacc_sc[k] += part
        for lt in range(LT):
            dext_sc[lt, TM:TM + HALO, :] = dext_sc[lt, 0:HALO, :]

        @pl.when(i == nt - 1)
        def _():
            for k in range(KC31):
                dw_ref[k:k + 1, :] = jnp.sum(dwacc_sc[k], axis=0, keepdims=True)
            dlg_ref[...] = jnp.sum(small_sc[0], axis=0, keepdims=True)
            dlb_ref[...] = jnp.sum(small_sc[1], axis=0, keepdims=True)
            db_ref[...] = jnp.sum(small_sc[2], axis=0, keepdims=True)

    rev = lambda i: (nt - 1 - i, 0)
    vec = pl.BlockSpec((1, D), lambda i: (0, 0))
    halo_row = lambda i: jnp.maximum((nt - 1 - i) * hb - 1, 0)
    body, in_specs, args = _ordered(
        body,
        [pl.BlockSpec((TM, D), rev),
         pl.BlockSpec((TM, D), rev),
         pl.BlockSpec((TM, D), lambda i: (nt - 1 - i, 2)),
         pl.BlockSpec((TM, D), lambda i: (nt - 1 - i, 3)),
         pl.BlockSpec((HALO, D), lambda i: (halo_row(i), 2)),
         pl.BlockSpec((HALO, D), lambda i: (halo_row(i), 3)),
         pl.BlockSpec(memory_space=pl.ANY),
         pl.BlockSpec((KC31, D), lambda i: (0, 0)),
         vec, vec],
        (ds, vc, proj, proj, proj, proj, dproj, w31, ln_g, ln_b), after)
    return pl.pallas_call(
        body, name="conv_bwd", grid=(nt,),
        in_specs=in_specs,
        out_specs=[
            pl.BlockSpec((TM, 2 * D), lambda i: (nt - 1 - i, 1)),
            pl.BlockSpec((KC31, D), lambda i: (0, 0)),
            vec, vec, vec,
        ],
        out_shape=[jax.ShapeDtypeStruct((t, NIN), bf16),
                   jax.ShapeDtypeStruct((KC31, D), f32),
                   jax.ShapeDtypeStruct((1, D), f32), jax.ShapeDtypeStruct((1, D), f32),
                   jax.ShapeDtypeStruct((1, D), f32)],
        scratch_shapes=[pltpu.VMEM((LT, TM + HALO, 128), f32), pltpu.VMEM((LT, TM + HALO, 128), f32),
                        pltpu.VMEM((LT, TM, 128), f32), pltpu.VMEM((KC31, 8, D), f32),
                        pltpu.VMEM((3, 8, D), f32)],
        input_output_aliases={6 + _n_after(after): 0},
        compiler_params=_cp(1),
    )(*args)


def _rnn_bwd(dz, xr, hr, gates, proj, dproj, cw, wa, wx, lam):
    t = dz.shape[0]
    nt = t // TM
    ng = TM // 8
    hq = HD // NCHIP

    def body(dz_ref, xr_ref, hr_ref, hrp_ref, x_ref, xp_ref, y_ref, dpin_ref,
             cw_ref, wa_ref, gates_ref, wx_ref, lam_ref,
             dxy_ref, dwa_ref, dwx_ref, dcw_ref, dcb_ref, dba_ref, dbx_ref, dlam_ref,
             anext_sc, gcarry_sc, dext_sc, xext_sc, m_sc, g_sc, dwa_sc, dwx_sc, dsp_sc):
        del dpin_ref
        i = pl.program_id(0)
        tile = nt - 1 - i

        @pl.when(i == 0)
        def _():
            anext_sc[...] = jnp.zeros_like(anext_sc)
            gcarry_sc[...] = jnp.zeros_like(gcarry_sc)
            dext_sc[TM:TM + 8, :] = jnp.zeros((8, D), f32)
            dwa_sc[...] = jnp.zeros_like(dwa_sc)
            dwx_sc[...] = jnp.zeros_like(dwx_sc)
            dsp_sc[...] = jnp.zeros_like(dsp_sc)
            dcw_ref[...] = jnp.zeros_like(dcw_ref)
            dcb_ref[...] = jnp.zeros_like(dcb_ref)
            dba_ref[...] = jnp.zeros_like(dba_ref)
            dbx_ref[...] = jnp.zeros_like(dbx_ref)

        xr = xr_ref[...]
        hr = hr_ref[...]
        dz = dz_ref[...]
        gel, dgel = _gelu_and_grad(y_ref[...])
        dxy_ref[:, D:2 * D] = (dz * hr * dgel).astype(bf16)
        ra, ii, a, sq = gates_ref[0], gates_ref[1], gates_ref[2], gates_ref[3]
        sp = _softplus(-lam_ref[...])

        row = _row_ids((TM, D))
        _to_lane_tiles(m_sc, jnp.where(row == TM - 1, anext_sc[...], pltpu.roll(a, TM - 1, 0)))
        anext_sc[...] = a[0:1, :]
        _to_lane_tiles(g_sc, dz * gel)
        gcarry_sc[...] = _chain_scan(m_sc, g_sc, gcarry_sc[...], reverse=True)
        gg = _from_lane_tiles(g_sc)

        hlast = jnp.where(tile > 0, hrp_ref[7:8, :], 0.0)
        hprev = jnp.where(row == 0, hlast, pltpu.roll(hr, 1, 0))
        d_a = gg * hprev
        dsq = gg * ii * xr
        dii = gg * sq * xr
        dxr = gg * sq * ii
        dlog = d_a * a - dsq * (a * a / sq)
        dsp_sc[...] += jnp.sum(dlog * (-8.0 * ra), axis=0, keepdims=True)
        dpa = dlog * (-8.0 * sp) * ra * (1.0 - ra)
        dpx = dii * ii * (1.0 - ii)
        dba_ref[...] += jnp.sum(dpa, axis=0, keepdims=True)
        dbx_ref[...] += jnp.sum(dpx, axis=0, keepdims=True)
        dpab = dpa.astype(bf16)
        dpxb = dpx.astype(bf16)
        xrb = xr.astype(bf16)
        back = []
        for hh in range(NHEAD):
            cols = slice(hh * HD, (hh + 1) * HD)
            back.append(_nt_dot(dpab[:, cols], wa_ref[hh]) + _nt_dot(dpxb[:, cols], wx_ref[hh]))
            dwa_sc[hh] += _tn_dot(xrb[:, cols], dpab[:, cols])
            dwx_sc[hh] += _tn_dot(xrb[:, cols], dpxb[:, cols])
        dxr = dxr + jnp.concatenate(back, axis=1)

        dext_sc[0:TM, :] = dxr
        de = dext_sc[...]
        dx = cw_ref[KC4 - 1:KC4, :] * dxr
        for k in range(KC4 - 1):
            dx = dx + cw_ref[k:k + 1, :] * pltpu.roll(de, TM + 8 - (KC4 - 1 - k), 0)[0:TM]
        dext_sc[TM:TM + 8, :] = dxr[0:8]
        dxy_ref[:, 0:D] = dx.astype(bf16)

        x = x_ref[...]
        xext_sc[0:8, :] = jnp.where(tile > 0, xp_ref[...], 0.0)
        xext_sc[8:8 + TM, :] = x
        xe = xext_sc[...]
        dcw_ref[KC4 - 1:KC4, :] += jnp.sum(dxr * x, axis=0, keepdims=True)
        for k in range(KC4 - 1):
            xs = pltpu.roll(xe, KC4 - 1 - k, 0)[8:8 + TM]
            dcw_ref[k:k + 1, :] += jnp.sum(dxr * xs, axis=0, keepdims=True)
        dcb_ref[...] += jnp.sum(dxr, axis=0, keepdims=True)

        @pl.when(i == nt - 1)
        def _():
            for hh in range(NHEAD):
                for qc in range(NCHIP):
                    dwa_ref[qc, hh] = dwa_sc[hh, qc * hq:(qc + 1) * hq, :].astype(bf16)
                    dwx_ref[qc, hh] = dwx_sc[hh, qc * hq:(qc + 1) * hq, :].astype(bf16)
            dlam_ref[...] = -dsp_sc[...] * _sigmoid(-lam_ref[...])

    rev = lambda i: (nt - 1 - i, 0)
    vec = pl.BlockSpec((1, D), lambda i: (0, 0))
    prev8 = lambda i: jnp.maximum((nt - 1 - i) * ng - 1, 0)
    wblk = pl.BlockSpec((NHEAD, HD, HD), lambda i: (0, 0, 0))
    gblk = pl.BlockSpec((NCHIP, NHEAD, hq, HD), lambda i: (0, 0, 0, 0))
    return pl.pallas_call(
        body, name="rnn_bwd", grid=(nt,),
        in_specs=[
            pl.BlockSpec((TM, D), rev),
            pl.BlockSpec((TM, D), rev),
            pl.BlockSpec((TM, D), rev),
            pl.BlockSpec((8, D), lambda i: (prev8(i), 0)),
            pl.BlockSpec((TM, D), lambda i: (nt - 1 - i, 0)),
            pl.BlockSpec((8, D), lambda i: (prev8(i), 0)),
            pl.BlockSpec((TM, D), lambda i: (nt - 1 - i, 1)),
            pl.BlockSpec(memory_space=pl.ANY),
            pl.BlockSpec((KC4, D), lambda i: (0, 0)),
            wblk, pl.BlockSpec((4, TM, D), lambda i: (0, nt - 1 - i, 0)), wblk, vec,
        ],
        out_specs=[
            pl.BlockSpec((TM, 2 * D), lambda i: (nt - 1 - i, 0)),
            gblk, gblk,
            pl.BlockSpec((KC4, D), lambda i: (0, 0)),
            vec, vec, vec, vec,
        ],
        out_shape=[jax.ShapeDtypeStruct((t, NIN), bf16),
                   jax.ShapeDtypeStruct((NCHIP, NHEAD, hq, HD), bf16),
                   jax.ShapeDtypeStruct((NCHIP, NHEAD, hq, HD), bf16),
                   jax.ShapeDtypeStruct((KC4, D), f32),
                   jax.ShapeDtypeStruct((1, D), f32), jax.ShapeDtypeStruct((1, D), f32),
                   jax.ShapeDtypeStruct((1, D), f32), jax.ShapeDtypeStruct((1, D), f32)],
        scratch_shapes=[pltpu.VMEM((1, D), f32), pltpu.VMEM((1, D), f32),
                        pltpu.VMEM((TM + 8, D), f32), pltpu.VMEM((TM + 8, D), f32),
                        pltpu.VMEM((LT, TM, 128), f32), pltpu.VMEM((LT, TM, 128), f32),
                        pltpu.VMEM((NHEAD, HD, HD), f32), pltpu.VMEM((NHEAD, HD, HD), f32),
                        pltpu.VMEM((1, D), f32)],
        input_output_aliases={7: 0},
        compiler_params=_cp(1),
    )(dz, xr, hr, hr, proj, proj, proj, dproj, cw, wa, gates, wx, lam)


def _inproj_bwd(dproj, dh, h, g, win, after=None):
    t = h.shape[0]
    tn = NIN // NCHIP

    def body(dp_ref, dh_ref, h_ref, g_ref, w_ref, dhi_ref, dg_ref, db_ref):
        @pl.when(pl.program_id(0) == 0)
        def _():
            dg_ref[...] = jnp.zeros_like(dg_ref)
            db_ref[...] = jnp.zeros_like(db_ref)

        dn = None
        for s in range(NCHIP):
            dp = dp_ref[:, s * tn:(s + 1) * tn]
            part = _nt_dot(dp, w_ref[s])
            dn = part if dn is None else dn + part
            db_ref[s] += jnp.sum(dp.astype(f32), axis=0, keepdims=True)
        dhin, dg = _rms_bwd(dn, h_ref[...], g_ref[...])
        dhi_ref[...] = dh_ref[...] + dhin
        dg_ref[...] += dg

    rowd = pl.BlockSpec((TM, D), lambda i: (i, 0))
    vec = pl.BlockSpec((1, D), lambda i: (0, 0))
    body, in_specs, args = _ordered(
        body,
        [pl.BlockSpec((TM, NIN), lambda i: (i, 0)), rowd, rowd, vec, RESIDENT],
        (dproj, dh, h, g, win), after)
    return pl.pallas_call(
        body, name="inproj_bwd", grid=(t // TM,),
        in_specs=in_specs,
        out_specs=[rowd, vec, pl.BlockSpec((NCHIP, 1, tn), lambda i: (0, 0, 0))],
        out_shape=[jax.ShapeDtypeStruct((t, D), f32), jax.ShapeDtypeStruct((1, D), f32),
                   jax.ShapeDtypeStruct((NCHIP, 1, tn), f32)],
        compiler_params=_cp(1),
    )(*args)


def _ffn_gu_grad(n, dgate, dup, tag, after=None):
    half = _tn_matmul(n, dgate, D, FS, (NCHIP, D, FS), (None, D, FS), lambda k, nn, m: (nn, 0, 0),
                      tag + "_dwg", after=after)
    return _tn_matmul(n, dup, D, FS, (NCHIP, D, FS), (None, D, FS), lambda k, nn, m: (2 + nn, 0, 0),
                      tag + "_dwu", base=half)


def _ffn_down_grad(a, df, tag, after=None):
    return _tn_matmul(a, df, FS, D, (F, D), (FS, D), lambda k, nn, m: (k, 0), tag + "_dwd", after=after)


def _square_grad(a, b, name):
    return _tn_matmul(a, b, D, D, (D, D), (D, D), lambda k, nn, m: (0, 0), name)


ANY = pl.BlockSpec(memory_space=pl.ANY)


def _place():
    x, y, c = lax.axis_index("x"), lax.axis_index("y"), lax.axis_index("c")
    chips = [(1 - x, y), (x, 1 - y), (1 - x, 1 - y)]
    return x, y, c, chips


def _chip_id(chip):
    return 2 * chip[0] + chip[1]


def _cast_into_slot(w2d, qc, dtype, name, after=None):
    r, cc = w2d.shape
    hr = r // 2

    def body(qc_ref, *refs):
        del qc_ref
        w_ref, o_ref = refs[-2:]
        o_ref[...] = w_ref[...].astype(dtype)

    in_specs, args = [pl.BlockSpec((hr, cc), lambda h, qc_ref: (h, 0))], (w2d,)
    if after is not None:
        in_specs, args = [ANY_SPEC] + in_specs, (after,) + args
    return pl.pallas_call(
        body, name=name,
        grid_spec=pltpu.PrefetchScalarGridSpec(
            num_scalar_prefetch=1, grid=(2,),
            in_specs=in_specs,
            out_specs=pl.BlockSpec((None, None, hr, cc), lambda h, qc_ref: (qc_ref[0], h, 0, 0))),
        out_shape=jax.ShapeDtypeStruct((NCHIP, 2, hr, cc), dtype),
        compiler_params=_cp(1),
    )(qc, *args)


def _place_pack(pieces, rows, qc):
    def body(qc_ref, *refs):
        del qc_ref
        o_ref = refs[-1]
        o_ref[...] = jnp.zeros(o_ref.shape, f32)
        for (start, arr), ref in zip(pieces, refs[:-1]):
            r, m = arr.shape[0], arr.shape[1] // D
            if m == 1:
                o_ref[start:start + r, :] = ref[...]
            else:
                for j in range(r):
                    for q in range(m):
                        row = start + j * m + q
                        o_ref[row:row + 1, :] = ref[j:j + 1, q * D:(q + 1) * D]

    return pl.pallas_call(
        body, name="place_pack",
        grid_spec=pltpu.PrefetchScalarGridSpec(
            num_scalar_prefetch=1, grid=(1,),
            in_specs=[pl.BlockSpec(a.shape, lambda i, qc_ref: (0, 0)) for _, a in pieces],
            out_specs=pl.BlockSpec((None, rows, D), lambda i, qc_ref: (2 * qc_ref[0] + qc_ref[1], 0, 0))),
        out_shape=jax.ShapeDtypeStruct((8, rows, D), f32),
        compiler_params=_cp(1),
    )(qc, *[a for _, a in pieces])


def _pair_add(parts, gots, qc, name):
    n = len(parts)

    def body(qc_ref, *refs):
        s = pl.program_id(0)
        for a in range(n):
            val = (refs[a][...].astype(f32) + refs[n + a][...].astype(f32)).astype(bf16)
            refs[2 * n + a][...] = val

            @pl.when(s == qc_ref[0])
            def _(val=val, land_ref=refs[3 * n + a]):
                land_ref[...] = val

    shapes = [p.shape[2:] for p in parts]
    mine = [pl.BlockSpec((None, None) + sh, lambda s, qc_ref: (s, qc_ref[1], 0, 0)) for sh in shapes]
    block = [pl.BlockSpec((None,) + sh, lambda s, qc_ref: (s, 0, 0)) for sh in shapes]
    own = [pl.BlockSpec((None,) + sh, lambda s, qc_ref: (qc_ref[0], 0, 0)) for sh in shapes]
    outs = pl.pallas_call(
        body, name=name,
        grid_spec=pltpu.PrefetchScalarGridSpec(
            num_scalar_prefetch=1, grid=(NCHIP,), in_specs=mine + block, out_specs=block + own),
        out_shape=[jax.ShapeDtypeStruct((NCHIP,) + sh, bf16) for sh in shapes] * 2,
        compiler_params=_cp(1),
    )(qc, *parts, *gots)
    return list(outs[:n]), list(outs[n:])


def _sum_chips(gots, name):
    n = len(gots)

    def body(*refs):
        for a in range(n):
            acc = refs[a][0].astype(f32)
            for s in range(1, NCHIP):
                acc = acc + refs[a][s].astype(f32)
            refs[n + a][...] = acc

    return list(pl.pallas_call(
        body, name=name, grid=(1,),
        in_specs=[pl.BlockSpec(g.shape, lambda i: (0, 0, 0)) for g in gots],
        out_specs=[pl.BlockSpec(g.shape[1:], lambda i: (0, 0)) for g in gots],
        out_shape=[jax.ShapeDtypeStruct(g.shape[1:], f32) for g in gots],
        compiler_params=_cp(1),
    )(*gots))


def _pair_share(halves, name, after=None):
    n = len(halves)
    extra = () if after is None else (after,)

    def body(*refs):
        refs = refs[len(extra):]
        ins, outs = refs[:n], refs[n:2 * n]
        send_sems, recv_sems = refs[2 * n:]
        x, y, c, _ = _place()
        copies = []
        for a in range(n):
            cp = pltpu.make_async_remote_copy(
                src_ref=ins[a], dst_ref=outs[a], send_sem=send_sems.at[a], recv_sem=recv_sems.at[a],
                device_id=(x, y, 1 - c), device_id_type=MESH)
            cp.start()
            copies.append(cp)
        for cp in copies:
            cp.wait()

    return pl.pallas_call(
        body, name=name,
        in_specs=[ANY] * (len(extra) + n), out_specs=[ANY] * n,
        out_shape=[jax.ShapeDtypeStruct(s.shape, s.dtype) for s in halves],
        scratch_shapes=[pltpu.SemaphoreType.DMA((n,)), pltpu.SemaphoreType.DMA((n,))],
    )(*extra, *halves)


def _all_copy(buf_ref, send_ref, recv_ref, k, x, y, c, landing):
    px, py, pc = (1 - x if k & 4 else x, 1 - y if k & 2 else y, 1 - c if k & 1 else c)
    me = 4 * x + 2 * y + c
    there = 4 * px + 2 * py + pc
    return pltpu.make_async_remote_copy(
        src_ref=buf_ref.at[me], dst_ref=buf_ref.at[there if landing else me],
        send_sem=send_ref.at[k - 1], recv_sem=recv_ref.at[k - 1],
        device_id=(px, py, pc), device_id_type=MESH)


def _gather_all_start(buf, name):
    def body(in_ref, send, recv, thru, token):
        del thru
        x, y, c, _ = _place()
        for k in range(1, 8):
            _all_copy(in_ref, send, recv, k, x, y, c, False).start()
        token[...] = jnp.zeros_like(token)

    return pl.pallas_call(
        body, name=name,
        in_specs=[HBM],
        out_specs=[SEM, SEM, HBM, pl.BlockSpec(memory_space=pltpu.VMEM)],
        out_shape=[pltpu.SemaphoreType.DMA((7,)), pltpu.SemaphoreType.DMA((7,)),
                   pltpu.HBM(buf.shape, buf.dtype), jax.ShapeDtypeStruct((8, 128), f32)],
        input_output_aliases={0: 2},
        compiler_params=pltpu.CompilerParams(has_side_effects=EFFECT),
    )(_in_hbm(buf))


def _gather_all_wait(send, recv, buf, after, name):
    def body(in_ref, send_r, recv_r, after_ref, out_ref):
        del after_ref, out_ref
        x, y, c, _ = _place()
        for k in range(1, 8):
            cp = _all_copy(in_ref, send_r, recv_r, k, x, y, c, True)
            cp.wait_send()
            cp.wait_recv()

    return pl.pallas_call(
        body, name=name,
        in_specs=[HBM, SEM, SEM, ANY],
        out_specs=HBM,
        out_shape=pltpu.HBM(buf.shape, buf.dtype),
        input_output_aliases={0: 0},
        compiler_params=pltpu.CompilerParams(has_side_effects=EFFECT),
    )(buf, send, recv, after)


HBM = pl.BlockSpec(memory_space=pltpu.HBM)
SEM = pl.BlockSpec(memory_space=pltpu.SEMAPHORE)
EFFECT = pltpu.SideEffectType.DATAFLOW_SIDE_EFFECTING
N_PEER = 3


def _in_hbm(a):
    return pltpu.with_memory_space_constraint(a, pltpu.HBM)


def _gather_copy(buf_ref, send_ref, recv_ref, j, chip, q, c, landing_chip):
    return pltpu.make_async_remote_copy(
        src_ref=buf_ref.at[q, c], dst_ref=buf_ref.at[landing_chip, c],
        send_sem=send_ref.at[j], recv_sem=recv_ref.at[j],
        device_id=(chip[0], chip[1], c), device_id_type=MESH)


def _gather_start(bufs, name):
    n = len(bufs)

    def body(*refs):
        ins = refs[:n]
        send, recv = refs[n:2 * n], refs[2 * n:3 * n]
        token = refs[4 * n]
        x, y, c, chips = _place()
        q = 2 * x + y
        for a in range(n):
            for j, chip in enumerate(chips):
                _gather_copy(ins[a], send[a], recv[a], j, chip, q, c, q).start()
        token[...] = jnp.zeros_like(token)

    sems = [pltpu.SemaphoreType.DMA((N_PEER,))] * (2 * n)
    outs = pl.pallas_call(
        body, name=name,
        in_specs=[HBM] * n,
        out_specs=[SEM] * (2 * n) + [HBM] * n + [pl.BlockSpec(memory_space=pltpu.VMEM)],
        out_shape=sems + [pltpu.HBM(b.shape, b.dtype) for b in bufs] + [jax.ShapeDtypeStruct((8, 128), f32)],
        input_output_aliases={a: 2 * n + a for a in range(n)},
        compiler_params=pltpu.CompilerParams(has_side_effects=EFFECT),
    )(*[_in_hbm(b) for b in bufs])
    return list(outs[:n]), list(outs[n:2 * n]), list(outs[2 * n:3 * n]), outs[3 * n]


def _gather_wait(send, recv, bufs, after, name):
    n = len(bufs)

    def body(*refs):
        ins = refs[:n]
        send_r, recv_r = refs[n:2 * n], refs[2 * n:3 * n]
        x, y, c, chips = _place()
        q = 2 * x + y
        for a in range(n):
            for j, chip in enumerate(chips):
                cp = _gather_copy(ins[a], send_r[a], recv_r[a], j, chip, q, c, _chip_id(chip))
                cp.wait_send()
                cp.wait_recv()

    afters = after if isinstance(after, (tuple, list)) else (after,)
    outs = pl.pallas_call(
        body, name=name,
        in_specs=[HBM] * n + [SEM] * (2 * n) + [ANY] * len(afters),
        out_specs=[HBM] * n,
        out_shape=[pltpu.HBM(b.shape, b.dtype) for b in bufs],
        input_output_aliases={a: a for a in range(n)},
        compiler_params=pltpu.CompilerParams(has_side_effects=EFFECT),
    )(*bufs, *send, *recv, *afters)
    return list(outs)


def _forward_halves(bufs, name):
    n = len(bufs)

    def body(*refs):
        outs = refs[n:2 * n]
        send_sems, recv_sems = refs[2 * n:]
        x, y, c, chips = _place()
        sibling = (x, y, 1 - c)

        def remote(a, j, blk):
            return pltpu.make_async_remote_copy(src_ref=blk, dst_ref=blk, send_sem=send_sems.at[a, j],
                                                recv_sem=recv_sems.at[a, j], device_id=sibling,
                                                device_id_type=MESH)

        sent = []
        for a in range(n):
            for j, chip in enumerate(chips):
                cp = remote(a, j, outs[a].at[_chip_id(chip), c])
                cp.start()
                sent.append(cp)
        for a in range(n):
            for j, chip in enumerate(chips):
                remote(a, j, outs[a].at[_chip_id(chip), 1 - c]).wait_recv()
        for cp in sent:
            cp.wait_send()

    return pl.pallas_call(
        body, name=name,
        in_specs=[ANY] * n, out_specs=[ANY] * n,
        out_shape=[jax.ShapeDtypeStruct(s.shape, s.dtype) for s in bufs],
        scratch_shapes=[pltpu.SemaphoreType.DMA((n, N_PEER)), pltpu.SemaphoreType.DMA((n, N_PEER))],
        input_output_aliases={a: a for a in range(n)},
    )(*bufs)


def _reduce_copy(sum_ref, land_ref, send_ref, recv_ref, j, chip, q, c, landing_chip):
    return pltpu.make_async_remote_copy(
        src_ref=sum_ref.at[_chip_id(chip)], dst_ref=land_ref.at[landing_chip],
        send_sem=send_ref.at[j], recv_sem=recv_ref.at[j],
        device_id=(chip[0], chip[1], c), device_id_type=MESH)


def _reduce_start(sums, lands, name):
    n = len(sums)

    def body(*refs):
        s_in, l_in = refs[:n], refs[n:2 * n]
        send, recv = refs[2 * n:3 * n], refs[3 * n:4 * n]
        token = refs[6 * n]
        x, y, c, chips = _place()
        q = 2 * x + y
        for a in range(n):
            for j, chip in enumerate(chips):
                _reduce_copy(s_in[a], l_in[a], send[a], recv[a], j, chip, q, c, q).start()
        token[...] = jnp.zeros_like(token)

    sems = [pltpu.SemaphoreType.DMA((N_PEER,))] * (2 * n)
    outs = pl.pallas_call(
        body, name=name,
        in_specs=[HBM] * (2 * n),
        out_specs=[SEM] * (2 * n) + [HBM] * (2 * n) + [pl.BlockSpec(memory_space=pltpu.VMEM)],
        out_shape=sems + [pltpu.HBM(b.shape, b.dtype) for b in list(sums) + list(lands)]
        + [jax.ShapeDtypeStruct((8, 128), f32)],
        input_output_aliases={a: 2 * n + a for a in range(2 * n)},
        compiler_params=pltpu.CompilerParams(has_side_effects=EFFECT),
    )(*[_in_hbm(b) for b in list(sums) + list(lands)])
    return (list(outs[:n]), list(outs[n:2 * n]), list(outs[2 * n:3 * n]), list(outs[3 * n:4 * n]),
            outs[4 * n])


def _reduce_wait(send, recv, sums, lands, after, name):
    n = len(sums)

    def body(*refs):
        s_in, l_in = refs[:n], refs[n:2 * n]
        send_r, recv_r = refs[2 * n:3 * n], refs[3 * n:4 * n]
        x, y, c, chips = _place()
        q = 2 * x + y
        for a in range(n):
            for j, chip in enumerate(chips):
                cp = _reduce_copy(s_in[a], l_in[a], send_r[a], recv_r[a], j, chip, q, c, _chip_id(chip))
                cp.wait_send()
                cp.wait_recv()

    afters = after if isinstance(after, (tuple, list)) else (after,)
    outs = pl.pallas_call(
        body, name=name,
        in_specs=[HBM] * (2 * n) + [SEM] * (2 * n) + [ANY] * len(afters),
        out_specs=[HBM] * (2 * n),
        out_shape=[pltpu.HBM(b.shape, b.dtype) for b in list(sums) + list(lands)],
        input_output_aliases={a: a for a in range(2 * n)},
        compiler_params=pltpu.CompilerParams(has_side_effects=EFFECT),
    )(*sums, *lands, *send, *recv, *afters)
    return list(outs[n:])


def _sibling_copy(part_ref, land_ref, send_ref, recv_ref, x, y, c):
    return pltpu.make_async_remote_copy(
        src_ref=part_ref.at[:, 1 - c], dst_ref=land_ref, send_sem=send_ref.at[0], recv_sem=recv_ref.at[0],
        device_id=(x, y, 1 - c), device_id_type=MESH)


def _pair_exchange_start(parts, name):
    n = len(parts)
    lands = [lax.empty((NCHIP,) + p.shape[2:], p.dtype) for p in parts]

    def body(*refs):
        p_in, l_in = refs[:n], refs[n:2 * n]
        send, recv = refs[2 * n:3 * n], refs[3 * n:4 * n]
        token = refs[6 * n]
        x, y, c, _ = _place()
        for a in range(n):
            _sibling_copy(p_in[a], l_in[a], send[a], recv[a], x, y, c).start()
        token[...] = jnp.zeros_like(token)

    sems = [pltpu.SemaphoreType.DMA((1,))] * (2 * n)
    outs = pl.pallas_call(
        body, name=name,
        in_specs=[HBM] * (2 * n),
        out_specs=[SEM] * (2 * n) + [HBM] * (2 * n) + [pl.BlockSpec(memory_space=pltpu.VMEM)],
        out_shape=sems + [pltpu.HBM(b.shape, b.dtype) for b in list(parts) + lands]
        + [jax.ShapeDtypeStruct((8, 128), f32)],
        input_output_aliases={a: 2 * n + a for a in range(2 * n)},
        compiler_params=pltpu.CompilerParams(has_side_effects=EFFECT),
    )(*[_in_hbm(b) for b in list(parts) + lands])
    return (list(outs[:n]), list(outs[n:2 * n]), list(outs[2 * n:3 * n]), list(outs[3 * n:4 * n]),
            outs[4 * n])


def _pair_exchange_wait(send, recv, parts, lands, after, name):
    n = len(parts)

    def body(*refs):
        p_in, l_in = refs[:n], refs[n:2 * n]
        send_r, recv_r = refs[2 * n:3 * n], refs[3 * n:4 * n]
        x, y, c, _ = _place()
        for a in range(n):
            cp = _sibling_copy(p_in[a], l_in[a], send_r[a], recv_r[a], x, y, c)
            cp.wait_send()
            cp.wait_recv()

    outs = pl.pallas_call(
        body, name=name,
        in_specs=[HBM] * (2 * n) + [SEM] * (2 * n) + [ANY],
        out_specs=[HBM] * (2 * n),
        out_shape=[pltpu.HBM(b.shape, b.dtype) for b in list(parts) + list(lands)],
        input_output_aliases={a: a for a in range(2 * n)},
        compiler_params=pltpu.CompilerParams(has_side_effects=EFFECT),
    )(*parts, *lands, *send, *recv, after)
    return list(outs[:n]), list(outs[n:])


def _forward_copy(buf_ref, send_ref, recv_ref, j, chip, x, y, c, landing):
    return pltpu.make_async_remote_copy(
        src_ref=buf_ref.at[_chip_id(chip), c], dst_ref=buf_ref.at[_chip_id(chip), 1 - c if landing else c],
        send_sem=send_ref.at[j], recv_sem=recv_ref.at[j], device_id=(x, y, 1 - c), device_id_type=MESH)


def _forward_start(bufs, name):
    n = len(bufs)

    def body(*refs):
        ins = refs[:n]
        send, recv = refs[n:2 * n], refs[2 * n:3 * n]
        token = refs[4 * n]
        x, y, c, chips = _place()
        for a in range(n):
            for j, chip in enumerate(chips):
                _forward_copy(ins[a], send[a], recv[a], j, chip, x, y, c, False).start()
        token[...] = jnp.zeros_like(token)

    sems = [pltpu.SemaphoreType.DMA((N_PEER,))] * (2 * n)
    outs = pl.pallas_call(
        body, name=name,
        in_specs=[HBM] * n,
        out_specs=[SEM] * (2 * n) + [HBM] * n + [pl.BlockSpec(memory_space=pltpu.VMEM)],
        out_shape=sems + [pltpu.HBM(b.shape, b.dtype) for b in bufs] + [jax.ShapeDtypeStruct((8, 128), f32)],
        input_output_aliases={a: 2 * n + a for a in range(n)},
        compiler_params=pltpu.CompilerParams(has_side_effects=EFFECT),
    )(*[_in_hbm(b) for b in bufs])
    return list(outs[:n]), list(outs[n:2 * n]), list(outs[2 * n:3 * n]), outs[3 * n]


def _forward_wait(send, recv, bufs, after, name):
    n = len(bufs)

    def body(*refs):
        ins = refs[:n]
        send_r, recv_r = refs[n:2 * n], refs[2 * n:3 * n]
        x, y, c, chips = _place()
        for a in range(n):
            for j, chip in enumerate(chips):
                cp = _forward_copy(ins[a], send_r[a], recv_r[a], j, chip, x, y, c, True)
                cp.wait_send()
                cp.wait_recv()

    outs = pl.pallas_call(
        body, name=name,
        in_specs=[HBM] * n + [SEM] * (2 * n) + [ANY],
        out_specs=[HBM] * n,
        out_shape=[pltpu.HBM(b.shape, b.dtype) for b in bufs],
        input_output_aliases={a: a for a in range(n)},
        compiler_params=pltpu.CompilerParams(has_side_effects=EFFECT),
    )(*bufs, *send, *recv, after)
    return list(outs)


def _adamw_math(w, g, m, v):
    m = ADAM_B1 * m + (1.0 - ADAM_B1) * g
    v = ADAM_B2 * v + (1.0 - ADAM_B2) * (g * g)
    m_hat = m / (1.0 - ADAM_B1 ** ADAM_STEP)
    v_hat = v / (1.0 - ADAM_B2 ** ADAM_STEP)
    delta = -ADAM_LR * (m_hat / (jnp.sqrt(v_hat) + ADAM_EPS) + ADAM_WD * w)
    return delta, m, v


ADAMW_BLOCK_BYTES = 3 << 19


def _adamw(ws, mines, theirs, ms, vs, qc, name):
    n = len(ws)
    halves = [w.shape[0] // 2 for w in ws]
    nb = next(k for k in range(1, min(halves) + 1)
              if all(hr % k == 0 and (hr // k) % 8 == 0 and (hr // k) * w.shape[1] * 4 <= ADAMW_BLOCK_BYTES
                     for hr, w in zip(halves, ws)))

    def body(qc_ref, *refs):
        mine_here = pl.program_id(0) == qc_ref[1]
        for a in range(n):
            w_ref, a_ref, b_ref, m_ref, v_ref = (refs[k * n + a] for k in range(5))
            g_ref, d_ref, mo_ref, vo_ref = (refs[(5 + k) * n + a] for k in range(4))
            g = jnp.where(mine_here, a_ref[...], b_ref[...])
            g_ref[...] = g
            d_ref[...], mo_ref[...], vo_ref[...] = _adamw_math(w_ref[...], g, m_ref[...], v_ref[...])

    blocks = [(hr // nb, w.shape[1]) for hr, w in zip(halves, ws)]
    full = [pl.BlockSpec(b, lambda h, i, qc_ref: (h * nb + i, 0)) for b in blocks]
    half = [pl.BlockSpec(b, lambda h, i, qc_ref: (i, 0)) for b in blocks]
    outs = pl.pallas_call(
        body, name=name,
        grid_spec=pltpu.PrefetchScalarGridSpec(
            num_scalar_prefetch=1, grid=(2, nb),
            in_specs=full + half + half + full + full, out_specs=full * 4),
        out_shape=[jax.ShapeDtypeStruct(w.shape, f32) for w in ws] * 4,
        compiler_params=_cp(2),
    )(qc, *ws, *mines, *theirs, *ms, *vs)
    return [tuple(outs[k * n + a] for k in range(4)) for a in range(n)]


REPL = [("ffn1_norm", 1), ("mix_norm", 1), ("b_in", 6), ("rnn_conv_b", 1), ("rg_b_a", 1), ("rg_b_x", 1),
        ("rg_lambda", 1), ("conv_dw_b", 1), ("conv_ln_g", 1), ("conv_ln_b", 1), ("conv_b_proj", 1),
        ("ffn2_norm", 1), ("final_norm", 1)]
COLSH = [("meta_tokens", NMETA), ("rnn_conv_w", KC4), ("conv_dw_w", KC31)]
SMALL = REPL + COLSH
CS = D // NCHIP


def _pack_rows():
    starts, row = {}, 0
    for k, rows in REPL:
        starts[k] = row
        row += rows
    for k, rows in COLSH:
        row = -(-row // 8) * 8
        starts[k] = row
        row += rows
    return starts, -(-row // 8) * 8


PACK_START, LOSS_ROW = _pack_rows()
SMALL_ROWS = LOSS_ROW + 8


def _small_pack(g, loss_row):
    pieces = []
    for k, rows in SMALL:
        a = g[k]
        if a.ndim != 2 or a.shape[1] % D or a.size != rows * D:
            a = a.reshape(rows, D)
        pieces.append((PACK_START[k], a.astype(f32)))
    pieces.append((LOSS_ROW, loss_row))
    return pieces


def _adamw_small(packs, ws, ms, vs):
    ns = len(SMALL)

    def body(*refs):
        pack_ref = refs[0]
        w_refs, m_refs, v_refs = refs[1:1 + ns], refs[1 + ns:1 + 2 * ns], refs[1 + 2 * ns:1 + 3 * ns]
        outs = refs[1 + 3 * ns:1 + 7 * ns]
        g_refs, d_refs, mo_refs, vo_refs = outs[:ns], outs[ns:2 * ns], outs[2 * ns:3 * ns], outs[3 * ns:]
        loss_ref = refs[1 + 7 * ns]
        gsum_sc = refs[2 + 7 * ns]
        q = 2 * lax.axis_index("x") + lax.axis_index("y")
        acc = pack_ref[0]
        for dev in range(1, 8):
            acc = acc + pack_ref[dev]
        gsum_sc[...] = acc
        loss_ref[...] = gsum_sc[LOSS_ROW:LOSS_ROW + 1, :]
        for idx, (name, rows) in enumerate(SMALL):
            row = PACK_START[name]
            if idx < len(REPL):
                for k in range(rows):
                    cols = slice(k * D, (k + 1) * D)
                    g = gsum_sc[row + k:row + k + 1, :]
                    d, mm, vv = _adamw_math(w_refs[idx][:, cols], g, m_refs[idx][:, cols], v_refs[idx][:, cols])
                    g_refs[idx][:, cols] = g
                    d_refs[idx][:, cols] = d
                    mo_refs[idx][:, cols] = mm
                    vo_refs[idx][:, cols] = vv
            else:
                g = gsum_sc[row:row + rows, pl.ds(pl.multiple_of(q * CS, CS), CS)]
                d, mm, vv = _adamw_math(w_refs[idx][...], g, m_refs[idx][...], v_refs[idx][...])
                g_refs[idx][...] = g
                d_refs[idx][...] = d
                mo_refs[idx][...] = mm
                vo_refs[idx][...] = vv

    shapes = [jax.ShapeDtypeStruct(w.shape, f32) for w in ws]
    return pl.pallas_call(
        body, name="adamw_small",
        out_shape=shapes * 4 + [jax.ShapeDtypeStruct((1, D), f32)],
        scratch_shapes=[pltpu.VMEM((SMALL_ROWS, D), f32)],
        compiler_params=pltpu.CompilerParams(vmem_limit_bytes=VMEM_LIMIT),
    )(packs, *ws, *ms, *vs)


WEIGHTS = ['meta_tokens', 'ffn1_norm', 'ffn1_w_gu', 'ffn1_w_down', 'mix_norm', 'w_in', 'b_in', 'rnn_conv_w',
           'rnn_conv_b', 'rg_w_a', 'rg_b_a', 'rg_w_x', 'rg_b_x', 'rg_lambda', 'rnn_w_proj', 'conv_dw_w',
           'conv_dw_b', 'conv_ln_g', 'conv_ln_b', 'conv_w_proj', 'conv_b_proj', 'w_out', 'ffn2_norm',
           'ffn2_w_gu', 'ffn2_w_down', 'final_norm']


def _as2d(a):
    return a.reshape(-1, a.shape[-1])


def _step(x, loss_target, w, m, v):
    seq = x.shape[1]
    n_valid = NMETA + seq
    t = -(-n_valid // TM) * TM

    qc = jnp.stack([2 * lax.axis_index("x") + lax.axis_index("y"), lax.axis_index("c")]).astype(jnp.int32)
    p = {k: w[k].reshape(1, rows * D) for k, rows in REPL}

    first = ["ffn1_w_gu", "ffn1_w_down", "small"]
    later = [["w_in"], ["rg_w_a", "rg_w_x", "rnn_w_proj", "conv_w_proj", "w_out"], ["ffn2_w_gu", "ffn2_w_down"]]
    small_rows = sum(r for _, r in COLSH)
    small = jnp.concatenate([_as2d(w[k]) for k, _ in COLSH] + [jnp.zeros((64 - small_rows, CS), f32)], axis=0)

    def cast(k, token=None):
        src, dtype = (small, f32) if k == "small" else (_as2d(w[k]), bf16)
        return _cast_into_slot(src, qc, dtype, "cast_" + k, after=token)

    send1, recv1, bufs1, token1 = _gather_start([cast(k) for k in first], "gather_start_first")
    rest = [k for grp in later for k in grp]
    send2, recv2, bufs2, token2 = _gather_start([cast(k, token1) for k in rest], "gather_start_rest")

    def install(names, done):
        for k, b in zip(names, done):
            full = b.reshape(NCHIP, 2 * b.shape[2], b.shape[3])
            if k in ("ffn1_w_down", "ffn2_w_down"):
                full = full.reshape(F, D)
            elif k in ("rnn_w_proj", "conv_w_proj", "w_out"):
                full = full.reshape(D, D)
            elif k in ("rg_w_a", "rg_w_x"):
                full = full.reshape(NCHIP, NHEAD, HD // NCHIP, HD).transpose(1, 0, 2, 3).reshape(NHEAD, HD, HD)
            p[k] = full

    def finish(names, send, recv, bufs, after, tag):
        install(names, _forward_halves(_gather_wait(send, recv, bufs, after, "gather_wait_" + tag),
                                       "gather_forward_" + tag))

    def group(names):
        idx = [rest.index(k) for k in names]
        return names, [send2[i] for i in idx], [recv2[i] for i in idx], [bufs2[i] for i in idx]

    h0 = jnp.pad(x[0] + token1[0:1, 0:1], ((NMETA, t - n_valid), (0, 0)))
    tgt = jnp.pad(loss_target[0] + token2[0:1, 0:1], ((NMETA, t - n_valid), (0, 0)))
    finish(first, send1, recv1, bufs1, (token2, h0, tgt), "first")
    small_full = p.pop("small").transpose(1, 0, 2).reshape(64, D)
    row = 0
    for k, rows in COLSH:
        p[k] = small_full[row:row + rows]
        row += rows

    h0 = lax.dynamic_update_slice(h0, p["meta_tokens"], (0, 0))
    h1, gate1, up1, a1, n1 = _ffn_fwd(h0, p["ffn1_norm"], p["ffn1_w_gu"], p["ffn1_w_down"], "ffn1_fwd")
    finish(*group(later[0]), h1, "in")
    proj, n2 = _inproj_fwd(h1, p["mix_norm"], p["w_in"], p["b_in"])
    names_l = later[1] + later[2]
    _, send_l, recv_l, bufs_l = group(names_l)
    send_f, recv_f, bufs_f, token = _forward_start(
        _gather_wait(send_l, recv_l, bufs_l, proj, "gather_wait_late"), "gather_forward_start")
    vc, s = _conv_fwd(proj, p["conv_dw_w"], p["conv_dw_b"], p["conv_ln_g"], p["conv_ln_b"], after=token)
    install(names_l, _forward_wait(send_f, recv_f, bufs_f, vc, "gather_forward_wait"))
    xr, hr, z, gates = _rnn_fwd(proj, p["rnn_conv_w"], p["rnn_conv_b"], p["rg_w_a"], p["rg_b_a"],
                         p["rg_w_x"], p["rg_b_x"], p["rg_lambda"])
    h2 = _merge_fwd(h1, z, s, proj, p["rnn_w_proj"], p["conv_w_proj"], p["conv_b_proj"], p["w_out"])
    dh3, loss_blk, d_final, gate2, up2, a2, n3 = _ffn_fwd(
        h2, p["ffn2_norm"], p["ffn2_w_gu"], p["ffn2_w_down"], "ffn2_fwd",
        loss_head=(p["final_norm"], tgt, n_valid))

    g = {"final_norm": d_final}
    pending = []

    def exchange_start(names, tag):
        parts = []
        for k in names:
            rows = g[k].size // (NCHIP * g[k].shape[-1])
            parts.append(g[k].reshape((NCHIP, 2, rows // 2, g[k].shape[-1])))
        send, recv, parts, lands, token = _pair_exchange_start(parts, "pair_exchange_start_" + tag)
        return (names, tag, send, recv, parts, lands), token

    def reduce_start(state, after):
        names, tag, send, recv, parts, lands = state
        parts, from_sibling = _pair_exchange_wait(send, recv, parts, lands, after, "pair_exchange_wait_" + tag)
        sums, lands = _pair_add(parts, from_sibling, qc, "pair_add_" + tag)
        send, recv, sums, lands, token = _reduce_start(sums, lands, "reduce_start_" + tag)
        pending.append((names, tag, send, recv, sums, lands))
        return token

    dh2, dgate2, dup2, df2, g["ffn2_norm"] = _ffn_bwd(
        dh3, h2, p["ffn2_norm"], gate2, up2, p["ffn2_w_gu"], p["ffn2_w_down"], "ffn2_bwd")
    g["ffn2_w_gu"] = _ffn_gu_grad(n3, dgate2, dup2, "ffn2")
    g["ffn2_w_down"] = _ffn_down_grad(a2, df2, "ffn2")
    state, token = exchange_start(["ffn2_w_gu", "ffn2_w_down"], "ffn2")

    dz, ds, dproj, dh2b, merged, dya, dyb, g["conv_b_proj"] = _merge_bwd(
        dh2, z, s, proj, p["rnn_w_proj"], p["conv_w_proj"], p["conv_b_proj"], p["w_out"], after=token)
    token = reduce_start(state, dz)
    dproj, g["conv_dw_w"], g["conv_dw_b"], g["conv_ln_g"], g["conv_ln_b"] = _conv_bwd(
        ds, vc, proj, dproj, p["conv_dw_w"], p["conv_ln_g"], p["conv_ln_b"], after=token)
    g["w_out"] = _square_grad(merged, dh2b, "dw_out")
    g["rnn_w_proj"] = _square_grad(z, dya, "dw_rnn_proj")
    g["conv_w_proj"] = _square_grad(s, dyb, "dw_conv_proj")
    (dproj, g["rg_w_a"], g["rg_w_x"], g["rnn_conv_w"], g["rnn_conv_b"], g["rg_b_a"], g["rg_b_x"],
     g["rg_lambda"]) = _rnn_bwd(dz, xr, hr, gates, proj, dproj, p["rnn_conv_w"], p["rg_w_a"],
                                p["rg_w_x"], p["rg_lambda"])

    dh1, g["mix_norm"], db_in = _inproj_bwd(dproj, dh2, h1, p["mix_norm"], p["w_in"])
    g["b_in"] = db_in.reshape(1, NIN)
    g["w_in"] = _tn_matmul(n2, dproj, D, NIN // NCHIP, (NCHIP, D, NIN // NCHIP),
                           (None, D, NIN // NCHIP), lambda k, nn, mm: (nn, 0, 0), "dw_in")
    state, token = exchange_start(["w_out", "rnn_w_proj", "conv_w_proj", "rg_w_a", "rg_w_x", "w_in"], "mix")

    dh0, dgate1, dup1, df1, g["ffn1_norm"], grad_x = _ffn_bwd(
        dh1, h0, p["ffn1_norm"], gate1, up1, p["ffn1_w_gu"], p["ffn1_w_down"], "ffn1_bwd", after=token,
        grad_rows=(NMETA, n_valid))
    g["meta_tokens"] = dh0[0:NMETA]
    grad_x = grad_x[None]
    token = reduce_start(state, dh0)

    send_s, recv_s, pack_buf, token_s = _gather_all_start(
        _place_pack(_small_pack(g, loss_blk.reshape(1, D)), SMALL_ROWS, qc), "gather_all_start")
    g["ffn1_w_down"] = _ffn_down_grad(a1, df1, "ffn1", after=(token, token_s))
    state, token = exchange_start(["ffn1_w_down"], "ffn1_down")
    gate_half = _tn_matmul(n1, dgate1, D, FS, (NCHIP, D, FS), (None, D, FS), lambda k, nn, mm: (nn, 0, 0),
                           "ffn1_dwg", after=token)
    token = reduce_start(state, gate_half)
    g["ffn1_w_gu"] = _tn_matmul(n1, dup1, D, FS, (NCHIP, D, FS), (None, D, FS), lambda k, nn, mm: (2 + nn, 0, 0),
                                "ffn1_dwu", base=gate_half, after=token)
    state_gu, token = exchange_start(["ffn1_w_gu"], "ffn1_gu")
    packs = _gather_all_wait(send_s, recv_s, pack_buf, token, "gather_all_wait")

    grads, deltas, new_m, new_v = {}, {}, {}, {}

    def landed_sums(items, after):
        names, mine = [], []
        for grp_names, grp_tag, send, recv, sums, lands in items:
            landed = _reduce_wait(send, recv, sums, lands, after, "reduce_wait_" + grp_tag)
            mine += _sum_chips(landed, "sum_chips_" + grp_tag)
            names += grp_names
            after = mine[-1]
        return names, mine

    def share_and_update(names, mine, tag, after=None):
        theirs = _pair_share(mine, "pair_share_" + tag, after=after)
        got = dict(zip(names, zip(mine, theirs)))
        square = [k for k in names if got[k][0].shape[0] * 2 <= HD]
        for batch in [[k] for k in names if k not in square] + ([square] if square else []):
            outs = _adamw([_as2d(w[k]) for k in batch], [got[k][0] for k in batch], [got[k][1] for k in batch],
                          [_as2d(m[k]) for k in batch], [_as2d(v[k]) for k in batch], qc,
                          "adamw_" + (batch[0] if len(batch) == 1 else "mixer"))
            for k, out in zip(batch, outs):
                grads[k], deltas[k], new_m[k], new_v[k] = (a.reshape(w[k].shape) for a in out)
        return [new_v[k] for k in names]

    early_names, early_mine = landed_sums(pending[:2], packs)
    token = reduce_start(state_gu, early_mine[-1])
    after = share_and_update(early_names, early_mine, "early", after=token)
    share_and_update(*landed_sums(pending[2:], after), "late")
    names = [k for k, _ in SMALL]
    shape2 = {k: ((1, rows * D) if (k, rows) in REPL else (rows, CS)) for k, rows in SMALL}
    outs = _adamw_small(packs, *[[a[k].reshape(shape2[k]) for k in names] for a in (w, m, v)])
    ns = len(names)
    for i, k in enumerate(names):
        grads[k], deltas[k], new_m[k], new_v[k] = (outs[j * ns + i].reshape(w[k].shape) for j in range(4))

    loss = outs[4 * ns][0, 0]
    return (loss, grad_x, *[grads[k] for k in WEIGHTS], *[deltas[k] for k in WEIGHTS],
            *[new_m[k] for k in WEIGHTS], *[new_v[k] for k in WEIGHTS])


def kernel(x, meta_tokens, ffn1_norm, ffn1_w_gu, ffn1_w_down, mix_norm, w_in, b_in, rnn_conv_w, rnn_conv_b, rg_w_a, rg_b_a, rg_w_x, rg_b_x, rg_lambda, rnn_w_proj, conv_dw_w, conv_dw_b, conv_ln_g, conv_ln_b, conv_w_proj, conv_b_proj, w_out, ffn2_norm, ffn2_w_gu, ffn2_w_down, final_norm, loss_target, m_meta_tokens, m_ffn1_norm, m_ffn1_w_gu, m_ffn1_w_down, m_mix_norm, m_w_in, m_b_in, m_rnn_conv_w, m_rnn_conv_b, m_rg_w_a, m_rg_b_a, m_rg_w_x, m_rg_b_x, m_rg_lambda, m_rnn_w_proj, m_conv_dw_w, m_conv_dw_b, m_conv_ln_g, m_conv_ln_b, m_conv_w_proj, m_conv_b_proj, m_w_out, m_ffn2_norm, m_ffn2_w_gu, m_ffn2_w_down, m_final_norm, v_meta_tokens, v_ffn1_norm, v_ffn1_w_gu, v_ffn1_w_down, v_mix_norm, v_w_in, v_b_in, v_rnn_conv_w, v_rnn_conv_b, v_rg_w_a, v_rg_b_a, v_rg_w_x, v_rg_b_x, v_rg_lambda, v_rnn_w_proj, v_conv_dw_w, v_conv_dw_b, v_conv_ln_g, v_conv_ln_b, v_conv_w_proj, v_conv_b_proj, v_w_out, v_ffn2_norm, v_ffn2_w_gu, v_ffn2_w_down, v_final_norm):
    args = locals()
    w = {k: args[k] for k in WEIGHTS}
    m = {k: args["m_" + k] for k in WEIGHTS}
    v = {k: args["v_" + k] for k in WEIGHTS}
    return _step(x, loss_target, w, m, v)
```
